```python
import math
import jax, jax.numpy as jnp
from jax import lax
import numpy as np

D_MODEL = 1024
BATCH = 8
SEQ = 2048
DEPTH = 1

PLE_DIM = 256
SSD_HEADS = 16
SSD_HEAD_DIM = 64
SSD_INNER = SSD_HEADS * SSD_HEAD_DIM
SSD_GROUPS = 2
SSD_STATE = 128
SSD_CONV = 4
SSD_CHUNK = 128
MLA_HEADS = 16
MLA_Q_RANK = 384
MLA_KV_RANK = 256
MLA_NOPE = 64
MLA_ROPE = 32
MLA_V = 64
MLA_OUT = MLA_HEADS * MLA_V
ROPE_BASE = 10000.0
Q_BLOCK = 128
MIX_WIDTH = SSD_INNER + MLA_OUT
SSD_XBC = SSD_INNER + 2 * SSD_GROUPS * SSD_STATE
IN_WIDTH = SSD_INNER + SSD_XBC + SSD_HEADS + MLA_Q_RANK + MLA_KV_RANK + MLA_ROPE
D_FF = -(-8 * D_MODEL // (3 * 256)) * 256
DEEPNORM_ALPHA = (2 * DEPTH) ** 0.25
DEEPNORM_BETA = (8 * DEPTH) ** -0.25
EPS = 1e-6

kernel_name = "hybrid_ssd_mla_deepnorm_ple_layer"


def rmsnorm(u, g):
    uf = u.astype(jnp.float32)
    out = uf * lax.rsqrt(jnp.mean(uf * uf, axis=-1, keepdims=True) + EPS)
    return (out * g.astype(jnp.float32)).astype(u.dtype)


def layernorm(u, g, b):
    uf = u.astype(jnp.float32)
    mu = jnp.mean(uf, axis=-1, keepdims=True)
    d = uf - mu
    var = jnp.mean(d * d, axis=-1, keepdims=True)
    out = d * lax.rsqrt(var + 1e-5) * g.astype(jnp.float32) + b.astype(jnp.float32)
    return out.astype(u.dtype)


def rope_tables(positions):
    inv_freq = 1.0 / (ROPE_BASE ** (jnp.arange(0, MLA_ROPE, 2, dtype=jnp.float32) / MLA_ROPE))
    ang = positions.astype(jnp.float32)[..., None] * inv_freq
    return jnp.cos(ang), jnp.sin(ang)


def apply_rope(u, cos, sin):
    cos = cos.astype(u.dtype)
    sin = sin.astype(u.dtype)
    u1, u2 = jnp.split(u, 2, axis=-1)
    return jnp.concatenate([u1 * cos - u2 * sin, u2 * cos + u1 * sin], axis=-1)


def causal_depthwise_conv(u, w, bias):
    c = u.shape[-1]
    out = lax.conv_general_dilated(
        u, w[:, None, :].astype(u.dtype), window_strides=(1,),
        padding=((SSD_CONV - 1, 0),), dimension_numbers=("NWC", "WIO", "NWC"),
        feature_group_count=c)
    return out + bias.astype(u.dtype)


def ssd_mixer(z, xBC, dt_raw, conv_w, conv_b, dt_bias, A_log, D_skip, norm_w):
    b, s, _ = xBC.shape
    G, E, P, N, L = SSD_GROUPS, SSD_HEADS // SSD_GROUPS, SSD_HEAD_DIM, SSD_STATE, SSD_CHUNK
    nc = s // L
    f32 = jnp.float32
    xBC = jax.nn.silu(causal_depthwise_conv(xBC, conv_w, conv_b))
    xs, Bm, Cm = jnp.split(xBC, [SSD_INNER, SSD_INNER + G * N], axis=-1)
    xs = xs.astype(f32).reshape(b, nc, L, G, E, P)
    Bm = Bm.astype(f32).reshape(b, nc, L, G, N)
    Cm = Cm.astype(f32).reshape(b, nc, L, G, N)
    dt = jax.nn.softplus(dt_raw.astype(f32) + dt_bias.astype(f32))
    A = -jnp.exp(A_log.astype(f32))
    dt_c = dt.reshape(b, nc, L, G, E)
    dA_cs = jnp.cumsum((dt * A).reshape(b, nc, L, G, E).transpose(0, 1, 3, 4, 2), axis=-1)
    X = xs * dt_c[..., None]
    causal = jnp.tril(jnp.ones((L, L), dtype=bool))
    seg = dA_cs[..., :, None] - dA_cs[..., None, :]
    Lmat = jnp.exp(jnp.where(causal, seg, -jnp.inf))
    CB = jnp.einsum("bclgn,bcsgn->bcgls", Cm, Bm)
    y_diag = jnp.einsum("bcgls,bcgels,bcsgep->bclgep", CB, Lmat, X)
    decay_states = jnp.exp(dA_cs[..., -1:] - dA_cs)
    states = jnp.einsum("bclgn,bcgel,bclgep->bcgepn", Bm, decay_states, X)
    chunk_decay = jnp.exp(dA_cs[..., -1])

    def step(carry, inp):
        dec, st = inp
        return carry * dec[..., None, None] + st, carry

    init = jnp.zeros((b, G, E, P, N), f32)
    _, prev = lax.scan(step, init, (chunk_decay.transpose(1, 0, 2, 3), states.transpose(1, 0, 2, 3, 4, 5)))
    prev = prev.transpose(1, 0, 2, 3, 4, 5)
    y_off = jnp.einsum("bclgn,bcgepn,bcgel->bclgep", Cm, prev, jnp.exp(dA_cs))
    y = y_diag + y_off + D_skip.astype(f32).reshape(G, E)[:, :, None] * xs
    y = y.reshape(b, s, SSD_INNER)
    y = rmsnorm(y * jax.nn.silu(z.astype(f32)), norm_w)
    return y.astype(z.dtype)


def mla_mixer(q_c, kv_c, k_rope, cos, sin, q_norm_w, w_q_b, kv_norm_w, w_kv_b, out_norm_w):
    b, s, _ = q_c.shape
    H = MLA_HEADS
    q = (rmsnorm(q_c, q_norm_w) @ w_q_b).reshape(b, s, H, MLA_NOPE + MLA_ROPE)
    q_nope, q_rope = jnp.split(q, [MLA_NOPE], axis=-1)
    kv = (rmsnorm(kv_c, kv_norm_w) @ w_kv_b).reshape(b, s, H, MLA_NOPE + MLA_V)
    k_nope, v = jnp.split(kv, [MLA_NOPE], axis=-1)
    q_rope = apply_rope(q_rope, cos[:, :, None, :], sin[:, :, None, :])
    k_rope = apply_rope(k_rope, cos, sin)
    scale = 1.0 / math.sqrt(MLA_NOPE + MLA_ROPE)
    nb = s // Q_BLOCK
    qn_blocks = q_nope.reshape(b, nb, Q_BLOCK, H, MLA_NOPE).transpose(1, 0, 2, 3, 4)
    qr_blocks = q_rope.reshape(b, nb, Q_BLOCK, H, MLA_ROPE).transpose(1, 0, 2, 3, 4)
    key_idx = jnp.arange(s)

    def attend(args):
        qn, qr, blk = args
        sc = (jnp.einsum("bqhd,bkhd->bhqk", qn, k_nope).astype(jnp.float32)
              + jnp.einsum("bqhr,bkr->bhqk", qr, k_rope).astype(jnp.float32)) * scale
        q_idx = blk * Q_BLOCK + jnp.arange(Q_BLOCK)
        sc = jnp.where(q_idx[:, None] >= key_idx[None, :], sc, -jnp.inf)
        pr = jax.nn.softmax(sc, axis=-1).astype(v.dtype)
        return jnp.einsum("bhqk,bkhd->bqhd", pr, v)

    out = lax.map(attend, (qn_blocks, qr_blocks, jnp.arange(nb)))
    out = out.transpose(1, 0, 2, 3, 4).reshape(b, s, MLA_OUT)
    return rmsnorm(out, out_norm_w)


def _fwd_setup_inputs(seed: int = 0) -> dict:
    key = jax.random.key(seed)
    ks = iter(jax.random.split(key, 40))
    f32 = jnp.float32

    def w(shape, fan_in, scale=1.0):
        return jax.random.normal(next(ks), shape, f32) * (fan_in ** -0.5) * scale

    def gain(shape):
        return 1.0 + 0.02 * jax.random.normal(next(ks), shape, f32)

    def small(shape):
        return 0.02 * jax.random.normal(next(ks), shape, f32)

    x = jax.random.normal(next(ks), (BATCH, SEQ, D_MODEL), f32)
    p = jax.random.normal(next(ks), (DEPTH, BATCH, SEQ, PLE_DIM), f32)
    offsets = jax.random.randint(next(ks), (BATCH, 1), 0, 1024, dtype=jnp.int32)
    positions = (jnp.arange(SEQ, dtype=jnp.int32)[None, :] + offsets).astype(jnp.int32)

    dt0 = jnp.exp(jax.random.uniform(next(ks), (DEPTH, SSD_HEADS), f32) * (math.log(0.1) - math.log(0.001)) + math.log(0.001))
    ssd_dt_bias = dt0 + jnp.log(-jnp.expm1(-dt0))
    ssd_A_log = jnp.log(jax.random.uniform(next(ks), (DEPTH, SSD_HEADS), f32, 1.0, 16.0))

    return {
        "x": x,
        "p": p,
        "positions": positions,
        "w_in": w((DEPTH, D_MODEL, IN_WIDTH), D_MODEL),
        "ssd_conv_w": w((DEPTH, SSD_CONV, SSD_XBC), SSD_CONV),
        "ssd_conv_b": small((DEPTH, SSD_XBC)),
        "ssd_dt_bias": ssd_dt_bias,
        "ssd_A_log": ssd_A_log,
        "ssd_D": gain((DEPTH, SSD_HEADS)),
        "ssd_norm_w": gain((DEPTH, SSD_INNER)),
        "mla_q_norm_w": gain((DEPTH, MLA_Q_RANK)),
        "mla_w_q_b": w((DEPTH, MLA_Q_RANK, MLA_HEADS * (MLA_NOPE + MLA_ROPE)), MLA_Q_RANK),
        "mla_kv_norm_w": gain((DEPTH, MLA_KV_RANK)),
        "mla_w_kv_b": w((DEPTH, MLA_KV_RANK, MLA_HEADS * (MLA_NOPE + MLA_V)), MLA_KV_RANK),
        "mla_out_norm_w": gain((DEPTH, MLA_OUT)),
        "w_out": w((DEPTH, MIX_WIDTH, D_MODEL), MIX_WIDTH, DEEPNORM_BETA),
        "ln_mix_g": gain((DEPTH, D_MODEL)),
        "ln_mix_b": small((DEPTH, D_MODEL)),
        "w_ffn_gate": w((DEPTH, D_MODEL, D_FF), D_MODEL),
        "w_ffn_up": w((DEPTH, D_MODEL, D_FF), D_MODEL),
        "w_ffn_down": w((DEPTH, D_FF, D_MODEL), D_FF, DEEPNORM_BETA),
        "w_ple_gate": w((DEPTH, D_MODEL, D_MODEL), D_MODEL),
        "w_ple_proj": w((DEPTH, PLE_DIM, D_MODEL), PLE_DIM, DEEPNORM_BETA),
        "ln_ffn_g": gain((DEPTH, D_MODEL)),
        "ln_ffn_b": small((DEPTH, D_MODEL)),
    }


def _fwd_reference(x, p, positions, w_in, ssd_conv_w, ssd_conv_b, ssd_dt_bias, ssd_A_log, ssd_D, ssd_norm_w,
              mla_q_norm_w, mla_w_q_b, mla_kv_norm_w, mla_w_kv_b, mla_out_norm_w, w_out,
              ln_mix_g, ln_mix_b, w_ffn_gate, w_ffn_up, w_ffn_down, w_ple_gate, w_ple_proj,
              ln_ffn_g, ln_ffn_b):
    s0 = SSD_INNER
    s1 = s0 + SSD_XBC
    s2 = s1 + SSD_HEADS
    s3 = s2 + MLA_Q_RANK
    s4 = s3 + MLA_KV_RANK
    splits = [s0, s1, s2, s3, s4]
    cos, sin = rope_tables(positions)
    h = x
    for i in range(DEPTH):
        proj = h @ w_in[i]
        z, xBC, dt_raw, q_c, kv_c, k_rope = jnp.split(proj, splits, axis=-1)
        y_ssd = ssd_mixer(z, xBC, dt_raw, ssd_conv_w[i], ssd_conv_b[i], ssd_dt_bias[i],
                          ssd_A_log[i], ssd_D[i], ssd_norm_w[i])
        y_mla = mla_mixer(q_c, kv_c, k_rope, cos, sin, mla_q_norm_w[i], mla_w_q_b[i],
                          mla_kv_norm_w[i], mla_w_kv_b[i], mla_out_norm_w[i])
        mix = jnp.concatenate([y_ssd, y_mla], axis=-1) @ w_out[i]
        h = layernorm(DEEPNORM_ALPHA * h + mix, ln_mix_g[i], ln_mix_b[i])
        ffn = (jax.nn.silu(h @ w_ffn_gate[i]) * (h @ w_ffn_up[i])) @ w_ffn_down[i]
        ple = jax.nn.sigmoid(h @ w_ple_gate[i]) * (p[i] @ w_ple_proj[i])
        h = layernorm(DEEPNORM_ALPHA * h + ffn + ple, ln_ffn_g[i], ln_ffn_b[i])
    return h


import jax as _jax
import jax.numpy as _jnp

TWIN_FORMAT = 'train_step'
FWD_PARAMS = ['x', 'p', 'positions', 'w_in', 'ssd_conv_w', 'ssd_conv_b', 'ssd_dt_bias', 'ssd_A_log', 'ssd_D', 'ssd_norm_w', 'mla_q_norm_w', 'mla_w_q_b', 'mla_kv_norm_w', 'mla_w_kv_b', 'mla_out_norm_w', 'w_out', 'ln_mix_g', 'ln_mix_b', 'w_ffn_gate', 'w_ffn_up', 'w_ffn_down', 'w_ple_gate', 'w_ple_proj', 'ln_ffn_g', 'ln_ffn_b']
TWIN_WEIGHTS = ['w_in', 'ssd_conv_w', 'ssd_conv_b', 'ssd_dt_bias', 'ssd_A_log', 'ssd_D', 'ssd_norm_w', 'mla_q_norm_w', 'mla_w_q_b', 'mla_kv_norm_w', 'mla_w_kv_b', 'mla_out_norm_w', 'w_out', 'ln_mix_g', 'ln_mix_b', 'w_ffn_gate', 'w_ffn_up', 'w_ffn_down', 'w_ple_gate', 'w_ple_proj', 'ln_ffn_g', 'ln_ffn_b']
TWIN_DIFF_INPUT = 'x'
TWIN_INPUTS = ['x', 'p', 'positions', 'w_in', 'ssd_conv_w', 'ssd_conv_b', 'ssd_dt_bias', 'ssd_A_log', 'ssd_D', 'ssd_norm_w', 'mla_q_norm_w', 'mla_w_q_b', 'mla_kv_norm_w', 'mla_w_kv_b', 'mla_out_norm_w', 'w_out', 'ln_mix_g', 'ln_mix_b', 'w_ffn_gate', 'w_ffn_up', 'w_ffn_down', 'w_ple_gate', 'w_ple_proj', 'ln_ffn_g', 'ln_ffn_b', 'loss_target', 'm_w_in', 'm_ssd_conv_w', 'm_ssd_conv_b', 'm_ssd_dt_bias', 'm_ssd_A_log', 'm_ssd_D', 'm_ssd_norm_w', 'm_mla_q_norm_w', 'm_mla_w_q_b', 'm_mla_kv_norm_w', 'm_mla_w_kv_b', 'm_mla_out_norm_w', 'm_w_out', 'm_ln_mix_g', 'm_ln_mix_b', 'm_w_ffn_gate', 'm_w_ffn_up', 'm_w_ffn_down', 'm_w_ple_gate', 'm_w_ple_proj', 'm_ln_ffn_g', 'm_ln_ffn_b', 'v_w_in', 'v_ssd_conv_w', 'v_ssd_conv_b', 'v_ssd_dt_bias', 'v_ssd_A_log', 'v_ssd_D', 'v_ssd_norm_w', 'v_mla_q_norm_w', 'v_mla_w_q_b', 'v_mla_kv_norm_w', 'v_mla_w_kv_b', 'v_mla_out_norm_w', 'v_w_out', 'v_ln_mix_g', 'v_ln_mix_b', 'v_w_ffn_gate', 'v_w_ffn_up', 'v_w_ffn_down', 'v_w_ple_gate', 'v_w_ple_proj', 'v_ln_ffn_g', 'v_ln_ffn_b']
TWIN_OUTPUTS = ['loss', 'grad_x', 'grad_w_in', 'grad_ssd_conv_w', 'grad_ssd_conv_b', 'grad_ssd_dt_bias', 'grad_ssd_A_log', 'grad_ssd_D', 'grad_ssd_norm_w', 'grad_mla_q_norm_w', 'grad_mla_w_q_b', 'grad_mla_kv_norm_w', 'grad_mla_w_kv_b', 'grad_mla_out_norm_w', 'grad_w_out', 'grad_ln_mix_g', 'grad_ln_mix_b', 'grad_w_ffn_gate', 'grad_w_ffn_up', 'grad_w_ffn_down', 'grad_w_ple_gate', 'grad_w_ple_proj', 'grad_ln_ffn_g', 'grad_ln_ffn_b', 'delta_w_in', 'delta_ssd_conv_w', 'delta_ssd_conv_b', 'delta_ssd_dt_bias', 'delta_ssd_A_log', 'delta_ssd_D', 'delta_ssd_norm_w', 'delta_mla_q_norm_w', 'delta_mla_w_q_b', 'delta_mla_kv_norm_w', 'delta_mla_w_kv_b', 'delta_mla_out_norm_w', 'delta_w_out', 'delta_ln_mix_g', 'delta_ln_mix_b', 'delta_w_ffn_gate', 'delta_w_ffn_up', 'delta_w_ffn_down', 'delta_w_ple_gate', 'delta_w_ple_proj', 'delta_ln_ffn_g', 'delta_ln_ffn_b', 'new_m_w_in', 'new_m_ssd_conv_w', 'new_m_ssd_conv_b', 'new_m_ssd_dt_bias', 'new_m_ssd_A_log', 'new_m_ssd_D', 'new_m_ssd_norm_w', 'new_m_mla_q_norm_w', 'new_m_mla_w_q_b', 'new_m_mla_kv_norm_w', 'new_m_mla_w_kv_b', 'new_m_mla_out_norm_w', 'new_m_w_out', 'new_m_ln_mix_g', 'new_m_ln_mix_b', 'new_m_w_ffn_gate', 'new_m_w_ffn_up', 'new_m_w_ffn_down', 'new_m_w_ple_gate', 'new_m_w_ple_proj', 'new_m_ln_ffn_g', 'new_m_ln_ffn_b', 'new_v_w_in', 'new_v_ssd_conv_w', 'new_v_ssd_conv_b', 'new_v_ssd_dt_bias', 'new_v_ssd_A_log', 'new_v_ssd_D', 'new_v_ssd_norm_w', 'new_v_mla_q_norm_w', 'new_v_mla_w_q_b', 'new_v_mla_kv_norm_w', 'new_v_mla_w_kv_b', 'new_v_mla_out_norm_w', 'new_v_w_out', 'new_v_ln_mix_g', 'new_v_ln_mix_b', 'new_v_w_ffn_gate', 'new_v_w_ffn_up', 'new_v_w_ffn_down', 'new_v_w_ple_gate', 'new_v_w_ple_proj', 'new_v_ln_ffn_g', 'new_v_ln_ffn_b']
TWIN_LEAF_KINDS = {'loss': 'loss', 'grad_x': 'grad_x', 'grad_w_in': 'grad_w', 'grad_ssd_conv_w': 'grad_w', 'grad_ssd_conv_b': 'grad_w', 'grad_ssd_dt_bias': 'grad_w', 'grad_ssd_A_log': 'grad_w', 'grad_ssd_D': 'grad_w', 'grad_ssd_norm_w': 'grad_w', 'grad_mla_q_norm_w': 'grad_w', 'grad_mla_w_q_b': 'grad_w', 'grad_mla_kv_norm_w': 'grad_w', 'grad_mla_w_kv_b': 'grad_w', 'grad_mla_out_norm_w': 'grad_w', 'grad_w_out': 'grad_w', 'grad_ln_mix_g': 'grad_w', 'grad_ln_mix_b': 'grad_w', 'grad_w_ffn_gate': 'grad_w', 'grad_w_ffn_up': 'grad_w', 'grad_w_ffn_down': 'grad_w', 'grad_w_ple_gate': 'grad_w', 'grad_w_ple_proj': 'grad_w', 'grad_ln_ffn_g': 'grad_w', 'grad_ln_ffn_b': 'grad_w', 'delta_w_in': 'delta_w', 'delta_ssd_conv_w': 'delta_w', 'delta_ssd_conv_b': 'delta_w', 'delta_ssd_dt_bias': 'delta_w', 'delta_ssd_A_log': 'delta_w', 'delta_ssd_D': 'delta_w', 'delta_ssd_norm_w': 'delta_w', 'delta_mla_q_norm_w': 'delta_w', 'delta_mla_w_q_b': 'delta_w', 'delta_mla_kv_norm_w': 'delta_w', 'delta_mla_w_kv_b': 'delta_w', 'delta_mla_out_norm_w': 'delta_w', 'delta_w_out': 'delta_w', 'delta_ln_mix_g': 'delta_w', 'delta_ln_mix_b': 'delta_w', 'delta_w_ffn_gate': 'delta_w', 'delta_w_ffn_up': 'delta_w', 'delta_w_ffn_down': 'delta_w', 'delta_w_ple_gate': 'delta_w', 'delta_w_ple_proj': 'delta_w', 'delta_ln_ffn_g': 'delta_w', 'delta_ln_ffn_b': 'delta_w', 'new_m_w_in': 'new_m', 'new_m_ssd_conv_w': 'new_m', 'new_m_ssd_conv_b': 'new_m', 'new_m_ssd_dt_bias': 'new_m', 'new_m_ssd_A_log': 'new_m', 'new_m_ssd_D': 'new_m', 'new_m_ssd_norm_w': 'new_m', 'new_m_mla_q_norm_w': 'new_m', 'new_m_mla_w_q_b': 'new_m', 'new_m_mla_kv_norm_w': 'new_m', 'new_m_mla_w_kv_b': 'new_m', 'new_m_mla_out_norm_w': 'new_m', 'new_m_w_out': 'new_m', 'new_m_ln_mix_g': 'new_m', 'new_m_ln_mix_b': 'new_m', 'new_m_w_ffn_gate': 'new_m', 'new_m_w_ffn_up': 'new_m', 'new_m_w_ffn_down': 'new_m', 'new_m_w_ple_gate': 'new_m', 'new_m_w_ple_proj': 'new_m', 'new_m_ln_ffn_g': 'new_m', 'new_m_ln_ffn_b': 'new_m', 'new_v_w_in': 'new_v', 'new_v_ssd_conv_w': 'new_v', 'new_v_ssd_conv_b': 'new_v', 'new_v_ssd_dt_bias': 'new_v', 'new_v_ssd_A_log': 'new_v', 'new_v_ssd_D': 'new_v', 'new_v_ssd_norm_w': 'new_v', 'new_v_mla_q_norm_w': 'new_v', 'new_v_mla_w_q_b': 'new_v', 'new_v_mla_kv_norm_w': 'new_v', 'new_v_mla_w_kv_b': 'new_v', 'new_v_mla_out_norm_w': 'new_v', 'new_v_w_out': 'new_v', 'new_v_ln_mix_g': 'new_v', 'new_v_ln_mix_b': 'new_v', 'new_v_w_ffn_gate': 'new_v', 'new_v_w_ffn_up': 'new_v', 'new_v_w_ffn_down': 'new_v', 'new_v_w_ple_gate': 'new_v', 'new_v_w_ple_proj': 'new_v', 'new_v_ln_ffn_g': 'new_v', 'new_v_ln_ffn_b': 'new_v'}


def _forward(args):
    return _fwd_reference(*[args[k] for k in FWD_PARAMS])


def _output_shape():
    out = _jax.eval_shape(lambda: _forward(_fwd_setup_inputs(0)))
    return out.shape, out.dtype

N_MICROBATCH = 1
ADAM_LR = 0.001
ADAM_B1 = 0.9
ADAM_B2 = 0.999
ADAM_EPS = 1e-08
ADAM_WD = 0.01
ADAM_STEP = 10
PER_EXAMPLE_BATCH_AXIS = {'x': 0, 'p': 1, 'positions': 0, 'loss_target': 0}
SHARED_INPUTS = []
_WEIGHT_DTYPES = {'w_in': _jnp.float32, 'ssd_conv_w': _jnp.float32, 'ssd_conv_b': _jnp.float32, 'ssd_dt_bias': _jnp.float32, 'ssd_A_log': _jnp.float32, 'ssd_D': _jnp.float32, 'ssd_norm_w': _jnp.float32, 'mla_q_norm_w': _jnp.float32, 'mla_w_q_b': _jnp.float32, 'mla_kv_norm_w': _jnp.float32, 'mla_w_kv_b': _jnp.float32, 'mla_out_norm_w': _jnp.float32, 'w_out': _jnp.float32, 'ln_mix_g': _jnp.float32, 'ln_mix_b': _jnp.float32, 'w_ffn_gate': _jnp.float32, 'w_ffn_up': _jnp.float32, 'w_ffn_down': _jnp.float32, 'w_ple_gate': _jnp.float32, 'w_ple_proj': _jnp.float32, 'ln_ffn_g': _jnp.float32, 'ln_ffn_b': _jnp.float32}
MOMENT_SCALE = {'w_in': 5.011135e-02, 'ssd_conv_w': 3.582686e-02, 'ssd_conv_b': 5.335818e-02, 'ssd_dt_bias': 9.683354e-02, 'ssd_A_log': 2.714500e-01, 'ssd_D': 2.647727e-01, 'ssd_norm_w': 4.689318e-02, 'mla_q_norm_w': 6.946823e-02, 'mla_w_q_b': 3.338329e-02, 'mla_kv_norm_w': 1.492064e-01, 'mla_w_kv_b': 3.999618e-02, 'mla_out_norm_w': 4.332435e-02, 'w_out': 1.002598e-01, 'ln_mix_g': 4.888751e-01, 'ln_mix_b': 2.685733e-01, 'w_ffn_gate': 2.158740e-02, 'w_ffn_up': 2.091699e-02, 'w_ffn_down': 5.836403e-02, 'w_ple_gate': 1.227736e-02, 'w_ple_proj': 5.294165e-02, 'ln_ffn_g': 1.600529e+01, 'ln_ffn_b': 1.144861e+00}


def _to_microbatches(a, axis):
    t = _jnp.moveaxis(a, axis, 0)
    t = t.reshape((N_MICROBATCH, t.shape[0] // N_MICROBATCH) + t.shape[1:])
    return _jnp.moveaxis(t, 1, axis + 1)


def setup_inputs(seed: int = 0) -> dict:
    inp = _fwd_setup_inputs(seed)
    key = _jax.random.fold_in(_jax.random.key(seed), 7919)
    shape, _ = _output_shape()
    out = dict(inp)
    out["loss_target"] = _jax.random.normal(_jax.random.fold_in(key, 0), shape, _jnp.float32)
    for i, name in enumerate(TWIN_WEIGHTS):
        w = inp[name].astype(_jnp.float32)
        if MOMENT_SCALE is None:
            s = _jnp.sqrt(_jnp.mean(_jnp.square(w)) + 1e-30)
        else:
            s = MOMENT_SCALE[name]
        km, kv = _jax.random.split(_jax.random.fold_in(key, i + 1))
        out[name] = w
        out["m_" + name] = s * _jax.random.normal(km, w.shape, _jnp.float32)
        out["v_" + name] = (s * s) * _jax.random.uniform(kv, w.shape, _jnp.float32, 0.5, 1.5)
    if N_MICROBATCH > 1:
        for name, axis in PER_EXAMPLE_BATCH_AXIS.items():
            out[name] = _to_microbatches(out[name], axis)
    return {'x': out['x'], 'p': out['p'], 'positions': out['positions'], 'w_in': out['w_in'], 'ssd_conv_w': out['ssd_conv_w'], 'ssd_conv_b': out['ssd_conv_b'], 'ssd_dt_bias': out['ssd_dt_bias'], 'ssd_A_log': out['ssd_A_log'], 'ssd_D': out['ssd_D'], 'ssd_norm_w': out['ssd_norm_w'], 'mla_q_norm_w': out['mla_q_norm_w'], 'mla_w_q_b': out['mla_w_q_b'], 'mla_kv_norm_w': out['mla_kv_norm_w'], 'mla_w_kv_b': out['mla_w_kv_b'], 'mla_out_norm_w': out['mla_out_norm_w'], 'w_out': out['w_out'], 'ln_mix_g': out['ln_mix_g'], 'ln_mix_b': out['ln_mix_b'], 'w_ffn_gate': out['w_ffn_gate'], 'w_ffn_up': out['w_ffn_up'], 'w_ffn_down': out['w_ffn_down'], 'w_ple_gate': out['w_ple_gate'], 'w_ple_proj': out['w_ple_proj'], 'ln_ffn_g': out['ln_ffn_g'], 'ln_ffn_b': out['ln_ffn_b'], 'loss_target': out['loss_target'], 'm_w_in': out['m_w_in'], 'm_ssd_conv_w': out['m_ssd_conv_w'], 'm_ssd_conv_b': out['m_ssd_conv_b'], 'm_ssd_dt_bias': out['m_ssd_dt_bias'], 'm_ssd_A_log': out['m_ssd_A_log'], 'm_ssd_D': out['m_ssd_D'], 'm_ssd_norm_w': out['m_ssd_norm_w'], 'm_mla_q_norm_w': out['m_mla_q_norm_w'], 'm_mla_w_q_b': out['m_mla_w_q_b'], 'm_mla_kv_norm_w': out['m_mla_kv_norm_w'], 'm_mla_w_kv_b': out['m_mla_w_kv_b'], 'm_mla_out_norm_w': out['m_mla_out_norm_w'], 'm_w_out': out['m_w_out'], 'm_ln_mix_g': out['m_ln_mix_g'], 'm_ln_mix_b': out['m_ln_mix_b'], 'm_w_ffn_gate': out['m_w_ffn_gate'], 'm_w_ffn_up': out['m_w_ffn_up'], 'm_w_ffn_down': out['m_w_ffn_down'], 'm_w_ple_gate': out['m_w_ple_gate'], 'm_w_ple_proj': out['m_w_ple_proj'], 'm_ln_ffn_g': out['m_ln_ffn_g'], 'm_ln_ffn_b': out['m_ln_ffn_b'], 'v_w_in': out['v_w_in'], 'v_ssd_conv_w': out['v_ssd_conv_w'], 'v_ssd_conv_b': out['v_ssd_conv_b'], 'v_ssd_dt_bias': out['v_ssd_dt_bias'], 'v_ssd_A_log': out['v_ssd_A_log'], 'v_ssd_D': out['v_ssd_D'], 'v_ssd_norm_w': out['v_ssd_norm_w'], 'v_mla_q_norm_w': out['v_mla_q_norm_w'], 'v_mla_w_q_b': out['v_mla_w_q_b'], 'v_mla_kv_norm_w': out['v_mla_kv_norm_w'], 'v_mla_w_kv_b': out['v_mla_w_kv_b'], 'v_mla_out_norm_w': out['v_mla_out_norm_w'], 'v_w_out': out['v_w_out'], 'v_ln_mix_g': out['v_ln_mix_g'], 'v_ln_mix_b': out['v_ln_mix_b'], 'v_w_ffn_gate': out['v_w_ffn_gate'], 'v_w_ffn_up': out['v_w_ffn_up'], 'v_w_ffn_down': out['v_w_ffn_down'], 'v_w_ple_gate': out['v_w_ple_gate'], 'v_w_ple_proj': out['v_w_ple_proj'], 'v_ln_ffn_g': out['v_ln_ffn_g'], 'v_ln_ffn_b': out['v_ln_ffn_b']}


def _loss(weights, diff, rest, loss_target):
    with _jax.named_scope("forward"):
        args = {**rest, TWIN_DIFF_INPUT: diff, **{k: w.astype(_WEIGHT_DTYPES[k]) for k, w in weights.items()}}
        y = _forward(args)
    with _jax.named_scope("loss_head"):
        err = _jnp.square(y.astype(_jnp.float32) - loss_target)
        return 0.5 * _jnp.sum(_jnp.mean(err, axis=-1)) if err.ndim else 0.5 * err


def _adamw(w, g, m, v):
    m = ADAM_B1 * m + (1.0 - ADAM_B1) * g
    v = ADAM_B2 * v + (1.0 - ADAM_B2) * _jnp.square(g)
    m_hat = m / (1.0 - ADAM_B1 ** ADAM_STEP)
    v_hat = v / (1.0 - ADAM_B2 ** ADAM_STEP)
    delta = -ADAM_LR * (m_hat / (_jnp.sqrt(v_hat) + ADAM_EPS) + ADAM_WD * w)
    return delta, m, v


def reference(x, p, positions, w_in, ssd_conv_w, ssd_conv_b, ssd_dt_bias, ssd_A_log, ssd_D, ssd_norm_w, mla_q_norm_w, mla_w_q_b, mla_kv_norm_w, mla_w_kv_b, mla_out_norm_w, w_out, ln_mix_g, ln_mix_b, w_ffn_gate, w_ffn_up, w_ffn_down, w_ple_gate, w_ple_proj, ln_ffn_g, ln_ffn_b, loss_target, m_w_in, m_ssd_conv_w, m_ssd_conv_b, m_ssd_dt_bias, m_ssd_A_log, m_ssd_D, m_ssd_norm_w, m_mla_q_norm_w, m_mla_w_q_b, m_mla_kv_norm_w, m_mla_w_kv_b, m_mla_out_norm_w, m_w_out, m_ln_mix_g, m_ln_mix_b, m_w_ffn_gate, m_w_ffn_up, m_w_ffn_down, m_w_ple_gate, m_w_ple_proj, m_ln_ffn_g, m_ln_ffn_b, v_w_in, v_ssd_conv_w, v_ssd_conv_b, v_ssd_dt_bias, v_ssd_A_log, v_ssd_D, v_ssd_norm_w, v_mla_q_norm_w, v_mla_w_q_b, v_mla_kv_norm_w, v_mla_w_kv_b, v_mla_out_norm_w, v_w_out, v_ln_mix_g, v_ln_mix_b, v_w_ffn_gate, v_w_ffn_up, v_w_ffn_down, v_w_ple_gate, v_w_ple_proj, v_ln_ffn_g, v_ln_ffn_b):
    given = dict(x=x, p=p, positions=positions, w_in=w_in, ssd_conv_w=ssd_conv_w, ssd_conv_b=ssd_conv_b, ssd_dt_bias=ssd_dt_bias, ssd_A_log=ssd_A_log, ssd_D=ssd_D, ssd_norm_w=ssd_norm_w, mla_q_norm_w=mla_q_norm_w, mla_w_q_b=mla_w_q_b, mla_kv_norm_w=mla_kv_norm_w, mla_w_kv_b=mla_w_kv_b, mla_out_norm_w=mla_out_norm_w, w_out=w_out, ln_mix_g=ln_mix_g, ln_mix_b=ln_mix_b, w_ffn_gate=w_ffn_gate, w_ffn_up=w_ffn_up, w_ffn_down=w_ffn_down, w_ple_gate=w_ple_gate, w_ple_proj=w_ple_proj, ln_ffn_g=ln_ffn_g, ln_ffn_b=ln_ffn_b, loss_target=loss_target, m_w_in=m_w_in, m_ssd_conv_w=m_ssd_conv_w, m_ssd_conv_b=m_ssd_conv_b, m_ssd_dt_bias=m_ssd_dt_bias, m_ssd_A_log=m_ssd_A_log, m_ssd_D=m_ssd_D, m_ssd_norm_w=m_ssd_norm_w, m_mla_q_norm_w=m_mla_q_norm_w, m_mla_w_q_b=m_mla_w_q_b, m_mla_kv_norm_w=m_mla_kv_norm_w, m_mla_w_kv_b=m_mla_w_kv_b, m_mla_out_norm_w=m_mla_out_norm_w, m_w_out=m_w_out, m_ln_mix_g=m_ln_mix_g, m_ln_mix_b=m_ln_mix_b, m_w_ffn_gate=m_w_ffn_gate, m_w_ffn_up=m_w_ffn_up, m_w_ffn_down=m_w_ffn_down, m_w_ple_gate=m_w_ple_gate, m_w_ple_proj=m_w_ple_proj, m_ln_ffn_g=m_ln_ffn_g, m_ln_ffn_b=m_ln_ffn_b, v_w_in=v_w_in, v_ssd_conv_w=v_ssd_conv_w, v_ssd_conv_b=v_ssd_conv_b, v_ssd_dt_bias=v_ssd_dt_bias, v_ssd_A_log=v_ssd_A_log, v_ssd_D=v_ssd_D, v_ssd_norm_w=v_ssd_norm_w, v_mla_q_norm_w=v_mla_q_norm_w, v_mla_w_q_b=v_mla_w_q_b, v_mla_kv_norm_w=v_mla_kv_norm_w, v_mla_w_kv_b=v_mla_w_kv_b, v_mla_out_norm_w=v_mla_out_norm_w, v_w_out=v_w_out, v_ln_mix_g=v_ln_mix_g, v_ln_mix_b=v_ln_mix_b, v_w_ffn_gate=v_w_ffn_gate, v_w_ffn_up=v_w_ffn_up, v_w_ffn_down=v_w_ffn_down, v_w_ple_gate=v_w_ple_gate, v_w_ple_proj=v_w_ple_proj, v_ln_ffn_g=v_ln_ffn_g, v_ln_ffn_b=v_ln_ffn_b)
    weights = {n: given[n] for n in TWIN_WEIGHTS}
    shared = {n: given[n] for n in SHARED_INPUTS}
    per_example = {n: given[n] for n in ['x', 'p', 'positions']}
    grad_fn = _jax.value_and_grad(_loss, argnums=(0, 1))

    def one_microbatch(ex, loss_target):
        ex = dict(ex)
        diff = ex.pop(TWIN_DIFF_INPUT)
        return grad_fn(weights, diff, {**shared, **ex}, loss_target)

    if N_MICROBATCH == 1:
        loss, (grad_w, grad_x) = one_microbatch(per_example, given["loss_target"])
    else:
        def body(carry, xs):
            loss_sum, grad_sum = carry
            l_k, (gw_k, gx_k) = one_microbatch(xs[0], xs[1])
            with _jax.named_scope("update"):
                return (loss_sum + l_k, _jax.tree.map(_jnp.add, grad_sum, gw_k)), gx_k

        init = (_jnp.zeros((), _jnp.float32), _jax.tree.map(_jnp.zeros_like, weights))
        (loss, grad_w), grad_x = _jax.lax.scan(body, init, (per_example, given["loss_target"]))
    with _jax.named_scope("update"):
        delta_w, new_m, new_v = {}, {}, {}
        for n in TWIN_WEIGHTS:
            delta_w[n], new_m[n], new_v[n] = _adamw(weights[n], grad_w[n], given["m_" + n], given["v_" + n])
    return (loss, grad_x, *[grad_w[n] for n in TWIN_WEIGHTS], *[delta_w[n] for n in TWIN_WEIGHTS],
            *[new_m[n] for n in TWIN_WEIGHTS], *[new_v[n] for n in TWIN_WEIGHTS])
```

```python
import functools
import math

import jax
import jax.numpy as jnp
from jax import lax
from jax.experimental import pallas as pl
from jax.experimental.pallas import tpu as pltpu

F32 = jnp.float32
BF16 = jnp.bfloat16

S = 2048
D = 1024
PLE = 256
H = 16
SSD_P = 64
SSD_INNER = 1024
SSD_N = 128
SSD_G = 2
SSD_L = 128
SSD_NC = S // SSD_L
SSD_XBC = 1536
SSD_K = 4
Q_RANK = 384
KV_RANK = 256
NOPE = 64
ROPE = 32
VDIM = 64
D_FF = 2816
IN_WIDTH = 3248
ALPHA = 2.0 ** 0.25
EPS_RMS = 1e-6
EPS_LN = 1e-5
ATT_SCALE = 1.0 / math.sqrt(NOPE + ROPE)
LANE = 128
SMALL_W = 896
SM_Q, SM_KV, SM_KR, SM_DT = 0, 384, 640, 768
NEG = -1e30

ADAM_LR = 0.001
ADAM_B1 = 0.9
ADAM_B2 = 0.999
ADAM_EPS = 1e-08
ADAM_WD = 0.01
ADAM_STEP = 10


def _sigmoid(v):
    return 1.0 / (1.0 + jnp.exp(-v))


def _tile(n, pref=512):
    for t in (pref, 512, 384, 256, 128):
        if t <= n and n % t == 0:
            return t
    return n


def _mm(pairs, *, ta=False, tb=False, out_dtype=F32, add=None, add_scale=1.0, tm=None, tn=None, name):
    n_pairs = len(pairs)
    a0, b0 = pairs[0]
    m = a0.shape[1] if ta else a0.shape[0]
    n = b0.shape[0] if tb else b0.shape[1]
    tm = tm or _tile(m)
    tn = tn or _tile(n)
    dims = (((0 if ta else 1,), (1 if tb else 0,)), ((), ()))

    def body(*refs):
        o_ref = refs[-1]
        acc = None
        for i in range(n_pairs):
            a = refs[2 * i][...].astype(BF16)
            b = refs[2 * i + 1][...].astype(BF16)
            part = lax.dot_general(a, b, dims, preferred_element_type=F32)
            acc = part if acc is None else acc + part
        if add is not None:
            acc = acc + add_scale * refs[2 * n_pairs][...]
        o_ref[...] = acc.astype(out_dtype)

    in_specs, args = [], []
    for a, b in pairs:
        k = a.shape[0] if ta else a.shape[1]
        in_specs.append(pl.BlockSpec((k, tm), lambda i, j: (0, i)) if ta else pl.BlockSpec((tm, k), lambda i, j: (i, 0)))
        in_specs.append(pl.BlockSpec((tn, k), lambda i, j: (j, 0)) if tb else pl.BlockSpec((k, tn), lambda i, j: (0, j)))
        args += [a, b]
    if add is not None:
        in_specs.append(pl.BlockSpec((tm, tn), lambda i, j: (i, j)))
        args.append(add)
    return pl.pallas_call(
        body, name=name, grid=(m // tm, n // tn), in_specs=in_specs,
        out_specs=pl.BlockSpec((tm, tn), lambda i, j: (i, j)),
        out_shape=jax.ShapeDtypeStruct((m, n), out_dtype),
        compiler_params=pltpu.CompilerParams(dimension_semantics=("parallel", "parallel")),
    )(*args)


TR = 256


def _row_spec(c):
    return pl.BlockSpec((TR, c), lambda i: (i, 0))


def _vec_spec(c):
    return pl.BlockSpec((1, c), lambda i: (0, 0))


def _acc_rows(ref, val):
    @pl.when(pl.program_id(0) == 0)
    def _():
        ref[...] = jnp.zeros_like(ref)
    ref[...] += val


def _rms_fwd(u, w, *, name):
    c = u.shape[1]

    def body(u_ref, w_ref, o_ref):
        v = u_ref[...]
        r = lax.rsqrt(jnp.mean(v * v, axis=-1, keepdims=True) + EPS_RMS)
        o_ref[...] = (v * r * w_ref[...]).astype(BF16)

    return pl.pallas_call(body, name=name, grid=(S // TR,), in_specs=[_row_spec(c), _vec_spec(c)], out_specs=_row_spec(c),
                          out_shape=jax.ShapeDtypeStruct((S, c), BF16))(u, w)


def _rms_bwd(u, w, dy, *, name):
    c = u.shape[1]

    def body(u_ref, w_ref, dy_ref, du_ref, dw_ref):
        v = u_ref[...]
        g = dy_ref[...].astype(F32)
        r = lax.rsqrt(jnp.mean(v * v, axis=-1, keepdims=True) + EPS_RMS)
        gw = g * w_ref[...]
        du_ref[...] = r * gw - v * (r * r * r * jnp.mean(gw * v, axis=-1, keepdims=True))
        _acc_rows(dw_ref, jnp.sum(g * v * r, axis=0, keepdims=True))

    return pl.pallas_call(body, name=name, grid=(S // TR,), in_specs=[_row_spec(c), _vec_spec(c), _row_spec(c)],
                          out_specs=[_row_spec(c), _vec_spec(c)],
                          out_shape=[jax.ShapeDtypeStruct((S, c), F32), jax.ShapeDtypeStruct((1, c), F32)])(u, w, dy)


def _gate_norm_fwd(y, z, w):
    def body(y_ref, z_ref, w_ref, o_ref):
        zz = z_ref[...]
        v = y_ref[...] * (zz * _sigmoid(zz))
        r = lax.rsqrt(jnp.mean(v * v, axis=-1, keepdims=True) + EPS_RMS)
        o_ref[...] = (v * r * w_ref[...]).astype(BF16)

    c = SSD_INNER
    return pl.pallas_call(body, name="ssd_gate_norm_fwd", grid=(S // TR,), in_specs=[_row_spec(c), _row_spec(c), _vec_spec(c)],
                          out_specs=_row_spec(c), out_shape=jax.ShapeDtypeStruct((S, c), BF16))(y, z, w)


def _gate_norm_bwd(y, z, w, dout):
    def body(y_ref, z_ref, w_ref, g_ref, dy_ref, dz_ref, dw_ref):
        yy = y_ref[...]
        zz = z_ref[...]
        sg = _sigmoid(zz)
        sz = zz * sg
        v = yy * sz
        g = g_ref[...]
        r = lax.rsqrt(jnp.mean(v * v, axis=-1, keepdims=True) + EPS_RMS)
        gw = g * w_ref[...]
        dv = r * gw - v * (r * r * r * jnp.mean(gw * v, axis=-1, keepdims=True))
        dy_ref[...] = dv * sz
        dz_ref[...] = dv * yy * (sg * (1.0 + zz * (1.0 - sg)))
        _acc_rows(dw_ref, jnp.sum(g * v * r, axis=0, keepdims=True))

    c = SSD_INNER
    return pl.pallas_call(body, name="ssd_gate_norm_bwd", grid=(S // TR,),
                          in_specs=[_row_spec(c), _row_spec(c), _vec_spec(c), _row_spec(c)],
                          out_specs=[_row_spec(c), _row_spec(c), _vec_spec(c)],
                          out_shape=[jax.ShapeDtypeStruct((S, c), F32), jax.ShapeDtypeStruct((S, c), F32),
                                     jax.ShapeDtypeStruct((1, c), F32)])(y, z, w, dout)


def _ln_fwd(xr, mix, g, b):
    def body(x_ref, m_ref, g_ref, b_ref, o_ref):
        pre = ALPHA * x_ref[...] + m_ref[...]
        mu = jnp.mean(pre, axis=-1, keepdims=True)
        d = pre - mu
        rs = lax.rsqrt(jnp.mean(d * d, axis=-1, keepdims=True) + EPS_LN)
        o_ref[...] = d * rs * g_ref[...] + b_ref[...]

    return pl.pallas_call(body, name="ln_mix_fwd", grid=(S // TR,), in_specs=[_row_spec(D), _row_spec(D), _vec_spec(D), _vec_spec(D)],
                          out_specs=_row_spec(D), out_shape=jax.ShapeDtypeStruct((S, D), F32))(xr, mix, g, b)


def _ln_bwd(xr, mix, g, dh):
    def body(x_ref, m_ref, g_ref, dh_ref, dpre_ref, dg_ref, db_ref):
        pre = ALPHA * x_ref[...] + m_ref[...]
        mu = jnp.mean(pre, axis=-1, keepdims=True)
        d = pre - mu
        rs = lax.rsqrt(jnp.mean(d * d, axis=-1, keepdims=True) + EPS_LN)
        xh = d * rs
        dy = dh_ref[...]
        gy = dy * g_ref[...]
        dpre_ref[...] = rs * (gy - jnp.mean(gy, axis=-1, keepdims=True) - xh * jnp.mean(gy * xh, axis=-1, keepdims=True))
        _acc_rows(dg_ref, jnp.sum(dy * xh, axis=0, keepdims=True))
        _acc_rows(db_ref, jnp.sum(dy, axis=0, keepdims=True))

    return pl.pallas_call(body, name="ln_mix_bwd", grid=(S // TR,),
                          in_specs=[_row_spec(D), _row_spec(D), _vec_spec(D), _row_spec(D)],
                          out_specs=[_row_spec(D), _vec_spec(D), _vec_spec(D)],
                          out_shape=[jax.ShapeDtypeStruct((S, D), F32), jax.ShapeDtypeStruct((1, D), F32),
                                     jax.ShapeDtypeStruct((1, D), F32)])(xr, mix, g, dh)


def _swiglu_fwd(gate, up):
    def body(g_ref, u_ref, o_ref):
        g = g_ref[...]
        o_ref[...] = (g * _sigmoid(g) * u_ref[...]).astype(BF16)

    c = D_FF
    return pl.pallas_call(body, name="swiglu_fwd", grid=(S // TR,), in_specs=[_row_spec(c), _row_spec(c)], out_specs=_row_spec(c),
                          out_shape=jax.ShapeDtypeStruct((S, c), BF16))(gate, up)


def _swiglu_bwd(gate, up, dact):
    def body(g_ref, u_ref, d_ref, dg_ref, du_ref):
        g = g_ref[...]
        sg = _sigmoid(g)
        d = d_ref[...]
        dg_ref[...] = (d * u_ref[...] * (sg * (1.0 + g * (1.0 - sg)))).astype(BF16)
        du_ref[...] = (d * g * sg).astype(BF16)

    c = D_FF
    return pl.pallas_call(body, name="swiglu_bwd", grid=(S // TR,), in_specs=[_row_spec(c)] * 3, out_specs=[_row_spec(c)] * 2,
                          out_shape=[jax.ShapeDtypeStruct((S, c), BF16)] * 2)(gate, up, dact)


def _final_fwd_bwd(h1, ffn, pg, pp, target, g2, b2):
    def body(h_ref, f_ref, pg_ref, pp_ref, t_ref, g_ref, b_ref, dpre_ref, dpg_ref, dpp_ref, dg_ref, db_ref, loss_ref):
        sg = _sigmoid(pg_ref[...])
        ppv = pp_ref[...]
        pre = ALPHA * h_ref[...] + f_ref[...] + sg * ppv
        mu = jnp.mean(pre, axis=-1, keepdims=True)
        d = pre - mu
        rs = lax.rsqrt(jnp.mean(d * d, axis=-1, keepdims=True) + EPS_LN)
        xh = d * rs
        err = xh * g_ref[...] + b_ref[...] - t_ref[...]
        dy = err * (1.0 / D)
        gy = dy * g_ref[...]
        dpre = rs * (gy - jnp.mean(gy, axis=-1, keepdims=True) - xh * jnp.mean(gy * xh, axis=-1, keepdims=True))
        dpre_ref[...] = dpre
        dpg_ref[...] = (dpre * ppv * sg * (1.0 - sg)).astype(BF16)
        dpp_ref[...] = (dpre * sg).astype(BF16)
        _acc_rows(dg_ref, jnp.sum(dy * xh, axis=0, keepdims=True))
        _acc_rows(db_ref, jnp.sum(dy, axis=0, keepdims=True))
        _acc_rows(loss_ref, 0.5 * jnp.sum(jnp.mean(err * err, axis=-1, keepdims=True), axis=0, keepdims=True) * jnp.ones((1, LANE), F32))

    return pl.pallas_call(
        body, name="final_ln_loss", grid=(S // TR,),
        in_specs=[_row_spec(D)] * 5 + [_vec_spec(D)] * 2,
        out_specs=[_row_spec(D)] * 3 + [_vec_spec(D), _vec_spec(D), _vec_spec(LANE)],
        out_shape=[jax.ShapeDtypeStruct((S, D), F32), jax.ShapeDtypeStruct((S, D), BF16), jax.ShapeDtypeStruct((S, D), BF16),
                   jax.ShapeDtypeStruct((1, D), F32), jax.ShapeDtypeStruct((1, D), F32), jax.ShapeDtypeStruct((1, LANE), F32)],
    )(h1, ffn, pg, pp, target, g2, b2)


def _rot(u, cos_t, sin_t, lane):
    partner = jnp.where(lane < NOPE + ROPE // 2, pltpu.roll(u, LANE - ROPE // 2, 1), pltpu.roll(u, ROPE // 2, 1))
    return u * cos_t + partner * sin_t


def _q_rope(qlin, cos_t, sin_t):
    def body(q_ref, c_ref, s_ref, o_ref):
        lane = lax.broadcasted_iota(jnp.int32, (TR, LANE), 1)
        c, s = c_ref[...], s_ref[...]
        for h in range(H):
            o_ref[:, h * LANE:(h + 1) * LANE] = _rot(q_ref[:, h * LANE:(h + 1) * LANE], c, s, lane).astype(BF16)

    w = H * LANE
    return pl.pallas_call(body, name="q_rope", grid=(S // TR,), in_specs=[_row_spec(w), _row_spec(LANE), _row_spec(LANE)],
                          out_specs=_row_spec(w), out_shape=jax.ShapeDtypeStruct((S, w), BF16))(qlin, cos_t, sin_t)


def _q_unrope(dq, cos_t, sin_t):
    def body(q_ref, c_ref, s_ref, o_ref):
        lane = lax.broadcasted_iota(jnp.int32, (TR, LANE), 1)
        c, s = c_ref[...], -s_ref[...]
        for h in range(H):
            o_ref[:, h * LANE:(h + 1) * LANE] = _rot(q_ref[:, h * LANE:(h + 1) * LANE], c, s, lane).astype(BF16)

    w = H * LANE
    return pl.pallas_call(body, name="q_unrope", grid=(S // TR,), in_specs=[_row_spec(w), _row_spec(LANE), _row_spec(LANE)],
                          out_specs=_row_spec(w), out_shape=jax.ShapeDtypeStruct((S, w), BF16))(dq, cos_t, sin_t)


def _k_prep(klin, small, cos_t, sin_t):
    def body(k_ref, kr_ref, c_ref, s_ref, o_ref):
        lane = lax.broadcasted_iota(jnp.int32, (TR, LANE), 1)
        kr = _rot(pltpu.roll(kr_ref[...], NOPE, 1), c_ref[...], s_ref[...], lane)
        for h in range(H):
            o_ref[:, h * LANE:(h + 1) * LANE] = (k_ref[:, h * LANE:(h + 1) * LANE] + kr).astype(BF16)

    w = H * LANE
    kr_spec = pl.BlockSpec((TR, LANE), lambda i: (i, SM_KR // LANE))
    return pl.pallas_call(body, name="k_prep", grid=(S // TR,), in_specs=[_row_spec(w), kr_spec, _row_spec(LANE), _row_spec(LANE)],
                          out_specs=_row_spec(w), out_shape=jax.ShapeDtypeStruct((S, w), BF16))(klin, small, cos_t, sin_t)


def _k_rope_bwd(dk, cos_t, sin_t):
    def body(k_ref, c_ref, s_ref, o_ref):
        lane = lax.broadcasted_iota(jnp.int32, (TR, LANE), 1)
        acc = k_ref[:, 0:LANE]
        for h in range(1, H):
            acc = acc + k_ref[:, h * LANE:(h + 1) * LANE]
        acc = jnp.where((lane >= NOPE) & (lane < NOPE + ROPE), acc, 0.0)
        o_ref[...] = pltpu.roll(_rot(acc, c_ref[...], -s_ref[...], lane), LANE - NOPE, 1)

    w = H * LANE
    return pl.pallas_call(body, name="k_rope_bwd", grid=(S // TR,), in_specs=[_row_spec(w), _row_spec(LANE), _row_spec(LANE)],
                          out_specs=_row_spec(LANE), out_shape=jax.ShapeDtypeStruct((S, LANE), F32))(dk, cos_t, sin_t)


CB = 256


def _shift_down(u, k, row):
    if k == 0:
        return u
    return jnp.where(row >= k, pltpu.roll(u, k, 0), 0.0)


def _shift_up(u, k, row):
    if k == 0:
        return u
    return jnp.where(row < S - k, pltpu.roll(u, S - k, 0), 0.0)


def _conv_fwd(u, w, b):
    def body(u_ref, w_ref, b_ref, o_ref):
        row = lax.broadcasted_iota(jnp.int32, (S, CB), 0)
        uu = u_ref[...]
        acc = b_ref[...] + w_ref[SSD_K - 1:SSD_K, :] * uu
        for k in range(SSD_K - 1):
            acc = acc + w_ref[k:k + 1, :] * _shift_down(uu, SSD_K - 1 - k, row)
        o_ref[...] = acc * _sigmoid(acc)

    c = u.shape[1]
    return pl.pallas_call(
        body, name="conv_fwd", grid=(c // CB,),
        in_specs=[pl.BlockSpec((S, CB), lambda j: (0, j)), pl.BlockSpec((SSD_K, CB), lambda j: (0, j)), pl.BlockSpec((1, CB), lambda j: (0, j))],
        out_specs=pl.BlockSpec((S, CB), lambda j: (0, j)), out_shape=jax.ShapeDtypeStruct((S, c), F32),
    )(u, w, b)


def _conv_bwd(u, w, b, dact):
    def body(u_ref, w_ref, b_ref, d_ref, du_ref, dw_ref, db_ref):
        row = lax.broadcasted_iota(jnp.int32, (S, CB), 0)
        uu = u_ref[...]
        sh = [_shift_down(uu, SSD_K - 1 - k, row) for k in range(SSD_K)]
        acc = b_ref[...]
        for k in range(SSD_K):
            acc = acc + w_ref[k:k + 1, :] * sh[k]
        sg = _sigmoid(acc)
        dacc = d_ref[...] * (sg * (1.0 + acc * (1.0 - sg)))
        du = w_ref[SSD_K - 1:SSD_K, :] * dacc
        for k in range(SSD_K - 1):
            du = du + w_ref[k:k + 1, :] * _shift_up(dacc, SSD_K - 1 - k, row)
        du_ref[...] = du
        for k in range(SSD_K):
            dw_ref[k:k + 1, :] = jnp.sum(dacc * sh[k], axis=0, keepdims=True)
        db_ref[...] = jnp.sum(dacc, axis=0, keepdims=True)

    c = u.shape[1]
    col = lambda r: pl.BlockSpec((r, CB), lambda j: (0, j))
    return pl.pallas_call(
        body, name="conv_bwd", grid=(c // CB,), in_specs=[col(S), col(SSD_K), col(1), col(S)], out_specs=[col(S), col(SSD_K), col(1)],
        out_shape=[jax.ShapeDtypeStruct((S, c), F32), jax.ShapeDtypeStruct((SSD_K, c), F32), jax.ShapeDtypeStruct((1, c), F32)],
    )(u, w, b, dact)


NPAIR = H // 2
PAIRS_PER_GROUP = NPAIR // SSD_G


def _softplus(v):
    return jnp.maximum(v, 0.0) + jnp.log(1.0 + jnp.exp(-jnp.abs(v)))


def _dot(a, b, dims):
    return lax.dot_general(a.astype(BF16), b.astype(BF16), (dims, ((), ())), preferred_element_type=F32)


def _dot3(a, b, dims, split_lhs):
    v = a if split_lhs else b
    v1 = v.astype(BF16)
    r1 = v - v1.astype(F32)
    v2 = r1.astype(BF16)
    v3 = (r1 - v2.astype(F32)).astype(BF16)
    acc = None
    for part in (v1, v2, v3):
        lhs, rhs = (part, b) if split_lhs else (a, part)
        t = lax.dot_general(lhs, rhs, (dims, ((), ())), preferred_element_type=F32)
        acc = t if acc is None else acc + t
    return acc


def _ssd_chunk_common(dt_ref, dtT_ref, prow_ref, pcol_ref):
    prow = prow_ref[...]
    pcol = pcol_ref[...]
    ri = lax.broadcasted_iota(jnp.int32, (SSD_L, SSD_L), 0)
    ci = lax.broadcasted_iota(jnp.int32, (SSD_L, SSD_L), 1)
    causal = ri >= ci
    pre_c = dt_ref[...] + prow[0:1, :]
    dtc = _softplus(pre_c)
    a_row = -jnp.exp(prow[1:2, :])
    cs_col = _dot3(causal.astype(BF16), dtc * a_row, ((1,), (0,)), False)
    dtr = _softplus(dtT_ref[...] + pcol[:, 0:1])
    a_col = -jnp.exp(pcol[:, 1:2])
    cs_row = _dot3(dtr * a_col, (ri <= ci).astype(BF16), ((1,), (0,)), True)
    return prow, causal, pre_c, dtc, a_row, cs_col, cs_row


def _ssd_fwd(act, small, dtT, prow, pcol):
    def body(x_ref, b_ref, c_ref, dt_ref, dtT_ref, prow_ref, pcol_ref, y_ref, st_ref, state):
        @pl.when(pl.program_id(0) == 0)
        def _():
            state[...] = jnp.zeros_like(state)

        prow, causal, _, dtc, _, cs_col, cs_row = _ssd_chunk_common(dt_ref, dtT_ref, prow_ref, pcol_ref)
        lo = lax.broadcasted_iota(jnp.int32, (SSD_L, LANE), 1) < SSD_P
        lo1 = lo[0:1, :]
        for g in range(SSD_G):
            bm = b_ref[:, g * SSD_N:(g + 1) * SSD_N]
            cm = c_ref[:, g * SSD_N:(g + 1) * SSD_N]
            cb = _dot(cm, bm, ((1,), (1,)))
            for qq in range(PAIRS_PER_GROUP):
                q = g * PAIRS_PER_GROUP + qq
                ha, hb = 2 * q, 2 * q + 1
                csa, csb = cs_col[:, ha:ha + 1], cs_col[:, hb:hb + 1]
                xp = x_ref[:, q * LANE:(q + 1) * LANE]
                xx = xp * jnp.where(lo, dtc[:, ha:ha + 1], dtc[:, hb:hb + 1])
                ga = cb * jnp.exp(jnp.where(causal, csa - cs_row[ha:ha + 1, :], NEG))
                gb = cb * jnp.exp(jnp.where(causal, csb - cs_row[hb:hb + 1, :], NEG))
                y = _dot(ga, jnp.where(lo, xx, 0.0), ((1,), (0,))) + _dot(gb, jnp.where(lo, 0.0, xx), ((1,), (0,)))
                s_in = state[q]
                y = y + _dot(cm, s_in, ((1,), (0,))) * jnp.where(lo, jnp.exp(csa), jnp.exp(csb))
                y = y + jnp.where(lo1, prow[2:3, ha:ha + 1], prow[2:3, hb:hb + 1]) * xp
                y_ref[:, q * LANE:(q + 1) * LANE] = y
                la, lb = csa[SSD_L - 1:SSD_L, :], csb[SSD_L - 1:SSD_L, :]
                decay = jnp.where(lo, jnp.exp(la - csa), jnp.exp(lb - csb))
                st_ref[q] = s_in
                state[q] = s_in * jnp.where(lo1, jnp.exp(la), jnp.exp(lb)) + _dot(bm, xx * decay, ((0,), (0,)))

    L = SSD_L
    return pl.pallas_call(
        body, name="ssd_fwd", grid=(SSD_NC,),
        in_specs=[pl.BlockSpec((L, SSD_INNER), lambda c: (c, 0)),
                  pl.BlockSpec((L, SSD_G * SSD_N), lambda c: (c, SSD_INNER // (SSD_G * SSD_N))),
                  pl.BlockSpec((L, SSD_G * SSD_N), lambda c: (c, SSD_INNER // (SSD_G * SSD_N) + 1)),
                  pl.BlockSpec((L, LANE), lambda c: (c, SM_DT // LANE)),
                  pl.BlockSpec((LANE, L), lambda c: (0, c)),
                  pl.BlockSpec((8, LANE), lambda c: (0, 0)), pl.BlockSpec((LANE, 8), lambda c: (0, 0))],
        out_specs=[pl.BlockSpec((L, SSD_INNER), lambda c: (c, 0)),
                   pl.BlockSpec((None, NPAIR, SSD_N, LANE), lambda c: (c, 0, 0, 0))],
        out_shape=[jax.ShapeDtypeStruct((S, SSD_INNER), F32), jax.ShapeDtypeStruct((SSD_NC, NPAIR, SSD_N, LANE), F32)],
        scratch_shapes=[pltpu.VMEM((NPAIR, SSD_N, LANE), F32)],
        compiler_params=pltpu.CompilerParams(dimension_semantics=("arbitrary",)),
    )(act, act, act, small, dtT, prow, pcol)


def _ssd_bwd(act, small, dtT, prow, pcol, states, dy):
    def body(x_ref, b_ref, c_ref, dt_ref, dtT_ref, prow_ref, pcol_ref, st_ref, dy_ref,
             dx_ref, db_ref, dc_ref, ddt_ref, dp_ref, dstate):
        @pl.when(pl.program_id(0) == 0)
        def _():
            dstate[...] = jnp.zeros_like(dstate)
            dp_ref[...] = jnp.zeros_like(dp_ref)

        prow, causal, pre_c, dtc, a_row, cs_col, cs_row = _ssd_chunk_common(dt_ref, dtT_ref, prow_ref, pcol_ref)
        lane = lax.broadcasted_iota(jnp.int32, (SSD_L, LANE), 1)
        sub = lax.broadcasted_iota(jnp.int32, (LANE, SSD_L), 0)
        rowi = lax.broadcasted_iota(jnp.int32, (SSD_L, 1), 0)
        lane1 = lane[0:1, :]
        lo = lane < SSD_P
        lo1 = lo[0:1, :]
        dcs_c = jnp.zeros((SSD_L, LANE), F32)
        dcs_r = jnp.zeros((LANE, SSD_L), F32)
        ddt_x = jnp.zeros((SSD_L, LANE), F32)
        dd_row = jnp.zeros((1, LANE), F32)
        for g in range(SSD_G):
            bm = b_ref[:, g * SSD_N:(g + 1) * SSD_N]
            cm = c_ref[:, g * SSD_N:(g + 1) * SSD_N]
            cb = _dot(cm, bm, ((1,), (1,)))
            dcb = jnp.zeros((SSD_L, SSD_L), F32)
            dbm = jnp.zeros((SSD_L, SSD_N), F32)
            dcm = jnp.zeros((SSD_L, SSD_N), F32)
            for qq in range(PAIRS_PER_GROUP):
                q = g * PAIRS_PER_GROUP + qq
                ha, hb = 2 * q, 2 * q + 1
                csa, csb = cs_col[:, ha:ha + 1], cs_col[:, hb:hb + 1]
                xp = x_ref[:, q * LANE:(q + 1) * LANE]
                dtp = jnp.where(lo, dtc[:, ha:ha + 1], dtc[:, hb:hb + 1])
                xx = xp * dtp
                lma = jnp.exp(jnp.where(causal, csa - cs_row[ha:ha + 1, :], NEG))
                lmb = jnp.exp(jnp.where(causal, csb - cs_row[hb:hb + 1, :], NEG))
                ga, gb = cb * lma, cb * lmb
                dyp = dy_ref[:, q * LANE:(q + 1) * LANE]
                dya, dyb = jnp.where(lo, dyp, 0.0), jnp.where(lo, 0.0, dyp)
                s_in = st_ref[q]
                ds_out = dstate[q]
                la, lb = csa[SSD_L - 1:SSD_L, :], csb[SSD_L - 1:SSD_L, :]
                ecs = jnp.where(lo, jnp.exp(csa), jnp.exp(csb))
                decay = jnp.where(lo, jnp.exp(la - csa), jnp.exp(lb - csb))
                cd = jnp.where(lo1, jnp.exp(la), jnp.exp(lb))
                bds = _dot(bm, ds_out, ((1,), (0,)))
                dxx = _dot(ga, dya, ((0,), (0,))) + _dot(gb, dyb, ((0,), (0,))) + bds * decay
                dga = _dot(dya, xx, ((1,), (1,)))
                dgb = _dot(dyb, xx, ((1,), (1,)))
                dsega, dsegb = dga * ga, dgb * gb
                dcb = dcb + dga * lma + dgb * lmb
                yoff = _dot(cm, s_in, ((1,), (0,))) * ecs
                dye = dyp * ecs
                dcm = dcm + _dot(dye, s_in, ((1,), (1,)))
                xd = xx * decay
                dbm = dbm + _dot(xd, ds_out, ((1,), (1,)))
                wv = xd * bds
                t1 = dyp * yoff - wv
                col_a = (jnp.sum(dsega, axis=1, keepdims=True) + jnp.sum(jnp.where(lo, t1, 0.0), axis=1, keepdims=True))
                col_b = (jnp.sum(dsegb, axis=1, keepdims=True) + jnp.sum(jnp.where(lo, 0.0, t1), axis=1, keepdims=True))
                sprod = ds_out * s_in
                end_a = jnp.sum(jnp.where(lo, wv, 0.0), keepdims=True) + jnp.exp(la) * jnp.sum(jnp.where(lo[:SSD_N], sprod, 0.0), keepdims=True)
                end_b = jnp.sum(jnp.where(lo, 0.0, wv), keepdims=True) + jnp.exp(lb) * jnp.sum(jnp.where(lo[:SSD_N], 0.0, sprod), keepdims=True)
                col_a = col_a + jnp.where(rowi == SSD_L - 1, end_a, 0.0)
                col_b = col_b + jnp.where(rowi == SSD_L - 1, end_b, 0.0)
                dcs_c = dcs_c + jnp.where(lane == ha, col_a, 0.0) + jnp.where(lane == hb, col_b, 0.0)
                dcs_r = (dcs_r + jnp.where(sub == ha, jnp.sum(dsega, axis=0, keepdims=True), 0.0)
                         + jnp.where(sub == hb, jnp.sum(dsegb, axis=0, keepdims=True), 0.0))
                dstate[q] = _dot(cm, dye, ((0,), (0,))) + cd * ds_out
                dpair = jnp.where(lo1, prow[2:3, ha:ha + 1], prow[2:3, hb:hb + 1])
                dx_ref[:, q * LANE:(q + 1) * LANE] = dxx * dtp + dpair * dyp
                t2 = dxx * xp
                ddt_x = (ddt_x + jnp.where(lane == ha, jnp.sum(jnp.where(lo, t2, 0.0), axis=1, keepdims=True), 0.0)
                         + jnp.where(lane == hb, jnp.sum(jnp.where(lo, 0.0, t2), axis=1, keepdims=True), 0.0))
                t3 = dyp * xp
                dd_row = (dd_row + jnp.where(lane1 == ha, jnp.sum(jnp.where(lo, t3, 0.0), keepdims=True), 0.0)
                          + jnp.where(lane1 == hb, jnp.sum(jnp.where(lo, 0.0, t3), keepdims=True), 0.0))
            dc_ref[:, g * SSD_N:(g + 1) * SSD_N] = dcm + _dot(dcb, bm, ((1,), (0,)))
            db_ref[:, g * SSD_N:(g + 1) * SSD_N] = dbm + _dot(dcb, cm, ((0,), (0,)))
        ri = lax.broadcasted_iota(jnp.int32, (SSD_L, SSD_L), 0)
        ci = lax.broadcasted_iota(jnp.int32, (SSD_L, SSD_L), 1)
        da = _dot3((ri <= ci).astype(BF16), dcs_c, ((1,), (0,)), False)
        da = da - _dot3(dcs_r, causal.astype(BF16), ((1,), (0,)), True).T
        ddt = ddt_x + da * a_row
        ddt_raw = ddt * _sigmoid(pre_c)
        ddt_ref[...] = ddt_raw
        da_head = jnp.sum(da * dtc, axis=0, keepdims=True) * a_row
        dp_ref[0:1, :] += jnp.sum(ddt_raw, axis=0, keepdims=True)
        dp_ref[1:2, :] += da_head
        dp_ref[2:3, :] += dd_row

    L = SSD_L
    rev = SSD_NC - 1
    bc_cols = SSD_INNER // (SSD_G * SSD_N)
    return pl.pallas_call(
        body, name="ssd_bwd", grid=(SSD_NC,),
        in_specs=[pl.BlockSpec((L, SSD_INNER), lambda c: (rev - c, 0)),
                  pl.BlockSpec((L, SSD_G * SSD_N), lambda c: (rev - c, bc_cols)),
                  pl.BlockSpec((L, SSD_G * SSD_N), lambda c: (rev - c, bc_cols + 1)),
                  pl.BlockSpec((L, LANE), lambda c: (rev - c, SM_DT // LANE)),
                  pl.BlockSpec((LANE, L), lambda c: (0, rev - c)),
                  pl.BlockSpec((8, LANE), lambda c: (0, 0)), pl.BlockSpec((LANE, 8), lambda c: (0, 0)),
                  pl.BlockSpec((None, NPAIR, SSD_N, LANE), lambda c: (rev - c, 0, 0, 0)),
                  pl.BlockSpec((L, SSD_INNER), lambda c: (rev - c, 0))],
        out_specs=[pl.BlockSpec((L, SSD_INNER), lambda c: (rev - c, 0)),
                   pl.BlockSpec((L, SSD_G * SSD_N), lambda c: (rev - c, 0)),
                   pl.BlockSpec((L, SSD_G * SSD_N), lambda c: (rev - c, 0)),
                   pl.BlockSpec((L, LANE), lambda c: (rev - c, 0)),
                   pl.BlockSpec((8, LANE), lambda c: (0, 0))],
        out_shape=[jax.ShapeDtypeStruct((S, SSD_INNER), F32), jax.ShapeDtypeStruct((S, SSD_G * SSD_N), F32),
                   jax.ShapeDtypeStruct((S, SSD_G * SSD_N), F32), jax.ShapeDtypeStruct((S, LANE), F32),
                   jax.ShapeDtypeStruct((8, LANE), F32)],
        scratch_shapes=[pltpu.VMEM((NPAIR, SSD_N, LANE), F32)],
        compiler_params=pltpu.CompilerParams(dimension_semantics=("arbitrary",)),
    )(act, act, act, small, dtT, prow, pcol, states, dy)


TQ = 256
TK = 256


def _attn_fwd(qc, kc, v):
    def body(q_ref, k_ref, v_ref, o_ref, lse_ref):
        i = pl.program_id(1)
        lo = lax.broadcasted_iota(jnp.int32, (TQ, LANE), 1) < VDIM
        rq = i * TQ + lax.broadcasted_iota(jnp.int32, (TQ, TK), 0)
        ck0 = lax.broadcasted_iota(jnp.int32, (TQ, TK), 1)
        qa, qb = q_ref[:, 0:LANE], q_ref[:, LANE:2 * LANE]

        def step(kb, carry):
            ma, la, mb, lb, acc = carry
            off = pl.multiple_of(kb * TK, TK)
            kk = k_ref[pl.ds(off, TK), :]
            vv = v_ref[pl.ds(off, TK), :]
            mask = rq >= ck0 + kb * TK
            sa = jnp.where(mask, _dot(qa, kk[:, 0:LANE], ((1,), (1,))) * ATT_SCALE, NEG)
            sb = jnp.where(mask, _dot(qb, kk[:, LANE:2 * LANE], ((1,), (1,))) * ATT_SCALE, NEG)
            na = jnp.maximum(ma, jnp.max(sa, axis=1, keepdims=True))
            nb = jnp.maximum(mb, jnp.max(sb, axis=1, keepdims=True))
            pa, pb = jnp.exp(sa - na), jnp.exp(sb - nb)
            fa, fb = jnp.exp(ma - na), jnp.exp(mb - nb)
            la = fa * la + jnp.sum(pa, axis=1, keepdims=True)
            lb = fb * lb + jnp.sum(pb, axis=1, keepdims=True)
            acc = (acc * jnp.where(lo, fa, fb) + _dot(pa, jnp.where(lo, vv, 0), ((1,), (0,)))
                   + _dot(pb, jnp.where(lo, 0, vv), ((1,), (0,))))
            return na, la, nb, lb, acc

        neg = jnp.full((TQ, 1), NEG, F32)
        zero = jnp.zeros((TQ, 1), F32)
        ma, la, mb, lb, acc = lax.fori_loop(0, i + 1, step, (neg, zero, neg, zero, jnp.zeros((TQ, LANE), F32)))
        o_ref[...] = acc / jnp.where(lo, la, lb)
        lse_ref[...] = jnp.where(lo, ma + jnp.log(la), mb + jnp.log(lb))

    return pl.pallas_call(
        body, name="attn_fwd", grid=(NPAIR, S // TQ),
        in_specs=[pl.BlockSpec((TQ, 2 * LANE), lambda j, i: (i, j)), pl.BlockSpec((S, 2 * LANE), lambda j, i: (0, j)),
                  pl.BlockSpec((S, LANE), lambda j, i: (0, j))],
        out_specs=[pl.BlockSpec((TQ, LANE), lambda j, i: (i, j)), pl.BlockSpec((None, TQ, LANE), lambda j, i: (j, i, 0))],
        out_shape=[jax.ShapeDtypeStruct((S, H * VDIM), F32), jax.ShapeDtypeStruct((NPAIR, S, LANE), F32)],
        compiler_params=pltpu.CompilerParams(dimension_semantics=("parallel", "parallel")),
    )(qc, kc, v)


def _attn_bwd(qc, kc, v, o, lse, do):
    nq = S // TQ

    def body(q_ref, k_ref, v_ref, o_ref, lse_ref, do_ref, dq_ref, dk_ref, dv_ref):
        kb = pl.program_id(1)

        @pl.when(kb == 0)
        def _():
            dq_ref[...] = jnp.zeros_like(dq_ref)

        lo = lax.broadcasted_iota(jnp.int32, (TQ, LANE), 1) < VDIM
        r0 = lax.broadcasted_iota(jnp.int32, (TQ, TK), 0)
        ck = kb * TK + lax.broadcasted_iota(jnp.int32, (TQ, TK), 1)
        ka, kbb = k_ref[:, 0:LANE], k_ref[:, LANE:2 * LANE]
        vv = v_ref[...]

        def step(qi, carry):
            dka, dkb, dv = carry
            off = pl.multiple_of(qi * TQ, TQ)
            qq = q_ref[pl.ds(off, TQ), :]
            dd = do_ref[pl.ds(off, TQ), :]
            ls = lse_ref[pl.ds(off, TQ), :]
            t = dd * o_ref[pl.ds(off, TQ), :]
            mask = r0 + qi * TQ >= ck
            outs = []
            for x, (kx, lsx) in enumerate(((ka, ls[:, 0:1]), (kbb, ls[:, VDIM:VDIM + 1]))):
                sel = lo if x == 0 else jnp.logical_not(lo)
                qx = qq[:, x * LANE:(x + 1) * LANE]
                dox = jnp.where(sel, dd, 0.0)
                delta = jnp.sum(jnp.where(sel, t, 0.0), axis=1, keepdims=True)
                sc = jnp.where(mask, _dot(qx, kx, ((1,), (1,))) * ATT_SCALE, NEG)
                p = jnp.exp(sc - lsx)
                dp = _dot(dox, vv, ((1,), (1,)))
                ds = p * (dp - delta) * ATT_SCALE
                dv = dv + _dot(p, dox, ((0,), (0,)))
                outs.append(_dot(ds, qx, ((0,), (0,))))
                dq_ref[pl.ds(off, TQ), x * LANE:(x + 1) * LANE] += _dot(ds, kx, ((1,), (0,)))
            return dka + outs[0], dkb + outs[1], dv

        z = jnp.zeros((TK, LANE), F32)
        dka, dkb, dv = lax.fori_loop(kb, nq, step, (z, z, z))
        dk_ref[:, 0:LANE] = dka
        dk_ref[:, LANE:2 * LANE] = dkb
        dv_ref[...] = dv

    return pl.pallas_call(
        body, name="attn_bwd", grid=(NPAIR, S // TK),
        in_specs=[pl.BlockSpec((S, 2 * LANE), lambda j, k: (0, j)), pl.BlockSpec((TK, 2 * LANE), lambda j, k: (k, j)),
                  pl.BlockSpec((TK, LANE), lambda j, k: (k, j)), pl.BlockSpec((S, LANE), lambda j, k: (0, j)),
                  pl.BlockSpec((None, S, LANE), lambda j, k: (j, 0, 0)), pl.BlockSpec((S, LANE), lambda j, k: (0, j))],
        out_specs=[pl.BlockSpec((S, 2 * LANE), lambda j, k: (0, j)), pl.BlockSpec((TK, 2 * LANE), lambda j, k: (k, j)),
                   pl.BlockSpec((TK, LANE), lambda j, k: (k, j))],
        out_shape=[jax.ShapeDtypeStruct((S, H * LANE), F32), jax.ShapeDtypeStruct((S, H * LANE), F32),
                   jax.ShapeDtypeStruct((S, H * VDIM), F32)],
        compiler_params=pltpu.CompilerParams(dimension_semantics=("parallel", "arbitrary")),
    )(qc, kc, v, o, lse, do)


_IN_Z, _IN_XBC, _IN_DT, _IN_Q, _IN_KV, _IN_KR = 0, 1024, 2560, 2576, 2960, 3216


def _prep_weights(w_in, w_qb, w_kvb):
    dt = w_in.dtype
    w_small = jnp.concatenate(
        [w_in[:, _IN_Q:_IN_KV], w_in[:, _IN_KV:_IN_KR], w_in[:, _IN_KR:IN_WIDTH], jnp.zeros((D, LANE - ROPE), dt),
         w_in[:, _IN_DT:_IN_Q], jnp.zeros((D, LANE - H), dt)], axis=1)
    w_q = jnp.pad(w_qb.reshape(Q_RANK, H, NOPE + ROPE), ((0, 0), (0, 0), (0, LANE - NOPE - ROPE))).reshape(Q_RANK, H * LANE)
    kv3 = w_kvb.reshape(KV_RANK, H, NOPE + VDIM)
    w_k = jnp.pad(kv3[:, :, :NOPE], ((0, 0), (0, 0), (0, LANE - NOPE))).reshape(KV_RANK, H * LANE)
    w_v = kv3[:, :, NOPE:].reshape(KV_RANK, H * VDIM)
    return w_in[:, _IN_Z:_IN_XBC], w_in[:, _IN_XBC:_IN_DT], w_small, w_q, w_k, w_v


def _rope_tables(positions):
    inv_freq = 1.0 / (10000.0 ** (jnp.arange(0, ROPE, 2, dtype=F32) / ROPE))
    ang = positions.astype(F32).reshape(S, 1) * inv_freq
    cos, sin = jnp.cos(ang), jnp.sin(ang)
    cos_t = jnp.concatenate([jnp.ones((S, NOPE), F32), cos, cos, jnp.ones((S, LANE - NOPE - ROPE), F32)], axis=1)
    sin_t = jnp.concatenate([jnp.zeros((S, NOPE), F32), -sin, sin, jnp.zeros((S, LANE - NOPE - ROPE), F32)], axis=1)
    return cos_t, sin_t


def _local_step(x, p, positions, target, wb, sp):
    w_z, w_xbc, w_small, w_q, w_k, w_v = _prep_weights(wb["w_in"], wb["w_qb"], wb["w_kvb"])
    w_out_s, w_out_m = wb["w_out"][:SSD_INNER], wb["w_out"][SSD_INNER:]
    cos_t, sin_t = _rope_tables(positions)
    prow = jnp.zeros((8, LANE), F32).at[0, :H].set(sp["dt_bias"][0]).at[1, :H].set(sp["A_log"][0]).at[2, :H].set(sp["D"][0])
    pcol = prow.T

    z = _mm([(x, w_z)], name="proj_z")
    xbc = _mm([(x, w_xbc)], name="proj_xbc")
    small = _mm([(x, w_small)], name="proj_small")
    act = _conv_fwd(xbc, sp["conv_w"], sp["conv_b"])
    dt_t = small[:, SM_DT:SM_DT + LANE].T
    y, states = _ssd_fwd(act, small, dt_t, prow, pcol)
    y_ssd = _gate_norm_fwd(y, z, sp["ssd_norm"])
    q_c, kv_c = small[:, SM_Q:SM_Q + Q_RANK], small[:, SM_KV:SM_KV + KV_RANK]
    qn = _rms_fwd(q_c, sp["q_norm"], name="q_norm_fwd")
    kvn = _rms_fwd(kv_c, sp["kv_norm"], name="kv_norm_fwd")
    qcat = _q_rope(_mm([(qn, w_q)], name="q_up"), cos_t, sin_t)
    kcat = _k_prep(_mm([(kvn, w_k)], name="k_up"), small, cos_t, sin_t)
    v = _mm([(kvn, w_v)], out_dtype=BF16, name="v_up")
    o, lse = _attn_fwd(qcat, kcat, v)
    y_mla = _rms_fwd(o, sp["out_norm"], name="out_norm_fwd")
    mix = _mm([(y_ssd, w_out_s), (y_mla, w_out_m)], name="out_proj")
    h1 = _ln_fwd(x, mix, sp["ln_mix_g"], sp["ln_mix_b"])
    gate = _mm([(h1, wb["w_gate"])], name="ffn_gate")
    up = _mm([(h1, wb["w_up"])], name="ffn_up")
    actf = _swiglu_fwd(gate, up)
    ffn = _mm([(actf, wb["w_down"])], name="ffn_down")
    pg = _mm([(h1, wb["w_pg"])], name="ple_gate")
    pp = _mm([(p, wb["w_pp"])], name="ple_proj")
    dpre2, dpg, dpp, dg2, db2, loss_row = _final_fwd_bwd(h1, ffn, pg, pp, target, sp["ln_ffn_g"], sp["ln_ffn_b"])

    g = {"ln_ffn_g": dg2, "ln_ffn_b": db2}
    g["w_pp"] = _mm([(p, dpp)], ta=True, name="d_w_ple_proj")
    g["w_pg"] = _mm([(h1, dpg)], ta=True, name="d_w_ple_gate")
    g["w_down"] = _mm([(actf, dpre2)], ta=True, name="d_w_down")
    dactf = _mm([(dpre2, wb["w_down"])], tb=True, name="d_act")
    dgate, dup = _swiglu_bwd(gate, up, dactf)
    g["w_gate"] = _mm([(h1, dgate)], ta=True, name="d_w_gate")
    g["w_up"] = _mm([(h1, dup)], ta=True, name="d_w_up")
    dh1 = _mm([(dpg, wb["w_pg"]), (dgate, wb["w_gate"]), (dup, wb["w_up"])], tb=True, add=dpre2, add_scale=ALPHA, name="d_h1")
    dpre1, g["ln_mix_g"], g["ln_mix_b"] = _ln_bwd(x, mix, sp["ln_mix_g"], dh1)
    dy_ssd = _mm([(dpre1, w_out_s)], tb=True, name="d_y_ssd")
    dy_mla = _mm([(dpre1, w_out_m)], tb=True, name="d_y_mla")
    g["w_out"] = jnp.concatenate([_mm([(y_ssd, dpre1)], ta=True, name="d_w_out_s"),
                                  _mm([(y_mla, dpre1)], ta=True, name="d_w_out_m")], axis=0)
    do, g["out_norm"] = _rms_bwd(o, sp["out_norm"], dy_mla, name="out_norm_bwd")
    dq, dk, dv = _attn_bwd(qcat, kcat, v, o, lse, do)
    dqlin = _q_unrope(dq, cos_t, sin_t)
    dw_q = _mm([(qn, dqlin)], ta=True, name="d_w_q")
    dqn = _mm([(dqlin, w_q)], tb=True, name="d_qn")
    dq_c, g["q_norm"] = _rms_bwd(q_c, sp["q_norm"], dqn, name="q_norm_bwd")
    dkr = _k_rope_bwd(dk, cos_t, sin_t)
    dw_k = _mm([(kvn, dk)], ta=True, name="d_w_k")
    dw_v = _mm([(kvn, dv)], ta=True, name="d_w_v")
    dkvn = _mm([(dk, w_k), (dv, w_v)], tb=True, name="d_kvn")
    dkv_c, g["kv_norm"] = _rms_bwd(kv_c, sp["kv_norm"], dkvn, name="kv_norm_bwd")
    g["w_qb"] = dw_q.reshape(Q_RANK, H, LANE)[:, :, :NOPE + ROPE].reshape(Q_RANK, H * (NOPE + ROPE))
    g["w_kvb"] = jnp.concatenate([dw_k.reshape(KV_RANK, H, LANE)[:, :, :NOPE], dw_v.reshape(KV_RANK, H, VDIM)],
                                 axis=2).reshape(KV_RANK, H * (NOPE + VDIM))
    dy, dz, g["ssd_norm"] = _gate_norm_bwd(y, z, sp["ssd_norm"], dy_ssd)
    dxs, dbm, dcm, ddt, dprow = _ssd_bwd(act, small, dt_t, prow, pcol, states, dy)
    g["dt_bias"], g["A_log"], g["D"] = dprow[0:1, :H], dprow[1:2, :H], dprow[2:3, :H]
    dxbc, g["conv_w"], g["conv_b"] = _conv_bwd(xbc, sp["conv_w"], sp["conv_b"], jnp.concatenate([dxs, dbm, dcm], axis=1))
    dsmall = jnp.concatenate([dq_c, dkv_c, dkr, ddt], axis=1)
    grad_x = _mm([(dz, w_z), (dxbc, w_xbc), (dsmall, w_small)], tb=True, add=dpre1, add_scale=ALPHA, name="d_x")
    dw_small = _mm([(x, dsmall)], ta=True, name="d_w_small")
    g["w_in"] = jnp.concatenate(
        [_mm([(x, dz)], ta=True, name="d_w_z"), _mm([(x, dxbc)], ta=True, name="d_w_xbc"), dw_small[:, SM_DT:SM_DT + H],
         dw_small[:, SM_Q:SM_Q + Q_RANK], dw_small[:, SM_KV:SM_KV + KV_RANK], dw_small[:, SM_KR:SM_KR + ROPE]], axis=1)
    return loss_row, grad_x, g


NCHIP = 4
MESH = pl.DeviceIdType.MESH
PACK_C = 1024
HALF_ROWS = 2016
PACK_WORDS = 2 * HALF_ROWS * PACK_C
BIG = (("w_in", (D, IN_WIDTH), 1), ("w_qb", (Q_RANK, H * (NOPE + ROPE)), 1), ("w_kvb", (KV_RANK, H * (NOPE + VDIM)), 1),
       ("w_out", (2 * SSD_INNER, D), 0), ("w_gate", (D, D_FF), 1), ("w_up", (D, D_FF), 1), ("w_down", (D_FF, D), 0),
       ("w_pg", (D, D), 0), ("w_pp", (PLE, D), 1))
CONV_SHARD = SSD_XBC // NCHIP


def _shard_shape(shape, axis):
    return (shape[0] // NCHIP, shape[1]) if axis == 0 else (shape[0], shape[1] // NCHIP)


def _pack(parts):
    flat = jnp.concatenate([a.reshape(-1) for a in parts])
    return jnp.pad(flat, (0, PACK_WORDS - flat.shape[0])).reshape(2, HALF_ROWS, PACK_C)


def _unpack(flat, with_conv):
    out, off = {}, 0
    for name, shape, axis in BIG:
        r, c = _shard_shape(shape, axis)
        out[name] = flat[off:off + r * c].reshape(r, c)
        off += r * c
    if with_conv:
        out["conv_w"] = lax.bitcast_convert_type(flat[off:off + 2 * SSD_K * CONV_SHARD].reshape(SSD_K, CONV_SHARD, 2), F32)
    return out


def _coords():
    return lax.axis_index("x"), lax.axis_index("y"), lax.axis_index("c")


def _gather_weights(wp):
    def body(wp_ref, out_ref, send_sems, recv_sems, local_sem):
        x, y, c = _coords()
        k = 2 * x + y
        chips = [(1 - x, y), (x, 1 - y), (1 - x, 1 - y)]
        sibling = (x, y, 1 - c)

        def copy(j, src, dst, to):
            return pltpu.make_async_remote_copy(src_ref=src, dst_ref=dst, send_sem=send_sems.at[j], recv_sem=recv_sems.at[j],
                                                device_id=to, device_id_type=MESH)

        mine = pltpu.make_async_copy(wp_ref, out_ref.at[k], local_sem)
        mine.start()
        first = [copy(j, wp_ref.at[c], out_ref.at[k, c], (cx, cy, c)) for j, (cx, cy) in enumerate(chips)]
        for cp in first:
            cp.start()
        passed = []
        for j, (cx, cy) in enumerate(chips):
            landed = out_ref.at[2 * cx + cy, c]
            copy(j, wp_ref.at[c], landed, (cx, cy, c)).wait_recv()
            passed.append(copy(NCHIP - 1 + j, landed, landed, sibling))
            passed[-1].start()
        for j, (cx, cy) in enumerate(chips):
            other = out_ref.at[2 * cx + cy, 1 - c]
            copy(NCHIP - 1 + j, other, other, sibling).wait_recv()
        for cp in first + passed:
            cp.wait_send()
        mine.wait()

    n = 2 * (NCHIP - 1)
    return pl.pallas_call(
        body, name="gather_weights", in_specs=[pl.BlockSpec(memory_space=pl.ANY)], out_specs=pl.BlockSpec(memory_space=pl.ANY),
        out_shape=jax.ShapeDtypeStruct((NCHIP, 2, HALF_ROWS, PACK_C), BF16),
        scratch_shapes=[pltpu.SemaphoreType.DMA((n,)), pltpu.SemaphoreType.DMA((n,)), pltpu.SemaphoreType.DMA],
    )(wp)


RS_CH = 672
RS_NCH = HALF_ROWS // RS_CH


def _reduce_grads(gp):
    def body(gp_ref, out_ref, r1_ref, part_ref, r2_ref, va, vb, vo, vs, vf, send_sems, recv_sems, local_sem):
        x, y, c = _coords()
        k = 2 * x + y
        chips = [(1 - x, y), (x, 1 - y), (1 - x, 1 - y)]
        sibling = (x, y, 1 - c)

        def copy(j, src, dst, to):
            return pltpu.make_async_remote_copy(src_ref=src, dst_ref=dst, send_sem=send_sems.at[j], recv_sem=recv_sems.at[j],
                                                device_id=to, device_id_type=MESH)

        pair = copy(0, gp_ref.at[1 - c], r1_ref, sibling)
        pair.start()
        pair.wait_recv()

        def add_chunk(t, carry):
            kk = t // RS_NCH
            rows = pl.ds(pl.multiple_of((t % RS_NCH) * RS_CH, 16), RS_CH)
            pltpu.sync_copy(gp_ref.at[c, kk, rows], va)
            pltpu.sync_copy(r1_ref.at[kk, rows], vb)
            vo[...] = (va[...].astype(F32) + vb[...].astype(F32)).astype(BF16)
            pltpu.sync_copy(vo, part_ref.at[kk, rows])
            return carry

        lax.fori_loop(0, NCHIP * RS_NCH, add_chunk, 0)
        sends = [copy(1 + j, part_ref.at[2 * cx + cy], r2_ref.at[k], (cx, cy, c)) for j, (cx, cy) in enumerate(chips)]
        for cp in sends:
            cp.start()
        own = pltpu.make_async_copy(part_ref.at[k], r2_ref.at[k], local_sem)
        own.start()
        own.wait()
        for j, (cx, cy) in enumerate(chips):
            copy(1 + j, part_ref.at[k], r2_ref.at[2 * cx + cy], (cx, cy, c)).wait_recv()

        def sum_chunk(r, carry):
            rows = pl.ds(pl.multiple_of(r * RS_CH, 16), RS_CH)
            pltpu.sync_copy(r2_ref.at[:, rows], vs)
            acc = vs[0].astype(F32)
            for kk in range(1, NCHIP):
                acc = acc + vs[kk].astype(F32)
            vf[...] = acc
            pltpu.sync_copy(vf, out_ref.at[c, rows])
            return carry

        lax.fori_loop(0, RS_NCH, sum_chunk, 0)
        done = copy(NCHIP, out_ref.at[c], out_ref.at[c], sibling)
        done.start()
        copy(NCHIP, out_ref.at[1 - c], out_ref.at[1 - c], sibling).wait_recv()
        done.wait_send()
        pair.wait_send()
        for cp in sends:
            cp.wait_send()

    any_spec = pl.BlockSpec(memory_space=pl.ANY)
    stage = jax.ShapeDtypeStruct((NCHIP, HALF_ROWS, PACK_C), BF16)
    n = NCHIP + 1
    return pl.pallas_call(
        body, name="reduce_grads", in_specs=[any_spec], out_specs=[any_spec] * 4,
        out_shape=[jax.ShapeDtypeStruct((2, HALF_ROWS, PACK_C), F32), stage, stage, stage],
        scratch_shapes=[pltpu.VMEM((RS_CH, PACK_C), BF16), pltpu.VMEM((RS_CH, PACK_C), BF16), pltpu.VMEM((RS_CH, PACK_C), BF16),
                        pltpu.VMEM((NCHIP, RS_CH, PACK_C), BF16), pltpu.VMEM((RS_CH, PACK_C), F32),
                        pltpu.SemaphoreType.DMA((n,)), pltpu.SemaphoreType.DMA((n,)), pltpu.SemaphoreType.DMA],
    )(gp)[0]


SMALL = (("conv_w", SSD_K * SSD_XBC), ("conv_b", SSD_XBC), ("dt_bias", H), ("A_log", H), ("D", H), ("ssd_norm", SSD_INNER),
         ("q_norm", Q_RANK), ("kv_norm", KV_RANK), ("out_norm", SSD_INNER), ("ln_mix_g", D), ("ln_mix_b", D),
         ("ln_ffn_g", D), ("ln_ffn_b", D))
SMALL_ROWS = 120
NDEV = 8


def _allreduce_small(sv):
    def body(sv_ref, out_ref, slots, send_sems, recv_sems):
        x, y, c = _coords()
        me = 4 * x + 2 * y + c
        slots[me] = sv_ref[...]
        copies = []
        for d in range(1, NDEV):
            to = (x ^ (d >> 2), y ^ ((d >> 1) & 1), c ^ (d & 1))
            copies.append(pltpu.make_async_remote_copy(src_ref=sv_ref, dst_ref=slots.at[me], send_sem=send_sems.at[d - 1],
                                                       recv_sem=recv_sems.at[d - 1], device_id=to, device_id_type=MESH))
            copies[-1].start()
        for cp in copies:
            cp.wait_recv()
        for cp in copies:
            cp.wait_send()
        acc = slots[0]
        for i in range(1, NDEV):
            acc = acc + slots[i]
        out_ref[...] = acc

    vm = pl.BlockSpec(memory_space=pltpu.VMEM)
    return pl.pallas_call(
        body, name="allreduce_small", in_specs=[vm], out_specs=vm, out_shape=jax.ShapeDtypeStruct((SMALL_ROWS, LANE), F32),
        scratch_shapes=[pltpu.VMEM((NDEV, SMALL_ROWS, LANE), F32), pltpu.SemaphoreType.DMA((NDEV - 1,)),
                        pltpu.SemaphoreType.DMA((NDEV - 1,))],
    )(sv)


def _adamw_math(w, g, m, v):
    m2 = ADAM_B1 * m + (1.0 - ADAM_B1) * g
    v2 = ADAM_B2 * v + (1.0 - ADAM_B2) * (g * g)
    m_hat = m2 / (1.0 - ADAM_B1 ** ADAM_STEP)
    v_hat = v2 / (1.0 - ADAM_B2 ** ADAM_STEP)
    return -ADAM_LR * (m_hat / (jnp.sqrt(v_hat) + ADAM_EPS) + ADAM_WD * w), m2, v2


def _adamw_big(w, g, m, v, *, name):
    r, c = w.shape
    tr = next(t for t in (512, 384, 352, 256, 128, 64, 8) if r % t == 0)

    def body(w_ref, g_ref, m_ref, v_ref, d_ref, m2_ref, v2_ref):
        d_ref[...], m2_ref[...], v2_ref[...] = _adamw_math(w_ref[...], g_ref[...], m_ref[...], v_ref[...])

    spec = pl.BlockSpec((tr, c), lambda i: (i, 0))
    return pl.pallas_call(body, name=name, grid=(r // tr,), in_specs=[spec] * 4, out_specs=[spec] * 3,
                          out_shape=[jax.ShapeDtypeStruct((r, c), F32)] * 3)(w, g, m, v)


def _adamw_small(ws, gs, ms, vs):
    n = len(ws)

    def body(*refs):
        for i in range(n):
            w_ref, g_ref, m_ref, v_ref = (refs[j * n + i] for j in range(4))
            d_ref, m2_ref, v2_ref = (refs[(4 + j) * n + i] for j in range(3))
            d_ref[...], m2_ref[...], v2_ref[...] = _adamw_math(w_ref[...], g_ref[...], m_ref[...], v_ref[...])

    vm = pl.BlockSpec(memory_space=pltpu.VMEM)
    shapes = [jax.ShapeDtypeStruct(w.shape, F32) for w in ws]
    outs = pl.pallas_call(body, name="adamw_small", in_specs=[vm] * (4 * n), out_specs=[vm] * (3 * n), out_shape=shapes * 3)(
        *ws, *gs, *ms, *vs)
    return outs[:n], outs[n:2 * n], outs[2 * n:]


_SMALL_ARG = {"conv_w": "ssd_conv_w", "conv_b": "ssd_conv_b", "dt_bias": "ssd_dt_bias", "A_log": "ssd_A_log", "D": "ssd_D",
              "ssd_norm": "ssd_norm_w", "q_norm": "mla_q_norm_w", "kv_norm": "mla_kv_norm_w", "out_norm": "mla_out_norm_w",
              "ln_mix_g": "ln_mix_g", "ln_mix_b": "ln_mix_b", "ln_ffn_g": "ln_ffn_g", "ln_ffn_b": "ln_ffn_b"}
_BIG_ARG = {"w_in": "w_in", "w_qb": "mla_w_q_b", "w_kvb": "mla_w_kv_b", "w_out": "w_out", "w_gate": "w_ffn_gate",
            "w_up": "w_ffn_up", "w_down": "w_ffn_down", "w_pg": "w_ple_gate", "w_pp": "w_ple_proj"}
_WEIGHT_ORDER = ("w_in", "ssd_conv_w", "ssd_conv_b", "ssd_dt_bias", "ssd_A_log", "ssd_D", "ssd_norm_w", "mla_q_norm_w", "mla_w_q_b",
                 "mla_kv_norm_w", "mla_w_kv_b", "mla_out_norm_w", "w_out", "ln_mix_g", "ln_mix_b", "w_ffn_gate", "w_ffn_up",
                 "w_ffn_down", "w_ple_gate", "w_ple_proj", "ln_ffn_g", "ln_ffn_b")


def _rows128(a):
    flat = a.reshape(-1)
    return jnp.pad(flat, (0, -flat.shape[0] % LANE)).reshape(-1, LANE)


def kernel(x, p, positions, w_in, ssd_conv_w, ssd_conv_b, ssd_dt_bias, ssd_A_log, ssd_D, ssd_norm_w, mla_q_norm_w, mla_w_q_b, mla_kv_norm_w, mla_w_kv_b, mla_out_norm_w, w_out, ln_mix_g, ln_mix_b, w_ffn_gate, w_ffn_up, w_ffn_down, w_ple_gate, w_ple_proj, ln_ffn_g, ln_ffn_b, loss_target, m_w_in, m_ssd_conv_w, m_ssd_conv_b, m_ssd_dt_bias, m_ssd_A_log, m_ssd_D, m_ssd_norm_w, m_mla_q_norm_w, m_mla_w_q_b, m_mla_kv_norm_w, m_mla_w_kv_b, m_mla_out_norm_w, m_w_out, m_ln_mix_g, m_ln_mix_b, m_w_ffn_gate, m_w_ffn_up, m_w_ffn_down, m_w_ple_gate, m_w_ple_proj, m_ln_ffn_g, m_ln_ffn_b, v_w_in, v_ssd_conv_w, v_ssd_conv_b, v_ssd_dt_bias, v_ssd_A_log, v_ssd_D, v_ssd_norm_w, v_mla_q_norm_w, v_mla_w_q_b, v_mla_kv_norm_w, v_mla_w_kv_b, v_mla_out_norm_w, v_w_out, v_ln_mix_g, v_ln_mix_b, v_w_ffn_gate, v_w_ffn_up, v_w_ffn_down, v_w_ple_gate, v_w_ple_proj, v_ln_ffn_g, v_ln_ffn_b):
    given = dict(locals())
    chip = 2 * lax.axis_index("x") + lax.axis_index("y")

    shard = [given[_BIG_ARG[name]][0].astype(BF16) for name, _, _ in BIG]
    shard.append(lax.bitcast_convert_type(ssd_conv_w[0], BF16))
    full = _gather_weights(_pack(shard)).reshape(NCHIP, PACK_WORDS)
    per_chip = [_unpack(full[k], True) for k in range(NCHIP)]
    wb = {name: jnp.concatenate([pc[name] for pc in per_chip], axis=axis) for name, _, axis in BIG}
    sp = {k: given[a] for k, a in _SMALL_ARG.items() if k != "conv_w"}
    sp["conv_w"] = jnp.concatenate([pc["conv_w"] for pc in per_chip], axis=1)

    loss_row, grad_x, g = _local_step(x[0], p[0, 0], positions[0], loss_target[0], wb, sp)

    by_chip = []
    for k in range(NCHIP):
        parts = []
        for name, shape, axis in BIG:
            r, c = _shard_shape(shape, axis)
            parts.append((g[name][k * r:(k + 1) * r] if axis == 0 else g[name][:, k * c:(k + 1) * c]).astype(BF16))
        by_chip.append(_pack(parts))
    gsum = _reduce_grads(jnp.stack(by_chip, axis=1)).reshape(PACK_WORDS)
    gbig = _unpack(gsum, False)
    small_in = jnp.concatenate([_rows128(g[name]) for name, _ in SMALL] + [loss_row], axis=0)
    small_sum = _allreduce_small(jnp.pad(small_in, ((0, SMALL_ROWS - small_in.shape[0]), (0, 0))))
    gsmall, row = {}, 0
    for name, size in SMALL:
        nrow = -(-size // LANE)
        gsmall[name] = small_sum[row:row + nrow].reshape(-1)[:size]
        row += nrow
    loss = small_sum[row, 0]

    grads = {}
    for name, shape, axis in BIG:
        grads[_BIG_ARG[name]] = gbig[name][None]
    for name, _ in SMALL:
        if name == "conv_w":
            full_g = gsmall[name].reshape(SSD_K, SSD_XBC)
            grads["ssd_conv_w"] = lax.dynamic_slice(full_g, (0, chip * CONV_SHARD), (SSD_K, CONV_SHARD))[None]
        else:
            grads[_SMALL_ARG[name]] = gsmall[name].reshape(given[_SMALL_ARG[name]].shape)

    delta, new_m, new_v = {}, {}, {}
    for name, _, _ in BIG:
        a = _BIG_ARG[name]
        d, m2, v2 = _adamw_big(given[a][0], grads[a][0], given["m_" + a][0], given["v_" + a][0], name="adamw_" + a)
        delta[a], new_m[a], new_v[a] = d[None], m2[None], v2[None]
    small_names = [_SMALL_ARG[name] for name, _ in SMALL]
    two_d = lambda t: t.reshape(t.shape[-2], t.shape[-1])
    ds, ms, vs = _adamw_small([two_d(given[a]) for a in small_names], [two_d(grads[a]) for a in small_names],
                              [two_d(given["m_" + a]) for a in small_names], [two_d(given["v_" + a]) for a in small_names])
    for a, d, m2, v2 in zip(small_names, ds, ms, vs):
        delta[a], new_m[a], new_v[a] = (t.reshape(given[a].shape) for t in (d, m2, v2))

    return (loss, grad_x[None], *[grads[n] for n in _WEIGHT_ORDER], *[delta[n] for n in _WEIGHT_ORDER],
            *[new_m[n] for n in _WEIGHT_ORDER], *[new_v[n] for n in _WEIGHT_ORDER])
```

```python
import functools
import math

import jax
import jax.numpy as jnp
from jax import lax
from jax.experimental import pallas as pl
from jax.experimental.pallas import tpu as pltpu

F32 = jnp.float32
BF16 = jnp.bfloat16

S = 2048
D = 1024
PLE = 256
H = 16
SSD_P = 64
SSD_INNER = 1024
SSD_N = 128
SSD_G = 2
SSD_L = 128
SSD_NC = S // SSD_L
SSD_XBC = 1536
SSD_K = 4
Q_RANK = 384
KV_RANK = 256
NOPE = 64
ROPE = 32
VDIM = 64
D_FF = 2816
IN_WIDTH = 3248
ALPHA = 2.0 ** 0.25
EPS_RMS = 1e-6
EPS_LN = 1e-5
ATT_SCALE = 1.0 / math.sqrt(NOPE + ROPE)
LANE = 128
SMALL_W = 896
SM_Q, SM_KV, SM_KR, SM_DT = 0, 384, 640, 768
NEG = -1e30

ADAM_LR = 0.001
ADAM_B1 = 0.9
ADAM_B2 = 0.999
ADAM_EPS = 1e-08
ADAM_WD = 0.01
ADAM_STEP = 10


def _sigmoid(v):
    return 1.0 / (1.0 + jnp.exp(-v))


MM_VMEM_BUDGET = 36 * 2 ** 20
MM_MAX_ACC = 2048 * 1024


def _mm_tiles(pairs, ta, tb, m, n, out_dtype, has_add):
    def divs(v):
        return [LANE * d for d in range(v // LANE, 0, -1) if (v // LANE) % d == 0] if v % LANE == 0 else [v]

    def cost(tm, tn):
        tot = tm * tn * (jnp.dtype(out_dtype).itemsize + (4 if has_add else 0))
        for a, b in pairs:
            k = a.shape[0] if ta else a.shape[1]
            tot += k * (tm * a.dtype.itemsize + tn * b.dtype.itemsize)
        return 2 * tot

    ok = [(tm * tn, tm, tn) for tm in divs(m) for tn in divs(n) if tm * tn <= MM_MAX_ACC and cost(tm, tn) <= MM_VMEM_BUDGET]
    _, tm, tn = max(ok)
    return tm, tn


def _mm(pairs, *, ta=False, tb=False, out_dtype=F32, add=None, add_scale=1.0, name):
    n_pairs = len(pairs)
    a0, b0 = pairs[0]
    m = a0.shape[1] if ta else a0.shape[0]
    n = b0.shape[0] if tb else b0.shape[1]
    tm, tn = _mm_tiles(pairs, ta, tb, m, n, out_dtype, add is not None)
    dims = (((0 if ta else 1,), (1 if tb else 0,)), ((), ()))

    def body(*refs):
        o_ref = refs[-1]
        acc = None
        for i in range(n_pairs):
            a = refs[2 * i][...].astype(BF16)
            b = refs[2 * i + 1][...].astype(BF16)
            part = lax.dot_general(a, b, dims, preferred_element_type=F32)
            acc = part if acc is None else acc + part
        if add is not None:
            acc = acc + add_scale * refs[2 * n_pairs][...]
        o_ref[...] = acc.astype(out_dtype)

    in_specs, args = [], []
    for a, b in pairs:
        k = a.shape[0] if ta else a.shape[1]
        in_specs.append(pl.BlockSpec((k, tm), lambda i, j: (0, i)) if ta else pl.BlockSpec((tm, k), lambda i, j: (i, 0)))
        in_specs.append(pl.BlockSpec((tn, k), lambda i, j: (j, 0)) if tb else pl.BlockSpec((k, tn), lambda i, j: (0, j)))
        args += [a, b]
    if add is not None:
        in_specs.append(pl.BlockSpec((tm, tn), lambda i, j: (i, j)))
        args.append(add)
    return pl.pallas_call(
        body, name=name, grid=(m // tm, n // tn), in_specs=in_specs,
        out_specs=pl.BlockSpec((tm, tn), lambda i, j: (i, j)),
        out_shape=jax.ShapeDtypeStruct((m, n), out_dtype),
        compiler_params=pltpu.CompilerParams(dimension_semantics=("parallel", "parallel")),
    )(*args)


TR = 256


def _row_spec(c):
    return pl.BlockSpec((TR, c), lambda i: (i, 0))


def _vec_spec(c):
    return pl.BlockSpec((1, c), lambda i: (0, 0))


def _acc_rows(ref, val):
    @pl.when(pl.program_id(0) == 0)
    def _():
        ref[...] = jnp.zeros_like(ref)
    ref[...] += val


def _rms_fwd(u, w, *, name):
    c = u.shape[1]

    def body(u_ref, w_ref, o_ref):
        v = u_ref[...]
        r = lax.rsqrt(jnp.mean(v * v, axis=-1, keepdims=True) + EPS_RMS)
        o_ref[...] = (v * r * w_ref[...]).astype(BF16)

    return pl.pallas_call(body, name=name, grid=(S // TR,), in_specs=[_row_spec(c), _vec_spec(c)], out_specs=_row_spec(c),
                          out_shape=jax.ShapeDtypeStruct((S, c), BF16))(u, w)


def _rms_bwd(u, w, dy, *, name):
    c = u.shape[1]

    def body(u_ref, w_ref, dy_ref, du_ref, dw_ref):
        v = u_ref[...]
        g = dy_ref[...].astype(F32)
        r = lax.rsqrt(jnp.mean(v * v, axis=-1, keepdims=True) + EPS_RMS)
        gw = g * w_ref[...]
        du_ref[...] = r * gw - v * (r * r * r * jnp.mean(gw * v, axis=-1, keepdims=True))
        _acc_rows(dw_ref, jnp.sum(g * v * r, axis=0, keepdims=True))

    return pl.pallas_call(body, name=name, grid=(S // TR,), in_specs=[_row_spec(c), _vec_spec(c), _row_spec(c)],
                          out_specs=[_row_spec(c), _vec_spec(c)],
                          out_shape=[jax.ShapeDtypeStruct((S, c), F32), jax.ShapeDtypeStruct((1, c), F32)])(u, w, dy)


def _gate_norm_fwd(y, z, w):
    def body(y_ref, z_ref, w_ref, o_ref):
        zz = z_ref[...]
        v = y_ref[...] * (zz * _sigmoid(zz))
        r = lax.rsqrt(jnp.mean(v * v, axis=-1, keepdims=True) + EPS_RMS)
        o_ref[...] = (v * r * w_ref[...]).astype(BF16)

    c = SSD_INNER
    return pl.pallas_call(body, name="ssd_gate_norm_fwd", grid=(S // TR,), in_specs=[_row_spec(c), _row_spec(c), _vec_spec(c)],
                          out_specs=_row_spec(c), out_shape=jax.ShapeDtypeStruct((S, c), BF16))(y, z, w)


def _gate_norm_bwd(y, z, w, dout):
    def body(y_ref, z_ref, w_ref, g_ref, dy_ref, dz_ref, dw_ref):
        yy = y_ref[...]
        zz = z_ref[...]
        sg = _sigmoid(zz)
        sz = zz * sg
        v = yy * sz
        g = g_ref[...]
        r = lax.rsqrt(jnp.mean(v * v, axis=-1, keepdims=True) + EPS_RMS)
        gw = g * w_ref[...]
        dv = r * gw - v * (r * r * r * jnp.mean(gw * v, axis=-1, keepdims=True))
        dy_ref[...] = dv * sz
        dz_ref[...] = (dv * yy * (sg * (1.0 + zz * (1.0 - sg)))).astype(BF16)
        _acc_rows(dw_ref, jnp.sum(g * v * r, axis=0, keepdims=True))

    c = SSD_INNER
    return pl.pallas_call(body, name="ssd_gate_norm_bwd", grid=(S // TR,),
                          in_specs=[_row_spec(c), _row_spec(c), _vec_spec(c), _row_spec(c)],
                          out_specs=[_row_spec(c), _row_spec(c), _vec_spec(c)],
                          out_shape=[jax.ShapeDtypeStruct((S, c), F32), jax.ShapeDtypeStruct((S, c), BF16),
                                     jax.ShapeDtypeStruct((1, c), F32)])(y, z, w, dout)


def _ln_fwd(xr, mix, g, b):
    def body(x_ref, m_ref, g_ref, b_ref, o_ref, ob_ref):
        pre = ALPHA * x_ref[...] + m_ref[...]
        mu = jnp.mean(pre, axis=-1, keepdims=True)
        d = pre - mu
        rs = lax.rsqrt(jnp.mean(d * d, axis=-1, keepdims=True) + EPS_LN)
        h = d * rs * g_ref[...] + b_ref[...]
        o_ref[...] = h
        ob_ref[...] = h.astype(BF16)

    return pl.pallas_call(body, name="ln_mix_fwd", grid=(S // TR,), in_specs=[_row_spec(D), _row_spec(D), _vec_spec(D), _vec_spec(D)],
                          out_specs=[_row_spec(D)] * 2,
                          out_shape=[jax.ShapeDtypeStruct((S, D), F32), jax.ShapeDtypeStruct((S, D), BF16)])(xr, mix, g, b)


def _ln_bwd(xr, mix, g, dh):
    def body(x_ref, m_ref, g_ref, dh_ref, dpre_ref, dpreb_ref, dg_ref, db_ref):
        pre = ALPHA * x_ref[...] + m_ref[...]
        mu = jnp.mean(pre, axis=-1, keepdims=True)
        d = pre - mu
        rs = lax.rsqrt(jnp.mean(d * d, axis=-1, keepdims=True) + EPS_LN)
        xh = d * rs
        dy = dh_ref[...]
        gy = dy * g_ref[...]
        dpre = rs * (gy - jnp.mean(gy, axis=-1, keepdims=True) - xh * jnp.mean(gy * xh, axis=-1, keepdims=True))
        dpre_ref[...] = dpre
        dpreb_ref[...] = dpre.astype(BF16)
        _acc_rows(dg_ref, jnp.sum(dy * xh, axis=0, keepdims=True))
        _acc_rows(db_ref, jnp.sum(dy, axis=0, keepdims=True))

    return pl.pallas_call(body, name="ln_mix_bwd", grid=(S // TR,),
                          in_specs=[_row_spec(D), _row_spec(D), _vec_spec(D), _row_spec(D)],
                          out_specs=[_row_spec(D), _row_spec(D), _vec_spec(D), _vec_spec(D)],
                          out_shape=[jax.ShapeDtypeStruct((S, D), F32), jax.ShapeDtypeStruct((S, D), BF16),
                                     jax.ShapeDtypeStruct((1, D), F32), jax.ShapeDtypeStruct((1, D), F32)])(xr, mix, g, dh)


def _swiglu_fwd(gate, up):
    def body(g_ref, u_ref, o_ref):
        g = g_ref[...]
        o_ref[...] = (g * _sigmoid(g) * u_ref[...]).astype(BF16)

    c = D_FF
    return pl.pallas_call(body, name="swiglu_fwd", grid=(S // TR,), in_specs=[_row_spec(c), _row_spec(c)], out_specs=_row_spec(c),
                          out_shape=jax.ShapeDtypeStruct((S, c), BF16))(gate, up)


def _swiglu_bwd(gate, up, dact):
    def body(g_ref, u_ref, d_ref, dg_ref, du_ref):
        g = g_ref[...]
        sg = _sigmoid(g)
        d = d_ref[...]
        dg_ref[...] = (d * u_ref[...] * (sg * (1.0 + g * (1.0 - sg)))).astype(BF16)
        du_ref[...] = (d * g * sg).astype(BF16)

    c = D_FF
    return pl.pallas_call(body, name="swiglu_bwd", grid=(S // TR,), in_specs=[_row_spec(c)] * 3, out_specs=[_row_spec(c)] * 2,
                          out_shape=[jax.ShapeDtypeStruct((S, c), BF16)] * 2)(gate, up, dact)


def _final_fwd_bwd(h1, ffn, pg, pp, target, g2, b2):
    def body(h_ref, f_ref, pg_ref, pp_ref, t_ref, g_ref, b_ref, dpre_ref, dpreb_ref, dpg_ref, dpp_ref, dg_ref, db_ref, loss_ref):
        sg = _sigmoid(pg_ref[...])
        ppv = pp_ref[...]
        pre = ALPHA * h_ref[...] + f_ref[...] + sg * ppv
        mu = jnp.mean(pre, axis=-1, keepdims=True)
        d = pre - mu
        rs = lax.rsqrt(jnp.mean(d * d, axis=-1, keepdims=True) + EPS_LN)
        xh = d * rs
        err = xh * g_ref[...] + b_ref[...] - t_ref[...]
        dy = err * (1.0 / D)
        gy = dy * g_ref[...]
        dpre = rs * (gy - jnp.mean(gy, axis=-1, keepdims=True) - xh * jnp.mean(gy * xh, axis=-1, keepdims=True))
        dpre_ref[...] = dpre
        dpreb_ref[...] = dpre.astype(BF16)
        dpg_ref[...] = (dpre * ppv * sg * (1.0 - sg)).astype(BF16)
        dpp_ref[...] = (dpre * sg).astype(BF16)
        _acc_rows(dg_ref, jnp.sum(dy * xh, axis=0, keepdims=True))
        _acc_rows(db_ref, jnp.sum(dy, axis=0, keepdims=True))
        _acc_rows(loss_ref, 0.5 * jnp.sum(jnp.mean(err * err, axis=-1, keepdims=True), axis=0, keepdims=True) * jnp.ones((1, LANE), F32))

    return pl.pallas_call(
        body, name="final_ln_loss", grid=(S // TR,),
        in_specs=[_row_spec(D)] * 5 + [_vec_spec(D)] * 2,
        out_specs=[_row_spec(D)] * 4 + [_vec_spec(D), _vec_spec(D), _vec_spec(LANE)],
        out_shape=[jax.ShapeDtypeStruct((S, D), F32)] + [jax.ShapeDtypeStruct((S, D), BF16)] * 3 + [
                   jax.ShapeDtypeStruct((1, D), F32), jax.ShapeDtypeStruct((1, D), F32), jax.ShapeDtypeStruct((1, LANE), F32)],
    )(h1, ffn, pg, pp, target, g2, b2)


def _rot(u, cos_t, sin_t, lane):
    partner = jnp.where(lane < NOPE + ROPE // 2, pltpu.roll(u, LANE - ROPE // 2, 1), pltpu.roll(u, ROPE // 2, 1))
    return u * cos_t + partner * sin_t


def _q_rope(qlin, cos_t, sin_t):
    def body(q_ref, c_ref, s_ref, o_ref):
        lane = lax.broadcasted_iota(jnp.int32, (TR, LANE), 1)
        c, s = c_ref[...], s_ref[...]
        for h in range(H):
            o_ref[:, h * LANE:(h + 1) * LANE] = _rot(q_ref[:, h * LANE:(h + 1) * LANE], c, s, lane).astype(BF16)

    w = H * LANE
    return pl.pallas_call(body, name="q_rope", grid=(S // TR,), in_specs=[_row_spec(w), _row_spec(LANE), _row_spec(LANE)],
                          out_specs=_row_spec(w), out_shape=jax.ShapeDtypeStruct((S, w), BF16))(qlin, cos_t, sin_t)


def _q_unrope(dq, cos_t, sin_t):
    def body(q_ref, c_ref, s_ref, o_ref):
        lane = lax.broadcasted_iota(jnp.int32, (TR, LANE), 1)
        c, s = c_ref[...], -s_ref[...]
        for h in range(H):
            o_ref[:, h * LANE:(h + 1) * LANE] = _rot(q_ref[:, h * LANE:(h + 1) * LANE], c, s, lane).astype(BF16)

    w = H * LANE
    return pl.pallas_call(body, name="q_unrope", grid=(S // TR,), in_specs=[_row_spec(w), _row_spec(LANE), _row_spec(LANE)],
                          out_specs=_row_spec(w), out_shape=jax.ShapeDtypeStruct((S, w), BF16))(dq, cos_t, sin_t)


def _k_prep(klin, small, cos_t, sin_t):
    def body(k_ref, kr_ref, c_ref, s_ref, o_ref):
        lane = lax.broadcasted_iota(jnp.int32, (TR, LANE), 1)
        kr = _rot(pltpu.roll(kr_ref[...], NOPE, 1), c_ref[...], s_ref[...], lane)
        for h in range(H):
            o_ref[:, h * LANE:(h + 1) * LANE] = (k_ref[:, h * LANE:(h + 1) * LANE] + kr).astype(BF16)

    w = H * LANE
    kr_spec = pl.BlockSpec((TR, LANE), lambda i: (i, SM_KR // LANE))
    return pl.pallas_call(body, name="k_prep", grid=(S // TR,), in_specs=[_row_spec(w), kr_spec, _row_spec(LANE), _row_spec(LANE)],
                          out_specs=_row_spec(w), out_shape=jax.ShapeDtypeStruct((S, w), BF16))(klin, small, cos_t, sin_t)


def _k_rope_bwd(dk, cos_t, sin_t):
    def body(k_ref, c_ref, s_ref, o_ref):
        lane = lax.broadcasted_iota(jnp.int32, (TR, LANE), 1)
        acc = k_ref[:, 0:LANE]
        for h in range(1, H):
            acc = acc + k_ref[:, h * LANE:(h + 1) * LANE]
        acc = jnp.where((lane >= NOPE) & (lane < NOPE + ROPE), acc, 0.0)
        o_ref[...] = pltpu.roll(_rot(acc, c_ref[...], -s_ref[...], lane), LANE - NOPE, 1)

    w = H * LANE
    return pl.pallas_call(body, name="k_rope_bwd", grid=(S // TR,), in_specs=[_row_spec(w), _row_spec(LANE), _row_spec(LANE)],
                          out_specs=_row_spec(LANE), out_shape=jax.ShapeDtypeStruct((S, LANE), F32))(dk, cos_t, sin_t)


CB = 256


def _shift_down(u, k, row):
    if k == 0:
        return u
    return jnp.where(row >= k, pltpu.roll(u, k, 0), 0.0)


def _shift_up(u, k, row):
    if k == 0:
        return u
    return jnp.where(row < S - k, pltpu.roll(u, S - k, 0), 0.0)


def _conv_fwd(u, w, b):
    def body(u_ref, w_ref, b_ref, o_ref):
        row = lax.broadcasted_iota(jnp.int32, (S, CB), 0)
        uu = u_ref[...]
        acc = b_ref[...] + w_ref[SSD_K - 1:SSD_K, :] * uu
        for k in range(SSD_K - 1):
            acc = acc + w_ref[k:k + 1, :] * _shift_down(uu, SSD_K - 1 - k, row)
        o_ref[...] = acc * _sigmoid(acc)

    c = u.shape[1]
    return pl.pallas_call(
        body, name="conv_fwd", grid=(c // CB,),
        in_specs=[pl.BlockSpec((S, CB), lambda j: (0, j)), pl.BlockSpec((SSD_K, CB), lambda j: (0, j)), pl.BlockSpec((1, CB), lambda j: (0, j))],
        out_specs=pl.BlockSpec((S, CB), lambda j: (0, j)), out_shape=jax.ShapeDtypeStruct((S, c), F32),
    )(u, w, b)


def _conv_bwd(u, w, b, dact):
    def body(u_ref, w_ref, b_ref, d_ref, du_ref, dw_ref, db_ref):
        row = lax.broadcasted_iota(jnp.int32, (S, CB), 0)
        uu = u_ref[...]
        sh = [_shift_down(uu, SSD_K - 1 - k, row) for k in range(SSD_K)]
        acc = b_ref[...]
        for k in range(SSD_K):
            acc = acc + w_ref[k:k + 1, :] * sh[k]
        sg = _sigmoid(acc)
        dacc = d_ref[...] * (sg * (1.0 + acc * (1.0 - sg)))
        du = w_ref[SSD_K - 1:SSD_K, :] * dacc
        for k in range(SSD_K - 1):
            du = du + w_ref[k:k + 1, :] * _shift_up(dacc, SSD_K - 1 - k, row)
        du_ref[...] = du.astype(BF16)
        for k in range(SSD_K):
            dw_ref[k:k + 1, :] = jnp.sum(dacc * sh[k], axis=0, keepdims=True)
        db_ref[...] = jnp.sum(dacc, axis=0, keepdims=True)

    c = u.shape[1]
    col = lambda r: pl.BlockSpec((r, CB), lambda j: (0, j))
    return pl.pallas_call(
        body, name="conv_bwd", grid=(c // CB,), in_specs=[col(S), col(SSD_K), col(1), col(S)], out_specs=[col(S), col(SSD_K), col(1)],
        out_shape=[jax.ShapeDtypeStruct((S, c), BF16), jax.ShapeDtypeStruct((SSD_K, c), F32), jax.ShapeDtypeStruct((1, c), F32)],
    )(u, w, b, dact)


NPAIR = H // 2
PAIRS_PER_GROUP = NPAIR // SSD_G


def _softplus(v):
    return jnp.maximum(v, 0.0) + jnp.log(1.0 + jnp.exp(-jnp.abs(v)))


def _dot(a, b, dims):
    return lax.dot_general(a.astype(BF16), b.astype(BF16), (dims, ((), ())), preferred_element_type=F32)


def _dot3(a, b, dims, split_lhs):
    v = a if split_lhs else b
    v1 = v.astype(BF16)
    r1 = v - v1.astype(F32)
    v2 = r1.astype(BF16)
    v3 = (r1 - v2.astype(F32)).astype(BF16)
    acc = None
    for part in (v1, v2, v3):
        lhs, rhs = (part, b) if split_lhs else (a, part)
        t = lax.dot_general(lhs, rhs, (dims, ((), ())), preferred_element_type=F32)
        acc = t if acc is None else acc + t
    return acc


def _ssd_chunk_common(dt_ref, dtT_ref, prow_ref, pcol_ref):
    prow = prow_ref[...]
    pcol = pcol_ref[...]
    ri = lax.broadcasted_iota(jnp.int32, (SSD_L, SSD_L), 0)
    ci = lax.broadcasted_iota(jnp.int32, (SSD_L, SSD_L), 1)
    causal = ri >= ci
    pre_c = dt_ref[...] + prow[0:1, :]
    dtc = _softplus(pre_c)
    a_row = -jnp.exp(prow[1:2, :])
    cs_col = _dot3(causal.astype(BF16), dtc * a_row, ((1,), (0,)), False)
    dtr = _softplus(dtT_ref[...] + pcol[:, 0:1])
    a_col = -jnp.exp(pcol[:, 1:2])
    cs_row = _dot3(dtr * a_col, (ri <= ci).astype(BF16), ((1,), (0,)), True)
    return prow, causal, pre_c, dtc, a_row, cs_col, cs_row


def _ssd_fwd(act, small, dtT, prow, pcol):
    def body(x_ref, b_ref, c_ref, dt_ref, dtT_ref, prow_ref, pcol_ref, y_ref, st_ref, state):
        @pl.when(pl.program_id(0) == 0)
        def _():
            state[...] = jnp.zeros_like(state)

        prow, causal, _, dtc, _, cs_col, cs_row = _ssd_chunk_common(dt_ref, dtT_ref, prow_ref, pcol_ref)
        lo = lax.broadcasted_iota(jnp.int32, (SSD_L, LANE), 1) < SSD_P
        lo1 = lo[0:1, :]
        for g in range(SSD_G):
            bm = b_ref[:, g * SSD_N:(g + 1) * SSD_N]
            cm = c_ref[:, g * SSD_N:(g + 1) * SSD_N]
            cb = _dot(cm, bm, ((1,), (1,)))
            for qq in range(PAIRS_PER_GROUP):
                q = g * PAIRS_PER_GROUP + qq
                ha, hb = 2 * q, 2 * q + 1
                csa, csb = cs_col[:, ha:ha + 1], cs_col[:, hb:hb + 1]
                xp = x_ref[:, q * LANE:(q + 1) * LANE]
                xx = xp * jnp.where(lo, dtc[:, ha:ha + 1], dtc[:, hb:hb + 1])
                ga = cb * jnp.exp(jnp.where(causal, csa - cs_row[ha:ha + 1, :], NEG))
                gb = cb * jnp.exp(jnp.where(causal, csb - cs_row[hb:hb + 1, :], NEG))
                y = _dot(ga, jnp.where(lo, xx, 0.0), ((1,), (0,))) + _dot(gb, jnp.where(lo, 0.0, xx), ((1,), (0,)))
                s_in = state[q]
                y = y + _dot(cm, s_in, ((1,), (0,))) * jnp.where(lo, jnp.exp(csa), jnp.exp(csb))
                y = y + jnp.where(lo1, prow[2:3, ha:ha + 1], prow[2:3, hb:hb + 1]) * xp
                y_ref[:, q * LANE:(q + 1) * LANE] = y
                la, lb = csa[SSD_L - 1:SSD_L, :], csb[SSD_L - 1:SSD_L, :]
                decay = jnp.where(lo, jnp.exp(la - csa), jnp.exp(lb - csb))
                st_ref[q] = s_in
                state[q] = s_in * jnp.where(lo1, jnp.exp(la), jnp.exp(lb)) + _dot(bm, xx * decay, ((0,), (0,)))

    L = SSD_L
    return pl.pallas_call(
        body, name="ssd_fwd", grid=(SSD_NC,),
        in_specs=[pl.BlockSpec((L, SSD_INNER), lambda c: (c, 0)),
                  pl.BlockSpec((L, SSD_G * SSD_N), lambda c: (c, SSD_INNER // (SSD_G * SSD_N))),
                  pl.BlockSpec((L, SSD_G * SSD_N), lambda c: (c, SSD_INNER // (SSD_G * SSD_N) + 1)),
                  pl.BlockSpec((L, LANE), lambda c: (c, SM_DT // LANE)),
                  pl.BlockSpec((LANE, L), lambda c: (0, c)),
                  pl.BlockSpec((8, LANE), lambda c: (0, 0)), pl.BlockSpec((LANE, 8), lambda c: (0, 0))],
        out_specs=[pl.BlockSpec((L, SSD_INNER), lambda c: (c, 0)),
                   pl.BlockSpec((None, NPAIR, SSD_N, LANE), lambda c: (c, 0, 0, 0))],
        out_shape=[jax.ShapeDtypeStruct((S, SSD_INNER), F32), jax.ShapeDtypeStruct((SSD_NC, NPAIR, SSD_N, LANE), F32)],
        scratch_shapes=[pltpu.VMEM((NPAIR, SSD_N, LANE), F32)],
        compiler_params=pltpu.CompilerParams(dimension_semantics=("arbitrary",)),
    )(act, act, act, small, dtT, prow, pcol)


def _ssd_bwd(act, small, dtT, prow, pcol, states, dy):
    def body(x_ref, b_ref, c_ref, dt_ref, dtT_ref, prow_ref, pcol_ref, st_ref, dy_ref,
             dx_ref, ddt_ref, dp_ref, dstate):
        @pl.when(pl.program_id(0) == 0)
        def _():
            dstate[...] = jnp.zeros_like(dstate)
            dp_ref[...] = jnp.zeros_like(dp_ref)

        prow, causal, pre_c, dtc, a_row, cs_col, cs_row = _ssd_chunk_common(dt_ref, dtT_ref, prow_ref, pcol_ref)
        lane = lax.broadcasted_iota(jnp.int32, (SSD_L, LANE), 1)
        sub = lax.broadcasted_iota(jnp.int32, (LANE, SSD_L), 0)
        rowi = lax.broadcasted_iota(jnp.int32, (SSD_L, 1), 0)
        lane1 = lane[0:1, :]
        lo = lane < SSD_P
        lo1 = lo[0:1, :]
        dcs_c = jnp.zeros((SSD_L, LANE), F32)
        dcs_r = jnp.zeros((LANE, SSD_L), F32)
        ddt_x = jnp.zeros((SSD_L, LANE), F32)
        dd_row = jnp.zeros((1, LANE), F32)
        for g in range(SSD_G):
            bm = b_ref[:, g * SSD_N:(g + 1) * SSD_N]
            cm = c_ref[:, g * SSD_N:(g + 1) * SSD_N]
            cb = _dot(cm, bm, ((1,), (1,)))
            dcb = jnp.zeros((SSD_L, SSD_L), F32)
            dbm = jnp.zeros((SSD_L, SSD_N), F32)
            dcm = jnp.zeros((SSD_L, SSD_N), F32)
            for qq in range(PAIRS_PER_GROUP):
                q = g * PAIRS_PER_GROUP + qq
                ha, hb = 2 * q, 2 * q + 1
                csa, csb = cs_col[:, ha:ha + 1], cs_col[:, hb:hb + 1]
                xp = x_ref[:, q * LANE:(q + 1) * LANE]
                dtp = jnp.where(lo, dtc[:, ha:ha + 1], dtc[:, hb:hb + 1])
                xx = xp * dtp
                lma = jnp.exp(jnp.where(causal, csa - cs_row[ha:ha + 1, :], NEG))
                lmb = jnp.exp(jnp.where(causal, csb - cs_row[hb:hb + 1, :], NEG))
                ga, gb = cb * lma, cb * lmb
                dyp = dy_ref[:, q * LANE:(q + 1) * LANE]
                dya, dyb = jnp.where(lo, dyp, 0.0), jnp.where(lo, 0.0, dyp)
                s_in = st_ref[q]
                ds_out = dstate[q]
                la, lb = csa[SSD_L - 1:SSD_L, :], csb[SSD_L - 1:SSD_L, :]
                ecs = jnp.where(lo, jnp.exp(csa), jnp.exp(csb))
                decay = jnp.where(lo, jnp.exp(la - csa), jnp.exp(lb - csb))
                cd = jnp.where(lo1, jnp.exp(la), jnp.exp(lb))
                bds = _dot(bm, ds_out, ((1,), (0,)))
                dxx = _dot(ga, dya, ((0,), (0,))) + _dot(gb, dyb, ((0,), (0,))) + bds * decay
                dga = _dot(dya, xx, ((1,), (1,)))
                dgb = _dot(dyb, xx, ((1,), (1,)))
                dsega, dsegb = dga * ga, dgb * gb
                dcb = dcb + dga * lma + dgb * lmb
                yoff = _dot(cm, s_in, ((1,), (0,))) * ecs
                dye = dyp * ecs
                dcm = dcm + _dot(dye, s_in, ((1,), (1,)))
                xd = xx * decay
                dbm = dbm + _dot(xd, ds_out, ((1,), (1,)))
                wv = xd * bds
                t1 = dyp * yoff - wv
                col_a = (jnp.sum(dsega, axis=1, keepdims=True) + jnp.sum(jnp.where(lo, t1, 0.0), axis=1, keepdims=True))
                col_b = (jnp.sum(dsegb, axis=1, keepdims=True) + jnp.sum(jnp.where(lo, 0.0, t1), axis=1, keepdims=True))
                sprod = ds_out * s_in
                end_a = jnp.sum(jnp.where(lo, wv, 0.0), keepdims=True) + jnp.exp(la) * jnp.sum(jnp.where(lo[:SSD_N], sprod, 0.0), keepdims=True)
                end_b = jnp.sum(jnp.where(lo, 0.0, wv), keepdims=True) + jnp.exp(lb) * jnp.sum(jnp.where(lo[:SSD_N], 0.0, sprod), keepdims=True)
                col_a = col_a + jnp.where(rowi == SSD_L - 1, end_a, 0.0)
                col_b = col_b + jnp.where(rowi == SSD_L - 1, end_b, 0.0)
                dcs_c = dcs_c + jnp.where(lane == ha, col_a, 0.0) + jnp.where(lane == hb, col_b, 0.0)
                dcs_r = (dcs_r + jnp.where(sub == ha, jnp.sum(dsega, axis=0, keepdims=True), 0.0)
                         + jnp.where(sub == hb, jnp.sum(dsegb, axis=0, keepdims=True), 0.0))
                dstate[q] = _dot(cm, dye, ((0,), (0,))) + cd * ds_out
                dpair = jnp.where(lo1, prow[2:3, ha:ha + 1], prow[2:3, hb:hb + 1])
                dx_ref[:, q * LANE:(q + 1) * LANE] = dxx * dtp + dpair * dyp
                t2 = dxx * xp
                ddt_x = (ddt_x + jnp.where(lane == ha, jnp.sum(jnp.where(lo, t2, 0.0), axis=1, keepdims=True), 0.0)
                         + jnp.where(lane == hb, jnp.sum(jnp.where(lo, 0.0, t2), axis=1, keepdims=True), 0.0))
                t3 = dyp * xp
                dd_row = (dd_row + jnp.where(lane1 == ha, jnp.sum(jnp.where(lo, t3, 0.0), keepdims=True), 0.0)
                          + jnp.where(lane1 == hb, jnp.sum(jnp.where(lo, 0.0, t3), keepdims=True), 0.0))
            dx_ref[:, SSD_INNER + g * SSD_N:SSD_INNER + (g + 1) * SSD_N] = dbm + _dot(dcb, cm, ((0,), (0,)))
            dx_ref[:, SSD_INNER + (SSD_G + g) * SSD_N:SSD_INNER + (SSD_G + g + 1) * SSD_N] = dcm + _dot(dcb, bm, ((1,), (0,)))
        ri = lax.broadcasted_iota(jnp.int32, (SSD_L, SSD_L), 0)
        ci = lax.broadcasted_iota(jnp.int32, (SSD_L, SSD_L), 1)
        da = _dot3((ri <= ci).astype(BF16), dcs_c, ((1,), (0,)), False)
        da = da - _dot3(dcs_r, causal.astype(BF16), ((1,), (0,)), True).T
        ddt = ddt_x + da * a_row
        ddt_raw = ddt * _sigmoid(pre_c)
        ddt_ref[...] = ddt_raw
        da_head = jnp.sum(da * dtc, axis=0, keepdims=True) * a_row
        dp_ref[0:1, :] += jnp.sum(ddt_raw, axis=0, keepdims=True)
        dp_ref[1:2, :] += da_head
        dp_ref[2:3, :] += dd_row

    L = SSD_L
    rev = SSD_NC - 1
    bc_cols = SSD_INNER // (SSD_G * SSD_N)
    return pl.pallas_call(
        body, name="ssd_bwd", grid=(SSD_NC,),
        in_specs=[pl.BlockSpec((L, SSD_INNER), lambda c: (rev - c, 0)),
                  pl.BlockSpec((L, SSD_G * SSD_N), lambda c: (rev - c, bc_cols)),
                  pl.BlockSpec((L, SSD_G * SSD_N), lambda c: (rev - c, bc_cols + 1)),
                  pl.BlockSpec((L, LANE), lambda c: (rev - c, SM_DT // LANE)),
                  pl.BlockSpec((LANE, L), lambda c: (0, rev - c)),
                  pl.BlockSpec((8, LANE), lambda c: (0, 0)), pl.BlockSpec((LANE, 8), lambda c: (0, 0)),
                  pl.BlockSpec((None, NPAIR, SSD_N, LANE), lambda c: (rev - c, 0, 0, 0)),
                  pl.BlockSpec((L, SSD_INNER), lambda c: (rev - c, 0))],
        out_specs=[pl.BlockSpec((L, SSD_XBC), lambda c: (rev - c, 0)),
                   pl.BlockSpec((L, LANE), lambda c: (rev - c, 0)),
                   pl.BlockSpec((8, LANE), lambda c: (0, 0))],
        out_shape=[jax.ShapeDtypeStruct((S, SSD_XBC), F32), jax.ShapeDtypeStruct((S, LANE), F32),
                   jax.ShapeDtypeStruct((8, LANE), F32)],
        scratch_shapes=[pltpu.VMEM((NPAIR, SSD_N, LANE), F32)],
        compiler_params=pltpu.CompilerParams(dimension_semantics=("arbitrary",)),
    )(act, act, act, small, dtT, prow, pcol, states, dy)


TQ = 256
TK = 256


def _attn_fwd(qc, kc, v):
    def body(q_ref, k_ref, v_ref, o_ref, lse_ref):
        i = pl.program_id(1)
        lo = lax.broadcasted_iota(jnp.int32, (TQ, LANE), 1) < VDIM
        rq = i * TQ + lax.broadcasted_iota(jnp.int32, (TQ, TK), 0)
        ck0 = lax.broadcasted_iota(jnp.int32, (TQ, TK), 1)
        qa, qb = q_ref[:, 0:LANE], q_ref[:, LANE:2 * LANE]

        def step(kb, carry):
            ma, la, mb, lb, acc = carry
            off = pl.multiple_of(kb * TK, TK)
            kk = k_ref[pl.ds(off, TK), :]
            vv = v_ref[pl.ds(off, TK), :]
            mask = rq >= ck0 + kb * TK
            sa = jnp.where(mask, _dot(qa, kk[:, 0:LANE], ((1,), (1,))) * ATT_SCALE, NEG)
            sb = jnp.where(mask, _dot(qb, kk[:, LANE:2 * LANE], ((1,), (1,))) * ATT_SCALE, NEG)
            na = jnp.maximum(ma, jnp.max(sa, axis=1, keepdims=True))
            nb = jnp.maximum(mb, jnp.max(sb, axis=1, keepdims=True))
            pa, pb = jnp.exp(sa - na), jnp.exp(sb - nb)
            fa, fb = jnp.exp(ma - na), jnp.exp(mb - nb)
            la = fa * la + jnp.sum(pa, axis=1, keepdims=True)
            lb = fb * lb + jnp.sum(pb, axis=1, keepdims=True)
            acc = (acc * jnp.where(lo, fa, fb) + _dot(pa, jnp.where(lo, vv, 0), ((1,), (0,)))
                   + _dot(pb, jnp.where(lo, 0, vv), ((1,), (0,))))
            return na, la, nb, lb, acc

        neg = jnp.full((TQ, 1), NEG, F32)
        zero = jnp.zeros((TQ, 1), F32)
        ma, la, mb, lb, acc = lax.fori_loop(0, i + 1, step, (neg, zero, neg, zero, jnp.zeros((TQ, LANE), F32)))
        o_ref[...] = acc / jnp.where(lo, la, lb)
        lse_ref[...] = jnp.where(lo, ma + jnp.log(la), mb + jnp.log(lb))

    return pl.pallas_call(
        body, name="attn_fwd", grid=(NPAIR, S // TQ),
        in_specs=[pl.BlockSpec((TQ, 2 * LANE), lambda j, i: (i, j)), pl.BlockSpec((S, 2 * LANE), lambda j, i: (0, j)),
                  pl.BlockSpec((S, LANE), lambda j, i: (0, j))],
        out_specs=[pl.BlockSpec((TQ, LANE), lambda j, i: (i, j)), pl.BlockSpec((None, TQ, LANE), lambda j, i: (j, i, 0))],
        out_shape=[jax.ShapeDtypeStruct((S, H * VDIM), F32), jax.ShapeDtypeStruct((NPAIR, S, LANE), F32)],
        compiler_params=pltpu.CompilerParams(dimension_semantics=("parallel", "parallel")),
    )(qc, kc, v)


def _attn_bwd(qc, kc, v, o, lse, do):
    nq = S // TQ

    def body(q_ref, k_ref, v_ref, o_ref, lse_ref, do_ref, dq_ref, dk_ref, dv_ref):
        kb = pl.program_id(1)

        @pl.when(kb == 0)
        def _():
            dq_ref[...] = jnp.zeros_like(dq_ref)

        lo = lax.broadcasted_iota(jnp.int32, (TQ, LANE), 1) < VDIM
        r0 = lax.broadcasted_iota(jnp.int32, (TQ, TK), 0)
        ck = kb * TK + lax.broadcasted_iota(jnp.int32, (TQ, TK), 1)
        ka, kbb = k_ref[:, 0:LANE], k_ref[:, LANE:2 * LANE]
        vv = v_ref[...]

        def step(qi, carry):
            dka, dkb, dv = carry
            off = pl.multiple_of(qi * TQ, TQ)
            qq = q_ref[pl.ds(off, TQ), :]
            dd = do_ref[pl.ds(off, TQ), :]
            ls = lse_ref[pl.ds(off, TQ), :]
            t = dd * o_ref[pl.ds(off, TQ), :]
            mask = r0 + qi * TQ >= ck
            outs = []
            for x, (kx, lsx) in enumerate(((ka, ls[:, 0:1]), (kbb, ls[:, VDIM:VDIM + 1]))):
                sel = lo if x == 0 else jnp.logical_not(lo)
                qx = qq[:, x * LANE:(x + 1) * LANE]
                dox = jnp.where(sel, dd, 0.0)
                delta = jnp.sum(jnp.where(sel, t, 0.0), axis=1, keepdims=True)
                sc = jnp.where(mask, _dot(qx, kx, ((1,), (1,))) * ATT_SCALE, NEG)
                p = jnp.exp(sc - lsx)
                dp = _dot(dox, vv, ((1,), (1,)))
                ds = p * (dp - delta) * ATT_SCALE
                dv = dv + _dot(p, dox, ((0,), (0,)))
                outs.append(_dot(ds, qx, ((0,), (0,))))
                dq_ref[pl.ds(off, TQ), x * LANE:(x + 1) * LANE] += _dot(ds, kx, ((1,), (0,)))
            return dka + outs[0], dkb + outs[1], dv

        z = jnp.zeros((TK, LANE), F32)
        dka, dkb, dv = lax.fori_loop(kb, nq, step, (z, z, z))
        dk_ref[:, 0:LANE] = dka
        dk_ref[:, LANE:2 * LANE] = dkb
        dv_ref[...] = dv

    return pl.pallas_call(
        body, name="attn_bwd", grid=(NPAIR, S // TK),
        in_specs=[pl.BlockSpec((S, 2 * LANE), lambda j, k: (0, j)), pl.BlockSpec((TK, 2 * LANE), lambda j, k: (k, j)),
                  pl.BlockSpec((TK, LANE), lambda j, k: (k, j)), pl.BlockSpec((S, LANE), lambda j, k: (0, j)),
                  pl.BlockSpec((None, S, LANE), lambda j, k: (j, 0, 0)), pl.BlockSpec((S, LANE), lambda j, k: (0, j))],
        out_specs=[pl.BlockSpec((S, 2 * LANE), lambda j, k: (0, j)), pl.BlockSpec((TK, 2 * LANE), lambda j, k: (k, j)),
                   pl.BlockSpec((TK, LANE), lambda j, k: (k, j))],
        out_shape=[jax.ShapeDtypeStruct((S, H * LANE), F32), jax.ShapeDtypeStruct((S, H * LANE), F32),
                   jax.ShapeDtypeStruct((S, H * VDIM), F32)],
        compiler_params=pltpu.CompilerParams(dimension_semantics=("parallel", "arbitrary")),
    )(qc, kc, v, o, lse, do)


_IN_Z, _IN_XBC, _IN_DT, _IN_Q, _IN_KV, _IN_KR = 0, 1024, 2560, 2576, 2960, 3216


def _prep_weights(w_in, w_qb, w_kvb):
    dt = w_in.dtype
    w_small = jnp.concatenate(
        [w_in[:, _IN_Q:_IN_KV], w_in[:, _IN_KV:_IN_KR], w_in[:, _IN_KR:IN_WIDTH], jnp.zeros((D, LANE - ROPE), dt),
         w_in[:, _IN_DT:_IN_Q], jnp.zeros((D, LANE - H), dt)], axis=1)
    w_q = jnp.pad(w_qb.reshape(Q_RANK, H, NOPE + ROPE), ((0, 0), (0, 0), (0, LANE - NOPE - ROPE))).reshape(Q_RANK, H * LANE)
    kv3 = w_kvb.reshape(KV_RANK, H, NOPE + VDIM)
    w_k = jnp.pad(kv3[:, :, :NOPE], ((0, 0), (0, 0), (0, LANE - NOPE))).reshape(KV_RANK, H * LANE)
    w_v = kv3[:, :, NOPE:].reshape(KV_RANK, H * VDIM)
    return w_in[:, _IN_Z:_IN_XBC], w_in[:, _IN_XBC:_IN_DT], w_small, w_q, w_k, w_v


def _rope_tables(positions):
    inv_freq = 1.0 / (10000.0 ** (jnp.arange(0, ROPE, 2, dtype=F32) / ROPE))
    ang = positions.astype(F32).reshape(S, 1) * inv_freq
    cos, sin = jnp.cos(ang), jnp.sin(ang)
    cos_t = jnp.concatenate([jnp.ones((S, NOPE), F32), cos, cos, jnp.ones((S, LANE - NOPE - ROPE), F32)], axis=1)
    sin_t = jnp.concatenate([jnp.zeros((S, NOPE), F32), -sin, sin, jnp.zeros((S, LANE - NOPE - ROPE), F32)], axis=1)
    return cos_t, sin_t


def _local_step(x, p, positions, target, wb, sp):
    w_z, w_xbc, w_small, w_q, w_k, w_v = _prep_weights(wb["w_in"], wb["w_qb"], wb["w_kvb"])
    w_out_s, w_out_m = wb["w_out"][:SSD_INNER], wb["w_out"][SSD_INNER:]
    cos_t, sin_t = _rope_tables(positions)
    prow = jnp.zeros((8, LANE), F32).at[0, :H].set(sp["dt_bias"][0]).at[1, :H].set(sp["A_log"][0]).at[2, :H].set(sp["D"][0])
    pcol = prow.T

    xb, pb = x.astype(BF16), p.astype(BF16)
    z = _mm([(xb, w_z)], name="proj_z")
    xbc = _mm([(xb, w_xbc)], name="proj_xbc")
    small = _mm([(xb, w_small)], name="proj_small")
    act = _conv_fwd(xbc, sp["conv_w"], sp["conv_b"])
    dt_t = small[:, SM_DT:SM_DT + LANE].T
    y, states = _ssd_fwd(act, small, dt_t, prow, pcol)
    y_ssd = _gate_norm_fwd(y, z, sp["ssd_norm"])
    q_c, kv_c = small[:, SM_Q:SM_Q + Q_RANK], small[:, SM_KV:SM_KV + KV_RANK]
    qn = _rms_fwd(q_c, sp["q_norm"], name="q_norm_fwd")
    kvn = _rms_fwd(kv_c, sp["kv_norm"], name="kv_norm_fwd")
    qcat = _q_rope(_mm([(qn, w_q)], name="q_up"), cos_t, sin_t)
    kcat = _k_prep(_mm([(kvn, w_k)], name="k_up"), small, cos_t, sin_t)
    v = _mm([(kvn, w_v)], out_dtype=BF16, name="v_up")
    o, lse = _attn_fwd(qcat, kcat, v)
    y_mla = _rms_fwd(o, sp["out_norm"], name="out_norm_fwd")
    mix = _mm([(y_ssd, w_out_s), (y_mla, w_out_m)], name="out_proj")
    h1, h1b = _ln_fwd(x, mix, sp["ln_mix_g"], sp["ln_mix_b"])
    gate = _mm([(h1b, wb["w_gate"])], name="ffn_gate")
    up = _mm([(h1b, wb["w_up"])], name="ffn_up")
    actf = _swiglu_fwd(gate, up)
    ffn = _mm([(actf, wb["w_down"])], name="ffn_down")
    pg = _mm([(h1b, wb["w_pg"])], name="ple_gate")
    pp = _mm([(pb, wb["w_pp"])], name="ple_proj")
    dpre2, dpre2b, dpg, dpp, dg2, db2, loss_row = _final_fwd_bwd(h1, ffn, pg, pp, target, sp["ln_ffn_g"], sp["ln_ffn_b"])

    g = {"ln_ffn_g": dg2, "ln_ffn_b": db2}
    g["w_pp"] = _mm([(pb, dpp)], ta=True, name="d_w_ple_proj")
    g["w_pg"] = _mm([(h1b, dpg)], ta=True, name="d_w_ple_gate")
    g["w_down"] = _mm([(actf, dpre2b)], ta=True, name="d_w_down")
    dactf = _mm([(dpre2b, wb["w_down"])], tb=True, name="d_act")
    dgate, dup = _swiglu_bwd(gate, up, dactf)
    g["w_gate"] = _mm([(h1b, dgate)], ta=True, name="d_w_gate")
    g["w_up"] = _mm([(h1b, dup)], ta=True, name="d_w_up")
    dh1 = _mm([(dpg, wb["w_pg"]), (dgate, wb["w_gate"]), (dup, wb["w_up"])], tb=True, add=dpre2, add_scale=ALPHA, name="d_h1")
    dpre1, dpre1b, g["ln_mix_g"], g["ln_mix_b"] = _ln_bwd(x, mix, sp["ln_mix_g"], dh1)
    dy_ssd = _mm([(dpre1b, w_out_s)], tb=True, name="d_y_ssd")
    dy_mla = _mm([(dpre1b, w_out_m)], tb=True, name="d_y_mla")
    g["w_out"] = jnp.concatenate([_mm([(y_ssd, dpre1b)], ta=True, name="d_w_out_s"),
                                  _mm([(y_mla, dpre1b)], ta=True, name="d_w_out_m")], axis=0)
    do, g["out_norm"] = _rms_bwd(o, sp["out_norm"], dy_mla, name="out_norm_bwd")
    dq, dk, dv = _attn_bwd(qcat, kcat, v, o, lse, do)
    dqlin = _q_unrope(dq, cos_t, sin_t)
    dw_q = _mm([(qn, dqlin)], ta=True, name="d_w_q")
    dqn = _mm([(dqlin, w_q)], tb=True, name="d_qn")
    dq_c, g["q_norm"] = _rms_bwd(q_c, sp["q_norm"], dqn, name="q_norm_bwd")
    dkr = _k_rope_bwd(dk, cos_t, sin_t)
    dw_k = _mm([(kvn, dk)], ta=True, name="d_w_k")
    dw_v = _mm([(kvn, dv)], ta=True, name="d_w_v")
    dkvn = _mm([(dk, w_k), (dv, w_v)], tb=True, name="d_kvn")
    dkv_c, g["kv_norm"] = _rms_bwd(kv_c, sp["kv_norm"], dkvn, name="kv_norm_bwd")
    g["w_qb"] = dw_q.reshape(Q_RANK, H, LANE)[:, :, :NOPE + ROPE].reshape(Q_RANK, H * (NOPE + ROPE))
    g["w_kvb"] = jnp.concatenate([dw_k.reshape(KV_RANK, H, LANE)[:, :, :NOPE], dw_v.reshape(KV_RANK, H, VDIM)],
                                 axis=2).reshape(KV_RANK, H * (NOPE + VDIM))
    dy, dz, g["ssd_norm"] = _gate_norm_bwd(y, z, sp["ssd_norm"], dy_ssd)
    dact, ddt, dprow = _ssd_bwd(act, small, dt_t, prow, pcol, states, dy)
    g["dt_bias"], g["A_log"], g["D"] = dprow[0:1, :H], dprow[1:2, :H], dprow[2:3, :H]
    dxbc, g["conv_w"], g["conv_b"] = _conv_bwd(xbc, sp["conv_w"], sp["conv_b"], dact)
    dsmall = jnp.concatenate([dq_c, dkv_c, dkr, ddt], axis=1).astype(BF16)
    grad_x = _mm([(dz, w_z), (dxbc, w_xbc), (dsmall, w_small)], tb=True, add=dpre1, add_scale=ALPHA, name="d_x")
    dw_small = _mm([(xb, dsmall)], ta=True, name="d_w_small")
    g["w_in"] = jnp.concatenate(
        [_mm([(xb, dz)], ta=True, name="d_w_z"), _mm([(xb, dxbc)], ta=True, name="d_w_xbc"), dw_small[:, SM_DT:SM_DT + H],
         dw_small[:, SM_Q:SM_Q + Q_RANK], dw_small[:, SM_KV:SM_KV + KV_RANK], dw_small[:, SM_KR:SM_KR + ROPE]], axis=1)
    return loss_row, grad_x, g


NCHIP = 4
MESH = pl.DeviceIdType.MESH
PACK_C = 1024
HALF_ROWS = 2016
PACK_WORDS = 2 * HALF_ROWS * PACK_C
BIG = (("w_in", (D, IN_WIDTH), 1), ("w_qb", (Q_RANK, H * (NOPE + ROPE)), 1), ("w_kvb", (KV_RANK, H * (NOPE + VDIM)), 1),
       ("w_out", (2 * SSD_INNER, D), 0), ("w_gate", (D, D_FF), 1), ("w_up", (D, D_FF), 1), ("w_down", (D_FF, D), 0),
       ("w_pg", (D, D), 0), ("w_pp", (PLE, D), 1))
CONV_SHARD = SSD_XBC // NCHIP


def _shard_shape(shape, axis):
    return (shape[0] // NCHIP, shape[1]) if axis == 0 else (shape[0], shape[1] // NCHIP)


def _pack(parts):
    flat = jnp.concatenate([a.reshape(-1) for a in parts])
    return jnp.pad(flat, (0, PACK_WORDS - flat.shape[0])).reshape(2, HALF_ROWS, PACK_C)


def _unpack(flat, with_conv):
    out, off = {}, 0
    for name, shape, axis in BIG:
        r, c = _shard_shape(shape, axis)
        out[name] = flat[off:off + r * c].reshape(r, c)
        off += r * c
    if with_conv:
        out["conv_w"] = lax.bitcast_convert_type(flat[off:off + 2 * SSD_K * CONV_SHARD].reshape(SSD_K, CONV_SHARD, 2), F32)
    return out


def _coords():
    return lax.axis_index("x"), lax.axis_index("y"), lax.axis_index("c")


def _gather_weights(wp):
    def body(wp_ref, out_ref, send_sems, recv_sems, local_sem):
        x, y, c = _coords()
        k = 2 * x + y
        chips = [(1 - x, y), (x, 1 - y), (1 - x, 1 - y)]
        sibling = (x, y, 1 - c)

        def copy(j, src, dst, to):
            return pltpu.make_async_remote_copy(src_ref=src, dst_ref=dst, send_sem=send_sems.at[j], recv_sem=recv_sems.at[j],
                                                device_id=to, device_id_type=MESH)

        mine = pltpu.make_async_copy(wp_ref, out_ref.at[k], local_sem)
        mine.start()
        first = [copy(j, wp_ref.at[c], out_ref.at[k, c], (cx, cy, c)) for j, (cx, cy) in enumerate(chips)]
        for cp in first:
            cp.start()
        passed = []
        for j, (cx, cy) in enumerate(chips):
            landed = out_ref.at[2 * cx + cy, c]
            copy(j, wp_ref.at[c], landed, (cx, cy, c)).wait_recv()
            passed.append(copy(NCHIP - 1 + j, landed, landed, sibling))
            passed[-1].start()
        for j, (cx, cy) in enumerate(chips):
            other = out_ref.at[2 * cx + cy, 1 - c]
            copy(NCHIP - 1 + j, other, other, sibling).wait_recv()
        for cp in first + passed:
            cp.wait_send()
        mine.wait()

    n = 2 * (NCHIP - 1)
    return pl.pallas_call(
        body, name="gather_weights", in_specs=[pl.BlockSpec(memory_space=pl.ANY)], out_specs=pl.BlockSpec(memory_space=pl.ANY),
        out_shape=jax.ShapeDtypeStruct((NCHIP, 2, HALF_ROWS, PACK_C), BF16),
        scratch_shapes=[pltpu.SemaphoreType.DMA((n,)), pltpu.SemaphoreType.DMA((n,)), pltpu.SemaphoreType.DMA],
    )(wp)


RS_CH = 672
RS_NCH = HALF_ROWS // RS_CH


def _reduce_grads(gp):
    def body(gp_ref, out_ref, r1_ref, part_ref, r2_ref, va, vb, vo, vs, vf, send_sems, recv_sems, local_sem):
        x, y, c = _coords()
        k = 2 * x + y
        chips = [(1 - x, y), (x, 1 - y), (1 - x, 1 - y)]
        sibling = (x, y, 1 - c)

        def copy(j, src, dst, to):
            return pltpu.make_async_remote_copy(src_ref=src, dst_ref=dst, send_sem=send_sems.at[j], recv_sem=recv_sems.at[j],
                                                device_id=to, device_id_type=MESH)

        pair = copy(0, gp_ref.at[1 - c], r1_ref, sibling)
        pair.start()
        pair.wait_recv()

        def add_chunk(t, carry):
            kk = t // RS_NCH
            rows = pl.ds(pl.multiple_of((t % RS_NCH) * RS_CH, 16), RS_CH)
            pltpu.sync_copy(gp_ref.at[c, kk, rows], va)
            pltpu.sync_copy(r1_ref.at[kk, rows], vb)
            vo[...] = (va[...].astype(F32) + vb[...].astype(F32)).astype(BF16)
            pltpu.sync_copy(vo, part_ref.at[kk, rows])
            return carry

        lax.fori_loop(0, NCHIP * RS_NCH, add_chunk, 0)
        sends = [copy(1 + j, part_ref.at[2 * cx + cy], r2_ref.at[k], (cx, cy, c)) for j, (cx, cy) in enumerate(chips)]
        for cp in sends:
            cp.start()
        own = pltpu.make_async_copy(part_ref.at[k], r2_ref.at[k], local_sem)
        own.start()
        own.wait()
        for j, (cx, cy) in enumerate(chips):
            copy(1 + j, part_ref.at[k], r2_ref.at[2 * cx + cy], (cx, cy, c)).wait_recv()

        def sum_chunk(r, carry):
            rows = pl.ds(pl.multiple_of(r * RS_CH, 16), RS_CH)
            pltpu.sync_copy(r2_ref.at[:, rows], vs)
            acc = vs[0].astype(F32)
            for kk in range(1, NCHIP):
                acc = acc + vs[kk].astype(F32)
            vf[...] = acc
            pltpu.sync_copy(vf, out_ref.at[c, rows])
            return carry

        lax.fori_loop(0, RS_NCH, sum_chunk, 0)
        done = copy(NCHIP, out_ref.at[c], out_ref.at[c], sibling)
        done.start()
        copy(NCHIP, out_ref.at[1 - c], out_ref.at[1 - c], sibling).wait_recv()
        done.wait_send()
        pair.wait_send()
        for cp in sends:
            cp.wait_send()

    any_spec = pl.BlockSpec(memory_space=pl.ANY)
    stage = jax.ShapeDtypeStruct((NCHIP, HALF_ROWS, PACK_C), BF16)
    n = NCHIP + 1
    return pl.pallas_call(
        body, name="reduce_grads", in_specs=[any_spec], out_specs=[any_spec] * 4,
        out_shape=[jax.ShapeDtypeStruct((2, HALF_ROWS, PACK_C), F32), stage, stage, stage],
        scratch_shapes=[pltpu.VMEM((RS_CH, PACK_C), BF16), pltpu.VMEM((RS_CH, PACK_C), BF16), pltpu.VMEM((RS_CH, PACK_C), BF16),
                        pltpu.VMEM((NCHIP, RS_CH, PACK_C), BF16), pltpu.VMEM((RS_CH, PACK_C), F32),
                        pltpu.SemaphoreType.DMA((n,)), pltpu.SemaphoreType.DMA((n,)), pltpu.SemaphoreType.DMA],
    )(gp)[0]


SMALL = (("conv_w", SSD_K * SSD_XBC), ("conv_b", SSD_XBC), ("dt_bias", H), ("A_log", H), ("D", H), ("ssd_norm", SSD_INNER),
         ("q_norm", Q_RANK), ("kv_norm", KV_RANK), ("out_norm", SSD_INNER), ("ln_mix_g", D), ("ln_mix_b", D),
         ("ln_ffn_g", D), ("ln_ffn_b", D))
SMALL_ROWS = 120
NDEV = 8


def _allreduce_small(sv):
    def body(sv_ref, out_ref, slots, send_sems, recv_sems):
        x, y, c = _coords()
        me = 4 * x + 2 * y + c
        slots[me] = sv_ref[...]
        copies = []
        for d in range(1, NDEV):
            to = (x ^ (d >> 2), y ^ ((d >> 1) & 1), c ^ (d & 1))
            copies.append(pltpu.make_async_remote_copy(src_ref=sv_ref, dst_ref=slots.at[me], send_sem=send_sems.at[d - 1],
                                                       recv_sem=recv_sems.at[d - 1], device_id=to, device_id_type=MESH))
            copies[-1].start()
        for cp in copies:
            cp.wait_recv()
        for cp in copies:
            cp.wait_send()
        acc = slots[0]
        for i in range(1, NDEV):
            acc = acc + slots[i]
        out_ref[...] = acc

    vm = pl.BlockSpec(memory_space=pltpu.VMEM)
    return pl.pallas_call(
        body, name="allreduce_small", in_specs=[vm], out_specs=vm, out_shape=jax.ShapeDtypeStruct((SMALL_ROWS, LANE), F32),
        scratch_shapes=[pltpu.VMEM((NDEV, SMALL_ROWS, LANE), F32), pltpu.SemaphoreType.DMA((NDEV - 1,)),
                        pltpu.SemaphoreType.DMA((NDEV - 1,))],
    )(sv)


def _adamw_math(w, g, m, v):
    m2 = ADAM_B1 * m + (1.0 - ADAM_B1) * g
    v2 = ADAM_B2 * v + (1.0 - ADAM_B2) * (g * g)
    m_hat = m2 / (1.0 - ADAM_B1 ** ADAM_STEP)
    v_hat = v2 / (1.0 - ADAM_B2 ** ADAM_STEP)
    return -ADAM_LR * (m_hat / (jnp.sqrt(v_hat) + ADAM_EPS) + ADAM_WD * w), m2, v2


def _adamw_big(w, g, m, v, *, name):
    r, c = w.shape
    tr = next(t for t in (512, 384, 352, 256, 128, 64, 8) if r % t == 0)

    def body(w_ref, g_ref, m_ref, v_ref, d_ref, m2_ref, v2_ref):
        d_ref[...], m2_ref[...], v2_ref[...] = _adamw_math(w_ref[...], g_ref[...], m_ref[...], v_ref[...])

    spec = pl.BlockSpec((tr, c), lambda i: (i, 0))
    return pl.pallas_call(body, name=name, grid=(r // tr,), in_specs=[spec] * 4, out_specs=[spec] * 3,
                          out_shape=[jax.ShapeDtypeStruct((r, c), F32)] * 3)(w, g, m, v)


def _adamw_small(ws, gs, ms, vs):
    n = len(ws)

    def body(*refs):
        for i in range(n):
            w_ref, g_ref, m_ref, v_ref = (refs[j * n + i] for j in range(4))
            d_ref, m2_ref, v2_ref = (refs[(4 + j) * n + i] for j in range(3))
            d_ref[...], m2_ref[...], v2_ref[...] = _adamw_math(w_ref[...], g_ref[...], m_ref[...], v_ref[...])

    vm = pl.BlockSpec(memory_space=pltpu.VMEM)
    shapes = [jax.ShapeDtypeStruct(w.shape, F32) for w in ws]
    outs = pl.pallas_call(body, name="adamw_small", in_specs=[vm] * (4 * n), out_specs=[vm] * (3 * n), out_shape=shapes * 3)(
        *ws, *gs, *ms, *vs)
    return outs[:n], outs[n:2 * n], outs[2 * n:]


_SMALL_ARG = {"conv_w": "ssd_conv_w", "conv_b": "ssd_conv_b", "dt_bias": "ssd_dt_bias", "A_log": "ssd_A_log", "D": "ssd_D",
              "ssd_norm": "ssd_norm_w", "q_norm": "mla_q_norm_w", "kv_norm": "mla_kv_norm_w", "out_norm": "mla_out_norm_w",
              "ln_mix_g": "ln_mix_g", "ln_mix_b": "ln_mix_b", "ln_ffn_g": "ln_ffn_g", "ln_ffn_b": "ln_ffn_b"}
_BIG_ARG = {"w_in": "w_in", "w_qb": "mla_w_q_b", "w_kvb": "mla_w_kv_b", "w_out": "w_out", "w_gate": "w_ffn_gate",
            "w_up": "w_ffn_up", "w_down": "w_ffn_down", "w_pg": "w_ple_gate", "w_pp": "w_ple_proj"}
_WEIGHT_ORDER = ("w_in", "ssd_conv_w", "ssd_conv_b", "ssd_dt_bias", "ssd_A_log", "ssd_D", "ssd_norm_w", "mla_q_norm_w", "mla_w_q_b",
                 "mla_kv_norm_w", "mla_w_kv_b", "mla_out_norm_w", "w_out", "ln_mix_g", "ln_mix_b", "w_ffn_gate", "w_ffn_up",
                 "w_ffn_down", "w_ple_gate", "w_ple_proj", "ln_ffn_g", "ln_ffn_b")


def _rows128(a):
    flat = a.reshape(-1)
    return jnp.pad(flat, (0, -flat.shape[0] % LANE)).reshape(-1, LANE)


def kernel(x, p, positions, w_in, ssd_conv_w, ssd_conv_b, ssd_dt_bias, ssd_A_log, ssd_D, ssd_norm_w, mla_q_norm_w, mla_w_q_b, mla_kv_norm_w, mla_w_kv_b, mla_out_norm_w, w_out, ln_mix_g, ln_mix_b, w_ffn_gate, w_ffn_up, w_ffn_down, w_ple_gate, w_ple_proj, ln_ffn_g, ln_ffn_b, loss_target, m_w_in, m_ssd_conv_w, m_ssd_conv_b, m_ssd_dt_bias, m_ssd_A_log, m_ssd_D, m_ssd_norm_w, m_mla_q_norm_w, m_mla_w_q_b, m_mla_kv_norm_w, m_mla_w_kv_b, m_mla_out_norm_w, m_w_out, m_ln_mix_g, m_ln_mix_b, m_w_ffn_gate, m_w_ffn_up, m_w_ffn_down, m_w_ple_gate, m_w_ple_proj, m_ln_ffn_g, m_ln_ffn_b, v_w_in, v_ssd_conv_w, v_ssd_conv_b, v_ssd_dt_bias, v_ssd_A_log, v_ssd_D, v_ssd_norm_w, v_mla_q_norm_w, v_mla_w_q_b, v_mla_kv_norm_w, v_mla_w_kv_b, v_mla_out_norm_w, v_w_out, v_ln_mix_g, v_ln_mix_b, v_w_ffn_gate, v_w_ffn_up, v_w_ffn_down, v_w_ple_gate, v_w_ple_proj, v_ln_ffn_g, v_ln_ffn_b):
    given = dict(locals())
    chip = 2 * lax.axis_index("x") + lax.axis_index("y")

    shard = [given[_BIG_ARG[name]][0].astype(BF16) for name, _, _ in BIG]
    shard.append(lax.bitcast_convert_type(ssd_conv_w[0], BF16))
    full = _gather_weights(_pack(shard)).reshape(NCHIP, PACK_WORDS)
    per_chip = [_unpack(full[k], True) for k in range(NCHIP)]
    wb = {name: jnp.concatenate([pc[name] for pc in per_chip], axis=axis) for name, _, axis in BIG}
    sp = {k: given[a] for k, a in _SMALL_ARG.items() if k != "conv_w"}
    sp["conv_w"] = jnp.concatenate([pc["conv_w"] for pc in per_chip], axis=1)

    loss_row, grad_x, g = _local_step(x[0], p[0, 0], positions[0], loss_target[0], wb, sp)

    by_chip = []
    for k in range(NCHIP):
        parts = []
        for name, shape, axis in BIG:
            r, c = _shard_shape(shape, axis)
            parts.append((g[name][k * r:(k + 1) * r] if axis == 0 else g[name][:, k * c:(k + 1) * c]).astype(BF16))
        by_chip.append(_pack(parts))
    gsum = _reduce_grads(jnp.stack(by_chip, axis=1)).reshape(PACK_WORDS)
    gbig = _unpack(gsum, False)
    small_in = jnp.concatenate([_rows128(g[name]) for name, _ in SMALL] + [loss_row], axis=0)
    small_sum = _allreduce_small(jnp.pad(small_in, ((0, SMALL_ROWS - small_in.shape[0]), (0, 0))))
    gsmall, row = {}, 0
    for name, size in SMALL:
        nrow = -(-size // LANE)
        gsmall[name] = small_sum[row:row + nrow].reshape(-1)[:size]
        row += nrow
    loss = small_sum[row, 0]

    grads = {}
    for name, shape, axis in BIG:
        grads[_BIG_ARG[name]] = gbig[name][None]
    for name, _ in SMALL:
        if name == "conv_w":
            full_g = gsmall[name].reshape(SSD_K, SSD_XBC)
            grads["ssd_conv_w"] = lax.dynamic_slice(full_g, (0, chip * CONV_SHARD), (SSD_K, CONV_SHARD))[None]
        else:
            grads[_SMALL_ARG[name]] = gsmall[name].reshape(given[_SMALL_ARG[name]].shape)

    delta, new_m, new_v = {}, {}, {}
    for name, _, _ in BIG:
        a = _BIG_ARG[name]
        d, m2, v2 = _adamw_big(given[a][0], grads[a][0], given["m_" + a][0], given["v_" + a][0], name="adamw_" + a)
        delta[a], new_m[a], new_v[a] = d[None], m2[None], v2[None]
    small_names = [_SMALL_ARG[name] for name, _ in SMALL]
    two_d = lambda t: t.reshape(t.shape[-2], t.shape[-1])
    ds, ms, vs = _adamw_small([two_d(given[a]) for a in small_names], [two_d(grads[a]) for a in small_names],
                              [two_d(given["m_" + a]) for a in small_names], [two_d(given["v_" + a]) for a in small_names])
    for a, d, m2, v2 in zip(small_names, ds, ms, vs):
        delta[a], new_m[a], new_v[a] = (t.reshape(given[a].shape) for t in (d, m2, v2))

    return (loss, grad_x[None], *[grads[n] for n in _WEIGHT_ORDER], *[delta[n] for n in _WEIGHT_ORDER],
            *[new_m[n] for n in _WEIGHT_ORDER], *[new_v[n] for n in _WEIGHT_ORDER])
```

```python
import functools
import math

import jax
import jax.numpy as jnp
from jax import lax
from jax.experimental import pallas as pl
from jax.experimental.pallas import tpu as pltpu

F32 = jnp.float32
BF16 = jnp.bfloat16

S = 2048
D = 1024
PLE = 256
H = 16
SSD_P = 64
SSD_INNER = 1024
SSD_N = 128
SSD_G = 2
SSD_L = 128
SSD_NC = S // SSD_L
SSD_XBC = 1536
SSD_K = 4
Q_RANK = 384
KV_RANK = 256
NOPE = 64
ROPE = 32
VDIM = 64
D_FF = 2816
IN_WIDTH = 3248
ALPHA = 2.0 ** 0.25
EPS_RMS = 1e-6
EPS_LN = 1e-5
ATT_SCALE = 1.0 / math.sqrt(NOPE + ROPE)
LANE = 128
NCHIP = 4
SMALL_W = 896
SM_Q, SM_KV, SM_KR, SM_DT = 0, 384, 640, 768
NEG = -1e30

ADAM_LR = 0.001
ADAM_B1 = 0.9
ADAM_B2 = 0.999
ADAM_EPS = 1e-08
ADAM_WD = 0.01
ADAM_STEP = 10


def _sigmoid(v):
    return 1.0 / (1.0 + jnp.exp(-v))


MM_VMEM_BUDGET = 36 * 2 ** 20
MM_MAX_ACC = 2048 * 1024


def _mm_tiles(pairs, ta, tb, m, n, out_dtype, has_add):
    def divs(v):
        return [LANE * d for d in range(v // LANE, 0, -1) if (v // LANE) % d == 0] if v % LANE == 0 else [v]

    def cost(tm, tn):
        tot = tm * tn * (jnp.dtype(out_dtype).itemsize + (4 if has_add else 0))
        for a, b in pairs:
            k = a.shape[-2] if ta else a.shape[-1]
            tot += k * (tm * a.dtype.itemsize + tn * b.dtype.itemsize)
        return 2 * tot

    ok = [(tm * tn, tm, tn) for tm in divs(m) for tn in divs(n) if tm * tn <= MM_MAX_ACC and cost(tm, tn) <= MM_VMEM_BUDGET]
    _, tm, tn = max(ok)
    return tm, tn


def _mm(pairs, *, ta=False, tb=False, out_dtype=F32, add=None, add_scale=1.0, chunk=None, name):
    n_pairs = len(pairs)
    a0, b0 = pairs[0]
    m = a0.shape[-1] if ta else a0.shape[-2]
    n = b0.shape[-2] if tb else b0.shape[-1]
    tm, tn = _mm_tiles(pairs, ta, tb, m, n, out_dtype, add is not None)
    dims = (((0 if ta else 1,), (1 if tb else 0,)), ((), ()))
    nk = NCHIP if chunk else 1
    assert chunk != "sum" or out_dtype == F32

    def body(*refs):
        o_ref = refs[-1]
        acc = None
        for i in range(n_pairs):
            a = refs[2 * i][...].astype(BF16)
            b = refs[2 * i + 1][...].astype(BF16)
            part = lax.dot_general(a, b, dims, preferred_element_type=F32)
            acc = part if acc is None else acc + part
        if chunk == "sum":
            k = pl.program_id(2)

            @pl.when(k == 0)
            def _():
                o_ref[...] = acc + add_scale * refs[2 * n_pairs][...] if add is not None else acc

            @pl.when(k > 0)
            def _():
                o_ref[...] += acc
        else:
            if add is not None:
                acc = acc + add_scale * refs[2 * n_pairs][...]
            o_ref[...] = acc.astype(out_dtype)

    def spec(arr, shape, idx2):
        if arr.ndim == 3:
            return pl.BlockSpec((None,) + shape, lambda i, j, k: (k,) + idx2(i, j))
        return pl.BlockSpec(shape, lambda i, j, k: idx2(i, j))

    in_specs, args = [], []
    for a, b in pairs:
        kdim = a.shape[-2] if ta else a.shape[-1]
        in_specs.append(spec(a, (kdim, tm), lambda i, j: (0, i)) if ta else spec(a, (tm, kdim), lambda i, j: (i, 0)))
        in_specs.append(spec(b, (tn, kdim), lambda i, j: (j, 0)) if tb else spec(b, (kdim, tn), lambda i, j: (0, j)))
        args += [a, b]
    if add is not None:
        in_specs.append(pl.BlockSpec((tm, tn), lambda i, j, k: (i, j)))
        args.append(add)
    if chunk == "out":
        out_spec = pl.BlockSpec((None, tm, tn), lambda i, j, k: (k, i, j))
        out_shape = jax.ShapeDtypeStruct((nk, m, n), out_dtype)
    else:
        out_spec = pl.BlockSpec((tm, tn), lambda i, j, k: (i, j))
        out_shape = jax.ShapeDtypeStruct((m, n), out_dtype)
    return pl.pallas_call(
        body, name=name, grid=(m // tm, n // tn, nk), in_specs=in_specs, out_specs=out_spec, out_shape=out_shape,
        compiler_params=pltpu.CompilerParams(dimension_semantics=("parallel", "parallel", "arbitrary")),
    )(*args)


TR = 256


def _row_spec(c):
    return pl.BlockSpec((TR, c), lambda i: (i, 0))


def _vec_spec(c):
    return pl.BlockSpec((1, c), lambda i: (0, 0))


def _acc_rows(ref, val):
    @pl.when(pl.program_id(0) == 0)
    def _():
        ref[...] = jnp.zeros_like(ref)
    ref[...] += val


def _rms_fwd(u, w, *, name):
    c = u.shape[1]

    def body(u_ref, w_ref, o_ref):
        v = u_ref[...]
        r = lax.rsqrt(jnp.mean(v * v, axis=-1, keepdims=True) + EPS_RMS)
        o_ref[...] = (v * r * w_ref[...]).astype(BF16)

    return pl.pallas_call(body, name=name, grid=(S // TR,), in_specs=[_row_spec(c), _vec_spec(c)], out_specs=_row_spec(c),
                          out_shape=jax.ShapeDtypeStruct((S, c), BF16))(u, w)


def _rms_bwd(u, w, dy, *, name):
    c = u.shape[1]

    def body(u_ref, w_ref, dy_ref, du_ref, dw_ref):
        v = u_ref[...]
        g = dy_ref[...].astype(F32)
        r = lax.rsqrt(jnp.mean(v * v, axis=-1, keepdims=True) + EPS_RMS)
        gw = g * w_ref[...]
        du_ref[...] = r * gw - v * (r * r * r * jnp.mean(gw * v, axis=-1, keepdims=True))
        _acc_rows(dw_ref, jnp.sum(g * v * r, axis=0, keepdims=True))

    return pl.pallas_call(body, name=name, grid=(S // TR,), in_specs=[_row_spec(c), _vec_spec(c), _row_spec(c)],
                          out_specs=[_row_spec(c), _vec_spec(c)],
                          out_shape=[jax.ShapeDtypeStruct((S, c), F32), jax.ShapeDtypeStruct((1, c), F32)])(u, w, dy)


def _gate_norm_fwd(y, z, w):
    def body(y_ref, z_ref, w_ref, o_ref):
        zz = z_ref[...]
        v = y_ref[...] * (zz * _sigmoid(zz))
        r = lax.rsqrt(jnp.mean(v * v, axis=-1, keepdims=True) + EPS_RMS)
        o_ref[...] = (v * r * w_ref[...]).astype(BF16)

    c = SSD_INNER
    return pl.pallas_call(body, name="ssd_gate_norm_fwd", grid=(S // TR,), in_specs=[_row_spec(c), _row_spec(c), _vec_spec(c)],
                          out_specs=_row_spec(c), out_shape=jax.ShapeDtypeStruct((S, c), BF16))(y, z, w)


def _gate_norm_bwd(y, z, w, dout):
    def body(y_ref, z_ref, w_ref, g_ref, dy_ref, dz_ref, dw_ref):
        yy = y_ref[...]
        zz = z_ref[...]
        sg = _sigmoid(zz)
        sz = zz * sg
        v = yy * sz
        g = g_ref[...]
        r = lax.rsqrt(jnp.mean(v * v, axis=-1, keepdims=True) + EPS_RMS)
        gw = g * w_ref[...]
        dv = r * gw - v * (r * r * r * jnp.mean(gw * v, axis=-1, keepdims=True))
        dy_ref[...] = dv * sz
        dz_ref[...] = (dv * yy * (sg * (1.0 + zz * (1.0 - sg)))).astype(BF16)
        _acc_rows(dw_ref, jnp.sum(g * v * r, axis=0, keepdims=True))

    c = SSD_INNER
    return pl.pallas_call(body, name="ssd_gate_norm_bwd", grid=(S // TR,),
                          in_specs=[_row_spec(c), _row_spec(c), _vec_spec(c), _row_spec(c)],
                          out_specs=[_row_spec(c), _row_spec(c), _vec_spec(c)],
                          out_shape=[jax.ShapeDtypeStruct((S, c), F32), jax.ShapeDtypeStruct((S, c), BF16),
                                     jax.ShapeDtypeStruct((1, c), F32)])(y, z, w, dout)


def _ln_fwd(xr, mix, g, b):
    def body(x_ref, m_ref, g_ref, b_ref, o_ref, ob_ref):
        pre = ALPHA * x_ref[...] + m_ref[...]
        mu = jnp.mean(pre, axis=-1, keepdims=True)
        d = pre - mu
        rs = lax.rsqrt(jnp.mean(d * d, axis=-1, keepdims=True) + EPS_LN)
        h = d * rs * g_ref[...] + b_ref[...]
        o_ref[...] = h
        ob_ref[...] = h.astype(BF16)

    return pl.pallas_call(body, name="ln_mix_fwd", grid=(S // TR,), in_specs=[_row_spec(D), _row_spec(D), _vec_spec(D), _vec_spec(D)],
                          out_specs=[_row_spec(D)] * 2,
                          out_shape=[jax.ShapeDtypeStruct((S, D), F32), jax.ShapeDtypeStruct((S, D), BF16)])(xr, mix, g, b)


def _ln_bwd(xr, mix, g, dh):
    def body(x_ref, m_ref, g_ref, dh_ref, dpre_ref, dpreb_ref, dg_ref, db_ref):
        pre = ALPHA * x_ref[...] + m_ref[...]
        mu = jnp.mean(pre, axis=-1, keepdims=True)
        d = pre - mu
        rs = lax.rsqrt(jnp.mean(d * d, axis=-1, keepdims=True) + EPS_LN)
        xh = d * rs
        dy = dh_ref[...]
        gy = dy * g_ref[...]
        dpre = rs * (gy - jnp.mean(gy, axis=-1, keepdims=True) - xh * jnp.mean(gy * xh, axis=-1, keepdims=True))
        dpre_ref[...] = dpre
        dpreb_ref[...] = dpre.astype(BF16)
        _acc_rows(dg_ref, jnp.sum(dy * xh, axis=0, keepdims=True))
        _acc_rows(db_ref, jnp.sum(dy, axis=0, keepdims=True))

    return pl.pallas_call(body, name="ln_mix_bwd", grid=(S // TR,),
                          in_specs=[_row_spec(D), _row_spec(D), _vec_spec(D), _row_spec(D)],
                          out_specs=[_row_spec(D), _row_spec(D), _vec_spec(D), _vec_spec(D)],
                          out_shape=[jax.ShapeDtypeStruct((S, D), F32), jax.ShapeDtypeStruct((S, D), BF16),
                                     jax.ShapeDtypeStruct((1, D), F32), jax.ShapeDtypeStruct((1, D), F32)])(xr, mix, g, dh)


FF_CHUNK = D_FF // NCHIP


def _ff_spec():
    return pl.BlockSpec((None, TR * 2, FF_CHUNK), lambda k, i: (k, i, 0))


def _swiglu_fwd(gate, up):
    def body(g_ref, u_ref, o_ref):
        g = g_ref[...]
        o_ref[...] = (g * _sigmoid(g) * u_ref[...]).astype(BF16)

    return pl.pallas_call(body, name="swiglu_fwd", grid=(NCHIP, S // (2 * TR)), in_specs=[_ff_spec()] * 2, out_specs=_ff_spec(),
                          out_shape=jax.ShapeDtypeStruct((NCHIP, S, FF_CHUNK), BF16))(gate, up)


def _swiglu_bwd(gate, up, dact):
    def body(g_ref, u_ref, d_ref, dg_ref, du_ref):
        g = g_ref[...]
        sg = _sigmoid(g)
        d = d_ref[...]
        dg_ref[...] = (d * u_ref[...] * (sg * (1.0 + g * (1.0 - sg)))).astype(BF16)
        du_ref[...] = (d * g * sg).astype(BF16)

    return pl.pallas_call(body, name="swiglu_bwd", grid=(NCHIP, S // (2 * TR)), in_specs=[_ff_spec()] * 3, out_specs=[_ff_spec()] * 2,
                          out_shape=[jax.ShapeDtypeStruct((NCHIP, S, FF_CHUNK), BF16)] * 2)(gate, up, dact)


def _final_fwd_bwd(h1, ffn, pg, pp, target, g2, b2):
    def body(h_ref, f_ref, pg_ref, pp_ref, t_ref, g_ref, b_ref, dpre_ref, dpreb_ref, dpg_ref, dpp_ref, dg_ref, db_ref, loss_ref):
        sg = _sigmoid(pg_ref[...])
        ppv = pp_ref[...]
        pre = ALPHA * h_ref[...] + f_ref[...] + sg * ppv
        mu = jnp.mean(pre, axis=-1, keepdims=True)
        d = pre - mu
        rs = lax.rsqrt(jnp.mean(d * d, axis=-1, keepdims=True) + EPS_LN)
        xh = d * rs
        err = xh * g_ref[...] + b_ref[...] - t_ref[...]
        dy = err * (1.0 / D)
        gy = dy * g_ref[...]
        dpre = rs * (gy - jnp.mean(gy, axis=-1, keepdims=True) - xh * jnp.mean(gy * xh, axis=-1, keepdims=True))
        dpre_ref[...] = dpre
        dpreb_ref[...] = dpre.astype(BF16)
        dpg_ref[...] = (dpre * ppv * sg * (1.0 - sg)).astype(BF16)
        dpp_ref[...] = (dpre * sg).astype(BF16)
        _acc_rows(dg_ref, jnp.sum(dy * xh, axis=0, keepdims=True))
        _acc_rows(db_ref, jnp.sum(dy, axis=0, keepdims=True))
        _acc_rows(loss_ref, 0.5 * jnp.sum(jnp.mean(err * err, axis=-1, keepdims=True), axis=0, keepdims=True) * jnp.ones((1, LANE), F32))

    return pl.pallas_call(
        body, name="final_ln_loss", grid=(S // TR,),
        in_specs=[_row_spec(D)] * 5 + [_vec_spec(D)] * 2,
        out_specs=[_row_spec(D)] * 4 + [_vec_spec(D), _vec_spec(D), _vec_spec(LANE)],
        out_shape=[jax.ShapeDtypeStruct((S, D), F32)] + [jax.ShapeDtypeStruct((S, D), BF16)] * 3 + [
                   jax.ShapeDtypeStruct((1, D), F32), jax.ShapeDtypeStruct((1, D), F32), jax.ShapeDtypeStruct((1, LANE), F32)],
    )(h1, ffn, pg, pp, target, g2, b2)


def _rot(u, cos_t, sin_t, lane):
    partner = jnp.where(lane < NOPE + ROPE // 2, pltpu.roll(u, LANE - ROPE // 2, 1), pltpu.roll(u, ROPE // 2, 1))
    return u * cos_t + partner * sin_t


def _q_rope(qlin, cos_t, sin_t):
    def body(q_ref, c_ref, s_ref, o_ref):
        lane = lax.broadcasted_iota(jnp.int32, (TR, LANE), 1)
        c, s = c_ref[...], s_ref[...]
        for h in range(H):
            o_ref[:, h * LANE:(h + 1) * LANE] = _rot(q_ref[:, h * LANE:(h + 1) * LANE], c, s, lane).astype(BF16)

    w = H * LANE
    return pl.pallas_call(body, name="q_rope", grid=(S // TR,), in_specs=[_row_spec(w), _row_spec(LANE), _row_spec(LANE)],
                          out_specs=_row_spec(w), out_shape=jax.ShapeDtypeStruct((S, w), BF16))(qlin, cos_t, sin_t)


def _q_unrope(dq, cos_t, sin_t):
    def body(q_ref, c_ref, s_ref, o_ref):
        lane = lax.broadcasted_iota(jnp.int32, (TR, LANE), 1)
        c, s = c_ref[...], -s_ref[...]
        for h in range(H):
            o_ref[:, h * LANE:(h + 1) * LANE] = _rot(q_ref[:, h * LANE:(h + 1) * LANE], c, s, lane).astype(BF16)

    w = H * LANE
    return pl.pallas_call(body, name="q_unrope", grid=(S // TR,), in_specs=[_row_spec(w), _row_spec(LANE), _row_spec(LANE)],
                          out_specs=_row_spec(w), out_shape=jax.ShapeDtypeStruct((S, w), BF16))(dq, cos_t, sin_t)


def _k_prep(klin, small, cos_t, sin_t):
    def body(k_ref, kr_ref, c_ref, s_ref, o_ref):
        lane = lax.broadcasted_iota(jnp.int32, (TR, LANE), 1)
        kr = _rot(pltpu.roll(kr_ref[...], NOPE, 1), c_ref[...], s_ref[...], lane)
        for h in range(H):
            o_ref[:, h * LANE:(h + 1) * LANE] = (k_ref[:, h * LANE:(h + 1) * LANE] + kr).astype(BF16)

    w = H * LANE
    kr_spec = pl.BlockSpec((TR, LANE), lambda i: (i, SM_KR // LANE))
    return pl.pallas_call(body, name="k_prep", grid=(S // TR,), in_specs=[_row_spec(w), kr_spec, _row_spec(LANE), _row_spec(LANE)],
                          out_specs=_row_spec(w), out_shape=jax.ShapeDtypeStruct((S, w), BF16))(klin, small, cos_t, sin_t)


def _k_rope_bwd(dk, cos_t, sin_t):
    def body(k_ref, c_ref, s_ref, o_ref):
        lane = lax.broadcasted_iota(jnp.int32, (TR, LANE), 1)
        acc = k_ref[:, 0:LANE]
        for h in range(1, H):
            acc = acc + k_ref[:, h * LANE:(h + 1) * LANE]
        acc = jnp.where((lane >= NOPE) & (lane < NOPE + ROPE), acc, 0.0)
        o_ref[...] = pltpu.roll(_rot(acc, c_ref[...], -s_ref[...], lane), LANE - NOPE, 1)

    w = H * LANE
    return pl.pallas_call(body, name="k_rope_bwd", grid=(S // TR,), in_specs=[_row_spec(w), _row_spec(LANE), _row_spec(LANE)],
                          out_specs=_row_spec(LANE), out_shape=jax.ShapeDtypeStruct((S, LANE), F32))(dk, cos_t, sin_t)


CB = 256


def _shift_down(u, k, row):
    if k == 0:
        return u
    return jnp.where(row >= k, pltpu.roll(u, k, 0), 0.0)


def _shift_up(u, k, row):
    if k == 0:
        return u
    return jnp.where(row < S - k, pltpu.roll(u, S - k, 0), 0.0)


def _conv_fwd(u, w, b):
    def body(u_ref, w_ref, b_ref, o_ref):
        row = lax.broadcasted_iota(jnp.int32, (S, CB), 0)
        uu = u_ref[...]
        acc = b_ref[...] + w_ref[SSD_K - 1:SSD_K, :] * uu
        for k in range(SSD_K - 1):
            acc = acc + w_ref[k:k + 1, :] * _shift_down(uu, SSD_K - 1 - k, row)
        o_ref[...] = acc * _sigmoid(acc)

    c = u.shape[1]
    return pl.pallas_call(
        body, name="conv_fwd", grid=(c // CB,),
        in_specs=[pl.BlockSpec((S, CB), lambda j: (0, j)), pl.BlockSpec((SSD_K, CB), lambda j: (0, j)), pl.BlockSpec((1, CB), lambda j: (0, j))],
        out_specs=pl.BlockSpec((S, CB), lambda j: (0, j)), out_shape=jax.ShapeDtypeStruct((S, c), F32),
    )(u, w, b)


def _conv_bwd(u, w, b, dact):
    def body(u_ref, w_ref, b_ref, d_ref, du_ref, dw_ref, db_ref):
        row = lax.broadcasted_iota(jnp.int32, (S, CB), 0)
        uu = u_ref[...]
        sh = [_shift_down(uu, SSD_K - 1 - k, row) for k in range(SSD_K)]
        acc = b_ref[...]
        for k in range(SSD_K):
            acc = acc + w_ref[k:k + 1, :] * sh[k]
        sg = _sigmoid(acc)
        dacc = d_ref[...] * (sg * (1.0 + acc * (1.0 - sg)))
        du = w_ref[SSD_K - 1:SSD_K, :] * dacc
        for k in range(SSD_K - 1):
            du = du + w_ref[k:k + 1, :] * _shift_up(dacc, SSD_K - 1 - k, row)
        du_ref[...] = du.astype(BF16)
        for k in range(SSD_K):
            dw_ref[k:k + 1, :] = jnp.sum(dacc * sh[k], axis=0, keepdims=True)
        db_ref[...] = jnp.sum(dacc, axis=0, keepdims=True)

    c = u.shape[1]
    col = lambda r: pl.BlockSpec((r, CB), lambda j: (0, j))
    return pl.pallas_call(
        body, name="conv_bwd", grid=(c // CB,), in_specs=[col(S), col(SSD_K), col(1), col(S)], out_specs=[col(S), col(SSD_K), col(1)],
        out_shape=[jax.ShapeDtypeStruct((S, c), BF16), jax.ShapeDtypeStruct((SSD_K, c), F32), jax.ShapeDtypeStruct((1, c), F32)],
    )(u, w, b, dact)


NPAIR = H // 2
PAIRS_PER_GROUP = NPAIR // SSD_G


def _softplus(v):
    return jnp.maximum(v, 0.0) + jnp.log(1.0 + jnp.exp(-jnp.abs(v)))


def _dot(a, b, dims):
    return lax.dot_general(a.astype(BF16), b.astype(BF16), (dims, ((), ())), preferred_element_type=F32)


def _dot3(a, b, dims, split_lhs):
    v = a if split_lhs else b
    v1 = v.astype(BF16)
    r1 = v - v1.astype(F32)
    v2 = r1.astype(BF16)
    v3 = (r1 - v2.astype(F32)).astype(BF16)
    acc = None
    for part in (v1, v2, v3):
        lhs, rhs = (part, b) if split_lhs else (a, part)
        t = lax.dot_general(lhs, rhs, (dims, ((), ())), preferred_element_type=F32)
        acc = t if acc is None else acc + t
    return acc


def _ssd_chunk_common(dt_ref, dtT_ref, prow_ref, pcol_ref):
    prow = prow_ref[...]
    pcol = pcol_ref[...]
    ri = lax.broadcasted_iota(jnp.int32, (SSD_L, SSD_L), 0)
    ci = lax.broadcasted_iota(jnp.int32, (SSD_L, SSD_L), 1)
    causal = ri >= ci
    pre_c = dt_ref[...] + prow[0:1, :]
    dtc = _softplus(pre_c)
    a_row = -jnp.exp(prow[1:2, :])
    cs_col = _dot3(causal.astype(BF16), dtc * a_row, ((1,), (0,)), False)
    dtr = _softplus(dtT_ref[...] + pcol[:, 0:1])
    a_col = -jnp.exp(pcol[:, 1:2])
    cs_row = _dot3(dtr * a_col, (ri <= ci).astype(BF16), ((1,), (0,)), True)
    return prow, causal, pre_c, dtc, a_row, cs_col, cs_row


def _ssd_fwd(act, small, dtT, prow, pcol):
    def body(x_ref, b_ref, c_ref, dt_ref, dtT_ref, prow_ref, pcol_ref, y_ref, st_ref, state):
        @pl.when(pl.program_id(0) == 0)
        def _():
            state[...] = jnp.zeros_like(state)

        prow, causal, _, dtc, _, cs_col, cs_row = _ssd_chunk_common(dt_ref, dtT_ref, prow_ref, pcol_ref)
        lo = lax.broadcasted_iota(jnp.int32, (SSD_L, LANE), 1) < SSD_P
        lo1 = lo[0:1, :]
        for g in range(SSD_G):
            bm = b_ref[:, g * SSD_N:(g + 1) * SSD_N]
            cm = c_ref[:, g * SSD_N:(g + 1) * SSD_N]
            cb = _dot(cm, bm, ((1,), (1,)))
            for qq in range(PAIRS_PER_GROUP):
                q = g * PAIRS_PER_GROUP + qq
                ha, hb = 2 * q, 2 * q + 1
                csa, csb = cs_col[:, ha:ha + 1], cs_col[:, hb:hb + 1]
                xp = x_ref[:, q * LANE:(q + 1) * LANE]
                xx = xp * jnp.where(lo, dtc[:, ha:ha + 1], dtc[:, hb:hb + 1])
                ga = cb * jnp.exp(jnp.where(causal, csa - cs_row[ha:ha + 1, :], NEG))
                gb = cb * jnp.exp(jnp.where(causal, csb - cs_row[hb:hb + 1, :], NEG))
                y = _dot(ga, jnp.where(lo, xx, 0.0), ((1,), (0,))) + _dot(gb, jnp.where(lo, 0.0, xx), ((1,), (0,)))
                s_in = state[q]
                y = y + _dot(cm, s_in, ((1,), (0,))) * jnp.where(lo, jnp.exp(csa), jnp.exp(csb))
                y = y + jnp.where(lo1, prow[2:3, ha:ha + 1], prow[2:3, hb:hb + 1]) * xp
                y_ref[:, q * LANE:(q + 1) * LANE] = y
                la, lb = csa[SSD_L - 1:SSD_L, :], csb[SSD_L - 1:SSD_L, :]
                decay = jnp.where(lo, jnp.exp(la - csa), jnp.exp(lb - csb))
                st_ref[q] = s_in
                state[q] = s_in * jnp.where(lo1, jnp.exp(la), jnp.exp(lb)) + _dot(bm, xx * decay, ((0,), (0,)))

    L = SSD_L
    return pl.pallas_call(
        body, name="ssd_fwd", grid=(SSD_NC,),
        in_specs=[pl.BlockSpec((L, SSD_INNER), lambda c: (c, 0)),
                  pl.BlockSpec((L, SSD_G * SSD_N), lambda c: (c, SSD_INNER // (SSD_G * SSD_N))),
                  pl.BlockSpec((L, SSD_G * SSD_N), lambda c: (c, SSD_INNER // (SSD_G * SSD_N) + 1)),
                  pl.BlockSpec((L, LANE), lambda c: (c, SM_DT // LANE)),
                  pl.BlockSpec((LANE, L), lambda c: (0, c)),
                  pl.BlockSpec((8, LANE), lambda c: (0, 0)), pl.BlockSpec((LANE, 8), lambda c: (0, 0))],
        out_specs=[pl.BlockSpec((L, SSD_INNER), lambda c: (c, 0)),
                   pl.BlockSpec((None, NPAIR, SSD_N, LANE), lambda c: (c, 0, 0, 0))],
        out_shape=[jax.ShapeDtypeStruct((S, SSD_INNER), F32), jax.ShapeDtypeStruct((SSD_NC, NPAIR, SSD_N, LANE), F32)],
        scratch_shapes=[pltpu.VMEM((NPAIR, SSD_N, LANE), F32)],
        compiler_params=pltpu.CompilerParams(dimension_semantics=("arbitrary",)),
    )(act, act, act, small, dtT, prow, pcol)


def _ssd_bwd(act, small, dtT, prow, pcol, states, dy):
    def body(x_ref, b_ref, c_ref, dt_ref, dtT_ref, prow_ref, pcol_ref, st_ref, dy_ref,
             dx_ref, ddt_ref, dp_ref, dstate):
        @pl.when(pl.program_id(0) == 0)
        def _():
            dstate[...] = jnp.zeros_like(dstate)
            dp_ref[...] = jnp.zeros_like(dp_ref)

        prow, causal, pre_c, dtc, a_row, cs_col, cs_row = _ssd_chunk_common(dt_ref, dtT_ref, prow_ref, pcol_ref)
        lane = lax.broadcasted_iota(jnp.int32, (SSD_L, LANE), 1)
        sub = lax.broadcasted_iota(jnp.int32, (LANE, SSD_L), 0)
        rowi = lax.broadcasted_iota(jnp.int32, (SSD_L, 1), 0)
        lane1 = lane[0:1, :]
        lo = lane < SSD_P
        lo1 = lo[0:1, :]
        dcs_c = jnp.zeros((SSD_L, LANE), F32)
        dcs_r = jnp.zeros((LANE, SSD_L), F32)
        ddt_x = jnp.zeros((SSD_L, LANE), F32)
        dd_row = jnp.zeros((1, LANE), F32)
        for g in range(SSD_G):
            bm = b_ref[:, g * SSD_N:(g + 1) * SSD_N]
            cm = c_ref[:, g * SSD_N:(g + 1) * SSD_N]
            cb = _dot(cm, bm, ((1,), (1,)))
            dcb = jnp.zeros((SSD_L, SSD_L), F32)
            dbm = jnp.zeros((SSD_L, SSD_N), F32)
            dcm = jnp.zeros((SSD_L, SSD_N), F32)
            for qq in range(PAIRS_PER_GROUP):
                q = g * PAIRS_PER_GROUP + qq
                ha, hb = 2 * q, 2 * q + 1
                csa, csb = cs_col[:, ha:ha + 1], cs_col[:, hb:hb + 1]
                xp = x_ref[:, q * LANE:(q + 1) * LANE]
                dtp = jnp.where(lo, dtc[:, ha:ha + 1], dtc[:, hb:hb + 1])
                xx = xp * dtp
                lma = jnp.exp(jnp.where(causal, csa - cs_row[ha:ha + 1, :], NEG))
                lmb = jnp.exp(jnp.where(causal, csb - cs_row[hb:hb + 1, :], NEG))
                ga, gb = cb * lma, cb * lmb
                dyp = dy_ref[:, q * LANE:(q + 1) * LANE]
                dya, dyb = jnp.where(lo, dyp, 0.0), jnp.where(lo, 0.0, dyp)
                s_in = st_ref[q]
                ds_out = dstate[q]
                la, lb = csa[SSD_L - 1:SSD_L, :], csb[SSD_L - 1:SSD_L, :]
                ecs = jnp.where(lo, jnp.exp(csa), jnp.exp(csb))
                decay = jnp.where(lo, jnp.exp(la - csa), jnp.exp(lb - csb))
                cd = jnp.where(lo1, jnp.exp(la), jnp.exp(lb))
                bds = _dot(bm, ds_out, ((1,), (0,)))
                dxx = _dot(ga, dya, ((0,), (0,))) + _dot(gb, dyb, ((0,), (0,))) + bds * decay
                dga = _dot(dya, xx, ((1,), (1,)))
                dgb = _dot(dyb, xx, ((1,), (1,)))
                dsega, dsegb = dga * ga, dgb * gb
                dcb = dcb + dga * lma + dgb * lmb
                yoff = _dot(cm, s_in, ((1,), (0,))) * ecs
                dye = dyp * ecs
                dcm = dcm + _dot(dye, s_in, ((1,), (1,)))
                xd = xx * decay
                dbm = dbm + _dot(xd, ds_out, ((1,), (1,)))
                wv = xd * bds
                t1 = dyp * yoff - wv
                col_a = (jnp.sum(dsega, axis=1, keepdims=True) + jnp.sum(jnp.where(lo, t1, 0.0), axis=1, keepdims=True))
                col_b = (jnp.sum(dsegb, axis=1, keepdims=True) + jnp.sum(jnp.where(lo, 0.0, t1), axis=1, keepdims=True))
                sprod = ds_out * s_in
                end_a = jnp.sum(jnp.where(lo, wv, 0.0), keepdims=True) + jnp.exp(la) * jnp.sum(jnp.where(lo[:SSD_N], sprod, 0.0), keepdims=True)
                end_b = jnp.sum(jnp.where(lo, 0.0, wv), keepdims=True) + jnp.exp(lb) * jnp.sum(jnp.where(lo[:SSD_N], 0.0, sprod), keepdims=True)
                col_a = col_a + jnp.where(rowi == SSD_L - 1, end_a, 0.0)
                col_b = col_b + jnp.where(rowi == SSD_L - 1, end_b, 0.0)
                dcs_c = dcs_c + jnp.where(lane == ha, col_a, 0.0) + jnp.where(lane == hb, col_b, 0.0)
                dcs_r = (dcs_r + jnp.where(sub == ha, jnp.sum(dsega, axis=0, keepdims=True), 0.0)
                         + jnp.where(sub == hb, jnp.sum(dsegb, axis=0, keepdims=True), 0.0))
                dstate[q] = _dot(cm, dye, ((0,), (0,))) + cd * ds_out
                dpair = jnp.where(lo1, prow[2:3, ha:ha + 1], prow[2:3, hb:hb + 1])
                dx_ref[:, q * LANE:(q + 1) * LANE] = dxx * dtp + dpair * dyp
                t2 = dxx * xp
                ddt_x = (ddt_x + jnp.where(lane == ha, jnp.sum(jnp.where(lo, t2, 0.0), axis=1, keepdims=True), 0.0)
                         + jnp.where(lane == hb, jnp.sum(jnp.where(lo, 0.0, t2), axis=1, keepdims=True), 0.0))
                t3 = dyp * xp
                dd_row = (dd_row + jnp.where(lane1 == ha, jnp.sum(jnp.where(lo, t3, 0.0), keepdims=True), 0.0)
                          + jnp.where(lane1 == hb, jnp.sum(jnp.where(lo, 0.0, t3), keepdims=True), 0.0))
            dx_ref[:, SSD_INNER + g * SSD_N:SSD_INNER + (g + 1) * SSD_N] = dbm + _dot(dcb, cm, ((0,), (0,)))
            dx_ref[:, SSD_INNER + (SSD_G + g) * SSD_N:SSD_INNER + (SSD_G + g + 1) * SSD_N] = dcm + _dot(dcb, bm, ((1,), (0,)))
        ri = lax.broadcasted_iota(jnp.int32, (SSD_L, SSD_L), 0)
        ci = lax.broadcasted_iota(jnp.int32, (SSD_L, SSD_L), 1)
        da = _dot3((ri <= ci).astype(BF16), dcs_c, ((1,), (0,)), False)
        da = da - _dot3(dcs_r, causal.astype(BF16), ((1,), (0,)), True).T
        ddt = ddt_x + da * a_row
        ddt_raw = ddt * _sigmoid(pre_c)
        ddt_ref[...] = ddt_raw
        da_head = jnp.sum(da * dtc, axis=0, keepdims=True) * a_row
        dp_ref[0:1, :] += jnp.sum(ddt_raw, axis=0, keepdims=True)
        dp_ref[1:2, :] += da_head
        dp_ref[2:3, :] += dd_row

    L = SSD_L
    rev = SSD_NC - 1
    bc_cols = SSD_INNER // (SSD_G * SSD_N)
    return pl.pallas_call(
        body, name="ssd_bwd", grid=(SSD_NC,),
        in_specs=[pl.BlockSpec((L, SSD_INNER), lambda c: (rev - c, 0)),
                  pl.BlockSpec((L, SSD_G * SSD_N), lambda c: (rev - c, bc_cols)),
                  pl.BlockSpec((L, SSD_G * SSD_N), lambda c: (rev - c, bc_cols + 1)),
                  pl.BlockSpec((L, LANE), lambda c: (rev - c, SM_DT // LANE)),
                  pl.BlockSpec((LANE, L), lambda c: (0, rev - c)),
                  pl.BlockSpec((8, LANE), lambda c: (0, 0)), pl.BlockSpec((LANE, 8), lambda c: (0, 0)),
                  pl.BlockSpec((None, NPAIR, SSD_N, LANE), lambda c: (rev - c, 0, 0, 0)),
                  pl.BlockSpec((L, SSD_INNER), lambda c: (rev - c, 0))],
        out_specs=[pl.BlockSpec((L, SSD_XBC), lambda c: (rev - c, 0)),
                   pl.BlockSpec((L, LANE), lambda c: (rev - c, 0)),
                   pl.BlockSpec((8, LANE), lambda c: (0, 0))],
        out_shape=[jax.ShapeDtypeStruct((S, SSD_XBC), F32), jax.ShapeDtypeStruct((S, LANE), F32),
                   jax.ShapeDtypeStruct((8, LANE), F32)],
        scratch_shapes=[pltpu.VMEM((NPAIR, SSD_N, LANE), F32)],
        compiler_params=pltpu.CompilerParams(dimension_semantics=("arbitrary",)),
    )(act, act, act, small, dtT, prow, pcol, states, dy)


TQ = 256
TK = 256


def _attn_fwd(qc, kc, v):
    def body(q_ref, k_ref, v_ref, o_ref, lse_ref):
        i = pl.program_id(1)
        lo = lax.broadcasted_iota(jnp.int32, (TQ, LANE), 1) < VDIM
        rq = i * TQ + lax.broadcasted_iota(jnp.int32, (TQ, TK), 0)
        ck0 = lax.broadcasted_iota(jnp.int32, (TQ, TK), 1)
        qa, qb = q_ref[:, 0:LANE], q_ref[:, LANE:2 * LANE]

        def step(kb, carry):
            ma, la, mb, lb, acc = carry
            off = pl.multiple_of(kb * TK, TK)
            kk = k_ref[pl.ds(off, TK), :]
            vv = v_ref[pl.ds(off, TK), :]
            mask = rq >= ck0 + kb * TK
            sa = jnp.where(mask, _dot(qa, kk[:, 0:LANE], ((1,), (1,))) * ATT_SCALE, NEG)
            sb = jnp.where(mask, _dot(qb, kk[:, LANE:2 * LANE], ((1,), (1,))) * ATT_SCALE, NEG)
            na = jnp.maximum(ma, jnp.max(sa, axis=1, keepdims=True))
            nb = jnp.maximum(mb, jnp.max(sb, axis=1, keepdims=True))
            pa, pb = jnp.exp(sa - na), jnp.exp(sb - nb)
            fa, fb = jnp.exp(ma - na), jnp.exp(mb - nb)
            la = fa * la + jnp.sum(pa, axis=1, keepdims=True)
            lb = fb * lb + jnp.sum(pb, axis=1, keepdims=True)
            acc = (acc * jnp.where(lo, fa, fb) + _dot(pa, jnp.where(lo, vv, 0), ((1,), (0,)))
                   + _dot(pb, jnp.where(lo, 0, vv), ((1,), (0,))))
            return na, la, nb, lb, acc

        neg = jnp.full((TQ, 1), NEG, F32)
        zero = jnp.zeros((TQ, 1), F32)
        ma, la, mb, lb, acc = lax.fori_loop(0, i + 1, step, (neg, zero, neg, zero, jnp.zeros((TQ, LANE), F32)))
        o_ref[...] = acc / jnp.where(lo, la, lb)
        lse_ref[...] = jnp.where(lo, ma + jnp.log(la), mb + jnp.log(lb))

    return pl.pallas_call(
        body, name="attn_fwd", grid=(NPAIR, S // TQ),
        in_specs=[pl.BlockSpec((TQ, 2 * LANE), lambda j, i: (i, j)), pl.BlockSpec((S, 2 * LANE), lambda j, i: (0, j)),
                  pl.BlockSpec((S, LANE), lambda j, i: (0, j))],
        out_specs=[pl.BlockSpec((TQ, LANE), lambda j, i: (i, j)), pl.BlockSpec((None, TQ, LANE), lambda j, i: (j, i, 0))],
        out_shape=[jax.ShapeDtypeStruct((S, H * VDIM), F32), jax.ShapeDtypeStruct((NPAIR, S, LANE), F32)],
        compiler_params=pltpu.CompilerParams(dimension_semantics=("parallel", "parallel")),
    )(qc, kc, v)


def _attn_bwd(qc, kc, v, o, lse, do):
    nq = S // TQ

    def body(q_ref, k_ref, v_ref, o_ref, lse_ref, do_ref, dq_ref, dk_ref, dv_ref):
        kb = pl.program_id(1)

        @pl.when(kb == 0)
        def _():
            dq_ref[...] = jnp.zeros_like(dq_ref)

        lo = lax.broadcasted_iota(jnp.int32, (TQ, LANE), 1) < VDIM
        r0 = lax.broadcasted_iota(jnp.int32, (TQ, TK), 0)
        ck = kb * TK + lax.broadcasted_iota(jnp.int32, (TQ, TK), 1)
        ka, kbb = k_ref[:, 0:LANE], k_ref[:, LANE:2 * LANE]
        vv = v_ref[...]

        def step(qi, carry):
            dka, dkb, dv = carry
            off = pl.multiple_of(qi * TQ, TQ)
            qq = q_ref[pl.ds(off, TQ), :]
            dd = do_ref[pl.ds(off, TQ), :]
            ls = lse_ref[pl.ds(off, TQ), :]
            t = dd * o_ref[pl.ds(off, TQ), :]
            mask = r0 + qi * TQ >= ck
            outs = []
            for x, (kx, lsx) in enumerate(((ka, ls[:, 0:1]), (kbb, ls[:, VDIM:VDIM + 1]))):
                sel = lo if x == 0 else jnp.logical_not(lo)
                qx = qq[:, x * LANE:(x + 1) * LANE]
                dox = jnp.where(sel, dd, 0.0)
                delta = jnp.sum(jnp.where(sel, t, 0.0), axis=1, keepdims=True)
                sc = jnp.where(mask, _dot(qx, kx, ((1,), (1,))) * ATT_SCALE, NEG)
                p = jnp.exp(sc - lsx)
                dp = _dot(dox, vv, ((1,), (1,)))
                ds = p * (dp - delta) * ATT_SCALE
                dv = dv + _dot(p, dox, ((0,), (0,)))
                outs.append(_dot(ds, qx, ((0,), (0,))))
                dq_ref[pl.ds(off, TQ), x * LANE:(x + 1) * LANE] += _dot(ds, kx, ((1,), (0,)))
            return dka + outs[0], dkb + outs[1], dv

        z = jnp.zeros((TK, LANE), F32)
        dka, dkb, dv = lax.fori_loop(kb, nq, step, (z, z, z))
        dk_ref[:, 0:LANE] = dka
        dk_ref[:, LANE:2 * LANE] = dkb
        dv_ref[...] = dv

    return pl.pallas_call(
        body, name="attn_bwd", grid=(NPAIR, S // TK),
        in_specs=[pl.BlockSpec((S, 2 * LANE), lambda j, k: (0, j)), pl.BlockSpec((TK, 2 * LANE), lambda j, k: (k, j)),
                  pl.BlockSpec((TK, LANE), lambda j, k: (k, j)), pl.BlockSpec((S, LANE), lambda j, k: (0, j)),
                  pl.BlockSpec((None, S, LANE), lambda j, k: (j, 0, 0)), pl.BlockSpec((S, LANE), lambda j, k: (0, j))],
        out_specs=[pl.BlockSpec((S, 2 * LANE), lambda j, k: (0, j)), pl.BlockSpec((TK, 2 * LANE), lambda j, k: (k, j)),
                   pl.BlockSpec((TK, LANE), lambda j, k: (k, j))],
        out_shape=[jax.ShapeDtypeStruct((S, H * LANE), F32), jax.ShapeDtypeStruct((S, H * LANE), F32),
                   jax.ShapeDtypeStruct((S, H * VDIM), F32)],
        compiler_params=pltpu.CompilerParams(dimension_semantics=("parallel", "arbitrary")),
    )(qc, kc, v, o, lse, do)


_IN_Z, _IN_XBC, _IN_DT, _IN_Q, _IN_KV, _IN_KR = 0, 1024, 2560, 2576, 2960, 3216


def _prep_weights(w_in, w_qb, w_kvb):
    dt = w_in.dtype
    w_small = jnp.concatenate(
        [w_in[:, _IN_Q:_IN_KV], w_in[:, _IN_KV:_IN_KR], w_in[:, _IN_KR:IN_WIDTH], jnp.zeros((D, LANE - ROPE), dt),
         w_in[:, _IN_DT:_IN_Q], jnp.zeros((D, LANE - H), dt)], axis=1)
    w_q = jnp.pad(w_qb.reshape(Q_RANK, H, NOPE + ROPE), ((0, 0), (0, 0), (0, LANE - NOPE - ROPE))).reshape(Q_RANK, H * LANE)
    kv3 = w_kvb.reshape(KV_RANK, H, NOPE + VDIM)
    w_k = jnp.pad(kv3[:, :, :NOPE], ((0, 0), (0, 0), (0, LANE - NOPE))).reshape(KV_RANK, H * LANE)
    w_v = kv3[:, :, NOPE:].reshape(KV_RANK, H * VDIM)
    return w_in[:, _IN_Z:_IN_XBC], w_in[:, _IN_XBC:_IN_DT], w_small, w_q, w_k, w_v


def _rope_tables(positions):
    inv_freq = 1.0 / (10000.0 ** (jnp.arange(0, ROPE, 2, dtype=F32) / ROPE))
    ang = positions.astype(F32).reshape(S, 1) * inv_freq
    cos, sin = jnp.cos(ang), jnp.sin(ang)
    cos_t = jnp.concatenate([jnp.ones((S, NOPE), F32), cos, cos, jnp.ones((S, LANE - NOPE - ROPE), F32)], axis=1)
    sin_t = jnp.concatenate([jnp.zeros((S, NOPE), F32), -sin, sin, jnp.zeros((S, LANE - NOPE - ROPE), F32)], axis=1)
    return cos_t, sin_t


def _local_step(x, p, positions, target, gw, sp):
    w_z, w_xbc, w_small, w_q, w_k, w_v = _prep_weights(_from_cols(gw["w_in"]), _from_cols(gw["w_qb"]), _from_cols(gw["w_kvb"]))
    w_out_s = gw["w_out"][:NCHIP // 2].reshape(SSD_INNER, D)
    w_out_m = gw["w_out"][NCHIP // 2:].reshape(SSD_INNER, D)
    w_pg, w_pp = gw["w_pg"].reshape(D, D), _from_cols(gw["w_pp"])
    w_gate, w_up, w_down = gw["w_gate"], gw["w_up"], gw["w_down"]
    cos_t, sin_t = _rope_tables(positions)
    prow = jnp.zeros((8, LANE), F32).at[0, :H].set(sp["dt_bias"][0]).at[1, :H].set(sp["A_log"][0]).at[2, :H].set(sp["D"][0])
    pcol = prow.T

    xb, pb = x.astype(BF16), p.astype(BF16)
    z = _mm([(xb, w_z)], name="proj_z")
    xbc = _mm([(xb, w_xbc)], name="proj_xbc")
    small = _mm([(xb, w_small)], name="proj_small")
    act = _conv_fwd(xbc, sp["conv_w"], sp["conv_b"])
    dt_t = small[:, SM_DT:SM_DT + LANE].T
    y, states = _ssd_fwd(act, small, dt_t, prow, pcol)
    y_ssd = _gate_norm_fwd(y, z, sp["ssd_norm"])
    q_c, kv_c = small[:, SM_Q:SM_Q + Q_RANK], small[:, SM_KV:SM_KV + KV_RANK]
    qn = _rms_fwd(q_c, sp["q_norm"], name="q_norm_fwd")
    kvn = _rms_fwd(kv_c, sp["kv_norm"], name="kv_norm_fwd")
    qcat = _q_rope(_mm([(qn, w_q)], name="q_up"), cos_t, sin_t)
    kcat = _k_prep(_mm([(kvn, w_k)], name="k_up"), small, cos_t, sin_t)
    v = _mm([(kvn, w_v)], out_dtype=BF16, name="v_up")
    o, lse = _attn_fwd(qcat, kcat, v)
    y_mla = _rms_fwd(o, sp["out_norm"], name="out_norm_fwd")
    mix = _mm([(y_ssd, w_out_s), (y_mla, w_out_m)], name="out_proj")
    h1, h1b = _ln_fwd(x, mix, sp["ln_mix_g"], sp["ln_mix_b"])
    gate = _mm([(h1b, w_gate)], chunk="out", name="ffn_gate")
    up = _mm([(h1b, w_up)], chunk="out", name="ffn_up")
    actf = _swiglu_fwd(gate, up)
    ffn = _mm([(actf, w_down)], chunk="sum", name="ffn_down")
    pg = _mm([(h1b, w_pg)], name="ple_gate")
    pp = _mm([(pb, w_pp)], name="ple_proj")
    dpre2, dpre2b, dpg, dpp, dg2, db2, loss_row = _final_fwd_bwd(h1, ffn, pg, pp, target, sp["ln_ffn_g"], sp["ln_ffn_b"])

    g = {"ln_ffn_g": dg2, "ln_ffn_b": db2}
    g["w_pp"] = _to_cols(_mm([(pb, dpp)], ta=True, out_dtype=BF16, name="d_w_ple_proj"))
    g["w_pg"] = _mm([(h1b, dpg)], ta=True, out_dtype=BF16, name="d_w_ple_gate").reshape(NCHIP, D // NCHIP, D)
    g["w_down"] = _mm([(actf, dpre2b)], ta=True, chunk="out", out_dtype=BF16, name="d_w_down")
    dactf = _mm([(dpre2b, w_down)], tb=True, chunk="out", name="d_act")
    dgate, dup = _swiglu_bwd(gate, up, dactf)
    g["w_gate"] = _mm([(h1b, dgate)], ta=True, chunk="out", out_dtype=BF16, name="d_w_gate")
    g["w_up"] = _mm([(h1b, dup)], ta=True, chunk="out", out_dtype=BF16, name="d_w_up")
    dh1 = _mm([(dpg, w_pg)], tb=True, add=dpre2, add_scale=ALPHA, name="d_h1_ple")
    dh1 = _mm([(dgate, w_gate), (dup, w_up)], tb=True, chunk="sum", add=dh1, name="d_h1")
    dpre1, dpre1b, g["ln_mix_g"], g["ln_mix_b"] = _ln_bwd(x, mix, sp["ln_mix_g"], dh1)
    dy_ssd = _mm([(dpre1b, w_out_s)], tb=True, name="d_y_ssd")
    dy_mla = _mm([(dpre1b, w_out_m)], tb=True, name="d_y_mla")
    g["w_out"] = jnp.concatenate([_mm([(y_ssd, dpre1b)], ta=True, out_dtype=BF16, name="d_w_out_s"),
                                  _mm([(y_mla, dpre1b)], ta=True, out_dtype=BF16, name="d_w_out_m")],
                                 axis=0).reshape(NCHIP, 2 * SSD_INNER // NCHIP, D)
    do, g["out_norm"] = _rms_bwd(o, sp["out_norm"], dy_mla, name="out_norm_bwd")
    dq, dk, dv = _attn_bwd(qcat, kcat, v, o, lse, do)
    dqlin = _q_unrope(dq, cos_t, sin_t)
    dw_q = _mm([(qn, dqlin)], ta=True, out_dtype=BF16, name="d_w_q")
    dqn = _mm([(dqlin, w_q)], tb=True, name="d_qn")
    dq_c, g["q_norm"] = _rms_bwd(q_c, sp["q_norm"], dqn, name="q_norm_bwd")
    dkr = _k_rope_bwd(dk, cos_t, sin_t)
    dw_k = _mm([(kvn, dk)], ta=True, out_dtype=BF16, name="d_w_k")
    dw_v = _mm([(kvn, dv)], ta=True, out_dtype=BF16, name="d_w_v")
    dkvn = _mm([(dk, w_k), (dv, w_v)], tb=True, name="d_kvn")
    dkv_c, g["kv_norm"] = _rms_bwd(kv_c, sp["kv_norm"], dkvn, name="kv_norm_bwd")
    g["w_qb"] = _to_cols(dw_q.reshape(Q_RANK, H, LANE)[:, :, :NOPE + ROPE].reshape(Q_RANK, H * (NOPE + ROPE)))
    g["w_kvb"] = _to_cols(jnp.concatenate([dw_k.reshape(KV_RANK, H, LANE)[:, :, :NOPE], dw_v.reshape(KV_RANK, H, VDIM)],
                                          axis=2).reshape(KV_RANK, H * (NOPE + VDIM)))
    dy, dz, g["ssd_norm"] = _gate_norm_bwd(y, z, sp["ssd_norm"], dy_ssd)
    dact, ddt, dprow = _ssd_bwd(act, small, dt_t, prow, pcol, states, dy)
    g["dt_bias"], g["A_log"], g["D"] = dprow[0:1, :H], dprow[1:2, :H], dprow[2:3, :H]
    dxbc, g["conv_w"], g["conv_b"] = _conv_bwd(xbc, sp["conv_w"], sp["conv_b"], dact)
    dsmall = jnp.concatenate([dq_c, dkv_c, dkr, ddt], axis=1).astype(BF16)
    grad_x = _mm([(dz, w_z), (dxbc, w_xbc), (dsmall, w_small)], tb=True, add=dpre1, add_scale=ALPHA, name="d_x")
    dw_small = _mm([(xb, dsmall)], ta=True, out_dtype=BF16, name="d_w_small")
    g["w_in"] = _to_cols(jnp.concatenate(
        [_mm([(xb, dz)], ta=True, out_dtype=BF16, name="d_w_z"), _mm([(xb, dxbc)], ta=True, out_dtype=BF16, name="d_w_xbc"),
         dw_small[:, SM_DT:SM_DT + H], dw_small[:, SM_Q:SM_Q + Q_RANK], dw_small[:, SM_KV:SM_KV + KV_RANK],
         dw_small[:, SM_KR:SM_KR + ROPE]], axis=1))
    return loss_row, grad_x, g


MESH = pl.DeviceIdType.MESH
BIG = (("w_in", (D, IN_WIDTH), 1), ("w_qb", (Q_RANK, H * (NOPE + ROPE)), 1), ("w_kvb", (KV_RANK, H * (NOPE + VDIM)), 1),
       ("w_out", (2 * SSD_INNER, D), 0), ("w_gate", (D, D_FF), 1), ("w_up", (D, D_FF), 1), ("w_down", (D_FF, D), 0),
       ("w_pg", (D, D), 0), ("w_pp", (PLE, D), 1))
CONV_SHARD = SSD_XBC // NCHIP
BF16_ROWS = 16


def _from_cols(stack):
    return jnp.concatenate([stack[k] for k in range(NCHIP)], axis=1)


def _to_cols(full):
    r, c4 = full.shape
    return full.reshape(r, NCHIP, c4 // NCHIP).transpose(1, 0, 2)


def _coords():
    return lax.axis_index("x"), lax.axis_index("y"), lax.axis_index("c")


def _peers():
    x, y, c = _coords()
    return 2 * x + y, c, [(1 - x, y), (x, 1 - y), (1 - x, 1 - y)], (x, y, 1 - c)


def _half(c, rows):
    return pl.ds(pl.multiple_of(c * (rows // 2), BF16_ROWS), rows // 2)


def _gather_weights(shards):
    n_arr = len(shards)
    split = [s.shape[0] % (2 * BF16_ROWS) == 0 for s in shards]
    per = 2 * (NCHIP - 1)

    def body(*refs):
        ins, outs = refs[:n_arr], refs[n_arr:2 * n_arr]
        send_sems, recv_sems, local_sems = refs[2 * n_arr:]
        k, c, chips, sibling = _peers()

        def copy(idx, src, dst, to):
            return pltpu.make_async_remote_copy(src_ref=src, dst_ref=dst, send_sem=send_sems.at[idx], recv_sem=recv_sems.at[idx],
                                                device_id=to, device_id_type=MESH)

        def part(a, chip, core):
            return outs[a].at[chip, _half(core, shards[a].shape[0])] if split[a] else outs[a].at[chip]

        mine = [pltpu.make_async_copy(ins[a], outs[a].at[k], local_sems.at[a]) for a in range(n_arr)]
        for cp in mine:
            cp.start()
        sends = []
        for a in range(n_arr):
            src = ins[a].at[_half(c, shards[a].shape[0])] if split[a] else ins[a]
            for j, (cx, cy) in enumerate(chips):
                sends.append(copy(per * a + j, src, part(a, k, c), (cx, cy, c)))
                sends[-1].start()
        for j, (cx, cy) in enumerate(chips):
            for a in range(n_arr):
                landed = part(a, 2 * cx + cy, c)
                copy(per * a + j, landed, landed, (cx, cy, c)).wait_recv()
                if split[a]:
                    sends.append(copy(per * a + NCHIP - 1 + j, landed, landed, sibling))
                    sends[-1].start()
        for j, (cx, cy) in enumerate(chips):
            for a in range(n_arr):
                if split[a]:
                    other = part(a, 2 * cx + cy, 1 - c)
                    copy(per * a + NCHIP - 1 + j, other, other, sibling).wait_recv()
        for cp in sends:
            cp.wait_send()
        for cp in mine:
            cp.wait()

    any_spec = pl.BlockSpec(memory_space=pl.ANY)
    return pl.pallas_call(
        body, name="gather_weights", in_specs=[any_spec] * n_arr, out_specs=[any_spec] * n_arr,
        out_shape=[jax.ShapeDtypeStruct((NCHIP,) + s.shape, s.dtype) for s in shards],
        scratch_shapes=[pltpu.SemaphoreType.DMA((per * n_arr,)), pltpu.SemaphoreType.DMA((per * n_arr,)),
                        pltpu.SemaphoreType.DMA((n_arr,))],
    )(*shards)


def _reduce_grads(stacks):
    n_arr = len(stacks)
    dims = [s.shape[1:] for s in stacks]
    per = NCHIP + 1

    def body(*refs):
        ins, fin, r1, part, r2 = (refs[i * n_arr:(i + 1) * n_arr] for i in range(5))
        send_sems, recv_sems, local_sems = refs[5 * n_arr:]
        k, c, chips, sibling = _peers()

        def copy(idx, src, dst, to):
            return pltpu.make_async_remote_copy(src_ref=src, dst_ref=dst, send_sem=send_sems.at[idx], recv_sem=recv_sems.at[idx],
                                                device_id=to, device_id_type=MESH)

        pairs = [copy(per * a, ins[a].at[:, _half(1 - c, dims[a][0])], r1[a], sibling) for a in range(n_arr)]
        for cp in pairs:
            cp.start()
        sends, own = [], []
        for a in range(n_arr):
            hr, cols = dims[a][0] // 2, dims[a][1]
            pairs[a].wait_recv()

            def pair_sum(va, vb, vo, a=a):
                for kk in range(NCHIP):
                    pltpu.sync_copy(ins[a].at[kk, _half(c, dims[a][0])], va)
                    pltpu.sync_copy(r1[a].at[kk], vb)
                    vo[...] = (va[...].astype(F32) + vb[...].astype(F32)).astype(BF16)
                    pltpu.sync_copy(vo, part[a].at[kk])

            pl.run_scoped(pair_sum, *[pltpu.VMEM((hr, cols), BF16)] * 3)
            for j, (cx, cy) in enumerate(chips):
                sends.append(copy(per * a + 1 + j, part[a].at[2 * cx + cy], r2[a].at[k], (cx, cy, c)))
                sends[-1].start()
            own.append(pltpu.make_async_copy(part[a].at[k], r2[a].at[k], local_sems.at[a]))
            own[-1].start()
        for a in range(n_arr):
            hr, cols = dims[a][0] // 2, dims[a][1]
            mine = fin[a].at[_half(c, dims[a][0])]
            own[a].wait()
            for j, (cx, cy) in enumerate(chips):
                landed = r2[a].at[2 * cx + cy]
                copy(per * a + 1 + j, landed, landed, (cx, cy, c)).wait_recv()

            def chip_sum(vs, vf, a=a, mine=mine):
                pltpu.sync_copy(r2[a], vs)
                acc = vs[0].astype(F32)
                for kk in range(1, NCHIP):
                    acc = acc + vs[kk].astype(F32)
                vf[...] = acc
                pltpu.sync_copy(vf, mine)

            pl.run_scoped(chip_sum, pltpu.VMEM((NCHIP, hr, cols), BF16), pltpu.VMEM((hr, cols), F32))
            sends.append(copy(per * a + NCHIP, mine, mine, sibling))
            sends[-1].start()
        for a in range(n_arr):
            other = fin[a].at[_half(1 - c, dims[a][0])]
            copy(per * a + NCHIP, other, other, sibling).wait_recv()
        for cp in pairs + sends:
            cp.wait_send()

    any_spec = pl.BlockSpec(memory_space=pl.ANY)
    stage = [jax.ShapeDtypeStruct((NCHIP, r // 2, cols), BF16) for r, cols in dims]
    return pl.pallas_call(
        body, name="reduce_grads", in_specs=[any_spec] * n_arr, out_specs=[any_spec] * (4 * n_arr),
        out_shape=[jax.ShapeDtypeStruct(d, F32) for d in dims] + stage * 3,
        scratch_shapes=[pltpu.SemaphoreType.DMA((per * n_arr,)), pltpu.SemaphoreType.DMA((per * n_arr,)),
                        pltpu.SemaphoreType.DMA((n_arr,))],
    )(*stacks)[:n_arr]


SMALL = (("conv_w", SSD_K * SSD_XBC), ("conv_b", SSD_XBC), ("dt_bias", H), ("A_log", H), ("D", H), ("ssd_norm", SSD_INNER),
         ("q_norm", Q_RANK), ("kv_norm", KV_RANK), ("out_norm", SSD_INNER), ("ln_mix_g", D), ("ln_mix_b", D),
         ("ln_ffn_g", D), ("ln_ffn_b", D))
SMALL_ROWS = 120
NDEV = 8


def _allreduce_small(sv):
    def body(sv_ref, out_ref, slots, send_sems, recv_sems):
        x, y, c = _coords()
        me = 4 * x + 2 * y + c
        slots[me] = sv_ref[...]
        copies = []
        for d in range(1, NDEV):
            to = (x ^ (d >> 2), y ^ ((d >> 1) & 1), c ^ (d & 1))
            copies.append(pltpu.make_async_remote_copy(src_ref=sv_ref, dst_ref=slots.at[me], send_sem=send_sems.at[d - 1],
                                                       recv_sem=recv_sems.at[d - 1], device_id=to, device_id_type=MESH))
            copies[-1].start()
        for cp in copies:
            cp.wait_recv()
        for cp in copies:
            cp.wait_send()
        acc = slots[0]
        for i in range(1, NDEV):
            acc = acc + slots[i]
        out_ref[...] = acc

    vm = pl.BlockSpec(memory_space=pltpu.VMEM)
    return pl.pallas_call(
        body, name="allreduce_small", in_specs=[vm], out_specs=vm, out_shape=jax.ShapeDtypeStruct((SMALL_ROWS, LANE), F32),
        scratch_shapes=[pltpu.VMEM((NDEV, SMALL_ROWS, LANE), F32), pltpu.SemaphoreType.DMA((NDEV - 1,)),
                        pltpu.SemaphoreType.DMA((NDEV - 1,))],
    )(sv)


def _adamw_math(w, g, m, v):
    m2 = ADAM_B1 * m + (1.0 - ADAM_B1) * g
    v2 = ADAM_B2 * v + (1.0 - ADAM_B2) * (g * g)
    m_hat = m2 / (1.0 - ADAM_B1 ** ADAM_STEP)
    v_hat = v2 / (1.0 - ADAM_B2 ** ADAM_STEP)
    return -ADAM_LR * (m_hat / (jnp.sqrt(v_hat) + ADAM_EPS) + ADAM_WD * w), m2, v2


def _adamw_big(w, g, m, v, *, name):
    r, c = w.shape
    tr = next(t for t in (512, 384, 352, 256, 128, 64, 8) if r % t == 0)

    def body(w_ref, g_ref, m_ref, v_ref, d_ref, m2_ref, v2_ref):
        d_ref[...], m2_ref[...], v2_ref[...] = _adamw_math(w_ref[...], g_ref[...], m_ref[...], v_ref[...])

    spec = pl.BlockSpec((tr, c), lambda i: (i, 0))
    return pl.pallas_call(body, name=name, grid=(r // tr,), in_specs=[spec] * 4, out_specs=[spec] * 3,
                          out_shape=[jax.ShapeDtypeStruct((r, c), F32)] * 3)(w, g, m, v)


def _adamw_small(ws, gs, ms, vs):
    n = len(ws)

    def body(*refs):
        for i in range(n):
            w_ref, g_ref, m_ref, v_ref = (refs[j * n + i] for j in range(4))
            d_ref, m2_ref, v2_ref = (refs[(4 + j) * n + i] for j in range(3))
            d_ref[...], m2_ref[...], v2_ref[...] = _adamw_math(w_ref[...], g_ref[...], m_ref[...], v_ref[...])

    vm = pl.BlockSpec(memory_space=pltpu.VMEM)
    shapes = [jax.ShapeDtypeStruct(w.shape, F32) for w in ws]
    outs = pl.pallas_call(body, name="adamw_small", in_specs=[vm] * (4 * n), out_specs=[vm] * (3 * n), out_shape=shapes * 3)(
        *ws, *gs, *ms, *vs)
    return outs[:n], outs[n:2 * n], outs[2 * n:]


_SMALL_ARG = {"conv_w": "ssd_conv_w", "conv_b": "ssd_conv_b", "dt_bias": "ssd_dt_bias", "A_log": "ssd_A_log", "D": "ssd_D",
              "ssd_norm": "ssd_norm_w", "q_norm": "mla_q_norm_w", "kv_norm": "mla_kv_norm_w", "out_norm": "mla_out_norm_w",
              "ln_mix_g": "ln_mix_g", "ln_mix_b": "ln_mix_b", "ln_ffn_g": "ln_ffn_g", "ln_ffn_b": "ln_ffn_b"}
_BIG_ARG = {"w_in": "w_in", "w_qb": "mla_w_q_b", "w_kvb": "mla_w_kv_b", "w_out": "w_out", "w_gate": "w_ffn_gate",
            "w_up": "w_ffn_up", "w_down": "w_ffn_down", "w_pg": "w_ple_gate", "w_pp": "w_ple_proj"}
_WEIGHT_ORDER = ("w_in", "ssd_conv_w", "ssd_conv_b", "ssd_dt_bias", "ssd_A_log", "ssd_D", "ssd_norm_w", "mla_q_norm_w", "mla_w_q_b",
                 "mla_kv_norm_w", "mla_w_kv_b", "mla_out_norm_w", "w_out", "ln_mix_g", "ln_mix_b", "w_ffn_gate", "w_ffn_up",
                 "w_ffn_down", "w_ple_gate", "w_ple_proj", "ln_ffn_g", "ln_ffn_b")


def _rows128(a):
    flat = a.reshape(-1)
    return jnp.pad(flat, (0, -flat.shape[0] % LANE)).reshape(-1, LANE)


def kernel(x, p, positions, w_in, ssd_conv_w, ssd_conv_b, ssd_dt_bias, ssd_A_log, ssd_D, ssd_norm_w, mla_q_norm_w, mla_w_q_b, mla_kv_norm_w, mla_w_kv_b, mla_out_norm_w, w_out, ln_mix_g, ln_mix_b, w_ffn_gate, w_ffn_up, w_ffn_down, w_ple_gate, w_ple_proj, ln_ffn_g, ln_ffn_b, loss_target, m_w_in, m_ssd_conv_w, m_ssd_conv_b, m_ssd_dt_bias, m_ssd_A_log, m_ssd_D, m_ssd_norm_w, m_mla_q_norm_w, m_mla_w_q_b, m_mla_kv_norm_w, m_mla_w_kv_b, m_mla_out_norm_w, m_w_out, m_ln_mix_g, m_ln_mix_b, m_w_ffn_gate, m_w_ffn_up, m_w_ffn_down, m_w_ple_gate, m_w_ple_proj, m_ln_ffn_g, m_ln_ffn_b, v_w_in, v_ssd_conv_w, v_ssd_conv_b, v_ssd_dt_bias, v_ssd_A_log, v_ssd_D, v_ssd_norm_w, v_mla_q_norm_w, v_mla_w_q_b, v_mla_kv_norm_w, v_mla_w_kv_b, v_mla_out_norm_w, v_w_out, v_ln_mix_g, v_ln_mix_b, v_w_ffn_gate, v_w_ffn_up, v_w_ffn_down, v_w_ple_gate, v_w_ple_proj, v_ln_ffn_g, v_ln_ffn_b):
    given = dict(locals())
    chip = 2 * lax.axis_index("x") + lax.axis_index("y")

    shards = [given[_BIG_ARG[name]][0].astype(BF16) for name, _, _ in BIG]
    conv_bits = lax.bitcast_convert_type(ssd_conv_w[0], BF16).reshape(SSD_K, 2 * CONV_SHARD)
    shards.append(jnp.pad(conv_bits, ((0, BF16_ROWS - SSD_K), (0, 0))))
    gathered = _gather_weights(shards)
    gw = {name: arr for (name, _, _), arr in zip(BIG, gathered)}
    conv_all = lax.bitcast_convert_type(gathered[-1][:, :SSD_K].reshape(NCHIP, SSD_K, CONV_SHARD, 2), F32)
    sp = {k: given[a] for k, a in _SMALL_ARG.items() if k != "conv_w"}
    sp["conv_w"] = _from_cols(conv_all)

    loss_row, grad_x, g = _local_step(x[0], p[0, 0], positions[0], loss_target[0], gw, sp)

    gbig = {name: arr for (name, _, _), arr in zip(BIG, _reduce_grads([g[name] for name, _, _ in BIG]))}
    small_in = jnp.concatenate([_rows128(g[name]) for name, _ in SMALL] + [loss_row], axis=0)
    small_sum = _allreduce_small(jnp.pad(small_in, ((0, SMALL_ROWS - small_in.shape[0]), (0, 0))))
    gsmall, row = {}, 0
    for name, size in SMALL:
        nrow = -(-size // LANE)
        gsmall[name] = small_sum[row:row + nrow].reshape(-1)[:size]
        row += nrow
    loss = small_sum[row, 0]

    grads = {}
    for name, shape, axis in BIG:
        grads[_BIG_ARG[name]] = gbig[name][None]
    for name, _ in SMALL:
        if name == "conv_w":
            full_g = gsmall[name].reshape(SSD_K, SSD_XBC)
            grads["ssd_conv_w"] = lax.dynamic_slice(full_g, (0, chip * CONV_SHARD), (SSD_K, CONV_SHARD))[None]
        else:
            grads[_SMALL_ARG[name]] = gsmall[name].reshape(given[_SMALL_ARG[name]].shape)

    delta, new_m, new_v = {}, {}, {}
    for name, _, _ in BIG:
        a = _BIG_ARG[name]
        d, m2, v2 = _adamw_big(given[a][0], grads[a][0], given["m_" + a][0], given["v_" + a][0], name="adamw_" + a)
        delta[a], new_m[a], new_v[a] = d[None], m2[None], v2[None]
    small_names = [_SMALL_ARG[name] for name, _ in SMALL]
    two_d = lambda t: t.reshape(t.shape[-2], t.shape[-1])
    ds, ms, vs = _adamw_small([two_d(given[a]) for a in small_names], [two_d(grads[a]) for a in small_names],
                              [two_d(given["m_" + a]) for a in small_names], [two_d(given["v_" + a]) for a in small_names])
    for a, d, m2, v2 in zip(small_names, ds, ms, vs):
        delta[a], new_m[a], new_v[a] = (t.reshape(given[a].shape) for t in (d, m2, v2))

    return (loss, grad_x[None], *[grads[n] for n in _WEIGHT_ORDER], *[delta[n] for n in _WEIGHT_ORDER],
            *[new_m[n] for n in _WEIGHT_ORDER], *[new_v[n] for n in _WEIGHT_ORDER])
```

```python
import functools
import math

import jax
import jax.numpy as jnp
from jax import lax
from jax.experimental import pallas as pl
from jax.experimental.pallas import tpu as pltpu

F32 = jnp.float32
BF16 = jnp.bfloat16

S = 2048
D = 1024
PLE = 256
H = 16
SSD_P = 64
SSD_INNER = 1024
SSD_N = 128
SSD_G = 2
SSD_L = 128
SSD_NC = S // SSD_L
SSD_XBC = 1536
SSD_K = 4
Q_RANK = 384
KV_RANK = 256
NOPE = 64
ROPE = 32
VDIM = 64
D_FF = 2816
IN_WIDTH = 3248
ALPHA = 2.0 ** 0.25
EPS_RMS = 1e-6
EPS_LN = 1e-5
ATT_SCALE = 1.0 / math.sqrt(NOPE + ROPE)
LN2 = math.log(2.0)
ATT_SCALE_LOG2 = ATT_SCALE / LN2
LANE = 128
NCHIP = 4
SMALL_W = 896
SM_Q, SM_KV, SM_KR, SM_DT = 0, 384, 640, 768
NEG = -1e30

ADAM_LR = 0.001
ADAM_B1 = 0.9
ADAM_B2 = 0.999
ADAM_EPS = 1e-08
ADAM_WD = 0.01
ADAM_STEP = 10


def _sigmoid(v):
    return 1.0 / (1.0 + jnp.exp(-v))


MM_VMEM_BUDGET = 36 * 2 ** 20
MM_MAX_ACC = 2048 * 1024


def _mm_tiles(pairs, ta, tb, m, n, out_dtype, has_add):
    def divs(v):
        return [LANE * d for d in range(v // LANE, 0, -1) if (v // LANE) % d == 0] if v % LANE == 0 else [v]

    def cost(tm, tn):
        tot = tm * tn * (jnp.dtype(out_dtype).itemsize + (4 if has_add else 0))
        for a, b in pairs:
            k = a.shape[-2] if ta else a.shape[-1]
            tot += k * (tm * a.dtype.itemsize + tn * b.dtype.itemsize)
        return 2 * tot

    ok = [(tm * tn, tm, tn) for tm in divs(m) for tn in divs(n) if tm * tn <= MM_MAX_ACC and cost(tm, tn) <= MM_VMEM_BUDGET]
    _, tm, tn = max(ok)
    return tm, tn


def _mm(pairs, *, ta=False, tb=False, out_dtype=F32, add=None, add_scale=1.0, chunk=None, name):
    n_pairs = len(pairs)
    a0, b0 = pairs[0]
    m = a0.shape[-1] if ta else a0.shape[-2]
    n = b0.shape[-2] if tb else b0.shape[-1]
    tm, tn = _mm_tiles(pairs, ta, tb, m, n, out_dtype, add is not None)
    dims = (((0 if ta else 1,), (1 if tb else 0,)), ((), ()))
    nk = NCHIP if chunk else 1
    assert chunk != "sum" or out_dtype == F32

    def body(*refs):
        o_ref = refs[-1]
        acc = None
        for i in range(n_pairs):
            a = refs[2 * i][...].astype(BF16)
            b = refs[2 * i + 1][...].astype(BF16)
            part = lax.dot_general(a, b, dims, preferred_element_type=F32)
            acc = part if acc is None else acc + part
        if chunk == "sum":
            k = pl.program_id(2)

            @pl.when(k == 0)
            def _():
                o_ref[...] = acc + add_scale * refs[2 * n_pairs][...] if add is not None else acc

            @pl.when(k > 0)
            def _():
                o_ref[...] += acc
        else:
            if add is not None:
                acc = acc + add_scale * refs[2 * n_pairs][...]
            o_ref[...] = acc.astype(out_dtype)

    def spec(arr, shape, idx2):
        if arr.ndim == 3:
            return pl.BlockSpec((None,) + shape, lambda i, j, k: (k,) + idx2(i, j))
        return pl.BlockSpec(shape, lambda i, j, k: idx2(i, j))

    in_specs, args = [], []
    for a, b in pairs:
        kdim = a.shape[-2] if ta else a.shape[-1]
        in_specs.append(spec(a, (kdim, tm), lambda i, j: (0, i)) if ta else spec(a, (tm, kdim), lambda i, j: (i, 0)))
        in_specs.append(spec(b, (tn, kdim), lambda i, j: (j, 0)) if tb else spec(b, (kdim, tn), lambda i, j: (0, j)))
        args += [a, b]
    if add is not None:
        in_specs.append(pl.BlockSpec((tm, tn), lambda i, j, k: (i, j)))
        args.append(add)
    if chunk == "out":
        out_spec = pl.BlockSpec((None, tm, tn), lambda i, j, k: (k, i, j))
        out_shape = jax.ShapeDtypeStruct((nk, m, n), out_dtype)
    else:
        out_spec = pl.BlockSpec((tm, tn), lambda i, j, k: (i, j))
        out_shape = jax.ShapeDtypeStruct((m, n), out_dtype)
    return pl.pallas_call(
        body, name=name, grid=(m // tm, n // tn, nk), in_specs=in_specs, out_specs=out_spec, out_shape=out_shape,
        compiler_params=pltpu.CompilerParams(dimension_semantics=("parallel", "parallel", "arbitrary")),
    )(*args)


TR = 256


def _row_spec(c):
    return pl.BlockSpec((TR, c), lambda i: (i, 0))


def _vec_spec(c):
    return pl.BlockSpec((1, c), lambda i: (0, 0))


def _acc_rows(ref, val):
    @pl.when(pl.program_id(0) == 0)
    def _():
        ref[...] = jnp.zeros_like(ref)
    ref[...] += val


def _rms_fwd(u, w, *, name):
    c = u.shape[1]

    def body(u_ref, w_ref, o_ref):
        v = u_ref[...]
        r = lax.rsqrt(jnp.mean(v * v, axis=-1, keepdims=True) + EPS_RMS)
        o_ref[...] = (v * r * w_ref[...]).astype(BF16)

    return pl.pallas_call(body, name=name, grid=(S // TR,), in_specs=[_row_spec(c), _vec_spec(c)], out_specs=_row_spec(c),
                          out_shape=jax.ShapeDtypeStruct((S, c), BF16))(u, w)


def _rms_bwd(u, w, dy, *, name):
    c = u.shape[1]

    def body(u_ref, w_ref, dy_ref, du_ref, dw_ref):
        v = u_ref[...]
        g = dy_ref[...].astype(F32)
        r = lax.rsqrt(jnp.mean(v * v, axis=-1, keepdims=True) + EPS_RMS)
        gw = g * w_ref[...]
        du_ref[...] = r * gw - v * (r * r * r * jnp.mean(gw * v, axis=-1, keepdims=True))
        _acc_rows(dw_ref, jnp.sum(g * v * r, axis=0, keepdims=True))

    return pl.pallas_call(body, name=name, grid=(S // TR,), in_specs=[_row_spec(c), _vec_spec(c), _row_spec(c)],
                          out_specs=[_row_spec(c), _vec_spec(c)],
                          out_shape=[jax.ShapeDtypeStruct((S, c), F32), jax.ShapeDtypeStruct((1, c), F32)])(u, w, dy)


def _gate_norm_fwd(y, z, w):
    def body(y_ref, z_ref, w_ref, o_ref):
        zz = z_ref[...]
        v = y_ref[...] * (zz * _sigmoid(zz))
        r = lax.rsqrt(jnp.mean(v * v, axis=-1, keepdims=True) + EPS_RMS)
        o_ref[...] = (v * r * w_ref[...]).astype(BF16)

    c = SSD_INNER
    return pl.pallas_call(body, name="ssd_gate_norm_fwd", grid=(S // TR,), in_specs=[_row_spec(c), _row_spec(c), _vec_spec(c)],
                          out_specs=_row_spec(c), out_shape=jax.ShapeDtypeStruct((S, c), BF16))(y, z, w)


def _gate_norm_bwd(y, z, w, dout):
    def body(y_ref, z_ref, w_ref, g_ref, dy_ref, dz_ref, dw_ref):
        yy = y_ref[...]
        zz = z_ref[...]
        sg = _sigmoid(zz)
        sz = zz * sg
        v = yy * sz
        g = g_ref[...]
        r = lax.rsqrt(jnp.mean(v * v, axis=-1, keepdims=True) + EPS_RMS)
        gw = g * w_ref[...]
        dv = r * gw - v * (r * r * r * jnp.mean(gw * v, axis=-1, keepdims=True))
        dy_ref[...] = dv * sz
        dz_ref[...] = (dv * yy * (sg * (1.0 + zz * (1.0 - sg)))).astype(BF16)
        _acc_rows(dw_ref, jnp.sum(g * v * r, axis=0, keepdims=True))

    c = SSD_INNER
    return pl.pallas_call(body, name="ssd_gate_norm_bwd", grid=(S // TR,),
                          in_specs=[_row_spec(c), _row_spec(c), _vec_spec(c), _row_spec(c)],
                          out_specs=[_row_spec(c), _row_spec(c), _vec_spec(c)],
                          out_shape=[jax.ShapeDtypeStruct((S, c), F32), jax.ShapeDtypeStruct((S, c), BF16),
                                     jax.ShapeDtypeStruct((1, c), F32)])(y, z, w, dout)


def _ln_fwd(xr, mix, g, b):
    def body(x_ref, m_ref, g_ref, b_ref, o_ref, ob_ref):
        pre = ALPHA * x_ref[...] + m_ref[...]
        mu = jnp.mean(pre, axis=-1, keepdims=True)
        d = pre - mu
        rs = lax.rsqrt(jnp.mean(d * d, axis=-1, keepdims=True) + EPS_LN)
        h = d * rs * g_ref[...] + b_ref[...]
        o_ref[...] = h
        ob_ref[...] = h.astype(BF16)

    return pl.pallas_call(body, name="ln_mix_fwd", grid=(S // TR,), in_specs=[_row_spec(D), _row_spec(D), _vec_spec(D), _vec_spec(D)],
                          out_specs=[_row_spec(D)] * 2,
                          out_shape=[jax.ShapeDtypeStruct((S, D), F32), jax.ShapeDtypeStruct((S, D), BF16)])(xr, mix, g, b)


def _ln_bwd(xr, mix, g, dh):
    def body(x_ref, m_ref, g_ref, dh_ref, dpre_ref, dpreb_ref, dg_ref, db_ref):
        pre = ALPHA * x_ref[...] + m_ref[...]
        mu = jnp.mean(pre, axis=-1, keepdims=True)
        d = pre - mu
        rs = lax.rsqrt(jnp.mean(d * d, axis=-1, keepdims=True) + EPS_LN)
        xh = d * rs
        dy = dh_ref[...]
        gy = dy * g_ref[...]
        dpre = rs * (gy - jnp.mean(gy, axis=-1, keepdims=True) - xh * jnp.mean(gy * xh, axis=-1, keepdims=True))
        dpre_ref[...] = dpre
        dpreb_ref[...] = dpre.astype(BF16)
        _acc_rows(dg_ref, jnp.sum(dy * xh, axis=0, keepdims=True))
        _acc_rows(db_ref, jnp.sum(dy, axis=0, keepdims=True))

    return pl.pallas_call(body, name="ln_mix_bwd", grid=(S // TR,),
                          in_specs=[_row_spec(D), _row_spec(D), _vec_spec(D), _row_spec(D)],
                          out_specs=[_row_spec(D), _row_spec(D), _vec_spec(D), _vec_spec(D)],
                          out_shape=[jax.ShapeDtypeStruct((S, D), F32), jax.ShapeDtypeStruct((S, D), BF16),
                                     jax.ShapeDtypeStruct((1, D), F32), jax.ShapeDtypeStruct((1, D), F32)])(xr, mix, g, dh)


FF_CHUNK = D_FF // NCHIP


def _ff_spec():
    return pl.BlockSpec((None, TR * 2, FF_CHUNK), lambda k, i: (k, i, 0))


def _swiglu_fwd(gate, up):
    def body(g_ref, u_ref, o_ref):
        g = g_ref[...]
        o_ref[...] = (g * _sigmoid(g) * u_ref[...]).astype(BF16)

    return pl.pallas_call(body, name="swiglu_fwd", grid=(NCHIP, S // (2 * TR)), in_specs=[_ff_spec()] * 2, out_specs=_ff_spec(),
                          out_shape=jax.ShapeDtypeStruct((NCHIP, S, FF_CHUNK), BF16))(gate, up)


def _swiglu_bwd(gate, up, dact):
    def body(g_ref, u_ref, d_ref, dg_ref, du_ref):
        g = g_ref[...]
        sg = _sigmoid(g)
        d = d_ref[...]
        dg_ref[...] = (d * u_ref[...] * (sg * (1.0 + g * (1.0 - sg)))).astype(BF16)
        du_ref[...] = (d * g * sg).astype(BF16)

    return pl.pallas_call(body, name="swiglu_bwd", grid=(NCHIP, S // (2 * TR)), in_specs=[_ff_spec()] * 3, out_specs=[_ff_spec()] * 2,
                          out_shape=[jax.ShapeDtypeStruct((NCHIP, S, FF_CHUNK), BF16)] * 2)(gate, up, dact)


def _final_fwd_bwd(h1, ffn, pg, pp, target, g2, b2):
    def body(h_ref, f_ref, pg_ref, pp_ref, t_ref, g_ref, b_ref, dpre_ref, dpreb_ref, dpg_ref, dpp_ref, dg_ref, db_ref, loss_ref):
        sg = _sigmoid(pg_ref[...])
        ppv = pp_ref[...]
        pre = ALPHA * h_ref[...] + f_ref[...] + sg * ppv
        mu = jnp.mean(pre, axis=-1, keepdims=True)
        d = pre - mu
        rs = lax.rsqrt(jnp.mean(d * d, axis=-1, keepdims=True) + EPS_LN)
        xh = d * rs
        err = xh * g_ref[...] + b_ref[...] - t_ref[...]
        dy = err * (1.0 / D)
        gy = dy * g_ref[...]
        dpre = rs * (gy - jnp.mean(gy, axis=-1, keepdims=True) - xh * jnp.mean(gy * xh, axis=-1, keepdims=True))
        dpre_ref[...] = dpre
        dpreb_ref[...] = dpre.astype(BF16)
        dpg_ref[...] = (dpre * ppv * sg * (1.0 - sg)).astype(BF16)
        dpp_ref[...] = (dpre * sg).astype(BF16)
        _acc_rows(dg_ref, jnp.sum(dy * xh, axis=0, keepdims=True))
        _acc_rows(db_ref, jnp.sum(dy, axis=0, keepdims=True))
        _acc_rows(loss_ref, 0.5 * jnp.sum(jnp.mean(err * err, axis=-1, keepdims=True), axis=0, keepdims=True) * jnp.ones((1, LANE), F32))

    return pl.pallas_call(
        body, name="final_ln_loss", grid=(S // TR,),
        in_specs=[_row_spec(D)] * 5 + [_vec_spec(D)] * 2,
        out_specs=[_row_spec(D)] * 4 + [_vec_spec(D), _vec_spec(D), _vec_spec(LANE)],
        out_shape=[jax.ShapeDtypeStruct((S, D), F32)] + [jax.ShapeDtypeStruct((S, D), BF16)] * 3 + [
                   jax.ShapeDtypeStruct((1, D), F32), jax.ShapeDtypeStruct((1, D), F32), jax.ShapeDtypeStruct((1, LANE), F32)],
    )(h1, ffn, pg, pp, target, g2, b2)


def _rot(u, cos_t, sin_t, lane):
    partner = jnp.where(lane < NOPE + ROPE // 2, pltpu.roll(u, LANE - ROPE // 2, 1), pltpu.roll(u, ROPE // 2, 1))
    return u * cos_t + partner * sin_t


def _q_rope(qlin, cos_t, sin_t):
    def body(q_ref, c_ref, s_ref, o_ref):
        lane = lax.broadcasted_iota(jnp.int32, (TR, LANE), 1)
        c, s = c_ref[...], s_ref[...]
        for h in range(H):
            o_ref[:, h * LANE:(h + 1) * LANE] = _rot(q_ref[:, h * LANE:(h + 1) * LANE], c, s, lane).astype(BF16)

    w = H * LANE
    return pl.pallas_call(body, name="q_rope", grid=(S // TR,), in_specs=[_row_spec(w), _row_spec(LANE), _row_spec(LANE)],
                          out_specs=_row_spec(w), out_shape=jax.ShapeDtypeStruct((S, w), BF16))(qlin, cos_t, sin_t)


def _q_unrope(dq, cos_t, sin_t):
    def body(q_ref, c_ref, s_ref, o_ref):
        lane = lax.broadcasted_iota(jnp.int32, (TR, LANE), 1)
        c, s = c_ref[...], -s_ref[...]
        for h in range(H):
            o_ref[:, h * LANE:(h + 1) * LANE] = _rot(q_ref[:, h * LANE:(h + 1) * LANE], c, s, lane).astype(BF16)

    w = H * LANE
    return pl.pallas_call(body, name="q_unrope", grid=(S // TR,), in_specs=[_row_spec(w), _row_spec(LANE), _row_spec(LANE)],
                          out_specs=_row_spec(w), out_shape=jax.ShapeDtypeStruct((S, w), BF16))(dq, cos_t, sin_t)


def _k_prep(klin, small, cos_t, sin_t):
    def body(k_ref, kr_ref, c_ref, s_ref, o_ref):
        lane = lax.broadcasted_iota(jnp.int32, (TR, LANE), 1)
        kr = _rot(pltpu.roll(kr_ref[...], NOPE, 1), c_ref[...], s_ref[...], lane)
        for h in range(H):
            o_ref[:, h * LANE:(h + 1) * LANE] = (k_ref[:, h * LANE:(h + 1) * LANE] + kr).astype(BF16)

    w = H * LANE
    kr_spec = pl.BlockSpec((TR, LANE), lambda i: (i, SM_KR // LANE))
    return pl.pallas_call(body, name="k_prep", grid=(S // TR,), in_specs=[_row_spec(w), kr_spec, _row_spec(LANE), _row_spec(LANE)],
                          out_specs=_row_spec(w), out_shape=jax.ShapeDtypeStruct((S, w), BF16))(klin, small, cos_t, sin_t)


def _k_rope_bwd(dk, cos_t, sin_t):
    def body(k_ref, c_ref, s_ref, o_ref):
        lane = lax.broadcasted_iota(jnp.int32, (TR, LANE), 1)
        acc = k_ref[:, 0:LANE]
        for h in range(1, H):
            acc = acc + k_ref[:, h * LANE:(h + 1) * LANE]
        acc = jnp.where((lane >= NOPE) & (lane < NOPE + ROPE), acc, 0.0)
        o_ref[...] = pltpu.roll(_rot(acc, c_ref[...], -s_ref[...], lane), LANE - NOPE, 1)

    w = H * LANE
    return pl.pallas_call(body, name="k_rope_bwd", grid=(S // TR,), in_specs=[_row_spec(w), _row_spec(LANE), _row_spec(LANE)],
                          out_specs=_row_spec(LANE), out_shape=jax.ShapeDtypeStruct((S, LANE), F32))(dk, cos_t, sin_t)


CB = 256


def _shift_down(u, k, row):
    if k == 0:
        return u
    return jnp.where(row >= k, pltpu.roll(u, k, 0), 0.0)


def _shift_up(u, k, row):
    if k == 0:
        return u
    return jnp.where(row < S - k, pltpu.roll(u, S - k, 0), 0.0)


def _conv_fwd(u, w, b):
    def body(u_ref, w_ref, b_ref, o_ref):
        row = lax.broadcasted_iota(jnp.int32, (S, CB), 0)
        uu = u_ref[...]
        acc = b_ref[...] + w_ref[SSD_K - 1:SSD_K, :] * uu
        for k in range(SSD_K - 1):
            acc = acc + w_ref[k:k + 1, :] * _shift_down(uu, SSD_K - 1 - k, row)
        o_ref[...] = acc * _sigmoid(acc)

    c = u.shape[1]
    return pl.pallas_call(
        body, name="conv_fwd", grid=(c // CB,),
        in_specs=[pl.BlockSpec((S, CB), lambda j: (0, j)), pl.BlockSpec((SSD_K, CB), lambda j: (0, j)), pl.BlockSpec((1, CB), lambda j: (0, j))],
        out_specs=pl.BlockSpec((S, CB), lambda j: (0, j)), out_shape=jax.ShapeDtypeStruct((S, c), F32),
    )(u, w, b)


def _conv_bwd(u, w, b, dact):
    def body(u_ref, w_ref, b_ref, d_ref, du_ref, dw_ref, db_ref):
        row = lax.broadcasted_iota(jnp.int32, (S, CB), 0)
        uu = u_ref[...]
        sh = [_shift_down(uu, SSD_K - 1 - k, row) for k in range(SSD_K)]
        acc = b_ref[...]
        for k in range(SSD_K):
            acc = acc + w_ref[k:k + 1, :] * sh[k]
        sg = _sigmoid(acc)
        dacc = d_ref[...] * (sg * (1.0 + acc * (1.0 - sg)))
        du = w_ref[SSD_K - 1:SSD_K, :] * dacc
        for k in range(SSD_K - 1):
            du = du + w_ref[k:k + 1, :] * _shift_up(dacc, SSD_K - 1 - k, row)
        du_ref[...] = du.astype(BF16)
        for k in range(SSD_K):
            dw_ref[k:k + 1, :] = jnp.sum(dacc * sh[k], axis=0, keepdims=True)
        db_ref[...] = jnp.sum(dacc, axis=0, keepdims=True)

    c = u.shape[1]
    col = lambda r: pl.BlockSpec((r, CB), lambda j: (0, j))
    return pl.pallas_call(
        body, name="conv_bwd", grid=(c // CB,), in_specs=[col(S), col(SSD_K), col(1), col(S)], out_specs=[col(S), col(SSD_K), col(1)],
        out_shape=[jax.ShapeDtypeStruct((S, c), BF16), jax.ShapeDtypeStruct((SSD_K, c), F32), jax.ShapeDtypeStruct((1, c), F32)],
    )(u, w, b, dact)


NPAIR = H // 2
PAIRS_PER_GROUP = NPAIR // SSD_G


def _softplus(v):
    return jnp.maximum(v, 0.0) + jnp.log(1.0 + jnp.exp(-jnp.abs(v)))


def _dot(a, b, dims):
    return lax.dot_general(a.astype(BF16), b.astype(BF16), (dims, ((), ())), preferred_element_type=F32)


def _dot3(a, b, dims, split_lhs):
    v = a if split_lhs else b
    v1 = v.astype(BF16)
    r1 = v - v1.astype(F32)
    v2 = r1.astype(BF16)
    v3 = (r1 - v2.astype(F32)).astype(BF16)
    acc = None
    for part in (v1, v2, v3):
        lhs, rhs = (part, b) if split_lhs else (a, part)
        t = lax.dot_general(lhs, rhs, (dims, ((), ())), preferred_element_type=F32)
        acc = t if acc is None else acc + t
    return acc


def _ssd_chunk_common(dt_ref, dtT_ref, prow_ref, pcol_ref):
    prow = prow_ref[...]
    pcol = pcol_ref[...]
    ri = lax.broadcasted_iota(jnp.int32, (SSD_L, SSD_L), 0)
    ci = lax.broadcasted_iota(jnp.int32, (SSD_L, SSD_L), 1)
    causal = ri >= ci
    pre_c = dt_ref[...] + prow[0:1, :]
    dtc = _softplus(pre_c)
    a_row = -jnp.exp(prow[1:2, :])
    cs_col = _dot3(causal.astype(BF16), dtc * a_row, ((1,), (0,)), False)
    dtr = _softplus(dtT_ref[...] + pcol[:, 0:1])
    a_col = -jnp.exp(pcol[:, 1:2])
    cs_row = _dot3(dtr * a_col, (ri <= ci).astype(BF16), ((1,), (0,)), True)
    return prow, causal, pre_c, dtc, a_row, cs_col, cs_row


def _ssd_fwd(act, small, dtT, prow, pcol):
    def body(x_ref, b_ref, c_ref, dt_ref, dtT_ref, prow_ref, pcol_ref, y_ref, st_ref, state):
        @pl.when(pl.program_id(0) == 0)
        def _():
            state[...] = jnp.zeros_like(state)

        prow, causal, _, dtc, _, cs_col, cs_row = _ssd_chunk_common(dt_ref, dtT_ref, prow_ref, pcol_ref)
        lo = lax.broadcasted_iota(jnp.int32, (SSD_L, LANE), 1) < SSD_P
        lo1 = lo[0:1, :]
        for g in range(SSD_G):
            bm = b_ref[:, g * SSD_N:(g + 1) * SSD_N]
            cm = c_ref[:, g * SSD_N:(g + 1) * SSD_N]
            cb = _dot(cm, bm, ((1,), (1,)))
            for qq in range(PAIRS_PER_GROUP):
                q = g * PAIRS_PER_GROUP + qq
                ha, hb = 2 * q, 2 * q + 1
                csa, csb = cs_col[:, ha:ha + 1], cs_col[:, hb:hb + 1]
                xp = x_ref[:, q * LANE:(q + 1) * LANE]
                xx = xp * jnp.where(lo, dtc[:, ha:ha + 1], dtc[:, hb:hb + 1])
                ga = cb * jnp.exp(jnp.where(causal, csa - cs_row[ha:ha + 1, :], NEG))
                gb = cb * jnp.exp(jnp.where(causal, csb - cs_row[hb:hb + 1, :], NEG))
                y = _dot(ga, jnp.where(lo, xx, 0.0), ((1,), (0,))) + _dot(gb, jnp.where(lo, 0.0, xx), ((1,), (0,)))
                s_in = state[q]
                y = y + _dot(cm, s_in, ((1,), (0,))) * jnp.where(lo, jnp.exp(csa), jnp.exp(csb))
                y = y + jnp.where(lo1, prow[2:3, ha:ha + 1], prow[2:3, hb:hb + 1]) * xp
                y_ref[:, q * LANE:(q + 1) * LANE] = y
                la, lb = csa[SSD_L - 1:SSD_L, :], csb[SSD_L - 1:SSD_L, :]
                decay = jnp.where(lo, jnp.exp(la - csa), jnp.exp(lb - csb))
                st_ref[q] = s_in
                state[q] = s_in * jnp.where(lo1, jnp.exp(la), jnp.exp(lb)) + _dot(bm, xx * decay, ((0,), (0,)))

    L = SSD_L
    return pl.pallas_call(
        body, name="ssd_fwd", grid=(SSD_NC,),
        in_specs=[pl.BlockSpec((L, SSD_INNER), lambda c: (c, 0)),
                  pl.BlockSpec((L, SSD_G * SSD_N), lambda c: (c, SSD_INNER // (SSD_G * SSD_N))),
                  pl.BlockSpec((L, SSD_G * SSD_N), lambda c: (c, SSD_INNER // (SSD_G * SSD_N) + 1)),
                  pl.BlockSpec((L, LANE), lambda c: (c, SM_DT // LANE)),
                  pl.BlockSpec((LANE, L), lambda c: (0, c)),
                  pl.BlockSpec((8, LANE), lambda c: (0, 0)), pl.BlockSpec((LANE, 8), lambda c: (0, 0))],
        out_specs=[pl.BlockSpec((L, SSD_INNER), lambda c: (c, 0)),
                   pl.BlockSpec((None, NPAIR, SSD_N, LANE), lambda c: (c, 0, 0, 0))],
        out_shape=[jax.ShapeDtypeStruct((S, SSD_INNER), F32), jax.ShapeDtypeStruct((SSD_NC, NPAIR, SSD_N, LANE), F32)],
        scratch_shapes=[pltpu.VMEM((NPAIR, SSD_N, LANE), F32)],
        compiler_params=pltpu.CompilerParams(dimension_semantics=("arbitrary",)),
    )(act, act, act, small, dtT, prow, pcol)


def _ssd_bwd(act, small, dtT, prow, pcol, states, dy):
    def body(x_ref, b_ref, c_ref, dt_ref, dtT_ref, prow_ref, pcol_ref, st_ref, dy_ref,
             dx_ref, ddt_ref, dp_ref, dstate):
        @pl.when(pl.program_id(0) == 0)
        def _():
            dstate[...] = jnp.zeros_like(dstate)
            dp_ref[...] = jnp.zeros_like(dp_ref)

        prow, causal, pre_c, dtc, a_row, cs_col, cs_row = _ssd_chunk_common(dt_ref, dtT_ref, prow_ref, pcol_ref)
        lane = lax.broadcasted_iota(jnp.int32, (SSD_L, LANE), 1)
        sub = lax.broadcasted_iota(jnp.int32, (LANE, SSD_L), 0)
        rowi = lax.broadcasted_iota(jnp.int32, (SSD_L, 1), 0)
        lane1 = lane[0:1, :]
        lo = lane < SSD_P
        lo1 = lo[0:1, :]
        dcs_c = jnp.zeros((SSD_L, LANE), F32)
        dcs_r = jnp.zeros((LANE, SSD_L), F32)
        ddt_x = jnp.zeros((SSD_L, LANE), F32)
        dd_row = jnp.zeros((1, LANE), F32)
        for g in range(SSD_G):
            bm = b_ref[:, g * SSD_N:(g + 1) * SSD_N]
            cm = c_ref[:, g * SSD_N:(g + 1) * SSD_N]
            cb = _dot(cm, bm, ((1,), (1,)))
            dcb = jnp.zeros((SSD_L, SSD_L), F32)
            dbm = jnp.zeros((SSD_L, SSD_N), F32)
            dcm = jnp.zeros((SSD_L, SSD_N), F32)
            for qq in range(PAIRS_PER_GROUP):
                q = g * PAIRS_PER_GROUP + qq
                ha, hb = 2 * q, 2 * q + 1
                csa, csb = cs_col[:, ha:ha + 1], cs_col[:, hb:hb + 1]
                xp = x_ref[:, q * LANE:(q + 1) * LANE]
                dtp = jnp.where(lo, dtc[:, ha:ha + 1], dtc[:, hb:hb + 1])
                xx = xp * dtp
                lma = jnp.exp(jnp.where(causal, csa - cs_row[ha:ha + 1, :], NEG))
                lmb = jnp.exp(jnp.where(causal, csb - cs_row[hb:hb + 1, :], NEG))
                ga, gb = cb * lma, cb * lmb
                dyp = dy_ref[:, q * LANE:(q + 1) * LANE]
                dya, dyb = jnp.where(lo, dyp, 0.0), jnp.where(lo, 0.0, dyp)
                s_in = st_ref[q]
                ds_out = dstate[q]
                la, lb = csa[SSD_L - 1:SSD_L, :], csb[SSD_L - 1:SSD_L, :]
                ecs = jnp.where(lo, jnp.exp(csa), jnp.exp(csb))
                decay = jnp.where(lo, jnp.exp(la - csa), jnp.exp(lb - csb))
                cd = jnp.where(lo1, jnp.exp(la), jnp.exp(lb))
                bds = _dot(bm, ds_out, ((1,), (0,)))
                dxx = _dot(ga, dya, ((0,), (0,))) + _dot(gb, dyb, ((0,), (0,))) + bds * decay
                dga = _dot(dya, xx, ((1,), (1,)))
                dgb = _dot(dyb, xx, ((1,), (1,)))
                dsega, dsegb = dga * ga, dgb * gb
                dcb = dcb + dga * lma + dgb * lmb
                yoff = _dot(cm, s_in, ((1,), (0,))) * ecs
                dye = dyp * ecs
                dcm = dcm + _dot(dye, s_in, ((1,), (1,)))
                xd = xx * decay
                dbm = dbm + _dot(xd, ds_out, ((1,), (1,)))
                wv = xd * bds
                t1 = dyp * yoff - wv
                col_a = (jnp.sum(dsega, axis=1, keepdims=True) + jnp.sum(jnp.where(lo, t1, 0.0), axis=1, keepdims=True))
                col_b = (jnp.sum(dsegb, axis=1, keepdims=True) + jnp.sum(jnp.where(lo, 0.0, t1), axis=1, keepdims=True))
                sprod = ds_out * s_in
                end_a = jnp.sum(jnp.where(lo, wv, 0.0), keepdims=True) + jnp.exp(la) * jnp.sum(jnp.where(lo[:SSD_N], sprod, 0.0), keepdims=True)
                end_b = jnp.sum(jnp.where(lo, 0.0, wv), keepdims=True) + jnp.exp(lb) * jnp.sum(jnp.where(lo[:SSD_N], 0.0, sprod), keepdims=True)
                col_a = col_a + jnp.where(rowi == SSD_L - 1, end_a, 0.0)
                col_b = col_b + jnp.where(rowi == SSD_L - 1, end_b, 0.0)
                dcs_c = dcs_c + jnp.where(lane == ha, col_a, 0.0) + jnp.where(lane == hb, col_b, 0.0)
                dcs_r = (dcs_r + jnp.where(sub == ha, jnp.sum(dsega, axis=0, keepdims=True), 0.0)
                         + jnp.where(sub == hb, jnp.sum(dsegb, axis=0, keepdims=True), 0.0))
                dstate[q] = _dot(cm, dye, ((0,), (0,))) + cd * ds_out
                dpair = jnp.where(lo1, prow[2:3, ha:ha + 1], prow[2:3, hb:hb + 1])
                dx_ref[:, q * LANE:(q + 1) * LANE] = dxx * dtp + dpair * dyp
                t2 = dxx * xp
                ddt_x = (ddt_x + jnp.where(lane == ha, jnp.sum(jnp.where(lo, t2, 0.0), axis=1, keepdims=True), 0.0)
                         + jnp.where(lane == hb, jnp.sum(jnp.where(lo, 0.0, t2), axis=1, keepdims=True), 0.0))
                t3 = dyp * xp
                dd_row = (dd_row + jnp.where(lane1 == ha, jnp.sum(jnp.where(lo, t3, 0.0), keepdims=True), 0.0)
                          + jnp.where(lane1 == hb, jnp.sum(jnp.where(lo, 0.0, t3), keepdims=True), 0.0))
            dx_ref[:, SSD_INNER + g * SSD_N:SSD_INNER + (g + 1) * SSD_N] = dbm + _dot(dcb, cm, ((0,), (0,)))
            dx_ref[:, SSD_INNER + (SSD_G + g) * SSD_N:SSD_INNER + (SSD_G + g + 1) * SSD_N] = dcm + _dot(dcb, bm, ((1,), (0,)))
        ri = lax.broadcasted_iota(jnp.int32, (SSD_L, SSD_L), 0)
        ci = lax.broadcasted_iota(jnp.int32, (SSD_L, SSD_L), 1)
        da = _dot3((ri <= ci).astype(BF16), dcs_c, ((1,), (0,)), False)
        da = da - _dot3(dcs_r, causal.astype(BF16), ((1,), (0,)), True).T
        ddt = ddt_x + da * a_row
        ddt_raw = ddt * _sigmoid(pre_c)
        ddt_ref[...] = ddt_raw
        da_head = jnp.sum(da * dtc, axis=0, keepdims=True) * a_row
        dp_ref[0:1, :] += jnp.sum(ddt_raw, axis=0, keepdims=True)
        dp_ref[1:2, :] += da_head
        dp_ref[2:3, :] += dd_row

    L = SSD_L
    rev = SSD_NC - 1
    bc_cols = SSD_INNER // (SSD_G * SSD_N)
    return pl.pallas_call(
        body, name="ssd_bwd", grid=(SSD_NC,),
        in_specs=[pl.BlockSpec((L, SSD_INNER), lambda c: (rev - c, 0)),
                  pl.BlockSpec((L, SSD_G * SSD_N), lambda c: (rev - c, bc_cols)),
                  pl.BlockSpec((L, SSD_G * SSD_N), lambda c: (rev - c, bc_cols + 1)),
                  pl.BlockSpec((L, LANE), lambda c: (rev - c, SM_DT // LANE)),
                  pl.BlockSpec((LANE, L), lambda c: (0, rev - c)),
                  pl.BlockSpec((8, LANE), lambda c: (0, 0)), pl.BlockSpec((LANE, 8), lambda c: (0, 0)),
                  pl.BlockSpec((None, NPAIR, SSD_N, LANE), lambda c: (rev - c, 0, 0, 0)),
                  pl.BlockSpec((L, SSD_INNER), lambda c: (rev - c, 0))],
        out_specs=[pl.BlockSpec((L, SSD_XBC), lambda c: (rev - c, 0)),
                   pl.BlockSpec((L, LANE), lambda c: (rev - c, 0)),
                   pl.BlockSpec((8, LANE), lambda c: (0, 0))],
        out_shape=[jax.ShapeDtypeStruct((S, SSD_XBC), F32), jax.ShapeDtypeStruct((S, LANE), F32),
                   jax.ShapeDtypeStruct((8, LANE), F32)],
        scratch_shapes=[pltpu.VMEM((NPAIR, SSD_N, LANE), F32)],
        compiler_params=pltpu.CompilerParams(dimension_semantics=("arbitrary",)),
    )(act, act, act, small, dtT, prow, pcol, states, dy)


TQ = 256
TK = 256
FWD_TQ = 256
FWD_TK = 256


def _attn_fwd(qc, kc, v):
    TQ, TK = FWD_TQ, FWD_TK

    def body(q_ref, k_ref, v_ref, o_ref, lse_ref):
        i = pl.program_id(1)
        lo = lax.broadcasted_iota(jnp.int32, (TQ, LANE), 1) < VDIM
        lo_k = lax.broadcasted_iota(jnp.int32, (TK, LANE), 1) < VDIM
        row_minus_col = lax.broadcasted_iota(jnp.int32, (TQ, TK), 0) - lax.broadcasted_iota(jnp.int32, (TQ, TK), 1)
        qa, qb = q_ref[:, 0:LANE], q_ref[:, LANE:2 * LANE]

        def scores(kb):
            kk = k_ref[pl.ds(pl.multiple_of(kb * TK, TK), TK), :]
            return (_dot(qa, kk[:, 0:LANE], ((1,), (1,))) * ATT_SCALE_LOG2, _dot(qb, kk[:, LANE:2 * LANE], ((1,), (1,))) * ATT_SCALE_LOG2)

        def update(kb, sa, sb, stats):
            ma, la, mb, lb, acc = stats
            vv = v_ref[pl.ds(pl.multiple_of(kb * TK, TK), TK), :]
            na = jnp.maximum(ma, jnp.max(sa, axis=1, keepdims=True))
            nb = jnp.maximum(mb, jnp.max(sb, axis=1, keepdims=True))
            pa, pb = jnp.exp2(sa - na), jnp.exp2(sb - nb)
            fa, fb = jnp.exp2(ma - na), jnp.exp2(mb - nb)
            la = fa * la + jnp.sum(pa, axis=1, keepdims=True)
            lb = fb * lb + jnp.sum(pb, axis=1, keepdims=True)
            acc = (acc * jnp.where(lo, fa, fb) + _dot(pa, jnp.where(lo_k, vv, 0), ((1,), (0,)))
                   + _dot(pb, jnp.where(lo_k, 0, vv), ((1,), (0,))))
            return na, la, nb, lb, acc

        def step(kb, carry):
            sa, sb = carry[:2]
            nxt = scores(kb + 1)
            return nxt + update(kb, sa, sb, carry[2:])

        neg = jnp.full((TQ, 1), NEG, F32)
        zero = jnp.zeros((TQ, 1), F32)
        n_full = i * (TQ // TK)
        carry = lax.fori_loop(0, n_full, step, scores(0) + (neg, zero, neg, zero, jnp.zeros((TQ, LANE), F32)))
        s, stats = carry[:2], carry[2:]
        for d in range(TQ // TK):
            nxt = scores(n_full + d + 1) if d + 1 < TQ // TK else None
            sa, sb = (jnp.where(row_minus_col >= d * TK, t, NEG) for t in s)
            stats = update(n_full + d, sa, sb, stats)
            s = nxt
        ma, la, mb, lb, acc = stats
        o_ref[...] = acc / jnp.where(lo, la, lb)
        lse_ref[...] = jnp.where(lo, ma + jnp.log2(la), mb + jnp.log2(lb)) * LN2

    return pl.pallas_call(
        body, name="attn_fwd", grid=(NPAIR, S // TQ),
        in_specs=[pl.BlockSpec((TQ, 2 * LANE), lambda j, i: (i, j)), pl.BlockSpec((S, 2 * LANE), lambda j, i: (0, j)),
                  pl.BlockSpec((S, LANE), lambda j, i: (0, j))],
        out_specs=[pl.BlockSpec((TQ, LANE), lambda j, i: (i, j)), pl.BlockSpec((None, TQ, LANE), lambda j, i: (j, i, 0))],
        out_shape=[jax.ShapeDtypeStruct((S, H * VDIM), F32), jax.ShapeDtypeStruct((NPAIR, S, LANE), F32)],
        compiler_params=pltpu.CompilerParams(dimension_semantics=("parallel", "parallel")),
    )(qc, kc, v)


def _attn_bwd(qc, kc, v, o, lse, do):
    nq = S // TQ

    def body(q_ref, k_ref, v_ref, o_ref, lse_ref, do_ref, dq_ref, dk_ref, dv_ref):
        kb = pl.program_id(1)

        @pl.when(kb == 0)
        def _():
            dq_ref[...] = jnp.zeros_like(dq_ref)

        lo = lax.broadcasted_iota(jnp.int32, (TQ, LANE), 1) < VDIM
        r0 = lax.broadcasted_iota(jnp.int32, (TQ, TK), 0)
        ck = kb * TK + lax.broadcasted_iota(jnp.int32, (TQ, TK), 1)
        ka, kbb = k_ref[:, 0:LANE], k_ref[:, LANE:2 * LANE]
        vv = v_ref[...]

        def step(qi, carry):
            dka, dkb, dv = carry
            off = pl.multiple_of(qi * TQ, TQ)
            qq = q_ref[pl.ds(off, TQ), :]
            dd = do_ref[pl.ds(off, TQ), :]
            ls = lse_ref[pl.ds(off, TQ), :]
            t = dd * o_ref[pl.ds(off, TQ), :]
            mask = r0 + qi * TQ >= ck
            outs = []
            for x, (kx, lsx) in enumerate(((ka, ls[:, 0:1]), (kbb, ls[:, VDIM:VDIM + 1]))):
                sel = lo if x == 0 else jnp.logical_not(lo)
                qx = qq[:, x * LANE:(x + 1) * LANE]
                dox = jnp.where(sel, dd, 0.0)
                delta = jnp.sum(jnp.where(sel, t, 0.0), axis=1, keepdims=True)
                sc = jnp.where(mask, _dot(qx, kx, ((1,), (1,))) * ATT_SCALE, NEG)
                p = jnp.exp(sc - lsx)
                dp = _dot(dox, vv, ((1,), (1,)))
                ds = p * (dp - delta) * ATT_SCALE
                dv = dv + _dot(p, dox, ((0,), (0,)))
                outs.append(_dot(ds, qx, ((0,), (0,))))
                dq_ref[pl.ds(off, TQ), x * LANE:(x + 1) * LANE] += _dot(ds, kx, ((1,), (0,)))
            return dka + outs[0], dkb + outs[1], dv

        z = jnp.zeros((TK, LANE), F32)
        dka, dkb, dv = lax.fori_loop(kb, nq, step, (z, z, z))
        dk_ref[:, 0:LANE] = dka
        dk_ref[:, LANE:2 * LANE] = dkb
        dv_ref[...] = dv

    return pl.pallas_call(
        body, name="attn_bwd", grid=(NPAIR, S // TK),
        in_specs=[pl.BlockSpec((S, 2 * LANE), lambda j, k: (0, j)), pl.BlockSpec((TK, 2 * LANE), lambda j, k: (k, j)),
                  pl.BlockSpec((TK, LANE), lambda j, k: (k, j)), pl.BlockSpec((S, LANE), lambda j, k: (0, j)),
                  pl.BlockSpec((None, S, LANE), lambda j, k: (j, 0, 0)), pl.BlockSpec((S, LANE), lambda j, k: (0, j))],
        out_specs=[pl.BlockSpec((S, 2 * LANE), lambda j, k: (0, j)), pl.BlockSpec((TK, 2 * LANE), lambda j, k: (k, j)),
                   pl.BlockSpec((TK, LANE), lambda j, k: (k, j))],
        out_shape=[jax.ShapeDtypeStruct((S, H * LANE), F32), jax.ShapeDtypeStruct((S, H * LANE), F32),
                   jax.ShapeDtypeStruct((S, H * VDIM), F32)],
        compiler_params=pltpu.CompilerParams(dimension_semantics=("parallel", "arbitrary")),
    )(qc, kc, v, o, lse, do)


_IN_Z, _IN_XBC, _IN_DT, _IN_Q, _IN_KV, _IN_KR = 0, 1024, 2560, 2576, 2960, 3216


def _prep_weights(w_in, w_qb, w_kvb):
    dt = w_in.dtype
    w_small = jnp.concatenate(
        [w_in[:, _IN_Q:_IN_KV], w_in[:, _IN_KV:_IN_KR], w_in[:, _IN_KR:IN_WIDTH], jnp.zeros((D, LANE - ROPE), dt),
         w_in[:, _IN_DT:_IN_Q], jnp.zeros((D, LANE - H), dt)], axis=1)
    w_q = jnp.pad(w_qb.reshape(Q_RANK, H, NOPE + ROPE), ((0, 0), (0, 0), (0, LANE - NOPE - ROPE))).reshape(Q_RANK, H * LANE)
    kv3 = w_kvb.reshape(KV_RANK, H, NOPE + VDIM)
    w_k = jnp.pad(kv3[:, :, :NOPE], ((0, 0), (0, 0), (0, LANE - NOPE))).reshape(KV_RANK, H * LANE)
    w_v = kv3[:, :, NOPE:].reshape(KV_RANK, H * VDIM)
    return w_in[:, _IN_Z:_IN_XBC], w_in[:, _IN_XBC:_IN_DT], w_small, w_q, w_k, w_v


def _rope_tables(positions):
    inv_freq = 1.0 / (10000.0 ** (jnp.arange(0, ROPE, 2, dtype=F32) / ROPE))
    ang = positions.astype(F32).reshape(S, 1) * inv_freq
    cos, sin = jnp.cos(ang), jnp.sin(ang)
    cos_t = jnp.concatenate([jnp.ones((S, NOPE), F32), cos, cos, jnp.ones((S, LANE - NOPE - ROPE), F32)], axis=1)
    sin_t = jnp.concatenate([jnp.zeros((S, NOPE), F32), -sin, sin, jnp.zeros((S, LANE - NOPE - ROPE), F32)], axis=1)
    return cos_t, sin_t


def _local_step(x, p, positions, target, gw, late_weights, sp):
    w_z, w_xbc, w_small, w_q, w_k, w_v = _prep_weights(_from_cols(gw["w_in"]), _from_cols(gw["w_qb"]), _from_cols(gw["w_kvb"]))
    w_out_s = gw["w_out"][:NCHIP // 2].reshape(SSD_INNER, D)
    w_out_m = gw["w_out"][NCHIP // 2:].reshape(SSD_INNER, D)
    cos_t, sin_t = _rope_tables(positions)
    prow = jnp.zeros((8, LANE), F32).at[0, :H].set(sp["dt_bias"][0]).at[1, :H].set(sp["A_log"][0]).at[2, :H].set(sp["D"][0])
    pcol = prow.T

    xb, pb = x.astype(BF16), p.astype(BF16)
    z = _mm([(xb, w_z)], name="proj_z")
    xbc = _mm([(xb, w_xbc)], name="proj_xbc")
    small = _mm([(xb, w_small)], name="proj_small")
    act = _conv_fwd(xbc, sp["conv_w"], sp["conv_b"])
    dt_t = small[:, SM_DT:SM_DT + LANE].T
    y, states = _ssd_fwd(act, small, dt_t, prow, pcol)
    y_ssd = _gate_norm_fwd(y, z, sp["ssd_norm"])
    q_c, kv_c = small[:, SM_Q:SM_Q + Q_RANK], small[:, SM_KV:SM_KV + KV_RANK]
    qn = _rms_fwd(q_c, sp["q_norm"], name="q_norm_fwd")
    kvn = _rms_fwd(kv_c, sp["kv_norm"], name="kv_norm_fwd")
    qcat = _q_rope(_mm([(qn, w_q)], name="q_up"), cos_t, sin_t)
    kcat = _k_prep(_mm([(kvn, w_k)], name="k_up"), small, cos_t, sin_t)
    v = _mm([(kvn, w_v)], out_dtype=BF16, name="v_up")
    o, lse = _attn_fwd(qcat, kcat, v)
    y_mla = _rms_fwd(o, sp["out_norm"], name="out_norm_fwd")
    mix = _mm([(y_ssd, w_out_s), (y_mla, w_out_m)], name="out_proj")
    h1, h1b = _ln_fwd(x, mix, sp["ln_mix_g"], sp["ln_mix_b"])
    gl = late_weights(h1b)
    w_pg, w_pp = gl["w_pg"].reshape(D, D), _from_cols(gl["w_pp"])
    w_gate, w_up, w_down = gl["w_gate"], gl["w_up"], gl["w_down"]
    gate = _mm([(h1b, w_gate)], chunk="out", name="ffn_gate")
    up = _mm([(h1b, w_up)], chunk="out", name="ffn_up")
    actf = _swiglu_fwd(gate, up)
    ffn = _mm([(actf, w_down)], chunk="sum", name="ffn_down")
    pg = _mm([(h1b, w_pg)], name="ple_gate")
    pp = _mm([(pb, w_pp)], name="ple_proj")
    dpre2, dpre2b, dpg, dpp, dg2, db2, loss_row = _final_fwd_bwd(h1, ffn, pg, pp, target, sp["ln_ffn_g"], sp["ln_ffn_b"])

    g = {"ln_ffn_g": dg2, "ln_ffn_b": db2}
    g["w_pp"] = _to_cols(_mm([(pb, dpp)], ta=True, out_dtype=BF16, name="d_w_ple_proj"))
    g["w_pg"] = _mm([(h1b, dpg)], ta=True, out_dtype=BF16, name="d_w_ple_gate").reshape(NCHIP, D // NCHIP, D)
    g["w_down"] = _mm([(actf, dpre2b)], ta=True, chunk="out", out_dtype=BF16, name="d_w_down")
    dactf = _mm([(dpre2b, w_down)], tb=True, chunk="out", name="d_act")
    dgate, dup = _swiglu_bwd(gate, up, dactf)
    g["w_gate"] = _mm([(h1b, dgate)], ta=True, chunk="out", out_dtype=BF16, name="d_w_gate")
    g["w_up"] = _mm([(h1b, dup)], ta=True, chunk="out", out_dtype=BF16, name="d_w_up")
    dh1 = _mm([(dpg, w_pg)], tb=True, add=dpre2, add_scale=ALPHA, name="d_h1_ple")
    dh1 = _mm([(dgate, w_gate), (dup, w_up)], tb=True, chunk="sum", add=dh1, name="d_h1")
    dpre1, dpre1b, g["ln_mix_g"], g["ln_mix_b"] = _ln_bwd(x, mix, sp["ln_mix_g"], dh1)
    dy_ssd = _mm([(dpre1b, w_out_s)], tb=True, name="d_y_ssd")
    dy_mla = _mm([(dpre1b, w_out_m)], tb=True, name="d_y_mla")
    g["w_out"] = jnp.concatenate([_mm([(y_ssd, dpre1b)], ta=True, out_dtype=BF16, name="d_w_out_s"),
                                  _mm([(y_mla, dpre1b)], ta=True, out_dtype=BF16, name="d_w_out_m")],
                                 axis=0).reshape(NCHIP, 2 * SSD_INNER // NCHIP, D)
    do, g["out_norm"] = _rms_bwd(o, sp["out_norm"], dy_mla, name="out_norm_bwd")
    dq, dk, dv = _attn_bwd(qcat, kcat, v, o, lse, do)
    dqlin = _q_unrope(dq, cos_t, sin_t)
    dw_q = _mm([(qn, dqlin)], ta=True, out_dtype=BF16, name="d_w_q")
    dqn = _mm([(dqlin, w_q)], tb=True, name="d_qn")
    dq_c, g["q_norm"] = _rms_bwd(q_c, sp["q_norm"], dqn, name="q_norm_bwd")
    dkr = _k_rope_bwd(dk, cos_t, sin_t)
    dw_k = _mm([(kvn, dk)], ta=True, out_dtype=BF16, name="d_w_k")
    dw_v = _mm([(kvn, dv)], ta=True, out_dtype=BF16, name="d_w_v")
    dkvn = _mm([(dk, w_k), (dv, w_v)], tb=True, name="d_kvn")
    dkv_c, g["kv_norm"] = _rms_bwd(kv_c, sp["kv_norm"], dkvn, name="kv_norm_bwd")
    g["w_qb"] = _to_cols(dw_q.reshape(Q_RANK, H, LANE)[:, :, :NOPE + ROPE].reshape(Q_RANK, H * (NOPE + ROPE)))
    g["w_kvb"] = _to_cols(jnp.concatenate([dw_k.reshape(KV_RANK, H, LANE)[:, :, :NOPE], dw_v.reshape(KV_RANK, H, VDIM)],
                                          axis=2).reshape(KV_RANK, H * (NOPE + VDIM)))
    dy, dz, g["ssd_norm"] = _gate_norm_bwd(y, z, sp["ssd_norm"], dy_ssd)
    dact, ddt, dprow = _ssd_bwd(act, small, dt_t, prow, pcol, states, dy)
    g["dt_bias"], g["A_log"], g["D"] = dprow[0:1, :H], dprow[1:2, :H], dprow[2:3, :H]
    dxbc, g["conv_w"], g["conv_b"] = _conv_bwd(xbc, sp["conv_w"], sp["conv_b"], dact)
    dsmall = jnp.concatenate([dq_c, dkv_c, dkr, ddt], axis=1).astype(BF16)
    grad_x = _mm([(dz, w_z), (dxbc, w_xbc), (dsmall, w_small)], tb=True, add=dpre1, add_scale=ALPHA, name="d_x")
    dw_small = _mm([(xb, dsmall)], ta=True, out_dtype=BF16, name="d_w_small")
    g["w_in"] = _to_cols(jnp.concatenate(
        [_mm([(xb, dz)], ta=True, out_dtype=BF16, name="d_w_z"), _mm([(xb, dxbc)], ta=True, out_dtype=BF16, name="d_w_xbc"),
         dw_small[:, SM_DT:SM_DT + H], dw_small[:, SM_Q:SM_Q + Q_RANK], dw_small[:, SM_KV:SM_KV + KV_RANK],
         dw_small[:, SM_KR:SM_KR + ROPE]], axis=1))
    return loss_row, grad_x, g


MESH = pl.DeviceIdType.MESH
BIG = (("w_in", (D, IN_WIDTH), 1), ("w_qb", (Q_RANK, H * (NOPE + ROPE)), 1), ("w_kvb", (KV_RANK, H * (NOPE + VDIM)), 1),
       ("w_out", (2 * SSD_INNER, D), 0), ("w_gate", (D, D_FF), 1), ("w_up", (D, D_FF), 1), ("w_down", (D_FF, D), 0),
       ("w_pg", (D, D), 0), ("w_pp", (PLE, D), 1))
CONV_SHARD = SSD_XBC // NCHIP
BF16_ROWS = 16


def _from_cols(stack):
    return jnp.concatenate([stack[k] for k in range(NCHIP)], axis=1)


def _to_cols(full):
    r, c4 = full.shape
    return full.reshape(r, NCHIP, c4 // NCHIP).transpose(1, 0, 2)


def _coords():
    return lax.axis_index("x"), lax.axis_index("y"), lax.axis_index("c")


def _peers():
    x, y, c = _coords()
    return 2 * x + y, c, [(1 - x, y), (x, 1 - y), (1 - x, 1 - y)], (x, y, 1 - c)


def _half(c, rows):
    return pl.ds(pl.multiple_of(c * (rows // 2), BF16_ROWS), rows // 2)


def _gather_weights(shards):
    n_arr = len(shards)
    split = [s.shape[0] % (2 * BF16_ROWS) == 0 for s in shards]
    per = 2 * (NCHIP - 1)

    def body(*refs):
        ins, outs = refs[:n_arr], refs[n_arr:2 * n_arr]
        send_sems, recv_sems, local_sems = refs[2 * n_arr:]
        k, c, chips, sibling = _peers()

        def copy(idx, src, dst, to):
            return pltpu.make_async_remote_copy(src_ref=src, dst_ref=dst, send_sem=send_sems.at[idx], recv_sem=recv_sems.at[idx],
                                                device_id=to, device_id_type=MESH)

        def part(a, chip, core):
            return outs[a].at[chip, _half(core, shards[a].shape[0])] if split[a] else outs[a].at[chip]

        mine = [pltpu.make_async_copy(ins[a], outs[a].at[k], local_sems.at[a]) for a in range(n_arr)]
        for cp in mine:
            cp.start()
        sends = []
        for a in range(n_arr):
            src = ins[a].at[_half(c, shards[a].shape[0])] if split[a] else ins[a]
            for j, (cx, cy) in enumerate(chips):
                sends.append(copy(per * a + j, src, part(a, k, c), (cx, cy, c)))
                sends[-1].start()
        for j, (cx, cy) in enumerate(chips):
            for a in range(n_arr):
                landed = part(a, 2 * cx + cy, c)
                copy(per * a + j, landed, landed, (cx, cy, c)).wait_recv()
                if split[a]:
                    sends.append(copy(per * a + NCHIP - 1 + j, landed, landed, sibling))
                    sends[-1].start()
        for j, (cx, cy) in enumerate(chips):
            for a in range(n_arr):
                if split[a]:
                    other = part(a, 2 * cx + cy, 1 - c)
                    copy(per * a + NCHIP - 1 + j, other, other, sibling).wait_recv()
        for cp in sends:
            cp.wait_send()
        for cp in mine:
            cp.wait()

    any_spec = pl.BlockSpec(memory_space=pl.ANY)
    return pl.pallas_call(
        body, name="gather_weights", in_specs=[any_spec] * n_arr, out_specs=[any_spec] * n_arr,
        out_shape=[jax.ShapeDtypeStruct((NCHIP,) + s.shape, s.dtype) for s in shards],
        scratch_shapes=[pltpu.SemaphoreType.DMA((per * n_arr,)), pltpu.SemaphoreType.DMA((per * n_arr,)),
                        pltpu.SemaphoreType.DMA((n_arr,))],
    )(*shards)


def _reduce_grads(stacks):
    n_arr = len(stacks)
    dims = [s.shape[1:] for s in stacks]
    per = NCHIP + 1

    def body(*refs):
        ins, fin, r1, part, r2 = (refs[i * n_arr:(i + 1) * n_arr] for i in range(5))
        send_sems, recv_sems, local_sems = refs[5 * n_arr:]
        k, c, chips, sibling = _peers()

        def copy(idx, src, dst, to):
            return pltpu.make_async_remote_copy(src_ref=src, dst_ref=dst, send_sem=send_sems.at[idx], recv_sem=recv_sems.at[idx],
                                                device_id=to, device_id_type=MESH)

        pairs = [copy(per * a, ins[a].at[:, _half(1 - c, dims[a][0])], r1[a], sibling) for a in range(n_arr)]
        for cp in pairs:
            cp.start()
        sends, own = [], []
        for a in range(n_arr):
            hr, cols = dims[a][0] // 2, dims[a][1]
            pairs[a].wait_recv()

            def pair_sum(va, vb, vo, a=a):
                for kk in range(NCHIP):
                    pltpu.sync_copy(ins[a].at[kk, _half(c, dims[a][0])], va)
                    pltpu.sync_copy(r1[a].at[kk], vb)
                    vo[...] = (va[...].astype(F32) + vb[...].astype(F32)).astype(BF16)
                    pltpu.sync_copy(vo, part[a].at[kk])

            pl.run_scoped(pair_sum, *[pltpu.VMEM((hr, cols), BF16)] * 3)
            for j, (cx, cy) in enumerate(chips):
                sends.append(copy(per * a + 1 + j, part[a].at[2 * cx + cy], r2[a].at[k], (cx, cy, c)))
                sends[-1].start()
            own.append(pltpu.make_async_copy(part[a].at[k], r2[a].at[k], local_sems.at[a]))
            own[-1].start()
        for a in range(n_arr):
            hr, cols = dims[a][0] // 2, dims[a][1]
            mine = fin[a].at[_half(c, dims[a][0])]
            own[a].wait()
            for j, (cx, cy) in enumerate(chips):
                landed = r2[a].at[2 * cx + cy]
                copy(per * a + 1 + j, landed, landed, (cx, cy, c)).wait_recv()

            def chip_sum(vs, vf, a=a, mine=mine):
                pltpu.sync_copy(r2[a], vs)
                acc = vs[0].astype(F32)
                for kk in range(1, NCHIP):
                    acc = acc + vs[kk].astype(F32)
                vf[...] = acc
                pltpu.sync_copy(vf, mine)

            pl.run_scoped(chip_sum, pltpu.VMEM((NCHIP, hr, cols), BF16), pltpu.VMEM((hr, cols), F32))
            sends.append(copy(per * a + NCHIP, mine, mine, sibling))
            sends[-1].start()
        for a in range(n_arr):
            other = fin[a].at[_half(1 - c, dims[a][0])]
            copy(per * a + NCHIP, other, other, sibling).wait_recv()
        for cp in pairs + sends:
            cp.wait_send()

    any_spec = pl.BlockSpec(memory_space=pl.ANY)
    stage = [jax.ShapeDtypeStruct((NCHIP, r // 2, cols), BF16) for r, cols in dims]
    return pl.pallas_call(
        body, name="reduce_grads", in_specs=[any_spec] * n_arr, out_specs=[any_spec] * (4 * n_arr),
        out_shape=[jax.ShapeDtypeStruct(d, F32) for d in dims] + stage * 3,
        scratch_shapes=[pltpu.SemaphoreType.DMA((per * n_arr,)), pltpu.SemaphoreType.DMA((per * n_arr,)),
                        pltpu.SemaphoreType.DMA((n_arr,))],
    )(*stacks)[:n_arr]


LATE = ("w_gate", "w_up", "w_down", "w_pg", "w_pp")
HBM_SPEC = pl.BlockSpec(memory_space=pltpu.HBM)
SEM_SPEC = pl.BlockSpec(memory_space=pltpu.SEMAPHORE)
IN_FLIGHT = pltpu.SideEffectType.DATAFLOW_SIDE_EFFECTING


def _in_hbm(a):
    return pltpu.with_memory_space_constraint(a, pltpu.HBM)


def _hbm_like(arrs, lead=()):
    return [pltpu.HBM(lead + a.shape, a.dtype) for a in arrs]


def _split_start(name, srcs, lands, after, start):
    n = len(srcs)
    n_sem = (NCHIP - 1) * n

    def body(*refs):
        src_refs, land_refs = refs[:n], refs[n:2 * n]
        send_sems, recv_sems = refs[2 * n + 1], refs[2 * n + 2]
        token = refs[-1]
        k, c, chips, _ = _peers()

        def copy(idx, src, dst, to):
            return pltpu.make_async_remote_copy(src_ref=src, dst_ref=dst, send_sem=send_sems.at[idx], recv_sem=recv_sems.at[idx],
                                                device_id=to, device_id_type=MESH)

        for cp in start(k, c, chips, src_refs, land_refs, copy):
            cp.start()
        token[...] = jnp.zeros_like(token)

    sem = pltpu.SemaphoreType.DMA((n_sem,))
    outs = pl.pallas_call(
        body, name=name, in_specs=[HBM_SPEC] * (2 * n) + [pl.BlockSpec(memory_space=pl.ANY)],
        out_specs=[SEM_SPEC, SEM_SPEC] + [HBM_SPEC] * (2 * n) + [pl.BlockSpec(memory_space=pltpu.VMEM)],
        out_shape=[sem, sem] + _hbm_like(srcs) + _hbm_like(lands) + [jax.ShapeDtypeStruct((8, LANE), F32)],
        input_output_aliases={i: 2 + i for i in range(2 * n)},
        compiler_params=pltpu.CompilerParams(has_side_effects=IN_FLIGHT),
    )(*[_in_hbm(a) for a in srcs], *[_in_hbm(a) for a in lands], after)
    return outs[0], outs[1], outs[2:2 + n], outs[2 + n:2 + 2 * n]


def _split_wait(name, send_sems, recv_sems, srcs, lands, after, waits):
    n = len(srcs)

    def body(*refs):
        src_refs, land_refs = refs[:n], refs[n:2 * n]
        send_ref, recv_ref = refs[2 * n], refs[2 * n + 1]
        k, c, chips, _ = _peers()

        def copy(idx, src, dst, to):
            return pltpu.make_async_remote_copy(src_ref=src, dst_ref=dst, send_sem=send_ref.at[idx], recv_sem=recv_ref.at[idx],
                                                device_id=to, device_id_type=MESH)

        for cp in waits(k, c, chips, src_refs, land_refs, copy):
            cp.wait_send()
            cp.wait_recv()

    outs = pl.pallas_call(
        body, name=name, in_specs=[HBM_SPEC] * (2 * n) + [SEM_SPEC, SEM_SPEC, pl.BlockSpec(memory_space=pl.ANY)],
        out_specs=[HBM_SPEC] * (2 * n), out_shape=_hbm_like(srcs) + _hbm_like(lands),
        input_output_aliases={i: i for i in range(2 * n)},
        compiler_params=pltpu.CompilerParams(has_side_effects=IN_FLIGHT),
    )(*srcs, *lands, send_sems, recv_sems, after)
    return outs[n:]


def _late_gather_copies(k, c, chips, shard_refs, stack_refs, copy):
    out = []
    for a, (src, dst) in enumerate(zip(shard_refs, stack_refs)):
        rows = src.shape[0]
        for j, (cx, cy) in enumerate(chips):
            out.append((a, j, rows, copy((NCHIP - 1) * a + j, src.at[_half(c, rows)], dst.at[k, _half(c, rows)], (cx, cy, c))))
    return out


def _gather_late_start(shards, after):
    def start(k, c, chips, srcs, lands, copy):
        return [cp for _, _, _, cp in _late_gather_copies(k, c, chips, srcs, lands, copy)]

    lands = [lax.empty((NCHIP,) + s.shape, s.dtype) for s in shards]
    return _split_start("gather_late_start", shards, lands, after, start)


def _gather_late_wait(send_sems, recv_sems, shards, lands, after):
    def waits(k, c, chips, srcs, lands_, copy):
        out = []
        for a, (src, dst) in enumerate(zip(srcs, lands_)):
            rows = src.shape[0]
            for j, (cx, cy) in enumerate(chips):
                landed = dst.at[2 * cx + cy, _half(c, rows)]
                out.append(copy((NCHIP - 1) * a + j, src.at[_half(c, rows)], landed, (cx, cy, c)))
        return out

    return _split_wait("gather_late_wait", send_sems, recv_sems, shards, lands, after, waits)


def _gather_late_finish(stacks, shards):
    n = len(stacks)
    per = NCHIP - 1

    def body(*refs):
        ins = refs[n:2 * n]
        outs = refs[2 * n:3 * n]
        send_sems, recv_sems, local_sems = refs[3 * n:]
        k, c, chips, sibling = _peers()
        mine = [pltpu.make_async_copy(ins[a], outs[a].at[k], local_sems.at[a]) for a in range(n)]
        for cp in mine:
            cp.start()
        sends = []
        for a in range(n):
            rows = shards[a].shape[0]
            for j, (cx, cy) in enumerate(chips):
                landed = outs[a].at[2 * cx + cy, _half(c, rows)]
                sends.append(pltpu.make_async_remote_copy(src_ref=landed, dst_ref=landed, send_sem=send_sems.at[per * a + j],
                                                          recv_sem=recv_sems.at[per * a + j], device_id=sibling, device_id_type=MESH))
                sends[-1].start()
        for a in range(n):
            rows = shards[a].shape[0]
            for j, (cx, cy) in enumerate(chips):
                other = outs[a].at[2 * cx + cy, _half(1 - c, rows)]
                pltpu.make_async_remote_copy(src_ref=other, dst_ref=other, send_sem=send_sems.at[per * a + j],
                                             recv_sem=recv_sems.at[per * a + j], device_id=sibling, device_id_type=MESH).wait_recv()
        for cp in sends:
            cp.wait_send()
        for cp in mine:
            cp.wait()

    any_spec = pl.BlockSpec(memory_space=pl.ANY)
    return pl.pallas_call(
        body, name="gather_late_finish", in_specs=[any_spec] * (2 * n), out_specs=[any_spec] * n,
        out_shape=[jax.ShapeDtypeStruct(s.shape, s.dtype) for s in stacks], input_output_aliases={i: i for i in range(n)},
        scratch_shapes=[pltpu.SemaphoreType.DMA((per * n,)), pltpu.SemaphoreType.DMA((per * n,)), pltpu.SemaphoreType.DMA((n,))],
    )(*stacks, *shards)


SMALL = (("conv_w", SSD_K * SSD_XBC), ("conv_b", SSD_XBC), ("dt_bias", H), ("A_log", H), ("D", H), ("ssd_norm", SSD_INNER),
         ("q_norm", Q_RANK), ("kv_norm", KV_RANK), ("out_norm", SSD_INNER), ("ln_mix_g", D), ("ln_mix_b", D),
         ("ln_ffn_g", D), ("ln_ffn_b", D))
SMALL_ROWS = 120
NDEV = 8


def _allreduce_small(sv):
    def body(sv_ref, out_ref, slots, send_sems, recv_sems):
        x, y, c = _coords()
        me = 4 * x + 2 * y + c
        slots[me] = sv_ref[...]
        copies = []
        for d in range(1, NDEV):
            to = (x ^ (d >> 2), y ^ ((d >> 1) & 1), c ^ (d & 1))
            copies.append(pltpu.make_async_remote_copy(src_ref=sv_ref, dst_ref=slots.at[me], send_sem=send_sems.at[d - 1],
                                                       recv_sem=recv_sems.at[d - 1], device_id=to, device_id_type=MESH))
            copies[-1].start()
        for cp in copies:
            cp.wait_recv()
        for cp in copies:
            cp.wait_send()
        acc = slots[0]
        for i in range(1, NDEV):
            acc = acc + slots[i]
        out_ref[...] = acc

    vm = pl.BlockSpec(memory_space=pltpu.VMEM)
    return pl.pallas_call(
        body, name="allreduce_small", in_specs=[vm], out_specs=vm, out_shape=jax.ShapeDtypeStruct((SMALL_ROWS, LANE), F32),
        scratch_shapes=[pltpu.VMEM((NDEV, SMALL_ROWS, LANE), F32), pltpu.SemaphoreType.DMA((NDEV - 1,)),
                        pltpu.SemaphoreType.DMA((NDEV - 1,))],
    )(sv)


def _adamw_math(w, g, m, v):
    m2 = ADAM_B1 * m + (1.0 - ADAM_B1) * g
    v2 = ADAM_B2 * v + (1.0 - ADAM_B2) * (g * g)
    m_hat = m2 / (1.0 - ADAM_B1 ** ADAM_STEP)
    v_hat = v2 / (1.0 - ADAM_B2 ** ADAM_STEP)
    return -ADAM_LR * (m_hat / (jnp.sqrt(v_hat) + ADAM_EPS) + ADAM_WD * w), m2, v2


def _adamw_big(w, g, m, v, *, name):
    r, c = w.shape
    tr = next(t for t in (512, 384, 352, 256, 128, 64, 8) if r % t == 0)

    def body(w_ref, g_ref, m_ref, v_ref, d_ref, m2_ref, v2_ref):
        d_ref[...], m2_ref[...], v2_ref[...] = _adamw_math(w_ref[...], g_ref[...], m_ref[...], v_ref[...])

    spec = pl.BlockSpec((tr, c), lambda i: (i, 0))
    return pl.pallas_call(body, name=name, grid=(r // tr,), in_specs=[spec] * 4, out_specs=[spec] * 3,
                          out_shape=[jax.ShapeDtypeStruct((r, c), F32)] * 3)(w, g, m, v)


def _adamw_small(ws, gs, ms, vs):
    n = len(ws)

    def body(*refs):
        for i in range(n):
            w_ref, g_ref, m_ref, v_ref = (refs[j * n + i] for j in range(4))
            d_ref, m2_ref, v2_ref = (refs[(4 + j) * n + i] for j in range(3))
            d_ref[...], m2_ref[...], v2_ref[...] = _adamw_math(w_ref[...], g_ref[...], m_ref[...], v_ref[...])

    vm = pl.BlockSpec(memory_space=pltpu.VMEM)
    shapes = [jax.ShapeDtypeStruct(w.shape, F32) for w in ws]
    outs = pl.pallas_call(body, name="adamw_small", in_specs=[vm] * (4 * n), out_specs=[vm] * (3 * n), out_shape=shapes * 3)(
        *ws, *gs, *ms, *vs)
    return outs[:n], outs[n:2 * n], outs[2 * n:]


_SMALL_ARG = {"conv_w": "ssd_conv_w", "conv_b": "ssd_conv_b", "dt_bias": "ssd_dt_bias", "A_log": "ssd_A_log", "D": "ssd_D",
              "ssd_norm": "ssd_norm_w", "q_norm": "mla_q_norm_w", "kv_norm": "mla_kv_norm_w", "out_norm": "mla_out_norm_w",
              "ln_mix_g": "ln_mix_g", "ln_mix_b": "ln_mix_b", "ln_ffn_g": "ln_ffn_g", "ln_ffn_b": "ln_ffn_b"}
_BIG_ARG = {"w_in": "w_in", "w_qb": "mla_w_q_b", "w_kvb": "mla_w_kv_b", "w_out": "w_out", "w_gate": "w_ffn_gate",
            "w_up": "w_ffn_up", "w_down": "w_ffn_down", "w_pg": "w_ple_gate", "w_pp": "w_ple_proj"}
_WEIGHT_ORDER = ("w_in", "ssd_conv_w", "ssd_conv_b", "ssd_dt_bias", "ssd_A_log", "ssd_D", "ssd_norm_w", "mla_q_norm_w", "mla_w_q_b",
                 "mla_kv_norm_w", "mla_w_kv_b", "mla_out_norm_w", "w_out", "ln_mix_g", "ln_mix_b", "w_ffn_gate", "w_ffn_up",
                 "w_ffn_down", "w_ple_gate", "w_ple_proj", "ln_ffn_g", "ln_ffn_b")


def _rows128(a):
    flat = a.reshape(-1)
    return jnp.pad(flat, (0, -flat.shape[0] % LANE)).reshape(-1, LANE)


def kernel(x, p, positions, w_in, ssd_conv_w, ssd_conv_b, ssd_dt_bias, ssd_A_log, ssd_D, ssd_norm_w, mla_q_norm_w, mla_w_q_b, mla_kv_norm_w, mla_w_kv_b, mla_out_norm_w, w_out, ln_mix_g, ln_mix_b, w_ffn_gate, w_ffn_up, w_ffn_down, w_ple_gate, w_ple_proj, ln_ffn_g, ln_ffn_b, loss_target, m_w_in, m_ssd_conv_w, m_ssd_conv_b, m_ssd_dt_bias, m_ssd_A_log, m_ssd_D, m_ssd_norm_w, m_mla_q_norm_w, m_mla_w_q_b, m_mla_kv_norm_w, m_mla_w_kv_b, m_mla_out_norm_w, m_w_out, m_ln_mix_g, m_ln_mix_b, m_w_ffn_gate, m_w_ffn_up, m_w_ffn_down, m_w_ple_gate, m_w_ple_proj, m_ln_ffn_g, m_ln_ffn_b, v_w_in, v_ssd_conv_w, v_ssd_conv_b, v_ssd_dt_bias, v_ssd_A_log, v_ssd_D, v_ssd_norm_w, v_mla_q_norm_w, v_mla_w_q_b, v_mla_kv_norm_w, v_mla_w_kv_b, v_mla_out_norm_w, v_w_out, v_ln_mix_g, v_ln_mix_b, v_w_ffn_gate, v_w_ffn_up, v_w_ffn_down, v_w_ple_gate, v_w_ple_proj, v_ln_ffn_g, v_ln_ffn_b):
    given = dict(locals())
    chip = 2 * lax.axis_index("x") + lax.axis_index("y")

    early = [name for name, _, _ in BIG if name not in LATE]
    shards = [given[_BIG_ARG[name]][0].astype(BF16) for name in early]
    conv_bits = lax.bitcast_convert_type(ssd_conv_w[0], BF16).reshape(SSD_K, 2 * CONV_SHARD)
    shards.append(jnp.pad(conv_bits, ((0, BF16_ROWS - SSD_K), (0, 0))))
    gathered = _gather_weights(shards)
    gw = dict(zip(early, gathered))
    conv_all = lax.bitcast_convert_type(gathered[-1][:, :SSD_K].reshape(NCHIP, SSD_K, CONV_SHARD, 2), F32)
    sp = {k: given[a] for k, a in _SMALL_ARG.items() if k != "conv_w"}
    sp["conv_w"] = _from_cols(conv_all)
    late_shards = [given[_BIG_ARG[name]][0].astype(BF16) for name in LATE]
    in_flight = _gather_late_start(late_shards, gathered[0])

    def late_weights(after):
        return dict(zip(LATE, _gather_late_finish(_gather_late_wait(*in_flight, after), late_shards)))

    loss_row, grad_x, g = _local_step(x[0], p[0, 0], positions[0], loss_target[0], gw, late_weights, sp)

    gbig = {name: arr for (name, _, _), arr in zip(BIG, _reduce_grads([g[name] for name, _, _ in BIG]))}
    small_in = jnp.concatenate([_rows128(g[name]) for name, _ in SMALL] + [loss_row], axis=0)
    small_sum = _allreduce_small(jnp.pad(small_in, ((0, SMALL_ROWS - small_in.shape[0]), (0, 0))))
    gsmall, row = {}, 0
    for name, size in SMALL:
        nrow = -(-size // LANE)
        gsmall[name] = small_sum[row:row + nrow].reshape(-1)[:size]
        row += nrow
    loss = small_sum[row, 0]

    grads = {}
    for name, shape, axis in BIG:
        grads[_BIG_ARG[name]] = gbig[name][None]
    for name, _ in SMALL:
        if name == "conv_w":
            full_g = gsmall[name].reshape(SSD_K, SSD_XBC)
            grads["ssd_conv_w"] = lax.dynamic_slice(full_g, (0, chip * CONV_SHARD), (SSD_K, CONV_SHARD))[None]
        else:
            grads[_SMALL_ARG[name]] = gsmall[name].reshape(given[_SMALL_ARG[name]].shape)

    delta, new_m, new_v = {}, {}, {}
    for name, _, _ in BIG:
        a = _BIG_ARG[name]
        d, m2, v2 = _adamw_big(given[a][0], grads[a][0], given["m_" + a][0], given["v_" + a][0], name="adamw_" + a)
        delta[a], new_m[a], new_v[a] = d[None], m2[None], v2[None]
    small_names = [_SMALL_ARG[name] for name, _ in SMALL]
    two_d = lambda t: t.reshape(t.shape[-2], t.shape[-1])
    ds, ms, vs = _adamw_small([two_d(given[a]) for a in small_names], [two_d(grads[a]) for a in small_names],
                              [two_d(given["m_" + a]) for a in small_names], [two_d(given["v_" + a]) for a in small_names])
    for a, d, m2, v2 in zip(small_names, ds, ms, vs):
        delta[a], new_m[a], new_v[a] = (t.reshape(given[a].shape) for t in (d, m2, v2))

    return (loss, grad_x[None], *[grads[n] for n in _WEIGHT_ORDER], *[delta[n] for n in _WEIGHT_ORDER],
            *[new_m[n] for n in _WEIGHT_ORDER], *[new_v[n] for n in _WEIGHT_ORDER])
```

```python
import functools
import math

import jax
import jax.numpy as jnp
from jax import lax
from jax.experimental import pallas as pl
from jax.experimental.pallas import tpu as pltpu

F32 = jnp.float32
BF16 = jnp.bfloat16

S = 2048
D = 1024
PLE = 256
H = 16
SSD_P = 64
SSD_INNER = 1024
SSD_N = 128
SSD_G = 2
SSD_L = 128
SSD_NC = S // SSD_L
SSD_XBC = 1536
SSD_K = 4
Q_RANK = 384
KV_RANK = 256
NOPE = 64
ROPE = 32
VDIM = 64
D_FF = 2816
IN_WIDTH = 3248
ALPHA = 2.0 ** 0.25
EPS_RMS = 1e-6
EPS_LN = 1e-5
ATT_SCALE = 1.0 / math.sqrt(NOPE + ROPE)
LN2 = math.log(2.0)
ATT_SCALE_LOG2 = ATT_SCALE / LN2
LANE = 128
NCHIP = 4
SMALL_W = 896
SM_Q, SM_KV, SM_KR, SM_DT = 0, 384, 640, 768
NEG = -1e30

ADAM_LR = 0.001
ADAM_B1 = 0.9
ADAM_B2 = 0.999
ADAM_EPS = 1e-08
ADAM_WD = 0.01
ADAM_STEP = 10


def _sigmoid(v):
    return 1.0 / (1.0 + jnp.exp(-v))


MM_VMEM_BUDGET = 36 * 2 ** 20
MM_MAX_ACC = 2048 * 1024


def _mm_tiles(pairs, ta, tb, m, n, out_dtype, has_add):
    def divs(v):
        return [LANE * d for d in range(v // LANE, 0, -1) if (v // LANE) % d == 0] if v % LANE == 0 else [v]

    def cost(tm, tn):
        tot = tm * tn * (jnp.dtype(out_dtype).itemsize + (4 if has_add else 0))
        for a, b in pairs:
            k = a.shape[-2] if ta else a.shape[-1]
            tot += k * (tm * a.dtype.itemsize + tn * b.dtype.itemsize)
        return 2 * tot

    ok = [(tm * tn, tm, tn) for tm in divs(m) for tn in divs(n) if tm * tn <= MM_MAX_ACC and cost(tm, tn) <= MM_VMEM_BUDGET]
    _, tm, tn = max(ok)
    return tm, tn


def _mm(pairs, *, ta=False, tb=False, out_dtype=F32, add=None, add_scale=1.0, chunk=None, name):
    n_pairs = len(pairs)
    a0, b0 = pairs[0]
    m = a0.shape[-1] if ta else a0.shape[-2]
    n = b0.shape[-2] if tb else b0.shape[-1]
    tm, tn = _mm_tiles(pairs, ta, tb, m, n, out_dtype, add is not None)
    dims = (((0 if ta else 1,), (1 if tb else 0,)), ((), ()))
    nk = NCHIP if chunk else 1
    assert chunk != "sum" or out_dtype == F32

    def body(*refs):
        o_ref = refs[-1]
        acc = None
        for i in range(n_pairs):
            a = refs[2 * i][...].astype(BF16)
            b = refs[2 * i + 1][...].astype(BF16)
            part = lax.dot_general(a, b, dims, preferred_element_type=F32)
            acc = part if acc is None else acc + part
        if chunk == "sum":
            k = pl.program_id(2)

            @pl.when(k == 0)
            def _():
                o_ref[...] = acc + add_scale * refs[2 * n_pairs][...] if add is not None else acc

            @pl.when(k > 0)
            def _():
                o_ref[...] += acc
        else:
            if add is not None:
                acc = acc + add_scale * refs[2 * n_pairs][...]
            o_ref[...] = acc.astype(out_dtype)

    def spec(arr, shape, idx2):
        if arr.ndim == 3:
            return pl.BlockSpec((None,) + shape, lambda i, j, k: (k,) + idx2(i, j))
        return pl.BlockSpec(shape, lambda i, j, k: idx2(i, j))

    in_specs, args = [], []
    for a, b in pairs:
        kdim = a.shape[-2] if ta else a.shape[-1]
        in_specs.append(spec(a, (kdim, tm), lambda i, j: (0, i)) if ta else spec(a, (tm, kdim), lambda i, j: (i, 0)))
        in_specs.append(spec(b, (tn, kdim), lambda i, j: (j, 0)) if tb else spec(b, (kdim, tn), lambda i, j: (0, j)))
        args += [a, b]
    if add is not None:
        in_specs.append(pl.BlockSpec((tm, tn), lambda i, j, k: (i, j)))
        args.append(add)
    if chunk == "out":
        out_spec = pl.BlockSpec((None, tm, tn), lambda i, j, k: (k, i, j))
        out_shape = jax.ShapeDtypeStruct((nk, m, n), out_dtype)
    else:
        out_spec = pl.BlockSpec((tm, tn), lambda i, j, k: (i, j))
        out_shape = jax.ShapeDtypeStruct((m, n), out_dtype)
    return pl.pallas_call(
        body, name=name, grid=(m // tm, n // tn, nk), in_specs=in_specs, out_specs=out_spec, out_shape=out_shape,
        compiler_params=pltpu.CompilerParams(dimension_semantics=("parallel", "parallel", "arbitrary")),
    )(*args)


TR = 256


def _row_spec(c):
    return pl.BlockSpec((TR, c), lambda i: (i, 0))


def _vec_spec(c):
    return pl.BlockSpec((1, c), lambda i: (0, 0))


def _acc_rows(ref, val):
    @pl.when(pl.program_id(0) == 0)
    def _():
        ref[...] = jnp.zeros_like(ref)
    ref[...] += val


def _rms_fwd(u, w, *, name):
    c = u.shape[1]

    def body(u_ref, w_ref, o_ref):
        v = u_ref[...]
        r = lax.rsqrt(jnp.mean(v * v, axis=-1, keepdims=True) + EPS_RMS)
        o_ref[...] = (v * r * w_ref[...]).astype(BF16)

    return pl.pallas_call(body, name=name, grid=(S // TR,), in_specs=[_row_spec(c), _vec_spec(c)], out_specs=_row_spec(c),
                          out_shape=jax.ShapeDtypeStruct((S, c), BF16))(u, w)


def _rms_bwd(u, w, dy, *, name):
    c = u.shape[1]

    def body(u_ref, w_ref, dy_ref, du_ref, dw_ref):
        v = u_ref[...]
        g = dy_ref[...].astype(F32)
        r = lax.rsqrt(jnp.mean(v * v, axis=-1, keepdims=True) + EPS_RMS)
        gw = g * w_ref[...]
        du_ref[...] = r * gw - v * (r * r * r * jnp.mean(gw * v, axis=-1, keepdims=True))
        _acc_rows(dw_ref, jnp.sum(g * v * r, axis=0, keepdims=True))

    return pl.pallas_call(body, name=name, grid=(S // TR,), in_specs=[_row_spec(c), _vec_spec(c), _row_spec(c)],
                          out_specs=[_row_spec(c), _vec_spec(c)],
                          out_shape=[jax.ShapeDtypeStruct((S, c), F32), jax.ShapeDtypeStruct((1, c), F32)])(u, w, dy)


def _gate_norm_fwd(y, z, w):
    def body(y_ref, z_ref, w_ref, o_ref):
        zz = z_ref[...]
        v = y_ref[...] * (zz * _sigmoid(zz))
        r = lax.rsqrt(jnp.mean(v * v, axis=-1, keepdims=True) + EPS_RMS)
        o_ref[...] = (v * r * w_ref[...]).astype(BF16)

    c = SSD_INNER
    return pl.pallas_call(body, name="ssd_gate_norm_fwd", grid=(S // TR,), in_specs=[_row_spec(c), _row_spec(c), _vec_spec(c)],
                          out_specs=_row_spec(c), out_shape=jax.ShapeDtypeStruct((S, c), BF16))(y, z, w)


def _gate_norm_bwd(y, z, w, dout):
    def body(y_ref, z_ref, w_ref, g_ref, dy_ref, dz_ref, dw_ref):
        yy = y_ref[...]
        zz = z_ref[...]
        sg = _sigmoid(zz)
        sz = zz * sg
        v = yy * sz
        g = g_ref[...]
        r = lax.rsqrt(jnp.mean(v * v, axis=-1, keepdims=True) + EPS_RMS)
        gw = g * w_ref[...]
        dv = r * gw - v * (r * r * r * jnp.mean(gw * v, axis=-1, keepdims=True))
        dy_ref[...] = dv * sz
        dz_ref[...] = (dv * yy * (sg * (1.0 + zz * (1.0 - sg)))).astype(BF16)
        _acc_rows(dw_ref, jnp.sum(g * v * r, axis=0, keepdims=True))

    c = SSD_INNER
    return pl.pallas_call(body, name="ssd_gate_norm_bwd", grid=(S // TR,),
                          in_specs=[_row_spec(c), _row_spec(c), _vec_spec(c), _row_spec(c)],
                          out_specs=[_row_spec(c), _row_spec(c), _vec_spec(c)],
                          out_shape=[jax.ShapeDtypeStruct((S, c), F32), jax.ShapeDtypeStruct((S, c), BF16),
                                     jax.ShapeDtypeStruct((1, c), F32)])(y, z, w, dout)


def _ln_fwd(xr, mix, g, b):
    def body(x_ref, m_ref, g_ref, b_ref, o_ref, ob_ref):
        pre = ALPHA * x_ref[...] + m_ref[...]
        mu = jnp.mean(pre, axis=-1, keepdims=True)
        d = pre - mu
        rs = lax.rsqrt(jnp.mean(d * d, axis=-1, keepdims=True) + EPS_LN)
        h = d * rs * g_ref[...] + b_ref[...]
        o_ref[...] = h
        ob_ref[...] = h.astype(BF16)

    return pl.pallas_call(body, name="ln_mix_fwd", grid=(S // TR,), in_specs=[_row_spec(D), _row_spec(D), _vec_spec(D), _vec_spec(D)],
                          out_specs=[_row_spec(D)] * 2,
                          out_shape=[jax.ShapeDtypeStruct((S, D), F32), jax.ShapeDtypeStruct((S, D), BF16)])(xr, mix, g, b)


def _ln_bwd(xr, mix, g, dh):
    def body(x_ref, m_ref, g_ref, dh_ref, dpre_ref, dpreb_ref, dg_ref, db_ref):
        pre = ALPHA * x_ref[...] + m_ref[...]
        mu = jnp.mean(pre, axis=-1, keepdims=True)
        d = pre - mu
        rs = lax.rsqrt(jnp.mean(d * d, axis=-1, keepdims=True) + EPS_LN)
        xh = d * rs
        dy = dh_ref[...]
        gy = dy * g_ref[...]
        dpre = rs * (gy - jnp.mean(gy, axis=-1, keepdims=True) - xh * jnp.mean(gy * xh, axis=-1, keepdims=True))
        dpre_ref[...] = dpre
        dpreb_ref[...] = dpre.astype(BF16)
        _acc_rows(dg_ref, jnp.sum(dy * xh, axis=0, keepdims=True))
        _acc_rows(db_ref, jnp.sum(dy, axis=0, keepdims=True))

    return pl.pallas_call(body, name="ln_mix_bwd", grid=(S // TR,),
                          in_specs=[_row_spec(D), _row_spec(D), _vec_spec(D), _row_spec(D)],
                          out_specs=[_row_spec(D), _row_spec(D), _vec_spec(D), _vec_spec(D)],
                          out_shape=[jax.ShapeDtypeStruct((S, D), F32), jax.ShapeDtypeStruct((S, D), BF16),
                                     jax.ShapeDtypeStruct((1, D), F32), jax.ShapeDtypeStruct((1, D), F32)])(xr, mix, g, dh)


FF_CHUNK = D_FF // NCHIP


def _ff_spec():
    return pl.BlockSpec((None, TR * 2, FF_CHUNK), lambda k, i: (k, i, 0))


def _swiglu_fwd(gate, up):
    def body(g_ref, u_ref, o_ref):
        g = g_ref[...]
        o_ref[...] = (g * _sigmoid(g) * u_ref[...]).astype(BF16)

    return pl.pallas_call(body, name="swiglu_fwd", grid=(NCHIP, S // (2 * TR)), in_specs=[_ff_spec()] * 2, out_specs=_ff_spec(),
                          out_shape=jax.ShapeDtypeStruct((NCHIP, S, FF_CHUNK), BF16))(gate, up)


def _swiglu_bwd(gate, up, dact):
    def body(g_ref, u_ref, d_ref, dg_ref, du_ref):
        g = g_ref[...]
        sg = _sigmoid(g)
        d = d_ref[...]
        dg_ref[...] = (d * u_ref[...] * (sg * (1.0 + g * (1.0 - sg)))).astype(BF16)
        du_ref[...] = (d * g * sg).astype(BF16)

    return pl.pallas_call(body, name="swiglu_bwd", grid=(NCHIP, S // (2 * TR)), in_specs=[_ff_spec()] * 3, out_specs=[_ff_spec()] * 2,
                          out_shape=[jax.ShapeDtypeStruct((NCHIP, S, FF_CHUNK), BF16)] * 2)(gate, up, dact)


def _final_fwd_bwd(h1, ffn, pg, pp, target, g2, b2):
    def body(h_ref, f_ref, pg_ref, pp_ref, t_ref, g_ref, b_ref, dpre_ref, dpreb_ref, dpg_ref, dpp_ref, dg_ref, db_ref, loss_ref):
        sg = _sigmoid(pg_ref[...])
        ppv = pp_ref[...]
        pre = ALPHA * h_ref[...] + f_ref[...] + sg * ppv
        mu = jnp.mean(pre, axis=-1, keepdims=True)
        d = pre - mu
        rs = lax.rsqrt(jnp.mean(d * d, axis=-1, keepdims=True) + EPS_LN)
        xh = d * rs
        err = xh * g_ref[...] + b_ref[...] - t_ref[...]
        dy = err * (1.0 / D)
        gy = dy * g_ref[...]
        dpre = rs * (gy - jnp.mean(gy, axis=-1, keepdims=True) - xh * jnp.mean(gy * xh, axis=-1, keepdims=True))
        dpre_ref[...] = dpre
        dpreb_ref[...] = dpre.astype(BF16)
        dpg_ref[...] = (dpre * ppv * sg * (1.0 - sg)).astype(BF16)
        dpp_ref[...] = (dpre * sg).astype(BF16)
        _acc_rows(dg_ref, jnp.sum(dy * xh, axis=0, keepdims=True))
        _acc_rows(db_ref, jnp.sum(dy, axis=0, keepdims=True))
        _acc_rows(loss_ref, 0.5 * jnp.sum(jnp.mean(err * err, axis=-1, keepdims=True), axis=0, keepdims=True) * jnp.ones((1, LANE), F32))

    return pl.pallas_call(
        body, name="final_ln_loss", grid=(S // TR,),
        in_specs=[_row_spec(D)] * 5 + [_vec_spec(D)] * 2,
        out_specs=[_row_spec(D)] * 4 + [_vec_spec(D), _vec_spec(D), _vec_spec(LANE)],
        out_shape=[jax.ShapeDtypeStruct((S, D), F32)] + [jax.ShapeDtypeStruct((S, D), BF16)] * 3 + [
                   jax.ShapeDtypeStruct((1, D), F32), jax.ShapeDtypeStruct((1, D), F32), jax.ShapeDtypeStruct((1, LANE), F32)],
    )(h1, ffn, pg, pp, target, g2, b2)


def _rot(u, cos_t, sin_t, lane):
    partner = jnp.where(lane < NOPE + ROPE // 2, pltpu.roll(u, LANE - ROPE // 2, 1), pltpu.roll(u, ROPE // 2, 1))
    return u * cos_t + partner * sin_t


def _q_rope(qlin, cos_t, sin_t):
    def body(q_ref, c_ref, s_ref, o_ref):
        lane = lax.broadcasted_iota(jnp.int32, (TR, LANE), 1)
        c, s = c_ref[...], s_ref[...]
        for h in range(H):
            o_ref[:, h * LANE:(h + 1) * LANE] = _rot(q_ref[:, h * LANE:(h + 1) * LANE], c, s, lane).astype(BF16)

    w = H * LANE
    return pl.pallas_call(body, name="q_rope", grid=(S // TR,), in_specs=[_row_spec(w), _row_spec(LANE), _row_spec(LANE)],
                          out_specs=_row_spec(w), out_shape=jax.ShapeDtypeStruct((S, w), BF16))(qlin, cos_t, sin_t)


def _q_unrope(dq, cos_t, sin_t):
    def body(q_ref, c_ref, s_ref, o_ref):
        lane = lax.broadcasted_iota(jnp.int32, (TR, LANE), 1)
        c, s = c_ref[...], -s_ref[...]
        for h in range(H):
            o_ref[:, h * LANE:(h + 1) * LANE] = _rot(q_ref[:, h * LANE:(h + 1) * LANE], c, s, lane).astype(BF16)

    w = H * LANE
    return pl.pallas_call(body, name="q_unrope", grid=(S // TR,), in_specs=[_row_spec(w), _row_spec(LANE), _row_spec(LANE)],
                          out_specs=_row_spec(w), out_shape=jax.ShapeDtypeStruct((S, w), BF16))(dq, cos_t, sin_t)


def _k_prep(klin, small, cos_t, sin_t):
    def body(k_ref, kr_ref, c_ref, s_ref, o_ref):
        lane = lax.broadcasted_iota(jnp.int32, (TR, LANE), 1)
        kr = _rot(pltpu.roll(kr_ref[...], NOPE, 1), c_ref[...], s_ref[...], lane)
        for h in range(H):
            o_ref[:, h * LANE:(h + 1) * LANE] = (k_ref[:, h * LANE:(h + 1) * LANE] + kr).astype(BF16)

    w = H * LANE
    kr_spec = pl.BlockSpec((TR, LANE), lambda i: (i, SM_KR // LANE))
    return pl.pallas_call(body, name="k_prep", grid=(S // TR,), in_specs=[_row_spec(w), kr_spec, _row_spec(LANE), _row_spec(LANE)],
                          out_specs=_row_spec(w), out_shape=jax.ShapeDtypeStruct((S, w), BF16))(klin, small, cos_t, sin_t)


def _k_rope_bwd(dk, cos_t, sin_t):
    def body(k_ref, c_ref, s_ref, o_ref):
        lane = lax.broadcasted_iota(jnp.int32, (TR, LANE), 1)
        acc = k_ref[:, 0:LANE]
        for h in range(1, H):
            acc = acc + k_ref[:, h * LANE:(h + 1) * LANE]
        acc = jnp.where((lane >= NOPE) & (lane < NOPE + ROPE), acc, 0.0)
        o_ref[...] = pltpu.roll(_rot(acc, c_ref[...], -s_ref[...], lane), LANE - NOPE, 1)

    w = H * LANE
    return pl.pallas_call(body, name="k_rope_bwd", grid=(S // TR,), in_specs=[_row_spec(w), _row_spec(LANE), _row_spec(LANE)],
                          out_specs=_row_spec(LANE), out_shape=jax.ShapeDtypeStruct((S, LANE), F32))(dk, cos_t, sin_t)


CB = 256


def _shift_down(u, k, row):
    if k == 0:
        return u
    return jnp.where(row >= k, pltpu.roll(u, k, 0), 0.0)


def _shift_up(u, k, row):
    if k == 0:
        return u
    return jnp.where(row < S - k, pltpu.roll(u, S - k, 0), 0.0)


def _conv_fwd(u, w, b):
    def body(u_ref, w_ref, b_ref, o_ref):
        row = lax.broadcasted_iota(jnp.int32, (S, CB), 0)
        uu = u_ref[...]
        acc = b_ref[...] + w_ref[SSD_K - 1:SSD_K, :] * uu
        for k in range(SSD_K - 1):
            acc = acc + w_ref[k:k + 1, :] * _shift_down(uu, SSD_K - 1 - k, row)
        o_ref[...] = acc * _sigmoid(acc)

    c = u.shape[1]
    return pl.pallas_call(
        body, name="conv_fwd", grid=(c // CB,),
        in_specs=[pl.BlockSpec((S, CB), lambda j: (0, j)), pl.BlockSpec((SSD_K, CB), lambda j: (0, j)), pl.BlockSpec((1, CB), lambda j: (0, j))],
        out_specs=pl.BlockSpec((S, CB), lambda j: (0, j)), out_shape=jax.ShapeDtypeStruct((S, c), F32),
    )(u, w, b)


def _conv_bwd(u, w, b, dact):
    def body(u_ref, w_ref, b_ref, d_ref, du_ref, dw_ref, db_ref):
        row = lax.broadcasted_iota(jnp.int32, (S, CB), 0)
        uu = u_ref[...]
        sh = [_shift_down(uu, SSD_K - 1 - k, row) for k in range(SSD_K)]
        acc = b_ref[...]
        for k in range(SSD_K):
            acc = acc + w_ref[k:k + 1, :] * sh[k]
        sg = _sigmoid(acc)
        dacc = d_ref[...] * (sg * (1.0 + acc * (1.0 - sg)))
        du = w_ref[SSD_K - 1:SSD_K, :] * dacc
        for k in range(SSD_K - 1):
            du = du + w_ref[k:k + 1, :] * _shift_up(dacc, SSD_K - 1 - k, row)
        du_ref[...] = du.astype(BF16)
        for k in range(SSD_K):
            dw_ref[k:k + 1, :] = jnp.sum(dacc * sh[k], axis=0, keepdims=True)
        db_ref[...] = jnp.sum(dacc, axis=0, keepdims=True)

    c = u.shape[1]
    col = lambda r: pl.BlockSpec((r, CB), lambda j: (0, j))
    return pl.pallas_call(
        body, name="conv_bwd", grid=(c // CB,), in_specs=[col(S), col(SSD_K), col(1), col(S)], out_specs=[col(S), col(SSD_K), col(1)],
        out_shape=[jax.ShapeDtypeStruct((S, c), BF16), jax.ShapeDtypeStruct((SSD_K, c), F32), jax.ShapeDtypeStruct((1, c), F32)],
    )(u, w, b, dact)


NPAIR = H // 2
PAIRS_PER_GROUP = NPAIR // SSD_G


def _softplus(v):
    return jnp.maximum(v, 0.0) + jnp.log(1.0 + jnp.exp(-jnp.abs(v)))


def _dot(a, b, dims):
    return lax.dot_general(a.astype(BF16), b.astype(BF16), (dims, ((), ())), preferred_element_type=F32)


def _dot3(a, b, dims, split_lhs):
    v = a if split_lhs else b
    v1 = v.astype(BF16)
    r1 = v - v1.astype(F32)
    v2 = r1.astype(BF16)
    v3 = (r1 - v2.astype(F32)).astype(BF16)
    acc = None
    for part in (v1, v2, v3):
        lhs, rhs = (part, b) if split_lhs else (a, part)
        t = lax.dot_general(lhs, rhs, (dims, ((), ())), preferred_element_type=F32)
        acc = t if acc is None else acc + t
    return acc


def _ssd_chunk_common(dt_ref, dtT_ref, prow_ref, pcol_ref):
    prow = prow_ref[...]
    pcol = pcol_ref[...]
    ri = lax.broadcasted_iota(jnp.int32, (SSD_L, SSD_L), 0)
    ci = lax.broadcasted_iota(jnp.int32, (SSD_L, SSD_L), 1)
    causal = ri >= ci
    pre_c = dt_ref[...] + prow[0:1, :]
    dtc = _softplus(pre_c)
    a_row = -jnp.exp(prow[1:2, :])
    cs_col = _dot3(causal.astype(BF16), dtc * a_row, ((1,), (0,)), False)
    dtr = _softplus(dtT_ref[...] + pcol[:, 0:1])
    a_col = -jnp.exp(pcol[:, 1:2])
    cs_row = _dot3(dtr * a_col, (ri <= ci).astype(BF16), ((1,), (0,)), True)
    return prow, causal, pre_c, dtc, a_row, cs_col, cs_row


def _ssd_fwd(act, small, dtT, prow, pcol):
    def body(x_ref, b_ref, c_ref, dt_ref, dtT_ref, prow_ref, pcol_ref, y_ref, st_ref, state):
        @pl.when(pl.program_id(0) == 0)
        def _():
            state[...] = jnp.zeros_like(state)

        prow, causal, _, dtc, _, cs_col, cs_row = _ssd_chunk_common(dt_ref, dtT_ref, prow_ref, pcol_ref)
        lo = lax.broadcasted_iota(jnp.int32, (SSD_L, LANE), 1) < SSD_P
        lo1 = lo[0:1, :]
        for g in range(SSD_G):
            bm = b_ref[:, g * SSD_N:(g + 1) * SSD_N]
            cm = c_ref[:, g * SSD_N:(g + 1) * SSD_N]
            cb = _dot(cm, bm, ((1,), (1,)))
            for qq in range(PAIRS_PER_GROUP):
                q = g * PAIRS_PER_GROUP + qq
                ha, hb = 2 * q, 2 * q + 1
                csa, csb = cs_col[:, ha:ha + 1], cs_col[:, hb:hb + 1]
                xp = x_ref[:, q * LANE:(q + 1) * LANE]
                xx = xp * jnp.where(lo, dtc[:, ha:ha + 1], dtc[:, hb:hb + 1])
                ga = cb * jnp.exp(jnp.where(causal, csa - cs_row[ha:ha + 1, :], NEG))
                gb = cb * jnp.exp(jnp.where(causal, csb - cs_row[hb:hb + 1, :], NEG))
                y = _dot(ga, jnp.where(lo, xx, 0.0), ((1,), (0,))) + _dot(gb, jnp.where(lo, 0.0, xx), ((1,), (0,)))
                s_in = state[q]
                y = y + _dot(cm, s_in, ((1,), (0,))) * jnp.where(lo, jnp.exp(csa), jnp.exp(csb))
                y = y + jnp.where(lo1, prow[2:3, ha:ha + 1], prow[2:3, hb:hb + 1]) * xp
                y_ref[:, q * LANE:(q + 1) * LANE] = y
                la, lb = csa[SSD_L - 1:SSD_L, :], csb[SSD_L - 1:SSD_L, :]
                decay = jnp.where(lo, jnp.exp(la - csa), jnp.exp(lb - csb))
                st_ref[q] = s_in
                state[q] = s_in * jnp.where(lo1, jnp.exp(la), jnp.exp(lb)) + _dot(bm, xx * decay, ((0,), (0,)))

    L = SSD_L
    return pl.pallas_call(
        body, name="ssd_fwd", grid=(SSD_NC,),
        in_specs=[pl.BlockSpec((L, SSD_INNER), lambda c: (c, 0)),
                  pl.BlockSpec((L, SSD_G * SSD_N), lambda c: (c, SSD_INNER // (SSD_G * SSD_N))),
                  pl.BlockSpec((L, SSD_G * SSD_N), lambda c: (c, SSD_INNER // (SSD_G * SSD_N) + 1)),
                  pl.BlockSpec((L, LANE), lambda c: (c, SM_DT // LANE)),
                  pl.BlockSpec((LANE, L), lambda c: (0, c)),
                  pl.BlockSpec((8, LANE), lambda c: (0, 0)), pl.BlockSpec((LANE, 8), lambda c: (0, 0))],
        out_specs=[pl.BlockSpec((L, SSD_INNER), lambda c: (c, 0)),
                   pl.BlockSpec((None, NPAIR, SSD_N, LANE), lambda c: (c, 0, 0, 0))],
        out_shape=[jax.ShapeDtypeStruct((S, SSD_INNER), F32), jax.ShapeDtypeStruct((SSD_NC, NPAIR, SSD_N, LANE), F32)],
        scratch_shapes=[pltpu.VMEM((NPAIR, SSD_N, LANE), F32)],
        compiler_params=pltpu.CompilerParams(dimension_semantics=("arbitrary",)),
    )(act, act, act, small, dtT, prow, pcol)


def _ssd_bwd(act, small, dtT, prow, pcol, states, dy):
    def body(x_ref, b_ref, c_ref, dt_ref, dtT_ref, prow_ref, pcol_ref, st_ref, dy_ref,
             dx_ref, ddt_ref, dp_ref, dstate):
        @pl.when(pl.program_id(0) == 0)
        def _():
            dstate[...] = jnp.zeros_like(dstate)
            dp_ref[...] = jnp.zeros_like(dp_ref)

        prow, causal, pre_c, dtc, a_row, cs_col, cs_row = _ssd_chunk_common(dt_ref, dtT_ref, prow_ref, pcol_ref)
        lane = lax.broadcasted_iota(jnp.int32, (SSD_L, LANE), 1)
        sub = lax.broadcasted_iota(jnp.int32, (LANE, SSD_L), 0)
        rowi = lax.broadcasted_iota(jnp.int32, (SSD_L, 1), 0)
        lane1 = lane[0:1, :]
        lo = lane < SSD_P
        lo1 = lo[0:1, :]
        dcs_c = jnp.zeros((SSD_L, LANE), F32)
        dcs_r = jnp.zeros((LANE, SSD_L), F32)
        ddt_x = jnp.zeros((SSD_L, LANE), F32)
        dd_row = jnp.zeros((1, LANE), F32)
        for g in range(SSD_G):
            bm = b_ref[:, g * SSD_N:(g + 1) * SSD_N]
            cm = c_ref[:, g * SSD_N:(g + 1) * SSD_N]
            cb = _dot(cm, bm, ((1,), (1,)))
            dcb = jnp.zeros((SSD_L, SSD_L), F32)
            dbm = jnp.zeros((SSD_L, SSD_N), F32)
            dcm = jnp.zeros((SSD_L, SSD_N), F32)
            for qq in range(PAIRS_PER_GROUP):
                q = g * PAIRS_PER_GROUP + qq
                ha, hb = 2 * q, 2 * q + 1
                csa, csb = cs_col[:, ha:ha + 1], cs_col[:, hb:hb + 1]
                xp = x_ref[:, q * LANE:(q + 1) * LANE]
                dtp = jnp.where(lo, dtc[:, ha:ha + 1], dtc[:, hb:hb + 1])
                xx = xp * dtp
                lma = jnp.exp(jnp.where(causal, csa - cs_row[ha:ha + 1, :], NEG))
                lmb = jnp.exp(jnp.where(causal, csb - cs_row[hb:hb + 1, :], NEG))
                ga, gb = cb * lma, cb * lmb
                dyp = dy_ref[:, q * LANE:(q + 1) * LANE]
                dya, dyb = jnp.where(lo, dyp, 0.0), jnp.where(lo, 0.0, dyp)
                s_in = st_ref[q]
                ds_out = dstate[q]
                la, lb = csa[SSD_L - 1:SSD_L, :], csb[SSD_L - 1:SSD_L, :]
                ecs = jnp.where(lo, jnp.exp(csa), jnp.exp(csb))
                decay = jnp.where(lo, jnp.exp(la - csa), jnp.exp(lb - csb))
                cd = jnp.where(lo1, jnp.exp(la), jnp.exp(lb))
                bds = _dot(bm, ds_out, ((1,), (0,)))
                dxx = _dot(ga, dya, ((0,), (0,))) + _dot(gb, dyb, ((0,), (0,))) + bds * decay
                dga = _dot(dya, xx, ((1,), (1,)))
                dgb = _dot(dyb, xx, ((1,), (1,)))
                dsega, dsegb = dga * ga, dgb * gb
                dcb = dcb + dga * lma + dgb * lmb
                yoff = _dot(cm, s_in, ((1,), (0,))) * ecs
                dye = dyp * ecs
                dcm = dcm + _dot(dye, s_in, ((1,), (1,)))
                xd = xx * decay
                dbm = dbm + _dot(xd, ds_out, ((1,), (1,)))
                wv = xd * bds
                t1 = dyp * yoff - wv
                col_a = (jnp.sum(dsega, axis=1, keepdims=True) + jnp.sum(jnp.where(lo, t1, 0.0), axis=1, keepdims=True))
                col_b = (jnp.sum(dsegb, axis=1, keepdims=True) + jnp.sum(jnp.where(lo, 0.0, t1), axis=1, keepdims=True))
                sprod = ds_out * s_in
                end_a = jnp.sum(jnp.where(lo, wv, 0.0), keepdims=True) + jnp.exp(la) * jnp.sum(jnp.where(lo[:SSD_N], sprod, 0.0), keepdims=True)
                end_b = jnp.sum(jnp.where(lo, 0.0, wv), keepdims=True) + jnp.exp(lb) * jnp.sum(jnp.where(lo[:SSD_N], 0.0, sprod), keepdims=True)
                col_a = col_a + jnp.where(rowi == SSD_L - 1, end_a, 0.0)
                col_b = col_b + jnp.where(rowi == SSD_L - 1, end_b, 0.0)
                dcs_c = dcs_c + jnp.where(lane == ha, col_a, 0.0) + jnp.where(lane == hb, col_b, 0.0)
                dcs_r = (dcs_r + jnp.where(sub == ha, jnp.sum(dsega, axis=0, keepdims=True), 0.0)
                         + jnp.where(sub == hb, jnp.sum(dsegb, axis=0, keepdims=True), 0.0))
                dstate[q] = _dot(cm, dye, ((0,), (0,))) + cd * ds_out
                dpair = jnp.where(lo1, prow[2:3, ha:ha + 1], prow[2:3, hb:hb + 1])
                dx_ref[:, q * LANE:(q + 1) * LANE] = dxx * dtp + dpair * dyp
                t2 = dxx * xp
                ddt_x = (ddt_x + jnp.where(lane == ha, jnp.sum(jnp.where(lo, t2, 0.0), axis=1, keepdims=True), 0.0)
                         + jnp.where(lane == hb, jnp.sum(jnp.where(lo, 0.0, t2), axis=1, keepdims=True), 0.0))
                t3 = dyp * xp
                dd_row = (dd_row + jnp.where(lane1 == ha, jnp.sum(jnp.where(lo, t3, 0.0), keepdims=True), 0.0)
                          + jnp.where(lane1 == hb, jnp.sum(jnp.where(lo, 0.0, t3), keepdims=True), 0.0))
            dx_ref[:, SSD_INNER + g * SSD_N:SSD_INNER + (g + 1) * SSD_N] = dbm + _dot(dcb, cm, ((0,), (0,)))
            dx_ref[:, SSD_INNER + (SSD_G + g) * SSD_N:SSD_INNER + (SSD_G + g + 1) * SSD_N] = dcm + _dot(dcb, bm, ((1,), (0,)))
        ri = lax.broadcasted_iota(jnp.int32, (SSD_L, SSD_L), 0)
        ci = lax.broadcasted_iota(jnp.int32, (SSD_L, SSD_L), 1)
        da = _dot3((ri <= ci).astype(BF16), dcs_c, ((1,), (0,)), False)
        da = da - _dot3(dcs_r, causal.astype(BF16), ((1,), (0,)), True).T
        ddt = ddt_x + da * a_row
        ddt_raw = ddt * _sigmoid(pre_c)
        ddt_ref[...] = ddt_raw
        da_head = jnp.sum(da * dtc, axis=0, keepdims=True) * a_row
        dp_ref[0:1, :] += jnp.sum(ddt_raw, axis=0, keepdims=True)
        dp_ref[1:2, :] += da_head
        dp_ref[2:3, :] += dd_row

    L = SSD_L
    rev = SSD_NC - 1
    bc_cols = SSD_INNER // (SSD_G * SSD_N)
    return pl.pallas_call(
        body, name="ssd_bwd", grid=(SSD_NC,),
        in_specs=[pl.BlockSpec((L, SSD_INNER), lambda c: (rev - c, 0)),
                  pl.BlockSpec((L, SSD_G * SSD_N), lambda c: (rev - c, bc_cols)),
                  pl.BlockSpec((L, SSD_G * SSD_N), lambda c: (rev - c, bc_cols + 1)),
                  pl.BlockSpec((L, LANE), lambda c: (rev - c, SM_DT // LANE)),
                  pl.BlockSpec((LANE, L), lambda c: (0, rev - c)),
                  pl.BlockSpec((8, LANE), lambda c: (0, 0)), pl.BlockSpec((LANE, 8), lambda c: (0, 0)),
                  pl.BlockSpec((None, NPAIR, SSD_N, LANE), lambda c: (rev - c, 0, 0, 0)),
                  pl.BlockSpec((L, SSD_INNER), lambda c: (rev - c, 0))],
        out_specs=[pl.BlockSpec((L, SSD_XBC), lambda c: (rev - c, 0)),
                   pl.BlockSpec((L, LANE), lambda c: (rev - c, 0)),
                   pl.BlockSpec((8, LANE), lambda c: (0, 0))],
        out_shape=[jax.ShapeDtypeStruct((S, SSD_XBC), F32), jax.ShapeDtypeStruct((S, LANE), F32),
                   jax.ShapeDtypeStruct((8, LANE), F32)],
        scratch_shapes=[pltpu.VMEM((NPAIR, SSD_N, LANE), F32)],
        compiler_params=pltpu.CompilerParams(dimension_semantics=("arbitrary",)),
    )(act, act, act, small, dtT, prow, pcol, states, dy)


TQ = 256
TK = 256
FWD_TQ = 256
FWD_TK = 256


def _attn_fwd(qc, kc, v):
    TQ, TK = FWD_TQ, FWD_TK

    def body(q_ref, k_ref, v_ref, o_ref, lse_ref):
        i = pl.program_id(1)
        lo = lax.broadcasted_iota(jnp.int32, (TQ, LANE), 1) < VDIM
        lo_k = lax.broadcasted_iota(jnp.int32, (TK, LANE), 1) < VDIM
        row_minus_col = lax.broadcasted_iota(jnp.int32, (TQ, TK), 0) - lax.broadcasted_iota(jnp.int32, (TQ, TK), 1)
        qa, qb = q_ref[:, 0:LANE], q_ref[:, LANE:2 * LANE]

        def scores(kb):
            kk = k_ref[pl.ds(pl.multiple_of(kb * TK, TK), TK), :]
            return (_dot(qa, kk[:, 0:LANE], ((1,), (1,))) * ATT_SCALE_LOG2, _dot(qb, kk[:, LANE:2 * LANE], ((1,), (1,))) * ATT_SCALE_LOG2)

        def update(kb, sa, sb, stats):
            ma, la, mb, lb, acc = stats
            vv = v_ref[pl.ds(pl.multiple_of(kb * TK, TK), TK), :]
            na = jnp.maximum(ma, jnp.max(sa, axis=1, keepdims=True))
            nb = jnp.maximum(mb, jnp.max(sb, axis=1, keepdims=True))
            pa, pb = jnp.exp2(sa - na), jnp.exp2(sb - nb)
            fa, fb = jnp.exp2(ma - na), jnp.exp2(mb - nb)
            la = fa * la + jnp.sum(pa, axis=1, keepdims=True)
            lb = fb * lb + jnp.sum(pb, axis=1, keepdims=True)
            acc = (acc * jnp.where(lo, fa, fb) + _dot(pa, jnp.where(lo_k, vv, 0), ((1,), (0,)))
                   + _dot(pb, jnp.where(lo_k, 0, vv), ((1,), (0,))))
            return na, la, nb, lb, acc

        def step(kb, carry):
            sa, sb = carry[:2]
            nxt = scores(kb + 1)
            return nxt + update(kb, sa, sb, carry[2:])

        neg = jnp.full((TQ, 1), NEG, F32)
        zero = jnp.zeros((TQ, 1), F32)
        n_full = i * (TQ // TK)
        carry = lax.fori_loop(0, n_full, step, scores(0) + (neg, zero, neg, zero, jnp.zeros((TQ, LANE), F32)))
        s, stats = carry[:2], carry[2:]
        for d in range(TQ // TK):
            nxt = scores(n_full + d + 1) if d + 1 < TQ // TK else None
            sa, sb = (jnp.where(row_minus_col >= d * TK, t, NEG) for t in s)
            stats = update(n_full + d, sa, sb, stats)
            s = nxt
        ma, la, mb, lb, acc = stats
        o_ref[...] = acc / jnp.where(lo, la, lb)
        lse_ref[...] = jnp.where(lo, ma + jnp.log2(la), mb + jnp.log2(lb)) * LN2

    return pl.pallas_call(
        body, name="attn_fwd", grid=(NPAIR, S // TQ),
        in_specs=[pl.BlockSpec((TQ, 2 * LANE), lambda j, i: (i, j)), pl.BlockSpec((S, 2 * LANE), lambda j, i: (0, j)),
                  pl.BlockSpec((S, LANE), lambda j, i: (0, j))],
        out_specs=[pl.BlockSpec((TQ, LANE), lambda j, i: (i, j)), pl.BlockSpec((None, TQ, LANE), lambda j, i: (j, i, 0))],
        out_shape=[jax.ShapeDtypeStruct((S, H * VDIM), F32), jax.ShapeDtypeStruct((NPAIR, S, LANE), F32)],
        compiler_params=pltpu.CompilerParams(dimension_semantics=("parallel", "parallel")),
    )(qc, kc, v)


def _attn_bwd(qc, kc, v, o, lse, do):
    nq = S // TQ

    def body(q_ref, k_ref, v_ref, o_ref, lse_ref, do_ref, dq_ref, dk_ref, dv_ref):
        kb = pl.program_id(1)

        @pl.when(kb == 0)
        def _():
            dq_ref[...] = jnp.zeros_like(dq_ref)

        lo = lax.broadcasted_iota(jnp.int32, (TQ, LANE), 1) < VDIM
        r0 = lax.broadcasted_iota(jnp.int32, (TQ, TK), 0)
        ck = kb * TK + lax.broadcasted_iota(jnp.int32, (TQ, TK), 1)
        ka, kbb = k_ref[:, 0:LANE], k_ref[:, LANE:2 * LANE]
        vv = v_ref[...]

        def step(qi, carry):
            dka, dkb, dv = carry
            off = pl.multiple_of(qi * TQ, TQ)
            qq = q_ref[pl.ds(off, TQ), :]
            dd = do_ref[pl.ds(off, TQ), :]
            ls = lse_ref[pl.ds(off, TQ), :]
            t = dd * o_ref[pl.ds(off, TQ), :]
            mask = r0 + qi * TQ >= ck
            outs = []
            for x, (kx, lsx) in enumerate(((ka, ls[:, 0:1]), (kbb, ls[:, VDIM:VDIM + 1]))):
                sel = lo if x == 0 else jnp.logical_not(lo)
                qx = qq[:, x * LANE:(x + 1) * LANE]
                dox = jnp.where(sel, dd, 0.0)
                delta = jnp.sum(jnp.where(sel, t, 0.0), axis=1, keepdims=True)
                sc = jnp.where(mask, _dot(qx, kx, ((1,), (1,))) * ATT_SCALE, NEG)
                p = jnp.exp(sc - lsx)
                dp = _dot(dox, vv, ((1,), (1,)))
                ds = p * (dp - delta) * ATT_SCALE
                dv = dv + _dot(p, dox, ((0,), (0,)))
                outs.append(_dot(ds, qx, ((0,), (0,))))
                dq_ref[pl.ds(off, TQ), x * LANE:(x + 1) * LANE] += _dot(ds, kx, ((1,), (0,)))
            return dka + outs[0], dkb + outs[1], dv

        z = jnp.zeros((TK, LANE), F32)
        dka, dkb, dv = lax.fori_loop(kb, nq, step, (z, z, z))
        dk_ref[:, 0:LANE] = dka
        dk_ref[:, LANE:2 * LANE] = dkb
        dv_ref[...] = dv

    return pl.pallas_call(
        body, name="attn_bwd", grid=(NPAIR, S // TK),
        in_specs=[pl.BlockSpec((S, 2 * LANE), lambda j, k: (0, j)), pl.BlockSpec((TK, 2 * LANE), lambda j, k: (k, j)),
                  pl.BlockSpec((TK, LANE), lambda j, k: (k, j)), pl.BlockSpec((S, LANE), lambda j, k: (0, j)),
                  pl.BlockSpec((None, S, LANE), lambda j, k: (j, 0, 0)), pl.BlockSpec((S, LANE), lambda j, k: (0, j))],
        out_specs=[pl.BlockSpec((S, 2 * LANE), lambda j, k: (0, j)), pl.BlockSpec((TK, 2 * LANE), lambda j, k: (k, j)),
                   pl.BlockSpec((TK, LANE), lambda j, k: (k, j))],
        out_shape=[jax.ShapeDtypeStruct((S, H * LANE), F32), jax.ShapeDtypeStruct((S, H * LANE), F32),
                   jax.ShapeDtypeStruct((S, H * VDIM), F32)],
        compiler_params=pltpu.CompilerParams(dimension_semantics=("parallel", "arbitrary")),
    )(qc, kc, v, o, lse, do)


_IN_Z, _IN_XBC, _IN_DT, _IN_Q, _IN_KV, _IN_KR = 0, 1024, 2560, 2576, 2960, 3216


def _prep_weights(w_in, w_qb, w_kvb):
    dt = w_in.dtype
    w_small = jnp.concatenate(
        [w_in[:, _IN_Q:_IN_KV], w_in[:, _IN_KV:_IN_KR], w_in[:, _IN_KR:IN_WIDTH], jnp.zeros((D, LANE - ROPE), dt),
         w_in[:, _IN_DT:_IN_Q], jnp.zeros((D, LANE - H), dt)], axis=1)
    w_q = jnp.pad(w_qb.reshape(Q_RANK, H, NOPE + ROPE), ((0, 0), (0, 0), (0, LANE - NOPE - ROPE))).reshape(Q_RANK, H * LANE)
    kv3 = w_kvb.reshape(KV_RANK, H, NOPE + VDIM)
    w_k = jnp.pad(kv3[:, :, :NOPE], ((0, 0), (0, 0), (0, LANE - NOPE))).reshape(KV_RANK, H * LANE)
    w_v = kv3[:, :, NOPE:].reshape(KV_RANK, H * VDIM)
    return w_in[:, _IN_Z:_IN_XBC], w_in[:, _IN_XBC:_IN_DT], w_small, w_q, w_k, w_v


def _rope_tables(positions):
    inv_freq = 1.0 / (10000.0 ** (jnp.arange(0, ROPE, 2, dtype=F32) / ROPE))
    ang = positions.astype(F32).reshape(S, 1) * inv_freq
    cos, sin = jnp.cos(ang), jnp.sin(ang)
    cos_t = jnp.concatenate([jnp.ones((S, NOPE), F32), cos, cos, jnp.ones((S, LANE - NOPE - ROPE), F32)], axis=1)
    sin_t = jnp.concatenate([jnp.zeros((S, NOPE), F32), -sin, sin, jnp.zeros((S, LANE - NOPE - ROPE), F32)], axis=1)
    return cos_t, sin_t


def _local_step(x, p, positions, target, gw, late_weights, send_late, sp):
    w_z, w_xbc, w_small, w_q, w_k, w_v = _prep_weights(_from_cols(gw["w_in"]), _from_cols(gw["w_qb"]), _from_cols(gw["w_kvb"]))
    w_out_s = gw["w_out"][:NCHIP // 2].reshape(SSD_INNER, D)
    w_out_m = gw["w_out"][NCHIP // 2:].reshape(SSD_INNER, D)
    cos_t, sin_t = _rope_tables(positions)
    prow = jnp.zeros((8, LANE), F32).at[0, :H].set(sp["dt_bias"][0]).at[1, :H].set(sp["A_log"][0]).at[2, :H].set(sp["D"][0])
    pcol = prow.T

    xb, pb = x.astype(BF16), p.astype(BF16)
    z = _mm([(xb, w_z)], name="proj_z")
    xbc = _mm([(xb, w_xbc)], name="proj_xbc")
    small = _mm([(xb, w_small)], name="proj_small")
    act = _conv_fwd(xbc, sp["conv_w"], sp["conv_b"])
    dt_t = small[:, SM_DT:SM_DT + LANE].T
    y, states = _ssd_fwd(act, small, dt_t, prow, pcol)
    y_ssd = _gate_norm_fwd(y, z, sp["ssd_norm"])
    q_c, kv_c = small[:, SM_Q:SM_Q + Q_RANK], small[:, SM_KV:SM_KV + KV_RANK]
    qn = _rms_fwd(q_c, sp["q_norm"], name="q_norm_fwd")
    kvn = _rms_fwd(kv_c, sp["kv_norm"], name="kv_norm_fwd")
    qcat = _q_rope(_mm([(qn, w_q)], name="q_up"), cos_t, sin_t)
    kcat = _k_prep(_mm([(kvn, w_k)], name="k_up"), small, cos_t, sin_t)
    v = _mm([(kvn, w_v)], out_dtype=BF16, name="v_up")
    o, lse = _attn_fwd(qcat, kcat, v)
    y_mla = _rms_fwd(o, sp["out_norm"], name="out_norm_fwd")
    mix = _mm([(y_ssd, w_out_s), (y_mla, w_out_m)], name="out_proj")
    h1, h1b = _ln_fwd(x, mix, sp["ln_mix_g"], sp["ln_mix_b"])
    gl = late_weights(h1b)
    w_pg, w_pp = gl["w_pg"].reshape(D, D), _from_cols(gl["w_pp"])
    w_gate, w_up, w_down = gl["w_gate"], gl["w_up"], gl["w_down"]
    gate = _mm([(h1b, w_gate)], chunk="out", name="ffn_gate")
    up = _mm([(h1b, w_up)], chunk="out", name="ffn_up")
    actf = _swiglu_fwd(gate, up)
    ffn = _mm([(actf, w_down)], chunk="sum", name="ffn_down")
    pg = _mm([(h1b, w_pg)], name="ple_gate")
    pp = _mm([(pb, w_pp)], name="ple_proj")
    dpre2, dpre2b, dpg, dpp, dg2, db2, loss_row = _final_fwd_bwd(h1, ffn, pg, pp, target, sp["ln_ffn_g"], sp["ln_ffn_b"])

    g = {"ln_ffn_g": dg2, "ln_ffn_b": db2}
    g["w_pp"] = _to_cols(_mm([(pb, dpp)], ta=True, out_dtype=BF16, name="d_w_ple_proj"))
    g["w_pg"] = _mm([(h1b, dpg)], ta=True, out_dtype=BF16, name="d_w_ple_gate").reshape(NCHIP, D // NCHIP, D)
    g["w_down"] = _mm([(actf, dpre2b)], ta=True, chunk="out", out_dtype=BF16, name="d_w_down")
    dactf = _mm([(dpre2b, w_down)], tb=True, chunk="out", name="d_act")
    dgate, dup = _swiglu_bwd(gate, up, dactf)
    g["w_gate"] = _mm([(h1b, dgate)], ta=True, chunk="out", out_dtype=BF16, name="d_w_gate")
    g["w_up"] = _mm([(h1b, dup)], ta=True, chunk="out", out_dtype=BF16, name="d_w_up")
    sent = send_late({name: g.pop(name) for name in LATE})
    dh1 = _mm([(dpg, w_pg)], tb=True, add=dpre2, add_scale=ALPHA, name="d_h1_ple")
    dh1 = _mm([(dgate, w_gate), (dup, w_up)], tb=True, chunk="sum", add=dh1, name="d_h1")
    dpre1, dpre1b, g["ln_mix_g"], g["ln_mix_b"] = _ln_bwd(x, mix, sp["ln_mix_g"] + sent, dh1)
    dy_ssd = _mm([(dpre1b, w_out_s)], tb=True, name="d_y_ssd")
    dy_mla = _mm([(dpre1b, w_out_m)], tb=True, name="d_y_mla")
    g["w_out"] = jnp.concatenate([_mm([(y_ssd, dpre1b)], ta=True, out_dtype=BF16, name="d_w_out_s"),
                                  _mm([(y_mla, dpre1b)], ta=True, out_dtype=BF16, name="d_w_out_m")],
                                 axis=0).reshape(NCHIP, 2 * SSD_INNER // NCHIP, D)
    do, g["out_norm"] = _rms_bwd(o, sp["out_norm"], dy_mla, name="out_norm_bwd")
    dq, dk, dv = _attn_bwd(qcat, kcat, v, o, lse, do)
    dqlin = _q_unrope(dq, cos_t, sin_t)
    dw_q = _mm([(qn, dqlin)], ta=True, out_dtype=BF16, name="d_w_q")
    dqn = _mm([(dqlin, w_q)], tb=True, name="d_qn")
    dq_c, g["q_norm"] = _rms_bwd(q_c, sp["q_norm"], dqn, name="q_norm_bwd")
    dkr = _k_rope_bwd(dk, cos_t, sin_t)
    dw_k = _mm([(kvn, dk)], ta=True, out_dtype=BF16, name="d_w_k")
    dw_v = _mm([(kvn, dv)], ta=True, out_dtype=BF16, name="d_w_v")
    dkvn = _mm([(dk, w_k), (dv, w_v)], tb=True, name="d_kvn")
    dkv_c, g["kv_norm"] = _rms_bwd(kv_c, sp["kv_norm"], dkvn, name="kv_norm_bwd")
    g["w_qb"] = _to_cols(dw_q.reshape(Q_RANK, H, LANE)[:, :, :NOPE + ROPE].reshape(Q_RANK, H * (NOPE + ROPE)))
    g["w_kvb"] = _to_cols(jnp.concatenate([dw_k.reshape(KV_RANK, H, LANE)[:, :, :NOPE], dw_v.reshape(KV_RANK, H, VDIM)],
                                          axis=2).reshape(KV_RANK, H * (NOPE + VDIM)))
    dy, dz, g["ssd_norm"] = _gate_norm_bwd(y, z, sp["ssd_norm"], dy_ssd)
    dact, ddt, dprow = _ssd_bwd(act, small, dt_t, prow, pcol, states, dy)
    g["dt_bias"], g["A_log"], g["D"] = dprow[0:1, :H], dprow[1:2, :H], dprow[2:3, :H]
    dxbc, g["conv_w"], g["conv_b"] = _conv_bwd(xbc, sp["conv_w"], sp["conv_b"], dact)
    dsmall = jnp.concatenate([dq_c, dkv_c, dkr, ddt], axis=1).astype(BF16)
    grad_x = _mm([(dz, w_z), (dxbc, w_xbc), (dsmall, w_small)], tb=True, add=dpre1, add_scale=ALPHA, name="d_x")
    dw_small = _mm([(xb, dsmall)], ta=True, out_dtype=BF16, name="d_w_small")
    g["w_in"] = _to_cols(jnp.concatenate(
        [_mm([(xb, dz)], ta=True, out_dtype=BF16, name="d_w_z"), _mm([(xb, dxbc)], ta=True, out_dtype=BF16, name="d_w_xbc"),
         dw_small[:, SM_DT:SM_DT + H], dw_small[:, SM_Q:SM_Q + Q_RANK], dw_small[:, SM_KV:SM_KV + KV_RANK],
         dw_small[:, SM_KR:SM_KR + ROPE]], axis=1))
    return loss_row, grad_x, g


MESH = pl.DeviceIdType.MESH
BIG = (("w_in", (D, IN_WIDTH), 1), ("w_qb", (Q_RANK, H * (NOPE + ROPE)), 1), ("w_kvb", (KV_RANK, H * (NOPE + VDIM)), 1),
       ("w_out", (2 * SSD_INNER, D), 0), ("w_gate", (D, D_FF), 1), ("w_up", (D, D_FF), 1), ("w_down", (D_FF, D), 0),
       ("w_pg", (D, D), 0), ("w_pp", (PLE, D), 1))
CONV_SHARD = SSD_XBC // NCHIP
BF16_ROWS = 16


def _from_cols(stack):
    return jnp.concatenate([stack[k] for k in range(NCHIP)], axis=1)


def _to_cols(full):
    r, c4 = full.shape
    return full.reshape(r, NCHIP, c4 // NCHIP).transpose(1, 0, 2)


def _coords():
    return lax.axis_index("x"), lax.axis_index("y"), lax.axis_index("c")


def _peers():
    x, y, c = _coords()
    return 2 * x + y, c, [(1 - x, y), (x, 1 - y), (1 - x, 1 - y)], (x, y, 1 - c)


def _half(c, rows):
    return pl.ds(pl.multiple_of(c * (rows // 2), BF16_ROWS), rows // 2)


def _gather_weights(shards):
    n_arr = len(shards)
    split = [s.shape[0] % (2 * BF16_ROWS) == 0 for s in shards]
    per = 2 * (NCHIP - 1)

    def body(*refs):
        ins, outs = refs[:n_arr], refs[n_arr:2 * n_arr]
        send_sems, recv_sems, local_sems = refs[2 * n_arr:]
        k, c, chips, sibling = _peers()

        def copy(idx, src, dst, to):
            return pltpu.make_async_remote_copy(src_ref=src, dst_ref=dst, send_sem=send_sems.at[idx], recv_sem=recv_sems.at[idx],
                                                device_id=to, device_id_type=MESH)

        def part(a, chip, core):
            return outs[a].at[chip, _half(core, shards[a].shape[0])] if split[a] else outs[a].at[chip]

        mine = [pltpu.make_async_copy(ins[a], outs[a].at[k], local_sems.at[a]) for a in range(n_arr)]
        for cp in mine:
            cp.start()
        sends = []
        for a in range(n_arr):
            src = ins[a].at[_half(c, shards[a].shape[0])] if split[a] else ins[a]
            for j, (cx, cy) in enumerate(chips):
                sends.append(copy(per * a + j, src, part(a, k, c), (cx, cy, c)))
                sends[-1].start()
        for j, (cx, cy) in enumerate(chips):
            for a in range(n_arr):
                landed = part(a, 2 * cx + cy, c)
                copy(per * a + j, landed, landed, (cx, cy, c)).wait_recv()
                if split[a]:
                    sends.append(copy(per * a + NCHIP - 1 + j, landed, landed, sibling))
                    sends[-1].start()
        for j, (cx, cy) in enumerate(chips):
            for a in range(n_arr):
                if split[a]:
                    other = part(a, 2 * cx + cy, 1 - c)
                    copy(per * a + NCHIP - 1 + j, other, other, sibling).wait_recv()
        for cp in sends:
            cp.wait_send()
        for cp in mine:
            cp.wait()

    any_spec = pl.BlockSpec(memory_space=pl.ANY)
    return pl.pallas_call(
        body, name="gather_weights", in_specs=[any_spec] * n_arr, out_specs=[any_spec] * n_arr,
        out_shape=[jax.ShapeDtypeStruct((NCHIP,) + s.shape, s.dtype) for s in shards],
        scratch_shapes=[pltpu.SemaphoreType.DMA((per * n_arr,)), pltpu.SemaphoreType.DMA((per * n_arr,)),
                        pltpu.SemaphoreType.DMA((n_arr,))],
    )(*shards)


def _reduce_grads(stacks):
    n_arr = len(stacks)
    dims = [s.shape[1:] for s in stacks]
    per = NCHIP + 1

    def body(*refs):
        ins, fin, r1, part, r2 = (refs[i * n_arr:(i + 1) * n_arr] for i in range(5))
        send_sems, recv_sems, local_sems = refs[5 * n_arr:]
        k, c, chips, sibling = _peers()

        def copy(idx, src, dst, to):
            return pltpu.make_async_remote_copy(src_ref=src, dst_ref=dst, send_sem=send_sems.at[idx], recv_sem=recv_sems.at[idx],
                                                device_id=to, device_id_type=MESH)

        pairs = [copy(per * a, ins[a].at[:, _half(1 - c, dims[a][0])], r1[a], sibling) for a in range(n_arr)]
        for cp in pairs:
            cp.start()
        sends, own = [], []
        for a in range(n_arr):
            hr, cols = dims[a][0] // 2, dims[a][1]
            pairs[a].wait_recv()

            def pair_sum(va, vb, vo, a=a):
                for kk in range(NCHIP):
                    pltpu.sync_copy(ins[a].at[kk, _half(c, dims[a][0])], va)
                    pltpu.sync_copy(r1[a].at[kk], vb)
                    vo[...] = (va[...].astype(F32) + vb[...].astype(F32)).astype(BF16)
                    pltpu.sync_copy(vo, part[a].at[kk])

            pl.run_scoped(pair_sum, *[pltpu.VMEM((hr, cols), BF16)] * 3)
            for j, (cx, cy) in enumerate(chips):
                sends.append(copy(per * a + 1 + j, part[a].at[2 * cx + cy], r2[a].at[k], (cx, cy, c)))
                sends[-1].start()
            own.append(pltpu.make_async_copy(part[a].at[k], r2[a].at[k], local_sems.at[a]))
            own[-1].start()
        for a in range(n_arr):
            hr, cols = dims[a][0] // 2, dims[a][1]
            mine = fin[a].at[_half(c, dims[a][0])]
            own[a].wait()
            for j, (cx, cy) in enumerate(chips):
                landed = r2[a].at[2 * cx + cy]
                copy(per * a + 1 + j, landed, landed, (cx, cy, c)).wait_recv()

            def chip_sum(vs, vf, a=a, mine=mine):
                pltpu.sync_copy(r2[a], vs)
                acc = vs[0].astype(F32)
                for kk in range(1, NCHIP):
                    acc = acc + vs[kk].astype(F32)
                vf[...] = acc
                pltpu.sync_copy(vf, mine)

            pl.run_scoped(chip_sum, pltpu.VMEM((NCHIP, hr, cols), BF16), pltpu.VMEM((hr, cols), F32))
            sends.append(copy(per * a + NCHIP, mine, mine, sibling))
            sends[-1].start()
        for a in range(n_arr):
            other = fin[a].at[_half(1 - c, dims[a][0])]
            copy(per * a + NCHIP, other, other, sibling).wait_recv()
        for cp in pairs + sends:
            cp.wait_send()

    any_spec = pl.BlockSpec(memory_space=pl.ANY)
    stage = [jax.ShapeDtypeStruct((NCHIP, r // 2, cols), BF16) for r, cols in dims]
    return pl.pallas_call(
        body, name="reduce_grads", in_specs=[any_spec] * n_arr, out_specs=[any_spec] * (4 * n_arr),
        out_shape=[jax.ShapeDtypeStruct(d, F32) for d in dims] + stage * 3,
        scratch_shapes=[pltpu.SemaphoreType.DMA((per * n_arr,)), pltpu.SemaphoreType.DMA((per * n_arr,)),
                        pltpu.SemaphoreType.DMA((n_arr,))],
    )(*stacks)[:n_arr]


LATE = ("w_gate", "w_up", "w_down", "w_pg", "w_pp")
HBM_SPEC = pl.BlockSpec(memory_space=pltpu.HBM)
SEM_SPEC = pl.BlockSpec(memory_space=pltpu.SEMAPHORE)
IN_FLIGHT = pltpu.SideEffectType.DATAFLOW_SIDE_EFFECTING


def _in_hbm(a):
    return pltpu.with_memory_space_constraint(a, pltpu.HBM)


def _hbm_like(arrs, lead=()):
    return [pltpu.HBM(lead + a.shape, a.dtype) for a in arrs]


def _split_start(name, srcs, lands, after, start):
    n = len(srcs)
    n_sem = (NCHIP - 1) * n

    def body(*refs):
        src_refs, land_refs = refs[:n], refs[n:2 * n]
        send_sems, recv_sems = refs[2 * n + 1], refs[2 * n + 2]
        token = refs[-1]
        k, c, chips, _ = _peers()

        def copy(idx, src, dst, to):
            return pltpu.make_async_remote_copy(src_ref=src, dst_ref=dst, send_sem=send_sems.at[idx], recv_sem=recv_sems.at[idx],
                                                device_id=to, device_id_type=MESH)

        for cp in start(k, c, chips, src_refs, land_refs, copy):
            cp.start()
        token[...] = jnp.zeros_like(token)

    sem = pltpu.SemaphoreType.DMA((n_sem,))
    outs = pl.pallas_call(
        body, name=name, in_specs=[HBM_SPEC] * (2 * n) + [pl.BlockSpec(memory_space=pl.ANY)],
        out_specs=[SEM_SPEC, SEM_SPEC] + [HBM_SPEC] * (2 * n) + [pl.BlockSpec(memory_space=pltpu.VMEM)],
        out_shape=[sem, sem] + _hbm_like(srcs) + _hbm_like(lands) + [jax.ShapeDtypeStruct((8, LANE), F32)],
        input_output_aliases={i: 2 + i for i in range(2 * n)},
        compiler_params=pltpu.CompilerParams(has_side_effects=IN_FLIGHT),
    )(*[_in_hbm(a) for a in srcs], *[_in_hbm(a) for a in lands], after)
    return (outs[0], outs[1], outs[2:2 + n], outs[2 + n:2 + 2 * n]), outs[-1][0, 0]


def _split_wait(name, send_sems, recv_sems, srcs, lands, after, waits):
    n = len(srcs)

    def body(*refs):
        src_refs, land_refs = refs[:n], refs[n:2 * n]
        send_ref, recv_ref = refs[2 * n], refs[2 * n + 1]
        k, c, chips, _ = _peers()

        def copy(idx, src, dst, to):
            return pltpu.make_async_remote_copy(src_ref=src, dst_ref=dst, send_sem=send_ref.at[idx], recv_sem=recv_ref.at[idx],
                                                device_id=to, device_id_type=MESH)

        for cp in waits(k, c, chips, src_refs, land_refs, copy):
            cp.wait_send()
            cp.wait_recv()

    outs = pl.pallas_call(
        body, name=name, in_specs=[HBM_SPEC] * (2 * n) + [SEM_SPEC, SEM_SPEC, pl.BlockSpec(memory_space=pl.ANY)],
        out_specs=[HBM_SPEC] * (2 * n), out_shape=_hbm_like(srcs) + _hbm_like(lands),
        input_output_aliases={i: i for i in range(2 * n)},
        compiler_params=pltpu.CompilerParams(has_side_effects=IN_FLIGHT),
    )(*srcs, *lands, send_sems, recv_sems, after)
    return outs[:n], outs[n:]


def _late_gather_copies(k, c, chips, shard_refs, stack_refs, copy):
    out = []
    for a, (src, dst) in enumerate(zip(shard_refs, stack_refs)):
        rows = src.shape[0]
        for j, (cx, cy) in enumerate(chips):
            out.append((a, j, rows, copy((NCHIP - 1) * a + j, src.at[_half(c, rows)], dst.at[k, _half(c, rows)], (cx, cy, c))))
    return out


def _gather_late_start(shards, after):
    def start(k, c, chips, srcs, lands, copy):
        return [cp for _, _, _, cp in _late_gather_copies(k, c, chips, srcs, lands, copy)]

    lands = [lax.empty((NCHIP,) + s.shape, s.dtype) for s in shards]
    return _split_start("gather_late_start", shards, lands, after, start)


def _gather_late_wait(send_sems, recv_sems, shards, lands, after):
    def waits(k, c, chips, srcs, lands_, copy):
        out = []
        for a, (src, dst) in enumerate(zip(srcs, lands_)):
            rows = src.shape[0]
            for j, (cx, cy) in enumerate(chips):
                landed = dst.at[2 * cx + cy, _half(c, rows)]
                out.append(copy((NCHIP - 1) * a + j, src.at[_half(c, rows)], landed, (cx, cy, c)))
        return out

    return _split_wait("gather_late_wait", send_sems, recv_sems, shards, lands, after, waits)[1]


def _gather_late_finish(stacks, shards):
    n = len(stacks)
    per = NCHIP - 1

    def body(*refs):
        ins = refs[n:2 * n]
        outs = refs[2 * n:3 * n]
        send_sems, recv_sems, local_sems = refs[3 * n:]
        k, c, chips, sibling = _peers()
        mine = [pltpu.make_async_copy(ins[a], outs[a].at[k], local_sems.at[a]) for a in range(n)]
        for cp in mine:
            cp.start()
        sends = []
        for a in range(n):
            rows = shards[a].shape[0]
            for j, (cx, cy) in enumerate(chips):
                landed = outs[a].at[2 * cx + cy, _half(c, rows)]
                sends.append(pltpu.make_async_remote_copy(src_ref=landed, dst_ref=landed, send_sem=send_sems.at[per * a + j],
                                                          recv_sem=recv_sems.at[per * a + j], device_id=sibling, device_id_type=MESH))
                sends[-1].start()
        for a in range(n):
            rows = shards[a].shape[0]
            for j, (cx, cy) in enumerate(chips):
                other = outs[a].at[2 * cx + cy, _half(1 - c, rows)]
                pltpu.make_async_remote_copy(src_ref=other, dst_ref=other, send_sem=send_sems.at[per * a + j],
                                             recv_sem=recv_sems.at[per * a + j], device_id=sibling, device_id_type=MESH).wait_recv()
        for cp in sends:
            cp.wait_send()
        for cp in mine:
            cp.wait()

    any_spec = pl.BlockSpec(memory_space=pl.ANY)
    return pl.pallas_call(
        body, name="gather_late_finish", in_specs=[any_spec] * (2 * n), out_specs=[any_spec] * n,
        out_shape=[jax.ShapeDtypeStruct(s.shape, s.dtype) for s in stacks], input_output_aliases={i: i for i in range(n)},
        scratch_shapes=[pltpu.SemaphoreType.DMA((per * n,)), pltpu.SemaphoreType.DMA((per * n,)), pltpu.SemaphoreType.DMA((n,))],
    )(*stacks, *shards)


def _reduce_pair(stacks):
    n_arr = len(stacks)
    dims = [s.shape[1:] for s in stacks]

    def body(*refs):
        ins, part, r1 = (refs[i * n_arr:(i + 1) * n_arr] for i in range(3))
        send_sems, recv_sems = refs[3 * n_arr:]
        _, c, _, sibling = _peers()
        pairs = [pltpu.make_async_remote_copy(src_ref=ins[a].at[:, _half(1 - c, dims[a][0])], dst_ref=r1[a], send_sem=send_sems.at[a],
                                              recv_sem=recv_sems.at[a], device_id=sibling, device_id_type=MESH) for a in range(n_arr)]
        for cp in pairs:
            cp.start()
        for a in range(n_arr):
            pairs[a].wait_recv()

            def pair_sum(va, vb, vo, a=a):
                for kk in range(NCHIP):
                    pltpu.sync_copy(ins[a].at[kk, _half(c, dims[a][0])], va)
                    pltpu.sync_copy(r1[a].at[kk], vb)
                    vo[...] = (va[...].astype(F32) + vb[...].astype(F32)).astype(BF16)
                    pltpu.sync_copy(vo, part[a].at[kk])

            pl.run_scoped(pair_sum, *[pltpu.VMEM((dims[a][0] // 2, dims[a][1]), BF16)] * 3)
        for cp in pairs:
            cp.wait_send()

    any_spec = pl.BlockSpec(memory_space=pl.ANY)
    stage = [jax.ShapeDtypeStruct((NCHIP, r // 2, cols), BF16) for r, cols in dims]
    return pl.pallas_call(
        body, name="reduce_pair", in_specs=[any_spec] * n_arr, out_specs=[any_spec] * (2 * n_arr), out_shape=stage * 2,
        scratch_shapes=[pltpu.SemaphoreType.DMA((n_arr,)), pltpu.SemaphoreType.DMA((n_arr,))],
    )(*stacks)[:n_arr]


def _reduce_late_start(parts, after):
    def start(k, c, chips, srcs, lands, copy):
        return [copy((NCHIP - 1) * a + j, src.at[2 * cx + cy], dst.at[k], (cx, cy, c))
                for a, (src, dst) in enumerate(zip(srcs, lands)) for j, (cx, cy) in enumerate(chips)]

    lands = [lax.empty(p.shape, p.dtype) for p in parts]
    return _split_start("reduce_late_start", parts, lands, after, start)


def _reduce_late_wait(send_sems, recv_sems, parts, lands, after):
    def waits(k, c, chips, srcs, lands_, copy):
        return [copy((NCHIP - 1) * a + j, src.at[k], dst.at[2 * cx + cy], (cx, cy, c))
                for a, (src, dst) in enumerate(zip(srcs, lands_)) for j, (cx, cy) in enumerate(chips)]

    return _split_wait("reduce_late_wait", send_sems, recv_sems, parts, lands, after, waits)


def _reduce_finish(parts, arrived):
    n_arr = len(parts)
    dims = [(2 * p.shape[1], p.shape[2]) for p in parts]

    def body(*refs):
        r2_in, part, r2, fin = (refs[i * n_arr:(i + 1) * n_arr] for i in range(4))
        send_sems, recv_sems, local_sems = refs[4 * n_arr:]
        k, c, _, sibling = _peers()
        own = [pltpu.make_async_copy(part[a].at[k], r2[a].at[k], local_sems.at[a]) for a in range(n_arr)]
        for cp in own:
            cp.start()
        sends = []
        for a in range(n_arr):
            mine = fin[a].at[_half(c, dims[a][0])]
            own[a].wait()

            def chip_sum(vs, vf, a=a, mine=mine):
                pltpu.sync_copy(r2[a], vs)
                acc = vs[0].astype(F32)
                for kk in range(1, NCHIP):
                    acc = acc + vs[kk].astype(F32)
                vf[...] = acc
                pltpu.sync_copy(vf, mine)

            pl.run_scoped(chip_sum, pltpu.VMEM((NCHIP, dims[a][0] // 2, dims[a][1]), BF16), pltpu.VMEM((dims[a][0] // 2, dims[a][1]), F32))
            sends.append(pltpu.make_async_remote_copy(src_ref=mine, dst_ref=mine, send_sem=send_sems.at[a], recv_sem=recv_sems.at[a],
                                                      device_id=sibling, device_id_type=MESH))
            sends[-1].start()
        for a in range(n_arr):
            other = fin[a].at[_half(1 - c, dims[a][0])]
            pltpu.make_async_remote_copy(src_ref=other, dst_ref=other, send_sem=send_sems.at[a], recv_sem=recv_sems.at[a],
                                         device_id=sibling, device_id_type=MESH).wait_recv()
        for cp in sends:
            cp.wait_send()

    any_spec = pl.BlockSpec(memory_space=pl.ANY)
    return pl.pallas_call(
        body, name="reduce_finish", in_specs=[any_spec] * (2 * n_arr), out_specs=[any_spec] * (2 * n_arr),
        out_shape=[jax.ShapeDtypeStruct(a.shape, a.dtype) for a in arrived] + [jax.ShapeDtypeStruct(d, F32) for d in dims],
        input_output_aliases={i: i for i in range(n_arr)},
        scratch_shapes=[pltpu.SemaphoreType.DMA((n_arr,)), pltpu.SemaphoreType.DMA((n_arr,)), pltpu.SemaphoreType.DMA((n_arr,))],
    )(*arrived, *parts)[n_arr:]


SMALL = (("conv_w", SSD_K * SSD_XBC), ("conv_b", SSD_XBC), ("dt_bias", H), ("A_log", H), ("D", H), ("ssd_norm", SSD_INNER),
         ("q_norm", Q_RANK), ("kv_norm", KV_RANK), ("out_norm", SSD_INNER), ("ln_mix_g", D), ("ln_mix_b", D),
         ("ln_ffn_g", D), ("ln_ffn_b", D))
SMALL_ROWS = 120
NDEV = 8


def _allreduce_small(sv):
    def body(sv_ref, out_ref, slots, send_sems, recv_sems):
        x, y, c = _coords()
        me = 4 * x + 2 * y + c
        slots[me] = sv_ref[...]
        copies = []
        for d in range(1, NDEV):
            to = (x ^ (d >> 2), y ^ ((d >> 1) & 1), c ^ (d & 1))
            copies.append(pltpu.make_async_remote_copy(src_ref=sv_ref, dst_ref=slots.at[me], send_sem=send_sems.at[d - 1],
                                                       recv_sem=recv_sems.at[d - 1], device_id=to, device_id_type=MESH))
            copies[-1].start()
        for cp in copies:
            cp.wait_recv()
        for cp in copies:
            cp.wait_send()
        acc = slots[0]
        for i in range(1, NDEV):
            acc = acc + slots[i]
        out_ref[...] = acc

    vm = pl.BlockSpec(memory_space=pltpu.VMEM)
    return pl.pallas_call(
        body, name="allreduce_small", in_specs=[vm], out_specs=vm, out_shape=jax.ShapeDtypeStruct((SMALL_ROWS, LANE), F32),
        scratch_shapes=[pltpu.VMEM((NDEV, SMALL_ROWS, LANE), F32), pltpu.SemaphoreType.DMA((NDEV - 1,)),
                        pltpu.SemaphoreType.DMA((NDEV - 1,))],
    )(sv)


def _adamw_math(w, g, m, v):
    m2 = ADAM_B1 * m + (1.0 - ADAM_B1) * g
    v2 = ADAM_B2 * v + (1.0 - ADAM_B2) * (g * g)
    m_hat = m2 / (1.0 - ADAM_B1 ** ADAM_STEP)
    v_hat = v2 / (1.0 - ADAM_B2 ** ADAM_STEP)
    return -ADAM_LR * (m_hat / (jnp.sqrt(v_hat) + ADAM_EPS) + ADAM_WD * w), m2, v2


def _adamw_big(w, g, m, v, *, name):
    r, c = w.shape
    tr = next(t for t in (512, 384, 352, 256, 128, 64, 8) if r % t == 0)

    def body(w_ref, g_ref, m_ref, v_ref, d_ref, m2_ref, v2_ref):
        d_ref[...], m2_ref[...], v2_ref[...] = _adamw_math(w_ref[...], g_ref[...], m_ref[...], v_ref[...])

    spec = pl.BlockSpec((tr, c), lambda i: (i, 0))
    return pl.pallas_call(body, name=name, grid=(r // tr,), in_specs=[spec] * 4, out_specs=[spec] * 3,
                          out_shape=[jax.ShapeDtypeStruct((r, c), F32)] * 3)(w, g, m, v)


def _adamw_small(ws, gs, ms, vs):
    n = len(ws)

    def body(*refs):
        for i in range(n):
            w_ref, g_ref, m_ref, v_ref = (refs[j * n + i] for j in range(4))
            d_ref, m2_ref, v2_ref = (refs[(4 + j) * n + i] for j in range(3))
            d_ref[...], m2_ref[...], v2_ref[...] = _adamw_math(w_ref[...], g_ref[...], m_ref[...], v_ref[...])

    vm = pl.BlockSpec(memory_space=pltpu.VMEM)
    shapes = [jax.ShapeDtypeStruct(w.shape, F32) for w in ws]
    outs = pl.pallas_call(body, name="adamw_small", in_specs=[vm] * (4 * n), out_specs=[vm] * (3 * n), out_shape=shapes * 3)(
        *ws, *gs, *ms, *vs)
    return outs[:n], outs[n:2 * n], outs[2 * n:]


_SMALL_ARG = {"conv_w": "ssd_conv_w", "conv_b": "ssd_conv_b", "dt_bias": "ssd_dt_bias", "A_log": "ssd_A_log", "D": "ssd_D",
              "ssd_norm": "ssd_norm_w", "q_norm": "mla_q_norm_w", "kv_norm": "mla_kv_norm_w", "out_norm": "mla_out_norm_w",
              "ln_mix_g": "ln_mix_g", "ln_mix_b": "ln_mix_b", "ln_ffn_g": "ln_ffn_g", "ln_ffn_b": "ln_ffn_b"}
_BIG_ARG = {"w_in": "w_in", "w_qb": "mla_w_q_b", "w_kvb": "mla_w_kv_b", "w_out": "w_out", "w_gate": "w_ffn_gate",
            "w_up": "w_ffn_up", "w_down": "w_ffn_down", "w_pg": "w_ple_gate", "w_pp": "w_ple_proj"}
_WEIGHT_ORDER = ("w_in", "ssd_conv_w", "ssd_conv_b", "ssd_dt_bias", "ssd_A_log", "ssd_D", "ssd_norm_w", "mla_q_norm_w", "mla_w_q_b",
                 "mla_kv_norm_w", "mla_w_kv_b", "mla_out_norm_w", "w_out", "ln_mix_g", "ln_mix_b", "w_ffn_gate", "w_ffn_up",
                 "w_ffn_down", "w_ple_gate", "w_ple_proj", "ln_ffn_g", "ln_ffn_b")


def _rows128(a):
    flat = a.reshape(-1)
    return jnp.pad(flat, (0, -flat.shape[0] % LANE)).reshape(-1, LANE)


def kernel(x, p, positions, w_in, ssd_conv_w, ssd_conv_b, ssd_dt_bias, ssd_A_log, ssd_D, ssd_norm_w, mla_q_norm_w, mla_w_q_b, mla_kv_norm_w, mla_w_kv_b, mla_out_norm_w, w_out, ln_mix_g, ln_mix_b, w_ffn_gate, w_ffn_up, w_ffn_down, w_ple_gate, w_ple_proj, ln_ffn_g, ln_ffn_b, loss_target, m_w_in, m_ssd_conv_w, m_ssd_conv_b, m_ssd_dt_bias, m_ssd_A_log, m_ssd_D, m_ssd_norm_w, m_mla_q_norm_w, m_mla_w_q_b, m_mla_kv_norm_w, m_mla_w_kv_b, m_mla_out_norm_w, m_w_out, m_ln_mix_g, m_ln_mix_b, m_w_ffn_gate, m_w_ffn_up, m_w_ffn_down, m_w_ple_gate, m_w_ple_proj, m_ln_ffn_g, m_ln_ffn_b, v_w_in, v_ssd_conv_w, v_ssd_conv_b, v_ssd_dt_bias, v_ssd_A_log, v_ssd_D, v_ssd_norm_w, v_mla_q_norm_w, v_mla_w_q_b, v_mla_kv_norm_w, v_mla_w_kv_b, v_mla_out_norm_w, v_w_out, v_ln_mix_g, v_ln_mix_b, v_w_ffn_gate, v_w_ffn_up, v_w_ffn_down, v_w_ple_gate, v_w_ple_proj, v_ln_ffn_g, v_ln_ffn_b):
    given = dict(locals())
    chip = 2 * lax.axis_index("x") + lax.axis_index("y")

    early = [name for name, _, _ in BIG if name not in LATE]
    shards = [given[_BIG_ARG[name]][0].astype(BF16) for name in early]
    conv_bits = lax.bitcast_convert_type(ssd_conv_w[0], BF16).reshape(SSD_K, 2 * CONV_SHARD)
    shards.append(jnp.pad(conv_bits, ((0, BF16_ROWS - SSD_K), (0, 0))))
    gathered = _gather_weights(shards)
    gw = dict(zip(early, gathered))
    conv_all = lax.bitcast_convert_type(gathered[-1][:, :SSD_K].reshape(NCHIP, SSD_K, CONV_SHARD, 2), F32)
    sp = {k: given[a] for k, a in _SMALL_ARG.items() if k != "conv_w"}
    sp["conv_w"] = _from_cols(conv_all)
    late_shards = [given[_BIG_ARG[name]][0].astype(BF16) for name in LATE]
    in_flight, started = _gather_late_start(late_shards, gathered[0])

    def late_weights(after):
        return dict(zip(LATE, _gather_late_finish(_gather_late_wait(*in_flight, after), late_shards)))

    reducing = []

    def send_late(grads):
        in_flight_grads, sent = _reduce_late_start(_reduce_pair([grads[name] for name in LATE]), grads[LATE[0]])
        reducing.append(in_flight_grads)
        return sent

    loss_row, grad_x, g = _local_step(x[0] + started, p[0, 0], positions[0], loss_target[0], gw, late_weights, send_late, sp)

    gbig = dict(zip(early, _reduce_grads([g[name] for name in early])))
    gbig.update(zip(LATE, _reduce_finish(*_reduce_late_wait(*reducing[0], grad_x))))
    small_in = jnp.concatenate([_rows128(g[name]) for name, _ in SMALL] + [loss_row], axis=0)
    small_sum = _allreduce_small(jnp.pad(small_in, ((0, SMALL_ROWS - small_in.shape[0]), (0, 0))))
    gsmall, row = {}, 0
    for name, size in SMALL:
        nrow = -(-size // LANE)
        gsmall[name] = small_sum[row:row + nrow].reshape(-1)[:size]
        row += nrow
    loss = small_sum[row, 0]

    grads = {}
    for name, shape, axis in BIG:
        grads[_BIG_ARG[name]] = gbig[name][None]
    for name, _ in SMALL:
        if name == "conv_w":
            full_g = gsmall[name].reshape(SSD_K, SSD_XBC)
            grads["ssd_conv_w"] = lax.dynamic_slice(full_g, (0, chip * CONV_SHARD), (SSD_K, CONV_SHARD))[None]
        else:
            grads[_SMALL_ARG[name]] = gsmall[name].reshape(given[_SMALL_ARG[name]].shape)

    delta, new_m, new_v = {}, {}, {}
    for name, _, _ in BIG:
        a = _BIG_ARG[name]
        d, m2, v2 = _adamw_big(given[a][0], grads[a][0], given["m_" + a][0], given["v_" + a][0], name="adamw_" + a)
        delta[a], new_m[a], new_v[a] = d[None], m2[None], v2[None]
    small_names = [_SMALL_ARG[name] for name, _ in SMALL]
    two_d = lambda t: t.reshape(t.shape[-2], t.shape[-1])
    ds, ms, vs = _adamw_small([two_d(given[a]) for a in small_names], [two_d(grads[a]) for a in small_names],
                              [two_d(given["m_" + a]) for a in small_names], [two_d(given["v_" + a]) for a in small_names])
    for a, d, m2, v2 in zip(small_names, ds, ms, vs):
        delta[a], new_m[a], new_v[a] = (t.reshape(given[a].shape) for t in (d, m2, v2))

    return (loss, grad_x[None], *[grads[n] for n in _WEIGHT_ORDER], *[delta[n] for n in _WEIGHT_ORDER],
            *[new_m[n] for n in _WEIGHT_ORDER], *[new_v[n] for n in _WEIGHT_ORDER])
```

```python
import functools
import math

import jax
import jax.numpy as jnp
from jax import lax
from jax.experimental import pallas as pl
from jax.experimental.pallas import tpu as pltpu

F32 = jnp.float32
BF16 = jnp.bfloat16

S = 2048
D = 1024
PLE = 256
H = 16
SSD_P = 64
SSD_INNER = 1024
SSD_N = 128
SSD_G = 2
SSD_L = 128
SSD_NC = S // SSD_L
SSD_XBC = 1536
SSD_K = 4
Q_RANK = 384
KV_RANK = 256
NOPE = 64
ROPE = 32
VDIM = 64
D_FF = 2816
IN_WIDTH = 3248
ALPHA = 2.0 ** 0.25
EPS_RMS = 1e-6
EPS_LN = 1e-5
ATT_SCALE = 1.0 / math.sqrt(NOPE + ROPE)
LN2 = math.log(2.0)
ATT_SCALE_LOG2 = ATT_SCALE / LN2
LANE = 128
NCHIP = 4
SMALL_W = 896
SM_Q, SM_KV, SM_KR, SM_DT = 0, 384, 640, 768
NEG = -1e30

ADAM_LR = 0.001
ADAM_B1 = 0.9
ADAM_B2 = 0.999
ADAM_EPS = 1e-08
ADAM_WD = 0.01
ADAM_STEP = 10


def _sigmoid(v):
    return 1.0 / (1.0 + jnp.exp(-v))


MM_VMEM_BUDGET = 36 * 2 ** 20
MM_MAX_ACC = 2048 * 1024


def _mm_tiles(pairs, ta, tb, m, n, out_dtype, has_add):
    def divs(v):
        return [LANE * d for d in range(v // LANE, 0, -1) if (v // LANE) % d == 0] if v % LANE == 0 else [v]

    def cost(tm, tn):
        tot = tm * tn * (jnp.dtype(out_dtype).itemsize + (4 if has_add else 0))
        for a, b in pairs:
            k = a.shape[-2] if ta else a.shape[-1]
            tot += k * (tm * a.dtype.itemsize + tn * b.dtype.itemsize)
        return 2 * tot

    ok = [(tm * tn, tm, tn) for tm in divs(m) for tn in divs(n) if tm * tn <= MM_MAX_ACC and cost(tm, tn) <= MM_VMEM_BUDGET]
    _, tm, tn = max(ok)
    return tm, tn


def _mm(pairs, *, ta=False, tb=False, out_dtype=F32, add=None, add_scale=1.0, chunk=None, name):
    n_pairs = len(pairs)
    a0, b0 = pairs[0]
    m = a0.shape[-1] if ta else a0.shape[-2]
    n = b0.shape[-2] if tb else b0.shape[-1]
    tm, tn = _mm_tiles(pairs, ta, tb, m, n, out_dtype, add is not None)
    dims = (((0 if ta else 1,), (1 if tb else 0,)), ((), ()))
    nk = NCHIP if chunk else 1
    assert chunk != "sum" or out_dtype == F32

    def body(*refs):
        o_ref = refs[-1]
        acc = None
        for i in range(n_pairs):
            a = refs[2 * i][...].astype(BF16)
            b = refs[2 * i + 1][...].astype(BF16)
            part = lax.dot_general(a, b, dims, preferred_element_type=F32)
            acc = part if acc is None else acc + part
        if chunk == "sum":
            k = pl.program_id(2)

            @pl.when(k == 0)
            def _():
                o_ref[...] = acc + add_scale * refs[2 * n_pairs][...] if add is not None else acc

            @pl.when(k > 0)
            def _():
                o_ref[...] += acc
        else:
            if add is not None:
                acc = acc + add_scale * refs[2 * n_pairs][...]
            o_ref[...] = acc.astype(out_dtype)

    def spec(arr, shape, idx2):
        if arr.ndim == 3:
            return pl.BlockSpec((None,) + shape, lambda i, j, k: (k,) + idx2(i, j))
        return pl.BlockSpec(shape, lambda i, j, k: idx2(i, j))

    in_specs, args = [], []
    for a, b in pairs:
        kdim = a.shape[-2] if ta else a.shape[-1]
        in_specs.append(spec(a, (kdim, tm), lambda i, j: (0, i)) if ta else spec(a, (tm, kdim), lambda i, j: (i, 0)))
        in_specs.append(spec(b, (tn, kdim), lambda i, j: (j, 0)) if tb else spec(b, (kdim, tn), lambda i, j: (0, j)))
        args += [a, b]
    if add is not None:
        in_specs.append(pl.BlockSpec((tm, tn), lambda i, j, k: (i, j)))
        args.append(add)
    if chunk == "out":
        out_spec = pl.BlockSpec((None, tm, tn), lambda i, j, k: (k, i, j))
        out_shape = jax.ShapeDtypeStruct((nk, m, n), out_dtype)
    else:
        out_spec = pl.BlockSpec((tm, tn), lambda i, j, k: (i, j))
        out_shape = jax.ShapeDtypeStruct((m, n), out_dtype)
    return pl.pallas_call(
        body, name=name, grid=(m // tm, n // tn, nk), in_specs=in_specs, out_specs=out_spec, out_shape=out_shape,
        compiler_params=pltpu.CompilerParams(dimension_semantics=("parallel", "parallel", "arbitrary")),
    )(*args)


TR = 256


def _row_spec(c):
    return pl.BlockSpec((TR, c), lambda i: (i, 0))


def _vec_spec(c):
    return pl.BlockSpec((1, c), lambda i: (0, 0))


def _acc_rows(ref, val):
    @pl.when(pl.program_id(0) == 0)
    def _():
        ref[...] = jnp.zeros_like(ref)
    ref[...] += val


def _rms_fwd(u, w, *, name):
    c = u.shape[1]

    def body(u_ref, w_ref, o_ref):
        v = u_ref[...]
        r = lax.rsqrt(jnp.mean(v * v, axis=-1, keepdims=True) + EPS_RMS)
        o_ref[...] = (v * r * w_ref[...]).astype(BF16)

    return pl.pallas_call(body, name=name, grid=(S // TR,), in_specs=[_row_spec(c), _vec_spec(c)], out_specs=_row_spec(c),
                          out_shape=jax.ShapeDtypeStruct((S, c), BF16))(u, w)


def _rms_bwd(u, w, dy, *, name):
    c = u.shape[1]

    def body(u_ref, w_ref, dy_ref, du_ref, dw_ref):
        v = u_ref[...]
        g = dy_ref[...].astype(F32)
        r = lax.rsqrt(jnp.mean(v * v, axis=-1, keepdims=True) + EPS_RMS)
        gw = g * w_ref[...]
        du_ref[...] = r * gw - v * (r * r * r * jnp.mean(gw * v, axis=-1, keepdims=True))
        _acc_rows(dw_ref, jnp.sum(g * v * r, axis=0, keepdims=True))

    return pl.pallas_call(body, name=name, grid=(S // TR,), in_specs=[_row_spec(c), _vec_spec(c), _row_spec(c)],
                          out_specs=[_row_spec(c), _vec_spec(c)],
                          out_shape=[jax.ShapeDtypeStruct((S, c), F32), jax.ShapeDtypeStruct((1, c), F32)])(u, w, dy)


def _gate_norm_fwd(y, z, w):
    def body(y_ref, z_ref, w_ref, o_ref):
        zz = z_ref[...]
        v = y_ref[...] * (zz * _sigmoid(zz))
        r = lax.rsqrt(jnp.mean(v * v, axis=-1, keepdims=True) + EPS_RMS)
        o_ref[...] = (v * r * w_ref[...]).astype(BF16)

    c = SSD_INNER
    return pl.pallas_call(body, name="ssd_gate_norm_fwd", grid=(S // TR,), in_specs=[_row_spec(c), _row_spec(c), _vec_spec(c)],
                          out_specs=_row_spec(c), out_shape=jax.ShapeDtypeStruct((S, c), BF16))(y, z, w)


def _gate_norm_bwd(y, z, w, dout):
    def body(y_ref, z_ref, w_ref, g_ref, dy_ref, dz_ref, dw_ref):
        yy = y_ref[...]
        zz = z_ref[...]
        sg = _sigmoid(zz)
        sz = zz * sg
        v = yy * sz
        g = g_ref[...]
        r = lax.rsqrt(jnp.mean(v * v, axis=-1, keepdims=True) + EPS_RMS)
        gw = g * w_ref[...]
        dv = r * gw - v * (r * r * r * jnp.mean(gw * v, axis=-1, keepdims=True))
        dy_ref[...] = dv * sz
        dz_ref[...] = (dv * yy * (sg * (1.0 + zz * (1.0 - sg)))).astype(BF16)
        _acc_rows(dw_ref, jnp.sum(g * v * r, axis=0, keepdims=True))

    c = SSD_INNER
    return pl.pallas_call(body, name="ssd_gate_norm_bwd", grid=(S // TR,),
                          in_specs=[_row_spec(c), _row_spec(c), _vec_spec(c), _row_spec(c)],
                          out_specs=[_row_spec(c), _row_spec(c), _vec_spec(c)],
                          out_shape=[jax.ShapeDtypeStruct((S, c), F32), jax.ShapeDtypeStruct((S, c), BF16),
                                     jax.ShapeDtypeStruct((1, c), F32)])(y, z, w, dout)


def _ln_fwd(xr, mix, g, b):
    def body(x_ref, m_ref, g_ref, b_ref, o_ref, ob_ref):
        pre = ALPHA * x_ref[...] + m_ref[...]
        mu = jnp.mean(pre, axis=-1, keepdims=True)
        d = pre - mu
        rs = lax.rsqrt(jnp.mean(d * d, axis=-1, keepdims=True) + EPS_LN)
        h = d * rs * g_ref[...] + b_ref[...]
        o_ref[...] = h
        ob_ref[...] = h.astype(BF16)

    return pl.pallas_call(body, name="ln_mix_fwd", grid=(S // TR,), in_specs=[_row_spec(D), _row_spec(D), _vec_spec(D), _vec_spec(D)],
                          out_specs=[_row_spec(D)] * 2,
                          out_shape=[jax.ShapeDtypeStruct((S, D), F32), jax.ShapeDtypeStruct((S, D), BF16)])(xr, mix, g, b)


def _ln_bwd(xr, mix, g, dh):
    def body(x_ref, m_ref, g_ref, dh_ref, dpre_ref, dpreb_ref, dg_ref, db_ref):
        pre = ALPHA * x_ref[...] + m_ref[...]
        mu = jnp.mean(pre, axis=-1, keepdims=True)
        d = pre - mu
        rs = lax.rsqrt(jnp.mean(d * d, axis=-1, keepdims=True) + EPS_LN)
        xh = d * rs
        dy = dh_ref[...]
        gy = dy * g_ref[...]
        dpre = rs * (gy - jnp.mean(gy, axis=-1, keepdims=True) - xh * jnp.mean(gy * xh, axis=-1, keepdims=True))
        dpre_ref[...] = dpre
        dpreb_ref[...] = dpre.astype(BF16)
        _acc_rows(dg_ref, jnp.sum(dy * xh, axis=0, keepdims=True))
        _acc_rows(db_ref, jnp.sum(dy, axis=0, keepdims=True))

    return pl.pallas_call(body, name="ln_mix_bwd", grid=(S // TR,),
                          in_specs=[_row_spec(D), _row_spec(D), _vec_spec(D), _row_spec(D)],
                          out_specs=[_row_spec(D), _row_spec(D), _vec_spec(D), _vec_spec(D)],
                          out_shape=[jax.ShapeDtypeStruct((S, D), F32), jax.ShapeDtypeStruct((S, D), BF16),
                                     jax.ShapeDtypeStruct((1, D), F32), jax.ShapeDtypeStruct((1, D), F32)])(xr, mix, g, dh)


FF_CHUNK = D_FF // NCHIP


def _ff_spec():
    return pl.BlockSpec((None, TR * 2, FF_CHUNK), lambda k, i: (k, i, 0))


def _swiglu_fwd(gate, up):
    def body(g_ref, u_ref, o_ref):
        g = g_ref[...]
        o_ref[...] = (g * _sigmoid(g) * u_ref[...]).astype(BF16)

    return pl.pallas_call(body, name="swiglu_fwd", grid=(NCHIP, S // (2 * TR)), in_specs=[_ff_spec()] * 2, out_specs=_ff_spec(),
                          out_shape=jax.ShapeDtypeStruct((NCHIP, S, FF_CHUNK), BF16))(gate, up)


def _swiglu_bwd(gate, up, dact):
    def body(g_ref, u_ref, d_ref, dg_ref, du_ref):
        g = g_ref[...]
        sg = _sigmoid(g)
        d = d_ref[...]
        dg_ref[...] = (d * u_ref[...] * (sg * (1.0 + g * (1.0 - sg)))).astype(BF16)
        du_ref[...] = (d * g * sg).astype(BF16)

    return pl.pallas_call(body, name="swiglu_bwd", grid=(NCHIP, S // (2 * TR)), in_specs=[_ff_spec()] * 3, out_specs=[_ff_spec()] * 2,
                          out_shape=[jax.ShapeDtypeStruct((NCHIP, S, FF_CHUNK), BF16)] * 2)(gate, up, dact)


def _final_fwd_bwd(h1, ffn, pg, pp, target, g2, b2):
    def body(h_ref, f_ref, pg_ref, pp_ref, t_ref, g_ref, b_ref, dpre_ref, dpreb_ref, dpg_ref, dpp_ref, dg_ref, db_ref, loss_ref):
        sg = _sigmoid(pg_ref[...])
        ppv = pp_ref[...]
        pre = ALPHA * h_ref[...] + f_ref[...] + sg * ppv
        mu = jnp.mean(pre, axis=-1, keepdims=True)
        d = pre - mu
        rs = lax.rsqrt(jnp.mean(d * d, axis=-1, keepdims=True) + EPS_LN)
        xh = d * rs
        err = xh * g_ref[...] + b_ref[...] - t_ref[...]
        dy = err * (1.0 / D)
        gy = dy * g_ref[...]
        dpre = rs * (gy - jnp.mean(gy, axis=-1, keepdims=True) - xh * jnp.mean(gy * xh, axis=-1, keepdims=True))
        dpre_ref[...] = dpre
        dpreb_ref[...] = dpre.astype(BF16)
        dpg_ref[...] = (dpre * ppv * sg * (1.0 - sg)).astype(BF16)
        dpp_ref[...] = (dpre * sg).astype(BF16)
        _acc_rows(dg_ref, jnp.sum(dy * xh, axis=0, keepdims=True))
        _acc_rows(db_ref, jnp.sum(dy, axis=0, keepdims=True))
        _acc_rows(loss_ref, 0.5 * jnp.sum(jnp.mean(err * err, axis=-1, keepdims=True), axis=0, keepdims=True) * jnp.ones((1, LANE), F32))

    return pl.pallas_call(
        body, name="final_ln_loss", grid=(S // TR,),
        in_specs=[_row_spec(D)] * 5 + [_vec_spec(D)] * 2,
        out_specs=[_row_spec(D)] * 4 + [_vec_spec(D), _vec_spec(D), _vec_spec(LANE)],
        out_shape=[jax.ShapeDtypeStruct((S, D), F32)] + [jax.ShapeDtypeStruct((S, D), BF16)] * 3 + [
                   jax.ShapeDtypeStruct((1, D), F32), jax.ShapeDtypeStruct((1, D), F32), jax.ShapeDtypeStruct((1, LANE), F32)],
    )(h1, ffn, pg, pp, target, g2, b2)


def _rot(u, cos_t, sin_t, lane):
    partner = jnp.where(lane < NOPE + ROPE // 2, pltpu.roll(u, LANE - ROPE // 2, 1), pltpu.roll(u, ROPE // 2, 1))
    return u * cos_t + partner * sin_t


def _q_rope(qlin, cos_t, sin_t):
    def body(q_ref, c_ref, s_ref, o_ref):
        lane = lax.broadcasted_iota(jnp.int32, (TR, LANE), 1)
        c, s = c_ref[...], s_ref[...]
        for h in range(H):
            o_ref[:, h * LANE:(h + 1) * LANE] = _rot(q_ref[:, h * LANE:(h + 1) * LANE], c, s, lane).astype(BF16)

    w = H * LANE
    return pl.pallas_call(body, name="q_rope", grid=(S // TR,), in_specs=[_row_spec(w), _row_spec(LANE), _row_spec(LANE)],
                          out_specs=_row_spec(w), out_shape=jax.ShapeDtypeStruct((S, w), BF16))(qlin, cos_t, sin_t)


def _q_unrope(dq, cos_t, sin_t):
    def body(q_ref, c_ref, s_ref, o_ref):
        lane = lax.broadcasted_iota(jnp.int32, (TR, LANE), 1)
        c, s = c_ref[...], -s_ref[...]
        for h in range(H):
            o_ref[:, h * LANE:(h + 1) * LANE] = _rot(q_ref[:, h * LANE:(h + 1) * LANE], c, s, lane).astype(BF16)

    w = H * LANE
    return pl.pallas_call(body, name="q_unrope", grid=(S // TR,), in_specs=[_row_spec(w), _row_spec(LANE), _row_spec(LANE)],
                          out_specs=_row_spec(w), out_shape=jax.ShapeDtypeStruct((S, w), BF16))(dq, cos_t, sin_t)


def _k_prep(klin, small, cos_t, sin_t):
    def body(k_ref, kr_ref, c_ref, s_ref, o_ref):
        lane = lax.broadcasted_iota(jnp.int32, (TR, LANE), 1)
        kr = _rot(pltpu.roll(kr_ref[...], NOPE, 1), c_ref[...], s_ref[...], lane)
        for h in range(H):
            o_ref[:, h * LANE:(h + 1) * LANE] = (k_ref[:, h * LANE:(h + 1) * LANE] + kr).astype(BF16)

    w = H * LANE
    kr_spec = pl.BlockSpec((TR, LANE), lambda i: (i, SM_KR // LANE))
    return pl.pallas_call(body, name="k_prep", grid=(S // TR,), in_specs=[_row_spec(w), kr_spec, _row_spec(LANE), _row_spec(LANE)],
                          out_specs=_row_spec(w), out_shape=jax.ShapeDtypeStruct((S, w), BF16))(klin, small, cos_t, sin_t)


def _k_rope_bwd(dk, cos_t, sin_t):
    def body(k_ref, c_ref, s_ref, o_ref):
        lane = lax.broadcasted_iota(jnp.int32, (TR, LANE), 1)
        acc = k_ref[:, 0:LANE]
        for h in range(1, H):
            acc = acc + k_ref[:, h * LANE:(h + 1) * LANE]
        acc = jnp.where((lane >= NOPE) & (lane < NOPE + ROPE), acc, 0.0)
        o_ref[...] = pltpu.roll(_rot(acc, c_ref[...], -s_ref[...], lane), LANE - NOPE, 1)

    w = H * LANE
    return pl.pallas_call(body, name="k_rope_bwd", grid=(S // TR,), in_specs=[_row_spec(w), _row_spec(LANE), _row_spec(LANE)],
                          out_specs=_row_spec(LANE), out_shape=jax.ShapeDtypeStruct((S, LANE), F32))(dk, cos_t, sin_t)


CB = 256


def _shift_down(u, k, row):
    if k == 0:
        return u
    return jnp.where(row >= k, pltpu.roll(u, k, 0), 0.0)


def _shift_up(u, k, row):
    if k == 0:
        return u
    return jnp.where(row < S - k, pltpu.roll(u, S - k, 0), 0.0)


def _conv_fwd(u, w, b):
    def body(u_ref, w_ref, b_ref, o_ref):
        row = lax.broadcasted_iota(jnp.int32, (S, CB), 0)
        uu = u_ref[...]
        acc = b_ref[...] + w_ref[SSD_K - 1:SSD_K, :] * uu
        for k in range(SSD_K - 1):
            acc = acc + w_ref[k:k + 1, :] * _shift_down(uu, SSD_K - 1 - k, row)
        o_ref[...] = acc * _sigmoid(acc)

    c = u.shape[1]
    return pl.pallas_call(
        body, name="conv_fwd", grid=(c // CB,),
        in_specs=[pl.BlockSpec((S, CB), lambda j: (0, j)), pl.BlockSpec((SSD_K, CB), lambda j: (0, j)), pl.BlockSpec((1, CB), lambda j: (0, j))],
        out_specs=pl.BlockSpec((S, CB), lambda j: (0, j)), out_shape=jax.ShapeDtypeStruct((S, c), F32),
    )(u, w, b)


def _conv_bwd(u, w, b, dact):
    def body(u_ref, w_ref, b_ref, d_ref, du_ref, dw_ref, db_ref):
        row = lax.broadcasted_iota(jnp.int32, (S, CB), 0)
        uu = u_ref[...]
        sh = [_shift_down(uu, SSD_K - 1 - k, row) for k in range(SSD_K)]
        acc = b_ref[...]
        for k in range(SSD_K):
            acc = acc + w_ref[k:k + 1, :] * sh[k]
        sg = _sigmoid(acc)
        dacc = d_ref[...] * (sg * (1.0 + acc * (1.0 - sg)))
        du = w_ref[SSD_K - 1:SSD_K, :] * dacc
        for k in range(SSD_K - 1):
            du = du + w_ref[k:k + 1, :] * _shift_up(dacc, SSD_K - 1 - k, row)
        du_ref[...] = du.astype(BF16)
        for k in range(SSD_K):
            dw_ref[k:k + 1, :] = jnp.sum(dacc * sh[k], axis=0, keepdims=True)
        db_ref[...] = jnp.sum(dacc, axis=0, keepdims=True)

    c = u.shape[1]
    col = lambda r: pl.BlockSpec((r, CB), lambda j: (0, j))
    return pl.pallas_call(
        body, name="conv_bwd", grid=(c // CB,), in_specs=[col(S), col(SSD_K), col(1), col(S)], out_specs=[col(S), col(SSD_K), col(1)],
        out_shape=[jax.ShapeDtypeStruct((S, c), BF16), jax.ShapeDtypeStruct((SSD_K, c), F32), jax.ShapeDtypeStruct((1, c), F32)],
    )(u, w, b, dact)


NPAIR = H // 2
PAIRS_PER_GROUP = NPAIR // SSD_G


def _softplus(v):
    return jnp.maximum(v, 0.0) + jnp.log(1.0 + jnp.exp(-jnp.abs(v)))


def _dot(a, b, dims):
    return lax.dot_general(a.astype(BF16), b.astype(BF16), (dims, ((), ())), preferred_element_type=F32)


def _dot3(a, b, dims, split_lhs):
    v = a if split_lhs else b
    v1 = v.astype(BF16)
    r1 = v - v1.astype(F32)
    v2 = r1.astype(BF16)
    v3 = (r1 - v2.astype(F32)).astype(BF16)
    acc = None
    for part in (v1, v2, v3):
        lhs, rhs = (part, b) if split_lhs else (a, part)
        t = lax.dot_general(lhs, rhs, (dims, ((), ())), preferred_element_type=F32)
        acc = t if acc is None else acc + t
    return acc


def _ssd_chunk_common(dt_ref, dtT_ref, prow_ref, pcol_ref):
    prow = prow_ref[...]
    pcol = pcol_ref[...]
    ri = lax.broadcasted_iota(jnp.int32, (SSD_L, SSD_L), 0)
    ci = lax.broadcasted_iota(jnp.int32, (SSD_L, SSD_L), 1)
    causal = ri >= ci
    pre_c = dt_ref[...] + prow[0:1, :]
    dtc = _softplus(pre_c)
    a_row = -jnp.exp(prow[1:2, :])
    cs_col = _dot3(causal.astype(BF16), dtc * a_row, ((1,), (0,)), False)
    dtr = _softplus(dtT_ref[...] + pcol[:, 0:1])
    a_col = -jnp.exp(pcol[:, 1:2])
    cs_row = _dot3(dtr * a_col, (ri <= ci).astype(BF16), ((1,), (0,)), True)
    return prow, causal, pre_c, dtc, a_row, cs_col, cs_row


def _ssd_fwd(act, small, dtT, prow, pcol):
    def body(x_ref, b_ref, c_ref, dt_ref, dtT_ref, prow_ref, pcol_ref, y_ref, st_ref, state):
        @pl.when(pl.program_id(0) == 0)
        def _():
            state[...] = jnp.zeros_like(state)

        prow, causal, _, dtc, _, cs_col, cs_row = _ssd_chunk_common(dt_ref, dtT_ref, prow_ref, pcol_ref)
        lo = lax.broadcasted_iota(jnp.int32, (SSD_L, LANE), 1) < SSD_P
        lo1 = lo[0:1, :]
        for g in range(SSD_G):
            bm = b_ref[:, g * SSD_N:(g + 1) * SSD_N]
            cm = c_ref[:, g * SSD_N:(g + 1) * SSD_N]
            cb = _dot(cm, bm, ((1,), (1,)))
            for qq in range(PAIRS_PER_GROUP):
                q = g * PAIRS_PER_GROUP + qq
                ha, hb = 2 * q, 2 * q + 1
                csa, csb = cs_col[:, ha:ha + 1], cs_col[:, hb:hb + 1]
                xp = x_ref[:, q * LANE:(q + 1) * LANE]
                xx = xp * jnp.where(lo, dtc[:, ha:ha + 1], dtc[:, hb:hb + 1])
                ga = cb * jnp.exp(jnp.where(causal, csa - cs_row[ha:ha + 1, :], NEG))
                gb = cb * jnp.exp(jnp.where(causal, csb - cs_row[hb:hb + 1, :], NEG))
                y = _dot(ga, jnp.where(lo, xx, 0.0), ((1,), (0,))) + _dot(gb, jnp.where(lo, 0.0, xx), ((1,), (0,)))
                s_in = state[q]
                y = y + _dot(cm, s_in, ((1,), (0,))) * jnp.where(lo, jnp.exp(csa), jnp.exp(csb))
                y = y + jnp.where(lo1, prow[2:3, ha:ha + 1], prow[2:3, hb:hb + 1]) * xp
                y_ref[:, q * LANE:(q + 1) * LANE] = y
                la, lb = csa[SSD_L - 1:SSD_L, :], csb[SSD_L - 1:SSD_L, :]
                decay = jnp.where(lo, jnp.exp(la - csa), jnp.exp(lb - csb))
                st_ref[q] = s_in
                state[q] = s_in * jnp.where(lo1, jnp.exp(la), jnp.exp(lb)) + _dot(bm, xx * decay, ((0,), (0,)))

    L = SSD_L
    return pl.pallas_call(
        body, name="ssd_fwd", grid=(SSD_NC,),
        in_specs=[pl.BlockSpec((L, SSD_INNER), lambda c: (c, 0)),
                  pl.BlockSpec((L, SSD_G * SSD_N), lambda c: (c, SSD_INNER // (SSD_G * SSD_N))),
                  pl.BlockSpec((L, SSD_G * SSD_N), lambda c: (c, SSD_INNER // (SSD_G * SSD_N) + 1)),
                  pl.BlockSpec((L, LANE), lambda c: (c, SM_DT // LANE)),
                  pl.BlockSpec((LANE, L), lambda c: (0, c)),
                  pl.BlockSpec((8, LANE), lambda c: (0, 0)), pl.BlockSpec((LANE, 8), lambda c: (0, 0))],
        out_specs=[pl.BlockSpec((L, SSD_INNER), lambda c: (c, 0)),
                   pl.BlockSpec((None, NPAIR, SSD_N, LANE), lambda c: (c, 0, 0, 0))],
        out_shape=[jax.ShapeDtypeStruct((S, SSD_INNER), F32), jax.ShapeDtypeStruct((SSD_NC, NPAIR, SSD_N, LANE), F32)],
        scratch_shapes=[pltpu.VMEM((NPAIR, SSD_N, LANE), F32)],
        compiler_params=pltpu.CompilerParams(dimension_semantics=("arbitrary",)),
    )(act, act, act, small, dtT, prow, pcol)


def _ssd_bwd(act, small, dtT, prow, pcol, states, dy):
    def body(x_ref, b_ref, c_ref, dt_ref, dtT_ref, prow_ref, pcol_ref, st_ref, dy_ref,
             dx_ref, ddt_ref, dp_ref, dstate):
        @pl.when(pl.program_id(0) == 0)
        def _():
            dstate[...] = jnp.zeros_like(dstate)
            dp_ref[...] = jnp.zeros_like(dp_ref)

        prow, causal, pre_c, dtc, a_row, cs_col, cs_row = _ssd_chunk_common(dt_ref, dtT_ref, prow_ref, pcol_ref)
        lane = lax.broadcasted_iota(jnp.int32, (SSD_L, LANE), 1)
        sub = lax.broadcasted_iota(jnp.int32, (LANE, SSD_L), 0)
        rowi = lax.broadcasted_iota(jnp.int32, (SSD_L, 1), 0)
        lane1 = lane[0:1, :]
        lo = lane < SSD_P
        lo1 = lo[0:1, :]
        dcs_c = jnp.zeros((SSD_L, LANE), F32)
        dcs_r = jnp.zeros((LANE, SSD_L), F32)
        ddt_x = jnp.zeros((SSD_L, LANE), F32)
        dd_row = jnp.zeros((1, LANE), F32)
        for g in range(SSD_G):
            bm = b_ref[:, g * SSD_N:(g + 1) * SSD_N]
            cm = c_ref[:, g * SSD_N:(g + 1) * SSD_N]
            cb = _dot(cm, bm, ((1,), (1,)))
            dcb = jnp.zeros((SSD_L, SSD_L), F32)
            dbm = jnp.zeros((SSD_L, SSD_N), F32)
            dcm = jnp.zeros((SSD_L, SSD_N), F32)
            for qq in range(PAIRS_PER_GROUP):
                q = g * PAIRS_PER_GROUP + qq
                ha, hb = 2 * q, 2 * q + 1
                csa, csb = cs_col[:, ha:ha + 1], cs_col[:, hb:hb + 1]
                xp = x_ref[:, q * LANE:(q + 1) * LANE]
                dtp = jnp.where(lo, dtc[:, ha:ha + 1], dtc[:, hb:hb + 1])
                xx = xp * dtp
                lma = jnp.exp(jnp.where(causal, csa - cs_row[ha:ha + 1, :], NEG))
                lmb = jnp.exp(jnp.where(causal, csb - cs_row[hb:hb + 1, :], NEG))
                ga, gb = cb * lma, cb * lmb
                dyp = dy_ref[:, q * LANE:(q + 1) * LANE]
                dya, dyb = jnp.where(lo, dyp, 0.0), jnp.where(lo, 0.0, dyp)
                s_in = st_ref[q]
                ds_out = dstate[q]
                la, lb = csa[SSD_L - 1:SSD_L, :], csb[SSD_L - 1:SSD_L, :]
                ecs = jnp.where(lo, jnp.exp(csa), jnp.exp(csb))
                decay = jnp.where(lo, jnp.exp(la - csa), jnp.exp(lb - csb))
                cd = jnp.where(lo1, jnp.exp(la), jnp.exp(lb))
                bds = _dot(bm, ds_out, ((1,), (0,)))
                dxx = _dot(ga, dya, ((0,), (0,))) + _dot(gb, dyb, ((0,), (0,))) + bds * decay
                dga = _dot(dya, xx, ((1,), (1,)))
                dgb = _dot(dyb, xx, ((1,), (1,)))
                dsega, dsegb = dga * ga, dgb * gb
                dcb = dcb + dga * lma + dgb * lmb
                yoff = _dot(cm, s_in, ((1,), (0,))) * ecs
                dye = dyp * ecs
                dcm = dcm + _dot(dye, s_in, ((1,), (1,)))
                xd = xx * decay
                dbm = dbm + _dot(xd, ds_out, ((1,), (1,)))
                wv = xd * bds
                t1 = dyp * yoff - wv
                col_a = (jnp.sum(dsega, axis=1, keepdims=True) + jnp.sum(jnp.where(lo, t1, 0.0), axis=1, keepdims=True))
                col_b = (jnp.sum(dsegb, axis=1, keepdims=True) + jnp.sum(jnp.where(lo, 0.0, t1), axis=1, keepdims=True))
                sprod = ds_out * s_in
                end_a = jnp.sum(jnp.where(lo, wv, 0.0), keepdims=True) + jnp.exp(la) * jnp.sum(jnp.where(lo[:SSD_N], sprod, 0.0), keepdims=True)
                end_b = jnp.sum(jnp.where(lo, 0.0, wv), keepdims=True) + jnp.exp(lb) * jnp.sum(jnp.where(lo[:SSD_N], 0.0, sprod), keepdims=True)
                col_a = col_a + jnp.where(rowi == SSD_L - 1, end_a, 0.0)
                col_b = col_b + jnp.where(rowi == SSD_L - 1, end_b, 0.0)
                dcs_c = dcs_c + jnp.where(lane == ha, col_a, 0.0) + jnp.where(lane == hb, col_b, 0.0)
                dcs_r = (dcs_r + jnp.where(sub == ha, jnp.sum(dsega, axis=0, keepdims=True), 0.0)
                         + jnp.where(sub == hb, jnp.sum(dsegb, axis=0, keepdims=True), 0.0))
                dstate[q] = _dot(cm, dye, ((0,), (0,))) + cd * ds_out
                dpair = jnp.where(lo1, prow[2:3, ha:ha + 1], prow[2:3, hb:hb + 1])
                dx_ref[:, q * LANE:(q + 1) * LANE] = dxx * dtp + dpair * dyp
                t2 = dxx * xp
                ddt_x = (ddt_x + jnp.where(lane == ha, jnp.sum(jnp.where(lo, t2, 0.0), axis=1, keepdims=True), 0.0)
                         + jnp.where(lane == hb, jnp.sum(jnp.where(lo, 0.0, t2), axis=1, keepdims=True), 0.0))
                t3 = dyp * xp
                dd_row = (dd_row + jnp.where(lane1 == ha, jnp.sum(jnp.where(lo, t3, 0.0), keepdims=True), 0.0)
                          + jnp.where(lane1 == hb, jnp.sum(jnp.where(lo, 0.0, t3), keepdims=True), 0.0))
            dx_ref[:, SSD_INNER + g * SSD_N:SSD_INNER + (g + 1) * SSD_N] = dbm + _dot(dcb, cm, ((0,), (0,)))
            dx_ref[:, SSD_INNER + (SSD_G + g) * SSD_N:SSD_INNER + (SSD_G + g + 1) * SSD_N] = dcm + _dot(dcb, bm, ((1,), (0,)))
        ri = lax.broadcasted_iota(jnp.int32, (SSD_L, SSD_L), 0)
        ci = lax.broadcasted_iota(jnp.int32, (SSD_L, SSD_L), 1)
        da = _dot3((ri <= ci).astype(BF16), dcs_c, ((1,), (0,)), False)
        da = da - _dot3(dcs_r, causal.astype(BF16), ((1,), (0,)), True).T
        ddt = ddt_x + da * a_row
        ddt_raw = ddt * _sigmoid(pre_c)
        ddt_ref[...] = ddt_raw
        da_head = jnp.sum(da * dtc, axis=0, keepdims=True) * a_row
        dp_ref[0:1, :] += jnp.sum(ddt_raw, axis=0, keepdims=True)
        dp_ref[1:2, :] += da_head
        dp_ref[2:3, :] += dd_row

    L = SSD_L
    rev = SSD_NC - 1
    bc_cols = SSD_INNER // (SSD_G * SSD_N)
    return pl.pallas_call(
        body, name="ssd_bwd", grid=(SSD_NC,),
        in_specs=[pl.BlockSpec((L, SSD_INNER), lambda c: (rev - c, 0)),
                  pl.BlockSpec((L, SSD_G * SSD_N), lambda c: (rev - c, bc_cols)),
                  pl.BlockSpec((L, SSD_G * SSD_N), lambda c: (rev - c, bc_cols + 1)),
                  pl.BlockSpec((L, LANE), lambda c: (rev - c, SM_DT // LANE)),
                  pl.BlockSpec((LANE, L), lambda c: (0, rev - c)),
                  pl.BlockSpec((8, LANE), lambda c: (0, 0)), pl.BlockSpec((LANE, 8), lambda c: (0, 0)),
                  pl.BlockSpec((None, NPAIR, SSD_N, LANE), lambda c: (rev - c, 0, 0, 0)),
                  pl.BlockSpec((L, SSD_INNER), lambda c: (rev - c, 0))],
        out_specs=[pl.BlockSpec((L, SSD_XBC), lambda c: (rev - c, 0)),
                   pl.BlockSpec((L, LANE), lambda c: (rev - c, 0)),
                   pl.BlockSpec((8, LANE), lambda c: (0, 0))],
        out_shape=[jax.ShapeDtypeStruct((S, SSD_XBC), F32), jax.ShapeDtypeStruct((S, LANE), F32),
                   jax.ShapeDtypeStruct((8, LANE), F32)],
        scratch_shapes=[pltpu.VMEM((NPAIR, SSD_N, LANE), F32)],
        compiler_params=pltpu.CompilerParams(dimension_semantics=("arbitrary",)),
    )(act, act, act, small, dtT, prow, pcol, states, dy)


TQ = 256
TK = 256
FWD_TQ = 256
FWD_TK = 256


def _attn_fwd(qc, kc, v):
    TQ, TK = FWD_TQ, FWD_TK

    def body(q_ref, k_ref, v_ref, o_ref, lse_ref):
        i = pl.program_id(1)
        lo = lax.broadcasted_iota(jnp.int32, (TQ, LANE), 1) < VDIM
        lo_k = lax.broadcasted_iota(jnp.int32, (TK, LANE), 1) < VDIM
        row_minus_col = lax.broadcasted_iota(jnp.int32, (TQ, TK), 0) - lax.broadcasted_iota(jnp.int32, (TQ, TK), 1)
        qa, qb = q_ref[:, 0:LANE], q_ref[:, LANE:2 * LANE]

        def scores(kb):
            kk = k_ref[pl.ds(pl.multiple_of(kb * TK, TK), TK), :]
            return (_dot(qa, kk[:, 0:LANE], ((1,), (1,))) * ATT_SCALE_LOG2, _dot(qb, kk[:, LANE:2 * LANE], ((1,), (1,))) * ATT_SCALE_LOG2)

        def update(kb, sa, sb, stats):
            ma, la, mb, lb, acc = stats
            vv = v_ref[pl.ds(pl.multiple_of(kb * TK, TK), TK), :]
            na = jnp.maximum(ma, jnp.max(sa, axis=1, keepdims=True))
            nb = jnp.maximum(mb, jnp.max(sb, axis=1, keepdims=True))
            pa, pb = jnp.exp2(sa - na), jnp.exp2(sb - nb)
            fa, fb = jnp.exp2(ma - na), jnp.exp2(mb - nb)
            la = fa * la + jnp.sum(pa, axis=1, keepdims=True)
            lb = fb * lb + jnp.sum(pb, axis=1, keepdims=True)
            acc = (acc * jnp.where(lo, fa, fb) + _dot(pa, jnp.where(lo_k, vv, 0), ((1,), (0,)))
                   + _dot(pb, jnp.where(lo_k, 0, vv), ((1,), (0,))))
            return na, la, nb, lb, acc

        def step(kb, carry):
            sa, sb = carry[:2]
            nxt = scores(kb + 1)
            return nxt + update(kb, sa, sb, carry[2:])

        neg = jnp.full((TQ, 1), NEG, F32)
        zero = jnp.zeros((TQ, 1), F32)
        n_full = i * (TQ // TK)
        carry = lax.fori_loop(0, n_full, step, scores(0) + (neg, zero, neg, zero, jnp.zeros((TQ, LANE), F32)))
        s, stats = carry[:2], carry[2:]
        for d in range(TQ // TK):
            nxt = scores(n_full + d + 1) if d + 1 < TQ // TK else None
            sa, sb = (jnp.where(row_minus_col >= d * TK, t, NEG) for t in s)
            stats = update(n_full + d, sa, sb, stats)
            s = nxt
        ma, la, mb, lb, acc = stats
        o_ref[...] = acc / jnp.where(lo, la, lb)
        lse_ref[...] = jnp.where(lo, ma + jnp.log2(la), mb + jnp.log2(lb)) * LN2

    return pl.pallas_call(
        body, name="attn_fwd", grid=(NPAIR, S // TQ),
        in_specs=[pl.BlockSpec((TQ, 2 * LANE), lambda j, i: (i, j)), pl.BlockSpec((S, 2 * LANE), lambda j, i: (0, j)),
                  pl.BlockSpec((S, LANE), lambda j, i: (0, j))],
        out_specs=[pl.BlockSpec((TQ, LANE), lambda j, i: (i, j)), pl.BlockSpec((None, TQ, LANE), lambda j, i: (j, i, 0))],
        out_shape=[jax.ShapeDtypeStruct((S, H * VDIM), F32), jax.ShapeDtypeStruct((NPAIR, S, LANE), F32)],
        compiler_params=pltpu.CompilerParams(dimension_semantics=("parallel", "parallel")),
    )(qc, kc, v)


def _attn_bwd(qc, kc, v, o, lse, do):
    nq = S // TQ

    def body(q_ref, k_ref, v_ref, o_ref, lse_ref, do_ref, dq_ref, dk_ref, dv_ref):
        kb = pl.program_id(1)

        @pl.when(kb == 0)
        def _():
            dq_ref[...] = jnp.zeros_like(dq_ref)

        lo = lax.broadcasted_iota(jnp.int32, (TQ, LANE), 1) < VDIM
        r0 = lax.broadcasted_iota(jnp.int32, (TQ, TK), 0)
        ck = kb * TK + lax.broadcasted_iota(jnp.int32, (TQ, TK), 1)
        ka, kbb = k_ref[:, 0:LANE], k_ref[:, LANE:2 * LANE]
        vv = v_ref[...]

        def step(qi, carry):
            dka, dkb, dv = carry
            off = pl.multiple_of(qi * TQ, TQ)
            qq = q_ref[pl.ds(off, TQ), :]
            dd = do_ref[pl.ds(off, TQ), :]
            ls = lse_ref[pl.ds(off, TQ), :]
            t = dd * o_ref[pl.ds(off, TQ), :]
            mask = r0 + qi * TQ >= ck
            outs = []
            for x, (kx, lsx) in enumerate(((ka, ls[:, 0:1]), (kbb, ls[:, VDIM:VDIM + 1]))):
                sel = lo if x == 0 else jnp.logical_not(lo)
                qx = qq[:, x * LANE:(x + 1) * LANE]
                dox = jnp.where(sel, dd, 0.0)
                delta = jnp.sum(jnp.where(sel, t, 0.0), axis=1, keepdims=True)
                sc = jnp.where(mask, _dot(qx, kx, ((1,), (1,))) * ATT_SCALE, NEG)
                p = jnp.exp(sc - lsx)
                dp = _dot(dox, vv, ((1,), (1,)))
                ds = p * (dp - delta) * ATT_SCALE
                dv = dv + _dot(p, dox, ((0,), (0,)))
                outs.append(_dot(ds, qx, ((0,), (0,))))
                dq_ref[pl.ds(off, TQ), x * LANE:(x + 1) * LANE] += _dot(ds, kx, ((1,), (0,)))
            return dka + outs[0], dkb + outs[1], dv

        z = jnp.zeros((TK, LANE), F32)
        dka, dkb, dv = lax.fori_loop(kb, nq, step, (z, z, z))
        dk_ref[:, 0:LANE] = dka
        dk_ref[:, LANE:2 * LANE] = dkb
        dv_ref[...] = dv

    return pl.pallas_call(
        body, name="attn_bwd", grid=(NPAIR, S // TK),
        in_specs=[pl.BlockSpec((S, 2 * LANE), lambda j, k: (0, j)), pl.BlockSpec((TK, 2 * LANE), lambda j, k: (k, j)),
                  pl.BlockSpec((TK, LANE), lambda j, k: (k, j)), pl.BlockSpec((S, LANE), lambda j, k: (0, j)),
                  pl.BlockSpec((None, S, LANE), lambda j, k: (j, 0, 0)), pl.BlockSpec((S, LANE), lambda j, k: (0, j))],
        out_specs=[pl.BlockSpec((S, 2 * LANE), lambda j, k: (0, j)), pl.BlockSpec((TK, 2 * LANE), lambda j, k: (k, j)),
                   pl.BlockSpec((TK, LANE), lambda j, k: (k, j))],
        out_shape=[jax.ShapeDtypeStruct((S, H * LANE), F32), jax.ShapeDtypeStruct((S, H * LANE), F32),
                   jax.ShapeDtypeStruct((S, H * VDIM), F32)],
        compiler_params=pltpu.CompilerParams(dimension_semantics=("parallel", "arbitrary")),
    )(qc, kc, v, o, lse, do)


_IN_Z, _IN_XBC, _IN_DT, _IN_Q, _IN_KV, _IN_KR = 0, 1024, 2560, 2576, 2960, 3216


def _prep_weights(w_in, w_qb, w_kvb):
    dt = w_in.dtype
    w_small = jnp.concatenate(
        [w_in[:, _IN_Q:_IN_KV], w_in[:, _IN_KV:_IN_KR], w_in[:, _IN_KR:IN_WIDTH], jnp.zeros((D, LANE - ROPE), dt),
         w_in[:, _IN_DT:_IN_Q], jnp.zeros((D, LANE - H), dt)], axis=1)
    w_q = jnp.pad(w_qb.reshape(Q_RANK, H, NOPE + ROPE), ((0, 0), (0, 0), (0, LANE - NOPE - ROPE))).reshape(Q_RANK, H * LANE)
    kv3 = w_kvb.reshape(KV_RANK, H, NOPE + VDIM)
    w_k = jnp.pad(kv3[:, :, :NOPE], ((0, 0), (0, 0), (0, LANE - NOPE))).reshape(KV_RANK, H * LANE)
    w_v = kv3[:, :, NOPE:].reshape(KV_RANK, H * VDIM)
    return w_in[:, _IN_Z:_IN_XBC], w_in[:, _IN_XBC:_IN_DT], w_small, w_q, w_k, w_v


def _rope_tables(positions):
    inv_freq = 1.0 / (10000.0 ** (jnp.arange(0, ROPE, 2, dtype=F32) / ROPE))
    ang = positions.astype(F32).reshape(S, 1) * inv_freq
    cos, sin = jnp.cos(ang), jnp.sin(ang)
    cos_t = jnp.concatenate([jnp.ones((S, NOPE), F32), cos, cos, jnp.ones((S, LANE - NOPE - ROPE), F32)], axis=1)
    sin_t = jnp.concatenate([jnp.zeros((S, NOPE), F32), -sin, sin, jnp.zeros((S, LANE - NOPE - ROPE), F32)], axis=1)
    return cos_t, sin_t


def _local_step(x, p, positions, target, gw, late_weights, send_late, sp):
    w_z, w_xbc, w_small, w_q, w_k, w_v = _prep_weights(_from_cols(gw["w_in"]), _from_cols(gw["w_qb"]), _from_cols(gw["w_kvb"]))
    w_out_s = gw["w_out"][:NCHIP // 2].reshape(SSD_INNER, D)
    w_out_m = gw["w_out"][NCHIP // 2:].reshape(SSD_INNER, D)
    cos_t, sin_t = _rope_tables(positions)
    prow = jnp.zeros((8, LANE), F32).at[0, :H].set(sp["dt_bias"][0]).at[1, :H].set(sp["A_log"][0]).at[2, :H].set(sp["D"][0])
    pcol = prow.T

    xb, pb = x.astype(BF16), p.astype(BF16)
    z = _mm([(xb, w_z)], name="proj_z")
    xbc = _mm([(xb, w_xbc)], name="proj_xbc")
    small = _mm([(xb, w_small)], name="proj_small")
    act = _conv_fwd(xbc, sp["conv_w"], sp["conv_b"])
    dt_t = small[:, SM_DT:SM_DT + LANE].T
    y, states = _ssd_fwd(act, small, dt_t, prow, pcol)
    y_ssd = _gate_norm_fwd(y, z, sp["ssd_norm"])
    q_c, kv_c = small[:, SM_Q:SM_Q + Q_RANK], small[:, SM_KV:SM_KV + KV_RANK]
    qn = _rms_fwd(q_c, sp["q_norm"], name="q_norm_fwd")
    kvn = _rms_fwd(kv_c, sp["kv_norm"], name="kv_norm_fwd")
    qcat = _q_rope(_mm([(qn, w_q)], name="q_up"), cos_t, sin_t)
    kcat = _k_prep(_mm([(kvn, w_k)], name="k_up"), small, cos_t, sin_t)
    v = _mm([(kvn, w_v)], out_dtype=BF16, name="v_up")
    o, lse = _attn_fwd(qcat, kcat, v)
    y_mla = _rms_fwd(o, sp["out_norm"], name="out_norm_fwd")
    mix = _mm([(y_ssd, w_out_s), (y_mla, w_out_m)], name="out_proj")
    h1, h1b = _ln_fwd(x, mix, sp["ln_mix_g"], sp["ln_mix_b"])
    gl = late_weights(h1b)
    w_pg, w_pp = gl["w_pg"].reshape(D, D), _from_cols(gl["w_pp"])
    w_gate, w_up, w_down = gl["w_gate"], gl["w_up"], gl["w_down"]
    gate = _mm([(h1b, w_gate)], chunk="out", name="ffn_gate")
    up = _mm([(h1b, w_up)], chunk="out", name="ffn_up")
    actf = _swiglu_fwd(gate, up)
    ffn = _mm([(actf, w_down)], chunk="sum", name="ffn_down")
    pg = _mm([(h1b, w_pg)], name="ple_gate")
    pp = _mm([(pb, w_pp)], name="ple_proj")
    dpre2, dpre2b, dpg, dpp, dg2, db2, loss_row = _final_fwd_bwd(h1, ffn, pg, pp, target, sp["ln_ffn_g"], sp["ln_ffn_b"])

    g = {"ln_ffn_g": dg2, "ln_ffn_b": db2}
    g["w_pp"] = _to_cols(_mm([(pb, dpp)], ta=True, out_dtype=BF16, name="d_w_ple_proj"))
    g["w_pg"] = _mm([(h1b, dpg)], ta=True, out_dtype=BF16, name="d_w_ple_gate").reshape(NCHIP, D // NCHIP, D)
    g["w_down"] = _mm([(actf, dpre2b)], ta=True, chunk="out", out_dtype=BF16, name="d_w_down")
    dactf = _mm([(dpre2b, w_down)], tb=True, chunk="out", name="d_act")
    dgate, dup = _swiglu_bwd(gate, up, dactf)
    g["w_gate"] = _mm([(h1b, dgate)], ta=True, chunk="out", out_dtype=BF16, name="d_w_gate")
    g["w_up"] = _mm([(h1b, dup)], ta=True, chunk="out", out_dtype=BF16, name="d_w_up")
    sent = send_late({name: g.pop(name) for name in LATE})
    dh1 = _mm([(dpg, w_pg)], tb=True, add=dpre2, add_scale=ALPHA, name="d_h1_ple")
    dh1 = _mm([(dgate, w_gate), (dup, w_up)], tb=True, chunk="sum", add=dh1, name="d_h1")
    dpre1, dpre1b, g["ln_mix_g"], g["ln_mix_b"] = _ln_bwd(x, mix, sp["ln_mix_g"] + sent, dh1)
    dy_ssd = _mm([(dpre1b, w_out_s)], tb=True, name="d_y_ssd")
    dy_mla = _mm([(dpre1b, w_out_m)], tb=True, name="d_y_mla")
    g["w_out"] = jnp.concatenate([_mm([(y_ssd, dpre1b)], ta=True, out_dtype=BF16, name="d_w_out_s"),
                                  _mm([(y_mla, dpre1b)], ta=True, out_dtype=BF16, name="d_w_out_m")],
                                 axis=0).reshape(NCHIP, 2 * SSD_INNER // NCHIP, D)
    do, g["out_norm"] = _rms_bwd(o, sp["out_norm"], dy_mla, name="out_norm_bwd")
    dq, dk, dv = _attn_bwd(qcat, kcat, v, o, lse, do)
    dqlin = _q_unrope(dq, cos_t, sin_t)
    dw_q = _mm([(qn, dqlin)], ta=True, out_dtype=BF16, name="d_w_q")
    dqn = _mm([(dqlin, w_q)], tb=True, name="d_qn")
    dq_c, g["q_norm"] = _rms_bwd(q_c, sp["q_norm"], dqn, name="q_norm_bwd")
    dkr = _k_rope_bwd(dk, cos_t, sin_t)
    dw_k = _mm([(kvn, dk)], ta=True, out_dtype=BF16, name="d_w_k")
    dw_v = _mm([(kvn, dv)], ta=True, out_dtype=BF16, name="d_w_v")
    dkvn = _mm([(dk, w_k), (dv, w_v)], tb=True, name="d_kvn")
    dkv_c, g["kv_norm"] = _rms_bwd(kv_c, sp["kv_norm"], dkvn, name="kv_norm_bwd")
    g["w_qb"] = _to_cols(dw_q.reshape(Q_RANK, H, LANE)[:, :, :NOPE + ROPE].reshape(Q_RANK, H * (NOPE + ROPE)))
    g["w_kvb"] = _to_cols(jnp.concatenate([dw_k.reshape(KV_RANK, H, LANE)[:, :, :NOPE], dw_v.reshape(KV_RANK, H, VDIM)],
                                          axis=2).reshape(KV_RANK, H * (NOPE + VDIM)))
    dy, dz, g["ssd_norm"] = _gate_norm_bwd(y, z, sp["ssd_norm"], dy_ssd)
    dact, ddt, dprow = _ssd_bwd(act, small, dt_t, prow, pcol, states, dy)
    g["dt_bias"], g["A_log"], g["D"] = dprow[0:1, :H], dprow[1:2, :H], dprow[2:3, :H]
    dxbc, g["conv_w"], g["conv_b"] = _conv_bwd(xbc, sp["conv_w"], sp["conv_b"], dact)
    dsmall = jnp.concatenate([dq_c, dkv_c, dkr, ddt], axis=1).astype(BF16)
    grad_x = _mm([(dz, w_z), (dxbc, w_xbc), (dsmall, w_small)], tb=True, add=dpre1, add_scale=ALPHA, name="d_x")
    dw_small = _mm([(xb, dsmall)], ta=True, out_dtype=BF16, name="d_w_small")
    g["w_in"] = _to_cols(jnp.concatenate(
        [_mm([(xb, dz)], ta=True, out_dtype=BF16, name="d_w_z"), _mm([(xb, dxbc)], ta=True, out_dtype=BF16, name="d_w_xbc"),
         dw_small[:, SM_DT:SM_DT + H], dw_small[:, SM_Q:SM_Q + Q_RANK], dw_small[:, SM_KV:SM_KV + KV_RANK],
         dw_small[:, SM_KR:SM_KR + ROPE]], axis=1))
    return loss_row, grad_x, g


MESH = pl.DeviceIdType.MESH
BIG = (("w_in", (D, IN_WIDTH), 1), ("w_qb", (Q_RANK, H * (NOPE + ROPE)), 1), ("w_kvb", (KV_RANK, H * (NOPE + VDIM)), 1),
       ("w_out", (2 * SSD_INNER, D), 0), ("w_gate", (D, D_FF), 1), ("w_up", (D, D_FF), 1), ("w_down", (D_FF, D), 0),
       ("w_pg", (D, D), 0), ("w_pp", (PLE, D), 1))
CONV_SHARD = SSD_XBC // NCHIP
BF16_ROWS = 16


def _from_cols(stack):
    return jnp.concatenate([stack[k] for k in range(NCHIP)], axis=1)


def _to_cols(full):
    r, c4 = full.shape
    return full.reshape(r, NCHIP, c4 // NCHIP).transpose(1, 0, 2)


def _coords():
    return lax.axis_index("x"), lax.axis_index("y"), lax.axis_index("c")


def _peers():
    x, y, c = _coords()
    return 2 * x + y, c, [(1 - x, y), (x, 1 - y), (1 - x, 1 - y)], (x, y, 1 - c)


def _half(c, rows):
    return pl.ds(pl.multiple_of(c * (rows // 2), BF16_ROWS), rows // 2)


def _gather_weights(shards):
    n_arr = len(shards)
    split = [s.shape[0] % (2 * BF16_ROWS) == 0 for s in shards]
    per = 2 * (NCHIP - 1)

    def body(*refs):
        ins, outs = refs[:n_arr], refs[n_arr:2 * n_arr]
        send_sems, recv_sems, local_sems = refs[2 * n_arr:]
        k, c, chips, sibling = _peers()

        def copy(idx, src, dst, to):
            return pltpu.make_async_remote_copy(src_ref=src, dst_ref=dst, send_sem=send_sems.at[idx], recv_sem=recv_sems.at[idx],
                                                device_id=to, device_id_type=MESH)

        def part(a, chip, core):
            return outs[a].at[chip, _half(core, shards[a].shape[0])] if split[a] else outs[a].at[chip]

        mine = [pltpu.make_async_copy(ins[a], outs[a].at[k], local_sems.at[a]) for a in range(n_arr)]
        for cp in mine:
            cp.start()
        sends = []
        for a in range(n_arr):
            src = ins[a].at[_half(c, shards[a].shape[0])] if split[a] else ins[a]
            for j, (cx, cy) in enumerate(chips):
                sends.append(copy(per * a + j, src, part(a, k, c), (cx, cy, c)))
                sends[-1].start()
        for j, (cx, cy) in enumerate(chips):
            for a in range(n_arr):
                landed = part(a, 2 * cx + cy, c)
                copy(per * a + j, landed, landed, (cx, cy, c)).wait_recv()
                if split[a]:
                    sends.append(copy(per * a + NCHIP - 1 + j, landed, landed, sibling))
                    sends[-1].start()
        for j, (cx, cy) in enumerate(chips):
            for a in range(n_arr):
                if split[a]:
                    other = part(a, 2 * cx + cy, 1 - c)
                    copy(per * a + NCHIP - 1 + j, other, other, sibling).wait_recv()
        for cp in sends:
            cp.wait_send()
        for cp in mine:
            cp.wait()

    any_spec = pl.BlockSpec(memory_space=pl.ANY)
    return pl.pallas_call(
        body, name="gather_weights", in_specs=[any_spec] * n_arr, out_specs=[any_spec] * n_arr,
        out_shape=[jax.ShapeDtypeStruct((NCHIP,) + s.shape, s.dtype) for s in shards],
        scratch_shapes=[pltpu.SemaphoreType.DMA((per * n_arr,)), pltpu.SemaphoreType.DMA((per * n_arr,)),
                        pltpu.SemaphoreType.DMA((n_arr,))],
    )(*shards)


def _reduce_grads(stacks):
    n_arr = len(stacks)
    dims = [s.shape[1:] for s in stacks]
    per = NCHIP + 1

    def body(*refs):
        ins, fin, r1, part, r2 = (refs[i * n_arr:(i + 1) * n_arr] for i in range(5))
        send_sems, recv_sems, local_sems = refs[5 * n_arr:]
        k, c, chips, sibling = _peers()

        def copy(idx, src, dst, to):
            return pltpu.make_async_remote_copy(src_ref=src, dst_ref=dst, send_sem=send_sems.at[idx], recv_sem=recv_sems.at[idx],
                                                device_id=to, device_id_type=MESH)

        pairs = [copy(per * a, ins[a].at[:, _half(1 - c, dims[a][0])], r1[a], sibling) for a in range(n_arr)]
        for cp in pairs:
            cp.start()
        sends, own = [], []
        for a in range(n_arr):
            hr, cols = dims[a][0] // 2, dims[a][1]
            pairs[a].wait_recv()

            def pair_sum(va, vb, vo, a=a):
                for kk in range(NCHIP):
                    pltpu.sync_copy(ins[a].at[kk, _half(c, dims[a][0])], va)
                    pltpu.sync_copy(r1[a].at[kk], vb)
                    vo[...] = (va[...].astype(F32) + vb[...].astype(F32)).astype(BF16)
                    pltpu.sync_copy(vo, part[a].at[kk])

            pl.run_scoped(pair_sum, *[pltpu.VMEM((hr, cols), BF16)] * 3)
            for j, (cx, cy) in enumerate(chips):
                sends.append(copy(per * a + 1 + j, part[a].at[2 * cx + cy], r2[a].at[k], (cx, cy, c)))
                sends[-1].start()
            own.append(pltpu.make_async_copy(part[a].at[k], r2[a].at[k], local_sems.at[a]))
            own[-1].start()
        for a in range(n_arr):
            hr, cols = dims[a][0] // 2, dims[a][1]
            mine = fin[a].at[_half(c, dims[a][0])]
            own[a].wait()
            for j, (cx, cy) in enumerate(chips):
                landed = r2[a].at[2 * cx + cy]
                copy(per * a + 1 + j, landed, landed, (cx, cy, c)).wait_recv()

            def chip_sum(vs, vf, a=a, mine=mine):
                pltpu.sync_copy(r2[a], vs)
                acc = vs[0].astype(F32)
                for kk in range(1, NCHIP):
                    acc = acc + vs[kk].astype(F32)
                vf[...] = acc
                pltpu.sync_copy(vf, mine)

            pl.run_scoped(chip_sum, pltpu.VMEM((NCHIP, hr, cols), BF16), pltpu.VMEM((hr, cols), F32))
            sends.append(copy(per * a + NCHIP, mine, mine, sibling))
            sends[-1].start()
        for a in range(n_arr):
            other = fin[a].at[_half(1 - c, dims[a][0])]
            copy(per * a + NCHIP, other, other, sibling).wait_recv()
        for cp in pairs + sends:
            cp.wait_send()

    any_spec = pl.BlockSpec(memory_space=pl.ANY)
    stage = [jax.ShapeDtypeStruct((NCHIP, r // 2, cols), BF16) for r, cols in dims]
    return pl.pallas_call(
        body, name="reduce_grads", in_specs=[any_spec] * n_arr, out_specs=[any_spec] * (4 * n_arr),
        out_shape=[jax.ShapeDtypeStruct(d, F32) for d in dims] + stage * 3,
        scratch_shapes=[pltpu.SemaphoreType.DMA((per * n_arr,)), pltpu.SemaphoreType.DMA((per * n_arr,)),
                        pltpu.SemaphoreType.DMA((n_arr,))],
    )(*stacks)[:n_arr]


LATE = ("w_gate", "w_up", "w_down", "w_pg", "w_pp")
HBM_SPEC = pl.BlockSpec(memory_space=pltpu.HBM)
SEM_SPEC = pl.BlockSpec(memory_space=pltpu.SEMAPHORE)
IN_FLIGHT = pltpu.SideEffectType.DATAFLOW_SIDE_EFFECTING


def _in_hbm(a):
    return pltpu.with_memory_space_constraint(a, pltpu.HBM)


def _hbm_like(arrs, lead=()):
    return [pltpu.HBM(lead + a.shape, a.dtype) for a in arrs]


def _split_start(name, srcs, lands, after, n_sem, start):
    n = len(srcs)

    def body(*refs):
        src_refs, land_refs = refs[:n], refs[n:2 * n]
        send_sems, recv_sems = refs[2 * n + 1], refs[2 * n + 2]
        token = refs[-1]

        def copy(send_idx, recv_idx, src, dst, to):
            return pltpu.make_async_remote_copy(src_ref=src, dst_ref=dst, send_sem=send_sems.at[send_idx],
                                                recv_sem=recv_sems.at[recv_idx], device_id=to, device_id_type=MESH)

        for cp in start(src_refs, land_refs, copy):
            cp.start()
        token[...] = jnp.zeros_like(token)

    sem = pltpu.SemaphoreType.DMA((n_sem,))
    outs = pl.pallas_call(
        body, name=name, in_specs=[HBM_SPEC] * (2 * n) + [pl.BlockSpec(memory_space=pl.ANY)],
        out_specs=[SEM_SPEC, SEM_SPEC] + [HBM_SPEC] * (2 * n) + [pl.BlockSpec(memory_space=pltpu.VMEM)],
        out_shape=[sem, sem] + _hbm_like(srcs) + _hbm_like(lands) + [jax.ShapeDtypeStruct((8, LANE), F32)],
        input_output_aliases={i: 2 + i for i in range(2 * n)},
        compiler_params=pltpu.CompilerParams(has_side_effects=IN_FLIGHT),
    )(*[_in_hbm(a) for a in srcs], *[_in_hbm(a) for a in lands], after)
    return (outs[0], outs[1], outs[2:2 + n], outs[2 + n:2 + 2 * n]), outs[-1][0, 0]


def _split_wait(name, send_sems, recv_sems, srcs, lands, after, waits):
    n = len(srcs)

    def body(*refs):
        src_refs, land_refs = refs[:n], refs[n:2 * n]
        send_ref, recv_ref = refs[2 * n], refs[2 * n + 1]

        def copy(send_idx, recv_idx, src, dst, to):
            return pltpu.make_async_remote_copy(src_ref=src, dst_ref=dst, send_sem=send_ref.at[send_idx],
                                                recv_sem=recv_ref.at[recv_idx], device_id=to, device_id_type=MESH)

        for cp in waits(src_refs, land_refs, copy):
            cp.wait_send()
            cp.wait_recv()

    outs = pl.pallas_call(
        body, name=name, in_specs=[HBM_SPEC] * (2 * n) + [SEM_SPEC, SEM_SPEC, pl.BlockSpec(memory_space=pl.ANY)],
        out_specs=[HBM_SPEC] * (2 * n), out_shape=_hbm_like(srcs) + _hbm_like(lands),
        input_output_aliases={i: i for i in range(2 * n)},
        compiler_params=pltpu.CompilerParams(has_side_effects=IN_FLIGHT),
    )(*srcs, *lands, send_sems, recv_sems, after)
    return outs[:n], outs[n:]


GATHER_LATE_SEMS = 2 * (NCHIP - 1)


def _gather_late_start(shards, after):
    def start(srcs, lands, copy):
        k, c, chips, _ = _peers()
        out = []
        for a, (src, dst) in enumerate(zip(srcs, lands)):
            rows = src.shape[0]
            for j, (cx, cy) in enumerate(chips):
                for core in range(2):
                    out.append(copy(GATHER_LATE_SEMS * a + 2 * j + core, GATHER_LATE_SEMS * a + 2 * j + c,
                                    src.at[_half(c, rows)], dst.at[k, _half(c, rows)], (cx, cy, core)))
        return out

    chip = 2 * lax.axis_index("x") + lax.axis_index("y")
    lands = [lax.dynamic_update_slice(lax.empty((NCHIP,) + s.shape, s.dtype), s[None], (chip, 0, 0)) for s in shards]
    return _split_start("gather_late_start", shards, lands, after, GATHER_LATE_SEMS * len(shards), start)


def _gather_late_wait(send_sems, recv_sems, shards, lands, after):
    def waits(srcs, lands_, copy):
        _, c, chips, _ = _peers()
        out = []
        for a, (src, dst) in enumerate(zip(srcs, lands_)):
            rows = src.shape[0]
            for j, (cx, cy) in enumerate(chips):
                for core in range(2):
                    idx = GATHER_LATE_SEMS * a + 2 * j + core
                    out.append(copy(idx, idx, src.at[_half(c, rows)], dst.at[2 * cx + cy, _half(core, rows)], (cx, cy, core)))
        return out

    return _split_wait("gather_late_wait", send_sems, recv_sems, shards, lands, after, waits)[1]


def _other_devices():
    x, y, c = _coords()
    out = []
    for d in range(1, NDEV):
        tx, ty, tc = x ^ (d >> 2), y ^ ((d >> 1) & 1), c ^ (d & 1)
        out.append((d, (tx, ty, tc), 2 * tx + ty, 4 * tx + 2 * ty + tc))
    return out


def _reduce_late_start(stacks, after):
    def start(srcs, lands, copy):
        x, y, c = _coords()
        me = 4 * x + 2 * y + c
        return [copy((NDEV - 1) * a + d - 1, (NDEV - 1) * a + d - 1, src.at[chip, _half(to[2], src.shape[1])], dst.at[me], to)
                for a, (src, dst) in enumerate(zip(srcs, lands)) for d, to, chip, _ in _other_devices()]

    x, y, c = _coords()
    lands = []
    for s in stacks:
        hr = s.shape[1] // 2
        own = lax.dynamic_slice(s, (2 * x + y, c * hr, 0), (1, hr, s.shape[2]))
        lands.append(lax.dynamic_update_slice(lax.empty((NDEV, hr, s.shape[2]), s.dtype), own, (4 * x + 2 * y + c, 0, 0)))
    return _split_start("reduce_late_start", stacks, lands, after, (NDEV - 1) * len(stacks), start)


def _reduce_late_wait(send_sems, recv_sems, stacks, lands, after):
    def waits(srcs, lands_, copy):
        return [copy((NDEV - 1) * a + d - 1, (NDEV - 1) * a + d - 1, src.at[chip, _half(to[2], src.shape[1])], dst.at[pos], to)
                for a, (src, dst) in enumerate(zip(srcs, lands_)) for d, to, chip, pos in _other_devices()]

    return _split_wait("reduce_late_wait", send_sems, recv_sems, stacks, lands, after, waits)[1]


def _reduce_finish(arrived):
    n_arr = len(arrived)
    dims = [(2 * p.shape[1], p.shape[2]) for p in arrived]

    def body(*refs):
        lands, fin = refs[:n_arr], refs[n_arr:2 * n_arr]
        send_sems, recv_sems = refs[2 * n_arr:]
        _, c, _, sibling = _peers()
        sends = []
        for a in range(n_arr):
            mine = fin[a].at[_half(c, dims[a][0])]

            def device_sum(vs, vf, a=a, mine=mine):
                pltpu.sync_copy(lands[a], vs)
                acc = vs[0].astype(F32)
                for i in range(1, NDEV):
                    acc = acc + vs[i].astype(F32)
                vf[...] = acc
                pltpu.sync_copy(vf, mine)

            pl.run_scoped(device_sum, pltpu.VMEM((NDEV, dims[a][0] // 2, dims[a][1]), BF16), pltpu.VMEM((dims[a][0] // 2, dims[a][1]), F32))
            sends.append(pltpu.make_async_remote_copy(src_ref=mine, dst_ref=mine, send_sem=send_sems.at[a], recv_sem=recv_sems.at[a],
                                                      device_id=sibling, device_id_type=MESH))
            sends[-1].start()
        for a in range(n_arr):
            other = fin[a].at[_half(1 - c, dims[a][0])]
            pltpu.make_async_remote_copy(src_ref=other, dst_ref=other, send_sem=send_sems.at[a], recv_sem=recv_sems.at[a],
                                         device_id=sibling, device_id_type=MESH).wait_recv()
        for cp in sends:
            cp.wait_send()

    any_spec = pl.BlockSpec(memory_space=pl.ANY)
    return pl.pallas_call(
        body, name="reduce_finish", in_specs=[any_spec] * n_arr, out_specs=[any_spec] * n_arr,
        out_shape=[jax.ShapeDtypeStruct(d, F32) for d in dims],
        scratch_shapes=[pltpu.SemaphoreType.DMA((n_arr,)), pltpu.SemaphoreType.DMA((n_arr,))],
    )(*arrived)


SMALL = (("conv_w", SSD_K * SSD_XBC), ("conv_b", SSD_XBC), ("dt_bias", H), ("A_log", H), ("D", H), ("ssd_norm", SSD_INNER),
         ("q_norm", Q_RANK), ("kv_norm", KV_RANK), ("out_norm", SSD_INNER), ("ln_mix_g", D), ("ln_mix_b", D),
         ("ln_ffn_g", D), ("ln_ffn_b", D))
SMALL_ROWS = 120
NDEV = 8


def _allreduce_small(sv):
    def body(sv_ref, out_ref, slots, send_sems, recv_sems):
        x, y, c = _coords()
        me = 4 * x + 2 * y + c
        slots[me] = sv_ref[...]
        copies = []
        for d in range(1, NDEV):
            to = (x ^ (d >> 2), y ^ ((d >> 1) & 1), c ^ (d & 1))
            copies.append(pltpu.make_async_remote_copy(src_ref=sv_ref, dst_ref=slots.at[me], send_sem=send_sems.at[d - 1],
                                                       recv_sem=recv_sems.at[d - 1], device_id=to, device_id_type=MESH))
            copies[-1].start()
        for cp in copies:
            cp.wait_recv()
        for cp in copies:
            cp.wait_send()
        acc = slots[0]
        for i in range(1, NDEV):
            acc = acc + slots[i]
        out_ref[...] = acc

    vm = pl.BlockSpec(memory_space=pltpu.VMEM)
    return pl.pallas_call(
        body, name="allreduce_small", in_specs=[vm], out_specs=vm, out_shape=jax.ShapeDtypeStruct((SMALL_ROWS, LANE), F32),
        scratch_shapes=[pltpu.VMEM((NDEV, SMALL_ROWS, LANE), F32), pltpu.SemaphoreType.DMA((NDEV - 1,)),
                        pltpu.SemaphoreType.DMA((NDEV - 1,))],
    )(sv)


def _adamw_math(w, g, m, v):
    m2 = ADAM_B1 * m + (1.0 - ADAM_B1) * g
    v2 = ADAM_B2 * v + (1.0 - ADAM_B2) * (g * g)
    m_hat = m2 / (1.0 - ADAM_B1 ** ADAM_STEP)
    v_hat = v2 / (1.0 - ADAM_B2 ** ADAM_STEP)
    return -ADAM_LR * (m_hat / (jnp.sqrt(v_hat) + ADAM_EPS) + ADAM_WD * w), m2, v2


def _adamw_big(w, g, m, v, *, name):
    r, c = w.shape
    tr = next(t for t in (512, 384, 352, 256, 128, 64, 8) if r % t == 0)

    def body(w_ref, g_ref, m_ref, v_ref, d_ref, m2_ref, v2_ref):
        d_ref[...], m2_ref[...], v2_ref[...] = _adamw_math(w_ref[...], g_ref[...], m_ref[...], v_ref[...])

    spec = pl.BlockSpec((tr, c), lambda i: (i, 0))
    return pl.pallas_call(body, name=name, grid=(r // tr,), in_specs=[spec] * 4, out_specs=[spec] * 3,
                          out_shape=[jax.ShapeDtypeStruct((r, c), F32)] * 3)(w, g, m, v)


def _adamw_small(ws, gs, ms, vs):
    n = len(ws)

    def body(*refs):
        for i in range(n):
            w_ref, g_ref, m_ref, v_ref = (refs[j * n + i] for j in range(4))
            d_ref, m2_ref, v2_ref = (refs[(4 + j) * n + i] for j in range(3))
            d_ref[...], m2_ref[...], v2_ref[...] = _adamw_math(w_ref[...], g_ref[...], m_ref[...], v_ref[...])

    vm = pl.BlockSpec(memory_space=pltpu.VMEM)
    shapes = [jax.ShapeDtypeStruct(w.shape, F32) for w in ws]
    outs = pl.pallas_call(body, name="adamw_small", in_specs=[vm] * (4 * n), out_specs=[vm] * (3 * n), out_shape=shapes * 3)(
        *ws, *gs, *ms, *vs)
    return outs[:n], outs[n:2 * n], outs[2 * n:]


_SMALL_ARG = {"conv_w": "ssd_conv_w", "conv_b": "ssd_conv_b", "dt_bias": "ssd_dt_bias", "A_log": "ssd_A_log", "D": "ssd_D",
              "ssd_norm": "ssd_norm_w", "q_norm": "mla_q_norm_w", "kv_norm": "mla_kv_norm_w", "out_norm": "mla_out_norm_w",
              "ln_mix_g": "ln_mix_g", "ln_mix_b": "ln_mix_b", "ln_ffn_g": "ln_ffn_g", "ln_ffn_b": "ln_ffn_b"}
_BIG_ARG = {"w_in": "w_in", "w_qb": "mla_w_q_b", "w_kvb": "mla_w_kv_b", "w_out": "w_out", "w_gate": "w_ffn_gate",
            "w_up": "w_ffn_up", "w_down": "w_ffn_down", "w_pg": "w_ple_gate", "w_pp": "w_ple_proj"}
_WEIGHT_ORDER = ("w_in", "ssd_conv_w", "ssd_conv_b", "ssd_dt_bias", "ssd_A_log", "ssd_D", "ssd_norm_w", "mla_q_norm_w", "mla_w_q_b",
                 "mla_kv_norm_w", "mla_w_kv_b", "mla_out_norm_w", "w_out", "ln_mix_g", "ln_mix_b", "w_ffn_gate", "w_ffn_up",
                 "w_ffn_down", "w_ple_gate", "w_ple_proj", "ln_ffn_g", "ln_ffn_b")


def _rows128(a):
    flat = a.reshape(-1)
    return jnp.pad(flat, (0, -flat.shape[0] % LANE)).reshape(-1, LANE)


def kernel(x, p, positions, w_in, ssd_conv_w, ssd_conv_b, ssd_dt_bias, ssd_A_log, ssd_D, ssd_norm_w, mla_q_norm_w, mla_w_q_b, mla_kv_norm_w, mla_w_kv_b, mla_out_norm_w, w_out, ln_mix_g, ln_mix_b, w_ffn_gate, w_ffn_up, w_ffn_down, w_ple_gate, w_ple_proj, ln_ffn_g, ln_ffn_b, loss_target, m_w_in, m_ssd_conv_w, m_ssd_conv_b, m_ssd_dt_bias, m_ssd_A_log, m_ssd_D, m_ssd_norm_w, m_mla_q_norm_w, m_mla_w_q_b, m_mla_kv_norm_w, m_mla_w_kv_b, m_mla_out_norm_w, m_w_out, m_ln_mix_g, m_ln_mix_b, m_w_ffn_gate, m_w_ffn_up, m_w_ffn_down, m_w_ple_gate, m_w_ple_proj, m_ln_ffn_g, m_ln_ffn_b, v_w_in, v_ssd_conv_w, v_ssd_conv_b, v_ssd_dt_bias, v_ssd_A_log, v_ssd_D, v_ssd_norm_w, v_mla_q_norm_w, v_mla_w_q_b, v_mla_kv_norm_w, v_mla_w_kv_b, v_mla_out_norm_w, v_w_out, v_ln_mix_g, v_ln_mix_b, v_w_ffn_gate, v_w_ffn_up, v_w_ffn_down, v_w_ple_gate, v_w_ple_proj, v_ln_ffn_g, v_ln_ffn_b):
    given = dict(locals())
    chip = 2 * lax.axis_index("x") + lax.axis_index("y")

    early = [name for name, _, _ in BIG if name not in LATE]
    shards = [given[_BIG_ARG[name]][0].astype(BF16) for name in early]
    conv_bits = lax.bitcast_convert_type(ssd_conv_w[0], BF16).reshape(SSD_K, 2 * CONV_SHARD)
    shards.append(jnp.pad(conv_bits, ((0, BF16_ROWS - SSD_K), (0, 0))))
    gathered = _gather_weights(shards)
    gw = dict(zip(early, gathered))
    conv_all = lax.bitcast_convert_type(gathered[-1][:, :SSD_K].reshape(NCHIP, SSD_K, CONV_SHARD, 2), F32)
    sp = {k: given[a] for k, a in _SMALL_ARG.items() if k != "conv_w"}
    sp["conv_w"] = _from_cols(conv_all)
    late_shards = [given[_BIG_ARG[name]][0].astype(BF16) for name in LATE]
    in_flight, started = _gather_late_start(late_shards, gathered[0])

    def late_weights(after):
        return dict(zip(LATE, _gather_late_wait(*in_flight, after)))

    reducing = []

    def send_late(grads):
        in_flight_grads, sent = _reduce_late_start([grads[name] for name in LATE], grads[LATE[0]])
        reducing.append(in_flight_grads)
        return sent

    loss_row, grad_x, g = _local_step(x[0] + started, p[0, 0], positions[0], loss_target[0], gw, late_weights, send_late, sp)

    gbig = dict(zip(early, _reduce_grads([g[name] for name in early])))
    gbig.update(zip(LATE, _reduce_finish(_reduce_late_wait(*reducing[0], grad_x))))
    small_in = jnp.concatenate([_rows128(g[name]) for name, _ in SMALL] + [loss_row], axis=0)
    small_sum = _allreduce_small(jnp.pad(small_in, ((0, SMALL_ROWS - small_in.shape[0]), (0, 0))))
    gsmall, row = {}, 0
    for name, size in SMALL:
        nrow = -(-size // LANE)
        gsmall[name] = small_sum[row:row + nrow].reshape(-1)[:size]
        row += nrow
    loss = small_sum[row, 0]

    grads = {}
    for name, shape, axis in BIG:
        grads[_BIG_ARG[name]] = gbig[name][None]
    for name, _ in SMALL:
        if name == "conv_w":
            full_g = gsmall[name].reshape(SSD_K, SSD_XBC)
            grads["ssd_conv_w"] = lax.dynamic_slice(full_g, (0, chip * CONV_SHARD), (SSD_K, CONV_SHARD))[None]
        else:
            grads[_SMALL_ARG[name]] = gsmall[name].reshape(given[_SMALL_ARG[name]].shape)

    delta, new_m, new_v = {}, {}, {}
    for name, _, _ in BIG:
        a = _BIG_ARG[name]
        d, m2, v2 = _adamw_big(given[a][0], grads[a][0], given["m_" + a][0], given["v_" + a][0], name="adamw_" + a)
        delta[a], new_m[a], new_v[a] = d[None], m2[None], v2[None]
    small_names = [_SMALL_ARG[name] for name, _ in SMALL]
    two_d = lambda t: t.reshape(t.shape[-2], t.shape[-1])
    ds, ms, vs = _adamw_small([two_d(given[a]) for a in small_names], [two_d(grads[a]) for a in small_names],
                              [two_d(given["m_" + a]) for a in small_names], [two_d(given["v_" + a]) for a in small_names])
    for a, d, m2, v2 in zip(small_names, ds, ms, vs):
        delta[a], new_m[a], new_v[a] = (t.reshape(given[a].shape) for t in (d, m2, v2))

    return (loss, grad_x[None], *[grads[n] for n in _WEIGHT_ORDER], *[delta[n] for n in _WEIGHT_ORDER],
            *[new_m[n] for n in _WEIGHT_ORDER], *[new_v[n] for n in _WEIGHT_ORDER])
```

```python
import functools
import math

import jax
import jax.numpy as jnp
from jax import lax
from jax.experimental import pallas as pl
from jax.experimental.pallas import tpu as pltpu

F32 = jnp.float32
BF16 = jnp.bfloat16

S = 2048
D = 1024
PLE = 256
H = 16
SSD_P = 64
SSD_INNER = 1024
SSD_N = 128
SSD_G = 2
SSD_L = 128
SSD_NC = S // SSD_L
SSD_XBC = 1536
SSD_K = 4
Q_RANK = 384
KV_RANK = 256
NOPE = 64
ROPE = 32
VDIM = 64
D_FF = 2816
IN_WIDTH = 3248
ALPHA = 2.0 ** 0.25
EPS_RMS = 1e-6
EPS_LN = 1e-5
ATT_SCALE = 1.0 / math.sqrt(NOPE + ROPE)
LN2 = math.log(2.0)
ATT_SCALE_LOG2 = ATT_SCALE / LN2
LANE = 128
NCHIP = 4
SMALL_W = 896
SM_Q, SM_KV, SM_KR, SM_DT = 0, 384, 640, 768
NEG = -1e30

ADAM_LR = 0.001
ADAM_B1 = 0.9
ADAM_B2 = 0.999
ADAM_EPS = 1e-08
ADAM_WD = 0.01
ADAM_STEP = 10


def _sigmoid(v):
    return 1.0 / (1.0 + jnp.exp(-v))


MM_VMEM_BUDGET = 36 * 2 ** 20
MM_MAX_ACC = 2048 * 1024


def _mm_tiles(pairs, ta, tb, m, n, out_dtype, has_add):
    def divs(v):
        return [LANE * d for d in range(v // LANE, 0, -1) if (v // LANE) % d == 0] if v % LANE == 0 else [v]

    def cost(tm, tn):
        tot = tm * tn * (jnp.dtype(out_dtype).itemsize + (4 if has_add else 0))
        for a, b in pairs:
            k = a.shape[-2] if ta else a.shape[-1]
            tot += k * (tm * a.dtype.itemsize + tn * b.dtype.itemsize)
        return 2 * tot

    ok = [(tm * tn, tm, tn) for tm in divs(m) for tn in divs(n) if tm * tn <= MM_MAX_ACC and cost(tm, tn) <= MM_VMEM_BUDGET]
    _, tm, tn = max(ok)
    return tm, tn


def _mm(pairs, *, ta=False, tb=False, out_dtype=F32, add=None, add_scale=1.0, chunk=None, name):
    n_pairs = len(pairs)
    a0, b0 = pairs[0]
    m = a0.shape[-1] if ta else a0.shape[-2]
    n = b0.shape[-2] if tb else b0.shape[-1]
    tm, tn = _mm_tiles(pairs, ta, tb, m, n, out_dtype, add is not None)
    dims = (((0 if ta else 1,), (1 if tb else 0,)), ((), ()))
    nk = NCHIP if chunk else 1
    assert chunk != "sum" or out_dtype == F32

    def body(*refs):
        o_ref = refs[-1]
        acc = None
        for i in range(n_pairs):
            a = refs[2 * i][...].astype(BF16)
            b = refs[2 * i + 1][...].astype(BF16)
            part = lax.dot_general(a, b, dims, preferred_element_type=F32)
            acc = part if acc is None else acc + part
        if chunk == "sum":
            k = pl.program_id(2)

            @pl.when(k == 0)
            def _():
                o_ref[...] = acc + add_scale * refs[2 * n_pairs][...] if add is not None else acc

            @pl.when(k > 0)
            def _():
                o_ref[...] += acc
        else:
            if add is not None:
                acc = acc + add_scale * refs[2 * n_pairs][...]
            o_ref[...] = acc.astype(out_dtype)

    def spec(arr, shape, idx2):
        if arr.ndim == 3:
            return pl.BlockSpec((None,) + shape, lambda i, j, k: (k,) + idx2(i, j))
        return pl.BlockSpec(shape, lambda i, j, k: idx2(i, j))

    in_specs, args = [], []
    for a, b in pairs:
        kdim = a.shape[-2] if ta else a.shape[-1]
        in_specs.append(spec(a, (kdim, tm), lambda i, j: (0, i)) if ta else spec(a, (tm, kdim), lambda i, j: (i, 0)))
        in_specs.append(spec(b, (tn, kdim), lambda i, j: (j, 0)) if tb else spec(b, (kdim, tn), lambda i, j: (0, j)))
        args += [a, b]
    if add is not None:
        in_specs.append(pl.BlockSpec((tm, tn), lambda i, j, k: (i, j)))
        args.append(add)
    if chunk == "out":
        out_spec = pl.BlockSpec((None, tm, tn), lambda i, j, k: (k, i, j))
        out_shape = jax.ShapeDtypeStruct((nk, m, n), out_dtype)
    else:
        out_spec = pl.BlockSpec((tm, tn), lambda i, j, k: (i, j))
        out_shape = jax.ShapeDtypeStruct((m, n), out_dtype)
    return pl.pallas_call(
        body, name=name, grid=(m // tm, n // tn, nk), in_specs=in_specs, out_specs=out_spec, out_shape=out_shape,
        compiler_params=pltpu.CompilerParams(dimension_semantics=("parallel", "parallel", "arbitrary")),
    )(*args)


TR = 256


def _row_spec(c):
    return pl.BlockSpec((TR, c), lambda i: (i, 0))


def _vec_spec(c):
    return pl.BlockSpec((1, c), lambda i: (0, 0))


def _acc_rows(ref, val):
    @pl.when(pl.program_id(0) == 0)
    def _():
        ref[...] = jnp.zeros_like(ref)
    ref[...] += val


def _rms_fwd(u, w, *, name):
    c = u.shape[1]

    def body(u_ref, w_ref, o_ref):
        v = u_ref[...]
        r = lax.rsqrt(jnp.mean(v * v, axis=-1, keepdims=True) + EPS_RMS)
        o_ref[...] = (v * r * w_ref[...]).astype(BF16)

    return pl.pallas_call(body, name=name, grid=(S // TR,), in_specs=[_row_spec(c), _vec_spec(c)], out_specs=_row_spec(c),
                          out_shape=jax.ShapeDtypeStruct((S, c), BF16))(u, w)


def _rms_bwd(u, w, dy, *, name):
    c = u.shape[1]

    def body(u_ref, w_ref, dy_ref, du_ref, dw_ref):
        v = u_ref[...]
        g = dy_ref[...].astype(F32)
        r = lax.rsqrt(jnp.mean(v * v, axis=-1, keepdims=True) + EPS_RMS)
        gw = g * w_ref[...]
        du_ref[...] = r * gw - v * (r * r * r * jnp.mean(gw * v, axis=-1, keepdims=True))
        _acc_rows(dw_ref, jnp.sum(g * v * r, axis=0, keepdims=True))

    return pl.pallas_call(body, name=name, grid=(S // TR,), in_specs=[_row_spec(c), _vec_spec(c), _row_spec(c)],
                          out_specs=[_row_spec(c), _vec_spec(c)],
                          out_shape=[jax.ShapeDtypeStruct((S, c), F32), jax.ShapeDtypeStruct((1, c), F32)])(u, w, dy)


def _gate_norm_fwd(y, z, w):
    def body(y_ref, z_ref, w_ref, o_ref):
        zz = z_ref[...]
        v = y_ref[...] * (zz * _sigmoid(zz))
        r = lax.rsqrt(jnp.mean(v * v, axis=-1, keepdims=True) + EPS_RMS)
        o_ref[...] = (v * r * w_ref[...]).astype(BF16)

    c = SSD_INNER
    return pl.pallas_call(body, name="ssd_gate_norm_fwd", grid=(S // TR,), in_specs=[_row_spec(c), _row_spec(c), _vec_spec(c)],
                          out_specs=_row_spec(c), out_shape=jax.ShapeDtypeStruct((S, c), BF16))(y, z, w)


def _gate_norm_bwd(y, z, w, dout):
    def body(y_ref, z_ref, w_ref, g_ref, dy_ref, dz_ref, dw_ref):
        yy = y_ref[...]
        zz = z_ref[...]
        sg = _sigmoid(zz)
        sz = zz * sg
        v = yy * sz
        g = g_ref[...]
        r = lax.rsqrt(jnp.mean(v * v, axis=-1, keepdims=True) + EPS_RMS)
        gw = g * w_ref[...]
        dv = r * gw - v * (r * r * r * jnp.mean(gw * v, axis=-1, keepdims=True))
        dy_ref[...] = dv * sz
        dz_ref[...] = (dv * yy * (sg * (1.0 + zz * (1.0 - sg)))).astype(BF16)
        _acc_rows(dw_ref, jnp.sum(g * v * r, axis=0, keepdims=True))

    c = SSD_INNER
    return pl.pallas_call(body, name="ssd_gate_norm_bwd", grid=(S // TR,),
                          in_specs=[_row_spec(c), _row_spec(c), _vec_spec(c), _row_spec(c)],
                          out_specs=[_row_spec(c), _row_spec(c), _vec_spec(c)],
                          out_shape=[jax.ShapeDtypeStruct((S, c), F32), jax.ShapeDtypeStruct((S, c), BF16),
                                     jax.ShapeDtypeStruct((1, c), F32)])(y, z, w, dout)


def _ln_fwd(xr, mix, g, b):
    def body(x_ref, m_ref, g_ref, b_ref, o_ref, ob_ref):
        pre = ALPHA * x_ref[...] + m_ref[...]
        mu = jnp.mean(pre, axis=-1, keepdims=True)
        d = pre - mu
        rs = lax.rsqrt(jnp.mean(d * d, axis=-1, keepdims=True) + EPS_LN)
        h = d * rs * g_ref[...] + b_ref[...]
        o_ref[...] = h
        ob_ref[...] = h.astype(BF16)

    return pl.pallas_call(body, name="ln_mix_fwd", grid=(S // TR,), in_specs=[_row_spec(D), _row_spec(D), _vec_spec(D), _vec_spec(D)],
                          out_specs=[_row_spec(D)] * 2,
                          out_shape=[jax.ShapeDtypeStruct((S, D), F32), jax.ShapeDtypeStruct((S, D), BF16)])(xr, mix, g, b)


def _ln_bwd(xr, mix, g, dh):
    def body(x_ref, m_ref, g_ref, dh_ref, dpre_ref, dpreb_ref, dg_ref, db_ref):
        pre = ALPHA * x_ref[...] + m_ref[...]
        mu = jnp.mean(pre, axis=-1, keepdims=True)
        d = pre - mu
        rs = lax.rsqrt(jnp.mean(d * d, axis=-1, keepdims=True) + EPS_LN)
        xh = d * rs
        dy = dh_ref[...]
        gy = dy * g_ref[...]
        dpre = rs * (gy - jnp.mean(gy, axis=-1, keepdims=True) - xh * jnp.mean(gy * xh, axis=-1, keepdims=True))
        dpre_ref[...] = dpre
        dpreb_ref[...] = dpre.astype(BF16)
        _acc_rows(dg_ref, jnp.sum(dy * xh, axis=0, keepdims=True))
        _acc_rows(db_ref, jnp.sum(dy, axis=0, keepdims=True))

    return pl.pallas_call(body, name="ln_mix_bwd", grid=(S // TR,),
                          in_specs=[_row_spec(D), _row_spec(D), _vec_spec(D), _row_spec(D)],
                          out_specs=[_row_spec(D), _row_spec(D), _vec_spec(D), _vec_spec(D)],
                          out_shape=[jax.ShapeDtypeStruct((S, D), F32), jax.ShapeDtypeStruct((S, D), BF16),
                                     jax.ShapeDtypeStruct((1, D), F32), jax.ShapeDtypeStruct((1, D), F32)])(xr, mix, g, dh)


FF_CHUNK = D_FF // NCHIP


def _ff_spec():
    return pl.BlockSpec((None, TR * 2, FF_CHUNK), lambda k, i: (k, i, 0))


def _swiglu_fwd(gate, up):
    def body(g_ref, u_ref, o_ref):
        g = g_ref[...]
        o_ref[...] = (g * _sigmoid(g) * u_ref[...]).astype(BF16)

    return pl.pallas_call(body, name="swiglu_fwd", grid=(NCHIP, S // (2 * TR)), in_specs=[_ff_spec()] * 2, out_specs=_ff_spec(),
                          out_shape=jax.ShapeDtypeStruct((NCHIP, S, FF_CHUNK), BF16))(gate, up)


def _swiglu_bwd(gate, up, dact):
    def body(g_ref, u_ref, d_ref, dg_ref, du_ref):
        g = g_ref[...]
        sg = _sigmoid(g)
        d = d_ref[...]
        dg_ref[...] = (d * u_ref[...] * (sg * (1.0 + g * (1.0 - sg)))).astype(BF16)
        du_ref[...] = (d * g * sg).astype(BF16)

    return pl.pallas_call(body, name="swiglu_bwd", grid=(NCHIP, S // (2 * TR)), in_specs=[_ff_spec()] * 3, out_specs=[_ff_spec()] * 2,
                          out_shape=[jax.ShapeDtypeStruct((NCHIP, S, FF_CHUNK), BF16)] * 2)(gate, up, dact)


def _final_fwd_bwd(h1, ffn, pg, pp, target, g2, b2):
    def body(h_ref, f_ref, pg_ref, pp_ref, t_ref, g_ref, b_ref, dpre_ref, dpreb_ref, dpg_ref, dpp_ref, dg_ref, db_ref, loss_ref):
        sg = _sigmoid(pg_ref[...])
        ppv = pp_ref[...]
        pre = ALPHA * h_ref[...] + f_ref[...] + sg * ppv
        mu = jnp.mean(pre, axis=-1, keepdims=True)
        d = pre - mu
        rs = lax.rsqrt(jnp.mean(d * d, axis=-1, keepdims=True) + EPS_LN)
        xh = d * rs
        err = xh * g_ref[...] + b_ref[...] - t_ref[...]
        dy = err * (1.0 / D)
        gy = dy * g_ref[...]
        dpre = rs * (gy - jnp.mean(gy, axis=-1, keepdims=True) - xh * jnp.mean(gy * xh, axis=-1, keepdims=True))
        dpre_ref[...] = dpre
        dpreb_ref[...] = dpre.astype(BF16)
        dpg_ref[...] = (dpre * ppv * sg * (1.0 - sg)).astype(BF16)
        dpp_ref[...] = (dpre * sg).astype(BF16)
        _acc_rows(dg_ref, jnp.sum(dy * xh, axis=0, keepdims=True))
        _acc_rows(db_ref, jnp.sum(dy, axis=0, keepdims=True))
        _acc_rows(loss_ref, 0.5 * jnp.sum(jnp.mean(err * err, axis=-1, keepdims=True), axis=0, keepdims=True) * jnp.ones((1, LANE), F32))

    return pl.pallas_call(
        body, name="final_ln_loss", grid=(S // TR,),
        in_specs=[_row_spec(D)] * 5 + [_vec_spec(D)] * 2,
        out_specs=[_row_spec(D)] * 4 + [_vec_spec(D), _vec_spec(D), _vec_spec(LANE)],
        out_shape=[jax.ShapeDtypeStruct((S, D), F32)] + [jax.ShapeDtypeStruct((S, D), BF16)] * 3 + [
                   jax.ShapeDtypeStruct((1, D), F32), jax.ShapeDtypeStruct((1, D), F32), jax.ShapeDtypeStruct((1, LANE), F32)],
    )(h1, ffn, pg, pp, target, g2, b2)


def _rot(u, cos_t, sin_t, lane):
    partner = jnp.where(lane < NOPE + ROPE // 2, pltpu.roll(u, LANE - ROPE // 2, 1), pltpu.roll(u, ROPE // 2, 1))
    return u * cos_t + partner * sin_t


def _q_rope(qlin, cos_t, sin_t):
    def body(q_ref, c_ref, s_ref, o_ref):
        lane = lax.broadcasted_iota(jnp.int32, (TR, LANE), 1)
        c, s = c_ref[...], s_ref[...]
        for h in range(H):
            o_ref[:, h * LANE:(h + 1) * LANE] = _rot(q_ref[:, h * LANE:(h + 1) * LANE], c, s, lane).astype(BF16)

    w = H * LANE
    return pl.pallas_call(body, name="q_rope", grid=(S // TR,), in_specs=[_row_spec(w), _row_spec(LANE), _row_spec(LANE)],
                          out_specs=_row_spec(w), out_shape=jax.ShapeDtypeStruct((S, w), BF16))(qlin, cos_t, sin_t)


def _q_unrope(dq, cos_t, sin_t):
    def body(q_ref, c_ref, s_ref, o_ref):
        lane = lax.broadcasted_iota(jnp.int32, (TR, LANE), 1)
        c, s = c_ref[...], -s_ref[...]
        for h in range(H):
            o_ref[:, h * LANE:(h + 1) * LANE] = _rot(q_ref[:, h * LANE:(h + 1) * LANE], c, s, lane).astype(BF16)

    w = H * LANE
    return pl.pallas_call(body, name="q_unrope", grid=(S // TR,), in_specs=[_row_spec(w), _row_spec(LANE), _row_spec(LANE)],
                          out_specs=_row_spec(w), out_shape=jax.ShapeDtypeStruct((S, w), BF16))(dq, cos_t, sin_t)


def _k_prep(klin, small, cos_t, sin_t):
    def body(k_ref, kr_ref, c_ref, s_ref, o_ref):
        lane = lax.broadcasted_iota(jnp.int32, (TR, LANE), 1)
        kr = _rot(pltpu.roll(kr_ref[...], NOPE, 1), c_ref[...], s_ref[...], lane)
        for h in range(H):
            o_ref[:, h * LANE:(h + 1) * LANE] = (k_ref[:, h * LANE:(h + 1) * LANE] + kr).astype(BF16)

    w = H * LANE
    kr_spec = pl.BlockSpec((TR, LANE), lambda i: (i, SM_KR // LANE))
    return pl.pallas_call(body, name="k_prep", grid=(S // TR,), in_specs=[_row_spec(w), kr_spec, _row_spec(LANE), _row_spec(LANE)],
                          out_specs=_row_spec(w), out_shape=jax.ShapeDtypeStruct((S, w), BF16))(klin, small, cos_t, sin_t)


def _k_rope_bwd(dk, cos_t, sin_t):
    def body(k_ref, c_ref, s_ref, o_ref):
        lane = lax.broadcasted_iota(jnp.int32, (TR, LANE), 1)
        acc = k_ref[:, 0:LANE]
        for h in range(1, H):
            acc = acc + k_ref[:, h * LANE:(h + 1) * LANE]
        acc = jnp.where((lane >= NOPE) & (lane < NOPE + ROPE), acc, 0.0)
        o_ref[...] = pltpu.roll(_rot(acc, c_ref[...], -s_ref[...], lane), LANE - NOPE, 1)

    w = H * LANE
    return pl.pallas_call(body, name="k_rope_bwd", grid=(S // TR,), in_specs=[_row_spec(w), _row_spec(LANE), _row_spec(LANE)],
                          out_specs=_row_spec(LANE), out_shape=jax.ShapeDtypeStruct((S, LANE), F32))(dk, cos_t, sin_t)


CB = 256


def _shift_down(u, k, row):
    if k == 0:
        return u
    return jnp.where(row >= k, pltpu.roll(u, k, 0), 0.0)


def _shift_up(u, k, row):
    if k == 0:
        return u
    return jnp.where(row < S - k, pltpu.roll(u, S - k, 0), 0.0)


def _conv_fwd(u, w, b):
    def body(u_ref, w_ref, b_ref, o_ref):
        row = lax.broadcasted_iota(jnp.int32, (S, CB), 0)
        uu = u_ref[...]
        acc = b_ref[...] + w_ref[SSD_K - 1:SSD_K, :] * uu
        for k in range(SSD_K - 1):
            acc = acc + w_ref[k:k + 1, :] * _shift_down(uu, SSD_K - 1 - k, row)
        o_ref[...] = acc * _sigmoid(acc)

    c = u.shape[1]
    return pl.pallas_call(
        body, name="conv_fwd", grid=(c // CB,),
        in_specs=[pl.BlockSpec((S, CB), lambda j: (0, j)), pl.BlockSpec((SSD_K, CB), lambda j: (0, j)), pl.BlockSpec((1, CB), lambda j: (0, j))],
        out_specs=pl.BlockSpec((S, CB), lambda j: (0, j)), out_shape=jax.ShapeDtypeStruct((S, c), F32),
    )(u, w, b)


def _conv_bwd(u, w, b, dact):
    def body(u_ref, w_ref, b_ref, d_ref, du_ref, dw_ref, db_ref):
        row = lax.broadcasted_iota(jnp.int32, (S, CB), 0)
        uu = u_ref[...]
        sh = [_shift_down(uu, SSD_K - 1 - k, row) for k in range(SSD_K)]
        acc = b_ref[...]
        for k in range(SSD_K):
            acc = acc + w_ref[k:k + 1, :] * sh[k]
        sg = _sigmoid(acc)
        dacc = d_ref[...] * (sg * (1.0 + acc * (1.0 - sg)))
        du = w_ref[SSD_K - 1:SSD_K, :] * dacc
        for k in range(SSD_K - 1):
            du = du + w_ref[k:k + 1, :] * _shift_up(dacc, SSD_K - 1 - k, row)
        du_ref[...] = du.astype(BF16)
        for k in range(SSD_K):
            dw_ref[k:k + 1, :] = jnp.sum(dacc * sh[k], axis=0, keepdims=True)
        db_ref[...] = jnp.sum(dacc, axis=0, keepdims=True)

    c = u.shape[1]
    col = lambda r: pl.BlockSpec((r, CB), lambda j: (0, j))
    return pl.pallas_call(
        body, name="conv_bwd", grid=(c // CB,), in_specs=[col(S), col(SSD_K), col(1), col(S)], out_specs=[col(S), col(SSD_K), col(1)],
        out_shape=[jax.ShapeDtypeStruct((S, c), BF16), jax.ShapeDtypeStruct((SSD_K, c), F32), jax.ShapeDtypeStruct((1, c), F32)],
    )(u, w, b, dact)


NPAIR = H // 2
PAIRS_PER_GROUP = NPAIR // SSD_G


def _softplus(v):
    return jnp.maximum(v, 0.0) + jnp.log(1.0 + jnp.exp(-jnp.abs(v)))


def _dot(a, b, dims):
    return lax.dot_general(a.astype(BF16), b.astype(BF16), (dims, ((), ())), preferred_element_type=F32)


def _dot3(a, b, dims, split_lhs):
    v = a if split_lhs else b
    v1 = v.astype(BF16)
    r1 = v - v1.astype(F32)
    v2 = r1.astype(BF16)
    v3 = (r1 - v2.astype(F32)).astype(BF16)
    acc = None
    for part in (v1, v2, v3):
        lhs, rhs = (part, b) if split_lhs else (a, part)
        t = lax.dot_general(lhs, rhs, (dims, ((), ())), preferred_element_type=F32)
        acc = t if acc is None else acc + t
    return acc


def _ssd_chunk_common(dt_ref, dtT_ref, prow_ref, pcol_ref):
    prow = prow_ref[...]
    pcol = pcol_ref[...]
    ri = lax.broadcasted_iota(jnp.int32, (SSD_L, SSD_L), 0)
    ci = lax.broadcasted_iota(jnp.int32, (SSD_L, SSD_L), 1)
    causal = ri >= ci
    pre_c = dt_ref[...] + prow[0:1, :]
    dtc = _softplus(pre_c)
    a_row = -jnp.exp(prow[1:2, :])
    cs_col = _dot3(causal.astype(BF16), dtc * a_row, ((1,), (0,)), False)
    dtr = _softplus(dtT_ref[...] + pcol[:, 0:1])
    a_col = -jnp.exp(pcol[:, 1:2])
    cs_row = _dot3(dtr * a_col, (ri <= ci).astype(BF16), ((1,), (0,)), True)
    return prow, causal, pre_c, dtc, a_row, cs_col, cs_row


def _ssd_fwd(act, small, dtT, prow, pcol):
    def body(x_ref, b_ref, c_ref, dt_ref, dtT_ref, prow_ref, pcol_ref, y_ref, st_ref, state):
        @pl.when(pl.program_id(0) == 0)
        def _():
            state[...] = jnp.zeros_like(state)

        prow, causal, _, dtc, _, cs_col, cs_row = _ssd_chunk_common(dt_ref, dtT_ref, prow_ref, pcol_ref)
        lo = lax.broadcasted_iota(jnp.int32, (SSD_L, LANE), 1) < SSD_P
        lo1 = lo[0:1, :]
        for g in range(SSD_G):
            bm = b_ref[:, g * SSD_N:(g + 1) * SSD_N]
            cm = c_ref[:, g * SSD_N:(g + 1) * SSD_N]
            cb = _dot(cm, bm, ((1,), (1,)))
            for qq in range(PAIRS_PER_GROUP):
                q = g * PAIRS_PER_GROUP + qq
                ha, hb = 2 * q, 2 * q + 1
                csa, csb = cs_col[:, ha:ha + 1], cs_col[:, hb:hb + 1]
                xp = x_ref[:, q * LANE:(q + 1) * LANE]
                xx = xp * jnp.where(lo, dtc[:, ha:ha + 1], dtc[:, hb:hb + 1])
                ga = cb * jnp.exp(jnp.where(causal, csa - cs_row[ha:ha + 1, :], NEG))
                gb = cb * jnp.exp(jnp.where(causal, csb - cs_row[hb:hb + 1, :], NEG))
                y = _dot(ga, jnp.where(lo, xx, 0.0), ((1,), (0,))) + _dot(gb, jnp.where(lo, 0.0, xx), ((1,), (0,)))
                s_in = state[q]
                y = y + _dot(cm, s_in, ((1,), (0,))) * jnp.where(lo, jnp.exp(csa), jnp.exp(csb))
                y = y + jnp.where(lo1, prow[2:3, ha:ha + 1], prow[2:3, hb:hb + 1]) * xp
                y_ref[:, q * LANE:(q + 1) * LANE] = y
                la, lb = csa[SSD_L - 1:SSD_L, :], csb[SSD_L - 1:SSD_L, :]
                decay = jnp.where(lo, jnp.exp(la - csa), jnp.exp(lb - csb))
                st_ref[q] = s_in
                state[q] = s_in * jnp.where(lo1, jnp.exp(la), jnp.exp(lb)) + _dot(bm, xx * decay, ((0,), (0,)))

    L = SSD_L
    return pl.pallas_call(
        body, name="ssd_fwd", grid=(SSD_NC,),
        in_specs=[pl.BlockSpec((L, SSD_INNER), lambda c: (c, 0)),
                  pl.BlockSpec((L, SSD_G * SSD_N), lambda c: (c, SSD_INNER // (SSD_G * SSD_N))),
                  pl.BlockSpec((L, SSD_G * SSD_N), lambda c: (c, SSD_INNER // (SSD_G * SSD_N) + 1)),
                  pl.BlockSpec((L, LANE), lambda c: (c, SM_DT // LANE)),
                  pl.BlockSpec((LANE, L), lambda c: (0, c)),
                  pl.BlockSpec((8, LANE), lambda c: (0, 0)), pl.BlockSpec((LANE, 8), lambda c: (0, 0))],
        out_specs=[pl.BlockSpec((L, SSD_INNER), lambda c: (c, 0)),
                   pl.BlockSpec((None, NPAIR, SSD_N, LANE), lambda c: (c, 0, 0, 0))],
        out_shape=[jax.ShapeDtypeStruct((S, SSD_INNER), F32), jax.ShapeDtypeStruct((SSD_NC, NPAIR, SSD_N, LANE), F32)],
        scratch_shapes=[pltpu.VMEM((NPAIR, SSD_N, LANE), F32)],
        compiler_params=pltpu.CompilerParams(dimension_semantics=("arbitrary",)),
    )(act, act, act, small, dtT, prow, pcol)


def _ssd_bwd(act, small, dtT, prow, pcol, states, dy):
    def body(x_ref, b_ref, c_ref, dt_ref, dtT_ref, prow_ref, pcol_ref, st_ref, dy_ref,
             dx_ref, ddt_ref, dp_ref, dstate):
        @pl.when(pl.program_id(0) == 0)
        def _():
            dstate[...] = jnp.zeros_like(dstate)
            dp_ref[...] = jnp.zeros_like(dp_ref)

        prow, causal, pre_c, dtc, a_row, cs_col, cs_row = _ssd_chunk_common(dt_ref, dtT_ref, prow_ref, pcol_ref)
        lane = lax.broadcasted_iota(jnp.int32, (SSD_L, LANE), 1)
        sub = lax.broadcasted_iota(jnp.int32, (LANE, SSD_L), 0)
        rowi = lax.broadcasted_iota(jnp.int32, (SSD_L, 1), 0)
        lane1 = lane[0:1, :]
        lo = lane < SSD_P
        lo1 = lo[0:1, :]
        dcs_c = jnp.zeros((SSD_L, LANE), F32)
        dcs_r = jnp.zeros((LANE, SSD_L), F32)
        ddt_x = jnp.zeros((SSD_L, LANE), F32)
        dd_row = jnp.zeros((1, LANE), F32)
        for g in range(SSD_G):
            bm = b_ref[:, g * SSD_N:(g + 1) * SSD_N]
            cm = c_ref[:, g * SSD_N:(g + 1) * SSD_N]
            cb = _dot(cm, bm, ((1,), (1,)))
            dcb = jnp.zeros((SSD_L, SSD_L), F32)
            dbm = jnp.zeros((SSD_L, SSD_N), F32)
            dcm = jnp.zeros((SSD_L, SSD_N), F32)
            for qq in range(PAIRS_PER_GROUP):
                q = g * PAIRS_PER_GROUP + qq
                ha, hb = 2 * q, 2 * q + 1
                csa, csb = cs_col[:, ha:ha + 1], cs_col[:, hb:hb + 1]
                xp = x_ref[:, q * LANE:(q + 1) * LANE]
                dtp = jnp.where(lo, dtc[:, ha:ha + 1], dtc[:, hb:hb + 1])
                xx = xp * dtp
                lma = jnp.exp(jnp.where(causal, csa - cs_row[ha:ha + 1, :], NEG))
                lmb = jnp.exp(jnp.where(causal, csb - cs_row[hb:hb + 1, :], NEG))
                ga, gb = cb * lma, cb * lmb
                dyp = dy_ref[:, q * LANE:(q + 1) * LANE]
                dya, dyb = jnp.where(lo, dyp, 0.0), jnp.where(lo, 0.0, dyp)
                s_in = st_ref[q]
                ds_out = dstate[q]
                la, lb = csa[SSD_L - 1:SSD_L, :], csb[SSD_L - 1:SSD_L, :]
                ecs = jnp.where(lo, jnp.exp(csa), jnp.exp(csb))
                decay = jnp.where(lo, jnp.exp(la - csa), jnp.exp(lb - csb))
                cd = jnp.where(lo1, jnp.exp(la), jnp.exp(lb))
                bds = _dot(bm, ds_out, ((1,), (0,)))
                dxx = _dot(ga, dya, ((0,), (0,))) + _dot(gb, dyb, ((0,), (0,))) + bds * decay
                dga = _dot(dya, xx, ((1,), (1,)))
                dgb = _dot(dyb, xx, ((1,), (1,)))
                dsega, dsegb = dga * ga, dgb * gb
                dcb = dcb + dga * lma + dgb * lmb
                yoff = _dot(cm, s_in, ((1,), (0,))) * ecs
                dye = dyp * ecs
                dcm = dcm + _dot(dye, s_in, ((1,), (1,)))
                xd = xx * decay
                dbm = dbm + _dot(xd, ds_out, ((1,), (1,)))
                wv = xd * bds
                t1 = dyp * yoff - wv
                col_a = (jnp.sum(dsega, axis=1, keepdims=True) + jnp.sum(jnp.where(lo, t1, 0.0), axis=1, keepdims=True))
                col_b = (jnp.sum(dsegb, axis=1, keepdims=True) + jnp.sum(jnp.where(lo, 0.0, t1), axis=1, keepdims=True))
                sprod = ds_out * s_in
                end_a = jnp.sum(jnp.where(lo, wv, 0.0), keepdims=True) + jnp.exp(la) * jnp.sum(jnp.where(lo[:SSD_N], sprod, 0.0), keepdims=True)
                end_b = jnp.sum(jnp.where(lo, 0.0, wv), keepdims=True) + jnp.exp(lb) * jnp.sum(jnp.where(lo[:SSD_N], 0.0, sprod), keepdims=True)
                col_a = col_a + jnp.where(rowi == SSD_L - 1, end_a, 0.0)
                col_b = col_b + jnp.where(rowi == SSD_L - 1, end_b, 0.0)
                dcs_c = dcs_c + jnp.where(lane == ha, col_a, 0.0) + jnp.where(lane == hb, col_b, 0.0)
                dcs_r = (dcs_r + jnp.where(sub == ha, jnp.sum(dsega, axis=0, keepdims=True), 0.0)
                         + jnp.where(sub == hb, jnp.sum(dsegb, axis=0, keepdims=True), 0.0))
                dstate[q] = _dot(cm, dye, ((0,), (0,))) + cd * ds_out
                dpair = jnp.where(lo1, prow[2:3, ha:ha + 1], prow[2:3, hb:hb + 1])
                dx_ref[:, q * LANE:(q + 1) * LANE] = dxx * dtp + dpair * dyp
                t2 = dxx * xp
                ddt_x = (ddt_x + jnp.where(lane == ha, jnp.sum(jnp.where(lo, t2, 0.0), axis=1, keepdims=True), 0.0)
                         + jnp.where(lane == hb, jnp.sum(jnp.where(lo, 0.0, t2), axis=1, keepdims=True), 0.0))
                t3 = dyp * xp
                dd_row = (dd_row + jnp.where(lane1 == ha, jnp.sum(jnp.where(lo, t3, 0.0), keepdims=True), 0.0)
                          + jnp.where(lane1 == hb, jnp.sum(jnp.where(lo, 0.0, t3), keepdims=True), 0.0))
            dx_ref[:, SSD_INNER + g * SSD_N:SSD_INNER + (g + 1) * SSD_N] = dbm + _dot(dcb, cm, ((0,), (0,)))
            dx_ref[:, SSD_INNER + (SSD_G + g) * SSD_N:SSD_INNER + (SSD_G + g + 1) * SSD_N] = dcm + _dot(dcb, bm, ((1,), (0,)))
        ri = lax.broadcasted_iota(jnp.int32, (SSD_L, SSD_L), 0)
        ci = lax.broadcasted_iota(jnp.int32, (SSD_L, SSD_L), 1)
        da = _dot3((ri <= ci).astype(BF16), dcs_c, ((1,), (0,)), False)
        da = da - _dot3(dcs_r, causal.astype(BF16), ((1,), (0,)), True).T
        ddt = ddt_x + da * a_row
        ddt_raw = ddt * _sigmoid(pre_c)
        ddt_ref[...] = ddt_raw
        da_head = jnp.sum(da * dtc, axis=0, keepdims=True) * a_row
        dp_ref[0:1, :] += jnp.sum(ddt_raw, axis=0, keepdims=True)
        dp_ref[1:2, :] += da_head
        dp_ref[2:3, :] += dd_row

    L = SSD_L
    rev = SSD_NC - 1
    bc_cols = SSD_INNER // (SSD_G * SSD_N)
    return pl.pallas_call(
        body, name="ssd_bwd", grid=(SSD_NC,),
        in_specs=[pl.BlockSpec((L, SSD_INNER), lambda c: (rev - c, 0)),
                  pl.BlockSpec((L, SSD_G * SSD_N), lambda c: (rev - c, bc_cols)),
                  pl.BlockSpec((L, SSD_G * SSD_N), lambda c: (rev - c, bc_cols + 1)),
                  pl.BlockSpec((L, LANE), lambda c: (rev - c, SM_DT // LANE)),
                  pl.BlockSpec((LANE, L), lambda c: (0, rev - c)),
                  pl.BlockSpec((8, LANE), lambda c: (0, 0)), pl.BlockSpec((LANE, 8), lambda c: (0, 0)),
                  pl.BlockSpec((None, NPAIR, SSD_N, LANE), lambda c: (rev - c, 0, 0, 0)),
                  pl.BlockSpec((L, SSD_INNER), lambda c: (rev - c, 0))],
        out_specs=[pl.BlockSpec((L, SSD_XBC), lambda c: (rev - c, 0)),
                   pl.BlockSpec((L, LANE), lambda c: (rev - c, 0)),
                   pl.BlockSpec((8, LANE), lambda c: (0, 0))],
        out_shape=[jax.ShapeDtypeStruct((S, SSD_XBC), F32), jax.ShapeDtypeStruct((S, LANE), F32),
                   jax.ShapeDtypeStruct((8, LANE), F32)],
        scratch_shapes=[pltpu.VMEM((NPAIR, SSD_N, LANE), F32)],
        compiler_params=pltpu.CompilerParams(dimension_semantics=("arbitrary",)),
    )(act, act, act, small, dtT, prow, pcol, states, dy)


TQ = 256
TK = 256
FWD_TQ = 256
FWD_TK = 256


def _attn_fwd(qc, kc, v):
    TQ, TK = FWD_TQ, FWD_TK

    def body(q_ref, k_ref, v_ref, o_ref, lse_ref):
        i = pl.program_id(1)
        lo = lax.broadcasted_iota(jnp.int32, (TQ, LANE), 1) < VDIM
        lo_k = lax.broadcasted_iota(jnp.int32, (TK, LANE), 1) < VDIM
        row_minus_col = lax.broadcasted_iota(jnp.int32, (TQ, TK), 0) - lax.broadcasted_iota(jnp.int32, (TQ, TK), 1)
        qa, qb = q_ref[:, 0:LANE], q_ref[:, LANE:2 * LANE]

        def scores(kb):
            kk = k_ref[pl.ds(pl.multiple_of(kb * TK, TK), TK), :]
            return (_dot(qa, kk[:, 0:LANE], ((1,), (1,))) * ATT_SCALE_LOG2, _dot(qb, kk[:, LANE:2 * LANE], ((1,), (1,))) * ATT_SCALE_LOG2)

        def update(kb, sa, sb, stats):
            ma, la, mb, lb, acc = stats
            vv = v_ref[pl.ds(pl.multiple_of(kb * TK, TK), TK), :]
            na = jnp.maximum(ma, jnp.max(sa, axis=1, keepdims=True))
            nb = jnp.maximum(mb, jnp.max(sb, axis=1, keepdims=True))
            pa, pb = jnp.exp2(sa - na), jnp.exp2(sb - nb)
            fa, fb = jnp.exp2(ma - na), jnp.exp2(mb - nb)
            la = fa * la + jnp.sum(pa, axis=1, keepdims=True)
            lb = fb * lb + jnp.sum(pb, axis=1, keepdims=True)
            acc = (acc * jnp.where(lo, fa, fb) + _dot(pa, jnp.where(lo_k, vv, 0), ((1,), (0,)))
                   + _dot(pb, jnp.where(lo_k, 0, vv), ((1,), (0,))))
            return na, la, nb, lb, acc

        def step(kb, carry):
            sa, sb = carry[:2]
            nxt = scores(kb + 1)
            return nxt + update(kb, sa, sb, carry[2:])

        neg = jnp.full((TQ, 1), NEG, F32)
        zero = jnp.zeros((TQ, 1), F32)
        n_full = i * (TQ // TK)
        carry = lax.fori_loop(0, n_full, step, scores(0) + (neg, zero, neg, zero, jnp.zeros((TQ, LANE), F32)))
        s, stats = carry[:2], carry[2:]
        for d in range(TQ // TK):
            nxt = scores(n_full + d + 1) if d + 1 < TQ // TK else None
            sa, sb = (jnp.where(row_minus_col >= d * TK, t, NEG) for t in s)
            stats = update(n_full + d, sa, sb, stats)
            s = nxt
        ma, la, mb, lb, acc = stats
        o_ref[...] = acc / jnp.where(lo, la, lb)
        lse_ref[...] = jnp.where(lo, ma + jnp.log2(la), mb + jnp.log2(lb)) * LN2

    return pl.pallas_call(
        body, name="attn_fwd", grid=(NPAIR, S // TQ),
        in_specs=[pl.BlockSpec((TQ, 2 * LANE), lambda j, i: (i, j)), pl.BlockSpec((S, 2 * LANE), lambda j, i: (0, j)),
                  pl.BlockSpec((S, LANE), lambda j, i: (0, j))],
        out_specs=[pl.BlockSpec((TQ, LANE), lambda j, i: (i, j)), pl.BlockSpec((None, TQ, LANE), lambda j, i: (j, i, 0))],
        out_shape=[jax.ShapeDtypeStruct((S, H * VDIM), F32), jax.ShapeDtypeStruct((NPAIR, S, LANE), F32)],
        compiler_params=pltpu.CompilerParams(dimension_semantics=("parallel", "parallel")),
    )(qc, kc, v)


def _attn_bwd(qc, kc, v, o, lse, do):
    nq = S // TQ

    def body(q_ref, k_ref, v_ref, o_ref, lse_ref, do_ref, dq_ref, dk_ref, dv_ref):
        kb = pl.program_id(1)

        @pl.when(kb == 0)
        def _():
            dq_ref[...] = jnp.zeros_like(dq_ref)

        lo = lax.broadcasted_iota(jnp.int32, (TQ, LANE), 1) < VDIM
        r0 = lax.broadcasted_iota(jnp.int32, (TQ, TK), 0)
        ck = kb * TK + lax.broadcasted_iota(jnp.int32, (TQ, TK), 1)
        ka, kbb = k_ref[:, 0:LANE], k_ref[:, LANE:2 * LANE]
        vv = v_ref[...]

        def step(qi, carry):
            dka, dkb, dv = carry
            off = pl.multiple_of(qi * TQ, TQ)
            qq = q_ref[pl.ds(off, TQ), :]
            dd = do_ref[pl.ds(off, TQ), :]
            ls = lse_ref[pl.ds(off, TQ), :]
            t = dd * o_ref[pl.ds(off, TQ), :]
            mask = r0 + qi * TQ >= ck
            outs = []
            for x, (kx, lsx) in enumerate(((ka, ls[:, 0:1]), (kbb, ls[:, VDIM:VDIM + 1]))):
                sel = lo if x == 0 else jnp.logical_not(lo)
                qx = qq[:, x * LANE:(x + 1) * LANE]
                dox = jnp.where(sel, dd, 0.0)
                delta = jnp.sum(jnp.where(sel, t, 0.0), axis=1, keepdims=True)
                sc = jnp.where(mask, _dot(qx, kx, ((1,), (1,))) * ATT_SCALE, NEG)
                p = jnp.exp(sc - lsx)
                dp = _dot(dox, vv, ((1,), (1,)))
                ds = p * (dp - delta) * ATT_SCALE
                dv = dv + _dot(p, dox, ((0,), (0,)))
                outs.append(_dot(ds, qx, ((0,), (0,))))
                dq_ref[pl.ds(off, TQ), x * LANE:(x + 1) * LANE] += _dot(ds, kx, ((1,), (0,)))
            return dka + outs[0], dkb + outs[1], dv

        z = jnp.zeros((TK, LANE), F32)
        dka, dkb, dv = lax.fori_loop(kb, nq, step, (z, z, z))
        dk_ref[:, 0:LANE] = dka
        dk_ref[:, LANE:2 * LANE] = dkb
        dv_ref[...] = dv

    return pl.pallas_call(
        body, name="attn_bwd", grid=(NPAIR, S // TK),
        in_specs=[pl.BlockSpec((S, 2 * LANE), lambda j, k: (0, j)), pl.BlockSpec((TK, 2 * LANE), lambda j, k: (k, j)),
                  pl.BlockSpec((TK, LANE), lambda j, k: (k, j)), pl.BlockSpec((S, LANE), lambda j, k: (0, j)),
                  pl.BlockSpec((None, S, LANE), lambda j, k: (j, 0, 0)), pl.BlockSpec((S, LANE), lambda j, k: (0, j))],
        out_specs=[pl.BlockSpec((S, 2 * LANE), lambda j, k: (0, j)), pl.BlockSpec((TK, 2 * LANE), lambda j, k: (k, j)),
                   pl.BlockSpec((TK, LANE), lambda j, k: (k, j))],
        out_shape=[jax.ShapeDtypeStruct((S, H * LANE), F32), jax.ShapeDtypeStruct((S, H * LANE), F32),
                   jax.ShapeDtypeStruct((S, H * VDIM), F32)],
        compiler_params=pltpu.CompilerParams(dimension_semantics=("parallel", "arbitrary")),
    )(qc, kc, v, o, lse, do)


_IN_Z, _IN_XBC, _IN_DT, _IN_Q, _IN_KV, _IN_KR = 0, 1024, 2560, 2576, 2960, 3216


def _prep_in(w_in):
    dt = w_in.dtype
    w_small = jnp.concatenate(
        [w_in[:, _IN_Q:_IN_KV], w_in[:, _IN_KV:_IN_KR], w_in[:, _IN_KR:IN_WIDTH], jnp.zeros((D, LANE - ROPE), dt),
         w_in[:, _IN_DT:_IN_Q], jnp.zeros((D, LANE - H), dt)], axis=1)
    return w_in[:, _IN_Z:_IN_XBC], w_in[:, _IN_XBC:_IN_DT], w_small


def _prep_attn(w_qb, w_kvb):
    w_q = jnp.pad(w_qb.reshape(Q_RANK, H, NOPE + ROPE), ((0, 0), (0, 0), (0, LANE - NOPE - ROPE))).reshape(Q_RANK, H * LANE)
    kv3 = w_kvb.reshape(KV_RANK, H, NOPE + VDIM)
    w_k = jnp.pad(kv3[:, :, :NOPE], ((0, 0), (0, 0), (0, LANE - NOPE))).reshape(KV_RANK, H * LANE)
    w_v = kv3[:, :, NOPE:].reshape(KV_RANK, H * VDIM)
    return w_q, w_k, w_v


def _rope_tables(positions):
    inv_freq = 1.0 / (10000.0 ** (jnp.arange(0, ROPE, 2, dtype=F32) / ROPE))
    ang = positions.astype(F32).reshape(S, 1) * inv_freq
    cos, sin = jnp.cos(ang), jnp.sin(ang)
    cos_t = jnp.concatenate([jnp.ones((S, NOPE), F32), cos, cos, jnp.ones((S, LANE - NOPE - ROPE), F32)], axis=1)
    sin_t = jnp.concatenate([jnp.zeros((S, NOPE), F32), -sin, sin, jnp.zeros((S, LANE - NOPE - ROPE), F32)], axis=1)
    return cos_t, sin_t


def _local_step(x, p, positions, target, w_in, fetch, send, sp):
    w_z, w_xbc, w_small = _prep_in(_from_cols(w_in))
    cos_t, sin_t = _rope_tables(positions)
    prow = jnp.zeros((8, LANE), F32).at[0, :H].set(sp["dt_bias"][0]).at[1, :H].set(sp["A_log"][0]).at[2, :H].set(sp["D"][0])
    pcol = prow.T

    xb, pb = x.astype(BF16), p.astype(BF16)
    z = _mm([(xb, w_z)], name="proj_z")
    xbc = _mm([(xb, w_xbc)], name="proj_xbc")
    small = _mm([(xb, w_small)], name="proj_small")
    act = _conv_fwd(xbc, sp["conv_w"], sp["conv_b"])
    dt_t = small[:, SM_DT:SM_DT + LANE].T
    y, states = _ssd_fwd(act, small, dt_t, prow, pcol)
    y_ssd = _gate_norm_fwd(y, z, sp["ssd_norm"])
    q_c, kv_c = small[:, SM_Q:SM_Q + Q_RANK], small[:, SM_KV:SM_KV + KV_RANK]
    qn = _rms_fwd(q_c, sp["q_norm"], name="q_norm_fwd")
    kvn = _rms_fwd(kv_c, sp["kv_norm"], name="kv_norm_fwd")
    gl = fetch("attn", y_ssd)
    w_q, w_k, w_v = _prep_attn(_from_cols(gl["w_qb"]), _from_cols(gl["w_kvb"]))
    qcat = _q_rope(_mm([(qn, w_q)], name="q_up"), cos_t, sin_t)
    kcat = _k_prep(_mm([(kvn, w_k)], name="k_up"), small, cos_t, sin_t)
    v = _mm([(kvn, w_v)], out_dtype=BF16, name="v_up")
    o, lse = _attn_fwd(qcat, kcat, v)
    y_mla = _rms_fwd(o, sp["out_norm"], name="out_norm_fwd")
    w_out = fetch("out", y_mla)["w_out"]
    w_out_s = w_out[:NCHIP // 2].reshape(SSD_INNER, D)
    w_out_m = w_out[NCHIP // 2:].reshape(SSD_INNER, D)
    mix = _mm([(y_ssd, w_out_s), (y_mla, w_out_m)], name="out_proj")
    h1, h1b = _ln_fwd(x, mix, sp["ln_mix_g"], sp["ln_mix_b"])
    gl = fetch("ffn", h1b)
    w_pg, w_pp = gl["w_pg"].reshape(D, D), _from_cols(gl["w_pp"])
    w_gate, w_up, w_down = gl["w_gate"], gl["w_up"], gl["w_down"]
    gate = _mm([(h1b, w_gate)], chunk="out", name="ffn_gate")
    up = _mm([(h1b, w_up)], chunk="out", name="ffn_up")
    actf = _swiglu_fwd(gate, up)
    ffn = _mm([(actf, w_down)], chunk="sum", name="ffn_down")
    pg = _mm([(h1b, w_pg)], name="ple_gate")
    pp = _mm([(pb, w_pp)], name="ple_proj")
    dpre2, dpre2b, dpg, dpp, dg2, db2, loss_row = _final_fwd_bwd(h1, ffn, pg, pp, target, sp["ln_ffn_g"], sp["ln_ffn_b"])

    g = {"ln_ffn_g": dg2, "ln_ffn_b": db2}
    g["w_pp"] = _to_cols(_mm([(pb, dpp)], ta=True, out_dtype=BF16, name="d_w_ple_proj"))
    g["w_pg"] = _mm([(h1b, dpg)], ta=True, out_dtype=BF16, name="d_w_ple_gate").reshape(NCHIP, D // NCHIP, D)
    g["w_down"] = _mm([(actf, dpre2b)], ta=True, chunk="out", out_dtype=BF16, name="d_w_down")
    dactf = _mm([(dpre2b, w_down)], tb=True, chunk="out", name="d_act")
    dgate, dup = _swiglu_bwd(gate, up, dactf)
    g["w_gate"] = _mm([(h1b, dgate)], ta=True, chunk="out", out_dtype=BF16, name="d_w_gate")
    g["w_up"] = _mm([(h1b, dup)], ta=True, chunk="out", out_dtype=BF16, name="d_w_up")
    sent = send("ffn", {name: g.pop(name) for name in dict(ASYNC_GROUPS)["ffn"]})
    dh1 = _mm([(dpg, w_pg)], tb=True, add=dpre2, add_scale=ALPHA, name="d_h1_ple")
    dh1 = _mm([(dgate, w_gate), (dup, w_up)], tb=True, chunk="sum", add=dh1, name="d_h1")
    dpre1, dpre1b, g["ln_mix_g"], g["ln_mix_b"] = _ln_bwd(x, mix, sp["ln_mix_g"] + sent, dh1)
    dy_ssd = _mm([(dpre1b, w_out_s)], tb=True, name="d_y_ssd")
    dy_mla = _mm([(dpre1b, w_out_m)], tb=True, name="d_y_mla")
    dw_out = jnp.concatenate([_mm([(y_ssd, dpre1b)], ta=True, out_dtype=BF16, name="d_w_out_s"),
                              _mm([(y_mla, dpre1b)], ta=True, out_dtype=BF16, name="d_w_out_m")], axis=0)
    sent = send("out", {"w_out": dw_out.reshape(NCHIP, 2 * SSD_INNER // NCHIP, D)})
    do, g["out_norm"] = _rms_bwd(o, sp["out_norm"] + sent, dy_mla, name="out_norm_bwd")
    dq, dk, dv = _attn_bwd(qcat, kcat, v, o, lse, do)
    dqlin = _q_unrope(dq, cos_t, sin_t)
    dw_q = _mm([(qn, dqlin)], ta=True, out_dtype=BF16, name="d_w_q")
    dqn = _mm([(dqlin, w_q)], tb=True, name="d_qn")
    dq_c, g["q_norm"] = _rms_bwd(q_c, sp["q_norm"], dqn, name="q_norm_bwd")
    dkr = _k_rope_bwd(dk, cos_t, sin_t)
    dw_k = _mm([(kvn, dk)], ta=True, out_dtype=BF16, name="d_w_k")
    dw_v = _mm([(kvn, dv)], ta=True, out_dtype=BF16, name="d_w_v")
    dkvn = _mm([(dk, w_k), (dv, w_v)], tb=True, name="d_kvn")
    dkv_c, g["kv_norm"] = _rms_bwd(kv_c, sp["kv_norm"], dkvn, name="kv_norm_bwd")
    dw_qb = _to_cols(dw_q.reshape(Q_RANK, H, LANE)[:, :, :NOPE + ROPE].reshape(Q_RANK, H * (NOPE + ROPE)))
    dw_kvb = _to_cols(jnp.concatenate([dw_k.reshape(KV_RANK, H, LANE)[:, :, :NOPE], dw_v.reshape(KV_RANK, H, VDIM)],
                                       axis=2).reshape(KV_RANK, H * (NOPE + VDIM)))
    sent = send("attn", {"w_qb": dw_qb, "w_kvb": dw_kvb})
    dy, dz, g["ssd_norm"] = _gate_norm_bwd(y, z, sp["ssd_norm"] + sent, dy_ssd)
    dact, ddt, dprow = _ssd_bwd(act, small, dt_t, prow, pcol, states, dy)
    g["dt_bias"], g["A_log"], g["D"] = dprow[0:1, :H], dprow[1:2, :H], dprow[2:3, :H]
    dxbc, g["conv_w"], g["conv_b"] = _conv_bwd(xbc, sp["conv_w"], sp["conv_b"], dact)
    dsmall = jnp.concatenate([dq_c, dkv_c, dkr, ddt], axis=1).astype(BF16)
    grad_x = _mm([(dz, w_z), (dxbc, w_xbc), (dsmall, w_small)], tb=True, add=dpre1, add_scale=ALPHA, name="d_x")
    dw_small = _mm([(xb, dsmall)], ta=True, out_dtype=BF16, name="d_w_small")
    dw_in = _to_cols(jnp.concatenate(
        [_mm([(xb, dz)], ta=True, out_dtype=BF16, name="d_w_z"), _mm([(xb, dxbc)], ta=True, out_dtype=BF16, name="d_w_xbc"),
         dw_small[:, SM_DT:SM_DT + H], dw_small[:, SM_Q:SM_Q + Q_RANK], dw_small[:, SM_KV:SM_KV + KV_RANK],
         dw_small[:, SM_KR:SM_KR + ROPE]], axis=1))
    return loss_row, grad_x, dw_in, g


MESH = pl.DeviceIdType.MESH
BIG = (("w_in", (D, IN_WIDTH), 1), ("w_qb", (Q_RANK, H * (NOPE + ROPE)), 1), ("w_kvb", (KV_RANK, H * (NOPE + VDIM)), 1),
       ("w_out", (2 * SSD_INNER, D), 0), ("w_gate", (D, D_FF), 1), ("w_up", (D, D_FF), 1), ("w_down", (D_FF, D), 0),
       ("w_pg", (D, D), 0), ("w_pp", (PLE, D), 1))
CONV_SHARD = SSD_XBC // NCHIP
BF16_ROWS = 16


def _from_cols(stack):
    return jnp.concatenate([stack[k] for k in range(NCHIP)], axis=1)


def _to_cols(full):
    r, c4 = full.shape
    return full.reshape(r, NCHIP, c4 // NCHIP).transpose(1, 0, 2)


def _coords():
    return lax.axis_index("x"), lax.axis_index("y"), lax.axis_index("c")


def _peers():
    x, y, c = _coords()
    return 2 * x + y, c, [(1 - x, y), (x, 1 - y), (1 - x, 1 - y)], (x, y, 1 - c)


def _half(c, rows):
    return pl.ds(pl.multiple_of(c * (rows // 2), BF16_ROWS), rows // 2)


def _gather_weights(shards):
    n_arr = len(shards)
    split = [s.shape[0] % (2 * BF16_ROWS) == 0 for s in shards]
    per = 2 * (NCHIP - 1)

    def body(*refs):
        ins, outs = refs[:n_arr], refs[n_arr:2 * n_arr]
        send_sems, recv_sems, local_sems = refs[2 * n_arr:]
        k, c, chips, sibling = _peers()

        def copy(idx, src, dst, to):
            return pltpu.make_async_remote_copy(src_ref=src, dst_ref=dst, send_sem=send_sems.at[idx], recv_sem=recv_sems.at[idx],
                                                device_id=to, device_id_type=MESH)

        def part(a, chip, core):
            return outs[a].at[chip, _half(core, shards[a].shape[0])] if split[a] else outs[a].at[chip]

        mine = [pltpu.make_async_copy(ins[a], outs[a].at[k], local_sems.at[a]) for a in range(n_arr)]
        for cp in mine:
            cp.start()
        sends = []
        for a in range(n_arr):
            src = ins[a].at[_half(c, shards[a].shape[0])] if split[a] else ins[a]
            for j, (cx, cy) in enumerate(chips):
                sends.append(copy(per * a + j, src, part(a, k, c), (cx, cy, c)))
                sends[-1].start()
        for j, (cx, cy) in enumerate(chips):
            for a in range(n_arr):
                landed = part(a, 2 * cx + cy, c)
                copy(per * a + j, landed, landed, (cx, cy, c)).wait_recv()
                if split[a]:
                    sends.append(copy(per * a + NCHIP - 1 + j, landed, landed, sibling))
                    sends[-1].start()
        for j, (cx, cy) in enumerate(chips):
            for a in range(n_arr):
                if split[a]:
                    other = part(a, 2 * cx + cy, 1 - c)
                    copy(per * a + NCHIP - 1 + j, other, other, sibling).wait_recv()
        for cp in sends:
            cp.wait_send()
        for cp in mine:
            cp.wait()

    any_spec = pl.BlockSpec(memory_space=pl.ANY)
    return pl.pallas_call(
        body, name="gather_weights", in_specs=[any_spec] * n_arr, out_specs=[any_spec] * n_arr,
        out_shape=[jax.ShapeDtypeStruct((NCHIP,) + s.shape, s.dtype) for s in shards],
        scratch_shapes=[pltpu.SemaphoreType.DMA((per * n_arr,)), pltpu.SemaphoreType.DMA((per * n_arr,)),
                        pltpu.SemaphoreType.DMA((n_arr,))],
    )(*shards)


ASYNC_GROUPS = (("attn", ("w_qb", "w_kvb")), ("out", ("w_out",)), ("ffn", ("w_gate", "w_up", "w_down", "w_pg", "w_pp")))
HBM_SPEC = pl.BlockSpec(memory_space=pltpu.HBM)
SEM_SPEC = pl.BlockSpec(memory_space=pltpu.SEMAPHORE)
IN_FLIGHT = pltpu.SideEffectType.DATAFLOW_SIDE_EFFECTING


def _in_hbm(a):
    return pltpu.with_memory_space_constraint(a, pltpu.HBM)


def _hbm_like(arrs, lead=()):
    return [pltpu.HBM(lead + a.shape, a.dtype) for a in arrs]


def _split_start(name, srcs, lands, after, n_sem, start):
    n = len(srcs)

    def body(*refs):
        src_refs, land_refs = refs[:n], refs[n:2 * n]
        send_sems, recv_sems = refs[2 * n + 1], refs[2 * n + 2]
        token = refs[-1]

        def copy(send_idx, recv_idx, src, dst, to):
            return pltpu.make_async_remote_copy(src_ref=src, dst_ref=dst, send_sem=send_sems.at[send_idx],
                                                recv_sem=recv_sems.at[recv_idx], device_id=to, device_id_type=MESH)

        for cp in start(src_refs, land_refs, copy):
            cp.start()
        token[...] = jnp.zeros_like(token)

    sem = pltpu.SemaphoreType.DMA((n_sem,))
    outs = pl.pallas_call(
        body, name=name, in_specs=[HBM_SPEC] * (2 * n) + [pl.BlockSpec(memory_space=pl.ANY)],
        out_specs=[SEM_SPEC, SEM_SPEC] + [HBM_SPEC] * (2 * n) + [pl.BlockSpec(memory_space=pltpu.VMEM)],
        out_shape=[sem, sem] + _hbm_like(srcs) + _hbm_like(lands) + [jax.ShapeDtypeStruct((8, LANE), F32)],
        input_output_aliases={i: 2 + i for i in range(2 * n)},
        compiler_params=pltpu.CompilerParams(has_side_effects=IN_FLIGHT),
    )(*[_in_hbm(a) for a in srcs], *[_in_hbm(a) for a in lands], after)
    return (outs[0], outs[1], outs[2:2 + n], outs[2 + n:2 + 2 * n]), outs[-1]


def _split_wait(name, send_sems, recv_sems, srcs, lands, after, waits):
    n = len(srcs)

    def body(*refs):
        src_refs, land_refs = refs[:n], refs[n:2 * n]
        send_ref, recv_ref = refs[2 * n], refs[2 * n + 1]

        def copy(send_idx, recv_idx, src, dst, to):
            return pltpu.make_async_remote_copy(src_ref=src, dst_ref=dst, send_sem=send_ref.at[send_idx],
                                                recv_sem=recv_ref.at[recv_idx], device_id=to, device_id_type=MESH)

        for cp in waits(src_refs, land_refs, copy):
            cp.wait_send()
            cp.wait_recv()

    outs = pl.pallas_call(
        body, name=name, in_specs=[HBM_SPEC] * (2 * n) + [SEM_SPEC, SEM_SPEC, pl.BlockSpec(memory_space=pl.ANY)],
        out_specs=[HBM_SPEC] * (2 * n), out_shape=_hbm_like(srcs) + _hbm_like(lands),
        input_output_aliases={i: i for i in range(2 * n)},
        compiler_params=pltpu.CompilerParams(has_side_effects=IN_FLIGHT),
    )(*srcs, *lands, send_sems, recv_sems, after)
    return outs[:n], outs[n:]


GATHER_LATE_SEMS = 2 * (NCHIP - 1)


def _gather_async_start(tag, shards, after):
    def start(srcs, lands, copy):
        k, c, chips, _ = _peers()
        out = []
        for a, (src, dst) in enumerate(zip(srcs, lands)):
            rows = src.shape[0]
            for j, (cx, cy) in enumerate(chips):
                for core in range(2):
                    out.append(copy(GATHER_LATE_SEMS * a + 2 * j + core, GATHER_LATE_SEMS * a + 2 * j + c,
                                    src.at[_half(c, rows)], dst.at[k, _half(c, rows)], (cx, cy, core)))
        return out

    chip = 2 * lax.axis_index("x") + lax.axis_index("y")
    lands = [lax.dynamic_update_slice(lax.empty((NCHIP,) + s.shape, s.dtype), s[None], (chip, 0, 0)) for s in shards]
    return _split_start("gather_%s_start" % tag, shards, lands, after, GATHER_LATE_SEMS * len(shards), start)


def _gather_async_wait(tag, send_sems, recv_sems, shards, lands, after):
    def waits(srcs, lands_, copy):
        _, c, chips, _ = _peers()
        out = []
        for a, (src, dst) in enumerate(zip(srcs, lands_)):
            rows = src.shape[0]
            for j, (cx, cy) in enumerate(chips):
                for core in range(2):
                    idx = GATHER_LATE_SEMS * a + 2 * j + core
                    out.append(copy(idx, idx, src.at[_half(c, rows)], dst.at[2 * cx + cy, _half(core, rows)], (cx, cy, core)))
        return out

    return _split_wait("gather_%s_wait" % tag, send_sems, recv_sems, shards, lands, after, waits)[1]


def _other_devices():
    x, y, c = _coords()
    out = []
    for d in range(1, NDEV):
        tx, ty, tc = x ^ (d >> 2), y ^ ((d >> 1) & 1), c ^ (d & 1)
        out.append((d, (tx, ty, tc), 2 * tx + ty, 4 * tx + 2 * ty + tc))
    return out


def _reduce_async_start(tag, stacks, after):
    def start(srcs, lands, copy):
        x, y, c = _coords()
        me = 4 * x + 2 * y + c
        return [copy((NDEV - 1) * a + d - 1, (NDEV - 1) * a + d - 1, src.at[chip, _half(to[2], src.shape[1])], dst.at[me], to)
                for a, (src, dst) in enumerate(zip(srcs, lands)) for d, to, chip, _ in _other_devices()]

    x, y, c = _coords()
    lands = []
    for s in stacks:
        hr = s.shape[1] // 2
        own = lax.dynamic_slice(s, (2 * x + y, c * hr, 0), (1, hr, s.shape[2]))
        lands.append(lax.dynamic_update_slice(lax.empty((NDEV, hr, s.shape[2]), s.dtype), own, (4 * x + 2 * y + c, 0, 0)))
    return _split_start("reduce_%s_start" % tag, stacks, lands, after, (NDEV - 1) * len(stacks), start)


def _reduce_async_wait(tag, send_sems, recv_sems, stacks, lands, after):
    def waits(srcs, lands_, copy):
        return [copy((NDEV - 1) * a + d - 1, (NDEV - 1) * a + d - 1, src.at[chip, _half(to[2], src.shape[1])], dst.at[pos], to)
                for a, (src, dst) in enumerate(zip(srcs, lands_)) for d, to, chip, pos in _other_devices()]

    return _split_wait("reduce_%s_wait" % tag, send_sems, recv_sems, stacks, lands, after, waits)[1]


def _reduce_finish(tag, arrived):
    n_arr = len(arrived)
    dims = [(2 * p.shape[1], p.shape[2]) for p in arrived]

    def body(*refs):
        lands, fin = refs[:n_arr], refs[n_arr:2 * n_arr]
        send_sems, recv_sems = refs[2 * n_arr:]
        _, c, _, sibling = _peers()
        sends = []
        for a in range(n_arr):
            mine = fin[a].at[_half(c, dims[a][0])]

            def device_sum(vs, vf, a=a, mine=mine):
                pltpu.sync_copy(lands[a], vs)
                acc = vs[0].astype(F32)
                for i in range(1, NDEV):
                    acc = acc + vs[i].astype(F32)
                vf[...] = acc
                pltpu.sync_copy(vf, mine)

            pl.run_scoped(device_sum, pltpu.VMEM((NDEV, dims[a][0] // 2, dims[a][1]), BF16), pltpu.VMEM((dims[a][0] // 2, dims[a][1]), F32))
            sends.append(pltpu.make_async_remote_copy(src_ref=mine, dst_ref=mine, send_sem=send_sems.at[a], recv_sem=recv_sems.at[a],
                                                      device_id=sibling, device_id_type=MESH))
            sends[-1].start()
        for a in range(n_arr):
            other = fin[a].at[_half(1 - c, dims[a][0])]
            pltpu.make_async_remote_copy(src_ref=other, dst_ref=other, send_sem=send_sems.at[a], recv_sem=recv_sems.at[a],
                                         device_id=sibling, device_id_type=MESH).wait_recv()
        for cp in sends:
            cp.wait_send()

    any_spec = pl.BlockSpec(memory_space=pl.ANY)
    return pl.pallas_call(
        body, name="reduce_%s_finish" % tag, in_specs=[any_spec] * n_arr, out_specs=[any_spec] * n_arr,
        out_shape=[jax.ShapeDtypeStruct(d, F32) for d in dims],
        scratch_shapes=[pltpu.SemaphoreType.DMA((n_arr,)), pltpu.SemaphoreType.DMA((n_arr,))],
    )(*arrived)


SMALL = (("conv_w", SSD_K * SSD_XBC), ("conv_b", SSD_XBC), ("dt_bias", H), ("A_log", H), ("D", H), ("ssd_norm", SSD_INNER),
         ("q_norm", Q_RANK), ("kv_norm", KV_RANK), ("out_norm", SSD_INNER), ("ln_mix_g", D), ("ln_mix_b", D),
         ("ln_ffn_g", D), ("ln_ffn_b", D))
SMALL_ROWS = 120
NDEV = 8


def _allreduce_small(sv):
    def body(sv_ref, out_ref, slots, send_sems, recv_sems):
        x, y, c = _coords()
        me = 4 * x + 2 * y + c
        slots[me] = sv_ref[...]
        copies = []
        for d in range(1, NDEV):
            to = (x ^ (d >> 2), y ^ ((d >> 1) & 1), c ^ (d & 1))
            copies.append(pltpu.make_async_remote_copy(src_ref=sv_ref, dst_ref=slots.at[me], send_sem=send_sems.at[d - 1],
                                                       recv_sem=recv_sems.at[d - 1], device_id=to, device_id_type=MESH))
            copies[-1].start()
        for cp in copies:
            cp.wait_recv()
        for cp in copies:
            cp.wait_send()
        acc = slots[0]
        for i in range(1, NDEV):
            acc = acc + slots[i]
        out_ref[...] = acc

    vm = pl.BlockSpec(memory_space=pltpu.VMEM)
    return pl.pallas_call(
        body, name="allreduce_small", in_specs=[vm], out_specs=vm, out_shape=jax.ShapeDtypeStruct((SMALL_ROWS, LANE), F32),
        scratch_shapes=[pltpu.VMEM((NDEV, SMALL_ROWS, LANE), F32), pltpu.SemaphoreType.DMA((NDEV - 1,)),
                        pltpu.SemaphoreType.DMA((NDEV - 1,))],
    )(sv)


def _adamw_math(w, g, m, v):
    m2 = ADAM_B1 * m + (1.0 - ADAM_B1) * g
    v2 = ADAM_B2 * v + (1.0 - ADAM_B2) * (g * g)
    m_hat = m2 / (1.0 - ADAM_B1 ** ADAM_STEP)
    v_hat = v2 / (1.0 - ADAM_B2 ** ADAM_STEP)
    return -ADAM_LR * (m_hat / (jnp.sqrt(v_hat) + ADAM_EPS) + ADAM_WD * w), m2, v2


def _adamw_big(w, g, m, v, *, name):
    r, c = w.shape
    tr = next(t for t in (512, 384, 352, 256, 128, 64, 8) if r % t == 0)

    def body(w_ref, g_ref, m_ref, v_ref, d_ref, m2_ref, v2_ref):
        d_ref[...], m2_ref[...], v2_ref[...] = _adamw_math(w_ref[...], g_ref[...], m_ref[...], v_ref[...])

    spec = pl.BlockSpec((tr, c), lambda i: (i, 0))
    return pl.pallas_call(body, name=name, grid=(r // tr,), in_specs=[spec] * 4, out_specs=[spec] * 3,
                          out_shape=[jax.ShapeDtypeStruct((r, c), F32)] * 3)(w, g, m, v)


def _adamw_small(ws, gs, ms, vs):
    n = len(ws)

    def body(*refs):
        for i in range(n):
            w_ref, g_ref, m_ref, v_ref = (refs[j * n + i] for j in range(4))
            d_ref, m2_ref, v2_ref = (refs[(4 + j) * n + i] for j in range(3))
            d_ref[...], m2_ref[...], v2_ref[...] = _adamw_math(w_ref[...], g_ref[...], m_ref[...], v_ref[...])

    vm = pl.BlockSpec(memory_space=pltpu.VMEM)
    shapes = [jax.ShapeDtypeStruct(w.shape, F32) for w in ws]
    outs = pl.pallas_call(body, name="adamw_small", in_specs=[vm] * (4 * n), out_specs=[vm] * (3 * n), out_shape=shapes * 3)(
        *ws, *gs, *ms, *vs)
    return outs[:n], outs[n:2 * n], outs[2 * n:]


_SMALL_ARG = {"conv_w": "ssd_conv_w", "conv_b": "ssd_conv_b", "dt_bias": "ssd_dt_bias", "A_log": "ssd_A_log", "D": "ssd_D",
              "ssd_norm": "ssd_norm_w", "q_norm": "mla_q_norm_w", "kv_norm": "mla_kv_norm_w", "out_norm": "mla_out_norm_w",
              "ln_mix_g": "ln_mix_g", "ln_mix_b": "ln_mix_b", "ln_ffn_g": "ln_ffn_g", "ln_ffn_b": "ln_ffn_b"}
_BIG_ARG = {"w_in": "w_in", "w_qb": "mla_w_q_b", "w_kvb": "mla_w_kv_b", "w_out": "w_out", "w_gate": "w_ffn_gate",
            "w_up": "w_ffn_up", "w_down": "w_ffn_down", "w_pg": "w_ple_gate", "w_pp": "w_ple_proj"}
_WEIGHT_ORDER = ("w_in", "ssd_conv_w", "ssd_conv_b", "ssd_dt_bias", "ssd_A_log", "ssd_D", "ssd_norm_w", "mla_q_norm_w", "mla_w_q_b",
                 "mla_kv_norm_w", "mla_w_kv_b", "mla_out_norm_w", "w_out", "ln_mix_g", "ln_mix_b", "w_ffn_gate", "w_ffn_up",
                 "w_ffn_down", "w_ple_gate", "w_ple_proj", "ln_ffn_g", "ln_ffn_b")


def _rows128(a):
    flat = a.reshape(-1)
    return jnp.pad(flat, (0, -flat.shape[0] % LANE)).reshape(-1, LANE)


def kernel(x, p, positions, w_in, ssd_conv_w, ssd_conv_b, ssd_dt_bias, ssd_A_log, ssd_D, ssd_norm_w, mla_q_norm_w, mla_w_q_b, mla_kv_norm_w, mla_w_kv_b, mla_out_norm_w, w_out, ln_mix_g, ln_mix_b, w_ffn_gate, w_ffn_up, w_ffn_down, w_ple_gate, w_ple_proj, ln_ffn_g, ln_ffn_b, loss_target, m_w_in, m_ssd_conv_w, m_ssd_conv_b, m_ssd_dt_bias, m_ssd_A_log, m_ssd_D, m_ssd_norm_w, m_mla_q_norm_w, m_mla_w_q_b, m_mla_kv_norm_w, m_mla_w_kv_b, m_mla_out_norm_w, m_w_out, m_ln_mix_g, m_ln_mix_b, m_w_ffn_gate, m_w_ffn_up, m_w_ffn_down, m_w_ple_gate, m_w_ple_proj, m_ln_ffn_g, m_ln_ffn_b, v_w_in, v_ssd_conv_w, v_ssd_conv_b, v_ssd_dt_bias, v_ssd_A_log, v_ssd_D, v_ssd_norm_w, v_mla_q_norm_w, v_mla_w_q_b, v_mla_kv_norm_w, v_mla_w_kv_b, v_mla_out_norm_w, v_w_out, v_ln_mix_g, v_ln_mix_b, v_w_ffn_gate, v_w_ffn_up, v_w_ffn_down, v_w_ple_gate, v_w_ple_proj, v_ln_ffn_g, v_ln_ffn_b):
    given = dict(locals())
    chip = 2 * lax.axis_index("x") + lax.axis_index("y")

    conv_bits = lax.bitcast_convert_type(ssd_conv_w[0], BF16).reshape(SSD_K, 2 * CONV_SHARD)
    w_in_all, conv_all = _gather_weights([w_in[0].astype(BF16), jnp.pad(conv_bits, ((0, BF16_ROWS - SSD_K), (0, 0)))])
    sp = {k: given[a] for k, a in _SMALL_ARG.items() if k != "conv_w"}
    sp["conv_w"] = _from_cols(lax.bitcast_convert_type(conv_all[:, :SSD_K].reshape(NCHIP, SSD_K, CONV_SHARD, 2), F32))
    gathering, tie = {}, w_in_all
    for group, names in ASYNC_GROUPS:
        gathering[group], tie = _gather_async_start(group, [given[_BIG_ARG[name]][0].astype(BF16) for name in names], tie)

    def fetch(group, after):
        return dict(zip(dict(ASYNC_GROUPS)[group], _gather_async_wait(group, *gathering[group], after)))

    reducing = {}

    def send(group, grads):
        reducing[group], sent = _reduce_async_start(group, [grads[name] for name in dict(ASYNC_GROUPS)[group]], grads[dict(ASYNC_GROUPS)[group][0]])
        return sent[0, 0]

    loss_row, grad_x, dw_in, g = _local_step(x[0] + tie[0, 0], p[0, 0], positions[0], loss_target[0], w_in_all, fetch, send, sp)

    reducing["in"], tie = _reduce_async_start("in", [dw_in], grad_x)
    gbig = {}
    for group, names in reversed(ASYNC_GROUPS):
        gbig.update(zip(names, _reduce_finish(group, _reduce_async_wait(group, *reducing[group], tie))))
    small_in = jnp.concatenate([_rows128(g[name]) for name, _ in SMALL] + [loss_row], axis=0)
    small_sum = _allreduce_small(jnp.pad(small_in, ((0, SMALL_ROWS - small_in.shape[0]), (0, 0))))
    gsmall, row = {}, 0
    for name, size in SMALL:
        nrow = -(-size // LANE)
        gsmall[name] = small_sum[row:row + nrow].reshape(-1)[:size]
        row += nrow
    loss = small_sum[row, 0]

    grads = {_BIG_ARG[name]: arr[None] for name, arr in gbig.items()}
    for name, _ in SMALL:
        if name == "conv_w":
            full_g = gsmall[name].reshape(SSD_K, SSD_XBC)
            grads["ssd_conv_w"] = lax.dynamic_slice(full_g, (0, chip * CONV_SHARD), (SSD_K, CONV_SHARD))[None]
        else:
            grads[_SMALL_ARG[name]] = gsmall[name].reshape(given[_SMALL_ARG[name]].shape)

    delta, new_m, new_v = {}, {}, {}

    def update_matrix(name):
        a = _BIG_ARG[name]
        d, m2, v2 = _adamw_big(given[a][0], grads[a][0], given["m_" + a][0], given["v_" + a][0], name="adamw_" + a)
        delta[a], new_m[a], new_v[a] = d[None], m2[None], v2[None]
        return d

    for name in gbig:
        last = update_matrix(name)
    grads["w_in"] = _reduce_finish("in", _reduce_async_wait("in", *reducing["in"], last))[0][None]
    update_matrix("w_in")
    small_names = [_SMALL_ARG[name] for name, _ in SMALL]
    two_d = lambda t: t.reshape(t.shape[-2], t.shape[-1])
    ds, ms, vs = _adamw_small([two_d(given[a]) for a in small_names], [two_d(grads[a]) for a in small_names],
                              [two_d(given["m_" + a]) for a in small_names], [two_d(given["v_" + a]) for a in small_names])
    for a, d, m2, v2 in zip(small_names, ds, ms, vs):
        delta[a], new_m[a], new_v[a] = (t.reshape(given[a].shape) for t in (d, m2, v2))

    return (loss, grad_x[None], *[grads[n] for n in _WEIGHT_ORDER], *[delta[n] for n in _WEIGHT_ORDER],
            *[new_m[n] for n in _WEIGHT_ORDER], *[new_v[n] for n in _WEIGHT_ORDER])
```

```python
import functools
import math

import jax
import jax.numpy as jnp
from jax import lax
from jax.experimental import pallas as pl
from jax.experimental.pallas import tpu as pltpu

F32 = jnp.float32
BF16 = jnp.bfloat16

S = 2048
D = 1024
PLE = 256
H = 16
SSD_P = 64
SSD_INNER = 1024
SSD_N = 128
SSD_G = 2
SSD_L = 128
SSD_NC = S // SSD_L
SSD_XBC = 1536
SSD_K = 4
Q_RANK = 384
KV_RANK = 256
NOPE = 64
ROPE = 32
VDIM = 64
D_FF = 2816
IN_WIDTH = 3248
ALPHA = 2.0 ** 0.25
EPS_RMS = 1e-6
EPS_LN = 1e-5
ATT_SCALE = 1.0 / math.sqrt(NOPE + ROPE)
LN2 = math.log(2.0)
ATT_SCALE_LOG2 = ATT_SCALE / LN2
LANE = 128
NCHIP = 4
SMALL_W = 896
SM_Q, SM_KV, SM_KR, SM_DT = 0, 384, 640, 768
NEG = -1e30

ADAM_LR = 0.001
ADAM_B1 = 0.9
ADAM_B2 = 0.999
ADAM_EPS = 1e-08
ADAM_WD = 0.01
ADAM_STEP = 10


def _sigmoid(v):
    return 1.0 / (1.0 + jnp.exp(-v))


MM_VMEM_BUDGET = 36 * 2 ** 20
MM_MAX_ACC = 2048 * 1024


def _mm_tiles(pairs, ta, tb, m, n, out_dtype, has_add):
    def divs(v):
        return [LANE * d for d in range(v // LANE, 0, -1) if (v // LANE) % d == 0] if v % LANE == 0 else [v]

    def cost(tm, tn):
        tot = tm * tn * (jnp.dtype(out_dtype).itemsize + (4 if has_add else 0))
        for a, b in pairs:
            k = a.shape[-2] if ta else a.shape[-1]
            tot += k * (tm * a.dtype.itemsize + tn * b.dtype.itemsize)
        return 2 * tot

    ok = [(tm * tn, tm, tn) for tm in divs(m) for tn in divs(n) if tm * tn <= MM_MAX_ACC and cost(tm, tn) <= MM_VMEM_BUDGET]
    _, tm, tn = max(ok)
    return tm, tn


def _mm(pairs, *, ta=False, tb=False, out_dtype=F32, add=None, add_scale=1.0, chunk=None, name):
    n_pairs = len(pairs)
    a0, b0 = pairs[0]
    m = a0.shape[-1] if ta else a0.shape[-2]
    n = b0.shape[-2] if tb else b0.shape[-1]
    tm, tn = _mm_tiles(pairs, ta, tb, m, n, out_dtype, add is not None)
    dims = (((0 if ta else 1,), (1 if tb else 0,)), ((), ()))
    nk = NCHIP if chunk else 1
    assert chunk != "sum" or out_dtype == F32

    def body(*refs):
        o_ref = refs[-1]
        acc = None
        for i in range(n_pairs):
            a = refs[2 * i][...].astype(BF16)
            b = refs[2 * i + 1][...].astype(BF16)
            part = lax.dot_general(a, b, dims, preferred_element_type=F32)
            acc = part if acc is None else acc + part
        if chunk == "sum":
            k = pl.program_id(2)

            @pl.when(k == 0)
            def _():
                o_ref[...] = acc + add_scale * refs[2 * n_pairs][...] if add is not None else acc

            @pl.when(k > 0)
            def _():
                o_ref[...] += acc
        else:
            if add is not None:
                acc = acc + add_scale * refs[2 * n_pairs][...]
            o_ref[...] = acc.astype(out_dtype)

    def spec(arr, shape, idx2):
        if arr.ndim == 3:
            return pl.BlockSpec((None,) + shape, lambda i, j, k: (k,) + idx2(i, j))
        return pl.BlockSpec(shape, lambda i, j, k: idx2(i, j))

    in_specs, args = [], []
    for a, b in pairs:
        kdim = a.shape[-2] if ta else a.shape[-1]
        in_specs.append(spec(a, (kdim, tm), lambda i, j: (0, i)) if ta else spec(a, (tm, kdim), lambda i, j: (i, 0)))
        in_specs.append(spec(b, (tn, kdim), lambda i, j: (j, 0)) if tb else spec(b, (kdim, tn), lambda i, j: (0, j)))
        args += [a, b]
    if add is not None:
        in_specs.append(pl.BlockSpec((tm, tn), lambda i, j, k: (i, j)))
        args.append(add)
    if chunk == "out":
        out_spec = pl.BlockSpec((None, tm, tn), lambda i, j, k: (k, i, j))
        out_shape = jax.ShapeDtypeStruct((nk, m, n), out_dtype)
    else:
        out_spec = pl.BlockSpec((tm, tn), lambda i, j, k: (i, j))
        out_shape = jax.ShapeDtypeStruct((m, n), out_dtype)
    return pl.pallas_call(
        body, name=name, grid=(m // tm, n // tn, nk), in_specs=in_specs, out_specs=out_spec, out_shape=out_shape,
        compiler_params=pltpu.CompilerParams(dimension_semantics=("parallel", "parallel", "arbitrary")),
    )(*args)


TR = 256


def _row_spec(c):
    return pl.BlockSpec((TR, c), lambda i: (i, 0))


def _vec_spec(c):
    return pl.BlockSpec((1, c), lambda i: (0, 0))


def _acc_rows(ref, val):
    @pl.when(pl.program_id(0) == 0)
    def _():
        ref[...] = jnp.zeros_like(ref)
    ref[...] += val


def _rms_fwd(u, w, *, name):
    c = u.shape[1]

    def body(u_ref, w_ref, o_ref):
        v = u_ref[...]
        r = lax.rsqrt(jnp.mean(v * v, axis=-1, keepdims=True) + EPS_RMS)
        o_ref[...] = (v * r * w_ref[...]).astype(BF16)

    return pl.pallas_call(body, name=name, grid=(S // TR,), in_specs=[_row_spec(c), _vec_spec(c)], out_specs=_row_spec(c),
                          out_shape=jax.ShapeDtypeStruct((S, c), BF16))(u, w)


def _rms_bwd(u, w, dy, *, name):
    c = u.shape[1]

    def body(u_ref, w_ref, dy_ref, du_ref, dw_ref):
        v = u_ref[...]
        g = dy_ref[...].astype(F32)
        r = lax.rsqrt(jnp.mean(v * v, axis=-1, keepdims=True) + EPS_RMS)
        gw = g * w_ref[...]
        du_ref[...] = r * gw - v * (r * r * r * jnp.mean(gw * v, axis=-1, keepdims=True))
        _acc_rows(dw_ref, jnp.sum(g * v * r, axis=0, keepdims=True))

    return pl.pallas_call(body, name=name, grid=(S // TR,), in_specs=[_row_spec(c), _vec_spec(c), _row_spec(c)],
                          out_specs=[_row_spec(c), _vec_spec(c)],
                          out_shape=[jax.ShapeDtypeStruct((S, c), F32), jax.ShapeDtypeStruct((1, c), F32)])(u, w, dy)


def _gate_norm_fwd(y, z, w):
    def body(y_ref, z_ref, w_ref, o_ref):
        zz = z_ref[...]
        v = y_ref[...] * (zz * _sigmoid(zz))
        r = lax.rsqrt(jnp.mean(v * v, axis=-1, keepdims=True) + EPS_RMS)
        o_ref[...] = (v * r * w_ref[...]).astype(BF16)

    c = SSD_INNER
    return pl.pallas_call(body, name="ssd_gate_norm_fwd", grid=(S // TR,), in_specs=[_row_spec(c), _row_spec(c), _vec_spec(c)],
                          out_specs=_row_spec(c), out_shape=jax.ShapeDtypeStruct((S, c), BF16))(y, z, w)


def _gate_norm_bwd(y, z, w, dout):
    def body(y_ref, z_ref, w_ref, g_ref, dy_ref, dz_ref, dw_ref):
        yy = y_ref[...]
        zz = z_ref[...]
        sg = _sigmoid(zz)
        sz = zz * sg
        v = yy * sz
        g = g_ref[...]
        r = lax.rsqrt(jnp.mean(v * v, axis=-1, keepdims=True) + EPS_RMS)
        gw = g * w_ref[...]
        dv = r * gw - v * (r * r * r * jnp.mean(gw * v, axis=-1, keepdims=True))
        dy_ref[...] = dv * sz
        dz_ref[...] = (dv * yy * (sg * (1.0 + zz * (1.0 - sg)))).astype(BF16)
        _acc_rows(dw_ref, jnp.sum(g * v * r, axis=0, keepdims=True))

    c = SSD_INNER
    return pl.pallas_call(body, name="ssd_gate_norm_bwd", grid=(S // TR,),
                          in_specs=[_row_spec(c), _row_spec(c), _vec_spec(c), _row_spec(c)],
                          out_specs=[_row_spec(c), _row_spec(c), _vec_spec(c)],
                          out_shape=[jax.ShapeDtypeStruct((S, c), F32), jax.ShapeDtypeStruct((S, c), BF16),
                                     jax.ShapeDtypeStruct((1, c), F32)])(y, z, w, dout)


def _ln_fwd(xr, mix, g, b):
    def body(x_ref, m_ref, g_ref, b_ref, o_ref, ob_ref):
        pre = ALPHA * x_ref[...] + m_ref[...]
        mu = jnp.mean(pre, axis=-1, keepdims=True)
        d = pre - mu
        rs = lax.rsqrt(jnp.mean(d * d, axis=-1, keepdims=True) + EPS_LN)
        h = d * rs * g_ref[...] + b_ref[...]
        o_ref[...] = h
        ob_ref[...] = h.astype(BF16)

    return pl.pallas_call(body, name="ln_mix_fwd", grid=(S // TR,), in_specs=[_row_spec(D), _row_spec(D), _vec_spec(D), _vec_spec(D)],
                          out_specs=[_row_spec(D)] * 2,
                          out_shape=[jax.ShapeDtypeStruct((S, D), F32), jax.ShapeDtypeStruct((S, D), BF16)])(xr, mix, g, b)


def _ln_bwd(xr, mix, g, dh):
    def body(x_ref, m_ref, g_ref, dh_ref, dpre_ref, dpreb_ref, dg_ref, db_ref):
        pre = ALPHA * x_ref[...] + m_ref[...]
        mu = jnp.mean(pre, axis=-1, keepdims=True)
        d = pre - mu
        rs = lax.rsqrt(jnp.mean(d * d, axis=-1, keepdims=True) + EPS_LN)
        xh = d * rs
        dy = dh_ref[...]
        gy = dy * g_ref[...]
        dpre = rs * (gy - jnp.mean(gy, axis=-1, keepdims=True) - xh * jnp.mean(gy * xh, axis=-1, keepdims=True))
        dpre_ref[...] = dpre
        dpreb_ref[...] = dpre.astype(BF16)
        _acc_rows(dg_ref, jnp.sum(dy * xh, axis=0, keepdims=True))
        _acc_rows(db_ref, jnp.sum(dy, axis=0, keepdims=True))

    return pl.pallas_call(body, name="ln_mix_bwd", grid=(S // TR,),
                          in_specs=[_row_spec(D), _row_spec(D), _vec_spec(D), _row_spec(D)],
                          out_specs=[_row_spec(D), _row_spec(D), _vec_spec(D), _vec_spec(D)],
                          out_shape=[jax.ShapeDtypeStruct((S, D), F32), jax.ShapeDtypeStruct((S, D), BF16),
                                     jax.ShapeDtypeStruct((1, D), F32), jax.ShapeDtypeStruct((1, D), F32)])(xr, mix, g, dh)


FF_CHUNK = D_FF // NCHIP


def _ff_spec():
    return pl.BlockSpec((None, TR * 2, FF_CHUNK), lambda k, i: (k, i, 0))


def _swiglu_fwd(gate, up):
    def body(g_ref, u_ref, o_ref):
        g = g_ref[...]
        o_ref[...] = (g * _sigmoid(g) * u_ref[...]).astype(BF16)

    return pl.pallas_call(body, name="swiglu_fwd", grid=(NCHIP, S // (2 * TR)), in_specs=[_ff_spec()] * 2, out_specs=_ff_spec(),
                          out_shape=jax.ShapeDtypeStruct((NCHIP, S, FF_CHUNK), BF16))(gate, up)


def _swiglu_bwd(gate, up, dact):
    def body(g_ref, u_ref, d_ref, dg_ref, du_ref):
        g = g_ref[...]
        sg = _sigmoid(g)
        d = d_ref[...]
        dg_ref[...] = (d * u_ref[...] * (sg * (1.0 + g * (1.0 - sg)))).astype(BF16)
        du_ref[...] = (d * g * sg).astype(BF16)

    return pl.pallas_call(body, name="swiglu_bwd", grid=(NCHIP, S // (2 * TR)), in_specs=[_ff_spec()] * 3, out_specs=[_ff_spec()] * 2,
                          out_shape=[jax.ShapeDtypeStruct((NCHIP, S, FF_CHUNK), BF16)] * 2)(gate, up, dact)


def _final_fwd_bwd(h1, ffn, pg, pp, target, g2, b2):
    def body(h_ref, f_ref, pg_ref, pp_ref, t_ref, g_ref, b_ref, dpre_ref, dpreb_ref, dpg_ref, dpp_ref, dg_ref, db_ref, loss_ref):
        sg = _sigmoid(pg_ref[...])
        ppv = pp_ref[...]
        pre = ALPHA * h_ref[...] + f_ref[...] + sg * ppv
        mu = jnp.mean(pre, axis=-1, keepdims=True)
        d = pre - mu
        rs = lax.rsqrt(jnp.mean(d * d, axis=-1, keepdims=True) + EPS_LN)
        xh = d * rs
        err = xh * g_ref[...] + b_ref[...] - t_ref[...]
        dy = err * (1.0 / D)
        gy = dy * g_ref[...]
        dpre = rs * (gy - jnp.mean(gy, axis=-1, keepdims=True) - xh * jnp.mean(gy * xh, axis=-1, keepdims=True))
        dpre_ref[...] = dpre
        dpreb_ref[...] = dpre.astype(BF16)
        dpg_ref[...] = (dpre * ppv * sg * (1.0 - sg)).astype(BF16)
        dpp_ref[...] = (dpre * sg).astype(BF16)
        _acc_rows(dg_ref, jnp.sum(dy * xh, axis=0, keepdims=True))
        _acc_rows(db_ref, jnp.sum(dy, axis=0, keepdims=True))
        _acc_rows(loss_ref, 0.5 * jnp.sum(jnp.mean(err * err, axis=-1, keepdims=True), axis=0, keepdims=True) * jnp.ones((1, LANE), F32))

    return pl.pallas_call(
        body, name="final_ln_loss", grid=(S // TR,),
        in_specs=[_row_spec(D)] * 5 + [_vec_spec(D)] * 2,
        out_specs=[_row_spec(D)] * 4 + [_vec_spec(D), _vec_spec(D), _vec_spec(LANE)],
        out_shape=[jax.ShapeDtypeStruct((S, D), F32)] + [jax.ShapeDtypeStruct((S, D), BF16)] * 3 + [
                   jax.ShapeDtypeStruct((1, D), F32), jax.ShapeDtypeStruct((1, D), F32), jax.ShapeDtypeStruct((1, LANE), F32)],
    )(h1, ffn, pg, pp, target, g2, b2)


def _rot(u, cos_t, sin_t, lane):
    partner = jnp.where(lane < NOPE + ROPE // 2, pltpu.roll(u, LANE - ROPE // 2, 1), pltpu.roll(u, ROPE // 2, 1))
    return u * cos_t + partner * sin_t


def _rms(v, w):
    r = lax.rsqrt(jnp.mean(v * v, axis=-1, keepdims=True) + EPS_RMS)
    return v * r * w, r


def _rms_grad(v, r, w, g):
    gw = g * w
    return r * gw - v * (r * r * r * jnp.mean(gw * v, axis=-1, keepdims=True)), jnp.sum(g * v * r, axis=0, keepdims=True)


def _whole(arr):
    return pl.BlockSpec(arr.shape, lambda i: (0,) * arr.ndim)


def _qkv_fwd(small, w_q, w_k, w_v, q_norm, kv_norm, cos_t, sin_t):
    def body(sm_ref, wq_ref, wk_ref, wv_ref, qw_ref, kw_ref, c_ref, s_ref, qn_ref, kvn_ref, q_ref, k_ref, v_ref):
        lane = lax.broadcasted_iota(jnp.int32, (TR, LANE), 1)
        c, s = c_ref[...], s_ref[...]
        qn = _rms(sm_ref[:, SM_Q:SM_Q + Q_RANK], qw_ref[...])[0].astype(BF16)
        kvn = _rms(sm_ref[:, SM_KV:SM_KV + KV_RANK], kw_ref[...])[0].astype(BF16)
        qn_ref[...] = qn
        kvn_ref[...] = kvn
        kr = _rot(pltpu.roll(sm_ref[:, SM_KR:SM_KR + LANE], NOPE, 1), c, s, lane)
        for h in range(H):
            tile = slice(h * LANE, (h + 1) * LANE)
            q_ref[:, tile] = _rot(_dot(qn, wq_ref[:, tile], ((1,), (0,))), c, s, lane).astype(BF16)
            k_ref[:, tile] = (_dot(kvn, wk_ref[:, tile], ((1,), (0,))) + kr).astype(BF16)
        v_ref[...] = _dot(kvn, wv_ref[...], ((1,), (0,))).astype(BF16)

    w = H * LANE
    return pl.pallas_call(
        body, name="qkv_fwd", grid=(S // TR,),
        in_specs=[_row_spec(SMALL_W), _whole(w_q), _whole(w_k), _whole(w_v), _vec_spec(Q_RANK), _vec_spec(KV_RANK), _row_spec(LANE), _row_spec(LANE)],
        out_specs=[_row_spec(Q_RANK), _row_spec(KV_RANK), _row_spec(w), _row_spec(w), _row_spec(H * VDIM)],
        out_shape=[jax.ShapeDtypeStruct((S, Q_RANK), BF16), jax.ShapeDtypeStruct((S, KV_RANK), BF16), jax.ShapeDtypeStruct((S, w), BF16),
                   jax.ShapeDtypeStruct((S, w), BF16), jax.ShapeDtypeStruct((S, H * VDIM), BF16)],
    )(small, w_q, w_k, w_v, q_norm, kv_norm, cos_t, sin_t)


def _qkv_bwd(dq, dk, dv, small, w_q, w_k, w_v, q_norm, kv_norm, cos_t, sin_t):
    def body(dq_ref, dk_ref, dv_ref, sm_ref, wq_ref, wk_ref, wv_ref, qw_ref, kw_ref, c_ref, s_ref,
             ds_ref, dql_ref, dkb_ref, dqw_ref, dkw_ref):
        lane = lax.broadcasted_iota(jnp.int32, (TR, LANE), 1)
        c, s = c_ref[...], -s_ref[...]
        dqn = jnp.zeros((TR, Q_RANK), F32)
        dkvn = _dot(dv_ref[...], wv_ref[...], ((1,), (1,)))
        dkr = jnp.zeros((TR, LANE), F32)
        for h in range(H):
            tile = slice(h * LANE, (h + 1) * LANE)
            dql = _rot(dq_ref[:, tile], c, s, lane).astype(BF16)
            dql_ref[:, tile] = dql
            dqn = dqn + _dot(dql, wq_ref[:, tile], ((1,), (1,)))
            dkt = dk_ref[:, tile]
            dkb_ref[:, tile] = dkt.astype(BF16)
            dkvn = dkvn + _dot(dkt, wk_ref[:, tile], ((1,), (1,)))
            dkr = dkr + dkt
        dkr = jnp.where((lane >= NOPE) & (lane < NOPE + ROPE), dkr, 0.0)
        q_c, kv_c = sm_ref[:, SM_Q:SM_Q + Q_RANK], sm_ref[:, SM_KV:SM_KV + KV_RANK]
        dq_c, dqw = _rms_grad(q_c, _rms(q_c, qw_ref[...])[1], qw_ref[...], dqn)
        dkv_c, dkw = _rms_grad(kv_c, _rms(kv_c, kw_ref[...])[1], kw_ref[...], dkvn)
        ds_ref[:, SM_Q:SM_Q + Q_RANK] = dq_c.astype(BF16)
        ds_ref[:, SM_KV:SM_KV + KV_RANK] = dkv_c.astype(BF16)
        ds_ref[:, SM_KR:SM_KR + LANE] = pltpu.roll(_rot(dkr, c, s, lane), LANE - NOPE, 1).astype(BF16)
        _acc_rows(dqw_ref, dqw)
        _acc_rows(dkw_ref, dkw)

    w = H * LANE
    return pl.pallas_call(
        body, name="qkv_bwd", grid=(S // TR,),
        in_specs=[_row_spec(w), _row_spec(w), _row_spec(H * VDIM), _row_spec(SMALL_W), _whole(w_q), _whole(w_k), _whole(w_v),
                  _vec_spec(Q_RANK), _vec_spec(KV_RANK), _row_spec(LANE), _row_spec(LANE)],
        out_specs=[_row_spec(SM_DT), _row_spec(w), _row_spec(w), _vec_spec(Q_RANK), _vec_spec(KV_RANK)],
        out_shape=[jax.ShapeDtypeStruct((S, SM_DT), BF16), jax.ShapeDtypeStruct((S, w), BF16), jax.ShapeDtypeStruct((S, w), BF16),
                   jax.ShapeDtypeStruct((1, Q_RANK), F32), jax.ShapeDtypeStruct((1, KV_RANK), F32)],
    )(dq, dk, dv, small, w_q, w_k, w_v, q_norm, kv_norm, cos_t, sin_t)


CB = 256


def _shift_down(u, k, row):
    if k == 0:
        return u
    return jnp.where(row >= k, pltpu.roll(u, k, 0), 0.0)


def _shift_up(u, k, row):
    if k == 0:
        return u
    return jnp.where(row < S - k, pltpu.roll(u, S - k, 0), 0.0)


def _conv_fwd(u, w, b):
    def body(u_ref, w_ref, b_ref, o_ref):
        row = lax.broadcasted_iota(jnp.int32, (S, CB), 0)
        uu = u_ref[...]
        acc = b_ref[...] + w_ref[SSD_K - 1:SSD_K, :] * uu
        for k in range(SSD_K - 1):
            acc = acc + w_ref[k:k + 1, :] * _shift_down(uu, SSD_K - 1 - k, row)
        o_ref[...] = acc * _sigmoid(acc)

    c = u.shape[1]
    return pl.pallas_call(
        body, name="conv_fwd", grid=(c // CB,),
        in_specs=[pl.BlockSpec((S, CB), lambda j: (0, j)), pl.BlockSpec((SSD_K, CB), lambda j: (0, j)), pl.BlockSpec((1, CB), lambda j: (0, j))],
        out_specs=pl.BlockSpec((S, CB), lambda j: (0, j)), out_shape=jax.ShapeDtypeStruct((S, c), F32),
    )(u, w, b)


def _conv_bwd(u, w, b, dact):
    def body(u_ref, w_ref, b_ref, d_ref, du_ref, dw_ref, db_ref):
        row = lax.broadcasted_iota(jnp.int32, (S, CB), 0)
        uu = u_ref[...]
        sh = [_shift_down(uu, SSD_K - 1 - k, row) for k in range(SSD_K)]
        acc = b_ref[...]
        for k in range(SSD_K):
            acc = acc + w_ref[k:k + 1, :] * sh[k]
        sg = _sigmoid(acc)
        dacc = d_ref[...] * (sg * (1.0 + acc * (1.0 - sg)))
        du = w_ref[SSD_K - 1:SSD_K, :] * dacc
        for k in range(SSD_K - 1):
            du = du + w_ref[k:k + 1, :] * _shift_up(dacc, SSD_K - 1 - k, row)
        du_ref[...] = du.astype(BF16)
        for k in range(SSD_K):
            dw_ref[k:k + 1, :] = jnp.sum(dacc * sh[k], axis=0, keepdims=True)
        db_ref[...] = jnp.sum(dacc, axis=0, keepdims=True)

    c = u.shape[1]
    col = lambda r: pl.BlockSpec((r, CB), lambda j: (0, j))
    return pl.pallas_call(
        body, name="conv_bwd", grid=(c // CB,), in_specs=[col(S), col(SSD_K), col(1), col(S)], out_specs=[col(S), col(SSD_K), col(1)],
        out_shape=[jax.ShapeDtypeStruct((S, c), BF16), jax.ShapeDtypeStruct((SSD_K, c), F32), jax.ShapeDtypeStruct((1, c), F32)],
    )(u, w, b, dact)


NPAIR = H // 2
PAIRS_PER_GROUP = NPAIR // SSD_G


def _softplus(v):
    return jnp.maximum(v, 0.0) + jnp.log(1.0 + jnp.exp(-jnp.abs(v)))


def _dot(a, b, dims):
    return lax.dot_general(a.astype(BF16), b.astype(BF16), (dims, ((), ())), preferred_element_type=F32)


def _dot3(a, b, dims, split_lhs):
    v = a if split_lhs else b
    v1 = v.astype(BF16)
    r1 = v - v1.astype(F32)
    v2 = r1.astype(BF16)
    v3 = (r1 - v2.astype(F32)).astype(BF16)
    acc = None
    for part in (v1, v2, v3):
        lhs, rhs = (part, b) if split_lhs else (a, part)
        t = lax.dot_general(lhs, rhs, (dims, ((), ())), preferred_element_type=F32)
        acc = t if acc is None else acc + t
    return acc


def _ssd_chunk_common(dt_ref, dtT_ref, prow_ref, pcol_ref):
    prow = prow_ref[...]
    pcol = pcol_ref[...]
    ri = lax.broadcasted_iota(jnp.int32, (SSD_L, SSD_L), 0)
    ci = lax.broadcasted_iota(jnp.int32, (SSD_L, SSD_L), 1)
    causal = ri >= ci
    pre_c = dt_ref[...] + prow[0:1, :]
    dtc = _softplus(pre_c)
    a_row = -jnp.exp(prow[1:2, :])
    cs_col = _dot3(causal.astype(BF16), dtc * a_row, ((1,), (0,)), False)
    dtr = _softplus(dtT_ref[...] + pcol[:, 0:1])
    a_col = -jnp.exp(pcol[:, 1:2])
    cs_row = _dot3(dtr * a_col, (ri <= ci).astype(BF16), ((1,), (0,)), True)
    return prow, causal, pre_c, dtc, a_row, cs_col, cs_row


def _ssd_fwd(act, small, dtT, prow, pcol):
    def body(x_ref, b_ref, c_ref, dt_ref, dtT_ref, prow_ref, pcol_ref, y_ref, st_ref, state):
        @pl.when(pl.program_id(0) == 0)
        def _():
            state[...] = jnp.zeros_like(state)

        prow, causal, _, dtc, _, cs_col, cs_row = _ssd_chunk_common(dt_ref, dtT_ref, prow_ref, pcol_ref)
        lo = lax.broadcasted_iota(jnp.int32, (SSD_L, LANE), 1) < SSD_P
        lo1 = lo[0:1, :]
        for g in range(SSD_G):
            bm = b_ref[:, g * SSD_N:(g + 1) * SSD_N]
            cm = c_ref[:, g * SSD_N:(g + 1) * SSD_N]
            cb = _dot(cm, bm, ((1,), (1,)))
            for qq in range(PAIRS_PER_GROUP):
                q = g * PAIRS_PER_GROUP + qq
                ha, hb = 2 * q, 2 * q + 1
                csa, csb = cs_col[:, ha:ha + 1], cs_col[:, hb:hb + 1]
                xp = x_ref[:, q * LANE:(q + 1) * LANE]
                xx = xp * jnp.where(lo, dtc[:, ha:ha + 1], dtc[:, hb:hb + 1])
                ga = cb * jnp.exp(jnp.where(causal, csa - cs_row[ha:ha + 1, :], NEG))
                gb = cb * jnp.exp(jnp.where(causal, csb - cs_row[hb:hb + 1, :], NEG))
                y = _dot(ga, jnp.where(lo, xx, 0.0), ((1,), (0,))) + _dot(gb, jnp.where(lo, 0.0, xx), ((1,), (0,)))
                s_in = state[q]
                y = y + _dot(cm, s_in, ((1,), (0,))) * jnp.where(lo, jnp.exp(csa), jnp.exp(csb))
                y = y + jnp.where(lo1, prow[2:3, ha:ha + 1], prow[2:3, hb:hb + 1]) * xp
                y_ref[:, q * LANE:(q + 1) * LANE] = y
                la, lb = csa[SSD_L - 1:SSD_L, :], csb[SSD_L - 1:SSD_L, :]
                decay = jnp.where(lo, jnp.exp(la - csa), jnp.exp(lb - csb))
                st_ref[q] = s_in
                state[q] = s_in * jnp.where(lo1, jnp.exp(la), jnp.exp(lb)) + _dot(bm, xx * decay, ((0,), (0,)))

    L = SSD_L
    return pl.pallas_call(
        body, name="ssd_fwd", grid=(SSD_NC,),
        in_specs=[pl.BlockSpec((L, SSD_INNER), lambda c: (c, 0)),
                  pl.BlockSpec((L, SSD_G * SSD_N), lambda c: (c, SSD_INNER // (SSD_G * SSD_N))),
                  pl.BlockSpec((L, SSD_G * SSD_N), lambda c: (c, SSD_INNER // (SSD_G * SSD_N) + 1)),
                  pl.BlockSpec((L, LANE), lambda c: (c, SM_DT // LANE)),
                  pl.BlockSpec((LANE, L), lambda c: (0, c)),
                  pl.BlockSpec((8, LANE), lambda c: (0, 0)), pl.BlockSpec((LANE, 8), lambda c: (0, 0))],
        out_specs=[pl.BlockSpec((L, SSD_INNER), lambda c: (c, 0)),
                   pl.BlockSpec((None, NPAIR, SSD_N, LANE), lambda c: (c, 0, 0, 0))],
        out_shape=[jax.ShapeDtypeStruct((S, SSD_INNER), F32), jax.ShapeDtypeStruct((SSD_NC, NPAIR, SSD_N, LANE), F32)],
        scratch_shapes=[pltpu.VMEM((NPAIR, SSD_N, LANE), F32)],
        compiler_params=pltpu.CompilerParams(dimension_semantics=("arbitrary",)),
    )(act, act, act, small, dtT, prow, pcol)


def _ssd_bwd(act, small, dtT, prow, pcol, states, dy):
    def body(x_ref, b_ref, c_ref, dt_ref, dtT_ref, prow_ref, pcol_ref, st_ref, dy_ref,
             dx_ref, ddt_ref, dp_ref, dstate):
        @pl.when(pl.program_id(0) == 0)
        def _():
            dstate[...] = jnp.zeros_like(dstate)
            dp_ref[...] = jnp.zeros_like(dp_ref)

        prow, causal, pre_c, dtc, a_row, cs_col, cs_row = _ssd_chunk_common(dt_ref, dtT_ref, prow_ref, pcol_ref)
        lane = lax.broadcasted_iota(jnp.int32, (SSD_L, LANE), 1)
        sub = lax.broadcasted_iota(jnp.int32, (LANE, SSD_L), 0)
        rowi = lax.broadcasted_iota(jnp.int32, (SSD_L, 1), 0)
        lane1 = lane[0:1, :]
        lo = lane < SSD_P
        lo1 = lo[0:1, :]
        dcs_c = jnp.zeros((SSD_L, LANE), F32)
        dcs_r = jnp.zeros((LANE, SSD_L), F32)
        ddt_x = jnp.zeros((SSD_L, LANE), F32)
        dd_row = jnp.zeros((1, LANE), F32)
        for g in range(SSD_G):
            bm = b_ref[:, g * SSD_N:(g + 1) * SSD_N]
            cm = c_ref[:, g * SSD_N:(g + 1) * SSD_N]
            cb = _dot(cm, bm, ((1,), (1,)))
            dcb = jnp.zeros((SSD_L, SSD_L), F32)
            dbm = jnp.zeros((SSD_L, SSD_N), F32)
            dcm = jnp.zeros((SSD_L, SSD_N), F32)
            for qq in range(PAIRS_PER_GROUP):
                q = g * PAIRS_PER_GROUP + qq
                ha, hb = 2 * q, 2 * q + 1
                csa, csb = cs_col[:, ha:ha + 1], cs_col[:, hb:hb + 1]
                xp = x_ref[:, q * LANE:(q + 1) * LANE]
                dtp = jnp.where(lo, dtc[:, ha:ha + 1], dtc[:, hb:hb + 1])
                xx = xp * dtp
                lma = jnp.exp(jnp.where(causal, csa - cs_row[ha:ha + 1, :], NEG))
                lmb = jnp.exp(jnp.where(causal, csb - cs_row[hb:hb + 1, :], NEG))
                ga, gb = cb * lma, cb * lmb
                dyp = dy_ref[:, q * LANE:(q + 1) * LANE]
                dya, dyb = jnp.where(lo, dyp, 0.0), jnp.where(lo, 0.0, dyp)
                s_in = st_ref[q]
                ds_out = dstate[q]
                la, lb = csa[SSD_L - 1:SSD_L, :], csb[SSD_L - 1:SSD_L, :]
                ecs = jnp.where(lo, jnp.exp(csa), jnp.exp(csb))
                decay = jnp.where(lo, jnp.exp(la - csa), jnp.exp(lb - csb))
                cd = jnp.where(lo1, jnp.exp(la), jnp.exp(lb))
                bds = _dot(bm, ds_out, ((1,), (0,)))
                dxx = _dot(ga, dya, ((0,), (0,))) + _dot(gb, dyb, ((0,), (0,))) + bds * decay
                dga = _dot(dya, xx, ((1,), (1,)))
                dgb = _dot(dyb, xx, ((1,), (1,)))
                dsega, dsegb = dga * ga, dgb * gb
                dcb = dcb + dga * lma + dgb * lmb
                yoff = _dot(cm, s_in, ((1,), (0,))) * ecs
                dye = dyp * ecs
                dcm = dcm + _dot(dye, s_in, ((1,), (1,)))
                xd = xx * decay
                dbm = dbm + _dot(xd, ds_out, ((1,), (1,)))
                wv = xd * bds
                t1 = dyp * yoff - wv
                col_a = (jnp.sum(dsega, axis=1, keepdims=True) + jnp.sum(jnp.where(lo, t1, 0.0), axis=1, keepdims=True))
                col_b = (jnp.sum(dsegb, axis=1, keepdims=True) + jnp.sum(jnp.where(lo, 0.0, t1), axis=1, keepdims=True))
                sprod = ds_out * s_in
                end_a = jnp.sum(jnp.where(lo, wv, 0.0), keepdims=True) + jnp.exp(la) * jnp.sum(jnp.where(lo[:SSD_N], sprod, 0.0), keepdims=True)
                end_b = jnp.sum(jnp.where(lo, 0.0, wv), keepdims=True) + jnp.exp(lb) * jnp.sum(jnp.where(lo[:SSD_N], 0.0, sprod), keepdims=True)
                col_a = col_a + jnp.where(rowi == SSD_L - 1, end_a, 0.0)
                col_b = col_b + jnp.where(rowi == SSD_L - 1, end_b, 0.0)
                dcs_c = dcs_c + jnp.where(lane == ha, col_a, 0.0) + jnp.where(lane == hb, col_b, 0.0)
                dcs_r = (dcs_r + jnp.where(sub == ha, jnp.sum(dsega, axis=0, keepdims=True), 0.0)
                         + jnp.where(sub == hb, jnp.sum(dsegb, axis=0, keepdims=True), 0.0))
                dstate[q] = _dot(cm, dye, ((0,), (0,))) + cd * ds_out
                dpair = jnp.where(lo1, prow[2:3, ha:ha + 1], prow[2:3, hb:hb + 1])
                dx_ref[:, q * LANE:(q + 1) * LANE] = dxx * dtp + dpair * dyp
                t2 = dxx * xp
                ddt_x = (ddt_x + jnp.where(lane == ha, jnp.sum(jnp.where(lo, t2, 0.0), axis=1, keepdims=True), 0.0)
                         + jnp.where(lane == hb, jnp.sum(jnp.where(lo, 0.0, t2), axis=1, keepdims=True), 0.0))
                t3 = dyp * xp
                dd_row = (dd_row + jnp.where(lane1 == ha, jnp.sum(jnp.where(lo, t3, 0.0), keepdims=True), 0.0)
                          + jnp.where(lane1 == hb, jnp.sum(jnp.where(lo, 0.0, t3), keepdims=True), 0.0))
            dx_ref[:, SSD_INNER + g * SSD_N:SSD_INNER + (g + 1) * SSD_N] = dbm + _dot(dcb, cm, ((0,), (0,)))
            dx_ref[:, SSD_INNER + (SSD_G + g) * SSD_N:SSD_INNER + (SSD_G + g + 1) * SSD_N] = dcm + _dot(dcb, bm, ((1,), (0,)))
        ri = lax.broadcasted_iota(jnp.int32, (SSD_L, SSD_L), 0)
        ci = lax.broadcasted_iota(jnp.int32, (SSD_L, SSD_L), 1)
        da = _dot3((ri <= ci).astype(BF16), dcs_c, ((1,), (0,)), False)
        da = da - _dot3(dcs_r, causal.astype(BF16), ((1,), (0,)), True).T
        ddt = ddt_x + da * a_row
        ddt_raw = ddt * _sigmoid(pre_c)
        ddt_ref[...] = ddt_raw
        da_head = jnp.sum(da * dtc, axis=0, keepdims=True) * a_row
        dp_ref[0:1, :] += jnp.sum(ddt_raw, axis=0, keepdims=True)
        dp_ref[1:2, :] += da_head
        dp_ref[2:3, :] += dd_row

    L = SSD_L
    rev = SSD_NC - 1
    bc_cols = SSD_INNER // (SSD_G * SSD_N)
    return pl.pallas_call(
        body, name="ssd_bwd", grid=(SSD_NC,),
        in_specs=[pl.BlockSpec((L, SSD_INNER), lambda c: (rev - c, 0)),
                  pl.BlockSpec((L, SSD_G * SSD_N), lambda c: (rev - c, bc_cols)),
                  pl.BlockSpec((L, SSD_G * SSD_N), lambda c: (rev - c, bc_cols + 1)),
                  pl.BlockSpec((L, LANE), lambda c: (rev - c, SM_DT // LANE)),
                  pl.BlockSpec((LANE, L), lambda c: (0, rev - c)),
                  pl.BlockSpec((8, LANE), lambda c: (0, 0)), pl.BlockSpec((LANE, 8), lambda c: (0, 0)),
                  pl.BlockSpec((None, NPAIR, SSD_N, LANE), lambda c: (rev - c, 0, 0, 0)),
                  pl.BlockSpec((L, SSD_INNER), lambda c: (rev - c, 0))],
        out_specs=[pl.BlockSpec((L, SSD_XBC), lambda c: (rev - c, 0)),
                   pl.BlockSpec((L, LANE), lambda c: (rev - c, 0)),
                   pl.BlockSpec((8, LANE), lambda c: (0, 0))],
        out_shape=[jax.ShapeDtypeStruct((S, SSD_XBC), F32), jax.ShapeDtypeStruct((S, LANE), F32),
                   jax.ShapeDtypeStruct((8, LANE), F32)],
        scratch_shapes=[pltpu.VMEM((NPAIR, SSD_N, LANE), F32)],
        compiler_params=pltpu.CompilerParams(dimension_semantics=("arbitrary",)),
    )(act, act, act, small, dtT, prow, pcol, states, dy)


TQ = 256
TK = 256
FWD_TQ = 256
FWD_TK = 256


def _attn_fwd(qc, kc, v):
    TQ, TK = FWD_TQ, FWD_TK

    def body(q_ref, k_ref, v_ref, o_ref, lse_ref):
        i = pl.program_id(1)
        lo = lax.broadcasted_iota(jnp.int32, (TQ, LANE), 1) < VDIM
        lo_k = lax.broadcasted_iota(jnp.int32, (TK, LANE), 1) < VDIM
        row_minus_col = lax.broadcasted_iota(jnp.int32, (TQ, TK), 0) - lax.broadcasted_iota(jnp.int32, (TQ, TK), 1)
        qa, qb = q_ref[:, 0:LANE], q_ref[:, LANE:2 * LANE]

        def scores(kb):
            kk = k_ref[pl.ds(pl.multiple_of(kb * TK, TK), TK), :]
            return (_dot(qa, kk[:, 0:LANE], ((1,), (1,))) * ATT_SCALE_LOG2, _dot(qb, kk[:, LANE:2 * LANE], ((1,), (1,))) * ATT_SCALE_LOG2)

        def update(kb, sa, sb, stats):
            ma, la, mb, lb, acc = stats
            vv = v_ref[pl.ds(pl.multiple_of(kb * TK, TK), TK), :]
            na = jnp.maximum(ma, jnp.max(sa, axis=1, keepdims=True))
            nb = jnp.maximum(mb, jnp.max(sb, axis=1, keepdims=True))
            pa, pb = jnp.exp2(sa - na), jnp.exp2(sb - nb)
            fa, fb = jnp.exp2(ma - na), jnp.exp2(mb - nb)
            la = fa * la + jnp.sum(pa, axis=1, keepdims=True)
            lb = fb * lb + jnp.sum(pb, axis=1, keepdims=True)
            acc = (acc * jnp.where(lo, fa, fb) + _dot(pa, jnp.where(lo_k, vv, 0), ((1,), (0,)))
                   + _dot(pb, jnp.where(lo_k, 0, vv), ((1,), (0,))))
            return na, la, nb, lb, acc

        def step(kb, carry):
            sa, sb = carry[:2]
            nxt = scores(kb + 1)
            return nxt + update(kb, sa, sb, carry[2:])

        neg = jnp.full((TQ, 1), NEG, F32)
        zero = jnp.zeros((TQ, 1), F32)
        n_full = i * (TQ // TK)
        carry = lax.fori_loop(0, n_full, step, scores(0) + (neg, zero, neg, zero, jnp.zeros((TQ, LANE), F32)))
        s, stats = carry[:2], carry[2:]
        for d in range(TQ // TK):
            nxt = scores(n_full + d + 1) if d + 1 < TQ // TK else None
            sa, sb = (jnp.where(row_minus_col >= d * TK, t, NEG) for t in s)
            stats = update(n_full + d, sa, sb, stats)
            s = nxt
        ma, la, mb, lb, acc = stats
        o_ref[...] = acc / jnp.where(lo, la, lb)
        lse_ref[...] = jnp.where(lo, ma + jnp.log2(la), mb + jnp.log2(lb)) * LN2

    return pl.pallas_call(
        body, name="attn_fwd", grid=(NPAIR, S // TQ),
        in_specs=[pl.BlockSpec((TQ, 2 * LANE), lambda j, i: (i, j)), pl.BlockSpec((S, 2 * LANE), lambda j, i: (0, j)),
                  pl.BlockSpec((S, LANE), lambda j, i: (0, j))],
        out_specs=[pl.BlockSpec((TQ, LANE), lambda j, i: (i, j)), pl.BlockSpec((None, TQ, LANE), lambda j, i: (j, i, 0))],
        out_shape=[jax.ShapeDtypeStruct((S, H * VDIM), F32), jax.ShapeDtypeStruct((NPAIR, S, LANE), F32)],
        compiler_params=pltpu.CompilerParams(dimension_semantics=("parallel", "parallel")),
    )(qc, kc, v)


def _attn_bwd(qc, kc, v, o, lse, do):
    nq = S // TQ

    def body(q_ref, k_ref, v_ref, o_ref, lse_ref, do_ref, dq_ref, dk_ref, dv_ref):
        kb = pl.program_id(1)

        @pl.when(kb == 0)
        def _():
            dq_ref[...] = jnp.zeros_like(dq_ref)

        lo = lax.broadcasted_iota(jnp.int32, (TQ, LANE), 1) < VDIM
        r0 = lax.broadcasted_iota(jnp.int32, (TQ, TK), 0)
        ck = kb * TK + lax.broadcasted_iota(jnp.int32, (TQ, TK), 1)
        ka, kbb = k_ref[:, 0:LANE], k_ref[:, LANE:2 * LANE]
        vv = v_ref[...]

        def step(qi, carry):
            dka, dkb, dv = carry
            off = pl.multiple_of(qi * TQ, TQ)
            qq = q_ref[pl.ds(off, TQ), :]
            dd = do_ref[pl.ds(off, TQ), :]
            ls = lse_ref[pl.ds(off, TQ), :]
            t = dd * o_ref[pl.ds(off, TQ), :]
            mask = r0 + qi * TQ >= ck
            outs = []
            for x, (kx, lsx) in enumerate(((ka, ls[:, 0:1]), (kbb, ls[:, VDIM:VDIM + 1]))):
                sel = lo if x == 0 else jnp.logical_not(lo)
                qx = qq[:, x * LANE:(x + 1) * LANE]
                dox = jnp.where(sel, dd, 0.0)
                delta = jnp.sum(jnp.where(sel, t, 0.0), axis=1, keepdims=True)
                sc = jnp.where(mask, _dot(qx, kx, ((1,), (1,))) * ATT_SCALE, NEG)
                p = jnp.exp(sc - lsx)
                dp = _dot(dox, vv, ((1,), (1,)))
                ds = p * (dp - delta) * ATT_SCALE
                dv = dv + _dot(p, dox, ((0,), (0,)))
                outs.append(_dot(ds, qx, ((0,), (0,))))
                dq_ref[pl.ds(off, TQ), x * LANE:(x + 1) * LANE] += _dot(ds, kx, ((1,), (0,)))
            return dka + outs[0], dkb + outs[1], dv

        z = jnp.zeros((TK, LANE), F32)
        dka, dkb, dv = lax.fori_loop(kb, nq, step, (z, z, z))
        dk_ref[:, 0:LANE] = dka
        dk_ref[:, LANE:2 * LANE] = dkb
        dv_ref[...] = dv

    return pl.pallas_call(
        body, name="attn_bwd", grid=(NPAIR, S // TK),
        in_specs=[pl.BlockSpec((S, 2 * LANE), lambda j, k: (0, j)), pl.BlockSpec((TK, 2 * LANE), lambda j, k: (k, j)),
                  pl.BlockSpec((TK, LANE), lambda j, k: (k, j)), pl.BlockSpec((S, LANE), lambda j, k: (0, j)),
                  pl.BlockSpec((None, S, LANE), lambda j, k: (j, 0, 0)), pl.BlockSpec((S, LANE), lambda j, k: (0, j))],
        out_specs=[pl.BlockSpec((S, 2 * LANE), lambda j, k: (0, j)), pl.BlockSpec((TK, 2 * LANE), lambda j, k: (k, j)),
                   pl.BlockSpec((TK, LANE), lambda j, k: (k, j))],
        out_shape=[jax.ShapeDtypeStruct((S, H * LANE), F32), jax.ShapeDtypeStruct((S, H * LANE), F32),
                   jax.ShapeDtypeStruct((S, H * VDIM), F32)],
        compiler_params=pltpu.CompilerParams(dimension_semantics=("parallel", "arbitrary")),
    )(qc, kc, v, o, lse, do)


_IN_Z, _IN_XBC, _IN_DT, _IN_Q, _IN_KV, _IN_KR = 0, 1024, 2560, 2576, 2960, 3216


def _prep_in(w_in):
    dt = w_in.dtype
    w_small = jnp.concatenate(
        [w_in[:, _IN_Q:_IN_KV], w_in[:, _IN_KV:_IN_KR], w_in[:, _IN_KR:IN_WIDTH], jnp.zeros((D, LANE - ROPE), dt),
         w_in[:, _IN_DT:_IN_Q], jnp.zeros((D, LANE - H), dt)], axis=1)
    return w_in[:, _IN_Z:_IN_XBC], w_in[:, _IN_XBC:_IN_DT], w_small


def _prep_attn(w_qb, w_kvb):
    w_q = jnp.pad(w_qb.reshape(Q_RANK, H, NOPE + ROPE), ((0, 0), (0, 0), (0, LANE - NOPE - ROPE))).reshape(Q_RANK, H * LANE)
    kv3 = w_kvb.reshape(KV_RANK, H, NOPE + VDIM)
    w_k = jnp.pad(kv3[:, :, :NOPE], ((0, 0), (0, 0), (0, LANE - NOPE))).reshape(KV_RANK, H * LANE)
    w_v = kv3[:, :, NOPE:].reshape(KV_RANK, H * VDIM)
    return w_q, w_k, w_v


def _rope_tables(positions):
    inv_freq = 1.0 / (10000.0 ** (jnp.arange(0, ROPE, 2, dtype=F32) / ROPE))
    ang = positions.astype(F32).reshape(S, 1) * inv_freq
    cos, sin = jnp.cos(ang), jnp.sin(ang)
    cos_t = jnp.concatenate([jnp.ones((S, NOPE), F32), cos, cos, jnp.ones((S, LANE - NOPE - ROPE), F32)], axis=1)
    sin_t = jnp.concatenate([jnp.zeros((S, NOPE), F32), -sin, sin, jnp.zeros((S, LANE - NOPE - ROPE), F32)], axis=1)
    return cos_t, sin_t


def _local_step(x, p, positions, target, w_in, fetch, send, sp):
    w_z, w_xbc, w_small = _prep_in(_from_cols(w_in))
    cos_t, sin_t = _rope_tables(positions)
    prow = jnp.zeros((8, LANE), F32).at[0, :H].set(sp["dt_bias"][0]).at[1, :H].set(sp["A_log"][0]).at[2, :H].set(sp["D"][0])
    pcol = prow.T

    xb, pb = x.astype(BF16), p.astype(BF16)
    z = _mm([(xb, w_z)], name="proj_z")
    xbc = _mm([(xb, w_xbc)], name="proj_xbc")
    small = _mm([(xb, w_small)], name="proj_small")
    act = _conv_fwd(xbc, sp["conv_w"], sp["conv_b"])
    dt_t = small[:, SM_DT:SM_DT + LANE].T
    y, states = _ssd_fwd(act, small, dt_t, prow, pcol)
    y_ssd = _gate_norm_fwd(y, z, sp["ssd_norm"])
    gl = fetch("attn", y_ssd)
    w_q, w_k, w_v = _prep_attn(_from_cols(gl["w_qb"]), _from_cols(gl["w_kvb"]))
    qn, kvn, qcat, kcat, v = _qkv_fwd(small, w_q, w_k, w_v, sp["q_norm"], sp["kv_norm"], cos_t, sin_t)
    o, lse = _attn_fwd(qcat, kcat, v)
    y_mla = _rms_fwd(o, sp["out_norm"], name="out_norm_fwd")
    w_out = fetch("out", y_mla)["w_out"]
    w_out_s = w_out[:NCHIP // 2].reshape(SSD_INNER, D)
    w_out_m = w_out[NCHIP // 2:].reshape(SSD_INNER, D)
    mix = _mm([(y_ssd, w_out_s), (y_mla, w_out_m)], name="out_proj")
    h1, h1b = _ln_fwd(x, mix, sp["ln_mix_g"], sp["ln_mix_b"])
    gl = fetch("ffn", h1b)
    w_pg, w_pp = gl["w_pg"].reshape(D, D), _from_cols(gl["w_pp"])
    w_gate, w_up, w_down = gl["w_gate"], gl["w_up"], gl["w_down"]
    gate = _mm([(h1b, w_gate)], chunk="out", name="ffn_gate")
    up = _mm([(h1b, w_up)], chunk="out", name="ffn_up")
    actf = _swiglu_fwd(gate, up)
    ffn = _mm([(actf, w_down)], chunk="sum", name="ffn_down")
    pg = _mm([(h1b, w_pg)], name="ple_gate")
    pp = _mm([(pb, w_pp)], name="ple_proj")
    dpre2, dpre2b, dpg, dpp, dg2, db2, loss_row = _final_fwd_bwd(h1, ffn, pg, pp, target, sp["ln_ffn_g"], sp["ln_ffn_b"])

    g = {"ln_ffn_g": dg2, "ln_ffn_b": db2}
    g["w_pp"] = _to_cols(_mm([(pb, dpp)], ta=True, out_dtype=BF16, name="d_w_ple_proj"))
    g["w_pg"] = _mm([(h1b, dpg)], ta=True, out_dtype=BF16, name="d_w_ple_gate").reshape(NCHIP, D // NCHIP, D)
    g["w_down"] = _mm([(actf, dpre2b)], ta=True, chunk="out", out_dtype=BF16, name="d_w_down")
    dactf = _mm([(dpre2b, w_down)], tb=True, chunk="out", name="d_act")
    dgate, dup = _swiglu_bwd(gate, up, dactf)
    g["w_gate"] = _mm([(h1b, dgate)], ta=True, chunk="out", out_dtype=BF16, name="d_w_gate")
    g["w_up"] = _mm([(h1b, dup)], ta=True, chunk="out", out_dtype=BF16, name="d_w_up")
    sent = send("ffn", {name: g.pop(name) for name in dict(ASYNC_GROUPS)["ffn"]})
    dh1 = _mm([(dpg, w_pg)], tb=True, add=dpre2, add_scale=ALPHA, name="d_h1_ple")
    dh1 = _mm([(dgate, w_gate), (dup, w_up)], tb=True, chunk="sum", add=dh1, name="d_h1")
    dpre1, dpre1b, g["ln_mix_g"], g["ln_mix_b"] = _ln_bwd(x, mix, sp["ln_mix_g"] + sent, dh1)
    dy_ssd = _mm([(dpre1b, w_out_s)], tb=True, name="d_y_ssd")
    dy_mla = _mm([(dpre1b, w_out_m)], tb=True, name="d_y_mla")
    dw_out = jnp.concatenate([_mm([(y_ssd, dpre1b)], ta=True, out_dtype=BF16, name="d_w_out_s"),
                              _mm([(y_mla, dpre1b)], ta=True, out_dtype=BF16, name="d_w_out_m")], axis=0)
    sent = send("out", {"w_out": dw_out.reshape(NCHIP, 2 * SSD_INNER // NCHIP, D)})
    do, g["out_norm"] = _rms_bwd(o, sp["out_norm"] + sent, dy_mla, name="out_norm_bwd")
    dq, dk, dv = _attn_bwd(qcat, kcat, v, o, lse, do)
    dlatent, dqlin, dkb, g["q_norm"], g["kv_norm"] = _qkv_bwd(dq, dk, dv, small, w_q, w_k, w_v, sp["q_norm"], sp["kv_norm"], cos_t, sin_t)
    dw_q = _mm([(qn, dqlin)], ta=True, out_dtype=BF16, name="d_w_q")
    dw_k = _mm([(kvn, dkb)], ta=True, out_dtype=BF16, name="d_w_k")
    dw_v = _mm([(kvn, dv)], ta=True, out_dtype=BF16, name="d_w_v")
    dw_qb = _to_cols(dw_q.reshape(Q_RANK, H, LANE)[:, :, :NOPE + ROPE].reshape(Q_RANK, H * (NOPE + ROPE)))
    dw_kvb = _to_cols(jnp.concatenate([dw_k.reshape(KV_RANK, H, LANE)[:, :, :NOPE], dw_v.reshape(KV_RANK, H, VDIM)],
                                       axis=2).reshape(KV_RANK, H * (NOPE + VDIM)))
    sent = send("attn", {"w_qb": dw_qb, "w_kvb": dw_kvb})
    dy, dz, g["ssd_norm"] = _gate_norm_bwd(y, z, sp["ssd_norm"] + sent, dy_ssd)
    dact, ddt, dprow = _ssd_bwd(act, small, dt_t, prow, pcol, states, dy)
    g["dt_bias"], g["A_log"], g["D"] = dprow[0:1, :H], dprow[1:2, :H], dprow[2:3, :H]
    dxbc, g["conv_w"], g["conv_b"] = _conv_bwd(xbc, sp["conv_w"], sp["conv_b"], dact)
    dsmall = jnp.concatenate([dlatent, ddt.astype(BF16)], axis=1)
    grad_x = _mm([(dz, w_z), (dxbc, w_xbc), (dsmall, w_small)], tb=True, add=dpre1, add_scale=ALPHA, name="d_x")
    dw_small = _mm([(xb, dsmall)], ta=True, out_dtype=BF16, name="d_w_small")
    dw_in = _to_cols(jnp.concatenate(
        [_mm([(xb, dz)], ta=True, out_dtype=BF16, name="d_w_z"), _mm([(xb, dxbc)], ta=True, out_dtype=BF16, name="d_w_xbc"),
         dw_small[:, SM_DT:SM_DT + H], dw_small[:, SM_Q:SM_Q + Q_RANK], dw_small[:, SM_KV:SM_KV + KV_RANK],
         dw_small[:, SM_KR:SM_KR + ROPE]], axis=1))
    return loss_row, grad_x, dw_in, g


MESH = pl.DeviceIdType.MESH
BIG = (("w_in", (D, IN_WIDTH), 1), ("w_qb", (Q_RANK, H * (NOPE + ROPE)), 1), ("w_kvb", (KV_RANK, H * (NOPE + VDIM)), 1),
       ("w_out", (2 * SSD_INNER, D), 0), ("w_gate", (D, D_FF), 1), ("w_up", (D, D_FF), 1), ("w_down", (D_FF, D), 0),
       ("w_pg", (D, D), 0), ("w_pp", (PLE, D), 1))
CONV_SHARD = SSD_XBC // NCHIP
BF16_ROWS = 16


def _from_cols(stack):
    return jnp.concatenate([stack[k] for k in range(NCHIP)], axis=1)


def _to_cols(full):
    r, c4 = full.shape
    return full.reshape(r, NCHIP, c4 // NCHIP).transpose(1, 0, 2)


def _coords():
    return lax.axis_index("x"), lax.axis_index("y"), lax.axis_index("c")


def _peers():
    x, y, c = _coords()
    return 2 * x + y, c, [(1 - x, y), (x, 1 - y), (1 - x, 1 - y)], (x, y, 1 - c)


def _half(c, rows):
    return pl.ds(pl.multiple_of(c * (rows // 2), BF16_ROWS), rows // 2)


def _gather_weights(shards):
    n_arr = len(shards)
    split = [s.shape[0] % (2 * BF16_ROWS) == 0 for s in shards]
    per = 2 * (NCHIP - 1)

    def body(*refs):
        ins, outs = refs[:n_arr], refs[n_arr:2 * n_arr]
        send_sems, recv_sems, local_sems = refs[2 * n_arr:]
        k, c, chips, sibling = _peers()

        def copy(idx, src, dst, to):
            return pltpu.make_async_remote_copy(src_ref=src, dst_ref=dst, send_sem=send_sems.at[idx], recv_sem=recv_sems.at[idx],
                                                device_id=to, device_id_type=MESH)

        def part(a, chip, core):
            return outs[a].at[chip, _half(core, shards[a].shape[0])] if split[a] else outs[a].at[chip]

        mine = [pltpu.make_async_copy(ins[a], outs[a].at[k], local_sems.at[a]) for a in range(n_arr)]
        for cp in mine:
            cp.start()
        sends = []
        for a in range(n_arr):
            src = ins[a].at[_half(c, shards[a].shape[0])] if split[a] else ins[a]
            for j, (cx, cy) in enumerate(chips):
                sends.append(copy(per * a + j, src, part(a, k, c), (cx, cy, c)))
                sends[-1].start()
        for j, (cx, cy) in enumerate(chips):
            for a in range(n_arr):
                landed = part(a, 2 * cx + cy, c)
                copy(per * a + j, landed, landed, (cx, cy, c)).wait_recv()
                if split[a]:
                    sends.append(copy(per * a + NCHIP - 1 + j, landed, landed, sibling))
                    sends[-1].start()
        for j, (cx, cy) in enumerate(chips):
            for a in range(n_arr):
                if split[a]:
                    other = part(a, 2 * cx + cy, 1 - c)
                    copy(per * a + NCHIP - 1 + j, other, other, sibling).wait_recv()
        for cp in sends:
            cp.wait_send()
        for cp in mine:
            cp.wait()

    any_spec = pl.BlockSpec(memory_space=pl.ANY)
    return pl.pallas_call(
        body, name="gather_weights", in_specs=[any_spec] * n_arr, out_specs=[any_spec] * n_arr,
        out_shape=[jax.ShapeDtypeStruct((NCHIP,) + s.shape, s.dtype) for s in shards],
        scratch_shapes=[pltpu.SemaphoreType.DMA((per * n_arr,)), pltpu.SemaphoreType.DMA((per * n_arr,)),
                        pltpu.SemaphoreType.DMA((n_arr,))],
    )(*shards)


ASYNC_GROUPS = (("attn", ("w_qb", "w_kvb")), ("out", ("w_out",)), ("ffn", ("w_gate", "w_up", "w_down", "w_pg", "w_pp")))
HBM_SPEC = pl.BlockSpec(memory_space=pltpu.HBM)
SEM_SPEC = pl.BlockSpec(memory_space=pltpu.SEMAPHORE)
IN_FLIGHT = pltpu.SideEffectType.DATAFLOW_SIDE_EFFECTING


def _in_hbm(a):
    return pltpu.with_memory_space_constraint(a, pltpu.HBM)


def _hbm_like(arrs, lead=()):
    return [pltpu.HBM(lead + a.shape, a.dtype) for a in arrs]


def _split_start(name, srcs, lands, after, n_sem, start):
    n = len(srcs)

    def body(*refs):
        src_refs, land_refs = refs[:n], refs[n:2 * n]
        send_sems, recv_sems = refs[2 * n + 1], refs[2 * n + 2]
        token = refs[-1]

        def copy(send_idx, recv_idx, src, dst, to):
            return pltpu.make_async_remote_copy(src_ref=src, dst_ref=dst, send_sem=send_sems.at[send_idx],
                                                recv_sem=recv_sems.at[recv_idx], device_id=to, device_id_type=MESH)

        for cp in start(src_refs, land_refs, copy):
            cp.start()
        token[...] = jnp.zeros_like(token)

    sem = pltpu.SemaphoreType.DMA((n_sem,))
    outs = pl.pallas_call(
        body, name=name, in_specs=[HBM_SPEC] * (2 * n) + [pl.BlockSpec(memory_space=pl.ANY)],
        out_specs=[SEM_SPEC, SEM_SPEC] + [HBM_SPEC] * (2 * n) + [pl.BlockSpec(memory_space=pltpu.VMEM)],
        out_shape=[sem, sem] + _hbm_like(srcs) + _hbm_like(lands) + [jax.ShapeDtypeStruct((8, LANE), F32)],
        input_output_aliases={i: 2 + i for i in range(2 * n)},
        compiler_params=pltpu.CompilerParams(has_side_effects=IN_FLIGHT),
    )(*[_in_hbm(a) for a in srcs], *[_in_hbm(a) for a in lands], after)
    return (outs[0], outs[1], outs[2:2 + n], outs[2 + n:2 + 2 * n]), outs[-1]


def _split_wait(name, send_sems, recv_sems, srcs, lands, after, waits):
    n = len(srcs)

    def body(*refs):
        src_refs, land_refs = refs[:n], refs[n:2 * n]
        send_ref, recv_ref = refs[2 * n], refs[2 * n + 1]

        def copy(send_idx, recv_idx, src, dst, to):
            return pltpu.make_async_remote_copy(src_ref=src, dst_ref=dst, send_sem=send_ref.at[send_idx],
                                                recv_sem=recv_ref.at[recv_idx], device_id=to, device_id_type=MESH)

        for cp in waits(src_refs, land_refs, copy):
            cp.wait_send()
            cp.wait_recv()

    outs = pl.pallas_call(
        body, name=name, in_specs=[HBM_SPEC] * (2 * n) + [SEM_SPEC, SEM_SPEC, pl.BlockSpec(memory_space=pl.ANY)],
        out_specs=[HBM_SPEC] * (2 * n), out_shape=_hbm_like(srcs) + _hbm_like(lands),
        input_output_aliases={i: i for i in range(2 * n)},
        compiler_params=pltpu.CompilerParams(has_side_effects=IN_FLIGHT),
    )(*srcs, *lands, send_sems, recv_sems, after)
    return outs[:n], outs[n:]


GATHER_LATE_SEMS = 2 * (NCHIP - 1)


def _gather_async_start(tag, shards, after):
    def start(srcs, lands, copy):
        k, c, chips, _ = _peers()
        out = []
        for a, (src, dst) in enumerate(zip(srcs, lands)):
            rows = src.shape[0]
            for j, (cx, cy) in enumerate(chips):
                for core in range(2):
                    out.append(copy(GATHER_LATE_SEMS * a + 2 * j + core, GATHER_LATE_SEMS * a + 2 * j + c,
                                    src.at[_half(c, rows)], dst.at[k, _half(c, rows)], (cx, cy, core)))
        return out

    chip = 2 * lax.axis_index("x") + lax.axis_index("y")
    lands = [lax.dynamic_update_slice(lax.empty((NCHIP,) + s.shape, s.dtype), s[None], (chip, 0, 0)) for s in shards]
    return _split_start("gather_%s_start" % tag, shards, lands, after, GATHER_LATE_SEMS * len(shards), start)


def _gather_async_wait(tag, send_sems, recv_sems, shards, lands, after):
    def waits(srcs, lands_, copy):
        _, c, chips, _ = _peers()
        out = []
        for a, (src, dst) in enumerate(zip(srcs, lands_)):
            rows = src.shape[0]
            for j, (cx, cy) in enumerate(chips):
                for core in range(2):
                    idx = GATHER_LATE_SEMS * a + 2 * j + core
                    out.append(copy(idx, idx, src.at[_half(c, rows)], dst.at[2 * cx + cy, _half(core, rows)], (cx, cy, core)))
        return out

    return _split_wait("gather_%s_wait" % tag, send_sems, recv_sems, shards, lands, after, waits)[1]


def _other_devices():
    x, y, c = _coords()
    out = []
    for d in range(1, NDEV):
        tx, ty, tc = x ^ (d >> 2), y ^ ((d >> 1) & 1), c ^ (d & 1)
        out.append((d, (tx, ty, tc), 2 * tx + ty, 4 * tx + 2 * ty + tc))
    return out


def _reduce_async_start(tag, stacks, after):
    def start(srcs, lands, copy):
        x, y, c = _coords()
        me = 4 * x + 2 * y + c
        return [copy((NDEV - 1) * a + d - 1, (NDEV - 1) * a + d - 1, src.at[chip, _half(to[2], src.shape[1])], dst.at[me], to)
                for a, (src, dst) in enumerate(zip(srcs, lands)) for d, to, chip, _ in _other_devices()]

    x, y, c = _coords()
    lands = []
    for s in stacks:
        hr = s.shape[1] // 2
        own = lax.dynamic_slice(s, (2 * x + y, c * hr, 0), (1, hr, s.shape[2]))
        lands.append(lax.dynamic_update_slice(lax.empty((NDEV, hr, s.shape[2]), s.dtype), own, (4 * x + 2 * y + c, 0, 0)))
    return _split_start("reduce_%s_start" % tag, stacks, lands, after, (NDEV - 1) * len(stacks), start)


def _reduce_async_wait(tag, send_sems, recv_sems, stacks, lands, after):
    def waits(srcs, lands_, copy):
        return [copy((NDEV - 1) * a + d - 1, (NDEV - 1) * a + d - 1, src.at[chip, _half(to[2], src.shape[1])], dst.at[pos], to)
                for a, (src, dst) in enumerate(zip(srcs, lands_)) for d, to, chip, pos in _other_devices()]

    return _split_wait("reduce_%s_wait" % tag, send_sems, recv_sems, stacks, lands, after, waits)[1]


def _reduce_finish(tag, arrived):
    n_arr = len(arrived)
    dims = [(2 * p.shape[1], p.shape[2]) for p in arrived]

    def body(*refs):
        lands, fin = refs[:n_arr], refs[n_arr:2 * n_arr]
        send_sems, recv_sems = refs[2 * n_arr:]
        _, c, _, sibling = _peers()
        sends = []
        for a in range(n_arr):
            mine = fin[a].at[_half(c, dims[a][0])]

            def device_sum(vs, vf, a=a, mine=mine):
                pltpu.sync_copy(lands[a], vs)
                acc = vs[0].astype(F32)
                for i in range(1, NDEV):
                    acc = acc + vs[i].astype(F32)
                vf[...] = acc
                pltpu.sync_copy(vf, mine)

            pl.run_scoped(device_sum, pltpu.VMEM((NDEV, dims[a][0] // 2, dims[a][1]), BF16), pltpu.VMEM((dims[a][0] // 2, dims[a][1]), F32))
            sends.append(pltpu.make_async_remote_copy(src_ref=mine, dst_ref=mine, send_sem=send_sems.at[a], recv_sem=recv_sems.at[a],
                                                      device_id=sibling, device_id_type=MESH))
            sends[-1].start()
        for a in range(n_arr):
            other = fin[a].at[_half(1 - c, dims[a][0])]
            pltpu.make_async_remote_copy(src_ref=other, dst_ref=other, send_sem=send_sems.at[a], recv_sem=recv_sems.at[a],
                                         device_id=sibling, device_id_type=MESH).wait_recv()
        for cp in sends:
            cp.wait_send()

    any_spec = pl.BlockSpec(memory_space=pl.ANY)
    return pl.pallas_call(
        body, name="reduce_%s_finish" % tag, in_specs=[any_spec] * n_arr, out_specs=[any_spec] * n_arr,
        out_shape=[jax.ShapeDtypeStruct(d, F32) for d in dims],
        scratch_shapes=[pltpu.SemaphoreType.DMA((n_arr,)), pltpu.SemaphoreType.DMA((n_arr,))],
    )(*arrived)


SMALL = (("conv_w", SSD_K * SSD_XBC), ("conv_b", SSD_XBC), ("dt_bias", H), ("A_log", H), ("D", H), ("ssd_norm", SSD_INNER),
         ("q_norm", Q_RANK), ("kv_norm", KV_RANK), ("out_norm", SSD_INNER), ("ln_mix_g", D), ("ln_mix_b", D),
         ("ln_ffn_g", D), ("ln_ffn_b", D))
SMALL_ROWS = 120
NDEV = 8


def _allreduce_small(sv):
    def body(sv_ref, out_ref, slots, send_sems, recv_sems):
        x, y, c = _coords()
        me = 4 * x + 2 * y + c
        slots[me] = sv_ref[...]
        copies = []
        for d in range(1, NDEV):
            to = (x ^ (d >> 2), y ^ ((d >> 1) & 1), c ^ (d & 1))
            copies.append(pltpu.make_async_remote_copy(src_ref=sv_ref, dst_ref=slots.at[me], send_sem=send_sems.at[d - 1],
                                                       recv_sem=recv_sems.at[d - 1], device_id=to, device_id_type=MESH))
            copies[-1].start()
        for cp in copies:
            cp.wait_recv()
        for cp in copies:
            cp.wait_send()
        acc = slots[0]
        for i in range(1, NDEV):
            acc = acc + slots[i]
        out_ref[...] = acc

    vm = pl.BlockSpec(memory_space=pltpu.VMEM)
    return pl.pallas_call(
        body, name="allreduce_small", in_specs=[vm], out_specs=vm, out_shape=jax.ShapeDtypeStruct((SMALL_ROWS, LANE), F32),
        scratch_shapes=[pltpu.VMEM((NDEV, SMALL_ROWS, LANE), F32), pltpu.SemaphoreType.DMA((NDEV - 1,)),
                        pltpu.SemaphoreType.DMA((NDEV - 1,))],
    )(sv)


def _adamw_math(w, g, m, v):
    m2 = ADAM_B1 * m + (1.0 - ADAM_B1) * g
    v2 = ADAM_B2 * v + (1.0 - ADAM_B2) * (g * g)
    m_hat = m2 / (1.0 - ADAM_B1 ** ADAM_STEP)
    v_hat = v2 / (1.0 - ADAM_B2 ** ADAM_STEP)
    return -ADAM_LR * (m_hat / (jnp.sqrt(v_hat) + ADAM_EPS) + ADAM_WD * w), m2, v2


def _adamw_big(w, g, m, v, *, name):
    r, c = w.shape
    tr = next(t for t in (512, 384, 352, 256, 128, 64, 8) if r % t == 0)

    def body(w_ref, g_ref, m_ref, v_ref, d_ref, m2_ref, v2_ref):
        d_ref[...], m2_ref[...], v2_ref[...] = _adamw_math(w_ref[...], g_ref[...], m_ref[...], v_ref[...])

    spec = pl.BlockSpec((tr, c), lambda i: (i, 0))
    return pl.pallas_call(body, name=name, grid=(r // tr,), in_specs=[spec] * 4, out_specs=[spec] * 3,
                          out_shape=[jax.ShapeDtypeStruct((r, c), F32)] * 3)(w, g, m, v)


def _adamw_small(ws, gs, ms, vs):
    n = len(ws)

    def body(*refs):
        for i in range(n):
            w_ref, g_ref, m_ref, v_ref = (refs[j * n + i] for j in range(4))
            d_ref, m2_ref, v2_ref = (refs[(4 + j) * n + i] for j in range(3))
            d_ref[...], m2_ref[...], v2_ref[...] = _adamw_math(w_ref[...], g_ref[...], m_ref[...], v_ref[...])

    vm = pl.BlockSpec(memory_space=pltpu.VMEM)
    shapes = [jax.ShapeDtypeStruct(w.shape, F32) for w in ws]
    outs = pl.pallas_call(body, name="adamw_small", in_specs=[vm] * (4 * n), out_specs=[vm] * (3 * n), out_shape=shapes * 3)(
        *ws, *gs, *ms, *vs)
    return outs[:n], outs[n:2 * n], outs[2 * n:]


_SMALL_ARG = {"conv_w": "ssd_conv_w", "conv_b": "ssd_conv_b", "dt_bias": "ssd_dt_bias", "A_log": "ssd_A_log", "D": "ssd_D",
              "ssd_norm": "ssd_norm_w", "q_norm": "mla_q_norm_w", "kv_norm": "mla_kv_norm_w", "out_norm": "mla_out_norm_w",
              "ln_mix_g": "ln_mix_g", "ln_mix_b": "ln_mix_b", "ln_ffn_g": "ln_ffn_g", "ln_ffn_b": "ln_ffn_b"}
_BIG_ARG = {"w_in": "w_in", "w_qb": "mla_w_q_b", "w_kvb": "mla_w_kv_b", "w_out": "w_out", "w_gate": "w_ffn_gate",
            "w_up": "w_ffn_up", "w_down": "w_ffn_down", "w_pg": "w_ple_gate", "w_pp": "w_ple_proj"}
_WEIGHT_ORDER = ("w_in", "ssd_conv_w", "ssd_conv_b", "ssd_dt_bias", "ssd_A_log", "ssd_D", "ssd_norm_w", "mla_q_norm_w", "mla_w_q_b",
                 "mla_kv_norm_w", "mla_w_kv_b", "mla_out_norm_w", "w_out", "ln_mix_g", "ln_mix_b", "w_ffn_gate", "w_ffn_up",
                 "w_ffn_down", "w_ple_gate", "w_ple_proj", "ln_ffn_g", "ln_ffn_b")


def _rows128(a):
    flat = a.reshape(-1)
    return jnp.pad(flat, (0, -flat.shape[0] % LANE)).reshape(-1, LANE)


def kernel(x, p, positions, w_in, ssd_conv_w, ssd_conv_b, ssd_dt_bias, ssd_A_log, ssd_D, ssd_norm_w, mla_q_norm_w, mla_w_q_b, mla_kv_norm_w, mla_w_kv_b, mla_out_norm_w, w_out, ln_mix_g, ln_mix_b, w_ffn_gate, w_ffn_up, w_ffn_down, w_ple_gate, w_ple_proj, ln_ffn_g, ln_ffn_b, loss_target, m_w_in, m_ssd_conv_w, m_ssd_conv_b, m_ssd_dt_bias, m_ssd_A_log, m_ssd_D, m_ssd_norm_w, m_mla_q_norm_w, m_mla_w_q_b, m_mla_kv_norm_w, m_mla_w_kv_b, m_mla_out_norm_w, m_w_out, m_ln_mix_g, m_ln_mix_b, m_w_ffn_gate, m_w_ffn_up, m_w_ffn_down, m_w_ple_gate, m_w_ple_proj, m_ln_ffn_g, m_ln_ffn_b, v_w_in, v_ssd_conv_w, v_ssd_conv_b, v_ssd_dt_bias, v_ssd_A_log, v_ssd_D, v_ssd_norm_w, v_mla_q_norm_w, v_mla_w_q_b, v_mla_kv_norm_w, v_mla_w_kv_b, v_mla_out_norm_w, v_w_out, v_ln_mix_g, v_ln_mix_b, v_w_ffn_gate, v_w_ffn_up, v_w_ffn_down, v_w_ple_gate, v_w_ple_proj, v_ln_ffn_g, v_ln_ffn_b):
    given = dict(locals())
    chip = 2 * lax.axis_index("x") + lax.axis_index("y")

    conv_bits = lax.bitcast_convert_type(ssd_conv_w[0], BF16).reshape(SSD_K, 2 * CONV_SHARD)
    w_in_all, conv_all = _gather_weights([w_in[0].astype(BF16), jnp.pad(conv_bits, ((0, BF16_ROWS - SSD_K), (0, 0)))])
    sp = {k: given[a] for k, a in _SMALL_ARG.items() if k != "conv_w"}
    sp["conv_w"] = _from_cols(lax.bitcast_convert_type(conv_all[:, :SSD_K].reshape(NCHIP, SSD_K, CONV_SHARD, 2), F32))
    gathering, tie = {}, w_in_all
    for group, names in ASYNC_GROUPS:
        gathering[group], tie = _gather_async_start(group, [given[_BIG_ARG[name]][0].astype(BF16) for name in names], tie)

    def fetch(group, after):
        return dict(zip(dict(ASYNC_GROUPS)[group], _gather_async_wait(group, *gathering[group], after)))

    reducing = {}

    def send(group, grads):
        reducing[group], sent = _reduce_async_start(group, [grads[name] for name in dict(ASYNC_GROUPS)[group]], grads[dict(ASYNC_GROUPS)[group][0]])
        return sent[0, 0]

    loss_row, grad_x, dw_in, g = _local_step(x[0] + tie[0, 0], p[0, 0], positions[0], loss_target[0], w_in_all, fetch, send, sp)

    reducing["in"], tie = _reduce_async_start("in", [dw_in], grad_x)
    gbig = {}
    for group, names in reversed(ASYNC_GROUPS):
        gbig.update(zip(names, _reduce_finish(group, _reduce_async_wait(group, *reducing[group], tie))))
    small_in = jnp.concatenate([_rows128(g[name]) for name, _ in SMALL] + [loss_row], axis=0)
    small_sum = _allreduce_small(jnp.pad(small_in, ((0, SMALL_ROWS - small_in.shape[0]), (0, 0))))
    gsmall, row = {}, 0
    for name, size in SMALL:
        nrow = -(-size // LANE)
        gsmall[name] = small_sum[row:row + nrow].reshape(-1)[:size]
        row += nrow
    loss = small_sum[row, 0]

    grads = {_BIG_ARG[name]: arr[None] for name, arr in gbig.items()}
    for name, _ in SMALL:
        if name == "conv_w":
            full_g = gsmall[name].reshape(SSD_K, SSD_XBC)
            grads["ssd_conv_w"] = lax.dynamic_slice(full_g, (0, chip * CONV_SHARD), (SSD_K, CONV_SHARD))[None]
        else:
            grads[_SMALL_ARG[name]] = gsmall[name].reshape(given[_SMALL_ARG[name]].shape)

    delta, new_m, new_v = {}, {}, {}

    def update_matrix(name):
        a = _BIG_ARG[name]
        d, m2, v2 = _adamw_big(given[a][0], grads[a][0], given["m_" + a][0], given["v_" + a][0], name="adamw_" + a)
        delta[a], new_m[a], new_v[a] = d[None], m2[None], v2[None]
        return d

    for name in gbig:
        last = update_matrix(name)
    grads["w_in"] = _reduce_finish("in", _reduce_async_wait("in", *reducing["in"], last))[0][None]
    update_matrix("w_in")
    small_names = [_SMALL_ARG[name] for name, _ in SMALL]
    two_d = lambda t: t.reshape(t.shape[-2], t.shape[-1])
    ds, ms, vs = _adamw_small([two_d(given[a]) for a in small_names], [two_d(grads[a]) for a in small_names],
                              [two_d(given["m_" + a]) for a in small_names], [two_d(given["v_" + a]) for a in small_names])
    for a, d, m2, v2 in zip(small_names, ds, ms, vs):
        delta[a], new_m[a], new_v[a] = (t.reshape(given[a].shape) for t in (d, m2, v2))

    return (loss, grad_x[None], *[grads[n] for n in _WEIGHT_ORDER], *[delta[n] for n in _WEIGHT_ORDER],
            *[new_m[n] for n in _WEIGHT_ORDER], *[new_v[n] for n in _WEIGHT_ORDER])
```

```python
import functools
import math

import jax
import jax.numpy as jnp
from jax import lax
from jax.experimental import pallas as pl
from jax.experimental.pallas import tpu as pltpu

F32 = jnp.float32
BF16 = jnp.bfloat16

S = 2048
D = 1024
PLE = 256
H = 16
SSD_P = 64
SSD_INNER = 1024
SSD_N = 128
SSD_G = 2
SSD_L = 128
SSD_NC = S // SSD_L
SSD_XBC = 1536
SSD_K = 4
Q_RANK = 384
KV_RANK = 256
NOPE = 64
ROPE = 32
VDIM = 64
D_FF = 2816
IN_WIDTH = 3248
ALPHA = 2.0 ** 0.25
EPS_RMS = 1e-6
EPS_LN = 1e-5
ATT_SCALE = 1.0 / math.sqrt(NOPE + ROPE)
LN2 = math.log(2.0)
ATT_SCALE_LOG2 = ATT_SCALE / LN2
LANE = 128
NCHIP = 4
SMALL_W = 896
SM_Q, SM_KV, SM_KR, SM_DT = 0, 384, 640, 768
NEG = -1e30

ADAM_LR = 0.001
ADAM_B1 = 0.9
ADAM_B2 = 0.999
ADAM_EPS = 1e-08
ADAM_WD = 0.01
ADAM_STEP = 10


def _sigmoid(v):
    return 1.0 / (1.0 + jnp.exp(-v))


MM_VMEM_BUDGET = 36 * 2 ** 20
MM_MAX_ACC = 2048 * 1024


def _mm_tiles(pairs, ta, tb, m, n, out_dtype, has_add):
    def divs(v):
        return [LANE * d for d in range(v // LANE, 0, -1) if (v // LANE) % d == 0] if v % LANE == 0 else [v]

    def cost(tm, tn):
        tot = tm * tn * (jnp.dtype(out_dtype).itemsize + (4 if has_add else 0))
        for a, b in pairs:
            k = a.shape[-2] if ta else a.shape[-1]
            tot += k * (tm * a.dtype.itemsize + tn * b.dtype.itemsize)
        return 2 * tot

    ok = [(tm * tn, tm, tn) for tm in divs(m) for tn in divs(n) if tm * tn <= MM_MAX_ACC and cost(tm, tn) <= MM_VMEM_BUDGET]
    _, tm, tn = max(ok)
    return tm, tn


def _mm(pairs, *, ta=False, tb=False, out_dtype=F32, add=None, add_scale=1.0, chunk=None, name):
    n_pairs = len(pairs)
    a0, b0 = pairs[0]
    m = a0.shape[-1] if ta else a0.shape[-2]
    n = b0.shape[-2] if tb else b0.shape[-1]
    tm, tn = _mm_tiles(pairs, ta, tb, m, n, out_dtype, add is not None)
    dims = (((0 if ta else 1,), (1 if tb else 0,)), ((), ()))
    nk = NCHIP if chunk else 1
    assert chunk != "sum" or out_dtype == F32

    def body(*refs):
        o_ref = refs[-1]
        acc = None
        for i in range(n_pairs):
            a = refs[2 * i][...].astype(BF16)
            b = refs[2 * i + 1][...].astype(BF16)
            part = lax.dot_general(a, b, dims, preferred_element_type=F32)
            acc = part if acc is None else acc + part
        if chunk == "sum":
            k = pl.program_id(2)

            @pl.when(k == 0)
            def _():
                o_ref[...] = acc + add_scale * refs[2 * n_pairs][...] if add is not None else acc

            @pl.when(k > 0)
            def _():
                o_ref[...] += acc
        else:
            if add is not None:
                acc = acc + add_scale * refs[2 * n_pairs][...]
            o_ref[...] = acc.astype(out_dtype)

    def spec(arr, shape, idx2):
        if arr.ndim == 3:
            return pl.BlockSpec((None,) + shape, lambda i, j, k: (k,) + idx2(i, j))
        return pl.BlockSpec(shape, lambda i, j, k: idx2(i, j))

    in_specs, args = [], []
    for a, b in pairs:
        kdim = a.shape[-2] if ta else a.shape[-1]
        in_specs.append(spec(a, (kdim, tm), lambda i, j: (0, i)) if ta else spec(a, (tm, kdim), lambda i, j: (i, 0)))
        in_specs.append(spec(b, (tn, kdim), lambda i, j: (j, 0)) if tb else spec(b, (kdim, tn), lambda i, j: (0, j)))
        args += [a, b]
    if add is not None:
        in_specs.append(pl.BlockSpec((tm, tn), lambda i, j, k: (i, j)))
        args.append(add)
    if chunk == "out":
        out_spec = pl.BlockSpec((None, tm, tn), lambda i, j, k: (k, i, j))
        out_shape = jax.ShapeDtypeStruct((nk, m, n), out_dtype)
    else:
        out_spec = pl.BlockSpec((tm, tn), lambda i, j, k: (i, j))
        out_shape = jax.ShapeDtypeStruct((m, n), out_dtype)
    return pl.pallas_call(
        body, name=name, grid=(m // tm, n // tn, nk), in_specs=in_specs, out_specs=out_spec, out_shape=out_shape,
        compiler_params=pltpu.CompilerParams(dimension_semantics=("parallel", "parallel", "arbitrary")),
    )(*args)


TR = 256


def _row_spec(c):
    return pl.BlockSpec((TR, c), lambda i: (i, 0))


def _vec_spec(c):
    return pl.BlockSpec((1, c), lambda i: (0, 0))


def _acc_rows(ref, val):
    @pl.when(pl.program_id(0) == 0)
    def _():
        ref[...] = jnp.zeros_like(ref)
    ref[...] += val


def _rms_fwd(u, w, *, name):
    c = u.shape[1]

    def body(u_ref, w_ref, o_ref):
        v = u_ref[...]
        r = lax.rsqrt(jnp.mean(v * v, axis=-1, keepdims=True) + EPS_RMS)
        o_ref[...] = (v * r * w_ref[...]).astype(BF16)

    return pl.pallas_call(body, name=name, grid=(S // TR,), in_specs=[_row_spec(c), _vec_spec(c)], out_specs=_row_spec(c),
                          out_shape=jax.ShapeDtypeStruct((S, c), BF16))(u, w)


def _rms_bwd(u, w, dy, *, name):
    c = u.shape[1]

    def body(u_ref, w_ref, dy_ref, du_ref, dw_ref):
        v = u_ref[...]
        g = dy_ref[...].astype(F32)
        r = lax.rsqrt(jnp.mean(v * v, axis=-1, keepdims=True) + EPS_RMS)
        gw = g * w_ref[...]
        du_ref[...] = r * gw - v * (r * r * r * jnp.mean(gw * v, axis=-1, keepdims=True))
        _acc_rows(dw_ref, jnp.sum(g * v * r, axis=0, keepdims=True))

    return pl.pallas_call(body, name=name, grid=(S // TR,), in_specs=[_row_spec(c), _vec_spec(c), _row_spec(c)],
                          out_specs=[_row_spec(c), _vec_spec(c)],
                          out_shape=[jax.ShapeDtypeStruct((S, c), F32), jax.ShapeDtypeStruct((1, c), F32)])(u, w, dy)


def _gate_norm_fwd(y, z, w):
    def body(y_ref, z_ref, w_ref, o_ref):
        zz = z_ref[...]
        v = y_ref[...] * (zz * _sigmoid(zz))
        r = lax.rsqrt(jnp.mean(v * v, axis=-1, keepdims=True) + EPS_RMS)
        o_ref[...] = (v * r * w_ref[...]).astype(BF16)

    c = SSD_INNER
    return pl.pallas_call(body, name="ssd_gate_norm_fwd", grid=(S // TR,), in_specs=[_row_spec(c), _row_spec(c), _vec_spec(c)],
                          out_specs=_row_spec(c), out_shape=jax.ShapeDtypeStruct((S, c), BF16))(y, z, w)


def _gate_norm_bwd(y, z, w, dout):
    def body(y_ref, z_ref, w_ref, g_ref, dy_ref, dz_ref, dw_ref):
        yy = y_ref[...]
        zz = z_ref[...]
        sg = _sigmoid(zz)
        sz = zz * sg
        v = yy * sz
        g = g_ref[...]
        r = lax.rsqrt(jnp.mean(v * v, axis=-1, keepdims=True) + EPS_RMS)
        gw = g * w_ref[...]
        dv = r * gw - v * (r * r * r * jnp.mean(gw * v, axis=-1, keepdims=True))
        dy_ref[...] = dv * sz
        dz_ref[...] = (dv * yy * (sg * (1.0 + zz * (1.0 - sg)))).astype(BF16)
        _acc_rows(dw_ref, jnp.sum(g * v * r, axis=0, keepdims=True))

    c = SSD_INNER
    return pl.pallas_call(body, name="ssd_gate_norm_bwd", grid=(S // TR,),
                          in_specs=[_row_spec(c), _row_spec(c), _vec_spec(c), _row_spec(c)],
                          out_specs=[_row_spec(c), _row_spec(c), _vec_spec(c)],
                          out_shape=[jax.ShapeDtypeStruct((S, c), F32), jax.ShapeDtypeStruct((S, c), BF16),
                                     jax.ShapeDtypeStruct((1, c), F32)])(y, z, w, dout)


def _ln_fwd(xr, mix, g, b):
    def body(x_ref, m_ref, g_ref, b_ref, o_ref, ob_ref):
        pre = ALPHA * x_ref[...] + m_ref[...]
        mu = jnp.mean(pre, axis=-1, keepdims=True)
        d = pre - mu
        rs = lax.rsqrt(jnp.mean(d * d, axis=-1, keepdims=True) + EPS_LN)
        h = d * rs * g_ref[...] + b_ref[...]
        o_ref[...] = h
        ob_ref[...] = h.astype(BF16)

    return pl.pallas_call(body, name="ln_mix_fwd", grid=(S // TR,), in_specs=[_row_spec(D), _row_spec(D), _vec_spec(D), _vec_spec(D)],
                          out_specs=[_row_spec(D)] * 2,
                          out_shape=[jax.ShapeDtypeStruct((S, D), F32), jax.ShapeDtypeStruct((S, D), BF16)])(xr, mix, g, b)


def _ln_bwd(xr, mix, g, dh):
    def body(x_ref, m_ref, g_ref, dh_ref, dpre_ref, dpreb_ref, dg_ref, db_ref):
        pre = ALPHA * x_ref[...] + m_ref[...]
        mu = jnp.mean(pre, axis=-1, keepdims=True)
        d = pre - mu
        rs = lax.rsqrt(jnp.mean(d * d, axis=-1, keepdims=True) + EPS_LN)
        xh = d * rs
        dy = dh_ref[...]
        gy = dy * g_ref[...]
        dpre = rs * (gy - jnp.mean(gy, axis=-1, keepdims=True) - xh * jnp.mean(gy * xh, axis=-1, keepdims=True))
        dpre_ref[...] = dpre
        dpreb_ref[...] = dpre.astype(BF16)
        _acc_rows(dg_ref, jnp.sum(dy * xh, axis=0, keepdims=True))
        _acc_rows(db_ref, jnp.sum(dy, axis=0, keepdims=True))

    return pl.pallas_call(body, name="ln_mix_bwd", grid=(S // TR,),
                          in_specs=[_row_spec(D), _row_spec(D), _vec_spec(D), _row_spec(D)],
                          out_specs=[_row_spec(D), _row_spec(D), _vec_spec(D), _vec_spec(D)],
                          out_shape=[jax.ShapeDtypeStruct((S, D), F32), jax.ShapeDtypeStruct((S, D), BF16),
                                     jax.ShapeDtypeStruct((1, D), F32), jax.ShapeDtypeStruct((1, D), F32)])(xr, mix, g, dh)


FF_CHUNK = D_FF // NCHIP


def _ff_spec():
    return pl.BlockSpec((None, TR * 2, FF_CHUNK), lambda k, i: (k, i, 0))


def _swiglu_fwd(gate, up):
    def body(g_ref, u_ref, o_ref):
        g = g_ref[...]
        o_ref[...] = (g * _sigmoid(g) * u_ref[...]).astype(BF16)

    return pl.pallas_call(body, name="swiglu_fwd", grid=(NCHIP, S // (2 * TR)), in_specs=[_ff_spec()] * 2, out_specs=_ff_spec(),
                          out_shape=jax.ShapeDtypeStruct((NCHIP, S, FF_CHUNK), BF16))(gate, up)


def _swiglu_bwd(gate, up, dact):
    def body(g_ref, u_ref, d_ref, dg_ref, du_ref):
        g = g_ref[...]
        sg = _sigmoid(g)
        d = d_ref[...]
        dg_ref[...] = (d * u_ref[...] * (sg * (1.0 + g * (1.0 - sg)))).astype(BF16)
        du_ref[...] = (d * g * sg).astype(BF16)

    return pl.pallas_call(body, name="swiglu_bwd", grid=(NCHIP, S // (2 * TR)), in_specs=[_ff_spec()] * 3, out_specs=[_ff_spec()] * 2,
                          out_shape=[jax.ShapeDtypeStruct((NCHIP, S, FF_CHUNK), BF16)] * 2)(gate, up, dact)


def _final_fwd_bwd(h1, ffn, pg, pp, target, g2, b2):
    def body(h_ref, f_ref, pg_ref, pp_ref, t_ref, g_ref, b_ref, dpre_ref, dpreb_ref, dpg_ref, dpp_ref, dg_ref, db_ref, loss_ref):
        sg = _sigmoid(pg_ref[...])
        ppv = pp_ref[...]
        pre = ALPHA * h_ref[...] + f_ref[...] + sg * ppv
        mu = jnp.mean(pre, axis=-1, keepdims=True)
        d = pre - mu
        rs = lax.rsqrt(jnp.mean(d * d, axis=-1, keepdims=True) + EPS_LN)
        xh = d * rs
        err = xh * g_ref[...] + b_ref[...] - t_ref[...]
        dy = err * (1.0 / D)
        gy = dy * g_ref[...]
        dpre = rs * (gy - jnp.mean(gy, axis=-1, keepdims=True) - xh * jnp.mean(gy * xh, axis=-1, keepdims=True))
        dpre_ref[...] = dpre
        dpreb_ref[...] = dpre.astype(BF16)
        dpg_ref[...] = (dpre * ppv * sg * (1.0 - sg)).astype(BF16)
        dpp_ref[...] = (dpre * sg).astype(BF16)
        _acc_rows(dg_ref, jnp.sum(dy * xh, axis=0, keepdims=True))
        _acc_rows(db_ref, jnp.sum(dy, axis=0, keepdims=True))
        _acc_rows(loss_ref, 0.5 * jnp.sum(jnp.mean(err * err, axis=-1, keepdims=True), axis=0, keepdims=True) * jnp.ones((1, LANE), F32))

    return pl.pallas_call(
        body, name="final_ln_loss", grid=(S // TR,),
        in_specs=[_row_spec(D)] * 5 + [_vec_spec(D)] * 2,
        out_specs=[_row_spec(D)] * 4 + [_vec_spec(D), _vec_spec(D), _vec_spec(LANE)],
        out_shape=[jax.ShapeDtypeStruct((S, D), F32)] + [jax.ShapeDtypeStruct((S, D), BF16)] * 3 + [
                   jax.ShapeDtypeStruct((1, D), F32), jax.ShapeDtypeStruct((1, D), F32), jax.ShapeDtypeStruct((1, LANE), F32)],
    )(h1, ffn, pg, pp, target, g2, b2)


def _rot(u, cos_t, sin_t, lane):
    partner = jnp.where(lane < NOPE + ROPE // 2, pltpu.roll(u, LANE - ROPE // 2, 1), pltpu.roll(u, ROPE // 2, 1))
    return u * cos_t + partner * sin_t


def _rms(v, w):
    r = lax.rsqrt(jnp.mean(v * v, axis=-1, keepdims=True) + EPS_RMS)
    return v * r * w, r


def _rms_grad(v, r, w, g):
    gw = g * w
    return r * gw - v * (r * r * r * jnp.mean(gw * v, axis=-1, keepdims=True)), jnp.sum(g * v * r, axis=0, keepdims=True)


def _whole(arr):
    return pl.BlockSpec(arr.shape, lambda i: (0,) * arr.ndim)


def _qkv_fwd(small, w_q, w_k, w_v, q_norm, kv_norm, cos_t, sin_t):
    def body(sm_ref, wq_ref, wk_ref, wv_ref, qw_ref, kw_ref, c_ref, s_ref, qn_ref, kvn_ref, q_ref, k_ref, kt_ref, v_ref):
        lane = lax.broadcasted_iota(jnp.int32, (TR, LANE), 1)
        c, s = c_ref[...], s_ref[...]
        qn = _rms(sm_ref[:, SM_Q:SM_Q + Q_RANK], qw_ref[...])[0].astype(BF16)
        kvn = _rms(sm_ref[:, SM_KV:SM_KV + KV_RANK], kw_ref[...])[0].astype(BF16)
        qn_ref[...] = qn
        kvn_ref[...] = kvn
        kr = _rot(pltpu.roll(sm_ref[:, SM_KR:SM_KR + LANE], NOPE, 1), c, s, lane)
        for h in range(H):
            tile = slice(h * LANE, (h + 1) * LANE)
            q_ref[:, tile] = _rot(_dot(qn, wq_ref[:, tile], ((1,), (0,))), c, s, lane).astype(BF16)
            kt = _dot(kvn, wk_ref[:, tile], ((1,), (0,))) + kr
            k_ref[:, tile] = kt.astype(BF16)
            kt_ref[tile, :] = kt.T.astype(BF16)
        v_ref[...] = _dot(kvn, wv_ref[...], ((1,), (0,))).astype(BF16)

    w = H * LANE
    return pl.pallas_call(
        body, name="qkv_fwd", grid=(S // TR,),
        in_specs=[_row_spec(SMALL_W), _whole(w_q), _whole(w_k), _whole(w_v), _vec_spec(Q_RANK), _vec_spec(KV_RANK), _row_spec(LANE), _row_spec(LANE)],
        out_specs=[_row_spec(Q_RANK), _row_spec(KV_RANK), _row_spec(w), _row_spec(w), pl.BlockSpec((w, TR), lambda i: (0, i)),
                   _row_spec(H * VDIM)],
        out_shape=[jax.ShapeDtypeStruct((S, Q_RANK), BF16), jax.ShapeDtypeStruct((S, KV_RANK), BF16), jax.ShapeDtypeStruct((S, w), BF16),
                   jax.ShapeDtypeStruct((S, w), BF16), jax.ShapeDtypeStruct((w, S), BF16), jax.ShapeDtypeStruct((S, H * VDIM), BF16)],
    )(small, w_q, w_k, w_v, q_norm, kv_norm, cos_t, sin_t)


def _qkv_bwd(dqt, dk, dv, small, w_q, w_k, w_v, q_norm, kv_norm, cos_t, sin_t):
    def body(dq_ref, dk_ref, dv_ref, sm_ref, wq_ref, wk_ref, wv_ref, qw_ref, kw_ref, c_ref, s_ref,
             ds_ref, dql_ref, dkb_ref, dqw_ref, dkw_ref):
        lane = lax.broadcasted_iota(jnp.int32, (TR, LANE), 1)
        c, s = c_ref[...], -s_ref[...]
        dqn = jnp.zeros((TR, Q_RANK), F32)
        dkvn = _dot(dv_ref[...], wv_ref[...], ((1,), (1,)))
        dkr = jnp.zeros((TR, LANE), F32)
        for h in range(H):
            tile = slice(h * LANE, (h + 1) * LANE)
            dql = _rot(dq_ref[tile, :].T, c, s, lane).astype(BF16)
            dql_ref[:, tile] = dql
            dqn = dqn + _dot(dql, wq_ref[:, tile], ((1,), (1,)))
            dkt = dk_ref[:, tile]
            dkb_ref[:, tile] = dkt.astype(BF16)
            dkvn = dkvn + _dot(dkt, wk_ref[:, tile], ((1,), (1,)))
            dkr = dkr + dkt
        dkr = jnp.where((lane >= NOPE) & (lane < NOPE + ROPE), dkr, 0.0)
        q_c, kv_c = sm_ref[:, SM_Q:SM_Q + Q_RANK], sm_ref[:, SM_KV:SM_KV + KV_RANK]
        dq_c, dqw = _rms_grad(q_c, _rms(q_c, qw_ref[...])[1], qw_ref[...], dqn)
        dkv_c, dkw = _rms_grad(kv_c, _rms(kv_c, kw_ref[...])[1], kw_ref[...], dkvn)
        ds_ref[:, SM_Q:SM_Q + Q_RANK] = dq_c.astype(BF16)
        ds_ref[:, SM_KV:SM_KV + KV_RANK] = dkv_c.astype(BF16)
        ds_ref[:, SM_KR:SM_KR + LANE] = pltpu.roll(_rot(dkr, c, s, lane), LANE - NOPE, 1).astype(BF16)
        _acc_rows(dqw_ref, dqw)
        _acc_rows(dkw_ref, dkw)

    w = H * LANE
    return pl.pallas_call(
        body, name="qkv_bwd", grid=(S // TR,),
        in_specs=[pl.BlockSpec((w, TR), lambda i: (0, i)), _row_spec(w), _row_spec(H * VDIM), _row_spec(SMALL_W), _whole(w_q), _whole(w_k),
                  _whole(w_v), _vec_spec(Q_RANK), _vec_spec(KV_RANK), _row_spec(LANE), _row_spec(LANE)],
        out_specs=[_row_spec(SM_DT), _row_spec(w), _row_spec(w), _vec_spec(Q_RANK), _vec_spec(KV_RANK)],
        out_shape=[jax.ShapeDtypeStruct((S, SM_DT), BF16), jax.ShapeDtypeStruct((S, w), BF16), jax.ShapeDtypeStruct((S, w), BF16),
                   jax.ShapeDtypeStruct((1, Q_RANK), F32), jax.ShapeDtypeStruct((1, KV_RANK), F32)],
    )(dqt, dk, dv, small, w_q, w_k, w_v, q_norm, kv_norm, cos_t, sin_t)


CB = 256


def _shift_down(u, k, row):
    if k == 0:
        return u
    return jnp.where(row >= k, pltpu.roll(u, k, 0), 0.0)


def _shift_up(u, k, row):
    if k == 0:
        return u
    return jnp.where(row < S - k, pltpu.roll(u, S - k, 0), 0.0)


def _conv_fwd(u, w, b):
    def body(u_ref, w_ref, b_ref, o_ref):
        row = lax.broadcasted_iota(jnp.int32, (S, CB), 0)
        uu = u_ref[...]
        acc = b_ref[...] + w_ref[SSD_K - 1:SSD_K, :] * uu
        for k in range(SSD_K - 1):
            acc = acc + w_ref[k:k + 1, :] * _shift_down(uu, SSD_K - 1 - k, row)
        o_ref[...] = acc * _sigmoid(acc)

    c = u.shape[1]
    return pl.pallas_call(
        body, name="conv_fwd", grid=(c // CB,),
        in_specs=[pl.BlockSpec((S, CB), lambda j: (0, j)), pl.BlockSpec((SSD_K, CB), lambda j: (0, j)), pl.BlockSpec((1, CB), lambda j: (0, j))],
        out_specs=pl.BlockSpec((S, CB), lambda j: (0, j)), out_shape=jax.ShapeDtypeStruct((S, c), F32),
    )(u, w, b)


def _conv_bwd(u, w, b, dact):
    def body(u_ref, w_ref, b_ref, d_ref, du_ref, dw_ref, db_ref):
        row = lax.broadcasted_iota(jnp.int32, (S, CB), 0)
        uu = u_ref[...]
        sh = [_shift_down(uu, SSD_K - 1 - k, row) for k in range(SSD_K)]
        acc = b_ref[...]
        for k in range(SSD_K):
            acc = acc + w_ref[k:k + 1, :] * sh[k]
        sg = _sigmoid(acc)
        dacc = d_ref[...] * (sg * (1.0 + acc * (1.0 - sg)))
        du = w_ref[SSD_K - 1:SSD_K, :] * dacc
        for k in range(SSD_K - 1):
            du = du + w_ref[k:k + 1, :] * _shift_up(dacc, SSD_K - 1 - k, row)
        du_ref[...] = du.astype(BF16)
        for k in range(SSD_K):
            dw_ref[k:k + 1, :] = jnp.sum(dacc * sh[k], axis=0, keepdims=True)
        db_ref[...] = jnp.sum(dacc, axis=0, keepdims=True)

    c = u.shape[1]
    col = lambda r: pl.BlockSpec((r, CB), lambda j: (0, j))
    return pl.pallas_call(
        body, name="conv_bwd", grid=(c // CB,), in_specs=[col(S), col(SSD_K), col(1), col(S)], out_specs=[col(S), col(SSD_K), col(1)],
        out_shape=[jax.ShapeDtypeStruct((S, c), BF16), jax.ShapeDtypeStruct((SSD_K, c), F32), jax.ShapeDtypeStruct((1, c), F32)],
    )(u, w, b, dact)


NPAIR = H // 2
PAIRS_PER_GROUP = NPAIR // SSD_G


def _softplus(v):
    return jnp.maximum(v, 0.0) + jnp.log(1.0 + jnp.exp(-jnp.abs(v)))


def _dot(a, b, dims):
    return lax.dot_general(a.astype(BF16), b.astype(BF16), (dims, ((), ())), preferred_element_type=F32)


def _dot3(a, b, dims, split_lhs):
    v = a if split_lhs else b
    v1 = v.astype(BF16)
    r1 = v - v1.astype(F32)
    v2 = r1.astype(BF16)
    v3 = (r1 - v2.astype(F32)).astype(BF16)
    acc = None
    for part in (v1, v2, v3):
        lhs, rhs = (part, b) if split_lhs else (a, part)
        t = lax.dot_general(lhs, rhs, (dims, ((), ())), preferred_element_type=F32)
        acc = t if acc is None else acc + t
    return acc


def _ssd_chunk_common(dt_ref, dtT_ref, prow_ref, pcol_ref):
    prow = prow_ref[...]
    pcol = pcol_ref[...]
    ri = lax.broadcasted_iota(jnp.int32, (SSD_L, SSD_L), 0)
    ci = lax.broadcasted_iota(jnp.int32, (SSD_L, SSD_L), 1)
    causal = ri >= ci
    pre_c = dt_ref[...] + prow[0:1, :]
    dtc = _softplus(pre_c)
    a_row = -jnp.exp(prow[1:2, :])
    cs_col = _dot3(causal.astype(BF16), dtc * a_row, ((1,), (0,)), False)
    dtr = _softplus(dtT_ref[...] + pcol[:, 0:1])
    a_col = -jnp.exp(pcol[:, 1:2])
    cs_row = _dot3(dtr * a_col, (ri <= ci).astype(BF16), ((1,), (0,)), True)
    return prow, causal, pre_c, dtc, a_row, cs_col, cs_row


def _ssd_fwd(act, small, dtT, prow, pcol):
    def body(x_ref, b_ref, c_ref, dt_ref, dtT_ref, prow_ref, pcol_ref, y_ref, st_ref, state):
        @pl.when(pl.program_id(0) == 0)
        def _():
            state[...] = jnp.zeros_like(state)

        prow, causal, _, dtc, _, cs_col, cs_row = _ssd_chunk_common(dt_ref, dtT_ref, prow_ref, pcol_ref)
        lo = lax.broadcasted_iota(jnp.int32, (SSD_L, LANE), 1) < SSD_P
        lo1 = lo[0:1, :]
        for g in range(SSD_G):
            bm = b_ref[:, g * SSD_N:(g + 1) * SSD_N]
            cm = c_ref[:, g * SSD_N:(g + 1) * SSD_N]
            cb = _dot(cm, bm, ((1,), (1,)))
            for qq in range(PAIRS_PER_GROUP):
                q = g * PAIRS_PER_GROUP + qq
                ha, hb = 2 * q, 2 * q + 1
                csa, csb = cs_col[:, ha:ha + 1], cs_col[:, hb:hb + 1]
                xp = x_ref[:, q * LANE:(q + 1) * LANE]
                xx = xp * jnp.where(lo, dtc[:, ha:ha + 1], dtc[:, hb:hb + 1])
                ga = cb * jnp.exp(jnp.where(causal, csa - cs_row[ha:ha + 1, :], NEG))
                gb = cb * jnp.exp(jnp.where(causal, csb - cs_row[hb:hb + 1, :], NEG))
                y = _dot(ga, jnp.where(lo, xx, 0.0), ((1,), (0,))) + _dot(gb, jnp.where(lo, 0.0, xx), ((1,), (0,)))
                s_in = state[q]
                y = y + _dot(cm, s_in, ((1,), (0,))) * jnp.where(lo, jnp.exp(csa), jnp.exp(csb))
                y = y + jnp.where(lo1, prow[2:3, ha:ha + 1], prow[2:3, hb:hb + 1]) * xp
                y_ref[:, q * LANE:(q + 1) * LANE] = y
                la, lb = csa[SSD_L - 1:SSD_L, :], csb[SSD_L - 1:SSD_L, :]
                decay = jnp.where(lo, jnp.exp(la - csa), jnp.exp(lb - csb))
                st_ref[q] = s_in
                state[q] = s_in * jnp.where(lo1, jnp.exp(la), jnp.exp(lb)) + _dot(bm, xx * decay, ((0,), (0,)))

    L = SSD_L
    return pl.pallas_call(
        body, name="ssd_fwd", grid=(SSD_NC,),
        in_specs=[pl.BlockSpec((L, SSD_INNER), lambda c: (c, 0)),
                  pl.BlockSpec((L, SSD_G * SSD_N), lambda c: (c, SSD_INNER // (SSD_G * SSD_N))),
                  pl.BlockSpec((L, SSD_G * SSD_N), lambda c: (c, SSD_INNER // (SSD_G * SSD_N) + 1)),
                  pl.BlockSpec((L, LANE), lambda c: (c, SM_DT // LANE)),
                  pl.BlockSpec((LANE, L), lambda c: (0, c)),
                  pl.BlockSpec((8, LANE), lambda c: (0, 0)), pl.BlockSpec((LANE, 8), lambda c: (0, 0))],
        out_specs=[pl.BlockSpec((L, SSD_INNER), lambda c: (c, 0)),
                   pl.BlockSpec((None, NPAIR, SSD_N, LANE), lambda c: (c, 0, 0, 0))],
        out_shape=[jax.ShapeDtypeStruct((S, SSD_INNER), F32), jax.ShapeDtypeStruct((SSD_NC, NPAIR, SSD_N, LANE), F32)],
        scratch_shapes=[pltpu.VMEM((NPAIR, SSD_N, LANE), F32)],
        compiler_params=pltpu.CompilerParams(dimension_semantics=("arbitrary",)),
    )(act, act, act, small, dtT, prow, pcol)


def _ssd_bwd(act, small, dtT, prow, pcol, states, dy):
    def body(x_ref, b_ref, c_ref, dt_ref, dtT_ref, prow_ref, pcol_ref, st_ref, dy_ref,
             dx_ref, ddt_ref, dp_ref, dstate):
        @pl.when(pl.program_id(0) == 0)
        def _():
            dstate[...] = jnp.zeros_like(dstate)
            dp_ref[...] = jnp.zeros_like(dp_ref)

        prow, causal, pre_c, dtc, a_row, cs_col, cs_row = _ssd_chunk_common(dt_ref, dtT_ref, prow_ref, pcol_ref)
        lane = lax.broadcasted_iota(jnp.int32, (SSD_L, LANE), 1)
        sub = lax.broadcasted_iota(jnp.int32, (LANE, SSD_L), 0)
        rowi = lax.broadcasted_iota(jnp.int32, (SSD_L, 1), 0)
        lane1 = lane[0:1, :]
        lo = lane < SSD_P
        lo1 = lo[0:1, :]
        dcs_c = jnp.zeros((SSD_L, LANE), F32)
        dcs_r = jnp.zeros((LANE, SSD_L), F32)
        ddt_x = jnp.zeros((SSD_L, LANE), F32)
        dd_row = jnp.zeros((1, LANE), F32)
        for g in range(SSD_G):
            bm = b_ref[:, g * SSD_N:(g + 1) * SSD_N]
            cm = c_ref[:, g * SSD_N:(g + 1) * SSD_N]
            cb = _dot(cm, bm, ((1,), (1,)))
            dcb = jnp.zeros((SSD_L, SSD_L), F32)
            dbm = jnp.zeros((SSD_L, SSD_N), F32)
            dcm = jnp.zeros((SSD_L, SSD_N), F32)
            for qq in range(PAIRS_PER_GROUP):
                q = g * PAIRS_PER_GROUP + qq
                ha, hb = 2 * q, 2 * q + 1
                csa, csb = cs_col[:, ha:ha + 1], cs_col[:, hb:hb + 1]
                xp = x_ref[:, q * LANE:(q + 1) * LANE]
                dtp = jnp.where(lo, dtc[:, ha:ha + 1], dtc[:, hb:hb + 1])
                xx = xp * dtp
                lma = jnp.exp(jnp.where(causal, csa - cs_row[ha:ha + 1, :], NEG))
                lmb = jnp.exp(jnp.where(causal, csb - cs_row[hb:hb + 1, :], NEG))
                ga, gb = cb * lma, cb * lmb
                dyp = dy_ref[:, q * LANE:(q + 1) * LANE]
                dya, dyb = jnp.where(lo, dyp, 0.0), jnp.where(lo, 0.0, dyp)
                s_in = st_ref[q]
                ds_out = dstate[q]
                la, lb = csa[SSD_L - 1:SSD_L, :], csb[SSD_L - 1:SSD_L, :]
                ecs = jnp.where(lo, jnp.exp(csa), jnp.exp(csb))
                decay = jnp.where(lo, jnp.exp(la - csa), jnp.exp(lb - csb))
                cd = jnp.where(lo1, jnp.exp(la), jnp.exp(lb))
                bds = _dot(bm, ds_out, ((1,), (0,)))
                dxx = _dot(ga, dya, ((0,), (0,))) + _dot(gb, dyb, ((0,), (0,))) + bds * decay
                dga = _dot(dya, xx, ((1,), (1,)))
                dgb = _dot(dyb, xx, ((1,), (1,)))
                dsega, dsegb = dga * ga, dgb * gb
                dcb = dcb + dga * lma + dgb * lmb
                yoff = _dot(cm, s_in, ((1,), (0,))) * ecs
                dye = dyp * ecs
                dcm = dcm + _dot(dye, s_in, ((1,), (1,)))
                xd = xx * decay
                dbm = dbm + _dot(xd, ds_out, ((1,), (1,)))
                wv = xd * bds
                t1 = dyp * yoff - wv
                col_a = (jnp.sum(dsega, axis=1, keepdims=True) + jnp.sum(jnp.where(lo, t1, 0.0), axis=1, keepdims=True))
                col_b = (jnp.sum(dsegb, axis=1, keepdims=True) + jnp.sum(jnp.where(lo, 0.0, t1), axis=1, keepdims=True))
                sprod = ds_out * s_in
                end_a = jnp.sum(jnp.where(lo, wv, 0.0), keepdims=True) + jnp.exp(la) * jnp.sum(jnp.where(lo[:SSD_N], sprod, 0.0), keepdims=True)
                end_b = jnp.sum(jnp.where(lo, 0.0, wv), keepdims=True) + jnp.exp(lb) * jnp.sum(jnp.where(lo[:SSD_N], 0.0, sprod), keepdims=True)
                col_a = col_a + jnp.where(rowi == SSD_L - 1, end_a, 0.0)
                col_b = col_b + jnp.where(rowi == SSD_L - 1, end_b, 0.0)
                dcs_c = dcs_c + jnp.where(lane == ha, col_a, 0.0) + jnp.where(lane == hb, col_b, 0.0)
                dcs_r = (dcs_r + jnp.where(sub == ha, jnp.sum(dsega, axis=0, keepdims=True), 0.0)
                         + jnp.where(sub == hb, jnp.sum(dsegb, axis=0, keepdims=True), 0.0))
                dstate[q] = _dot(cm, dye, ((0,), (0,))) + cd * ds_out
                dpair = jnp.where(lo1, prow[2:3, ha:ha + 1], prow[2:3, hb:hb + 1])
                dx_ref[:, q * LANE:(q + 1) * LANE] = dxx * dtp + dpair * dyp
                t2 = dxx * xp
                ddt_x = (ddt_x + jnp.where(lane == ha, jnp.sum(jnp.where(lo, t2, 0.0), axis=1, keepdims=True), 0.0)
                         + jnp.where(lane == hb, jnp.sum(jnp.where(lo, 0.0, t2), axis=1, keepdims=True), 0.0))
                t3 = dyp * xp
                dd_row = (dd_row + jnp.where(lane1 == ha, jnp.sum(jnp.where(lo, t3, 0.0), keepdims=True), 0.0)
                          + jnp.where(lane1 == hb, jnp.sum(jnp.where(lo, 0.0, t3), keepdims=True), 0.0))
            dx_ref[:, SSD_INNER + g * SSD_N:SSD_INNER + (g + 1) * SSD_N] = dbm + _dot(dcb, cm, ((0,), (0,)))
            dx_ref[:, SSD_INNER + (SSD_G + g) * SSD_N:SSD_INNER + (SSD_G + g + 1) * SSD_N] = dcm + _dot(dcb, bm, ((1,), (0,)))
        ri = lax.broadcasted_iota(jnp.int32, (SSD_L, SSD_L), 0)
        ci = lax.broadcasted_iota(jnp.int32, (SSD_L, SSD_L), 1)
        da = _dot3((ri <= ci).astype(BF16), dcs_c, ((1,), (0,)), False)
        da = da - _dot3(dcs_r, causal.astype(BF16), ((1,), (0,)), True).T
        ddt = ddt_x + da * a_row
        ddt_raw = ddt * _sigmoid(pre_c)
        ddt_ref[...] = ddt_raw
        da_head = jnp.sum(da * dtc, axis=0, keepdims=True) * a_row
        dp_ref[0:1, :] += jnp.sum(ddt_raw, axis=0, keepdims=True)
        dp_ref[1:2, :] += da_head
        dp_ref[2:3, :] += dd_row

    L = SSD_L
    rev = SSD_NC - 1
    bc_cols = SSD_INNER // (SSD_G * SSD_N)
    return pl.pallas_call(
        body, name="ssd_bwd", grid=(SSD_NC,),
        in_specs=[pl.BlockSpec((L, SSD_INNER), lambda c: (rev - c, 0)),
                  pl.BlockSpec((L, SSD_G * SSD_N), lambda c: (rev - c, bc_cols)),
                  pl.BlockSpec((L, SSD_G * SSD_N), lambda c: (rev - c, bc_cols + 1)),
                  pl.BlockSpec((L, LANE), lambda c: (rev - c, SM_DT // LANE)),
                  pl.BlockSpec((LANE, L), lambda c: (0, rev - c)),
                  pl.BlockSpec((8, LANE), lambda c: (0, 0)), pl.BlockSpec((LANE, 8), lambda c: (0, 0)),
                  pl.BlockSpec((None, NPAIR, SSD_N, LANE), lambda c: (rev - c, 0, 0, 0)),
                  pl.BlockSpec((L, SSD_INNER), lambda c: (rev - c, 0))],
        out_specs=[pl.BlockSpec((L, SSD_XBC), lambda c: (rev - c, 0)),
                   pl.BlockSpec((L, LANE), lambda c: (rev - c, 0)),
                   pl.BlockSpec((8, LANE), lambda c: (0, 0))],
        out_shape=[jax.ShapeDtypeStruct((S, SSD_XBC), F32), jax.ShapeDtypeStruct((S, LANE), F32),
                   jax.ShapeDtypeStruct((8, LANE), F32)],
        scratch_shapes=[pltpu.VMEM((NPAIR, SSD_N, LANE), F32)],
        compiler_params=pltpu.CompilerParams(dimension_semantics=("arbitrary",)),
    )(act, act, act, small, dtT, prow, pcol, states, dy)


TQ = 256
TK = 256
FWD_TQ = 256
FWD_TK = 256


def _attn_fwd(qc, kc, v):
    TQ, TK = FWD_TQ, FWD_TK

    def body(q_ref, k_ref, v_ref, o_ref, lse_ref):
        i = pl.program_id(1)
        lo = lax.broadcasted_iota(jnp.int32, (TQ, LANE), 1) < VDIM
        lo_k = lax.broadcasted_iota(jnp.int32, (TK, LANE), 1) < VDIM
        row_minus_col = lax.broadcasted_iota(jnp.int32, (TQ, TK), 0) - lax.broadcasted_iota(jnp.int32, (TQ, TK), 1)
        qa, qb = q_ref[:, 0:LANE], q_ref[:, LANE:2 * LANE]

        def scores(kb):
            kk = k_ref[pl.ds(pl.multiple_of(kb * TK, TK), TK), :]
            return (_dot(qa, kk[:, 0:LANE], ((1,), (1,))) * ATT_SCALE_LOG2, _dot(qb, kk[:, LANE:2 * LANE], ((1,), (1,))) * ATT_SCALE_LOG2)

        def update(kb, sa, sb, stats):
            ma, la, mb, lb, acc = stats
            vv = v_ref[pl.ds(pl.multiple_of(kb * TK, TK), TK), :]
            na = jnp.maximum(ma, jnp.max(sa, axis=1, keepdims=True))
            nb = jnp.maximum(mb, jnp.max(sb, axis=1, keepdims=True))
            pa, pb = jnp.exp2(sa - na), jnp.exp2(sb - nb)
            fa, fb = jnp.exp2(ma - na), jnp.exp2(mb - nb)
            la = fa * la + jnp.sum(pa, axis=1, keepdims=True)
            lb = fb * lb + jnp.sum(pb, axis=1, keepdims=True)
            acc = (acc * jnp.where(lo, fa, fb) + _dot(pa, jnp.where(lo_k, vv, 0), ((1,), (0,)))
                   + _dot(pb, jnp.where(lo_k, 0, vv), ((1,), (0,))))
            return na, la, nb, lb, acc

        def step(kb, carry):
            sa, sb = carry[:2]
            nxt = scores(kb + 1)
            return nxt + update(kb, sa, sb, carry[2:])

        neg = jnp.full((TQ, 1), NEG, F32)
        zero = jnp.zeros((TQ, 1), F32)
        n_full = i * (TQ // TK)
        carry = lax.fori_loop(0, n_full, step, scores(0) + (neg, zero, neg, zero, jnp.zeros((TQ, LANE), F32)))
        s, stats = carry[:2], carry[2:]
        for d in range(TQ // TK):
            nxt = scores(n_full + d + 1) if d + 1 < TQ // TK else None
            sa, sb = (jnp.where(row_minus_col >= d * TK, t, NEG) for t in s)
            stats = update(n_full + d, sa, sb, stats)
            s = nxt
        ma, la, mb, lb, acc = stats
        o_ref[...] = acc / jnp.where(lo, la, lb)
        lse_ref[...] = jnp.where(lo, ma + jnp.log2(la), mb + jnp.log2(lb)) * LN2

    return pl.pallas_call(
        body, name="attn_fwd", grid=(NPAIR, S // TQ),
        in_specs=[pl.BlockSpec((TQ, 2 * LANE), lambda j, i: (i, j)), pl.BlockSpec((S, 2 * LANE), lambda j, i: (0, j)),
                  pl.BlockSpec((S, LANE), lambda j, i: (0, j))],
        out_specs=[pl.BlockSpec((TQ, LANE), lambda j, i: (i, j)), pl.BlockSpec((None, TQ, LANE), lambda j, i: (j, i, 0))],
        out_shape=[jax.ShapeDtypeStruct((S, H * VDIM), F32), jax.ShapeDtypeStruct((NPAIR, S, LANE), F32)],
        compiler_params=pltpu.CompilerParams(dimension_semantics=("parallel", "parallel")),
    )(qc, kc, v)


def _attn_rows(lse, o, do):
    def body(lse_ref, o_ref, do_ref, r_ref):
        lt = lse_ref[...].T * (1.0 / LN2)
        tt = (o_ref[...] * do_ref[...]).T
        r_ref[...] = jnp.zeros_like(r_ref)
        r_ref[0:1, :] = lt[0:1, :]
        r_ref[1:2, :] = lt[VDIM:VDIM + 1, :]
        r_ref[2:3, :] = jnp.sum(tt[0:VDIM, :], axis=0, keepdims=True)
        r_ref[3:4, :] = jnp.sum(tt[VDIM:LANE, :], axis=0, keepdims=True)

    tile = pl.BlockSpec((TR, LANE), lambda j, i: (i, j))
    return pl.pallas_call(
        body, name="attn_rows", grid=(NPAIR, S // TR),
        in_specs=[pl.BlockSpec((None, TR, LANE), lambda j, i: (j, i, 0)), tile, tile],
        out_specs=pl.BlockSpec((None, 8, TR), lambda j, i: (j, 0, i)), out_shape=jax.ShapeDtypeStruct((NPAIR, 8, S), F32),
    )(lse, o, do)


def _attn_bwd(qc, kc, kct, v, do, rows):
    nq = S // TQ

    def body(q_ref, k_ref, kt_ref, v_ref, do_ref, r_ref, dqt_ref, dk_ref, dv_ref):
        kb = pl.program_id(1)

        @pl.when(kb == 0)
        def _():
            dqt_ref[...] = jnp.zeros_like(dqt_ref)

        lo = lax.broadcasted_iota(jnp.int32, (TK, LANE), 1) < VDIM
        q_minus_k = lax.broadcasted_iota(jnp.int32, (TK, TQ), 1) - lax.broadcasted_iota(jnp.int32, (TK, TQ), 0)
        vv = v_ref[...]
        kk = k_ref[...]

        def step(qi, carry):
            off = pl.multiple_of(qi * TQ, TQ)
            qq = q_ref[pl.ds(off, TQ), :]
            dd = do_ref[pl.ds(off, TQ), :].astype(BF16)
            rr = r_ref[:, pl.ds(off, TQ)]
            keep = q_minus_k >= (kb - qi) * TQ
            out = []
            for x in range(2):
                sel = lo if x == 0 else jnp.logical_not(lo)
                kx, qx = kk[:, x * LANE:(x + 1) * LANE], qq[:, x * LANE:(x + 1) * LANE]
                st = jnp.where(keep, _dot(kx, qx, ((1,), (1,))) * ATT_SCALE_LOG2, NEG)
                pt = jnp.exp2(st - rr[x:x + 1, :])
                dpt = _dot(jnp.where(sel, vv, 0), dd, ((1,), (1,)))
                dst = (pt * (dpt - rr[2 + x:3 + x, :]) * ATT_SCALE).astype(BF16)
                out.append(carry[x] + _dot(dst, qx, ((1,), (0,))))
                out.append(_dot(pt, jnp.where(sel, dd, 0), ((1,), (0,))))
                dqt_ref[x * LANE:(x + 1) * LANE, pl.ds(off, TQ)] += _dot(kt_ref[x * LANE:(x + 1) * LANE, :], dst, ((1,), (0,)))
            return out[0], out[2], carry[2] + out[1] + out[3]

        z = jnp.zeros((TK, LANE), F32)
        dka, dkb, dv = lax.fori_loop(kb, nq, step, (z, z, z))
        dk_ref[:, 0:LANE] = dka
        dk_ref[:, LANE:2 * LANE] = dkb
        dv_ref[...] = dv.astype(BF16)

    return pl.pallas_call(
        body, name="attn_bwd", grid=(NPAIR, S // TK),
        in_specs=[pl.BlockSpec((S, 2 * LANE), lambda j, k: (0, j)), pl.BlockSpec((TK, 2 * LANE), lambda j, k: (k, j)),
                  pl.BlockSpec((2 * LANE, TK), lambda j, k: (j, k)), pl.BlockSpec((TK, LANE), lambda j, k: (k, j)),
                  pl.BlockSpec((S, LANE), lambda j, k: (0, j)), pl.BlockSpec((None, 8, S), lambda j, k: (j, 0, 0))],
        out_specs=[pl.BlockSpec((2 * LANE, S), lambda j, k: (j, 0)), pl.BlockSpec((TK, 2 * LANE), lambda j, k: (k, j)),
                   pl.BlockSpec((TK, LANE), lambda j, k: (k, j))],
        out_shape=[jax.ShapeDtypeStruct((H * LANE, S), F32), jax.ShapeDtypeStruct((S, H * LANE), F32),
                   jax.ShapeDtypeStruct((S, H * VDIM), BF16)],
        compiler_params=pltpu.CompilerParams(dimension_semantics=("parallel", "arbitrary")),
    )(qc, kc, kct, v, do, rows)


_IN_Z, _IN_XBC, _IN_DT, _IN_Q, _IN_KV, _IN_KR = 0, 1024, 2560, 2576, 2960, 3216


def _prep_in(w_in):
    dt = w_in.dtype
    w_small = jnp.concatenate(
        [w_in[:, _IN_Q:_IN_KV], w_in[:, _IN_KV:_IN_KR], w_in[:, _IN_KR:IN_WIDTH], jnp.zeros((D, LANE - ROPE), dt),
         w_in[:, _IN_DT:_IN_Q], jnp.zeros((D, LANE - H), dt)], axis=1)
    return w_in[:, _IN_Z:_IN_XBC], w_in[:, _IN_XBC:_IN_DT], w_small


def _prep_attn(w_qb, w_kvb):
    w_q = jnp.pad(w_qb.reshape(Q_RANK, H, NOPE + ROPE), ((0, 0), (0, 0), (0, LANE - NOPE - ROPE))).reshape(Q_RANK, H * LANE)
    kv3 = w_kvb.reshape(KV_RANK, H, NOPE + VDIM)
    w_k = jnp.pad(kv3[:, :, :NOPE], ((0, 0), (0, 0), (0, LANE - NOPE))).reshape(KV_RANK, H * LANE)
    w_v = kv3[:, :, NOPE:].reshape(KV_RANK, H * VDIM)
    return w_q, w_k, w_v


def _rope_tables(positions):
    inv_freq = 1.0 / (10000.0 ** (jnp.arange(0, ROPE, 2, dtype=F32) / ROPE))
    ang = positions.astype(F32).reshape(S, 1) * inv_freq
    cos, sin = jnp.cos(ang), jnp.sin(ang)
    cos_t = jnp.concatenate([jnp.ones((S, NOPE), F32), cos, cos, jnp.ones((S, LANE - NOPE - ROPE), F32)], axis=1)
    sin_t = jnp.concatenate([jnp.zeros((S, NOPE), F32), -sin, sin, jnp.zeros((S, LANE - NOPE - ROPE), F32)], axis=1)
    return cos_t, sin_t


def _local_step(x, p, positions, target, w_in, fetch, send, sp):
    w_z, w_xbc, w_small = _prep_in(_from_cols(w_in))
    cos_t, sin_t = _rope_tables(positions)
    prow = jnp.zeros((8, LANE), F32).at[0, :H].set(sp["dt_bias"][0]).at[1, :H].set(sp["A_log"][0]).at[2, :H].set(sp["D"][0])
    pcol = prow.T

    xb, pb = x.astype(BF16), p.astype(BF16)
    z = _mm([(xb, w_z)], name="proj_z")
    xbc = _mm([(xb, w_xbc)], name="proj_xbc")
    small = _mm([(xb, w_small)], name="proj_small")
    act = _conv_fwd(xbc, sp["conv_w"], sp["conv_b"])
    dt_t = small[:, SM_DT:SM_DT + LANE].T
    y, states = _ssd_fwd(act, small, dt_t, prow, pcol)
    y_ssd = _gate_norm_fwd(y, z, sp["ssd_norm"])
    gl = fetch("attn", y_ssd)
    w_q, w_k, w_v = _prep_attn(_from_cols(gl["w_qb"]), _from_cols(gl["w_kvb"]))
    qn, kvn, qcat, kcat, kcat_t, v = _qkv_fwd(small, w_q, w_k, w_v, sp["q_norm"], sp["kv_norm"], cos_t, sin_t)
    o, lse = _attn_fwd(qcat, kcat, v)
    y_mla = _rms_fwd(o, sp["out_norm"], name="out_norm_fwd")
    w_out = fetch("out", y_mla)["w_out"]
    w_out_s = w_out[:NCHIP // 2].reshape(SSD_INNER, D)
    w_out_m = w_out[NCHIP // 2:].reshape(SSD_INNER, D)
    mix = _mm([(y_ssd, w_out_s), (y_mla, w_out_m)], name="out_proj")
    h1, h1b = _ln_fwd(x, mix, sp["ln_mix_g"], sp["ln_mix_b"])
    gl = fetch("ffn", h1b)
    w_pg, w_pp = gl["w_pg"].reshape(D, D), _from_cols(gl["w_pp"])
    w_gate, w_up, w_down = gl["w_gate"], gl["w_up"], gl["w_down"]
    gate = _mm([(h1b, w_gate)], chunk="out", name="ffn_gate")
    up = _mm([(h1b, w_up)], chunk="out", name="ffn_up")
    actf = _swiglu_fwd(gate, up)
    ffn = _mm([(actf, w_down)], chunk="sum", name="ffn_down")
    pg = _mm([(h1b, w_pg)], name="ple_gate")
    pp = _mm([(pb, w_pp)], name="ple_proj")
    dpre2, dpre2b, dpg, dpp, dg2, db2, loss_row = _final_fwd_bwd(h1, ffn, pg, pp, target, sp["ln_ffn_g"], sp["ln_ffn_b"])

    g = {"ln_ffn_g": dg2, "ln_ffn_b": db2}
    g["w_pp"] = _to_cols(_mm([(pb, dpp)], ta=True, out_dtype=BF16, name="d_w_ple_proj"))
    g["w_pg"] = _mm([(h1b, dpg)], ta=True, out_dtype=BF16, name="d_w_ple_gate").reshape(NCHIP, D // NCHIP, D)
    g["w_down"] = _mm([(actf, dpre2b)], ta=True, chunk="out", out_dtype=BF16, name="d_w_down")
    dactf = _mm([(dpre2b, w_down)], tb=True, chunk="out", name="d_act")
    dgate, dup = _swiglu_bwd(gate, up, dactf)
    g["w_gate"] = _mm([(h1b, dgate)], ta=True, chunk="out", out_dtype=BF16, name="d_w_gate")
    g["w_up"] = _mm([(h1b, dup)], ta=True, chunk="out", out_dtype=BF16, name="d_w_up")
    sent = send("ffn", {name: g.pop(name) for name in dict(ASYNC_GROUPS)["ffn"]})
    dh1 = _mm([(dpg, w_pg)], tb=True, add=dpre2, add_scale=ALPHA, name="d_h1_ple")
    dh1 = _mm([(dgate, w_gate), (dup, w_up)], tb=True, chunk="sum", add=dh1, name="d_h1")
    dpre1, dpre1b, g["ln_mix_g"], g["ln_mix_b"] = _ln_bwd(x, mix, sp["ln_mix_g"] + sent, dh1)
    dy_ssd = _mm([(dpre1b, w_out_s)], tb=True, name="d_y_ssd")
    dy_mla = _mm([(dpre1b, w_out_m)], tb=True, name="d_y_mla")
    dw_out = jnp.concatenate([_mm([(y_ssd, dpre1b)], ta=True, out_dtype=BF16, name="d_w_out_s"),
                              _mm([(y_mla, dpre1b)], ta=True, out_dtype=BF16, name="d_w_out_m")], axis=0)
    sent = send("out", {"w_out": dw_out.reshape(NCHIP, 2 * SSD_INNER // NCHIP, D)})
    do, g["out_norm"] = _rms_bwd(o, sp["out_norm"] + sent, dy_mla, name="out_norm_bwd")
    dqt, dk, dv = _attn_bwd(qcat, kcat, kcat_t, v, do, _attn_rows(lse, o, do))
    dlatent, dqlin, dkb, g["q_norm"], g["kv_norm"] = _qkv_bwd(dqt, dk, dv, small, w_q, w_k, w_v, sp["q_norm"], sp["kv_norm"], cos_t, sin_t)
    dw_q = _mm([(qn, dqlin)], ta=True, out_dtype=BF16, name="d_w_q")
    dw_k = _mm([(kvn, dkb)], ta=True, out_dtype=BF16, name="d_w_k")
    dw_v = _mm([(kvn, dv)], ta=True, out_dtype=BF16, name="d_w_v")
    dw_qb = _to_cols(dw_q.reshape(Q_RANK, H, LANE)[:, :, :NOPE + ROPE].reshape(Q_RANK, H * (NOPE + ROPE)))
    dw_kvb = _to_cols(jnp.concatenate([dw_k.reshape(KV_RANK, H, LANE)[:, :, :NOPE], dw_v.reshape(KV_RANK, H, VDIM)],
                                       axis=2).reshape(KV_RANK, H * (NOPE + VDIM)))
    sent = send("attn", {"w_qb": dw_qb, "w_kvb": dw_kvb})
    dy, dz, g["ssd_norm"] = _gate_norm_bwd(y, z, sp["ssd_norm"] + sent, dy_ssd)
    dact, ddt, dprow = _ssd_bwd(act, small, dt_t, prow, pcol, states, dy)
    g["dt_bias"], g["A_log"], g["D"] = dprow[0:1, :H], dprow[1:2, :H], dprow[2:3, :H]
    dxbc, g["conv_w"], g["conv_b"] = _conv_bwd(xbc, sp["conv_w"], sp["conv_b"], dact)
    dsmall = jnp.concatenate([dlatent, ddt.astype(BF16)], axis=1)
    grad_x = _mm([(dz, w_z), (dxbc, w_xbc), (dsmall, w_small)], tb=True, add=dpre1, add_scale=ALPHA, name="d_x")
    dw_small = _mm([(xb, dsmall)], ta=True, out_dtype=BF16, name="d_w_small")
    dw_in = _to_cols(jnp.concatenate(
        [_mm([(xb, dz)], ta=True, out_dtype=BF16, name="d_w_z"), _mm([(xb, dxbc)], ta=True, out_dtype=BF16, name="d_w_xbc"),
         dw_small[:, SM_DT:SM_DT + H], dw_small[:, SM_Q:SM_Q + Q_RANK], dw_small[:, SM_KV:SM_KV + KV_RANK],
         dw_small[:, SM_KR:SM_KR + ROPE]], axis=1))
    return loss_row, grad_x, dw_in, g


MESH = pl.DeviceIdType.MESH
BIG = (("w_in", (D, IN_WIDTH), 1), ("w_qb", (Q_RANK, H * (NOPE + ROPE)), 1), ("w_kvb", (KV_RANK, H * (NOPE + VDIM)), 1),
       ("w_out", (2 * SSD_INNER, D), 0), ("w_gate", (D, D_FF), 1), ("w_up", (D, D_FF), 1), ("w_down", (D_FF, D), 0),
       ("w_pg", (D, D), 0), ("w_pp", (PLE, D), 1))
CONV_SHARD = SSD_XBC // NCHIP
BF16_ROWS = 16


def _from_cols(stack):
    return jnp.concatenate([stack[k] for k in range(NCHIP)], axis=1)


def _to_cols(full):
    r, c4 = full.shape
    return full.reshape(r, NCHIP, c4 // NCHIP).transpose(1, 0, 2)


def _coords():
    return lax.axis_index("x"), lax.axis_index("y"), lax.axis_index("c")


def _peers():
    x, y, c = _coords()
    return 2 * x + y, c, [(1 - x, y), (x, 1 - y), (1 - x, 1 - y)], (x, y, 1 - c)


def _half(c, rows):
    return pl.ds(pl.multiple_of(c * (rows // 2), BF16_ROWS), rows // 2)


def _gather_weights(shards):
    n_arr = len(shards)
    split = [s.shape[0] % (2 * BF16_ROWS) == 0 for s in shards]
    per = 2 * (NCHIP - 1)

    def body(*refs):
        ins, outs = refs[:n_arr], refs[n_arr:2 * n_arr]
        send_sems, recv_sems, local_sems = refs[2 * n_arr:]
        k, c, chips, sibling = _peers()

        def copy(idx, src, dst, to):
            return pltpu.make_async_remote_copy(src_ref=src, dst_ref=dst, send_sem=send_sems.at[idx], recv_sem=recv_sems.at[idx],
                                                device_id=to, device_id_type=MESH)

        def part(a, chip, core):
            return outs[a].at[chip, _half(core, shards[a].shape[0])] if split[a] else outs[a].at[chip]

        mine = [pltpu.make_async_copy(ins[a], outs[a].at[k], local_sems.at[a]) for a in range(n_arr)]
        for cp in mine:
            cp.start()
        sends = []
        for a in range(n_arr):
            src = ins[a].at[_half(c, shards[a].shape[0])] if split[a] else ins[a]
            for j, (cx, cy) in enumerate(chips):
                sends.append(copy(per * a + j, src, part(a, k, c), (cx, cy, c)))
                sends[-1].start()
        for j, (cx, cy) in enumerate(chips):
            for a in range(n_arr):
                landed = part(a, 2 * cx + cy, c)
                copy(per * a + j, landed, landed, (cx, cy, c)).wait_recv()
                if split[a]:
                    sends.append(copy(per * a + NCHIP - 1 + j, landed, landed, sibling))
                    sends[-1].start()
        for j, (cx, cy) in enumerate(chips):
            for a in range(n_arr):
                if split[a]:
                    other = part(a, 2 * cx + cy, 1 - c)
                    copy(per * a + NCHIP - 1 + j, other, other, sibling).wait_recv()
        for cp in sends:
            cp.wait_send()
        for cp in mine:
            cp.wait()

    any_spec = pl.BlockSpec(memory_space=pl.ANY)
    return pl.pallas_call(
        body, name="gather_weights", in_specs=[any_spec] * n_arr, out_specs=[any_spec] * n_arr,
        out_shape=[jax.ShapeDtypeStruct((NCHIP,) + s.shape, s.dtype) for s in shards],
        scratch_shapes=[pltpu.SemaphoreType.DMA((per * n_arr,)), pltpu.SemaphoreType.DMA((per * n_arr,)),
                        pltpu.SemaphoreType.DMA((n_arr,))],
    )(*shards)


ASYNC_GROUPS = (("attn", ("w_qb", "w_kvb")), ("out", ("w_out",)), ("ffn", ("w_gate", "w_up", "w_down", "w_pg", "w_pp")))
HBM_SPEC = pl.BlockSpec(memory_space=pltpu.HBM)
SEM_SPEC = pl.BlockSpec(memory_space=pltpu.SEMAPHORE)
IN_FLIGHT = pltpu.SideEffectType.DATAFLOW_SIDE_EFFECTING


def _in_hbm(a):
    return pltpu.with_memory_space_constraint(a, pltpu.HBM)


def _hbm_like(arrs, lead=()):
    return [pltpu.HBM(lead + a.shape, a.dtype) for a in arrs]


def _split_start(name, srcs, lands, after, n_sem, start):
    n = len(srcs)

    def body(*refs):
        src_refs, land_refs = refs[:n], refs[n:2 * n]
        send_sems, recv_sems = refs[2 * n + 1], refs[2 * n + 2]
        token = refs[-1]

        def copy(send_idx, recv_idx, src, dst, to):
            return pltpu.make_async_remote_copy(src_ref=src, dst_ref=dst, send_sem=send_sems.at[send_idx],
                                                recv_sem=recv_sems.at[recv_idx], device_id=to, device_id_type=MESH)

        for cp in start(src_refs, land_refs, copy):
            cp.start()
        token[...] = jnp.zeros_like(token)

    sem = pltpu.SemaphoreType.DMA((n_sem,))
    outs = pl.pallas_call(
        body, name=name, in_specs=[HBM_SPEC] * (2 * n) + [pl.BlockSpec(memory_space=pl.ANY)],
        out_specs=[SEM_SPEC, SEM_SPEC] + [HBM_SPEC] * (2 * n) + [pl.BlockSpec(memory_space=pltpu.VMEM)],
        out_shape=[sem, sem] + _hbm_like(srcs) + _hbm_like(lands) + [jax.ShapeDtypeStruct((8, LANE), F32)],
        input_output_aliases={i: 2 + i for i in range(2 * n)},
        compiler_params=pltpu.CompilerParams(has_side_effects=IN_FLIGHT),
    )(*[_in_hbm(a) for a in srcs], *[_in_hbm(a) for a in lands], after)
    return (outs[0], outs[1], outs[2:2 + n], outs[2 + n:2 + 2 * n]), outs[-1]


def _split_wait(name, send_sems, recv_sems, srcs, lands, after, waits):
    n = len(srcs)

    def body(*refs):
        src_refs, land_refs = refs[:n], refs[n:2 * n]
        send_ref, recv_ref = refs[2 * n], refs[2 * n + 1]

        def copy(send_idx, recv_idx, src, dst, to):
            return pltpu.make_async_remote_copy(src_ref=src, dst_ref=dst, send_sem=send_ref.at[send_idx],
                                                recv_sem=recv_ref.at[recv_idx], device_id=to, device_id_type=MESH)

        for cp in waits(src_refs, land_refs, copy):
            cp.wait_send()
            cp.wait_recv()

    outs = pl.pallas_call(
        body, name=name, in_specs=[HBM_SPEC] * (2 * n) + [SEM_SPEC, SEM_SPEC, pl.BlockSpec(memory_space=pl.ANY)],
        out_specs=[HBM_SPEC] * (2 * n), out_shape=_hbm_like(srcs) + _hbm_like(lands),
        input_output_aliases={i: i for i in range(2 * n)},
        compiler_params=pltpu.CompilerParams(has_side_effects=IN_FLIGHT),
    )(*srcs, *lands, send_sems, recv_sems, after)
    return outs[:n], outs[n:]


GATHER_LATE_SEMS = 2 * (NCHIP - 1)


def _gather_async_start(tag, shards, after):
    def start(srcs, lands, copy):
        k, c, chips, _ = _peers()
        out = []
        for a, (src, dst) in enumerate(zip(srcs, lands)):
            rows = src.shape[0]
            for j, (cx, cy) in enumerate(chips):
                for core in range(2):
                    out.append(copy(GATHER_LATE_SEMS * a + 2 * j + core, GATHER_LATE_SEMS * a + 2 * j + c,
                                    src.at[_half(c, rows)], dst.at[k, _half(c, rows)], (cx, cy, core)))
        return out

    chip = 2 * lax.axis_index("x") + lax.axis_index("y")
    lands = [lax.dynamic_update_slice(lax.empty((NCHIP,) + s.shape, s.dtype), s[None], (chip, 0, 0)) for s in shards]
    return _split_start("gather_%s_start" % tag, shards, lands, after, GATHER_LATE_SEMS * len(shards), start)


def _gather_async_wait(tag, send_sems, recv_sems, shards, lands, after):
    def waits(srcs, lands_, copy):
        _, c, chips, _ = _peers()
        out = []
        for a, (src, dst) in enumerate(zip(srcs, lands_)):
            rows = src.shape[0]
            for j, (cx, cy) in enumerate(chips):
                for core in range(2):
                    idx = GATHER_LATE_SEMS * a + 2 * j + core
                    out.append(copy(idx, idx, src.at[_half(c, rows)], dst.at[2 * cx + cy, _half(core, rows)], (cx, cy, core)))
        return out

    return _split_wait("gather_%s_wait" % tag, send_sems, recv_sems, shards, lands, after, waits)[1]


def _other_devices():
    x, y, c = _coords()
    out = []
    for d in range(1, NDEV):
        tx, ty, tc = x ^ (d >> 2), y ^ ((d >> 1) & 1), c ^ (d & 1)
        out.append((d, (tx, ty, tc), 2 * tx + ty, 4 * tx + 2 * ty + tc))
    return out


def _reduce_async_start(tag, stacks, after):
    def start(srcs, lands, copy):
        x, y, c = _coords()
        me = 4 * x + 2 * y + c
        return [copy((NDEV - 1) * a + d - 1, (NDEV - 1) * a + d - 1, src.at[chip, _half(to[2], src.shape[1])], dst.at[me], to)
                for a, (src, dst) in enumerate(zip(srcs, lands)) for d, to, chip, _ in _other_devices()]

    x, y, c = _coords()
    lands = []
    for s in stacks:
        hr = s.shape[1] // 2
        own = lax.dynamic_slice(s, (2 * x + y, c * hr, 0), (1, hr, s.shape[2]))
        lands.append(lax.dynamic_update_slice(lax.empty((NDEV, hr, s.shape[2]), s.dtype), own, (4 * x + 2 * y + c, 0, 0)))
    return _split_start("reduce_%s_start" % tag, stacks, lands, after, (NDEV - 1) * len(stacks), start)


def _reduce_async_wait(tag, send_sems, recv_sems, stacks, lands, after):
    def waits(srcs, lands_, copy):
        return [copy((NDEV - 1) * a + d - 1, (NDEV - 1) * a + d - 1, src.at[chip, _half(to[2], src.shape[1])], dst.at[pos], to)
                for a, (src, dst) in enumerate(zip(srcs, lands_)) for d, to, chip, pos in _other_devices()]

    return _split_wait("reduce_%s_wait" % tag, send_sems, recv_sems, stacks, lands, after, waits)[1]


def _reduce_finish(tag, arrived):
    n_arr = len(arrived)
    dims = [(2 * p.shape[1], p.shape[2]) for p in arrived]

    def body(*refs):
        lands, fin = refs[:n_arr], refs[n_arr:2 * n_arr]
        send_sems, recv_sems = refs[2 * n_arr:]
        _, c, _, sibling = _peers()
        sends = []
        for a in range(n_arr):
            mine = fin[a].at[_half(c, dims[a][0])]

            def device_sum(vs, vf, a=a, mine=mine):
                pltpu.sync_copy(lands[a], vs)
                acc = vs[0].astype(F32)
                for i in range(1, NDEV):
                    acc = acc + vs[i].astype(F32)
                vf[...] = acc
                pltpu.sync_copy(vf, mine)

            pl.run_scoped(device_sum, pltpu.VMEM((NDEV, dims[a][0] // 2, dims[a][1]), BF16), pltpu.VMEM((dims[a][0] // 2, dims[a][1]), F32))
            sends.append(pltpu.make_async_remote_copy(src_ref=mine, dst_ref=mine, send_sem=send_sems.at[a], recv_sem=recv_sems.at[a],
                                                      device_id=sibling, device_id_type=MESH))
            sends[-1].start()
        for a in range(n_arr):
            other = fin[a].at[_half(1 - c, dims[a][0])]
            pltpu.make_async_remote_copy(src_ref=other, dst_ref=other, send_sem=send_sems.at[a], recv_sem=recv_sems.at[a],
                                         device_id=sibling, device_id_type=MESH).wait_recv()
        for cp in sends:
            cp.wait_send()

    any_spec = pl.BlockSpec(memory_space=pl.ANY)
    return pl.pallas_call(
        body, name="reduce_%s_finish" % tag, in_specs=[any_spec] * n_arr, out_specs=[any_spec] * n_arr,
        out_shape=[jax.ShapeDtypeStruct(d, F32) for d in dims],
        scratch_shapes=[pltpu.SemaphoreType.DMA((n_arr,)), pltpu.SemaphoreType.DMA((n_arr,))],
    )(*arrived)


SMALL = (("conv_w", SSD_K * SSD_XBC), ("conv_b", SSD_XBC), ("dt_bias", H), ("A_log", H), ("D", H), ("ssd_norm", SSD_INNER),
         ("q_norm", Q_RANK), ("kv_norm", KV_RANK), ("out_norm", SSD_INNER), ("ln_mix_g", D), ("ln_mix_b", D),
         ("ln_ffn_g", D), ("ln_ffn_b", D))
SMALL_ROWS = 120
NDEV = 8


def _allreduce_small(sv):
    def body(sv_ref, out_ref, slots, send_sems, recv_sems):
        x, y, c = _coords()
        me = 4 * x + 2 * y + c
        slots[me] = sv_ref[...]
        copies = []
        for d in range(1, NDEV):
            to = (x ^ (d >> 2), y ^ ((d >> 1) & 1), c ^ (d & 1))
            copies.append(pltpu.make_async_remote_copy(src_ref=sv_ref, dst_ref=slots.at[me], send_sem=send_sems.at[d - 1],
                                                       recv_sem=recv_sems.at[d - 1], device_id=to, device_id_type=MESH))
            copies[-1].start()
        for cp in copies:
            cp.wait_recv()
        for cp in copies:
            cp.wait_send()
        acc = slots[0]
        for i in range(1, NDEV):
            acc = acc + slots[i]
        out_ref[...] = acc

    vm = pl.BlockSpec(memory_space=pltpu.VMEM)
    return pl.pallas_call(
        body, name="allreduce_small", in_specs=[vm], out_specs=vm, out_shape=jax.ShapeDtypeStruct((SMALL_ROWS, LANE), F32),
        scratch_shapes=[pltpu.VMEM((NDEV, SMALL_ROWS, LANE), F32), pltpu.SemaphoreType.DMA((NDEV - 1,)),
                        pltpu.SemaphoreType.DMA((NDEV - 1,))],
    )(sv)


def _adamw_math(w, g, m, v):
    m2 = ADAM_B1 * m + (1.0 - ADAM_B1) * g
    v2 = ADAM_B2 * v + (1.0 - ADAM_B2) * (g * g)
    m_hat = m2 / (1.0 - ADAM_B1 ** ADAM_STEP)
    v_hat = v2 / (1.0 - ADAM_B2 ** ADAM_STEP)
    return -ADAM_LR * (m_hat / (jnp.sqrt(v_hat) + ADAM_EPS) + ADAM_WD * w), m2, v2


def _adamw_big(w, g, m, v, *, name):
    r, c = w.shape
    tr = next(t for t in (512, 384, 352, 256, 128, 64, 8) if r % t == 0)

    def body(w_ref, g_ref, m_ref, v_ref, d_ref, m2_ref, v2_ref):
        d_ref[...], m2_ref[...], v2_ref[...] = _adamw_math(w_ref[...], g_ref[...], m_ref[...], v_ref[...])

    spec = pl.BlockSpec((tr, c), lambda i: (i, 0))
    return pl.pallas_call(body, name=name, grid=(r // tr,), in_specs=[spec] * 4, out_specs=[spec] * 3,
                          out_shape=[jax.ShapeDtypeStruct((r, c), F32)] * 3)(w, g, m, v)


def _adamw_small(ws, gs, ms, vs):
    n = len(ws)

    def body(*refs):
        for i in range(n):
            w_ref, g_ref, m_ref, v_ref = (refs[j * n + i] for j in range(4))
            d_ref, m2_ref, v2_ref = (refs[(4 + j) * n + i] for j in range(3))
            d_ref[...], m2_ref[...], v2_ref[...] = _adamw_math(w_ref[...], g_ref[...], m_ref[...], v_ref[...])

    vm = pl.BlockSpec(memory_space=pltpu.VMEM)
    shapes = [jax.ShapeDtypeStruct(w.shape, F32) for w in ws]
    outs = pl.pallas_call(body, name="adamw_small", in_specs=[vm] * (4 * n), out_specs=[vm] * (3 * n), out_shape=shapes * 3)(
        *ws, *gs, *ms, *vs)
    return outs[:n], outs[n:2 * n], outs[2 * n:]


_SMALL_ARG = {"conv_w": "ssd_conv_w", "conv_b": "ssd_conv_b", "dt_bias": "ssd_dt_bias", "A_log": "ssd_A_log", "D": "ssd_D",
              "ssd_norm": "ssd_norm_w", "q_norm": "mla_q_norm_w", "kv_norm": "mla_kv_norm_w", "out_norm": "mla_out_norm_w",
              "ln_mix_g": "ln_mix_g", "ln_mix_b": "ln_mix_b", "ln_ffn_g": "ln_ffn_g", "ln_ffn_b": "ln_ffn_b"}
_BIG_ARG = {"w_in": "w_in", "w_qb": "mla_w_q_b", "w_kvb": "mla_w_kv_b", "w_out": "w_out", "w_gate": "w_ffn_gate",
            "w_up": "w_ffn_up", "w_down": "w_ffn_down", "w_pg": "w_ple_gate", "w_pp": "w_ple_proj"}
_WEIGHT_ORDER = ("w_in", "ssd_conv_w", "ssd_conv_b", "ssd_dt_bias", "ssd_A_log", "ssd_D", "ssd_norm_w", "mla_q_norm_w", "mla_w_q_b",
                 "mla_kv_norm_w", "mla_w_kv_b", "mla_out_norm_w", "w_out", "ln_mix_g", "ln_mix_b", "w_ffn_gate", "w_ffn_up",
                 "w_ffn_down", "w_ple_gate", "w_ple_proj", "ln_ffn_g", "ln_ffn_b")


def _rows128(a):
    flat = a.reshape(-1)
    return jnp.pad(flat, (0, -flat.shape[0] % LANE)).reshape(-1, LANE)


def kernel(x, p, positions, w_in, ssd_conv_w, ssd_conv_b, ssd_dt_bias, ssd_A_log, ssd_D, ssd_norm_w, mla_q_norm_w, mla_w_q_b, mla_kv_norm_w, mla_w_kv_b, mla_out_norm_w, w_out, ln_mix_g, ln_mix_b, w_ffn_gate, w_ffn_up, w_ffn_down, w_ple_gate, w_ple_proj, ln_ffn_g, ln_ffn_b, loss_target, m_w_in, m_ssd_conv_w, m_ssd_conv_b, m_ssd_dt_bias, m_ssd_A_log, m_ssd_D, m_ssd_norm_w, m_mla_q_norm_w, m_mla_w_q_b, m_mla_kv_norm_w, m_mla_w_kv_b, m_mla_out_norm_w, m_w_out, m_ln_mix_g, m_ln_mix_b, m_w_ffn_gate, m_w_ffn_up, m_w_ffn_down, m_w_ple_gate, m_w_ple_proj, m_ln_ffn_g, m_ln_ffn_b, v_w_in, v_ssd_conv_w, v_ssd_conv_b, v_ssd_dt_bias, v_ssd_A_log, v_ssd_D, v_ssd_norm_w, v_mla_q_norm_w, v_mla_w_q_b, v_mla_kv_norm_w, v_mla_w_kv_b, v_mla_out_norm_w, v_w_out, v_ln_mix_g, v_ln_mix_b, v_w_ffn_gate, v_w_ffn_up, v_w_ffn_down, v_w_ple_gate, v_w_ple_proj, v_ln_ffn_g, v_ln_ffn_b):
    given = dict(locals())
    chip = 2 * lax.axis_index("x") + lax.axis_index("y")

    conv_bits = lax.bitcast_convert_type(ssd_conv_w[0], BF16).reshape(SSD_K, 2 * CONV_SHARD)
    w_in_all, conv_all = _gather_weights([w_in[0].astype(BF16), jnp.pad(conv_bits, ((0, BF16_ROWS - SSD_K), (0, 0)))])
    sp = {k: given[a] for k, a in _SMALL_ARG.items() if k != "conv_w"}
    sp["conv_w"] = _from_cols(lax.bitcast_convert_type(conv_all[:, :SSD_K].reshape(NCHIP, SSD_K, CONV_SHARD, 2), F32))
    gathering, tie = {}, w_in_all
    for group, names in ASYNC_GROUPS:
        gathering[group], tie = _gather_async_start(group, [given[_BIG_ARG[name]][0].astype(BF16) for name in names], tie)

    def fetch(group, after):
        return dict(zip(dict(ASYNC_GROUPS)[group], _gather_async_wait(group, *gathering[group], after)))

    reducing = {}

    def send(group, grads):
        reducing[group], sent = _reduce_async_start(group, [grads[name] for name in dict(ASYNC_GROUPS)[group]], grads[dict(ASYNC_GROUPS)[group][0]])
        return sent[0, 0]

    loss_row, grad_x, dw_in, g = _local_step(x[0] + tie[0, 0], p[0, 0], positions[0], loss_target[0], w_in_all, fetch, send, sp)

    reducing["in"], tie = _reduce_async_start("in", [dw_in], grad_x)
    gbig = {}
    for group, names in reversed(ASYNC_GROUPS):
        gbig.update(zip(names, _reduce_finish(group, _reduce_async_wait(group, *reducing[group], tie))))
    small_in = jnp.concatenate([_rows128(g[name]) for name, _ in SMALL] + [loss_row], axis=0)
    small_sum = _allreduce_small(jnp.pad(small_in, ((0, SMALL_ROWS - small_in.shape[0]), (0, 0))))
    gsmall, row = {}, 0
    for name, size in SMALL:
        nrow = -(-size // LANE)
        gsmall[name] = small_sum[row:row + nrow].reshape(-1)[:size]
        row += nrow
    loss = small_sum[row, 0]

    grads = {_BIG_ARG[name]: arr[None] for name, arr in gbig.items()}
    for name, _ in SMALL:
        if name == "conv_w":
            full_g = gsmall[name].reshape(SSD_K, SSD_XBC)
            grads["ssd_conv_w"] = lax.dynamic_slice(full_g, (0, chip * CONV_SHARD), (SSD_K, CONV_SHARD))[None]
        else:
            grads[_SMALL_ARG[name]] = gsmall[name].reshape(given[_SMALL_ARG[name]].shape)

    delta, new_m, new_v = {}, {}, {}

    def update_matrix(name):
        a = _BIG_ARG[name]
        d, m2, v2 = _adamw_big(given[a][0], grads[a][0], given["m_" + a][0], given["v_" + a][0], name="adamw_" + a)
        delta[a], new_m[a], new_v[a] = d[None], m2[None], v2[None]
        return d

    for name in gbig:
        last = update_matrix(name)
    grads["w_in"] = _reduce_finish("in", _reduce_async_wait("in", *reducing["in"], last))[0][None]
    update_matrix("w_in")
    small_names = [_SMALL_ARG[name] for name, _ in SMALL]
    two_d = lambda t: t.reshape(t.shape[-2], t.shape[-1])
    ds, ms, vs = _adamw_small([two_d(given[a]) for a in small_names], [two_d(grads[a]) for a in small_names],
                              [two_d(given["m_" + a]) for a in small_names], [two_d(given["v_" + a]) for a in small_names])
    for a, d, m2, v2 in zip(small_names, ds, ms, vs):
        delta[a], new_m[a], new_v[a] = (t.reshape(given[a].shape) for t in (d, m2, v2))

    return (loss, grad_x[None], *[grads[n] for n in _WEIGHT_ORDER], *[delta[n] for n in _WEIGHT_ORDER],
            *[new_m[n] for n in _WEIGHT_ORDER], *[new_v[n] for n in _WEIGHT_ORDER])
```

```python
import functools
import math

import jax
import jax.numpy as jnp
from jax import lax
from jax.experimental import pallas as pl
from jax.experimental.pallas import tpu as pltpu

F32 = jnp.float32
BF16 = jnp.bfloat16

S = 2048
D = 1024
PLE = 256
H = 16
SSD_P = 64
SSD_INNER = 1024
SSD_N = 128
SSD_G = 2
SSD_L = 128
SSD_NC = S // SSD_L
SSD_XBC = 1536
SSD_K = 4
Q_RANK = 384
KV_RANK = 256
NOPE = 64
ROPE = 32
VDIM = 64
D_FF = 2816
IN_WIDTH = 3248
ALPHA = 2.0 ** 0.25
EPS_RMS = 1e-6
EPS_LN = 1e-5
ATT_SCALE = 1.0 / math.sqrt(NOPE + ROPE)
LN2 = math.log(2.0)
ATT_SCALE_LOG2 = ATT_SCALE / LN2
LANE = 128
NCHIP = 4
SMALL_W = 896
SM_Q, SM_KV, SM_KR, SM_DT = 0, 384, 640, 768
NEG = -1e30

ADAM_LR = 0.001
ADAM_B1 = 0.9
ADAM_B2 = 0.999
ADAM_EPS = 1e-08
ADAM_WD = 0.01
ADAM_STEP = 10


def _sigmoid(v):
    return 1.0 / (1.0 + jnp.exp(-v))


MM_VMEM_BUDGET = 36 * 2 ** 20
MM_MAX_ACC = 2048 * 1024


def _mm_tiles(pairs, ta, tb, m, n, out_dtype, has_add):
    def divs(v):
        return [LANE * d for d in range(v // LANE, 0, -1) if (v // LANE) % d == 0] if v % LANE == 0 else [v]

    def cost(tm, tn):
        tot = tm * tn * (jnp.dtype(out_dtype).itemsize + (4 if has_add else 0))
        for a, b in pairs:
            k = a.shape[-2] if ta else a.shape[-1]
            tot += k * (tm * a.dtype.itemsize + tn * b.dtype.itemsize)
        return 2 * tot

    ok = [(tm * tn, tm, tn) for tm in divs(m) for tn in divs(n) if tm * tn <= MM_MAX_ACC and cost(tm, tn) <= MM_VMEM_BUDGET]
    _, tm, tn = max(ok)
    return tm, tn


def _mm(pairs, *, ta=False, tb=False, out_dtype=F32, add=None, add_scale=1.0, chunk=None, name):
    n_pairs = len(pairs)
    a0, b0 = pairs[0]
    m = a0.shape[-1] if ta else a0.shape[-2]
    n = b0.shape[-2] if tb else b0.shape[-1]
    tm, tn = _mm_tiles(pairs, ta, tb, m, n, out_dtype, add is not None)
    dims = (((0 if ta else 1,), (1 if tb else 0,)), ((), ()))
    nk = NCHIP if chunk else 1
    assert chunk != "sum" or out_dtype == F32

    def body(*refs):
        o_ref = refs[-1]
        acc = None
        for i in range(n_pairs):
            a = refs[2 * i][...].astype(BF16)
            b = refs[2 * i + 1][...].astype(BF16)
            part = lax.dot_general(a, b, dims, preferred_element_type=F32)
            acc = part if acc is None else acc + part
        if chunk == "sum":
            k = pl.program_id(2)

            @pl.when(k == 0)
            def _():
                o_ref[...] = acc + add_scale * refs[2 * n_pairs][...] if add is not None else acc

            @pl.when(k > 0)
            def _():
                o_ref[...] += acc
        else:
            if add is not None:
                acc = acc + add_scale * refs[2 * n_pairs][...]
            o_ref[...] = acc.astype(out_dtype)

    def spec(arr, shape, idx2):
        if arr.ndim == 3:
            return pl.BlockSpec((None,) + shape, lambda i, j, k: (k,) + idx2(i, j))
        return pl.BlockSpec(shape, lambda i, j, k: idx2(i, j))

    in_specs, args = [], []
    for a, b in pairs:
        kdim = a.shape[-2] if ta else a.shape[-1]
        in_specs.append(spec(a, (kdim, tm), lambda i, j: (0, i)) if ta else spec(a, (tm, kdim), lambda i, j: (i, 0)))
        in_specs.append(spec(b, (tn, kdim), lambda i, j: (j, 0)) if tb else spec(b, (kdim, tn), lambda i, j: (0, j)))
        args += [a, b]
    if add is not None:
        in_specs.append(pl.BlockSpec((tm, tn), lambda i, j, k: (i, j)))
        args.append(add)
    if chunk == "out":
        out_spec = pl.BlockSpec((None, tm, tn), lambda i, j, k: (k, i, j))
        out_shape = jax.ShapeDtypeStruct((nk, m, n), out_dtype)
    else:
        out_spec = pl.BlockSpec((tm, tn), lambda i, j, k: (i, j))
        out_shape = jax.ShapeDtypeStruct((m, n), out_dtype)
    return pl.pallas_call(
        body, name=name, grid=(m // tm, n // tn, nk), in_specs=in_specs, out_specs=out_spec, out_shape=out_shape,
        compiler_params=pltpu.CompilerParams(dimension_semantics=("parallel", "parallel", "arbitrary")),
    )(*args)


TR = 256


def _row_spec(c):
    return pl.BlockSpec((TR, c), lambda i: (i, 0))


def _vec_spec(c):
    return pl.BlockSpec((1, c), lambda i: (0, 0))


def _acc_rows(ref, val):
    @pl.when(pl.program_id(0) == 0)
    def _():
        ref[...] = jnp.zeros_like(ref)
    ref[...] += val


def _rms_fwd(u, w, *, name):
    c = u.shape[1]

    def body(u_ref, w_ref, o_ref):
        v = u_ref[...]
        r = lax.rsqrt(jnp.mean(v * v, axis=-1, keepdims=True) + EPS_RMS)
        o_ref[...] = (v * r * w_ref[...]).astype(BF16)

    return pl.pallas_call(body, name=name, grid=(S // TR,), in_specs=[_row_spec(c), _vec_spec(c)], out_specs=_row_spec(c),
                          out_shape=jax.ShapeDtypeStruct((S, c), BF16))(u, w)


def _rms_bwd(u, w, dy, *, name):
    c = u.shape[1]

    def body(u_ref, w_ref, dy_ref, du_ref, dw_ref):
        v = u_ref[...]
        g = dy_ref[...].astype(F32)
        r = lax.rsqrt(jnp.mean(v * v, axis=-1, keepdims=True) + EPS_RMS)
        gw = g * w_ref[...]
        du_ref[...] = r * gw - v * (r * r * r * jnp.mean(gw * v, axis=-1, keepdims=True))
        _acc_rows(dw_ref, jnp.sum(g * v * r, axis=0, keepdims=True))

    return pl.pallas_call(body, name=name, grid=(S // TR,), in_specs=[_row_spec(c), _vec_spec(c), _row_spec(c)],
                          out_specs=[_row_spec(c), _vec_spec(c)],
                          out_shape=[jax.ShapeDtypeStruct((S, c), F32), jax.ShapeDtypeStruct((1, c), F32)])(u, w, dy)


def _gate_norm_fwd(y, z, w):
    def body(y_ref, z_ref, w_ref, o_ref):
        zz = z_ref[...]
        v = y_ref[...] * (zz * _sigmoid(zz))
        r = lax.rsqrt(jnp.mean(v * v, axis=-1, keepdims=True) + EPS_RMS)
        o_ref[...] = (v * r * w_ref[...]).astype(BF16)

    c = SSD_INNER
    return pl.pallas_call(body, name="ssd_gate_norm_fwd", grid=(S // TR,), in_specs=[_row_spec(c), _row_spec(c), _vec_spec(c)],
                          out_specs=_row_spec(c), out_shape=jax.ShapeDtypeStruct((S, c), BF16))(y, z, w)


def _gate_norm_bwd(y, z, w, dout):
    def body(y_ref, z_ref, w_ref, g_ref, dy_ref, dz_ref, dw_ref):
        yy = y_ref[...]
        zz = z_ref[...]
        sg = _sigmoid(zz)
        sz = zz * sg
        v = yy * sz
        g = g_ref[...]
        r = lax.rsqrt(jnp.mean(v * v, axis=-1, keepdims=True) + EPS_RMS)
        gw = g * w_ref[...]
        dv = r * gw - v * (r * r * r * jnp.mean(gw * v, axis=-1, keepdims=True))
        dy_ref[...] = dv * sz
        dz_ref[...] = (dv * yy * (sg * (1.0 + zz * (1.0 - sg)))).astype(BF16)
        _acc_rows(dw_ref, jnp.sum(g * v * r, axis=0, keepdims=True))

    c = SSD_INNER
    return pl.pallas_call(body, name="ssd_gate_norm_bwd", grid=(S // TR,),
                          in_specs=[_row_spec(c), _row_spec(c), _vec_spec(c), _row_spec(c)],
                          out_specs=[_row_spec(c), _row_spec(c), _vec_spec(c)],
                          out_shape=[jax.ShapeDtypeStruct((S, c), F32), jax.ShapeDtypeStruct((S, c), BF16),
                                     jax.ShapeDtypeStruct((1, c), F32)])(y, z, w, dout)


def _ln_fwd(xr, mix, g, b):
    def body(x_ref, m_ref, g_ref, b_ref, o_ref, ob_ref):
        pre = ALPHA * x_ref[...] + m_ref[...]
        mu = jnp.mean(pre, axis=-1, keepdims=True)
        d = pre - mu
        rs = lax.rsqrt(jnp.mean(d * d, axis=-1, keepdims=True) + EPS_LN)
        h = d * rs * g_ref[...] + b_ref[...]
        o_ref[...] = h
        ob_ref[...] = h.astype(BF16)

    return pl.pallas_call(body, name="ln_mix_fwd", grid=(S // TR,), in_specs=[_row_spec(D), _row_spec(D), _vec_spec(D), _vec_spec(D)],
                          out_specs=[_row_spec(D)] * 2,
                          out_shape=[jax.ShapeDtypeStruct((S, D), F32), jax.ShapeDtypeStruct((S, D), BF16)])(xr, mix, g, b)


def _ln_bwd(xr, mix, g, dh):
    def body(x_ref, m_ref, g_ref, dh_ref, dpre_ref, dpreb_ref, dg_ref, db_ref):
        pre = ALPHA * x_ref[...] + m_ref[...]
        mu = jnp.mean(pre, axis=-1, keepdims=True)
        d = pre - mu
        rs = lax.rsqrt(jnp.mean(d * d, axis=-1, keepdims=True) + EPS_LN)
        xh = d * rs
        dy = dh_ref[...]
        gy = dy * g_ref[...]
        dpre = rs * (gy - jnp.mean(gy, axis=-1, keepdims=True) - xh * jnp.mean(gy * xh, axis=-1, keepdims=True))
        dpre_ref[...] = dpre
        dpreb_ref[...] = dpre.astype(BF16)
        _acc_rows(dg_ref, jnp.sum(dy * xh, axis=0, keepdims=True))
        _acc_rows(db_ref, jnp.sum(dy, axis=0, keepdims=True))

    return pl.pallas_call(body, name="ln_mix_bwd", grid=(S // TR,),
                          in_specs=[_row_spec(D), _row_spec(D), _vec_spec(D), _row_spec(D)],
                          out_specs=[_row_spec(D), _row_spec(D), _vec_spec(D), _vec_spec(D)],
                          out_shape=[jax.ShapeDtypeStruct((S, D), F32), jax.ShapeDtypeStruct((S, D), BF16),
                                     jax.ShapeDtypeStruct((1, D), F32), jax.ShapeDtypeStruct((1, D), F32)])(xr, mix, g, dh)


FF_CHUNK = D_FF // NCHIP


def _ff_spec():
    return pl.BlockSpec((None, TR * 2, FF_CHUNK), lambda k, i: (k, i, 0))


def _swiglu_fwd(gate, up):
    def body(g_ref, u_ref, o_ref):
        g = g_ref[...]
        o_ref[...] = (g * _sigmoid(g) * u_ref[...]).astype(BF16)

    return pl.pallas_call(body, name="swiglu_fwd", grid=(NCHIP, S // (2 * TR)), in_specs=[_ff_spec()] * 2, out_specs=_ff_spec(),
                          out_shape=jax.ShapeDtypeStruct((NCHIP, S, FF_CHUNK), BF16))(gate, up)


def _swiglu_bwd(gate, up, dact):
    def body(g_ref, u_ref, d_ref, dg_ref, du_ref):
        g = g_ref[...]
        sg = _sigmoid(g)
        d = d_ref[...]
        dg_ref[...] = (d * u_ref[...] * (sg * (1.0 + g * (1.0 - sg)))).astype(BF16)
        du_ref[...] = (d * g * sg).astype(BF16)

    return pl.pallas_call(body, name="swiglu_bwd", grid=(NCHIP, S // (2 * TR)), in_specs=[_ff_spec()] * 3, out_specs=[_ff_spec()] * 2,
                          out_shape=[jax.ShapeDtypeStruct((NCHIP, S, FF_CHUNK), BF16)] * 2)(gate, up, dact)


def _final_fwd_bwd(h1, ffn, pg, pp, target, g2, b2):
    def body(h_ref, f_ref, pg_ref, pp_ref, t_ref, g_ref, b_ref, dpre_ref, dpreb_ref, dpg_ref, dpp_ref, dg_ref, db_ref, loss_ref):
        sg = _sigmoid(pg_ref[...])
        ppv = pp_ref[...]
        pre = ALPHA * h_ref[...] + f_ref[...] + sg * ppv
        mu = jnp.mean(pre, axis=-1, keepdims=True)
        d = pre - mu
        rs = lax.rsqrt(jnp.mean(d * d, axis=-1, keepdims=True) + EPS_LN)
        xh = d * rs
        err = xh * g_ref[...] + b_ref[...] - t_ref[...]
        dy = err * (1.0 / D)
        gy = dy * g_ref[...]
        dpre = rs * (gy - jnp.mean(gy, axis=-1, keepdims=True) - xh * jnp.mean(gy * xh, axis=-1, keepdims=True))
        dpre_ref[...] = dpre
        dpreb_ref[...] = dpre.astype(BF16)
        dpg_ref[...] = (dpre * ppv * sg * (1.0 - sg)).astype(BF16)
        dpp_ref[...] = (dpre * sg).astype(BF16)
        _acc_rows(dg_ref, jnp.sum(dy * xh, axis=0, keepdims=True))
        _acc_rows(db_ref, jnp.sum(dy, axis=0, keepdims=True))
        _acc_rows(loss_ref, 0.5 * jnp.sum(jnp.mean(err * err, axis=-1, keepdims=True), axis=0, keepdims=True) * jnp.ones((1, LANE), F32))

    return pl.pallas_call(
        body, name="final_ln_loss", grid=(S // TR,),
        in_specs=[_row_spec(D)] * 5 + [_vec_spec(D)] * 2,
        out_specs=[_row_spec(D)] * 4 + [_vec_spec(D), _vec_spec(D), _vec_spec(LANE)],
        out_shape=[jax.ShapeDtypeStruct((S, D), F32)] + [jax.ShapeDtypeStruct((S, D), BF16)] * 3 + [
                   jax.ShapeDtypeStruct((1, D), F32), jax.ShapeDtypeStruct((1, D), F32), jax.ShapeDtypeStruct((1, LANE), F32)],
    )(h1, ffn, pg, pp, target, g2, b2)


def _rot(u, cos_t, sin_t, lane):
    partner = jnp.where(lane < NOPE + ROPE // 2, pltpu.roll(u, LANE - ROPE // 2, 1), pltpu.roll(u, ROPE // 2, 1))
    return u * cos_t + partner * sin_t


def _rms(v, w):
    r = lax.rsqrt(jnp.mean(v * v, axis=-1, keepdims=True) + EPS_RMS)
    return v * r * w, r


def _rms_grad(v, r, w, g):
    gw = g * w
    return r * gw - v * (r * r * r * jnp.mean(gw * v, axis=-1, keepdims=True)), jnp.sum(g * v * r, axis=0, keepdims=True)


def _whole(arr):
    return pl.BlockSpec(arr.shape, lambda i: (0,) * arr.ndim)


def _qkv_fwd(small, w_q, w_k, w_v, q_norm, kv_norm, cos_t, sin_t):
    def body(sm_ref, wq_ref, wk_ref, wv_ref, qw_ref, kw_ref, c_ref, s_ref, qn_ref, kvn_ref, q_ref, k_ref, kt_ref, v_ref):
        lane = lax.broadcasted_iota(jnp.int32, (TR, LANE), 1)
        c, s = c_ref[...], s_ref[...]
        qn = _rms(sm_ref[:, SM_Q:SM_Q + Q_RANK], qw_ref[...])[0].astype(BF16)
        kvn = _rms(sm_ref[:, SM_KV:SM_KV + KV_RANK], kw_ref[...])[0].astype(BF16)
        qn_ref[...] = qn
        kvn_ref[...] = kvn
        kr = _rot(pltpu.roll(sm_ref[:, SM_KR:SM_KR + LANE], NOPE, 1), c, s, lane)
        for h in range(H):
            tile = slice(h * LANE, (h + 1) * LANE)
            q_ref[:, tile] = _rot(_dot(qn, wq_ref[:, tile], ((1,), (0,))), c, s, lane).astype(BF16)
            kt = _dot(kvn, wk_ref[:, tile], ((1,), (0,))) + kr
            k_ref[:, tile] = kt.astype(BF16)
            kt_ref[tile, :] = kt.T.astype(BF16)
        v_ref[...] = _dot(kvn, wv_ref[...], ((1,), (0,))).astype(BF16)

    w = H * LANE
    return pl.pallas_call(
        body, name="qkv_fwd", grid=(S // TR,),
        in_specs=[_row_spec(SMALL_W), _whole(w_q), _whole(w_k), _whole(w_v), _vec_spec(Q_RANK), _vec_spec(KV_RANK), _row_spec(LANE), _row_spec(LANE)],
        out_specs=[_row_spec(Q_RANK), _row_spec(KV_RANK), _row_spec(w), _row_spec(w), pl.BlockSpec((w, TR), lambda i: (0, i)),
                   _row_spec(H * VDIM)],
        out_shape=[jax.ShapeDtypeStruct((S, Q_RANK), BF16), jax.ShapeDtypeStruct((S, KV_RANK), BF16), jax.ShapeDtypeStruct((S, w), BF16),
                   jax.ShapeDtypeStruct((S, w), BF16), jax.ShapeDtypeStruct((w, S), BF16), jax.ShapeDtypeStruct((S, H * VDIM), BF16)],
    )(small, w_q, w_k, w_v, q_norm, kv_norm, cos_t, sin_t)


def _qkv_bwd(dqt, dk, dv, small, w_q, w_k, w_v, q_norm, kv_norm, cos_t, sin_t):
    def body(dq_ref, dk_ref, dv_ref, sm_ref, wq_ref, wk_ref, wv_ref, qw_ref, kw_ref, c_ref, s_ref,
             ds_ref, dql_ref, dkb_ref, dqw_ref, dkw_ref):
        lane = lax.broadcasted_iota(jnp.int32, (TR, LANE), 1)
        c, s = c_ref[...], -s_ref[...]
        dqn = jnp.zeros((TR, Q_RANK), F32)
        dkvn = _dot(dv_ref[...], wv_ref[...], ((1,), (1,)))
        dkr = jnp.zeros((TR, LANE), F32)
        for h in range(H):
            tile = slice(h * LANE, (h + 1) * LANE)
            dql = _rot(dq_ref[tile, :].T, c, s, lane).astype(BF16)
            dql_ref[:, tile] = dql
            dqn = dqn + _dot(dql, wq_ref[:, tile], ((1,), (1,)))
            dkt = dk_ref[:, tile]
            dkb_ref[:, tile] = dkt.astype(BF16)
            dkvn = dkvn + _dot(dkt, wk_ref[:, tile], ((1,), (1,)))
            dkr = dkr + dkt
        dkr = jnp.where((lane >= NOPE) & (lane < NOPE + ROPE), dkr, 0.0)
        q_c, kv_c = sm_ref[:, SM_Q:SM_Q + Q_RANK], sm_ref[:, SM_KV:SM_KV + KV_RANK]
        dq_c, dqw = _rms_grad(q_c, _rms(q_c, qw_ref[...])[1], qw_ref[...], dqn)
        dkv_c, dkw = _rms_grad(kv_c, _rms(kv_c, kw_ref[...])[1], kw_ref[...], dkvn)
        ds_ref[:, SM_Q:SM_Q + Q_RANK] = dq_c.astype(BF16)
        ds_ref[:, SM_KV:SM_KV + KV_RANK] = dkv_c.astype(BF16)
        ds_ref[:, SM_KR:SM_KR + LANE] = pltpu.roll(_rot(dkr, c, s, lane), LANE - NOPE, 1).astype(BF16)
        _acc_rows(dqw_ref, dqw)
        _acc_rows(dkw_ref, dkw)

    w = H * LANE
    return pl.pallas_call(
        body, name="qkv_bwd", grid=(S // TR,),
        in_specs=[pl.BlockSpec((w, TR), lambda i: (0, i)), _row_spec(w), _row_spec(H * VDIM), _row_spec(SMALL_W), _whole(w_q), _whole(w_k),
                  _whole(w_v), _vec_spec(Q_RANK), _vec_spec(KV_RANK), _row_spec(LANE), _row_spec(LANE)],
        out_specs=[_row_spec(SM_DT), _row_spec(w), _row_spec(w), _vec_spec(Q_RANK), _vec_spec(KV_RANK)],
        out_shape=[jax.ShapeDtypeStruct((S, SM_DT), BF16), jax.ShapeDtypeStruct((S, w), BF16), jax.ShapeDtypeStruct((S, w), BF16),
                   jax.ShapeDtypeStruct((1, Q_RANK), F32), jax.ShapeDtypeStruct((1, KV_RANK), F32)],
    )(dqt, dk, dv, small, w_q, w_k, w_v, q_norm, kv_norm, cos_t, sin_t)


CB = 256


def _shift_down(u, k, row):
    if k == 0:
        return u
    return jnp.where(row >= k, pltpu.roll(u, k, 0), 0.0)


def _shift_up(u, k, row):
    if k == 0:
        return u
    return jnp.where(row < S - k, pltpu.roll(u, S - k, 0), 0.0)


def _conv_fwd(u, w, b):
    def body(u_ref, w_ref, b_ref, o_ref):
        row = lax.broadcasted_iota(jnp.int32, (S, CB), 0)
        uu = u_ref[...]
        acc = b_ref[...] + w_ref[SSD_K - 1:SSD_K, :] * uu
        for k in range(SSD_K - 1):
            acc = acc + w_ref[k:k + 1, :] * _shift_down(uu, SSD_K - 1 - k, row)
        o_ref[...] = acc * _sigmoid(acc)

    c = u.shape[1]
    return pl.pallas_call(
        body, name="conv_fwd", grid=(c // CB,),
        in_specs=[pl.BlockSpec((S, CB), lambda j: (0, j)), pl.BlockSpec((SSD_K, CB), lambda j: (0, j)), pl.BlockSpec((1, CB), lambda j: (0, j))],
        out_specs=pl.BlockSpec((S, CB), lambda j: (0, j)), out_shape=jax.ShapeDtypeStruct((S, c), F32),
    )(u, w, b)


def _conv_bwd(u, w, b, dact):
    def body(u_ref, w_ref, b_ref, d_ref, du_ref, dw_ref, db_ref):
        row = lax.broadcasted_iota(jnp.int32, (S, CB), 0)
        uu = u_ref[...]
        sh = [_shift_down(uu, SSD_K - 1 - k, row) for k in range(SSD_K)]
        acc = b_ref[...]
        for k in range(SSD_K):
            acc = acc + w_ref[k:k + 1, :] * sh[k]
        sg = _sigmoid(acc)
        dacc = d_ref[...] * (sg * (1.0 + acc * (1.0 - sg)))
        du = w_ref[SSD_K - 1:SSD_K, :] * dacc
        for k in range(SSD_K - 1):
            du = du + w_ref[k:k + 1, :] * _shift_up(dacc, SSD_K - 1 - k, row)
        du_ref[...] = du.astype(BF16)
        for k in range(SSD_K):
            dw_ref[k:k + 1, :] = jnp.sum(dacc * sh[k], axis=0, keepdims=True)
        db_ref[...] = jnp.sum(dacc, axis=0, keepdims=True)

    c = u.shape[1]
    col = lambda r: pl.BlockSpec((r, CB), lambda j: (0, j))
    return pl.pallas_call(
        body, name="conv_bwd", grid=(c // CB,), in_specs=[col(S), col(SSD_K), col(1), col(S)], out_specs=[col(S), col(SSD_K), col(1)],
        out_shape=[jax.ShapeDtypeStruct((S, c), BF16), jax.ShapeDtypeStruct((SSD_K, c), F32), jax.ShapeDtypeStruct((1, c), F32)],
    )(u, w, b, dact)


NPAIR = H // 2
PAIRS_PER_GROUP = NPAIR // SSD_G


def _softplus(v):
    return jnp.maximum(v, 0.0) + jnp.log(1.0 + jnp.exp(-jnp.abs(v)))


def _dot(a, b, dims):
    return lax.dot_general(a.astype(BF16), b.astype(BF16), (dims, ((), ())), preferred_element_type=F32)


def _dot3(a, b, dims, split_lhs):
    v = a if split_lhs else b
    v1 = v.astype(BF16)
    r1 = v - v1.astype(F32)
    v2 = r1.astype(BF16)
    v3 = (r1 - v2.astype(F32)).astype(BF16)
    acc = None
    for part in (v1, v2, v3):
        lhs, rhs = (part, b) if split_lhs else (a, part)
        t = lax.dot_general(lhs, rhs, (dims, ((), ())), preferred_element_type=F32)
        acc = t if acc is None else acc + t
    return acc


def _ssd_chunk_common(dt_ref, dtT_ref, prow_ref, pcol_ref):
    prow = prow_ref[...]
    pcol = pcol_ref[...]
    ri = lax.broadcasted_iota(jnp.int32, (SSD_L, SSD_L), 0)
    ci = lax.broadcasted_iota(jnp.int32, (SSD_L, SSD_L), 1)
    causal = ri >= ci
    pre_c = dt_ref[...] + prow[0:1, :]
    dtc = _softplus(pre_c)
    a_row = -jnp.exp(prow[1:2, :])
    cs_col = _dot3(causal.astype(BF16), dtc * a_row, ((1,), (0,)), False)
    dtr = _softplus(dtT_ref[...] + pcol[:, 0:1])
    a_col = -jnp.exp(pcol[:, 1:2])
    cs_row = _dot3(dtr * a_col, (ri <= ci).astype(BF16), ((1,), (0,)), True)
    return prow, causal, pre_c, dtc, a_row, cs_col, cs_row


def _ssd_fwd(act, small, dtT, prow, pcol):
    def body(x_ref, b_ref, c_ref, dt_ref, dtT_ref, prow_ref, pcol_ref, y_ref, st_ref, state):
        @pl.when(pl.program_id(0) == 0)
        def _():
            state[...] = jnp.zeros_like(state)

        prow, causal, _, dtc, _, cs_col, cs_row = _ssd_chunk_common(dt_ref, dtT_ref, prow_ref, pcol_ref)
        lo = lax.broadcasted_iota(jnp.int32, (SSD_L, LANE), 1) < SSD_P
        lo1 = lo[0:1, :]
        for g in range(SSD_G):
            bm = b_ref[:, g * SSD_N:(g + 1) * SSD_N]
            cm = c_ref[:, g * SSD_N:(g + 1) * SSD_N]
            cb = _dot(cm, bm, ((1,), (1,)))
            for qq in range(PAIRS_PER_GROUP):
                q = g * PAIRS_PER_GROUP + qq
                ha, hb = 2 * q, 2 * q + 1
                csa, csb = cs_col[:, ha:ha + 1], cs_col[:, hb:hb + 1]
                xp = x_ref[:, q * LANE:(q + 1) * LANE]
                xx = xp * jnp.where(lo, dtc[:, ha:ha + 1], dtc[:, hb:hb + 1])
                ga = cb * jnp.exp(jnp.where(causal, csa - cs_row[ha:ha + 1, :], NEG))
                gb = cb * jnp.exp(jnp.where(causal, csb - cs_row[hb:hb + 1, :], NEG))
                y = _dot(ga, jnp.where(lo, xx, 0.0), ((1,), (0,))) + _dot(gb, jnp.where(lo, 0.0, xx), ((1,), (0,)))
                s_in = state[q]
                y = y + _dot(cm, s_in, ((1,), (0,))) * jnp.where(lo, jnp.exp(csa), jnp.exp(csb))
                y = y + jnp.where(lo1, prow[2:3, ha:ha + 1], prow[2:3, hb:hb + 1]) * xp
                y_ref[:, q * LANE:(q + 1) * LANE] = y
                la, lb = csa[SSD_L - 1:SSD_L, :], csb[SSD_L - 1:SSD_L, :]
                decay = jnp.where(lo, jnp.exp(la - csa), jnp.exp(lb - csb))
                st_ref[q] = s_in
                state[q] = s_in * jnp.where(lo1, jnp.exp(la), jnp.exp(lb)) + _dot(bm, xx * decay, ((0,), (0,)))

    L = SSD_L
    return pl.pallas_call(
        body, name="ssd_fwd", grid=(SSD_NC,),
        in_specs=[pl.BlockSpec((L, SSD_INNER), lambda c: (c, 0)),
                  pl.BlockSpec((L, SSD_G * SSD_N), lambda c: (c, SSD_INNER // (SSD_G * SSD_N))),
                  pl.BlockSpec((L, SSD_G * SSD_N), lambda c: (c, SSD_INNER // (SSD_G * SSD_N) + 1)),
                  pl.BlockSpec((L, LANE), lambda c: (c, SM_DT // LANE)),
                  pl.BlockSpec((LANE, L), lambda c: (0, c)),
                  pl.BlockSpec((8, LANE), lambda c: (0, 0)), pl.BlockSpec((LANE, 8), lambda c: (0, 0))],
        out_specs=[pl.BlockSpec((L, SSD_INNER), lambda c: (c, 0)),
                   pl.BlockSpec((None, NPAIR, SSD_N, LANE), lambda c: (c, 0, 0, 0))],
        out_shape=[jax.ShapeDtypeStruct((S, SSD_INNER), F32), jax.ShapeDtypeStruct((SSD_NC, NPAIR, SSD_N, LANE), F32)],
        scratch_shapes=[pltpu.VMEM((NPAIR, SSD_N, LANE), F32)],
        compiler_params=pltpu.CompilerParams(dimension_semantics=("arbitrary",)),
    )(act, act, act, small, dtT, prow, pcol)


def _ssd_bwd(act, small, dtT, prow, pcol, states, dy):
    def body(x_ref, b_ref, c_ref, dt_ref, dtT_ref, prow_ref, pcol_ref, st_ref, dy_ref,
             dx_ref, ddt_ref, dp_ref, dstate):
        @pl.when(pl.program_id(0) == 0)
        def _():
            dstate[...] = jnp.zeros_like(dstate)
            dp_ref[...] = jnp.zeros_like(dp_ref)

        prow, causal, pre_c, dtc, a_row, cs_col, cs_row = _ssd_chunk_common(dt_ref, dtT_ref, prow_ref, pcol_ref)
        lane = lax.broadcasted_iota(jnp.int32, (SSD_L, LANE), 1)
        sub = lax.broadcasted_iota(jnp.int32, (LANE, SSD_L), 0)
        rowi = lax.broadcasted_iota(jnp.int32, (SSD_L, 1), 0)
        lane1 = lane[0:1, :]
        lo = lane < SSD_P
        lo1 = lo[0:1, :]
        dcs_c = jnp.zeros((SSD_L, LANE), F32)
        dcs_r = jnp.zeros((LANE, SSD_L), F32)
        ddt_x = jnp.zeros((SSD_L, LANE), F32)
        dd_row = jnp.zeros((1, LANE), F32)
        for g in range(SSD_G):
            bm = b_ref[:, g * SSD_N:(g + 1) * SSD_N]
            cm = c_ref[:, g * SSD_N:(g + 1) * SSD_N]
            cb = _dot(cm, bm, ((1,), (1,)))
            dcb = jnp.zeros((SSD_L, SSD_L), F32)
            dbm = jnp.zeros((SSD_L, SSD_N), F32)
            dcm = jnp.zeros((SSD_L, SSD_N), F32)
            for qq in range(PAIRS_PER_GROUP):
                q = g * PAIRS_PER_GROUP + qq
                ha, hb = 2 * q, 2 * q + 1
                csa, csb = cs_col[:, ha:ha + 1], cs_col[:, hb:hb + 1]
                xp = x_ref[:, q * LANE:(q + 1) * LANE]
                dtp = jnp.where(lo, dtc[:, ha:ha + 1], dtc[:, hb:hb + 1])
                xx = xp * dtp
                lma = jnp.exp(jnp.where(causal, csa - cs_row[ha:ha + 1, :], NEG))
                lmb = jnp.exp(jnp.where(causal, csb - cs_row[hb:hb + 1, :], NEG))
                ga, gb = cb * lma, cb * lmb
                dyp = dy_ref[:, q * LANE:(q + 1) * LANE]
                dya, dyb = jnp.where(lo, dyp, 0.0), jnp.where(lo, 0.0, dyp)
                s_in = st_ref[q]
                ds_out = dstate[q]
                la, lb = csa[SSD_L - 1:SSD_L, :], csb[SSD_L - 1:SSD_L, :]
                ecs = jnp.where(lo, jnp.exp(csa), jnp.exp(csb))
                decay = jnp.where(lo, jnp.exp(la - csa), jnp.exp(lb - csb))
                cd = jnp.where(lo1, jnp.exp(la), jnp.exp(lb))
                bds = _dot(bm, ds_out, ((1,), (0,)))
                dxx = _dot(ga, dya, ((0,), (0,))) + _dot(gb, dyb, ((0,), (0,))) + bds * decay
                dga = _dot(dya, xx, ((1,), (1,)))
                dgb = _dot(dyb, xx, ((1,), (1,)))
                dsega, dsegb = dga * ga, dgb * gb
                dcb = dcb + dga * lma + dgb * lmb
                yoff = _dot(cm, s_in, ((1,), (0,))) * ecs
                dye = dyp * ecs
                dcm = dcm + _dot(dye, s_in, ((1,), (1,)))
                xd = xx * decay
                dbm = dbm + _dot(xd, ds_out, ((1,), (1,)))
                wv = xd * bds
                t1 = dyp * yoff - wv
                col_a = (jnp.sum(dsega, axis=1, keepdims=True) + jnp.sum(jnp.where(lo, t1, 0.0), axis=1, keepdims=True))
                col_b = (jnp.sum(dsegb, axis=1, keepdims=True) + jnp.sum(jnp.where(lo, 0.0, t1), axis=1, keepdims=True))
                sprod = ds_out * s_in
                end_a = jnp.sum(jnp.where(lo, wv, 0.0), keepdims=True) + jnp.exp(la) * jnp.sum(jnp.where(lo[:SSD_N], sprod, 0.0), keepdims=True)
                end_b = jnp.sum(jnp.where(lo, 0.0, wv), keepdims=True) + jnp.exp(lb) * jnp.sum(jnp.where(lo[:SSD_N], 0.0, sprod), keepdims=True)
                col_a = col_a + jnp.where(rowi == SSD_L - 1, end_a, 0.0)
                col_b = col_b + jnp.where(rowi == SSD_L - 1, end_b, 0.0)
                dcs_c = dcs_c + jnp.where(lane == ha, col_a, 0.0) + jnp.where(lane == hb, col_b, 0.0)
                dcs_r = (dcs_r + jnp.where(sub == ha, jnp.sum(dsega, axis=0, keepdims=True), 0.0)
                         + jnp.where(sub == hb, jnp.sum(dsegb, axis=0, keepdims=True), 0.0))
                dstate[q] = _dot(cm, dye, ((0,), (0,))) + cd * ds_out
                dpair = jnp.where(lo1, prow[2:3, ha:ha + 1], prow[2:3, hb:hb + 1])
                dx_ref[:, q * LANE:(q + 1) * LANE] = dxx * dtp + dpair * dyp
                t2 = dxx * xp
                ddt_x = (ddt_x + jnp.where(lane == ha, jnp.sum(jnp.where(lo, t2, 0.0), axis=1, keepdims=True), 0.0)
                         + jnp.where(lane == hb, jnp.sum(jnp.where(lo, 0.0, t2), axis=1, keepdims=True), 0.0))
                t3 = dyp * xp
                dd_row = (dd_row + jnp.where(lane1 == ha, jnp.sum(jnp.where(lo, t3, 0.0), keepdims=True), 0.0)
                          + jnp.where(lane1 == hb, jnp.sum(jnp.where(lo, 0.0, t3), keepdims=True), 0.0))
            dx_ref[:, SSD_INNER + g * SSD_N:SSD_INNER + (g + 1) * SSD_N] = dbm + _dot(dcb, cm, ((0,), (0,)))
            dx_ref[:, SSD_INNER + (SSD_G + g) * SSD_N:SSD_INNER + (SSD_G + g + 1) * SSD_N] = dcm + _dot(dcb, bm, ((1,), (0,)))
        ri = lax.broadcasted_iota(jnp.int32, (SSD_L, SSD_L), 0)
        ci = lax.broadcasted_iota(jnp.int32, (SSD_L, SSD_L), 1)
        da = _dot3((ri <= ci).astype(BF16), dcs_c, ((1,), (0,)), False)
        da = da - _dot3(dcs_r, causal.astype(BF16), ((1,), (0,)), True).T
        ddt = ddt_x + da * a_row
        ddt_raw = ddt * _sigmoid(pre_c)
        ddt_ref[...] = ddt_raw
        da_head = jnp.sum(da * dtc, axis=0, keepdims=True) * a_row
        dp_ref[0:1, :] += jnp.sum(ddt_raw, axis=0, keepdims=True)
        dp_ref[1:2, :] += da_head
        dp_ref[2:3, :] += dd_row

    L = SSD_L
    rev = SSD_NC - 1
    bc_cols = SSD_INNER // (SSD_G * SSD_N)
    return pl.pallas_call(
        body, name="ssd_bwd", grid=(SSD_NC,),
        in_specs=[pl.BlockSpec((L, SSD_INNER), lambda c: (rev - c, 0)),
                  pl.BlockSpec((L, SSD_G * SSD_N), lambda c: (rev - c, bc_cols)),
                  pl.BlockSpec((L, SSD_G * SSD_N), lambda c: (rev - c, bc_cols + 1)),
                  pl.BlockSpec((L, LANE), lambda c: (rev - c, SM_DT // LANE)),
                  pl.BlockSpec((LANE, L), lambda c: (0, rev - c)),
                  pl.BlockSpec((8, LANE), lambda c: (0, 0)), pl.BlockSpec((LANE, 8), lambda c: (0, 0)),
                  pl.BlockSpec((None, NPAIR, SSD_N, LANE), lambda c: (rev - c, 0, 0, 0)),
                  pl.BlockSpec((L, SSD_INNER), lambda c: (rev - c, 0))],
        out_specs=[pl.BlockSpec((L, SSD_XBC), lambda c: (rev - c, 0)),
                   pl.BlockSpec((L, LANE), lambda c: (rev - c, 0)),
                   pl.BlockSpec((8, LANE), lambda c: (0, 0))],
        out_shape=[jax.ShapeDtypeStruct((S, SSD_XBC), F32), jax.ShapeDtypeStruct((S, LANE), F32),
                   jax.ShapeDtypeStruct((8, LANE), F32)],
        scratch_shapes=[pltpu.VMEM((NPAIR, SSD_N, LANE), F32)],
        compiler_params=pltpu.CompilerParams(dimension_semantics=("arbitrary",)),
    )(act, act, act, small, dtT, prow, pcol, states, dy)


TQ = 256
TK = 256
FWD_TQ = 256
FWD_TK = 256


def _attn_fwd(qc, kc, v):
    TQ, TK = FWD_TQ, FWD_TK

    def body(q_ref, k_ref, v_ref, o_ref, lse_ref):
        i = pl.program_id(1)
        lo = lax.broadcasted_iota(jnp.int32, (TQ, LANE), 1) < VDIM
        lo_k = lax.broadcasted_iota(jnp.int32, (TK, LANE), 1) < VDIM
        row_minus_col = lax.broadcasted_iota(jnp.int32, (TQ, TK), 0) - lax.broadcasted_iota(jnp.int32, (TQ, TK), 1)
        qa, qb = q_ref[:, 0:LANE], q_ref[:, LANE:2 * LANE]

        def scores(kb):
            kk = k_ref[pl.ds(pl.multiple_of(kb * TK, TK), TK), :]
            return (_dot(qa, kk[:, 0:LANE], ((1,), (1,))) * ATT_SCALE_LOG2, _dot(qb, kk[:, LANE:2 * LANE], ((1,), (1,))) * ATT_SCALE_LOG2)

        def update(kb, sa, sb, stats):
            ma, la, mb, lb, acc = stats
            vv = v_ref[pl.ds(pl.multiple_of(kb * TK, TK), TK), :]
            na = jnp.maximum(ma, jnp.max(sa, axis=1, keepdims=True))
            nb = jnp.maximum(mb, jnp.max(sb, axis=1, keepdims=True))
            pa, pb = jnp.exp2(sa - na), jnp.exp2(sb - nb)
            fa, fb = jnp.exp2(ma - na), jnp.exp2(mb - nb)
            la = fa * la + jnp.sum(pa, axis=1, keepdims=True)
            lb = fb * lb + jnp.sum(pb, axis=1, keepdims=True)
            acc = (acc * jnp.where(lo, fa, fb) + _dot(pa, jnp.where(lo_k, vv, 0), ((1,), (0,)))
                   + _dot(pb, jnp.where(lo_k, 0, vv), ((1,), (0,))))
            return na, la, nb, lb, acc

        def step(kb, carry):
            sa, sb = carry[:2]
            nxt = scores(kb + 1)
            return nxt + update(kb, sa, sb, carry[2:])

        neg = jnp.full((TQ, 1), NEG, F32)
        zero = jnp.zeros((TQ, 1), F32)
        n_full = i * (TQ // TK)
        carry = lax.fori_loop(0, n_full, step, scores(0) + (neg, zero, neg, zero, jnp.zeros((TQ, LANE), F32)))
        s, stats = carry[:2], carry[2:]
        for d in range(TQ // TK):
            nxt = scores(n_full + d + 1) if d + 1 < TQ // TK else None
            sa, sb = (jnp.where(row_minus_col >= d * TK, t, NEG) for t in s)
            stats = update(n_full + d, sa, sb, stats)
            s = nxt
        ma, la, mb, lb, acc = stats
        o_ref[...] = acc / jnp.where(lo, la, lb)
        lse_ref[...] = jnp.where(lo, ma + jnp.log2(la), mb + jnp.log2(lb)) * LN2

    return pl.pallas_call(
        body, name="attn_fwd", grid=(NPAIR, S // TQ),
        in_specs=[pl.BlockSpec((TQ, 2 * LANE), lambda j, i: (i, j)), pl.BlockSpec((S, 2 * LANE), lambda j, i: (0, j)),
                  pl.BlockSpec((S, LANE), lambda j, i: (0, j))],
        out_specs=[pl.BlockSpec((TQ, LANE), lambda j, i: (i, j)), pl.BlockSpec((None, TQ, LANE), lambda j, i: (j, i, 0))],
        out_shape=[jax.ShapeDtypeStruct((S, H * VDIM), F32), jax.ShapeDtypeStruct((NPAIR, S, LANE), F32)],
        compiler_params=pltpu.CompilerParams(dimension_semantics=("parallel", "parallel")),
    )(qc, kc, v)


def _attn_rows(lse, o, do):
    def body(lse_ref, o_ref, do_ref, r_ref):
        lt = lse_ref[...].T * (1.0 / LN2)
        tt = (o_ref[...] * do_ref[...]).T
        r_ref[...] = jnp.zeros_like(r_ref)
        r_ref[0:1, :] = lt[0:1, :]
        r_ref[1:2, :] = lt[VDIM:VDIM + 1, :]
        r_ref[2:3, :] = jnp.sum(tt[0:VDIM, :], axis=0, keepdims=True)
        r_ref[3:4, :] = jnp.sum(tt[VDIM:LANE, :], axis=0, keepdims=True)

    tile = pl.BlockSpec((S, LANE), lambda j: (0, j))
    return pl.pallas_call(
        body, name="attn_rows", grid=(NPAIR,), in_specs=[pl.BlockSpec((None, S, LANE), lambda j: (j, 0, 0)), tile, tile],
        out_specs=pl.BlockSpec((None, 8, S), lambda j: (j, 0, 0)), out_shape=jax.ShapeDtypeStruct((NPAIR, 8, S), F32),
    )(lse, o, do)


def _attn_bwd(qc, kc, kct, v, do, rows):
    nq = S // TQ

    def body(q_ref, k_ref, kt_ref, v_ref, do_ref, r_ref, dqt_ref, dk_ref, dv_ref):
        kb = pl.program_id(1)

        @pl.when(kb == 0)
        def _():
            dqt_ref[...] = jnp.zeros_like(dqt_ref)

        lo = lax.broadcasted_iota(jnp.int32, (TK, LANE), 1) < VDIM
        q_minus_k = lax.broadcasted_iota(jnp.int32, (TK, TQ), 1) - lax.broadcasted_iota(jnp.int32, (TK, TQ), 0)
        vv = v_ref[...]
        kk = k_ref[...]

        def step(qi, carry):
            off = pl.multiple_of(qi * TQ, TQ)
            qq = q_ref[pl.ds(off, TQ), :]
            dd = do_ref[pl.ds(off, TQ), :].astype(BF16)
            rr = r_ref[:, pl.ds(off, TQ)]
            keep = q_minus_k >= (kb - qi) * TQ
            out = []
            for x in range(2):
                sel = lo if x == 0 else jnp.logical_not(lo)
                kx, qx = kk[:, x * LANE:(x + 1) * LANE], qq[:, x * LANE:(x + 1) * LANE]
                st = jnp.where(keep, _dot(kx, qx, ((1,), (1,))) * ATT_SCALE_LOG2, NEG)
                pt = jnp.exp2(st - rr[x:x + 1, :])
                dpt = _dot(jnp.where(sel, vv, 0), dd, ((1,), (1,)))
                dst = (pt * (dpt - rr[2 + x:3 + x, :]) * ATT_SCALE).astype(BF16)
                out.append(carry[x] + _dot(dst, qx, ((1,), (0,))))
                out.append(_dot(pt, jnp.where(sel, dd, 0), ((1,), (0,))))
                dqt_ref[x * LANE:(x + 1) * LANE, pl.ds(off, TQ)] += _dot(kt_ref[x * LANE:(x + 1) * LANE, :], dst, ((1,), (0,)))
            return out[0], out[2], carry[2] + out[1] + out[3]

        z = jnp.zeros((TK, LANE), F32)
        dka, dkb, dv = lax.fori_loop(kb, nq, step, (z, z, z))
        dk_ref[:, 0:LANE] = dka
        dk_ref[:, LANE:2 * LANE] = dkb
        dv_ref[...] = dv.astype(BF16)

    return pl.pallas_call(
        body, name="attn_bwd", grid=(NPAIR, S // TK),
        in_specs=[pl.BlockSpec((S, 2 * LANE), lambda j, k: (0, j)), pl.BlockSpec((TK, 2 * LANE), lambda j, k: (k, j)),
                  pl.BlockSpec((2 * LANE, TK), lambda j, k: (j, k)), pl.BlockSpec((TK, LANE), lambda j, k: (k, j)),
                  pl.BlockSpec((S, LANE), lambda j, k: (0, j)), pl.BlockSpec((None, 8, S), lambda j, k: (j, 0, 0))],
        out_specs=[pl.BlockSpec((2 * LANE, S), lambda j, k: (j, 0)), pl.BlockSpec((TK, 2 * LANE), lambda j, k: (k, j)),
                   pl.BlockSpec((TK, LANE), lambda j, k: (k, j))],
        out_shape=[jax.ShapeDtypeStruct((H * LANE, S), F32), jax.ShapeDtypeStruct((S, H * LANE), F32),
                   jax.ShapeDtypeStruct((S, H * VDIM), BF16)],
        compiler_params=pltpu.CompilerParams(dimension_semantics=("parallel", "arbitrary")),
    )(qc, kc, kct, v, do, rows)


_IN_Z, _IN_XBC, _IN_DT, _IN_Q, _IN_KV, _IN_KR = 0, 1024, 2560, 2576, 2960, 3216


def _prep_in(w_in_t):
    dt = w_in_t.dtype
    w_small_t = jnp.concatenate(
        [w_in_t[_IN_Q:_IN_KV], w_in_t[_IN_KV:_IN_KR], w_in_t[_IN_KR:IN_WIDTH], jnp.zeros((LANE - ROPE, D), dt),
         w_in_t[_IN_DT:_IN_Q], jnp.zeros((LANE - H, D), dt)], axis=0)
    return w_in_t[_IN_Z:_IN_XBC], w_in_t[_IN_XBC:_IN_DT], w_small_t


def _prep_attn(w_qb, w_kvb):
    w_q = jnp.pad(w_qb.reshape(Q_RANK, H, NOPE + ROPE), ((0, 0), (0, 0), (0, LANE - NOPE - ROPE))).reshape(Q_RANK, H * LANE)
    kv3 = w_kvb.reshape(KV_RANK, H, NOPE + VDIM)
    w_k = jnp.pad(kv3[:, :, :NOPE], ((0, 0), (0, 0), (0, LANE - NOPE))).reshape(KV_RANK, H * LANE)
    w_v = kv3[:, :, NOPE:].reshape(KV_RANK, H * VDIM)
    return w_q, w_k, w_v


def _rope_tables(positions):
    inv_freq = 1.0 / (10000.0 ** (jnp.arange(0, ROPE, 2, dtype=F32) / ROPE))
    ang = positions.astype(F32).reshape(S, 1) * inv_freq
    cos, sin = jnp.cos(ang), jnp.sin(ang)
    cos_t = jnp.concatenate([jnp.ones((S, NOPE), F32), cos, cos, jnp.ones((S, LANE - NOPE - ROPE), F32)], axis=1)
    sin_t = jnp.concatenate([jnp.zeros((S, NOPE), F32), -sin, sin, jnp.zeros((S, LANE - NOPE - ROPE), F32)], axis=1)
    return cos_t, sin_t


def _local_step(x, p, positions, target, w_in, fetch, send, sp):
    w_z, w_xbc, w_small = _prep_in(w_in.reshape(IN_WIDTH, D))
    cos_t, sin_t = _rope_tables(positions)
    prow = jnp.zeros((8, LANE), F32).at[0, :H].set(sp["dt_bias"][0]).at[1, :H].set(sp["A_log"][0]).at[2, :H].set(sp["D"][0])
    pcol = prow.T

    xb, pb = x.astype(BF16), p.astype(BF16)
    z = _mm([(xb, w_z)], tb=True, name="proj_z")
    xbc = _mm([(xb, w_xbc)], tb=True, name="proj_xbc")
    small = _mm([(xb, w_small)], tb=True, name="proj_small")
    act = _conv_fwd(xbc, sp["conv_w"], sp["conv_b"])
    dt_t = small[:, SM_DT:SM_DT + LANE].T
    y, states = _ssd_fwd(act, small, dt_t, prow, pcol)
    y_ssd = _gate_norm_fwd(y, z, sp["ssd_norm"])
    gl = fetch("attn", y_ssd)
    w_q, w_k, w_v = _prep_attn(_from_cols(gl["w_qb"]), _from_cols(gl["w_kvb"]))
    qn, kvn, qcat, kcat, kcat_t, v = _qkv_fwd(small, w_q, w_k, w_v, sp["q_norm"], sp["kv_norm"], cos_t, sin_t)
    o, lse = _attn_fwd(qcat, kcat, v)
    y_mla = _rms_fwd(o, sp["out_norm"], name="out_norm_fwd")
    w_out = fetch("out", y_mla)["w_out"]
    w_out_s = w_out[:NCHIP // 2].reshape(SSD_INNER, D)
    w_out_m = w_out[NCHIP // 2:].reshape(SSD_INNER, D)
    mix = _mm([(y_ssd, w_out_s), (y_mla, w_out_m)], name="out_proj")
    h1, h1b = _ln_fwd(x, mix, sp["ln_mix_g"], sp["ln_mix_b"])
    gl = fetch("ffn", h1b)
    w_pg, w_pp = gl["w_pg"].reshape(D, D), _from_cols(gl["w_pp"])
    w_gate, w_up, w_down = gl["w_gate"], gl["w_up"], gl["w_down"]
    gate = _mm([(h1b, w_gate)], tb=True, chunk="out", name="ffn_gate")
    up = _mm([(h1b, w_up)], tb=True, chunk="out", name="ffn_up")
    actf = _swiglu_fwd(gate, up)
    ffn = _mm([(actf, w_down)], chunk="sum", name="ffn_down")
    pg = _mm([(h1b, w_pg)], name="ple_gate")
    pp = _mm([(pb, w_pp)], name="ple_proj")
    dpre2, dpre2b, dpg, dpp, dg2, db2, loss_row = _final_fwd_bwd(h1, ffn, pg, pp, target, sp["ln_ffn_g"], sp["ln_ffn_b"])

    g = {"ln_ffn_g": dg2, "ln_ffn_b": db2}
    g["w_pp"] = _to_cols(_mm([(pb, dpp)], ta=True, out_dtype=BF16, name="d_w_ple_proj"))
    g["w_pg"] = _mm([(h1b, dpg)], ta=True, out_dtype=BF16, name="d_w_ple_gate").reshape(NCHIP, D // NCHIP, D)
    g["w_down"] = _mm([(actf, dpre2b)], ta=True, chunk="out", out_dtype=BF16, name="d_w_down")
    dactf = _mm([(dpre2b, w_down)], tb=True, chunk="out", name="d_act")
    dgate, dup = _swiglu_bwd(gate, up, dactf)
    g["w_gate"] = _mm([(dgate, h1b)], ta=True, chunk="out", out_dtype=BF16, name="d_w_gate")
    g["w_up"] = _mm([(dup, h1b)], ta=True, chunk="out", out_dtype=BF16, name="d_w_up")
    sent = send("ffn", {name: g.pop(name) for name in dict(ASYNC_GROUPS)["ffn"]})
    dh1 = _mm([(dpg, w_pg)], tb=True, add=dpre2, add_scale=ALPHA, name="d_h1_ple")
    dh1 = _mm([(dgate, w_gate), (dup, w_up)], chunk="sum", add=dh1, name="d_h1")
    dpre1, dpre1b, g["ln_mix_g"], g["ln_mix_b"] = _ln_bwd(x, mix, sp["ln_mix_g"] + sent, dh1)
    dy_ssd = _mm([(dpre1b, w_out_s)], tb=True, name="d_y_ssd")
    dy_mla = _mm([(dpre1b, w_out_m)], tb=True, name="d_y_mla")
    dw_out = jnp.concatenate([_mm([(y_ssd, dpre1b)], ta=True, out_dtype=BF16, name="d_w_out_s"),
                              _mm([(y_mla, dpre1b)], ta=True, out_dtype=BF16, name="d_w_out_m")], axis=0)
    sent = send("out", {"w_out": dw_out.reshape(NCHIP, 2 * SSD_INNER // NCHIP, D)})
    do, g["out_norm"] = _rms_bwd(o, sp["out_norm"] + sent, dy_mla, name="out_norm_bwd")
    dqt, dk, dv = _attn_bwd(qcat, kcat, kcat_t, v, do, _attn_rows(lse, o, do))
    dlatent, dqlin, dkb, g["q_norm"], g["kv_norm"] = _qkv_bwd(dqt, dk, dv, small, w_q, w_k, w_v, sp["q_norm"], sp["kv_norm"], cos_t, sin_t)
    dw_q = _mm([(qn, dqlin)], ta=True, out_dtype=BF16, name="d_w_q")
    dw_k = _mm([(kvn, dkb)], ta=True, out_dtype=BF16, name="d_w_k")
    dw_v = _mm([(kvn, dv)], ta=True, out_dtype=BF16, name="d_w_v")
    dw_qb = _to_cols(dw_q.reshape(Q_RANK, H, LANE)[:, :, :NOPE + ROPE].reshape(Q_RANK, H * (NOPE + ROPE)))
    dw_kvb = _to_cols(jnp.concatenate([dw_k.reshape(KV_RANK, H, LANE)[:, :, :NOPE], dw_v.reshape(KV_RANK, H, VDIM)],
                                       axis=2).reshape(KV_RANK, H * (NOPE + VDIM)))
    sent = send("attn", {"w_qb": dw_qb, "w_kvb": dw_kvb})
    dy, dz, g["ssd_norm"] = _gate_norm_bwd(y, z, sp["ssd_norm"] + sent, dy_ssd)
    dact, ddt, dprow = _ssd_bwd(act, small, dt_t, prow, pcol, states, dy)
    g["dt_bias"], g["A_log"], g["D"] = dprow[0:1, :H], dprow[1:2, :H], dprow[2:3, :H]
    dxbc, g["conv_w"], g["conv_b"] = _conv_bwd(xbc, sp["conv_w"], sp["conv_b"], dact)
    dsmall = jnp.concatenate([dlatent, ddt.astype(BF16)], axis=1)
    grad_x = _mm([(dz, w_z), (dxbc, w_xbc), (dsmall, w_small)], add=dpre1, add_scale=ALPHA, name="d_x")
    dw_small = _mm([(dsmall, xb)], ta=True, out_dtype=BF16, name="d_w_small")
    dw_in = jnp.concatenate(
        [_mm([(dz, xb)], ta=True, out_dtype=BF16, name="d_w_z"), _mm([(dxbc, xb)], ta=True, out_dtype=BF16, name="d_w_xbc"),
         dw_small[SM_DT:SM_DT + H], dw_small[SM_Q:SM_Q + Q_RANK], dw_small[SM_KV:SM_KV + KV_RANK], dw_small[SM_KR:SM_KR + ROPE]],
        axis=0).reshape(NCHIP, IN_WIDTH // NCHIP, D)
    return loss_row, grad_x, dw_in, g


MESH = pl.DeviceIdType.MESH
BIG = (("w_in", (D, IN_WIDTH), 1), ("w_qb", (Q_RANK, H * (NOPE + ROPE)), 1), ("w_kvb", (KV_RANK, H * (NOPE + VDIM)), 1),
       ("w_out", (2 * SSD_INNER, D), 0), ("w_gate", (D, D_FF), 1), ("w_up", (D, D_FF), 1), ("w_down", (D_FF, D), 0),
       ("w_pg", (D, D), 0), ("w_pp", (PLE, D), 1))
CONV_SHARD = SSD_XBC // NCHIP
BF16_ROWS = 16


def _from_cols(stack):
    return jnp.concatenate([stack[k] for k in range(NCHIP)], axis=1)


def _to_cols(full):
    r, c4 = full.shape
    return full.reshape(r, NCHIP, c4 // NCHIP).transpose(1, 0, 2)


def _coords():
    return lax.axis_index("x"), lax.axis_index("y"), lax.axis_index("c")


def _peers():
    x, y, c = _coords()
    return 2 * x + y, c, [(1 - x, y), (x, 1 - y), (1 - x, 1 - y)], (x, y, 1 - c)


def _half_axis(shape):
    return 0 if shape[-2] % (2 * BF16_ROWS) == 0 else 1


def _half_shape(shape):
    r, c = shape[-2:]
    return (r // 2, c) if _half_axis(shape) == 0 else (r, c // 2)


def _half(core, shape):
    r, c = shape[-2:]
    if _half_axis(shape) == 0:
        return pl.ds(pl.multiple_of(core * (r // 2), BF16_ROWS), r // 2), slice(None)
    return slice(None), pl.ds(pl.multiple_of(core * (c // 2), LANE), c // 2)


def _gather_weights(shards):
    n_arr = len(shards)
    per = 2 * (NCHIP - 1)

    def body(*refs):
        ins, outs = refs[:n_arr], refs[n_arr:2 * n_arr]
        send_sems, recv_sems, local_sems = refs[2 * n_arr:]
        k, c, chips, sibling = _peers()

        def copy(idx, src, dst, to):
            return pltpu.make_async_remote_copy(src_ref=src, dst_ref=dst, send_sem=send_sems.at[idx], recv_sem=recv_sems.at[idx],
                                                device_id=to, device_id_type=MESH)

        def part(a, chip, core):
            return outs[a].at[chip, *_half(core, shards[a].shape)]

        mine = [pltpu.make_async_copy(ins[a], outs[a].at[k], local_sems.at[a]) for a in range(n_arr)]
        for cp in mine:
            cp.start()
        sends = []
        for a in range(n_arr):
            for j, (cx, cy) in enumerate(chips):
                sends.append(copy(per * a + j, ins[a].at[*_half(c, shards[a].shape)], part(a, k, c), (cx, cy, c)))
                sends[-1].start()
        for j, (cx, cy) in enumerate(chips):
            for a in range(n_arr):
                landed = part(a, 2 * cx + cy, c)
                copy(per * a + j, landed, landed, (cx, cy, c)).wait_recv()
                sends.append(copy(per * a + NCHIP - 1 + j, landed, landed, sibling))
                sends[-1].start()
        for j, (cx, cy) in enumerate(chips):
            for a in range(n_arr):
                other = part(a, 2 * cx + cy, 1 - c)
                copy(per * a + NCHIP - 1 + j, other, other, sibling).wait_recv()
        for cp in sends:
            cp.wait_send()
        for cp in mine:
            cp.wait()

    any_spec = pl.BlockSpec(memory_space=pl.ANY)
    return pl.pallas_call(
        body, name="gather_weights", in_specs=[any_spec] * n_arr, out_specs=[any_spec] * n_arr,
        out_shape=[jax.ShapeDtypeStruct((NCHIP,) + s.shape, s.dtype) for s in shards],
        scratch_shapes=[pltpu.SemaphoreType.DMA((per * n_arr,)), pltpu.SemaphoreType.DMA((per * n_arr,)),
                        pltpu.SemaphoreType.DMA((n_arr,))],
    )(*shards)


ASYNC_GROUPS = (("attn", ("w_qb", "w_kvb")), ("out", ("w_out",)), ("ffn", ("w_gate", "w_up", "w_down", "w_pg", "w_pp")))
TRANSPOSED = ("w_in", "w_gate", "w_up")
HBM_SPEC = pl.BlockSpec(memory_space=pltpu.HBM)
SEM_SPEC = pl.BlockSpec(memory_space=pltpu.SEMAPHORE)
IN_FLIGHT = pltpu.SideEffectType.DATAFLOW_SIDE_EFFECTING


def _in_hbm(a):
    return pltpu.with_memory_space_constraint(a, pltpu.HBM)


def _hbm_like(arrs, lead=()):
    return [pltpu.HBM(lead + a.shape, a.dtype) for a in arrs]


def _split_start(name, srcs, lands, after, n_sem, start):
    n = len(srcs)

    def body(*refs):
        src_refs, land_refs = refs[:n], refs[n:2 * n]
        send_sems, recv_sems = refs[2 * n + 1], refs[2 * n + 2]
        token = refs[-1]

        def copy(send_idx, recv_idx, src, dst, to):
            return pltpu.make_async_remote_copy(src_ref=src, dst_ref=dst, send_sem=send_sems.at[send_idx],
                                                recv_sem=recv_sems.at[recv_idx], device_id=to, device_id_type=MESH)

        for cp in start(src_refs, land_refs, copy):
            cp.start()
        token[...] = jnp.zeros_like(token)

    sem = pltpu.SemaphoreType.DMA((n_sem,))
    outs = pl.pallas_call(
        body, name=name, in_specs=[HBM_SPEC] * (2 * n) + [pl.BlockSpec(memory_space=pl.ANY)],
        out_specs=[SEM_SPEC, SEM_SPEC] + [HBM_SPEC] * (2 * n) + [pl.BlockSpec(memory_space=pltpu.VMEM)],
        out_shape=[sem, sem] + _hbm_like(srcs) + _hbm_like(lands) + [jax.ShapeDtypeStruct((8, LANE), F32)],
        input_output_aliases={i: 2 + i for i in range(2 * n)},
        compiler_params=pltpu.CompilerParams(has_side_effects=IN_FLIGHT),
    )(*[_in_hbm(a) for a in srcs], *[_in_hbm(a) for a in lands], after)
    return (outs[0], outs[1], outs[2:2 + n], outs[2 + n:2 + 2 * n]), outs[-1]


def _split_wait(name, send_sems, recv_sems, srcs, lands, after, waits):
    n = len(srcs)

    def body(*refs):
        src_refs, land_refs = refs[:n], refs[n:2 * n]
        send_ref, recv_ref = refs[2 * n], refs[2 * n + 1]

        def copy(send_idx, recv_idx, src, dst, to):
            return pltpu.make_async_remote_copy(src_ref=src, dst_ref=dst, send_sem=send_ref.at[send_idx],
                                                recv_sem=recv_ref.at[recv_idx], device_id=to, device_id_type=MESH)

        for cp in waits(src_refs, land_refs, copy):
            cp.wait_send()
            cp.wait_recv()

    outs = pl.pallas_call(
        body, name=name, in_specs=[HBM_SPEC] * (2 * n) + [SEM_SPEC, SEM_SPEC, pl.BlockSpec(memory_space=pl.ANY)],
        out_specs=[HBM_SPEC] * (2 * n), out_shape=_hbm_like(srcs) + _hbm_like(lands),
        input_output_aliases={i: i for i in range(2 * n)},
        compiler_params=pltpu.CompilerParams(has_side_effects=IN_FLIGHT),
    )(*srcs, *lands, send_sems, recv_sems, after)
    return outs[:n], outs[n:]


GATHER_LATE_SEMS = 2 * (NCHIP - 1)


def _gather_async_start(tag, shards, after):
    def start(srcs, lands, copy):
        k, c, chips, _ = _peers()
        out = []
        for a, (src, dst) in enumerate(zip(srcs, lands)):
            for j, (cx, cy) in enumerate(chips):
                for core in range(2):
                    out.append(copy(GATHER_LATE_SEMS * a + 2 * j + core, GATHER_LATE_SEMS * a + 2 * j + c,
                                    src.at[*_half(c, src.shape)], dst.at[k, *_half(c, src.shape)], (cx, cy, core)))
        return out

    chip = 2 * lax.axis_index("x") + lax.axis_index("y")
    lands = [lax.dynamic_update_slice(lax.empty((NCHIP,) + s.shape, s.dtype), s[None], (chip, 0, 0)) for s in shards]
    return _split_start("gather_%s_start" % tag, shards, lands, after, GATHER_LATE_SEMS * len(shards), start)


def _gather_async_wait(tag, send_sems, recv_sems, shards, lands, after):
    def waits(srcs, lands_, copy):
        _, c, chips, _ = _peers()
        out = []
        for a, (src, dst) in enumerate(zip(srcs, lands_)):
            for j, (cx, cy) in enumerate(chips):
                for core in range(2):
                    idx = GATHER_LATE_SEMS * a + 2 * j + core
                    out.append(copy(idx, idx, src.at[*_half(c, src.shape)], dst.at[2 * cx + cy, *_half(core, src.shape)], (cx, cy, core)))
        return out

    return _split_wait("gather_%s_wait" % tag, send_sems, recv_sems, shards, lands, after, waits)[1]


def _other_devices():
    x, y, c = _coords()
    out = []
    for d in range(1, NDEV):
        tx, ty, tc = x ^ (d >> 2), y ^ ((d >> 1) & 1), c ^ (d & 1)
        out.append((d, (tx, ty, tc), 2 * tx + ty, 4 * tx + 2 * ty + tc))
    return out


def _reduce_async_start(tag, stacks, after):
    def start(srcs, lands, copy):
        x, y, c = _coords()
        me = 4 * x + 2 * y + c
        return [copy((NDEV - 1) * a + d - 1, (NDEV - 1) * a + d - 1, src.at[chip, *_half(to[2], src.shape)], dst.at[me], to)
                for a, (src, dst) in enumerate(zip(srcs, lands)) for d, to, chip, _ in _other_devices()]

    x, y, c = _coords()
    lands = []
    for s in stacks:
        hr, hc = _half_shape(s.shape)
        at = (c * hr, 0) if _half_axis(s.shape) == 0 else (0, c * hc)
        own = lax.dynamic_slice(s, (2 * x + y,) + at, (1, hr, hc))
        lands.append(lax.dynamic_update_slice(lax.empty((NDEV, hr, hc), s.dtype), own, (4 * x + 2 * y + c, 0, 0)))
    return _split_start("reduce_%s_start" % tag, stacks, lands, after, (NDEV - 1) * len(stacks), start)


def _reduce_async_wait(tag, send_sems, recv_sems, stacks, lands, after):
    def waits(srcs, lands_, copy):
        return [copy((NDEV - 1) * a + d - 1, (NDEV - 1) * a + d - 1, src.at[chip, *_half(to[2], src.shape)], dst.at[pos], to)
                for a, (src, dst) in enumerate(zip(srcs, lands_)) for d, to, chip, pos in _other_devices()]

    return _split_wait("reduce_%s_wait" % tag, send_sems, recv_sems, stacks, lands, after, waits)[1]


def _reduce_finish(tag, arrived, dims):
    n_arr = len(arrived)

    def body(*refs):
        lands, fin = refs[:n_arr], refs[n_arr:2 * n_arr]
        send_sems, recv_sems = refs[2 * n_arr:]
        _, c, _, sibling = _peers()
        sends = []
        for a in range(n_arr):
            mine = fin[a].at[*_half(c, dims[a])]

            def device_sum(vs, vf, a=a, mine=mine):
                pltpu.sync_copy(lands[a], vs)
                acc = vs[0].astype(F32)
                for i in range(1, NDEV):
                    acc = acc + vs[i].astype(F32)
                vf[...] = acc
                pltpu.sync_copy(vf, mine)

            pl.run_scoped(device_sum, pltpu.VMEM((NDEV,) + _half_shape(dims[a]), BF16), pltpu.VMEM(_half_shape(dims[a]), F32))
            sends.append(pltpu.make_async_remote_copy(src_ref=mine, dst_ref=mine, send_sem=send_sems.at[a], recv_sem=recv_sems.at[a],
                                                      device_id=sibling, device_id_type=MESH))
            sends[-1].start()
        for a in range(n_arr):
            other = fin[a].at[*_half(1 - c, dims[a])]
            pltpu.make_async_remote_copy(src_ref=other, dst_ref=other, send_sem=send_sems.at[a], recv_sem=recv_sems.at[a],
                                         device_id=sibling, device_id_type=MESH).wait_recv()
        for cp in sends:
            cp.wait_send()

    any_spec = pl.BlockSpec(memory_space=pl.ANY)
    return pl.pallas_call(
        body, name="reduce_%s_finish" % tag, in_specs=[any_spec] * n_arr, out_specs=[any_spec] * n_arr,
        out_shape=[jax.ShapeDtypeStruct(d, F32) for d in dims],
        scratch_shapes=[pltpu.SemaphoreType.DMA((n_arr,)), pltpu.SemaphoreType.DMA((n_arr,))],
    )(*arrived)


SMALL = (("conv_w", SSD_K * SSD_XBC), ("conv_b", SSD_XBC), ("dt_bias", H), ("A_log", H), ("D", H), ("ssd_norm", SSD_INNER),
         ("q_norm", Q_RANK), ("kv_norm", KV_RANK), ("out_norm", SSD_INNER), ("ln_mix_g", D), ("ln_mix_b", D),
         ("ln_ffn_g", D), ("ln_ffn_b", D))
SMALL_ROWS = 120
NDEV = 8


def _allreduce_small(sv):
    def body(sv_ref, out_ref, slots, send_sems, recv_sems):
        x, y, c = _coords()
        me = 4 * x + 2 * y + c
        slots[me] = sv_ref[...]
        copies = []
        for d in range(1, NDEV):
            to = (x ^ (d >> 2), y ^ ((d >> 1) & 1), c ^ (d & 1))
            copies.append(pltpu.make_async_remote_copy(src_ref=sv_ref, dst_ref=slots.at[me], send_sem=send_sems.at[d - 1],
                                                       recv_sem=recv_sems.at[d - 1], device_id=to, device_id_type=MESH))
            copies[-1].start()
        for cp in copies:
            cp.wait_recv()
        for cp in copies:
            cp.wait_send()
        acc = slots[0]
        for i in range(1, NDEV):
            acc = acc + slots[i]
        out_ref[...] = acc

    vm = pl.BlockSpec(memory_space=pltpu.VMEM)
    return pl.pallas_call(
        body, name="allreduce_small", in_specs=[vm], out_specs=vm, out_shape=jax.ShapeDtypeStruct((SMALL_ROWS, LANE), F32),
        scratch_shapes=[pltpu.VMEM((NDEV, SMALL_ROWS, LANE), F32), pltpu.SemaphoreType.DMA((NDEV - 1,)),
                        pltpu.SemaphoreType.DMA((NDEV - 1,))],
    )(sv)


def _adamw_math(w, g, m, v):
    m2 = ADAM_B1 * m + (1.0 - ADAM_B1) * g
    v2 = ADAM_B2 * v + (1.0 - ADAM_B2) * (g * g)
    m_hat = m2 / (1.0 - ADAM_B1 ** ADAM_STEP)
    v_hat = v2 / (1.0 - ADAM_B2 ** ADAM_STEP)
    return -ADAM_LR * (m_hat / (jnp.sqrt(v_hat) + ADAM_EPS) + ADAM_WD * w), m2, v2


def _adamw_big(w, g, m, v, *, name):
    r, c = w.shape

    def body(w_ref, g_ref, m_ref, v_ref, d_ref, m2_ref, v2_ref):
        d_ref[...], m2_ref[...], v2_ref[...] = _adamw_math(w_ref[...], g_ref[...], m_ref[...], v_ref[...])

    if r % 8 == 0:
        tr = next(t for t in (512, 384, 352, 256, 128, 64, 8) if r % t == 0)
        steps, spec = r // tr, pl.BlockSpec((tr, c), lambda i: (i, 0))
    else:
        steps, spec = c // (2 * LANE), pl.BlockSpec((r, 2 * LANE), lambda i: (0, i))
    return pl.pallas_call(body, name=name, grid=(steps,), in_specs=[spec] * 4, out_specs=[spec] * 3,
                          out_shape=[jax.ShapeDtypeStruct((r, c), F32)] * 3)(w, g, m, v)


def _adamw_small(ws, gs, ms, vs):
    n = len(ws)

    def body(*refs):
        for i in range(n):
            w_ref, g_ref, m_ref, v_ref = (refs[j * n + i] for j in range(4))
            d_ref, m2_ref, v2_ref = (refs[(4 + j) * n + i] for j in range(3))
            d_ref[...], m2_ref[...], v2_ref[...] = _adamw_math(w_ref[...], g_ref[...], m_ref[...], v_ref[...])

    vm = pl.BlockSpec(memory_space=pltpu.VMEM)
    shapes = [jax.ShapeDtypeStruct(w.shape, F32) for w in ws]
    outs = pl.pallas_call(body, name="adamw_small", in_specs=[vm] * (4 * n), out_specs=[vm] * (3 * n), out_shape=shapes * 3)(
        *ws, *gs, *ms, *vs)
    return outs[:n], outs[n:2 * n], outs[2 * n:]


_SMALL_ARG = {"conv_w": "ssd_conv_w", "conv_b": "ssd_conv_b", "dt_bias": "ssd_dt_bias", "A_log": "ssd_A_log", "D": "ssd_D",
              "ssd_norm": "ssd_norm_w", "q_norm": "mla_q_norm_w", "kv_norm": "mla_kv_norm_w", "out_norm": "mla_out_norm_w",
              "ln_mix_g": "ln_mix_g", "ln_mix_b": "ln_mix_b", "ln_ffn_g": "ln_ffn_g", "ln_ffn_b": "ln_ffn_b"}
_BIG_ARG = {"w_in": "w_in", "w_qb": "mla_w_q_b", "w_kvb": "mla_w_kv_b", "w_out": "w_out", "w_gate": "w_ffn_gate",
            "w_up": "w_ffn_up", "w_down": "w_ffn_down", "w_pg": "w_ple_gate", "w_pp": "w_ple_proj"}
_WEIGHT_ORDER = ("w_in", "ssd_conv_w", "ssd_conv_b", "ssd_dt_bias", "ssd_A_log", "ssd_D", "ssd_norm_w", "mla_q_norm_w", "mla_w_q_b",
                 "mla_kv_norm_w", "mla_w_kv_b", "mla_out_norm_w", "w_out", "ln_mix_g", "ln_mix_b", "w_ffn_gate", "w_ffn_up",
                 "w_ffn_down", "w_ple_gate", "w_ple_proj", "ln_ffn_g", "ln_ffn_b")


def _rows128(a):
    flat = a.reshape(-1)
    return jnp.pad(flat, (0, -flat.shape[0] % LANE)).reshape(-1, LANE)


def kernel(x, p, positions, w_in, ssd_conv_w, ssd_conv_b, ssd_dt_bias, ssd_A_log, ssd_D, ssd_norm_w, mla_q_norm_w, mla_w_q_b, mla_kv_norm_w, mla_w_kv_b, mla_out_norm_w, w_out, ln_mix_g, ln_mix_b, w_ffn_gate, w_ffn_up, w_ffn_down, w_ple_gate, w_ple_proj, ln_ffn_g, ln_ffn_b, loss_target, m_w_in, m_ssd_conv_w, m_ssd_conv_b, m_ssd_dt_bias, m_ssd_A_log, m_ssd_D, m_ssd_norm_w, m_mla_q_norm_w, m_mla_w_q_b, m_mla_kv_norm_w, m_mla_w_kv_b, m_mla_out_norm_w, m_w_out, m_ln_mix_g, m_ln_mix_b, m_w_ffn_gate, m_w_ffn_up, m_w_ffn_down, m_w_ple_gate, m_w_ple_proj, m_ln_ffn_g, m_ln_ffn_b, v_w_in, v_ssd_conv_w, v_ssd_conv_b, v_ssd_dt_bias, v_ssd_A_log, v_ssd_D, v_ssd_norm_w, v_mla_q_norm_w, v_mla_w_q_b, v_mla_kv_norm_w, v_mla_w_kv_b, v_mla_out_norm_w, v_w_out, v_ln_mix_g, v_ln_mix_b, v_w_ffn_gate, v_w_ffn_up, v_w_ffn_down, v_w_ple_gate, v_w_ple_proj, v_ln_ffn_g, v_ln_ffn_b):
    given = dict(locals())
    chip = 2 * lax.axis_index("x") + lax.axis_index("y")

    def local(name, prefix=""):
        a = given[prefix + _BIG_ARG[name]][0]
        return a.T if name in TRANSPOSED else a

    def global_layout(name, arr):
        return (arr.T if name in TRANSPOSED else arr)[None]

    conv_bits = lax.bitcast_convert_type(ssd_conv_w[0], BF16).reshape(SSD_K, 2 * CONV_SHARD)
    w_in_all, conv_all = _gather_weights([local("w_in").astype(BF16), jnp.pad(conv_bits, ((0, BF16_ROWS - SSD_K), (0, 0)))])
    sp = {k: given[a] for k, a in _SMALL_ARG.items() if k != "conv_w"}
    sp["conv_w"] = _from_cols(lax.bitcast_convert_type(conv_all[:, :SSD_K].reshape(NCHIP, SSD_K, CONV_SHARD, 2), F32))
    gathering, tie = {}, w_in_all
    for group, names in ASYNC_GROUPS:
        gathering[group], tie = _gather_async_start(group, [local(name).astype(BF16) for name in names], tie)

    def fetch(group, after):
        return dict(zip(dict(ASYNC_GROUPS)[group], _gather_async_wait(group, *gathering[group], after)))

    reducing = {}

    def send(group, grads):
        reducing[group], sent = _reduce_async_start(group, [grads[name] for name in dict(ASYNC_GROUPS)[group]], grads[dict(ASYNC_GROUPS)[group][0]])
        return sent[0, 0]

    loss_row, grad_x, dw_in, g = _local_step(x[0] + tie[0, 0], p[0, 0], positions[0], loss_target[0], w_in_all, fetch, send, sp)

    reducing["in"], tie = _reduce_async_start("in", [dw_in], grad_x)
    gbig = {}
    for group, names in reversed(ASYNC_GROUPS):
        arrived = _reduce_async_wait(group, *reducing[group], tie)
        gbig.update(zip(names, _reduce_finish(group, arrived, [local(name).shape for name in names])))
    small_in = jnp.concatenate([_rows128(g[name]) for name, _ in SMALL] + [loss_row], axis=0)
    small_sum = _allreduce_small(jnp.pad(small_in, ((0, SMALL_ROWS - small_in.shape[0]), (0, 0))))
    gsmall, row = {}, 0
    for name, size in SMALL:
        nrow = -(-size // LANE)
        gsmall[name] = small_sum[row:row + nrow].reshape(-1)[:size]
        row += nrow
    loss = small_sum[row, 0]

    grads = {_BIG_ARG[name]: global_layout(name, arr) for name, arr in gbig.items()}
    for name, _ in SMALL:
        if name == "conv_w":
            full_g = gsmall[name].reshape(SSD_K, SSD_XBC)
            grads["ssd_conv_w"] = lax.dynamic_slice(full_g, (0, chip * CONV_SHARD), (SSD_K, CONV_SHARD))[None]
        else:
            grads[_SMALL_ARG[name]] = gsmall[name].reshape(given[_SMALL_ARG[name]].shape)

    delta, new_m, new_v = {}, {}, {}

    def update_matrix(name, grad):
        a = _BIG_ARG[name]
        d, m2, v2 = _adamw_big(local(name), grad, local(name, "m_"), local(name, "v_"), name="adamw_" + a)
        delta[a], new_m[a], new_v[a] = (global_layout(name, t) for t in (d, m2, v2))
        return d

    for name, grad in gbig.items():
        last = update_matrix(name, grad)
    g_in = _reduce_finish("in", _reduce_async_wait("in", *reducing["in"], last), [local("w_in").shape])[0]
    grads["w_in"] = global_layout("w_in", g_in)
    update_matrix("w_in", g_in)
    small_names = [_SMALL_ARG[name] for name, _ in SMALL]
    two_d = lambda t: t.reshape(t.shape[-2], t.shape[-1])
    ds, ms, vs = _adamw_small([two_d(given[a]) for a in small_names], [two_d(grads[a]) for a in small_names],
                              [two_d(given["m_" + a]) for a in small_names], [two_d(given["v_" + a]) for a in small_names])
    for a, d, m2, v2 in zip(small_names, ds, ms, vs):
        delta[a], new_m[a], new_v[a] = (t.reshape(given[a].shape) for t in (d, m2, v2))

    return (loss, grad_x[None], *[grads[n] for n in _WEIGHT_ORDER], *[delta[n] for n in _WEIGHT_ORDER],
            *[new_m[n] for n in _WEIGHT_ORDER], *[new_v[n] for n in _WEIGHT_ORDER])
```

```python
import functools
import math

import jax
import jax.numpy as jnp
from jax import lax
from jax.experimental import pallas as pl
from jax.experimental.pallas import tpu as pltpu

F32 = jnp.float32
BF16 = jnp.bfloat16

S = 2048
D = 1024
PLE = 256
H = 16
SSD_P = 64
SSD_INNER = 1024
SSD_N = 128
SSD_G = 2
SSD_L = 128
SSD_NC = S // SSD_L
SSD_XBC = 1536
SSD_K = 4
Q_RANK = 384
KV_RANK = 256
NOPE = 64
ROPE = 32
VDIM = 64
D_FF = 2816
IN_WIDTH = 3248
ALPHA = 2.0 ** 0.25
EPS_RMS = 1e-6
EPS_LN = 1e-5
ATT_SCALE = 1.0 / math.sqrt(NOPE + ROPE)
LN2 = math.log(2.0)
ATT_SCALE_LOG2 = ATT_SCALE / LN2
LANE = 128
NCHIP = 4
SMALL_W = 896
SM_Q, SM_KV, SM_KR, SM_DT = 0, 384, 640, 768
NEG = -1e30

ADAM_LR = 0.001
ADAM_B1 = 0.9
ADAM_B2 = 0.999
ADAM_EPS = 1e-08
ADAM_WD = 0.01
ADAM_STEP = 10


def _sigmoid(v):
    return 1.0 / (1.0 + jnp.exp(-v))


MM_VMEM_BUDGET = 36 * 2 ** 20
MM_MAX_ACC = 2048 * 1024


def _mm_tiles(pairs, ta, tb, m, n, out_dtype, has_add):
    def divs(v):
        return [LANE * d for d in range(v // LANE, 0, -1) if (v // LANE) % d == 0] if v % LANE == 0 else [v]

    def cost(tm, tn):
        tot = tm * tn * (jnp.dtype(out_dtype).itemsize + (4 if has_add else 0))
        for a, b in pairs:
            k = a.shape[-2] if ta else a.shape[-1]
            tot += k * (tm * a.dtype.itemsize + tn * b.dtype.itemsize)
        return 2 * tot

    ok = [(tm * tn, tm, tn) for tm in divs(m) for tn in divs(n) if tm * tn <= MM_MAX_ACC and cost(tm, tn) <= MM_VMEM_BUDGET]
    _, tm, tn = max(ok)
    return tm, tn


def _mm(pairs, *, ta=False, tb=False, out_dtype=F32, add=None, add_scale=1.0, chunk=None, name):
    n_pairs = len(pairs)
    a0, b0 = pairs[0]
    m = a0.shape[-1] if ta else a0.shape[-2]
    n = b0.shape[-2] if tb else b0.shape[-1]
    tm, tn = _mm_tiles(pairs, ta, tb, m, n, out_dtype, add is not None)
    dims = (((0 if ta else 1,), (1 if tb else 0,)), ((), ()))
    nk = NCHIP if chunk else 1
    assert chunk != "sum" or out_dtype == F32

    def body(*refs):
        o_ref = refs[-1]
        acc = None
        for i in range(n_pairs):
            a = refs[2 * i][...].astype(BF16)
            b = refs[2 * i + 1][...].astype(BF16)
            part = lax.dot_general(a, b, dims, preferred_element_type=F32)
            acc = part if acc is None else acc + part
        if chunk == "sum":
            k = pl.program_id(2)

            @pl.when(k == 0)
            def _():
                o_ref[...] = acc + add_scale * refs[2 * n_pairs][...] if add is not None else acc

            @pl.when(k > 0)
            def _():
                o_ref[...] += acc
        else:
            if add is not None:
                acc = acc + add_scale * refs[2 * n_pairs][...]
            o_ref[...] = acc.astype(out_dtype)

    def spec(arr, shape, idx2):
        if arr.ndim == 3:
            return pl.BlockSpec((None,) + shape, lambda i, j, k: (k,) + idx2(i, j))
        return pl.BlockSpec(shape, lambda i, j, k: idx2(i, j))

    in_specs, args = [], []
    for a, b in pairs:
        kdim = a.shape[-2] if ta else a.shape[-1]
        in_specs.append(spec(a, (kdim, tm), lambda i, j: (0, i)) if ta else spec(a, (tm, kdim), lambda i, j: (i, 0)))
        in_specs.append(spec(b, (tn, kdim), lambda i, j: (j, 0)) if tb else spec(b, (kdim, tn), lambda i, j: (0, j)))
        args += [a, b]
    if add is not None:
        in_specs.append(pl.BlockSpec((tm, tn), lambda i, j, k: (i, j)))
        args.append(add)
    if chunk == "out":
        out_spec = pl.BlockSpec((None, tm, tn), lambda i, j, k: (k, i, j))
        out_shape = jax.ShapeDtypeStruct((nk, m, n), out_dtype)
    else:
        out_spec = pl.BlockSpec((tm, tn), lambda i, j, k: (i, j))
        out_shape = jax.ShapeDtypeStruct((m, n), out_dtype)
    return pl.pallas_call(
        body, name=name, grid=(m // tm, n // tn, nk), in_specs=in_specs, out_specs=out_spec, out_shape=out_shape,
        compiler_params=pltpu.CompilerParams(dimension_semantics=("parallel", "parallel", "arbitrary")),
    )(*args)


TR = 256


def _row_spec(c):
    return pl.BlockSpec((TR, c), lambda i: (i, 0))


def _vec_spec(c):
    return pl.BlockSpec((1, c), lambda i: (0, 0))


def _acc_rows(ref, val):
    @pl.when(pl.program_id(0) == 0)
    def _():
        ref[...] = jnp.zeros_like(ref)
    ref[...] += val


def _rms_fwd(u, w, *, name):
    c = u.shape[1]

    def body(u_ref, w_ref, o_ref):
        v = u_ref[...]
        r = lax.rsqrt(jnp.mean(v * v, axis=-1, keepdims=True) + EPS_RMS)
        o_ref[...] = (v * r * w_ref[...]).astype(BF16)

    return pl.pallas_call(body, name=name, grid=(S // TR,), in_specs=[_row_spec(c), _vec_spec(c)], out_specs=_row_spec(c),
                          out_shape=jax.ShapeDtypeStruct((S, c), BF16))(u, w)


def _rms_bwd(u, w, dy, *, name):
    c = u.shape[1]

    def body(u_ref, w_ref, dy_ref, du_ref, dw_ref):
        v = u_ref[...]
        g = dy_ref[...].astype(F32)
        r = lax.rsqrt(jnp.mean(v * v, axis=-1, keepdims=True) + EPS_RMS)
        gw = g * w_ref[...]
        du_ref[...] = r * gw - v * (r * r * r * jnp.mean(gw * v, axis=-1, keepdims=True))
        _acc_rows(dw_ref, jnp.sum(g * v * r, axis=0, keepdims=True))

    return pl.pallas_call(body, name=name, grid=(S // TR,), in_specs=[_row_spec(c), _vec_spec(c), _row_spec(c)],
                          out_specs=[_row_spec(c), _vec_spec(c)],
                          out_shape=[jax.ShapeDtypeStruct((S, c), F32), jax.ShapeDtypeStruct((1, c), F32)])(u, w, dy)


def _gate_norm_fwd(y, z, w):
    def body(y_ref, z_ref, w_ref, o_ref):
        zz = z_ref[...]
        v = y_ref[...] * (zz * _sigmoid(zz))
        r = lax.rsqrt(jnp.mean(v * v, axis=-1, keepdims=True) + EPS_RMS)
        o_ref[...] = (v * r * w_ref[...]).astype(BF16)

    c = SSD_INNER
    return pl.pallas_call(body, name="ssd_gate_norm_fwd", grid=(S // TR,), in_specs=[_row_spec(c), _row_spec(c), _vec_spec(c)],
                          out_specs=_row_spec(c), out_shape=jax.ShapeDtypeStruct((S, c), BF16))(y, z, w)


def _gate_norm_bwd(y, z, w, dout):
    def body(y_ref, z_ref, w_ref, g_ref, dy_ref, dz_ref, dw_ref):
        yy = y_ref[...]
        zz = z_ref[...]
        sg = _sigmoid(zz)
        sz = zz * sg
        v = yy * sz
        g = g_ref[...]
        r = lax.rsqrt(jnp.mean(v * v, axis=-1, keepdims=True) + EPS_RMS)
        gw = g * w_ref[...]
        dv = r * gw - v * (r * r * r * jnp.mean(gw * v, axis=-1, keepdims=True))
        dy_ref[...] = dv * sz
        dz_ref[...] = (dv * yy * (sg * (1.0 + zz * (1.0 - sg)))).astype(BF16)
        _acc_rows(dw_ref, jnp.sum(g * v * r, axis=0, keepdims=True))

    c = SSD_INNER
    return pl.pallas_call(body, name="ssd_gate_norm_bwd", grid=(S // TR,),
                          in_specs=[_row_spec(c), _row_spec(c), _vec_spec(c), _row_spec(c)],
                          out_specs=[_row_spec(c), _row_spec(c), _vec_spec(c)],
                          out_shape=[jax.ShapeDtypeStruct((S, c), F32), jax.ShapeDtypeStruct((S, c), BF16),
                                     jax.ShapeDtypeStruct((1, c), F32)])(y, z, w, dout)


def _ln_fwd(xr, mix, g, b):
    def body(x_ref, m_ref, g_ref, b_ref, o_ref, ob_ref):
        pre = ALPHA * x_ref[...] + m_ref[...]
        mu = jnp.mean(pre, axis=-1, keepdims=True)
        d = pre - mu
        rs = lax.rsqrt(jnp.mean(d * d, axis=-1, keepdims=True) + EPS_LN)
        h = d * rs * g_ref[...] + b_ref[...]
        o_ref[...] = h
        ob_ref[...] = h.astype(BF16)

    return pl.pallas_call(body, name="ln_mix_fwd", grid=(S // TR,), in_specs=[_row_spec(D), _row_spec(D), _vec_spec(D), _vec_spec(D)],
                          out_specs=[_row_spec(D)] * 2,
                          out_shape=[jax.ShapeDtypeStruct((S, D), F32), jax.ShapeDtypeStruct((S, D), BF16)])(xr, mix, g, b)


def _ln_bwd(xr, mix, g, dh):
    def body(x_ref, m_ref, g_ref, dh_ref, dpre_ref, dpreb_ref, dg_ref, db_ref):
        pre = ALPHA * x_ref[...] + m_ref[...]
        mu = jnp.mean(pre, axis=-1, keepdims=True)
        d = pre - mu
        rs = lax.rsqrt(jnp.mean(d * d, axis=-1, keepdims=True) + EPS_LN)
        xh = d * rs
        dy = dh_ref[...]
        gy = dy * g_ref[...]
        dpre = rs * (gy - jnp.mean(gy, axis=-1, keepdims=True) - xh * jnp.mean(gy * xh, axis=-1, keepdims=True))
        dpre_ref[...] = dpre
        dpreb_ref[...] = dpre.astype(BF16)
        _acc_rows(dg_ref, jnp.sum(dy * xh, axis=0, keepdims=True))
        _acc_rows(db_ref, jnp.sum(dy, axis=0, keepdims=True))

    return pl.pallas_call(body, name="ln_mix_bwd", grid=(S // TR,),
                          in_specs=[_row_spec(D), _row_spec(D), _vec_spec(D), _row_spec(D)],
                          out_specs=[_row_spec(D), _row_spec(D), _vec_spec(D), _vec_spec(D)],
                          out_shape=[jax.ShapeDtypeStruct((S, D), F32), jax.ShapeDtypeStruct((S, D), BF16),
                                     jax.ShapeDtypeStruct((1, D), F32), jax.ShapeDtypeStruct((1, D), F32)])(xr, mix, g, dh)


FF_CHUNK = D_FF // NCHIP


FF_ROWS = 1024


def _ff_act_spec():
    return pl.BlockSpec((None, FF_ROWS, FF_CHUNK), lambda i, k: (k, i, 0))


def _ff_w_spec():
    return pl.BlockSpec((None, FF_CHUNK, D), lambda i, k: (k, 0, 0))


def _ffn_hidden_fwd(h, w_gate_t, w_up_t):
    def body(h_ref, wg_ref, wu_ref, g_ref, u_ref, a_ref):
        hh = h_ref[...]
        g = _dot(hh, wg_ref[...], ((1,), (1,)))
        u = _dot(hh, wu_ref[...], ((1,), (1,)))
        g_ref[...] = g.astype(BF16)
        u_ref[...] = u.astype(BF16)
        a_ref[...] = (g * _sigmoid(g) * u).astype(BF16)

    return pl.pallas_call(
        body, name="ffn_hidden_fwd", grid=(S // FF_ROWS, NCHIP),
        in_specs=[pl.BlockSpec((FF_ROWS, D), lambda i, k: (i, 0)), _ff_w_spec(), _ff_w_spec()], out_specs=[_ff_act_spec()] * 3,
        out_shape=[jax.ShapeDtypeStruct((NCHIP, S, FF_CHUNK), BF16)] * 3,
        compiler_params=pltpu.CompilerParams(dimension_semantics=("parallel", "parallel")),
    )(h, w_gate_t, w_up_t)


def _ffn_hidden_bwd(dout, w_down, gate, up):
    def body(d_ref, wd_ref, g_ref, u_ref, dg_ref, du_ref):
        d = _dot(d_ref[...], wd_ref[...], ((1,), (1,)))
        g = g_ref[...].astype(F32)
        sg = _sigmoid(g)
        dg_ref[...] = (d * u_ref[...].astype(F32) * (sg * (1.0 + g * (1.0 - sg)))).astype(BF16)
        du_ref[...] = (d * g * sg).astype(BF16)

    return pl.pallas_call(
        body, name="ffn_hidden_bwd", grid=(S // FF_ROWS, NCHIP),
        in_specs=[pl.BlockSpec((FF_ROWS, D), lambda i, k: (i, 0)), _ff_w_spec(), _ff_act_spec(), _ff_act_spec()],
        out_specs=[_ff_act_spec()] * 2, out_shape=[jax.ShapeDtypeStruct((NCHIP, S, FF_CHUNK), BF16)] * 2,
        compiler_params=pltpu.CompilerParams(dimension_semantics=("parallel", "parallel")),
    )(dout, w_down, gate, up)


def _final_fwd_bwd(h1, ffn, pg, pp, target, g2, b2):
    def body(h_ref, f_ref, pg_ref, pp_ref, t_ref, g_ref, b_ref, dpre_ref, dpreb_ref, dpg_ref, dpp_ref, dg_ref, db_ref, loss_ref):
        sg = _sigmoid(pg_ref[...])
        ppv = pp_ref[...]
        pre = ALPHA * h_ref[...] + f_ref[...] + sg * ppv
        mu = jnp.mean(pre, axis=-1, keepdims=True)
        d = pre - mu
        rs = lax.rsqrt(jnp.mean(d * d, axis=-1, keepdims=True) + EPS_LN)
        xh = d * rs
        err = xh * g_ref[...] + b_ref[...] - t_ref[...]
        dy = err * (1.0 / D)
        gy = dy * g_ref[...]
        dpre = rs * (gy - jnp.mean(gy, axis=-1, keepdims=True) - xh * jnp.mean(gy * xh, axis=-1, keepdims=True))
        dpre_ref[...] = dpre
        dpreb_ref[...] = dpre.astype(BF16)
        dpg_ref[...] = (dpre * ppv * sg * (1.0 - sg)).astype(BF16)
        dpp_ref[...] = (dpre * sg).astype(BF16)
        _acc_rows(dg_ref, jnp.sum(dy * xh, axis=0, keepdims=True))
        _acc_rows(db_ref, jnp.sum(dy, axis=0, keepdims=True))
        _acc_rows(loss_ref, 0.5 * jnp.sum(jnp.mean(err * err, axis=-1, keepdims=True), axis=0, keepdims=True) * jnp.ones((1, LANE), F32))

    return pl.pallas_call(
        body, name="final_ln_loss", grid=(S // TR,),
        in_specs=[_row_spec(D)] * 5 + [_vec_spec(D)] * 2,
        out_specs=[_row_spec(D)] * 4 + [_vec_spec(D), _vec_spec(D), _vec_spec(LANE)],
        out_shape=[jax.ShapeDtypeStruct((S, D), F32)] + [jax.ShapeDtypeStruct((S, D), BF16)] * 3 + [
                   jax.ShapeDtypeStruct((1, D), F32), jax.ShapeDtypeStruct((1, D), F32), jax.ShapeDtypeStruct((1, LANE), F32)],
    )(h1, ffn, pg, pp, target, g2, b2)


def _rot(u, cos_t, sin_t, lane):
    partner = jnp.where(lane < NOPE + ROPE // 2, pltpu.roll(u, LANE - ROPE // 2, 1), pltpu.roll(u, ROPE // 2, 1))
    return u * cos_t + partner * sin_t


def _rms(v, w):
    r = lax.rsqrt(jnp.mean(v * v, axis=-1, keepdims=True) + EPS_RMS)
    return v * r * w, r


def _rms_grad(v, r, w, g):
    gw = g * w
    return r * gw - v * (r * r * r * jnp.mean(gw * v, axis=-1, keepdims=True)), jnp.sum(g * v * r, axis=0, keepdims=True)


def _whole(arr):
    return pl.BlockSpec(arr.shape, lambda i: (0,) * arr.ndim)


def _qkv_fwd(small, w_q, w_k, w_v, q_norm, kv_norm, cos_t, sin_t):
    def body(sm_ref, wq_ref, wk_ref, wv_ref, qw_ref, kw_ref, c_ref, s_ref, qn_ref, kvn_ref, q_ref, k_ref, kt_ref, v_ref):
        lane = lax.broadcasted_iota(jnp.int32, (TR, LANE), 1)
        c, s = c_ref[...], s_ref[...]
        qn = _rms(sm_ref[:, SM_Q:SM_Q + Q_RANK], qw_ref[...])[0].astype(BF16)
        kvn = _rms(sm_ref[:, SM_KV:SM_KV + KV_RANK], kw_ref[...])[0].astype(BF16)
        qn_ref[...] = qn
        kvn_ref[...] = kvn
        kr = _rot(pltpu.roll(sm_ref[:, SM_KR:SM_KR + LANE], NOPE, 1), c, s, lane)
        for h in range(H):
            tile = slice(h * LANE, (h + 1) * LANE)
            q_ref[:, tile] = _rot(_dot(qn, wq_ref[:, tile], ((1,), (0,))), c, s, lane).astype(BF16)
            kt = _dot(kvn, wk_ref[:, tile], ((1,), (0,))) + kr
            k_ref[:, tile] = kt.astype(BF16)
            kt_ref[tile, :] = kt.T.astype(BF16)
        v_ref[...] = _dot(kvn, wv_ref[...], ((1,), (0,))).astype(BF16)

    w = H * LANE
    return pl.pallas_call(
        body, name="qkv_fwd", grid=(S // TR,),
        in_specs=[_row_spec(SMALL_W), _whole(w_q), _whole(w_k), _whole(w_v), _vec_spec(Q_RANK), _vec_spec(KV_RANK), _row_spec(LANE), _row_spec(LANE)],
        out_specs=[_row_spec(Q_RANK), _row_spec(KV_RANK), _row_spec(w), _row_spec(w), pl.BlockSpec((w, TR), lambda i: (0, i)),
                   _row_spec(H * VDIM)],
        out_shape=[jax.ShapeDtypeStruct((S, Q_RANK), BF16), jax.ShapeDtypeStruct((S, KV_RANK), BF16), jax.ShapeDtypeStruct((S, w), BF16),
                   jax.ShapeDtypeStruct((S, w), BF16), jax.ShapeDtypeStruct((w, S), BF16), jax.ShapeDtypeStruct((S, H * VDIM), BF16)],
    )(small, w_q, w_k, w_v, q_norm, kv_norm, cos_t, sin_t)


def _qkv_bwd(dqt, dk, dv, small, w_q, w_k, w_v, q_norm, kv_norm, cos_t, sin_t):
    def body(dq_ref, dk_ref, dv_ref, sm_ref, wq_ref, wk_ref, wv_ref, qw_ref, kw_ref, c_ref, s_ref,
             ds_ref, dql_ref, dkb_ref, dqw_ref, dkw_ref):
        lane = lax.broadcasted_iota(jnp.int32, (TR, LANE), 1)
        c, s = c_ref[...], -s_ref[...]
        dqn = jnp.zeros((TR, Q_RANK), F32)
        dkvn = _dot(dv_ref[...], wv_ref[...], ((1,), (1,)))
        dkr = jnp.zeros((TR, LANE), F32)
        for h in range(H):
            tile = slice(h * LANE, (h + 1) * LANE)
            dql = _rot(dq_ref[tile, :].T, c, s, lane).astype(BF16)
            dql_ref[:, tile] = dql
            dqn = dqn + _dot(dql, wq_ref[:, tile], ((1,), (1,)))
            dkt = dk_ref[:, tile]
            dkb_ref[:, tile] = dkt.astype(BF16)
            dkvn = dkvn + _dot(dkt, wk_ref[:, tile], ((1,), (1,)))
            dkr = dkr + dkt
        dkr = jnp.where((lane >= NOPE) & (lane < NOPE + ROPE), dkr, 0.0)
        q_c, kv_c = sm_ref[:, SM_Q:SM_Q + Q_RANK], sm_ref[:, SM_KV:SM_KV + KV_RANK]
        dq_c, dqw = _rms_grad(q_c, _rms(q_c, qw_ref[...])[1], qw_ref[...], dqn)
        dkv_c, dkw = _rms_grad(kv_c, _rms(kv_c, kw_ref[...])[1], kw_ref[...], dkvn)
        ds_ref[:, SM_Q:SM_Q + Q_RANK] = dq_c.astype(BF16)
        ds_ref[:, SM_KV:SM_KV + KV_RANK] = dkv_c.astype(BF16)
        ds_ref[:, SM_KR:SM_KR + LANE] = pltpu.roll(_rot(dkr, c, s, lane), LANE - NOPE, 1).astype(BF16)
        _acc_rows(dqw_ref, dqw)
        _acc_rows(dkw_ref, dkw)

    w = H * LANE
    return pl.pallas_call(
        body, name="qkv_bwd", grid=(S // TR,),
        in_specs=[pl.BlockSpec((w, TR), lambda i: (0, i)), _row_spec(w), _row_spec(H * VDIM), _row_spec(SMALL_W), _whole(w_q), _whole(w_k),
                  _whole(w_v), _vec_spec(Q_RANK), _vec_spec(KV_RANK), _row_spec(LANE), _row_spec(LANE)],
        out_specs=[_row_spec(SM_DT), _row_spec(w), _row_spec(w), _vec_spec(Q_RANK), _vec_spec(KV_RANK)],
        out_shape=[jax.ShapeDtypeStruct((S, SM_DT), BF16), jax.ShapeDtypeStruct((S, w), BF16), jax.ShapeDtypeStruct((S, w), BF16),
                   jax.ShapeDtypeStruct((1, Q_RANK), F32), jax.ShapeDtypeStruct((1, KV_RANK), F32)],
    )(dqt, dk, dv, small, w_q, w_k, w_v, q_norm, kv_norm, cos_t, sin_t)


CB = 256


def _shift_down(u, k, row):
    if k == 0:
        return u
    return jnp.where(row >= k, pltpu.roll(u, k, 0), 0.0)


def _shift_up(u, k, row):
    if k == 0:
        return u
    return jnp.where(row < S - k, pltpu.roll(u, S - k, 0), 0.0)


def _conv_fwd(u, w, b):
    def body(u_ref, w_ref, b_ref, o_ref):
        row = lax.broadcasted_iota(jnp.int32, (S, CB), 0)
        uu = u_ref[...]
        acc = b_ref[...] + w_ref[SSD_K - 1:SSD_K, :] * uu
        for k in range(SSD_K - 1):
            acc = acc + w_ref[k:k + 1, :] * _shift_down(uu, SSD_K - 1 - k, row)
        o_ref[...] = acc * _sigmoid(acc)

    c = u.shape[1]
    return pl.pallas_call(
        body, name="conv_fwd", grid=(c // CB,),
        in_specs=[pl.BlockSpec((S, CB), lambda j: (0, j)), pl.BlockSpec((SSD_K, CB), lambda j: (0, j)), pl.BlockSpec((1, CB), lambda j: (0, j))],
        out_specs=pl.BlockSpec((S, CB), lambda j: (0, j)), out_shape=jax.ShapeDtypeStruct((S, c), F32),
    )(u, w, b)


def _conv_bwd(u, w, b, dact):
    def body(u_ref, w_ref, b_ref, d_ref, du_ref, dw_ref, db_ref):
        row = lax.broadcasted_iota(jnp.int32, (S, CB), 0)
        uu = u_ref[...]
        sh = [_shift_down(uu, SSD_K - 1 - k, row) for k in range(SSD_K)]
        acc = b_ref[...]
        for k in range(SSD_K):
            acc = acc + w_ref[k:k + 1, :] * sh[k]
        sg = _sigmoid(acc)
        dacc = d_ref[...] * (sg * (1.0 + acc * (1.0 - sg)))
        du = w_ref[SSD_K - 1:SSD_K, :] * dacc
        for k in range(SSD_K - 1):
            du = du + w_ref[k:k + 1, :] * _shift_up(dacc, SSD_K - 1 - k, row)
        du_ref[...] = du.astype(BF16)
        for k in range(SSD_K):
            dw_ref[k:k + 1, :] = jnp.sum(dacc * sh[k], axis=0, keepdims=True)
        db_ref[...] = jnp.sum(dacc, axis=0, keepdims=True)

    c = u.shape[1]
    col = lambda r: pl.BlockSpec((r, CB), lambda j: (0, j))
    return pl.pallas_call(
        body, name="conv_bwd", grid=(c // CB,), in_specs=[col(S), col(SSD_K), col(1), col(S)], out_specs=[col(S), col(SSD_K), col(1)],
        out_shape=[jax.ShapeDtypeStruct((S, c), BF16), jax.ShapeDtypeStruct((SSD_K, c), F32), jax.ShapeDtypeStruct((1, c), F32)],
    )(u, w, b, dact)


NPAIR = H // 2
PAIRS_PER_GROUP = NPAIR // SSD_G


def _softplus(v):
    return jnp.maximum(v, 0.0) + jnp.log(1.0 + jnp.exp(-jnp.abs(v)))


def _dot(a, b, dims):
    return lax.dot_general(a.astype(BF16), b.astype(BF16), (dims, ((), ())), preferred_element_type=F32)


def _dot3(a, b, dims, split_lhs):
    v = a if split_lhs else b
    v1 = v.astype(BF16)
    r1 = v - v1.astype(F32)
    v2 = r1.astype(BF16)
    v3 = (r1 - v2.astype(F32)).astype(BF16)
    acc = None
    for part in (v1, v2, v3):
        lhs, rhs = (part, b) if split_lhs else (a, part)
        t = lax.dot_general(lhs, rhs, (dims, ((), ())), preferred_element_type=F32)
        acc = t if acc is None else acc + t
    return acc


def _ssd_chunk_common(dt_ref, dtT_ref, prow_ref, pcol_ref):
    prow = prow_ref[...]
    pcol = pcol_ref[...]
    ri = lax.broadcasted_iota(jnp.int32, (SSD_L, SSD_L), 0)
    ci = lax.broadcasted_iota(jnp.int32, (SSD_L, SSD_L), 1)
    causal = ri >= ci
    pre_c = dt_ref[...] + prow[0:1, :]
    dtc = _softplus(pre_c)
    a_row = -jnp.exp(prow[1:2, :])
    cs_col = _dot3(causal.astype(BF16), dtc * a_row, ((1,), (0,)), False)
    dtr = _softplus(dtT_ref[...] + pcol[:, 0:1])
    a_col = -jnp.exp(pcol[:, 1:2])
    cs_row = _dot3(dtr * a_col, (ri <= ci).astype(BF16), ((1,), (0,)), True)
    return prow, causal, pre_c, dtc, a_row, cs_col, cs_row


def _ssd_fwd(act, small, dtT, prow, pcol):
    def body(x_ref, b_ref, c_ref, dt_ref, dtT_ref, prow_ref, pcol_ref, y_ref, st_ref, state):
        @pl.when(pl.program_id(0) == 0)
        def _():
            state[...] = jnp.zeros_like(state)

        prow, causal, _, dtc, _, cs_col, cs_row = _ssd_chunk_common(dt_ref, dtT_ref, prow_ref, pcol_ref)
        lo = lax.broadcasted_iota(jnp.int32, (SSD_L, LANE), 1) < SSD_P
        lo1 = lo[0:1, :]
        for g in range(SSD_G):
            bm = b_ref[:, g * SSD_N:(g + 1) * SSD_N]
            cm = c_ref[:, g * SSD_N:(g + 1) * SSD_N]
            cb = _dot(cm, bm, ((1,), (1,)))
            for qq in range(PAIRS_PER_GROUP):
                q = g * PAIRS_PER_GROUP + qq
                ha, hb = 2 * q, 2 * q + 1
                csa, csb = cs_col[:, ha:ha + 1], cs_col[:, hb:hb + 1]
                xp = x_ref[:, q * LANE:(q + 1) * LANE]
                xx = xp * jnp.where(lo, dtc[:, ha:ha + 1], dtc[:, hb:hb + 1])
                ga = cb * jnp.exp(jnp.where(causal, csa - cs_row[ha:ha + 1, :], NEG))
                gb = cb * jnp.exp(jnp.where(causal, csb - cs_row[hb:hb + 1, :], NEG))
                y = _dot(ga, jnp.where(lo, xx, 0.0), ((1,), (0,))) + _dot(gb, jnp.where(lo, 0.0, xx), ((1,), (0,)))
                s_in = state[q]
                y = y + _dot(cm, s_in, ((1,), (0,))) * jnp.where(lo, jnp.exp(csa), jnp.exp(csb))
                y = y + jnp.where(lo1, prow[2:3, ha:ha + 1], prow[2:3, hb:hb + 1]) * xp
                y_ref[:, q * LANE:(q + 1) * LANE] = y
                la, lb = csa[SSD_L - 1:SSD_L, :], csb[SSD_L - 1:SSD_L, :]
                decay = jnp.where(lo, jnp.exp(la - csa), jnp.exp(lb - csb))
                st_ref[q] = s_in
                state[q] = s_in * jnp.where(lo1, jnp.exp(la), jnp.exp(lb)) + _dot(bm, xx * decay, ((0,), (0,)))

    L = SSD_L
    return pl.pallas_call(
        body, name="ssd_fwd", grid=(SSD_NC,),
        in_specs=[pl.BlockSpec((L, SSD_INNER), lambda c: (c, 0)),
                  pl.BlockSpec((L, SSD_G * SSD_N), lambda c: (c, SSD_INNER // (SSD_G * SSD_N))),
                  pl.BlockSpec((L, SSD_G * SSD_N), lambda c: (c, SSD_INNER // (SSD_G * SSD_N) + 1)),
                  pl.BlockSpec((L, LANE), lambda c: (c, SM_DT // LANE)),
                  pl.BlockSpec((LANE, L), lambda c: (0, c)),
                  pl.BlockSpec((8, LANE), lambda c: (0, 0)), pl.BlockSpec((LANE, 8), lambda c: (0, 0))],
        out_specs=[pl.BlockSpec((L, SSD_INNER), lambda c: (c, 0)),
                   pl.BlockSpec((None, NPAIR, SSD_N, LANE), lambda c: (c, 0, 0, 0))],
        out_shape=[jax.ShapeDtypeStruct((S, SSD_INNER), F32), jax.ShapeDtypeStruct((SSD_NC, NPAIR, SSD_N, LANE), F32)],
        scratch_shapes=[pltpu.VMEM((NPAIR, SSD_N, LANE), F32)],
        compiler_params=pltpu.CompilerParams(dimension_semantics=("arbitrary",)),
    )(act, act, act, small, dtT, prow, pcol)


def _ssd_bwd(act, small, dtT, prow, pcol, states, dy):
    def body(x_ref, b_ref, c_ref, dt_ref, dtT_ref, prow_ref, pcol_ref, st_ref, dy_ref,
             dx_ref, ddt_ref, dp_ref, dstate):
        @pl.when(pl.program_id(0) == 0)
        def _():
            dstate[...] = jnp.zeros_like(dstate)
            dp_ref[...] = jnp.zeros_like(dp_ref)

        prow, causal, pre_c, dtc, a_row, cs_col, cs_row = _ssd_chunk_common(dt_ref, dtT_ref, prow_ref, pcol_ref)
        lane = lax.broadcasted_iota(jnp.int32, (SSD_L, LANE), 1)
        sub = lax.broadcasted_iota(jnp.int32, (LANE, SSD_L), 0)
        rowi = lax.broadcasted_iota(jnp.int32, (SSD_L, 1), 0)
        lane1 = lane[0:1, :]
        lo = lane < SSD_P
        lo1 = lo[0:1, :]
        dcs_c = jnp.zeros((SSD_L, LANE), F32)
        dcs_r = jnp.zeros((LANE, SSD_L), F32)
        ddt_x = jnp.zeros((SSD_L, LANE), F32)
        dd_row = jnp.zeros((1, LANE), F32)
        for g in range(SSD_G):
            bm = b_ref[:, g * SSD_N:(g + 1) * SSD_N]
            cm = c_ref[:, g * SSD_N:(g + 1) * SSD_N]
            cb = _dot(cm, bm, ((1,), (1,)))
            dcb = jnp.zeros((SSD_L, SSD_L), F32)
            dbm = jnp.zeros((SSD_L, SSD_N), F32)
            dcm = jnp.zeros((SSD_L, SSD_N), F32)
            for qq in range(PAIRS_PER_GROUP):
                q = g * PAIRS_PER_GROUP + qq
                ha, hb = 2 * q, 2 * q + 1
                csa, csb = cs_col[:, ha:ha + 1], cs_col[:, hb:hb + 1]
                xp = x_ref[:, q * LANE:(q + 1) * LANE]
                dtp = jnp.where(lo, dtc[:, ha:ha + 1], dtc[:, hb:hb + 1])
                xx = xp * dtp
                lma = jnp.exp(jnp.where(causal, csa - cs_row[ha:ha + 1, :], NEG))
                lmb = jnp.exp(jnp.where(causal, csb - cs_row[hb:hb + 1, :], NEG))
                ga, gb = cb * lma, cb * lmb
                dyp = dy_ref[:, q * LANE:(q + 1) * LANE]
                dya, dyb = jnp.where(lo, dyp, 0.0), jnp.where(lo, 0.0, dyp)
                s_in = st_ref[q]
                ds_out = dstate[q]
                la, lb = csa[SSD_L - 1:SSD_L, :], csb[SSD_L - 1:SSD_L, :]
                ecs = jnp.where(lo, jnp.exp(csa), jnp.exp(csb))
                decay = jnp.where(lo, jnp.exp(la - csa), jnp.exp(lb - csb))
                cd = jnp.where(lo1, jnp.exp(la), jnp.exp(lb))
                bds = _dot(bm, ds_out, ((1,), (0,)))
                dxx = _dot(ga, dya, ((0,), (0,))) + _dot(gb, dyb, ((0,), (0,))) + bds * decay
                dga = _dot(dya, xx, ((1,), (1,)))
                dgb = _dot(dyb, xx, ((1,), (1,)))
                dsega, dsegb = dga * ga, dgb * gb
                dcb = dcb + dga * lma + dgb * lmb
                yoff = _dot(cm, s_in, ((1,), (0,))) * ecs
                dye = dyp * ecs
                dcm = dcm + _dot(dye, s_in, ((1,), (1,)))
                xd = xx * decay
                dbm = dbm + _dot(xd, ds_out, ((1,), (1,)))
                wv = xd * bds
                t1 = dyp * yoff - wv
                col_a = (jnp.sum(dsega, axis=1, keepdims=True) + jnp.sum(jnp.where(lo, t1, 0.0), axis=1, keepdims=True))
                col_b = (jnp.sum(dsegb, axis=1, keepdims=True) + jnp.sum(jnp.where(lo, 0.0, t1), axis=1, keepdims=True))
                sprod = ds_out * s_in
                end_a = jnp.sum(jnp.where(lo, wv, 0.0), keepdims=True) + jnp.exp(la) * jnp.sum(jnp.where(lo[:SSD_N], sprod, 0.0), keepdims=True)
                end_b = jnp.sum(jnp.where(lo, 0.0, wv), keepdims=True) + jnp.exp(lb) * jnp.sum(jnp.where(lo[:SSD_N], 0.0, sprod), keepdims=True)
                col_a = col_a + jnp.where(rowi == SSD_L - 1, end_a, 0.0)
                col_b = col_b + jnp.where(rowi == SSD_L - 1, end_b, 0.0)
                dcs_c = dcs_c + jnp.where(lane == ha, col_a, 0.0) + jnp.where(lane == hb, col_b, 0.0)
                dcs_r = (dcs_r + jnp.where(sub == ha, jnp.sum(dsega, axis=0, keepdims=True), 0.0)
                         + jnp.where(sub == hb, jnp.sum(dsegb, axis=0, keepdims=True), 0.0))
                dstate[q] = _dot(cm, dye, ((0,), (0,))) + cd * ds_out
                dpair = jnp.where(lo1, prow[2:3, ha:ha + 1], prow[2:3, hb:hb + 1])
                dx_ref[:, q * LANE:(q + 1) * LANE] = dxx * dtp + dpair * dyp
                t2 = dxx * xp
                ddt_x = (ddt_x + jnp.where(lane == ha, jnp.sum(jnp.where(lo, t2, 0.0), axis=1, keepdims=True), 0.0)
                         + jnp.where(lane == hb, jnp.sum(jnp.where(lo, 0.0, t2), axis=1, keepdims=True), 0.0))
                t3 = dyp * xp
                dd_row = (dd_row + jnp.where(lane1 == ha, jnp.sum(jnp.where(lo, t3, 0.0), keepdims=True), 0.0)
                          + jnp.where(lane1 == hb, jnp.sum(jnp.where(lo, 0.0, t3), keepdims=True), 0.0))
            dx_ref[:, SSD_INNER + g * SSD_N:SSD_INNER + (g + 1) * SSD_N] = dbm + _dot(dcb, cm, ((0,), (0,)))
            dx_ref[:, SSD_INNER + (SSD_G + g) * SSD_N:SSD_INNER + (SSD_G + g + 1) * SSD_N] = dcm + _dot(dcb, bm, ((1,), (0,)))
        ri = lax.broadcasted_iota(jnp.int32, (SSD_L, SSD_L), 0)
        ci = lax.broadcasted_iota(jnp.int32, (SSD_L, SSD_L), 1)
        da = _dot3((ri <= ci).astype(BF16), dcs_c, ((1,), (0,)), False)
        da = da - _dot3(dcs_r, causal.astype(BF16), ((1,), (0,)), True).T
        ddt = ddt_x + da * a_row
        ddt_raw = ddt * _sigmoid(pre_c)
        ddt_ref[...] = ddt_raw
        da_head = jnp.sum(da * dtc, axis=0, keepdims=True) * a_row
        dp_ref[0:1, :] += jnp.sum(ddt_raw, axis=0, keepdims=True)
        dp_ref[1:2, :] += da_head
        dp_ref[2:3, :] += dd_row

    L = SSD_L
    rev = SSD_NC - 1
    bc_cols = SSD_INNER // (SSD_G * SSD_N)
    return pl.pallas_call(
        body, name="ssd_bwd", grid=(SSD_NC,),
        in_specs=[pl.BlockSpec((L, SSD_INNER), lambda c: (rev - c, 0)),
                  pl.BlockSpec((L, SSD_G * SSD_N), lambda c: (rev - c, bc_cols)),
                  pl.BlockSpec((L, SSD_G * SSD_N), lambda c: (rev - c, bc_cols + 1)),
                  pl.BlockSpec((L, LANE), lambda c: (rev - c, SM_DT // LANE)),
                  pl.BlockSpec((LANE, L), lambda c: (0, rev - c)),
                  pl.BlockSpec((8, LANE), lambda c: (0, 0)), pl.BlockSpec((LANE, 8), lambda c: (0, 0)),
                  pl.BlockSpec((None, NPAIR, SSD_N, LANE), lambda c: (rev - c, 0, 0, 0)),
                  pl.BlockSpec((L, SSD_INNER), lambda c: (rev - c, 0))],
        out_specs=[pl.BlockSpec((L, SSD_XBC), lambda c: (rev - c, 0)),
                   pl.BlockSpec((L, LANE), lambda c: (rev - c, 0)),
                   pl.BlockSpec((8, LANE), lambda c: (0, 0))],
        out_shape=[jax.ShapeDtypeStruct((S, SSD_XBC), F32), jax.ShapeDtypeStruct((S, LANE), F32),
                   jax.ShapeDtypeStruct((8, LANE), F32)],
        scratch_shapes=[pltpu.VMEM((NPAIR, SSD_N, LANE), F32)],
        compiler_params=pltpu.CompilerParams(dimension_semantics=("arbitrary",)),
    )(act, act, act, small, dtT, prow, pcol, states, dy)


TQ = 256
TK = 256
FWD_TQ = 256
FWD_TK = 256


def _attn_fwd(qc, kc, v):
    TQ, TK = FWD_TQ, FWD_TK

    def body(q_ref, k_ref, v_ref, o_ref, lse_ref):
        i = pl.program_id(1)
        lo = lax.broadcasted_iota(jnp.int32, (TQ, LANE), 1) < VDIM
        lo_k = lax.broadcasted_iota(jnp.int32, (TK, LANE), 1) < VDIM
        row_minus_col = lax.broadcasted_iota(jnp.int32, (TQ, TK), 0) - lax.broadcasted_iota(jnp.int32, (TQ, TK), 1)
        qa, qb = q_ref[:, 0:LANE], q_ref[:, LANE:2 * LANE]

        def scores(kb):
            kk = k_ref[pl.ds(pl.multiple_of(kb * TK, TK), TK), :]
            return (_dot(qa, kk[:, 0:LANE], ((1,), (1,))) * ATT_SCALE_LOG2, _dot(qb, kk[:, LANE:2 * LANE], ((1,), (1,))) * ATT_SCALE_LOG2)

        def update(kb, sa, sb, stats):
            ma, la, mb, lb, acc = stats
            vv = v_ref[pl.ds(pl.multiple_of(kb * TK, TK), TK), :]
            na = jnp.maximum(ma, jnp.max(sa, axis=1, keepdims=True))
            nb = jnp.maximum(mb, jnp.max(sb, axis=1, keepdims=True))
            pa, pb = jnp.exp2(sa - na), jnp.exp2(sb - nb)
            fa, fb = jnp.exp2(ma - na), jnp.exp2(mb - nb)
            la = fa * la + jnp.sum(pa, axis=1, keepdims=True)
            lb = fb * lb + jnp.sum(pb, axis=1, keepdims=True)
            acc = (acc * jnp.where(lo, fa, fb) + _dot(pa, jnp.where(lo_k, vv, 0), ((1,), (0,)))
                   + _dot(pb, jnp.where(lo_k, 0, vv), ((1,), (0,))))
            return na, la, nb, lb, acc

        def step(kb, carry):
            sa, sb = carry[:2]
            nxt = scores(kb + 1)
            return nxt + update(kb, sa, sb, carry[2:])

        neg = jnp.full((TQ, 1), NEG, F32)
        zero = jnp.zeros((TQ, 1), F32)
        n_full = i * (TQ // TK)
        carry = lax.fori_loop(0, n_full, step, scores(0) + (neg, zero, neg, zero, jnp.zeros((TQ, LANE), F32)))
        s, stats = carry[:2], carry[2:]
        for d in range(TQ // TK):
            nxt = scores(n_full + d + 1) if d + 1 < TQ // TK else None
            sa, sb = (jnp.where(row_minus_col >= d * TK, t, NEG) for t in s)
            stats = update(n_full + d, sa, sb, stats)
            s = nxt
        ma, la, mb, lb, acc = stats
        o_ref[...] = acc / jnp.where(lo, la, lb)
        lse_ref[...] = jnp.where(lo, ma + jnp.log2(la), mb + jnp.log2(lb)) * LN2

    return pl.pallas_call(
        body, name="attn_fwd", grid=(NPAIR, S // TQ),
        in_specs=[pl.BlockSpec((TQ, 2 * LANE), lambda j, i: (i, j)), pl.BlockSpec((S, 2 * LANE), lambda j, i: (0, j)),
                  pl.BlockSpec((S, LANE), lambda j, i: (0, j))],
        out_specs=[pl.BlockSpec((TQ, LANE), lambda j, i: (i, j)), pl.BlockSpec((None, TQ, LANE), lambda j, i: (j, i, 0))],
        out_shape=[jax.ShapeDtypeStruct((S, H * VDIM), F32), jax.ShapeDtypeStruct((NPAIR, S, LANE), F32)],
        compiler_params=pltpu.CompilerParams(dimension_semantics=("parallel", "parallel")),
    )(qc, kc, v)


def _attn_rows(lse, o, do):
    def body(lse_ref, o_ref, do_ref, r_ref):
        lt = lse_ref[...].T * (1.0 / LN2)
        tt = (o_ref[...] * do_ref[...]).T
        r_ref[...] = jnp.zeros_like(r_ref)
        r_ref[0:1, :] = lt[0:1, :]
        r_ref[1:2, :] = lt[VDIM:VDIM + 1, :]
        r_ref[2:3, :] = jnp.sum(tt[0:VDIM, :], axis=0, keepdims=True)
        r_ref[3:4, :] = jnp.sum(tt[VDIM:LANE, :], axis=0, keepdims=True)

    tile = pl.BlockSpec((S, LANE), lambda j: (0, j))
    return pl.pallas_call(
        body, name="attn_rows", grid=(NPAIR,), in_specs=[pl.BlockSpec((None, S, LANE), lambda j: (j, 0, 0)), tile, tile],
        out_specs=pl.BlockSpec((None, 8, S), lambda j: (j, 0, 0)), out_shape=jax.ShapeDtypeStruct((NPAIR, 8, S), F32),
    )(lse, o, do)


def _attn_bwd(qc, kc, kct, v, do, rows):
    nq = S // TQ

    def body(q_ref, k_ref, kt_ref, v_ref, do_ref, r_ref, dqt_ref, dk_ref, dv_ref):
        kb = pl.program_id(1)

        @pl.when(kb == 0)
        def _():
            dqt_ref[...] = jnp.zeros_like(dqt_ref)

        lo = lax.broadcasted_iota(jnp.int32, (TK, LANE), 1) < VDIM
        q_minus_k = lax.broadcasted_iota(jnp.int32, (TK, TQ), 1) - lax.broadcasted_iota(jnp.int32, (TK, TQ), 0)
        vv = v_ref[...]
        kk = k_ref[...]

        def step(qi, carry):
            off = pl.multiple_of(qi * TQ, TQ)
            qq = q_ref[pl.ds(off, TQ), :]
            dd = do_ref[pl.ds(off, TQ), :].astype(BF16)
            rr = r_ref[:, pl.ds(off, TQ)]
            keep = q_minus_k >= (kb - qi) * TQ
            out = []
            for x in range(2):
                sel = lo if x == 0 else jnp.logical_not(lo)
                kx, qx = kk[:, x * LANE:(x + 1) * LANE], qq[:, x * LANE:(x + 1) * LANE]
                st = jnp.where(keep, _dot(kx, qx, ((1,), (1,))) * ATT_SCALE_LOG2, NEG)
                pt = jnp.exp2(st - rr[x:x + 1, :])
                dpt = _dot(jnp.where(sel, vv, 0), dd, ((1,), (1,)))
                dst = (pt * (dpt - rr[2 + x:3 + x, :]) * ATT_SCALE).astype(BF16)
                out.append(carry[x] + _dot(dst, qx, ((1,), (0,))))
                out.append(_dot(pt, jnp.where(sel, dd, 0), ((1,), (0,))))
                dqt_ref[x * LANE:(x + 1) * LANE, pl.ds(off, TQ)] += _dot(kt_ref[x * LANE:(x + 1) * LANE, :], dst, ((1,), (0,)))
            return out[0], out[2], carry[2] + out[1] + out[3]

        z = jnp.zeros((TK, LANE), F32)
        dka, dkb, dv = lax.fori_loop(kb, nq, step, (z, z, z))
        dk_ref[:, 0:LANE] = dka
        dk_ref[:, LANE:2 * LANE] = dkb
        dv_ref[...] = dv.astype(BF16)

    return pl.pallas_call(
        body, name="attn_bwd", grid=(NPAIR, S // TK),
        in_specs=[pl.BlockSpec((S, 2 * LANE), lambda j, k: (0, j)), pl.BlockSpec((TK, 2 * LANE), lambda j, k: (k, j)),
                  pl.BlockSpec((2 * LANE, TK), lambda j, k: (j, k)), pl.BlockSpec((TK, LANE), lambda j, k: (k, j)),
                  pl.BlockSpec((S, LANE), lambda j, k: (0, j)), pl.BlockSpec((None, 8, S), lambda j, k: (j, 0, 0))],
        out_specs=[pl.BlockSpec((2 * LANE, S), lambda j, k: (j, 0)), pl.BlockSpec((TK, 2 * LANE), lambda j, k: (k, j)),
                   pl.BlockSpec((TK, LANE), lambda j, k: (k, j))],
        out_shape=[jax.ShapeDtypeStruct((H * LANE, S), F32), jax.ShapeDtypeStruct((S, H * LANE), F32),
                   jax.ShapeDtypeStruct((S, H * VDIM), BF16)],
        compiler_params=pltpu.CompilerParams(dimension_semantics=("parallel", "arbitrary")),
    )(qc, kc, kct, v, do, rows)


_IN_Z, _IN_XBC, _IN_DT, _IN_Q, _IN_KV, _IN_KR = 0, 1024, 2560, 2576, 2960, 3216


def _prep_in(w_in_t):
    dt = w_in_t.dtype
    w_small_t = jnp.concatenate(
        [w_in_t[_IN_Q:_IN_KV], w_in_t[_IN_KV:_IN_KR], w_in_t[_IN_KR:IN_WIDTH], jnp.zeros((LANE - ROPE, D), dt),
         w_in_t[_IN_DT:_IN_Q], jnp.zeros((LANE - H, D), dt)], axis=0)
    return w_in_t[_IN_Z:_IN_XBC], w_in_t[_IN_XBC:_IN_DT], w_small_t


def _prep_attn(w_qb, w_kvb):
    w_q = jnp.pad(w_qb.reshape(Q_RANK, H, NOPE + ROPE), ((0, 0), (0, 0), (0, LANE - NOPE - ROPE))).reshape(Q_RANK, H * LANE)
    kv3 = w_kvb.reshape(KV_RANK, H, NOPE + VDIM)
    w_k = jnp.pad(kv3[:, :, :NOPE], ((0, 0), (0, 0), (0, LANE - NOPE))).reshape(KV_RANK, H * LANE)
    w_v = kv3[:, :, NOPE:].reshape(KV_RANK, H * VDIM)
    return w_q, w_k, w_v


def _rope_tables(positions):
    inv_freq = 1.0 / (10000.0 ** (jnp.arange(0, ROPE, 2, dtype=F32) / ROPE))
    ang = positions.astype(F32).reshape(S, 1) * inv_freq
    cos, sin = jnp.cos(ang), jnp.sin(ang)
    cos_t = jnp.concatenate([jnp.ones((S, NOPE), F32), cos, cos, jnp.ones((S, LANE - NOPE - ROPE), F32)], axis=1)
    sin_t = jnp.concatenate([jnp.zeros((S, NOPE), F32), -sin, sin, jnp.zeros((S, LANE - NOPE - ROPE), F32)], axis=1)
    return cos_t, sin_t


def _local_step(x, p, positions, target, w_in, fetch, send, sp):
    w_z, w_xbc, w_small = _prep_in(w_in.reshape(IN_WIDTH, D))
    cos_t, sin_t = _rope_tables(positions)
    prow = jnp.zeros((8, LANE), F32).at[0, :H].set(sp["dt_bias"][0]).at[1, :H].set(sp["A_log"][0]).at[2, :H].set(sp["D"][0])
    pcol = prow.T

    xb, pb = x.astype(BF16), p.astype(BF16)
    z = _mm([(xb, w_z)], tb=True, name="proj_z")
    xbc = _mm([(xb, w_xbc)], tb=True, name="proj_xbc")
    small = _mm([(xb, w_small)], tb=True, name="proj_small")
    act = _conv_fwd(xbc, sp["conv_w"], sp["conv_b"])
    dt_t = small[:, SM_DT:SM_DT + LANE].T
    y, states = _ssd_fwd(act, small, dt_t, prow, pcol)
    y_ssd = _gate_norm_fwd(y, z, sp["ssd_norm"])
    gl = fetch("attn", y_ssd)
    w_q, w_k, w_v = _prep_attn(_from_cols(gl["w_qb"]), _from_cols(gl["w_kvb"]))
    qn, kvn, qcat, kcat, kcat_t, v = _qkv_fwd(small, w_q, w_k, w_v, sp["q_norm"], sp["kv_norm"], cos_t, sin_t)
    o, lse = _attn_fwd(qcat, kcat, v)
    y_mla = _rms_fwd(o, sp["out_norm"], name="out_norm_fwd")
    w_out = fetch("out", y_mla)["w_out"]
    w_out_s = w_out[:NCHIP // 2].reshape(SSD_INNER, D)
    w_out_m = w_out[NCHIP // 2:].reshape(SSD_INNER, D)
    mix = _mm([(y_ssd, w_out_s), (y_mla, w_out_m)], name="out_proj")
    h1, h1b = _ln_fwd(x, mix, sp["ln_mix_g"], sp["ln_mix_b"])
    gl = fetch("ffn", h1b)
    w_pg, w_pp = gl["w_pg"].reshape(D, D), _from_cols(gl["w_pp"])
    w_gate, w_up, w_down = gl["w_gate"], gl["w_up"], gl["w_down"]
    gate, up, actf = _ffn_hidden_fwd(h1b, w_gate, w_up)
    ffn = _mm([(actf, w_down)], chunk="sum", name="ffn_down")
    pg = _mm([(h1b, w_pg)], name="ple_gate")
    pp = _mm([(pb, w_pp)], name="ple_proj")
    dpre2, dpre2b, dpg, dpp, dg2, db2, loss_row = _final_fwd_bwd(h1, ffn, pg, pp, target, sp["ln_ffn_g"], sp["ln_ffn_b"])

    g = {"ln_ffn_g": dg2, "ln_ffn_b": db2}
    g["w_pp"] = _to_cols(_mm([(pb, dpp)], ta=True, out_dtype=BF16, name="d_w_ple_proj"))
    g["w_pg"] = _mm([(h1b, dpg)], ta=True, out_dtype=BF16, name="d_w_ple_gate").reshape(NCHIP, D // NCHIP, D)
    g["w_down"] = _mm([(actf, dpre2b)], ta=True, chunk="out", out_dtype=BF16, name="d_w_down")
    dgate, dup = _ffn_hidden_bwd(dpre2b, w_down, gate, up)
    g["w_gate"] = _mm([(dgate, h1b)], ta=True, chunk="out", out_dtype=BF16, name="d_w_gate")
    g["w_up"] = _mm([(dup, h1b)], ta=True, chunk="out", out_dtype=BF16, name="d_w_up")
    sent = send("ffn", {name: g.pop(name) for name in dict(ASYNC_GROUPS)["ffn"]})
    dh1 = _mm([(dpg, w_pg)], tb=True, add=dpre2, add_scale=ALPHA, name="d_h1_ple")
    dh1 = _mm([(dgate, w_gate), (dup, w_up)], chunk="sum", add=dh1, name="d_h1")
    dpre1, dpre1b, g["ln_mix_g"], g["ln_mix_b"] = _ln_bwd(x, mix, sp["ln_mix_g"] + sent, dh1)
    dy_ssd = _mm([(dpre1b, w_out_s)], tb=True, name="d_y_ssd")
    dy_mla = _mm([(dpre1b, w_out_m)], tb=True, name="d_y_mla")
    dw_out = jnp.concatenate([_mm([(y_ssd, dpre1b)], ta=True, out_dtype=BF16, name="d_w_out_s"),
                              _mm([(y_mla, dpre1b)], ta=True, out_dtype=BF16, name="d_w_out_m")], axis=0)
    sent = send("out", {"w_out": dw_out.reshape(NCHIP, 2 * SSD_INNER // NCHIP, D)})
    do, g["out_norm"] = _rms_bwd(o, sp["out_norm"] + sent, dy_mla, name="out_norm_bwd")
    dqt, dk, dv = _attn_bwd(qcat, kcat, kcat_t, v, do, _attn_rows(lse, o, do))
    dlatent, dqlin, dkb, g["q_norm"], g["kv_norm"] = _qkv_bwd(dqt, dk, dv, small, w_q, w_k, w_v, sp["q_norm"], sp["kv_norm"], cos_t, sin_t)
    dw_q = _mm([(qn, dqlin)], ta=True, out_dtype=BF16, name="d_w_q")
    dw_k = _mm([(kvn, dkb)], ta=True, out_dtype=BF16, name="d_w_k")
    dw_v = _mm([(kvn, dv)], ta=True, out_dtype=BF16, name="d_w_v")
    dw_qb = _to_cols(dw_q.reshape(Q_RANK, H, LANE)[:, :, :NOPE + ROPE].reshape(Q_RANK, H * (NOPE + ROPE)))
    dw_kvb = _to_cols(jnp.concatenate([dw_k.reshape(KV_RANK, H, LANE)[:, :, :NOPE], dw_v.reshape(KV_RANK, H, VDIM)],
                                       axis=2).reshape(KV_RANK, H * (NOPE + VDIM)))
    sent = send("attn", {"w_qb": dw_qb, "w_kvb": dw_kvb})
    dy, dz, g["ssd_norm"] = _gate_norm_bwd(y, z, sp["ssd_norm"] + sent, dy_ssd)
    dact, ddt, dprow = _ssd_bwd(act, small, dt_t, prow, pcol, states, dy)
    g["dt_bias"], g["A_log"], g["D"] = dprow[0:1, :H], dprow[1:2, :H], dprow[2:3, :H]
    dxbc, g["conv_w"], g["conv_b"] = _conv_bwd(xbc, sp["conv_w"], sp["conv_b"], dact)
    dsmall = jnp.concatenate([dlatent, ddt.astype(BF16)], axis=1)
    grad_x = _mm([(dz, w_z), (dxbc, w_xbc), (dsmall, w_small)], add=dpre1, add_scale=ALPHA, name="d_x")
    dw_small = _mm([(dsmall, xb)], ta=True, out_dtype=BF16, name="d_w_small")
    dw_in = jnp.concatenate(
        [_mm([(dz, xb)], ta=True, out_dtype=BF16, name="d_w_z"), _mm([(dxbc, xb)], ta=True, out_dtype=BF16, name="d_w_xbc"),
         dw_small[SM_DT:SM_DT + H], dw_small[SM_Q:SM_Q + Q_RANK], dw_small[SM_KV:SM_KV + KV_RANK], dw_small[SM_KR:SM_KR + ROPE]],
        axis=0).reshape(NCHIP, IN_WIDTH // NCHIP, D)
    return loss_row, grad_x, dw_in, g


MESH = pl.DeviceIdType.MESH
BIG = (("w_in", (D, IN_WIDTH), 1), ("w_qb", (Q_RANK, H * (NOPE + ROPE)), 1), ("w_kvb", (KV_RANK, H * (NOPE + VDIM)), 1),
       ("w_out", (2 * SSD_INNER, D), 0), ("w_gate", (D, D_FF), 1), ("w_up", (D, D_FF), 1), ("w_down", (D_FF, D), 0),
       ("w_pg", (D, D), 0), ("w_pp", (PLE, D), 1))
CONV_SHARD = SSD_XBC // NCHIP
BF16_ROWS = 16


def _from_cols(stack):
    return jnp.concatenate([stack[k] for k in range(NCHIP)], axis=1)


def _to_cols(full):
    r, c4 = full.shape
    return full.reshape(r, NCHIP, c4 // NCHIP).transpose(1, 0, 2)


def _coords():
    return lax.axis_index("x"), lax.axis_index("y"), lax.axis_index("c")


def _peers():
    x, y, c = _coords()
    return 2 * x + y, c, [(1 - x, y), (x, 1 - y), (1 - x, 1 - y)], (x, y, 1 - c)


def _half_axis(shape):
    return 0 if shape[-2] % (2 * BF16_ROWS) == 0 else 1


def _half_shape(shape):
    r, c = shape[-2:]
    return (r // 2, c) if _half_axis(shape) == 0 else (r, c // 2)


def _half(core, shape):
    r, c = shape[-2:]
    if _half_axis(shape) == 0:
        return pl.ds(pl.multiple_of(core * (r // 2), BF16_ROWS), r // 2), slice(None)
    return slice(None), pl.ds(pl.multiple_of(core * (c // 2), LANE), c // 2)


def _gather_weights(shards):
    n_arr = len(shards)
    per = 2 * (NCHIP - 1)

    def body(*refs):
        ins, outs = refs[:n_arr], refs[n_arr:2 * n_arr]
        send_sems, recv_sems, local_sems = refs[2 * n_arr:]
        k, c, chips, sibling = _peers()

        def copy(idx, src, dst, to):
            return pltpu.make_async_remote_copy(src_ref=src, dst_ref=dst, send_sem=send_sems.at[idx], recv_sem=recv_sems.at[idx],
                                                device_id=to, device_id_type=MESH)

        def part(a, chip, core):
            return outs[a].at[chip, *_half(core, shards[a].shape)]

        mine = [pltpu.make_async_copy(ins[a], outs[a].at[k], local_sems.at[a]) for a in range(n_arr)]
        for cp in mine:
            cp.start()
        sends = []
        for a in range(n_arr):
            for j, (cx, cy) in enumerate(chips):
                sends.append(copy(per * a + j, ins[a].at[*_half(c, shards[a].shape)], part(a, k, c), (cx, cy, c)))
                sends[-1].start()
        for j, (cx, cy) in enumerate(chips):
            for a in range(n_arr):
                landed = part(a, 2 * cx + cy, c)
                copy(per * a + j, landed, landed, (cx, cy, c)).wait_recv()
                sends.append(copy(per * a + NCHIP - 1 + j, landed, landed, sibling))
                sends[-1].start()
        for j, (cx, cy) in enumerate(chips):
            for a in range(n_arr):
                other = part(a, 2 * cx + cy, 1 - c)
                copy(per * a + NCHIP - 1 + j, other, other, sibling).wait_recv()
        for cp in sends:
            cp.wait_send()
        for cp in mine:
            cp.wait()

    any_spec = pl.BlockSpec(memory_space=pl.ANY)
    return pl.pallas_call(
        body, name="gather_weights", in_specs=[any_spec] * n_arr, out_specs=[any_spec] * n_arr,
        out_shape=[jax.ShapeDtypeStruct((NCHIP,) + s.shape, s.dtype) for s in shards],
        scratch_shapes=[pltpu.SemaphoreType.DMA((per * n_arr,)), pltpu.SemaphoreType.DMA((per * n_arr,)),
                        pltpu.SemaphoreType.DMA((n_arr,))],
    )(*shards)


ASYNC_GROUPS = (("attn", ("w_qb", "w_kvb")), ("out", ("w_out",)), ("ffn", ("w_gate", "w_up", "w_down", "w_pg", "w_pp")))
TRANSPOSED = ("w_in", "w_gate", "w_up")
HBM_SPEC = pl.BlockSpec(memory_space=pltpu.HBM)
SEM_SPEC = pl.BlockSpec(memory_space=pltpu.SEMAPHORE)
IN_FLIGHT = pltpu.SideEffectType.DATAFLOW_SIDE_EFFECTING


def _in_hbm(a):
    return pltpu.with_memory_space_constraint(a, pltpu.HBM)


def _hbm_like(arrs, lead=()):
    return [pltpu.HBM(lead + a.shape, a.dtype) for a in arrs]


def _split_start(name, srcs, lands, after, n_sem, start):
    n = len(srcs)

    def body(*refs):
        src_refs, land_refs = refs[:n], refs[n:2 * n]
        send_sems, recv_sems = refs[2 * n + 1], refs[2 * n + 2]
        token = refs[-1]

        def copy(send_idx, recv_idx, src, dst, to):
            return pltpu.make_async_remote_copy(src_ref=src, dst_ref=dst, send_sem=send_sems.at[send_idx],
                                                recv_sem=recv_sems.at[recv_idx], device_id=to, device_id_type=MESH)

        for cp in start(src_refs, land_refs, copy):
            cp.start()
        token[...] = jnp.zeros_like(token)

    sem = pltpu.SemaphoreType.DMA((n_sem,))
    outs = pl.pallas_call(
        body, name=name, in_specs=[HBM_SPEC] * (2 * n) + [pl.BlockSpec(memory_space=pl.ANY)],
        out_specs=[SEM_SPEC, SEM_SPEC] + [HBM_SPEC] * (2 * n) + [pl.BlockSpec(memory_space=pltpu.VMEM)],
        out_shape=[sem, sem] + _hbm_like(srcs) + _hbm_like(lands) + [jax.ShapeDtypeStruct((8, LANE), F32)],
        input_output_aliases={i: 2 + i for i in range(2 * n)},
        compiler_params=pltpu.CompilerParams(has_side_effects=IN_FLIGHT),
    )(*[_in_hbm(a) for a in srcs], *[_in_hbm(a) for a in lands], after)
    return (outs[0], outs[1], outs[2:2 + n], outs[2 + n:2 + 2 * n]), outs[-1]


def _split_wait(name, send_sems, recv_sems, srcs, lands, after, waits):
    n = len(srcs)

    def body(*refs):
        src_refs, land_refs = refs[:n], refs[n:2 * n]
        send_ref, recv_ref = refs[2 * n], refs[2 * n + 1]

        def copy(send_idx, recv_idx, src, dst, to):
            return pltpu.make_async_remote_copy(src_ref=src, dst_ref=dst, send_sem=send_ref.at[send_idx],
                                                recv_sem=recv_ref.at[recv_idx], device_id=to, device_id_type=MESH)

        for cp in waits(src_refs, land_refs, copy):
            cp.wait_send()
            cp.wait_recv()

    outs = pl.pallas_call(
        body, name=name, in_specs=[HBM_SPEC] * (2 * n) + [SEM_SPEC, SEM_SPEC, pl.BlockSpec(memory_space=pl.ANY)],
        out_specs=[HBM_SPEC] * (2 * n), out_shape=_hbm_like(srcs) + _hbm_like(lands),
        input_output_aliases={i: i for i in range(2 * n)},
        compiler_params=pltpu.CompilerParams(has_side_effects=IN_FLIGHT),
    )(*srcs, *lands, send_sems, recv_sems, after)
    return outs[:n], outs[n:]


GATHER_LATE_SEMS = 2 * (NCHIP - 1)


def _gather_async_start(tag, shards, after):
    def start(srcs, lands, copy):
        k, c, chips, _ = _peers()
        out = []
        for a, (src, dst) in enumerate(zip(srcs, lands)):
            for j, (cx, cy) in enumerate(chips):
                for core in range(2):
                    out.append(copy(GATHER_LATE_SEMS * a + 2 * j + core, GATHER_LATE_SEMS * a + 2 * j + c,
                                    src.at[*_half(c, src.shape)], dst.at[k, *_half(c, src.shape)], (cx, cy, core)))
        return out

    chip = 2 * lax.axis_index("x") + lax.axis_index("y")
    lands = [lax.dynamic_update_slice(lax.empty((NCHIP,) + s.shape, s.dtype), s[None], (chip, 0, 0)) for s in shards]
    return _split_start("gather_%s_start" % tag, shards, lands, after, GATHER_LATE_SEMS * len(shards), start)


def _gather_async_wait(tag, send_sems, recv_sems, shards, lands, after):
    def waits(srcs, lands_, copy):
        _, c, chips, _ = _peers()
        out = []
        for a, (src, dst) in enumerate(zip(srcs, lands_)):
            for j, (cx, cy) in enumerate(chips):
                for core in range(2):
                    idx = GATHER_LATE_SEMS * a + 2 * j + core
                    out.append(copy(idx, idx, src.at[*_half(c, src.shape)], dst.at[2 * cx + cy, *_half(core, src.shape)], (cx, cy, core)))
        return out

    return _split_wait("gather_%s_wait" % tag, send_sems, recv_sems, shards, lands, after, waits)[1]


def _other_devices():
    x, y, c = _coords()
    out = []
    for d in range(1, NDEV):
        tx, ty, tc = x ^ (d >> 2), y ^ ((d >> 1) & 1), c ^ (d & 1)
        out.append((d, (tx, ty, tc), 2 * tx + ty, 4 * tx + 2 * ty + tc))
    return out


def _reduce_async_start(tag, stacks, after):
    def start(srcs, lands, copy):
        x, y, c = _coords()
        me = 4 * x + 2 * y + c
        return [copy((NDEV - 1) * a + d - 1, (NDEV - 1) * a + d - 1, src.at[chip, *_half(to[2], src.shape)], dst.at[me], to)
                for a, (src, dst) in enumerate(zip(srcs, lands)) for d, to, chip, _ in _other_devices()]

    x, y, c = _coords()
    lands = []
    for s in stacks:
        hr, hc = _half_shape(s.shape)
        at = (c * hr, 0) if _half_axis(s.shape) == 0 else (0, c * hc)
        own = lax.dynamic_slice(s, (2 * x + y,) + at, (1, hr, hc))
        lands.append(lax.dynamic_update_slice(lax.empty((NDEV, hr, hc), s.dtype), own, (4 * x + 2 * y + c, 0, 0)))
    return _split_start("reduce_%s_start" % tag, stacks, lands, after, (NDEV - 1) * len(stacks), start)


def _reduce_async_wait(tag, send_sems, recv_sems, stacks, lands, after):
    def waits(srcs, lands_, copy):
        return [copy((NDEV - 1) * a + d - 1, (NDEV - 1) * a + d - 1, src.at[chip, *_half(to[2], src.shape)], dst.at[pos], to)
                for a, (src, dst) in enumerate(zip(srcs, lands_)) for d, to, chip, pos in _other_devices()]

    return _split_wait("reduce_%s_wait" % tag, send_sems, recv_sems, stacks, lands, after, waits)[1]


def _reduce_finish(tag, arrived, dims):
    n_arr = len(arrived)

    def body(*refs):
        lands, fin = refs[:n_arr], refs[n_arr:2 * n_arr]
        send_sems, recv_sems = refs[2 * n_arr:]
        _, c, _, sibling = _peers()
        sends = []
        for a in range(n_arr):
            mine = fin[a].at[*_half(c, dims[a])]

            def device_sum(vs, vf, a=a, mine=mine):
                pltpu.sync_copy(lands[a], vs)
                acc = vs[0].astype(F32)
                for i in range(1, NDEV):
                    acc = acc + vs[i].astype(F32)
                vf[...] = acc
                pltpu.sync_copy(vf, mine)

            pl.run_scoped(device_sum, pltpu.VMEM((NDEV,) + _half_shape(dims[a]), BF16), pltpu.VMEM(_half_shape(dims[a]), F32))
            sends.append(pltpu.make_async_remote_copy(src_ref=mine, dst_ref=mine, send_sem=send_sems.at[a], recv_sem=recv_sems.at[a],
                                                      device_id=sibling, device_id_type=MESH))
            sends[-1].start()
        for a in range(n_arr):
            other = fin[a].at[*_half(1 - c, dims[a])]
            pltpu.make_async_remote_copy(src_ref=other, dst_ref=other, send_sem=send_sems.at[a], recv_sem=recv_sems.at[a],
                                         device_id=sibling, device_id_type=MESH).wait_recv()
        for cp in sends:
            cp.wait_send()

    any_spec = pl.BlockSpec(memory_space=pl.ANY)
    return pl.pallas_call(
        body, name="reduce_%s_finish" % tag, in_specs=[any_spec] * n_arr, out_specs=[any_spec] * n_arr,
        out_shape=[jax.ShapeDtypeStruct(d, F32) for d in dims],
        scratch_shapes=[pltpu.SemaphoreType.DMA((n_arr,)), pltpu.SemaphoreType.DMA((n_arr,))],
    )(*arrived)


SMALL = (("conv_w", SSD_K * SSD_XBC), ("conv_b", SSD_XBC), ("dt_bias", H), ("A_log", H), ("D", H), ("ssd_norm", SSD_INNER),
         ("q_norm", Q_RANK), ("kv_norm", KV_RANK), ("out_norm", SSD_INNER), ("ln_mix_g", D), ("ln_mix_b", D),
         ("ln_ffn_g", D), ("ln_ffn_b", D))
SMALL_ROWS = 120
NDEV = 8


def _allreduce_small(sv):
    def body(sv_ref, out_ref, slots, send_sems, recv_sems):
        x, y, c = _coords()
        me = 4 * x + 2 * y + c
        slots[me] = sv_ref[...]
        copies = []
        for d in range(1, NDEV):
            to = (x ^ (d >> 2), y ^ ((d >> 1) & 1), c ^ (d & 1))
            copies.append(pltpu.make_async_remote_copy(src_ref=sv_ref, dst_ref=slots.at[me], send_sem=send_sems.at[d - 1],
                                                       recv_sem=recv_sems.at[d - 1], device_id=to, device_id_type=MESH))
            copies[-1].start()
        for cp in copies:
            cp.wait_recv()
        for cp in copies:
            cp.wait_send()
        acc = slots[0]
        for i in range(1, NDEV):
            acc = acc + slots[i]
        out_ref[...] = acc

    vm = pl.BlockSpec(memory_space=pltpu.VMEM)
    return pl.pallas_call(
        body, name="allreduce_small", in_specs=[vm], out_specs=vm, out_shape=jax.ShapeDtypeStruct((SMALL_ROWS, LANE), F32),
        scratch_shapes=[pltpu.VMEM((NDEV, SMALL_ROWS, LANE), F32), pltpu.SemaphoreType.DMA((NDEV - 1,)),
                        pltpu.SemaphoreType.DMA((NDEV - 1,))],
    )(sv)


def _adamw_math(w, g, m, v):
    m2 = ADAM_B1 * m + (1.0 - ADAM_B1) * g
    v2 = ADAM_B2 * v + (1.0 - ADAM_B2) * (g * g)
    m_hat = m2 / (1.0 - ADAM_B1 ** ADAM_STEP)
    v_hat = v2 / (1.0 - ADAM_B2 ** ADAM_STEP)
    return -ADAM_LR * (m_hat / (jnp.sqrt(v_hat) + ADAM_EPS) + ADAM_WD * w), m2, v2


def _adamw_big(w, g, m, v, *, name):
    r, c = w.shape

    def body(w_ref, g_ref, m_ref, v_ref, d_ref, m2_ref, v2_ref):
        d_ref[...], m2_ref[...], v2_ref[...] = _adamw_math(w_ref[...], g_ref[...], m_ref[...], v_ref[...])

    if r % 8 == 0:
        tr = next(t for t in (512, 384, 352, 256, 128, 64, 8) if r % t == 0)
        steps, spec = r // tr, pl.BlockSpec((tr, c), lambda i: (i, 0))
    else:
        steps, spec = c // (2 * LANE), pl.BlockSpec((r, 2 * LANE), lambda i: (0, i))
    return pl.pallas_call(body, name=name, grid=(steps,), in_specs=[spec] * 4, out_specs=[spec] * 3,
                          out_shape=[jax.ShapeDtypeStruct((r, c), F32)] * 3)(w, g, m, v)


def _adamw_small(ws, gs, ms, vs):
    n = len(ws)

    def body(*refs):
        for i in range(n):
            w_ref, g_ref, m_ref, v_ref = (refs[j * n + i] for j in range(4))
            d_ref, m2_ref, v2_ref = (refs[(4 + j) * n + i] for j in range(3))
            d_ref[...], m2_ref[...], v2_ref[...] = _adamw_math(w_ref[...], g_ref[...], m_ref[...], v_ref[...])

    vm = pl.BlockSpec(memory_space=pltpu.VMEM)
    shapes = [jax.ShapeDtypeStruct(w.shape, F32) for w in ws]
    outs = pl.pallas_call(body, name="adamw_small", in_specs=[vm] * (4 * n), out_specs=[vm] * (3 * n), out_shape=shapes * 3)(
        *ws, *gs, *ms, *vs)
    return outs[:n], outs[n:2 * n], outs[2 * n:]


_SMALL_ARG = {"conv_w": "ssd_conv_w", "conv_b": "ssd_conv_b", "dt_bias": "ssd_dt_bias", "A_log": "ssd_A_log", "D": "ssd_D",
              "ssd_norm": "ssd_norm_w", "q_norm": "mla_q_norm_w", "kv_norm": "mla_kv_norm_w", "out_norm": "mla_out_norm_w",
              "ln_mix_g": "ln_mix_g", "ln_mix_b": "ln_mix_b", "ln_ffn_g": "ln_ffn_g", "ln_ffn_b": "ln_ffn_b"}
_BIG_ARG = {"w_in": "w_in", "w_qb": "mla_w_q_b", "w_kvb": "mla_w_kv_b", "w_out": "w_out", "w_gate": "w_ffn_gate",
            "w_up": "w_ffn_up", "w_down": "w_ffn_down", "w_pg": "w_ple_gate", "w_pp": "w_ple_proj"}
_WEIGHT_ORDER = ("w_in", "ssd_conv_w", "ssd_conv_b", "ssd_dt_bias", "ssd_A_log", "ssd_D", "ssd_norm_w", "mla_q_norm_w", "mla_w_q_b",
                 "mla_kv_norm_w", "mla_w_kv_b", "mla_out_norm_w", "w_out", "ln_mix_g", "ln_mix_b", "w_ffn_gate", "w_ffn_up",
                 "w_ffn_down", "w_ple_gate", "w_ple_proj", "ln_ffn_g", "ln_ffn_b")


def _rows128(a):
    flat = a.reshape(-1)
    return jnp.pad(flat, (0, -flat.shape[0] % LANE)).reshape(-1, LANE)


def kernel(x, p, positions, w_in, ssd_conv_w, ssd_conv_b, ssd_dt_bias, ssd_A_log, ssd_D, ssd_norm_w, mla_q_norm_w, mla_w_q_b, mla_kv_norm_w, mla_w_kv_b, mla_out_norm_w, w_out, ln_mix_g, ln_mix_b, w_ffn_gate, w_ffn_up, w_ffn_down, w_ple_gate, w_ple_proj, ln_ffn_g, ln_ffn_b, loss_target, m_w_in, m_ssd_conv_w, m_ssd_conv_b, m_ssd_dt_bias, m_ssd_A_log, m_ssd_D, m_ssd_norm_w, m_mla_q_norm_w, m_mla_w_q_b, m_mla_kv_norm_w, m_mla_w_kv_b, m_mla_out_norm_w, m_w_out, m_ln_mix_g, m_ln_mix_b, m_w_ffn_gate, m_w_ffn_up, m_w_ffn_down, m_w_ple_gate, m_w_ple_proj, m_ln_ffn_g, m_ln_ffn_b, v_w_in, v_ssd_conv_w, v_ssd_conv_b, v_ssd_dt_bias, v_ssd_A_log, v_ssd_D, v_ssd_norm_w, v_mla_q_norm_w, v_mla_w_q_b, v_mla_kv_norm_w, v_mla_w_kv_b, v_mla_out_norm_w, v_w_out, v_ln_mix_g, v_ln_mix_b, v_w_ffn_gate, v_w_ffn_up, v_w_ffn_down, v_w_ple_gate, v_w_ple_proj, v_ln_ffn_g, v_ln_ffn_b):
    given = dict(locals())
    chip = 2 * lax.axis_index("x") + lax.axis_index("y")

    def local(name, prefix=""):
        a = given[prefix + _BIG_ARG[name]][0]
        return a.T if name in TRANSPOSED else a

    def global_layout(name, arr):
        return (arr.T if name in TRANSPOSED else arr)[None]

    conv_bits = lax.bitcast_convert_type(ssd_conv_w[0], BF16).reshape(SSD_K, 2 * CONV_SHARD)
    w_in_all, conv_all = _gather_weights([local("w_in").astype(BF16), jnp.pad(conv_bits, ((0, BF16_ROWS - SSD_K), (0, 0)))])
    sp = {k: given[a] for k, a in _SMALL_ARG.items() if k != "conv_w"}
    sp["conv_w"] = _from_cols(lax.bitcast_convert_type(conv_all[:, :SSD_K].reshape(NCHIP, SSD_K, CONV_SHARD, 2), F32))
    gathering, tie = {}, w_in_all
    for group, names in ASYNC_GROUPS:
        gathering[group], tie = _gather_async_start(group, [local(name).astype(BF16) for name in names], tie)

    def fetch(group, after):
        return dict(zip(dict(ASYNC_GROUPS)[group], _gather_async_wait(group, *gathering[group], after)))

    reducing = {}

    def send(group, grads):
        reducing[group], sent = _reduce_async_start(group, [grads[name] for name in dict(ASYNC_GROUPS)[group]], grads[dict(ASYNC_GROUPS)[group][0]])
        return sent[0, 0]

    loss_row, grad_x, dw_in, g = _local_step(x[0] + tie[0, 0], p[0, 0], positions[0], loss_target[0], w_in_all, fetch, send, sp)

    reducing["in"], tie = _reduce_async_start("in", [dw_in], grad_x)
    gbig = {}
    for group, names in reversed(ASYNC_GROUPS):
        arrived = _reduce_async_wait(group, *reducing[group], tie)
        gbig.update(zip(names, _reduce_finish(group, arrived, [local(name).shape for name in names])))
    small_in = jnp.concatenate([_rows128(g[name]) for name, _ in SMALL] + [loss_row], axis=0)
    small_sum = _allreduce_small(jnp.pad(small_in, ((0, SMALL_ROWS - small_in.shape[0]), (0, 0))))
    gsmall, row = {}, 0
    for name, size in SMALL:
        nrow = -(-size // LANE)
        gsmall[name] = small_sum[row:row + nrow].reshape(-1)[:size]
        row += nrow
    loss = small_sum[row, 0]

    grads = {_BIG_ARG[name]: global_layout(name, arr) for name, arr in gbig.items()}
    for name, _ in SMALL:
        if name == "conv_w":
            full_g = gsmall[name].reshape(SSD_K, SSD_XBC)
            grads["ssd_conv_w"] = lax.dynamic_slice(full_g, (0, chip * CONV_SHARD), (SSD_K, CONV_SHARD))[None]
        else:
            grads[_SMALL_ARG[name]] = gsmall[name].reshape(given[_SMALL_ARG[name]].shape)

    delta, new_m, new_v = {}, {}, {}

    def update_matrix(name, grad):
        a = _BIG_ARG[name]
        d, m2, v2 = _adamw_big(local(name), grad, local(name, "m_"), local(name, "v_"), name="adamw_" + a)
        delta[a], new_m[a], new_v[a] = (global_layout(name, t) for t in (d, m2, v2))
        return d

    for name, grad in gbig.items():
        last = update_matrix(name, grad)
    g_in = _reduce_finish("in", _reduce_async_wait("in", *reducing["in"], last), [local("w_in").shape])[0]
    grads["w_in"] = global_layout("w_in", g_in)
    update_matrix("w_in", g_in)
    small_names = [_SMALL_ARG[name] for name, _ in SMALL]
    two_d = lambda t: t.reshape(t.shape[-2], t.shape[-1])
    ds, ms, vs = _adamw_small([two_d(given[a]) for a in small_names], [two_d(grads[a]) for a in small_names],
                              [two_d(given["m_" + a]) for a in small_names], [two_d(given["v_" + a]) for a in small_names])
    for a, d, m2, v2 in zip(small_names, ds, ms, vs):
        delta[a], new_m[a], new_v[a] = (t.reshape(given[a].shape) for t in (d, m2, v2))

    return (loss, grad_x[None], *[grads[n] for n in _WEIGHT_ORDER], *[delta[n] for n in _WEIGHT_ORDER],
            *[new_m[n] for n in _WEIGHT_ORDER], *[new_v[n] for n in _WEIGHT_ORDER])
```

```python
import functools
import math

import jax
import jax.numpy as jnp
from jax import lax
from jax.experimental import pallas as pl
from jax.experimental.pallas import tpu as pltpu

F32 = jnp.float32
BF16 = jnp.bfloat16

S = 2048
D = 1024
PLE = 256
H = 16
SSD_P = 64
SSD_INNER = 1024
SSD_N = 128
SSD_G = 2
SSD_L = 128
SSD_NC = S // SSD_L
SSD_XBC = 1536
SSD_K = 4
Q_RANK = 384
KV_RANK = 256
NOPE = 64
ROPE = 32
VDIM = 64
D_FF = 2816
IN_WIDTH = 3248
ALPHA = 2.0 ** 0.25
EPS_RMS = 1e-6
EPS_LN = 1e-5
ATT_SCALE = 1.0 / math.sqrt(NOPE + ROPE)
LN2 = math.log(2.0)
ATT_SCALE_LOG2 = ATT_SCALE / LN2
LANE = 128
NCHIP = 4
SMALL_W = 896
SM_Q, SM_KV, SM_KR, SM_DT = 0, 384, 640, 768
NEG = -1e30

ADAM_LR = 0.001
ADAM_B1 = 0.9
ADAM_B2 = 0.999
ADAM_EPS = 1e-08
ADAM_WD = 0.01
ADAM_STEP = 10


def _sigmoid(v):
    return 1.0 / (1.0 + jnp.exp(-v))


MM_VMEM_BUDGET = 36 * 2 ** 20
MM_MAX_ACC = 2048 * 1024


def _mm_tiles(pairs, ta, tb, m, n, out_dtype, has_add):
    def divs(v):
        return [LANE * d for d in range(v // LANE, 0, -1) if (v // LANE) % d == 0] if v % LANE == 0 else [v]

    def cost(tm, tn):
        tot = tm * tn * (jnp.dtype(out_dtype).itemsize + (4 if has_add else 0))
        for a, b in pairs:
            k = a.shape[-2] if ta else a.shape[-1]
            tot += k * (tm * a.dtype.itemsize + tn * b.dtype.itemsize)
        return 2 * tot

    ok = [(tm * tn, tm, tn) for tm in divs(m) for tn in divs(n) if tm * tn <= MM_MAX_ACC and cost(tm, tn) <= MM_VMEM_BUDGET]
    _, tm, tn = max(ok)
    return tm, tn


def _mm(pairs, *, ta=False, tb=False, out_dtype=F32, add=None, add_scale=1.0, chunk=None, name):
    n_pairs = len(pairs)
    a0, b0 = pairs[0]
    m = a0.shape[-1] if ta else a0.shape[-2]
    n = b0.shape[-2] if tb else b0.shape[-1]
    tm, tn = _mm_tiles(pairs, ta, tb, m, n, out_dtype, add is not None)
    dims = (((0 if ta else 1,), (1 if tb else 0,)), ((), ()))
    nk = NCHIP if chunk else 1
    assert chunk != "sum" or out_dtype == F32

    def body(*refs):
        o_ref = refs[-1]
        acc = None
        for i in range(n_pairs):
            a = refs[2 * i][...].astype(BF16)
            b = refs[2 * i + 1][...].astype(BF16)
            part = lax.dot_general(a, b, dims, preferred_element_type=F32)
            acc = part if acc is None else acc + part
        if chunk == "sum":
            k = pl.program_id(2)

            @pl.when(k == 0)
            def _():
                o_ref[...] = acc + add_scale * refs[2 * n_pairs][...] if add is not None else acc

            @pl.when(k > 0)
            def _():
                o_ref[...] += acc
        else:
            if add is not None:
                acc = acc + add_scale * refs[2 * n_pairs][...]
            o_ref[...] = acc.astype(out_dtype)

    def spec(arr, shape, idx2):
        if arr.ndim == 3:
            return pl.BlockSpec((None,) + shape, lambda i, j, k: (k,) + idx2(i, j))
        return pl.BlockSpec(shape, lambda i, j, k: idx2(i, j))

    in_specs, args = [], []
    for a, b in pairs:
        kdim = a.shape[-2] if ta else a.shape[-1]
        in_specs.append(spec(a, (kdim, tm), lambda i, j: (0, i)) if ta else spec(a, (tm, kdim), lambda i, j: (i, 0)))
        in_specs.append(spec(b, (tn, kdim), lambda i, j: (j, 0)) if tb else spec(b, (kdim, tn), lambda i, j: (0, j)))
        args += [a, b]
    if add is not None:
        in_specs.append(pl.BlockSpec((tm, tn), lambda i, j, k: (i, j)))
        args.append(add)
    if chunk == "out":
        out_spec = pl.BlockSpec((None, tm, tn), lambda i, j, k: (k, i, j))
        out_shape = jax.ShapeDtypeStruct((nk, m, n), out_dtype)
    else:
        out_spec = pl.BlockSpec((tm, tn), lambda i, j, k: (i, j))
        out_shape = jax.ShapeDtypeStruct((m, n), out_dtype)
    return pl.pallas_call(
        body, name=name, grid=(m // tm, n // tn, nk), in_specs=in_specs, out_specs=out_spec, out_shape=out_shape,
        compiler_params=pltpu.CompilerParams(dimension_semantics=("parallel", "parallel", "arbitrary")),
    )(*args)


TR = 256


def _row_spec(c):
    return pl.BlockSpec((TR, c), lambda i: (i, 0))


def _vec_spec(c):
    return pl.BlockSpec((1, c), lambda i: (0, 0))


def _acc_rows(ref, val):
    @pl.when(pl.program_id(0) == 0)
    def _():
        ref[...] = jnp.zeros_like(ref)
    ref[...] += val


def _rms_fwd(u, w, *, name):
    c = u.shape[1]

    def body(u_ref, w_ref, o_ref):
        v = u_ref[...]
        r = lax.rsqrt(jnp.mean(v * v, axis=-1, keepdims=True) + EPS_RMS)
        o_ref[...] = (v * r * w_ref[...]).astype(BF16)

    return pl.pallas_call(body, name=name, grid=(S // TR,), in_specs=[_row_spec(c), _vec_spec(c)], out_specs=_row_spec(c),
                          out_shape=jax.ShapeDtypeStruct((S, c), BF16))(u, w)


def _rms_bwd(u, w, dy, *, name):
    c = u.shape[1]

    def body(u_ref, w_ref, dy_ref, du_ref, dw_ref):
        v = u_ref[...]
        g = dy_ref[...].astype(F32)
        r = lax.rsqrt(jnp.mean(v * v, axis=-1, keepdims=True) + EPS_RMS)
        gw = g * w_ref[...]
        du_ref[...] = r * gw - v * (r * r * r * jnp.mean(gw * v, axis=-1, keepdims=True))
        _acc_rows(dw_ref, jnp.sum(g * v * r, axis=0, keepdims=True))

    return pl.pallas_call(body, name=name, grid=(S // TR,), in_specs=[_row_spec(c), _vec_spec(c), _row_spec(c)],
                          out_specs=[_row_spec(c), _vec_spec(c)],
                          out_shape=[jax.ShapeDtypeStruct((S, c), F32), jax.ShapeDtypeStruct((1, c), F32)])(u, w, dy)


def _gate_norm_fwd(y, z, w):
    def body(y_ref, z_ref, w_ref, o_ref):
        zz = z_ref[...]
        v = y_ref[...] * (zz * _sigmoid(zz))
        r = lax.rsqrt(jnp.mean(v * v, axis=-1, keepdims=True) + EPS_RMS)
        o_ref[...] = (v * r * w_ref[...]).astype(BF16)

    c = SSD_INNER
    return pl.pallas_call(body, name="ssd_gate_norm_fwd", grid=(S // TR,), in_specs=[_row_spec(c), _row_spec(c), _vec_spec(c)],
                          out_specs=_row_spec(c), out_shape=jax.ShapeDtypeStruct((S, c), BF16))(y, z, w)


def _gate_norm_bwd(y, z, w, dout):
    def body(y_ref, z_ref, w_ref, g_ref, dy_ref, dz_ref, dw_ref):
        yy = y_ref[...]
        zz = z_ref[...]
        sg = _sigmoid(zz)
        sz = zz * sg
        v = yy * sz
        g = g_ref[...]
        r = lax.rsqrt(jnp.mean(v * v, axis=-1, keepdims=True) + EPS_RMS)
        gw = g * w_ref[...]
        dv = r * gw - v * (r * r * r * jnp.mean(gw * v, axis=-1, keepdims=True))
        dy_ref[...] = dv * sz
        dz_ref[...] = (dv * yy * (sg * (1.0 + zz * (1.0 - sg)))).astype(BF16)
        _acc_rows(dw_ref, jnp.sum(g * v * r, axis=0, keepdims=True))

    c = SSD_INNER
    return pl.pallas_call(body, name="ssd_gate_norm_bwd", grid=(S // TR,),
                          in_specs=[_row_spec(c), _row_spec(c), _vec_spec(c), _row_spec(c)],
                          out_specs=[_row_spec(c), _row_spec(c), _vec_spec(c)],
                          out_shape=[jax.ShapeDtypeStruct((S, c), F32), jax.ShapeDtypeStruct((S, c), BF16),
                                     jax.ShapeDtypeStruct((1, c), F32)])(y, z, w, dout)


def _ln_fwd(xr, mix, g, b):
    def body(x_ref, m_ref, g_ref, b_ref, o_ref, ob_ref):
        pre = ALPHA * x_ref[...] + m_ref[...]
        mu = jnp.mean(pre, axis=-1, keepdims=True)
        d = pre - mu
        rs = lax.rsqrt(jnp.mean(d * d, axis=-1, keepdims=True) + EPS_LN)
        h = d * rs * g_ref[...] + b_ref[...]
        o_ref[...] = h
        ob_ref[...] = h.astype(BF16)

    return pl.pallas_call(body, name="ln_mix_fwd", grid=(S // TR,), in_specs=[_row_spec(D), _row_spec(D), _vec_spec(D), _vec_spec(D)],
                          out_specs=[_row_spec(D)] * 2,
                          out_shape=[jax.ShapeDtypeStruct((S, D), F32), jax.ShapeDtypeStruct((S, D), BF16)])(xr, mix, g, b)


def _ln_bwd(xr, mix, g, dh):
    def body(x_ref, m_ref, g_ref, dh_ref, dpre_ref, dpreb_ref, dg_ref, db_ref):
        pre = ALPHA * x_ref[...] + m_ref[...]
        mu = jnp.mean(pre, axis=-1, keepdims=True)
        d = pre - mu
        rs = lax.rsqrt(jnp.mean(d * d, axis=-1, keepdims=True) + EPS_LN)
        xh = d * rs
        dy = dh_ref[...]
        gy = dy * g_ref[...]
        dpre = rs * (gy - jnp.mean(gy, axis=-1, keepdims=True) - xh * jnp.mean(gy * xh, axis=-1, keepdims=True))
        dpre_ref[...] = dpre
        dpreb_ref[...] = dpre.astype(BF16)
        _acc_rows(dg_ref, jnp.sum(dy * xh, axis=0, keepdims=True))
        _acc_rows(db_ref, jnp.sum(dy, axis=0, keepdims=True))

    return pl.pallas_call(body, name="ln_mix_bwd", grid=(S // TR,),
                          in_specs=[_row_spec(D), _row_spec(D), _vec_spec(D), _row_spec(D)],
                          out_specs=[_row_spec(D), _row_spec(D), _vec_spec(D), _vec_spec(D)],
                          out_shape=[jax.ShapeDtypeStruct((S, D), F32), jax.ShapeDtypeStruct((S, D), BF16),
                                     jax.ShapeDtypeStruct((1, D), F32), jax.ShapeDtypeStruct((1, D), F32)])(xr, mix, g, dh)


FF_CHUNK = D_FF // NCHIP


FF_ROWS = 1024


def _ff_act_spec():
    return pl.BlockSpec((None, FF_ROWS, FF_CHUNK), lambda i, k: (k, i, 0))


def _ff_w_spec():
    return pl.BlockSpec((None, FF_CHUNK, D), lambda i, k: (k, 0, 0))


def _ffn_hidden_fwd(h, w_gate_t, w_up_t):
    def body(h_ref, wg_ref, wu_ref, g_ref, u_ref, a_ref):
        hh = h_ref[...]
        g = _dot(hh, wg_ref[...], ((1,), (1,)))
        u = _dot(hh, wu_ref[...], ((1,), (1,)))
        g_ref[...] = g.astype(BF16)
        u_ref[...] = u.astype(BF16)
        a_ref[...] = (g * _sigmoid(g) * u).astype(BF16)

    return pl.pallas_call(
        body, name="ffn_hidden_fwd", grid=(S // FF_ROWS, NCHIP),
        in_specs=[pl.BlockSpec((FF_ROWS, D), lambda i, k: (i, 0)), _ff_w_spec(), _ff_w_spec()], out_specs=[_ff_act_spec()] * 3,
        out_shape=[jax.ShapeDtypeStruct((NCHIP, S, FF_CHUNK), BF16)] * 3,
        compiler_params=pltpu.CompilerParams(dimension_semantics=("parallel", "parallel")),
    )(h, w_gate_t, w_up_t)


def _ffn_hidden_bwd(dout, w_down, gate, up):
    def body(d_ref, wd_ref, g_ref, u_ref, dg_ref, du_ref):
        d = _dot(d_ref[...], wd_ref[...], ((1,), (1,)))
        g = g_ref[...].astype(F32)
        sg = _sigmoid(g)
        dg_ref[...] = (d * u_ref[...].astype(F32) * (sg * (1.0 + g * (1.0 - sg)))).astype(BF16)
        du_ref[...] = (d * g * sg).astype(BF16)

    return pl.pallas_call(
        body, name="ffn_hidden_bwd", grid=(S // FF_ROWS, NCHIP),
        in_specs=[pl.BlockSpec((FF_ROWS, D), lambda i, k: (i, 0)), _ff_w_spec(), _ff_act_spec(), _ff_act_spec()],
        out_specs=[_ff_act_spec()] * 2, out_shape=[jax.ShapeDtypeStruct((NCHIP, S, FF_CHUNK), BF16)] * 2,
        compiler_params=pltpu.CompilerParams(dimension_semantics=("parallel", "parallel")),
    )(dout, w_down, gate, up)


def _final_fwd_bwd(h1, ffn, pg, pp, target, g2, b2):
    def body(h_ref, f_ref, pg_ref, pp_ref, t_ref, g_ref, b_ref, dpre_ref, dpreb_ref, dpg_ref, dpp_ref, dg_ref, db_ref, loss_ref):
        sg = _sigmoid(pg_ref[...])
        ppv = pp_ref[...]
        pre = ALPHA * h_ref[...] + f_ref[...] + sg * ppv
        mu = jnp.mean(pre, axis=-1, keepdims=True)
        d = pre - mu
        rs = lax.rsqrt(jnp.mean(d * d, axis=-1, keepdims=True) + EPS_LN)
        xh = d * rs
        err = xh * g_ref[...] + b_ref[...] - t_ref[...]
        dy = err * (1.0 / D)
        gy = dy * g_ref[...]
        dpre = rs * (gy - jnp.mean(gy, axis=-1, keepdims=True) - xh * jnp.mean(gy * xh, axis=-1, keepdims=True))
        dpre_ref[...] = dpre
        dpreb_ref[...] = dpre.astype(BF16)
        dpg_ref[...] = (dpre * ppv * sg * (1.0 - sg)).astype(BF16)
        dpp_ref[...] = (dpre * sg).astype(BF16)
        _acc_rows(dg_ref, jnp.sum(dy * xh, axis=0, keepdims=True))
        _acc_rows(db_ref, jnp.sum(dy, axis=0, keepdims=True))
        _acc_rows(loss_ref, 0.5 * jnp.sum(jnp.mean(err * err, axis=-1, keepdims=True), axis=0, keepdims=True) * jnp.ones((1, LANE), F32))

    return pl.pallas_call(
        body, name="final_ln_loss", grid=(S // TR,),
        in_specs=[_row_spec(D)] * 5 + [_vec_spec(D)] * 2,
        out_specs=[_row_spec(D)] * 4 + [_vec_spec(D), _vec_spec(D), _vec_spec(LANE)],
        out_shape=[jax.ShapeDtypeStruct((S, D), F32)] + [jax.ShapeDtypeStruct((S, D), BF16)] * 3 + [
                   jax.ShapeDtypeStruct((1, D), F32), jax.ShapeDtypeStruct((1, D), F32), jax.ShapeDtypeStruct((1, LANE), F32)],
    )(h1, ffn, pg, pp, target, g2, b2)


def _rot(u, cos_t, sin_t, lane):
    partner = jnp.where(lane < NOPE + ROPE // 2, pltpu.roll(u, LANE - ROPE // 2, 1), pltpu.roll(u, ROPE // 2, 1))
    return u * cos_t + partner * sin_t


def _rms(v, w):
    r = lax.rsqrt(jnp.mean(v * v, axis=-1, keepdims=True) + EPS_RMS)
    return v * r * w, r


def _rms_grad(v, r, w, g):
    gw = g * w
    return r * gw - v * (r * r * r * jnp.mean(gw * v, axis=-1, keepdims=True)), jnp.sum(g * v * r, axis=0, keepdims=True)


def _whole(arr):
    return pl.BlockSpec(arr.shape, lambda i: (0,) * arr.ndim)


def _qkv_fwd(small, w_q, w_k, w_v, q_norm, kv_norm, cos_t, sin_t):
    def body(sm_ref, wq_ref, wk_ref, wv_ref, qw_ref, kw_ref, c_ref, s_ref, qn_ref, kvn_ref, q_ref, k_ref, kt_ref, v_ref):
        lane = lax.broadcasted_iota(jnp.int32, (TR, LANE), 1)
        c, s = c_ref[...], s_ref[...]
        qn = _rms(sm_ref[:, SM_Q:SM_Q + Q_RANK], qw_ref[...])[0].astype(BF16)
        kvn = _rms(sm_ref[:, SM_KV:SM_KV + KV_RANK], kw_ref[...])[0].astype(BF16)
        qn_ref[...] = qn
        kvn_ref[...] = kvn
        kr = _rot(pltpu.roll(sm_ref[:, SM_KR:SM_KR + LANE], NOPE, 1), c, s, lane)
        for h in range(H):
            tile = slice(h * LANE, (h + 1) * LANE)
            q_ref[:, tile] = _rot(_dot(qn, wq_ref[:, tile], ((1,), (0,))), c, s, lane).astype(BF16)
            kt = _dot(kvn, wk_ref[:, tile], ((1,), (0,))) + kr
            k_ref[:, tile] = kt.astype(BF16)
            kt_ref[tile, :] = kt.T.astype(BF16)
        v_ref[...] = _dot(kvn, wv_ref[...], ((1,), (0,))).astype(BF16)

    w = H * LANE
    return pl.pallas_call(
        body, name="qkv_fwd", grid=(S // TR,),
        in_specs=[_row_spec(SMALL_W), _whole(w_q), _whole(w_k), _whole(w_v), _vec_spec(Q_RANK), _vec_spec(KV_RANK), _row_spec(LANE), _row_spec(LANE)],
        out_specs=[_row_spec(Q_RANK), _row_spec(KV_RANK), _row_spec(w), _row_spec(w), pl.BlockSpec((w, TR), lambda i: (0, i)),
                   _row_spec(H * VDIM)],
        out_shape=[jax.ShapeDtypeStruct((S, Q_RANK), BF16), jax.ShapeDtypeStruct((S, KV_RANK), BF16), jax.ShapeDtypeStruct((S, w), BF16),
                   jax.ShapeDtypeStruct((S, w), BF16), jax.ShapeDtypeStruct((w, S), BF16), jax.ShapeDtypeStruct((S, H * VDIM), BF16)],
    )(small, w_q, w_k, w_v, q_norm, kv_norm, cos_t, sin_t)


def _qkv_bwd(dqt, dk, dv, small, w_q, w_k, w_v, q_norm, kv_norm, cos_t, sin_t):
    def body(dq_ref, dk_ref, dv_ref, sm_ref, wq_ref, wk_ref, wv_ref, qw_ref, kw_ref, c_ref, s_ref,
             ds_ref, dql_ref, dkb_ref, dqw_ref, dkw_ref):
        lane = lax.broadcasted_iota(jnp.int32, (TR, LANE), 1)
        c, s = c_ref[...], -s_ref[...]
        dqn = jnp.zeros((TR, Q_RANK), F32)
        dkvn = _dot(dv_ref[...], wv_ref[...], ((1,), (1,)))
        dkr = jnp.zeros((TR, LANE), F32)
        for h in range(H):
            tile = slice(h * LANE, (h + 1) * LANE)
            dql = _rot(dq_ref[tile, :].T, c, s, lane).astype(BF16)
            dql_ref[:, tile] = dql
            dqn = dqn + _dot(dql, wq_ref[:, tile], ((1,), (1,)))
            dkt = dk_ref[:, tile]
            dkb_ref[:, tile] = dkt.astype(BF16)
            dkvn = dkvn + _dot(dkt, wk_ref[:, tile], ((1,), (1,)))
            dkr = dkr + dkt
        dkr = jnp.where((lane >= NOPE) & (lane < NOPE + ROPE), dkr, 0.0)
        q_c, kv_c = sm_ref[:, SM_Q:SM_Q + Q_RANK], sm_ref[:, SM_KV:SM_KV + KV_RANK]
        dq_c, dqw = _rms_grad(q_c, _rms(q_c, qw_ref[...])[1], qw_ref[...], dqn)
        dkv_c, dkw = _rms_grad(kv_c, _rms(kv_c, kw_ref[...])[1], kw_ref[...], dkvn)
        ds_ref[:, SM_Q:SM_Q + Q_RANK] = dq_c.astype(BF16)
        ds_ref[:, SM_KV:SM_KV + KV_RANK] = dkv_c.astype(BF16)
        ds_ref[:, SM_KR:SM_KR + LANE] = pltpu.roll(_rot(dkr, c, s, lane), LANE - NOPE, 1).astype(BF16)
        _acc_rows(dqw_ref, dqw)
        _acc_rows(dkw_ref, dkw)

    w = H * LANE
    return pl.pallas_call(
        body, name="qkv_bwd", grid=(S // TR,),
        in_specs=[pl.BlockSpec((w, TR), lambda i: (0, i)), _row_spec(w), _row_spec(H * VDIM), _row_spec(SMALL_W), _whole(w_q), _whole(w_k),
                  _whole(w_v), _vec_spec(Q_RANK), _vec_spec(KV_RANK), _row_spec(LANE), _row_spec(LANE)],
        out_specs=[_row_spec(SM_DT), _row_spec(w), _row_spec(w), _vec_spec(Q_RANK), _vec_spec(KV_RANK)],
        out_shape=[jax.ShapeDtypeStruct((S, SM_DT), BF16), jax.ShapeDtypeStruct((S, w), BF16), jax.ShapeDtypeStruct((S, w), BF16),
                   jax.ShapeDtypeStruct((1, Q_RANK), F32), jax.ShapeDtypeStruct((1, KV_RANK), F32)],
    )(dqt, dk, dv, small, w_q, w_k, w_v, q_norm, kv_norm, cos_t, sin_t)


CB = 256


def _shift_down(u, k, row):
    if k == 0:
        return u
    return jnp.where(row >= k, pltpu.roll(u, k, 0), 0.0)


def _shift_up(u, k, row):
    if k == 0:
        return u
    return jnp.where(row < S - k, pltpu.roll(u, S - k, 0), 0.0)


def _conv_fwd(u, w, b):
    def body(u_ref, w_ref, b_ref, o_ref):
        row = lax.broadcasted_iota(jnp.int32, (S, CB), 0)
        uu = u_ref[...]
        acc = b_ref[...] + w_ref[SSD_K - 1:SSD_K, :] * uu
        for k in range(SSD_K - 1):
            acc = acc + w_ref[k:k + 1, :] * _shift_down(uu, SSD_K - 1 - k, row)
        o_ref[...] = acc * _sigmoid(acc)

    c = u.shape[1]
    return pl.pallas_call(
        body, name="conv_fwd", grid=(c // CB,),
        in_specs=[pl.BlockSpec((S, CB), lambda j: (0, j)), pl.BlockSpec((SSD_K, CB), lambda j: (0, j)), pl.BlockSpec((1, CB), lambda j: (0, j))],
        out_specs=pl.BlockSpec((S, CB), lambda j: (0, j)), out_shape=jax.ShapeDtypeStruct((S, c), F32),
    )(u, w, b)


def _conv_bwd(u, w, b, dact):
    def body(u_ref, w_ref, b_ref, d_ref, du_ref, dw_ref, db_ref):
        row = lax.broadcasted_iota(jnp.int32, (S, CB), 0)
        uu = u_ref[...]
        sh = [_shift_down(uu, SSD_K - 1 - k, row) for k in range(SSD_K)]
        acc = b_ref[...]
        for k in range(SSD_K):
            acc = acc + w_ref[k:k + 1, :] * sh[k]
        sg = _sigmoid(acc)
        dacc = d_ref[...] * (sg * (1.0 + acc * (1.0 - sg)))
        du = w_ref[SSD_K - 1:SSD_K, :] * dacc
        for k in range(SSD_K - 1):
            du = du + w_ref[k:k + 1, :] * _shift_up(dacc, SSD_K - 1 - k, row)
        du_ref[...] = du.astype(BF16)
        for k in range(SSD_K):
            dw_ref[k:k + 1, :] = jnp.sum(dacc * sh[k], axis=0, keepdims=True)
        db_ref[...] = jnp.sum(dacc, axis=0, keepdims=True)

    c = u.shape[1]
    col = lambda r: pl.BlockSpec((r, CB), lambda j: (0, j))
    return pl.pallas_call(
        body, name="conv_bwd", grid=(c // CB,), in_specs=[col(S), col(SSD_K), col(1), col(S)], out_specs=[col(S), col(SSD_K), col(1)],
        out_shape=[jax.ShapeDtypeStruct((S, c), BF16), jax.ShapeDtypeStruct((SSD_K, c), F32), jax.ShapeDtypeStruct((1, c), F32)],
    )(u, w, b, dact)


NPAIR = H // 2
PAIRS_PER_GROUP = NPAIR // SSD_G


def _softplus(v):
    return jnp.maximum(v, 0.0) + jnp.log(1.0 + jnp.exp(-jnp.abs(v)))


def _dot(a, b, dims):
    return lax.dot_general(a.astype(BF16), b.astype(BF16), (dims, ((), ())), preferred_element_type=F32)


def _dot2(a, sel):
    hi = a.astype(BF16)
    lo = (a - hi.astype(F32)).astype(BF16)
    dims = (((1,), (0,)), ((), ()))
    return lax.dot_general(hi, sel, dims, preferred_element_type=F32) + lax.dot_general(lo, sel, dims, preferred_element_type=F32)


def _dot3(a, b, dims, split_lhs):
    v = a if split_lhs else b
    v1 = v.astype(BF16)
    r1 = v - v1.astype(F32)
    v2 = r1.astype(BF16)
    v3 = (r1 - v2.astype(F32)).astype(BF16)
    acc = None
    for part in (v1, v2, v3):
        lhs, rhs = (part, b) if split_lhs else (a, part)
        t = lax.dot_general(lhs, rhs, (dims, ((), ())), preferred_element_type=F32)
        acc = t if acc is None else acc + t
    return acc


def _ssd_chunk_common(dt_ref, dtT_ref, prow_ref, pcol_ref):
    prow = prow_ref[...]
    pcol = pcol_ref[...]
    ri = lax.broadcasted_iota(jnp.int32, (SSD_L, SSD_L), 0)
    ci = lax.broadcasted_iota(jnp.int32, (SSD_L, SSD_L), 1)
    causal = ri >= ci
    pre_c = dt_ref[...] + prow[0:1, :]
    dtc = _softplus(pre_c)
    a_row = -jnp.exp(prow[1:2, :])
    cs_col = _dot3(causal.astype(BF16), dtc * a_row, ((1,), (0,)), False)
    dtr = _softplus(dtT_ref[...] + pcol[:, 0:1])
    a_col = -jnp.exp(pcol[:, 1:2])
    cs_row = _dot3(dtr * a_col, (ri <= ci).astype(BF16), ((1,), (0,)), True)
    return prow, causal, pre_c, dtc, a_row, cs_col, cs_row


def _ssd_fwd(act, small, dtT, prow, pcol):
    def body(x_ref, b_ref, c_ref, dt_ref, dtT_ref, prow_ref, pcol_ref, y_ref, st_ref, state):
        @pl.when(pl.program_id(0) == 0)
        def _():
            state[...] = jnp.zeros_like(state)

        prow, causal, _, dtc, _, cs_col, cs_row = _ssd_chunk_common(dt_ref, dtT_ref, prow_ref, pcol_ref)
        lo = lax.broadcasted_iota(jnp.int32, (SSD_L, LANE), 1) < SSD_P
        lo1 = lo[0:1, :]
        for g in range(SSD_G):
            bm = b_ref[:, g * SSD_N:(g + 1) * SSD_N]
            cm = c_ref[:, g * SSD_N:(g + 1) * SSD_N]
            cb = _dot(cm, bm, ((1,), (1,)))
            for qq in range(PAIRS_PER_GROUP):
                q = g * PAIRS_PER_GROUP + qq
                ha, hb = 2 * q, 2 * q + 1
                csa, csb = cs_col[:, ha:ha + 1], cs_col[:, hb:hb + 1]
                xp = x_ref[:, q * LANE:(q + 1) * LANE]
                xx = xp * jnp.where(lo, dtc[:, ha:ha + 1], dtc[:, hb:hb + 1])
                ga = cb * jnp.exp(jnp.where(causal, csa - cs_row[ha:ha + 1, :], NEG))
                gb = cb * jnp.exp(jnp.where(causal, csb - cs_row[hb:hb + 1, :], NEG))
                y = _dot(ga, jnp.where(lo, xx, 0.0), ((1,), (0,))) + _dot(gb, jnp.where(lo, 0.0, xx), ((1,), (0,)))
                s_in = state[q]
                y = y + _dot(cm, s_in, ((1,), (0,))) * jnp.where(lo, jnp.exp(csa), jnp.exp(csb))
                y = y + jnp.where(lo1, prow[2:3, ha:ha + 1], prow[2:3, hb:hb + 1]) * xp
                y_ref[:, q * LANE:(q + 1) * LANE] = y
                la, lb = csa[SSD_L - 1:SSD_L, :], csb[SSD_L - 1:SSD_L, :]
                decay = jnp.where(lo, jnp.exp(la - csa), jnp.exp(lb - csb))
                st_ref[q] = s_in
                state[q] = s_in * jnp.where(lo1, jnp.exp(la), jnp.exp(lb)) + _dot(bm, xx * decay, ((0,), (0,)))

    L = SSD_L
    return pl.pallas_call(
        body, name="ssd_fwd", grid=(SSD_NC,),
        in_specs=[pl.BlockSpec((L, SSD_INNER), lambda c: (c, 0)),
                  pl.BlockSpec((L, SSD_G * SSD_N), lambda c: (c, SSD_INNER // (SSD_G * SSD_N))),
                  pl.BlockSpec((L, SSD_G * SSD_N), lambda c: (c, SSD_INNER // (SSD_G * SSD_N) + 1)),
                  pl.BlockSpec((L, LANE), lambda c: (c, SM_DT // LANE)),
                  pl.BlockSpec((LANE, L), lambda c: (0, c)),
                  pl.BlockSpec((8, LANE), lambda c: (0, 0)), pl.BlockSpec((LANE, 8), lambda c: (0, 0))],
        out_specs=[pl.BlockSpec((L, SSD_INNER), lambda c: (c, 0)),
                   pl.BlockSpec((None, NPAIR, SSD_N, LANE), lambda c: (c, 0, 0, 0))],
        out_shape=[jax.ShapeDtypeStruct((S, SSD_INNER), F32), jax.ShapeDtypeStruct((SSD_NC, NPAIR, SSD_N, LANE), F32)],
        scratch_shapes=[pltpu.VMEM((NPAIR, SSD_N, LANE), F32)],
        compiler_params=pltpu.CompilerParams(dimension_semantics=("arbitrary",)),
    )(act, act, act, small, dtT, prow, pcol)


def _ssd_bwd(act, small, dtT, prow, pcol, states, dy):
    def body(x_ref, b_ref, c_ref, dt_ref, dtT_ref, prow_ref, pcol_ref, st_ref, dy_ref,
             dx_ref, ddt_ref, dp_ref, dstate):
        @pl.when(pl.program_id(0) == 0)
        def _():
            dstate[...] = jnp.zeros_like(dstate)
            dp_ref[...] = jnp.zeros_like(dp_ref)

        prow, causal, pre_c, dtc, a_row, cs_col, cs_row = _ssd_chunk_common(dt_ref, dtT_ref, prow_ref, pcol_ref)
        lane = lax.broadcasted_iota(jnp.int32, (SSD_L, LANE), 1)
        sub = lax.broadcasted_iota(jnp.int32, (LANE, SSD_L), 0)
        rowi = lax.broadcasted_iota(jnp.int32, (SSD_L, 1), 0)
        pick_p = lax.broadcasted_iota(jnp.int32, (LANE, LANE), 0)
        pick_l = lax.broadcasted_iota(jnp.int32, (LANE, LANE), 1)
        lo = lane < SSD_P
        lo1 = lo[0:1, :]
        dcs_c = jnp.zeros((SSD_L, LANE), F32)
        dcs_r = jnp.zeros((LANE, SSD_L), F32)
        ddt_x = jnp.zeros((SSD_L, LANE), F32)
        dd_row = jnp.zeros((1, LANE), F32)
        for g in range(SSD_G):
            bm = b_ref[:, g * SSD_N:(g + 1) * SSD_N]
            cm = c_ref[:, g * SSD_N:(g + 1) * SSD_N]
            cb = _dot(cm, bm, ((1,), (1,)))
            dcb = jnp.zeros((SSD_L, SSD_L), F32)
            dbm = jnp.zeros((SSD_L, SSD_N), F32)
            dcm = jnp.zeros((SSD_L, SSD_N), F32)
            for qq in range(PAIRS_PER_GROUP):
                q = g * PAIRS_PER_GROUP + qq
                ha, hb = 2 * q, 2 * q + 1
                csa, csb = cs_col[:, ha:ha + 1], cs_col[:, hb:hb + 1]
                xp = x_ref[:, q * LANE:(q + 1) * LANE]
                dtp = jnp.where(lo, dtc[:, ha:ha + 1], dtc[:, hb:hb + 1])
                xx = xp * dtp
                lma = jnp.exp(jnp.where(causal, csa - cs_row[ha:ha + 1, :], NEG))
                lmb = jnp.exp(jnp.where(causal, csb - cs_row[hb:hb + 1, :], NEG))
                ga, gb = cb * lma, cb * lmb
                dyp = dy_ref[:, q * LANE:(q + 1) * LANE]
                dya, dyb = jnp.where(lo, dyp, 0.0), jnp.where(lo, 0.0, dyp)
                s_in = st_ref[q]
                ds_out = dstate[q]
                la, lb = csa[SSD_L - 1:SSD_L, :], csb[SSD_L - 1:SSD_L, :]
                ecs = jnp.where(lo, jnp.exp(csa), jnp.exp(csb))
                decay = jnp.where(lo, jnp.exp(la - csa), jnp.exp(lb - csb))
                cd = jnp.where(lo1, jnp.exp(la), jnp.exp(lb))
                bds = _dot(bm, ds_out, ((1,), (0,)))
                dxx = _dot(ga, dya, ((0,), (0,))) + _dot(gb, dyb, ((0,), (0,))) + bds * decay
                dga = _dot(dya, xx, ((1,), (1,)))
                dgb = _dot(dyb, xx, ((1,), (1,)))
                dsega, dsegb = dga * ga, dgb * gb
                dcb = dcb + dga * lma + dgb * lmb
                yoff = _dot(cm, s_in, ((1,), (0,))) * ecs
                dye = dyp * ecs
                dcm = dcm + _dot(dye, s_in, ((1,), (1,)))
                xd = xx * decay
                dbm = dbm + _dot(xd, ds_out, ((1,), (1,)))
                wv = xd * bds
                ends = jnp.sum(wv, axis=0, keepdims=True) + cd * jnp.sum(ds_out * s_in, axis=0, keepdims=True)
                t1 = dyp * yoff - wv + jnp.where(rowi == SSD_L - 1, ends, 0.0)
                to_pair = (((pick_p < SSD_P) & (pick_l == ha)) | ((pick_p >= SSD_P) & (pick_l == hb))).astype(BF16)
                to_a_b = jnp.concatenate([(pick_l == ha).astype(BF16), (pick_l == hb).astype(BF16)], axis=0)
                dcs_c = dcs_c + _dot2(t1, to_pair) + _dot2(jnp.concatenate([dsega, dsegb], axis=1), to_a_b)
                dcs_r = (dcs_r + jnp.where(sub == ha, jnp.sum(dsega, axis=0, keepdims=True), 0.0)
                         + jnp.where(sub == hb, jnp.sum(dsegb, axis=0, keepdims=True), 0.0))
                dstate[q] = _dot(cm, dye, ((0,), (0,))) + cd * ds_out
                dpair = jnp.where(lo1, prow[2:3, ha:ha + 1], prow[2:3, hb:hb + 1])
                dx_ref[:, q * LANE:(q + 1) * LANE] = dxx * dtp + dpair * dyp
                ddt_x = ddt_x + _dot2(dxx * xp, to_pair)
                dd_row = dd_row + jnp.sum(_dot2(dyp * xp, to_pair), axis=0, keepdims=True)
            dx_ref[:, SSD_INNER + g * SSD_N:SSD_INNER + (g + 1) * SSD_N] = dbm + _dot(dcb, cm, ((0,), (0,)))
            dx_ref[:, SSD_INNER + (SSD_G + g) * SSD_N:SSD_INNER + (SSD_G + g + 1) * SSD_N] = dcm + _dot(dcb, bm, ((1,), (0,)))
        ri = lax.broadcasted_iota(jnp.int32, (SSD_L, SSD_L), 0)
        ci = lax.broadcasted_iota(jnp.int32, (SSD_L, SSD_L), 1)
        da = _dot3((ri <= ci).astype(BF16), dcs_c, ((1,), (0,)), False)
        da = da - _dot3(dcs_r, causal.astype(BF16), ((1,), (0,)), True).T
        ddt = ddt_x + da * a_row
        ddt_raw = ddt * _sigmoid(pre_c)
        ddt_ref[...] = ddt_raw
        da_head = jnp.sum(da * dtc, axis=0, keepdims=True) * a_row
        dp_ref[0:1, :] += jnp.sum(ddt_raw, axis=0, keepdims=True)
        dp_ref[1:2, :] += da_head
        dp_ref[2:3, :] += dd_row

    L = SSD_L
    rev = SSD_NC - 1
    bc_cols = SSD_INNER // (SSD_G * SSD_N)
    return pl.pallas_call(
        body, name="ssd_bwd", grid=(SSD_NC,),
        in_specs=[pl.BlockSpec((L, SSD_INNER), lambda c: (rev - c, 0)),
                  pl.BlockSpec((L, SSD_G * SSD_N), lambda c: (rev - c, bc_cols)),
                  pl.BlockSpec((L, SSD_G * SSD_N), lambda c: (rev - c, bc_cols + 1)),
                  pl.BlockSpec((L, LANE), lambda c: (rev - c, SM_DT // LANE)),
                  pl.BlockSpec((LANE, L), lambda c: (0, rev - c)),
                  pl.BlockSpec((8, LANE), lambda c: (0, 0)), pl.BlockSpec((LANE, 8), lambda c: (0, 0)),
                  pl.BlockSpec((None, NPAIR, SSD_N, LANE), lambda c: (rev - c, 0, 0, 0)),
                  pl.BlockSpec((L, SSD_INNER), lambda c: (rev - c, 0))],
        out_specs=[pl.BlockSpec((L, SSD_XBC), lambda c: (rev - c, 0)),
                   pl.BlockSpec((L, LANE), lambda c: (rev - c, 0)),
                   pl.BlockSpec((8, LANE), lambda c: (0, 0))],
        out_shape=[jax.ShapeDtypeStruct((S, SSD_XBC), F32), jax.ShapeDtypeStruct((S, LANE), F32),
                   jax.ShapeDtypeStruct((8, LANE), F32)],
        scratch_shapes=[pltpu.VMEM((NPAIR, SSD_N, LANE), F32)],
        compiler_params=pltpu.CompilerParams(dimension_semantics=("arbitrary",)),
    )(act, act, act, small, dtT, prow, pcol, states, dy)


TQ = 256
TK = 256
FWD_TQ = 256
FWD_TK = 256


def _attn_fwd(qc, kc, v):
    TQ, TK = FWD_TQ, FWD_TK

    def body(q_ref, k_ref, v_ref, o_ref, lse_ref):
        i = pl.program_id(1)
        lo = lax.broadcasted_iota(jnp.int32, (TQ, LANE), 1) < VDIM
        lo_k = lax.broadcasted_iota(jnp.int32, (TK, LANE), 1) < VDIM
        row_minus_col = lax.broadcasted_iota(jnp.int32, (TQ, TK), 0) - lax.broadcasted_iota(jnp.int32, (TQ, TK), 1)
        qa, qb = q_ref[:, 0:LANE], q_ref[:, LANE:2 * LANE]

        def scores(kb):
            kk = k_ref[pl.ds(pl.multiple_of(kb * TK, TK), TK), :]
            return (_dot(qa, kk[:, 0:LANE], ((1,), (1,))) * ATT_SCALE_LOG2, _dot(qb, kk[:, LANE:2 * LANE], ((1,), (1,))) * ATT_SCALE_LOG2)

        def update(kb, sa, sb, stats):
            ma, la, mb, lb, acc = stats
            vv = v_ref[pl.ds(pl.multiple_of(kb * TK, TK), TK), :]
            na = jnp.maximum(ma, jnp.max(sa, axis=1, keepdims=True))
            nb = jnp.maximum(mb, jnp.max(sb, axis=1, keepdims=True))
            pa, pb = jnp.exp2(sa - na), jnp.exp2(sb - nb)
            fa, fb = jnp.exp2(ma - na), jnp.exp2(mb - nb)
            la = fa * la + jnp.sum(pa, axis=1, keepdims=True)
            lb = fb * lb + jnp.sum(pb, axis=1, keepdims=True)
            acc = (acc * jnp.where(lo, fa, fb) + _dot(pa, jnp.where(lo_k, vv, 0), ((1,), (0,)))
                   + _dot(pb, jnp.where(lo_k, 0, vv), ((1,), (0,))))
            return na, la, nb, lb, acc

        def step(kb, carry):
            sa, sb = carry[:2]
            nxt = scores(kb + 1)
            return nxt + update(kb, sa, sb, carry[2:])

        neg = jnp.full((TQ, 1), NEG, F32)
        zero = jnp.zeros((TQ, 1), F32)
        n_full = i * (TQ // TK)
        carry = lax.fori_loop(0, n_full, step, scores(0) + (neg, zero, neg, zero, jnp.zeros((TQ, LANE), F32)))
        s, stats = carry[:2], carry[2:]
        for d in range(TQ // TK):
            nxt = scores(n_full + d + 1) if d + 1 < TQ // TK else None
            sa, sb = (jnp.where(row_minus_col >= d * TK, t, NEG) for t in s)
            stats = update(n_full + d, sa, sb, stats)
            s = nxt
        ma, la, mb, lb, acc = stats
        o_ref[...] = acc / jnp.where(lo, la, lb)
        lse_ref[...] = jnp.where(lo, ma + jnp.log2(la), mb + jnp.log2(lb)) * LN2

    return pl.pallas_call(
        body, name="attn_fwd", grid=(NPAIR, S // TQ),
        in_specs=[pl.BlockSpec((TQ, 2 * LANE), lambda j, i: (i, j)), pl.BlockSpec((S, 2 * LANE), lambda j, i: (0, j)),
                  pl.BlockSpec((S, LANE), lambda j, i: (0, j))],
        out_specs=[pl.BlockSpec((TQ, LANE), lambda j, i: (i, j)), pl.BlockSpec((None, TQ, LANE), lambda j, i: (j, i, 0))],
        out_shape=[jax.ShapeDtypeStruct((S, H * VDIM), F32), jax.ShapeDtypeStruct((NPAIR, S, LANE), F32)],
        compiler_params=pltpu.CompilerParams(dimension_semantics=("parallel", "parallel")),
    )(qc, kc, v)


def _attn_rows(lse, o, do):
    def body(lse_ref, o_ref, do_ref, r_ref):
        lt = lse_ref[...].T * (1.0 / LN2)
        tt = (o_ref[...] * do_ref[...]).T
        r_ref[...] = jnp.zeros_like(r_ref)
        r_ref[0:1, :] = lt[0:1, :]
        r_ref[1:2, :] = lt[VDIM:VDIM + 1, :]
        r_ref[2:3, :] = jnp.sum(tt[0:VDIM, :], axis=0, keepdims=True)
        r_ref[3:4, :] = jnp.sum(tt[VDIM:LANE, :], axis=0, keepdims=True)

    tile = pl.BlockSpec((S, LANE), lambda j: (0, j))
    return pl.pallas_call(
        body, name="attn_rows", grid=(NPAIR,), in_specs=[pl.BlockSpec((None, S, LANE), lambda j: (j, 0, 0)), tile, tile],
        out_specs=pl.BlockSpec((None, 8, S), lambda j: (j, 0, 0)), out_shape=jax.ShapeDtypeStruct((NPAIR, 8, S), F32),
    )(lse, o, do)


def _attn_bwd(qc, kc, kct, v, do, rows):
    nq = S // TQ

    def body(q_ref, k_ref, kt_ref, v_ref, do_ref, r_ref, dqt_ref, dk_ref, dv_ref):
        kb = pl.program_id(1)

        @pl.when(kb == 0)
        def _():
            dqt_ref[...] = jnp.zeros_like(dqt_ref)

        lo = lax.broadcasted_iota(jnp.int32, (TK, LANE), 1) < VDIM
        q_minus_k = lax.broadcasted_iota(jnp.int32, (TK, TQ), 1) - lax.broadcasted_iota(jnp.int32, (TK, TQ), 0)
        vv = v_ref[...]
        kk = k_ref[...]

        def step(qi, carry):
            off = pl.multiple_of(qi * TQ, TQ)
            qq = q_ref[pl.ds(off, TQ), :]
            dd = do_ref[pl.ds(off, TQ), :].astype(BF16)
            rr = r_ref[:, pl.ds(off, TQ)]
            keep = q_minus_k >= (kb - qi) * TQ
            out = []
            for x in range(2):
                sel = lo if x == 0 else jnp.logical_not(lo)
                kx, qx = kk[:, x * LANE:(x + 1) * LANE], qq[:, x * LANE:(x + 1) * LANE]
                st = jnp.where(keep, _dot(kx, qx, ((1,), (1,))) * ATT_SCALE_LOG2, NEG)
                pt = jnp.exp2(st - rr[x:x + 1, :])
                dpt = _dot(jnp.where(sel, vv, 0), dd, ((1,), (1,)))
                dst = (pt * (dpt - rr[2 + x:3 + x, :]) * ATT_SCALE).astype(BF16)
                out.append(carry[x] + _dot(dst, qx, ((1,), (0,))))
                out.append(_dot(pt, jnp.where(sel, dd, 0), ((1,), (0,))))
                dqt_ref[x * LANE:(x + 1) * LANE, pl.ds(off, TQ)] += _dot(kt_ref[x * LANE:(x + 1) * LANE, :], dst, ((1,), (0,)))
            return out[0], out[2], carry[2] + out[1] + out[3]

        z = jnp.zeros((TK, LANE), F32)
        dka, dkb, dv = lax.fori_loop(kb, nq, step, (z, z, z))
        dk_ref[:, 0:LANE] = dka
        dk_ref[:, LANE:2 * LANE] = dkb
        dv_ref[...] = dv.astype(BF16)

    return pl.pallas_call(
        body, name="attn_bwd", grid=(NPAIR, S // TK),
        in_specs=[pl.BlockSpec((S, 2 * LANE), lambda j, k: (0, j)), pl.BlockSpec((TK, 2 * LANE), lambda j, k: (k, j)),
                  pl.BlockSpec((2 * LANE, TK), lambda j, k: (j, k)), pl.BlockSpec((TK, LANE), lambda j, k: (k, j)),
                  pl.BlockSpec((S, LANE), lambda j, k: (0, j)), pl.BlockSpec((None, 8, S), lambda j, k: (j, 0, 0))],
        out_specs=[pl.BlockSpec((2 * LANE, S), lambda j, k: (j, 0)), pl.BlockSpec((TK, 2 * LANE), lambda j, k: (k, j)),
                   pl.BlockSpec((TK, LANE), lambda j, k: (k, j))],
        out_shape=[jax.ShapeDtypeStruct((H * LANE, S), F32), jax.ShapeDtypeStruct((S, H * LANE), F32),
                   jax.ShapeDtypeStruct((S, H * VDIM), BF16)],
        compiler_params=pltpu.CompilerParams(dimension_semantics=("parallel", "arbitrary")),
    )(qc, kc, kct, v, do, rows)


_IN_Z, _IN_XBC, _IN_DT, _IN_Q, _IN_KV, _IN_KR = 0, 1024, 2560, 2576, 2960, 3216


def _prep_in(w_in_t):
    dt = w_in_t.dtype
    w_small_t = jnp.concatenate(
        [w_in_t[_IN_Q:_IN_KV], w_in_t[_IN_KV:_IN_KR], w_in_t[_IN_KR:IN_WIDTH], jnp.zeros((LANE - ROPE, D), dt),
         w_in_t[_IN_DT:_IN_Q], jnp.zeros((LANE - H, D), dt)], axis=0)
    return w_in_t[_IN_Z:_IN_XBC], w_in_t[_IN_XBC:_IN_DT], w_small_t


def _prep_attn(w_qb, w_kvb):
    w_q = jnp.pad(w_qb.reshape(Q_RANK, H, NOPE + ROPE), ((0, 0), (0, 0), (0, LANE - NOPE - ROPE))).reshape(Q_RANK, H * LANE)
    kv3 = w_kvb.reshape(KV_RANK, H, NOPE + VDIM)
    w_k = jnp.pad(kv3[:, :, :NOPE], ((0, 0), (0, 0), (0, LANE - NOPE))).reshape(KV_RANK, H * LANE)
    w_v = kv3[:, :, NOPE:].reshape(KV_RANK, H * VDIM)
    return w_q, w_k, w_v


def _rope_tables(positions):
    inv_freq = 1.0 / (10000.0 ** (jnp.arange(0, ROPE, 2, dtype=F32) / ROPE))
    ang = positions.astype(F32).reshape(S, 1) * inv_freq
    cos, sin = jnp.cos(ang), jnp.sin(ang)
    cos_t = jnp.concatenate([jnp.ones((S, NOPE), F32), cos, cos, jnp.ones((S, LANE - NOPE - ROPE), F32)], axis=1)
    sin_t = jnp.concatenate([jnp.zeros((S, NOPE), F32), -sin, sin, jnp.zeros((S, LANE - NOPE - ROPE), F32)], axis=1)
    return cos_t, sin_t


def _local_step(x, p, positions, target, w_in, fetch, send, sp):
    w_z, w_xbc, w_small = _prep_in(w_in.reshape(IN_WIDTH, D))
    cos_t, sin_t = _rope_tables(positions)
    prow = jnp.zeros((8, LANE), F32).at[0, :H].set(sp["dt_bias"][0]).at[1, :H].set(sp["A_log"][0]).at[2, :H].set(sp["D"][0])
    pcol = prow.T

    xb, pb = x.astype(BF16), p.astype(BF16)
    z = _mm([(xb, w_z)], tb=True, name="proj_z")
    xbc = _mm([(xb, w_xbc)], tb=True, name="proj_xbc")
    small = _mm([(xb, w_small)], tb=True, name="proj_small")
    act = _conv_fwd(xbc, sp["conv_w"], sp["conv_b"])
    dt_t = small[:, SM_DT:SM_DT + LANE].T
    y, states = _ssd_fwd(act, small, dt_t, prow, pcol)
    y_ssd = _gate_norm_fwd(y, z, sp["ssd_norm"])
    gl = fetch("attn", y_ssd)
    w_q, w_k, w_v = _prep_attn(_from_cols(gl["w_qb"]), _from_cols(gl["w_kvb"]))
    qn, kvn, qcat, kcat, kcat_t, v = _qkv_fwd(small, w_q, w_k, w_v, sp["q_norm"], sp["kv_norm"], cos_t, sin_t)
    o, lse = _attn_fwd(qcat, kcat, v)
    y_mla = _rms_fwd(o, sp["out_norm"], name="out_norm_fwd")
    w_out = fetch("out", y_mla)["w_out"]
    w_out_s = w_out[:NCHIP // 2].reshape(SSD_INNER, D)
    w_out_m = w_out[NCHIP // 2:].reshape(SSD_INNER, D)
    mix = _mm([(y_ssd, w_out_s), (y_mla, w_out_m)], name="out_proj")
    h1, h1b = _ln_fwd(x, mix, sp["ln_mix_g"], sp["ln_mix_b"])
    gl = fetch("ffn", h1b)
    w_pg, w_pp = gl["w_pg"].reshape(D, D), _from_cols(gl["w_pp"])
    w_gate, w_up, w_down = gl["w_gate"], gl["w_up"], gl["w_down"]
    gate, up, actf = _ffn_hidden_fwd(h1b, w_gate, w_up)
    ffn = _mm([(actf, w_down)], chunk="sum", name="ffn_down")
    pg = _mm([(h1b, w_pg)], name="ple_gate")
    pp = _mm([(pb, w_pp)], name="ple_proj")
    dpre2, dpre2b, dpg, dpp, dg2, db2, loss_row = _final_fwd_bwd(h1, ffn, pg, pp, target, sp["ln_ffn_g"], sp["ln_ffn_b"])

    g = {"ln_ffn_g": dg2, "ln_ffn_b": db2}
    g["w_pp"] = _to_cols(_mm([(pb, dpp)], ta=True, out_dtype=BF16, name="d_w_ple_proj"))
    g["w_pg"] = _mm([(h1b, dpg)], ta=True, out_dtype=BF16, name="d_w_ple_gate").reshape(NCHIP, D // NCHIP, D)
    g["w_down"] = _mm([(actf, dpre2b)], ta=True, chunk="out", out_dtype=BF16, name="d_w_down")
    dgate, dup = _ffn_hidden_bwd(dpre2b, w_down, gate, up)
    g["w_gate"] = _mm([(dgate, h1b)], ta=True, chunk="out", out_dtype=BF16, name="d_w_gate")
    g["w_up"] = _mm([(dup, h1b)], ta=True, chunk="out", out_dtype=BF16, name="d_w_up")
    sent = send("ffn", {name: g.pop(name) for name in dict(ASYNC_GROUPS)["ffn"]})
    dh1 = _mm([(dpg, w_pg)], tb=True, add=dpre2, add_scale=ALPHA, name="d_h1_ple")
    dh1 = _mm([(dgate, w_gate), (dup, w_up)], chunk="sum", add=dh1, name="d_h1")
    dpre1, dpre1b, g["ln_mix_g"], g["ln_mix_b"] = _ln_bwd(x, mix, sp["ln_mix_g"] + sent, dh1)
    dy_ssd = _mm([(dpre1b, w_out_s)], tb=True, name="d_y_ssd")
    dy_mla = _mm([(dpre1b, w_out_m)], tb=True, name="d_y_mla")
    dw_out = jnp.concatenate([_mm([(y_ssd, dpre1b)], ta=True, out_dtype=BF16, name="d_w_out_s"),
                              _mm([(y_mla, dpre1b)], ta=True, out_dtype=BF16, name="d_w_out_m")], axis=0)
    sent = send("out", {"w_out": dw_out.reshape(NCHIP, 2 * SSD_INNER // NCHIP, D)})
    do, g["out_norm"] = _rms_bwd(o, sp["out_norm"] + sent, dy_mla, name="out_norm_bwd")
    dqt, dk, dv = _attn_bwd(qcat, kcat, kcat_t, v, do, _attn_rows(lse, o, do))
    dlatent, dqlin, dkb, g["q_norm"], g["kv_norm"] = _qkv_bwd(dqt, dk, dv, small, w_q, w_k, w_v, sp["q_norm"], sp["kv_norm"], cos_t, sin_t)
    dw_q = _mm([(qn, dqlin)], ta=True, out_dtype=BF16, name="d_w_q")
    dw_k = _mm([(kvn, dkb)], ta=True, out_dtype=BF16, name="d_w_k")
    dw_v = _mm([(kvn, dv)], ta=True, out_dtype=BF16, name="d_w_v")
    dw_qb = _to_cols(dw_q.reshape(Q_RANK, H, LANE)[:, :, :NOPE + ROPE].reshape(Q_RANK, H * (NOPE + ROPE)))
    dw_kvb = _to_cols(jnp.concatenate([dw_k.reshape(KV_RANK, H, LANE)[:, :, :NOPE], dw_v.reshape(KV_RANK, H, VDIM)],
                                       axis=2).reshape(KV_RANK, H * (NOPE + VDIM)))
    sent = send("attn", {"w_qb": dw_qb, "w_kvb": dw_kvb})
    dy, dz, g["ssd_norm"] = _gate_norm_bwd(y, z, sp["ssd_norm"] + sent, dy_ssd)
    dact, ddt, dprow = _ssd_bwd(act, small, dt_t, prow, pcol, states, dy)
    g["dt_bias"], g["A_log"], g["D"] = dprow[0:1, :H], dprow[1:2, :H], dprow[2:3, :H]
    dxbc, g["conv_w"], g["conv_b"] = _conv_bwd(xbc, sp["conv_w"], sp["conv_b"], dact)
    dsmall = jnp.concatenate([dlatent, ddt.astype(BF16)], axis=1)
    grad_x = _mm([(dz, w_z), (dxbc, w_xbc), (dsmall, w_small)], add=dpre1, add_scale=ALPHA, name="d_x")
    dw_small = _mm([(dsmall, xb)], ta=True, out_dtype=BF16, name="d_w_small")
    dw_in = jnp.concatenate(
        [_mm([(dz, xb)], ta=True, out_dtype=BF16, name="d_w_z"), _mm([(dxbc, xb)], ta=True, out_dtype=BF16, name="d_w_xbc"),
         dw_small[SM_DT:SM_DT + H], dw_small[SM_Q:SM_Q + Q_RANK], dw_small[SM_KV:SM_KV + KV_RANK], dw_small[SM_KR:SM_KR + ROPE]],
        axis=0).reshape(NCHIP, IN_WIDTH // NCHIP, D)
    return loss_row, grad_x, dw_in, g


MESH = pl.DeviceIdType.MESH
BIG = (("w_in", (D, IN_WIDTH), 1), ("w_qb", (Q_RANK, H * (NOPE + ROPE)), 1), ("w_kvb", (KV_RANK, H * (NOPE + VDIM)), 1),
       ("w_out", (2 * SSD_INNER, D), 0), ("w_gate", (D, D_FF), 1), ("w_up", (D, D_FF), 1), ("w_down", (D_FF, D), 0),
       ("w_pg", (D, D), 0), ("w_pp", (PLE, D), 1))
CONV_SHARD = SSD_XBC // NCHIP
BF16_ROWS = 16


def _from_cols(stack):
    return jnp.concatenate([stack[k] for k in range(NCHIP)], axis=1)


def _to_cols(full):
    r, c4 = full.shape
    return full.reshape(r, NCHIP, c4 // NCHIP).transpose(1, 0, 2)


def _coords():
    return lax.axis_index("x"), lax.axis_index("y"), lax.axis_index("c")


def _peers():
    x, y, c = _coords()
    return 2 * x + y, c, [(1 - x, y), (x, 1 - y), (1 - x, 1 - y)], (x, y, 1 - c)


def _half_axis(shape):
    return 0 if shape[-2] % (2 * BF16_ROWS) == 0 else 1


def _half_shape(shape):
    r, c = shape[-2:]
    return (r // 2, c) if _half_axis(shape) == 0 else (r, c // 2)


def _half(core, shape):
    r, c = shape[-2:]
    if _half_axis(shape) == 0:
        return pl.ds(pl.multiple_of(core * (r // 2), BF16_ROWS), r // 2), slice(None)
    return slice(None), pl.ds(pl.multiple_of(core * (c // 2), LANE), c // 2)


def _gather_weights(shards):
    n_arr = len(shards)
    per = 2 * (NCHIP - 1)

    def body(*refs):
        ins, outs = refs[:n_arr], refs[n_arr:2 * n_arr]
        send_sems, recv_sems, local_sems = refs[2 * n_arr:]
        k, c, chips, sibling = _peers()

        def copy(idx, src, dst, to):
            return pltpu.make_async_remote_copy(src_ref=src, dst_ref=dst, send_sem=send_sems.at[idx], recv_sem=recv_sems.at[idx],
                                                device_id=to, device_id_type=MESH)

        def part(a, chip, core):
            return outs[a].at[chip, *_half(core, shards[a].shape)]

        mine = [pltpu.make_async_copy(ins[a], outs[a].at[k], local_sems.at[a]) for a in range(n_arr)]
        for cp in mine:
            cp.start()
        sends = []
        for a in range(n_arr):
            for j, (cx, cy) in enumerate(chips):
                sends.append(copy(per * a + j, ins[a].at[*_half(c, shards[a].shape)], part(a, k, c), (cx, cy, c)))
                sends[-1].start()
        for j, (cx, cy) in enumerate(chips):
            for a in range(n_arr):
                landed = part(a, 2 * cx + cy, c)
                copy(per * a + j, landed, landed, (cx, cy, c)).wait_recv()
                sends.append(copy(per * a + NCHIP - 1 + j, landed, landed, sibling))
                sends[-1].start()
        for j, (cx, cy) in enumerate(chips):
            for a in range(n_arr):
                other = part(a, 2 * cx + cy, 1 - c)
                copy(per * a + NCHIP - 1 + j, other, other, sibling).wait_recv()
        for cp in sends:
            cp.wait_send()
        for cp in mine:
            cp.wait()

    any_spec = pl.BlockSpec(memory_space=pl.ANY)
    return pl.pallas_call(
        body, name="gather_weights", in_specs=[any_spec] * n_arr, out_specs=[any_spec] * n_arr,
        out_shape=[jax.ShapeDtypeStruct((NCHIP,) + s.shape, s.dtype) for s in shards],
        scratch_shapes=[pltpu.SemaphoreType.DMA((per * n_arr,)), pltpu.SemaphoreType.DMA((per * n_arr,)),
                        pltpu.SemaphoreType.DMA((n_arr,))],
    )(*shards)


ASYNC_GROUPS = (("attn", ("w_qb", "w_kvb")), ("out", ("w_out",)), ("ffn", ("w_gate", "w_up", "w_down", "w_pg", "w_pp")))
TRANSPOSED = ("w_in", "w_gate", "w_up")
HBM_SPEC = pl.BlockSpec(memory_space=pltpu.HBM)
SEM_SPEC = pl.BlockSpec(memory_space=pltpu.SEMAPHORE)
IN_FLIGHT = pltpu.SideEffectType.DATAFLOW_SIDE_EFFECTING


def _in_hbm(a):
    return pltpu.with_memory_space_constraint(a, pltpu.HBM)


def _hbm_like(arrs, lead=()):
    return [pltpu.HBM(lead + a.shape, a.dtype) for a in arrs]


def _split_start(name, srcs, lands, after, n_sem, start):
    n = len(srcs)

    def body(*refs):
        src_refs, land_refs = refs[:n], refs[n:2 * n]
        send_sems, recv_sems = refs[2 * n + 1], refs[2 * n + 2]
        token = refs[-1]

        def copy(send_idx, recv_idx, src, dst, to):
            return pltpu.make_async_remote_copy(src_ref=src, dst_ref=dst, send_sem=send_sems.at[send_idx],
                                                recv_sem=recv_sems.at[recv_idx], device_id=to, device_id_type=MESH)

        for cp in start(src_refs, land_refs, copy):
            cp.start()
        token[...] = jnp.zeros_like(token)

    sem = pltpu.SemaphoreType.DMA((n_sem,))
    outs = pl.pallas_call(
        body, name=name, in_specs=[HBM_SPEC] * (2 * n) + [pl.BlockSpec(memory_space=pl.ANY)],
        out_specs=[SEM_SPEC, SEM_SPEC] + [HBM_SPEC] * (2 * n) + [pl.BlockSpec(memory_space=pltpu.VMEM)],
        out_shape=[sem, sem] + _hbm_like(srcs) + _hbm_like(lands) + [jax.ShapeDtypeStruct((8, LANE), F32)],
        input_output_aliases={i: 2 + i for i in range(2 * n)},
        compiler_params=pltpu.CompilerParams(has_side_effects=IN_FLIGHT),
    )(*[_in_hbm(a) for a in srcs], *[_in_hbm(a) for a in lands], after)
    return (outs[0], outs[1], outs[2:2 + n], outs[2 + n:2 + 2 * n]), outs[-1]


def _split_wait(name, send_sems, recv_sems, srcs, lands, after, waits):
    n = len(srcs)

    def body(*refs):
        src_refs, land_refs = refs[:n], refs[n:2 * n]
        send_ref, recv_ref = refs[2 * n], refs[2 * n + 1]

        def copy(send_idx, recv_idx, src, dst, to):
            return pltpu.make_async_remote_copy(src_ref=src, dst_ref=dst, send_sem=send_ref.at[send_idx],
                                                recv_sem=recv_ref.at[recv_idx], device_id=to, device_id_type=MESH)

        for cp in waits(src_refs, land_refs, copy):
            cp.wait_send()
            cp.wait_recv()

    outs = pl.pallas_call(
        body, name=name, in_specs=[HBM_SPEC] * (2 * n) + [SEM_SPEC, SEM_SPEC, pl.BlockSpec(memory_space=pl.ANY)],
        out_specs=[HBM_SPEC] * (2 * n), out_shape=_hbm_like(srcs) + _hbm_like(lands),
        input_output_aliases={i: i for i in range(2 * n)},
        compiler_params=pltpu.CompilerParams(has_side_effects=IN_FLIGHT),
    )(*srcs, *lands, send_sems, recv_sems, after)
    return outs[:n], outs[n:]


GATHER_LATE_SEMS = 2 * (NCHIP - 1)


def _gather_async_start(tag, shards, after):
    def start(srcs, lands, copy):
        k, c, chips, _ = _peers()
        out = []
        for a, (src, dst) in enumerate(zip(srcs, lands)):
            for j, (cx, cy) in enumerate(chips):
                for core in range(2):
                    out.append(copy(GATHER_LATE_SEMS * a + 2 * j + core, GATHER_LATE_SEMS * a + 2 * j + c,
                                    src.at[*_half(c, src.shape)], dst.at[k, *_half(c, src.shape)], (cx, cy, core)))
        return out

    chip = 2 * lax.axis_index("x") + lax.axis_index("y")
    lands = [lax.dynamic_update_slice(lax.empty((NCHIP,) + s.shape, s.dtype), s[None], (chip, 0, 0)) for s in shards]
    return _split_start("gather_%s_start" % tag, shards, lands, after, GATHER_LATE_SEMS * len(shards), start)


def _gather_async_wait(tag, send_sems, recv_sems, shards, lands, after):
    def waits(srcs, lands_, copy):
        _, c, chips, _ = _peers()
        out = []
        for a, (src, dst) in enumerate(zip(srcs, lands_)):
            for j, (cx, cy) in enumerate(chips):
                for core in range(2):
                    idx = GATHER_LATE_SEMS * a + 2 * j + core
                    out.append(copy(idx, idx, src.at[*_half(c, src.shape)], dst.at[2 * cx + cy, *_half(core, src.shape)], (cx, cy, core)))
        return out

    return _split_wait("gather_%s_wait" % tag, send_sems, recv_sems, shards, lands, after, waits)[1]


def _other_devices():
    x, y, c = _coords()
    out = []
    for d in range(1, NDEV):
        tx, ty, tc = x ^ (d >> 2), y ^ ((d >> 1) & 1), c ^ (d & 1)
        out.append((d, (tx, ty, tc), 2 * tx + ty, 4 * tx + 2 * ty + tc))
    return out


def _reduce_async_start(tag, stacks, after):
    def start(srcs, lands, copy):
        x, y, c = _coords()
        me = 4 * x + 2 * y + c
        return [copy((NDEV - 1) * a + d - 1, (NDEV - 1) * a + d - 1, src.at[chip, *_half(to[2], src.shape)], dst.at[me], to)
                for a, (src, dst) in enumerate(zip(srcs, lands)) for d, to, chip, _ in _other_devices()]

    x, y, c = _coords()
    lands = []
    for s in stacks:
        hr, hc = _half_shape(s.shape)
        at = (c * hr, 0) if _half_axis(s.shape) == 0 else (0, c * hc)
        own = lax.dynamic_slice(s, (2 * x + y,) + at, (1, hr, hc))
        lands.append(lax.dynamic_update_slice(lax.empty((NDEV, hr, hc), s.dtype), own, (4 * x + 2 * y + c, 0, 0)))
    return _split_start("reduce_%s_start" % tag, stacks, lands, after, (NDEV - 1) * len(stacks), start)


def _reduce_async_wait(tag, send_sems, recv_sems, stacks, lands, after):
    def waits(srcs, lands_, copy):
        return [copy((NDEV - 1) * a + d - 1, (NDEV - 1) * a + d - 1, src.at[chip, *_half(to[2], src.shape)], dst.at[pos], to)
                for a, (src, dst) in enumerate(zip(srcs, lands_)) for d, to, chip, pos in _other_devices()]

    return _split_wait("reduce_%s_wait" % tag, send_sems, recv_sems, stacks, lands, after, waits)[1]


def _reduce_finish(tag, arrived, dims):
    n_arr = len(arrived)

    def body(*refs):
        lands, fin = refs[:n_arr], refs[n_arr:2 * n_arr]
        send_sems, recv_sems = refs[2 * n_arr:]
        _, c, _, sibling = _peers()
        sends = []
        for a in range(n_arr):
            mine = fin[a].at[*_half(c, dims[a])]

            def device_sum(vs, vf, a=a, mine=mine):
                pltpu.sync_copy(lands[a], vs)
                acc = vs[0].astype(F32)
                for i in range(1, NDEV):
                    acc = acc + vs[i].astype(F32)
                vf[...] = acc
                pltpu.sync_copy(vf, mine)

            pl.run_scoped(device_sum, pltpu.VMEM((NDEV,) + _half_shape(dims[a]), BF16), pltpu.VMEM(_half_shape(dims[a]), F32))
            sends.append(pltpu.make_async_remote_copy(src_ref=mine, dst_ref=mine, send_sem=send_sems.at[a], recv_sem=recv_sems.at[a],
                                                      device_id=sibling, device_id_type=MESH))
            sends[-1].start()
        for a in range(n_arr):
            other = fin[a].at[*_half(1 - c, dims[a])]
            pltpu.make_async_remote_copy(src_ref=other, dst_ref=other, send_sem=send_sems.at[a], recv_sem=recv_sems.at[a],
                                         device_id=sibling, device_id_type=MESH).wait_recv()
        for cp in sends:
            cp.wait_send()

    any_spec = pl.BlockSpec(memory_space=pl.ANY)
    return pl.pallas_call(
        body, name="reduce_%s_finish" % tag, in_specs=[any_spec] * n_arr, out_specs=[any_spec] * n_arr,
        out_shape=[jax.ShapeDtypeStruct(d, F32) for d in dims],
        scratch_shapes=[pltpu.SemaphoreType.DMA((n_arr,)), pltpu.SemaphoreType.DMA((n_arr,))],
    )(*arrived)


SMALL = (("conv_w", SSD_K * SSD_XBC), ("conv_b", SSD_XBC), ("dt_bias", H), ("A_log", H), ("D", H), ("ssd_norm", SSD_INNER),
         ("q_norm", Q_RANK), ("kv_norm", KV_RANK), ("out_norm", SSD_INNER), ("ln_mix_g", D), ("ln_mix_b", D),
         ("ln_ffn_g", D), ("ln_ffn_b", D))
SMALL_ROWS = 120
NDEV = 8


def _allreduce_small(sv):
    def body(sv_ref, out_ref, slots, send_sems, recv_sems):
        x, y, c = _coords()
        me = 4 * x + 2 * y + c
        slots[me] = sv_ref[...]
        copies = []
        for d in range(1, NDEV):
            to = (x ^ (d >> 2), y ^ ((d >> 1) & 1), c ^ (d & 1))
            copies.append(pltpu.make_async_remote_copy(src_ref=sv_ref, dst_ref=slots.at[me], send_sem=send_sems.at[d - 1],
                                                       recv_sem=recv_sems.at[d - 1], device_id=to, device_id_type=MESH))
            copies[-1].start()
        for cp in copies:
            cp.wait_recv()
        for cp in copies:
            cp.wait_send()
        acc = slots[0]
        for i in range(1, NDEV):
            acc = acc + slots[i]
        out_ref[...] = acc

    vm = pl.BlockSpec(memory_space=pltpu.VMEM)
    return pl.pallas_call(
        body, name="allreduce_small", in_specs=[vm], out_specs=vm, out_shape=jax.ShapeDtypeStruct((SMALL_ROWS, LANE), F32),
        scratch_shapes=[pltpu.VMEM((NDEV, SMALL_ROWS, LANE), F32), pltpu.SemaphoreType.DMA((NDEV - 1,)),
                        pltpu.SemaphoreType.DMA((NDEV - 1,))],
    )(sv)


def _adamw_math(w, g, m, v):
    m2 = ADAM_B1 * m + (1.0 - ADAM_B1) * g
    v2 = ADAM_B2 * v + (1.0 - ADAM_B2) * (g * g)
    m_hat = m2 / (1.0 - ADAM_B1 ** ADAM_STEP)
    v_hat = v2 / (1.0 - ADAM_B2 ** ADAM_STEP)
    return -ADAM_LR * (m_hat / (jnp.sqrt(v_hat) + ADAM_EPS) + ADAM_WD * w), m2, v2


def _adamw_big(w, g, m, v, *, name):
    r, c = w.shape

    def body(w_ref, g_ref, m_ref, v_ref, d_ref, m2_ref, v2_ref):
        d_ref[...], m2_ref[...], v2_ref[...] = _adamw_math(w_ref[...], g_ref[...], m_ref[...], v_ref[...])

    if r % 8 == 0:
        tr = next(t for t in (512, 384, 352, 256, 128, 64, 8) if r % t == 0)
        steps, spec = r // tr, pl.BlockSpec((tr, c), lambda i: (i, 0))
    else:
        steps, spec = c // (2 * LANE), pl.BlockSpec((r, 2 * LANE), lambda i: (0, i))
    return pl.pallas_call(body, name=name, grid=(steps,), in_specs=[spec] * 4, out_specs=[spec] * 3,
                          out_shape=[jax.ShapeDtypeStruct((r, c), F32)] * 3)(w, g, m, v)


def _adamw_small(ws, gs, ms, vs):
    n = len(ws)

    def body(*refs):
        for i in range(n):
            w_ref, g_ref, m_ref, v_ref = (refs[j * n + i] for j in range(4))
            d_ref, m2_ref, v2_ref = (refs[(4 + j) * n + i] for j in range(3))
            d_ref[...], m2_ref[...], v2_ref[...] = _adamw_math(w_ref[...], g_ref[...], m_ref[...], v_ref[...])

    vm = pl.BlockSpec(memory_space=pltpu.VMEM)
    shapes = [jax.ShapeDtypeStruct(w.shape, F32) for w in ws]
    outs = pl.pallas_call(body, name="adamw_small", in_specs=[vm] * (4 * n), out_specs=[vm] * (3 * n), out_shape=shapes * 3)(
        *ws, *gs, *ms, *vs)
    return outs[:n], outs[n:2 * n], outs[2 * n:]


_SMALL_ARG = {"conv_w": "ssd_conv_w", "conv_b": "ssd_conv_b", "dt_bias": "ssd_dt_bias", "A_log": "ssd_A_log", "D": "ssd_D",
              "ssd_norm": "ssd_norm_w", "q_norm": "mla_q_norm_w", "kv_norm": "mla_kv_norm_w", "out_norm": "mla_out_norm_w",
              "ln_mix_g": "ln_mix_g", "ln_mix_b": "ln_mix_b", "ln_ffn_g": "ln_ffn_g", "ln_ffn_b": "ln_ffn_b"}
_BIG_ARG = {"w_in": "w_in", "w_qb": "mla_w_q_b", "w_kvb": "mla_w_kv_b", "w_out": "w_out", "w_gate": "w_ffn_gate",
            "w_up": "w_ffn_up", "w_down": "w_ffn_down", "w_pg": "w_ple_gate", "w_pp": "w_ple_proj"}
_WEIGHT_ORDER = ("w_in", "ssd_conv_w", "ssd_conv_b", "ssd_dt_bias", "ssd_A_log", "ssd_D", "ssd_norm_w", "mla_q_norm_w", "mla_w_q_b",
                 "mla_kv_norm_w", "mla_w_kv_b", "mla_out_norm_w", "w_out", "ln_mix_g", "ln_mix_b", "w_ffn_gate", "w_ffn_up",
                 "w_ffn_down", "w_ple_gate", "w_ple_proj", "ln_ffn_g", "ln_ffn_b")


def _rows128(a):
    flat = a.reshape(-1)
    return jnp.pad(flat, (0, -flat.shape[0] % LANE)).reshape(-1, LANE)


def kernel(x, p, positions, w_in, ssd_conv_w, ssd_conv_b, ssd_dt_bias, ssd_A_log, ssd_D, ssd_norm_w, mla_q_norm_w, mla_w_q_b, mla_kv_norm_w, mla_w_kv_b, mla_out_norm_w, w_out, ln_mix_g, ln_mix_b, w_ffn_gate, w_ffn_up, w_ffn_down, w_ple_gate, w_ple_proj, ln_ffn_g, ln_ffn_b, loss_target, m_w_in, m_ssd_conv_w, m_ssd_conv_b, m_ssd_dt_bias, m_ssd_A_log, m_ssd_D, m_ssd_norm_w, m_mla_q_norm_w, m_mla_w_q_b, m_mla_kv_norm_w, m_mla_w_kv_b, m_mla_out_norm_w, m_w_out, m_ln_mix_g, m_ln_mix_b, m_w_ffn_gate, m_w_ffn_up, m_w_ffn_down, m_w_ple_gate, m_w_ple_proj, m_ln_ffn_g, m_ln_ffn_b, v_w_in, v_ssd_conv_w, v_ssd_conv_b, v_ssd_dt_bias, v_ssd_A_log, v_ssd_D, v_ssd_norm_w, v_mla_q_norm_w, v_mla_w_q_b, v_mla_kv_norm_w, v_mla_w_kv_b, v_mla_out_norm_w, v_w_out, v_ln_mix_g, v_ln_mix_b, v_w_ffn_gate, v_w_ffn_up, v_w_ffn_down, v_w_ple_gate, v_w_ple_proj, v_ln_ffn_g, v_ln_ffn_b):
    given = dict(locals())
    chip = 2 * lax.axis_index("x") + lax.axis_index("y")

    def local(name, prefix=""):
        a = given[prefix + _BIG_ARG[name]][0]
        return a.T if name in TRANSPOSED else a

    def global_layout(name, arr):
        return (arr.T if name in TRANSPOSED else arr)[None]

    conv_bits = lax.bitcast_convert_type(ssd_conv_w[0], BF16).reshape(SSD_K, 2 * CONV_SHARD)
    w_in_all, conv_all = _gather_weights([local("w_in").astype(BF16), jnp.pad(conv_bits, ((0, BF16_ROWS - SSD_K), (0, 0)))])
    sp = {k: given[a] for k, a in _SMALL_ARG.items() if k != "conv_w"}
    sp["conv_w"] = _from_cols(lax.bitcast_convert_type(conv_all[:, :SSD_K].reshape(NCHIP, SSD_K, CONV_SHARD, 2), F32))
    gathering, tie = {}, w_in_all
    for group, names in ASYNC_GROUPS:
        gathering[group], tie = _gather_async_start(group, [local(name).astype(BF16) for name in names], tie)

    def fetch(group, after):
        return dict(zip(dict(ASYNC_GROUPS)[group], _gather_async_wait(group, *gathering[group], after)))

    reducing = {}

    def send(group, grads):
        reducing[group], sent = _reduce_async_start(group, [grads[name] for name in dict(ASYNC_GROUPS)[group]], grads[dict(ASYNC_GROUPS)[group][0]])
        return sent[0, 0]

    loss_row, grad_x, dw_in, g = _local_step(x[0] + tie[0, 0], p[0, 0], positions[0], loss_target[0], w_in_all, fetch, send, sp)

    reducing["in"], tie = _reduce_async_start("in", [dw_in], grad_x)
    gbig = {}
    for group, names in reversed(ASYNC_GROUPS):
        arrived = _reduce_async_wait(group, *reducing[group], tie)
        gbig.update(zip(names, _reduce_finish(group, arrived, [local(name).shape for name in names])))
    small_in = jnp.concatenate([_rows128(g[name]) for name, _ in SMALL] + [loss_row], axis=0)
    small_sum = _allreduce_small(jnp.pad(small_in, ((0, SMALL_ROWS - small_in.shape[0]), (0, 0))))
    gsmall, row = {}, 0
    for name, size in SMALL:
        nrow = -(-size // LANE)
        gsmall[name] = small_sum[row:row + nrow].reshape(-1)[:size]
        row += nrow
    loss = small_sum[row, 0]

    grads = {_BIG_ARG[name]: global_layout(name, arr) for name, arr in gbig.items()}
    for name, _ in SMALL:
        if name == "conv_w":
            full_g = gsmall[name].reshape(SSD_K, SSD_XBC)
            grads["ssd_conv_w"] = lax.dynamic_slice(full_g, (0, chip * CONV_SHARD), (SSD_K, CONV_SHARD))[None]
        else:
            grads[_SMALL_ARG[name]] = gsmall[name].reshape(given[_SMALL_ARG[name]].shape)

    delta, new_m, new_v = {}, {}, {}

    def update_matrix(name, grad):
        a = _BIG_ARG[name]
        d, m2, v2 = _adamw_big(local(name), grad, local(name, "m_"), local(name, "v_"), name="adamw_" + a)
        delta[a], new_m[a], new_v[a] = (global_layout(name, t) for t in (d, m2, v2))
        return d

    for name, grad in gbig.items():
        last = update_matrix(name, grad)
    g_in = _reduce_finish("in", _reduce_async_wait("in", *reducing["in"], last), [local("w_in").shape])[0]
    grads["w_in"] = global_layout("w_in", g_in)
    update_matrix("w_in", g_in)
    small_names = [_SMALL_ARG[name] for name, _ in SMALL]
    two_d = lambda t: t.reshape(t.shape[-2], t.shape[-1])
    ds, ms, vs = _adamw_small([two_d(given[a]) for a in small_names], [two_d(grads[a]) for a in small_names],
                              [two_d(given["m_" + a]) for a in small_names], [two_d(given["v_" + a]) for a in small_names])
    for a, d, m2, v2 in zip(small_names, ds, ms, vs):
        delta[a], new_m[a], new_v[a] = (t.reshape(given[a].shape) for t in (d, m2, v2))

    return (loss, grad_x[None], *[grads[n] for n in _WEIGHT_ORDER], *[delta[n] for n in _WEIGHT_ORDER],
            *[new_m[n] for n in _WEIGHT_ORDER], *[new_v[n] for n in _WEIGHT_ORDER])
```

```python
import functools
import math

import jax
import jax.numpy as jnp
from jax import lax
from jax.experimental import pallas as pl
from jax.experimental.pallas import tpu as pltpu

F32 = jnp.float32
BF16 = jnp.bfloat16

S = 2048
D = 1024
PLE = 256
H = 16
SSD_P = 64
SSD_INNER = 1024
SSD_N = 128
SSD_G = 2
SSD_L = 128
SSD_NC = S // SSD_L
SSD_XBC = 1536
SSD_K = 4
Q_RANK = 384
KV_RANK = 256
NOPE = 64
ROPE = 32
VDIM = 64
D_FF = 2816
IN_WIDTH = 3248
ALPHA = 2.0 ** 0.25
EPS_RMS = 1e-6
EPS_LN = 1e-5
ATT_SCALE = 1.0 / math.sqrt(NOPE + ROPE)
LN2 = math.log(2.0)
ATT_SCALE_LOG2 = ATT_SCALE / LN2
LANE = 128
NCHIP = 4
SMALL_W = 896
SM_Q, SM_KV, SM_KR, SM_DT = 0, 384, 640, 768
NEG = -1e30

ADAM_LR = 0.001
ADAM_B1 = 0.9
ADAM_B2 = 0.999
ADAM_EPS = 1e-08
ADAM_WD = 0.01
ADAM_STEP = 10


def _sigmoid(v):
    return 1.0 / (1.0 + jnp.exp(-v))


MM_VMEM_BUDGET = 36 * 2 ** 20
MM_MAX_ACC = 2048 * 1024


def _mm_tiles(pairs, ta, tb, m, n, out_dtype, has_add):
    def divs(v):
        return [LANE * d for d in range(v // LANE, 0, -1) if (v // LANE) % d == 0] if v % LANE == 0 else [v]

    def cost(tm, tn):
        tot = tm * tn * (jnp.dtype(out_dtype).itemsize + (4 if has_add else 0))
        for a, b in pairs:
            k = a.shape[-2] if ta else a.shape[-1]
            tot += k * (tm * a.dtype.itemsize + tn * b.dtype.itemsize)
        return 2 * tot

    ok = [(tm * tn, tm, tn) for tm in divs(m) for tn in divs(n) if tm * tn <= MM_MAX_ACC and cost(tm, tn) <= MM_VMEM_BUDGET]
    _, tm, tn = max(ok)
    return tm, tn


def _mm(pairs, *, ta=False, tb=False, out_dtype=F32, add=None, add_scale=1.0, chunk=None, name):
    n_pairs = len(pairs)
    a0, b0 = pairs[0]
    m = a0.shape[-1] if ta else a0.shape[-2]
    n = b0.shape[-2] if tb else b0.shape[-1]
    tm, tn = _mm_tiles(pairs, ta, tb, m, n, out_dtype, add is not None)
    dims = (((0 if ta else 1,), (1 if tb else 0,)), ((), ()))
    nk = NCHIP if chunk else 1
    assert chunk != "sum" or out_dtype == F32

    def body(*refs):
        o_ref = refs[-1]
        acc = None
        for i in range(n_pairs):
            a = refs[2 * i][...].astype(BF16)
            b = refs[2 * i + 1][...].astype(BF16)
            part = lax.dot_general(a, b, dims, preferred_element_type=F32)
            acc = part if acc is None else acc + part
        if chunk == "sum":
            k = pl.program_id(2)

            @pl.when(k == 0)
            def _():
                o_ref[...] = acc + add_scale * refs[2 * n_pairs][...] if add is not None else acc

            @pl.when(k > 0)
            def _():
                o_ref[...] += acc
        else:
            if add is not None:
                acc = acc + add_scale * refs[2 * n_pairs][...]
            o_ref[...] = acc.astype(out_dtype)

    def spec(arr, shape, idx2):
        if arr.ndim == 3:
            return pl.BlockSpec((None,) + shape, lambda i, j, k: (k,) + idx2(i, j))
        return pl.BlockSpec(shape, lambda i, j, k: idx2(i, j))

    in_specs, args = [], []
    for a, b in pairs:
        kdim = a.shape[-2] if ta else a.shape[-1]
        in_specs.append(spec(a, (kdim, tm), lambda i, j: (0, i)) if ta else spec(a, (tm, kdim), lambda i, j: (i, 0)))
        in_specs.append(spec(b, (tn, kdim), lambda i, j: (j, 0)) if tb else spec(b, (kdim, tn), lambda i, j: (0, j)))
        args += [a, b]
    if add is not None:
        in_specs.append(pl.BlockSpec((tm, tn), lambda i, j, k: (i, j)))
        args.append(add)
    if chunk == "out":
        out_spec = pl.BlockSpec((None, tm, tn), lambda i, j, k: (k, i, j))
        out_shape = jax.ShapeDtypeStruct((nk, m, n), out_dtype)
    else:
        out_spec = pl.BlockSpec((tm, tn), lambda i, j, k: (i, j))
        out_shape = jax.ShapeDtypeStruct((m, n), out_dtype)
    return pl.pallas_call(
        body, name=name, grid=(m // tm, n // tn, nk), in_specs=in_specs, out_specs=out_spec, out_shape=out_shape,
        compiler_params=pltpu.CompilerParams(dimension_semantics=("parallel", "parallel", "arbitrary")),
    )(*args)


TR = 256


def _row_spec(c):
    return pl.BlockSpec((TR, c), lambda i: (i, 0))


def _vec_spec(c):
    return pl.BlockSpec((1, c), lambda i: (0, 0))


def _acc_rows(ref, val):
    @pl.when(pl.program_id(0) == 0)
    def _():
        ref[...] = jnp.zeros_like(ref)
    ref[...] += val


def _rms_fwd(u, w, *, name):
    c = u.shape[1]

    def body(u_ref, w_ref, o_ref):
        v = u_ref[...]
        r = lax.rsqrt(jnp.mean(v * v, axis=-1, keepdims=True) + EPS_RMS)
        o_ref[...] = (v * r * w_ref[...]).astype(BF16)

    return pl.pallas_call(body, name=name, grid=(S // TR,), in_specs=[_row_spec(c), _vec_spec(c)], out_specs=_row_spec(c),
                          out_shape=jax.ShapeDtypeStruct((S, c), BF16))(u, w)


def _rms_bwd(u, w, dy, *, name):
    c = u.shape[1]

    def body(u_ref, w_ref, dy_ref, du_ref, dw_ref):
        v = u_ref[...]
        g = dy_ref[...].astype(F32)
        r = lax.rsqrt(jnp.mean(v * v, axis=-1, keepdims=True) + EPS_RMS)
        gw = g * w_ref[...]
        du_ref[...] = r * gw - v * (r * r * r * jnp.mean(gw * v, axis=-1, keepdims=True))
        _acc_rows(dw_ref, jnp.sum(g * v * r, axis=0, keepdims=True))

    return pl.pallas_call(body, name=name, grid=(S // TR,), in_specs=[_row_spec(c), _vec_spec(c), _row_spec(c)],
                          out_specs=[_row_spec(c), _vec_spec(c)],
                          out_shape=[jax.ShapeDtypeStruct((S, c), F32), jax.ShapeDtypeStruct((1, c), F32)])(u, w, dy)


def _gate_norm_fwd(y, z, w):
    def body(y_ref, z_ref, w_ref, o_ref):
        zz = z_ref[...]
        v = y_ref[...] * (zz * _sigmoid(zz))
        r = lax.rsqrt(jnp.mean(v * v, axis=-1, keepdims=True) + EPS_RMS)
        o_ref[...] = (v * r * w_ref[...]).astype(BF16)

    c = SSD_INNER
    return pl.pallas_call(body, name="ssd_gate_norm_fwd", grid=(S // TR,), in_specs=[_row_spec(c), _row_spec(c), _vec_spec(c)],
                          out_specs=_row_spec(c), out_shape=jax.ShapeDtypeStruct((S, c), BF16))(y, z, w)


def _gate_norm_bwd(y, z, w, dout):
    def body(y_ref, z_ref, w_ref, g_ref, dy_ref, dz_ref, dw_ref):
        yy = y_ref[...]
        zz = z_ref[...]
        sg = _sigmoid(zz)
        sz = zz * sg
        v = yy * sz
        g = g_ref[...]
        r = lax.rsqrt(jnp.mean(v * v, axis=-1, keepdims=True) + EPS_RMS)
        gw = g * w_ref[...]
        dv = r * gw - v * (r * r * r * jnp.mean(gw * v, axis=-1, keepdims=True))
        dy_ref[...] = dv * sz
        dz_ref[...] = (dv * yy * (sg * (1.0 + zz * (1.0 - sg)))).astype(BF16)
        _acc_rows(dw_ref, jnp.sum(g * v * r, axis=0, keepdims=True))

    c = SSD_INNER
    return pl.pallas_call(body, name="ssd_gate_norm_bwd", grid=(S // TR,),
                          in_specs=[_row_spec(c), _row_spec(c), _vec_spec(c), _row_spec(c)],
                          out_specs=[_row_spec(c), _row_spec(c), _vec_spec(c)],
                          out_shape=[jax.ShapeDtypeStruct((S, c), F32), jax.ShapeDtypeStruct((S, c), BF16),
                                     jax.ShapeDtypeStruct((1, c), F32)])(y, z, w, dout)


def _ln_fwd(xr, mix, g, b):
    def body(x_ref, m_ref, g_ref, b_ref, o_ref, ob_ref):
        pre = ALPHA * x_ref[...] + m_ref[...]
        mu = jnp.mean(pre, axis=-1, keepdims=True)
        d = pre - mu
        rs = lax.rsqrt(jnp.mean(d * d, axis=-1, keepdims=True) + EPS_LN)
        h = d * rs * g_ref[...] + b_ref[...]
        o_ref[...] = h
        ob_ref[...] = h.astype(BF16)

    return pl.pallas_call(body, name="ln_mix_fwd", grid=(S // TR,), in_specs=[_row_spec(D), _row_spec(D), _vec_spec(D), _vec_spec(D)],
                          out_specs=[_row_spec(D)] * 2,
                          out_shape=[jax.ShapeDtypeStruct((S, D), F32), jax.ShapeDtypeStruct((S, D), BF16)])(xr, mix, g, b)


def _ln_bwd(xr, mix, g, dh):
    def body(x_ref, m_ref, g_ref, dh_ref, dpre_ref, dpreb_ref, dg_ref, db_ref):
        pre = ALPHA * x_ref[...] + m_ref[...]
        mu = jnp.mean(pre, axis=-1, keepdims=True)
        d = pre - mu
        rs = lax.rsqrt(jnp.mean(d * d, axis=-1, keepdims=True) + EPS_LN)
        xh = d * rs
        dy = dh_ref[...]
        gy = dy * g_ref[...]
        dpre = rs * (gy - jnp.mean(gy, axis=-1, keepdims=True) - xh * jnp.mean(gy * xh, axis=-1, keepdims=True))
        dpre_ref[...] = dpre
        dpreb_ref[...] = dpre.astype(BF16)
        _acc_rows(dg_ref, jnp.sum(dy * xh, axis=0, keepdims=True))
        _acc_rows(db_ref, jnp.sum(dy, axis=0, keepdims=True))

    return pl.pallas_call(body, name="ln_mix_bwd", grid=(S // TR,),
                          in_specs=[_row_spec(D), _row_spec(D), _vec_spec(D), _row_spec(D)],
                          out_specs=[_row_spec(D), _row_spec(D), _vec_spec(D), _vec_spec(D)],
                          out_shape=[jax.ShapeDtypeStruct((S, D), F32), jax.ShapeDtypeStruct((S, D), BF16),
                                     jax.ShapeDtypeStruct((1, D), F32), jax.ShapeDtypeStruct((1, D), F32)])(xr, mix, g, dh)


FF_CHUNK = D_FF // NCHIP


FF_ROWS = 1024


def _ff_act_spec():
    return pl.BlockSpec((None, FF_ROWS, FF_CHUNK), lambda i, k: (k, i, 0))


def _ff_w_spec():
    return pl.BlockSpec((None, FF_CHUNK, D), lambda i, k: (k, 0, 0))


def _ffn_hidden_fwd(h, w_gate_t, w_up_t):
    def body(h_ref, wg_ref, wu_ref, g_ref, u_ref, a_ref):
        hh = h_ref[...]
        g = _dot(hh, wg_ref[...], ((1,), (1,)))
        u = _dot(hh, wu_ref[...], ((1,), (1,)))
        g_ref[...] = g.astype(BF16)
        u_ref[...] = u.astype(BF16)
        a_ref[...] = (g * _sigmoid(g) * u).astype(BF16)

    return pl.pallas_call(
        body, name="ffn_hidden_fwd", grid=(S // FF_ROWS, NCHIP),
        in_specs=[pl.BlockSpec((FF_ROWS, D), lambda i, k: (i, 0)), _ff_w_spec(), _ff_w_spec()], out_specs=[_ff_act_spec()] * 3,
        out_shape=[jax.ShapeDtypeStruct((NCHIP, S, FF_CHUNK), BF16)] * 3,
        compiler_params=pltpu.CompilerParams(dimension_semantics=("parallel", "parallel")),
    )(h, w_gate_t, w_up_t)


def _ffn_hidden_bwd(dout, w_down, gate, up):
    def body(d_ref, wd_ref, g_ref, u_ref, dg_ref, du_ref):
        d = _dot(d_ref[...], wd_ref[...], ((1,), (1,)))
        g = g_ref[...].astype(F32)
        sg = _sigmoid(g)
        dg_ref[...] = (d * u_ref[...].astype(F32) * (sg * (1.0 + g * (1.0 - sg)))).astype(BF16)
        du_ref[...] = (d * g * sg).astype(BF16)

    return pl.pallas_call(
        body, name="ffn_hidden_bwd", grid=(S // FF_ROWS, NCHIP),
        in_specs=[pl.BlockSpec((FF_ROWS, D), lambda i, k: (i, 0)), _ff_w_spec(), _ff_act_spec(), _ff_act_spec()],
        out_specs=[_ff_act_spec()] * 2, out_shape=[jax.ShapeDtypeStruct((NCHIP, S, FF_CHUNK), BF16)] * 2,
        compiler_params=pltpu.CompilerParams(dimension_semantics=("parallel", "parallel")),
    )(dout, w_down, gate, up)


def _final_fwd_bwd(h1, ffn, pg, pp, target, g2, b2):
    def body(h_ref, f_ref, pg_ref, pp_ref, t_ref, g_ref, b_ref, dpre_ref, dpreb_ref, dpg_ref, dpp_ref, dg_ref, db_ref, loss_ref):
        sg = _sigmoid(pg_ref[...])
        ppv = pp_ref[...]
        pre = ALPHA * h_ref[...] + f_ref[...] + sg * ppv
        mu = jnp.mean(pre, axis=-1, keepdims=True)
        d = pre - mu
        rs = lax.rsqrt(jnp.mean(d * d, axis=-1, keepdims=True) + EPS_LN)
        xh = d * rs
        err = xh * g_ref[...] + b_ref[...] - t_ref[...]
        dy = err * (1.0 / D)
        gy = dy * g_ref[...]
        dpre = rs * (gy - jnp.mean(gy, axis=-1, keepdims=True) - xh * jnp.mean(gy * xh, axis=-1, keepdims=True))
        dpre_ref[...] = dpre
        dpreb_ref[...] = dpre.astype(BF16)
        dpg_ref[...] = (dpre * ppv * sg * (1.0 - sg)).astype(BF16)
        dpp_ref[...] = (dpre * sg).astype(BF16)
        _acc_rows(dg_ref, jnp.sum(dy * xh, axis=0, keepdims=True))
        _acc_rows(db_ref, jnp.sum(dy, axis=0, keepdims=True))
        _acc_rows(loss_ref, 0.5 * jnp.sum(jnp.mean(err * err, axis=-1, keepdims=True), axis=0, keepdims=True) * jnp.ones((1, LANE), F32))

    return pl.pallas_call(
        body, name="final_ln_loss", grid=(S // TR,),
        in_specs=[_row_spec(D)] * 5 + [_vec_spec(D)] * 2,
        out_specs=[_row_spec(D)] * 4 + [_vec_spec(D), _vec_spec(D), _vec_spec(LANE)],
        out_shape=[jax.ShapeDtypeStruct((S, D), F32)] + [jax.ShapeDtypeStruct((S, D), BF16)] * 3 + [
                   jax.ShapeDtypeStruct((1, D), F32), jax.ShapeDtypeStruct((1, D), F32), jax.ShapeDtypeStruct((1, LANE), F32)],
    )(h1, ffn, pg, pp, target, g2, b2)


def _rot(u, cos_t, sin_t, lane):
    partner = jnp.where(lane < NOPE + ROPE // 2, pltpu.roll(u, LANE - ROPE // 2, 1), pltpu.roll(u, ROPE // 2, 1))
    return u * cos_t + partner * sin_t


def _rms(v, w):
    r = lax.rsqrt(jnp.mean(v * v, axis=-1, keepdims=True) + EPS_RMS)
    return v * r * w, r


def _rms_grad(v, r, w, g):
    gw = g * w
    return r * gw - v * (r * r * r * jnp.mean(gw * v, axis=-1, keepdims=True)), jnp.sum(g * v * r, axis=0, keepdims=True)


def _whole(arr):
    return pl.BlockSpec(arr.shape, lambda i: (0,) * arr.ndim)


def _qkv_fwd(small, w_q, w_k, w_v, q_norm, kv_norm, cos_t, sin_t):
    def body(sm_ref, wq_ref, wk_ref, wv_ref, qw_ref, kw_ref, c_ref, s_ref, qn_ref, kvn_ref, q_ref, k_ref, kt_ref, v_ref):
        lane = lax.broadcasted_iota(jnp.int32, (TR, LANE), 1)
        c, s = c_ref[...], s_ref[...]
        qn = _rms(sm_ref[:, SM_Q:SM_Q + Q_RANK], qw_ref[...])[0].astype(BF16)
        kvn = _rms(sm_ref[:, SM_KV:SM_KV + KV_RANK], kw_ref[...])[0].astype(BF16)
        qn_ref[...] = qn
        kvn_ref[...] = kvn
        kr = _rot(pltpu.roll(sm_ref[:, SM_KR:SM_KR + LANE], NOPE, 1), c, s, lane)
        for h in range(H):
            tile = slice(h * LANE, (h + 1) * LANE)
            q_ref[:, tile] = _rot(_dot(qn, wq_ref[:, tile], ((1,), (0,))), c, s, lane).astype(BF16)
            kt = _dot(kvn, wk_ref[:, tile], ((1,), (0,))) + kr
            k_ref[:, tile] = kt.astype(BF16)
            kt_ref[tile, :] = kt.T.astype(BF16)
        v_ref[...] = _dot(kvn, wv_ref[...], ((1,), (0,))).astype(BF16)

    w = H * LANE
    return pl.pallas_call(
        body, name="qkv_fwd", grid=(S // TR,),
        in_specs=[_row_spec(SMALL_W), _whole(w_q), _whole(w_k), _whole(w_v), _vec_spec(Q_RANK), _vec_spec(KV_RANK), _row_spec(LANE), _row_spec(LANE)],
        out_specs=[_row_spec(Q_RANK), _row_spec(KV_RANK), _row_spec(w), _row_spec(w), pl.BlockSpec((w, TR), lambda i: (0, i)),
                   _row_spec(H * VDIM)],
        out_shape=[jax.ShapeDtypeStruct((S, Q_RANK), BF16), jax.ShapeDtypeStruct((S, KV_RANK), BF16), jax.ShapeDtypeStruct((S, w), BF16),
                   jax.ShapeDtypeStruct((S, w), BF16), jax.ShapeDtypeStruct((w, S), BF16), jax.ShapeDtypeStruct((S, H * VDIM), BF16)],
    )(small, w_q, w_k, w_v, q_norm, kv_norm, cos_t, sin_t)


def _qkv_bwd(dqt, dk, dv, small, w_q, w_k, w_v, q_norm, kv_norm, cos_t, sin_t):
    def body(dq_ref, dk_ref, dv_ref, sm_ref, wq_ref, wk_ref, wv_ref, qw_ref, kw_ref, c_ref, s_ref,
             ds_ref, dql_ref, dkb_ref, dqw_ref, dkw_ref):
        lane = lax.broadcasted_iota(jnp.int32, (TR, LANE), 1)
        c, s = c_ref[...], -s_ref[...]
        dqn = jnp.zeros((TR, Q_RANK), F32)
        dkvn = _dot(dv_ref[...], wv_ref[...], ((1,), (1,)))
        dkr = jnp.zeros((TR, LANE), F32)
        for h in range(H):
            tile = slice(h * LANE, (h + 1) * LANE)
            dql = _rot(dq_ref[tile, :].T, c, s, lane).astype(BF16)
            dql_ref[:, tile] = dql
            dqn = dqn + _dot(dql, wq_ref[:, tile], ((1,), (1,)))
            dkt = dk_ref[:, tile]
            dkb_ref[:, tile] = dkt.astype(BF16)
            dkvn = dkvn + _dot(dkt, wk_ref[:, tile], ((1,), (1,)))
            dkr = dkr + dkt
        dkr = jnp.where((lane >= NOPE) & (lane < NOPE + ROPE), dkr, 0.0)
        q_c, kv_c = sm_ref[:, SM_Q:SM_Q + Q_RANK], sm_ref[:, SM_KV:SM_KV + KV_RANK]
        dq_c, dqw = _rms_grad(q_c, _rms(q_c, qw_ref[...])[1], qw_ref[...], dqn)
        dkv_c, dkw = _rms_grad(kv_c, _rms(kv_c, kw_ref[...])[1], kw_ref[...], dkvn)
        ds_ref[:, SM_Q:SM_Q + Q_RANK] = dq_c.astype(BF16)
        ds_ref[:, SM_KV:SM_KV + KV_RANK] = dkv_c.astype(BF16)
        ds_ref[:, SM_KR:SM_KR + LANE] = pltpu.roll(_rot(dkr, c, s, lane), LANE - NOPE, 1).astype(BF16)
        _acc_rows(dqw_ref, dqw)
        _acc_rows(dkw_ref, dkw)

    w = H * LANE
    return pl.pallas_call(
        body, name="qkv_bwd", grid=(S // TR,),
        in_specs=[pl.BlockSpec((w, TR), lambda i: (0, i)), _row_spec(w), _row_spec(H * VDIM), _row_spec(SMALL_W), _whole(w_q), _whole(w_k),
                  _whole(w_v), _vec_spec(Q_RANK), _vec_spec(KV_RANK), _row_spec(LANE), _row_spec(LANE)],
        out_specs=[_row_spec(SM_DT), _row_spec(w), _row_spec(w), _vec_spec(Q_RANK), _vec_spec(KV_RANK)],
        out_shape=[jax.ShapeDtypeStruct((S, SM_DT), BF16), jax.ShapeDtypeStruct((S, w), BF16), jax.ShapeDtypeStruct((S, w), BF16),
                   jax.ShapeDtypeStruct((1, Q_RANK), F32), jax.ShapeDtypeStruct((1, KV_RANK), F32)],
    )(dqt, dk, dv, small, w_q, w_k, w_v, q_norm, kv_norm, cos_t, sin_t)


CB = 256


def _shift_down(u, k, row):
    if k == 0:
        return u
    return jnp.where(row >= k, pltpu.roll(u, k, 0), 0.0)


def _shift_up(u, k, row):
    if k == 0:
        return u
    return jnp.where(row < S - k, pltpu.roll(u, S - k, 0), 0.0)


def _conv_fwd(u, w, b):
    def body(u_ref, w_ref, b_ref, o_ref):
        row = lax.broadcasted_iota(jnp.int32, (S, CB), 0)
        uu = u_ref[...]
        acc = b_ref[...] + w_ref[SSD_K - 1:SSD_K, :] * uu
        for k in range(SSD_K - 1):
            acc = acc + w_ref[k:k + 1, :] * _shift_down(uu, SSD_K - 1 - k, row)
        o_ref[...] = acc * _sigmoid(acc)

    c = u.shape[1]
    return pl.pallas_call(
        body, name="conv_fwd", grid=(c // CB,),
        in_specs=[pl.BlockSpec((S, CB), lambda j: (0, j)), pl.BlockSpec((SSD_K, CB), lambda j: (0, j)), pl.BlockSpec((1, CB), lambda j: (0, j))],
        out_specs=pl.BlockSpec((S, CB), lambda j: (0, j)), out_shape=jax.ShapeDtypeStruct((S, c), F32),
    )(u, w, b)


def _conv_bwd(u, w, b, dact):
    def body(u_ref, w_ref, b_ref, d_ref, du_ref, dw_ref, db_ref):
        row = lax.broadcasted_iota(jnp.int32, (S, CB), 0)
        uu = u_ref[...]
        sh = [_shift_down(uu, SSD_K - 1 - k, row) for k in range(SSD_K)]
        acc = b_ref[...]
        for k in range(SSD_K):
            acc = acc + w_ref[k:k + 1, :] * sh[k]
        sg = _sigmoid(acc)
        dacc = d_ref[...] * (sg * (1.0 + acc * (1.0 - sg)))
        du = w_ref[SSD_K - 1:SSD_K, :] * dacc
        for k in range(SSD_K - 1):
            du = du + w_ref[k:k + 1, :] * _shift_up(dacc, SSD_K - 1 - k, row)
        du_ref[...] = du.astype(BF16)
        for k in range(SSD_K):
            dw_ref[k:k + 1, :] = jnp.sum(dacc * sh[k], axis=0, keepdims=True)
        db_ref[...] = jnp.sum(dacc, axis=0, keepdims=True)

    c = u.shape[1]
    col = lambda r: pl.BlockSpec((r, CB), lambda j: (0, j))
    return pl.pallas_call(
        body, name="conv_bwd", grid=(c // CB,), in_specs=[col(S), col(SSD_K), col(1), col(S)], out_specs=[col(S), col(SSD_K), col(1)],
        out_shape=[jax.ShapeDtypeStruct((S, c), BF16), jax.ShapeDtypeStruct((SSD_K, c), F32), jax.ShapeDtypeStruct((1, c), F32)],
    )(u, w, b, dact)


NPAIR = H // 2
PAIRS_PER_GROUP = NPAIR // SSD_G


def _softplus(v):
    return jnp.maximum(v, 0.0) + jnp.log(1.0 + jnp.exp(-jnp.abs(v)))


def _dot(a, b, dims):
    return lax.dot_general(a.astype(BF16), b.astype(BF16), (dims, ((), ())), preferred_element_type=F32)


def _dot2(a, sel):
    hi = a.astype(BF16)
    lo = (a - hi.astype(F32)).astype(BF16)
    dims = (((1,), (0,)), ((), ()))
    return lax.dot_general(hi, sel, dims, preferred_element_type=F32) + lax.dot_general(lo, sel, dims, preferred_element_type=F32)


def _dot3(a, b, dims, split_lhs):
    v = a if split_lhs else b
    v1 = v.astype(BF16)
    r1 = v - v1.astype(F32)
    v2 = r1.astype(BF16)
    v3 = (r1 - v2.astype(F32)).astype(BF16)
    acc = None
    for part in (v1, v2, v3):
        lhs, rhs = (part, b) if split_lhs else (a, part)
        t = lax.dot_general(lhs, rhs, (dims, ((), ())), preferred_element_type=F32)
        acc = t if acc is None else acc + t
    return acc


def _ssd_chunk_common(dt_ref, dtT_ref, prow_ref, pcol_ref):
    prow = prow_ref[...]
    pcol = pcol_ref[...]
    ri = lax.broadcasted_iota(jnp.int32, (SSD_L, SSD_L), 0)
    ci = lax.broadcasted_iota(jnp.int32, (SSD_L, SSD_L), 1)
    causal = ri >= ci
    pre_c = dt_ref[...] + prow[0:1, :]
    dtc = _softplus(pre_c)
    a_row = -jnp.exp(prow[1:2, :])
    cs_col = _dot3(causal.astype(BF16), dtc * a_row, ((1,), (0,)), False)
    dtr = _softplus(dtT_ref[...] + pcol[:, 0:1])
    a_col = -jnp.exp(pcol[:, 1:2])
    cs_row = _dot3(dtr * a_col, (ri <= ci).astype(BF16), ((1,), (0,)), True)
    return prow, causal, pre_c, dtc, a_row, cs_col, cs_row


def _ssd_fwd(act, small, dtT, prow, pcol):
    def body(x_ref, b_ref, c_ref, dt_ref, dtT_ref, prow_ref, pcol_ref, y_ref, st_ref, state):
        @pl.when(pl.program_id(0) == 0)
        def _():
            state[...] = jnp.zeros_like(state)

        prow, causal, _, dtc, _, cs_col, cs_row = _ssd_chunk_common(dt_ref, dtT_ref, prow_ref, pcol_ref)
        lo = lax.broadcasted_iota(jnp.int32, (SSD_L, LANE), 1) < SSD_P
        lo1 = lo[0:1, :]
        for g in range(SSD_G):
            bm = b_ref[:, g * SSD_N:(g + 1) * SSD_N]
            cm = c_ref[:, g * SSD_N:(g + 1) * SSD_N]
            cb = _dot(cm, bm, ((1,), (1,)))
            for qq in range(PAIRS_PER_GROUP):
                q = g * PAIRS_PER_GROUP + qq
                ha, hb = 2 * q, 2 * q + 1
                csa, csb = cs_col[:, ha:ha + 1], cs_col[:, hb:hb + 1]
                xp = x_ref[:, q * LANE:(q + 1) * LANE]
                xx = xp * jnp.where(lo, dtc[:, ha:ha + 1], dtc[:, hb:hb + 1])
                ga = cb * jnp.exp(jnp.where(causal, csa - cs_row[ha:ha + 1, :], NEG))
                gb = cb * jnp.exp(jnp.where(causal, csb - cs_row[hb:hb + 1, :], NEG))
                y = _dot(ga, jnp.where(lo, xx, 0.0), ((1,), (0,))) + _dot(gb, jnp.where(lo, 0.0, xx), ((1,), (0,)))
                s_in = state[q]
                y = y + _dot(cm, s_in, ((1,), (0,))) * jnp.where(lo, jnp.exp(csa), jnp.exp(csb))
                y = y + jnp.where(lo1, prow[2:3, ha:ha + 1], prow[2:3, hb:hb + 1]) * xp
                y_ref[:, q * LANE:(q + 1) * LANE] = y
                la, lb = csa[SSD_L - 1:SSD_L, :], csb[SSD_L - 1:SSD_L, :]
                decay = jnp.where(lo, jnp.exp(la - csa), jnp.exp(lb - csb))
                st_ref[q] = s_in
                state[q] = s_in * jnp.where(lo1, jnp.exp(la), jnp.exp(lb)) + _dot(bm, xx * decay, ((0,), (0,)))

    L = SSD_L
    return pl.pallas_call(
        body, name="ssd_fwd", grid=(SSD_NC,),
        in_specs=[pl.BlockSpec((L, SSD_INNER), lambda c: (c, 0)),
                  pl.BlockSpec((L, SSD_G * SSD_N), lambda c: (c, SSD_INNER // (SSD_G * SSD_N))),
                  pl.BlockSpec((L, SSD_G * SSD_N), lambda c: (c, SSD_INNER // (SSD_G * SSD_N) + 1)),
                  pl.BlockSpec((L, LANE), lambda c: (c, SM_DT // LANE)),
                  pl.BlockSpec((LANE, L), lambda c: (0, c)),
                  pl.BlockSpec((8, LANE), lambda c: (0, 0)), pl.BlockSpec((LANE, 8), lambda c: (0, 0))],
        out_specs=[pl.BlockSpec((L, SSD_INNER), lambda c: (c, 0)),
                   pl.BlockSpec((None, NPAIR, SSD_N, LANE), lambda c: (c, 0, 0, 0))],
        out_shape=[jax.ShapeDtypeStruct((S, SSD_INNER), F32), jax.ShapeDtypeStruct((SSD_NC, NPAIR, SSD_N, LANE), F32)],
        scratch_shapes=[pltpu.VMEM((NPAIR, SSD_N, LANE), F32)],
        compiler_params=pltpu.CompilerParams(dimension_semantics=("arbitrary",)),
    )(act, act, act, small, dtT, prow, pcol)


def _ssd_bwd(act, small, dtT, prow, pcol, states, dy):
    def body(x_ref, b_ref, c_ref, dt_ref, dtT_ref, prow_ref, pcol_ref, st_ref, dy_ref,
             dx_ref, ddt_ref, dp_ref, dstate):
        @pl.when(pl.program_id(0) == 0)
        def _():
            dstate[...] = jnp.zeros_like(dstate)
            dp_ref[...] = jnp.zeros_like(dp_ref)

        prow, causal, pre_c, dtc, a_row, cs_col, cs_row = _ssd_chunk_common(dt_ref, dtT_ref, prow_ref, pcol_ref)
        lane = lax.broadcasted_iota(jnp.int32, (SSD_L, LANE), 1)
        sub = lax.broadcasted_iota(jnp.int32, (LANE, SSD_L), 0)
        rowi = lax.broadcasted_iota(jnp.int32, (SSD_L, 1), 0)
        pick_p = lax.broadcasted_iota(jnp.int32, (LANE, LANE), 0)
        pick_l = lax.broadcasted_iota(jnp.int32, (LANE, LANE), 1)
        lo = lane < SSD_P
        lo1 = lo[0:1, :]
        dcs_c = jnp.zeros((SSD_L, LANE), F32)
        dcs_r = jnp.zeros((LANE, SSD_L), F32)
        ddt_x = jnp.zeros((SSD_L, LANE), F32)
        dd_row = jnp.zeros((1, LANE), F32)
        for g in range(SSD_G):
            bm = b_ref[:, g * SSD_N:(g + 1) * SSD_N]
            cm = c_ref[:, g * SSD_N:(g + 1) * SSD_N]
            cb = _dot(cm, bm, ((1,), (1,)))
            dcb = jnp.zeros((SSD_L, SSD_L), F32)
            dbm = jnp.zeros((SSD_L, SSD_N), F32)
            dcm = jnp.zeros((SSD_L, SSD_N), F32)
            for qq in range(PAIRS_PER_GROUP):
                q = g * PAIRS_PER_GROUP + qq
                ha, hb = 2 * q, 2 * q + 1
                csa, csb = cs_col[:, ha:ha + 1], cs_col[:, hb:hb + 1]
                xp = x_ref[:, q * LANE:(q + 1) * LANE]
                dtp = jnp.where(lo, dtc[:, ha:ha + 1], dtc[:, hb:hb + 1])
                xx = xp * dtp
                lma = jnp.exp(jnp.where(causal, csa - cs_row[ha:ha + 1, :], NEG))
                lmb = jnp.exp(jnp.where(causal, csb - cs_row[hb:hb + 1, :], NEG))
                ga, gb = cb * lma, cb * lmb
                dyp = dy_ref[:, q * LANE:(q + 1) * LANE]
                dya, dyb = jnp.where(lo, dyp, 0.0), jnp.where(lo, 0.0, dyp)
                s_in = st_ref[q]
                ds_out = dstate[q]
                la, lb = csa[SSD_L - 1:SSD_L, :], csb[SSD_L - 1:SSD_L, :]
                ecs = jnp.where(lo, jnp.exp(csa), jnp.exp(csb))
                decay = jnp.where(lo, jnp.exp(la - csa), jnp.exp(lb - csb))
                cd = jnp.where(lo1, jnp.exp(la), jnp.exp(lb))
                bds = _dot(bm, ds_out, ((1,), (0,)))
                dxx = _dot(ga, dya, ((0,), (0,))) + _dot(gb, dyb, ((0,), (0,))) + bds * decay
                dga = _dot(dya, xx, ((1,), (1,)))
                dgb = _dot(dyb, xx, ((1,), (1,)))
                dsega, dsegb = dga * ga, dgb * gb
                dcb = dcb + dga * lma + dgb * lmb
                yoff = _dot(cm, s_in, ((1,), (0,))) * ecs
                dye = dyp * ecs
                dcm = dcm + _dot(dye, s_in, ((1,), (1,)))
                xd = xx * decay
                dbm = dbm + _dot(xd, ds_out, ((1,), (1,)))
                wv = xd * bds
                ends = jnp.sum(wv, axis=0, keepdims=True) + cd * jnp.sum(ds_out * s_in, axis=0, keepdims=True)
                t1 = dyp * yoff - wv + jnp.where(rowi == SSD_L - 1, ends, 0.0)
                to_pair = (((pick_p < SSD_P) & (pick_l == ha)) | ((pick_p >= SSD_P) & (pick_l == hb))).astype(BF16)
                to_a_b = jnp.concatenate([(pick_l == ha).astype(BF16), (pick_l == hb).astype(BF16)], axis=0)
                dcs_c = dcs_c + _dot2(t1, to_pair) + _dot2(jnp.concatenate([dsega, dsegb], axis=1), to_a_b)
                dcs_r = (dcs_r + jnp.where(sub == ha, jnp.sum(dsega, axis=0, keepdims=True), 0.0)
                         + jnp.where(sub == hb, jnp.sum(dsegb, axis=0, keepdims=True), 0.0))
                dstate[q] = _dot(cm, dye, ((0,), (0,))) + cd * ds_out
                dpair = jnp.where(lo1, prow[2:3, ha:ha + 1], prow[2:3, hb:hb + 1])
                dx_ref[:, q * LANE:(q + 1) * LANE] = dxx * dtp + dpair * dyp
                ddt_x = ddt_x + _dot2(dxx * xp, to_pair)
                dd_row = dd_row + jnp.sum(_dot2(dyp * xp, to_pair), axis=0, keepdims=True)
            dx_ref[:, SSD_INNER + g * SSD_N:SSD_INNER + (g + 1) * SSD_N] = dbm + _dot(dcb, cm, ((0,), (0,)))
            dx_ref[:, SSD_INNER + (SSD_G + g) * SSD_N:SSD_INNER + (SSD_G + g + 1) * SSD_N] = dcm + _dot(dcb, bm, ((1,), (0,)))
        ri = lax.broadcasted_iota(jnp.int32, (SSD_L, SSD_L), 0)
        ci = lax.broadcasted_iota(jnp.int32, (SSD_L, SSD_L), 1)
        da = _dot3((ri <= ci).astype(BF16), dcs_c, ((1,), (0,)), False)
        da = da - _dot3(dcs_r, causal.astype(BF16), ((1,), (0,)), True).T
        ddt = ddt_x + da * a_row
        ddt_raw = ddt * _sigmoid(pre_c)
        ddt_ref[...] = ddt_raw
        da_head = jnp.sum(da * dtc, axis=0, keepdims=True) * a_row
        dp_ref[0:1, :] += jnp.sum(ddt_raw, axis=0, keepdims=True)
        dp_ref[1:2, :] += da_head
        dp_ref[2:3, :] += dd_row

    L = SSD_L
    rev = SSD_NC - 1
    bc_cols = SSD_INNER // (SSD_G * SSD_N)
    return pl.pallas_call(
        body, name="ssd_bwd", grid=(SSD_NC,),
        in_specs=[pl.BlockSpec((L, SSD_INNER), lambda c: (rev - c, 0)),
                  pl.BlockSpec((L, SSD_G * SSD_N), lambda c: (rev - c, bc_cols)),
                  pl.BlockSpec((L, SSD_G * SSD_N), lambda c: (rev - c, bc_cols + 1)),
                  pl.BlockSpec((L, LANE), lambda c: (rev - c, SM_DT // LANE)),
                  pl.BlockSpec((LANE, L), lambda c: (0, rev - c)),
                  pl.BlockSpec((8, LANE), lambda c: (0, 0)), pl.BlockSpec((LANE, 8), lambda c: (0, 0)),
                  pl.BlockSpec((None, NPAIR, SSD_N, LANE), lambda c: (rev - c, 0, 0, 0)),
                  pl.BlockSpec((L, SSD_INNER), lambda c: (rev - c, 0))],
        out_specs=[pl.BlockSpec((L, SSD_XBC), lambda c: (rev - c, 0)),
                   pl.BlockSpec((L, LANE), lambda c: (rev - c, 0)),
                   pl.BlockSpec((8, LANE), lambda c: (0, 0))],
        out_shape=[jax.ShapeDtypeStruct((S, SSD_XBC), F32), jax.ShapeDtypeStruct((S, LANE), F32),
                   jax.ShapeDtypeStruct((8, LANE), F32)],
        scratch_shapes=[pltpu.VMEM((NPAIR, SSD_N, LANE), F32)],
        compiler_params=pltpu.CompilerParams(dimension_semantics=("arbitrary",)),
    )(act, act, act, small, dtT, prow, pcol, states, dy)


TQ = 256
TK = 256
FWD_TQ = 256
FWD_TK = 256


def _attn_fwd(qc, kc, v):
    TQ, TK = FWD_TQ, FWD_TK

    def body(q_ref, k_ref, v_ref, o_ref, lse_ref):
        i = pl.program_id(1)
        lo = lax.broadcasted_iota(jnp.int32, (TQ, LANE), 1) < VDIM
        lo_k = lax.broadcasted_iota(jnp.int32, (TK, LANE), 1) < VDIM
        row_minus_col = lax.broadcasted_iota(jnp.int32, (TQ, TK), 0) - lax.broadcasted_iota(jnp.int32, (TQ, TK), 1)
        qa, qb = q_ref[:, 0:LANE], q_ref[:, LANE:2 * LANE]

        def scores(kb):
            kk = k_ref[pl.ds(pl.multiple_of(kb * TK, TK), TK), :]
            return (_dot(qa, kk[:, 0:LANE], ((1,), (1,))) * ATT_SCALE_LOG2, _dot(qb, kk[:, LANE:2 * LANE], ((1,), (1,))) * ATT_SCALE_LOG2)

        def update(kb, sa, sb, stats):
            ma, la, mb, lb, acc = stats
            vv = v_ref[pl.ds(pl.multiple_of(kb * TK, TK), TK), :]
            na = jnp.maximum(ma, jnp.max(sa, axis=1, keepdims=True))
            nb = jnp.maximum(mb, jnp.max(sb, axis=1, keepdims=True))
            pa, pb = jnp.exp2(sa - na), jnp.exp2(sb - nb)
            fa, fb = jnp.exp2(ma - na), jnp.exp2(mb - nb)
            la = fa * la + jnp.sum(pa, axis=1, keepdims=True)
            lb = fb * lb + jnp.sum(pb, axis=1, keepdims=True)
            acc = (acc * jnp.where(lo, fa, fb) + _dot(pa, jnp.where(lo_k, vv, 0), ((1,), (0,)))
                   + _dot(pb, jnp.where(lo_k, 0, vv), ((1,), (0,))))
            return na, la, nb, lb, acc

        def step(kb, carry):
            sa, sb = carry[:2]
            nxt = scores(kb + 1)
            return nxt + update(kb, sa, sb, carry[2:])

        neg = jnp.full((TQ, 1), NEG, F32)
        zero = jnp.zeros((TQ, 1), F32)
        n_full = i * (TQ // TK)
        carry = lax.fori_loop(0, n_full, step, scores(0) + (neg, zero, neg, zero, jnp.zeros((TQ, LANE), F32)))
        s, stats = carry[:2], carry[2:]
        for d in range(TQ // TK):
            nxt = scores(n_full + d + 1) if d + 1 < TQ // TK else None
            sa, sb = (jnp.where(row_minus_col >= d * TK, t, NEG) for t in s)
            stats = update(n_full + d, sa, sb, stats)
            s = nxt
        ma, la, mb, lb, acc = stats
        o_ref[...] = acc / jnp.where(lo, la, lb)
        lse_ref[...] = jnp.where(lo, ma + jnp.log2(la), mb + jnp.log2(lb)) * LN2

    return pl.pallas_call(
        body, name="attn_fwd", grid=(NPAIR, S // TQ),
        in_specs=[pl.BlockSpec((TQ, 2 * LANE), lambda j, i: (i, j)), pl.BlockSpec((S, 2 * LANE), lambda j, i: (0, j)),
                  pl.BlockSpec((S, LANE), lambda j, i: (0, j))],
        out_specs=[pl.BlockSpec((TQ, LANE), lambda j, i: (i, j)), pl.BlockSpec((None, TQ, LANE), lambda j, i: (j, i, 0))],
        out_shape=[jax.ShapeDtypeStruct((S, H * VDIM), F32), jax.ShapeDtypeStruct((NPAIR, S, LANE), F32)],
        compiler_params=pltpu.CompilerParams(dimension_semantics=("parallel", "parallel")),
    )(qc, kc, v)


def _attn_rows(lse, o, do):
    def body(lse_ref, o_ref, do_ref, r_ref):
        lt = lse_ref[...].T * (1.0 / LN2)
        tt = (o_ref[...] * do_ref[...]).T
        r_ref[...] = jnp.zeros_like(r_ref)
        r_ref[0:1, :] = lt[0:1, :]
        r_ref[1:2, :] = lt[VDIM:VDIM + 1, :]
        r_ref[2:3, :] = jnp.sum(tt[0:VDIM, :], axis=0, keepdims=True)
        r_ref[3:4, :] = jnp.sum(tt[VDIM:LANE, :], axis=0, keepdims=True)

    tile = pl.BlockSpec((S, LANE), lambda j: (0, j))
    return pl.pallas_call(
        body, name="attn_rows", grid=(NPAIR,), in_specs=[pl.BlockSpec((None, S, LANE), lambda j: (j, 0, 0)), tile, tile],
        out_specs=pl.BlockSpec((None, 8, S), lambda j: (j, 0, 0)), out_shape=jax.ShapeDtypeStruct((NPAIR, 8, S), F32),
    )(lse, o, do)


def _attn_bwd(qc, kc, kct, v, do, rows):
    nq = S // TQ

    def body(q_ref, k_ref, kt_ref, v_ref, do_ref, r_ref, dqt_ref, dk_ref, dv_ref):
        kb = pl.program_id(1)

        @pl.when(kb == 0)
        def _():
            dqt_ref[...] = jnp.zeros_like(dqt_ref)

        lo = lax.broadcasted_iota(jnp.int32, (TK, LANE), 1) < VDIM
        q_minus_k = lax.broadcasted_iota(jnp.int32, (TK, TQ), 1) - lax.broadcasted_iota(jnp.int32, (TK, TQ), 0)
        vv = v_ref[...]
        kk = k_ref[...]

        def step(qi, carry):
            off = pl.multiple_of(qi * TQ, TQ)
            qq = q_ref[pl.ds(off, TQ), :]
            dd = do_ref[pl.ds(off, TQ), :].astype(BF16)
            rr = r_ref[:, pl.ds(off, TQ)]
            keep = q_minus_k >= (kb - qi) * TQ
            out = []
            for x in range(2):
                sel = lo if x == 0 else jnp.logical_not(lo)
                kx, qx = kk[:, x * LANE:(x + 1) * LANE], qq[:, x * LANE:(x + 1) * LANE]
                st = jnp.where(keep, _dot(kx, qx, ((1,), (1,))) * ATT_SCALE_LOG2, NEG)
                pt = jnp.exp2(st - rr[x:x + 1, :])
                dpt = _dot(jnp.where(sel, vv, 0), dd, ((1,), (1,)))
                dst = (pt * (dpt - rr[2 + x:3 + x, :]) * ATT_SCALE).astype(BF16)
                out.append(carry[x] + _dot(dst, qx, ((1,), (0,))))
                out.append(_dot(pt, jnp.where(sel, dd, 0), ((1,), (0,))))
                dqt_ref[x * LANE:(x + 1) * LANE, pl.ds(off, TQ)] += _dot(kt_ref[x * LANE:(x + 1) * LANE, :], dst, ((1,), (0,)))
            return out[0], out[2], carry[2] + out[1] + out[3]

        z = jnp.zeros((TK, LANE), F32)
        dka, dkb, dv = lax.fori_loop(kb, nq, step, (z, z, z))
        dk_ref[:, 0:LANE] = dka
        dk_ref[:, LANE:2 * LANE] = dkb
        dv_ref[...] = dv.astype(BF16)

    return pl.pallas_call(
        body, name="attn_bwd", grid=(NPAIR, S // TK),
        in_specs=[pl.BlockSpec((S, 2 * LANE), lambda j, k: (0, j)), pl.BlockSpec((TK, 2 * LANE), lambda j, k: (k, j)),
                  pl.BlockSpec((2 * LANE, TK), lambda j, k: (j, k)), pl.BlockSpec((TK, LANE), lambda j, k: (k, j)),
                  pl.BlockSpec((S, LANE), lambda j, k: (0, j)), pl.BlockSpec((None, 8, S), lambda j, k: (j, 0, 0))],
        out_specs=[pl.BlockSpec((2 * LANE, S), lambda j, k: (j, 0)), pl.BlockSpec((TK, 2 * LANE), lambda j, k: (k, j)),
                   pl.BlockSpec((TK, LANE), lambda j, k: (k, j))],
        out_shape=[jax.ShapeDtypeStruct((H * LANE, S), F32), jax.ShapeDtypeStruct((S, H * LANE), F32),
                   jax.ShapeDtypeStruct((S, H * VDIM), BF16)],
        compiler_params=pltpu.CompilerParams(dimension_semantics=("parallel", "arbitrary")),
    )(qc, kc, kct, v, do, rows)


_IN_Z, _IN_XBC, _IN_DT, _IN_Q, _IN_KV, _IN_KR = 0, 1024, 2560, 2576, 2960, 3216


def _prep_in(w_in_t):
    dt = w_in_t.dtype
    w_small_t = jnp.concatenate(
        [w_in_t[_IN_Q:_IN_KV], w_in_t[_IN_KV:_IN_KR], w_in_t[_IN_KR:IN_WIDTH], jnp.zeros((LANE - ROPE, D), dt),
         w_in_t[_IN_DT:_IN_Q], jnp.zeros((LANE - H, D), dt)], axis=0)
    return w_in_t[_IN_Z:_IN_XBC], w_in_t[_IN_XBC:_IN_DT], w_small_t


def _prep_attn(w_qb, w_kvb):
    w_q = jnp.pad(w_qb.reshape(Q_RANK, H, NOPE + ROPE), ((0, 0), (0, 0), (0, LANE - NOPE - ROPE))).reshape(Q_RANK, H * LANE)
    kv3 = w_kvb.reshape(KV_RANK, H, NOPE + VDIM)
    w_k = jnp.pad(kv3[:, :, :NOPE], ((0, 0), (0, 0), (0, LANE - NOPE))).reshape(KV_RANK, H * LANE)
    w_v = kv3[:, :, NOPE:].reshape(KV_RANK, H * VDIM)
    return w_q, w_k, w_v


def _rope_tables(positions):
    inv_freq = 1.0 / (10000.0 ** (jnp.arange(0, ROPE, 2, dtype=F32) / ROPE))
    ang = positions.astype(F32).reshape(S, 1) * inv_freq
    cos, sin = jnp.cos(ang), jnp.sin(ang)
    cos_t = jnp.concatenate([jnp.ones((S, NOPE), F32), cos, cos, jnp.ones((S, LANE - NOPE - ROPE), F32)], axis=1)
    sin_t = jnp.concatenate([jnp.zeros((S, NOPE), F32), -sin, sin, jnp.zeros((S, LANE - NOPE - ROPE), F32)], axis=1)
    return cos_t, sin_t


def _local_step(x, p, positions, target, w_in, fetch, send, sp):
    w_z, w_xbc, w_small = _prep_in(w_in.reshape(IN_WIDTH, D))
    cos_t, sin_t = _rope_tables(positions)
    prow = jnp.zeros((8, LANE), F32).at[0, :H].set(sp["dt_bias"][0]).at[1, :H].set(sp["A_log"][0]).at[2, :H].set(sp["D"][0])
    pcol = prow.T

    xb, pb = x.astype(BF16), p.astype(BF16)
    z = _mm([(xb, w_z)], tb=True, name="proj_z")
    xbc = _mm([(xb, w_xbc)], tb=True, name="proj_xbc")
    small = _mm([(xb, w_small)], tb=True, name="proj_small")
    act = _conv_fwd(xbc, sp["conv_w"], sp["conv_b"])
    dt_t = small[:, SM_DT:SM_DT + LANE].T
    y, states = _ssd_fwd(act, small, dt_t, prow, pcol)
    y_ssd = _gate_norm_fwd(y, z, sp["ssd_norm"])
    gl = fetch("attn", y_ssd)
    w_q, w_k, w_v = _prep_attn(_from_cols(gl["w_qb"]), _from_cols(gl["w_kvb"]))
    qn, kvn, qcat, kcat, kcat_t, v = _qkv_fwd(small, w_q, w_k, w_v, sp["q_norm"], sp["kv_norm"], cos_t, sin_t)
    o, lse = _attn_fwd(qcat, kcat, v)
    y_mla = _rms_fwd(o, sp["out_norm"], name="out_norm_fwd")
    w_out = fetch("out", y_mla)["w_out"]
    w_out_s = w_out[:NCHIP // 2].reshape(SSD_INNER, D)
    w_out_m = w_out[NCHIP // 2:].reshape(SSD_INNER, D)
    mix = _mm([(y_ssd, w_out_s), (y_mla, w_out_m)], name="out_proj")
    h1, h1b = _ln_fwd(x, mix, sp["ln_mix_g"], sp["ln_mix_b"])
    gl = fetch("ffn", h1b)
    w_pg, w_pp = gl["w_pg"].reshape(D, D), _from_cols(gl["w_pp"])
    w_gate, w_up, w_down = gl["w_gate"], gl["w_up"], gl["w_down"]
    gate, up, actf = _ffn_hidden_fwd(h1b, w_gate, w_up)
    ffn = _mm([(actf, w_down)], chunk="sum", name="ffn_down")
    pg = _mm([(h1b, w_pg)], name="ple_gate")
    pp = _mm([(pb, w_pp)], name="ple_proj")
    dpre2, dpre2b, dpg, dpp, dg2, db2, loss_row = _final_fwd_bwd(h1, ffn, pg, pp, target, sp["ln_ffn_g"], sp["ln_ffn_b"])

    g = {"ln_ffn_g": dg2, "ln_ffn_b": db2}
    g["w_pp"] = _to_cols(_mm([(pb, dpp)], ta=True, out_dtype=BF16, name="d_w_ple_proj"))
    g["w_pg"] = _mm([(h1b, dpg)], ta=True, out_dtype=BF16, name="d_w_ple_gate").reshape(NCHIP, D // NCHIP, D)
    g["w_down"] = _mm([(actf, dpre2b)], ta=True, chunk="out", out_dtype=BF16, name="d_w_down")
    dgate, dup = _ffn_hidden_bwd(dpre2b, w_down, gate, up)
    g["w_gate"] = _mm([(dgate, h1b)], ta=True, chunk="out", out_dtype=BF16, name="d_w_gate")
    g["w_up"] = _mm([(dup, h1b)], ta=True, chunk="out", out_dtype=BF16, name="d_w_up")
    sent = send("ffn", {name: g.pop(name) for name in dict(ASYNC_GROUPS)["ffn"]})
    dh1 = _mm([(dpg, w_pg)], tb=True, add=dpre2, add_scale=ALPHA, name="d_h1_ple")
    dh1 = _mm([(dgate, w_gate), (dup, w_up)], chunk="sum", add=dh1, name="d_h1")
    dpre1, dpre1b, g["ln_mix_g"], g["ln_mix_b"] = _ln_bwd(x, mix, sp["ln_mix_g"] + sent, dh1)
    dy_ssd = _mm([(dpre1b, w_out_s)], tb=True, name="d_y_ssd")
    dy_mla = _mm([(dpre1b, w_out_m)], tb=True, name="d_y_mla")
    dw_out = jnp.concatenate([_mm([(y_ssd, dpre1b)], ta=True, out_dtype=BF16, name="d_w_out_s"),
                              _mm([(y_mla, dpre1b)], ta=True, out_dtype=BF16, name="d_w_out_m")], axis=0)
    sent = send("out", {"w_out": dw_out.reshape(NCHIP, 2 * SSD_INNER // NCHIP, D)})
    do, g["out_norm"] = _rms_bwd(o, sp["out_norm"] + sent, dy_mla, name="out_norm_bwd")
    dqt, dk, dv = _attn_bwd(qcat, kcat, kcat_t, v, do, _attn_rows(lse, o, do))
    dlatent, dqlin, dkb, g["q_norm"], g["kv_norm"] = _qkv_bwd(dqt, dk, dv, small, w_q, w_k, w_v, sp["q_norm"], sp["kv_norm"], cos_t, sin_t)
    dw_q = _mm([(qn, dqlin)], ta=True, out_dtype=BF16, name="d_w_q")
    dw_k = _mm([(kvn, dkb)], ta=True, out_dtype=BF16, name="d_w_k")
    dw_v = _mm([(kvn, dv)], ta=True, out_dtype=BF16, name="d_w_v")
    dw_qb = _to_cols(dw_q.reshape(Q_RANK, H, LANE)[:, :, :NOPE + ROPE].reshape(Q_RANK, H * (NOPE + ROPE)))
    dw_kvb = _to_cols(jnp.concatenate([dw_k.reshape(KV_RANK, H, LANE)[:, :, :NOPE], dw_v.reshape(KV_RANK, H, VDIM)],
                                       axis=2).reshape(KV_RANK, H * (NOPE + VDIM)))
    sent = send("attn", {"w_qb": dw_qb, "w_kvb": dw_kvb})
    dy, dz, g["ssd_norm"] = _gate_norm_bwd(y, z, sp["ssd_norm"] + sent, dy_ssd)
    dact, ddt, dprow = _ssd_bwd(act, small, dt_t, prow, pcol, states, dy)
    g["dt_bias"], g["A_log"], g["D"] = dprow[0:1, :H], dprow[1:2, :H], dprow[2:3, :H]
    dxbc, g["conv_w"], g["conv_b"] = _conv_bwd(xbc, sp["conv_w"], sp["conv_b"], dact)
    dsmall = jnp.concatenate([dlatent, ddt.astype(BF16)], axis=1)
    grad_x = _mm([(dz, w_z), (dxbc, w_xbc), (dsmall, w_small)], add=dpre1, add_scale=ALPHA, name="d_x")
    dw_small = _mm([(dsmall, xb)], ta=True, out_dtype=BF16, name="d_w_small")
    dw_in = jnp.concatenate(
        [_mm([(dz, xb)], ta=True, out_dtype=BF16, name="d_w_z"), _mm([(dxbc, xb)], ta=True, out_dtype=BF16, name="d_w_xbc"),
         dw_small[SM_DT:SM_DT + H], dw_small[SM_Q:SM_Q + Q_RANK], dw_small[SM_KV:SM_KV + KV_RANK], dw_small[SM_KR:SM_KR + ROPE]],
        axis=0).reshape(NCHIP, IN_WIDTH // NCHIP, D)
    return loss_row, grad_x, dw_in, g


MESH = pl.DeviceIdType.MESH
BIG = (("w_in", (D, IN_WIDTH), 1), ("w_qb", (Q_RANK, H * (NOPE + ROPE)), 1), ("w_kvb", (KV_RANK, H * (NOPE + VDIM)), 1),
       ("w_out", (2 * SSD_INNER, D), 0), ("w_gate", (D, D_FF), 1), ("w_up", (D, D_FF), 1), ("w_down", (D_FF, D), 0),
       ("w_pg", (D, D), 0), ("w_pp", (PLE, D), 1))
CONV_SHARD = SSD_XBC // NCHIP
BF16_ROWS = 16


def _from_cols(stack):
    return jnp.concatenate([stack[k] for k in range(NCHIP)], axis=1)


def _to_cols(full):
    r, c4 = full.shape
    return full.reshape(r, NCHIP, c4 // NCHIP).transpose(1, 0, 2)


def _coords():
    return lax.axis_index("x"), lax.axis_index("y"), lax.axis_index("c")


def _peers():
    x, y, c = _coords()
    return 2 * x + y, c, [(1 - x, y), (x, 1 - y), (1 - x, 1 - y)], (x, y, 1 - c)


def _half_axis(shape):
    return 0 if shape[-2] % (2 * BF16_ROWS) == 0 else 1


def _half_shape(shape):
    r, c = shape[-2:]
    return (r // 2, c) if _half_axis(shape) == 0 else (r, c // 2)


def _half(core, shape):
    r, c = shape[-2:]
    if _half_axis(shape) == 0:
        return pl.ds(pl.multiple_of(core * (r // 2), BF16_ROWS), r // 2), slice(None)
    return slice(None), pl.ds(pl.multiple_of(core * (c // 2), LANE), c // 2)


def _gather_weights(shards):
    n_arr = len(shards)
    per = 2 * (NCHIP - 1)

    def body(*refs):
        ins, outs = refs[:n_arr], refs[n_arr:2 * n_arr]
        send_sems, recv_sems, local_sems = refs[2 * n_arr:]
        k, c, chips, sibling = _peers()

        def copy(idx, src, dst, to):
            return pltpu.make_async_remote_copy(src_ref=src, dst_ref=dst, send_sem=send_sems.at[idx], recv_sem=recv_sems.at[idx],
                                                device_id=to, device_id_type=MESH)

        def part(a, chip, core):
            return outs[a].at[chip, *_half(core, shards[a].shape)]

        mine = [pltpu.make_async_copy(ins[a], outs[a].at[k], local_sems.at[a]) for a in range(n_arr)]
        for cp in mine:
            cp.start()
        sends = []
        for a in range(n_arr):
            for j, (cx, cy) in enumerate(chips):
                sends.append(copy(per * a + j, ins[a].at[*_half(c, shards[a].shape)], part(a, k, c), (cx, cy, c)))
                sends[-1].start()
        for j, (cx, cy) in enumerate(chips):
            for a in range(n_arr):
                landed = part(a, 2 * cx + cy, c)
                copy(per * a + j, landed, landed, (cx, cy, c)).wait_recv()
                sends.append(copy(per * a + NCHIP - 1 + j, landed, landed, sibling))
                sends[-1].start()
        for j, (cx, cy) in enumerate(chips):
            for a in range(n_arr):
                other = part(a, 2 * cx + cy, 1 - c)
                copy(per * a + NCHIP - 1 + j, other, other, sibling).wait_recv()
        for cp in sends:
            cp.wait_send()
        for cp in mine:
            cp.wait()

    any_spec = pl.BlockSpec(memory_space=pl.ANY)
    return pl.pallas_call(
        body, name="gather_weights", in_specs=[any_spec] * n_arr, out_specs=[any_spec] * n_arr,
        out_shape=[jax.ShapeDtypeStruct((NCHIP,) + s.shape, s.dtype) for s in shards],
        scratch_shapes=[pltpu.SemaphoreType.DMA((per * n_arr,)), pltpu.SemaphoreType.DMA((per * n_arr,)),
                        pltpu.SemaphoreType.DMA((n_arr,))],
    )(*shards)


ASYNC_GROUPS = (("attn", ("w_qb", "w_kvb")), ("out", ("w_out",)), ("ffn", ("w_gate", "w_up", "w_down", "w_pg", "w_pp")))
TRANSPOSED = ("w_in", "w_gate", "w_up")
HBM_SPEC = pl.BlockSpec(memory_space=pltpu.HBM)
SEM_SPEC = pl.BlockSpec(memory_space=pltpu.SEMAPHORE)
IN_FLIGHT = pltpu.SideEffectType.DATAFLOW_SIDE_EFFECTING


def _in_hbm(a):
    return pltpu.with_memory_space_constraint(a, pltpu.HBM)


def _hbm_like(arrs, lead=()):
    return [pltpu.HBM(lead + a.shape, a.dtype) for a in arrs]


def _split_start(name, srcs, lands, after, n_sem, start):
    n = len(srcs)

    def body(*refs):
        src_refs, land_refs = refs[:n], refs[n:2 * n]
        send_sems, recv_sems = refs[2 * n + 1], refs[2 * n + 2]
        token = refs[-1]

        def copy(send_idx, recv_idx, src, dst, to):
            return pltpu.make_async_remote_copy(src_ref=src, dst_ref=dst, send_sem=send_sems.at[send_idx],
                                                recv_sem=recv_sems.at[recv_idx], device_id=to, device_id_type=MESH)

        for cp in start(src_refs, land_refs, copy):
            cp.start()
        token[...] = jnp.zeros_like(token)

    sem = pltpu.SemaphoreType.DMA((n_sem,))
    outs = pl.pallas_call(
        body, name=name, in_specs=[HBM_SPEC] * (2 * n) + [pl.BlockSpec(memory_space=pl.ANY)],
        out_specs=[SEM_SPEC, SEM_SPEC] + [HBM_SPEC] * (2 * n) + [pl.BlockSpec(memory_space=pltpu.VMEM)],
        out_shape=[sem, sem] + _hbm_like(srcs) + _hbm_like(lands) + [jax.ShapeDtypeStruct((8, LANE), F32)],
        input_output_aliases={i: 2 + i for i in range(2 * n)},
        compiler_params=pltpu.CompilerParams(has_side_effects=IN_FLIGHT),
    )(*[_in_hbm(a) for a in srcs], *[_in_hbm(a) for a in lands], after)
    return (outs[0], outs[1], outs[2:2 + n], outs[2 + n:2 + 2 * n]), outs[-1]


def _split_wait(name, send_sems, recv_sems, srcs, lands, after, waits):
    n = len(srcs)

    def body(*refs):
        src_refs, land_refs = refs[:n], refs[n:2 * n]
        send_ref, recv_ref = refs[2 * n], refs[2 * n + 1]

        def copy(send_idx, recv_idx, src, dst, to):
            return pltpu.make_async_remote_copy(src_ref=src, dst_ref=dst, send_sem=send_ref.at[send_idx],
                                                recv_sem=recv_ref.at[recv_idx], device_id=to, device_id_type=MESH)

        for cp in waits(src_refs, land_refs, copy):
            cp.wait_send()
            cp.wait_recv()

    outs = pl.pallas_call(
        body, name=name, in_specs=[HBM_SPEC] * (2 * n) + [SEM_SPEC, SEM_SPEC, pl.BlockSpec(memory_space=pl.ANY)],
        out_specs=[HBM_SPEC] * (2 * n), out_shape=_hbm_like(srcs) + _hbm_like(lands),
        input_output_aliases={i: i for i in range(2 * n)},
        compiler_params=pltpu.CompilerParams(has_side_effects=IN_FLIGHT),
    )(*srcs, *lands, send_sems, recv_sems, after)
    return outs[:n], outs[n:]


GATHER_LATE_SEMS = 2 * (NCHIP - 1)


def _gather_async_start(tag, shards, after):
    def start(srcs, lands, copy):
        k, c, chips, _ = _peers()
        out = []
        for a, (src, dst) in enumerate(zip(srcs, lands)):
            for j, (cx, cy) in enumerate(chips):
                for core in range(2):
                    out.append(copy(GATHER_LATE_SEMS * a + 2 * j + core, GATHER_LATE_SEMS * a + 2 * j + c,
                                    src.at[*_half(c, src.shape)], dst.at[k, *_half(c, src.shape)], (cx, cy, core)))
        return out

    chip = 2 * lax.axis_index("x") + lax.axis_index("y")
    lands = [lax.dynamic_update_slice(lax.empty((NCHIP,) + s.shape, s.dtype), s[None], (chip, 0, 0)) for s in shards]
    return _split_start("gather_%s_start" % tag, shards, lands, after, GATHER_LATE_SEMS * len(shards), start)


def _gather_async_wait(tag, send_sems, recv_sems, shards, lands, after):
    def waits(srcs, lands_, copy):
        _, c, chips, _ = _peers()
        out = []
        for a, (src, dst) in enumerate(zip(srcs, lands_)):
            for j, (cx, cy) in enumerate(chips):
                for core in range(2):
                    idx = GATHER_LATE_SEMS * a + 2 * j + core
                    out.append(copy(idx, idx, src.at[*_half(c, src.shape)], dst.at[2 * cx + cy, *_half(core, src.shape)], (cx, cy, core)))
        return out

    return _split_wait("gather_%s_wait" % tag, send_sems, recv_sems, shards, lands, after, waits)[1]


def _other_devices():
    x, y, c = _coords()
    out = []
    for d in range(1, NDEV):
        tx, ty, tc = x ^ (d >> 2), y ^ ((d >> 1) & 1), c ^ (d & 1)
        out.append((d, (tx, ty, tc), 2 * tx + ty, 4 * tx + 2 * ty + tc))
    return out


def _reduce_async_start(tag, stacks, after):
    def start(srcs, lands, copy):
        x, y, c = _coords()
        me = 4 * x + 2 * y + c
        return [copy((NDEV - 1) * a + d - 1, (NDEV - 1) * a + d - 1, src.at[chip, *_half(to[2], src.shape)], dst.at[me], to)
                for a, (src, dst) in enumerate(zip(srcs, lands)) for d, to, chip, _ in _other_devices()]

    x, y, c = _coords()
    lands = []
    for s in stacks:
        hr, hc = _half_shape(s.shape)
        at = (c * hr, 0) if _half_axis(s.shape) == 0 else (0, c * hc)
        own = lax.dynamic_slice(s, (2 * x + y,) + at, (1, hr, hc))
        lands.append(lax.dynamic_update_slice(lax.empty((NDEV, hr, hc), s.dtype), own, (4 * x + 2 * y + c, 0, 0)))
    return _split_start("reduce_%s_start" % tag, stacks, lands, after, (NDEV - 1) * len(stacks), start)


def _reduce_async_wait(tag, send_sems, recv_sems, stacks, lands, after):
    def waits(srcs, lands_, copy):
        return [copy((NDEV - 1) * a + d - 1, (NDEV - 1) * a + d - 1, src.at[chip, *_half(to[2], src.shape)], dst.at[pos], to)
                for a, (src, dst) in enumerate(zip(srcs, lands_)) for d, to, chip, pos in _other_devices()]

    return _split_wait("reduce_%s_wait" % tag, send_sems, recv_sems, stacks, lands, after, waits)[1]


def _reduce_finish(tag, arrived, dims):
    n_arr = len(arrived)

    def body(*refs):
        lands, fin = refs[:n_arr], refs[n_arr:2 * n_arr]
        send_sems, recv_sems = refs[2 * n_arr:]
        _, c, _, sibling = _peers()
        sends = []
        for a in range(n_arr):
            mine = fin[a].at[*_half(c, dims[a])]

            def device_sum(vs, vf, a=a, mine=mine):
                pltpu.sync_copy(lands[a], vs)
                acc = vs[0].astype(F32)
                for i in range(1, NDEV):
                    acc = acc + vs[i].astype(F32)
                vf[...] = acc
                pltpu.sync_copy(vf, mine)

            pl.run_scoped(device_sum, pltpu.VMEM((NDEV,) + _half_shape(dims[a]), BF16), pltpu.VMEM(_half_shape(dims[a]), F32))
            sends.append(pltpu.make_async_remote_copy(src_ref=mine, dst_ref=mine, send_sem=send_sems.at[a], recv_sem=recv_sems.at[a],
                                                      device_id=sibling, device_id_type=MESH))
            sends[-1].start()
        for a in range(n_arr):
            other = fin[a].at[*_half(1 - c, dims[a])]
            pltpu.make_async_remote_copy(src_ref=other, dst_ref=other, send_sem=send_sems.at[a], recv_sem=recv_sems.at[a],
                                         device_id=sibling, device_id_type=MESH).wait_recv()
        for cp in sends:
            cp.wait_send()

    any_spec = pl.BlockSpec(memory_space=pl.ANY)
    return pl.pallas_call(
        body, name="reduce_%s_finish" % tag, in_specs=[any_spec] * n_arr, out_specs=[any_spec] * n_arr,
        out_shape=[jax.ShapeDtypeStruct(d, F32) for d in dims],
        scratch_shapes=[pltpu.SemaphoreType.DMA((n_arr,)), pltpu.SemaphoreType.DMA((n_arr,))],
    )(*arrived)


SMALL = (("conv_w", SSD_K * SSD_XBC), ("conv_b", SSD_XBC), ("dt_bias", H), ("A_log", H), ("D", H), ("ssd_norm", SSD_INNER),
         ("q_norm", Q_RANK), ("kv_norm", KV_RANK), ("out_norm", SSD_INNER), ("ln_mix_g", D), ("ln_mix_b", D),
         ("ln_ffn_g", D), ("ln_ffn_b", D))
SMALL_ROWS = 120
NDEV = 8


def _allreduce_small(sv):
    def body(sv_ref, out_ref, slots, send_sems, recv_sems):
        x, y, c = _coords()
        me = 4 * x + 2 * y + c
        slots[me] = sv_ref[...]
        copies = []
        for d in range(1, NDEV):
            to = (x ^ (d >> 2), y ^ ((d >> 1) & 1), c ^ (d & 1))
            copies.append(pltpu.make_async_remote_copy(src_ref=sv_ref, dst_ref=slots.at[me], send_sem=send_sems.at[d - 1],
                                                       recv_sem=recv_sems.at[d - 1], device_id=to, device_id_type=MESH))
            copies[-1].start()
        for cp in copies:
            cp.wait_recv()
        for cp in copies:
            cp.wait_send()
        acc = slots[0]
        for i in range(1, NDEV):
            acc = acc + slots[i]
        out_ref[...] = acc

    vm = pl.BlockSpec(memory_space=pltpu.VMEM)
    return pl.pallas_call(
        body, name="allreduce_small", in_specs=[vm], out_specs=vm, out_shape=jax.ShapeDtypeStruct((SMALL_ROWS, LANE), F32),
        scratch_shapes=[pltpu.VMEM((NDEV, SMALL_ROWS, LANE), F32), pltpu.SemaphoreType.DMA((NDEV - 1,)),
                        pltpu.SemaphoreType.DMA((NDEV - 1,))],
    )(sv)


def _adamw_math(w, g, m, v):
    m2 = ADAM_B1 * m + (1.0 - ADAM_B1) * g
    v2 = ADAM_B2 * v + (1.0 - ADAM_B2) * (g * g)
    m_hat = m2 / (1.0 - ADAM_B1 ** ADAM_STEP)
    v_hat = v2 / (1.0 - ADAM_B2 ** ADAM_STEP)
    return -ADAM_LR * (m_hat / (jnp.sqrt(v_hat) + ADAM_EPS) + ADAM_WD * w), m2, v2


def _adamw_big(w, g, m, v, *, name):
    r, c = w.shape

    def body(w_ref, g_ref, m_ref, v_ref, d_ref, m2_ref, v2_ref):
        d_ref[...], m2_ref[...], v2_ref[...] = _adamw_math(w_ref[...], g_ref[...], m_ref[...], v_ref[...])

    if r % 8 == 0:
        tr = next(t for t in (512, 384, 352, 256, 128, 64, 8) if r % t == 0)
        steps, spec = r // tr, pl.BlockSpec((tr, c), lambda i: (i, 0))
    else:
        steps, spec = c // (2 * LANE), pl.BlockSpec((r, 2 * LANE), lambda i: (0, i))
    return pl.pallas_call(body, name=name, grid=(steps,), in_specs=[spec] * 4, out_specs=[spec] * 3,
                          out_shape=[jax.ShapeDtypeStruct((r, c), F32)] * 3)(w, g, m, v)


def _adamw_small(ws, gs, ms, vs):
    n = len(ws)

    def body(*refs):
        for i in range(n):
            w_ref, g_ref, m_ref, v_ref = (refs[j * n + i] for j in range(4))
            d_ref, m2_ref, v2_ref = (refs[(4 + j) * n + i] for j in range(3))
            d_ref[...], m2_ref[...], v2_ref[...] = _adamw_math(w_ref[...], g_ref[...], m_ref[...], v_ref[...])

    vm = pl.BlockSpec(memory_space=pltpu.VMEM)
    shapes = [jax.ShapeDtypeStruct(w.shape, F32) for w in ws]
    outs = pl.pallas_call(body, name="adamw_small", in_specs=[vm] * (4 * n), out_specs=[vm] * (3 * n), out_shape=shapes * 3)(
        *ws, *gs, *ms, *vs)
    return outs[:n], outs[n:2 * n], outs[2 * n:]


_SMALL_ARG = {"conv_w": "ssd_conv_w", "conv_b": "ssd_conv_b", "dt_bias": "ssd_dt_bias", "A_log": "ssd_A_log", "D": "ssd_D",
              "ssd_norm": "ssd_norm_w", "q_norm": "mla_q_norm_w", "kv_norm": "mla_kv_norm_w", "out_norm": "mla_out_norm_w",
              "ln_mix_g": "ln_mix_g", "ln_mix_b": "ln_mix_b", "ln_ffn_g": "ln_ffn_g", "ln_ffn_b": "ln_ffn_b"}
_BIG_ARG = {"w_in": "w_in", "w_qb": "mla_w_q_b", "w_kvb": "mla_w_kv_b", "w_out": "w_out", "w_gate": "w_ffn_gate",
            "w_up": "w_ffn_up", "w_down": "w_ffn_down", "w_pg": "w_ple_gate", "w_pp": "w_ple_proj"}
_WEIGHT_ORDER = ("w_in", "ssd_conv_w", "ssd_conv_b", "ssd_dt_bias", "ssd_A_log", "ssd_D", "ssd_norm_w", "mla_q_norm_w", "mla_w_q_b",
                 "mla_kv_norm_w", "mla_w_kv_b", "mla_out_norm_w", "w_out", "ln_mix_g", "ln_mix_b", "w_ffn_gate", "w_ffn_up",
                 "w_ffn_down", "w_ple_gate", "w_ple_proj", "ln_ffn_g", "ln_ffn_b")


def _rows128(a):
    flat = a.reshape(-1)
    return jnp.pad(flat, (0, -flat.shape[0] % LANE)).reshape(-1, LANE)


def kernel(x, p, positions, w_in, ssd_conv_w, ssd_conv_b, ssd_dt_bias, ssd_A_log, ssd_D, ssd_norm_w, mla_q_norm_w, mla_w_q_b, mla_kv_norm_w, mla_w_kv_b, mla_out_norm_w, w_out, ln_mix_g, ln_mix_b, w_ffn_gate, w_ffn_up, w_ffn_down, w_ple_gate, w_ple_proj, ln_ffn_g, ln_ffn_b, loss_target, m_w_in, m_ssd_conv_w, m_ssd_conv_b, m_ssd_dt_bias, m_ssd_A_log, m_ssd_D, m_ssd_norm_w, m_mla_q_norm_w, m_mla_w_q_b, m_mla_kv_norm_w, m_mla_w_kv_b, m_mla_out_norm_w, m_w_out, m_ln_mix_g, m_ln_mix_b, m_w_ffn_gate, m_w_ffn_up, m_w_ffn_down, m_w_ple_gate, m_w_ple_proj, m_ln_ffn_g, m_ln_ffn_b, v_w_in, v_ssd_conv_w, v_ssd_conv_b, v_ssd_dt_bias, v_ssd_A_log, v_ssd_D, v_ssd_norm_w, v_mla_q_norm_w, v_mla_w_q_b, v_mla_kv_norm_w, v_mla_w_kv_b, v_mla_out_norm_w, v_w_out, v_ln_mix_g, v_ln_mix_b, v_w_ffn_gate, v_w_ffn_up, v_w_ffn_down, v_w_ple_gate, v_w_ple_proj, v_ln_ffn_g, v_ln_ffn_b):
    given = dict(locals())
    chip = 2 * lax.axis_index("x") + lax.axis_index("y")

    def local(name, prefix=""):
        a = given[prefix + _BIG_ARG[name]][0]
        return a.T if name in TRANSPOSED else a

    def global_layout(name, arr):
        return (arr.T if name in TRANSPOSED else arr)[None]

    conv_bits = lax.bitcast_convert_type(ssd_conv_w[0], BF16).reshape(SSD_K, 2 * CONV_SHARD)
    w_in_all, conv_all = _gather_weights([local("w_in").astype(BF16), jnp.pad(conv_bits, ((0, BF16_ROWS - SSD_K), (0, 0)))])
    sp = {k: given[a] for k, a in _SMALL_ARG.items() if k != "conv_w"}
    sp["conv_w"] = _from_cols(lax.bitcast_convert_type(conv_all[:, :SSD_K].reshape(NCHIP, SSD_K, CONV_SHARD, 2), F32))
    gathering, tie = {}, w_in_all
    for group, names in ASYNC_GROUPS:
        gathering[group], tie = _gather_async_start(group, [local(name).astype(BF16) for name in names], tie)

    def fetch(group, after):
        return dict(zip(dict(ASYNC_GROUPS)[group], _gather_async_wait(group, *gathering[group], after)))

    reducing = {}

    def send(group, grads):
        reducing[group], sent = _reduce_async_start(group, [grads[name] for name in dict(ASYNC_GROUPS)[group]], grads[dict(ASYNC_GROUPS)[group][0]])
        return sent[0, 0]

    loss_row, grad_x, dw_in, g = _local_step(x[0] + tie[0, 0], p[0, 0], positions[0], loss_target[0], w_in_all, fetch, send, sp)

    reducing["in"], tie = _reduce_async_start("in", [dw_in], grad_x)
    done_names = [name for _, names in reversed(ASYNC_GROUPS) for name in names]
    arrived = [a for group, _ in reversed(ASYNC_GROUPS) for a in _reduce_async_wait(group, *reducing[group], tie)]
    gbig = dict(zip(done_names, _reduce_finish("groups", arrived, [local(name).shape for name in done_names])))
    small_in = jnp.concatenate([_rows128(g[name]) for name, _ in SMALL] + [loss_row], axis=0)
    small_sum = _allreduce_small(jnp.pad(small_in, ((0, SMALL_ROWS - small_in.shape[0]), (0, 0))))
    gsmall, row = {}, 0
    for name, size in SMALL:
        nrow = -(-size // LANE)
        gsmall[name] = small_sum[row:row + nrow].reshape(-1)[:size]
        row += nrow
    loss = small_sum[row, 0]

    grads = {_BIG_ARG[name]: global_layout(name, arr) for name, arr in gbig.items()}
    for name, _ in SMALL:
        if name == "conv_w":
            full_g = gsmall[name].reshape(SSD_K, SSD_XBC)
            grads["ssd_conv_w"] = lax.dynamic_slice(full_g, (0, chip * CONV_SHARD), (SSD_K, CONV_SHARD))[None]
        else:
            grads[_SMALL_ARG[name]] = gsmall[name].reshape(given[_SMALL_ARG[name]].shape)

    delta, new_m, new_v = {}, {}, {}

    def update_matrix(name, grad):
        a = _BIG_ARG[name]
        d, m2, v2 = _adamw_big(local(name), grad, local(name, "m_"), local(name, "v_"), name="adamw_" + a)
        delta[a], new_m[a], new_v[a] = (global_layout(name, t) for t in (d, m2, v2))
        return d

    for name, grad in gbig.items():
        last = update_matrix(name, grad)
    g_in = _reduce_finish("in", _reduce_async_wait("in", *reducing["in"], last), [local("w_in").shape])[0]
    grads["w_in"] = global_layout("w_in", g_in)
    update_matrix("w_in", g_in)
    small_names = [_SMALL_ARG[name] for name, _ in SMALL]
    two_d = lambda t: t.reshape(t.shape[-2], t.shape[-1])
    ds, ms, vs = _adamw_small([two_d(given[a]) for a in small_names], [two_d(grads[a]) for a in small_names],
                              [two_d(given["m_" + a]) for a in small_names], [two_d(given["v_" + a]) for a in small_names])
    for a, d, m2, v2 in zip(small_names, ds, ms, vs):
        delta[a], new_m[a], new_v[a] = (t.reshape(given[a].shape) for t in (d, m2, v2))

    return (loss, grad_x[None], *[grads[n] for n in _WEIGHT_ORDER], *[delta[n] for n in _WEIGHT_ORDER],
            *[new_m[n] for n in _WEIGHT_ORDER], *[new_v[n] for n in _WEIGHT_ORDER])
```

```python
import functools
import math

import jax
import jax.numpy as jnp
from jax import lax
from jax.experimental import pallas as pl
from jax.experimental.pallas import tpu as pltpu

F32 = jnp.float32
BF16 = jnp.bfloat16

S = 2048
D = 1024
PLE = 256
H = 16
SSD_P = 64
SSD_INNER = 1024
SSD_N = 128
SSD_G = 2
SSD_L = 128
SSD_NC = S // SSD_L
SSD_XBC = 1536
SSD_K = 4
Q_RANK = 384
KV_RANK = 256
NOPE = 64
ROPE = 32
VDIM = 64
D_FF = 2816
IN_WIDTH = 3248
ALPHA = 2.0 ** 0.25
EPS_RMS = 1e-6
EPS_LN = 1e-5
ATT_SCALE = 1.0 / math.sqrt(NOPE + ROPE)
LN2 = math.log(2.0)
ATT_SCALE_LOG2 = ATT_SCALE / LN2
LANE = 128
NCHIP = 4
SMALL_W = 896
SM_Q, SM_KV, SM_KR, SM_DT = 0, 384, 640, 768
NEG = -1e30

ADAM_LR = 0.001
ADAM_B1 = 0.9
ADAM_B2 = 0.999
ADAM_EPS = 1e-08
ADAM_WD = 0.01
ADAM_STEP = 10


def _sigmoid(v):
    return 1.0 / (1.0 + jnp.exp(-v))


MM_VMEM_BUDGET = 36 * 2 ** 20
MM_MAX_ACC = 2048 * 1024


def _mm_tiles(pairs, ta, tb, m, n, out_dtype, has_add):
    def divs(v):
        return [LANE * d for d in range(v // LANE, 0, -1) if (v // LANE) % d == 0] if v % LANE == 0 else [v]

    def cost(tm, tn):
        tot = tm * tn * (jnp.dtype(out_dtype).itemsize + (4 if has_add else 0))
        for a, b in pairs:
            k = a.shape[-2] if ta else a.shape[-1]
            tot += k * (tm * a.dtype.itemsize + tn * b.dtype.itemsize)
        return 2 * tot

    ok = [(tm * tn, tm, tn) for tm in divs(m) for tn in divs(n) if tm * tn <= MM_MAX_ACC and cost(tm, tn) <= MM_VMEM_BUDGET]
    _, tm, tn = max(ok)
    return tm, tn


def _mm(pairs, *, ta=False, tb=False, out_dtype=F32, add=None, add_scale=1.0, chunk=None, name):
    n_pairs = len(pairs)
    a0, b0 = pairs[0]
    m = a0.shape[-1] if ta else a0.shape[-2]
    n = b0.shape[-2] if tb else b0.shape[-1]
    tm, tn = _mm_tiles(pairs, ta, tb, m, n, out_dtype, add is not None)
    dims = (((0 if ta else 1,), (1 if tb else 0,)), ((), ()))
    nk = NCHIP if chunk else 1
    assert chunk != "sum" or out_dtype == F32

    def body(*refs):
        o_ref = refs[-1]
        acc = None
        for i in range(n_pairs):
            a = refs[2 * i][...].astype(BF16)
            b = refs[2 * i + 1][...].astype(BF16)
            part = lax.dot_general(a, b, dims, preferred_element_type=F32)
            acc = part if acc is None else acc + part
        if chunk == "sum":
            k = pl.program_id(2)

            @pl.when(k == 0)
            def _():
                o_ref[...] = acc + add_scale * refs[2 * n_pairs][...] if add is not None else acc

            @pl.when(k > 0)
            def _():
                o_ref[...] += acc
        else:
            if add is not None:
                acc = acc + add_scale * refs[2 * n_pairs][...]
            o_ref[...] = acc.astype(out_dtype)

    def spec(arr, shape, idx2):
        if arr.ndim == 3:
            return pl.BlockSpec((None,) + shape, lambda i, j, k: (k,) + idx2(i, j))
        return pl.BlockSpec(shape, lambda i, j, k: idx2(i, j))

    in_specs, args = [], []
    for a, b in pairs:
        kdim = a.shape[-2] if ta else a.shape[-1]
        in_specs.append(spec(a, (kdim, tm), lambda i, j: (0, i)) if ta else spec(a, (tm, kdim), lambda i, j: (i, 0)))
        in_specs.append(spec(b, (tn, kdim), lambda i, j: (j, 0)) if tb else spec(b, (kdim, tn), lambda i, j: (0, j)))
        args += [a, b]
    if add is not None:
        in_specs.append(pl.BlockSpec((tm, tn), lambda i, j, k: (i, j)))
        args.append(add)
    if chunk == "out":
        out_spec = pl.BlockSpec((None, tm, tn), lambda i, j, k: (k, i, j))
        out_shape = jax.ShapeDtypeStruct((nk, m, n), out_dtype)
    else:
        out_spec = pl.BlockSpec((tm, tn), lambda i, j, k: (i, j))
        out_shape = jax.ShapeDtypeStruct((m, n), out_dtype)
    return pl.pallas_call(
        body, name=name, grid=(m // tm, n // tn, nk), in_specs=in_specs, out_specs=out_spec, out_shape=out_shape,
        compiler_params=pltpu.CompilerParams(dimension_semantics=("parallel", "parallel", "arbitrary")),
    )(*args)


TR = 256


def _row_spec(c):
    return pl.BlockSpec((TR, c), lambda i: (i, 0))


def _vec_spec(c):
    return pl.BlockSpec((1, c), lambda i: (0, 0))


def _acc_rows(ref, val):
    @pl.when(pl.program_id(0) == 0)
    def _():
        ref[...] = jnp.zeros_like(ref)
    ref[...] += val


def _rms_fwd(u, w, *, name):
    c = u.shape[1]

    def body(u_ref, w_ref, o_ref):
        v = u_ref[...]
        r = lax.rsqrt(jnp.mean(v * v, axis=-1, keepdims=True) + EPS_RMS)
        o_ref[...] = (v * r * w_ref[...]).astype(BF16)

    return pl.pallas_call(body, name=name, grid=(S // TR,), in_specs=[_row_spec(c), _vec_spec(c)], out_specs=_row_spec(c),
                          out_shape=jax.ShapeDtypeStruct((S, c), BF16))(u, w)


def _rms_bwd(u, w, dy, *, name):
    c = u.shape[1]

    def body(u_ref, w_ref, dy_ref, du_ref, dw_ref):
        v = u_ref[...]
        g = dy_ref[...].astype(F32)
        r = lax.rsqrt(jnp.mean(v * v, axis=-1, keepdims=True) + EPS_RMS)
        gw = g * w_ref[...]
        du_ref[...] = r * gw - v * (r * r * r * jnp.mean(gw * v, axis=-1, keepdims=True))
        _acc_rows(dw_ref, jnp.sum(g * v * r, axis=0, keepdims=True))

    return pl.pallas_call(body, name=name, grid=(S // TR,), in_specs=[_row_spec(c), _vec_spec(c), _row_spec(c)],
                          out_specs=[_row_spec(c), _vec_spec(c)],
                          out_shape=[jax.ShapeDtypeStruct((S, c), F32), jax.ShapeDtypeStruct((1, c), F32)])(u, w, dy)


def _gate_norm_fwd(y, z, w):
    def body(y_ref, z_ref, w_ref, o_ref):
        zz = z_ref[...]
        v = y_ref[...] * (zz * _sigmoid(zz))
        r = lax.rsqrt(jnp.mean(v * v, axis=-1, keepdims=True) + EPS_RMS)
        o_ref[...] = (v * r * w_ref[...]).astype(BF16)

    c = SSD_INNER
    return pl.pallas_call(body, name="ssd_gate_norm_fwd", grid=(S // TR,), in_specs=[_row_spec(c), _row_spec(c), _vec_spec(c)],
                          out_specs=_row_spec(c), out_shape=jax.ShapeDtypeStruct((S, c), BF16))(y, z, w)


def _gate_norm_bwd(y, z, w, dout):
    def body(y_ref, z_ref, w_ref, g_ref, dy_ref, dz_ref, dw_ref):
        yy = y_ref[...]
        zz = z_ref[...]
        sg = _sigmoid(zz)
        sz = zz * sg
        v = yy * sz
        g = g_ref[...]
        r = lax.rsqrt(jnp.mean(v * v, axis=-1, keepdims=True) + EPS_RMS)
        gw = g * w_ref[...]
        dv = r * gw - v * (r * r * r * jnp.mean(gw * v, axis=-1, keepdims=True))
        dy_ref[...] = dv * sz
        dz_ref[...] = (dv * yy * (sg * (1.0 + zz * (1.0 - sg)))).astype(BF16)
        _acc_rows(dw_ref, jnp.sum(g * v * r, axis=0, keepdims=True))

    c = SSD_INNER
    return pl.pallas_call(body, name="ssd_gate_norm_bwd", grid=(S // TR,),
                          in_specs=[_row_spec(c), _row_spec(c), _vec_spec(c), _row_spec(c)],
                          out_specs=[_row_spec(c), _row_spec(c), _vec_spec(c)],
                          out_shape=[jax.ShapeDtypeStruct((S, c), F32), jax.ShapeDtypeStruct((S, c), BF16),
                                     jax.ShapeDtypeStruct((1, c), F32)])(y, z, w, dout)


def _ln_fwd(xr, mix, g, b):
    def body(x_ref, m_ref, g_ref, b_ref, o_ref, ob_ref):
        pre = ALPHA * x_ref[...] + m_ref[...]
        mu = jnp.mean(pre, axis=-1, keepdims=True)
        d = pre - mu
        rs = lax.rsqrt(jnp.mean(d * d, axis=-1, keepdims=True) + EPS_LN)
        h = d * rs * g_ref[...] + b_ref[...]
        o_ref[...] = h
        ob_ref[...] = h.astype(BF16)

    return pl.pallas_call(body, name="ln_mix_fwd", grid=(S // TR,), in_specs=[_row_spec(D), _row_spec(D), _vec_spec(D), _vec_spec(D)],
                          out_specs=[_row_spec(D)] * 2,
                          out_shape=[jax.ShapeDtypeStruct((S, D), F32), jax.ShapeDtypeStruct((S, D), BF16)])(xr, mix, g, b)


def _ln_bwd(xr, mix, g, dh):
    def body(x_ref, m_ref, g_ref, dh_ref, dpre_ref, dpreb_ref, dg_ref, db_ref):
        pre = ALPHA * x_ref[...] + m_ref[...]
        mu = jnp.mean(pre, axis=-1, keepdims=True)
        d = pre - mu
        rs = lax.rsqrt(jnp.mean(d * d, axis=-1, keepdims=True) + EPS_LN)
        xh = d * rs
        dy = dh_ref[...]
        gy = dy * g_ref[...]
        dpre = rs * (gy - jnp.mean(gy, axis=-1, keepdims=True) - xh * jnp.mean(gy * xh, axis=-1, keepdims=True))
        dpre_ref[...] = dpre
        dpreb_ref[...] = dpre.astype(BF16)
        _acc_rows(dg_ref, jnp.sum(dy * xh, axis=0, keepdims=True))
        _acc_rows(db_ref, jnp.sum(dy, axis=0, keepdims=True))

    return pl.pallas_call(body, name="ln_mix_bwd", grid=(S // TR,),
                          in_specs=[_row_spec(D), _row_spec(D), _vec_spec(D), _row_spec(D)],
                          out_specs=[_row_spec(D), _row_spec(D), _vec_spec(D), _vec_spec(D)],
                          out_shape=[jax.ShapeDtypeStruct((S, D), F32), jax.ShapeDtypeStruct((S, D), BF16),
                                     jax.ShapeDtypeStruct((1, D), F32), jax.ShapeDtypeStruct((1, D), F32)])(xr, mix, g, dh)


FF_CHUNK = D_FF // NCHIP


FF_ROWS = 1024


def _ff_act_spec():
    return pl.BlockSpec((None, FF_ROWS, FF_CHUNK), lambda i, k: (k, i, 0))


def _ff_w_spec():
    return pl.BlockSpec((None, FF_CHUNK, D), lambda i, k: (k, 0, 0))


def _ffn_hidden_fwd(h, w_gate_t, w_up_t):
    def body(h_ref, wg_ref, wu_ref, g_ref, u_ref, a_ref):
        hh = h_ref[...]
        g = _dot(hh, wg_ref[...], ((1,), (1,)))
        u = _dot(hh, wu_ref[...], ((1,), (1,)))
        g_ref[...] = g.astype(BF16)
        u_ref[...] = u.astype(BF16)
        a_ref[...] = (g * _sigmoid(g) * u).astype(BF16)

    return pl.pallas_call(
        body, name="ffn_hidden_fwd", grid=(S // FF_ROWS, NCHIP),
        in_specs=[pl.BlockSpec((FF_ROWS, D), lambda i, k: (i, 0)), _ff_w_spec(), _ff_w_spec()], out_specs=[_ff_act_spec()] * 3,
        out_shape=[jax.ShapeDtypeStruct((NCHIP, S, FF_CHUNK), BF16)] * 3,
        compiler_params=pltpu.CompilerParams(dimension_semantics=("parallel", "parallel")),
    )(h, w_gate_t, w_up_t)


def _ffn_hidden_bwd(dout, w_down, gate, up):
    def body(d_ref, wd_ref, g_ref, u_ref, dg_ref, du_ref):
        d = _dot(d_ref[...], wd_ref[...], ((1,), (1,)))
        g = g_ref[...].astype(F32)
        sg = _sigmoid(g)
        dg_ref[...] = (d * u_ref[...].astype(F32) * (sg * (1.0 + g * (1.0 - sg)))).astype(BF16)
        du_ref[...] = (d * g * sg).astype(BF16)

    return pl.pallas_call(
        body, name="ffn_hidden_bwd", grid=(S // FF_ROWS, NCHIP),
        in_specs=[pl.BlockSpec((FF_ROWS, D), lambda i, k: (i, 0)), _ff_w_spec(), _ff_act_spec(), _ff_act_spec()],
        out_specs=[_ff_act_spec()] * 2, out_shape=[jax.ShapeDtypeStruct((NCHIP, S, FF_CHUNK), BF16)] * 2,
        compiler_params=pltpu.CompilerParams(dimension_semantics=("parallel", "parallel")),
    )(dout, w_down, gate, up)


def _final_fwd_bwd(h1, ffn, pg, pp, target, g2, b2):
    def body(h_ref, f_ref, pg_ref, pp_ref, t_ref, g_ref, b_ref, dpre_ref, dpreb_ref, dpg_ref, dpp_ref, dg_ref, db_ref, loss_ref):
        sg = _sigmoid(pg_ref[...])
        ppv = pp_ref[...]
        pre = ALPHA * h_ref[...] + f_ref[...] + sg * ppv
        mu = jnp.mean(pre, axis=-1, keepdims=True)
        d = pre - mu
        rs = lax.rsqrt(jnp.mean(d * d, axis=-1, keepdims=True) + EPS_LN)
        xh = d * rs
        err = xh * g_ref[...] + b_ref[...] - t_ref[...]
        dy = err * (1.0 / D)
        gy = dy * g_ref[...]
        dpre = rs * (gy - jnp.mean(gy, axis=-1, keepdims=True) - xh * jnp.mean(gy * xh, axis=-1, keepdims=True))
        dpre_ref[...] = dpre
        dpreb_ref[...] = dpre.astype(BF16)
        dpg_ref[...] = (dpre * ppv * sg * (1.0 - sg)).astype(BF16)
        dpp_ref[...] = (dpre * sg).astype(BF16)
        _acc_rows(dg_ref, jnp.sum(dy * xh, axis=0, keepdims=True))
        _acc_rows(db_ref, jnp.sum(dy, axis=0, keepdims=True))
        _acc_rows(loss_ref, 0.5 * jnp.sum(jnp.mean(err * err, axis=-1, keepdims=True), axis=0, keepdims=True) * jnp.ones((1, LANE), F32))

    return pl.pallas_call(
        body, name="final_ln_loss", grid=(S // TR,),
        in_specs=[_row_spec(D)] * 5 + [_vec_spec(D)] * 2,
        out_specs=[_row_spec(D)] * 4 + [_vec_spec(D), _vec_spec(D), _vec_spec(LANE)],
        out_shape=[jax.ShapeDtypeStruct((S, D), F32)] + [jax.ShapeDtypeStruct((S, D), BF16)] * 3 + [
                   jax.ShapeDtypeStruct((1, D), F32), jax.ShapeDtypeStruct((1, D), F32), jax.ShapeDtypeStruct((1, LANE), F32)],
    )(h1, ffn, pg, pp, target, g2, b2)


def _rot(u, cos_t, sin_t, lane):
    partner = jnp.where(lane < NOPE + ROPE // 2, pltpu.roll(u, LANE - ROPE // 2, 1), pltpu.roll(u, ROPE // 2, 1))
    return u * cos_t + partner * sin_t


def _rms(v, w):
    r = lax.rsqrt(jnp.mean(v * v, axis=-1, keepdims=True) + EPS_RMS)
    return v * r * w, r


def _rms_grad(v, r, w, g):
    gw = g * w
    return r * gw - v * (r * r * r * jnp.mean(gw * v, axis=-1, keepdims=True)), jnp.sum(g * v * r, axis=0, keepdims=True)


def _whole(arr):
    return pl.BlockSpec(arr.shape, lambda i: (0,) * arr.ndim)


def _qkv_fwd(small, w_q, w_k, w_v, q_norm, kv_norm, cos_t, sin_t):
    def body(sm_ref, wq_ref, wk_ref, wv_ref, qw_ref, kw_ref, c_ref, s_ref, qn_ref, kvn_ref, q_ref, k_ref, kt_ref, v_ref):
        lane = lax.broadcasted_iota(jnp.int32, (TR, LANE), 1)
        c, s = c_ref[...], s_ref[...]
        qn = _rms(sm_ref[:, SM_Q:SM_Q + Q_RANK], qw_ref[...])[0].astype(BF16)
        kvn = _rms(sm_ref[:, SM_KV:SM_KV + KV_RANK], kw_ref[...])[0].astype(BF16)
        qn_ref[...] = qn
        kvn_ref[...] = kvn
        kr = _rot(pltpu.roll(sm_ref[:, SM_KR:SM_KR + LANE], NOPE, 1), c, s, lane)
        for h in range(H):
            tile = slice(h * LANE, (h + 1) * LANE)
            q_ref[:, tile] = _rot(_dot(qn, wq_ref[:, tile], ((1,), (0,))), c, s, lane).astype(BF16)
            kt = _dot(kvn, wk_ref[:, tile], ((1,), (0,))) + kr
            k_ref[:, tile] = kt.astype(BF16)
            kt_ref[tile, :] = kt.T.astype(BF16)
        v_ref[...] = _dot(kvn, wv_ref[...], ((1,), (0,))).astype(BF16)

    w = H * LANE
    return pl.pallas_call(
        body, name="qkv_fwd", grid=(S // TR,),
        in_specs=[_row_spec(SMALL_W), _whole(w_q), _whole(w_k), _whole(w_v), _vec_spec(Q_RANK), _vec_spec(KV_RANK), _row_spec(LANE), _row_spec(LANE)],
        out_specs=[_row_spec(Q_RANK), _row_spec(KV_RANK), _row_spec(w), _row_spec(w), pl.BlockSpec((w, TR), lambda i: (0, i)),
                   _row_spec(H * VDIM)],
        out_shape=[jax.ShapeDtypeStruct((S, Q_RANK), BF16), jax.ShapeDtypeStruct((S, KV_RANK), BF16), jax.ShapeDtypeStruct((S, w), BF16),
                   jax.ShapeDtypeStruct((S, w), BF16), jax.ShapeDtypeStruct((w, S), BF16), jax.ShapeDtypeStruct((S, H * VDIM), BF16)],
    )(small, w_q, w_k, w_v, q_norm, kv_norm, cos_t, sin_t)


def _qkv_bwd(dqt, dk, dv, small, w_q, w_k, w_v, q_norm, kv_norm, cos_t, sin_t):
    def body(dq_ref, dk_ref, dv_ref, sm_ref, wq_ref, wk_ref, wv_ref, qw_ref, kw_ref, c_ref, s_ref,
             ds_ref, dql_ref, dkb_ref, dqw_ref, dkw_ref):
        lane = lax.broadcasted_iota(jnp.int32, (TR, LANE), 1)
        c, s = c_ref[...], -s_ref[...]
        dqn = jnp.zeros((TR, Q_RANK), F32)
        dkvn = _dot(dv_ref[...], wv_ref[...], ((1,), (1,)))
        dkr = jnp.zeros((TR, LANE), F32)
        for h in range(H):
            tile = slice(h * LANE, (h + 1) * LANE)
            dql = _rot(dq_ref[tile, :].T, c, s, lane).astype(BF16)
            dql_ref[:, tile] = dql
            dqn = dqn + _dot(dql, wq_ref[:, tile], ((1,), (1,)))
            dkt = dk_ref[:, tile]
            dkb_ref[:, tile] = dkt.astype(BF16)
            dkvn = dkvn + _dot(dkt, wk_ref[:, tile], ((1,), (1,)))
            dkr = dkr + dkt
        dkr = jnp.where((lane >= NOPE) & (lane < NOPE + ROPE), dkr, 0.0)
        q_c, kv_c = sm_ref[:, SM_Q:SM_Q + Q_RANK], sm_ref[:, SM_KV:SM_KV + KV_RANK]
        dq_c, dqw = _rms_grad(q_c, _rms(q_c, qw_ref[...])[1], qw_ref[...], dqn)
        dkv_c, dkw = _rms_grad(kv_c, _rms(kv_c, kw_ref[...])[1], kw_ref[...], dkvn)
        ds_ref[:, SM_Q:SM_Q + Q_RANK] = dq_c.astype(BF16)
        ds_ref[:, SM_KV:SM_KV + KV_RANK] = dkv_c.astype(BF16)
        ds_ref[:, SM_KR:SM_KR + LANE] = pltpu.roll(_rot(dkr, c, s, lane), LANE - NOPE, 1).astype(BF16)
        _acc_rows(dqw_ref, dqw)
        _acc_rows(dkw_ref, dkw)

    w = H * LANE
    return pl.pallas_call(
        body, name="qkv_bwd", grid=(S // TR,),
        in_specs=[pl.BlockSpec((w, TR), lambda i: (0, i)), _row_spec(w), _row_spec(H * VDIM), _row_spec(SMALL_W), _whole(w_q), _whole(w_k),
                  _whole(w_v), _vec_spec(Q_RANK), _vec_spec(KV_RANK), _row_spec(LANE), _row_spec(LANE)],
        out_specs=[_row_spec(SM_DT), _row_spec(w), _row_spec(w), _vec_spec(Q_RANK), _vec_spec(KV_RANK)],
        out_shape=[jax.ShapeDtypeStruct((S, SM_DT), BF16), jax.ShapeDtypeStruct((S, w), BF16), jax.ShapeDtypeStruct((S, w), BF16),
                   jax.ShapeDtypeStruct((1, Q_RANK), F32), jax.ShapeDtypeStruct((1, KV_RANK), F32)],
    )(dqt, dk, dv, small, w_q, w_k, w_v, q_norm, kv_norm, cos_t, sin_t)


CB = 256


def _shift_down(u, k, row):
    if k == 0:
        return u
    return jnp.where(row >= k, pltpu.roll(u, k, 0), 0.0)


def _shift_up(u, k, row):
    if k == 0:
        return u
    return jnp.where(row < S - k, pltpu.roll(u, S - k, 0), 0.0)


def _conv_fwd(u, w, b):
    def body(u_ref, w_ref, b_ref, o_ref):
        row = lax.broadcasted_iota(jnp.int32, (S, CB), 0)
        uu = u_ref[...]
        acc = b_ref[...] + w_ref[SSD_K - 1:SSD_K, :] * uu
        for k in range(SSD_K - 1):
            acc = acc + w_ref[k:k + 1, :] * _shift_down(uu, SSD_K - 1 - k, row)
        o_ref[...] = acc * _sigmoid(acc)

    c = u.shape[1]
    return pl.pallas_call(
        body, name="conv_fwd", grid=(c // CB,),
        in_specs=[pl.BlockSpec((S, CB), lambda j: (0, j)), pl.BlockSpec((SSD_K, CB), lambda j: (0, j)), pl.BlockSpec((1, CB), lambda j: (0, j))],
        out_specs=pl.BlockSpec((S, CB), lambda j: (0, j)), out_shape=jax.ShapeDtypeStruct((S, c), F32),
    )(u, w, b)


def _conv_bwd(u, w, b, dact):
    def body(u_ref, w_ref, b_ref, d_ref, du_ref, dw_ref, db_ref):
        row = lax.broadcasted_iota(jnp.int32, (S, CB), 0)
        uu = u_ref[...]
        sh = [_shift_down(uu, SSD_K - 1 - k, row) for k in range(SSD_K)]
        acc = b_ref[...]
        for k in range(SSD_K):
            acc = acc + w_ref[k:k + 1, :] * sh[k]
        sg = _sigmoid(acc)
        dacc = d_ref[...] * (sg * (1.0 + acc * (1.0 - sg)))
        du = w_ref[SSD_K - 1:SSD_K, :] * dacc
        for k in range(SSD_K - 1):
            du = du + w_ref[k:k + 1, :] * _shift_up(dacc, SSD_K - 1 - k, row)
        du_ref[...] = du.astype(BF16)
        for k in range(SSD_K):
            dw_ref[k:k + 1, :] = jnp.sum(dacc * sh[k], axis=0, keepdims=True)
        db_ref[...] = jnp.sum(dacc, axis=0, keepdims=True)

    c = u.shape[1]
    col = lambda r: pl.BlockSpec((r, CB), lambda j: (0, j))
    return pl.pallas_call(
        body, name="conv_bwd", grid=(c // CB,), in_specs=[col(S), col(SSD_K), col(1), col(S)], out_specs=[col(S), col(SSD_K), col(1)],
        out_shape=[jax.ShapeDtypeStruct((S, c), BF16), jax.ShapeDtypeStruct((SSD_K, c), F32), jax.ShapeDtypeStruct((1, c), F32)],
    )(u, w, b, dact)


NPAIR = H // 2
PAIRS_PER_GROUP = NPAIR // SSD_G


def _softplus(v):
    return jnp.maximum(v, 0.0) + jnp.log(1.0 + jnp.exp(-jnp.abs(v)))


def _dot(a, b, dims):
    return lax.dot_general(a.astype(BF16), b.astype(BF16), (dims, ((), ())), preferred_element_type=F32)


def _dot2(a, sel):
    hi = a.astype(BF16)
    lo = (a - hi.astype(F32)).astype(BF16)
    dims = (((1,), (0,)), ((), ()))
    return lax.dot_general(hi, sel, dims, preferred_element_type=F32) + lax.dot_general(lo, sel, dims, preferred_element_type=F32)


def _dot3(a, b, dims, split_lhs):
    v = a if split_lhs else b
    v1 = v.astype(BF16)
    r1 = v - v1.astype(F32)
    v2 = r1.astype(BF16)
    v3 = (r1 - v2.astype(F32)).astype(BF16)
    acc = None
    for part in (v1, v2, v3):
        lhs, rhs = (part, b) if split_lhs else (a, part)
        t = lax.dot_general(lhs, rhs, (dims, ((), ())), preferred_element_type=F32)
        acc = t if acc is None else acc + t
    return acc


def _ssd_chunk_common(dt_ref, dtT_ref, prow_ref, pcol_ref):
    prow = prow_ref[...]
    pcol = pcol_ref[...]
    ri = lax.broadcasted_iota(jnp.int32, (SSD_L, SSD_L), 0)
    ci = lax.broadcasted_iota(jnp.int32, (SSD_L, SSD_L), 1)
    causal = ri >= ci
    pre_c = dt_ref[...] + prow[0:1, :]
    dtc = _softplus(pre_c)
    a_row = -jnp.exp(prow[1:2, :])
    cs_col = _dot3(causal.astype(BF16), dtc * a_row, ((1,), (0,)), False)
    dtr = _softplus(dtT_ref[...] + pcol[:, 0:1])
    a_col = -jnp.exp(pcol[:, 1:2])
    cs_row = _dot3(dtr * a_col, (ri <= ci).astype(BF16), ((1,), (0,)), True)
    return prow, causal, pre_c, dtc, a_row, cs_col, cs_row


def _ssd_fwd(act, small, dtT, prow, pcol):
    def body(x_ref, b_ref, c_ref, dt_ref, dtT_ref, prow_ref, pcol_ref, y_ref, st_ref, state):
        @pl.when(pl.program_id(0) == 0)
        def _():
            state[...] = jnp.zeros_like(state)

        prow, causal, _, dtc, _, cs_col, cs_row = _ssd_chunk_common(dt_ref, dtT_ref, prow_ref, pcol_ref)
        lo = lax.broadcasted_iota(jnp.int32, (SSD_L, LANE), 1) < SSD_P
        lo1 = lo[0:1, :]
        for g in range(SSD_G):
            bm = b_ref[:, g * SSD_N:(g + 1) * SSD_N]
            cm = c_ref[:, g * SSD_N:(g + 1) * SSD_N]
            cb = _dot(cm, bm, ((1,), (1,)))
            for qq in range(PAIRS_PER_GROUP):
                q = g * PAIRS_PER_GROUP + qq
                ha, hb = 2 * q, 2 * q + 1
                csa, csb = cs_col[:, ha:ha + 1], cs_col[:, hb:hb + 1]
                xp = x_ref[:, q * LANE:(q + 1) * LANE]
                xx = xp * jnp.where(lo, dtc[:, ha:ha + 1], dtc[:, hb:hb + 1])
                ga = cb * jnp.exp(jnp.where(causal, csa - cs_row[ha:ha + 1, :], NEG))
                gb = cb * jnp.exp(jnp.where(causal, csb - cs_row[hb:hb + 1, :], NEG))
                y = _dot(ga, jnp.where(lo, xx, 0.0), ((1,), (0,))) + _dot(gb, jnp.where(lo, 0.0, xx), ((1,), (0,)))
                s_in = state[q]
                y = y + _dot(cm, s_in, ((1,), (0,))) * jnp.where(lo, jnp.exp(csa), jnp.exp(csb))
                y = y + jnp.where(lo1, prow[2:3, ha:ha + 1], prow[2:3, hb:hb + 1]) * xp
                y_ref[:, q * LANE:(q + 1) * LANE] = y
                la, lb = csa[SSD_L - 1:SSD_L, :], csb[SSD_L - 1:SSD_L, :]
                decay = jnp.where(lo, jnp.exp(la - csa), jnp.exp(lb - csb))
                st_ref[q] = s_in
                state[q] = s_in * jnp.where(lo1, jnp.exp(la), jnp.exp(lb)) + _dot(bm, xx * decay, ((0,), (0,)))

    L = SSD_L
    return pl.pallas_call(
        body, name="ssd_fwd", grid=(SSD_NC,),
        in_specs=[pl.BlockSpec((L, SSD_INNER), lambda c: (c, 0)),
                  pl.BlockSpec((L, SSD_G * SSD_N), lambda c: (c, SSD_INNER // (SSD_G * SSD_N))),
                  pl.BlockSpec((L, SSD_G * SSD_N), lambda c: (c, SSD_INNER // (SSD_G * SSD_N) + 1)),
                  pl.BlockSpec((L, LANE), lambda c: (c, SM_DT // LANE)),
                  pl.BlockSpec((LANE, L), lambda c: (0, c)),
                  pl.BlockSpec((8, LANE), lambda c: (0, 0)), pl.BlockSpec((LANE, 8), lambda c: (0, 0))],
        out_specs=[pl.BlockSpec((L, SSD_INNER), lambda c: (c, 0)),
                   pl.BlockSpec((None, NPAIR, SSD_N, LANE), lambda c: (c, 0, 0, 0))],
        out_shape=[jax.ShapeDtypeStruct((S, SSD_INNER), F32), jax.ShapeDtypeStruct((SSD_NC, NPAIR, SSD_N, LANE), F32)],
        scratch_shapes=[pltpu.VMEM((NPAIR, SSD_N, LANE), F32)],
        compiler_params=pltpu.CompilerParams(dimension_semantics=("arbitrary",)),
    )(act, act, act, small, dtT, prow, pcol)


def _ssd_bwd(act, small, dtT, prow, pcol, states, dy):
    def body(x_ref, b_ref, c_ref, dt_ref, dtT_ref, prow_ref, pcol_ref, st_ref, dy_ref,
             dx_ref, ddt_ref, dp_ref, dstate):
        @pl.when(pl.program_id(0) == 0)
        def _():
            dstate[...] = jnp.zeros_like(dstate)
            dp_ref[...] = jnp.zeros_like(dp_ref)

        prow, causal, pre_c, dtc, a_row, cs_col, cs_row = _ssd_chunk_common(dt_ref, dtT_ref, prow_ref, pcol_ref)
        lane = lax.broadcasted_iota(jnp.int32, (SSD_L, LANE), 1)
        sub = lax.broadcasted_iota(jnp.int32, (LANE, SSD_L), 0)
        rowi = lax.broadcasted_iota(jnp.int32, (SSD_L, 1), 0)
        pick_p = lax.broadcasted_iota(jnp.int32, (LANE, LANE), 0)
        pick_l = lax.broadcasted_iota(jnp.int32, (LANE, LANE), 1)
        lo = lane < SSD_P
        lo1 = lo[0:1, :]
        dcs_c = jnp.zeros((SSD_L, LANE), F32)
        dcs_r = jnp.zeros((LANE, SSD_L), F32)
        ddt_x = jnp.zeros((SSD_L, LANE), F32)
        dd_row = jnp.zeros((1, LANE), F32)
        for g in range(SSD_G):
            bm = b_ref[:, g * SSD_N:(g + 1) * SSD_N]
            cm = c_ref[:, g * SSD_N:(g + 1) * SSD_N]
            cb = _dot(cm, bm, ((1,), (1,)))
            dcb = jnp.zeros((SSD_L, SSD_L), F32)
            dbm = jnp.zeros((SSD_L, SSD_N), F32)
            dcm = jnp.zeros((SSD_L, SSD_N), F32)
            for qq in range(PAIRS_PER_GROUP):
                q = g * PAIRS_PER_GROUP + qq
                ha, hb = 2 * q, 2 * q + 1
                csa, csb = cs_col[:, ha:ha + 1], cs_col[:, hb:hb + 1]
                xp = x_ref[:, q * LANE:(q + 1) * LANE]
                dtp = jnp.where(lo, dtc[:, ha:ha + 1], dtc[:, hb:hb + 1])
                xx = xp * dtp
                lma = jnp.exp(jnp.where(causal, csa - cs_row[ha:ha + 1, :], NEG))
                lmb = jnp.exp(jnp.where(causal, csb - cs_row[hb:hb + 1, :], NEG))
                ga, gb = cb * lma, cb * lmb
                dyp = dy_ref[:, q * LANE:(q + 1) * LANE]
                dya, dyb = jnp.where(lo, dyp, 0.0), jnp.where(lo, 0.0, dyp)
                s_in = st_ref[q]
                ds_out = dstate[q]
                la, lb = csa[SSD_L - 1:SSD_L, :], csb[SSD_L - 1:SSD_L, :]
                ecs = jnp.where(lo, jnp.exp(csa), jnp.exp(csb))
                decay = jnp.where(lo, jnp.exp(la - csa), jnp.exp(lb - csb))
                cd = jnp.where(lo1, jnp.exp(la), jnp.exp(lb))
                bds = _dot(bm, ds_out, ((1,), (0,)))
                dxx = _dot(ga, dya, ((0,), (0,))) + _dot(gb, dyb, ((0,), (0,))) + bds * decay
                dga = _dot(dya, xx, ((1,), (1,)))
                dgb = _dot(dyb, xx, ((1,), (1,)))
                dsega, dsegb = dga * ga, dgb * gb
                dcb = dcb + dga * lma + dgb * lmb
                yoff = _dot(cm, s_in, ((1,), (0,))) * ecs
                dye = dyp * ecs
                dcm = dcm + _dot(dye, s_in, ((1,), (1,)))
                xd = xx * decay
                dbm = dbm + _dot(xd, ds_out, ((1,), (1,)))
                wv = xd * bds
                ends = jnp.sum(wv, axis=0, keepdims=True) + cd * jnp.sum(ds_out * s_in, axis=0, keepdims=True)
                t1 = dyp * yoff - wv + jnp.where(rowi == SSD_L - 1, ends, 0.0)
                to_pair = (((pick_p < SSD_P) & (pick_l == ha)) | ((pick_p >= SSD_P) & (pick_l == hb))).astype(BF16)
                to_a_b = jnp.concatenate([(pick_l == ha).astype(BF16), (pick_l == hb).astype(BF16)], axis=0)
                dcs_c = dcs_c + _dot2(t1, to_pair) + _dot2(jnp.concatenate([dsega, dsegb], axis=1), to_a_b)
                dcs_r = (dcs_r + jnp.where(sub == ha, jnp.sum(dsega, axis=0, keepdims=True), 0.0)
                         + jnp.where(sub == hb, jnp.sum(dsegb, axis=0, keepdims=True), 0.0))
                dstate[q] = _dot(cm, dye, ((0,), (0,))) + cd * ds_out
                dpair = jnp.where(lo1, prow[2:3, ha:ha + 1], prow[2:3, hb:hb + 1])
                dx_ref[:, q * LANE:(q + 1) * LANE] = dxx * dtp + dpair * dyp
                ddt_x = ddt_x + _dot2(dxx * xp, to_pair)
                dd_row = dd_row + jnp.sum(_dot2(dyp * xp, to_pair), axis=0, keepdims=True)
            dx_ref[:, SSD_INNER + g * SSD_N:SSD_INNER + (g + 1) * SSD_N] = dbm + _dot(dcb, cm, ((0,), (0,)))
            dx_ref[:, SSD_INNER + (SSD_G + g) * SSD_N:SSD_INNER + (SSD_G + g + 1) * SSD_N] = dcm + _dot(dcb, bm, ((1,), (0,)))
        ri = lax.broadcasted_iota(jnp.int32, (SSD_L, SSD_L), 0)
        ci = lax.broadcasted_iota(jnp.int32, (SSD_L, SSD_L), 1)
        da = _dot3((ri <= ci).astype(BF16), dcs_c, ((1,), (0,)), False)
        da = da - _dot3(dcs_r, causal.astype(BF16), ((1,), (0,)), True).T
        ddt = ddt_x + da * a_row
        ddt_raw = ddt * _sigmoid(pre_c)
        ddt_ref[...] = ddt_raw
        da_head = jnp.sum(da * dtc, axis=0, keepdims=True) * a_row
        dp_ref[0:1, :] += jnp.sum(ddt_raw, axis=0, keepdims=True)
        dp_ref[1:2, :] += da_head
        dp_ref[2:3, :] += dd_row

    L = SSD_L
    rev = SSD_NC - 1
    bc_cols = SSD_INNER // (SSD_G * SSD_N)
    return pl.pallas_call(
        body, name="ssd_bwd", grid=(SSD_NC,),
        in_specs=[pl.BlockSpec((L, SSD_INNER), lambda c: (rev - c, 0)),
                  pl.BlockSpec((L, SSD_G * SSD_N), lambda c: (rev - c, bc_cols)),
                  pl.BlockSpec((L, SSD_G * SSD_N), lambda c: (rev - c, bc_cols + 1)),
                  pl.BlockSpec((L, LANE), lambda c: (rev - c, SM_DT // LANE)),
                  pl.BlockSpec((LANE, L), lambda c: (0, rev - c)),
                  pl.BlockSpec((8, LANE), lambda c: (0, 0)), pl.BlockSpec((LANE, 8), lambda c: (0, 0)),
                  pl.BlockSpec((None, NPAIR, SSD_N, LANE), lambda c: (rev - c, 0, 0, 0)),
                  pl.BlockSpec((L, SSD_INNER), lambda c: (rev - c, 0))],
        out_specs=[pl.BlockSpec((L, SSD_XBC), lambda c: (rev - c, 0)),
                   pl.BlockSpec((L, LANE), lambda c: (rev - c, 0)),
                   pl.BlockSpec((8, LANE), lambda c: (0, 0))],
        out_shape=[jax.ShapeDtypeStruct((S, SSD_XBC), F32), jax.ShapeDtypeStruct((S, LANE), F32),
                   jax.ShapeDtypeStruct((8, LANE), F32)],
        scratch_shapes=[pltpu.VMEM((NPAIR, SSD_N, LANE), F32)],
        compiler_params=pltpu.CompilerParams(dimension_semantics=("arbitrary",)),
    )(act, act, act, small, dtT, prow, pcol, states, dy)


TQ = 256
TK = 256
FWD_TQ = 256
FWD_TK = 256


def _attn_fwd(qc, kc, v):
    TQ, TK = FWD_TQ, FWD_TK

    def body(q_ref, k_ref, v_ref, o_ref, lse_ref):
        i = pl.program_id(1)
        lo = lax.broadcasted_iota(jnp.int32, (TQ, LANE), 1) < VDIM
        lo_k = lax.broadcasted_iota(jnp.int32, (TK, LANE), 1) < VDIM
        row_minus_col = lax.broadcasted_iota(jnp.int32, (TQ, TK), 0) - lax.broadcasted_iota(jnp.int32, (TQ, TK), 1)
        qa, qb = q_ref[:, 0:LANE], q_ref[:, LANE:2 * LANE]

        def scores(kb):
            kk = k_ref[pl.ds(pl.multiple_of(kb * TK, TK), TK), :]
            return (_dot(qa, kk[:, 0:LANE], ((1,), (1,))) * ATT_SCALE_LOG2, _dot(qb, kk[:, LANE:2 * LANE], ((1,), (1,))) * ATT_SCALE_LOG2)

        def update(kb, sa, sb, stats):
            ma, la, mb, lb, acc = stats
            vv = v_ref[pl.ds(pl.multiple_of(kb * TK, TK), TK), :]
            na = jnp.maximum(ma, jnp.max(sa, axis=1, keepdims=True))
            nb = jnp.maximum(mb, jnp.max(sb, axis=1, keepdims=True))
            pa, pb = jnp.exp2(sa - na), jnp.exp2(sb - nb)
            fa, fb = jnp.exp2(ma - na), jnp.exp2(mb - nb)
            la = fa * la + jnp.sum(pa, axis=1, keepdims=True)
            lb = fb * lb + jnp.sum(pb, axis=1, keepdims=True)
            acc = (acc * jnp.where(lo, fa, fb) + _dot(pa, jnp.where(lo_k, vv, 0), ((1,), (0,)))
                   + _dot(pb, jnp.where(lo_k, 0, vv), ((1,), (0,))))
            return na, la, nb, lb, acc

        def step(kb, carry):
            sa, sb = carry[:2]
            nxt = scores(kb + 1)
            return nxt + update(kb, sa, sb, carry[2:])

        neg = jnp.full((TQ, 1), NEG, F32)
        zero = jnp.zeros((TQ, 1), F32)
        n_full = i * (TQ // TK)
        carry = lax.fori_loop(0, n_full, step, scores(0) + (neg, zero, neg, zero, jnp.zeros((TQ, LANE), F32)))
        s, stats = carry[:2], carry[2:]
        for d in range(TQ // TK):
            nxt = scores(n_full + d + 1) if d + 1 < TQ // TK else None
            sa, sb = (jnp.where(row_minus_col >= d * TK, t, NEG) for t in s)
            stats = update(n_full + d, sa, sb, stats)
            s = nxt
        ma, la, mb, lb, acc = stats
        o_ref[...] = acc / jnp.where(lo, la, lb)
        lse_ref[...] = jnp.where(lo, ma + jnp.log2(la), mb + jnp.log2(lb)) * LN2

    return pl.pallas_call(
        body, name="attn_fwd", grid=(NPAIR, S // TQ),
        in_specs=[pl.BlockSpec((TQ, 2 * LANE), lambda j, i: (i, j)), pl.BlockSpec((S, 2 * LANE), lambda j, i: (0, j)),
                  pl.BlockSpec((S, LANE), lambda j, i: (0, j))],
        out_specs=[pl.BlockSpec((TQ, LANE), lambda j, i: (i, j)), pl.BlockSpec((None, TQ, LANE), lambda j, i: (j, i, 0))],
        out_shape=[jax.ShapeDtypeStruct((S, H * VDIM), F32), jax.ShapeDtypeStruct((NPAIR, S, LANE), F32)],
        compiler_params=pltpu.CompilerParams(dimension_semantics=("parallel", "parallel")),
    )(qc, kc, v)


def _attn_rows(lse, o, do):
    def body(lse_ref, o_ref, do_ref, r_ref):
        lt = lse_ref[...].T * (1.0 / LN2)
        tt = (o_ref[...] * do_ref[...]).T
        r_ref[...] = jnp.zeros_like(r_ref)
        r_ref[0:1, :] = lt[0:1, :]
        r_ref[1:2, :] = lt[VDIM:VDIM + 1, :]
        r_ref[2:3, :] = jnp.sum(tt[0:VDIM, :], axis=0, keepdims=True)
        r_ref[3:4, :] = jnp.sum(tt[VDIM:LANE, :], axis=0, keepdims=True)

    tile = pl.BlockSpec((S, LANE), lambda j: (0, j))
    return pl.pallas_call(
        body, name="attn_rows", grid=(NPAIR,), in_specs=[pl.BlockSpec((None, S, LANE), lambda j: (j, 0, 0)), tile, tile],
        out_specs=pl.BlockSpec((None, 8, S), lambda j: (j, 0, 0)), out_shape=jax.ShapeDtypeStruct((NPAIR, 8, S), F32),
    )(lse, o, do)


def _attn_bwd(qc, kc, kct, v, do, rows):
    nq = S // TQ

    def body(q_ref, k_ref, kt_ref, v_ref, do_ref, r_ref, dqt_ref, dk_ref, dv_ref):
        kb = pl.program_id(1)

        @pl.when(kb == 0)
        def _():
            dqt_ref[...] = jnp.zeros_like(dqt_ref)

        lo = lax.broadcasted_iota(jnp.int32, (TK, LANE), 1) < VDIM
        q_minus_k = lax.broadcasted_iota(jnp.int32, (TK, TQ), 1) - lax.broadcasted_iota(jnp.int32, (TK, TQ), 0)
        vv = v_ref[...]
        kk = k_ref[...]

        def step(qi, carry):
            off = pl.multiple_of(qi * TQ, TQ)
            qq = q_ref[pl.ds(off, TQ), :]
            dd = do_ref[pl.ds(off, TQ), :].astype(BF16)
            rr = r_ref[:, pl.ds(off, TQ)]
            keep = q_minus_k >= (kb - qi) * TQ
            out = []
            for x in range(2):
                sel = lo if x == 0 else jnp.logical_not(lo)
                kx, qx = kk[:, x * LANE:(x + 1) * LANE], qq[:, x * LANE:(x + 1) * LANE]
                st = jnp.where(keep, _dot(kx, qx, ((1,), (1,))) * ATT_SCALE_LOG2, NEG)
                pt = jnp.exp2(st - rr[x:x + 1, :])
                dpt = _dot(jnp.where(sel, vv, 0), dd, ((1,), (1,)))
                dst = (pt * (dpt - rr[2 + x:3 + x, :]) * ATT_SCALE).astype(BF16)
                out.append(carry[x] + _dot(dst, qx, ((1,), (0,))))
                out.append(_dot(pt, jnp.where(sel, dd, 0), ((1,), (0,))))
                dqt_ref[x * LANE:(x + 1) * LANE, pl.ds(off, TQ)] += _dot(kt_ref[x * LANE:(x + 1) * LANE, :], dst, ((1,), (0,)))
            return out[0], out[2], carry[2] + out[1] + out[3]

        z = jnp.zeros((TK, LANE), F32)
        dka, dkb, dv = lax.fori_loop(kb, nq, step, (z, z, z))
        dk_ref[:, 0:LANE] = dka
        dk_ref[:, LANE:2 * LANE] = dkb
        dv_ref[...] = dv.astype(BF16)

    return pl.pallas_call(
        body, name="attn_bwd", grid=(NPAIR, S // TK),
        in_specs=[pl.BlockSpec((S, 2 * LANE), lambda j, k: (0, j)), pl.BlockSpec((TK, 2 * LANE), lambda j, k: (k, j)),
                  pl.BlockSpec((2 * LANE, TK), lambda j, k: (j, k)), pl.BlockSpec((TK, LANE), lambda j, k: (k, j)),
                  pl.BlockSpec((S, LANE), lambda j, k: (0, j)), pl.BlockSpec((None, 8, S), lambda j, k: (j, 0, 0))],
        out_specs=[pl.BlockSpec((2 * LANE, S), lambda j, k: (j, 0)), pl.BlockSpec((TK, 2 * LANE), lambda j, k: (k, j)),
                   pl.BlockSpec((TK, LANE), lambda j, k: (k, j))],
        out_shape=[jax.ShapeDtypeStruct((H * LANE, S), F32), jax.ShapeDtypeStruct((S, H * LANE), F32),
                   jax.ShapeDtypeStruct((S, H * VDIM), BF16)],
        compiler_params=pltpu.CompilerParams(dimension_semantics=("parallel", "arbitrary")),
    )(qc, kc, kct, v, do, rows)


_IN_Z, _IN_XBC, _IN_DT, _IN_Q, _IN_KV, _IN_KR = 0, 1024, 2560, 2576, 2960, 3216


def _prep_in(w_in_t):
    dt = w_in_t.dtype
    w_small_t = jnp.concatenate(
        [w_in_t[_IN_Q:_IN_KV], w_in_t[_IN_KV:_IN_KR], w_in_t[_IN_KR:IN_WIDTH], jnp.zeros((LANE - ROPE, D), dt),
         w_in_t[_IN_DT:_IN_Q], jnp.zeros((LANE - H, D), dt)], axis=0)
    return w_in_t[_IN_Z:_IN_XBC], w_in_t[_IN_XBC:_IN_DT], w_small_t


def _prep_attn(w_qb, w_kvb):
    w_q = jnp.pad(w_qb.reshape(Q_RANK, H, NOPE + ROPE), ((0, 0), (0, 0), (0, LANE - NOPE - ROPE))).reshape(Q_RANK, H * LANE)
    kv3 = w_kvb.reshape(KV_RANK, H, NOPE + VDIM)
    w_k = jnp.pad(kv3[:, :, :NOPE], ((0, 0), (0, 0), (0, LANE - NOPE))).reshape(KV_RANK, H * LANE)
    w_v = kv3[:, :, NOPE:].reshape(KV_RANK, H * VDIM)
    return w_q, w_k, w_v


def _rope_tables(positions):
    inv_freq = 1.0 / (10000.0 ** (jnp.arange(0, ROPE, 2, dtype=F32) / ROPE))
    ang = positions.astype(F32).reshape(S, 1) * inv_freq
    cos, sin = jnp.cos(ang), jnp.sin(ang)
    cos_t = jnp.concatenate([jnp.ones((S, NOPE), F32), cos, cos, jnp.ones((S, LANE - NOPE - ROPE), F32)], axis=1)
    sin_t = jnp.concatenate([jnp.zeros((S, NOPE), F32), -sin, sin, jnp.zeros((S, LANE - NOPE - ROPE), F32)], axis=1)
    return cos_t, sin_t


def _local_step(x, p, positions, target, w_in, fetch, send, sp, started):
    w_z, w_xbc, w_small = _prep_in(w_in.reshape(IN_WIDTH, D))
    cos_t, sin_t = _rope_tables(positions)
    prow = jnp.zeros((8, LANE), F32).at[0, :H].set(sp["dt_bias"][0]).at[1, :H].set(sp["A_log"][0]).at[2, :H].set(sp["D"][0])
    pcol = prow.T

    xb, pb = (x + started).astype(BF16), p.astype(BF16)
    z = _mm([(xb, w_z)], tb=True, name="proj_z")
    xbc = _mm([(xb, w_xbc)], tb=True, name="proj_xbc")
    small = _mm([(xb, w_small)], tb=True, name="proj_small")
    act = _conv_fwd(xbc, sp["conv_w"], sp["conv_b"])
    dt_t = small[:, SM_DT:SM_DT + LANE].T
    y, states = _ssd_fwd(act, small, dt_t, prow, pcol)
    y_ssd = _gate_norm_fwd(y, z, sp["ssd_norm"])
    gl = fetch("attn", y_ssd)
    w_q, w_k, w_v = _prep_attn(_from_cols(gl["w_qb"]), _from_cols(gl["w_kvb"]))
    qn, kvn, qcat, kcat, kcat_t, v = _qkv_fwd(small, w_q, w_k, w_v, sp["q_norm"], sp["kv_norm"], cos_t, sin_t)
    o, lse = _attn_fwd(qcat, kcat, v)
    y_mla = _rms_fwd(o, sp["out_norm"], name="out_norm_fwd")
    w_out = fetch("out", y_mla)["w_out"]
    w_out_s = w_out[:NCHIP // 2].reshape(SSD_INNER, D)
    w_out_m = w_out[NCHIP // 2:].reshape(SSD_INNER, D)
    mix = _mm([(y_ssd, w_out_s), (y_mla, w_out_m)], name="out_proj")
    h1, h1b = _ln_fwd(x, mix, sp["ln_mix_g"], sp["ln_mix_b"])
    gl = fetch("ffn", h1b)
    w_pg, w_pp = gl["w_pg"].reshape(D, D), _from_cols(gl["w_pp"])
    w_gate, w_up, w_down = gl["w_gate"], gl["w_up"], gl["w_down"]
    gate, up, actf = _ffn_hidden_fwd(h1b, w_gate, w_up)
    ffn = _mm([(actf, w_down)], chunk="sum", name="ffn_down")
    pg = _mm([(h1b, w_pg)], name="ple_gate")
    pp = _mm([(pb, w_pp)], name="ple_proj")
    dpre2, dpre2b, dpg, dpp, dg2, db2, loss_row = _final_fwd_bwd(h1, ffn, pg, pp, target, sp["ln_ffn_g"], sp["ln_ffn_b"])

    g = {"ln_ffn_g": dg2, "ln_ffn_b": db2}
    g["w_pp"] = _to_cols(_mm([(pb, dpp)], ta=True, out_dtype=BF16, name="d_w_ple_proj"))
    g["w_pg"] = _mm([(h1b, dpg)], ta=True, out_dtype=BF16, name="d_w_ple_gate").reshape(NCHIP, D // NCHIP, D)
    g["w_down"] = _mm([(actf, dpre2b)], ta=True, chunk="out", out_dtype=BF16, name="d_w_down")
    dgate, dup = _ffn_hidden_bwd(dpre2b, w_down, gate, up)
    g["w_gate"] = _mm([(dgate, h1b)], ta=True, chunk="out", out_dtype=BF16, name="d_w_gate")
    g["w_up"] = _mm([(dup, h1b)], ta=True, chunk="out", out_dtype=BF16, name="d_w_up")
    sent = send("ffn", {name: g.pop(name) for name in dict(ASYNC_GROUPS)["ffn"]})
    dh1 = _mm([(dpg, w_pg)], tb=True, add=dpre2, add_scale=ALPHA, name="d_h1_ple")
    dh1 = _mm([(dgate, w_gate), (dup, w_up)], chunk="sum", add=dh1, name="d_h1")
    dpre1, dpre1b, g["ln_mix_g"], g["ln_mix_b"] = _ln_bwd(x, mix, sp["ln_mix_g"] + sent, dh1)
    dy_ssd = _mm([(dpre1b, w_out_s)], tb=True, name="d_y_ssd")
    dy_mla = _mm([(dpre1b, w_out_m)], tb=True, name="d_y_mla")
    dw_out = jnp.concatenate([_mm([(y_ssd, dpre1b)], ta=True, out_dtype=BF16, name="d_w_out_s"),
                              _mm([(y_mla, dpre1b)], ta=True, out_dtype=BF16, name="d_w_out_m")], axis=0)
    sent = send("out", {"w_out": dw_out.reshape(NCHIP, 2 * SSD_INNER // NCHIP, D)})
    do, g["out_norm"] = _rms_bwd(o, sp["out_norm"] + sent, dy_mla, name="out_norm_bwd")
    dqt, dk, dv = _attn_bwd(qcat, kcat, kcat_t, v, do, _attn_rows(lse, o, do))
    dlatent, dqlin, dkb, g["q_norm"], g["kv_norm"] = _qkv_bwd(dqt, dk, dv, small, w_q, w_k, w_v, sp["q_norm"], sp["kv_norm"], cos_t, sin_t)
    dw_q = _mm([(qn, dqlin)], ta=True, out_dtype=BF16, name="d_w_q")
    dw_k = _mm([(kvn, dkb)], ta=True, out_dtype=BF16, name="d_w_k")
    dw_v = _mm([(kvn, dv)], ta=True, out_dtype=BF16, name="d_w_v")
    dw_qb = _to_cols(dw_q.reshape(Q_RANK, H, LANE)[:, :, :NOPE + ROPE].reshape(Q_RANK, H * (NOPE + ROPE)))
    dw_kvb = _to_cols(jnp.concatenate([dw_k.reshape(KV_RANK, H, LANE)[:, :, :NOPE], dw_v.reshape(KV_RANK, H, VDIM)],
                                       axis=2).reshape(KV_RANK, H * (NOPE + VDIM)))
    sent = send("attn", {"w_qb": dw_qb, "w_kvb": dw_kvb})
    dy, dz, g["ssd_norm"] = _gate_norm_bwd(y, z, sp["ssd_norm"] + sent, dy_ssd)
    dact, ddt, dprow = _ssd_bwd(act, small, dt_t, prow, pcol, states, dy)
    g["dt_bias"], g["A_log"], g["D"] = dprow[0:1, :H], dprow[1:2, :H], dprow[2:3, :H]
    dxbc, g["conv_w"], g["conv_b"] = _conv_bwd(xbc, sp["conv_w"], sp["conv_b"], dact)
    dsmall = jnp.concatenate([dlatent, ddt.astype(BF16)], axis=1)
    grad_x = _mm([(dz, w_z), (dxbc, w_xbc), (dsmall, w_small)], add=dpre1, add_scale=ALPHA, name="d_x")
    dw_small = _mm([(dsmall, xb)], ta=True, out_dtype=BF16, name="d_w_small")
    dw_in = jnp.concatenate(
        [_mm([(dz, xb)], ta=True, out_dtype=BF16, name="d_w_z"), _mm([(dxbc, xb)], ta=True, out_dtype=BF16, name="d_w_xbc"),
         dw_small[SM_DT:SM_DT + H], dw_small[SM_Q:SM_Q + Q_RANK], dw_small[SM_KV:SM_KV + KV_RANK], dw_small[SM_KR:SM_KR + ROPE]],
        axis=0).reshape(NCHIP, IN_WIDTH // NCHIP * D // LANE, LANE)
    return loss_row, grad_x, dw_in, g


MESH = pl.DeviceIdType.MESH
BIG = (("w_in", (D, IN_WIDTH), 1), ("w_qb", (Q_RANK, H * (NOPE + ROPE)), 1), ("w_kvb", (KV_RANK, H * (NOPE + VDIM)), 1),
       ("w_out", (2 * SSD_INNER, D), 0), ("w_gate", (D, D_FF), 1), ("w_up", (D, D_FF), 1), ("w_down", (D_FF, D), 0),
       ("w_pg", (D, D), 0), ("w_pp", (PLE, D), 1))
CONV_SHARD = SSD_XBC // NCHIP
BF16_ROWS = 16


def _from_cols(stack):
    return jnp.concatenate([stack[k] for k in range(NCHIP)], axis=1)


def _to_cols(full):
    r, c4 = full.shape
    return full.reshape(r, NCHIP, c4 // NCHIP).transpose(1, 0, 2)


def _coords():
    return lax.axis_index("x"), lax.axis_index("y"), lax.axis_index("c")


def _peers():
    x, y, c = _coords()
    return 2 * x + y, c, [(1 - x, y), (x, 1 - y), (1 - x, 1 - y)], (x, y, 1 - c)


def _half_axis(shape):
    return 0 if shape[-2] % (2 * BF16_ROWS) == 0 else 1


def _half_shape(shape):
    r, c = shape[-2:]
    return (r // 2, c) if _half_axis(shape) == 0 else (r, c // 2)


def _half(core, shape):
    r, c = shape[-2:]
    if _half_axis(shape) == 0:
        return pl.ds(pl.multiple_of(core * (r // 2), BF16_ROWS), r // 2), slice(None)
    return slice(None), pl.ds(pl.multiple_of(core * (c // 2), LANE), c // 2)


def _gather_weights(shards):
    n_arr = len(shards)
    per = 2 * (NCHIP - 1)

    def body(*refs):
        ins, outs = refs[:n_arr], refs[n_arr:2 * n_arr]
        send_sems, recv_sems, local_sems = refs[2 * n_arr:]
        k, c, chips, sibling = _peers()

        def copy(idx, src, dst, to):
            return pltpu.make_async_remote_copy(src_ref=src, dst_ref=dst, send_sem=send_sems.at[idx], recv_sem=recv_sems.at[idx],
                                                device_id=to, device_id_type=MESH)

        def part(a, chip, core):
            return outs[a].at[chip, *_half(core, shards[a].shape)]

        mine = [pltpu.make_async_copy(ins[a], outs[a].at[k], local_sems.at[a]) for a in range(n_arr)]
        for cp in mine:
            cp.start()
        sends = []
        for a in range(n_arr):
            for j, (cx, cy) in enumerate(chips):
                sends.append(copy(per * a + j, ins[a].at[*_half(c, shards[a].shape)], part(a, k, c), (cx, cy, c)))
                sends[-1].start()
        for j, (cx, cy) in enumerate(chips):
            for a in range(n_arr):
                landed = part(a, 2 * cx + cy, c)
                copy(per * a + j, landed, landed, (cx, cy, c)).wait_recv()
                sends.append(copy(per * a + NCHIP - 1 + j, landed, landed, sibling))
                sends[-1].start()
        for j, (cx, cy) in enumerate(chips):
            for a in range(n_arr):
                other = part(a, 2 * cx + cy, 1 - c)
                copy(per * a + NCHIP - 1 + j, other, other, sibling).wait_recv()
        for cp in sends:
            cp.wait_send()
        for cp in mine:
            cp.wait()

    any_spec = pl.BlockSpec(memory_space=pl.ANY)
    return pl.pallas_call(
        body, name="gather_weights", in_specs=[any_spec] * n_arr, out_specs=[any_spec] * n_arr,
        out_shape=[jax.ShapeDtypeStruct((NCHIP,) + s.shape, s.dtype) for s in shards],
        scratch_shapes=[pltpu.SemaphoreType.DMA((per * n_arr,)), pltpu.SemaphoreType.DMA((per * n_arr,)),
                        pltpu.SemaphoreType.DMA((n_arr,))],
    )(*shards)


ASYNC_GROUPS = (("attn", ("w_qb", "w_kvb")), ("out", ("w_out",)), ("ffn", ("w_gate", "w_up", "w_down", "w_pg", "w_pp")))
TRANSPOSED = ("w_in", "w_gate", "w_up")
ROW_MAJOR = ("w_in",)
HBM_SPEC = pl.BlockSpec(memory_space=pltpu.HBM)
SEM_SPEC = pl.BlockSpec(memory_space=pltpu.SEMAPHORE)
IN_FLIGHT = pltpu.SideEffectType.DATAFLOW_SIDE_EFFECTING


def _in_hbm(a):
    return pltpu.with_memory_space_constraint(a, pltpu.HBM)


def _hbm_like(arrs, lead=()):
    return [pltpu.HBM(lead + a.shape, a.dtype) for a in arrs]


def _split_start(name, srcs, lands, after, n_sem, start):
    n = len(srcs)
    order = [] if after is None else [after]

    def body(*refs):
        src_refs, land_refs = refs[:n], refs[n:2 * n]
        send_sems, recv_sems = refs[2 * n + len(order)], refs[2 * n + len(order) + 1]
        token = refs[-1]

        def copy(send_idx, recv_idx, src, dst, to):
            return pltpu.make_async_remote_copy(src_ref=src, dst_ref=dst, send_sem=send_sems.at[send_idx],
                                                recv_sem=recv_sems.at[recv_idx], device_id=to, device_id_type=MESH)

        for cp in start(src_refs, land_refs, copy):
            cp.start()
        token[...] = jnp.zeros_like(token)

    sem = pltpu.SemaphoreType.DMA((n_sem,))
    outs = pl.pallas_call(
        body, name=name, in_specs=[HBM_SPEC] * (2 * n) + [pl.BlockSpec(memory_space=pl.ANY)] * len(order),
        out_specs=[SEM_SPEC, SEM_SPEC] + [HBM_SPEC] * (2 * n) + [pl.BlockSpec(memory_space=pltpu.VMEM)],
        out_shape=[sem, sem] + _hbm_like(srcs) + _hbm_like(lands) + [jax.ShapeDtypeStruct((8, LANE), F32)],
        input_output_aliases={i: 2 + i for i in range(2 * n)},
        compiler_params=pltpu.CompilerParams(has_side_effects=IN_FLIGHT),
    )(*[_in_hbm(a) for a in srcs], *[_in_hbm(a) for a in lands], *order)
    return (outs[0], outs[1], outs[2:2 + n], outs[2 + n:2 + 2 * n]), outs[-1]


def _split_wait(name, send_sems, recv_sems, srcs, lands, after, waits):
    n = len(srcs)

    def body(*refs):
        src_refs, land_refs = refs[:n], refs[n:2 * n]
        send_ref, recv_ref = refs[2 * n], refs[2 * n + 1]

        def copy(send_idx, recv_idx, src, dst, to):
            return pltpu.make_async_remote_copy(src_ref=src, dst_ref=dst, send_sem=send_ref.at[send_idx],
                                                recv_sem=recv_ref.at[recv_idx], device_id=to, device_id_type=MESH)

        for cp in waits(src_refs, land_refs, copy):
            cp.wait_send()
            cp.wait_recv()

    outs = pl.pallas_call(
        body, name=name, in_specs=[HBM_SPEC] * (2 * n) + [SEM_SPEC, SEM_SPEC, pl.BlockSpec(memory_space=pl.ANY)],
        out_specs=[HBM_SPEC] * (2 * n), out_shape=_hbm_like(srcs) + _hbm_like(lands),
        input_output_aliases={i: i for i in range(2 * n)},
        compiler_params=pltpu.CompilerParams(has_side_effects=IN_FLIGHT),
    )(*srcs, *lands, send_sems, recv_sems, after)
    return outs[:n], outs[n:]


GATHER_LATE_SEMS = 2 * (NCHIP - 1)


def _gather_async_start(tag, shards, after):
    def start(srcs, lands, copy):
        k, c, chips, _ = _peers()
        out = []
        for a, (src, dst) in enumerate(zip(srcs, lands)):
            for j, (cx, cy) in enumerate(chips):
                for core in range(2):
                    out.append(copy(GATHER_LATE_SEMS * a + 2 * j + core, GATHER_LATE_SEMS * a + 2 * j + c,
                                    src.at[*_half(c, src.shape)], dst.at[k, *_half(c, src.shape)], (cx, cy, core)))
        return out

    chip = 2 * lax.axis_index("x") + lax.axis_index("y")
    lands = [lax.dynamic_update_slice(lax.empty((NCHIP,) + s.shape, s.dtype), s[None], (chip, 0, 0)) for s in shards]
    return _split_start("gather_%s_start" % tag, shards, lands, after, GATHER_LATE_SEMS * len(shards), start)


def _gather_async_wait(tag, send_sems, recv_sems, shards, lands, after):
    def waits(srcs, lands_, copy):
        _, c, chips, _ = _peers()
        out = []
        for a, (src, dst) in enumerate(zip(srcs, lands_)):
            for j, (cx, cy) in enumerate(chips):
                for core in range(2):
                    idx = GATHER_LATE_SEMS * a + 2 * j + core
                    out.append(copy(idx, idx, src.at[*_half(c, src.shape)], dst.at[2 * cx + cy, *_half(core, src.shape)], (cx, cy, core)))
        return out

    return _split_wait("gather_%s_wait" % tag, send_sems, recv_sems, shards, lands, after, waits)[1]


def _other_devices():
    x, y, c = _coords()
    out = []
    for d in range(1, NDEV):
        tx, ty, tc = x ^ (d >> 2), y ^ ((d >> 1) & 1), c ^ (d & 1)
        out.append((d, (tx, ty, tc), 2 * tx + ty, 4 * tx + 2 * ty + tc))
    return out


def _reduce_async_start(tag, stacks, after):
    def start(srcs, lands, copy):
        x, y, c = _coords()
        me = 4 * x + 2 * y + c
        return [copy((NDEV - 1) * a + d - 1, (NDEV - 1) * a + d - 1, src.at[chip, *_half(to[2], src.shape)], dst.at[me], to)
                for a, (src, dst) in enumerate(zip(srcs, lands)) for d, to, chip, _ in _other_devices()]

    x, y, c = _coords()
    lands = []
    for s in stacks:
        hr, hc = _half_shape(s.shape)
        at = (c * hr, 0) if _half_axis(s.shape) == 0 else (0, c * hc)
        own = lax.dynamic_slice(s, (2 * x + y,) + at, (1, hr, hc))
        lands.append(lax.dynamic_update_slice(lax.empty((NDEV, hr, hc), s.dtype), own, (4 * x + 2 * y + c, 0, 0)))
    return _split_start("reduce_%s_start" % tag, stacks, lands, after, (NDEV - 1) * len(stacks), start)


def _reduce_async_wait(tag, send_sems, recv_sems, stacks, lands, after):
    def waits(srcs, lands_, copy):
        return [copy((NDEV - 1) * a + d - 1, (NDEV - 1) * a + d - 1, src.at[chip, *_half(to[2], src.shape)], dst.at[pos], to)
                for a, (src, dst) in enumerate(zip(srcs, lands_)) for d, to, chip, pos in _other_devices()]

    return _split_wait("reduce_%s_wait" % tag, send_sems, recv_sems, stacks, lands, after, waits)[1]


def _reduce_finish(tag, arrived, dims):
    n_arr = len(arrived)

    def body(*refs):
        lands, fin = refs[:n_arr], refs[n_arr:2 * n_arr]
        send_sems, recv_sems = refs[2 * n_arr:]
        _, c, _, sibling = _peers()
        sends = []
        for a in range(n_arr):
            mine = fin[a].at[*_half(c, dims[a])]

            def device_sum(vs, vf, a=a, mine=mine):
                pltpu.sync_copy(lands[a], vs)
                acc = vs[0].astype(F32)
                for i in range(1, NDEV):
                    acc = acc + vs[i].astype(F32)
                vf[...] = acc
                pltpu.sync_copy(vf, mine)

            pl.run_scoped(device_sum, pltpu.VMEM((NDEV,) + _half_shape(dims[a]), BF16), pltpu.VMEM(_half_shape(dims[a]), F32))
            sends.append(pltpu.make_async_remote_copy(src_ref=mine, dst_ref=mine, send_sem=send_sems.at[a], recv_sem=recv_sems.at[a],
                                                      device_id=sibling, device_id_type=MESH))
            sends[-1].start()
        for a in range(n_arr):
            other = fin[a].at[*_half(1 - c, dims[a])]
            pltpu.make_async_remote_copy(src_ref=other, dst_ref=other, send_sem=send_sems.at[a], recv_sem=recv_sems.at[a],
                                         device_id=sibling, device_id_type=MESH).wait_recv()
        for cp in sends:
            cp.wait_send()

    any_spec = pl.BlockSpec(memory_space=pl.ANY)
    return pl.pallas_call(
        body, name="reduce_%s_finish" % tag, in_specs=[any_spec] * n_arr, out_specs=[any_spec] * n_arr,
        out_shape=[jax.ShapeDtypeStruct(d, F32) for d in dims],
        scratch_shapes=[pltpu.SemaphoreType.DMA((n_arr,)), pltpu.SemaphoreType.DMA((n_arr,))],
    )(*arrived)


SMALL = (("conv_w", SSD_K * SSD_XBC), ("conv_b", SSD_XBC), ("dt_bias", H), ("A_log", H), ("D", H), ("ssd_norm", SSD_INNER),
         ("q_norm", Q_RANK), ("kv_norm", KV_RANK), ("out_norm", SSD_INNER), ("ln_mix_g", D), ("ln_mix_b", D),
         ("ln_ffn_g", D), ("ln_ffn_b", D))
SMALL_ROWS = 120
NDEV = 8


def _allreduce_small(sv):
    def body(sv_ref, out_ref, slots, send_sems, recv_sems):
        x, y, c = _coords()
        me = 4 * x + 2 * y + c
        slots[me] = sv_ref[...]
        copies = []
        for d in range(1, NDEV):
            to = (x ^ (d >> 2), y ^ ((d >> 1) & 1), c ^ (d & 1))
            copies.append(pltpu.make_async_remote_copy(src_ref=sv_ref, dst_ref=slots.at[me], send_sem=send_sems.at[d - 1],
                                                       recv_sem=recv_sems.at[d - 1], device_id=to, device_id_type=MESH))
            copies[-1].start()
        for cp in copies:
            cp.wait_recv()
        for cp in copies:
            cp.wait_send()
        acc = slots[0]
        for i in range(1, NDEV):
            acc = acc + slots[i]
        out_ref[...] = acc

    vm = pl.BlockSpec(memory_space=pltpu.VMEM)
    return pl.pallas_call(
        body, name="allreduce_small", in_specs=[vm], out_specs=vm, out_shape=jax.ShapeDtypeStruct((SMALL_ROWS, LANE), F32),
        scratch_shapes=[pltpu.VMEM((NDEV, SMALL_ROWS, LANE), F32), pltpu.SemaphoreType.DMA((NDEV - 1,)),
                        pltpu.SemaphoreType.DMA((NDEV - 1,))],
    )(sv)


def _adamw_math(w, g, m, v):
    m2 = ADAM_B1 * m + (1.0 - ADAM_B1) * g
    v2 = ADAM_B2 * v + (1.0 - ADAM_B2) * (g * g)
    m_hat = m2 / (1.0 - ADAM_B1 ** ADAM_STEP)
    v_hat = v2 / (1.0 - ADAM_B2 ** ADAM_STEP)
    return -ADAM_LR * (m_hat / (jnp.sqrt(v_hat) + ADAM_EPS) + ADAM_WD * w), m2, v2


ADAM_BLOCK_BYTES = 2 * 1024 * 1024


def _adamw_big(w, g, m, v, *, name):
    r, c = w.shape

    def body(w_ref, g_ref, m_ref, v_ref, d_ref, m2_ref, v2_ref):
        d_ref[...], m2_ref[...], v2_ref[...] = _adamw_math(w_ref[...], g_ref[...], m_ref[...], v_ref[...])

    tr = max(t for t in range(8, r + 1, 8) if r % t == 0 and t * c * 4 <= ADAM_BLOCK_BYTES)
    steps, spec = r // tr, pl.BlockSpec((tr, c), lambda i: (i, 0))
    return pl.pallas_call(body, name=name, grid=(steps,), in_specs=[spec] * 4, out_specs=[spec] * 3,
                          out_shape=[jax.ShapeDtypeStruct((r, c), F32)] * 3)(w, g, m, v)


def _adamw_small(ws, gs, ms, vs):
    n = len(ws)

    def body(*refs):
        for i in range(n):
            w_ref, g_ref, m_ref, v_ref = (refs[j * n + i] for j in range(4))
            d_ref, m2_ref, v2_ref = (refs[(4 + j) * n + i] for j in range(3))
            d_ref[...], m2_ref[...], v2_ref[...] = _adamw_math(w_ref[...], g_ref[...], m_ref[...], v_ref[...])

    vm = pl.BlockSpec(memory_space=pltpu.VMEM)
    shapes = [jax.ShapeDtypeStruct(w.shape, F32) for w in ws]
    outs = pl.pallas_call(body, name="adamw_small", in_specs=[vm] * (4 * n), out_specs=[vm] * (3 * n), out_shape=shapes * 3)(
        *ws, *gs, *ms, *vs)
    return outs[:n], outs[n:2 * n], outs[2 * n:]


_SMALL_ARG = {"conv_w": "ssd_conv_w", "conv_b": "ssd_conv_b", "dt_bias": "ssd_dt_bias", "A_log": "ssd_A_log", "D": "ssd_D",
              "ssd_norm": "ssd_norm_w", "q_norm": "mla_q_norm_w", "kv_norm": "mla_kv_norm_w", "out_norm": "mla_out_norm_w",
              "ln_mix_g": "ln_mix_g", "ln_mix_b": "ln_mix_b", "ln_ffn_g": "ln_ffn_g", "ln_ffn_b": "ln_ffn_b"}
_BIG_ARG = {"w_in": "w_in", "w_qb": "mla_w_q_b", "w_kvb": "mla_w_kv_b", "w_out": "w_out", "w_gate": "w_ffn_gate",
            "w_up": "w_ffn_up", "w_down": "w_ffn_down", "w_pg": "w_ple_gate", "w_pp": "w_ple_proj"}
_WEIGHT_ORDER = ("w_in", "ssd_conv_w", "ssd_conv_b", "ssd_dt_bias", "ssd_A_log", "ssd_D", "ssd_norm_w", "mla_q_norm_w", "mla_w_q_b",
                 "mla_kv_norm_w", "mla_w_kv_b", "mla_out_norm_w", "w_out", "ln_mix_g", "ln_mix_b", "w_ffn_gate", "w_ffn_up",
                 "w_ffn_down", "w_ple_gate", "w_ple_proj", "ln_ffn_g", "ln_ffn_b")


def _rows128(a):
    flat = a.reshape(-1)
    return jnp.pad(flat, (0, -flat.shape[0] % LANE)).reshape(-1, LANE)


def kernel(x, p, positions, w_in, ssd_conv_w, ssd_conv_b, ssd_dt_bias, ssd_A_log, ssd_D, ssd_norm_w, mla_q_norm_w, mla_w_q_b, mla_kv_norm_w, mla_w_kv_b, mla_out_norm_w, w_out, ln_mix_g, ln_mix_b, w_ffn_gate, w_ffn_up, w_ffn_down, w_ple_gate, w_ple_proj, ln_ffn_g, ln_ffn_b, loss_target, m_w_in, m_ssd_conv_w, m_ssd_conv_b, m_ssd_dt_bias, m_ssd_A_log, m_ssd_D, m_ssd_norm_w, m_mla_q_norm_w, m_mla_w_q_b, m_mla_kv_norm_w, m_mla_w_kv_b, m_mla_out_norm_w, m_w_out, m_ln_mix_g, m_ln_mix_b, m_w_ffn_gate, m_w_ffn_up, m_w_ffn_down, m_w_ple_gate, m_w_ple_proj, m_ln_ffn_g, m_ln_ffn_b, v_w_in, v_ssd_conv_w, v_ssd_conv_b, v_ssd_dt_bias, v_ssd_A_log, v_ssd_D, v_ssd_norm_w, v_mla_q_norm_w, v_mla_w_q_b, v_mla_kv_norm_w, v_mla_w_kv_b, v_mla_out_norm_w, v_w_out, v_ln_mix_g, v_ln_mix_b, v_w_ffn_gate, v_w_ffn_up, v_w_ffn_down, v_w_ple_gate, v_w_ple_proj, v_ln_ffn_g, v_ln_ffn_b):
    given = dict(locals())
    chip = 2 * lax.axis_index("x") + lax.axis_index("y")

    def local(name, prefix=""):
        a = given[prefix + _BIG_ARG[name]][0]
        return a.T if name in TRANSPOSED else a

    def updated(name, prefix=""):
        if name in ROW_MAJOR:
            _, c, r = given[prefix + _BIG_ARG[name]].shape
            return given[prefix + _BIG_ARG[name]].reshape(c // LANE, LANE, r).transpose(2, 0, 1).reshape(-1, LANE)
        return local(name, prefix)

    def global_layout(name, arr):
        if name in ROW_MAJOR:
            r, c = local(name).shape
            return arr.reshape(r, c // LANE, LANE).transpose(1, 2, 0).reshape(1, c, r)
        return (arr.T if name in TRANSPOSED else arr)[None]

    conv_bits = lax.bitcast_convert_type(ssd_conv_w[0], BF16).reshape(SSD_K, 2 * CONV_SHARD)
    w_in_all, conv_all = _gather_weights([local("w_in").astype(BF16), jnp.pad(conv_bits, ((0, BF16_ROWS - SSD_K), (0, 0)))])
    sp = {k: given[a] for k, a in _SMALL_ARG.items() if k != "conv_w"}
    sp["conv_w"] = _from_cols(lax.bitcast_convert_type(conv_all[:, :SSD_K].reshape(NCHIP, SSD_K, CONV_SHARD, 2), F32))
    gathering, tie = {}, w_in_all
    for group, names in ASYNC_GROUPS:
        gathering[group], tie = _gather_async_start(group, [local(name).astype(BF16) for name in names], tie)

    def fetch(group, after):
        return dict(zip(dict(ASYNC_GROUPS)[group], _gather_async_wait(group, *gathering[group], after)))

    reducing = {}

    def send(group, grads):
        reducing[group], sent = _reduce_async_start(group, [grads[name] for name in dict(ASYNC_GROUPS)[group]], None)
        return sent[0, 0]

    loss_row, grad_x, dw_in, g = _local_step(x[0], p[0, 0], positions[0], loss_target[0], w_in_all, fetch, send, sp, tie[0, 0])

    reducing["in"], tie = _reduce_async_start("in", [dw_in], grad_x)
    gbig = {}
    for group, names in reversed(ASYNC_GROUPS):
        arrived = _reduce_async_wait(group, *reducing[group], tie)
        gbig.update(zip(names, _reduce_finish(group, arrived, [local(name).shape for name in names])))
    small_in = jnp.concatenate([_rows128(g[name]) for name, _ in SMALL] + [loss_row], axis=0)
    small_sum = _allreduce_small(jnp.pad(small_in, ((0, SMALL_ROWS - small_in.shape[0]), (0, 0))))
    gsmall, row = {}, 0
    for name, size in SMALL:
        nrow = -(-size // LANE)
        gsmall[name] = small_sum[row:row + nrow].reshape(-1)[:size]
        row += nrow
    loss = small_sum[row, 0]

    grads = {_BIG_ARG[name]: global_layout(name, arr) for name, arr in gbig.items()}
    for name, _ in SMALL:
        if name == "conv_w":
            full_g = gsmall[name].reshape(SSD_K, SSD_XBC)
            grads["ssd_conv_w"] = lax.dynamic_slice(full_g, (0, chip * CONV_SHARD), (SSD_K, CONV_SHARD))[None]
        else:
            grads[_SMALL_ARG[name]] = gsmall[name].reshape(given[_SMALL_ARG[name]].shape)

    delta, new_m, new_v = {}, {}, {}

    def update_matrix(name, grad):
        a = _BIG_ARG[name]
        d, m2, v2 = _adamw_big(updated(name), grad, updated(name, "m_"), updated(name, "v_"), name="adamw_" + a)
        delta[a], new_m[a], new_v[a] = (global_layout(name, t) for t in (d, m2, v2))
        return d

    for name, grad in gbig.items():
        last = update_matrix(name, grad)
    g_in = _reduce_finish("in", _reduce_async_wait("in", *reducing["in"], last), [updated("w_in").shape])[0]
    grads["w_in"] = global_layout("w_in", g_in)
    update_matrix("w_in", g_in)
    small_names = [_SMALL_ARG[name] for name, _ in SMALL]
    two_d = lambda t: t.reshape(t.shape[-2], t.shape[-1])
    ds, ms, vs = _adamw_small([two_d(given[a]) for a in small_names], [two_d(grads[a]) for a in small_names],
                              [two_d(given["m_" + a]) for a in small_names], [two_d(given["v_" + a]) for a in small_names])
    for a, d, m2, v2 in zip(small_names, ds, ms, vs):
        delta[a], new_m[a], new_v[a] = (t.reshape(given[a].shape) for t in (d, m2, v2))

    return (loss, grad_x[None], *[grads[n] for n in _WEIGHT_ORDER], *[delta[n] for n in _WEIGHT_ORDER],
            *[new_m[n] for n in _WEIGHT_ORDER], *[new_v[n] for n in _WEIGHT_ORDER])
```

```python
import functools
import math

import jax
import jax.numpy as jnp
from jax import lax
from jax.experimental import pallas as pl
from jax.experimental.pallas import tpu as pltpu

F32 = jnp.float32
BF16 = jnp.bfloat16

S = 2048
D = 1024
PLE = 256
H = 16
SSD_P = 64
SSD_INNER = 1024
SSD_N = 128
SSD_G = 2
SSD_L = 128
SSD_NC = S // SSD_L
SSD_XBC = 1536
SSD_K = 4
Q_RANK = 384
KV_RANK = 256
NOPE = 64
ROPE = 32
VDIM = 64
D_FF = 2816
IN_WIDTH = 3248
ALPHA = 2.0 ** 0.25
EPS_RMS = 1e-6
EPS_LN = 1e-5
ATT_SCALE = 1.0 / math.sqrt(NOPE + ROPE)
LN2 = math.log(2.0)
ATT_SCALE_LOG2 = ATT_SCALE / LN2
LANE = 128
NCHIP = 4
SMALL_W = 896
SM_Q, SM_KV, SM_KR, SM_DT = 0, 384, 640, 768
NEG = -1e30

ADAM_LR = 0.001
ADAM_B1 = 0.9
ADAM_B2 = 0.999
ADAM_EPS = 1e-08
ADAM_WD = 0.01
ADAM_STEP = 10


def _sigmoid(v):
    return 1.0 / (1.0 + jnp.exp(-v))


MM_VMEM_BUDGET = 36 * 2 ** 20
MM_MAX_ACC = 2048 * 1024


def _mm_tiles(pairs, ta, tb, m, n, out_dtype, has_add):
    def divs(v):
        return [LANE * d for d in range(v // LANE, 0, -1) if (v // LANE) % d == 0] if v % LANE == 0 else [v]

    def cost(tm, tn):
        tot = tm * tn * (jnp.dtype(out_dtype).itemsize + (4 if has_add else 0))
        for a, b in pairs:
            k = a.shape[-2] if ta else a.shape[-1]
            tot += k * (tm * a.dtype.itemsize + tn * b.dtype.itemsize)
        return 2 * tot

    ok = [(tm * tn, tm, tn) for tm in divs(m) for tn in divs(n) if tm * tn <= MM_MAX_ACC and cost(tm, tn) <= MM_VMEM_BUDGET]
    _, tm, tn = max(ok)
    return tm, tn


def _mm(pairs, *, ta=False, tb=False, out_dtype=F32, add=None, add_scale=1.0, chunk=None, name):
    n_pairs = len(pairs)
    a0, b0 = pairs[0]
    m = a0.shape[-1] if ta else a0.shape[-2]
    n = b0.shape[-2] if tb else b0.shape[-1]
    tm, tn = _mm_tiles(pairs, ta, tb, m, n, out_dtype, add is not None)
    dims = (((0 if ta else 1,), (1 if tb else 0,)), ((), ()))
    nk = NCHIP if chunk else 1
    assert chunk != "sum" or out_dtype == F32

    def body(*refs):
        o_ref = refs[-1]
        acc = None
        for i in range(n_pairs):
            a = refs[2 * i][...].astype(BF16)
            b = refs[2 * i + 1][...].astype(BF16)
            part = lax.dot_general(a, b, dims, preferred_element_type=F32)
            acc = part if acc is None else acc + part
        if chunk == "sum":
            k = pl.program_id(2)

            @pl.when(k == 0)
            def _():
                o_ref[...] = acc + add_scale * refs[2 * n_pairs][...] if add is not None else acc

            @pl.when(k > 0)
            def _():
                o_ref[...] += acc
        else:
            if add is not None:
                acc = acc + add_scale * refs[2 * n_pairs][...]
            o_ref[...] = acc.astype(out_dtype)

    def spec(arr, shape, idx2):
        if arr.ndim == 3:
            return pl.BlockSpec((None,) + shape, lambda i, j, k: (k,) + idx2(i, j))
        return pl.BlockSpec(shape, lambda i, j, k: idx2(i, j))

    in_specs, args = [], []
    for a, b in pairs:
        kdim = a.shape[-2] if ta else a.shape[-1]
        in_specs.append(spec(a, (kdim, tm), lambda i, j: (0, i)) if ta else spec(a, (tm, kdim), lambda i, j: (i, 0)))
        in_specs.append(spec(b, (tn, kdim), lambda i, j: (j, 0)) if tb else spec(b, (kdim, tn), lambda i, j: (0, j)))
        args += [a, b]
    if add is not None:
        in_specs.append(pl.BlockSpec((tm, tn), lambda i, j, k: (i, j)))
        args.append(add)
    if chunk == "out":
        out_spec = pl.BlockSpec((None, tm, tn), lambda i, j, k: (k, i, j))
        out_shape = jax.ShapeDtypeStruct((nk, m, n), out_dtype)
    else:
        out_spec = pl.BlockSpec((tm, tn), lambda i, j, k: (i, j))
        out_shape = jax.ShapeDtypeStruct((m, n), out_dtype)
    return pl.pallas_call(
        body, name=name, grid=(m // tm, n // tn, nk), in_specs=in_specs, out_specs=out_spec, out_shape=out_shape,
        compiler_params=pltpu.CompilerParams(dimension_semantics=("parallel", "parallel", "arbitrary")),
    )(*args)


TR = 256


def _row_spec(c):
    return pl.BlockSpec((TR, c), lambda i: (i, 0))


def _vec_spec(c):
    return pl.BlockSpec((1, c), lambda i: (0, 0))


def _acc_rows(ref, val):
    @pl.when(pl.program_id(0) == 0)
    def _():
        ref[...] = jnp.zeros_like(ref)
    ref[...] += val


def _rms_fwd(u, w, *, name):
    c = u.shape[1]

    def body(u_ref, w_ref, o_ref):
        v = u_ref[...]
        r = lax.rsqrt(jnp.mean(v * v, axis=-1, keepdims=True) + EPS_RMS)
        o_ref[...] = (v * r * w_ref[...]).astype(BF16)

    return pl.pallas_call(body, name=name, grid=(S // TR,), in_specs=[_row_spec(c), _vec_spec(c)], out_specs=_row_spec(c),
                          out_shape=jax.ShapeDtypeStruct((S, c), BF16))(u, w)


def _rms_bwd(u, w, dy, *, name):
    c = u.shape[1]

    def body(u_ref, w_ref, dy_ref, du_ref, dw_ref):
        v = u_ref[...]
        g = dy_ref[...].astype(F32)
        r = lax.rsqrt(jnp.mean(v * v, axis=-1, keepdims=True) + EPS_RMS)
        gw = g * w_ref[...]
        du_ref[...] = r * gw - v * (r * r * r * jnp.mean(gw * v, axis=-1, keepdims=True))
        _acc_rows(dw_ref, jnp.sum(g * v * r, axis=0, keepdims=True))

    return pl.pallas_call(body, name=name, grid=(S // TR,), in_specs=[_row_spec(c), _vec_spec(c), _row_spec(c)],
                          out_specs=[_row_spec(c), _vec_spec(c)],
                          out_shape=[jax.ShapeDtypeStruct((S, c), F32), jax.ShapeDtypeStruct((1, c), F32)])(u, w, dy)


def _gate_norm_fwd(y, z, w):
    def body(y_ref, z_ref, w_ref, o_ref):
        zz = z_ref[...]
        v = y_ref[...] * (zz * _sigmoid(zz))
        r = lax.rsqrt(jnp.mean(v * v, axis=-1, keepdims=True) + EPS_RMS)
        o_ref[...] = (v * r * w_ref[...]).astype(BF16)

    c = SSD_INNER
    return pl.pallas_call(body, name="ssd_gate_norm_fwd", grid=(S // TR,), in_specs=[_row_spec(c), _row_spec(c), _vec_spec(c)],
                          out_specs=_row_spec(c), out_shape=jax.ShapeDtypeStruct((S, c), BF16))(y, z, w)


def _gate_norm_bwd(y, z, w, dout):
    def body(y_ref, z_ref, w_ref, g_ref, dy_ref, dz_ref, dw_ref):
        yy = y_ref[...]
        zz = z_ref[...]
        sg = _sigmoid(zz)
        sz = zz * sg
        v = yy * sz
        g = g_ref[...]
        r = lax.rsqrt(jnp.mean(v * v, axis=-1, keepdims=True) + EPS_RMS)
        gw = g * w_ref[...]
        dv = r * gw - v * (r * r * r * jnp.mean(gw * v, axis=-1, keepdims=True))
        dy_ref[...] = dv * sz
        dz_ref[...] = (dv * yy * (sg * (1.0 + zz * (1.0 - sg)))).astype(BF16)
        _acc_rows(dw_ref, jnp.sum(g * v * r, axis=0, keepdims=True))

    c = SSD_INNER
    return pl.pallas_call(body, name="ssd_gate_norm_bwd", grid=(S // TR,),
                          in_specs=[_row_spec(c), _row_spec(c), _vec_spec(c), _row_spec(c)],
                          out_specs=[_row_spec(c), _row_spec(c), _vec_spec(c)],
                          out_shape=[jax.ShapeDtypeStruct((S, c), F32), jax.ShapeDtypeStruct((S, c), BF16),
                                     jax.ShapeDtypeStruct((1, c), F32)])(y, z, w, dout)


def _ln_fwd(xr, mix, g, b):
    def body(x_ref, m_ref, g_ref, b_ref, o_ref, ob_ref):
        pre = ALPHA * x_ref[...] + m_ref[...]
        mu = jnp.mean(pre, axis=-1, keepdims=True)
        d = pre - mu
        rs = lax.rsqrt(jnp.mean(d * d, axis=-1, keepdims=True) + EPS_LN)
        h = d * rs * g_ref[...] + b_ref[...]
        o_ref[...] = h
        ob_ref[...] = h.astype(BF16)

    return pl.pallas_call(body, name="ln_mix_fwd", grid=(S // TR,), in_specs=[_row_spec(D), _row_spec(D), _vec_spec(D), _vec_spec(D)],
                          out_specs=[_row_spec(D)] * 2,
                          out_shape=[jax.ShapeDtypeStruct((S, D), F32), jax.ShapeDtypeStruct((S, D), BF16)])(xr, mix, g, b)


def _ln_bwd(xr, mix, g, dh):
    def body(x_ref, m_ref, g_ref, dh_ref, dpre_ref, dpreb_ref, dg_ref, db_ref):
        pre = ALPHA * x_ref[...] + m_ref[...]
        mu = jnp.mean(pre, axis=-1, keepdims=True)
        d = pre - mu
        rs = lax.rsqrt(jnp.mean(d * d, axis=-1, keepdims=True) + EPS_LN)
        xh = d * rs
        dy = dh_ref[...]
        gy = dy * g_ref[...]
        dpre = rs * (gy - jnp.mean(gy, axis=-1, keepdims=True) - xh * jnp.mean(gy * xh, axis=-1, keepdims=True))
        dpre_ref[...] = dpre
        dpreb_ref[...] = dpre.astype(BF16)
        _acc_rows(dg_ref, jnp.sum(dy * xh, axis=0, keepdims=True))
        _acc_rows(db_ref, jnp.sum(dy, axis=0, keepdims=True))

    return pl.pallas_call(body, name="ln_mix_bwd", grid=(S // TR,),
                          in_specs=[_row_spec(D), _row_spec(D), _vec_spec(D), _row_spec(D)],
                          out_specs=[_row_spec(D), _row_spec(D), _vec_spec(D), _vec_spec(D)],
                          out_shape=[jax.ShapeDtypeStruct((S, D), F32), jax.ShapeDtypeStruct((S, D), BF16),
                                     jax.ShapeDtypeStruct((1, D), F32), jax.ShapeDtypeStruct((1, D), F32)])(xr, mix, g, dh)


FF_CHUNK = D_FF // NCHIP


FF_ROWS = 1024


def _ff_act_spec():
    return pl.BlockSpec((None, FF_ROWS, FF_CHUNK), lambda i, k: (k, i, 0))


def _ff_w_spec():
    return pl.BlockSpec((None, FF_CHUNK, D), lambda i, k: (k, 0, 0))


def _ffn_hidden_fwd(h, w_gate_t, w_up_t):
    def body(h_ref, wg_ref, wu_ref, g_ref, u_ref, a_ref):
        hh = h_ref[...]
        g = _dot(hh, wg_ref[...], ((1,), (1,)))
        u = _dot(hh, wu_ref[...], ((1,), (1,)))
        g_ref[...] = g.astype(BF16)
        u_ref[...] = u.astype(BF16)
        a_ref[...] = (g * _sigmoid(g) * u).astype(BF16)

    return pl.pallas_call(
        body, name="ffn_hidden_fwd", grid=(S // FF_ROWS, NCHIP),
        in_specs=[pl.BlockSpec((FF_ROWS, D), lambda i, k: (i, 0)), _ff_w_spec(), _ff_w_spec()], out_specs=[_ff_act_spec()] * 3,
        out_shape=[jax.ShapeDtypeStruct((NCHIP, S, FF_CHUNK), BF16)] * 3,
        compiler_params=pltpu.CompilerParams(dimension_semantics=("parallel", "parallel")),
    )(h, w_gate_t, w_up_t)


def _ffn_hidden_bwd(dout, w_down, gate, up):
    def body(d_ref, wd_ref, g_ref, u_ref, dg_ref, du_ref):
        d = _dot(d_ref[...], wd_ref[...], ((1,), (1,)))
        g = g_ref[...].astype(F32)
        sg = _sigmoid(g)
        dg_ref[...] = (d * u_ref[...].astype(F32) * (sg * (1.0 + g * (1.0 - sg)))).astype(BF16)
        du_ref[...] = (d * g * sg).astype(BF16)

    return pl.pallas_call(
        body, name="ffn_hidden_bwd", grid=(S // FF_ROWS, NCHIP),
        in_specs=[pl.BlockSpec((FF_ROWS, D), lambda i, k: (i, 0)), _ff_w_spec(), _ff_act_spec(), _ff_act_spec()],
        out_specs=[_ff_act_spec()] * 2, out_shape=[jax.ShapeDtypeStruct((NCHIP, S, FF_CHUNK), BF16)] * 2,
        compiler_params=pltpu.CompilerParams(dimension_semantics=("parallel", "parallel")),
    )(dout, w_down, gate, up)


def _final_fwd_bwd(h1, ffn, pg, pp, target, g2, b2):
    def body(h_ref, f_ref, pg_ref, pp_ref, t_ref, g_ref, b_ref, dpre_ref, dpreb_ref, dpg_ref, dpp_ref, dg_ref, db_ref, loss_ref):
        sg = _sigmoid(pg_ref[...])
        ppv = pp_ref[...]
        pre = ALPHA * h_ref[...] + f_ref[...] + sg * ppv
        mu = jnp.mean(pre, axis=-1, keepdims=True)
        d = pre - mu
        rs = lax.rsqrt(jnp.mean(d * d, axis=-1, keepdims=True) + EPS_LN)
        xh = d * rs
        err = xh * g_ref[...] + b_ref[...] - t_ref[...]
        dy = err * (1.0 / D)
        gy = dy * g_ref[...]
        dpre = rs * (gy - jnp.mean(gy, axis=-1, keepdims=True) - xh * jnp.mean(gy * xh, axis=-1, keepdims=True))
        dpre_ref[...] = dpre
        dpreb_ref[...] = dpre.astype(BF16)
        dpg_ref[...] = (dpre * ppv * sg * (1.0 - sg)).astype(BF16)
        dpp_ref[...] = (dpre * sg).astype(BF16)
        _acc_rows(dg_ref, jnp.sum(dy * xh, axis=0, keepdims=True))
        _acc_rows(db_ref, jnp.sum(dy, axis=0, keepdims=True))
        _acc_rows(loss_ref, 0.5 * jnp.sum(jnp.mean(err * err, axis=-1, keepdims=True), axis=0, keepdims=True) * jnp.ones((1, LANE), F32))

    return pl.pallas_call(
        body, name="final_ln_loss", grid=(S // TR,),
        in_specs=[_row_spec(D)] * 5 + [_vec_spec(D)] * 2,
        out_specs=[_row_spec(D)] * 4 + [_vec_spec(D), _vec_spec(D), _vec_spec(LANE)],
        out_shape=[jax.ShapeDtypeStruct((S, D), F32)] + [jax.ShapeDtypeStruct((S, D), BF16)] * 3 + [
                   jax.ShapeDtypeStruct((1, D), F32), jax.ShapeDtypeStruct((1, D), F32), jax.ShapeDtypeStruct((1, LANE), F32)],
    )(h1, ffn, pg, pp, target, g2, b2)


def _rot(u, cos_t, sin_t, lane):
    partner = jnp.where(lane < NOPE + ROPE // 2, pltpu.roll(u, LANE - ROPE // 2, 1), pltpu.roll(u, ROPE // 2, 1))
    return u * cos_t + partner * sin_t


def _rms(v, w):
    r = lax.rsqrt(jnp.mean(v * v, axis=-1, keepdims=True) + EPS_RMS)
    return v * r * w, r


def _rms_grad(v, r, w, g):
    gw = g * w
    return r * gw - v * (r * r * r * jnp.mean(gw * v, axis=-1, keepdims=True)), jnp.sum(g * v * r, axis=0, keepdims=True)


def _whole(arr):
    return pl.BlockSpec(arr.shape, lambda i: (0,) * arr.ndim)


def _qkv_fwd(small, w_q, w_k, w_v, q_norm, kv_norm, cos_t, sin_t):
    def body(sm_ref, wq_ref, wk_ref, wv_ref, qw_ref, kw_ref, c_ref, s_ref, qn_ref, kvn_ref, q_ref, k_ref, kt_ref, v_ref):
        lane = lax.broadcasted_iota(jnp.int32, (TR, LANE), 1)
        c, s = c_ref[...], s_ref[...]
        qn = _rms(sm_ref[:, SM_Q:SM_Q + Q_RANK], qw_ref[...])[0].astype(BF16)
        kvn = _rms(sm_ref[:, SM_KV:SM_KV + KV_RANK], kw_ref[...])[0].astype(BF16)
        qn_ref[...] = qn
        kvn_ref[...] = kvn
        kr = _rot(pltpu.roll(sm_ref[:, SM_KR:SM_KR + LANE], NOPE, 1), c, s, lane)
        for h in range(H):
            tile = slice(h * LANE, (h + 1) * LANE)
            q_ref[:, tile] = _rot(_dot(qn, wq_ref[:, tile], ((1,), (0,))), c, s, lane).astype(BF16)
            kt = _dot(kvn, wk_ref[:, tile], ((1,), (0,))) + kr
            k_ref[:, tile] = kt.astype(BF16)
            kt_ref[tile, :] = kt.T.astype(BF16)
        v_ref[...] = _dot(kvn, wv_ref[...], ((1,), (0,))).astype(BF16)

    w = H * LANE
    return pl.pallas_call(
        body, name="qkv_fwd", grid=(S // TR,),
        in_specs=[_row_spec(SMALL_W), _whole(w_q), _whole(w_k), _whole(w_v), _vec_spec(Q_RANK), _vec_spec(KV_RANK), _row_spec(LANE), _row_spec(LANE)],
        out_specs=[_row_spec(Q_RANK), _row_spec(KV_RANK), _row_spec(w), _row_spec(w), pl.BlockSpec((w, TR), lambda i: (0, i)),
                   _row_spec(H * VDIM)],
        out_shape=[jax.ShapeDtypeStruct((S, Q_RANK), BF16), jax.ShapeDtypeStruct((S, KV_RANK), BF16), jax.ShapeDtypeStruct((S, w), BF16),
                   jax.ShapeDtypeStruct((S, w), BF16), jax.ShapeDtypeStruct((w, S), BF16), jax.ShapeDtypeStruct((S, H * VDIM), BF16)],
    )(small, w_q, w_k, w_v, q_norm, kv_norm, cos_t, sin_t)


def _qkv_bwd(dqt, dk, dv, small, w_q, w_k, w_v, q_norm, kv_norm, cos_t, sin_t):
    def body(dq_ref, dk_ref, dv_ref, sm_ref, wq_ref, wk_ref, wv_ref, qw_ref, kw_ref, c_ref, s_ref,
             ds_ref, dql_ref, dkb_ref, dqw_ref, dkw_ref):
        lane = lax.broadcasted_iota(jnp.int32, (TR, LANE), 1)
        c, s = c_ref[...], -s_ref[...]
        dqn = jnp.zeros((TR, Q_RANK), F32)
        dkvn = _dot(dv_ref[...], wv_ref[...], ((1,), (1,)))
        dkr = jnp.zeros((TR, LANE), F32)
        for h in range(H):
            tile = slice(h * LANE, (h + 1) * LANE)
            dql = _rot(dq_ref[tile, :].T, c, s, lane).astype(BF16)
            dql_ref[:, tile] = dql
            dqn = dqn + _dot(dql, wq_ref[:, tile], ((1,), (1,)))
            dkt = dk_ref[:, tile]
            dkb_ref[:, tile] = dkt.astype(BF16)
            dkvn = dkvn + _dot(dkt, wk_ref[:, tile], ((1,), (1,)))
            dkr = dkr + dkt
        dkr = jnp.where((lane >= NOPE) & (lane < NOPE + ROPE), dkr, 0.0)
        q_c, kv_c = sm_ref[:, SM_Q:SM_Q + Q_RANK], sm_ref[:, SM_KV:SM_KV + KV_RANK]
        dq_c, dqw = _rms_grad(q_c, _rms(q_c, qw_ref[...])[1], qw_ref[...], dqn)
        dkv_c, dkw = _rms_grad(kv_c, _rms(kv_c, kw_ref[...])[1], kw_ref[...], dkvn)
        ds_ref[:, SM_Q:SM_Q + Q_RANK] = dq_c.astype(BF16)
        ds_ref[:, SM_KV:SM_KV + KV_RANK] = dkv_c.astype(BF16)
        ds_ref[:, SM_KR:SM_KR + LANE] = pltpu.roll(_rot(dkr, c, s, lane), LANE - NOPE, 1).astype(BF16)
        _acc_rows(dqw_ref, dqw)
        _acc_rows(dkw_ref, dkw)

    w = H * LANE
    return pl.pallas_call(
        body, name="qkv_bwd", grid=(S // TR,),
        in_specs=[pl.BlockSpec((w, TR), lambda i: (0, i)), _row_spec(w), _row_spec(H * VDIM), _row_spec(SMALL_W), _whole(w_q), _whole(w_k),
                  _whole(w_v), _vec_spec(Q_RANK), _vec_spec(KV_RANK), _row_spec(LANE), _row_spec(LANE)],
        out_specs=[_row_spec(SM_DT), _row_spec(w), _row_spec(w), _vec_spec(Q_RANK), _vec_spec(KV_RANK)],
        out_shape=[jax.ShapeDtypeStruct((S, SM_DT), BF16), jax.ShapeDtypeStruct((S, w), BF16), jax.ShapeDtypeStruct((S, w), BF16),
                   jax.ShapeDtypeStruct((1, Q_RANK), F32), jax.ShapeDtypeStruct((1, KV_RANK), F32)],
    )(dqt, dk, dv, small, w_q, w_k, w_v, q_norm, kv_norm, cos_t, sin_t)


CB = 256


def _shift_down(u, k, row):
    if k == 0:
        return u
    return jnp.where(row >= k, pltpu.roll(u, k, 0), 0.0)


def _shift_up(u, k, row):
    if k == 0:
        return u
    return jnp.where(row < S - k, pltpu.roll(u, S - k, 0), 0.0)


def _conv_fwd(u, w, b):
    def body(u_ref, w_ref, b_ref, o_ref):
        row = lax.broadcasted_iota(jnp.int32, (S, CB), 0)
        uu = u_ref[...]
        acc = b_ref[...] + w_ref[SSD_K - 1:SSD_K, :] * uu
        for k in range(SSD_K - 1):
            acc = acc + w_ref[k:k + 1, :] * _shift_down(uu, SSD_K - 1 - k, row)
        o_ref[...] = acc * _sigmoid(acc)

    c = u.shape[1]
    return pl.pallas_call(
        body, name="conv_fwd", grid=(c // CB,),
        in_specs=[pl.BlockSpec((S, CB), lambda j: (0, j)), pl.BlockSpec((SSD_K, CB), lambda j: (0, j)), pl.BlockSpec((1, CB), lambda j: (0, j))],
        out_specs=pl.BlockSpec((S, CB), lambda j: (0, j)), out_shape=jax.ShapeDtypeStruct((S, c), F32),
    )(u, w, b)


def _conv_bwd(u, w, b, dact):
    def body(u_ref, w_ref, b_ref, d_ref, du_ref, dw_ref, db_ref):
        row = lax.broadcasted_iota(jnp.int32, (S, CB), 0)
        uu = u_ref[...]
        sh = [_shift_down(uu, SSD_K - 1 - k, row) for k in range(SSD_K)]
        acc = b_ref[...]
        for k in range(SSD_K):
            acc = acc + w_ref[k:k + 1, :] * sh[k]
        sg = _sigmoid(acc)
        dacc = d_ref[...] * (sg * (1.0 + acc * (1.0 - sg)))
        du = w_ref[SSD_K - 1:SSD_K, :] * dacc
        for k in range(SSD_K - 1):
            du = du + w_ref[k:k + 1, :] * _shift_up(dacc, SSD_K - 1 - k, row)
        du_ref[...] = du.astype(BF16)
        for k in range(SSD_K):
            dw_ref[k:k + 1, :] = jnp.sum(dacc * sh[k], axis=0, keepdims=True)
        db_ref[...] = jnp.sum(dacc, axis=0, keepdims=True)

    c = u.shape[1]
    col = lambda r: pl.BlockSpec((r, CB), lambda j: (0, j))
    return pl.pallas_call(
        body, name="conv_bwd", grid=(c // CB,), in_specs=[col(S), col(SSD_K), col(1), col(S)], out_specs=[col(S), col(SSD_K), col(1)],
        out_shape=[jax.ShapeDtypeStruct((S, c), BF16), jax.ShapeDtypeStruct((SSD_K, c), F32), jax.ShapeDtypeStruct((1, c), F32)],
    )(u, w, b, dact)


NPAIR = H // 2
PAIRS_PER_GROUP = NPAIR // SSD_G


def _softplus(v):
    return jnp.maximum(v, 0.0) + jnp.log(1.0 + jnp.exp(-jnp.abs(v)))


def _dot(a, b, dims):
    return lax.dot_general(a.astype(BF16), b.astype(BF16), (dims, ((), ())), preferred_element_type=F32)


def _dot2(a, sel):
    hi = a.astype(BF16)
    lo = (a - hi.astype(F32)).astype(BF16)
    dims = (((1,), (0,)), ((), ()))
    return lax.dot_general(hi, sel, dims, preferred_element_type=F32) + lax.dot_general(lo, sel, dims, preferred_element_type=F32)


def _dot3(a, b, dims, split_lhs):
    v = a if split_lhs else b
    v1 = v.astype(BF16)
    r1 = v - v1.astype(F32)
    v2 = r1.astype(BF16)
    v3 = (r1 - v2.astype(F32)).astype(BF16)
    acc = None
    for part in (v1, v2, v3):
        lhs, rhs = (part, b) if split_lhs else (a, part)
        t = lax.dot_general(lhs, rhs, (dims, ((), ())), preferred_element_type=F32)
        acc = t if acc is None else acc + t
    return acc


def _ssd_chunk_common(dt_ref, dtT_ref, prow_ref, pcol_ref):
    prow = prow_ref[...]
    pcol = pcol_ref[...]
    ri = lax.broadcasted_iota(jnp.int32, (SSD_L, SSD_L), 0)
    ci = lax.broadcasted_iota(jnp.int32, (SSD_L, SSD_L), 1)
    causal = ri >= ci
    pre_c = dt_ref[...] + prow[0:1, :]
    dtc = _softplus(pre_c)
    a_row = -jnp.exp(prow[1:2, :])
    cs_col = _dot3(causal.astype(BF16), dtc * a_row, ((1,), (0,)), False)
    dtr = _softplus(dtT_ref[...] + pcol[:, 0:1])
    a_col = -jnp.exp(pcol[:, 1:2])
    cs_row = _dot3(dtr * a_col, (ri <= ci).astype(BF16), ((1,), (0,)), True)
    return prow, causal, pre_c, dtc, a_row, cs_col, cs_row


def _ssd_fwd(act, small, dtT, prow, pcol):
    def body(x_ref, b_ref, c_ref, dt_ref, dtT_ref, prow_ref, pcol_ref, y_ref, st_ref, state):
        @pl.when(pl.program_id(0) == 0)
        def _():
            state[...] = jnp.zeros_like(state)

        prow, causal, _, dtc, _, cs_col, cs_row = _ssd_chunk_common(dt_ref, dtT_ref, prow_ref, pcol_ref)
        lo = lax.broadcasted_iota(jnp.int32, (SSD_L, LANE), 1) < SSD_P
        lo1 = lo[0:1, :]
        for g in range(SSD_G):
            bm = b_ref[:, g * SSD_N:(g + 1) * SSD_N]
            cm = c_ref[:, g * SSD_N:(g + 1) * SSD_N]
            cb = _dot(cm, bm, ((1,), (1,)))
            for qq in range(PAIRS_PER_GROUP):
                q = g * PAIRS_PER_GROUP + qq
                ha, hb = 2 * q, 2 * q + 1
                csa, csb = cs_col[:, ha:ha + 1], cs_col[:, hb:hb + 1]
                xp = x_ref[:, q * LANE:(q + 1) * LANE]
                xx = xp * jnp.where(lo, dtc[:, ha:ha + 1], dtc[:, hb:hb + 1])
                ga = cb * jnp.exp(jnp.where(causal, csa - cs_row[ha:ha + 1, :], NEG))
                gb = cb * jnp.exp(jnp.where(causal, csb - cs_row[hb:hb + 1, :], NEG))
                y = _dot(ga, jnp.where(lo, xx, 0.0), ((1,), (0,))) + _dot(gb, jnp.where(lo, 0.0, xx), ((1,), (0,)))
                s_in = state[q]
                y = y + _dot(cm, s_in, ((1,), (0,))) * jnp.where(lo, jnp.exp(csa), jnp.exp(csb))
                y = y + jnp.where(lo1, prow[2:3, ha:ha + 1], prow[2:3, hb:hb + 1]) * xp
                y_ref[:, q * LANE:(q + 1) * LANE] = y
                la, lb = csa[SSD_L - 1:SSD_L, :], csb[SSD_L - 1:SSD_L, :]
                decay = jnp.where(lo, jnp.exp(la - csa), jnp.exp(lb - csb))
                st_ref[q] = s_in
                state[q] = s_in * jnp.where(lo1, jnp.exp(la), jnp.exp(lb)) + _dot(bm, xx * decay, ((0,), (0,)))

    L = SSD_L
    return pl.pallas_call(
        body, name="ssd_fwd", grid=(SSD_NC,),
        in_specs=[pl.BlockSpec((L, SSD_INNER), lambda c: (c, 0)),
                  pl.BlockSpec((L, SSD_G * SSD_N), lambda c: (c, SSD_INNER // (SSD_G * SSD_N))),
                  pl.BlockSpec((L, SSD_G * SSD_N), lambda c: (c, SSD_INNER // (SSD_G * SSD_N) + 1)),
                  pl.BlockSpec((L, LANE), lambda c: (c, SM_DT // LANE)),
                  pl.BlockSpec((LANE, L), lambda c: (0, c)),
                  pl.BlockSpec((8, LANE), lambda c: (0, 0)), pl.BlockSpec((LANE, 8), lambda c: (0, 0))],
        out_specs=[pl.BlockSpec((L, SSD_INNER), lambda c: (c, 0)),
                   pl.BlockSpec((None, NPAIR, SSD_N, LANE), lambda c: (c, 0, 0, 0))],
        out_shape=[jax.ShapeDtypeStruct((S, SSD_INNER), F32), jax.ShapeDtypeStruct((SSD_NC, NPAIR, SSD_N, LANE), F32)],
        scratch_shapes=[pltpu.VMEM((NPAIR, SSD_N, LANE), F32)],
        compiler_params=pltpu.CompilerParams(dimension_semantics=("arbitrary",)),
    )(act, act, act, small, dtT, prow, pcol)


def _ssd_bwd(act, small, dtT, prow, pcol, states, dy):
    def body(x_ref, b_ref, c_ref, dt_ref, dtT_ref, prow_ref, pcol_ref, st_ref, dy_ref,
             dx_ref, ddt_ref, dp_ref, dstate):
        @pl.when(pl.program_id(0) == 0)
        def _():
            dstate[...] = jnp.zeros_like(dstate)
            dp_ref[...] = jnp.zeros_like(dp_ref)

        prow, causal, pre_c, dtc, a_row, cs_col, cs_row = _ssd_chunk_common(dt_ref, dtT_ref, prow_ref, pcol_ref)
        lane = lax.broadcasted_iota(jnp.int32, (SSD_L, LANE), 1)
        sub = lax.broadcasted_iota(jnp.int32, (LANE, SSD_L), 0)
        rowi = lax.broadcasted_iota(jnp.int32, (SSD_L, 1), 0)
        pick_p = lax.broadcasted_iota(jnp.int32, (LANE, LANE), 0)
        pick_l = lax.broadcasted_iota(jnp.int32, (LANE, LANE), 1)
        lo = lane < SSD_P
        lo1 = lo[0:1, :]
        dcs_c = jnp.zeros((SSD_L, LANE), F32)
        dcs_r = jnp.zeros((LANE, SSD_L), F32)
        ddt_x = jnp.zeros((SSD_L, LANE), F32)
        dd_row = jnp.zeros((1, LANE), F32)
        for g in range(SSD_G):
            bm = b_ref[:, g * SSD_N:(g + 1) * SSD_N]
            cm = c_ref[:, g * SSD_N:(g + 1) * SSD_N]
            cb = _dot(cm, bm, ((1,), (1,)))
            dcb = jnp.zeros((SSD_L, SSD_L), F32)
            dbm = jnp.zeros((SSD_L, SSD_N), F32)
            dcm = jnp.zeros((SSD_L, SSD_N), F32)
            for qq in range(PAIRS_PER_GROUP):
                q = g * PAIRS_PER_GROUP + qq
                ha, hb = 2 * q, 2 * q + 1
                csa, csb = cs_col[:, ha:ha + 1], cs_col[:, hb:hb + 1]
                xp = x_ref[:, q * LANE:(q + 1) * LANE]
                dtp = jnp.where(lo, dtc[:, ha:ha + 1], dtc[:, hb:hb + 1])
                xx = xp * dtp
                lma = jnp.exp(jnp.where(causal, csa - cs_row[ha:ha + 1, :], NEG))
                lmb = jnp.exp(jnp.where(causal, csb - cs_row[hb:hb + 1, :], NEG))
                ga, gb = cb * lma, cb * lmb
                dyp = dy_ref[:, q * LANE:(q + 1) * LANE]
                dya, dyb = jnp.where(lo, dyp, 0.0), jnp.where(lo, 0.0, dyp)
                s_in = st_ref[q]
                ds_out = dstate[q]
                la, lb = csa[SSD_L - 1:SSD_L, :], csb[SSD_L - 1:SSD_L, :]
                ecs = jnp.where(lo, jnp.exp(csa), jnp.exp(csb))
                decay = jnp.where(lo, jnp.exp(la - csa), jnp.exp(lb - csb))
                cd = jnp.where(lo1, jnp.exp(la), jnp.exp(lb))
                bds = _dot(bm, ds_out, ((1,), (0,)))
                dxx = _dot(ga, dya, ((0,), (0,))) + _dot(gb, dyb, ((0,), (0,))) + bds * decay
                dga = _dot(dya, xx, ((1,), (1,)))
                dgb = _dot(dyb, xx, ((1,), (1,)))
                dsega, dsegb = dga * ga, dgb * gb
                dcb = dcb + dga * lma + dgb * lmb
                yoff = _dot(cm, s_in, ((1,), (0,))) * ecs
                dye = dyp * ecs
                dcm = dcm + _dot(dye, s_in, ((1,), (1,)))
                xd = xx * decay
                dbm = dbm + _dot(xd, ds_out, ((1,), (1,)))
                wv = xd * bds
                ends = jnp.sum(wv, axis=0, keepdims=True) + cd * jnp.sum(ds_out * s_in, axis=0, keepdims=True)
                t1 = dyp * yoff - wv + jnp.where(rowi == SSD_L - 1, ends, 0.0)
                to_pair = (((pick_p < SSD_P) & (pick_l == ha)) | ((pick_p >= SSD_P) & (pick_l == hb))).astype(BF16)
                to_a_b = jnp.concatenate([(pick_l == ha).astype(BF16), (pick_l == hb).astype(BF16)], axis=0)
                dcs_c = dcs_c + _dot2(t1, to_pair) + _dot2(jnp.concatenate([dsega, dsegb], axis=1), to_a_b)
                dcs_r = (dcs_r + jnp.where(sub == ha, jnp.sum(dsega, axis=0, keepdims=True), 0.0)
                         + jnp.where(sub == hb, jnp.sum(dsegb, axis=0, keepdims=True), 0.0))
                dstate[q] = _dot(cm, dye, ((0,), (0,))) + cd * ds_out
                dpair = jnp.where(lo1, prow[2:3, ha:ha + 1], prow[2:3, hb:hb + 1])
                dx_ref[:, q * LANE:(q + 1) * LANE] = dxx * dtp + dpair * dyp
                ddt_x = ddt_x + _dot2(dxx * xp, to_pair)
                dd_row = dd_row + jnp.sum(_dot2(dyp * xp, to_pair), axis=0, keepdims=True)
            dx_ref[:, SSD_INNER + g * SSD_N:SSD_INNER + (g + 1) * SSD_N] = dbm + _dot(dcb, cm, ((0,), (0,)))
            dx_ref[:, SSD_INNER + (SSD_G + g) * SSD_N:SSD_INNER + (SSD_G + g + 1) * SSD_N] = dcm + _dot(dcb, bm, ((1,), (0,)))
        ri = lax.broadcasted_iota(jnp.int32, (SSD_L, SSD_L), 0)
        ci = lax.broadcasted_iota(jnp.int32, (SSD_L, SSD_L), 1)
        da = _dot3((ri <= ci).astype(BF16), dcs_c, ((1,), (0,)), False)
        da = da - _dot3(dcs_r, causal.astype(BF16), ((1,), (0,)), True).T
        ddt = ddt_x + da * a_row
        ddt_raw = ddt * _sigmoid(pre_c)
        ddt_ref[...] = ddt_raw
        da_head = jnp.sum(da * dtc, axis=0, keepdims=True) * a_row
        dp_ref[0:1, :] += jnp.sum(ddt_raw, axis=0, keepdims=True)
        dp_ref[1:2, :] += da_head
        dp_ref[2:3, :] += dd_row

    L = SSD_L
    rev = SSD_NC - 1
    bc_cols = SSD_INNER // (SSD_G * SSD_N)
    return pl.pallas_call(
        body, name="ssd_bwd", grid=(SSD_NC,),
        in_specs=[pl.BlockSpec((L, SSD_INNER), lambda c: (rev - c, 0)),
                  pl.BlockSpec((L, SSD_G * SSD_N), lambda c: (rev - c, bc_cols)),
                  pl.BlockSpec((L, SSD_G * SSD_N), lambda c: (rev - c, bc_cols + 1)),
                  pl.BlockSpec((L, LANE), lambda c: (rev - c, SM_DT // LANE)),
                  pl.BlockSpec((LANE, L), lambda c: (0, rev - c)),
                  pl.BlockSpec((8, LANE), lambda c: (0, 0)), pl.BlockSpec((LANE, 8), lambda c: (0, 0)),
                  pl.BlockSpec((None, NPAIR, SSD_N, LANE), lambda c: (rev - c, 0, 0, 0)),
                  pl.BlockSpec((L, SSD_INNER), lambda c: (rev - c, 0))],
        out_specs=[pl.BlockSpec((L, SSD_XBC), lambda c: (rev - c, 0)),
                   pl.BlockSpec((L, LANE), lambda c: (rev - c, 0)),
                   pl.BlockSpec((8, LANE), lambda c: (0, 0))],
        out_shape=[jax.ShapeDtypeStruct((S, SSD_XBC), F32), jax.ShapeDtypeStruct((S, LANE), F32),
                   jax.ShapeDtypeStruct((8, LANE), F32)],
        scratch_shapes=[pltpu.VMEM((NPAIR, SSD_N, LANE), F32)],
        compiler_params=pltpu.CompilerParams(dimension_semantics=("arbitrary",)),
    )(act, act, act, small, dtT, prow, pcol, states, dy)


TQ = 256
TK = 256
FWD_TQ = 256
FWD_TK = 256


def _attn_fwd(qc, kc, v):
    TQ, TK = FWD_TQ, FWD_TK

    def body(q_ref, k_ref, v_ref, o_ref, lse_ref):
        i = pl.program_id(1)
        lo = lax.broadcasted_iota(jnp.int32, (TQ, LANE), 1) < VDIM
        lo_k = lax.broadcasted_iota(jnp.int32, (TK, LANE), 1) < VDIM
        row_minus_col = lax.broadcasted_iota(jnp.int32, (TQ, TK), 0) - lax.broadcasted_iota(jnp.int32, (TQ, TK), 1)
        qa, qb = q_ref[:, 0:LANE], q_ref[:, LANE:2 * LANE]

        def scores(kb):
            kk = k_ref[pl.ds(pl.multiple_of(kb * TK, TK), TK), :]
            return (_dot(qa, kk[:, 0:LANE], ((1,), (1,))) * ATT_SCALE_LOG2, _dot(qb, kk[:, LANE:2 * LANE], ((1,), (1,))) * ATT_SCALE_LOG2)

        def update(kb, sa, sb, stats):
            ma, la, mb, lb, acc = stats
            vv = v_ref[pl.ds(pl.multiple_of(kb * TK, TK), TK), :]
            na = jnp.maximum(ma, jnp.max(sa, axis=1, keepdims=True))
            nb = jnp.maximum(mb, jnp.max(sb, axis=1, keepdims=True))
            pa, pb = jnp.exp2(sa - na), jnp.exp2(sb - nb)
            fa, fb = jnp.exp2(ma - na), jnp.exp2(mb - nb)
            la = fa * la + jnp.sum(pa, axis=1, keepdims=True)
            lb = fb * lb + jnp.sum(pb, axis=1, keepdims=True)
            acc = (acc * jnp.where(lo, fa, fb) + _dot(pa, jnp.where(lo_k, vv, 0), ((1,), (0,)))
                   + _dot(pb, jnp.where(lo_k, 0, vv), ((1,), (0,))))
            return na, la, nb, lb, acc

        def step(kb, carry):
            sa, sb = carry[:2]
            nxt = scores(kb + 1)
            return nxt + update(kb, sa, sb, carry[2:])

        neg = jnp.full((TQ, 1), NEG, F32)
        zero = jnp.zeros((TQ, 1), F32)
        n_full = i * (TQ // TK)
        carry = lax.fori_loop(0, n_full, step, scores(0) + (neg, zero, neg, zero, jnp.zeros((TQ, LANE), F32)))
        s, stats = carry[:2], carry[2:]
        for d in range(TQ // TK):
            nxt = scores(n_full + d + 1) if d + 1 < TQ // TK else None
            sa, sb = (jnp.where(row_minus_col >= d * TK, t, NEG) for t in s)
            stats = update(n_full + d, sa, sb, stats)
            s = nxt
        ma, la, mb, lb, acc = stats
        o_ref[...] = acc / jnp.where(lo, la, lb)
        lse_ref[...] = jnp.where(lo, ma + jnp.log2(la), mb + jnp.log2(lb)) * LN2

    return pl.pallas_call(
        body, name="attn_fwd", grid=(NPAIR, S // TQ),
        in_specs=[pl.BlockSpec((TQ, 2 * LANE), lambda j, i: (i, j)), pl.BlockSpec((S, 2 * LANE), lambda j, i: (0, j)),
                  pl.BlockSpec((S, LANE), lambda j, i: (0, j))],
        out_specs=[pl.BlockSpec((TQ, LANE), lambda j, i: (i, j)), pl.BlockSpec((None, TQ, LANE), lambda j, i: (j, i, 0))],
        out_shape=[jax.ShapeDtypeStruct((S, H * VDIM), F32), jax.ShapeDtypeStruct((NPAIR, S, LANE), F32)],
        compiler_params=pltpu.CompilerParams(dimension_semantics=("parallel", "parallel")),
    )(qc, kc, v)


def _attn_rows(lse, o, do):
    def body(lse_ref, o_ref, do_ref, r_ref):
        lt = lse_ref[...].T * (1.0 / LN2)
        tt = (o_ref[...] * do_ref[...]).T
        r_ref[...] = jnp.zeros_like(r_ref)
        r_ref[0:1, :] = lt[0:1, :]
        r_ref[1:2, :] = lt[VDIM:VDIM + 1, :]
        r_ref[2:3, :] = jnp.sum(tt[0:VDIM, :], axis=0, keepdims=True)
        r_ref[3:4, :] = jnp.sum(tt[VDIM:LANE, :], axis=0, keepdims=True)

    tile = pl.BlockSpec((S, LANE), lambda j: (0, j))
    return pl.pallas_call(
        body, name="attn_rows", grid=(NPAIR,), in_specs=[pl.BlockSpec((None, S, LANE), lambda j: (j, 0, 0)), tile, tile],
        out_specs=pl.BlockSpec((None, 8, S), lambda j: (j, 0, 0)), out_shape=jax.ShapeDtypeStruct((NPAIR, 8, S), F32),
    )(lse, o, do)


def _attn_bwd(qc, kc, kct, v, do, rows):
    nq = S // TQ

    def body(q_ref, k_ref, kt_ref, v_ref, do_ref, r_ref, dqt_ref, dk_ref, dv_ref):
        kb = pl.program_id(1)

        @pl.when(kb == 0)
        def _():
            dqt_ref[...] = jnp.zeros_like(dqt_ref)

        lo = lax.broadcasted_iota(jnp.int32, (TK, LANE), 1) < VDIM
        q_minus_k = lax.broadcasted_iota(jnp.int32, (TK, TQ), 1) - lax.broadcasted_iota(jnp.int32, (TK, TQ), 0)
        vv = v_ref[...]
        kk = k_ref[...]

        def step(qi, carry):
            off = pl.multiple_of(qi * TQ, TQ)
            qq = q_ref[pl.ds(off, TQ), :]
            dd = do_ref[pl.ds(off, TQ), :].astype(BF16)
            rr = r_ref[:, pl.ds(off, TQ)]
            keep = q_minus_k >= (kb - qi) * TQ
            out = []
            for x in range(2):
                sel = lo if x == 0 else jnp.logical_not(lo)
                kx, qx = kk[:, x * LANE:(x + 1) * LANE], qq[:, x * LANE:(x + 1) * LANE]
                st = jnp.where(keep, _dot(kx, qx, ((1,), (1,))) * ATT_SCALE_LOG2, NEG)
                pt = jnp.exp2(st - rr[x:x + 1, :])
                dpt = _dot(jnp.where(sel, vv, 0), dd, ((1,), (1,)))
                dst = (pt * (dpt - rr[2 + x:3 + x, :]) * ATT_SCALE).astype(BF16)
                out.append(carry[x] + _dot(dst, qx, ((1,), (0,))))
                out.append(_dot(pt, jnp.where(sel, dd, 0), ((1,), (0,))))
                dqt_ref[x * LANE:(x + 1) * LANE, pl.ds(off, TQ)] += _dot(kt_ref[x * LANE:(x + 1) * LANE, :], dst, ((1,), (0,)))
            return out[0], out[2], carry[2] + out[1] + out[3]

        z = jnp.zeros((TK, LANE), F32)
        dka, dkb, dv = lax.fori_loop(kb, nq, step, (z, z, z))
        dk_ref[:, 0:LANE] = dka
        dk_ref[:, LANE:2 * LANE] = dkb
        dv_ref[...] = dv.astype(BF16)

    return pl.pallas_call(
        body, name="attn_bwd", grid=(NPAIR, S // TK),
        in_specs=[pl.BlockSpec((S, 2 * LANE), lambda j, k: (0, j)), pl.BlockSpec((TK, 2 * LANE), lambda j, k: (k, j)),
                  pl.BlockSpec((2 * LANE, TK), lambda j, k: (j, k)), pl.BlockSpec((TK, LANE), lambda j, k: (k, j)),
                  pl.BlockSpec((S, LANE), lambda j, k: (0, j)), pl.BlockSpec((None, 8, S), lambda j, k: (j, 0, 0))],
        out_specs=[pl.BlockSpec((2 * LANE, S), lambda j, k: (j, 0)), pl.BlockSpec((TK, 2 * LANE), lambda j, k: (k, j)),
                   pl.BlockSpec((TK, LANE), lambda j, k: (k, j))],
        out_shape=[jax.ShapeDtypeStruct((H * LANE, S), F32), jax.ShapeDtypeStruct((S, H * LANE), F32),
                   jax.ShapeDtypeStruct((S, H * VDIM), BF16)],
        compiler_params=pltpu.CompilerParams(dimension_semantics=("parallel", "arbitrary")),
    )(qc, kc, kct, v, do, rows)


_IN_Z, _IN_XBC, _IN_DT, _IN_Q, _IN_KV, _IN_KR = 0, 1024, 2560, 2576, 2960, 3216


def _prep_in(w_in_t):
    dt = w_in_t.dtype
    w_small_t = jnp.concatenate(
        [w_in_t[_IN_Q:_IN_KV], w_in_t[_IN_KV:_IN_KR], w_in_t[_IN_KR:IN_WIDTH], jnp.zeros((LANE - ROPE, D), dt),
         w_in_t[_IN_DT:_IN_Q], jnp.zeros((LANE - H, D), dt)], axis=0)
    return w_in_t[_IN_Z:_IN_XBC], w_in_t[_IN_XBC:_IN_DT], w_small_t


DW_IN_ROWS = 512


def _d_w_in(dz, dxbc, dsm, xb):
    parts = (dz, dxbc, dsm)
    first = [0]
    for a in parts:
        first.append(first[-1] + a.shape[1] // DW_IN_ROWS)
    assert first[-1] == pl.cdiv(IN_WIDTH, DW_IN_ROWS)

    def body(dz_ref, dxbc_ref, dsm_ref, x_ref, o_ref):
        i = pl.program_id(0)
        for a_ref, lo, hi in zip((dz_ref, dxbc_ref, dsm_ref), first[:-1], first[1:]):
            @pl.when((i >= lo) & (i < hi))
            def _(a_ref=a_ref):
                o_ref[...] = lax.dot_general(a_ref[...], x_ref[...], (((0,), (0,)), ((), ())),
                                             preferred_element_type=F32).astype(BF16)

    def a_spec(lo, hi):
        return pl.BlockSpec((S, DW_IN_ROWS), lambda i: (0, jnp.clip(i - lo, 0, hi - lo - 1)))

    return pl.pallas_call(
        body, name="d_w_in", grid=(first[-1],),
        in_specs=[a_spec(lo, hi) for lo, hi in zip(first[:-1], first[1:])] + [pl.BlockSpec((S, D), lambda i: (0, 0))],
        out_specs=pl.BlockSpec((DW_IN_ROWS, D), lambda i: (i, 0)), out_shape=jax.ShapeDtypeStruct((IN_WIDTH, D), BF16),
    )(dz, dxbc, dsm, xb)


def _prep_attn(w_qb, w_kvb):
    w_q = jnp.pad(w_qb.reshape(Q_RANK, H, NOPE + ROPE), ((0, 0), (0, 0), (0, LANE - NOPE - ROPE))).reshape(Q_RANK, H * LANE)
    kv3 = w_kvb.reshape(KV_RANK, H, NOPE + VDIM)
    w_k = jnp.pad(kv3[:, :, :NOPE], ((0, 0), (0, 0), (0, LANE - NOPE))).reshape(KV_RANK, H * LANE)
    w_v = kv3[:, :, NOPE:].reshape(KV_RANK, H * VDIM)
    return w_q, w_k, w_v


def _rope_tables(positions):
    inv_freq = 1.0 / (10000.0 ** (jnp.arange(0, ROPE, 2, dtype=F32) / ROPE))
    ang = positions.astype(F32).reshape(S, 1) * inv_freq
    cos, sin = jnp.cos(ang), jnp.sin(ang)
    cos_t = jnp.concatenate([jnp.ones((S, NOPE), F32), cos, cos, jnp.ones((S, LANE - NOPE - ROPE), F32)], axis=1)
    sin_t = jnp.concatenate([jnp.zeros((S, NOPE), F32), -sin, sin, jnp.zeros((S, LANE - NOPE - ROPE), F32)], axis=1)
    return cos_t, sin_t


def _local_step(x, p, positions, target, w_in, fetch, send, sp, started):
    w_z, w_xbc, w_small = _prep_in(w_in.reshape(IN_WIDTH, D))
    cos_t, sin_t = _rope_tables(positions)
    prow = jnp.zeros((8, LANE), F32).at[0, :H].set(sp["dt_bias"][0]).at[1, :H].set(sp["A_log"][0]).at[2, :H].set(sp["D"][0])
    pcol = prow.T

    xb, pb = (x + started).astype(BF16), p.astype(BF16)
    z = _mm([(xb, w_z)], tb=True, name="proj_z")
    xbc = _mm([(xb, w_xbc)], tb=True, name="proj_xbc")
    small = _mm([(xb, w_small)], tb=True, name="proj_small")
    act = _conv_fwd(xbc, sp["conv_w"], sp["conv_b"])
    dt_t = small[:, SM_DT:SM_DT + LANE].T
    y, states = _ssd_fwd(act, small, dt_t, prow, pcol)
    y_ssd = _gate_norm_fwd(y, z, sp["ssd_norm"])
    gl = fetch("attn", y_ssd)
    w_q, w_k, w_v = _prep_attn(_from_cols(gl["w_qb"]), _from_cols(gl["w_kvb"]))
    qn, kvn, qcat, kcat, kcat_t, v = _qkv_fwd(small, w_q, w_k, w_v, sp["q_norm"], sp["kv_norm"], cos_t, sin_t)
    o, lse = _attn_fwd(qcat, kcat, v)
    y_mla = _rms_fwd(o, sp["out_norm"], name="out_norm_fwd")
    w_out = fetch("out", y_mla)["w_out"]
    w_out_s = w_out[:NCHIP // 2].reshape(SSD_INNER, D)
    w_out_m = w_out[NCHIP // 2:].reshape(SSD_INNER, D)
    mix = _mm([(y_ssd, w_out_s), (y_mla, w_out_m)], name="out_proj")
    h1, h1b = _ln_fwd(x, mix, sp["ln_mix_g"], sp["ln_mix_b"])
    gl = fetch("ffn", h1b)
    w_pg, w_pp = gl["w_pg"].reshape(D, D), _from_cols(gl["w_pp"])
    w_gate, w_up, w_down = gl["w_gate"], gl["w_up"], gl["w_down"]
    gate, up, actf = _ffn_hidden_fwd(h1b, w_gate, w_up)
    ffn = _mm([(actf, w_down)], chunk="sum", name="ffn_down")
    pg = _mm([(h1b, w_pg)], name="ple_gate")
    pp = _mm([(pb, w_pp)], name="ple_proj")
    dpre2, dpre2b, dpg, dpp, dg2, db2, loss_row = _final_fwd_bwd(h1, ffn, pg, pp, target, sp["ln_ffn_g"], sp["ln_ffn_b"])

    g = {"ln_ffn_g": dg2, "ln_ffn_b": db2}
    g["w_pp"] = _to_cols(_mm([(pb, dpp)], ta=True, out_dtype=BF16, name="d_w_ple_proj"))
    g["w_pg"] = _mm([(h1b, dpg)], ta=True, out_dtype=BF16, name="d_w_ple_gate").reshape(NCHIP, D // NCHIP, D)
    g["w_down"] = _mm([(actf, dpre2b)], ta=True, chunk="out", out_dtype=BF16, name="d_w_down")
    dgate, dup = _ffn_hidden_bwd(dpre2b, w_down, gate, up)
    g["w_gate"] = _mm([(dgate, h1b)], ta=True, chunk="out", out_dtype=BF16, name="d_w_gate")
    g["w_up"] = _mm([(dup, h1b)], ta=True, chunk="out", out_dtype=BF16, name="d_w_up")
    sent = send("ffn", {name: g.pop(name) for name in dict(ASYNC_GROUPS)["ffn"]})
    dh1 = _mm([(dpg, w_pg)], tb=True, add=dpre2, add_scale=ALPHA, name="d_h1_ple")
    dh1 = _mm([(dgate, w_gate), (dup, w_up)], chunk="sum", add=dh1, name="d_h1")
    dpre1, dpre1b, g["ln_mix_g"], g["ln_mix_b"] = _ln_bwd(x, mix, sp["ln_mix_g"] + sent, dh1)
    dy_ssd = _mm([(dpre1b, w_out_s)], tb=True, name="d_y_ssd")
    dy_mla = _mm([(dpre1b, w_out_m)], tb=True, name="d_y_mla")
    dw_out = jnp.concatenate([_mm([(y_ssd, dpre1b)], ta=True, out_dtype=BF16, name="d_w_out_s"),
                              _mm([(y_mla, dpre1b)], ta=True, out_dtype=BF16, name="d_w_out_m")], axis=0)
    sent = send("out", {"w_out": dw_out.reshape(NCHIP, 2 * SSD_INNER // NCHIP, D)})
    do, g["out_norm"] = _rms_bwd(o, sp["out_norm"] + sent, dy_mla, name="out_norm_bwd")
    dqt, dk, dv = _attn_bwd(qcat, kcat, kcat_t, v, do, _attn_rows(lse, o, do))
    dlatent, dqlin, dkb, g["q_norm"], g["kv_norm"] = _qkv_bwd(dqt, dk, dv, small, w_q, w_k, w_v, sp["q_norm"], sp["kv_norm"], cos_t, sin_t)
    dw_q = _mm([(qn, dqlin)], ta=True, out_dtype=BF16, name="d_w_q")
    dw_k = _mm([(kvn, dkb)], ta=True, out_dtype=BF16, name="d_w_k")
    dw_v = _mm([(kvn, dv)], ta=True, out_dtype=BF16, name="d_w_v")
    dw_qb = _to_cols(dw_q.reshape(Q_RANK, H, LANE)[:, :, :NOPE + ROPE].reshape(Q_RANK, H * (NOPE + ROPE)))
    dw_kvb = _to_cols(jnp.concatenate([dw_k.reshape(KV_RANK, H, LANE)[:, :, :NOPE], dw_v.reshape(KV_RANK, H, VDIM)],
                                       axis=2).reshape(KV_RANK, H * (NOPE + VDIM)))
    sent = send("attn", {"w_qb": dw_qb, "w_kvb": dw_kvb})
    dy, dz, g["ssd_norm"] = _gate_norm_bwd(y, z, sp["ssd_norm"] + sent, dy_ssd)
    dact, ddt, dprow = _ssd_bwd(act, small, dt_t, prow, pcol, states, dy)
    g["dt_bias"], g["A_log"], g["D"] = dprow[0:1, :H], dprow[1:2, :H], dprow[2:3, :H]
    dxbc, g["conv_w"], g["conv_b"] = _conv_bwd(xbc, sp["conv_w"], sp["conv_b"], dact)
    dsmall = jnp.concatenate([dlatent, ddt.astype(BF16)], axis=1)
    grad_x = _mm([(dz, w_z), (dxbc, w_xbc), (dsmall, w_small)], add=dpre1, add_scale=ALPHA, name="d_x")
    n_small = IN_WIDTH - _IN_DT
    dsm = jnp.concatenate([ddt[:, :H].astype(BF16), dlatent[:, :n_small - H], jnp.zeros((S, D - n_small), BF16)], axis=1)
    dw_in = _d_w_in(dz, dxbc, dsm, xb).reshape(NCHIP, IN_WIDTH // NCHIP * D // LANE, LANE)
    return loss_row, grad_x, dw_in, g


MESH = pl.DeviceIdType.MESH
BIG = (("w_in", (D, IN_WIDTH), 1), ("w_qb", (Q_RANK, H * (NOPE + ROPE)), 1), ("w_kvb", (KV_RANK, H * (NOPE + VDIM)), 1),
       ("w_out", (2 * SSD_INNER, D), 0), ("w_gate", (D, D_FF), 1), ("w_up", (D, D_FF), 1), ("w_down", (D_FF, D), 0),
       ("w_pg", (D, D), 0), ("w_pp", (PLE, D), 1))
CONV_SHARD = SSD_XBC // NCHIP
BF16_ROWS = 16


def _from_cols(stack):
    return jnp.concatenate([stack[k] for k in range(NCHIP)], axis=1)


def _to_cols(full):
    r, c4 = full.shape
    return full.reshape(r, NCHIP, c4 // NCHIP).transpose(1, 0, 2)


def _coords():
    return lax.axis_index("x"), lax.axis_index("y"), lax.axis_index("c")


def _peers():
    x, y, c = _coords()
    return 2 * x + y, c, [(1 - x, y), (x, 1 - y), (1 - x, 1 - y)], (x, y, 1 - c)


def _half_axis(shape):
    return 0 if shape[-2] % (2 * BF16_ROWS) == 0 else 1


def _half_shape(shape):
    r, c = shape[-2:]
    return (r // 2, c) if _half_axis(shape) == 0 else (r, c // 2)


def _half(core, shape):
    r, c = shape[-2:]
    if _half_axis(shape) == 0:
        return pl.ds(pl.multiple_of(core * (r // 2), BF16_ROWS), r // 2), slice(None)
    return slice(None), pl.ds(pl.multiple_of(core * (c // 2), LANE), c // 2)


def _gather_weights(shards):
    n_arr = len(shards)
    per = 2 * (NCHIP - 1)

    def body(*refs):
        ins, outs = refs[:n_arr], refs[n_arr:2 * n_arr]
        send_sems, recv_sems, local_sems = refs[2 * n_arr:]
        k, c, chips, sibling = _peers()

        def copy(idx, src, dst, to):
            return pltpu.make_async_remote_copy(src_ref=src, dst_ref=dst, send_sem=send_sems.at[idx], recv_sem=recv_sems.at[idx],
                                                device_id=to, device_id_type=MESH)

        def part(a, chip, core):
            return outs[a].at[chip, *_half(core, shards[a].shape)]

        mine = [pltpu.make_async_copy(ins[a], outs[a].at[k], local_sems.at[a]) for a in range(n_arr)]
        for cp in mine:
            cp.start()
        sends = []
        for a in range(n_arr):
            for j, (cx, cy) in enumerate(chips):
                sends.append(copy(per * a + j, ins[a].at[*_half(c, shards[a].shape)], part(a, k, c), (cx, cy, c)))
                sends[-1].start()
        for j, (cx, cy) in enumerate(chips):
            for a in range(n_arr):
                landed = part(a, 2 * cx + cy, c)
                copy(per * a + j, landed, landed, (cx, cy, c)).wait_recv()
                sends.append(copy(per * a + NCHIP - 1 + j, landed, landed, sibling))
                sends[-1].start()
        for j, (cx, cy) in enumerate(chips):
            for a in range(n_arr):
                other = part(a, 2 * cx + cy, 1 - c)
                copy(per * a + NCHIP - 1 + j, other, other, sibling).wait_recv()
        for cp in sends:
            cp.wait_send()
        for cp in mine:
            cp.wait()

    any_spec = pl.BlockSpec(memory_space=pl.ANY)
    return pl.pallas_call(
        body, name="gather_weights", in_specs=[any_spec] * n_arr, out_specs=[any_spec] * n_arr,
        out_shape=[jax.ShapeDtypeStruct((NCHIP,) + s.shape, s.dtype) for s in shards],
        scratch_shapes=[pltpu.SemaphoreType.DMA((per * n_arr,)), pltpu.SemaphoreType.DMA((per * n_arr,)),
                        pltpu.SemaphoreType.DMA((n_arr,))],
    )(*shards)


ASYNC_GROUPS = (("attn", ("w_qb", "w_kvb")), ("out", ("w_out",)), ("ffn", ("w_gate", "w_up", "w_down", "w_pg", "w_pp")))
TRANSPOSED = ("w_in", "w_gate", "w_up")
ROW_MAJOR = ("w_in",)
HBM_SPEC = pl.BlockSpec(memory_space=pltpu.HBM)
SEM_SPEC = pl.BlockSpec(memory_space=pltpu.SEMAPHORE)
IN_FLIGHT = pltpu.SideEffectType.DATAFLOW_SIDE_EFFECTING


def _in_hbm(a):
    return pltpu.with_memory_space_constraint(a, pltpu.HBM)


def _hbm_like(arrs, lead=()):
    return [pltpu.HBM(lead + a.shape, a.dtype) for a in arrs]


def _split_start(name, srcs, lands, after, n_sem, start):
    n = len(srcs)
    order = [] if after is None else [after]

    def body(*refs):
        src_refs, land_refs = refs[:n], refs[n:2 * n]
        send_sems, recv_sems = refs[2 * n + len(order)], refs[2 * n + len(order) + 1]
        token = refs[-1]

        def copy(send_idx, recv_idx, src, dst, to):
            return pltpu.make_async_remote_copy(src_ref=src, dst_ref=dst, send_sem=send_sems.at[send_idx],
                                                recv_sem=recv_sems.at[recv_idx], device_id=to, device_id_type=MESH)

        for cp in start(src_refs, land_refs, copy):
            cp.start()
        token[...] = jnp.zeros_like(token)

    sem = pltpu.SemaphoreType.DMA((n_sem,))
    outs = pl.pallas_call(
        body, name=name, in_specs=[HBM_SPEC] * (2 * n) + [pl.BlockSpec(memory_space=pl.ANY)] * len(order),
        out_specs=[SEM_SPEC, SEM_SPEC] + [HBM_SPEC] * (2 * n) + [pl.BlockSpec(memory_space=pltpu.VMEM)],
        out_shape=[sem, sem] + _hbm_like(srcs) + _hbm_like(lands) + [jax.ShapeDtypeStruct((8, LANE), F32)],
        input_output_aliases={i: 2 + i for i in range(2 * n)},
        compiler_params=pltpu.CompilerParams(has_side_effects=IN_FLIGHT),
    )(*[_in_hbm(a) for a in srcs], *[_in_hbm(a) for a in lands], *order)
    return (outs[0], outs[1], outs[2:2 + n], outs[2 + n:2 + 2 * n]), outs[-1]


def _split_wait(name, send_sems, recv_sems, srcs, lands, after, waits):
    n = len(srcs)

    def body(*refs):
        src_refs, land_refs = refs[:n], refs[n:2 * n]
        send_ref, recv_ref = refs[2 * n], refs[2 * n + 1]

        def copy(send_idx, recv_idx, src, dst, to):
            return pltpu.make_async_remote_copy(src_ref=src, dst_ref=dst, send_sem=send_ref.at[send_idx],
                                                recv_sem=recv_ref.at[recv_idx], device_id=to, device_id_type=MESH)

        for cp in waits(src_refs, land_refs, copy):
            cp.wait_send()
            cp.wait_recv()

    outs = pl.pallas_call(
        body, name=name, in_specs=[HBM_SPEC] * (2 * n) + [SEM_SPEC, SEM_SPEC, pl.BlockSpec(memory_space=pl.ANY)],
        out_specs=[HBM_SPEC] * (2 * n), out_shape=_hbm_like(srcs) + _hbm_like(lands),
        input_output_aliases={i: i for i in range(2 * n)},
        compiler_params=pltpu.CompilerParams(has_side_effects=IN_FLIGHT),
    )(*srcs, *lands, send_sems, recv_sems, after)
    return outs[:n], outs[n:]


GATHER_LATE_SEMS = 2 * (NCHIP - 1)


def _gather_async_start(tag, shards, after):
    def start(srcs, lands, copy):
        k, c, chips, _ = _peers()
        out = []
        for a, (src, dst) in enumerate(zip(srcs, lands)):
            for j, (cx, cy) in enumerate(chips):
                for core in range(2):
                    out.append(copy(GATHER_LATE_SEMS * a + 2 * j + core, GATHER_LATE_SEMS * a + 2 * j + c,
                                    src.at[*_half(c, src.shape)], dst.at[k, *_half(c, src.shape)], (cx, cy, core)))
        return out

    chip = 2 * lax.axis_index("x") + lax.axis_index("y")
    lands = [lax.dynamic_update_slice(lax.empty((NCHIP,) + s.shape, s.dtype), s[None], (chip, 0, 0)) for s in shards]
    return _split_start("gather_%s_start" % tag, shards, lands, after, GATHER_LATE_SEMS * len(shards), start)


def _gather_async_wait(tag, send_sems, recv_sems, shards, lands, after):
    def waits(srcs, lands_, copy):
        _, c, chips, _ = _peers()
        out = []
        for a, (src, dst) in enumerate(zip(srcs, lands_)):
            for j, (cx, cy) in enumerate(chips):
                for core in range(2):
                    idx = GATHER_LATE_SEMS * a + 2 * j + core
                    out.append(copy(idx, idx, src.at[*_half(c, src.shape)], dst.at[2 * cx + cy, *_half(core, src.shape)], (cx, cy, core)))
        return out

    return _split_wait("gather_%s_wait" % tag, send_sems, recv_sems, shards, lands, after, waits)[1]


def _other_devices():
    x, y, c = _coords()
    out = []
    for d in range(1, NDEV):
        tx, ty, tc = x ^ (d >> 2), y ^ ((d >> 1) & 1), c ^ (d & 1)
        out.append((d, (tx, ty, tc), 2 * tx + ty, 4 * tx + 2 * ty + tc))
    return out


def _reduce_async_start(tag, stacks, after):
    def start(srcs, lands, copy):
        x, y, c = _coords()
        me = 4 * x + 2 * y + c
        return [copy((NDEV - 1) * a + d - 1, (NDEV - 1) * a + d - 1, src.at[chip, *_half(to[2], src.shape)], dst.at[me], to)
                for a, (src, dst) in enumerate(zip(srcs, lands)) for d, to, chip, _ in _other_devices()]

    x, y, c = _coords()
    lands = []
    for s in stacks:
        hr, hc = _half_shape(s.shape)
        at = (c * hr, 0) if _half_axis(s.shape) == 0 else (0, c * hc)
        own = lax.dynamic_slice(s, (2 * x + y,) + at, (1, hr, hc))
        lands.append(lax.dynamic_update_slice(lax.empty((NDEV, hr, hc), s.dtype), own, (4 * x + 2 * y + c, 0, 0)))
    return _split_start("reduce_%s_start" % tag, stacks, lands, after, (NDEV - 1) * len(stacks), start)


def _reduce_async_wait(tag, send_sems, recv_sems, stacks, lands, after):
    def waits(srcs, lands_, copy):
        return [copy((NDEV - 1) * a + d - 1, (NDEV - 1) * a + d - 1, src.at[chip, *_half(to[2], src.shape)], dst.at[pos], to)
                for a, (src, dst) in enumerate(zip(srcs, lands_)) for d, to, chip, pos in _other_devices()]

    return _split_wait("reduce_%s_wait" % tag, send_sems, recv_sems, stacks, lands, after, waits)[1]


def _reduce_finish(tag, arrived, dims):
    n_arr = len(arrived)

    def body(*refs):
        lands, fin = refs[:n_arr], refs[n_arr:2 * n_arr]
        send_sems, recv_sems = refs[2 * n_arr:]
        _, c, _, sibling = _peers()
        sends = []
        for a in range(n_arr):
            mine = fin[a].at[*_half(c, dims[a])]

            def device_sum(vs, vf, a=a, mine=mine):
                pltpu.sync_copy(lands[a], vs)
                acc = vs[0].astype(F32)
                for i in range(1, NDEV):
                    acc = acc + vs[i].astype(F32)
                vf[...] = acc
                pltpu.sync_copy(vf, mine)

            pl.run_scoped(device_sum, pltpu.VMEM((NDEV,) + _half_shape(dims[a]), BF16), pltpu.VMEM(_half_shape(dims[a]), F32))
            sends.append(pltpu.make_async_remote_copy(src_ref=mine, dst_ref=mine, send_sem=send_sems.at[a], recv_sem=recv_sems.at[a],
                                                      device_id=sibling, device_id_type=MESH))
            sends[-1].start()
        for a in range(n_arr):
            other = fin[a].at[*_half(1 - c, dims[a])]
            pltpu.make_async_remote_copy(src_ref=other, dst_ref=other, send_sem=send_sems.at[a], recv_sem=recv_sems.at[a],
                                         device_id=sibling, device_id_type=MESH).wait_recv()
        for cp in sends:
            cp.wait_send()

    any_spec = pl.BlockSpec(memory_space=pl.ANY)
    return pl.pallas_call(
        body, name="reduce_%s_finish" % tag, in_specs=[any_spec] * n_arr, out_specs=[any_spec] * n_arr,
        out_shape=[jax.ShapeDtypeStruct(d, F32) for d in dims],
        scratch_shapes=[pltpu.SemaphoreType.DMA((n_arr,)), pltpu.SemaphoreType.DMA((n_arr,))],
    )(*arrived)


SMALL = (("conv_w", SSD_K * SSD_XBC), ("conv_b", SSD_XBC), ("dt_bias", H), ("A_log", H), ("D", H), ("ssd_norm", SSD_INNER),
         ("q_norm", Q_RANK), ("kv_norm", KV_RANK), ("out_norm", SSD_INNER), ("ln_mix_g", D), ("ln_mix_b", D),
         ("ln_ffn_g", D), ("ln_ffn_b", D))
SMALL_ROWS = 120
NDEV = 8


def _allreduce_small(sv):
    def body(sv_ref, out_ref, slots, send_sems, recv_sems):
        x, y, c = _coords()
        me = 4 * x + 2 * y + c
        slots[me] = sv_ref[...]
        copies = []
        for d in range(1, NDEV):
            to = (x ^ (d >> 2), y ^ ((d >> 1) & 1), c ^ (d & 1))
            copies.append(pltpu.make_async_remote_copy(src_ref=sv_ref, dst_ref=slots.at[me], send_sem=send_sems.at[d - 1],
                                                       recv_sem=recv_sems.at[d - 1], device_id=to, device_id_type=MESH))
            copies[-1].start()
        for cp in copies:
            cp.wait_recv()
        for cp in copies:
            cp.wait_send()
        acc = slots[0]
        for i in range(1, NDEV):
            acc = acc + slots[i]
        out_ref[...] = acc

    vm = pl.BlockSpec(memory_space=pltpu.VMEM)
    return pl.pallas_call(
        body, name="allreduce_small", in_specs=[vm], out_specs=vm, out_shape=jax.ShapeDtypeStruct((SMALL_ROWS, LANE), F32),
        scratch_shapes=[pltpu.VMEM((NDEV, SMALL_ROWS, LANE), F32), pltpu.SemaphoreType.DMA((NDEV - 1,)),
                        pltpu.SemaphoreType.DMA((NDEV - 1,))],
    )(sv)


def _adamw_math(w, g, m, v):
    m2 = ADAM_B1 * m + (1.0 - ADAM_B1) * g
    v2 = ADAM_B2 * v + (1.0 - ADAM_B2) * (g * g)
    m_hat = m2 / (1.0 - ADAM_B1 ** ADAM_STEP)
    v_hat = v2 / (1.0 - ADAM_B2 ** ADAM_STEP)
    return -ADAM_LR * (m_hat / (jnp.sqrt(v_hat) + ADAM_EPS) + ADAM_WD * w), m2, v2


ADAM_BLOCK_BYTES = 2 * 1024 * 1024


def _adamw_big(w, g, m, v, *, name):
    r, c = w.shape

    def body(w_ref, g_ref, m_ref, v_ref, d_ref, m2_ref, v2_ref):
        d_ref[...], m2_ref[...], v2_ref[...] = _adamw_math(w_ref[...], g_ref[...], m_ref[...], v_ref[...])

    tr = max(t for t in range(8, r + 1, 8) if r % t == 0 and t * c * 4 <= ADAM_BLOCK_BYTES)
    steps, spec = r // tr, pl.BlockSpec((tr, c), lambda i: (i, 0))
    return pl.pallas_call(body, name=name, grid=(steps,), in_specs=[spec] * 4, out_specs=[spec] * 3,
                          out_shape=[jax.ShapeDtypeStruct((r, c), F32)] * 3)(w, g, m, v)


def _adamw_small(ws, gs, ms, vs):
    n = len(ws)

    def body(*refs):
        for i in range(n):
            w_ref, g_ref, m_ref, v_ref = (refs[j * n + i] for j in range(4))
            d_ref, m2_ref, v2_ref = (refs[(4 + j) * n + i] for j in range(3))
            d_ref[...], m2_ref[...], v2_ref[...] = _adamw_math(w_ref[...], g_ref[...], m_ref[...], v_ref[...])

    vm = pl.BlockSpec(memory_space=pltpu.VMEM)
    shapes = [jax.ShapeDtypeStruct(w.shape, F32) for w in ws]
    outs = pl.pallas_call(body, name="adamw_small", in_specs=[vm] * (4 * n), out_specs=[vm] * (3 * n), out_shape=shapes * 3)(
        *ws, *gs, *ms, *vs)
    return outs[:n], outs[n:2 * n], outs[2 * n:]


_SMALL_ARG = {"conv_w": "ssd_conv_w", "conv_b": "ssd_conv_b", "dt_bias": "ssd_dt_bias", "A_log": "ssd_A_log", "D": "ssd_D",
              "ssd_norm": "ssd_norm_w", "q_norm": "mla_q_norm_w", "kv_norm": "mla_kv_norm_w", "out_norm": "mla_out_norm_w",
              "ln_mix_g": "ln_mix_g", "ln_mix_b": "ln_mix_b", "ln_ffn_g": "ln_ffn_g", "ln_ffn_b": "ln_ffn_b"}
_BIG_ARG = {"w_in": "w_in", "w_qb": "mla_w_q_b", "w_kvb": "mla_w_kv_b", "w_out": "w_out", "w_gate": "w_ffn_gate",
            "w_up": "w_ffn_up", "w_down": "w_ffn_down", "w_pg": "w_ple_gate", "w_pp": "w_ple_proj"}
_WEIGHT_ORDER = ("w_in", "ssd_conv_w", "ssd_conv_b", "ssd_dt_bias", "ssd_A_log", "ssd_D", "ssd_norm_w", "mla_q_norm_w", "mla_w_q_b",
                 "mla_kv_norm_w", "mla_w_kv_b", "mla_out_norm_w", "w_out", "ln_mix_g", "ln_mix_b", "w_ffn_gate", "w_ffn_up",
                 "w_ffn_down", "w_ple_gate", "w_ple_proj", "ln_ffn_g", "ln_ffn_b")


def _rows128(a):
    flat = a.reshape(-1)
    return jnp.pad(flat, (0, -flat.shape[0] % LANE)).reshape(-1, LANE)


def kernel(x, p, positions, w_in, ssd_conv_w, ssd_conv_b, ssd_dt_bias, ssd_A_log, ssd_D, ssd_norm_w, mla_q_norm_w, mla_w_q_b, mla_kv_norm_w, mla_w_kv_b, mla_out_norm_w, w_out, ln_mix_g, ln_mix_b, w_ffn_gate, w_ffn_up, w_ffn_down, w_ple_gate, w_ple_proj, ln_ffn_g, ln_ffn_b, loss_target, m_w_in, m_ssd_conv_w, m_ssd_conv_b, m_ssd_dt_bias, m_ssd_A_log, m_ssd_D, m_ssd_norm_w, m_mla_q_norm_w, m_mla_w_q_b, m_mla_kv_norm_w, m_mla_w_kv_b, m_mla_out_norm_w, m_w_out, m_ln_mix_g, m_ln_mix_b, m_w_ffn_gate, m_w_ffn_up, m_w_ffn_down, m_w_ple_gate, m_w_ple_proj, m_ln_ffn_g, m_ln_ffn_b, v_w_in, v_ssd_conv_w, v_ssd_conv_b, v_ssd_dt_bias, v_ssd_A_log, v_ssd_D, v_ssd_norm_w, v_mla_q_norm_w, v_mla_w_q_b, v_mla_kv_norm_w, v_mla_w_kv_b, v_mla_out_norm_w, v_w_out, v_ln_mix_g, v_ln_mix_b, v_w_ffn_gate, v_w_ffn_up, v_w_ffn_down, v_w_ple_gate, v_w_ple_proj, v_ln_ffn_g, v_ln_ffn_b):
    given = dict(locals())
    chip = 2 * lax.axis_index("x") + lax.axis_index("y")

    def local(name, prefix=""):
        a = given[prefix + _BIG_ARG[name]][0]
        return a.T if name in TRANSPOSED else a

    def updated(name, prefix=""):
        if name in ROW_MAJOR:
            _, c, r = given[prefix + _BIG_ARG[name]].shape
            return given[prefix + _BIG_ARG[name]].reshape(c // LANE, LANE, r).transpose(2, 0, 1).reshape(-1, LANE)
        return local(name, prefix)

    def global_layout(name, arr):
        if name in ROW_MAJOR:
            r, c = local(name).shape
            return arr.reshape(r, c // LANE, LANE).transpose(1, 2, 0).reshape(1, c, r)
        return (arr.T if name in TRANSPOSED else arr)[None]

    conv_bits = lax.bitcast_convert_type(ssd_conv_w[0], BF16).reshape(SSD_K, 2 * CONV_SHARD)
    w_in_all, conv_all = _gather_weights([local("w_in").astype(BF16), jnp.pad(conv_bits, ((0, BF16_ROWS - SSD_K), (0, 0)))])
    sp = {k: given[a] for k, a in _SMALL_ARG.items() if k != "conv_w"}
    sp["conv_w"] = _from_cols(lax.bitcast_convert_type(conv_all[:, :SSD_K].reshape(NCHIP, SSD_K, CONV_SHARD, 2), F32))
    gathering, tie = {}, w_in_all
    for group, names in ASYNC_GROUPS:
        gathering[group], tie = _gather_async_start(group, [local(name).astype(BF16) for name in names], tie)

    def fetch(group, after):
        return dict(zip(dict(ASYNC_GROUPS)[group], _gather_async_wait(group, *gathering[group], after)))

    reducing = {}

    def send(group, grads):
        reducing[group], sent = _reduce_async_start(group, [grads[name] for name in dict(ASYNC_GROUPS)[group]], None)
        return sent[0, 0]

    loss_row, grad_x, dw_in, g = _local_step(x[0], p[0, 0], positions[0], loss_target[0], w_in_all, fetch, send, sp, tie[0, 0])

    reducing["in"], tie = _reduce_async_start("in", [dw_in], grad_x)
    gbig = {}
    for group, names in reversed(ASYNC_GROUPS):
        arrived = _reduce_async_wait(group, *reducing[group], tie)
        gbig.update(zip(names, _reduce_finish(group, arrived, [local(name).shape for name in names])))
    small_in = jnp.concatenate([_rows128(g[name]) for name, _ in SMALL] + [loss_row], axis=0)
    small_sum = _allreduce_small(jnp.pad(small_in, ((0, SMALL_ROWS - small_in.shape[0]), (0, 0))))
    gsmall, row = {}, 0
    for name, size in SMALL:
        nrow = -(-size // LANE)
        gsmall[name] = small_sum[row:row + nrow].reshape(-1)[:size]
        row += nrow
    loss = small_sum[row, 0]

    grads = {_BIG_ARG[name]: global_layout(name, arr) for name, arr in gbig.items()}
    for name, _ in SMALL:
        if name == "conv_w":
            full_g = gsmall[name].reshape(SSD_K, SSD_XBC)
            grads["ssd_conv_w"] = lax.dynamic_slice(full_g, (0, chip * CONV_SHARD), (SSD_K, CONV_SHARD))[None]
        else:
            grads[_SMALL_ARG[name]] = gsmall[name].reshape(given[_SMALL_ARG[name]].shape)

    delta, new_m, new_v = {}, {}, {}

    def update_matrix(name, grad):
        a = _BIG_ARG[name]
        d, m2, v2 = _adamw_big(updated(name), grad, updated(name, "m_"), updated(name, "v_"), name="adamw_" + a)
        delta[a], new_m[a], new_v[a] = (global_layout(name, t) for t in (d, m2, v2))
        return d

    for name, grad in gbig.items():
        last = update_matrix(name, grad)
    g_in = _reduce_finish("in", _reduce_async_wait("in", *reducing["in"], last), [updated("w_in").shape])[0]
    grads["w_in"] = global_layout("w_in", g_in)
    update_matrix("w_in", g_in)
    small_names = [_SMALL_ARG[name] for name, _ in SMALL]
    two_d = lambda t: t.reshape(t.shape[-2], t.shape[-1])
    ds, ms, vs = _adamw_small([two_d(given[a]) for a in small_names], [two_d(grads[a]) for a in small_names],
                              [two_d(given["m_" + a]) for a in small_names], [two_d(given["v_" + a]) for a in small_names])
    for a, d, m2, v2 in zip(small_names, ds, ms, vs):
        delta[a], new_m[a], new_v[a] = (t.reshape(given[a].shape) for t in (d, m2, v2))

    return (loss, grad_x[None], *[grads[n] for n in _WEIGHT_ORDER], *[delta[n] for n in _WEIGHT_ORDER],
            *[new_m[n] for n in _WEIGHT_ORDER], *[new_v[n] for n in _WEIGHT_ORDER])
```

```python
import functools
import math

import jax
import jax.numpy as jnp
from jax import lax
from jax.experimental import pallas as pl
from jax.experimental.pallas import tpu as pltpu

F32 = jnp.float32
BF16 = jnp.bfloat16

S = 2048
D = 1024
PLE = 256
H = 16
SSD_P = 64
SSD_INNER = 1024
SSD_N = 128
SSD_G = 2
SSD_L = 128
SSD_NC = S // SSD_L
SSD_XBC = 1536
SSD_K = 4
Q_RANK = 384
KV_RANK = 256
NOPE = 64
ROPE = 32
VDIM = 64
D_FF = 2816
IN_WIDTH = 3248
ALPHA = 2.0 ** 0.25
EPS_RMS = 1e-6
EPS_LN = 1e-5
ATT_SCALE = 1.0 / math.sqrt(NOPE + ROPE)
LN2 = math.log(2.0)
ATT_SCALE_LOG2 = ATT_SCALE / LN2
LANE = 128
NCHIP = 4
SMALL_W = 896
SM_Q, SM_KV, SM_KR, SM_DT = 0, 384, 640, 768
NEG = -1e30

ADAM_LR = 0.001
ADAM_B1 = 0.9
ADAM_B2 = 0.999
ADAM_EPS = 1e-08
ADAM_WD = 0.01
ADAM_STEP = 10


def _sigmoid(v):
    return 1.0 / (1.0 + jnp.exp(-v))


MM_VMEM_BUDGET = 36 * 2 ** 20
MM_MAX_ACC = 2048 * 1024


def _mm_tiles(pairs, ks, m, n, out_dtype, has_add):
    def divs(v):
        return [LANE * d for d in range(v // LANE, 0, -1) if (v // LANE) % d == 0] if v % LANE == 0 else [v]

    def cost(tm, tn):
        tot = tm * tn * (jnp.dtype(out_dtype).itemsize + (4 if has_add else 0))
        for (a, b), k in zip(pairs, ks):
            tot += k * (tm * a.dtype.itemsize + tn * b.dtype.itemsize)
        return 2 * tot

    ok = [(tm * tn, tm, tn) for tm in divs(m) for tn in divs(n) if tm * tn <= MM_MAX_ACC and cost(tm, tn) <= MM_VMEM_BUDGET]
    _, tm, tn = max(ok)
    return tm, tn


def _mm(pairs, *, ta=False, tb=False, out_dtype=F32, add=None, add_scale=1.0, chunk=None, name):
    n_pairs = len(pairs)
    windows = [pr[2] if len(pr) == 3 else None for pr in pairs]
    pairs = [pr[:2] for pr in pairs]
    assert not ((ta or tb) and any(windows))
    ks = [w[2] if w else (a.shape[-2] if ta else a.shape[-1]) for (a, _), w in zip(pairs, windows)]
    a0, b0 = pairs[0]
    m = a0.shape[-1] if ta else a0.shape[-2]
    n = b0.shape[-2] if tb else b0.shape[-1]
    tm, tn = _mm_tiles(pairs, ks, m, n, out_dtype, add is not None)
    dims = (((0 if ta else 1,), (1 if tb else 0,)), ((), ()))
    nk = NCHIP if chunk else 1
    assert chunk != "sum" or out_dtype == F32

    def body(*refs):
        o_ref = refs[-1]
        acc = None
        for i in range(n_pairs):
            a = refs[2 * i][...].astype(BF16)
            b = refs[2 * i + 1][...].astype(BF16)
            part = lax.dot_general(a, b, dims, preferred_element_type=F32)
            acc = part if acc is None else acc + part
        if chunk == "sum":
            k = pl.program_id(2)

            @pl.when(k == 0)
            def _():
                o_ref[...] = acc + add_scale * refs[2 * n_pairs][...] if add is not None else acc

            @pl.when(k > 0)
            def _():
                o_ref[...] += acc
        else:
            if add is not None:
                acc = acc + add_scale * refs[2 * n_pairs][...]
            o_ref[...] = acc.astype(out_dtype)

    def spec(arr, shape, idx2):
        if arr.ndim == 3:
            return pl.BlockSpec((None,) + shape, lambda i, j, k: (k,) + idx2(i, j))
        return pl.BlockSpec(shape, lambda i, j, k: idx2(i, j))

    in_specs, args = [], []
    for (a, b), kdim, window in zip(pairs, ks, windows):
        ka, kb = window[:2] if window else (0, 0)
        in_specs.append(spec(a, (kdim, tm), lambda i, j: (0, i)) if ta else spec(a, (tm, kdim), lambda i, j, ka=ka: (i, ka)))
        in_specs.append(spec(b, (tn, kdim), lambda i, j: (j, 0)) if tb else spec(b, (kdim, tn), lambda i, j, kb=kb: (kb, j)))
        args += [a, b]
    if add is not None:
        in_specs.append(pl.BlockSpec((tm, tn), lambda i, j, k: (i, j)))
        args.append(add)
    if chunk == "out":
        out_spec = pl.BlockSpec((None, tm, tn), lambda i, j, k: (k, i, j))
        out_shape = jax.ShapeDtypeStruct((nk, m, n), out_dtype)
    else:
        out_spec = pl.BlockSpec((tm, tn), lambda i, j, k: (i, j))
        out_shape = jax.ShapeDtypeStruct((m, n), out_dtype)
    return pl.pallas_call(
        body, name=name, grid=(m // tm, n // tn, nk), in_specs=in_specs, out_specs=out_spec, out_shape=out_shape,
        compiler_params=pltpu.CompilerParams(dimension_semantics=("parallel", "parallel", "arbitrary")),
    )(*args)


TR = 256


def _row_spec(c):
    return pl.BlockSpec((TR, c), lambda i: (i, 0))


def _vec_spec(c):
    return pl.BlockSpec((1, c), lambda i: (0, 0))


def _acc_rows(ref, val):
    @pl.when(pl.program_id(0) == 0)
    def _():
        ref[...] = jnp.zeros_like(ref)
    ref[...] += val


def _rms_fwd(u, w, *, name):
    c = u.shape[1]

    def body(u_ref, w_ref, o_ref):
        v = u_ref[...]
        r = lax.rsqrt(jnp.mean(v * v, axis=-1, keepdims=True) + EPS_RMS)
        o_ref[...] = (v * r * w_ref[...]).astype(BF16)

    return pl.pallas_call(body, name=name, grid=(S // TR,), in_specs=[_row_spec(c), _vec_spec(c)], out_specs=_row_spec(c),
                          out_shape=jax.ShapeDtypeStruct((S, c), BF16))(u, w)


def _rms_bwd(u, w, dy, *, name):
    c = u.shape[1]

    def body(u_ref, w_ref, dy_ref, du_ref, dw_ref):
        v = u_ref[...]
        g = dy_ref[...].astype(F32)
        r = lax.rsqrt(jnp.mean(v * v, axis=-1, keepdims=True) + EPS_RMS)
        gw = g * w_ref[...]
        du_ref[...] = r * gw - v * (r * r * r * jnp.mean(gw * v, axis=-1, keepdims=True))
        _acc_rows(dw_ref, jnp.sum(g * v * r, axis=0, keepdims=True))

    return pl.pallas_call(body, name=name, grid=(S // TR,), in_specs=[_row_spec(c), _vec_spec(c), _row_spec(c)],
                          out_specs=[_row_spec(c), _vec_spec(c)],
                          out_shape=[jax.ShapeDtypeStruct((S, c), F32), jax.ShapeDtypeStruct((1, c), F32)])(u, w, dy)


def _gate_norm_fwd(y, z, w):
    def body(y_ref, z_ref, w_ref, o_ref):
        zz = z_ref[...]
        v = y_ref[...] * (zz * _sigmoid(zz))
        r = lax.rsqrt(jnp.mean(v * v, axis=-1, keepdims=True) + EPS_RMS)
        o_ref[...] = (v * r * w_ref[...]).astype(BF16)

    c = SSD_INNER
    return pl.pallas_call(body, name="ssd_gate_norm_fwd", grid=(S // TR,), in_specs=[_row_spec(c), _row_spec(c), _vec_spec(c)],
                          out_specs=_row_spec(c), out_shape=jax.ShapeDtypeStruct((S, c), BF16))(y, z, w)


def _gate_norm_bwd(y, z, w, dout):
    def body(y_ref, z_ref, w_ref, g_ref, dy_ref, dz_ref, dw_ref):
        yy = y_ref[...]
        zz = z_ref[...]
        sg = _sigmoid(zz)
        sz = zz * sg
        v = yy * sz
        g = g_ref[...]
        r = lax.rsqrt(jnp.mean(v * v, axis=-1, keepdims=True) + EPS_RMS)
        gw = g * w_ref[...]
        dv = r * gw - v * (r * r * r * jnp.mean(gw * v, axis=-1, keepdims=True))
        dy_ref[...] = dv * sz
        dz_ref[...] = (dv * yy * (sg * (1.0 + zz * (1.0 - sg)))).astype(BF16)
        _acc_rows(dw_ref, jnp.sum(g * v * r, axis=0, keepdims=True))

    c = SSD_INNER
    return pl.pallas_call(body, name="ssd_gate_norm_bwd", grid=(S // TR,),
                          in_specs=[_row_spec(c), _row_spec(c), _vec_spec(c), _row_spec(c)],
                          out_specs=[_row_spec(c), _row_spec(c), _vec_spec(c)],
                          out_shape=[jax.ShapeDtypeStruct((S, c), F32), jax.ShapeDtypeStruct((S, c), BF16),
                                     jax.ShapeDtypeStruct((1, c), F32)])(y, z, w, dout)


def _ln_fwd(xr, mix, g, b):
    def body(x_ref, m_ref, g_ref, b_ref, o_ref, ob_ref):
        pre = ALPHA * x_ref[...] + m_ref[...]
        mu = jnp.mean(pre, axis=-1, keepdims=True)
        d = pre - mu
        rs = lax.rsqrt(jnp.mean(d * d, axis=-1, keepdims=True) + EPS_LN)
        h = d * rs * g_ref[...] + b_ref[...]
        o_ref[...] = h
        ob_ref[...] = h.astype(BF16)

    return pl.pallas_call(body, name="ln_mix_fwd", grid=(S // TR,), in_specs=[_row_spec(D), _row_spec(D), _vec_spec(D), _vec_spec(D)],
                          out_specs=[_row_spec(D)] * 2,
                          out_shape=[jax.ShapeDtypeStruct((S, D), F32), jax.ShapeDtypeStruct((S, D), BF16)])(xr, mix, g, b)


def _ln_bwd(xr, mix, g, dh):
    def body(x_ref, m_ref, g_ref, dh_ref, dpre_ref, dpreb_ref, dg_ref, db_ref):
        pre = ALPHA * x_ref[...] + m_ref[...]
        mu = jnp.mean(pre, axis=-1, keepdims=True)
        d = pre - mu
        rs = lax.rsqrt(jnp.mean(d * d, axis=-1, keepdims=True) + EPS_LN)
        xh = d * rs
        dy = dh_ref[...]
        gy = dy * g_ref[...]
        dpre = rs * (gy - jnp.mean(gy, axis=-1, keepdims=True) - xh * jnp.mean(gy * xh, axis=-1, keepdims=True))
        dpre_ref[...] = dpre
        dpreb_ref[...] = dpre.astype(BF16)
        _acc_rows(dg_ref, jnp.sum(dy * xh, axis=0, keepdims=True))
        _acc_rows(db_ref, jnp.sum(dy, axis=0, keepdims=True))

    return pl.pallas_call(body, name="ln_mix_bwd", grid=(S // TR,),
                          in_specs=[_row_spec(D), _row_spec(D), _vec_spec(D), _row_spec(D)],
                          out_specs=[_row_spec(D), _row_spec(D), _vec_spec(D), _vec_spec(D)],
                          out_shape=[jax.ShapeDtypeStruct((S, D), F32), jax.ShapeDtypeStruct((S, D), BF16),
                                     jax.ShapeDtypeStruct((1, D), F32), jax.ShapeDtypeStruct((1, D), F32)])(xr, mix, g, dh)


FF_CHUNK = D_FF // NCHIP


FF_ROWS = 1024


def _ff_act_spec():
    return pl.BlockSpec((None, FF_ROWS, FF_CHUNK), lambda i, k: (k, i, 0))


def _ff_w_spec():
    return pl.BlockSpec((None, FF_CHUNK, D), lambda i, k: (k, 0, 0))


def _ffn_hidden_fwd(h, w_gate_t, w_up_t):
    def body(h_ref, wg_ref, wu_ref, g_ref, u_ref, a_ref):
        hh = h_ref[...]
        g = _dot(hh, wg_ref[...], ((1,), (1,)))
        u = _dot(hh, wu_ref[...], ((1,), (1,)))
        g_ref[...] = g.astype(BF16)
        u_ref[...] = u.astype(BF16)
        a_ref[...] = (g * _sigmoid(g) * u).astype(BF16)

    return pl.pallas_call(
        body, name="ffn_hidden_fwd", grid=(S // FF_ROWS, NCHIP),
        in_specs=[pl.BlockSpec((FF_ROWS, D), lambda i, k: (i, 0)), _ff_w_spec(), _ff_w_spec()], out_specs=[_ff_act_spec()] * 3,
        out_shape=[jax.ShapeDtypeStruct((NCHIP, S, FF_CHUNK), BF16)] * 3,
        compiler_params=pltpu.CompilerParams(dimension_semantics=("parallel", "parallel")),
    )(h, w_gate_t, w_up_t)


def _ffn_hidden_bwd(dout, w_down, gate, up):
    def body(d_ref, wd_ref, g_ref, u_ref, dg_ref, du_ref):
        d = _dot(d_ref[...], wd_ref[...], ((1,), (1,)))
        g = g_ref[...].astype(F32)
        sg = _sigmoid(g)
        dg_ref[...] = (d * u_ref[...].astype(F32) * (sg * (1.0 + g * (1.0 - sg)))).astype(BF16)
        du_ref[...] = (d * g * sg).astype(BF16)

    return pl.pallas_call(
        body, name="ffn_hidden_bwd", grid=(S // FF_ROWS, NCHIP),
        in_specs=[pl.BlockSpec((FF_ROWS, D), lambda i, k: (i, 0)), _ff_w_spec(), _ff_act_spec(), _ff_act_spec()],
        out_specs=[_ff_act_spec()] * 2, out_shape=[jax.ShapeDtypeStruct((NCHIP, S, FF_CHUNK), BF16)] * 2,
        compiler_params=pltpu.CompilerParams(dimension_semantics=("parallel", "parallel")),
    )(dout, w_down, gate, up)


def _final_fwd_bwd(h1, ffn, pg, pp, target, g2, b2):
    def body(h_ref, f_ref, pg_ref, pp_ref, t_ref, g_ref, b_ref, dpre_ref, dpreb_ref, dpg_ref, dpp_ref, dg_ref, db_ref, loss_ref):
        sg = _sigmoid(pg_ref[...])
        ppv = pp_ref[...]
        pre = ALPHA * h_ref[...] + f_ref[...] + sg * ppv
        mu = jnp.mean(pre, axis=-1, keepdims=True)
        d = pre - mu
        rs = lax.rsqrt(jnp.mean(d * d, axis=-1, keepdims=True) + EPS_LN)
        xh = d * rs
        err = xh * g_ref[...] + b_ref[...] - t_ref[...]
        dy = err * (1.0 / D)
        gy = dy * g_ref[...]
        dpre = rs * (gy - jnp.mean(gy, axis=-1, keepdims=True) - xh * jnp.mean(gy * xh, axis=-1, keepdims=True))
        dpre_ref[...] = dpre
        dpreb_ref[...] = dpre.astype(BF16)
        dpg_ref[...] = (dpre * ppv * sg * (1.0 - sg)).astype(BF16)
        dpp_ref[...] = (dpre * sg).astype(BF16)
        _acc_rows(dg_ref, jnp.sum(dy * xh, axis=0, keepdims=True))
        _acc_rows(db_ref, jnp.sum(dy, axis=0, keepdims=True))
        _acc_rows(loss_ref, 0.5 * jnp.sum(jnp.mean(err * err, axis=-1, keepdims=True), axis=0, keepdims=True) * jnp.ones((1, LANE), F32))

    return pl.pallas_call(
        body, name="final_ln_loss", grid=(S // TR,),
        in_specs=[_row_spec(D)] * 5 + [_vec_spec(D)] * 2,
        out_specs=[_row_spec(D)] * 4 + [_vec_spec(D), _vec_spec(D), _vec_spec(LANE)],
        out_shape=[jax.ShapeDtypeStruct((S, D), F32)] + [jax.ShapeDtypeStruct((S, D), BF16)] * 3 + [
                   jax.ShapeDtypeStruct((1, D), F32), jax.ShapeDtypeStruct((1, D), F32), jax.ShapeDtypeStruct((1, LANE), F32)],
    )(h1, ffn, pg, pp, target, g2, b2)


def _rot(u, cos_t, sin_t, lane):
    partner = jnp.where(lane < NOPE + ROPE // 2, pltpu.roll(u, LANE - ROPE // 2, 1), pltpu.roll(u, ROPE // 2, 1))
    return u * cos_t + partner * sin_t


def _rms(v, w):
    r = lax.rsqrt(jnp.mean(v * v, axis=-1, keepdims=True) + EPS_RMS)
    return v * r * w, r


def _rms_grad(v, r, w, g):
    gw = g * w
    return r * gw - v * (r * r * r * jnp.mean(gw * v, axis=-1, keepdims=True)), jnp.sum(g * v * r, axis=0, keepdims=True)


def _whole(arr):
    return pl.BlockSpec(arr.shape, lambda i: (0,) * arr.ndim)


def _qkv_fwd(small, w_q, w_k, w_v, q_norm, kv_norm, cos_t, sin_t):
    def body(sm_ref, wq_ref, wk_ref, wv_ref, qw_ref, kw_ref, c_ref, s_ref, qn_ref, kvn_ref, q_ref, k_ref, kt_ref, v_ref):
        lane = lax.broadcasted_iota(jnp.int32, (TR, LANE), 1)
        c, s = c_ref[...], s_ref[...]
        qn = _rms(sm_ref[:, SM_Q:SM_Q + Q_RANK], qw_ref[...])[0].astype(BF16)
        kvn = _rms(sm_ref[:, SM_KV:SM_KV + KV_RANK], kw_ref[...])[0].astype(BF16)
        qn_ref[...] = qn
        kvn_ref[...] = kvn
        kr = _rot(pltpu.roll(sm_ref[:, SM_KR:SM_KR + LANE], NOPE, 1), c, s, lane)
        for h in range(H):
            tile = slice(h * LANE, (h + 1) * LANE)
            q_ref[:, tile] = _rot(_dot(qn, wq_ref[:, tile], ((1,), (0,))), c, s, lane).astype(BF16)
            kt = _dot(kvn, wk_ref[:, tile], ((1,), (0,))) + kr
            k_ref[:, tile] = kt.astype(BF16)
            kt_ref[tile, :] = kt.T.astype(BF16)
        v_ref[...] = _dot(kvn, wv_ref[...], ((1,), (0,))).astype(BF16)

    w = H * LANE
    return pl.pallas_call(
        body, name="qkv_fwd", grid=(S // TR,),
        in_specs=[_row_spec(SMALL_W), _whole(w_q), _whole(w_k), _whole(w_v), _vec_spec(Q_RANK), _vec_spec(KV_RANK), _row_spec(LANE), _row_spec(LANE)],
        out_specs=[_row_spec(Q_RANK), _row_spec(KV_RANK), _row_spec(w), _row_spec(w), pl.BlockSpec((w, TR), lambda i: (0, i)),
                   _row_spec(H * VDIM)],
        out_shape=[jax.ShapeDtypeStruct((S, Q_RANK), BF16), jax.ShapeDtypeStruct((S, KV_RANK), BF16), jax.ShapeDtypeStruct((S, w), BF16),
                   jax.ShapeDtypeStruct((S, w), BF16), jax.ShapeDtypeStruct((w, S), BF16), jax.ShapeDtypeStruct((S, H * VDIM), BF16)],
    )(small, w_q, w_k, w_v, q_norm, kv_norm, cos_t, sin_t)


def _qkv_bwd(dqt, dk, dv, small, w_q, w_k, w_v, q_norm, kv_norm, cos_t, sin_t):
    def body(dq_ref, dk_ref, dv_ref, sm_ref, wq_ref, wk_ref, wv_ref, qw_ref, kw_ref, c_ref, s_ref,
             ds_ref, dql_ref, dkb_ref, dqw_ref, dkw_ref):
        lane = lax.broadcasted_iota(jnp.int32, (TR, LANE), 1)
        c, s = c_ref[...], -s_ref[...]
        dqn = jnp.zeros((TR, Q_RANK), F32)
        dkvn = _dot(dv_ref[...], wv_ref[...], ((1,), (1,)))
        dkr = jnp.zeros((TR, LANE), F32)
        for h in range(H):
            tile = slice(h * LANE, (h + 1) * LANE)
            dql = _rot(dq_ref[tile, :].T, c, s, lane).astype(BF16)
            dql_ref[:, tile] = dql
            dqn = dqn + _dot(dql, wq_ref[:, tile], ((1,), (1,)))
            dkt = dk_ref[:, tile]
            dkb_ref[:, tile] = dkt.astype(BF16)
            dkvn = dkvn + _dot(dkt, wk_ref[:, tile], ((1,), (1,)))
            dkr = dkr + dkt
        dkr = jnp.where((lane >= NOPE) & (lane < NOPE + ROPE), dkr, 0.0)
        q_c, kv_c = sm_ref[:, SM_Q:SM_Q + Q_RANK], sm_ref[:, SM_KV:SM_KV + KV_RANK]
        dq_c, dqw = _rms_grad(q_c, _rms(q_c, qw_ref[...])[1], qw_ref[...], dqn)
        dkv_c, dkw = _rms_grad(kv_c, _rms(kv_c, kw_ref[...])[1], kw_ref[...], dkvn)
        ds_ref[:, SM_Q:SM_Q + Q_RANK] = dq_c.astype(BF16)
        ds_ref[:, SM_KV:SM_KV + KV_RANK] = dkv_c.astype(BF16)
        ds_ref[:, SM_KR:SM_KR + LANE] = pltpu.roll(_rot(dkr, c, s, lane), LANE - NOPE, 1).astype(BF16)
        _acc_rows(dqw_ref, dqw)
        _acc_rows(dkw_ref, dkw)

    w = H * LANE
    return pl.pallas_call(
        body, name="qkv_bwd", grid=(S // TR,),
        in_specs=[pl.BlockSpec((w, TR), lambda i: (0, i)), _row_spec(w), _row_spec(H * VDIM), _row_spec(SMALL_W), _whole(w_q), _whole(w_k),
                  _whole(w_v), _vec_spec(Q_RANK), _vec_spec(KV_RANK), _row_spec(LANE), _row_spec(LANE)],
        out_specs=[_row_spec(SM_DT), _row_spec(w), _row_spec(w), _vec_spec(Q_RANK), _vec_spec(KV_RANK)],
        out_shape=[jax.ShapeDtypeStruct((S, SM_DT), BF16), jax.ShapeDtypeStruct((S, w), BF16), jax.ShapeDtypeStruct((S, w), BF16),
                   jax.ShapeDtypeStruct((1, Q_RANK), F32), jax.ShapeDtypeStruct((1, KV_RANK), F32)],
    )(dqt, dk, dv, small, w_q, w_k, w_v, q_norm, kv_norm, cos_t, sin_t)


CB = 256


def _shift_down(u, k, row):
    if k == 0:
        return u
    return jnp.where(row >= k, pltpu.roll(u, k, 0), 0.0)


def _shift_up(u, k, row):
    if k == 0:
        return u
    return jnp.where(row < S - k, pltpu.roll(u, S - k, 0), 0.0)


def _conv_fwd(u, w, b):
    def body(u_ref, w_ref, b_ref, o_ref):
        row = lax.broadcasted_iota(jnp.int32, (S, CB), 0)
        uu = u_ref[...]
        acc = b_ref[...] + w_ref[SSD_K - 1:SSD_K, :] * uu
        for k in range(SSD_K - 1):
            acc = acc + w_ref[k:k + 1, :] * _shift_down(uu, SSD_K - 1 - k, row)
        o_ref[...] = acc * _sigmoid(acc)

    c = u.shape[1]
    return pl.pallas_call(
        body, name="conv_fwd", grid=(c // CB,),
        in_specs=[pl.BlockSpec((S, CB), lambda j: (0, j)), pl.BlockSpec((SSD_K, CB), lambda j: (0, j)), pl.BlockSpec((1, CB), lambda j: (0, j))],
        out_specs=pl.BlockSpec((S, CB), lambda j: (0, j)), out_shape=jax.ShapeDtypeStruct((S, c), F32),
    )(u, w, b)


def _conv_bwd(u, w, b, dact):
    def body(u_ref, w_ref, b_ref, d_ref, du_ref, dw_ref, db_ref):
        row = lax.broadcasted_iota(jnp.int32, (S, CB), 0)
        uu = u_ref[...]
        sh = [_shift_down(uu, SSD_K - 1 - k, row) for k in range(SSD_K)]
        acc = b_ref[...]
        for k in range(SSD_K):
            acc = acc + w_ref[k:k + 1, :] * sh[k]
        sg = _sigmoid(acc)
        dacc = d_ref[...] * (sg * (1.0 + acc * (1.0 - sg)))
        du = w_ref[SSD_K - 1:SSD_K, :] * dacc
        for k in range(SSD_K - 1):
            du = du + w_ref[k:k + 1, :] * _shift_up(dacc, SSD_K - 1 - k, row)
        du_ref[...] = du.astype(BF16)
        for k in range(SSD_K):
            dw_ref[k:k + 1, :] = jnp.sum(dacc * sh[k], axis=0, keepdims=True)
        db_ref[...] = jnp.sum(dacc, axis=0, keepdims=True)

    c = u.shape[1]
    col = lambda r: pl.BlockSpec((r, CB), lambda j: (0, j))
    return pl.pallas_call(
        body, name="conv_bwd", grid=(c // CB,), in_specs=[col(S), col(SSD_K), col(1), col(S)], out_specs=[col(S), col(SSD_K), col(1)],
        out_shape=[jax.ShapeDtypeStruct((S, c), BF16), jax.ShapeDtypeStruct((SSD_K, c), F32), jax.ShapeDtypeStruct((1, c), F32)],
    )(u, w, b, dact)


NPAIR = H // 2
PAIRS_PER_GROUP = NPAIR // SSD_G


def _softplus(v):
    return jnp.maximum(v, 0.0) + jnp.log(1.0 + jnp.exp(-jnp.abs(v)))


def _dot(a, b, dims):
    return lax.dot_general(a.astype(BF16), b.astype(BF16), (dims, ((), ())), preferred_element_type=F32)


def _dot2(a, sel):
    hi = a.astype(BF16)
    lo = (a - hi.astype(F32)).astype(BF16)
    dims = (((1,), (0,)), ((), ()))
    return lax.dot_general(hi, sel, dims, preferred_element_type=F32) + lax.dot_general(lo, sel, dims, preferred_element_type=F32)


def _dot3(a, b, dims, split_lhs):
    v = a if split_lhs else b
    v1 = v.astype(BF16)
    r1 = v - v1.astype(F32)
    v2 = r1.astype(BF16)
    v3 = (r1 - v2.astype(F32)).astype(BF16)
    acc = None
    for part in (v1, v2, v3):
        lhs, rhs = (part, b) if split_lhs else (a, part)
        t = lax.dot_general(lhs, rhs, (dims, ((), ())), preferred_element_type=F32)
        acc = t if acc is None else acc + t
    return acc


def _ssd_chunk_common(dt_ref, dtT_ref, prow_ref, pcol_ref):
    prow = prow_ref[...]
    pcol = pcol_ref[...]
    ri = lax.broadcasted_iota(jnp.int32, (SSD_L, SSD_L), 0)
    ci = lax.broadcasted_iota(jnp.int32, (SSD_L, SSD_L), 1)
    causal = ri >= ci
    pre_c = dt_ref[...] + prow[0:1, :]
    dtc = _softplus(pre_c)
    a_row = -jnp.exp(prow[1:2, :])
    cs_col = _dot3(causal.astype(BF16), dtc * a_row, ((1,), (0,)), False)
    dtr = _softplus(dtT_ref[...] + pcol[:, 0:1])
    a_col = -jnp.exp(pcol[:, 1:2])
    cs_row = _dot3(dtr * a_col, (ri <= ci).astype(BF16), ((1,), (0,)), True)
    return prow, causal, pre_c, dtc, a_row, cs_col, cs_row


def _ssd_fwd(act, small, dtT, prow, pcol):
    def body(x_ref, b_ref, c_ref, dt_ref, dtT_ref, prow_ref, pcol_ref, y_ref, st_ref, state):
        @pl.when(pl.program_id(0) == 0)
        def _():
            state[...] = jnp.zeros_like(state)

        prow, causal, _, dtc, _, cs_col, cs_row = _ssd_chunk_common(dt_ref, dtT_ref, prow_ref, pcol_ref)
        lo = lax.broadcasted_iota(jnp.int32, (SSD_L, LANE), 1) < SSD_P
        lo1 = lo[0:1, :]
        for g in range(SSD_G):
            bm = b_ref[:, g * SSD_N:(g + 1) * SSD_N]
            cm = c_ref[:, g * SSD_N:(g + 1) * SSD_N]
            cb = _dot(cm, bm, ((1,), (1,)))
            for qq in range(PAIRS_PER_GROUP):
                q = g * PAIRS_PER_GROUP + qq
                ha, hb = 2 * q, 2 * q + 1
                csa, csb = cs_col[:, ha:ha + 1], cs_col[:, hb:hb + 1]
                xp = x_ref[:, q * LANE:(q + 1) * LANE]
                xx = xp * jnp.where(lo, dtc[:, ha:ha + 1], dtc[:, hb:hb + 1])
                ga = cb * jnp.exp(jnp.where(causal, csa - cs_row[ha:ha + 1, :], NEG))
                gb = cb * jnp.exp(jnp.where(causal, csb - cs_row[hb:hb + 1, :], NEG))
                y = _dot(ga, jnp.where(lo, xx, 0.0), ((1,), (0,))) + _dot(gb, jnp.where(lo, 0.0, xx), ((1,), (0,)))
                s_in = state[q]
                y = y + _dot(cm, s_in, ((1,), (0,))) * jnp.where(lo, jnp.exp(csa), jnp.exp(csb))
                y = y + jnp.where(lo1, prow[2:3, ha:ha + 1], prow[2:3, hb:hb + 1]) * xp
                y_ref[:, q * LANE:(q + 1) * LANE] = y
                la, lb = csa[SSD_L - 1:SSD_L, :], csb[SSD_L - 1:SSD_L, :]
                decay = jnp.where(lo, jnp.exp(la - csa), jnp.exp(lb - csb))
                st_ref[q] = s_in
                state[q] = s_in * jnp.where(lo1, jnp.exp(la), jnp.exp(lb)) + _dot(bm, xx * decay, ((0,), (0,)))

    L = SSD_L
    return pl.pallas_call(
        body, name="ssd_fwd", grid=(SSD_NC,),
        in_specs=[pl.BlockSpec((L, SSD_INNER), lambda c: (c, 0)),
                  pl.BlockSpec((L, SSD_G * SSD_N), lambda c: (c, SSD_INNER // (SSD_G * SSD_N))),
                  pl.BlockSpec((L, SSD_G * SSD_N), lambda c: (c, SSD_INNER // (SSD_G * SSD_N) + 1)),
                  pl.BlockSpec((L, LANE), lambda c: (c, SM_DT // LANE)),
                  pl.BlockSpec((LANE, L), lambda c: (0, c)),
                  pl.BlockSpec((8, LANE), lambda c: (0, 0)), pl.BlockSpec((LANE, 8), lambda c: (0, 0))],
        out_specs=[pl.BlockSpec((L, SSD_INNER), lambda c: (c, 0)),
                   pl.BlockSpec((None, NPAIR, SSD_N, LANE), lambda c: (c, 0, 0, 0))],
        out_shape=[jax.ShapeDtypeStruct((S, SSD_INNER), F32), jax.ShapeDtypeStruct((SSD_NC, NPAIR, SSD_N, LANE), F32)],
        scratch_shapes=[pltpu.VMEM((NPAIR, SSD_N, LANE), F32)],
        compiler_params=pltpu.CompilerParams(dimension_semantics=("arbitrary",)),
    )(act, act, act, small, dtT, prow, pcol)


def _ssd_bwd(act, small, dtT, prow, pcol, states, dy):
    def body(x_ref, b_ref, c_ref, dt_ref, dtT_ref, prow_ref, pcol_ref, st_ref, dy_ref,
             dx_ref, ddt_ref, dp_ref, dstate):
        @pl.when(pl.program_id(0) == 0)
        def _():
            dstate[...] = jnp.zeros_like(dstate)
            dp_ref[...] = jnp.zeros_like(dp_ref)

        prow, causal, pre_c, dtc, a_row, cs_col, cs_row = _ssd_chunk_common(dt_ref, dtT_ref, prow_ref, pcol_ref)
        lane = lax.broadcasted_iota(jnp.int32, (SSD_L, LANE), 1)
        sub = lax.broadcasted_iota(jnp.int32, (LANE, SSD_L), 0)
        rowi = lax.broadcasted_iota(jnp.int32, (SSD_L, 1), 0)
        pick_p = lax.broadcasted_iota(jnp.int32, (LANE, LANE), 0)
        pick_l = lax.broadcasted_iota(jnp.int32, (LANE, LANE), 1)
        lo = lane < SSD_P
        lo1 = lo[0:1, :]
        dcs_c = jnp.zeros((SSD_L, LANE), F32)
        dcs_r = jnp.zeros((LANE, SSD_L), F32)
        ddt_x = jnp.zeros((SSD_L, LANE), F32)
        dd_row = jnp.zeros((1, LANE), F32)
        for g in range(SSD_G):
            bm = b_ref[:, g * SSD_N:(g + 1) * SSD_N]
            cm = c_ref[:, g * SSD_N:(g + 1) * SSD_N]
            cb = _dot(cm, bm, ((1,), (1,)))
            dcb = jnp.zeros((SSD_L, SSD_L), F32)
            dbm = jnp.zeros((SSD_L, SSD_N), F32)
            dcm = jnp.zeros((SSD_L, SSD_N), F32)
            for qq in range(PAIRS_PER_GROUP):
                q = g * PAIRS_PER_GROUP + qq
                ha, hb = 2 * q, 2 * q + 1
                csa, csb = cs_col[:, ha:ha + 1], cs_col[:, hb:hb + 1]
                xp = x_ref[:, q * LANE:(q + 1) * LANE]
                dtp = jnp.where(lo, dtc[:, ha:ha + 1], dtc[:, hb:hb + 1])
                xx = xp * dtp
                lma = jnp.exp(jnp.where(causal, csa - cs_row[ha:ha + 1, :], NEG))
                lmb = jnp.exp(jnp.where(causal, csb - cs_row[hb:hb + 1, :], NEG))
                ga, gb = cb * lma, cb * lmb
                dyp = dy_ref[:, q * LANE:(q + 1) * LANE]
                dya, dyb = jnp.where(lo, dyp, 0.0), jnp.where(lo, 0.0, dyp)
                s_in = st_ref[q]
                ds_out = dstate[q]
                la, lb = csa[SSD_L - 1:SSD_L, :], csb[SSD_L - 1:SSD_L, :]
                ecs = jnp.where(lo, jnp.exp(csa), jnp.exp(csb))
                decay = jnp.where(lo, jnp.exp(la - csa), jnp.exp(lb - csb))
                cd = jnp.where(lo1, jnp.exp(la), jnp.exp(lb))
                bds = _dot(bm, ds_out, ((1,), (0,)))
                dxx = _dot(ga, dya, ((0,), (0,))) + _dot(gb, dyb, ((0,), (0,))) + bds * decay
                dga = _dot(dya, xx, ((1,), (1,)))
                dgb = _dot(dyb, xx, ((1,), (1,)))
                dsega, dsegb = dga * ga, dgb * gb
                dcb = dcb + dga * lma + dgb * lmb
                yoff = _dot(cm, s_in, ((1,), (0,))) * ecs
                dye = dyp * ecs
                dcm = dcm + _dot(dye, s_in, ((1,), (1,)))
                xd = xx * decay
                dbm = dbm + _dot(xd, ds_out, ((1,), (1,)))
                wv = xd * bds
                ends = jnp.sum(wv, axis=0, keepdims=True) + cd * jnp.sum(ds_out * s_in, axis=0, keepdims=True)
                t1 = dyp * yoff - wv + jnp.where(rowi == SSD_L - 1, ends, 0.0)
                to_pair = (((pick_p < SSD_P) & (pick_l == ha)) | ((pick_p >= SSD_P) & (pick_l == hb))).astype(BF16)
                to_a_b = jnp.concatenate([(pick_l == ha).astype(BF16), (pick_l == hb).astype(BF16)], axis=0)
                dcs_c = dcs_c + _dot2(t1, to_pair) + _dot2(jnp.concatenate([dsega, dsegb], axis=1), to_a_b)
                dcs_r = (dcs_r + jnp.where(sub == ha, jnp.sum(dsega, axis=0, keepdims=True), 0.0)
                         + jnp.where(sub == hb, jnp.sum(dsegb, axis=0, keepdims=True), 0.0))
                dstate[q] = _dot(cm, dye, ((0,), (0,))) + cd * ds_out
                dpair = jnp.where(lo1, prow[2:3, ha:ha + 1], prow[2:3, hb:hb + 1])
                dx_ref[:, q * LANE:(q + 1) * LANE] = dxx * dtp + dpair * dyp
                ddt_x = ddt_x + _dot2(dxx * xp, to_pair)
                dd_row = dd_row + jnp.sum(_dot2(dyp * xp, to_pair), axis=0, keepdims=True)
            dx_ref[:, SSD_INNER + g * SSD_N:SSD_INNER + (g + 1) * SSD_N] = dbm + _dot(dcb, cm, ((0,), (0,)))
            dx_ref[:, SSD_INNER + (SSD_G + g) * SSD_N:SSD_INNER + (SSD_G + g + 1) * SSD_N] = dcm + _dot(dcb, bm, ((1,), (0,)))
        ri = lax.broadcasted_iota(jnp.int32, (SSD_L, SSD_L), 0)
        ci = lax.broadcasted_iota(jnp.int32, (SSD_L, SSD_L), 1)
        da = _dot3((ri <= ci).astype(BF16), dcs_c, ((1,), (0,)), False)
        da = da - _dot3(dcs_r, causal.astype(BF16), ((1,), (0,)), True).T
        ddt = ddt_x + da * a_row
        ddt_raw = ddt * _sigmoid(pre_c)
        ddt_ref[...] = ddt_raw
        da_head = jnp.sum(da * dtc, axis=0, keepdims=True) * a_row
        dp_ref[0:1, :] += jnp.sum(ddt_raw, axis=0, keepdims=True)
        dp_ref[1:2, :] += da_head
        dp_ref[2:3, :] += dd_row

    L = SSD_L
    rev = SSD_NC - 1
    bc_cols = SSD_INNER // (SSD_G * SSD_N)
    return pl.pallas_call(
        body, name="ssd_bwd", grid=(SSD_NC,),
        in_specs=[pl.BlockSpec((L, SSD_INNER), lambda c: (rev - c, 0)),
                  pl.BlockSpec((L, SSD_G * SSD_N), lambda c: (rev - c, bc_cols)),
                  pl.BlockSpec((L, SSD_G * SSD_N), lambda c: (rev - c, bc_cols + 1)),
                  pl.BlockSpec((L, LANE), lambda c: (rev - c, SM_DT // LANE)),
                  pl.BlockSpec((LANE, L), lambda c: (0, rev - c)),
                  pl.BlockSpec((8, LANE), lambda c: (0, 0)), pl.BlockSpec((LANE, 8), lambda c: (0, 0)),
                  pl.BlockSpec((None, NPAIR, SSD_N, LANE), lambda c: (rev - c, 0, 0, 0)),
                  pl.BlockSpec((L, SSD_INNER), lambda c: (rev - c, 0))],
        out_specs=[pl.BlockSpec((L, SSD_XBC), lambda c: (rev - c, 0)),
                   pl.BlockSpec((L, LANE), lambda c: (rev - c, 0)),
                   pl.BlockSpec((8, LANE), lambda c: (0, 0))],
        out_shape=[jax.ShapeDtypeStruct((S, SSD_XBC), F32), jax.ShapeDtypeStruct((S, LANE), F32),
                   jax.ShapeDtypeStruct((8, LANE), F32)],
        scratch_shapes=[pltpu.VMEM((NPAIR, SSD_N, LANE), F32)],
        compiler_params=pltpu.CompilerParams(dimension_semantics=("arbitrary",)),
    )(act, act, act, small, dtT, prow, pcol, states, dy)


TQ = 256
TK = 256
FWD_TQ = 256
FWD_TK = 256


def _attn_fwd(qc, kc, v):
    TQ, TK = FWD_TQ, FWD_TK

    def body(q_ref, k_ref, v_ref, o_ref, lse_ref):
        i = pl.program_id(1)
        lo = lax.broadcasted_iota(jnp.int32, (TQ, LANE), 1) < VDIM
        lo_k = lax.broadcasted_iota(jnp.int32, (TK, LANE), 1) < VDIM
        row_minus_col = lax.broadcasted_iota(jnp.int32, (TQ, TK), 0) - lax.broadcasted_iota(jnp.int32, (TQ, TK), 1)
        qa, qb = q_ref[:, 0:LANE], q_ref[:, LANE:2 * LANE]

        def scores(kb):
            kk = k_ref[pl.ds(pl.multiple_of(kb * TK, TK), TK), :]
            return (_dot(qa, kk[:, 0:LANE], ((1,), (1,))) * ATT_SCALE_LOG2, _dot(qb, kk[:, LANE:2 * LANE], ((1,), (1,))) * ATT_SCALE_LOG2)

        def update(kb, sa, sb, stats):
            ma, la, mb, lb, acc = stats
            vv = v_ref[pl.ds(pl.multiple_of(kb * TK, TK), TK), :]
            na = jnp.maximum(ma, jnp.max(sa, axis=1, keepdims=True))
            nb = jnp.maximum(mb, jnp.max(sb, axis=1, keepdims=True))
            pa, pb = jnp.exp2(sa - na), jnp.exp2(sb - nb)
            fa, fb = jnp.exp2(ma - na), jnp.exp2(mb - nb)
            la = fa * la + jnp.sum(pa, axis=1, keepdims=True)
            lb = fb * lb + jnp.sum(pb, axis=1, keepdims=True)
            acc = (acc * jnp.where(lo, fa, fb) + _dot(pa, jnp.where(lo_k, vv, 0), ((1,), (0,)))
                   + _dot(pb, jnp.where(lo_k, 0, vv), ((1,), (0,))))
            return na, la, nb, lb, acc

        def step(kb, carry):
            sa, sb = carry[:2]
            nxt = scores(kb + 1)
            return nxt + update(kb, sa, sb, carry[2:])

        neg = jnp.full((TQ, 1), NEG, F32)
        zero = jnp.zeros((TQ, 1), F32)
        n_full = i * (TQ // TK)
        carry = lax.fori_loop(0, n_full, step, scores(0) + (neg, zero, neg, zero, jnp.zeros((TQ, LANE), F32)))
        s, stats = carry[:2], carry[2:]
        for d in range(TQ // TK):
            nxt = scores(n_full + d + 1) if d + 1 < TQ // TK else None
            sa, sb = (jnp.where(row_minus_col >= d * TK, t, NEG) for t in s)
            stats = update(n_full + d, sa, sb, stats)
            s = nxt
        ma, la, mb, lb, acc = stats
        o_ref[...] = acc / jnp.where(lo, la, lb)
        lse_ref[...] = jnp.where(lo, ma + jnp.log2(la), mb + jnp.log2(lb)) * LN2

    return pl.pallas_call(
        body, name="attn_fwd", grid=(NPAIR, S // TQ),
        in_specs=[pl.BlockSpec((TQ, 2 * LANE), lambda j, i: (i, j)), pl.BlockSpec((S, 2 * LANE), lambda j, i: (0, j)),
                  pl.BlockSpec((S, LANE), lambda j, i: (0, j))],
        out_specs=[pl.BlockSpec((TQ, LANE), lambda j, i: (i, j)), pl.BlockSpec((None, TQ, LANE), lambda j, i: (j, i, 0))],
        out_shape=[jax.ShapeDtypeStruct((S, H * VDIM), F32), jax.ShapeDtypeStruct((NPAIR, S, LANE), F32)],
        compiler_params=pltpu.CompilerParams(dimension_semantics=("parallel", "parallel")),
    )(qc, kc, v)


def _attn_rows(lse, o, do):
    def body(lse_ref, o_ref, do_ref, r_ref):
        lt = lse_ref[...].T * (1.0 / LN2)
        tt = (o_ref[...] * do_ref[...]).T
        r_ref[...] = jnp.zeros_like(r_ref)
        r_ref[0:1, :] = lt[0:1, :]
        r_ref[1:2, :] = lt[VDIM:VDIM + 1, :]
        r_ref[2:3, :] = jnp.sum(tt[0:VDIM, :], axis=0, keepdims=True)
        r_ref[3:4, :] = jnp.sum(tt[VDIM:LANE, :], axis=0, keepdims=True)

    tile = pl.BlockSpec((S, LANE), lambda j: (0, j))
    return pl.pallas_call(
        body, name="attn_rows", grid=(NPAIR,), in_specs=[pl.BlockSpec((None, S, LANE), lambda j: (j, 0, 0)), tile, tile],
        out_specs=pl.BlockSpec((None, 8, S), lambda j: (j, 0, 0)), out_shape=jax.ShapeDtypeStruct((NPAIR, 8, S), F32),
    )(lse, o, do)


def _attn_bwd(qc, kc, kct, v, do, rows):
    nq = S // TQ

    def body(q_ref, k_ref, kt_ref, v_ref, do_ref, r_ref, dqt_ref, dk_ref, dv_ref):
        kb = pl.program_id(1)

        @pl.when(kb == 0)
        def _():
            dqt_ref[...] = jnp.zeros_like(dqt_ref)

        lo = lax.broadcasted_iota(jnp.int32, (TK, LANE), 1) < VDIM
        q_minus_k = lax.broadcasted_iota(jnp.int32, (TK, TQ), 1) - lax.broadcasted_iota(jnp.int32, (TK, TQ), 0)
        vv = v_ref[...]
        kk = k_ref[...]

        def step(qi, carry):
            off = pl.multiple_of(qi * TQ, TQ)
            qq = q_ref[pl.ds(off, TQ), :]
            dd = do_ref[pl.ds(off, TQ), :].astype(BF16)
            rr = r_ref[:, pl.ds(off, TQ)]
            keep = q_minus_k >= (kb - qi) * TQ
            out = []
            for x in range(2):
                sel = lo if x == 0 else jnp.logical_not(lo)
                kx, qx = kk[:, x * LANE:(x + 1) * LANE], qq[:, x * LANE:(x + 1) * LANE]
                st = jnp.where(keep, _dot(kx, qx, ((1,), (1,))) * ATT_SCALE_LOG2, NEG)
                pt = jnp.exp2(st - rr[x:x + 1, :])
                dpt = _dot(jnp.where(sel, vv, 0), dd, ((1,), (1,)))
                dst = (pt * (dpt - rr[2 + x:3 + x, :]) * ATT_SCALE).astype(BF16)
                out.append(carry[x] + _dot(dst, qx, ((1,), (0,))))
                out.append(_dot(pt, jnp.where(sel, dd, 0), ((1,), (0,))))
                dqt_ref[x * LANE:(x + 1) * LANE, pl.ds(off, TQ)] += _dot(kt_ref[x * LANE:(x + 1) * LANE, :], dst, ((1,), (0,)))
            return out[0], out[2], carry[2] + out[1] + out[3]

        z = jnp.zeros((TK, LANE), F32)
        dka, dkb, dv = lax.fori_loop(kb, nq, step, (z, z, z))
        dk_ref[:, 0:LANE] = dka
        dk_ref[:, LANE:2 * LANE] = dkb
        dv_ref[...] = dv.astype(BF16)

    return pl.pallas_call(
        body, name="attn_bwd", grid=(NPAIR, S // TK),
        in_specs=[pl.BlockSpec((S, 2 * LANE), lambda j, k: (0, j)), pl.BlockSpec((TK, 2 * LANE), lambda j, k: (k, j)),
                  pl.BlockSpec((2 * LANE, TK), lambda j, k: (j, k)), pl.BlockSpec((TK, LANE), lambda j, k: (k, j)),
                  pl.BlockSpec((S, LANE), lambda j, k: (0, j)), pl.BlockSpec((None, 8, S), lambda j, k: (j, 0, 0))],
        out_specs=[pl.BlockSpec((2 * LANE, S), lambda j, k: (j, 0)), pl.BlockSpec((TK, 2 * LANE), lambda j, k: (k, j)),
                   pl.BlockSpec((TK, LANE), lambda j, k: (k, j))],
        out_shape=[jax.ShapeDtypeStruct((H * LANE, S), F32), jax.ShapeDtypeStruct((S, H * LANE), F32),
                   jax.ShapeDtypeStruct((S, H * VDIM), BF16)],
        compiler_params=pltpu.CompilerParams(dimension_semantics=("parallel", "arbitrary")),
    )(qc, kc, kct, v, do, rows)


_IN_Z, _IN_XBC, _IN_DT, _IN_Q, _IN_KV, _IN_KR = 0, 1024, 2560, 2576, 2960, 3216


PROJ_COLS = 512
SMALL_PAD = pl.cdiv(SMALL_W, PROJ_COLS) * PROJ_COLS


def _prep_in(w_in_t):
    dt = w_in_t.dtype
    return jnp.concatenate(
        [w_in_t[_IN_Q:_IN_KV], w_in_t[_IN_KV:_IN_KR], w_in_t[_IN_KR:IN_WIDTH], jnp.zeros((LANE - ROPE, D), dt),
         w_in_t[_IN_DT:_IN_Q], jnp.zeros((SMALL_PAD - SM_DT - H, D), dt)], axis=0)


def _proj_in(xb, w_in_t, w_small):
    nz, nx, ns = (_IN_XBC - _IN_Z) // PROJ_COLS, (_IN_DT - _IN_XBC) // PROJ_COLS, SMALL_PAD // PROJ_COLS

    def body(x_ref, w_ref, ws_ref, z_ref, xbc_ref, sm_ref):
        i = pl.program_id(0)

        def emit(w, o_ref):
            o_ref[...] = lax.dot_general(x_ref[...], w[...], (((1,), (1,)), ((), ())), preferred_element_type=F32)

        pl.when(i < nz)(lambda: emit(w_ref, z_ref))
        pl.when((i >= nz) & (i < nz + nx))(lambda: emit(w_ref, xbc_ref))
        pl.when(i >= nz + nx)(lambda: emit(ws_ref, sm_ref))

    def blocks(first, count, rows):
        at = lambda i: jnp.clip(i - first, 0, count - 1)
        return pl.BlockSpec((PROJ_COLS, D), lambda i: (at(i), 0)) if rows else pl.BlockSpec((S, PROJ_COLS), lambda i: (0, at(i)))

    return pl.pallas_call(
        body, name="proj_in", grid=(nz + nx + ns,),
        in_specs=[pl.BlockSpec((S, D), lambda i: (0, 0)), blocks(0, nz + nx, True), blocks(nz + nx, ns, True)],
        out_specs=[blocks(0, nz, False), blocks(nz, nx, False), blocks(nz + nx, ns, False)],
        out_shape=[jax.ShapeDtypeStruct((S, _IN_XBC - _IN_Z), F32), jax.ShapeDtypeStruct((S, _IN_DT - _IN_XBC), F32),
                   jax.ShapeDtypeStruct((S, SMALL_W), F32)],
    )(xb, w_in_t, w_small)


DW_IN_ROWS = 512


def _d_w_in(dz, dxbc, dsm, xb):
    parts = (dz, dxbc, dsm)
    first = [0]
    for a in parts:
        first.append(first[-1] + a.shape[1] // DW_IN_ROWS)
    assert first[-1] == pl.cdiv(IN_WIDTH, DW_IN_ROWS)

    def body(dz_ref, dxbc_ref, dsm_ref, x_ref, o_ref):
        i = pl.program_id(0)
        for a_ref, lo, hi in zip((dz_ref, dxbc_ref, dsm_ref), first[:-1], first[1:]):
            @pl.when((i >= lo) & (i < hi))
            def _(a_ref=a_ref):
                o_ref[...] = lax.dot_general(a_ref[...], x_ref[...], (((0,), (0,)), ((), ())),
                                             preferred_element_type=F32).astype(BF16)

    def a_spec(lo, hi):
        return pl.BlockSpec((S, DW_IN_ROWS), lambda i: (0, jnp.clip(i - lo, 0, hi - lo - 1)))

    return pl.pallas_call(
        body, name="d_w_in", grid=(first[-1],),
        in_specs=[a_spec(lo, hi) for lo, hi in zip(first[:-1], first[1:])] + [pl.BlockSpec((S, D), lambda i: (0, 0))],
        out_specs=pl.BlockSpec((DW_IN_ROWS, D), lambda i: (i, 0)), out_shape=jax.ShapeDtypeStruct((IN_WIDTH, D), BF16),
    )(dz, dxbc, dsm, xb)


def _prep_attn(w_qb, w_kvb):
    w_q = jnp.pad(w_qb.reshape(Q_RANK, H, NOPE + ROPE), ((0, 0), (0, 0), (0, LANE - NOPE - ROPE))).reshape(Q_RANK, H * LANE)
    kv3 = w_kvb.reshape(KV_RANK, H, NOPE + VDIM)
    w_k = jnp.pad(kv3[:, :, :NOPE], ((0, 0), (0, 0), (0, LANE - NOPE))).reshape(KV_RANK, H * LANE)
    w_v = kv3[:, :, NOPE:].reshape(KV_RANK, H * VDIM)
    return w_q, w_k, w_v


def _rope_tables(positions):
    inv_freq = 1.0 / (10000.0 ** (jnp.arange(0, ROPE, 2, dtype=F32) / ROPE))
    ang = positions.astype(F32).reshape(S, 1) * inv_freq
    cos, sin = jnp.cos(ang), jnp.sin(ang)
    cos_t = jnp.concatenate([jnp.ones((S, NOPE), F32), cos, cos, jnp.ones((S, LANE - NOPE - ROPE), F32)], axis=1)
    sin_t = jnp.concatenate([jnp.zeros((S, NOPE), F32), -sin, sin, jnp.zeros((S, LANE - NOPE - ROPE), F32)], axis=1)
    return cos_t, sin_t


def _local_step(x, p, positions, target, w_in, fetch, send, sp, started):
    w_in_t = w_in.reshape(IN_WIDTH, D)
    w_small = _prep_in(w_in_t)
    cos_t, sin_t = _rope_tables(positions)
    prow = jnp.zeros((8, LANE), F32).at[0, :H].set(sp["dt_bias"][0]).at[1, :H].set(sp["A_log"][0]).at[2, :H].set(sp["D"][0])
    pcol = prow.T

    xb, pb = (x + started).astype(BF16), p.astype(BF16)
    z, xbc, small = _proj_in(xb, w_in_t, w_small)
    act = _conv_fwd(xbc, sp["conv_w"], sp["conv_b"])
    dt_t = small[:, SM_DT:SM_DT + LANE].T
    y, states = _ssd_fwd(act, small, dt_t, prow, pcol)
    y_ssd = _gate_norm_fwd(y, z, sp["ssd_norm"])
    gl = fetch("attn", y_ssd)
    w_q, w_k, w_v = _prep_attn(_from_cols(gl["w_qb"]), _from_cols(gl["w_kvb"]))
    qn, kvn, qcat, kcat, kcat_t, v = _qkv_fwd(small, w_q, w_k, w_v, sp["q_norm"], sp["kv_norm"], cos_t, sin_t)
    o, lse = _attn_fwd(qcat, kcat, v)
    y_mla = _rms_fwd(o, sp["out_norm"], name="out_norm_fwd")
    w_out = fetch("out", y_mla)["w_out"]
    w_out_s = w_out[:NCHIP // 2].reshape(SSD_INNER, D)
    w_out_m = w_out[NCHIP // 2:].reshape(SSD_INNER, D)
    mix = _mm([(y_ssd, w_out_s), (y_mla, w_out_m)], name="out_proj")
    h1, h1b = _ln_fwd(x, mix, sp["ln_mix_g"], sp["ln_mix_b"])
    gl = fetch("ffn", h1b)
    w_pg, w_pp = gl["w_pg"].reshape(D, D), _from_cols(gl["w_pp"])
    w_gate, w_up, w_down = gl["w_gate"], gl["w_up"], gl["w_down"]
    gate, up, actf = _ffn_hidden_fwd(h1b, w_gate, w_up)
    ffn = _mm([(actf, w_down)], chunk="sum", name="ffn_down")
    pg = _mm([(h1b, w_pg)], name="ple_gate")
    pp = _mm([(pb, w_pp)], name="ple_proj")
    dpre2, dpre2b, dpg, dpp, dg2, db2, loss_row = _final_fwd_bwd(h1, ffn, pg, pp, target, sp["ln_ffn_g"], sp["ln_ffn_b"])

    g = {"ln_ffn_g": dg2, "ln_ffn_b": db2}
    g["w_pp"] = _to_cols(_mm([(pb, dpp)], ta=True, out_dtype=BF16, name="d_w_ple_proj"))
    g["w_pg"] = _mm([(h1b, dpg)], ta=True, out_dtype=BF16, name="d_w_ple_gate").reshape(NCHIP, D // NCHIP, D)
    g["w_down"] = _mm([(actf, dpre2b)], ta=True, chunk="out", out_dtype=BF16, name="d_w_down")
    dgate, dup = _ffn_hidden_bwd(dpre2b, w_down, gate, up)
    g["w_gate"] = _mm([(dgate, h1b)], ta=True, chunk="out", out_dtype=BF16, name="d_w_gate")
    g["w_up"] = _mm([(dup, h1b)], ta=True, chunk="out", out_dtype=BF16, name="d_w_up")
    sent = send("ffn", {name: g.pop(name) for name in dict(ASYNC_GROUPS)["ffn"]})
    dh1 = _mm([(dpg, w_pg)], tb=True, add=dpre2, add_scale=ALPHA, name="d_h1_ple")
    dh1 = _mm([(dgate, w_gate), (dup, w_up)], chunk="sum", add=dh1, name="d_h1")
    dpre1, dpre1b, g["ln_mix_g"], g["ln_mix_b"] = _ln_bwd(x, mix, sp["ln_mix_g"] + sent, dh1)
    dy_ssd = _mm([(dpre1b, w_out_s)], tb=True, name="d_y_ssd")
    dy_mla = _mm([(dpre1b, w_out_m)], tb=True, name="d_y_mla")
    dw_out = jnp.concatenate([_mm([(y_ssd, dpre1b)], ta=True, out_dtype=BF16, name="d_w_out_s"),
                              _mm([(y_mla, dpre1b)], ta=True, out_dtype=BF16, name="d_w_out_m")], axis=0)
    sent = send("out", {"w_out": dw_out.reshape(NCHIP, 2 * SSD_INNER // NCHIP, D)})
    do, g["out_norm"] = _rms_bwd(o, sp["out_norm"] + sent, dy_mla, name="out_norm_bwd")
    dqt, dk, dv = _attn_bwd(qcat, kcat, kcat_t, v, do, _attn_rows(lse, o, do))
    dlatent, dqlin, dkb, g["q_norm"], g["kv_norm"] = _qkv_bwd(dqt, dk, dv, small, w_q, w_k, w_v, sp["q_norm"], sp["kv_norm"], cos_t, sin_t)
    dw_q = _mm([(qn, dqlin)], ta=True, out_dtype=BF16, name="d_w_q")
    dw_k = _mm([(kvn, dkb)], ta=True, out_dtype=BF16, name="d_w_k")
    dw_v = _mm([(kvn, dv)], ta=True, out_dtype=BF16, name="d_w_v")
    dw_qb = _to_cols(dw_q.reshape(Q_RANK, H, LANE)[:, :, :NOPE + ROPE].reshape(Q_RANK, H * (NOPE + ROPE)))
    dw_kvb = _to_cols(jnp.concatenate([dw_k.reshape(KV_RANK, H, LANE)[:, :, :NOPE], dw_v.reshape(KV_RANK, H, VDIM)],
                                       axis=2).reshape(KV_RANK, H * (NOPE + VDIM)))
    sent = send("attn", {"w_qb": dw_qb, "w_kvb": dw_kvb})
    dy, dz, g["ssd_norm"] = _gate_norm_bwd(y, z, sp["ssd_norm"] + sent, dy_ssd)
    dact, ddt, dprow = _ssd_bwd(act, small, dt_t, prow, pcol, states, dy)
    g["dt_bias"], g["A_log"], g["D"] = dprow[0:1, :H], dprow[1:2, :H], dprow[2:3, :H]
    dxbc, g["conv_w"], g["conv_b"] = _conv_bwd(xbc, sp["conv_w"], sp["conv_b"], dact)
    dsmall = jnp.concatenate([dlatent, ddt.astype(BF16)], axis=1)
    in_blocks = [(d, w_in_t, (k, first // PROJ_COLS + k, PROJ_COLS))
                 for d, first in ((dz, _IN_Z), (dxbc, _IN_XBC)) for k in range(d.shape[1] // PROJ_COLS)]
    grad_x = _mm(in_blocks + [(dsmall, w_small, (0, 0, SMALL_W))], add=dpre1, add_scale=ALPHA, name="d_x")
    n_small = IN_WIDTH - _IN_DT
    dsm = jnp.concatenate([ddt[:, :H].astype(BF16), dlatent[:, :n_small - H], jnp.zeros((S, D - n_small), BF16)], axis=1)
    dw_in = _d_w_in(dz, dxbc, dsm, xb).reshape(NCHIP, IN_WIDTH // NCHIP * D // LANE, LANE)
    return loss_row, grad_x, dw_in, g


MESH = pl.DeviceIdType.MESH
BIG = (("w_in", (D, IN_WIDTH), 1), ("w_qb", (Q_RANK, H * (NOPE + ROPE)), 1), ("w_kvb", (KV_RANK, H * (NOPE + VDIM)), 1),
       ("w_out", (2 * SSD_INNER, D), 0), ("w_gate", (D, D_FF), 1), ("w_up", (D, D_FF), 1), ("w_down", (D_FF, D), 0),
       ("w_pg", (D, D), 0), ("w_pp", (PLE, D), 1))
CONV_SHARD = SSD_XBC // NCHIP
BF16_ROWS = 16


def _from_cols(stack):
    return jnp.concatenate([stack[k] for k in range(NCHIP)], axis=1)


def _to_cols(full):
    r, c4 = full.shape
    return full.reshape(r, NCHIP, c4 // NCHIP).transpose(1, 0, 2)


def _coords():
    return lax.axis_index("x"), lax.axis_index("y"), lax.axis_index("c")


def _peers():
    x, y, c = _coords()
    return 2 * x + y, c, [(1 - x, y), (x, 1 - y), (1 - x, 1 - y)], (x, y, 1 - c)


def _half_axis(shape):
    return 0 if shape[-2] % (2 * BF16_ROWS) == 0 else 1


def _half_shape(shape):
    r, c = shape[-2:]
    return (r // 2, c) if _half_axis(shape) == 0 else (r, c // 2)


def _half(core, shape):
    r, c = shape[-2:]
    if _half_axis(shape) == 0:
        return pl.ds(pl.multiple_of(core * (r // 2), BF16_ROWS), r // 2), slice(None)
    return slice(None), pl.ds(pl.multiple_of(core * (c // 2), LANE), c // 2)


def _gather_weights(shards):
    n_arr = len(shards)
    per = 2 * (NCHIP - 1)

    def body(*refs):
        ins, outs = refs[:n_arr], refs[n_arr:2 * n_arr]
        send_sems, recv_sems, local_sems = refs[2 * n_arr:]
        k, c, chips, sibling = _peers()

        def copy(idx, src, dst, to):
            return pltpu.make_async_remote_copy(src_ref=src, dst_ref=dst, send_sem=send_sems.at[idx], recv_sem=recv_sems.at[idx],
                                                device_id=to, device_id_type=MESH)

        def part(a, chip, core):
            return outs[a].at[chip, *_half(core, shards[a].shape)]

        mine = [pltpu.make_async_copy(ins[a], outs[a].at[k], local_sems.at[a]) for a in range(n_arr)]
        for cp in mine:
            cp.start()
        sends = []
        for a in range(n_arr):
            for j, (cx, cy) in enumerate(chips):
                sends.append(copy(per * a + j, ins[a].at[*_half(c, shards[a].shape)], part(a, k, c), (cx, cy, c)))
                sends[-1].start()
        for j, (cx, cy) in enumerate(chips):
            for a in range(n_arr):
                landed = part(a, 2 * cx + cy, c)
                copy(per * a + j, landed, landed, (cx, cy, c)).wait_recv()
                sends.append(copy(per * a + NCHIP - 1 + j, landed, landed, sibling))
                sends[-1].start()
        for j, (cx, cy) in enumerate(chips):
            for a in range(n_arr):
                other = part(a, 2 * cx + cy, 1 - c)
                copy(per * a + NCHIP - 1 + j, other, other, sibling).wait_recv()
        for cp in sends:
            cp.wait_send()
        for cp in mine:
            cp.wait()

    any_spec = pl.BlockSpec(memory_space=pl.ANY)
    return pl.pallas_call(
        body, name="gather_weights", in_specs=[any_spec] * n_arr, out_specs=[any_spec] * n_arr,
        out_shape=[jax.ShapeDtypeStruct((NCHIP,) + s.shape, s.dtype) for s in shards],
        scratch_shapes=[pltpu.SemaphoreType.DMA((per * n_arr,)), pltpu.SemaphoreType.DMA((per * n_arr,)),
                        pltpu.SemaphoreType.DMA((n_arr,))],
    )(*shards)


ASYNC_GROUPS = (("attn", ("w_qb", "w_kvb")), ("out", ("w_out",)), ("ffn", ("w_gate", "w_up", "w_down", "w_pg", "w_pp")))
TRANSPOSED = ("w_in", "w_gate", "w_up")
ROW_MAJOR = ("w_in",)
HBM_SPEC = pl.BlockSpec(memory_space=pltpu.HBM)
SEM_SPEC = pl.BlockSpec(memory_space=pltpu.SEMAPHORE)
IN_FLIGHT = pltpu.SideEffectType.DATAFLOW_SIDE_EFFECTING


def _in_hbm(a):
    return pltpu.with_memory_space_constraint(a, pltpu.HBM)


def _hbm_like(arrs, lead=()):
    return [pltpu.HBM(lead + a.shape, a.dtype) for a in arrs]


def _split_start(name, srcs, lands, after, n_sem, start):
    n = len(srcs)
    order = [] if after is None else [after]

    def body(*refs):
        src_refs, land_refs = refs[:n], refs[n:2 * n]
        send_sems, recv_sems = refs[2 * n + len(order)], refs[2 * n + len(order) + 1]
        token = refs[-1]

        def copy(send_idx, recv_idx, src, dst, to):
            return pltpu.make_async_remote_copy(src_ref=src, dst_ref=dst, send_sem=send_sems.at[send_idx],
                                                recv_sem=recv_sems.at[recv_idx], device_id=to, device_id_type=MESH)

        for cp in start(src_refs, land_refs, copy):
            cp.start()
        token[...] = jnp.zeros_like(token)

    sem = pltpu.SemaphoreType.DMA((n_sem,))
    outs = pl.pallas_call(
        body, name=name, in_specs=[HBM_SPEC] * (2 * n) + [pl.BlockSpec(memory_space=pl.ANY)] * len(order),
        out_specs=[SEM_SPEC, SEM_SPEC] + [HBM_SPEC] * (2 * n) + [pl.BlockSpec(memory_space=pltpu.VMEM)],
        out_shape=[sem, sem] + _hbm_like(srcs) + _hbm_like(lands) + [jax.ShapeDtypeStruct((8, LANE), F32)],
        input_output_aliases={i: 2 + i for i in range(2 * n)},
        compiler_params=pltpu.CompilerParams(has_side_effects=IN_FLIGHT),
    )(*[_in_hbm(a) for a in srcs], *[_in_hbm(a) for a in lands], *order)
    return (outs[0], outs[1], outs[2:2 + n], outs[2 + n:2 + 2 * n]), outs[-1]


def _split_wait(name, send_sems, recv_sems, srcs, lands, after, waits):
    n = len(srcs)

    def body(*refs):
        src_refs, land_refs = refs[:n], refs[n:2 * n]
        send_ref, recv_ref = refs[2 * n], refs[2 * n + 1]

        def copy(send_idx, recv_idx, src, dst, to):
            return pltpu.make_async_remote_copy(src_ref=src, dst_ref=dst, send_sem=send_ref.at[send_idx],
                                                recv_sem=recv_ref.at[recv_idx], device_id=to, device_id_type=MESH)

        for cp in waits(src_refs, land_refs, copy):
            cp.wait_send()
            cp.wait_recv()

    outs = pl.pallas_call(
        body, name=name, in_specs=[HBM_SPEC] * (2 * n) + [SEM_SPEC, SEM_SPEC, pl.BlockSpec(memory_space=pl.ANY)],
        out_specs=[HBM_SPEC] * (2 * n), out_shape=_hbm_like(srcs) + _hbm_like(lands),
        input_output_aliases={i: i for i in range(2 * n)},
        compiler_params=pltpu.CompilerParams(has_side_effects=IN_FLIGHT),
    )(*srcs, *lands, send_sems, recv_sems, after)
    return outs[:n], outs[n:]


GATHER_LATE_SEMS = 2 * (NCHIP - 1)


def _gather_async_start(tag, shards, after):
    def start(srcs, lands, copy):
        k, c, chips, _ = _peers()
        out = []
        for a, (src, dst) in enumerate(zip(srcs, lands)):
            for j, (cx, cy) in enumerate(chips):
                for core in range(2):
                    out.append(copy(GATHER_LATE_SEMS * a + 2 * j + core, GATHER_LATE_SEMS * a + 2 * j + c,
                                    src.at[*_half(c, src.shape)], dst.at[k, *_half(c, src.shape)], (cx, cy, core)))
        return out

    chip = 2 * lax.axis_index("x") + lax.axis_index("y")
    lands = [lax.dynamic_update_slice(lax.empty((NCHIP,) + s.shape, s.dtype), s[None], (chip, 0, 0)) for s in shards]
    return _split_start("gather_%s_start" % tag, shards, lands, after, GATHER_LATE_SEMS * len(shards), start)


def _gather_async_wait(tag, send_sems, recv_sems, shards, lands, after):
    def waits(srcs, lands_, copy):
        _, c, chips, _ = _peers()
        out = []
        for a, (src, dst) in enumerate(zip(srcs, lands_)):
            for j, (cx, cy) in enumerate(chips):
                for core in range(2):
                    idx = GATHER_LATE_SEMS * a + 2 * j + core
                    out.append(copy(idx, idx, src.at[*_half(c, src.shape)], dst.at[2 * cx + cy, *_half(core, src.shape)], (cx, cy, core)))
        return out

    return _split_wait("gather_%s_wait" % tag, send_sems, recv_sems, shards, lands, after, waits)[1]


def _other_devices():
    x, y, c = _coords()
    out = []
    for d in range(1, NDEV):
        tx, ty, tc = x ^ (d >> 2), y ^ ((d >> 1) & 1), c ^ (d & 1)
        out.append((d, (tx, ty, tc), 2 * tx + ty, 4 * tx + 2 * ty + tc))
    return out


def _reduce_async_start(tag, stacks, after):
    def start(srcs, lands, copy):
        x, y, c = _coords()
        me = 4 * x + 2 * y + c
        return [copy((NDEV - 1) * a + d - 1, (NDEV - 1) * a + d - 1, src.at[chip, *_half(to[2], src.shape)], dst.at[me], to)
                for a, (src, dst) in enumerate(zip(srcs, lands)) for d, to, chip, _ in _other_devices()]

    x, y, c = _coords()
    lands = []
    for s in stacks:
        hr, hc = _half_shape(s.shape)
        at = (c * hr, 0) if _half_axis(s.shape) == 0 else (0, c * hc)
        own = lax.dynamic_slice(s, (2 * x + y,) + at, (1, hr, hc))
        lands.append(lax.dynamic_update_slice(lax.empty((NDEV, hr, hc), s.dtype), own, (4 * x + 2 * y + c, 0, 0)))
    return _split_start("reduce_%s_start" % tag, stacks, lands, after, (NDEV - 1) * len(stacks), start)


def _reduce_async_wait(tag, send_sems, recv_sems, stacks, lands, after):
    def waits(srcs, lands_, copy):
        return [copy((NDEV - 1) * a + d - 1, (NDEV - 1) * a + d - 1, src.at[chip, *_half(to[2], src.shape)], dst.at[pos], to)
                for a, (src, dst) in enumerate(zip(srcs, lands_)) for d, to, chip, pos in _other_devices()]

    return _split_wait("reduce_%s_wait" % tag, send_sems, recv_sems, stacks, lands, after, waits)[1]


def _reduce_finish(tag, arrived, dims):
    n_arr = len(arrived)

    def body(*refs):
        lands, fin = refs[:n_arr], refs[n_arr:2 * n_arr]
        send_sems, recv_sems = refs[2 * n_arr:]
        _, c, _, sibling = _peers()
        sends = []
        for a in range(n_arr):
            mine = fin[a].at[*_half(c, dims[a])]

            def device_sum(vs, vf, a=a, mine=mine):
                pltpu.sync_copy(lands[a], vs)
                acc = vs[0].astype(F32)
                for i in range(1, NDEV):
                    acc = acc + vs[i].astype(F32)
                vf[...] = acc
                pltpu.sync_copy(vf, mine)

            pl.run_scoped(device_sum, pltpu.VMEM((NDEV,) + _half_shape(dims[a]), BF16), pltpu.VMEM(_half_shape(dims[a]), F32))
            sends.append(pltpu.make_async_remote_copy(src_ref=mine, dst_ref=mine, send_sem=send_sems.at[a], recv_sem=recv_sems.at[a],
                                                      device_id=sibling, device_id_type=MESH))
            sends[-1].start()
        for a in range(n_arr):
            other = fin[a].at[*_half(1 - c, dims[a])]
            pltpu.make_async_remote_copy(src_ref=other, dst_ref=other, send_sem=send_sems.at[a], recv_sem=recv_sems.at[a],
                                         device_id=sibling, device_id_type=MESH).wait_recv()
        for cp in sends:
            cp.wait_send()

    any_spec = pl.BlockSpec(memory_space=pl.ANY)
    return pl.pallas_call(
        body, name="reduce_%s_finish" % tag, in_specs=[any_spec] * n_arr, out_specs=[any_spec] * n_arr,
        out_shape=[jax.ShapeDtypeStruct(d, F32) for d in dims],
        scratch_shapes=[pltpu.SemaphoreType.DMA((n_arr,)), pltpu.SemaphoreType.DMA((n_arr,))],
    )(*arrived)


SMALL = (("conv_w", SSD_K * SSD_XBC), ("conv_b", SSD_XBC), ("dt_bias", H), ("A_log", H), ("D", H), ("ssd_norm", SSD_INNER),
         ("q_norm", Q_RANK), ("kv_norm", KV_RANK), ("out_norm", SSD_INNER), ("ln_mix_g", D), ("ln_mix_b", D),
         ("ln_ffn_g", D), ("ln_ffn_b", D))
SMALL_ROWS = 120
NDEV = 8


def _allreduce_small(sv):
    def body(sv_ref, out_ref, slots, send_sems, recv_sems):
        x, y, c = _coords()
        me = 4 * x + 2 * y + c
        slots[me] = sv_ref[...]
        copies = []
        for d in range(1, NDEV):
            to = (x ^ (d >> 2), y ^ ((d >> 1) & 1), c ^ (d & 1))
            copies.append(pltpu.make_async_remote_copy(src_ref=sv_ref, dst_ref=slots.at[me], send_sem=send_sems.at[d - 1],
                                                       recv_sem=recv_sems.at[d - 1], device_id=to, device_id_type=MESH))
            copies[-1].start()
        for cp in copies:
            cp.wait_recv()
        for cp in copies:
            cp.wait_send()
        acc = slots[0]
        for i in range(1, NDEV):
            acc = acc + slots[i]
        out_ref[...] = acc

    vm = pl.BlockSpec(memory_space=pltpu.VMEM)
    return pl.pallas_call(
        body, name="allreduce_small", in_specs=[vm], out_specs=vm, out_shape=jax.ShapeDtypeStruct((SMALL_ROWS, LANE), F32),
        scratch_shapes=[pltpu.VMEM((NDEV, SMALL_ROWS, LANE), F32), pltpu.SemaphoreType.DMA((NDEV - 1,)),
                        pltpu.SemaphoreType.DMA((NDEV - 1,))],
    )(sv)


def _adamw_math(w, g, m, v):
    m2 = ADAM_B1 * m + (1.0 - ADAM_B1) * g
    v2 = ADAM_B2 * v + (1.0 - ADAM_B2) * (g * g)
    m_hat = m2 / (1.0 - ADAM_B1 ** ADAM_STEP)
    v_hat = v2 / (1.0 - ADAM_B2 ** ADAM_STEP)
    return -ADAM_LR * (m_hat / (jnp.sqrt(v_hat) + ADAM_EPS) + ADAM_WD * w), m2, v2


ADAM_BLOCK_BYTES = 2 * 1024 * 1024


def _adamw_big(w, g, m, v, *, name):
    r, c = w.shape

    def body(w_ref, g_ref, m_ref, v_ref, d_ref, m2_ref, v2_ref):
        d_ref[...], m2_ref[...], v2_ref[...] = _adamw_math(w_ref[...], g_ref[...], m_ref[...], v_ref[...])

    tr = max(t for t in range(8, r + 1, 8) if r % t == 0 and t * c * 4 <= ADAM_BLOCK_BYTES)
    steps, spec = r // tr, pl.BlockSpec((tr, c), lambda i: (i, 0))
    return pl.pallas_call(body, name=name, grid=(steps,), in_specs=[spec] * 4, out_specs=[spec] * 3,
                          out_shape=[jax.ShapeDtypeStruct((r, c), F32)] * 3)(w, g, m, v)


def _adamw_small(ws, gs, ms, vs):
    n = len(ws)

    def body(*refs):
        for i in range(n):
            w_ref, g_ref, m_ref, v_ref = (refs[j * n + i] for j in range(4))
            d_ref, m2_ref, v2_ref = (refs[(4 + j) * n + i] for j in range(3))
            d_ref[...], m2_ref[...], v2_ref[...] = _adamw_math(w_ref[...], g_ref[...], m_ref[...], v_ref[...])

    vm = pl.BlockSpec(memory_space=pltpu.VMEM)
    shapes = [jax.ShapeDtypeStruct(w.shape, F32) for w in ws]
    outs = pl.pallas_call(body, name="adamw_small", in_specs=[vm] * (4 * n), out_specs=[vm] * (3 * n), out_shape=shapes * 3)(
        *ws, *gs, *ms, *vs)
    return outs[:n], outs[n:2 * n], outs[2 * n:]


_SMALL_ARG = {"conv_w": "ssd_conv_w", "conv_b": "ssd_conv_b", "dt_bias": "ssd_dt_bias", "A_log": "ssd_A_log", "D": "ssd_D",
              "ssd_norm": "ssd_norm_w", "q_norm": "mla_q_norm_w", "kv_norm": "mla_kv_norm_w", "out_norm": "mla_out_norm_w",
              "ln_mix_g": "ln_mix_g", "ln_mix_b": "ln_mix_b", "ln_ffn_g": "ln_ffn_g", "ln_ffn_b": "ln_ffn_b"}
_BIG_ARG = {"w_in": "w_in", "w_qb": "mla_w_q_b", "w_kvb": "mla_w_kv_b", "w_out": "w_out", "w_gate": "w_ffn_gate",
            "w_up": "w_ffn_up", "w_down": "w_ffn_down", "w_pg": "w_ple_gate", "w_pp": "w_ple_proj"}
_WEIGHT_ORDER = ("w_in", "ssd_conv_w", "ssd_conv_b", "ssd_dt_bias", "ssd_A_log", "ssd_D", "ssd_norm_w", "mla_q_norm_w", "mla_w_q_b",
                 "mla_kv_norm_w", "mla_w_kv_b", "mla_out_norm_w", "w_out", "ln_mix_g", "ln_mix_b", "w_ffn_gate", "w_ffn_up",
                 "w_ffn_down", "w_ple_gate", "w_ple_proj", "ln_ffn_g", "ln_ffn_b")


def _rows128(a):
    flat = a.reshape(-1)
    return jnp.pad(flat, (0, -flat.shape[0] % LANE)).reshape(-1, LANE)


def kernel(x, p, positions, w_in, ssd_conv_w, ssd_conv_b, ssd_dt_bias, ssd_A_log, ssd_D, ssd_norm_w, mla_q_norm_w, mla_w_q_b, mla_kv_norm_w, mla_w_kv_b, mla_out_norm_w, w_out, ln_mix_g, ln_mix_b, w_ffn_gate, w_ffn_up, w_ffn_down, w_ple_gate, w_ple_proj, ln_ffn_g, ln_ffn_b, loss_target, m_w_in, m_ssd_conv_w, m_ssd_conv_b, m_ssd_dt_bias, m_ssd_A_log, m_ssd_D, m_ssd_norm_w, m_mla_q_norm_w, m_mla_w_q_b, m_mla_kv_norm_w, m_mla_w_kv_b, m_mla_out_norm_w, m_w_out, m_ln_mix_g, m_ln_mix_b, m_w_ffn_gate, m_w_ffn_up, m_w_ffn_down, m_w_ple_gate, m_w_ple_proj, m_ln_ffn_g, m_ln_ffn_b, v_w_in, v_ssd_conv_w, v_ssd_conv_b, v_ssd_dt_bias, v_ssd_A_log, v_ssd_D, v_ssd_norm_w, v_mla_q_norm_w, v_mla_w_q_b, v_mla_kv_norm_w, v_mla_w_kv_b, v_mla_out_norm_w, v_w_out, v_ln_mix_g, v_ln_mix_b, v_w_ffn_gate, v_w_ffn_up, v_w_ffn_down, v_w_ple_gate, v_w_ple_proj, v_ln_ffn_g, v_ln_ffn_b):
    given = dict(locals())
    chip = 2 * lax.axis_index("x") + lax.axis_index("y")

    def local(name, prefix=""):
        a = given[prefix + _BIG_ARG[name]][0]
        return a.T if name in TRANSPOSED else a

    def updated(name, prefix=""):
        if name in ROW_MAJOR:
            _, c, r = given[prefix + _BIG_ARG[name]].shape
            return given[prefix + _BIG_ARG[name]].reshape(c // LANE, LANE, r).transpose(2, 0, 1).reshape(-1, LANE)
        return local(name, prefix)

    def global_layout(name, arr):
        if name in ROW_MAJOR:
            r, c = local(name).shape
            return arr.reshape(r, c // LANE, LANE).transpose(1, 2, 0).reshape(1, c, r)
        return (arr.T if name in TRANSPOSED else arr)[None]

    conv_bits = lax.bitcast_convert_type(ssd_conv_w[0], BF16).reshape(SSD_K, 2 * CONV_SHARD)
    w_in_all, conv_all = _gather_weights([local("w_in").astype(BF16), jnp.pad(conv_bits, ((0, BF16_ROWS - SSD_K), (0, 0)))])
    sp = {k: given[a] for k, a in _SMALL_ARG.items() if k != "conv_w"}
    sp["conv_w"] = _from_cols(lax.bitcast_convert_type(conv_all[:, :SSD_K].reshape(NCHIP, SSD_K, CONV_SHARD, 2), F32))
    gathering, tie = {}, w_in_all
    for group, names in ASYNC_GROUPS:
        gathering[group], tie = _gather_async_start(group, [local(name).astype(BF16) for name in names], tie)

    def fetch(group, after):
        return dict(zip(dict(ASYNC_GROUPS)[group], _gather_async_wait(group, *gathering[group], after)))

    reducing = {}

    def send(group, grads):
        reducing[group], sent = _reduce_async_start(group, [grads[name] for name in dict(ASYNC_GROUPS)[group]], None)
        return sent[0, 0]

    loss_row, grad_x, dw_in, g = _local_step(x[0], p[0, 0], positions[0], loss_target[0], w_in_all, fetch, send, sp, tie[0, 0])

    reducing["in"], tie = _reduce_async_start("in", [dw_in], grad_x)
    gbig = {}
    for group, names in reversed(ASYNC_GROUPS):
        arrived = _reduce_async_wait(group, *reducing[group], tie)
        gbig.update(zip(names, _reduce_finish(group, arrived, [local(name).shape for name in names])))
    small_in = jnp.concatenate([_rows128(g[name]) for name, _ in SMALL] + [loss_row], axis=0)
    small_sum = _allreduce_small(jnp.pad(small_in, ((0, SMALL_ROWS - small_in.shape[0]), (0, 0))))
    gsmall, row = {}, 0
    for name, size in SMALL:
        nrow = -(-size // LANE)
        gsmall[name] = small_sum[row:row + nrow].reshape(-1)[:size]
        row += nrow
    loss = small_sum[row, 0]

    grads = {_BIG_ARG[name]: global_layout(name, arr) for name, arr in gbig.items()}
    for name, _ in SMALL:
        if name == "conv_w":
            full_g = gsmall[name].reshape(SSD_K, SSD_XBC)
            grads["ssd_conv_w"] = lax.dynamic_slice(full_g, (0, chip * CONV_SHARD), (SSD_K, CONV_SHARD))[None]
        else:
            grads[_SMALL_ARG[name]] = gsmall[name].reshape(given[_SMALL_ARG[name]].shape)

    delta, new_m, new_v = {}, {}, {}

    def update_matrix(name, grad):
        a = _BIG_ARG[name]
        d, m2, v2 = _adamw_big(updated(name), grad, updated(name, "m_"), updated(name, "v_"), name="adamw_" + a)
        delta[a], new_m[a], new_v[a] = (global_layout(name, t) for t in (d, m2, v2))
        return d

    for name, grad in gbig.items():
        last = update_matrix(name, grad)
    g_in = _reduce_finish("in", _reduce_async_wait("in", *reducing["in"], last), [updated("w_in").shape])[0]
    grads["w_in"] = global_layout("w_in", g_in)
    update_matrix("w_in", g_in)
    small_names = [_SMALL_ARG[name] for name, _ in SMALL]
    two_d = lambda t: t.reshape(t.shape[-2], t.shape[-1])
    ds, ms, vs = _adamw_small([two_d(given[a]) for a in small_names], [two_d(grads[a]) for a in small_names],
                              [two_d(given["m_" + a]) for a in small_names], [two_d(given["v_" + a]) for a in small_names])
    for a, d, m2, v2 in zip(small_names, ds, ms, vs):
        delta[a], new_m[a], new_v[a] = (t.reshape(given[a].shape) for t in (d, m2, v2))

    return (loss, grad_x[None], *[grads[n] for n in _WEIGHT_ORDER], *[delta[n] for n in _WEIGHT_ORDER],
            *[new_m[n] for n in _WEIGHT_ORDER], *[new_v[n] for n in _WEIGHT_ORDER])
```

```python
import functools
import math

import jax
import jax.numpy as jnp
from jax import lax
from jax.experimental import pallas as pl
from jax.experimental.pallas import tpu as pltpu

F32 = jnp.float32
BF16 = jnp.bfloat16

S = 2048
D = 1024
PLE = 256
H = 16
SSD_P = 64
SSD_INNER = 1024
SSD_N = 128
SSD_G = 2
SSD_L = 128
SSD_NC = S // SSD_L
SSD_XBC = 1536
SSD_K = 4
Q_RANK = 384
KV_RANK = 256
NOPE = 64
ROPE = 32
VDIM = 64
D_FF = 2816
IN_WIDTH = 3248
ALPHA = 2.0 ** 0.25
EPS_RMS = 1e-6
EPS_LN = 1e-5
ATT_SCALE = 1.0 / math.sqrt(NOPE + ROPE)
LN2 = math.log(2.0)
ATT_SCALE_LOG2 = ATT_SCALE / LN2
LANE = 128
NCHIP = 4
SMALL_W = 896
SM_Q, SM_KV, SM_KR, SM_DT = 0, 384, 640, 768
NEG = -1e30

ADAM_LR = 0.001
ADAM_B1 = 0.9
ADAM_B2 = 0.999
ADAM_EPS = 1e-08
ADAM_WD = 0.01
ADAM_STEP = 10


def _sigmoid(v):
    return 1.0 / (1.0 + jnp.exp(-v))


MM_VMEM_BUDGET = 36 * 2 ** 20
MM_MAX_ACC = 2048 * 1024


def _mm_tiles(pairs, ks, m, n, out_dtype, has_add):
    def divs(v):
        return [LANE * d for d in range(v // LANE, 0, -1) if (v // LANE) % d == 0] if v % LANE == 0 else [v]

    def cost(tm, tn):
        tot = tm * tn * (jnp.dtype(out_dtype).itemsize + (4 if has_add else 0))
        for (a, b), k in zip(pairs, ks):
            tot += k * (tm * a.dtype.itemsize + tn * b.dtype.itemsize)
        return 2 * tot

    ok = [(tm * tn, tm, tn) for tm in divs(m) for tn in divs(n) if tm * tn <= MM_MAX_ACC and cost(tm, tn) <= MM_VMEM_BUDGET]
    _, tm, tn = max(ok)
    return tm, tn


def _mm(pairs, *, ta=False, tb=False, out_dtype=F32, add=None, add_scale=1.0, chunk=None, name):
    n_pairs = len(pairs)
    windows = [pr[2] if len(pr) == 3 else None for pr in pairs]
    pairs = [pr[:2] for pr in pairs]
    assert not ((ta or tb) and any(windows))
    ks = [w[2] if w else (a.shape[-2] if ta else a.shape[-1]) for (a, _), w in zip(pairs, windows)]
    a0, b0 = pairs[0]
    m = a0.shape[-1] if ta else a0.shape[-2]
    n = b0.shape[-2] if tb else b0.shape[-1]
    tm, tn = _mm_tiles(pairs, ks, m, n, out_dtype, add is not None)
    dims = (((0 if ta else 1,), (1 if tb else 0,)), ((), ()))
    nk = NCHIP if chunk else 1
    assert chunk != "sum" or out_dtype == F32

    def body(*refs):
        o_ref = refs[-1]
        acc = None
        for i in range(n_pairs):
            a = refs[2 * i][...].astype(BF16)
            b = refs[2 * i + 1][...].astype(BF16)
            part = lax.dot_general(a, b, dims, preferred_element_type=F32)
            acc = part if acc is None else acc + part
        if chunk == "sum":
            k = pl.program_id(2)

            @pl.when(k == 0)
            def _():
                o_ref[...] = acc + add_scale * refs[2 * n_pairs][...] if add is not None else acc

            @pl.when(k > 0)
            def _():
                o_ref[...] += acc
        else:
            if add is not None:
                acc = acc + add_scale * refs[2 * n_pairs][...]
            o_ref[...] = acc.astype(out_dtype)

    def spec(arr, shape, idx2):
        if arr.ndim == 3:
            return pl.BlockSpec((None,) + shape, lambda i, j, k: (k,) + idx2(i, j))
        return pl.BlockSpec(shape, lambda i, j, k: idx2(i, j))

    in_specs, args = [], []
    for (a, b), kdim, window in zip(pairs, ks, windows):
        ka, kb = window[:2] if window else (0, 0)
        in_specs.append(spec(a, (kdim, tm), lambda i, j: (0, i)) if ta else spec(a, (tm, kdim), lambda i, j, ka=ka: (i, ka)))
        in_specs.append(spec(b, (tn, kdim), lambda i, j: (j, 0)) if tb else spec(b, (kdim, tn), lambda i, j, kb=kb: (kb, j)))
        args += [a, b]
    if add is not None:
        in_specs.append(pl.BlockSpec((tm, tn), lambda i, j, k: (i, j)))
        args.append(add)
    if chunk == "out":
        out_spec = pl.BlockSpec((None, tm, tn), lambda i, j, k: (k, i, j))
        out_shape = jax.ShapeDtypeStruct((nk, m, n), out_dtype)
    else:
        out_spec = pl.BlockSpec((tm, tn), lambda i, j, k: (i, j))
        out_shape = jax.ShapeDtypeStruct((m, n), out_dtype)
    return pl.pallas_call(
        body, name=name, grid=(m // tm, n // tn, nk), in_specs=in_specs, out_specs=out_spec, out_shape=out_shape,
        compiler_params=pltpu.CompilerParams(dimension_semantics=("parallel", "parallel", "arbitrary")),
    )(*args)


TR = 256


def _row_spec(c):
    return pl.BlockSpec((TR, c), lambda i: (i, 0))


def _vec_spec(c):
    return pl.BlockSpec((1, c), lambda i: (0, 0))


def _acc_rows(ref, val):
    @pl.when(pl.program_id(0) == 0)
    def _():
        ref[...] = jnp.zeros_like(ref)
    ref[...] += val


def _rms_fwd(u, w, *, name):
    c = u.shape[1]

    def body(u_ref, w_ref, o_ref):
        v = u_ref[...]
        r = lax.rsqrt(jnp.mean(v * v, axis=-1, keepdims=True) + EPS_RMS)
        o_ref[...] = (v * r * w_ref[...]).astype(BF16)

    return pl.pallas_call(body, name=name, grid=(S // TR,), in_specs=[_row_spec(c), _vec_spec(c)], out_specs=_row_spec(c),
                          out_shape=jax.ShapeDtypeStruct((S, c), BF16))(u, w)


def _rms_bwd(u, w, dy, *, name):
    c = u.shape[1]

    def body(u_ref, w_ref, dy_ref, du_ref, dw_ref):
        v = u_ref[...]
        g = dy_ref[...].astype(F32)
        r = lax.rsqrt(jnp.mean(v * v, axis=-1, keepdims=True) + EPS_RMS)
        gw = g * w_ref[...]
        du_ref[...] = r * gw - v * (r * r * r * jnp.mean(gw * v, axis=-1, keepdims=True))
        _acc_rows(dw_ref, jnp.sum(g * v * r, axis=0, keepdims=True))

    return pl.pallas_call(body, name=name, grid=(S // TR,), in_specs=[_row_spec(c), _vec_spec(c), _row_spec(c)],
                          out_specs=[_row_spec(c), _vec_spec(c)],
                          out_shape=[jax.ShapeDtypeStruct((S, c), F32), jax.ShapeDtypeStruct((1, c), F32)])(u, w, dy)


def _gate_norm_fwd(y, z, w):
    def body(y_ref, z_ref, w_ref, o_ref):
        zz = z_ref[...]
        v = y_ref[...] * (zz * _sigmoid(zz))
        r = lax.rsqrt(jnp.mean(v * v, axis=-1, keepdims=True) + EPS_RMS)
        o_ref[...] = (v * r * w_ref[...]).astype(BF16)

    c = SSD_INNER
    return pl.pallas_call(body, name="ssd_gate_norm_fwd", grid=(S // TR,), in_specs=[_row_spec(c), _row_spec(c), _vec_spec(c)],
                          out_specs=_row_spec(c), out_shape=jax.ShapeDtypeStruct((S, c), BF16))(y, z, w)


def _gate_norm_bwd(y, z, w, dout):
    def body(y_ref, z_ref, w_ref, g_ref, dy_ref, dz_ref, dw_ref):
        yy = y_ref[...]
        zz = z_ref[...]
        sg = _sigmoid(zz)
        sz = zz * sg
        v = yy * sz
        g = g_ref[...]
        r = lax.rsqrt(jnp.mean(v * v, axis=-1, keepdims=True) + EPS_RMS)
        gw = g * w_ref[...]
        dv = r * gw - v * (r * r * r * jnp.mean(gw * v, axis=-1, keepdims=True))
        dy_ref[...] = dv * sz
        dz_ref[...] = (dv * yy * (sg * (1.0 + zz * (1.0 - sg)))).astype(BF16)
        _acc_rows(dw_ref, jnp.sum(g * v * r, axis=0, keepdims=True))

    c = SSD_INNER
    return pl.pallas_call(body, name="ssd_gate_norm_bwd", grid=(S // TR,),
                          in_specs=[_row_spec(c), _row_spec(c), _vec_spec(c), _row_spec(c)],
                          out_specs=[_row_spec(c), _row_spec(c), _vec_spec(c)],
                          out_shape=[jax.ShapeDtypeStruct((S, c), F32), jax.ShapeDtypeStruct((S, c), BF16),
                                     jax.ShapeDtypeStruct((1, c), F32)])(y, z, w, dout)


def _ln_fwd(xr, mix, g, b):
    def body(x_ref, m_ref, g_ref, b_ref, o_ref, ob_ref):
        pre = ALPHA * x_ref[...] + m_ref[...]
        mu = jnp.mean(pre, axis=-1, keepdims=True)
        d = pre - mu
        rs = lax.rsqrt(jnp.mean(d * d, axis=-1, keepdims=True) + EPS_LN)
        h = d * rs * g_ref[...] + b_ref[...]
        o_ref[...] = h
        ob_ref[...] = h.astype(BF16)

    return pl.pallas_call(body, name="ln_mix_fwd", grid=(S // TR,), in_specs=[_row_spec(D), _row_spec(D), _vec_spec(D), _vec_spec(D)],
                          out_specs=[_row_spec(D)] * 2,
                          out_shape=[jax.ShapeDtypeStruct((S, D), F32), jax.ShapeDtypeStruct((S, D), BF16)])(xr, mix, g, b)


def _ln_bwd(xr, mix, g, dh):
    def body(x_ref, m_ref, g_ref, dh_ref, dpre_ref, dpreb_ref, dg_ref, db_ref):
        pre = ALPHA * x_ref[...] + m_ref[...]
        mu = jnp.mean(pre, axis=-1, keepdims=True)
        d = pre - mu
        rs = lax.rsqrt(jnp.mean(d * d, axis=-1, keepdims=True) + EPS_LN)
        xh = d * rs
        dy = dh_ref[...]
        gy = dy * g_ref[...]
        dpre = rs * (gy - jnp.mean(gy, axis=-1, keepdims=True) - xh * jnp.mean(gy * xh, axis=-1, keepdims=True))
        dpre_ref[...] = dpre
        dpreb_ref[...] = dpre.astype(BF16)
        _acc_rows(dg_ref, jnp.sum(dy * xh, axis=0, keepdims=True))
        _acc_rows(db_ref, jnp.sum(dy, axis=0, keepdims=True))

    return pl.pallas_call(body, name="ln_mix_bwd", grid=(S // TR,),
                          in_specs=[_row_spec(D), _row_spec(D), _vec_spec(D), _row_spec(D)],
                          out_specs=[_row_spec(D), _row_spec(D), _vec_spec(D), _vec_spec(D)],
                          out_shape=[jax.ShapeDtypeStruct((S, D), F32), jax.ShapeDtypeStruct((S, D), BF16),
                                     jax.ShapeDtypeStruct((1, D), F32), jax.ShapeDtypeStruct((1, D), F32)])(xr, mix, g, dh)


FF_CHUNK = D_FF // NCHIP


FF_ROWS = 1024


def _ff_act_spec():
    return pl.BlockSpec((None, FF_ROWS, FF_CHUNK), lambda i, k: (k, i, 0))


def _ff_w_spec():
    return pl.BlockSpec((None, FF_CHUNK, D), lambda i, k: (k, 0, 0))


def _ffn_hidden_fwd(h, w_gate_t, w_up_t):
    def body(h_ref, wg_ref, wu_ref, g_ref, u_ref, a_ref):
        hh = h_ref[...]
        g = _dot(hh, wg_ref[...], ((1,), (1,)))
        u = _dot(hh, wu_ref[...], ((1,), (1,)))
        g_ref[...] = g.astype(BF16)
        u_ref[...] = u.astype(BF16)
        a_ref[...] = (g * _sigmoid(g) * u).astype(BF16)

    return pl.pallas_call(
        body, name="ffn_hidden_fwd", grid=(S // FF_ROWS, NCHIP),
        in_specs=[pl.BlockSpec((FF_ROWS, D), lambda i, k: (i, 0)), _ff_w_spec(), _ff_w_spec()], out_specs=[_ff_act_spec()] * 3,
        out_shape=[jax.ShapeDtypeStruct((NCHIP, S, FF_CHUNK), BF16)] * 3,
        compiler_params=pltpu.CompilerParams(dimension_semantics=("parallel", "parallel")),
    )(h, w_gate_t, w_up_t)


def _ffn_hidden_bwd(dout, w_down, gate, up):
    def body(d_ref, wd_ref, g_ref, u_ref, dg_ref, du_ref):
        d = _dot(d_ref[...], wd_ref[...], ((1,), (1,)))
        g = g_ref[...].astype(F32)
        sg = _sigmoid(g)
        dg_ref[...] = (d * u_ref[...].astype(F32) * (sg * (1.0 + g * (1.0 - sg)))).astype(BF16)
        du_ref[...] = (d * g * sg).astype(BF16)

    return pl.pallas_call(
        body, name="ffn_hidden_bwd", grid=(S // FF_ROWS, NCHIP),
        in_specs=[pl.BlockSpec((FF_ROWS, D), lambda i, k: (i, 0)), _ff_w_spec(), _ff_act_spec(), _ff_act_spec()],
        out_specs=[_ff_act_spec()] * 2, out_shape=[jax.ShapeDtypeStruct((NCHIP, S, FF_CHUNK), BF16)] * 2,
        compiler_params=pltpu.CompilerParams(dimension_semantics=("parallel", "parallel")),
    )(dout, w_down, gate, up)


def _final_fwd_bwd(h1, ffn, pg, pp, target, g2, b2):
    def body(h_ref, f_ref, pg_ref, pp_ref, t_ref, g_ref, b_ref, dpre_ref, dpreb_ref, dpg_ref, dpp_ref, dg_ref, db_ref, loss_ref):
        sg = _sigmoid(pg_ref[...])
        ppv = pp_ref[...]
        pre = ALPHA * h_ref[...] + f_ref[...] + sg * ppv
        mu = jnp.mean(pre, axis=-1, keepdims=True)
        d = pre - mu
        rs = lax.rsqrt(jnp.mean(d * d, axis=-1, keepdims=True) + EPS_LN)
        xh = d * rs
        err = xh * g_ref[...] + b_ref[...] - t_ref[...]
        dy = err * (1.0 / D)
        gy = dy * g_ref[...]
        dpre = rs * (gy - jnp.mean(gy, axis=-1, keepdims=True) - xh * jnp.mean(gy * xh, axis=-1, keepdims=True))
        dpre_ref[...] = dpre
        dpreb_ref[...] = dpre.astype(BF16)
        dpg_ref[...] = (dpre * ppv * sg * (1.0 - sg)).astype(BF16)
        dpp_ref[...] = (dpre * sg).astype(BF16)
        _acc_rows(dg_ref, jnp.sum(dy * xh, axis=0, keepdims=True))
        _acc_rows(db_ref, jnp.sum(dy, axis=0, keepdims=True))
        _acc_rows(loss_ref, 0.5 * jnp.sum(jnp.mean(err * err, axis=-1, keepdims=True), axis=0, keepdims=True) * jnp.ones((1, LANE), F32))

    return pl.pallas_call(
        body, name="final_ln_loss", grid=(S // TR,),
        in_specs=[_row_spec(D)] * 5 + [_vec_spec(D)] * 2,
        out_specs=[_row_spec(D)] * 4 + [_vec_spec(D), _vec_spec(D), _vec_spec(LANE)],
        out_shape=[jax.ShapeDtypeStruct((S, D), F32)] + [jax.ShapeDtypeStruct((S, D), BF16)] * 3 + [
                   jax.ShapeDtypeStruct((1, D), F32), jax.ShapeDtypeStruct((1, D), F32), jax.ShapeDtypeStruct((1, LANE), F32)],
    )(h1, ffn, pg, pp, target, g2, b2)


def _rot(u, cos_t, sin_t, lane):
    partner = jnp.where(lane < NOPE + ROPE // 2, pltpu.roll(u, LANE - ROPE // 2, 1), pltpu.roll(u, ROPE // 2, 1))
    return u * cos_t + partner * sin_t


def _rms(v, w):
    r = lax.rsqrt(jnp.mean(v * v, axis=-1, keepdims=True) + EPS_RMS)
    return v * r * w, r


def _rms_grad(v, r, w, g):
    gw = g * w
    return r * gw - v * (r * r * r * jnp.mean(gw * v, axis=-1, keepdims=True)), jnp.sum(g * v * r, axis=0, keepdims=True)


def _whole(arr):
    return pl.BlockSpec(arr.shape, lambda i: (0,) * arr.ndim)


def _qkv_fwd(small, w_q, w_k, w_v, q_norm, kv_norm, cos_t, sin_t):
    def body(sm_ref, wq_ref, wk_ref, wv_ref, qw_ref, kw_ref, c_ref, s_ref, qn_ref, kvn_ref, q_ref, k_ref, kt_ref, v_ref):
        lane = lax.broadcasted_iota(jnp.int32, (TR, LANE), 1)
        c, s = c_ref[...], s_ref[...]
        qn = _rms(sm_ref[:, SM_Q:SM_Q + Q_RANK], qw_ref[...])[0].astype(BF16)
        kvn = _rms(sm_ref[:, SM_KV:SM_KV + KV_RANK], kw_ref[...])[0].astype(BF16)
        qn_ref[...] = qn
        kvn_ref[...] = kvn
        kr = _rot(pltpu.roll(sm_ref[:, SM_KR:SM_KR + LANE], NOPE, 1), c, s, lane)
        for h in range(H):
            tile = slice(h * LANE, (h + 1) * LANE)
            q_ref[:, tile] = _rot(_dot(qn, wq_ref[:, tile], ((1,), (0,))), c, s, lane).astype(BF16)
            kt = _dot(kvn, wk_ref[:, tile], ((1,), (0,))) + kr
            k_ref[:, tile] = kt.astype(BF16)
            kt_ref[tile, :] = kt.T.astype(BF16)
        v_ref[...] = _dot(kvn, wv_ref[...], ((1,), (0,))).astype(BF16)

    w = H * LANE
    return pl.pallas_call(
        body, name="qkv_fwd", grid=(S // TR,),
        in_specs=[_row_spec(SMALL_W), _whole(w_q), _whole(w_k), _whole(w_v), _vec_spec(Q_RANK), _vec_spec(KV_RANK), _row_spec(LANE), _row_spec(LANE)],
        out_specs=[_row_spec(Q_RANK), _row_spec(KV_RANK), _row_spec(w), _row_spec(w), pl.BlockSpec((w, TR), lambda i: (0, i)),
                   _row_spec(H * VDIM)],
        out_shape=[jax.ShapeDtypeStruct((S, Q_RANK), BF16), jax.ShapeDtypeStruct((S, KV_RANK), BF16), jax.ShapeDtypeStruct((S, w), BF16),
                   jax.ShapeDtypeStruct((S, w), BF16), jax.ShapeDtypeStruct((w, S), BF16), jax.ShapeDtypeStruct((S, H * VDIM), BF16)],
    )(small, w_q, w_k, w_v, q_norm, kv_norm, cos_t, sin_t)


def _qkv_bwd(dqt, dk, dv, small, w_q, w_k, w_v, q_norm, kv_norm, cos_t, sin_t):
    def body(dq_ref, dk_ref, dv_ref, sm_ref, wq_ref, wk_ref, wv_ref, qw_ref, kw_ref, c_ref, s_ref,
             ds_ref, dql_ref, dkb_ref, dqw_ref, dkw_ref):
        lane = lax.broadcasted_iota(jnp.int32, (TR, LANE), 1)
        c, s = c_ref[...], -s_ref[...]
        dqn = jnp.zeros((TR, Q_RANK), F32)
        dkvn = _dot(dv_ref[...], wv_ref[...], ((1,), (1,)))
        dkr = jnp.zeros((TR, LANE), F32)
        for h in range(H):
            tile = slice(h * LANE, (h + 1) * LANE)
            dql = _rot(dq_ref[tile, :].T, c, s, lane).astype(BF16)
            dql_ref[:, tile] = dql
            dqn = dqn + _dot(dql, wq_ref[:, tile], ((1,), (1,)))
            dkt = dk_ref[:, tile]
            dkb_ref[:, tile] = dkt.astype(BF16)
            dkvn = dkvn + _dot(dkt, wk_ref[:, tile], ((1,), (1,)))
            dkr = dkr + dkt
        dkr = jnp.where((lane >= NOPE) & (lane < NOPE + ROPE), dkr, 0.0)
        q_c, kv_c = sm_ref[:, SM_Q:SM_Q + Q_RANK], sm_ref[:, SM_KV:SM_KV + KV_RANK]
        dq_c, dqw = _rms_grad(q_c, _rms(q_c, qw_ref[...])[1], qw_ref[...], dqn)
        dkv_c, dkw = _rms_grad(kv_c, _rms(kv_c, kw_ref[...])[1], kw_ref[...], dkvn)
        ds_ref[:, SM_Q:SM_Q + Q_RANK] = dq_c.astype(BF16)
        ds_ref[:, SM_KV:SM_KV + KV_RANK] = dkv_c.astype(BF16)
        ds_ref[:, SM_KR:SM_KR + LANE] = pltpu.roll(_rot(dkr, c, s, lane), LANE - NOPE, 1).astype(BF16)
        _acc_rows(dqw_ref, dqw)
        _acc_rows(dkw_ref, dkw)

    w = H * LANE
    return pl.pallas_call(
        body, name="qkv_bwd", grid=(S // TR,),
        in_specs=[pl.BlockSpec((w, TR), lambda i: (0, i)), _row_spec(w), _row_spec(H * VDIM), _row_spec(SMALL_W), _whole(w_q), _whole(w_k),
                  _whole(w_v), _vec_spec(Q_RANK), _vec_spec(KV_RANK), _row_spec(LANE), _row_spec(LANE)],
        out_specs=[_row_spec(SM_DT), _row_spec(w), _row_spec(w), _vec_spec(Q_RANK), _vec_spec(KV_RANK)],
        out_shape=[jax.ShapeDtypeStruct((S, SM_DT), BF16), jax.ShapeDtypeStruct((S, w), BF16), jax.ShapeDtypeStruct((S, w), BF16),
                   jax.ShapeDtypeStruct((1, Q_RANK), F32), jax.ShapeDtypeStruct((1, KV_RANK), F32)],
    )(dqt, dk, dv, small, w_q, w_k, w_v, q_norm, kv_norm, cos_t, sin_t)


CB = 256


def _shift_down(u, k, row):
    if k == 0:
        return u
    return jnp.where(row >= k, pltpu.roll(u, k, 0), 0.0)


def _shift_up(u, k, row):
    if k == 0:
        return u
    return jnp.where(row < S - k, pltpu.roll(u, S - k, 0), 0.0)


def _conv_fwd(u, w, b):
    def body(u_ref, w_ref, b_ref, o_ref):
        row = lax.broadcasted_iota(jnp.int32, (S, CB), 0)
        uu = u_ref[...]
        acc = b_ref[...] + w_ref[SSD_K - 1:SSD_K, :] * uu
        for k in range(SSD_K - 1):
            acc = acc + w_ref[k:k + 1, :] * _shift_down(uu, SSD_K - 1 - k, row)
        o_ref[...] = acc * _sigmoid(acc)

    c = u.shape[1]
    return pl.pallas_call(
        body, name="conv_fwd", grid=(c // CB,),
        in_specs=[pl.BlockSpec((S, CB), lambda j: (0, j)), pl.BlockSpec((SSD_K, CB), lambda j: (0, j)), pl.BlockSpec((1, CB), lambda j: (0, j))],
        out_specs=pl.BlockSpec((S, CB), lambda j: (0, j)), out_shape=jax.ShapeDtypeStruct((S, c), F32),
    )(u, w, b)


def _conv_bwd(u, w, b, dact):
    def body(u_ref, w_ref, b_ref, d_ref, du_ref, dw_ref, db_ref):
        row = lax.broadcasted_iota(jnp.int32, (S, CB), 0)
        uu = u_ref[...]
        sh = [_shift_down(uu, SSD_K - 1 - k, row) for k in range(SSD_K)]
        acc = b_ref[...]
        for k in range(SSD_K):
            acc = acc + w_ref[k:k + 1, :] * sh[k]
        sg = _sigmoid(acc)
        dacc = d_ref[...] * (sg * (1.0 + acc * (1.0 - sg)))
        du = w_ref[SSD_K - 1:SSD_K, :] * dacc
        for k in range(SSD_K - 1):
            du = du + w_ref[k:k + 1, :] * _shift_up(dacc, SSD_K - 1 - k, row)
        du_ref[...] = du.astype(BF16)
        for k in range(SSD_K):
            dw_ref[k:k + 1, :] = jnp.sum(dacc * sh[k], axis=0, keepdims=True)
        db_ref[...] = jnp.sum(dacc, axis=0, keepdims=True)

    c = u.shape[1]
    col = lambda r: pl.BlockSpec((r, CB), lambda j: (0, j))
    return pl.pallas_call(
        body, name="conv_bwd", grid=(c // CB,), in_specs=[col(S), col(SSD_K), col(1), col(S)], out_specs=[col(S), col(SSD_K), col(1)],
        out_shape=[jax.ShapeDtypeStruct((S, c), BF16), jax.ShapeDtypeStruct((SSD_K, c), F32), jax.ShapeDtypeStruct((1, c), F32)],
    )(u, w, b, dact)


NPAIR = H // 2
PAIRS_PER_GROUP = NPAIR // SSD_G


def _softplus(v):
    return jnp.maximum(v, 0.0) + jnp.log(1.0 + jnp.exp(-jnp.abs(v)))


def _dot(a, b, dims):
    return lax.dot_general(a.astype(BF16), b.astype(BF16), (dims, ((), ())), preferred_element_type=F32)


def _dot2(a, sel):
    hi = a.astype(BF16)
    lo = (a - hi.astype(F32)).astype(BF16)
    dims = (((1,), (0,)), ((), ()))
    return lax.dot_general(hi, sel, dims, preferred_element_type=F32) + lax.dot_general(lo, sel, dims, preferred_element_type=F32)


def _dot3(a, b, dims, split_lhs):
    v = a if split_lhs else b
    v1 = v.astype(BF16)
    r1 = v - v1.astype(F32)
    v2 = r1.astype(BF16)
    v3 = (r1 - v2.astype(F32)).astype(BF16)
    acc = None
    for part in (v1, v2, v3):
        lhs, rhs = (part, b) if split_lhs else (a, part)
        t = lax.dot_general(lhs, rhs, (dims, ((), ())), preferred_element_type=F32)
        acc = t if acc is None else acc + t
    return acc


def _ssd_chunk_common(dt_ref, dtT_ref, prow_ref, pcol_ref):
    prow = prow_ref[...]
    pcol = pcol_ref[...]
    ri = lax.broadcasted_iota(jnp.int32, (SSD_L, SSD_L), 0)
    ci = lax.broadcasted_iota(jnp.int32, (SSD_L, SSD_L), 1)
    causal = ri >= ci
    pre_c = dt_ref[...] + prow[0:1, :]
    dtc = _softplus(pre_c)
    a_row = -jnp.exp(prow[1:2, :])
    cs_col = _dot3(causal.astype(BF16), dtc * a_row, ((1,), (0,)), False)
    dtr = _softplus(dtT_ref[...] + pcol[:, 0:1])
    a_col = -jnp.exp(pcol[:, 1:2])
    cs_row = _dot3(dtr * a_col, (ri <= ci).astype(BF16), ((1,), (0,)), True)
    return prow, causal, pre_c, dtc, a_row, cs_col, cs_row


def _ssd_fwd(act, small, dtT, prow, pcol):
    def body(x_ref, b_ref, c_ref, dt_ref, dtT_ref, prow_ref, pcol_ref, y_ref, st_ref, state):
        @pl.when(pl.program_id(0) == 0)
        def _():
            state[...] = jnp.zeros_like(state)

        prow, causal, _, dtc, _, cs_col, cs_row = _ssd_chunk_common(dt_ref, dtT_ref, prow_ref, pcol_ref)
        lo = lax.broadcasted_iota(jnp.int32, (SSD_L, LANE), 1) < SSD_P
        lo1 = lo[0:1, :]
        for g in range(SSD_G):
            bm = b_ref[:, g * SSD_N:(g + 1) * SSD_N]
            cm = c_ref[:, g * SSD_N:(g + 1) * SSD_N]
            cb = _dot(cm, bm, ((1,), (1,)))
            for qq in range(PAIRS_PER_GROUP):
                q = g * PAIRS_PER_GROUP + qq
                ha, hb = 2 * q, 2 * q + 1
                csa, csb = cs_col[:, ha:ha + 1], cs_col[:, hb:hb + 1]
                xp = x_ref[:, q * LANE:(q + 1) * LANE]
                xx = xp * jnp.where(lo, dtc[:, ha:ha + 1], dtc[:, hb:hb + 1])
                ga = cb * jnp.exp(jnp.where(causal, csa - cs_row[ha:ha + 1, :], NEG))
                gb = cb * jnp.exp(jnp.where(causal, csb - cs_row[hb:hb + 1, :], NEG))
                y = _dot(ga, jnp.where(lo, xx, 0.0), ((1,), (0,))) + _dot(gb, jnp.where(lo, 0.0, xx), ((1,), (0,)))
                s_in = state[q]
                y = y + _dot(cm, s_in, ((1,), (0,))) * jnp.where(lo, jnp.exp(csa), jnp.exp(csb))
                y = y + jnp.where(lo1, prow[2:3, ha:ha + 1], prow[2:3, hb:hb + 1]) * xp
                y_ref[:, q * LANE:(q + 1) * LANE] = y
                la, lb = csa[SSD_L - 1:SSD_L, :], csb[SSD_L - 1:SSD_L, :]
                decay = jnp.where(lo, jnp.exp(la - csa), jnp.exp(lb - csb))
                st_ref[q] = s_in
                state[q] = s_in * jnp.where(lo1, jnp.exp(la), jnp.exp(lb)) + _dot(bm, xx * decay, ((0,), (0,)))

    L = SSD_L
    return pl.pallas_call(
        body, name="ssd_fwd", grid=(SSD_NC,),
        in_specs=[pl.BlockSpec((L, SSD_INNER), lambda c: (c, 0)),
                  pl.BlockSpec((L, SSD_G * SSD_N), lambda c: (c, SSD_INNER // (SSD_G * SSD_N))),
                  pl.BlockSpec((L, SSD_G * SSD_N), lambda c: (c, SSD_INNER // (SSD_G * SSD_N) + 1)),
                  pl.BlockSpec((L, LANE), lambda c: (c, SM_DT // LANE)),
                  pl.BlockSpec((LANE, L), lambda c: (0, c)),
                  pl.BlockSpec((8, LANE), lambda c: (0, 0)), pl.BlockSpec((LANE, 8), lambda c: (0, 0))],
        out_specs=[pl.BlockSpec((L, SSD_INNER), lambda c: (c, 0)),
                   pl.BlockSpec((None, NPAIR, SSD_N, LANE), lambda c: (c, 0, 0, 0))],
        out_shape=[jax.ShapeDtypeStruct((S, SSD_INNER), F32), jax.ShapeDtypeStruct((SSD_NC, NPAIR, SSD_N, LANE), F32)],
        scratch_shapes=[pltpu.VMEM((NPAIR, SSD_N, LANE), F32)],
        compiler_params=pltpu.CompilerParams(dimension_semantics=("arbitrary",)),
    )(act, act, act, small, dtT, prow, pcol)


def _ssd_bwd(act, small, dtT, prow, pcol, states, dy):
    def body(x_ref, b_ref, c_ref, dt_ref, dtT_ref, prow_ref, pcol_ref, st_ref, dy_ref,
             dx_ref, ddt_ref, dp_ref, dstate):
        @pl.when(pl.program_id(0) == 0)
        def _():
            dstate[...] = jnp.zeros_like(dstate)
            dp_ref[...] = jnp.zeros_like(dp_ref)

        prow, causal, pre_c, dtc, a_row, cs_col, cs_row = _ssd_chunk_common(dt_ref, dtT_ref, prow_ref, pcol_ref)
        lane = lax.broadcasted_iota(jnp.int32, (SSD_L, LANE), 1)
        sub = lax.broadcasted_iota(jnp.int32, (LANE, SSD_L), 0)
        rowi = lax.broadcasted_iota(jnp.int32, (SSD_L, 1), 0)
        pick_p = lax.broadcasted_iota(jnp.int32, (LANE, LANE), 0)
        pick_l = lax.broadcasted_iota(jnp.int32, (LANE, LANE), 1)
        lo = lane < SSD_P
        lo1 = lo[0:1, :]
        dcs_c = jnp.zeros((SSD_L, LANE), F32)
        dcs_r = jnp.zeros((LANE, SSD_L), F32)
        ddt_x = jnp.zeros((SSD_L, LANE), F32)
        dd_row = jnp.zeros((1, LANE), F32)
        for g in range(SSD_G):
            bm = b_ref[:, g * SSD_N:(g + 1) * SSD_N]
            cm = c_ref[:, g * SSD_N:(g + 1) * SSD_N]
            cb = _dot(cm, bm, ((1,), (1,)))
            dcb = jnp.zeros((SSD_L, SSD_L), F32)
            dbm = jnp.zeros((SSD_L, SSD_N), F32)
            dcm = jnp.zeros((SSD_L, SSD_N), F32)
            for qq in range(PAIRS_PER_GROUP):
                q = g * PAIRS_PER_GROUP + qq
                ha, hb = 2 * q, 2 * q + 1
                csa, csb = cs_col[:, ha:ha + 1], cs_col[:, hb:hb + 1]
                xp = x_ref[:, q * LANE:(q + 1) * LANE]
                dtp = jnp.where(lo, dtc[:, ha:ha + 1], dtc[:, hb:hb + 1])
                xx = xp * dtp
                lma = jnp.exp(jnp.where(causal, csa - cs_row[ha:ha + 1, :], NEG))
                lmb = jnp.exp(jnp.where(causal, csb - cs_row[hb:hb + 1, :], NEG))
                ga, gb = cb * lma, cb * lmb
                dyp = dy_ref[:, q * LANE:(q + 1) * LANE]
                dya, dyb = jnp.where(lo, dyp, 0.0), jnp.where(lo, 0.0, dyp)
                s_in = st_ref[q]
                ds_out = dstate[q]
                la, lb = csa[SSD_L - 1:SSD_L, :], csb[SSD_L - 1:SSD_L, :]
                ecs = jnp.where(lo, jnp.exp(csa), jnp.exp(csb))
                decay = jnp.where(lo, jnp.exp(la - csa), jnp.exp(lb - csb))
                cd = jnp.where(lo1, jnp.exp(la), jnp.exp(lb))
                bds = _dot(bm, ds_out, ((1,), (0,)))
                dxx = _dot(ga, dya, ((0,), (0,))) + _dot(gb, dyb, ((0,), (0,))) + bds * decay
                dga = _dot(dya, xx, ((1,), (1,)))
                dgb = _dot(dyb, xx, ((1,), (1,)))
                dsega, dsegb = dga * ga, dgb * gb
                dcb = dcb + dga * lma + dgb * lmb
                yoff = _dot(cm, s_in, ((1,), (0,))) * ecs
                dye = dyp * ecs
                dcm = dcm + _dot(dye, s_in, ((1,), (1,)))
                xd = xx * decay
                dbm = dbm + _dot(xd, ds_out, ((1,), (1,)))
                wv = xd * bds
                ends = jnp.sum(wv, axis=0, keepdims=True) + cd * jnp.sum(ds_out * s_in, axis=0, keepdims=True)
                t1 = dyp * yoff - wv + jnp.where(rowi == SSD_L - 1, ends, 0.0)
                to_pair = (((pick_p < SSD_P) & (pick_l == ha)) | ((pick_p >= SSD_P) & (pick_l == hb))).astype(BF16)
                to_a_b = jnp.concatenate([(pick_l == ha).astype(BF16), (pick_l == hb).astype(BF16)], axis=0)
                dcs_c = dcs_c + _dot2(t1, to_pair) + _dot2(jnp.concatenate([dsega, dsegb], axis=1), to_a_b)
                dcs_r = (dcs_r + jnp.where(sub == ha, jnp.sum(dsega, axis=0, keepdims=True), 0.0)
                         + jnp.where(sub == hb, jnp.sum(dsegb, axis=0, keepdims=True), 0.0))
                dstate[q] = _dot(cm, dye, ((0,), (0,))) + cd * ds_out
                dpair = jnp.where(lo1, prow[2:3, ha:ha + 1], prow[2:3, hb:hb + 1])
                dx_ref[:, q * LANE:(q + 1) * LANE] = dxx * dtp + dpair * dyp
                ddt_x = ddt_x + _dot2(dxx * xp, to_pair)
                dd_row = dd_row + jnp.sum(_dot2(dyp * xp, to_pair), axis=0, keepdims=True)
            dx_ref[:, SSD_INNER + g * SSD_N:SSD_INNER + (g + 1) * SSD_N] = dbm + _dot(dcb, cm, ((0,), (0,)))
            dx_ref[:, SSD_INNER + (SSD_G + g) * SSD_N:SSD_INNER + (SSD_G + g + 1) * SSD_N] = dcm + _dot(dcb, bm, ((1,), (0,)))
        ri = lax.broadcasted_iota(jnp.int32, (SSD_L, SSD_L), 0)
        ci = lax.broadcasted_iota(jnp.int32, (SSD_L, SSD_L), 1)
        da = _dot3((ri <= ci).astype(BF16), dcs_c, ((1,), (0,)), False)
        da = da - _dot3(dcs_r, causal.astype(BF16), ((1,), (0,)), True).T
        ddt = ddt_x + da * a_row
        ddt_raw = ddt * _sigmoid(pre_c)
        ddt_ref[...] = ddt_raw
        da_head = jnp.sum(da * dtc, axis=0, keepdims=True) * a_row
        dp_ref[0:1, :] += jnp.sum(ddt_raw, axis=0, keepdims=True)
        dp_ref[1:2, :] += da_head
        dp_ref[2:3, :] += dd_row

    L = SSD_L
    rev = SSD_NC - 1
    bc_cols = SSD_INNER // (SSD_G * SSD_N)
    return pl.pallas_call(
        body, name="ssd_bwd", grid=(SSD_NC,),
        in_specs=[pl.BlockSpec((L, SSD_INNER), lambda c: (rev - c, 0)),
                  pl.BlockSpec((L, SSD_G * SSD_N), lambda c: (rev - c, bc_cols)),
                  pl.BlockSpec((L, SSD_G * SSD_N), lambda c: (rev - c, bc_cols + 1)),
                  pl.BlockSpec((L, LANE), lambda c: (rev - c, SM_DT // LANE)),
                  pl.BlockSpec((LANE, L), lambda c: (0, rev - c)),
                  pl.BlockSpec((8, LANE), lambda c: (0, 0)), pl.BlockSpec((LANE, 8), lambda c: (0, 0)),
                  pl.BlockSpec((None, NPAIR, SSD_N, LANE), lambda c: (rev - c, 0, 0, 0)),
                  pl.BlockSpec((L, SSD_INNER), lambda c: (rev - c, 0))],
        out_specs=[pl.BlockSpec((L, SSD_XBC), lambda c: (rev - c, 0)),
                   pl.BlockSpec((L, LANE), lambda c: (rev - c, 0)),
                   pl.BlockSpec((8, LANE), lambda c: (0, 0))],
        out_shape=[jax.ShapeDtypeStruct((S, SSD_XBC), F32), jax.ShapeDtypeStruct((S, LANE), F32),
                   jax.ShapeDtypeStruct((8, LANE), F32)],
        scratch_shapes=[pltpu.VMEM((NPAIR, SSD_N, LANE), F32)],
        compiler_params=pltpu.CompilerParams(dimension_semantics=("arbitrary",)),
    )(act, act, act, small, dtT, prow, pcol, states, dy)


TQ = 256
TK = 256
FWD_TQ = 256
FWD_TK = 256


def _attn_fwd(qc, kc, v):
    TQ, TK = FWD_TQ, FWD_TK

    def body(q_ref, k_ref, v_ref, o_ref, lse_ref):
        i = pl.program_id(1)
        lo = lax.broadcasted_iota(jnp.int32, (TQ, LANE), 1) < VDIM
        lo_k = lax.broadcasted_iota(jnp.int32, (TK, LANE), 1) < VDIM
        row_minus_col = lax.broadcasted_iota(jnp.int32, (TQ, TK), 0) - lax.broadcasted_iota(jnp.int32, (TQ, TK), 1)
        qa, qb = q_ref[:, 0:LANE], q_ref[:, LANE:2 * LANE]

        def scores(kb):
            kk = k_ref[pl.ds(pl.multiple_of(kb * TK, TK), TK), :]
            return (_dot(qa, kk[:, 0:LANE], ((1,), (1,))) * ATT_SCALE_LOG2, _dot(qb, kk[:, LANE:2 * LANE], ((1,), (1,))) * ATT_SCALE_LOG2)

        def update(kb, sa, sb, stats):
            ma, la, mb, lb, acc = stats
            vv = v_ref[pl.ds(pl.multiple_of(kb * TK, TK), TK), :]
            na = jnp.maximum(ma, jnp.max(sa, axis=1, keepdims=True))
            nb = jnp.maximum(mb, jnp.max(sb, axis=1, keepdims=True))
            pa, pb = jnp.exp2(sa - na), jnp.exp2(sb - nb)
            fa, fb = jnp.exp2(ma - na), jnp.exp2(mb - nb)
            la = fa * la + jnp.sum(pa, axis=1, keepdims=True)
            lb = fb * lb + jnp.sum(pb, axis=1, keepdims=True)
            acc = (acc * jnp.where(lo, fa, fb) + _dot(pa, jnp.where(lo_k, vv, 0), ((1,), (0,)))
                   + _dot(pb, jnp.where(lo_k, 0, vv), ((1,), (0,))))
            return na, la, nb, lb, acc

        def step(kb, carry):
            sa, sb = carry[:2]
            nxt = scores(kb + 1)
            return nxt + update(kb, sa, sb, carry[2:])

        neg = jnp.full((TQ, 1), NEG, F32)
        zero = jnp.zeros((TQ, 1), F32)
        n_full = i * (TQ // TK)
        carry = lax.fori_loop(0, n_full, step, scores(0) + (neg, zero, neg, zero, jnp.zeros((TQ, LANE), F32)))
        s, stats = carry[:2], carry[2:]
        for d in range(TQ // TK):
            nxt = scores(n_full + d + 1) if d + 1 < TQ // TK else None
            sa, sb = (jnp.where(row_minus_col >= d * TK, t, NEG) for t in s)
            stats = update(n_full + d, sa, sb, stats)
            s = nxt
        ma, la, mb, lb, acc = stats
        o_ref[...] = acc / jnp.where(lo, la, lb)
        lse_ref[...] = jnp.where(lo, ma + jnp.log2(la), mb + jnp.log2(lb)) * LN2

    return pl.pallas_call(
        body, name="attn_fwd", grid=(NPAIR, S // TQ),
        in_specs=[pl.BlockSpec((TQ, 2 * LANE), lambda j, i: (i, j)), pl.BlockSpec((S, 2 * LANE), lambda j, i: (0, j)),
                  pl.BlockSpec((S, LANE), lambda j, i: (0, j))],
        out_specs=[pl.BlockSpec((TQ, LANE), lambda j, i: (i, j)), pl.BlockSpec((None, TQ, LANE), lambda j, i: (j, i, 0))],
        out_shape=[jax.ShapeDtypeStruct((S, H * VDIM), F32), jax.ShapeDtypeStruct((NPAIR, S, LANE), F32)],
        compiler_params=pltpu.CompilerParams(dimension_semantics=("parallel", "parallel")),
    )(qc, kc, v)


def _attn_rows(lse, o, do):
    def body(lse_ref, o_ref, do_ref, r_ref):
        lt = lse_ref[...].T * (1.0 / LN2)
        tt = (o_ref[...] * do_ref[...]).T
        r_ref[...] = jnp.zeros_like(r_ref)
        r_ref[0:1, :] = lt[0:1, :]
        r_ref[1:2, :] = lt[VDIM:VDIM + 1, :]
        r_ref[2:3, :] = jnp.sum(tt[0:VDIM, :], axis=0, keepdims=True)
        r_ref[3:4, :] = jnp.sum(tt[VDIM:LANE, :], axis=0, keepdims=True)

    tile = pl.BlockSpec((S, LANE), lambda j: (0, j))
    return pl.pallas_call(
        body, name="attn_rows", grid=(NPAIR,), in_specs=[pl.BlockSpec((None, S, LANE), lambda j: (j, 0, 0)), tile, tile],
        out_specs=pl.BlockSpec((None, 8, S), lambda j: (j, 0, 0)), out_shape=jax.ShapeDtypeStruct((NPAIR, 8, S), F32),
    )(lse, o, do)


def _attn_bwd(qc, kc, kct, v, do, rows):
    nq = S // TQ

    def body(q_ref, k_ref, kt_ref, v_ref, do_ref, r_ref, dqt_ref, dk_ref, dv_ref):
        kb = pl.program_id(1)

        @pl.when(kb == 0)
        def _():
            dqt_ref[...] = jnp.zeros_like(dqt_ref)

        lo = lax.broadcasted_iota(jnp.int32, (TK, LANE), 1) < VDIM
        q_minus_k = lax.broadcasted_iota(jnp.int32, (TK, TQ), 1) - lax.broadcasted_iota(jnp.int32, (TK, TQ), 0)
        vv = v_ref[...]
        kk = k_ref[...]

        def step(qi, carry):
            off = pl.multiple_of(qi * TQ, TQ)
            qq = q_ref[pl.ds(off, TQ), :]
            dd = do_ref[pl.ds(off, TQ), :].astype(BF16)
            rr = r_ref[:, pl.ds(off, TQ)]
            keep = q_minus_k >= (kb - qi) * TQ
            out = []
            for x in range(2):
                sel = lo if x == 0 else jnp.logical_not(lo)
                kx, qx = kk[:, x * LANE:(x + 1) * LANE], qq[:, x * LANE:(x + 1) * LANE]
                st = jnp.where(keep, _dot(kx, qx, ((1,), (1,))) * ATT_SCALE_LOG2, NEG)
                pt = jnp.exp2(st - rr[x:x + 1, :])
                dpt = _dot(jnp.where(sel, vv, 0), dd, ((1,), (1,)))
                dst = (pt * (dpt - rr[2 + x:3 + x, :]) * ATT_SCALE).astype(BF16)
                out.append(carry[x] + _dot(dst, qx, ((1,), (0,))))
                out.append(_dot(pt, jnp.where(sel, dd, 0), ((1,), (0,))))
                dqt_ref[x * LANE:(x + 1) * LANE, pl.ds(off, TQ)] += _dot(kt_ref[x * LANE:(x + 1) * LANE, :], dst, ((1,), (0,)))
            return out[0], out[2], carry[2] + out[1] + out[3]

        z = jnp.zeros((TK, LANE), F32)
        dka, dkb, dv = lax.fori_loop(kb, nq, step, (z, z, z))
        dk_ref[:, 0:LANE] = dka
        dk_ref[:, LANE:2 * LANE] = dkb
        dv_ref[...] = dv.astype(BF16)

    return pl.pallas_call(
        body, name="attn_bwd", grid=(NPAIR, S // TK),
        in_specs=[pl.BlockSpec((S, 2 * LANE), lambda j, k: (0, j)), pl.BlockSpec((TK, 2 * LANE), lambda j, k: (k, j)),
                  pl.BlockSpec((2 * LANE, TK), lambda j, k: (j, k)), pl.BlockSpec((TK, LANE), lambda j, k: (k, j)),
                  pl.BlockSpec((S, LANE), lambda j, k: (0, j)), pl.BlockSpec((None, 8, S), lambda j, k: (j, 0, 0))],
        out_specs=[pl.BlockSpec((2 * LANE, S), lambda j, k: (j, 0)), pl.BlockSpec((TK, 2 * LANE), lambda j, k: (k, j)),
                   pl.BlockSpec((TK, LANE), lambda j, k: (k, j))],
        out_shape=[jax.ShapeDtypeStruct((H * LANE, S), F32), jax.ShapeDtypeStruct((S, H * LANE), F32),
                   jax.ShapeDtypeStruct((S, H * VDIM), BF16)],
        compiler_params=pltpu.CompilerParams(dimension_semantics=("parallel", "arbitrary")),
    )(qc, kc, kct, v, do, rows)


_IN_Z, _IN_XBC, _IN_DT, _IN_Q, _IN_KV, _IN_KR = 0, 1024, 2560, 2576, 2960, 3216


PROJ_COLS = 512
SMALL_PAD = pl.cdiv(SMALL_W, PROJ_COLS) * PROJ_COLS


def _prep_in(w_in_t):
    dt = w_in_t.dtype
    return jnp.concatenate(
        [w_in_t[_IN_Q:_IN_KV], w_in_t[_IN_KV:_IN_KR], w_in_t[_IN_KR:IN_WIDTH], jnp.zeros((LANE - ROPE, D), dt),
         w_in_t[_IN_DT:_IN_Q], jnp.zeros((SMALL_PAD - SM_DT - H, D), dt)], axis=0)


def _proj_in(xb, w_in_t, w_small):
    nz, nx, ns = (_IN_XBC - _IN_Z) // PROJ_COLS, (_IN_DT - _IN_XBC) // PROJ_COLS, SMALL_PAD // PROJ_COLS

    dt_block, dt_at = divmod(SM_DT, PROJ_COLS)

    def body(x_ref, w_ref, ws_ref, z_ref, xbc_ref, sm_ref, dtt_ref):
        i = pl.program_id(0)

        def emit(w, o_ref):
            o_ref[...] = lax.dot_general(x_ref[...], w[...], (((1,), (1,)), ((), ())), preferred_element_type=F32)

        pl.when(i < nz)(lambda: emit(w_ref, z_ref))
        pl.when((i >= nz) & (i < nz + nx))(lambda: emit(w_ref, xbc_ref))
        pl.when(i >= nz + nx)(lambda: emit(ws_ref, sm_ref))

        @pl.when(i == nz + nx + dt_block)
        def _():
            dtt_ref[...] = sm_ref[:, dt_at:dt_at + LANE].T

    def blocks(first, count, rows):
        at = lambda i: jnp.clip(i - first, 0, count - 1)
        return pl.BlockSpec((PROJ_COLS, D), lambda i: (at(i), 0)) if rows else pl.BlockSpec((S, PROJ_COLS), lambda i: (0, at(i)))

    return pl.pallas_call(
        body, name="proj_in", grid=(nz + nx + ns,),
        in_specs=[pl.BlockSpec((S, D), lambda i: (0, 0)), blocks(0, nz + nx, True), blocks(nz + nx, ns, True)],
        out_specs=[blocks(0, nz, False), blocks(nz, nx, False), blocks(nz + nx, ns, False), pl.BlockSpec((LANE, S), lambda i: (0, 0))],
        out_shape=[jax.ShapeDtypeStruct((S, _IN_XBC - _IN_Z), F32), jax.ShapeDtypeStruct((S, _IN_DT - _IN_XBC), F32),
                   jax.ShapeDtypeStruct((S, SMALL_W), F32), jax.ShapeDtypeStruct((LANE, S), F32)],
    )(xb, w_in_t, w_small)


DW_IN_ROWS = 512


def _d_w_in(dz, dxbc, dsm, xb):
    parts = (dz, dxbc, dsm)
    first = [0]
    for a in parts:
        first.append(first[-1] + a.shape[1] // DW_IN_ROWS)
    assert first[-1] == pl.cdiv(IN_WIDTH, DW_IN_ROWS)

    def body(dz_ref, dxbc_ref, dsm_ref, x_ref, o_ref):
        i = pl.program_id(0)
        for a_ref, lo, hi in zip((dz_ref, dxbc_ref, dsm_ref), first[:-1], first[1:]):
            @pl.when((i >= lo) & (i < hi))
            def _(a_ref=a_ref):
                o_ref[...] = lax.dot_general(a_ref[...], x_ref[...], (((0,), (0,)), ((), ())),
                                             preferred_element_type=F32).astype(BF16)

    def a_spec(lo, hi):
        return pl.BlockSpec((S, DW_IN_ROWS), lambda i: (0, jnp.clip(i - lo, 0, hi - lo - 1)))

    return pl.pallas_call(
        body, name="d_w_in", grid=(first[-1],),
        in_specs=[a_spec(lo, hi) for lo, hi in zip(first[:-1], first[1:])] + [pl.BlockSpec((S, D), lambda i: (0, 0))],
        out_specs=pl.BlockSpec((DW_IN_ROWS, D), lambda i: (i, 0)), out_shape=jax.ShapeDtypeStruct((IN_WIDTH, D), BF16),
    )(dz, dxbc, dsm, xb)


def _prep_attn(w_qb, w_kvb):
    w_q = jnp.pad(w_qb.reshape(Q_RANK, H, NOPE + ROPE), ((0, 0), (0, 0), (0, LANE - NOPE - ROPE))).reshape(Q_RANK, H * LANE)
    kv3 = w_kvb.reshape(KV_RANK, H, NOPE + VDIM)
    w_k = jnp.pad(kv3[:, :, :NOPE], ((0, 0), (0, 0), (0, LANE - NOPE))).reshape(KV_RANK, H * LANE)
    w_v = kv3[:, :, NOPE:].reshape(KV_RANK, H * VDIM)
    return w_q, w_k, w_v


def _rope_tables(positions):
    inv_freq = 1.0 / (10000.0 ** (jnp.arange(0, ROPE, 2, dtype=F32) / ROPE))
    ang = positions.astype(F32).reshape(S, 1) * inv_freq
    cos, sin = jnp.cos(ang), jnp.sin(ang)
    cos_t = jnp.concatenate([jnp.ones((S, NOPE), F32), cos, cos, jnp.ones((S, LANE - NOPE - ROPE), F32)], axis=1)
    sin_t = jnp.concatenate([jnp.zeros((S, NOPE), F32), -sin, sin, jnp.zeros((S, LANE - NOPE - ROPE), F32)], axis=1)
    return cos_t, sin_t


def _local_step(x, p, positions, target, w_in, fetch, send, sp, started):
    w_in_t = w_in.reshape(IN_WIDTH, D)
    w_small = _prep_in(w_in_t)
    cos_t, sin_t = _rope_tables(positions)
    prow = jnp.zeros((8, LANE), F32).at[0, :H].set(sp["dt_bias"][0]).at[1, :H].set(sp["A_log"][0]).at[2, :H].set(sp["D"][0])
    pcol = prow.T

    xb, pb = (x + started).astype(BF16), p.astype(BF16)
    z, xbc, small, dt_t = _proj_in(xb, w_in_t, w_small)
    act = _conv_fwd(xbc, sp["conv_w"], sp["conv_b"])
    y, states = _ssd_fwd(act, small, dt_t, prow, pcol)
    y_ssd = _gate_norm_fwd(y, z, sp["ssd_norm"])
    gl = fetch("attn", y_ssd)
    w_q, w_k, w_v = _prep_attn(_from_cols(gl["w_qb"]), _from_cols(gl["w_kvb"]))
    qn, kvn, qcat, kcat, kcat_t, v = _qkv_fwd(small, w_q, w_k, w_v, sp["q_norm"], sp["kv_norm"], cos_t, sin_t)
    o, lse = _attn_fwd(qcat, kcat, v)
    y_mla = _rms_fwd(o, sp["out_norm"], name="out_norm_fwd")
    w_out = fetch("out", y_mla)["w_out"]
    w_out_s = w_out[:NCHIP // 2].reshape(SSD_INNER, D)
    w_out_m = w_out[NCHIP // 2:].reshape(SSD_INNER, D)
    mix = _mm([(y_ssd, w_out_s), (y_mla, w_out_m)], name="out_proj")
    h1, h1b = _ln_fwd(x, mix, sp["ln_mix_g"], sp["ln_mix_b"])
    gl = fetch("ffn", h1b)
    w_pg, w_pp = gl["w_pg"].reshape(D, D), _from_cols(gl["w_pp"])
    w_gate, w_up, w_down = gl["w_gate"], gl["w_up"], gl["w_down"]
    gate, up, actf = _ffn_hidden_fwd(h1b, w_gate, w_up)
    ffn = _mm([(actf, w_down)], chunk="sum", name="ffn_down")
    pg = _mm([(h1b, w_pg)], name="ple_gate")
    pp = _mm([(pb, w_pp)], name="ple_proj")
    dpre2, dpre2b, dpg, dpp, dg2, db2, loss_row = _final_fwd_bwd(h1, ffn, pg, pp, target, sp["ln_ffn_g"], sp["ln_ffn_b"])

    g = {"ln_ffn_g": dg2, "ln_ffn_b": db2}
    g["w_pp"] = _to_cols(_mm([(pb, dpp)], ta=True, out_dtype=BF16, name="d_w_ple_proj"))
    g["w_pg"] = _mm([(h1b, dpg)], ta=True, out_dtype=BF16, name="d_w_ple_gate").reshape(NCHIP, D // NCHIP, D)
    g["w_down"] = _mm([(actf, dpre2b)], ta=True, chunk="out", out_dtype=BF16, name="d_w_down")
    dgate, dup = _ffn_hidden_bwd(dpre2b, w_down, gate, up)
    g["w_gate"] = _mm([(dgate, h1b)], ta=True, chunk="out", out_dtype=BF16, name="d_w_gate")
    g["w_up"] = _mm([(dup, h1b)], ta=True, chunk="out", out_dtype=BF16, name="d_w_up")
    sent = send("ffn", {name: g.pop(name) for name in dict(ASYNC_GROUPS)["ffn"]})
    dh1 = _mm([(dpg, w_pg)], tb=True, add=dpre2, add_scale=ALPHA, name="d_h1_ple")
    dh1 = _mm([(dgate, w_gate), (dup, w_up)], chunk="sum", add=dh1, name="d_h1")
    dpre1, dpre1b, g["ln_mix_g"], g["ln_mix_b"] = _ln_bwd(x, mix, sp["ln_mix_g"] + sent, dh1)
    dy_ssd = _mm([(dpre1b, w_out_s)], tb=True, name="d_y_ssd")
    dy_mla = _mm([(dpre1b, w_out_m)], tb=True, name="d_y_mla")
    dw_out = jnp.concatenate([_mm([(y_ssd, dpre1b)], ta=True, out_dtype=BF16, name="d_w_out_s"),
                              _mm([(y_mla, dpre1b)], ta=True, out_dtype=BF16, name="d_w_out_m")], axis=0)
    sent = send("out", {"w_out": dw_out.reshape(NCHIP, 2 * SSD_INNER // NCHIP, D)})
    do, g["out_norm"] = _rms_bwd(o, sp["out_norm"] + sent, dy_mla, name="out_norm_bwd")
    dqt, dk, dv = _attn_bwd(qcat, kcat, kcat_t, v, do, _attn_rows(lse, o, do))
    dlatent, dqlin, dkb, g["q_norm"], g["kv_norm"] = _qkv_bwd(dqt, dk, dv, small, w_q, w_k, w_v, sp["q_norm"], sp["kv_norm"], cos_t, sin_t)
    dw_q = _mm([(qn, dqlin)], ta=True, out_dtype=BF16, name="d_w_q")
    dw_k = _mm([(kvn, dkb)], ta=True, out_dtype=BF16, name="d_w_k")
    dw_v = _mm([(kvn, dv)], ta=True, out_dtype=BF16, name="d_w_v")
    dw_qb = _to_cols(dw_q.reshape(Q_RANK, H, LANE)[:, :, :NOPE + ROPE].reshape(Q_RANK, H * (NOPE + ROPE)))
    dw_kvb = _to_cols(jnp.concatenate([dw_k.reshape(KV_RANK, H, LANE)[:, :, :NOPE], dw_v.reshape(KV_RANK, H, VDIM)],
                                       axis=2).reshape(KV_RANK, H * (NOPE + VDIM)))
    sent = send("attn", {"w_qb": dw_qb, "w_kvb": dw_kvb})
    dy, dz, g["ssd_norm"] = _gate_norm_bwd(y, z, sp["ssd_norm"] + sent, dy_ssd)
    dact, ddt, dprow = _ssd_bwd(act, small, dt_t, prow, pcol, states, dy)
    g["dt_bias"], g["A_log"], g["D"] = dprow[0:1, :H], dprow[1:2, :H], dprow[2:3, :H]
    dxbc, g["conv_w"], g["conv_b"] = _conv_bwd(xbc, sp["conv_w"], sp["conv_b"], dact)
    dsmall = jnp.concatenate([dlatent, ddt.astype(BF16)], axis=1)
    in_blocks = [(d, w_in_t, (k, first // PROJ_COLS + k, PROJ_COLS))
                 for d, first in ((dz, _IN_Z), (dxbc, _IN_XBC)) for k in range(d.shape[1] // PROJ_COLS)]
    grad_x = _mm(in_blocks + [(dsmall, w_small, (0, 0, SMALL_W))], add=dpre1, add_scale=ALPHA, name="d_x")
    n_small = IN_WIDTH - _IN_DT
    dsm = jnp.concatenate([ddt[:, :H].astype(BF16), dlatent[:, :n_small - H], jnp.zeros((S, D - n_small), BF16)], axis=1)
    dw_in = _d_w_in(dz, dxbc, dsm, xb).reshape(NCHIP, IN_WIDTH // NCHIP * D // LANE, LANE)
    return loss_row, grad_x, dw_in, g


MESH = pl.DeviceIdType.MESH
BIG = (("w_in", (D, IN_WIDTH), 1), ("w_qb", (Q_RANK, H * (NOPE + ROPE)), 1), ("w_kvb", (KV_RANK, H * (NOPE + VDIM)), 1),
       ("w_out", (2 * SSD_INNER, D), 0), ("w_gate", (D, D_FF), 1), ("w_up", (D, D_FF), 1), ("w_down", (D_FF, D), 0),
       ("w_pg", (D, D), 0), ("w_pp", (PLE, D), 1))
CONV_SHARD = SSD_XBC // NCHIP
BF16_ROWS = 16


def _from_cols(stack):
    return jnp.concatenate([stack[k] for k in range(NCHIP)], axis=1)


def _to_cols(full):
    r, c4 = full.shape
    return full.reshape(r, NCHIP, c4 // NCHIP).transpose(1, 0, 2)


def _coords():
    return lax.axis_index("x"), lax.axis_index("y"), lax.axis_index("c")


def _peers():
    x, y, c = _coords()
    return 2 * x + y, c, [(1 - x, y), (x, 1 - y), (1 - x, 1 - y)], (x, y, 1 - c)


def _half_axis(shape):
    return 0 if shape[-2] % (2 * BF16_ROWS) == 0 else 1


def _half_shape(shape):
    r, c = shape[-2:]
    return (r // 2, c) if _half_axis(shape) == 0 else (r, c // 2)


def _half(core, shape):
    r, c = shape[-2:]
    if _half_axis(shape) == 0:
        return pl.ds(pl.multiple_of(core * (r // 2), BF16_ROWS), r // 2), slice(None)
    return slice(None), pl.ds(pl.multiple_of(core * (c // 2), LANE), c // 2)


def _gather_weights(shards):
    n_arr = len(shards)
    per = 2 * (NCHIP - 1)

    def body(*refs):
        ins, outs = refs[:n_arr], refs[n_arr:2 * n_arr]
        send_sems, recv_sems, local_sems = refs[2 * n_arr:]
        k, c, chips, sibling = _peers()

        def copy(idx, src, dst, to):
            return pltpu.make_async_remote_copy(src_ref=src, dst_ref=dst, send_sem=send_sems.at[idx], recv_sem=recv_sems.at[idx],
                                                device_id=to, device_id_type=MESH)

        def part(a, chip, core):
            return outs[a].at[chip, *_half(core, shards[a].shape)]

        mine = [pltpu.make_async_copy(ins[a], outs[a].at[k], local_sems.at[a]) for a in range(n_arr)]
        for cp in mine:
            cp.start()
        sends = []
        for a in range(n_arr):
            for j, (cx, cy) in enumerate(chips):
                sends.append(copy(per * a + j, ins[a].at[*_half(c, shards[a].shape)], part(a, k, c), (cx, cy, c)))
                sends[-1].start()
        for j, (cx, cy) in enumerate(chips):
            for a in range(n_arr):
                landed = part(a, 2 * cx + cy, c)
                copy(per * a + j, landed, landed, (cx, cy, c)).wait_recv()
                sends.append(copy(per * a + NCHIP - 1 + j, landed, landed, sibling))
                sends[-1].start()
        for j, (cx, cy) in enumerate(chips):
            for a in range(n_arr):
                other = part(a, 2 * cx + cy, 1 - c)
                copy(per * a + NCHIP - 1 + j, other, other, sibling).wait_recv()
        for cp in sends:
            cp.wait_send()
        for cp in mine:
            cp.wait()

    any_spec = pl.BlockSpec(memory_space=pl.ANY)
    return pl.pallas_call(
        body, name="gather_weights", in_specs=[any_spec] * n_arr, out_specs=[any_spec] * n_arr,
        out_shape=[jax.ShapeDtypeStruct((NCHIP,) + s.shape, s.dtype) for s in shards],
        scratch_shapes=[pltpu.SemaphoreType.DMA((per * n_arr,)), pltpu.SemaphoreType.DMA((per * n_arr,)),
                        pltpu.SemaphoreType.DMA((n_arr,))],
    )(*shards)


ASYNC_GROUPS = (("attn", ("w_qb", "w_kvb")), ("out", ("w_out",)), ("ffn", ("w_gate", "w_up", "w_down", "w_pg", "w_pp")))
TRANSPOSED = ("w_in", "w_gate", "w_up")
ROW_MAJOR = ("w_in",)
HBM_SPEC = pl.BlockSpec(memory_space=pltpu.HBM)
SEM_SPEC = pl.BlockSpec(memory_space=pltpu.SEMAPHORE)
IN_FLIGHT = pltpu.SideEffectType.DATAFLOW_SIDE_EFFECTING


def _in_hbm(a):
    return pltpu.with_memory_space_constraint(a, pltpu.HBM)


def _hbm_like(arrs, lead=()):
    return [pltpu.HBM(lead + a.shape, a.dtype) for a in arrs]


def _split_start(name, srcs, lands, after, n_sem, start):
    n = len(srcs)
    order = [] if after is None else [after]

    def body(*refs):
        src_refs, land_refs = refs[:n], refs[n:2 * n]
        send_sems, recv_sems = refs[2 * n + len(order)], refs[2 * n + len(order) + 1]
        token = refs[-1]

        def copy(send_idx, recv_idx, src, dst, to):
            return pltpu.make_async_remote_copy(src_ref=src, dst_ref=dst, send_sem=send_sems.at[send_idx],
                                                recv_sem=recv_sems.at[recv_idx], device_id=to, device_id_type=MESH)

        for cp in start(src_refs, land_refs, copy):
            cp.start()
        token[...] = jnp.zeros_like(token)

    sem = pltpu.SemaphoreType.DMA((n_sem,))
    outs = pl.pallas_call(
        body, name=name, in_specs=[HBM_SPEC] * (2 * n) + [pl.BlockSpec(memory_space=pl.ANY)] * len(order),
        out_specs=[SEM_SPEC, SEM_SPEC] + [HBM_SPEC] * (2 * n) + [pl.BlockSpec(memory_space=pltpu.VMEM)],
        out_shape=[sem, sem] + _hbm_like(srcs) + _hbm_like(lands) + [jax.ShapeDtypeStruct((8, LANE), F32)],
        input_output_aliases={i: 2 + i for i in range(2 * n)},
        compiler_params=pltpu.CompilerParams(has_side_effects=IN_FLIGHT),
    )(*[_in_hbm(a) for a in srcs], *[_in_hbm(a) for a in lands], *order)
    return (outs[0], outs[1], outs[2:2 + n], outs[2 + n:2 + 2 * n]), outs[-1]


def _split_wait(name, send_sems, recv_sems, srcs, lands, after, waits):
    n = len(srcs)

    def body(*refs):
        src_refs, land_refs = refs[:n], refs[n:2 * n]
        send_ref, recv_ref = refs[2 * n], refs[2 * n + 1]

        def copy(send_idx, recv_idx, src, dst, to):
            return pltpu.make_async_remote_copy(src_ref=src, dst_ref=dst, send_sem=send_ref.at[send_idx],
                                                recv_sem=recv_ref.at[recv_idx], device_id=to, device_id_type=MESH)

        for cp in waits(src_refs, land_refs, copy):
            cp.wait_send()
            cp.wait_recv()

    outs = pl.pallas_call(
        body, name=name, in_specs=[HBM_SPEC] * (2 * n) + [SEM_SPEC, SEM_SPEC, pl.BlockSpec(memory_space=pl.ANY)],
        out_specs=[HBM_SPEC] * (2 * n), out_shape=_hbm_like(srcs) + _hbm_like(lands),
        input_output_aliases={i: i for i in range(2 * n)},
        compiler_params=pltpu.CompilerParams(has_side_effects=IN_FLIGHT),
    )(*srcs, *lands, send_sems, recv_sems, after)
    return outs[:n], outs[n:]


GATHER_LATE_SEMS = 2 * (NCHIP - 1)


def _gather_async_start(tag, shards, after):
    def start(srcs, lands, copy):
        k, c, chips, _ = _peers()
        out = []
        for a, (src, dst) in enumerate(zip(srcs, lands)):
            for j, (cx, cy) in enumerate(chips):
                for core in range(2):
                    out.append(copy(GATHER_LATE_SEMS * a + 2 * j + core, GATHER_LATE_SEMS * a + 2 * j + c,
                                    src.at[*_half(c, src.shape)], dst.at[k, *_half(c, src.shape)], (cx, cy, core)))
        return out

    chip = 2 * lax.axis_index("x") + lax.axis_index("y")
    lands = [lax.dynamic_update_slice(lax.empty((NCHIP,) + s.shape, s.dtype), s[None], (chip, 0, 0)) for s in shards]
    return _split_start("gather_%s_start" % tag, shards, lands, after, GATHER_LATE_SEMS * len(shards), start)


def _gather_async_wait(tag, send_sems, recv_sems, shards, lands, after):
    def waits(srcs, lands_, copy):
        _, c, chips, _ = _peers()
        out = []
        for a, (src, dst) in enumerate(zip(srcs, lands_)):
            for j, (cx, cy) in enumerate(chips):
                for core in range(2):
                    idx = GATHER_LATE_SEMS * a + 2 * j + core
                    out.append(copy(idx, idx, src.at[*_half(c, src.shape)], dst.at[2 * cx + cy, *_half(core, src.shape)], (cx, cy, core)))
        return out

    return _split_wait("gather_%s_wait" % tag, send_sems, recv_sems, shards, lands, after, waits)[1]


def _other_devices():
    x, y, c = _coords()
    out = []
    for d in range(1, NDEV):
        tx, ty, tc = x ^ (d >> 2), y ^ ((d >> 1) & 1), c ^ (d & 1)
        out.append((d, (tx, ty, tc), 2 * tx + ty, 4 * tx + 2 * ty + tc))
    return out


def _reduce_async_start(tag, stacks, after):
    def start(srcs, lands, copy):
        x, y, c = _coords()
        me = 4 * x + 2 * y + c
        return [copy((NDEV - 1) * a + d - 1, (NDEV - 1) * a + d - 1, src.at[chip, *_half(to[2], src.shape)], dst.at[me], to)
                for a, (src, dst) in enumerate(zip(srcs, lands)) for d, to, chip, _ in _other_devices()]

    x, y, c = _coords()
    lands = []
    for s in stacks:
        hr, hc = _half_shape(s.shape)
        at = (c * hr, 0) if _half_axis(s.shape) == 0 else (0, c * hc)
        own = lax.dynamic_slice(s, (2 * x + y,) + at, (1, hr, hc))
        lands.append(lax.dynamic_update_slice(lax.empty((NDEV, hr, hc), s.dtype), own, (4 * x + 2 * y + c, 0, 0)))
    return _split_start("reduce_%s_start" % tag, stacks, lands, after, (NDEV - 1) * len(stacks), start)


def _reduce_async_wait(tag, send_sems, recv_sems, stacks, lands, after):
    def waits(srcs, lands_, copy):
        return [copy((NDEV - 1) * a + d - 1, (NDEV - 1) * a + d - 1, src.at[chip, *_half(to[2], src.shape)], dst.at[pos], to)
                for a, (src, dst) in enumerate(zip(srcs, lands_)) for d, to, chip, pos in _other_devices()]

    return _split_wait("reduce_%s_wait" % tag, send_sems, recv_sems, stacks, lands, after, waits)[1]


def _reduce_finish(tag, arrived, dims):
    n_arr = len(arrived)

    def body(*refs):
        lands, fin = refs[:n_arr], refs[n_arr:2 * n_arr]
        send_sems, recv_sems = refs[2 * n_arr:]
        _, c, _, sibling = _peers()
        sends = []
        for a in range(n_arr):
            mine = fin[a].at[*_half(c, dims[a])]

            def device_sum(vs, vf, a=a, mine=mine):
                pltpu.sync_copy(lands[a], vs)
                acc = vs[0].astype(F32)
                for i in range(1, NDEV):
                    acc = acc + vs[i].astype(F32)
                vf[...] = acc
                pltpu.sync_copy(vf, mine)

            pl.run_scoped(device_sum, pltpu.VMEM((NDEV,) + _half_shape(dims[a]), BF16), pltpu.VMEM(_half_shape(dims[a]), F32))
            sends.append(pltpu.make_async_remote_copy(src_ref=mine, dst_ref=mine, send_sem=send_sems.at[a], recv_sem=recv_sems.at[a],
                                                      device_id=sibling, device_id_type=MESH))
            sends[-1].start()
        for a in range(n_arr):
            other = fin[a].at[*_half(1 - c, dims[a])]
            pltpu.make_async_remote_copy(src_ref=other, dst_ref=other, send_sem=send_sems.at[a], recv_sem=recv_sems.at[a],
                                         device_id=sibling, device_id_type=MESH).wait_recv()
        for cp in sends:
            cp.wait_send()

    any_spec = pl.BlockSpec(memory_space=pl.ANY)
    return pl.pallas_call(
        body, name="reduce_%s_finish" % tag, in_specs=[any_spec] * n_arr, out_specs=[any_spec] * n_arr,
        out_shape=[jax.ShapeDtypeStruct(d, F32) for d in dims],
        scratch_shapes=[pltpu.SemaphoreType.DMA((n_arr,)), pltpu.SemaphoreType.DMA((n_arr,))],
    )(*arrived)


SMALL = (("conv_w", SSD_K * SSD_XBC), ("conv_b", SSD_XBC), ("dt_bias", H), ("A_log", H), ("D", H), ("ssd_norm", SSD_INNER),
         ("q_norm", Q_RANK), ("kv_norm", KV_RANK), ("out_norm", SSD_INNER), ("ln_mix_g", D), ("ln_mix_b", D),
         ("ln_ffn_g", D), ("ln_ffn_b", D))
SMALL_ROWS = 120
NDEV = 8


def _allreduce_small(sv):
    def body(sv_ref, out_ref, slots, send_sems, recv_sems):
        x, y, c = _coords()
        me = 4 * x + 2 * y + c
        slots[me] = sv_ref[...]
        copies = []
        for d in range(1, NDEV):
            to = (x ^ (d >> 2), y ^ ((d >> 1) & 1), c ^ (d & 1))
            copies.append(pltpu.make_async_remote_copy(src_ref=sv_ref, dst_ref=slots.at[me], send_sem=send_sems.at[d - 1],
                                                       recv_sem=recv_sems.at[d - 1], device_id=to, device_id_type=MESH))
            copies[-1].start()
        for cp in copies:
            cp.wait_recv()
        for cp in copies:
            cp.wait_send()
        acc = slots[0]
        for i in range(1, NDEV):
            acc = acc + slots[i]
        out_ref[...] = acc

    vm = pl.BlockSpec(memory_space=pltpu.VMEM)
    return pl.pallas_call(
        body, name="allreduce_small", in_specs=[vm], out_specs=vm, out_shape=jax.ShapeDtypeStruct((SMALL_ROWS, LANE), F32),
        scratch_shapes=[pltpu.VMEM((NDEV, SMALL_ROWS, LANE), F32), pltpu.SemaphoreType.DMA((NDEV - 1,)),
                        pltpu.SemaphoreType.DMA((NDEV - 1,))],
    )(sv)


def _adamw_math(w, g, m, v):
    m2 = ADAM_B1 * m + (1.0 - ADAM_B1) * g
    v2 = ADAM_B2 * v + (1.0 - ADAM_B2) * (g * g)
    m_hat = m2 / (1.0 - ADAM_B1 ** ADAM_STEP)
    v_hat = v2 / (1.0 - ADAM_B2 ** ADAM_STEP)
    return -ADAM_LR * (m_hat / (jnp.sqrt(v_hat) + ADAM_EPS) + ADAM_WD * w), m2, v2


ADAM_BLOCK_BYTES = 2 * 1024 * 1024


def _adamw_big(w, g, m, v, *, name):
    r, c = w.shape

    def body(w_ref, g_ref, m_ref, v_ref, d_ref, m2_ref, v2_ref):
        d_ref[...], m2_ref[...], v2_ref[...] = _adamw_math(w_ref[...], g_ref[...], m_ref[...], v_ref[...])

    tr = max(t for t in range(8, r + 1, 8) if r % t == 0 and t * c * 4 <= ADAM_BLOCK_BYTES)
    steps, spec = r // tr, pl.BlockSpec((tr, c), lambda i: (i, 0))
    return pl.pallas_call(body, name=name, grid=(steps,), in_specs=[spec] * 4, out_specs=[spec] * 3,
                          out_shape=[jax.ShapeDtypeStruct((r, c), F32)] * 3)(w, g, m, v)


def _adamw_small(ws, gs, ms, vs):
    n = len(ws)

    def body(*refs):
        for i in range(n):
            w_ref, g_ref, m_ref, v_ref = (refs[j * n + i] for j in range(4))
            d_ref, m2_ref, v2_ref = (refs[(4 + j) * n + i] for j in range(3))
            d_ref[...], m2_ref[...], v2_ref[...] = _adamw_math(w_ref[...], g_ref[...], m_ref[...], v_ref[...])

    vm = pl.BlockSpec(memory_space=pltpu.VMEM)
    shapes = [jax.ShapeDtypeStruct(w.shape, F32) for w in ws]
    outs = pl.pallas_call(body, name="adamw_small", in_specs=[vm] * (4 * n), out_specs=[vm] * (3 * n), out_shape=shapes * 3)(
        *ws, *gs, *ms, *vs)
    return outs[:n], outs[n:2 * n], outs[2 * n:]


_SMALL_ARG = {"conv_w": "ssd_conv_w", "conv_b": "ssd_conv_b", "dt_bias": "ssd_dt_bias", "A_log": "ssd_A_log", "D": "ssd_D",
              "ssd_norm": "ssd_norm_w", "q_norm": "mla_q_norm_w", "kv_norm": "mla_kv_norm_w", "out_norm": "mla_out_norm_w",
              "ln_mix_g": "ln_mix_g", "ln_mix_b": "ln_mix_b", "ln_ffn_g": "ln_ffn_g", "ln_ffn_b": "ln_ffn_b"}
_BIG_ARG = {"w_in": "w_in", "w_qb": "mla_w_q_b", "w_kvb": "mla_w_kv_b", "w_out": "w_out", "w_gate": "w_ffn_gate",
            "w_up": "w_ffn_up", "w_down": "w_ffn_down", "w_pg": "w_ple_gate", "w_pp": "w_ple_proj"}
_WEIGHT_ORDER = ("w_in", "ssd_conv_w", "ssd_conv_b", "ssd_dt_bias", "ssd_A_log", "ssd_D", "ssd_norm_w", "mla_q_norm_w", "mla_w_q_b",
                 "mla_kv_norm_w", "mla_w_kv_b", "mla_out_norm_w", "w_out", "ln_mix_g", "ln_mix_b", "w_ffn_gate", "w_ffn_up",
                 "w_ffn_down", "w_ple_gate", "w_ple_proj", "ln_ffn_g", "ln_ffn_b")


def _rows128(a):
    flat = a.reshape(-1)
    return jnp.pad(flat, (0, -flat.shape[0] % LANE)).reshape(-1, LANE)


def kernel(x, p, positions, w_in, ssd_conv_w, ssd_conv_b, ssd_dt_bias, ssd_A_log, ssd_D, ssd_norm_w, mla_q_norm_w, mla_w_q_b, mla_kv_norm_w, mla_w_kv_b, mla_out_norm_w, w_out, ln_mix_g, ln_mix_b, w_ffn_gate, w_ffn_up, w_ffn_down, w_ple_gate, w_ple_proj, ln_ffn_g, ln_ffn_b, loss_target, m_w_in, m_ssd_conv_w, m_ssd_conv_b, m_ssd_dt_bias, m_ssd_A_log, m_ssd_D, m_ssd_norm_w, m_mla_q_norm_w, m_mla_w_q_b, m_mla_kv_norm_w, m_mla_w_kv_b, m_mla_out_norm_w, m_w_out, m_ln_mix_g, m_ln_mix_b, m_w_ffn_gate, m_w_ffn_up, m_w_ffn_down, m_w_ple_gate, m_w_ple_proj, m_ln_ffn_g, m_ln_ffn_b, v_w_in, v_ssd_conv_w, v_ssd_conv_b, v_ssd_dt_bias, v_ssd_A_log, v_ssd_D, v_ssd_norm_w, v_mla_q_norm_w, v_mla_w_q_b, v_mla_kv_norm_w, v_mla_w_kv_b, v_mla_out_norm_w, v_w_out, v_ln_mix_g, v_ln_mix_b, v_w_ffn_gate, v_w_ffn_up, v_w_ffn_down, v_w_ple_gate, v_w_ple_proj, v_ln_ffn_g, v_ln_ffn_b):
    given = dict(locals())
    chip = 2 * lax.axis_index("x") + lax.axis_index("y")

    def local(name, prefix=""):
        a = given[prefix + _BIG_ARG[name]][0]
        return a.T if name in TRANSPOSED else a

    def updated(name, prefix=""):
        if name in ROW_MAJOR:
            _, c, r = given[prefix + _BIG_ARG[name]].shape
            return given[prefix + _BIG_ARG[name]].reshape(c // LANE, LANE, r).transpose(2, 0, 1).reshape(-1, LANE)
        return local(name, prefix)

    def global_layout(name, arr):
        if name in ROW_MAJOR:
            r, c = local(name).shape
            return arr.reshape(r, c // LANE, LANE).transpose(1, 2, 0).reshape(1, c, r)
        return (arr.T if name in TRANSPOSED else arr)[None]

    conv_bits = lax.bitcast_convert_type(ssd_conv_w[0], BF16).reshape(SSD_K, 2 * CONV_SHARD)
    w_in_all, conv_all = _gather_weights([local("w_in").astype(BF16), jnp.pad(conv_bits, ((0, BF16_ROWS - SSD_K), (0, 0)))])
    sp = {k: given[a] for k, a in _SMALL_ARG.items() if k != "conv_w"}
    sp["conv_w"] = _from_cols(lax.bitcast_convert_type(conv_all[:, :SSD_K].reshape(NCHIP, SSD_K, CONV_SHARD, 2), F32))
    gathering, tie = {}, w_in_all
    for group, names in ASYNC_GROUPS:
        gathering[group], tie = _gather_async_start(group, [local(name).astype(BF16) for name in names], tie)

    def fetch(group, after):
        return dict(zip(dict(ASYNC_GROUPS)[group], _gather_async_wait(group, *gathering[group], after)))

    reducing = {}

    def send(group, grads):
        reducing[group], sent = _reduce_async_start(group, [grads[name] for name in dict(ASYNC_GROUPS)[group]], None)
        return sent[0, 0]

    loss_row, grad_x, dw_in, g = _local_step(x[0], p[0, 0], positions[0], loss_target[0], w_in_all, fetch, send, sp, tie[0, 0])

    reducing["in"], tie = _reduce_async_start("in", [dw_in], grad_x)
    gbig = {}
    for group, names in reversed(ASYNC_GROUPS):
        arrived = _reduce_async_wait(group, *reducing[group], tie)
        gbig.update(zip(names, _reduce_finish(group, arrived, [local(name).shape for name in names])))
    small_in = jnp.concatenate([_rows128(g[name]) for name, _ in SMALL] + [loss_row], axis=0)
    small_sum = _allreduce_small(jnp.pad(small_in, ((0, SMALL_ROWS - small_in.shape[0]), (0, 0))))
    gsmall, row = {}, 0
    for name, size in SMALL:
        nrow = -(-size // LANE)
        gsmall[name] = small_sum[row:row + nrow].reshape(-1)[:size]
        row += nrow
    loss = small_sum[row, 0]

    grads = {_BIG_ARG[name]: global_layout(name, arr) for name, arr in gbig.items()}
    for name, _ in SMALL:
        if name == "conv_w":
            full_g = gsmall[name].reshape(SSD_K, SSD_XBC)
            grads["ssd_conv_w"] = lax.dynamic_slice(full_g, (0, chip * CONV_SHARD), (SSD_K, CONV_SHARD))[None]
        else:
            grads[_SMALL_ARG[name]] = gsmall[name].reshape(given[_SMALL_ARG[name]].shape)

    delta, new_m, new_v = {}, {}, {}

    def update_matrix(name, grad):
        a = _BIG_ARG[name]
        d, m2, v2 = _adamw_big(updated(name), grad, updated(name, "m_"), updated(name, "v_"), name="adamw_" + a)
        delta[a], new_m[a], new_v[a] = (global_layout(name, t) for t in (d, m2, v2))
        return d

    for name, grad in gbig.items():
        last = update_matrix(name, grad)
    g_in = _reduce_finish("in", _reduce_async_wait("in", *reducing["in"], last), [updated("w_in").shape])[0]
    grads["w_in"] = global_layout("w_in", g_in)
    update_matrix("w_in", g_in)
    small_names = [_SMALL_ARG[name] for name, _ in SMALL]
    two_d = lambda t: t.reshape(t.shape[-2], t.shape[-1])
    ds, ms, vs = _adamw_small([two_d(given[a]) for a in small_names], [two_d(grads[a]) for a in small_names],
                              [two_d(given["m_" + a]) for a in small_names], [two_d(given["v_" + a]) for a in small_names])
    for a, d, m2, v2 in zip(small_names, ds, ms, vs):
        delta[a], new_m[a], new_v[a] = (t.reshape(given[a].shape) for t in (d, m2, v2))

    return (loss, grad_x[None], *[grads[n] for n in _WEIGHT_ORDER], *[delta[n] for n in _WEIGHT_ORDER],
            *[new_m[n] for n in _WEIGHT_ORDER], *[new_v[n] for n in _WEIGHT_ORDER])
```

```python
import functools
import math

import jax
import jax.numpy as jnp
from jax import lax
from jax.experimental import pallas as pl
from jax.experimental.pallas import tpu as pltpu

F32 = jnp.float32
BF16 = jnp.bfloat16

S = 2048
D = 1024
PLE = 256
H = 16
SSD_P = 64
SSD_INNER = 1024
SSD_N = 128
SSD_G = 2
SSD_L = 128
SSD_NC = S // SSD_L
SSD_XBC = 1536
SSD_K = 4
Q_RANK = 384
KV_RANK = 256
NOPE = 64
ROPE = 32
VDIM = 64
D_FF = 2816
IN_WIDTH = 3248
ALPHA = 2.0 ** 0.25
EPS_RMS = 1e-6
EPS_LN = 1e-5
ATT_SCALE = 1.0 / math.sqrt(NOPE + ROPE)
LN2 = math.log(2.0)
ATT_SCALE_LOG2 = ATT_SCALE / LN2
LANE = 128
NCHIP = 4
SMALL_W = 896
SM_Q, SM_KV, SM_KR, SM_DT = 0, 384, 640, 768
NEG = -1e30

ADAM_LR = 0.001
ADAM_B1 = 0.9
ADAM_B2 = 0.999
ADAM_EPS = 1e-08
ADAM_WD = 0.01
ADAM_STEP = 10


def _sigmoid(v):
    return 1.0 / (1.0 + jnp.exp(-v))


MM_VMEM_BUDGET = 36 * 2 ** 20
MM_MAX_ACC = 2048 * 1024


def _mm_tiles(pairs, ks, m, n, out_dtype, has_add):
    def divs(v):
        return [LANE * d for d in range(v // LANE, 0, -1) if (v // LANE) % d == 0] if v % LANE == 0 else [v]

    def cost(tm, tn):
        tot = tm * tn * (jnp.dtype(out_dtype).itemsize + (4 if has_add else 0))
        for (a, b), k in zip(pairs, ks):
            tot += k * (tm * a.dtype.itemsize + tn * b.dtype.itemsize)
        return 2 * tot

    ok = [(tm * tn, tm, tn) for tm in divs(m) for tn in divs(n) if tm * tn <= MM_MAX_ACC and cost(tm, tn) <= MM_VMEM_BUDGET]
    _, tm, tn = max(ok)
    return tm, tn


def _mm(pairs, *, ta=False, tb=False, out_dtype=F32, add=None, add_scale=1.0, chunk=None, name):
    n_pairs = len(pairs)
    windows = [pr[2] if len(pr) == 3 else None for pr in pairs]
    pairs = [pr[:2] for pr in pairs]
    assert not ((ta or tb) and any(windows))
    ks = [w[2] if w else (a.shape[-2] if ta else a.shape[-1]) for (a, _), w in zip(pairs, windows)]
    a0, b0 = pairs[0]
    m = a0.shape[-1] if ta else a0.shape[-2]
    n = b0.shape[-2] if tb else b0.shape[-1]
    tm, tn = _mm_tiles(pairs, ks, m, n, out_dtype, add is not None)
    dims = (((0 if ta else 1,), (1 if tb else 0,)), ((), ()))
    nk = NCHIP if chunk else 1
    assert chunk != "sum" or out_dtype == F32

    def body(*refs):
        o_ref = refs[-1]
        acc = None
        for i in range(n_pairs):
            a = refs[2 * i][...].astype(BF16)
            b = refs[2 * i + 1][...].astype(BF16)
            part = lax.dot_general(a, b, dims, preferred_element_type=F32)
            acc = part if acc is None else acc + part
        if chunk == "sum":
            k = pl.program_id(2)

            @pl.when(k == 0)
            def _():
                o_ref[...] = acc + add_scale * refs[2 * n_pairs][...] if add is not None else acc

            @pl.when(k > 0)
            def _():
                o_ref[...] += acc
        else:
            if add is not None:
                acc = acc + add_scale * refs[2 * n_pairs][...]
            o_ref[...] = acc.astype(out_dtype)

    def spec(arr, shape, idx2):
        if arr.ndim == 3:
            return pl.BlockSpec((None,) + shape, lambda i, j, k: (k,) + idx2(i, j))
        return pl.BlockSpec(shape, lambda i, j, k: idx2(i, j))

    in_specs, args = [], []
    for (a, b), kdim, window in zip(pairs, ks, windows):
        ka, kb = window[:2] if window else (0, 0)
        in_specs.append(spec(a, (kdim, tm), lambda i, j: (0, i)) if ta else spec(a, (tm, kdim), lambda i, j, ka=ka: (i, ka)))
        in_specs.append(spec(b, (tn, kdim), lambda i, j: (j, 0)) if tb else spec(b, (kdim, tn), lambda i, j, kb=kb: (kb, j)))
        args += [a, b]
    if add is not None:
        in_specs.append(pl.BlockSpec((tm, tn), lambda i, j, k: (i, j)))
        args.append(add)
    if chunk == "out":
        out_spec = pl.BlockSpec((None, tm, tn), lambda i, j, k: (k, i, j))
        out_shape = jax.ShapeDtypeStruct((nk, m, n), out_dtype)
    else:
        out_spec = pl.BlockSpec((tm, tn), lambda i, j, k: (i, j))
        out_shape = jax.ShapeDtypeStruct((m, n), out_dtype)
    return pl.pallas_call(
        body, name=name, grid=(m // tm, n // tn, nk), in_specs=in_specs, out_specs=out_spec, out_shape=out_shape,
        compiler_params=pltpu.CompilerParams(dimension_semantics=("parallel", "parallel", "arbitrary")),
    )(*args)


TR = 256


def _row_spec(c):
    return pl.BlockSpec((TR, c), lambda i: (i, 0))


def _vec_spec(c):
    return pl.BlockSpec((1, c), lambda i: (0, 0))


def _acc_rows(ref, val):
    @pl.when(pl.program_id(0) == 0)
    def _():
        ref[...] = jnp.zeros_like(ref)
    ref[...] += val


def _rms_fwd(u, w, *, name):
    c = u.shape[1]

    def body(u_ref, w_ref, o_ref):
        v = u_ref[...]
        r = lax.rsqrt(jnp.mean(v * v, axis=-1, keepdims=True) + EPS_RMS)
        o_ref[...] = (v * r * w_ref[...]).astype(BF16)

    return pl.pallas_call(body, name=name, grid=(S // TR,), in_specs=[_row_spec(c), _vec_spec(c)], out_specs=_row_spec(c),
                          out_shape=jax.ShapeDtypeStruct((S, c), BF16))(u, w)


def _rms_bwd(u, w, dy, *, name):
    c = u.shape[1]

    def body(u_ref, w_ref, dy_ref, du_ref, dw_ref):
        v = u_ref[...]
        g = dy_ref[...].astype(F32)
        r = lax.rsqrt(jnp.mean(v * v, axis=-1, keepdims=True) + EPS_RMS)
        gw = g * w_ref[...]
        du_ref[...] = r * gw - v * (r * r * r * jnp.mean(gw * v, axis=-1, keepdims=True))
        _acc_rows(dw_ref, jnp.sum(g * v * r, axis=0, keepdims=True))

    return pl.pallas_call(body, name=name, grid=(S // TR,), in_specs=[_row_spec(c), _vec_spec(c), _row_spec(c)],
                          out_specs=[_row_spec(c), _vec_spec(c)],
                          out_shape=[jax.ShapeDtypeStruct((S, c), F32), jax.ShapeDtypeStruct((1, c), F32)])(u, w, dy)


def _gate_norm_fwd(y, z, w):
    def body(y_ref, z_ref, w_ref, o_ref):
        zz = z_ref[...]
        v = y_ref[...] * (zz * _sigmoid(zz))
        r = lax.rsqrt(jnp.mean(v * v, axis=-1, keepdims=True) + EPS_RMS)
        o_ref[...] = (v * r * w_ref[...]).astype(BF16)

    c = SSD_INNER
    return pl.pallas_call(body, name="ssd_gate_norm_fwd", grid=(S // TR,), in_specs=[_row_spec(c), _row_spec(c), _vec_spec(c)],
                          out_specs=_row_spec(c), out_shape=jax.ShapeDtypeStruct((S, c), BF16))(y, z, w)


def _gate_norm_bwd(y, z, w, dout):
    def body(y_ref, z_ref, w_ref, g_ref, dy_ref, dz_ref, dw_ref):
        yy = y_ref[...]
        zz = z_ref[...]
        sg = _sigmoid(zz)
        sz = zz * sg
        v = yy * sz
        g = g_ref[...]
        r = lax.rsqrt(jnp.mean(v * v, axis=-1, keepdims=True) + EPS_RMS)
        gw = g * w_ref[...]
        dv = r * gw - v * (r * r * r * jnp.mean(gw * v, axis=-1, keepdims=True))
        dy_ref[...] = dv * sz
        dz_ref[...] = (dv * yy * (sg * (1.0 + zz * (1.0 - sg)))).astype(BF16)
        _acc_rows(dw_ref, jnp.sum(g * v * r, axis=0, keepdims=True))

    c = SSD_INNER
    return pl.pallas_call(body, name="ssd_gate_norm_bwd", grid=(S // TR,),
                          in_specs=[_row_spec(c), _row_spec(c), _vec_spec(c), _row_spec(c)],
                          out_specs=[_row_spec(c), _row_spec(c), _vec_spec(c)],
                          out_shape=[jax.ShapeDtypeStruct((S, c), F32), jax.ShapeDtypeStruct((S, c), BF16),
                                     jax.ShapeDtypeStruct((1, c), F32)])(y, z, w, dout)


def _ln_fwd(xr, mix, g, b):
    def body(x_ref, m_ref, g_ref, b_ref, o_ref, ob_ref):
        pre = ALPHA * x_ref[...] + m_ref[...]
        mu = jnp.mean(pre, axis=-1, keepdims=True)
        d = pre - mu
        rs = lax.rsqrt(jnp.mean(d * d, axis=-1, keepdims=True) + EPS_LN)
        h = d * rs * g_ref[...] + b_ref[...]
        o_ref[...] = h
        ob_ref[...] = h.astype(BF16)

    return pl.pallas_call(body, name="ln_mix_fwd", grid=(S // TR,), in_specs=[_row_spec(D), _row_spec(D), _vec_spec(D), _vec_spec(D)],
                          out_specs=[_row_spec(D)] * 2,
                          out_shape=[jax.ShapeDtypeStruct((S, D), F32), jax.ShapeDtypeStruct((S, D), BF16)])(xr, mix, g, b)


def _ln_bwd(xr, mix, g, dh):
    def body(x_ref, m_ref, g_ref, dh_ref, dpre_ref, dpreb_ref, dg_ref, db_ref):
        pre = ALPHA * x_ref[...] + m_ref[...]
        mu = jnp.mean(pre, axis=-1, keepdims=True)
        d = pre - mu
        rs = lax.rsqrt(jnp.mean(d * d, axis=-1, keepdims=True) + EPS_LN)
        xh = d * rs
        dy = dh_ref[...]
        gy = dy * g_ref[...]
        dpre = rs * (gy - jnp.mean(gy, axis=-1, keepdims=True) - xh * jnp.mean(gy * xh, axis=-1, keepdims=True))
        dpre_ref[...] = dpre
        dpreb_ref[...] = dpre.astype(BF16)
        _acc_rows(dg_ref, jnp.sum(dy * xh, axis=0, keepdims=True))
        _acc_rows(db_ref, jnp.sum(dy, axis=0, keepdims=True))

    return pl.pallas_call(body, name="ln_mix_bwd", grid=(S // TR,),
                          in_specs=[_row_spec(D), _row_spec(D), _vec_spec(D), _row_spec(D)],
                          out_specs=[_row_spec(D), _row_spec(D), _vec_spec(D), _vec_spec(D)],
                          out_shape=[jax.ShapeDtypeStruct((S, D), F32), jax.ShapeDtypeStruct((S, D), BF16),
                                     jax.ShapeDtypeStruct((1, D), F32), jax.ShapeDtypeStruct((1, D), F32)])(xr, mix, g, dh)


FF_CHUNK = D_FF // NCHIP


FF_ROWS = 1024


def _ff_act_spec():
    return pl.BlockSpec((None, FF_ROWS, FF_CHUNK), lambda i, k: (k, i, 0))


def _ff_w_spec():
    return pl.BlockSpec((None, FF_CHUNK, D), lambda i, k: (k, 0, 0))


def _ffn_hidden_fwd(h, w_gate_t, w_up_t):
    def body(h_ref, wg_ref, wu_ref, g_ref, u_ref, a_ref):
        hh = h_ref[...]
        g = _dot(hh, wg_ref[...], ((1,), (1,)))
        u = _dot(hh, wu_ref[...], ((1,), (1,)))
        g_ref[...] = g.astype(BF16)
        u_ref[...] = u.astype(BF16)
        a_ref[...] = (g * _sigmoid(g) * u).astype(BF16)

    return pl.pallas_call(
        body, name="ffn_hidden_fwd", grid=(S // FF_ROWS, NCHIP),
        in_specs=[pl.BlockSpec((FF_ROWS, D), lambda i, k: (i, 0)), _ff_w_spec(), _ff_w_spec()], out_specs=[_ff_act_spec()] * 3,
        out_shape=[jax.ShapeDtypeStruct((NCHIP, S, FF_CHUNK), BF16)] * 3,
        compiler_params=pltpu.CompilerParams(dimension_semantics=("parallel", "parallel")),
    )(h, w_gate_t, w_up_t)


def _ffn_hidden_bwd(dout, w_down, gate, up):
    def body(d_ref, wd_ref, g_ref, u_ref, dg_ref, du_ref):
        d = _dot(d_ref[...], wd_ref[...], ((1,), (1,)))
        g = g_ref[...].astype(F32)
        sg = _sigmoid(g)
        dg_ref[...] = (d * u_ref[...].astype(F32) * (sg * (1.0 + g * (1.0 - sg)))).astype(BF16)
        du_ref[...] = (d * g * sg).astype(BF16)

    return pl.pallas_call(
        body, name="ffn_hidden_bwd", grid=(S // FF_ROWS, NCHIP),
        in_specs=[pl.BlockSpec((FF_ROWS, D), lambda i, k: (i, 0)), _ff_w_spec(), _ff_act_spec(), _ff_act_spec()],
        out_specs=[_ff_act_spec()] * 2, out_shape=[jax.ShapeDtypeStruct((NCHIP, S, FF_CHUNK), BF16)] * 2,
        compiler_params=pltpu.CompilerParams(dimension_semantics=("parallel", "parallel")),
    )(dout, w_down, gate, up)


def _final_fwd_bwd(h1, ffn, pg, pp, target, g2, b2):
    def body(h_ref, f_ref, pg_ref, pp_ref, t_ref, g_ref, b_ref, dpre_ref, dpreb_ref, dpg_ref, dpp_ref, dg_ref, db_ref, loss_ref):
        sg = _sigmoid(pg_ref[...])
        ppv = pp_ref[...]
        pre = ALPHA * h_ref[...] + f_ref[...] + sg * ppv
        mu = jnp.mean(pre, axis=-1, keepdims=True)
        d = pre - mu
        rs = lax.rsqrt(jnp.mean(d * d, axis=-1, keepdims=True) + EPS_LN)
        xh = d * rs
        err = xh * g_ref[...] + b_ref[...] - t_ref[...]
        dy = err * (1.0 / D)
        gy = dy * g_ref[...]
        dpre = rs * (gy - jnp.mean(gy, axis=-1, keepdims=True) - xh * jnp.mean(gy * xh, axis=-1, keepdims=True))
        dpre_ref[...] = dpre
        dpreb_ref[...] = dpre.astype(BF16)
        dpg_ref[...] = (dpre * ppv * sg * (1.0 - sg)).astype(BF16)
        dpp_ref[...] = (dpre * sg).astype(BF16)
        _acc_rows(dg_ref, jnp.sum(dy * xh, axis=0, keepdims=True))
        _acc_rows(db_ref, jnp.sum(dy, axis=0, keepdims=True))
        _acc_rows(loss_ref, 0.5 * jnp.sum(jnp.mean(err * err, axis=-1, keepdims=True), axis=0, keepdims=True) * jnp.ones((1, LANE), F32))

    return pl.pallas_call(
        body, name="final_ln_loss", grid=(S // TR,),
        in_specs=[_row_spec(D)] * 5 + [_vec_spec(D)] * 2,
        out_specs=[_row_spec(D)] * 4 + [_vec_spec(D), _vec_spec(D), _vec_spec(LANE)],
        out_shape=[jax.ShapeDtypeStruct((S, D), F32)] + [jax.ShapeDtypeStruct((S, D), BF16)] * 3 + [
                   jax.ShapeDtypeStruct((1, D), F32), jax.ShapeDtypeStruct((1, D), F32), jax.ShapeDtypeStruct((1, LANE), F32)],
    )(h1, ffn, pg, pp, target, g2, b2)


def _rot(u, cos_t, sin_t, lane):
    partner = jnp.where(lane < NOPE + ROPE // 2, pltpu.roll(u, LANE - ROPE // 2, 1), pltpu.roll(u, ROPE // 2, 1))
    return u * cos_t + partner * sin_t


def _rms(v, w):
    r = lax.rsqrt(jnp.mean(v * v, axis=-1, keepdims=True) + EPS_RMS)
    return v * r * w, r


def _rms_grad(v, r, w, g):
    gw = g * w
    return r * gw - v * (r * r * r * jnp.mean(gw * v, axis=-1, keepdims=True)), jnp.sum(g * v * r, axis=0, keepdims=True)


def _whole(arr):
    return pl.BlockSpec(arr.shape, lambda i: (0,) * arr.ndim)


def _qkv_fwd(small, w_q, w_k, w_v, q_norm, kv_norm, cos_t, sin_t):
    def body(sm_ref, wq_ref, wk_ref, wv_ref, qw_ref, kw_ref, c_ref, s_ref, qn_ref, kvn_ref, q_ref, k_ref, kt_ref, v_ref):
        lane = lax.broadcasted_iota(jnp.int32, (TR, LANE), 1)
        c, s = c_ref[...], s_ref[...]
        qn = _rms(sm_ref[:, SM_Q:SM_Q + Q_RANK], qw_ref[...])[0].astype(BF16)
        kvn = _rms(sm_ref[:, SM_KV:SM_KV + KV_RANK], kw_ref[...])[0].astype(BF16)
        qn_ref[...] = qn
        kvn_ref[...] = kvn
        kr = _rot(pltpu.roll(sm_ref[:, SM_KR:SM_KR + LANE], NOPE, 1), c, s, lane)
        for h in range(H):
            tile = slice(h * LANE, (h + 1) * LANE)
            q_ref[:, tile] = _rot(_dot(qn, wq_ref[:, tile], ((1,), (0,))), c, s, lane).astype(BF16)
            kt = _dot(kvn, wk_ref[:, tile], ((1,), (0,))) + kr
            k_ref[:, tile] = kt.astype(BF16)
            kt_ref[tile, :] = kt.T.astype(BF16)
        v_ref[...] = _dot(kvn, wv_ref[...], ((1,), (0,))).astype(BF16)

    w = H * LANE
    return pl.pallas_call(
        body, name="qkv_fwd", grid=(S // TR,),
        in_specs=[_row_spec(SMALL_W), _whole(w_q), _whole(w_k), _whole(w_v), _vec_spec(Q_RANK), _vec_spec(KV_RANK), _row_spec(LANE), _row_spec(LANE)],
        out_specs=[_row_spec(Q_RANK), _row_spec(KV_RANK), _row_spec(w), _row_spec(w), pl.BlockSpec((w, TR), lambda i: (0, i)),
                   _row_spec(H * VDIM)],
        out_shape=[jax.ShapeDtypeStruct((S, Q_RANK), BF16), jax.ShapeDtypeStruct((S, KV_RANK), BF16), jax.ShapeDtypeStruct((S, w), BF16),
                   jax.ShapeDtypeStruct((S, w), BF16), jax.ShapeDtypeStruct((w, S), BF16), jax.ShapeDtypeStruct((S, H * VDIM), BF16)],
    )(small, w_q, w_k, w_v, q_norm, kv_norm, cos_t, sin_t)


def _qkv_bwd(dqt, dk, dv, small, w_q, w_k, w_v, q_norm, kv_norm, cos_t, sin_t):
    def body(dq_ref, dk_ref, dv_ref, sm_ref, wq_ref, wk_ref, wv_ref, qw_ref, kw_ref, c_ref, s_ref,
             ds_ref, dql_ref, dkb_ref, dqw_ref, dkw_ref):
        lane = lax.broadcasted_iota(jnp.int32, (TR, LANE), 1)
        c, s = c_ref[...], -s_ref[...]
        dqn = jnp.zeros((TR, Q_RANK), F32)
        dkvn = _dot(dv_ref[...], wv_ref[...], ((1,), (1,)))
        dkr = jnp.zeros((TR, LANE), F32)
        for h in range(H):
            tile = slice(h * LANE, (h + 1) * LANE)
            dql = _rot(dq_ref[tile, :].T, c, s, lane).astype(BF16)
            dql_ref[:, tile] = dql
            dqn = dqn + _dot(dql, wq_ref[:, tile], ((1,), (1,)))
            dkt = dk_ref[:, tile]
            dkb_ref[:, tile] = dkt.astype(BF16)
            dkvn = dkvn + _dot(dkt, wk_ref[:, tile], ((1,), (1,)))
            dkr = dkr + dkt
        dkr = jnp.where((lane >= NOPE) & (lane < NOPE + ROPE), dkr, 0.0)
        q_c, kv_c = sm_ref[:, SM_Q:SM_Q + Q_RANK], sm_ref[:, SM_KV:SM_KV + KV_RANK]
        dq_c, dqw = _rms_grad(q_c, _rms(q_c, qw_ref[...])[1], qw_ref[...], dqn)
        dkv_c, dkw = _rms_grad(kv_c, _rms(kv_c, kw_ref[...])[1], kw_ref[...], dkvn)
        ds_ref[:, SM_Q:SM_Q + Q_RANK] = dq_c.astype(BF16)
        ds_ref[:, SM_KV:SM_KV + KV_RANK] = dkv_c.astype(BF16)
        ds_ref[:, SM_KR:SM_KR + LANE] = pltpu.roll(_rot(dkr, c, s, lane), LANE - NOPE, 1).astype(BF16)
        _acc_rows(dqw_ref, dqw)
        _acc_rows(dkw_ref, dkw)

    w = H * LANE
    return pl.pallas_call(
        body, name="qkv_bwd", grid=(S // TR,),
        in_specs=[pl.BlockSpec((w, TR), lambda i: (0, i)), _row_spec(w), _row_spec(H * VDIM), _row_spec(SMALL_W), _whole(w_q), _whole(w_k),
                  _whole(w_v), _vec_spec(Q_RANK), _vec_spec(KV_RANK), _row_spec(LANE), _row_spec(LANE)],
        out_specs=[_row_spec(SM_DT), _row_spec(w), _row_spec(w), _vec_spec(Q_RANK), _vec_spec(KV_RANK)],
        out_shape=[jax.ShapeDtypeStruct((S, SM_DT), BF16), jax.ShapeDtypeStruct((S, w), BF16), jax.ShapeDtypeStruct((S, w), BF16),
                   jax.ShapeDtypeStruct((1, Q_RANK), F32), jax.ShapeDtypeStruct((1, KV_RANK), F32)],
    )(dqt, dk, dv, small, w_q, w_k, w_v, q_norm, kv_norm, cos_t, sin_t)


CB = 256


def _shift_down(u, k, row):
    if k == 0:
        return u
    return jnp.where(row >= k, pltpu.roll(u, k, 0), 0.0)


def _shift_up(u, k, row):
    if k == 0:
        return u
    return jnp.where(row < S - k, pltpu.roll(u, S - k, 0), 0.0)


def _conv_fwd(u, w, b):
    def body(u_ref, w_ref, b_ref, o_ref):
        row = lax.broadcasted_iota(jnp.int32, (S, CB), 0)
        uu = u_ref[...]
        acc = b_ref[...] + w_ref[SSD_K - 1:SSD_K, :] * uu
        for k in range(SSD_K - 1):
            acc = acc + w_ref[k:k + 1, :] * _shift_down(uu, SSD_K - 1 - k, row)
        o_ref[...] = acc * _sigmoid(acc)

    c = u.shape[1]
    return pl.pallas_call(
        body, name="conv_fwd", grid=(c // CB,),
        in_specs=[pl.BlockSpec((S, CB), lambda j: (0, j)), pl.BlockSpec((SSD_K, CB), lambda j: (0, j)), pl.BlockSpec((1, CB), lambda j: (0, j))],
        out_specs=pl.BlockSpec((S, CB), lambda j: (0, j)), out_shape=jax.ShapeDtypeStruct((S, c), F32),
    )(u, w, b)


def _conv_bwd(u, w, b, dact):
    def body(u_ref, w_ref, b_ref, d_ref, du_ref, dw_ref, db_ref):
        row = lax.broadcasted_iota(jnp.int32, (S, CB), 0)
        uu = u_ref[...]
        sh = [_shift_down(uu, SSD_K - 1 - k, row) for k in range(SSD_K)]
        acc = b_ref[...]
        for k in range(SSD_K):
            acc = acc + w_ref[k:k + 1, :] * sh[k]
        sg = _sigmoid(acc)
        dacc = d_ref[...] * (sg * (1.0 + acc * (1.0 - sg)))
        du = w_ref[SSD_K - 1:SSD_K, :] * dacc
        for k in range(SSD_K - 1):
            du = du + w_ref[k:k + 1, :] * _shift_up(dacc, SSD_K - 1 - k, row)
        du_ref[...] = du.astype(BF16)
        for k in range(SSD_K):
            dw_ref[k:k + 1, :] = jnp.sum(dacc * sh[k], axis=0, keepdims=True)
        db_ref[...] = jnp.sum(dacc, axis=0, keepdims=True)

    c = u.shape[1]
    col = lambda r: pl.BlockSpec((r, CB), lambda j: (0, j))
    return pl.pallas_call(
        body, name="conv_bwd", grid=(c // CB,), in_specs=[col(S), col(SSD_K), col(1), col(S)], out_specs=[col(S), col(SSD_K), col(1)],
        out_shape=[jax.ShapeDtypeStruct((S, c), BF16), jax.ShapeDtypeStruct((SSD_K, c), F32), jax.ShapeDtypeStruct((1, c), F32)],
    )(u, w, b, dact)


NPAIR = H // 2
PAIRS_PER_GROUP = NPAIR // SSD_G


def _softplus(v):
    return jnp.maximum(v, 0.0) + jnp.log(1.0 + jnp.exp(-jnp.abs(v)))


def _dot(a, b, dims):
    return lax.dot_general(a.astype(BF16), b.astype(BF16), (dims, ((), ())), preferred_element_type=F32)


def _dot2(a, sel):
    hi = a.astype(BF16)
    lo = (a - hi.astype(F32)).astype(BF16)
    dims = (((1,), (0,)), ((), ()))
    return lax.dot_general(hi, sel, dims, preferred_element_type=F32) + lax.dot_general(lo, sel, dims, preferred_element_type=F32)


def _dot3(a, b, dims, split_lhs):
    v = a if split_lhs else b
    v1 = v.astype(BF16)
    r1 = v - v1.astype(F32)
    v2 = r1.astype(BF16)
    v3 = (r1 - v2.astype(F32)).astype(BF16)
    acc = None
    for part in (v1, v2, v3):
        lhs, rhs = (part, b) if split_lhs else (a, part)
        t = lax.dot_general(lhs, rhs, (dims, ((), ())), preferred_element_type=F32)
        acc = t if acc is None else acc + t
    return acc


def _ssd_chunk_common(dt_ref, dtT_ref, prow_ref, pcol_ref):
    prow = prow_ref[...]
    pcol = pcol_ref[...]
    ri = lax.broadcasted_iota(jnp.int32, (SSD_L, SSD_L), 0)
    ci = lax.broadcasted_iota(jnp.int32, (SSD_L, SSD_L), 1)
    causal = ri >= ci
    pre_c = dt_ref[...] + prow[0:1, :]
    dtc = _softplus(pre_c)
    a_row = -jnp.exp(prow[1:2, :])
    cs_col = _dot3(causal.astype(BF16), dtc * a_row, ((1,), (0,)), False)
    dtr = _softplus(dtT_ref[...] + pcol[:, 0:1])
    a_col = -jnp.exp(pcol[:, 1:2])
    cs_row = _dot3(dtr * a_col, (ri <= ci).astype(BF16), ((1,), (0,)), True)
    return prow, causal, pre_c, dtc, a_row, cs_col, cs_row


def _ssd_fwd(act, small, dtT, prow, pcol):
    def body(x_ref, b_ref, c_ref, dt_ref, dtT_ref, prow_ref, pcol_ref, y_ref, st_ref, state):
        @pl.when(pl.program_id(0) == 0)
        def _():
            state[...] = jnp.zeros_like(state)

        prow, causal, _, dtc, _, cs_col, cs_row = _ssd_chunk_common(dt_ref, dtT_ref, prow_ref, pcol_ref)
        lo = lax.broadcasted_iota(jnp.int32, (SSD_L, LANE), 1) < SSD_P
        lo1 = lo[0:1, :]
        for g in range(SSD_G):
            bm = b_ref[:, g * SSD_N:(g + 1) * SSD_N]
            cm = c_ref[:, g * SSD_N:(g + 1) * SSD_N]
            cb = _dot(cm, bm, ((1,), (1,)))
            for qq in range(PAIRS_PER_GROUP):
                q = g * PAIRS_PER_GROUP + qq
                ha, hb = 2 * q, 2 * q + 1
                csa, csb = cs_col[:, ha:ha + 1], cs_col[:, hb:hb + 1]
                xp = x_ref[:, q * LANE:(q + 1) * LANE]
                xx = xp * jnp.where(lo, dtc[:, ha:ha + 1], dtc[:, hb:hb + 1])
                ga = cb * jnp.exp(jnp.where(causal, csa - cs_row[ha:ha + 1, :], NEG))
                gb = cb * jnp.exp(jnp.where(causal, csb - cs_row[hb:hb + 1, :], NEG))
                y = _dot(ga, jnp.where(lo, xx, 0.0), ((1,), (0,))) + _dot(gb, jnp.where(lo, 0.0, xx), ((1,), (0,)))
                s_in = state[q]
                y = y + _dot(cm, s_in, ((1,), (0,))) * jnp.where(lo, jnp.exp(csa), jnp.exp(csb))
                y = y + jnp.where(lo1, prow[2:3, ha:ha + 1], prow[2:3, hb:hb + 1]) * xp
                y_ref[:, q * LANE:(q + 1) * LANE] = y
                la, lb = csa[SSD_L - 1:SSD_L, :], csb[SSD_L - 1:SSD_L, :]
                decay = jnp.where(lo, jnp.exp(la - csa), jnp.exp(lb - csb))
                st_ref[q] = s_in
                state[q] = s_in * jnp.where(lo1, jnp.exp(la), jnp.exp(lb)) + _dot(bm, xx * decay, ((0,), (0,)))

    L = SSD_L
    return pl.pallas_call(
        body, name="ssd_fwd", grid=(SSD_NC,),
        in_specs=[pl.BlockSpec((L, SSD_INNER), lambda c: (c, 0)),
                  pl.BlockSpec((L, SSD_G * SSD_N), lambda c: (c, SSD_INNER // (SSD_G * SSD_N))),
                  pl.BlockSpec((L, SSD_G * SSD_N), lambda c: (c, SSD_INNER // (SSD_G * SSD_N) + 1)),
                  pl.BlockSpec((L, LANE), lambda c: (c, SM_DT // LANE)),
                  pl.BlockSpec((LANE, L), lambda c: (0, c)),
                  pl.BlockSpec((8, LANE), lambda c: (0, 0)), pl.BlockSpec((LANE, 8), lambda c: (0, 0))],
        out_specs=[pl.BlockSpec((L, SSD_INNER), lambda c: (c, 0)),
                   pl.BlockSpec((None, NPAIR, SSD_N, LANE), lambda c: (c, 0, 0, 0))],
        out_shape=[jax.ShapeDtypeStruct((S, SSD_INNER), F32), jax.ShapeDtypeStruct((SSD_NC, NPAIR, SSD_N, LANE), F32)],
        scratch_shapes=[pltpu.VMEM((NPAIR, SSD_N, LANE), F32)],
        compiler_params=pltpu.CompilerParams(dimension_semantics=("arbitrary",)),
    )(act, act, act, small, dtT, prow, pcol)


def _ssd_bwd(act, small, dtT, prow, pcol, states, dy):
    def body(x_ref, b_ref, c_ref, dt_ref, dtT_ref, prow_ref, pcol_ref, st_ref, dy_ref,
             dx_ref, ddt_ref, dp_ref, dstate):
        @pl.when(pl.program_id(0) == 0)
        def _():
            dstate[...] = jnp.zeros_like(dstate)
            dp_ref[...] = jnp.zeros_like(dp_ref)

        prow, causal, pre_c, dtc, a_row, cs_col, cs_row = _ssd_chunk_common(dt_ref, dtT_ref, prow_ref, pcol_ref)
        lane = lax.broadcasted_iota(jnp.int32, (SSD_L, LANE), 1)
        sub = lax.broadcasted_iota(jnp.int32, (LANE, SSD_L), 0)
        rowi = lax.broadcasted_iota(jnp.int32, (SSD_L, 1), 0)
        pick_p = lax.broadcasted_iota(jnp.int32, (LANE, LANE), 0)
        pick_l = lax.broadcasted_iota(jnp.int32, (LANE, LANE), 1)
        lo = lane < SSD_P
        lo1 = lo[0:1, :]
        dcs_c = jnp.zeros((SSD_L, LANE), F32)
        dcs_r = jnp.zeros((LANE, SSD_L), F32)
        ddt_x = jnp.zeros((SSD_L, LANE), F32)
        dd_row = jnp.zeros((1, LANE), F32)
        for g in range(SSD_G):
            bm = b_ref[:, g * SSD_N:(g + 1) * SSD_N]
            cm = c_ref[:, g * SSD_N:(g + 1) * SSD_N]
            cb = _dot(cm, bm, ((1,), (1,)))
            dcb = jnp.zeros((SSD_L, SSD_L), F32)
            dbm = jnp.zeros((SSD_L, SSD_N), F32)
            dcm = jnp.zeros((SSD_L, SSD_N), F32)
            for qq in range(PAIRS_PER_GROUP):
                q = g * PAIRS_PER_GROUP + qq
                ha, hb = 2 * q, 2 * q + 1
                csa, csb = cs_col[:, ha:ha + 1], cs_col[:, hb:hb + 1]
                xp = x_ref[:, q * LANE:(q + 1) * LANE]
                dtp = jnp.where(lo, dtc[:, ha:ha + 1], dtc[:, hb:hb + 1])
                xx = xp * dtp
                lma = jnp.exp(jnp.where(causal, csa - cs_row[ha:ha + 1, :], NEG))
                lmb = jnp.exp(jnp.where(causal, csb - cs_row[hb:hb + 1, :], NEG))
                ga, gb = cb * lma, cb * lmb
                dyp = dy_ref[:, q * LANE:(q + 1) * LANE]
                dya, dyb = jnp.where(lo, dyp, 0.0), jnp.where(lo, 0.0, dyp)
                s_in = st_ref[q]
                ds_out = dstate[q]
                la, lb = csa[SSD_L - 1:SSD_L, :], csb[SSD_L - 1:SSD_L, :]
                ecs = jnp.where(lo, jnp.exp(csa), jnp.exp(csb))
                decay = jnp.where(lo, jnp.exp(la - csa), jnp.exp(lb - csb))
                cd = jnp.where(lo1, jnp.exp(la), jnp.exp(lb))
                bds = _dot(bm, ds_out, ((1,), (0,)))
                dxx = _dot(ga, dya, ((0,), (0,))) + _dot(gb, dyb, ((0,), (0,))) + bds * decay
                dga = _dot(dya, xx, ((1,), (1,)))
                dgb = _dot(dyb, xx, ((1,), (1,)))
                dsega, dsegb = dga * ga, dgb * gb
                dcb = dcb + dga * lma + dgb * lmb
                yoff = _dot(cm, s_in, ((1,), (0,))) * ecs
                dye = dyp * ecs
                dcm = dcm + _dot(dye, s_in, ((1,), (1,)))
                xd = xx * decay
                dbm = dbm + _dot(xd, ds_out, ((1,), (1,)))
                wv = xd * bds
                ends = jnp.sum(wv, axis=0, keepdims=True) + cd * jnp.sum(ds_out * s_in, axis=0, keepdims=True)
                t1 = dyp * yoff - wv + jnp.where(rowi == SSD_L - 1, ends, 0.0)
                to_pair = (((pick_p < SSD_P) & (pick_l == ha)) | ((pick_p >= SSD_P) & (pick_l == hb))).astype(BF16)
                to_a_b = jnp.concatenate([(pick_l == ha).astype(BF16), (pick_l == hb).astype(BF16)], axis=0)
                dcs_c = dcs_c + _dot2(t1, to_pair) + _dot2(jnp.concatenate([dsega, dsegb], axis=1), to_a_b)
                dcs_r = (dcs_r + jnp.where(sub == ha, jnp.sum(dsega, axis=0, keepdims=True), 0.0)
                         + jnp.where(sub == hb, jnp.sum(dsegb, axis=0, keepdims=True), 0.0))
                dstate[q] = _dot(cm, dye, ((0,), (0,))) + cd * ds_out
                dpair = jnp.where(lo1, prow[2:3, ha:ha + 1], prow[2:3, hb:hb + 1])
                dx_ref[:, q * LANE:(q + 1) * LANE] = dxx * dtp + dpair * dyp
                ddt_x = ddt_x + _dot2(dxx * xp, to_pair)
                dd_row = dd_row + jnp.sum(_dot2(dyp * xp, to_pair), axis=0, keepdims=True)
            dx_ref[:, SSD_INNER + g * SSD_N:SSD_INNER + (g + 1) * SSD_N] = dbm + _dot(dcb, cm, ((0,), (0,)))
            dx_ref[:, SSD_INNER + (SSD_G + g) * SSD_N:SSD_INNER + (SSD_G + g + 1) * SSD_N] = dcm + _dot(dcb, bm, ((1,), (0,)))
        ri = lax.broadcasted_iota(jnp.int32, (SSD_L, SSD_L), 0)
        ci = lax.broadcasted_iota(jnp.int32, (SSD_L, SSD_L), 1)
        da = _dot3((ri <= ci).astype(BF16), dcs_c, ((1,), (0,)), False)
        da = da - _dot3(dcs_r, causal.astype(BF16), ((1,), (0,)), True).T
        ddt = ddt_x + da * a_row
        ddt_raw = ddt * _sigmoid(pre_c)
        ddt_ref[...] = ddt_raw
        da_head = jnp.sum(da * dtc, axis=0, keepdims=True) * a_row
        dp_ref[0:1, :] += jnp.sum(ddt_raw, axis=0, keepdims=True)
        dp_ref[1:2, :] += da_head
        dp_ref[2:3, :] += dd_row

    L = SSD_L
    rev = SSD_NC - 1
    bc_cols = SSD_INNER // (SSD_G * SSD_N)
    return pl.pallas_call(
        body, name="ssd_bwd", grid=(SSD_NC,),
        in_specs=[pl.BlockSpec((L, SSD_INNER), lambda c: (rev - c, 0)),
                  pl.BlockSpec((L, SSD_G * SSD_N), lambda c: (rev - c, bc_cols)),
                  pl.BlockSpec((L, SSD_G * SSD_N), lambda c: (rev - c, bc_cols + 1)),
                  pl.BlockSpec((L, LANE), lambda c: (rev - c, SM_DT // LANE)),
                  pl.BlockSpec((LANE, L), lambda c: (0, rev - c)),
                  pl.BlockSpec((8, LANE), lambda c: (0, 0)), pl.BlockSpec((LANE, 8), lambda c: (0, 0)),
                  pl.BlockSpec((None, NPAIR, SSD_N, LANE), lambda c: (rev - c, 0, 0, 0)),
                  pl.BlockSpec((L, SSD_INNER), lambda c: (rev - c, 0))],
        out_specs=[pl.BlockSpec((L, SSD_XBC), lambda c: (rev - c, 0)),
                   pl.BlockSpec((L, LANE), lambda c: (rev - c, 0)),
                   pl.BlockSpec((8, LANE), lambda c: (0, 0))],
        out_shape=[jax.ShapeDtypeStruct((S, SSD_XBC), F32), jax.ShapeDtypeStruct((S, LANE), F32),
                   jax.ShapeDtypeStruct((8, LANE), F32)],
        scratch_shapes=[pltpu.VMEM((NPAIR, SSD_N, LANE), F32)],
        compiler_params=pltpu.CompilerParams(dimension_semantics=("arbitrary",)),
    )(act, act, act, small, dtT, prow, pcol, states, dy)


TQ = 256
TK = 256
FWD_TQ = 256
FWD_TK = 256


def _attn_fwd(qc, kc, v):
    TQ, TK = FWD_TQ, FWD_TK

    def body(q_ref, k_ref, v_ref, o_ref, lse_ref):
        i = pl.program_id(1)
        lo = lax.broadcasted_iota(jnp.int32, (TQ, LANE), 1) < VDIM
        lo_k = lax.broadcasted_iota(jnp.int32, (TK, LANE), 1) < VDIM
        row_minus_col = lax.broadcasted_iota(jnp.int32, (TQ, TK), 0) - lax.broadcasted_iota(jnp.int32, (TQ, TK), 1)
        qa, qb = q_ref[:, 0:LANE], q_ref[:, LANE:2 * LANE]

        def scores(kb):
            kk = k_ref[pl.ds(pl.multiple_of(kb * TK, TK), TK), :]
            return (_dot(qa, kk[:, 0:LANE], ((1,), (1,))) * ATT_SCALE_LOG2, _dot(qb, kk[:, LANE:2 * LANE], ((1,), (1,))) * ATT_SCALE_LOG2)

        def update(kb, sa, sb, stats):
            ma, la, mb, lb, acc = stats
            vv = v_ref[pl.ds(pl.multiple_of(kb * TK, TK), TK), :]
            na = jnp.maximum(ma, jnp.max(sa, axis=1, keepdims=True))
            nb = jnp.maximum(mb, jnp.max(sb, axis=1, keepdims=True))
            pa, pb = jnp.exp2(sa - na), jnp.exp2(sb - nb)
            fa, fb = jnp.exp2(ma - na), jnp.exp2(mb - nb)
            la = fa * la + jnp.sum(pa, axis=1, keepdims=True)
            lb = fb * lb + jnp.sum(pb, axis=1, keepdims=True)
            acc = (acc * jnp.where(lo, fa, fb) + _dot(pa, jnp.where(lo_k, vv, 0), ((1,), (0,)))
                   + _dot(pb, jnp.where(lo_k, 0, vv), ((1,), (0,))))
            return na, la, nb, lb, acc

        def step(kb, carry):
            sa, sb = carry[:2]
            nxt = scores(kb + 1)
            return nxt + update(kb, sa, sb, carry[2:])

        neg = jnp.full((TQ, 1), NEG, F32)
        zero = jnp.zeros((TQ, 1), F32)
        n_full = i * (TQ // TK)
        carry = lax.fori_loop(0, n_full, step, scores(0) + (neg, zero, neg, zero, jnp.zeros((TQ, LANE), F32)))
        s, stats = carry[:2], carry[2:]
        for d in range(TQ // TK):
            nxt = scores(n_full + d + 1) if d + 1 < TQ // TK else None
            sa, sb = (jnp.where(row_minus_col >= d * TK, t, NEG) for t in s)
            stats = update(n_full + d, sa, sb, stats)
            s = nxt
        ma, la, mb, lb, acc = stats
        o_ref[...] = acc / jnp.where(lo, la, lb)
        lse_ref[...] = jnp.where(lo, ma + jnp.log2(la), mb + jnp.log2(lb)) * LN2

    return pl.pallas_call(
        body, name="attn_fwd", grid=(NPAIR, S // TQ),
        in_specs=[pl.BlockSpec((TQ, 2 * LANE), lambda j, i: (i, j)), pl.BlockSpec((S, 2 * LANE), lambda j, i: (0, j)),
                  pl.BlockSpec((S, LANE), lambda j, i: (0, j))],
        out_specs=[pl.BlockSpec((TQ, LANE), lambda j, i: (i, j)), pl.BlockSpec((None, TQ, LANE), lambda j, i: (j, i, 0))],
        out_shape=[jax.ShapeDtypeStruct((S, H * VDIM), F32), jax.ShapeDtypeStruct((NPAIR, S, LANE), F32)],
        compiler_params=pltpu.CompilerParams(dimension_semantics=("parallel", "parallel")),
    )(qc, kc, v)


def _attn_rows(lse, o, do):
    def body(lse_ref, o_ref, do_ref, r_ref):
        lt = lse_ref[...].T * (1.0 / LN2)
        tt = (o_ref[...] * do_ref[...]).T
        r_ref[...] = jnp.zeros_like(r_ref)
        r_ref[0:1, :] = lt[0:1, :]
        r_ref[1:2, :] = lt[VDIM:VDIM + 1, :]
        r_ref[2:3, :] = jnp.sum(tt[0:VDIM, :], axis=0, keepdims=True)
        r_ref[3:4, :] = jnp.sum(tt[VDIM:LANE, :], axis=0, keepdims=True)

    tile = pl.BlockSpec((S, LANE), lambda j: (0, j))
    return pl.pallas_call(
        body, name="attn_rows", grid=(NPAIR,), in_specs=[pl.BlockSpec((None, S, LANE), lambda j: (j, 0, 0)), tile, tile],
        out_specs=pl.BlockSpec((None, 8, S), lambda j: (j, 0, 0)), out_shape=jax.ShapeDtypeStruct((NPAIR, 8, S), F32),
    )(lse, o, do)


def _attn_bwd(qc, kc, kct, v, do, rows):
    nq = S // TQ

    def body(q_ref, k_ref, kt_ref, v_ref, do_ref, r_ref, dqt_ref, dk_ref, dv_ref):
        kb = pl.program_id(1)

        @pl.when(kb == 0)
        def _():
            dqt_ref[...] = jnp.zeros_like(dqt_ref)

        lo = lax.broadcasted_iota(jnp.int32, (TK, LANE), 1) < VDIM
        q_minus_k = lax.broadcasted_iota(jnp.int32, (TK, TQ), 1) - lax.broadcasted_iota(jnp.int32, (TK, TQ), 0)
        vv = v_ref[...]
        kk = k_ref[...]

        def step(qi, carry):
            off = pl.multiple_of(qi * TQ, TQ)
            qq = q_ref[pl.ds(off, TQ), :]
            dd = do_ref[pl.ds(off, TQ), :].astype(BF16)
            rr = r_ref[:, pl.ds(off, TQ)]
            keep = q_minus_k >= (kb - qi) * TQ
            out = []
            for x in range(2):
                sel = lo if x == 0 else jnp.logical_not(lo)
                kx, qx = kk[:, x * LANE:(x + 1) * LANE], qq[:, x * LANE:(x + 1) * LANE]
                st = jnp.where(keep, _dot(kx, qx, ((1,), (1,))) * ATT_SCALE_LOG2, NEG)
                pt = jnp.exp2(st - rr[x:x + 1, :])
                dpt = _dot(jnp.where(sel, vv, 0), dd, ((1,), (1,)))
                dst = (pt * (dpt - rr[2 + x:3 + x, :]) * ATT_SCALE).astype(BF16)
                out.append(carry[x] + _dot(dst, qx, ((1,), (0,))))
                out.append(_dot(pt, jnp.where(sel, dd, 0), ((1,), (0,))))
                dqt_ref[x * LANE:(x + 1) * LANE, pl.ds(off, TQ)] += _dot(kt_ref[x * LANE:(x + 1) * LANE, :], dst, ((1,), (0,)))
            return out[0], out[2], carry[2] + out[1] + out[3]

        z = jnp.zeros((TK, LANE), F32)
        dka, dkb, dv = lax.fori_loop(kb, nq, step, (z, z, z))
        dk_ref[:, 0:LANE] = dka
        dk_ref[:, LANE:2 * LANE] = dkb
        dv_ref[...] = dv.astype(BF16)

    return pl.pallas_call(
        body, name="attn_bwd", grid=(NPAIR, S // TK),
        in_specs=[pl.BlockSpec((S, 2 * LANE), lambda j, k: (0, j)), pl.BlockSpec((TK, 2 * LANE), lambda j, k: (k, j)),
                  pl.BlockSpec((2 * LANE, TK), lambda j, k: (j, k)), pl.BlockSpec((TK, LANE), lambda j, k: (k, j)),
                  pl.BlockSpec((S, LANE), lambda j, k: (0, j)), pl.BlockSpec((None, 8, S), lambda j, k: (j, 0, 0))],
        out_specs=[pl.BlockSpec((2 * LANE, S), lambda j, k: (j, 0)), pl.BlockSpec((TK, 2 * LANE), lambda j, k: (k, j)),
                   pl.BlockSpec((TK, LANE), lambda j, k: (k, j))],
        out_shape=[jax.ShapeDtypeStruct((H * LANE, S), F32), jax.ShapeDtypeStruct((S, H * LANE), F32),
                   jax.ShapeDtypeStruct((S, H * VDIM), BF16)],
        compiler_params=pltpu.CompilerParams(dimension_semantics=("parallel", "arbitrary")),
    )(qc, kc, kct, v, do, rows)


_IN_Z, _IN_XBC, _IN_DT, _IN_Q, _IN_KV, _IN_KR = 0, 1024, 2560, 2576, 2960, 3216


PROJ_COLS = 512
SMALL_PAD = pl.cdiv(SMALL_W, PROJ_COLS) * PROJ_COLS


def _prep_in(w_in_t):
    dt = w_in_t.dtype
    return jnp.concatenate(
        [w_in_t[_IN_Q:_IN_KV], w_in_t[_IN_KV:_IN_KR], w_in_t[_IN_KR:IN_WIDTH], jnp.zeros((LANE - ROPE, D), dt),
         w_in_t[_IN_DT:_IN_Q], jnp.zeros((SMALL_PAD - SM_DT - H, D), dt)], axis=0)


def _proj_in(xb, w_in_t, w_small):
    nz, nx, ns = (_IN_XBC - _IN_Z) // PROJ_COLS, (_IN_DT - _IN_XBC) // PROJ_COLS, SMALL_PAD // PROJ_COLS

    dt_block, dt_at = divmod(SM_DT, PROJ_COLS)

    def body(x_ref, w_ref, ws_ref, z_ref, xbc_ref, sm_ref, dtt_ref):
        i = pl.program_id(0)

        def emit(w, o_ref):
            o_ref[...] = lax.dot_general(x_ref[...], w[...], (((1,), (1,)), ((), ())), preferred_element_type=F32)

        pl.when(i < nz)(lambda: emit(w_ref, z_ref))
        pl.when((i >= nz) & (i < nz + nx))(lambda: emit(w_ref, xbc_ref))
        pl.when(i >= nz + nx)(lambda: emit(ws_ref, sm_ref))

        @pl.when(i == nz + nx + dt_block)
        def _():
            dtt_ref[...] = sm_ref[:, dt_at:dt_at + LANE].T

    def blocks(first, count, rows):
        at = lambda i: jnp.clip(i - first, 0, count - 1)
        return pl.BlockSpec((PROJ_COLS, D), lambda i: (at(i), 0)) if rows else pl.BlockSpec((S, PROJ_COLS), lambda i: (0, at(i)))

    return pl.pallas_call(
        body, name="proj_in", grid=(nz + nx + ns,),
        in_specs=[pl.BlockSpec((S, D), lambda i: (0, 0)), blocks(0, nz + nx, True), blocks(nz + nx, ns, True)],
        out_specs=[blocks(0, nz, False), blocks(nz, nx, False), blocks(nz + nx, ns, False), pl.BlockSpec((LANE, S), lambda i: (0, 0))],
        out_shape=[jax.ShapeDtypeStruct((S, _IN_XBC - _IN_Z), F32), jax.ShapeDtypeStruct((S, _IN_DT - _IN_XBC), F32),
                   jax.ShapeDtypeStruct((S, SMALL_W), F32), jax.ShapeDtypeStruct((LANE, S), F32)],
    )(xb, w_in_t, w_small)


PART_COLS = 512


def _part_blocks(widths):
    first = [0]
    for w in widths:
        first.append(first[-1] + w // PART_COLS)

    def at(part):
        return lambda i: jnp.clip(i - first[part], 0, first[part + 1] - first[part] - 1)

    return first, at


def _mm_ta_stacked(parts, b, rows, name):
    n = b.shape[1]
    first, at = _part_blocks([a.shape[1] for a in parts])
    assert first[-1] == pl.cdiv(rows, PART_COLS)

    def body(*refs):
        b_ref, o_ref = refs[-2:]
        i = pl.program_id(0)
        for part, a_ref in enumerate(refs[:-2]):
            @pl.when((i >= first[part]) & (i < first[part + 1]))
            def _(a_ref=a_ref):
                o_ref[...] = lax.dot_general(a_ref[...], b_ref[...], (((0,), (0,)), ((), ())),
                                             preferred_element_type=F32).astype(BF16)

    return pl.pallas_call(
        body, name=name, grid=(first[-1],),
        in_specs=[pl.BlockSpec((S, PART_COLS), lambda i, at=at(part): (0, at(i))) for part in range(len(parts))]
        + [pl.BlockSpec((S, n), lambda i: (0, 0))],
        out_specs=pl.BlockSpec((PART_COLS, n), lambda i: (i, 0)), out_shape=jax.ShapeDtypeStruct((rows, n), BF16),
    )(*parts, b)


def _mm_tb_split(a, b, widths, name):
    k = a.shape[1]
    first, at = _part_blocks(widths)

    def body(a_ref, b_ref, *o_refs):
        i = pl.program_id(0)
        for part, o_ref in enumerate(o_refs):
            @pl.when((i >= first[part]) & (i < first[part + 1]))
            def _(o_ref=o_ref):
                o_ref[...] = lax.dot_general(a_ref[...], b_ref[...], (((1,), (1,)), ((), ())), preferred_element_type=F32)

    return pl.pallas_call(
        body, name=name, grid=(first[-1],),
        in_specs=[pl.BlockSpec((S, k), lambda i: (0, 0)), pl.BlockSpec((PART_COLS, k), lambda i: (i, 0))],
        out_specs=[pl.BlockSpec((S, PART_COLS), lambda i, at=at(part): (0, at(i))) for part in range(len(widths))],
        out_shape=[jax.ShapeDtypeStruct((S, w), F32) for w in widths],
    )(a, b)


def _prep_attn(w_qb, w_kvb):
    w_q = jnp.pad(w_qb.reshape(Q_RANK, H, NOPE + ROPE), ((0, 0), (0, 0), (0, LANE - NOPE - ROPE))).reshape(Q_RANK, H * LANE)
    kv3 = w_kvb.reshape(KV_RANK, H, NOPE + VDIM)
    w_k = jnp.pad(kv3[:, :, :NOPE], ((0, 0), (0, 0), (0, LANE - NOPE))).reshape(KV_RANK, H * LANE)
    w_v = kv3[:, :, NOPE:].reshape(KV_RANK, H * VDIM)
    return w_q, w_k, w_v


def _rope_tables(positions):
    inv_freq = 1.0 / (10000.0 ** (jnp.arange(0, ROPE, 2, dtype=F32) / ROPE))
    ang = positions.astype(F32).reshape(S, 1) * inv_freq
    cos, sin = jnp.cos(ang), jnp.sin(ang)
    cos_t = jnp.concatenate([jnp.ones((S, NOPE), F32), cos, cos, jnp.ones((S, LANE - NOPE - ROPE), F32)], axis=1)
    sin_t = jnp.concatenate([jnp.zeros((S, NOPE), F32), -sin, sin, jnp.zeros((S, LANE - NOPE - ROPE), F32)], axis=1)
    return cos_t, sin_t


def _local_step(x, p, positions, target, w_in, fetch, send, sp, started):
    w_in_t = w_in.reshape(IN_WIDTH, D)
    w_small = _prep_in(w_in_t)
    cos_t, sin_t = _rope_tables(positions)
    prow = jnp.zeros((8, LANE), F32).at[0, :H].set(sp["dt_bias"][0]).at[1, :H].set(sp["A_log"][0]).at[2, :H].set(sp["D"][0])
    pcol = prow.T

    xb, pb = (x + started).astype(BF16), p.astype(BF16)
    z, xbc, small, dt_t = _proj_in(xb, w_in_t, w_small)
    act = _conv_fwd(xbc, sp["conv_w"], sp["conv_b"])
    y, states = _ssd_fwd(act, small, dt_t, prow, pcol)
    y_ssd = _gate_norm_fwd(y, z, sp["ssd_norm"])
    gl = fetch("attn", y_ssd)
    w_q, w_k, w_v = _prep_attn(_from_cols(gl["w_qb"]), _from_cols(gl["w_kvb"]))
    qn, kvn, qcat, kcat, kcat_t, v = _qkv_fwd(small, w_q, w_k, w_v, sp["q_norm"], sp["kv_norm"], cos_t, sin_t)
    o, lse = _attn_fwd(qcat, kcat, v)
    y_mla = _rms_fwd(o, sp["out_norm"], name="out_norm_fwd")
    w_out = fetch("out", y_mla)["w_out"]
    w_out = w_out.reshape(2 * SSD_INNER, D)
    mix = _mm([(y_ssd, w_out, (0, 0, SSD_INNER)), (y_mla, w_out, (0, 1, SSD_INNER))], name="out_proj")
    h1, h1b = _ln_fwd(x, mix, sp["ln_mix_g"], sp["ln_mix_b"])
    gl = fetch("ffn", h1b)
    w_pg, w_pp = gl["w_pg"].reshape(D, D), _from_cols(gl["w_pp"])
    w_gate, w_up, w_down = gl["w_gate"], gl["w_up"], gl["w_down"]
    gate, up, actf = _ffn_hidden_fwd(h1b, w_gate, w_up)
    ffn = _mm([(actf, w_down)], chunk="sum", name="ffn_down")
    pg = _mm([(h1b, w_pg)], name="ple_gate")
    pp = _mm([(pb, w_pp)], name="ple_proj")
    dpre2, dpre2b, dpg, dpp, dg2, db2, loss_row = _final_fwd_bwd(h1, ffn, pg, pp, target, sp["ln_ffn_g"], sp["ln_ffn_b"])

    g = {"ln_ffn_g": dg2, "ln_ffn_b": db2}
    g["w_pp"] = _to_cols(_mm([(pb, dpp)], ta=True, out_dtype=BF16, name="d_w_ple_proj"))
    g["w_pg"] = _mm([(h1b, dpg)], ta=True, out_dtype=BF16, name="d_w_ple_gate").reshape(NCHIP, D // NCHIP, D)
    g["w_down"] = _mm([(actf, dpre2b)], ta=True, chunk="out", out_dtype=BF16, name="d_w_down")
    dgate, dup = _ffn_hidden_bwd(dpre2b, w_down, gate, up)
    g["w_gate"] = _mm([(dgate, h1b)], ta=True, chunk="out", out_dtype=BF16, name="d_w_gate")
    g["w_up"] = _mm([(dup, h1b)], ta=True, chunk="out", out_dtype=BF16, name="d_w_up")
    sent = send("ffn", {name: g.pop(name) for name in dict(ASYNC_GROUPS)["ffn"]})
    dh1 = _mm([(dpg, w_pg)], tb=True, add=dpre2, add_scale=ALPHA, name="d_h1_ple")
    dh1 = _mm([(dgate, w_gate), (dup, w_up)], chunk="sum", add=dh1, name="d_h1")
    dpre1, dpre1b, g["ln_mix_g"], g["ln_mix_b"] = _ln_bwd(x, mix, sp["ln_mix_g"] + sent, dh1)
    dy_ssd, dy_mla = _mm_tb_split(dpre1b, w_out, (SSD_INNER, SSD_INNER), "d_y")
    dw_out = _mm_ta_stacked((y_ssd, y_mla), dpre1b, 2 * SSD_INNER, "d_w_out")
    sent = send("out", {"w_out": dw_out.reshape(NCHIP, 2 * SSD_INNER // NCHIP, D)})
    do, g["out_norm"] = _rms_bwd(o, sp["out_norm"] + sent, dy_mla, name="out_norm_bwd")
    dqt, dk, dv = _attn_bwd(qcat, kcat, kcat_t, v, do, _attn_rows(lse, o, do))
    dlatent, dqlin, dkb, g["q_norm"], g["kv_norm"] = _qkv_bwd(dqt, dk, dv, small, w_q, w_k, w_v, sp["q_norm"], sp["kv_norm"], cos_t, sin_t)
    dw_q = _mm([(qn, dqlin)], ta=True, out_dtype=BF16, name="d_w_q")
    dw_k = _mm([(kvn, dkb)], ta=True, out_dtype=BF16, name="d_w_k")
    dw_v = _mm([(kvn, dv)], ta=True, out_dtype=BF16, name="d_w_v")
    dw_qb = _to_cols(dw_q.reshape(Q_RANK, H, LANE)[:, :, :NOPE + ROPE].reshape(Q_RANK, H * (NOPE + ROPE)))
    dw_kvb = _to_cols(jnp.concatenate([dw_k.reshape(KV_RANK, H, LANE)[:, :, :NOPE], dw_v.reshape(KV_RANK, H, VDIM)],
                                       axis=2).reshape(KV_RANK, H * (NOPE + VDIM)))
    sent = send("attn", {"w_qb": dw_qb, "w_kvb": dw_kvb})
    dy, dz, g["ssd_norm"] = _gate_norm_bwd(y, z, sp["ssd_norm"] + sent, dy_ssd)
    dact, ddt, dprow = _ssd_bwd(act, small, dt_t, prow, pcol, states, dy)
    g["dt_bias"], g["A_log"], g["D"] = dprow[0:1, :H], dprow[1:2, :H], dprow[2:3, :H]
    dxbc, g["conv_w"], g["conv_b"] = _conv_bwd(xbc, sp["conv_w"], sp["conv_b"], dact)
    dsmall = jnp.concatenate([dlatent, ddt.astype(BF16)], axis=1)
    in_blocks = [(d, w_in_t, (k, first // PROJ_COLS + k, PROJ_COLS))
                 for d, first in ((dz, _IN_Z), (dxbc, _IN_XBC)) for k in range(d.shape[1] // PROJ_COLS)]
    grad_x = _mm(in_blocks + [(dsmall, w_small, (0, 0, SMALL_W))], add=dpre1, add_scale=ALPHA, name="d_x")
    sent = send("small", dict(g, loss=loss_row))
    n_small = IN_WIDTH - _IN_DT
    dsm = jnp.concatenate([(ddt[:, :H] + sent).astype(BF16), dlatent[:, :n_small - H], jnp.zeros((S, D - n_small), BF16)], axis=1)
    dw_in = _mm_ta_stacked((dz, dxbc, dsm), xb, IN_WIDTH, "d_w_in").reshape(NCHIP, IN_WIDTH // NCHIP * D // LANE, LANE)
    return loss_row, grad_x, dw_in, g


MESH = pl.DeviceIdType.MESH
BIG = (("w_in", (D, IN_WIDTH), 1), ("w_qb", (Q_RANK, H * (NOPE + ROPE)), 1), ("w_kvb", (KV_RANK, H * (NOPE + VDIM)), 1),
       ("w_out", (2 * SSD_INNER, D), 0), ("w_gate", (D, D_FF), 1), ("w_up", (D, D_FF), 1), ("w_down", (D_FF, D), 0),
       ("w_pg", (D, D), 0), ("w_pp", (PLE, D), 1))
CONV_SHARD = SSD_XBC // NCHIP
BF16_ROWS = 16


def _from_cols(stack):
    return jnp.concatenate([stack[k] for k in range(NCHIP)], axis=1)


def _to_cols(full):
    r, c4 = full.shape
    return full.reshape(r, NCHIP, c4 // NCHIP).transpose(1, 0, 2)


def _coords():
    return lax.axis_index("x"), lax.axis_index("y"), lax.axis_index("c")


def _peers():
    x, y, c = _coords()
    return 2 * x + y, c, [(1 - x, y), (x, 1 - y), (1 - x, 1 - y)], (x, y, 1 - c)


def _half_axis(shape):
    return 0 if shape[-2] % (2 * BF16_ROWS) == 0 else 1


def _half_shape(shape):
    r, c = shape[-2:]
    return (r // 2, c) if _half_axis(shape) == 0 else (r, c // 2)


def _half(core, shape):
    r, c = shape[-2:]
    if _half_axis(shape) == 0:
        return pl.ds(pl.multiple_of(core * (r // 2), BF16_ROWS), r // 2), slice(None)
    return slice(None), pl.ds(pl.multiple_of(core * (c // 2), LANE), c // 2)


def _gather_weights(shards):
    n_arr = len(shards)
    per = 2 * (NCHIP - 1)

    def body(*refs):
        ins, outs = refs[:n_arr], refs[n_arr:2 * n_arr]
        send_sems, recv_sems, local_sems = refs[2 * n_arr:]
        k, c, chips, sibling = _peers()

        def copy(idx, src, dst, to):
            return pltpu.make_async_remote_copy(src_ref=src, dst_ref=dst, send_sem=send_sems.at[idx], recv_sem=recv_sems.at[idx],
                                                device_id=to, device_id_type=MESH)

        def part(a, chip, core):
            return outs[a].at[chip, *_half(core, shards[a].shape)]

        mine = [pltpu.make_async_copy(ins[a], outs[a].at[k], local_sems.at[a]) for a in range(n_arr)]
        for cp in mine:
            cp.start()
        sends = []
        for a in range(n_arr):
            for j, (cx, cy) in enumerate(chips):
                sends.append(copy(per * a + j, ins[a].at[*_half(c, shards[a].shape)], part(a, k, c), (cx, cy, c)))
                sends[-1].start()
        for j, (cx, cy) in enumerate(chips):
            for a in range(n_arr):
                landed = part(a, 2 * cx + cy, c)
                copy(per * a + j, landed, landed, (cx, cy, c)).wait_recv()
                sends.append(copy(per * a + NCHIP - 1 + j, landed, landed, sibling))
                sends[-1].start()
        for j, (cx, cy) in enumerate(chips):
            for a in range(n_arr):
                other = part(a, 2 * cx + cy, 1 - c)
                copy(per * a + NCHIP - 1 + j, other, other, sibling).wait_recv()
        for cp in sends:
            cp.wait_send()
        for cp in mine:
            cp.wait()

    any_spec = pl.BlockSpec(memory_space=pl.ANY)
    return pl.pallas_call(
        body, name="gather_weights", in_specs=[any_spec] * n_arr, out_specs=[any_spec] * n_arr,
        out_shape=[jax.ShapeDtypeStruct((NCHIP,) + s.shape, s.dtype) for s in shards],
        scratch_shapes=[pltpu.SemaphoreType.DMA((per * n_arr,)), pltpu.SemaphoreType.DMA((per * n_arr,)),
                        pltpu.SemaphoreType.DMA((n_arr,))],
    )(*shards)


ASYNC_GROUPS = (("attn", ("w_qb", "w_kvb")), ("out", ("w_out",)), ("ffn", ("w_gate", "w_up", "w_down", "w_pg", "w_pp")))
TRANSPOSED = ("w_in", "w_gate", "w_up")
ROW_MAJOR = ("w_in",)
HBM_SPEC = pl.BlockSpec(memory_space=pltpu.HBM)
SEM_SPEC = pl.BlockSpec(memory_space=pltpu.SEMAPHORE)
IN_FLIGHT = pltpu.SideEffectType.DATAFLOW_SIDE_EFFECTING


def _in_hbm(a):
    return pltpu.with_memory_space_constraint(a, pltpu.HBM)


def _hbm_like(arrs, lead=()):
    return [pltpu.HBM(lead + a.shape, a.dtype) for a in arrs]


def _split_start(name, srcs, lands, after, n_sem, start):
    n = len(srcs)
    order = [] if after is None else [after]

    def body(*refs):
        src_refs, land_refs = refs[:n], refs[n:2 * n]
        send_sems, recv_sems = refs[2 * n + len(order)], refs[2 * n + len(order) + 1]
        token = refs[-1]

        def copy(send_idx, recv_idx, src, dst, to):
            return pltpu.make_async_remote_copy(src_ref=src, dst_ref=dst, send_sem=send_sems.at[send_idx],
                                                recv_sem=recv_sems.at[recv_idx], device_id=to, device_id_type=MESH)

        for cp in start(src_refs, land_refs, copy):
            cp.start()
        token[...] = jnp.zeros_like(token)

    sem = pltpu.SemaphoreType.DMA((n_sem,))
    outs = pl.pallas_call(
        body, name=name, in_specs=[HBM_SPEC] * (2 * n) + [pl.BlockSpec(memory_space=pl.ANY)] * len(order),
        out_specs=[SEM_SPEC, SEM_SPEC] + [HBM_SPEC] * (2 * n) + [pl.BlockSpec(memory_space=pltpu.VMEM)],
        out_shape=[sem, sem] + _hbm_like(srcs) + _hbm_like(lands) + [jax.ShapeDtypeStruct((8, LANE), F32)],
        input_output_aliases={i: 2 + i for i in range(2 * n)},
        compiler_params=pltpu.CompilerParams(has_side_effects=IN_FLIGHT),
    )(*[_in_hbm(a) for a in srcs], *[_in_hbm(a) for a in lands], *order)
    return (outs[0], outs[1], outs[2:2 + n], outs[2 + n:2 + 2 * n]), outs[-1]


def _split_wait(name, send_sems, recv_sems, srcs, lands, after, waits):
    n = len(srcs)

    def body(*refs):
        src_refs, land_refs = refs[:n], refs[n:2 * n]
        send_ref, recv_ref = refs[2 * n], refs[2 * n + 1]

        def copy(send_idx, recv_idx, src, dst, to):
            return pltpu.make_async_remote_copy(src_ref=src, dst_ref=dst, send_sem=send_ref.at[send_idx],
                                                recv_sem=recv_ref.at[recv_idx], device_id=to, device_id_type=MESH)

        for cp in waits(src_refs, land_refs, copy):
            cp.wait_send()
            cp.wait_recv()

    outs = pl.pallas_call(
        body, name=name, in_specs=[HBM_SPEC] * (2 * n) + [SEM_SPEC, SEM_SPEC, pl.BlockSpec(memory_space=pl.ANY)],
        out_specs=[HBM_SPEC] * (2 * n), out_shape=_hbm_like(srcs) + _hbm_like(lands),
        input_output_aliases={i: i for i in range(2 * n)},
        compiler_params=pltpu.CompilerParams(has_side_effects=IN_FLIGHT),
    )(*srcs, *lands, send_sems, recv_sems, after)
    return outs[:n], outs[n:]


GATHER_LATE_SEMS = 2 * (NCHIP - 1)


def _gather_async_start(tag, shards, after):
    def start(srcs, lands, copy):
        k, c, chips, _ = _peers()
        out = []
        for a, (src, dst) in enumerate(zip(srcs, lands)):
            for j, (cx, cy) in enumerate(chips):
                for core in range(2):
                    out.append(copy(GATHER_LATE_SEMS * a + 2 * j + core, GATHER_LATE_SEMS * a + 2 * j + c,
                                    src.at[*_half(c, src.shape)], dst.at[k, *_half(c, src.shape)], (cx, cy, core)))
        return out

    chip = 2 * lax.axis_index("x") + lax.axis_index("y")
    lands = [lax.dynamic_update_slice(lax.empty((NCHIP,) + s.shape, s.dtype), s[None], (chip, 0, 0)) for s in shards]
    return _split_start("gather_%s_start" % tag, shards, lands, after, GATHER_LATE_SEMS * len(shards), start)


def _gather_async_wait(tag, send_sems, recv_sems, shards, lands, after):
    def waits(srcs, lands_, copy):
        _, c, chips, _ = _peers()
        out = []
        for a, (src, dst) in enumerate(zip(srcs, lands_)):
            for j, (cx, cy) in enumerate(chips):
                for core in range(2):
                    idx = GATHER_LATE_SEMS * a + 2 * j + core
                    out.append(copy(idx, idx, src.at[*_half(c, src.shape)], dst.at[2 * cx + cy, *_half(core, src.shape)], (cx, cy, core)))
        return out

    return _split_wait("gather_%s_wait" % tag, send_sems, recv_sems, shards, lands, after, waits)[1]


def _other_devices():
    x, y, c = _coords()
    out = []
    for d in range(1, NDEV):
        tx, ty, tc = x ^ (d >> 2), y ^ ((d >> 1) & 1), c ^ (d & 1)
        out.append((d, (tx, ty, tc), 2 * tx + ty, 4 * tx + 2 * ty + tc))
    return out


def _reduce_async_start(tag, stacks, after):
    def start(srcs, lands, copy):
        x, y, c = _coords()
        me = 4 * x + 2 * y + c
        return [copy((NDEV - 1) * a + d - 1, (NDEV - 1) * a + d - 1, src.at[chip, *_half(to[2], src.shape)], dst.at[me], to)
                for a, (src, dst) in enumerate(zip(srcs, lands)) for d, to, chip, _ in _other_devices()]

    x, y, c = _coords()
    lands = []
    for s in stacks:
        hr, hc = _half_shape(s.shape)
        at = (c * hr, 0) if _half_axis(s.shape) == 0 else (0, c * hc)
        own = lax.dynamic_slice(s, (2 * x + y,) + at, (1, hr, hc))
        lands.append(lax.dynamic_update_slice(lax.empty((NDEV, hr, hc), s.dtype), own, (4 * x + 2 * y + c, 0, 0)))
    return _split_start("reduce_%s_start" % tag, stacks, lands, after, (NDEV - 1) * len(stacks), start)


def _reduce_async_wait(tag, send_sems, recv_sems, stacks, lands, after):
    def waits(srcs, lands_, copy):
        return [copy((NDEV - 1) * a + d - 1, (NDEV - 1) * a + d - 1, src.at[chip, *_half(to[2], src.shape)], dst.at[pos], to)
                for a, (src, dst) in enumerate(zip(srcs, lands_)) for d, to, chip, pos in _other_devices()]

    return _split_wait("reduce_%s_wait" % tag, send_sems, recv_sems, stacks, lands, after, waits)[1]


def _reduce_finish(tag, arrived, dims):
    n_arr = len(arrived)

    def body(*refs):
        lands, fin = refs[:n_arr], refs[n_arr:2 * n_arr]
        send_sems, recv_sems = refs[2 * n_arr:]
        _, c, _, sibling = _peers()
        sends = []
        for a in range(n_arr):
            mine = fin[a].at[*_half(c, dims[a])]

            def device_sum(vs, vf, a=a, mine=mine):
                pltpu.sync_copy(lands[a], vs)
                acc = vs[0].astype(F32)
                for i in range(1, NDEV):
                    acc = acc + vs[i].astype(F32)
                vf[...] = acc
                pltpu.sync_copy(vf, mine)

            pl.run_scoped(device_sum, pltpu.VMEM((NDEV,) + _half_shape(dims[a]), BF16), pltpu.VMEM(_half_shape(dims[a]), F32))
            sends.append(pltpu.make_async_remote_copy(src_ref=mine, dst_ref=mine, send_sem=send_sems.at[a], recv_sem=recv_sems.at[a],
                                                      device_id=sibling, device_id_type=MESH))
            sends[-1].start()
        for a in range(n_arr):
            other = fin[a].at[*_half(1 - c, dims[a])]
            pltpu.make_async_remote_copy(src_ref=other, dst_ref=other, send_sem=send_sems.at[a], recv_sem=recv_sems.at[a],
                                         device_id=sibling, device_id_type=MESH).wait_recv()
        for cp in sends:
            cp.wait_send()

    any_spec = pl.BlockSpec(memory_space=pl.ANY)
    return pl.pallas_call(
        body, name="reduce_%s_finish" % tag, in_specs=[any_spec] * n_arr, out_specs=[any_spec] * n_arr,
        out_shape=[jax.ShapeDtypeStruct(d, F32) for d in dims],
        scratch_shapes=[pltpu.SemaphoreType.DMA((n_arr,)), pltpu.SemaphoreType.DMA((n_arr,))],
    )(*arrived)


SMALL = (("conv_w", SSD_K * SSD_XBC), ("conv_b", SSD_XBC), ("dt_bias", H), ("A_log", H), ("D", H), ("ssd_norm", SSD_INNER),
         ("q_norm", Q_RANK), ("kv_norm", KV_RANK), ("out_norm", SSD_INNER), ("ln_mix_g", D), ("ln_mix_b", D),
         ("ln_ffn_g", D), ("ln_ffn_b", D))
SMALL_ROWS = 120
NDEV = 8


def _allreduce_small_start(sv):
    def start(srcs, lands, copy):
        x, y, c = _coords()
        return [copy(d - 1, d - 1, srcs[0], lands[0].at[4 * x + 2 * y + c], to) for d, to, _, _ in _other_devices()]

    x, y, c = _coords()
    slots = lax.dynamic_update_slice(lax.empty((NDEV,) + sv.shape, sv.dtype), sv[None], (4 * x + 2 * y + c, 0, 0))
    return _split_start("allreduce_small_start", [sv], [slots], None, NDEV - 1, start)


def _allreduce_small_wait(send_sems, recv_sems, srcs, lands, after):
    def waits(srcs_, lands_, copy):
        return [copy(d - 1, d - 1, srcs_[0], lands_[0].at[pos], to) for d, to, _, pos in _other_devices()]

    def device_sum(slots_ref, out_ref):
        acc = slots_ref[0]
        for i in range(1, NDEV):
            acc = acc + slots_ref[i]
        out_ref[...] = acc

    slots = _split_wait("allreduce_small_wait", send_sems, recv_sems, srcs, lands, after, waits)[1][0]
    vm = pl.BlockSpec(memory_space=pltpu.VMEM)
    return pl.pallas_call(device_sum, name="allreduce_small_sum", in_specs=[vm], out_specs=vm,
                          out_shape=jax.ShapeDtypeStruct(slots.shape[1:], slots.dtype))(slots)


def _adamw_math(w, g, m, v):
    m2 = ADAM_B1 * m + (1.0 - ADAM_B1) * g
    v2 = ADAM_B2 * v + (1.0 - ADAM_B2) * (g * g)
    m_hat = m2 / (1.0 - ADAM_B1 ** ADAM_STEP)
    v_hat = v2 / (1.0 - ADAM_B2 ** ADAM_STEP)
    return -ADAM_LR * (m_hat / (jnp.sqrt(v_hat) + ADAM_EPS) + ADAM_WD * w), m2, v2


ADAM_BLOCK_BYTES = 2 * 1024 * 1024


def _adamw_big(w, g, m, v, *, name):
    r, c = w.shape

    def body(w_ref, g_ref, m_ref, v_ref, d_ref, m2_ref, v2_ref):
        d_ref[...], m2_ref[...], v2_ref[...] = _adamw_math(w_ref[...], g_ref[...], m_ref[...], v_ref[...])

    tr = max(t for t in range(8, r + 1, 8) if r % t == 0 and t * c * 4 <= ADAM_BLOCK_BYTES)
    steps, spec = r // tr, pl.BlockSpec((tr, c), lambda i: (i, 0))
    return pl.pallas_call(body, name=name, grid=(steps,), in_specs=[spec] * 4, out_specs=[spec] * 3,
                          out_shape=[jax.ShapeDtypeStruct((r, c), F32)] * 3)(w, g, m, v)


def _adamw_small(ws, gs, ms, vs):
    n = len(ws)

    def body(*refs):
        for i in range(n):
            w_ref, g_ref, m_ref, v_ref = (refs[j * n + i] for j in range(4))
            d_ref, m2_ref, v2_ref = (refs[(4 + j) * n + i] for j in range(3))
            d_ref[...], m2_ref[...], v2_ref[...] = _adamw_math(w_ref[...], g_ref[...], m_ref[...], v_ref[...])

    vm = pl.BlockSpec(memory_space=pltpu.VMEM)
    shapes = [jax.ShapeDtypeStruct(w.shape, F32) for w in ws]
    outs = pl.pallas_call(body, name="adamw_small", in_specs=[vm] * (4 * n), out_specs=[vm] * (3 * n), out_shape=shapes * 3)(
        *ws, *gs, *ms, *vs)
    return outs[:n], outs[n:2 * n], outs[2 * n:]


_SMALL_ARG = {"conv_w": "ssd_conv_w", "conv_b": "ssd_conv_b", "dt_bias": "ssd_dt_bias", "A_log": "ssd_A_log", "D": "ssd_D",
              "ssd_norm": "ssd_norm_w", "q_norm": "mla_q_norm_w", "kv_norm": "mla_kv_norm_w", "out_norm": "mla_out_norm_w",
              "ln_mix_g": "ln_mix_g", "ln_mix_b": "ln_mix_b", "ln_ffn_g": "ln_ffn_g", "ln_ffn_b": "ln_ffn_b"}
_BIG_ARG = {"w_in": "w_in", "w_qb": "mla_w_q_b", "w_kvb": "mla_w_kv_b", "w_out": "w_out", "w_gate": "w_ffn_gate",
            "w_up": "w_ffn_up", "w_down": "w_ffn_down", "w_pg": "w_ple_gate", "w_pp": "w_ple_proj"}
_WEIGHT_ORDER = ("w_in", "ssd_conv_w", "ssd_conv_b", "ssd_dt_bias", "ssd_A_log", "ssd_D", "ssd_norm_w", "mla_q_norm_w", "mla_w_q_b",
                 "mla_kv_norm_w", "mla_w_kv_b", "mla_out_norm_w", "w_out", "ln_mix_g", "ln_mix_b", "w_ffn_gate", "w_ffn_up",
                 "w_ffn_down", "w_ple_gate", "w_ple_proj", "ln_ffn_g", "ln_ffn_b")


def _rows128(a):
    flat = a.reshape(-1)
    return jnp.pad(flat, (0, -flat.shape[0] % LANE)).reshape(-1, LANE)


def kernel(x, p, positions, w_in, ssd_conv_w, ssd_conv_b, ssd_dt_bias, ssd_A_log, ssd_D, ssd_norm_w, mla_q_norm_w, mla_w_q_b, mla_kv_norm_w, mla_w_kv_b, mla_out_norm_w, w_out, ln_mix_g, ln_mix_b, w_ffn_gate, w_ffn_up, w_ffn_down, w_ple_gate, w_ple_proj, ln_ffn_g, ln_ffn_b, loss_target, m_w_in, m_ssd_conv_w, m_ssd_conv_b, m_ssd_dt_bias, m_ssd_A_log, m_ssd_D, m_ssd_norm_w, m_mla_q_norm_w, m_mla_w_q_b, m_mla_kv_norm_w, m_mla_w_kv_b, m_mla_out_norm_w, m_w_out, m_ln_mix_g, m_ln_mix_b, m_w_ffn_gate, m_w_ffn_up, m_w_ffn_down, m_w_ple_gate, m_w_ple_proj, m_ln_ffn_g, m_ln_ffn_b, v_w_in, v_ssd_conv_w, v_ssd_conv_b, v_ssd_dt_bias, v_ssd_A_log, v_ssd_D, v_ssd_norm_w, v_mla_q_norm_w, v_mla_w_q_b, v_mla_kv_norm_w, v_mla_w_kv_b, v_mla_out_norm_w, v_w_out, v_ln_mix_g, v_ln_mix_b, v_w_ffn_gate, v_w_ffn_up, v_w_ffn_down, v_w_ple_gate, v_w_ple_proj, v_ln_ffn_g, v_ln_ffn_b):
    given = dict(locals())
    chip = 2 * lax.axis_index("x") + lax.axis_index("y")

    def local(name, prefix=""):
        a = given[prefix + _BIG_ARG[name]][0]
        return a.T if name in TRANSPOSED else a

    def updated(name, prefix=""):
        if name in ROW_MAJOR:
            _, c, r = given[prefix + _BIG_ARG[name]].shape
            return given[prefix + _BIG_ARG[name]].reshape(c // LANE, LANE, r).transpose(2, 0, 1).reshape(-1, LANE)
        return local(name, prefix)

    def global_layout(name, arr):
        if name in ROW_MAJOR:
            r, c = local(name).shape
            return arr.reshape(r, c // LANE, LANE).transpose(1, 2, 0).reshape(1, c, r)
        return (arr.T if name in TRANSPOSED else arr)[None]

    conv_bits = lax.bitcast_convert_type(ssd_conv_w[0], BF16).reshape(SSD_K, 2 * CONV_SHARD)
    w_in_all, conv_all = _gather_weights([local("w_in").astype(BF16), jnp.pad(conv_bits, ((0, BF16_ROWS - SSD_K), (0, 0)))])
    sp = {k: given[a] for k, a in _SMALL_ARG.items() if k != "conv_w"}
    sp["conv_w"] = _from_cols(lax.bitcast_convert_type(conv_all[:, :SSD_K].reshape(NCHIP, SSD_K, CONV_SHARD, 2), F32))
    gathering, tie = {}, w_in_all
    for group, names in ASYNC_GROUPS:
        gathering[group], tie = _gather_async_start(group, [local(name).astype(BF16) for name in names], tie)

    def fetch(group, after):
        return dict(zip(dict(ASYNC_GROUPS)[group], _gather_async_wait(group, *gathering[group], after)))

    reducing = {}

    def send(group, grads):
        if group == "small":
            rows = jnp.concatenate([_rows128(grads[name]) for name, _ in SMALL] + [grads["loss"]], axis=0)
            reducing[group], sent = _allreduce_small_start(jnp.pad(rows, ((0, SMALL_ROWS - rows.shape[0]), (0, 0))))
        else:
            reducing[group], sent = _reduce_async_start(group, [grads[name] for name in dict(ASYNC_GROUPS)[group]], None)
        return sent[0, 0]

    loss_row, grad_x, dw_in, g = _local_step(x[0], p[0, 0], positions[0], loss_target[0], w_in_all, fetch, send, sp, tie[0, 0])

    reducing["in"], tie = _reduce_async_start("in", [dw_in], grad_x)
    gbig = {}
    for group, names in reversed(ASYNC_GROUPS):
        arrived = _reduce_async_wait(group, *reducing[group], tie)
        gbig.update(zip(names, _reduce_finish(group, arrived, [local(name).shape for name in names])))
    small_sum = _allreduce_small_wait(*reducing.pop("small"), tie)
    gsmall, row = {}, 0
    for name, size in SMALL:
        nrow = -(-size // LANE)
        gsmall[name] = small_sum[row:row + nrow].reshape(-1)[:size]
        row += nrow
    loss = small_sum[row, 0]

    grads = {_BIG_ARG[name]: global_layout(name, arr) for name, arr in gbig.items()}
    for name, _ in SMALL:
        if name == "conv_w":
            full_g = gsmall[name].reshape(SSD_K, SSD_XBC)
            grads["ssd_conv_w"] = lax.dynamic_slice(full_g, (0, chip * CONV_SHARD), (SSD_K, CONV_SHARD))[None]
        else:
            grads[_SMALL_ARG[name]] = gsmall[name].reshape(given[_SMALL_ARG[name]].shape)

    delta, new_m, new_v = {}, {}, {}

    def update_matrix(name, grad):
        a = _BIG_ARG[name]
        d, m2, v2 = _adamw_big(updated(name), grad, updated(name, "m_"), updated(name, "v_"), name="adamw_" + a)
        delta[a], new_m[a], new_v[a] = (global_layout(name, t) for t in (d, m2, v2))
        return d

    for name, grad in gbig.items():
        last = update_matrix(name, grad)
    g_in = _reduce_finish("in", _reduce_async_wait("in", *reducing["in"], last), [updated("w_in").shape])[0]
    grads["w_in"] = global_layout("w_in", g_in)
    update_matrix("w_in", g_in)
    small_names = [_SMALL_ARG[name] for name, _ in SMALL]
    two_d = lambda t: t.reshape(t.shape[-2], t.shape[-1])
    ds, ms, vs = _adamw_small([two_d(given[a]) for a in small_names], [two_d(grads[a]) for a in small_names],
                              [two_d(given["m_" + a]) for a in small_names], [two_d(given["v_" + a]) for a in small_names])
    for a, d, m2, v2 in zip(small_names, ds, ms, vs):
        delta[a], new_m[a], new_v[a] = (t.reshape(given[a].shape) for t in (d, m2, v2))

    return (loss, grad_x[None], *[grads[n] for n in _WEIGHT_ORDER], *[delta[n] for n in _WEIGHT_ORDER],
            *[new_m[n] for n in _WEIGHT_ORDER], *[new_v[n] for n in _WEIGHT_ORDER])
```

```python
import functools
import math

import jax
import jax.numpy as jnp
from jax import lax
from jax.experimental import pallas as pl
from jax.experimental.pallas import tpu as pltpu

F32 = jnp.float32
BF16 = jnp.bfloat16

S = 2048
D = 1024
PLE = 256
H = 16
SSD_P = 64
SSD_INNER = 1024
SSD_N = 128
SSD_G = 2
SSD_L = 128
SSD_NC = S // SSD_L
SSD_XBC = 1536
SSD_K = 4
Q_RANK = 384
KV_RANK = 256
NOPE = 64
ROPE = 32
VDIM = 64
D_FF = 2816
IN_WIDTH = 3248
ALPHA = 2.0 ** 0.25
EPS_RMS = 1e-6
EPS_LN = 1e-5
ATT_SCALE = 1.0 / math.sqrt(NOPE + ROPE)
LN2 = math.log(2.0)
ATT_SCALE_LOG2 = ATT_SCALE / LN2
LANE = 128
NCHIP = 4
SMALL_W = 896
SM_Q, SM_KV, SM_KR, SM_DT = 0, 384, 640, 768
NEG = -1e30

ADAM_LR = 0.001
ADAM_B1 = 0.9
ADAM_B2 = 0.999
ADAM_EPS = 1e-08
ADAM_WD = 0.01
ADAM_STEP = 10


def _sigmoid(v):
    return 1.0 / (1.0 + jnp.exp(-v))


MM_VMEM_BUDGET = 36 * 2 ** 20
MM_MAX_ACC = 2048 * 1024


def _mm_tiles(pairs, ks, m, n, out_dtype, has_add):
    def divs(v):
        return [LANE * d for d in range(v // LANE, 0, -1) if (v // LANE) % d == 0] if v % LANE == 0 else [v]

    def cost(tm, tn):
        tot = tm * tn * (jnp.dtype(out_dtype).itemsize + (4 if has_add else 0))
        for (a, b), k in zip(pairs, ks):
            tot += k * (tm * a.dtype.itemsize + tn * b.dtype.itemsize)
        return 2 * tot

    ok = [(tm * tn, tm, tn) for tm in divs(m) for tn in divs(n) if tm * tn <= MM_MAX_ACC and cost(tm, tn) <= MM_VMEM_BUDGET]
    _, tm, tn = max(ok)
    return tm, tn


def _mm(pairs, *, ta=False, tb=False, out_dtype=F32, add=None, add_scale=1.0, chunk=None, name):
    n_pairs = len(pairs)
    windows = [pr[2] if len(pr) == 3 else None for pr in pairs]
    pairs = [pr[:2] for pr in pairs]
    assert not ((ta or tb) and any(windows))
    ks = [w[2] if w else (a.shape[-2] if ta else a.shape[-1]) for (a, _), w in zip(pairs, windows)]
    a0, b0 = pairs[0]
    m = a0.shape[-1] if ta else a0.shape[-2]
    n = b0.shape[-2] if tb else b0.shape[-1]
    tm, tn = _mm_tiles(pairs, ks, m, n, out_dtype, add is not None)
    dims = (((0 if ta else 1,), (1 if tb else 0,)), ((), ()))
    nk = NCHIP if chunk else 1
    assert chunk != "sum" or out_dtype == F32
    flat = [i for i, (a, b) in enumerate(pairs) if a.ndim == 2 and b.ndim == 2]
    stacked = [i for i in range(n_pairs) if i not in flat]

    def body(*refs):
        o_ref = refs[-1]

        def products(which):
            acc = None
            for i in which:
                a = refs[2 * i][...].astype(BF16)
                b = refs[2 * i + 1][...].astype(BF16)
                part = lax.dot_general(a, b, dims, preferred_element_type=F32)
                acc = part if acc is None else acc + part
            return acc

        if chunk == "sum":
            k = pl.program_id(2)
            acc = products(stacked)

            @pl.when(k == 0)
            def _():
                first = acc + products(flat) if flat else acc
                o_ref[...] = first + add_scale * refs[2 * n_pairs][...] if add is not None else first

            @pl.when(k > 0)
            def _():
                o_ref[...] += acc
            return
        acc = products(range(n_pairs))
        if add is not None:
            acc = acc + add_scale * refs[2 * n_pairs][...]
        o_ref[...] = acc.astype(out_dtype)

    def spec(arr, shape, idx2):
        if arr.ndim == 3:
            return pl.BlockSpec((None,) + shape, lambda i, j, k: (k,) + idx2(i, j))
        return pl.BlockSpec(shape, lambda i, j, k: idx2(i, j))

    in_specs, args = [], []
    for (a, b), kdim, window in zip(pairs, ks, windows):
        ka, kb = window[:2] if window else (0, 0)
        in_specs.append(spec(a, (kdim, tm), lambda i, j: (0, i)) if ta else spec(a, (tm, kdim), lambda i, j, ka=ka: (i, ka)))
        in_specs.append(spec(b, (tn, kdim), lambda i, j: (j, 0)) if tb else spec(b, (kdim, tn), lambda i, j, kb=kb: (kb, j)))
        args += [a, b]
    if add is not None:
        in_specs.append(pl.BlockSpec((tm, tn), lambda i, j, k: (i, j)))
        args.append(add)
    if chunk == "out":
        out_spec = pl.BlockSpec((None, tm, tn), lambda i, j, k: (k, i, j))
        out_shape = jax.ShapeDtypeStruct((nk, m, n), out_dtype)
    else:
        out_spec = pl.BlockSpec((tm, tn), lambda i, j, k: (i, j))
        out_shape = jax.ShapeDtypeStruct((m, n), out_dtype)
    return pl.pallas_call(
        body, name=name, grid=(m // tm, n // tn, nk), in_specs=in_specs, out_specs=out_spec, out_shape=out_shape,
        compiler_params=pltpu.CompilerParams(dimension_semantics=("parallel", "parallel", "arbitrary")),
    )(*args)


TR = 256


def _row_spec(c):
    return pl.BlockSpec((TR, c), lambda i: (i, 0))


def _vec_spec(c):
    return pl.BlockSpec((1, c), lambda i: (0, 0))


def _acc_rows(ref, val):
    @pl.when(pl.program_id(0) == 0)
    def _():
        ref[...] = jnp.zeros_like(ref)
    ref[...] += val


def _rms_fwd(u, w, *, name):
    c = u.shape[1]

    def body(u_ref, w_ref, o_ref):
        v = u_ref[...]
        r = lax.rsqrt(jnp.mean(v * v, axis=-1, keepdims=True) + EPS_RMS)
        o_ref[...] = (v * r * w_ref[...]).astype(BF16)

    return pl.pallas_call(body, name=name, grid=(S // TR,), in_specs=[_row_spec(c), _vec_spec(c)], out_specs=_row_spec(c),
                          out_shape=jax.ShapeDtypeStruct((S, c), BF16))(u, w)


def _rms_bwd(u, w, dy, *, name):
    c = u.shape[1]

    def body(u_ref, w_ref, dy_ref, du_ref, dw_ref):
        v = u_ref[...]
        g = dy_ref[...].astype(F32)
        r = lax.rsqrt(jnp.mean(v * v, axis=-1, keepdims=True) + EPS_RMS)
        gw = g * w_ref[...]
        du_ref[...] = r * gw - v * (r * r * r * jnp.mean(gw * v, axis=-1, keepdims=True))
        _acc_rows(dw_ref, jnp.sum(g * v * r, axis=0, keepdims=True))

    return pl.pallas_call(body, name=name, grid=(S // TR,), in_specs=[_row_spec(c), _vec_spec(c), _row_spec(c)],
                          out_specs=[_row_spec(c), _vec_spec(c)],
                          out_shape=[jax.ShapeDtypeStruct((S, c), F32), jax.ShapeDtypeStruct((1, c), F32)])(u, w, dy)


def _gate_norm_fwd(y, z, w):
    def body(y_ref, z_ref, w_ref, o_ref):
        zz = z_ref[...]
        v = y_ref[...] * (zz * _sigmoid(zz))
        r = lax.rsqrt(jnp.mean(v * v, axis=-1, keepdims=True) + EPS_RMS)
        o_ref[...] = (v * r * w_ref[...]).astype(BF16)

    c = SSD_INNER
    return pl.pallas_call(body, name="ssd_gate_norm_fwd", grid=(S // TR,), in_specs=[_row_spec(c), _row_spec(c), _vec_spec(c)],
                          out_specs=_row_spec(c), out_shape=jax.ShapeDtypeStruct((S, c), BF16))(y, z, w)


def _gate_norm_bwd(y, z, w, dout):
    def body(y_ref, z_ref, w_ref, g_ref, dy_ref, dz_ref, dw_ref):
        yy = y_ref[...]
        zz = z_ref[...]
        sg = _sigmoid(zz)
        sz = zz * sg
        v = yy * sz
        g = g_ref[...]
        r = lax.rsqrt(jnp.mean(v * v, axis=-1, keepdims=True) + EPS_RMS)
        gw = g * w_ref[...]
        dv = r * gw - v * (r * r * r * jnp.mean(gw * v, axis=-1, keepdims=True))
        dy_ref[...] = dv * sz
        dz_ref[...] = (dv * yy * (sg * (1.0 + zz * (1.0 - sg)))).astype(BF16)
        _acc_rows(dw_ref, jnp.sum(g * v * r, axis=0, keepdims=True))

    c = SSD_INNER
    return pl.pallas_call(body, name="ssd_gate_norm_bwd", grid=(S // TR,),
                          in_specs=[_row_spec(c), _row_spec(c), _vec_spec(c), _row_spec(c)],
                          out_specs=[_row_spec(c), _row_spec(c), _vec_spec(c)],
                          out_shape=[jax.ShapeDtypeStruct((S, c), F32), jax.ShapeDtypeStruct((S, c), BF16),
                                     jax.ShapeDtypeStruct((1, c), F32)])(y, z, w, dout)


def _ln_fwd(xr, mix, g, b):
    def body(x_ref, m_ref, g_ref, b_ref, o_ref, ob_ref):
        pre = ALPHA * x_ref[...] + m_ref[...]
        mu = jnp.mean(pre, axis=-1, keepdims=True)
        d = pre - mu
        rs = lax.rsqrt(jnp.mean(d * d, axis=-1, keepdims=True) + EPS_LN)
        h = d * rs * g_ref[...] + b_ref[...]
        o_ref[...] = h
        ob_ref[...] = h.astype(BF16)

    return pl.pallas_call(body, name="ln_mix_fwd", grid=(S // TR,), in_specs=[_row_spec(D), _row_spec(D), _vec_spec(D), _vec_spec(D)],
                          out_specs=[_row_spec(D)] * 2,
                          out_shape=[jax.ShapeDtypeStruct((S, D), F32), jax.ShapeDtypeStruct((S, D), BF16)])(xr, mix, g, b)


def _ln_bwd(xr, mix, g, dh):
    def body(x_ref, m_ref, g_ref, dh_ref, dpre_ref, dpreb_ref, dg_ref, db_ref):
        pre = ALPHA * x_ref[...] + m_ref[...]
        mu = jnp.mean(pre, axis=-1, keepdims=True)
        d = pre - mu
        rs = lax.rsqrt(jnp.mean(d * d, axis=-1, keepdims=True) + EPS_LN)
        xh = d * rs
        dy = dh_ref[...]
        gy = dy * g_ref[...]
        dpre = rs * (gy - jnp.mean(gy, axis=-1, keepdims=True) - xh * jnp.mean(gy * xh, axis=-1, keepdims=True))
        dpre_ref[...] = dpre
        dpreb_ref[...] = dpre.astype(BF16)
        _acc_rows(dg_ref, jnp.sum(dy * xh, axis=0, keepdims=True))
        _acc_rows(db_ref, jnp.sum(dy, axis=0, keepdims=True))

    return pl.pallas_call(body, name="ln_mix_bwd", grid=(S // TR,),
                          in_specs=[_row_spec(D), _row_spec(D), _vec_spec(D), _row_spec(D)],
                          out_specs=[_row_spec(D), _row_spec(D), _vec_spec(D), _vec_spec(D)],
                          out_shape=[jax.ShapeDtypeStruct((S, D), F32), jax.ShapeDtypeStruct((S, D), BF16),
                                     jax.ShapeDtypeStruct((1, D), F32), jax.ShapeDtypeStruct((1, D), F32)])(xr, mix, g, dh)


FF_CHUNK = D_FF // NCHIP


FF_ROWS = 1024


def _ff_act_spec():
    return pl.BlockSpec((None, FF_ROWS, FF_CHUNK), lambda i, k: (k, i, 0))


def _ff_w_spec():
    return pl.BlockSpec((None, FF_CHUNK, D), lambda i, k: (k, 0, 0))


def _ffn_hidden_fwd(h, w_gate_t, w_up_t):
    def body(h_ref, wg_ref, wu_ref, g_ref, u_ref, a_ref):
        hh = h_ref[...]
        g = _dot(hh, wg_ref[...], ((1,), (1,)))
        u = _dot(hh, wu_ref[...], ((1,), (1,)))
        g_ref[...] = g.astype(BF16)
        u_ref[...] = u.astype(BF16)
        a_ref[...] = (g * _sigmoid(g) * u).astype(BF16)

    return pl.pallas_call(
        body, name="ffn_hidden_fwd", grid=(S // FF_ROWS, NCHIP),
        in_specs=[pl.BlockSpec((FF_ROWS, D), lambda i, k: (i, 0)), _ff_w_spec(), _ff_w_spec()], out_specs=[_ff_act_spec()] * 3,
        out_shape=[jax.ShapeDtypeStruct((NCHIP, S, FF_CHUNK), BF16)] * 3,
        compiler_params=pltpu.CompilerParams(dimension_semantics=("parallel", "parallel")),
    )(h, w_gate_t, w_up_t)


def _ffn_hidden_bwd(dout, w_down, gate, up):
    def body(d_ref, wd_ref, g_ref, u_ref, dg_ref, du_ref):
        d = _dot(d_ref[...], wd_ref[...], ((1,), (1,)))
        g = g_ref[...].astype(F32)
        sg = _sigmoid(g)
        dg_ref[...] = (d * u_ref[...].astype(F32) * (sg * (1.0 + g * (1.0 - sg)))).astype(BF16)
        du_ref[...] = (d * g * sg).astype(BF16)

    return pl.pallas_call(
        body, name="ffn_hidden_bwd", grid=(S // FF_ROWS, NCHIP),
        in_specs=[pl.BlockSpec((FF_ROWS, D), lambda i, k: (i, 0)), _ff_w_spec(), _ff_act_spec(), _ff_act_spec()],
        out_specs=[_ff_act_spec()] * 2, out_shape=[jax.ShapeDtypeStruct((NCHIP, S, FF_CHUNK), BF16)] * 2,
        compiler_params=pltpu.CompilerParams(dimension_semantics=("parallel", "parallel")),
    )(dout, w_down, gate, up)


def _final_fwd_bwd(h1, ffn, h1b, pb, w_pg, w_pp, target, g2, b2):
    def body(h_ref, f_ref, hb_ref, pb_ref, wpg_ref, wpp_ref, t_ref, g_ref, b_ref,
             dpre_ref, dpreb_ref, dpg_ref, dpp_ref, dg_ref, db_ref, loss_ref):
        sg = _sigmoid(jnp.dot(hb_ref[...], wpg_ref[...], preferred_element_type=F32))
        ppv = jnp.dot(pb_ref[...], wpp_ref[...], preferred_element_type=F32)
        pre = ALPHA * h_ref[...] + f_ref[...] + sg * ppv
        mu = jnp.mean(pre, axis=-1, keepdims=True)
        d = pre - mu
        rs = lax.rsqrt(jnp.mean(d * d, axis=-1, keepdims=True) + EPS_LN)
        xh = d * rs
        err = xh * g_ref[...] + b_ref[...] - t_ref[...]
        dy = err * (1.0 / D)
        gy = dy * g_ref[...]
        dpre = rs * (gy - jnp.mean(gy, axis=-1, keepdims=True) - xh * jnp.mean(gy * xh, axis=-1, keepdims=True))
        dpre_ref[...] = dpre
        dpreb_ref[...] = dpre.astype(BF16)
        dpg_ref[...] = (dpre * ppv * sg * (1.0 - sg)).astype(BF16)
        dpp_ref[...] = (dpre * sg).astype(BF16)
        _acc_rows(dg_ref, jnp.sum(dy * xh, axis=0, keepdims=True))
        _acc_rows(db_ref, jnp.sum(dy, axis=0, keepdims=True))
        _acc_rows(loss_ref, 0.5 * jnp.sum(jnp.mean(err * err, axis=-1, keepdims=True), axis=0, keepdims=True) * jnp.ones((1, LANE), F32))

    return pl.pallas_call(
        body, name="final_ln_loss", grid=(S // TR,),
        in_specs=[_row_spec(D)] * 3 + [_row_spec(pb.shape[1]), _whole(w_pg), _whole(w_pp), _row_spec(D)] + [_vec_spec(D)] * 2,
        out_specs=[_row_spec(D)] * 4 + [_vec_spec(D), _vec_spec(D), _vec_spec(LANE)],
        out_shape=[jax.ShapeDtypeStruct((S, D), F32)] + [jax.ShapeDtypeStruct((S, D), BF16)] * 3 + [
                   jax.ShapeDtypeStruct((1, D), F32), jax.ShapeDtypeStruct((1, D), F32), jax.ShapeDtypeStruct((1, LANE), F32)],
    )(h1, ffn, h1b, pb, w_pg, w_pp, target, g2, b2)


def _rot(u, cos_t, sin_t, lane):
    partner = jnp.where(lane < NOPE + ROPE // 2, pltpu.roll(u, LANE - ROPE // 2, 1), pltpu.roll(u, ROPE // 2, 1))
    return u * cos_t + partner * sin_t


def _rms(v, w):
    r = lax.rsqrt(jnp.mean(v * v, axis=-1, keepdims=True) + EPS_RMS)
    return v * r * w, r


def _rms_grad(v, r, w, g):
    gw = g * w
    return r * gw - v * (r * r * r * jnp.mean(gw * v, axis=-1, keepdims=True)), jnp.sum(g * v * r, axis=0, keepdims=True)


def _whole(arr):
    return pl.BlockSpec(arr.shape, lambda i: (0,) * arr.ndim)


def _qkv_fwd(small, w_q, w_k, w_v, q_norm, kv_norm, cos_t, sin_t):
    def body(sm_ref, wq_ref, wk_ref, wv_ref, qw_ref, kw_ref, c_ref, s_ref, qn_ref, kvn_ref, q_ref, k_ref, kt_ref, v_ref):
        lane = lax.broadcasted_iota(jnp.int32, (TR, LANE), 1)
        c, s = c_ref[...], s_ref[...]
        qn = _rms(sm_ref[:, SM_Q:SM_Q + Q_RANK], qw_ref[...])[0].astype(BF16)
        kvn = _rms(sm_ref[:, SM_KV:SM_KV + KV_RANK], kw_ref[...])[0].astype(BF16)
        qn_ref[...] = qn
        kvn_ref[...] = kvn
        kr = _rot(pltpu.roll(sm_ref[:, SM_KR:SM_KR + LANE], NOPE, 1), c, s, lane)
        for h in range(H):
            tile = slice(h * LANE, (h + 1) * LANE)
            q_ref[:, tile] = _rot(_dot(qn, wq_ref[:, tile], ((1,), (0,))), c, s, lane).astype(BF16)
            kt = _dot(kvn, wk_ref[:, tile], ((1,), (0,))) + kr
            k_ref[:, tile] = kt.astype(BF16)
            kt_ref[tile, :] = kt.T.astype(BF16)
        v_ref[...] = _dot(kvn, wv_ref[...], ((1,), (0,))).astype(BF16)

    w = H * LANE
    return pl.pallas_call(
        body, name="qkv_fwd", grid=(S // TR,),
        in_specs=[_row_spec(SMALL_W), _whole(w_q), _whole(w_k), _whole(w_v), _vec_spec(Q_RANK), _vec_spec(KV_RANK), _row_spec(LANE), _row_spec(LANE)],
        out_specs=[_row_spec(Q_RANK), _row_spec(KV_RANK), _row_spec(w), _row_spec(w), pl.BlockSpec((w, TR), lambda i: (0, i)),
                   _row_spec(H * VDIM)],
        out_shape=[jax.ShapeDtypeStruct((S, Q_RANK), BF16), jax.ShapeDtypeStruct((S, KV_RANK), BF16), jax.ShapeDtypeStruct((S, w), BF16),
                   jax.ShapeDtypeStruct((S, w), BF16), jax.ShapeDtypeStruct((w, S), BF16), jax.ShapeDtypeStruct((S, H * VDIM), BF16)],
    )(small, w_q, w_k, w_v, q_norm, kv_norm, cos_t, sin_t)


def _qkv_bwd(dqt, dk, dv, small, w_q, w_k, w_v, q_norm, kv_norm, cos_t, sin_t):
    def body(dq_ref, dk_ref, dv_ref, sm_ref, wq_ref, wk_ref, wv_ref, qw_ref, kw_ref, c_ref, s_ref,
             ds_ref, dql_ref, dkb_ref, dqw_ref, dkw_ref):
        lane = lax.broadcasted_iota(jnp.int32, (TR, LANE), 1)
        c, s = c_ref[...], -s_ref[...]
        dqn = jnp.zeros((TR, Q_RANK), F32)
        dkvn = _dot(dv_ref[...], wv_ref[...], ((1,), (1,)))
        dkr = jnp.zeros((TR, LANE), F32)
        for h in range(H):
            tile = slice(h * LANE, (h + 1) * LANE)
            dql = _rot(dq_ref[tile, :].T, c, s, lane).astype(BF16)
            dql_ref[:, tile] = dql
            dqn = dqn + _dot(dql, wq_ref[:, tile], ((1,), (1,)))
            dkt = dk_ref[:, tile]
            dkb_ref[:, tile] = dkt.astype(BF16)
            dkvn = dkvn + _dot(dkt, wk_ref[:, tile], ((1,), (1,)))
            dkr = dkr + dkt
        dkr = jnp.where((lane >= NOPE) & (lane < NOPE + ROPE), dkr, 0.0)
        q_c, kv_c = sm_ref[:, SM_Q:SM_Q + Q_RANK], sm_ref[:, SM_KV:SM_KV + KV_RANK]
        dq_c, dqw = _rms_grad(q_c, _rms(q_c, qw_ref[...])[1], qw_ref[...], dqn)
        dkv_c, dkw = _rms_grad(kv_c, _rms(kv_c, kw_ref[...])[1], kw_ref[...], dkvn)
        ds_ref[:, SM_Q:SM_Q + Q_RANK] = dq_c.astype(BF16)
        ds_ref[:, SM_KV:SM_KV + KV_RANK] = dkv_c.astype(BF16)
        ds_ref[:, SM_KR:SM_KR + LANE] = pltpu.roll(_rot(dkr, c, s, lane), LANE - NOPE, 1).astype(BF16)
        _acc_rows(dqw_ref, dqw)
        _acc_rows(dkw_ref, dkw)

    w = H * LANE
    return pl.pallas_call(
        body, name="qkv_bwd", grid=(S // TR,),
        in_specs=[pl.BlockSpec((w, TR), lambda i: (0, i)), _row_spec(w), _row_spec(H * VDIM), _row_spec(SMALL_W), _whole(w_q), _whole(w_k),
                  _whole(w_v), _vec_spec(Q_RANK), _vec_spec(KV_RANK), _row_spec(LANE), _row_spec(LANE)],
        out_specs=[_row_spec(SM_DT), _row_spec(w), _row_spec(w), _vec_spec(Q_RANK), _vec_spec(KV_RANK)],
        out_shape=[jax.ShapeDtypeStruct((S, SM_DT), BF16), jax.ShapeDtypeStruct((S, w), BF16), jax.ShapeDtypeStruct((S, w), BF16),
                   jax.ShapeDtypeStruct((1, Q_RANK), F32), jax.ShapeDtypeStruct((1, KV_RANK), F32)],
    )(dqt, dk, dv, small, w_q, w_k, w_v, q_norm, kv_norm, cos_t, sin_t)


CB = 256


def _shift_down(u, k, row):
    if k == 0:
        return u
    return jnp.where(row >= k, pltpu.roll(u, k, 0), 0.0)


def _shift_up(u, k, row):
    if k == 0:
        return u
    return jnp.where(row < S - k, pltpu.roll(u, S - k, 0), 0.0)


def _conv_fwd(u, w, b):
    def body(u_ref, w_ref, b_ref, o_ref):
        row = lax.broadcasted_iota(jnp.int32, (S, CB), 0)
        uu = u_ref[...]
        acc = b_ref[...] + w_ref[SSD_K - 1:SSD_K, :] * uu
        for k in range(SSD_K - 1):
            acc = acc + w_ref[k:k + 1, :] * _shift_down(uu, SSD_K - 1 - k, row)
        o_ref[...] = acc * _sigmoid(acc)

    c = u.shape[1]
    return pl.pallas_call(
        body, name="conv_fwd", grid=(c // CB,),
        in_specs=[pl.BlockSpec((S, CB), lambda j: (0, j)), pl.BlockSpec((SSD_K, CB), lambda j: (0, j)), pl.BlockSpec((1, CB), lambda j: (0, j))],
        out_specs=pl.BlockSpec((S, CB), lambda j: (0, j)), out_shape=jax.ShapeDtypeStruct((S, c), F32),
    )(u, w, b)


def _conv_bwd(u, w, b, dact):
    def body(u_ref, w_ref, b_ref, d_ref, du_ref, dw_ref, db_ref):
        row = lax.broadcasted_iota(jnp.int32, (S, CB), 0)
        uu = u_ref[...]
        sh = [_shift_down(uu, SSD_K - 1 - k, row) for k in range(SSD_K)]
        acc = b_ref[...]
        for k in range(SSD_K):
            acc = acc + w_ref[k:k + 1, :] * sh[k]
        sg = _sigmoid(acc)
        dacc = d_ref[...] * (sg * (1.0 + acc * (1.0 - sg)))
        du = w_ref[SSD_K - 1:SSD_K, :] * dacc
        for k in range(SSD_K - 1):
            du = du + w_ref[k:k + 1, :] * _shift_up(dacc, SSD_K - 1 - k, row)
        du_ref[...] = du.astype(BF16)
        for k in range(SSD_K):
            dw_ref[k:k + 1, :] = jnp.sum(dacc * sh[k], axis=0, keepdims=True)
        db_ref[...] = jnp.sum(dacc, axis=0, keepdims=True)

    c = u.shape[1]
    col = lambda r: pl.BlockSpec((r, CB), lambda j: (0, j))
    return pl.pallas_call(
        body, name="conv_bwd", grid=(c // CB,), in_specs=[col(S), col(SSD_K), col(1), col(S)], out_specs=[col(S), col(SSD_K), col(1)],
        out_shape=[jax.ShapeDtypeStruct((S, c), BF16), jax.ShapeDtypeStruct((SSD_K, c), F32), jax.ShapeDtypeStruct((1, c), F32)],
    )(u, w, b, dact)


NPAIR = H // 2
PAIRS_PER_GROUP = NPAIR // SSD_G


def _softplus(v):
    return jnp.maximum(v, 0.0) + jnp.log(1.0 + jnp.exp(-jnp.abs(v)))


def _dot(a, b, dims):
    return lax.dot_general(a.astype(BF16), b.astype(BF16), (dims, ((), ())), preferred_element_type=F32)


def _dot2(a, sel):
    hi = a.astype(BF16)
    lo = (a - hi.astype(F32)).astype(BF16)
    dims = (((1,), (0,)), ((), ()))
    return lax.dot_general(hi, sel, dims, preferred_element_type=F32) + lax.dot_general(lo, sel, dims, preferred_element_type=F32)


def _dot3(a, b, dims, split_lhs):
    v = a if split_lhs else b
    v1 = v.astype(BF16)
    r1 = v - v1.astype(F32)
    v2 = r1.astype(BF16)
    v3 = (r1 - v2.astype(F32)).astype(BF16)
    acc = None
    for part in (v1, v2, v3):
        lhs, rhs = (part, b) if split_lhs else (a, part)
        t = lax.dot_general(lhs, rhs, (dims, ((), ())), preferred_element_type=F32)
        acc = t if acc is None else acc + t
    return acc


def _ssd_chunk_common(dt_ref, dtT_ref, prow_ref, pcol_ref):
    prow = prow_ref[...]
    pcol = pcol_ref[...]
    ri = lax.broadcasted_iota(jnp.int32, (SSD_L, SSD_L), 0)
    ci = lax.broadcasted_iota(jnp.int32, (SSD_L, SSD_L), 1)
    causal = ri >= ci
    pre_c = dt_ref[...] + prow[0:1, :]
    dtc = _softplus(pre_c)
    a_row = -jnp.exp(prow[1:2, :])
    cs_col = _dot3(causal.astype(BF16), dtc * a_row, ((1,), (0,)), False)
    dtr = _softplus(dtT_ref[...] + pcol[:, 0:1])
    a_col = -jnp.exp(pcol[:, 1:2])
    cs_row = _dot3(dtr * a_col, (ri <= ci).astype(BF16), ((1,), (0,)), True)
    return prow, causal, pre_c, dtc, a_row, cs_col, cs_row


def _ssd_fwd(act, small, dtT, prow, pcol):
    def body(x_ref, b_ref, c_ref, dt_ref, dtT_ref, prow_ref, pcol_ref, y_ref, st_ref, state):
        @pl.when(pl.program_id(0) == 0)
        def _():
            state[...] = jnp.zeros_like(state)

        prow, causal, _, dtc, _, cs_col, cs_row = _ssd_chunk_common(dt_ref, dtT_ref, prow_ref, pcol_ref)
        lo = lax.broadcasted_iota(jnp.int32, (SSD_L, LANE), 1) < SSD_P
        lo1 = lo[0:1, :]
        for g in range(SSD_G):
            bm = b_ref[:, g * SSD_N:(g + 1) * SSD_N]
            cm = c_ref[:, g * SSD_N:(g + 1) * SSD_N]
            cb = _dot(cm, bm, ((1,), (1,)))
            for qq in range(PAIRS_PER_GROUP):
                q = g * PAIRS_PER_GROUP + qq
                ha, hb = 2 * q, 2 * q + 1
                csa, csb = cs_col[:, ha:ha + 1], cs_col[:, hb:hb + 1]
                xp = x_ref[:, q * LANE:(q + 1) * LANE]
                xx = xp * jnp.where(lo, dtc[:, ha:ha + 1], dtc[:, hb:hb + 1])
                ga = cb * jnp.exp(jnp.where(causal, csa - cs_row[ha:ha + 1, :], NEG))
                gb = cb * jnp.exp(jnp.where(causal, csb - cs_row[hb:hb + 1, :], NEG))
                y = _dot(ga, jnp.where(lo, xx, 0.0), ((1,), (0,))) + _dot(gb, jnp.where(lo, 0.0, xx), ((1,), (0,)))
                s_in = state[q]
                y = y + _dot(cm, s_in, ((1,), (0,))) * jnp.where(lo, jnp.exp(csa), jnp.exp(csb))
                y = y + jnp.where(lo1, prow[2:3, ha:ha + 1], prow[2:3, hb:hb + 1]) * xp
                y_ref[:, q * LANE:(q + 1) * LANE] = y
                la, lb = csa[SSD_L - 1:SSD_L, :], csb[SSD_L - 1:SSD_L, :]
                decay = jnp.where(lo, jnp.exp(la - csa), jnp.exp(lb - csb))
                st_ref[q] = s_in
                state[q] = s_in * jnp.where(lo1, jnp.exp(la), jnp.exp(lb)) + _dot(bm, xx * decay, ((0,), (0,)))

    L = SSD_L
    return pl.pallas_call(
        body, name="ssd_fwd", grid=(SSD_NC,),
        in_specs=[pl.BlockSpec((L, SSD_INNER), lambda c: (c, 0)),
                  pl.BlockSpec((L, SSD_G * SSD_N), lambda c: (c, SSD_INNER // (SSD_G * SSD_N))),
                  pl.BlockSpec((L, SSD_G * SSD_N), lambda c: (c, SSD_INNER // (SSD_G * SSD_N) + 1)),
                  pl.BlockSpec((L, LANE), lambda c: (c, SM_DT // LANE)),
                  pl.BlockSpec((LANE, L), lambda c: (0, c)),
                  pl.BlockSpec((8, LANE), lambda c: (0, 0)), pl.BlockSpec((LANE, 8), lambda c: (0, 0))],
        out_specs=[pl.BlockSpec((L, SSD_INNER), lambda c: (c, 0)),
                   pl.BlockSpec((None, NPAIR, SSD_N, LANE), lambda c: (c, 0, 0, 0))],
        out_shape=[jax.ShapeDtypeStruct((S, SSD_INNER), F32), jax.ShapeDtypeStruct((SSD_NC, NPAIR, SSD_N, LANE), F32)],
        scratch_shapes=[pltpu.VMEM((NPAIR, SSD_N, LANE), F32)],
        compiler_params=pltpu.CompilerParams(dimension_semantics=("arbitrary",)),
    )(act, act, act, small, dtT, prow, pcol)


def _ssd_bwd(act, small, dtT, prow, pcol, states, dy):
    def body(x_ref, b_ref, c_ref, dt_ref, dtT_ref, prow_ref, pcol_ref, st_ref, dy_ref,
             dx_ref, ddt_ref, dp_ref, dstate):
        @pl.when(pl.program_id(0) == 0)
        def _():
            dstate[...] = jnp.zeros_like(dstate)
            dp_ref[...] = jnp.zeros_like(dp_ref)

        prow, causal, pre_c, dtc, a_row, cs_col, cs_row = _ssd_chunk_common(dt_ref, dtT_ref, prow_ref, pcol_ref)
        lane = lax.broadcasted_iota(jnp.int32, (SSD_L, LANE), 1)
        sub = lax.broadcasted_iota(jnp.int32, (LANE, SSD_L), 0)
        rowi = lax.broadcasted_iota(jnp.int32, (SSD_L, 1), 0)
        pick_p = lax.broadcasted_iota(jnp.int32, (LANE, LANE), 0)
        pick_l = lax.broadcasted_iota(jnp.int32, (LANE, LANE), 1)
        lo = lane < SSD_P
        lo1 = lo[0:1, :]
        dcs_c = jnp.zeros((SSD_L, LANE), F32)
        dcs_r = jnp.zeros((LANE, SSD_L), F32)
        ddt_x = jnp.zeros((SSD_L, LANE), F32)
        dd_row = jnp.zeros((1, LANE), F32)
        for g in range(SSD_G):
            bm = b_ref[:, g * SSD_N:(g + 1) * SSD_N]
            cm = c_ref[:, g * SSD_N:(g + 1) * SSD_N]
            cb = _dot(cm, bm, ((1,), (1,)))
            dcb = jnp.zeros((SSD_L, SSD_L), F32)
            dbm = jnp.zeros((SSD_L, SSD_N), F32)
            dcm = jnp.zeros((SSD_L, SSD_N), F32)
            for qq in range(PAIRS_PER_GROUP):
                q = g * PAIRS_PER_GROUP + qq
                ha, hb = 2 * q, 2 * q + 1
                csa, csb = cs_col[:, ha:ha + 1], cs_col[:, hb:hb + 1]
                xp = x_ref[:, q * LANE:(q + 1) * LANE]
                dtp = jnp.where(lo, dtc[:, ha:ha + 1], dtc[:, hb:hb + 1])
                xx = xp * dtp
                lma = jnp.exp(jnp.where(causal, csa - cs_row[ha:ha + 1, :], NEG))
                lmb = jnp.exp(jnp.where(causal, csb - cs_row[hb:hb + 1, :], NEG))
                ga, gb = cb * lma, cb * lmb
                dyp = dy_ref[:, q * LANE:(q + 1) * LANE]
                dya, dyb = jnp.where(lo, dyp, 0.0), jnp.where(lo, 0.0, dyp)
                s_in = st_ref[q]
                ds_out = dstate[q]
                la, lb = csa[SSD_L - 1:SSD_L, :], csb[SSD_L - 1:SSD_L, :]
                ecs = jnp.where(lo, jnp.exp(csa), jnp.exp(csb))
                decay = jnp.where(lo, jnp.exp(la - csa), jnp.exp(lb - csb))
                cd = jnp.where(lo1, jnp.exp(la), jnp.exp(lb))
                bds = _dot(bm, ds_out, ((1,), (0,)))
                dxx = _dot(ga, dya, ((0,), (0,))) + _dot(gb, dyb, ((0,), (0,))) + bds * decay
                dga = _dot(dya, xx, ((1,), (1,)))
                dgb = _dot(dyb, xx, ((1,), (1,)))
                dsega, dsegb = dga * ga, dgb * gb
                dcb = dcb + dga * lma + dgb * lmb
                yoff = _dot(cm, s_in, ((1,), (0,))) * ecs
                dye = dyp * ecs
                dcm = dcm + _dot(dye, s_in, ((1,), (1,)))
                xd = xx * decay
                dbm = dbm + _dot(xd, ds_out, ((1,), (1,)))
                wv = xd * bds
                ends = jnp.sum(wv, axis=0, keepdims=True) + cd * jnp.sum(ds_out * s_in, axis=0, keepdims=True)
                t1 = dyp * yoff - wv + jnp.where(rowi == SSD_L - 1, ends, 0.0)
                to_pair = (((pick_p < SSD_P) & (pick_l == ha)) | ((pick_p >= SSD_P) & (pick_l == hb))).astype(BF16)
                to_a_b = jnp.concatenate([(pick_l == ha).astype(BF16), (pick_l == hb).astype(BF16)], axis=0)
                dcs_c = dcs_c + _dot2(t1, to_pair) + _dot2(jnp.concatenate([dsega, dsegb], axis=1), to_a_b)
                dcs_r = (dcs_r + jnp.where(sub == ha, jnp.sum(dsega, axis=0, keepdims=True), 0.0)
                         + jnp.where(sub == hb, jnp.sum(dsegb, axis=0, keepdims=True), 0.0))
                dstate[q] = _dot(cm, dye, ((0,), (0,))) + cd * ds_out
                dpair = jnp.where(lo1, prow[2:3, ha:ha + 1], prow[2:3, hb:hb + 1])
                dx_ref[:, q * LANE:(q + 1) * LANE] = dxx * dtp + dpair * dyp
                ddt_x = ddt_x + _dot2(dxx * xp, to_pair)
                dd_row = dd_row + jnp.sum(_dot2(dyp * xp, to_pair), axis=0, keepdims=True)
            dx_ref[:, SSD_INNER + g * SSD_N:SSD_INNER + (g + 1) * SSD_N] = dbm + _dot(dcb, cm, ((0,), (0,)))
            dx_ref[:, SSD_INNER + (SSD_G + g) * SSD_N:SSD_INNER + (SSD_G + g + 1) * SSD_N] = dcm + _dot(dcb, bm, ((1,), (0,)))
        ri = lax.broadcasted_iota(jnp.int32, (SSD_L, SSD_L), 0)
        ci = lax.broadcasted_iota(jnp.int32, (SSD_L, SSD_L), 1)
        da = _dot3((ri <= ci).astype(BF16), dcs_c, ((1,), (0,)), False)
        da = da - _dot3(dcs_r, causal.astype(BF16), ((1,), (0,)), True).T
        ddt = ddt_x + da * a_row
        ddt_raw = ddt * _sigmoid(pre_c)
        ddt_ref[...] = ddt_raw
        da_head = jnp.sum(da * dtc, axis=0, keepdims=True) * a_row
        dp_ref[0:1, :] += jnp.sum(ddt_raw, axis=0, keepdims=True)
        dp_ref[1:2, :] += da_head
        dp_ref[2:3, :] += dd_row

    L = SSD_L
    rev = SSD_NC - 1
    bc_cols = SSD_INNER // (SSD_G * SSD_N)
    return pl.pallas_call(
        body, name="ssd_bwd", grid=(SSD_NC,),
        in_specs=[pl.BlockSpec((L, SSD_INNER), lambda c: (rev - c, 0)),
                  pl.BlockSpec((L, SSD_G * SSD_N), lambda c: (rev - c, bc_cols)),
                  pl.BlockSpec((L, SSD_G * SSD_N), lambda c: (rev - c, bc_cols + 1)),
                  pl.BlockSpec((L, LANE), lambda c: (rev - c, SM_DT // LANE)),
                  pl.BlockSpec((LANE, L), lambda c: (0, rev - c)),
                  pl.BlockSpec((8, LANE), lambda c: (0, 0)), pl.BlockSpec((LANE, 8), lambda c: (0, 0)),
                  pl.BlockSpec((None, NPAIR, SSD_N, LANE), lambda c: (rev - c, 0, 0, 0)),
                  pl.BlockSpec((L, SSD_INNER), lambda c: (rev - c, 0))],
        out_specs=[pl.BlockSpec((L, SSD_XBC), lambda c: (rev - c, 0)),
                   pl.BlockSpec((L, LANE), lambda c: (rev - c, 0)),
                   pl.BlockSpec((8, LANE), lambda c: (0, 0))],
        out_shape=[jax.ShapeDtypeStruct((S, SSD_XBC), F32), jax.ShapeDtypeStruct((S, LANE), F32),
                   jax.ShapeDtypeStruct((8, LANE), F32)],
        scratch_shapes=[pltpu.VMEM((NPAIR, SSD_N, LANE), F32)],
        compiler_params=pltpu.CompilerParams(dimension_semantics=("arbitrary",)),
    )(act, act, act, small, dtT, prow, pcol, states, dy)


TQ = 256
TK = 256
FWD_TQ = 256
FWD_TK = 256


def _attn_fwd(qc, kc, v):
    TQ, TK = FWD_TQ, FWD_TK

    def body(q_ref, k_ref, v_ref, o_ref, lse_ref):
        i = pl.program_id(1)
        lo = lax.broadcasted_iota(jnp.int32, (TQ, LANE), 1) < VDIM
        lo_k = lax.broadcasted_iota(jnp.int32, (TK, LANE), 1) < VDIM
        row_minus_col = lax.broadcasted_iota(jnp.int32, (TQ, TK), 0) - lax.broadcasted_iota(jnp.int32, (TQ, TK), 1)
        qa, qb = q_ref[:, 0:LANE], q_ref[:, LANE:2 * LANE]

        def scores(kb):
            kk = k_ref[pl.ds(pl.multiple_of(kb * TK, TK), TK), :]
            return (_dot(qa, kk[:, 0:LANE], ((1,), (1,))) * ATT_SCALE_LOG2, _dot(qb, kk[:, LANE:2 * LANE], ((1,), (1,))) * ATT_SCALE_LOG2)

        def update(kb, sa, sb, stats):
            ma, la, mb, lb, acc = stats
            vv = v_ref[pl.ds(pl.multiple_of(kb * TK, TK), TK), :]
            na = jnp.maximum(ma, jnp.max(sa, axis=1, keepdims=True))
            nb = jnp.maximum(mb, jnp.max(sb, axis=1, keepdims=True))
            pa, pb = jnp.exp2(sa - na), jnp.exp2(sb - nb)
            fa, fb = jnp.exp2(ma - na), jnp.exp2(mb - nb)
            la = fa * la + jnp.sum(pa, axis=1, keepdims=True)
            lb = fb * lb + jnp.sum(pb, axis=1, keepdims=True)
            acc = (acc * jnp.where(lo, fa, fb) + _dot(pa, jnp.where(lo_k, vv, 0), ((1,), (0,)))
                   + _dot(pb, jnp.where(lo_k, 0, vv), ((1,), (0,))))
            return na, la, nb, lb, acc

        def step(kb, carry):
            sa, sb = carry[:2]
            nxt = scores(kb + 1)
            return nxt + update(kb, sa, sb, carry[2:])

        neg = jnp.full((TQ, 1), NEG, F32)
        zero = jnp.zeros((TQ, 1), F32)
        n_full = i * (TQ // TK)
        carry = lax.fori_loop(0, n_full, step, scores(0) + (neg, zero, neg, zero, jnp.zeros((TQ, LANE), F32)))
        s, stats = carry[:2], carry[2:]
        for d in range(TQ // TK):
            nxt = scores(n_full + d + 1) if d + 1 < TQ // TK else None
            sa, sb = (jnp.where(row_minus_col >= d * TK, t, NEG) for t in s)
            stats = update(n_full + d, sa, sb, stats)
            s = nxt
        ma, la, mb, lb, acc = stats
        o_ref[...] = acc / jnp.where(lo, la, lb)
        lse_ref[...] = jnp.where(lo, ma + jnp.log2(la), mb + jnp.log2(lb)) * LN2

    return pl.pallas_call(
        body, name="attn_fwd", grid=(NPAIR, S // TQ),
        in_specs=[pl.BlockSpec((TQ, 2 * LANE), lambda j, i: (i, j)), pl.BlockSpec((S, 2 * LANE), lambda j, i: (0, j)),
                  pl.BlockSpec((S, LANE), lambda j, i: (0, j))],
        out_specs=[pl.BlockSpec((TQ, LANE), lambda j, i: (i, j)), pl.BlockSpec((None, TQ, LANE), lambda j, i: (j, i, 0))],
        out_shape=[jax.ShapeDtypeStruct((S, H * VDIM), F32), jax.ShapeDtypeStruct((NPAIR, S, LANE), F32)],
        compiler_params=pltpu.CompilerParams(dimension_semantics=("parallel", "parallel")),
    )(qc, kc, v)


def _attn_rows(lse, o, do):
    def body(lse_ref, o_ref, do_ref, r_ref):
        lt = lse_ref[...].T * (1.0 / LN2)
        tt = (o_ref[...] * do_ref[...]).T
        r_ref[...] = jnp.zeros_like(r_ref)
        r_ref[0:1, :] = lt[0:1, :]
        r_ref[1:2, :] = lt[VDIM:VDIM + 1, :]
        r_ref[2:3, :] = jnp.sum(tt[0:VDIM, :], axis=0, keepdims=True)
        r_ref[3:4, :] = jnp.sum(tt[VDIM:LANE, :], axis=0, keepdims=True)

    tile = pl.BlockSpec((S, LANE), lambda j: (0, j))
    return pl.pallas_call(
        body, name="attn_rows", grid=(NPAIR,), in_specs=[pl.BlockSpec((None, S, LANE), lambda j: (j, 0, 0)), tile, tile],
        out_specs=pl.BlockSpec((None, 8, S), lambda j: (j, 0, 0)), out_shape=jax.ShapeDtypeStruct((NPAIR, 8, S), F32),
    )(lse, o, do)


def _attn_bwd(qc, kc, kct, v, do, rows):
    nq = S // TQ

    def body(q_ref, k_ref, kt_ref, v_ref, do_ref, r_ref, dqt_ref, dk_ref, dv_ref):
        kb = pl.program_id(1)

        @pl.when(kb == 0)
        def _():
            dqt_ref[...] = jnp.zeros_like(dqt_ref)

        lo = lax.broadcasted_iota(jnp.int32, (TK, LANE), 1) < VDIM
        q_minus_k = lax.broadcasted_iota(jnp.int32, (TK, TQ), 1) - lax.broadcasted_iota(jnp.int32, (TK, TQ), 0)
        vv = v_ref[...]
        kk = k_ref[...]

        def step(qi, carry):
            off = pl.multiple_of(qi * TQ, TQ)
            qq = q_ref[pl.ds(off, TQ), :]
            dd = do_ref[pl.ds(off, TQ), :].astype(BF16)
            rr = r_ref[:, pl.ds(off, TQ)]
            keep = q_minus_k >= (kb - qi) * TQ
            out = []
            for x in range(2):
                sel = lo if x == 0 else jnp.logical_not(lo)
                kx, qx = kk[:, x * LANE:(x + 1) * LANE], qq[:, x * LANE:(x + 1) * LANE]
                st = jnp.where(keep, _dot(kx, qx, ((1,), (1,))) * ATT_SCALE_LOG2, NEG)
                pt = jnp.exp2(st - rr[x:x + 1, :])
                dpt = _dot(jnp.where(sel, vv, 0), dd, ((1,), (1,)))
                dst = (pt * (dpt - rr[2 + x:3 + x, :]) * ATT_SCALE).astype(BF16)
                out.append(carry[x] + _dot(dst, qx, ((1,), (0,))))
                out.append(_dot(pt, jnp.where(sel, dd, 0), ((1,), (0,))))
                dqt_ref[x * LANE:(x + 1) * LANE, pl.ds(off, TQ)] += _dot(kt_ref[x * LANE:(x + 1) * LANE, :], dst, ((1,), (0,)))
            return out[0], out[2], carry[2] + out[1] + out[3]

        z = jnp.zeros((TK, LANE), F32)
        dka, dkb, dv = lax.fori_loop(kb, nq, step, (z, z, z))
        dk_ref[:, 0:LANE] = dka
        dk_ref[:, LANE:2 * LANE] = dkb
        dv_ref[...] = dv.astype(BF16)

    return pl.pallas_call(
        body, name="attn_bwd", grid=(NPAIR, S // TK),
        in_specs=[pl.BlockSpec((S, 2 * LANE), lambda j, k: (0, j)), pl.BlockSpec((TK, 2 * LANE), lambda j, k: (k, j)),
                  pl.BlockSpec((2 * LANE, TK), lambda j, k: (j, k)), pl.BlockSpec((TK, LANE), lambda j, k: (k, j)),
                  pl.BlockSpec((S, LANE), lambda j, k: (0, j)), pl.BlockSpec((None, 8, S), lambda j, k: (j, 0, 0))],
        out_specs=[pl.BlockSpec((2 * LANE, S), lambda j, k: (j, 0)), pl.BlockSpec((TK, 2 * LANE), lambda j, k: (k, j)),
                   pl.BlockSpec((TK, LANE), lambda j, k: (k, j))],
        out_shape=[jax.ShapeDtypeStruct((H * LANE, S), F32), jax.ShapeDtypeStruct((S, H * LANE), F32),
                   jax.ShapeDtypeStruct((S, H * VDIM), BF16)],
        compiler_params=pltpu.CompilerParams(dimension_semantics=("parallel", "arbitrary")),
    )(qc, kc, kct, v, do, rows)


_IN_Z, _IN_XBC, _IN_DT, _IN_Q, _IN_KV, _IN_KR = 0, 1024, 2560, 2576, 2960, 3216


PROJ_COLS = 512
SMALL_PAD = pl.cdiv(SMALL_W, PROJ_COLS) * PROJ_COLS


def _prep_in(w_in_t):
    dt = w_in_t.dtype
    return jnp.concatenate(
        [w_in_t[_IN_Q:_IN_KV], w_in_t[_IN_KV:_IN_KR], w_in_t[_IN_KR:IN_WIDTH], jnp.zeros((LANE - ROPE, D), dt),
         w_in_t[_IN_DT:_IN_Q], jnp.zeros((SMALL_PAD - SM_DT - H, D), dt)], axis=0)


def _proj_in(xb, w_in_t, w_small):
    nz, nx, ns = (_IN_XBC - _IN_Z) // PROJ_COLS, (_IN_DT - _IN_XBC) // PROJ_COLS, SMALL_PAD // PROJ_COLS

    dt_block, dt_at = divmod(SM_DT, PROJ_COLS)

    def body(x_ref, w_ref, ws_ref, z_ref, xbc_ref, sm_ref, dtt_ref):
        i = pl.program_id(0)

        def emit(w, o_ref):
            o_ref[...] = lax.dot_general(x_ref[...], w[...], (((1,), (1,)), ((), ())), preferred_element_type=F32)

        pl.when(i < nz)(lambda: emit(w_ref, z_ref))
        pl.when((i >= nz) & (i < nz + nx))(lambda: emit(w_ref, xbc_ref))
        pl.when(i >= nz + nx)(lambda: emit(ws_ref, sm_ref))

        @pl.when(i == nz + nx + dt_block)
        def _():
            dtt_ref[...] = sm_ref[:, dt_at:dt_at + LANE].T

    def blocks(first, count, rows):
        at = lambda i: jnp.clip(i - first, 0, count - 1)
        return pl.BlockSpec((PROJ_COLS, D), lambda i: (at(i), 0)) if rows else pl.BlockSpec((S, PROJ_COLS), lambda i: (0, at(i)))

    return pl.pallas_call(
        body, name="proj_in", grid=(nz + nx + ns,),
        in_specs=[pl.BlockSpec((S, D), lambda i: (0, 0)), blocks(0, nz + nx, True), blocks(nz + nx, ns, True)],
        out_specs=[blocks(0, nz, False), blocks(nz, nx, False), blocks(nz + nx, ns, False), pl.BlockSpec((LANE, S), lambda i: (0, 0))],
        out_shape=[jax.ShapeDtypeStruct((S, _IN_XBC - _IN_Z), F32), jax.ShapeDtypeStruct((S, _IN_DT - _IN_XBC), F32),
                   jax.ShapeDtypeStruct((S, SMALL_W), F32), jax.ShapeDtypeStruct((LANE, S), F32)],
    )(xb, w_in_t, w_small)


PART_COLS = 512


def _part_blocks(widths):
    first = [0]
    for w in widths:
        first.append(first[-1] + w // PART_COLS)

    def at(part):
        return lambda i: jnp.clip(i - first[part], 0, first[part + 1] - first[part] - 1)

    return first, at


def _mm_ta_stacked(parts, b, rows, name):
    n = b.shape[1]
    first, at = _part_blocks([a.shape[1] for a in parts])
    assert first[-1] == pl.cdiv(rows, PART_COLS)

    def body(*refs):
        b_ref, o_ref = refs[-2:]
        i = pl.program_id(0)
        for part, a_ref in enumerate(refs[:-2]):
            @pl.when((i >= first[part]) & (i < first[part + 1]))
            def _(a_ref=a_ref):
                o_ref[...] = lax.dot_general(a_ref[...], b_ref[...], (((0,), (0,)), ((), ())),
                                             preferred_element_type=F32).astype(BF16)

    return pl.pallas_call(
        body, name=name, grid=(first[-1],),
        in_specs=[pl.BlockSpec((S, PART_COLS), lambda i, at=at(part): (0, at(i))) for part in range(len(parts))]
        + [pl.BlockSpec((S, n), lambda i: (0, 0))],
        out_specs=pl.BlockSpec((PART_COLS, n), lambda i: (i, 0)), out_shape=jax.ShapeDtypeStruct((rows, n), BF16),
    )(*parts, b)


def _mm_tb_split(a, b, widths, name):
    k = a.shape[1]
    first, at = _part_blocks(widths)

    def body(a_ref, b_ref, *o_refs):
        i = pl.program_id(0)
        for part, o_ref in enumerate(o_refs):
            @pl.when((i >= first[part]) & (i < first[part + 1]))
            def _(o_ref=o_ref):
                o_ref[...] = lax.dot_general(a_ref[...], b_ref[...], (((1,), (1,)), ((), ())), preferred_element_type=F32)

    return pl.pallas_call(
        body, name=name, grid=(first[-1],),
        in_specs=[pl.BlockSpec((S, k), lambda i: (0, 0)), pl.BlockSpec((PART_COLS, k), lambda i: (i, 0))],
        out_specs=[pl.BlockSpec((S, PART_COLS), lambda i, at=at(part): (0, at(i))) for part in range(len(widths))],
        out_shape=[jax.ShapeDtypeStruct((S, w), F32) for w in widths],
    )(a, b)


def _prep_attn(w_qb, w_kvb):
    w_q = jnp.pad(w_qb.reshape(Q_RANK, H, NOPE + ROPE), ((0, 0), (0, 0), (0, LANE - NOPE - ROPE))).reshape(Q_RANK, H * LANE)
    kv3 = w_kvb.reshape(KV_RANK, H, NOPE + VDIM)
    w_k = jnp.pad(kv3[:, :, :NOPE], ((0, 0), (0, 0), (0, LANE - NOPE))).reshape(KV_RANK, H * LANE)
    w_v = kv3[:, :, NOPE:].reshape(KV_RANK, H * VDIM)
    return w_q, w_k, w_v


def _rope_tables(positions):
    inv_freq = 1.0 / (10000.0 ** (jnp.arange(0, ROPE, 2, dtype=F32) / ROPE))
    ang = positions.astype(F32).reshape(S, 1) * inv_freq
    cos, sin = jnp.cos(ang), jnp.sin(ang)
    cos_t = jnp.concatenate([jnp.ones((S, NOPE), F32), cos, cos, jnp.ones((S, LANE - NOPE - ROPE), F32)], axis=1)
    sin_t = jnp.concatenate([jnp.zeros((S, NOPE), F32), -sin, sin, jnp.zeros((S, LANE - NOPE - ROPE), F32)], axis=1)
    return cos_t, sin_t


def _local_step(x, p, positions, target, w_in, fetch, send, sp, started):
    w_in_t = w_in.reshape(IN_WIDTH, D)
    w_small = _prep_in(w_in_t)
    cos_t, sin_t = _rope_tables(positions)
    prow = jnp.zeros((8, LANE), F32).at[0, :H].set(sp["dt_bias"][0]).at[1, :H].set(sp["A_log"][0]).at[2, :H].set(sp["D"][0])
    pcol = prow.T

    xb, pb = (x + started).astype(BF16), p.astype(BF16)
    z, xbc, small, dt_t = _proj_in(xb, w_in_t, w_small)
    act = _conv_fwd(xbc, sp["conv_w"], sp["conv_b"])
    y, states = _ssd_fwd(act, small, dt_t, prow, pcol)
    y_ssd = _gate_norm_fwd(y, z, sp["ssd_norm"])
    gl = fetch("attn", y_ssd)
    w_q, w_k, w_v = _prep_attn(_from_cols(gl["w_qb"]), _from_cols(gl["w_kvb"]))
    qn, kvn, qcat, kcat, kcat_t, v = _qkv_fwd(small, w_q, w_k, w_v, sp["q_norm"], sp["kv_norm"], cos_t, sin_t)
    o, lse = _attn_fwd(qcat, kcat, v)
    y_mla = _rms_fwd(o, sp["out_norm"], name="out_norm_fwd")
    w_out = fetch("out", y_mla)["w_out"]
    w_out = w_out.reshape(2 * SSD_INNER, D)
    mix = _mm([(y_ssd, w_out, (0, 0, SSD_INNER)), (y_mla, w_out, (0, 1, SSD_INNER))], name="out_proj")
    h1, h1b = _ln_fwd(x, mix, sp["ln_mix_g"], sp["ln_mix_b"])
    gl = fetch("ffn", h1b)
    w_pg, w_pp = gl["w_pg"].reshape(D, D), _from_cols(gl["w_pp"])
    w_gate, w_up, w_down = gl["w_gate"], gl["w_up"], gl["w_down"]
    gate, up, actf = _ffn_hidden_fwd(h1b, w_gate, w_up)
    ffn = _mm([(actf, w_down)], chunk="sum", name="ffn_down")
    dpre2, dpre2b, dpg, dpp, dg2, db2, loss_row = _final_fwd_bwd(h1, ffn, h1b, pb, w_pg, w_pp, target, sp["ln_ffn_g"], sp["ln_ffn_b"])

    g = {"ln_ffn_g": dg2, "ln_ffn_b": db2}
    g["w_pp"] = _to_cols(_mm([(pb, dpp)], ta=True, out_dtype=BF16, name="d_w_ple_proj"))
    g["w_pg"] = _mm([(h1b, dpg)], ta=True, out_dtype=BF16, name="d_w_ple_gate").reshape(NCHIP, D // NCHIP, D)
    g["w_down"] = _mm([(actf, dpre2b)], ta=True, chunk="out", out_dtype=BF16, name="d_w_down")
    dgate, dup = _ffn_hidden_bwd(dpre2b, w_down, gate, up)
    g["w_gate"] = _mm([(dgate, h1b)], ta=True, chunk="out", out_dtype=BF16, name="d_w_gate")
    g["w_up"] = _mm([(dup, h1b)], ta=True, chunk="out", out_dtype=BF16, name="d_w_up")
    sent = send("ffn", {name: g.pop(name) for name in dict(ASYNC_GROUPS)["ffn"]})
    dh1 = _mm([(dgate, w_gate), (dup, w_up), (dpg, w_pg.T)], chunk="sum", add=dpre2, add_scale=ALPHA, name="d_h1")
    dpre1, dpre1b, g["ln_mix_g"], g["ln_mix_b"] = _ln_bwd(x, mix, sp["ln_mix_g"] + sent, dh1)
    dy_ssd, dy_mla = _mm_tb_split(dpre1b, w_out, (SSD_INNER, SSD_INNER), "d_y")
    dw_out = _mm_ta_stacked((y_ssd, y_mla), dpre1b, 2 * SSD_INNER, "d_w_out")
    sent = send("out", {"w_out": dw_out.reshape(NCHIP, 2 * SSD_INNER // NCHIP, D)})
    do, g["out_norm"] = _rms_bwd(o, sp["out_norm"] + sent, dy_mla, name="out_norm_bwd")
    dqt, dk, dv = _attn_bwd(qcat, kcat, kcat_t, v, do, _attn_rows(lse, o, do))
    dlatent, dqlin, dkb, g["q_norm"], g["kv_norm"] = _qkv_bwd(dqt, dk, dv, small, w_q, w_k, w_v, sp["q_norm"], sp["kv_norm"], cos_t, sin_t)
    dw_q = _mm([(qn, dqlin)], ta=True, out_dtype=BF16, name="d_w_q")
    dw_k = _mm([(kvn, dkb)], ta=True, out_dtype=BF16, name="d_w_k")
    dw_v = _mm([(kvn, dv)], ta=True, out_dtype=BF16, name="d_w_v")
    dw_qb = _to_cols(dw_q.reshape(Q_RANK, H, LANE)[:, :, :NOPE + ROPE].reshape(Q_RANK, H * (NOPE + ROPE)))
    dw_kvb = _to_cols(jnp.concatenate([dw_k.reshape(KV_RANK, H, LANE)[:, :, :NOPE], dw_v.reshape(KV_RANK, H, VDIM)],
                                       axis=2).reshape(KV_RANK, H * (NOPE + VDIM)))
    sent = send("attn", {"w_qb": dw_qb, "w_kvb": dw_kvb})
    dy, dz, g["ssd_norm"] = _gate_norm_bwd(y, z, sp["ssd_norm"] + sent, dy_ssd)
    dact, ddt, dprow = _ssd_bwd(act, small, dt_t, prow, pcol, states, dy)
    g["dt_bias"], g["A_log"], g["D"] = dprow[0:1, :H], dprow[1:2, :H], dprow[2:3, :H]
    dxbc, g["conv_w"], g["conv_b"] = _conv_bwd(xbc, sp["conv_w"], sp["conv_b"], dact)
    dsmall = jnp.concatenate([dlatent, ddt.astype(BF16)], axis=1)
    in_blocks = [(d, w_in_t, (k, first // PROJ_COLS + k, PROJ_COLS))
                 for d, first in ((dz, _IN_Z), (dxbc, _IN_XBC)) for k in range(d.shape[1] // PROJ_COLS)]
    grad_x = _mm(in_blocks + [(dsmall, w_small, (0, 0, SMALL_W))], add=dpre1, add_scale=ALPHA, name="d_x")
    sent = send("small", dict(g, loss=loss_row))
    n_small = IN_WIDTH - _IN_DT
    dsm = jnp.concatenate([(ddt[:, :H] + sent).astype(BF16), dlatent[:, :n_small - H], jnp.zeros((S, D - n_small), BF16)], axis=1)
    dw_in = _mm_ta_stacked((dz, dxbc, dsm), xb, IN_WIDTH, "d_w_in").reshape(NCHIP, IN_WIDTH // NCHIP * D // LANE, LANE)
    return loss_row, grad_x, dw_in, g


MESH = pl.DeviceIdType.MESH
BIG = (("w_in", (D, IN_WIDTH), 1), ("w_qb", (Q_RANK, H * (NOPE + ROPE)), 1), ("w_kvb", (KV_RANK, H * (NOPE + VDIM)), 1),
       ("w_out", (2 * SSD_INNER, D), 0), ("w_gate", (D, D_FF), 1), ("w_up", (D, D_FF), 1), ("w_down", (D_FF, D), 0),
       ("w_pg", (D, D), 0), ("w_pp", (PLE, D), 1))
CONV_SHARD = SSD_XBC // NCHIP
BF16_ROWS = 16


def _from_cols(stack):
    return jnp.concatenate([stack[k] for k in range(NCHIP)], axis=1)


def _to_cols(full):
    r, c4 = full.shape
    return full.reshape(r, NCHIP, c4 // NCHIP).transpose(1, 0, 2)


def _coords():
    return lax.axis_index("x"), lax.axis_index("y"), lax.axis_index("c")


def _peers():
    x, y, c = _coords()
    return 2 * x + y, c, [(1 - x, y), (x, 1 - y), (1 - x, 1 - y)], (x, y, 1 - c)


def _half_axis(shape):
    return 0 if shape[-2] % (2 * BF16_ROWS) == 0 else 1


def _half_shape(shape):
    r, c = shape[-2:]
    return (r // 2, c) if _half_axis(shape) == 0 else (r, c // 2)


def _half(core, shape):
    r, c = shape[-2:]
    if _half_axis(shape) == 0:
        return pl.ds(pl.multiple_of(core * (r // 2), BF16_ROWS), r // 2), slice(None)
    return slice(None), pl.ds(pl.multiple_of(core * (c // 2), LANE), c // 2)


def _gather_weights(shards):
    n_arr = len(shards)
    per = 2 * (NCHIP - 1)

    def body(*refs):
        ins, outs = refs[:n_arr], refs[n_arr:2 * n_arr]
        send_sems, recv_sems, local_sems = refs[2 * n_arr:]
        k, c, chips, sibling = _peers()

        def copy(idx, src, dst, to):
            return pltpu.make_async_remote_copy(src_ref=src, dst_ref=dst, send_sem=send_sems.at[idx], recv_sem=recv_sems.at[idx],
                                                device_id=to, device_id_type=MESH)

        def part(a, chip, core):
            return outs[a].at[chip, *_half(core, shards[a].shape)]

        mine = [pltpu.make_async_copy(ins[a], outs[a].at[k], local_sems.at[a]) for a in range(n_arr)]
        for cp in mine:
            cp.start()
        sends = []
        for a in range(n_arr):
            for j, (cx, cy) in enumerate(chips):
                sends.append(copy(per * a + j, ins[a].at[*_half(c, shards[a].shape)], part(a, k, c), (cx, cy, c)))
                sends[-1].start()
        for j, (cx, cy) in enumerate(chips):
            for a in range(n_arr):
                landed = part(a, 2 * cx + cy, c)
                copy(per * a + j, landed, landed, (cx, cy, c)).wait_recv()
                sends.append(copy(per * a + NCHIP - 1 + j, landed, landed, sibling))
                sends[-1].start()
        for j, (cx, cy) in enumerate(chips):
            for a in range(n_arr):
                other = part(a, 2 * cx + cy, 1 - c)
                copy(per * a + NCHIP - 1 + j, other, other, sibling).wait_recv()
        for cp in sends:
            cp.wait_send()
        for cp in mine:
            cp.wait()

    any_spec = pl.BlockSpec(memory_space=pl.ANY)
    return pl.pallas_call(
        body, name="gather_weights", in_specs=[any_spec] * n_arr, out_specs=[any_spec] * n_arr,
        out_shape=[jax.ShapeDtypeStruct((NCHIP,) + s.shape, s.dtype) for s in shards],
        scratch_shapes=[pltpu.SemaphoreType.DMA((per * n_arr,)), pltpu.SemaphoreType.DMA((per * n_arr,)),
                        pltpu.SemaphoreType.DMA((n_arr,))],
    )(*shards)


ASYNC_GROUPS = (("attn", ("w_qb", "w_kvb")), ("out", ("w_out",)), ("ffn", ("w_gate", "w_up", "w_down", "w_pg", "w_pp")))
TRANSPOSED = ("w_in", "w_gate", "w_up")
ROW_MAJOR = ("w_in",)
HBM_SPEC = pl.BlockSpec(memory_space=pltpu.HBM)
SEM_SPEC = pl.BlockSpec(memory_space=pltpu.SEMAPHORE)
IN_FLIGHT = pltpu.SideEffectType.DATAFLOW_SIDE_EFFECTING


def _in_hbm(a):
    return pltpu.with_memory_space_constraint(a, pltpu.HBM)


def _hbm_like(arrs, lead=()):
    return [pltpu.HBM(lead + a.shape, a.dtype) for a in arrs]


def _split_start(name, srcs, lands, after, n_sem, start):
    n = len(srcs)
    order = [] if after is None else [after]

    def body(*refs):
        src_refs, land_refs = refs[:n], refs[n:2 * n]
        send_sems, recv_sems = refs[2 * n + len(order)], refs[2 * n + len(order) + 1]
        token = refs[-1]

        def copy(send_idx, recv_idx, src, dst, to):
            return pltpu.make_async_remote_copy(src_ref=src, dst_ref=dst, send_sem=send_sems.at[send_idx],
                                                recv_sem=recv_sems.at[recv_idx], device_id=to, device_id_type=MESH)

        for cp in start(src_refs, land_refs, copy):
            cp.start()
        token[...] = jnp.zeros_like(token)

    sem = pltpu.SemaphoreType.DMA((n_sem,))
    outs = pl.pallas_call(
        body, name=name, in_specs=[HBM_SPEC] * (2 * n) + [pl.BlockSpec(memory_space=pl.ANY)] * len(order),
        out_specs=[SEM_SPEC, SEM_SPEC] + [HBM_SPEC] * (2 * n) + [pl.BlockSpec(memory_space=pltpu.VMEM)],
        out_shape=[sem, sem] + _hbm_like(srcs) + _hbm_like(lands) + [jax.ShapeDtypeStruct((8, LANE), F32)],
        input_output_aliases={i: 2 + i for i in range(2 * n)},
        compiler_params=pltpu.CompilerParams(has_side_effects=IN_FLIGHT),
    )(*[_in_hbm(a) for a in srcs], *[_in_hbm(a) for a in lands], *order)
    return (outs[0], outs[1], outs[2:2 + n], outs[2 + n:2 + 2 * n]), outs[-1]


def _split_wait(name, send_sems, recv_sems, srcs, lands, after, waits):
    n = len(srcs)

    def body(*refs):
        src_refs, land_refs = refs[:n], refs[n:2 * n]
        send_ref, recv_ref = refs[2 * n], refs[2 * n + 1]

        def copy(send_idx, recv_idx, src, dst, to):
            return pltpu.make_async_remote_copy(src_ref=src, dst_ref=dst, send_sem=send_ref.at[send_idx],
                                                recv_sem=recv_ref.at[recv_idx], device_id=to, device_id_type=MESH)

        for cp in waits(src_refs, land_refs, copy):
            cp.wait_send()
            cp.wait_recv()

    outs = pl.pallas_call(
        body, name=name, in_specs=[HBM_SPEC] * (2 * n) + [SEM_SPEC, SEM_SPEC, pl.BlockSpec(memory_space=pl.ANY)],
        out_specs=[HBM_SPEC] * (2 * n), out_shape=_hbm_like(srcs) + _hbm_like(lands),
        input_output_aliases={i: i for i in range(2 * n)},
        compiler_params=pltpu.CompilerParams(has_side_effects=IN_FLIGHT),
    )(*srcs, *lands, send_sems, recv_sems, after)
    return outs[:n], outs[n:]


GATHER_LATE_SEMS = 2 * (NCHIP - 1)


def _gather_async_start(tag, shards, after):
    def start(srcs, lands, copy):
        k, c, chips, _ = _peers()
        out = []
        for a, (src, dst) in enumerate(zip(srcs, lands)):
            for j, (cx, cy) in enumerate(chips):
                for core in range(2):
                    out.append(copy(GATHER_LATE_SEMS * a + 2 * j + core, GATHER_LATE_SEMS * a + 2 * j + c,
                                    src.at[*_half(c, src.shape)], dst.at[k, *_half(c, src.shape)], (cx, cy, core)))
        return out

    chip = 2 * lax.axis_index("x") + lax.axis_index("y")
    lands = [lax.dynamic_update_slice(lax.empty((NCHIP,) + s.shape, s.dtype), s[None], (chip, 0, 0)) for s in shards]
    return _split_start("gather_%s_start" % tag, shards, lands, after, GATHER_LATE_SEMS * len(shards), start)


def _gather_async_wait(tag, send_sems, recv_sems, shards, lands, after):
    def waits(srcs, lands_, copy):
        _, c, chips, _ = _peers()
        out = []
        for a, (src, dst) in enumerate(zip(srcs, lands_)):
            for j, (cx, cy) in enumerate(chips):
                for core in range(2):
                    idx = GATHER_LATE_SEMS * a + 2 * j + core
                    out.append(copy(idx, idx, src.at[*_half(c, src.shape)], dst.at[2 * cx + cy, *_half(core, src.shape)], (cx, cy, core)))
        return out

    return _split_wait("gather_%s_wait" % tag, send_sems, recv_sems, shards, lands, after, waits)[1]


def _other_devices():
    x, y, c = _coords()
    out = []
    for d in range(1, NDEV):
        tx, ty, tc = x ^ (d >> 2), y ^ ((d >> 1) & 1), c ^ (d & 1)
        out.append((d, (tx, ty, tc), 2 * tx + ty, 4 * tx + 2 * ty + tc))
    return out


def _reduce_async_start(tag, stacks, after):
    def start(srcs, lands, copy):
        x, y, c = _coords()
        me = 4 * x + 2 * y + c
        return [copy((NDEV - 1) * a + d - 1, (NDEV - 1) * a + d - 1, src.at[chip, *_half(to[2], src.shape)], dst.at[me], to)
                for a, (src, dst) in enumerate(zip(srcs, lands)) for d, to, chip, _ in _other_devices()]

    x, y, c = _coords()
    lands = []
    for s in stacks:
        hr, hc = _half_shape(s.shape)
        at = (c * hr, 0) if _half_axis(s.shape) == 0 else (0, c * hc)
        own = lax.dynamic_slice(s, (2 * x + y,) + at, (1, hr, hc))
        lands.append(lax.dynamic_update_slice(lax.empty((NDEV, hr, hc), s.dtype), own, (4 * x + 2 * y + c, 0, 0)))
    return _split_start("reduce_%s_start" % tag, stacks, lands, after, (NDEV - 1) * len(stacks), start)


def _reduce_async_wait(tag, send_sems, recv_sems, stacks, lands, after):
    def waits(srcs, lands_, copy):
        return [copy((NDEV - 1) * a + d - 1, (NDEV - 1) * a + d - 1, src.at[chip, *_half(to[2], src.shape)], dst.at[pos], to)
                for a, (src, dst) in enumerate(zip(srcs, lands_)) for d, to, chip, pos in _other_devices()]

    return _split_wait("reduce_%s_wait" % tag, send_sems, recv_sems, stacks, lands, after, waits)[1]


def _reduce_finish(tag, arrived, dims):
    n_arr = len(arrived)

    def body(*refs):
        lands, fin = refs[:n_arr], refs[n_arr:2 * n_arr]
        send_sems, recv_sems = refs[2 * n_arr:]
        _, c, _, sibling = _peers()
        sends = []
        for a in range(n_arr):
            mine = fin[a].at[*_half(c, dims[a])]

            def device_sum(vs, vf, a=a, mine=mine):
                pltpu.sync_copy(lands[a], vs)
                acc = vs[0].astype(F32)
                for i in range(1, NDEV):
                    acc = acc + vs[i].astype(F32)
                vf[...] = acc
                pltpu.sync_copy(vf, mine)

            pl.run_scoped(device_sum, pltpu.VMEM((NDEV,) + _half_shape(dims[a]), BF16), pltpu.VMEM(_half_shape(dims[a]), F32))
            sends.append(pltpu.make_async_remote_copy(src_ref=mine, dst_ref=mine, send_sem=send_sems.at[a], recv_sem=recv_sems.at[a],
                                                      device_id=sibling, device_id_type=MESH))
            sends[-1].start()
        for a in range(n_arr):
            other = fin[a].at[*_half(1 - c, dims[a])]
            pltpu.make_async_remote_copy(src_ref=other, dst_ref=other, send_sem=send_sems.at[a], recv_sem=recv_sems.at[a],
                                         device_id=sibling, device_id_type=MESH).wait_recv()
        for cp in sends:
            cp.wait_send()

    any_spec = pl.BlockSpec(memory_space=pl.ANY)
    return pl.pallas_call(
        body, name="reduce_%s_finish" % tag, in_specs=[any_spec] * n_arr, out_specs=[any_spec] * n_arr,
        out_shape=[jax.ShapeDtypeStruct(d, F32) for d in dims],
        scratch_shapes=[pltpu.SemaphoreType.DMA((n_arr,)), pltpu.SemaphoreType.DMA((n_arr,))],
    )(*arrived)


SMALL = (("conv_w", SSD_K * SSD_XBC), ("conv_b", SSD_XBC), ("dt_bias", H), ("A_log", H), ("D", H), ("ssd_norm", SSD_INNER),
         ("q_norm", Q_RANK), ("kv_norm", KV_RANK), ("out_norm", SSD_INNER), ("ln_mix_g", D), ("ln_mix_b", D),
         ("ln_ffn_g", D), ("ln_ffn_b", D))
SMALL_ROWS = 120
NDEV = 8


def _allreduce_small_start(sv):
    def start(srcs, lands, copy):
        x, y, c = _coords()
        return [copy(d - 1, d - 1, srcs[0], lands[0].at[4 * x + 2 * y + c], to) for d, to, _, _ in _other_devices()]

    x, y, c = _coords()
    slots = lax.dynamic_update_slice(lax.empty((NDEV,) + sv.shape, sv.dtype), sv[None], (4 * x + 2 * y + c, 0, 0))
    return _split_start("allreduce_small_start", [sv], [slots], None, NDEV - 1, start)


def _allreduce_small_wait(send_sems, recv_sems, srcs, lands, after):
    def waits(srcs_, lands_, copy):
        return [copy(d - 1, d - 1, srcs_[0], lands_[0].at[pos], to) for d, to, _, pos in _other_devices()]

    def device_sum(slots_ref, out_ref):
        acc = slots_ref[0]
        for i in range(1, NDEV):
            acc = acc + slots_ref[i]
        out_ref[...] = acc

    slots = _split_wait("allreduce_small_wait", send_sems, recv_sems, srcs, lands, after, waits)[1][0]
    vm = pl.BlockSpec(memory_space=pltpu.VMEM)
    return pl.pallas_call(device_sum, name="allreduce_small_sum", in_specs=[vm], out_specs=vm,
                          out_shape=jax.ShapeDtypeStruct(slots.shape[1:], slots.dtype))(slots)


def _adamw_math(w, g, m, v):
    m2 = ADAM_B1 * m + (1.0 - ADAM_B1) * g
    v2 = ADAM_B2 * v + (1.0 - ADAM_B2) * (g * g)
    m_hat = m2 / (1.0 - ADAM_B1 ** ADAM_STEP)
    v_hat = v2 / (1.0 - ADAM_B2 ** ADAM_STEP)
    return -ADAM_LR * (m_hat / (jnp.sqrt(v_hat) + ADAM_EPS) + ADAM_WD * w), m2, v2


ADAM_BLOCK_BYTES = 2 * 1024 * 1024


def _adamw_big(w, g, m, v, *, name):
    r, c = w.shape

    def body(w_ref, g_ref, m_ref, v_ref, d_ref, m2_ref, v2_ref):
        d_ref[...], m2_ref[...], v2_ref[...] = _adamw_math(w_ref[...], g_ref[...], m_ref[...], v_ref[...])

    tr = max(t for t in range(8, r + 1, 8) if r % t == 0 and t * c * 4 <= ADAM_BLOCK_BYTES)
    steps, spec = r // tr, pl.BlockSpec((tr, c), lambda i: (i, 0))
    return pl.pallas_call(body, name=name, grid=(steps,), in_specs=[spec] * 4, out_specs=[spec] * 3,
                          out_shape=[jax.ShapeDtypeStruct((r, c), F32)] * 3)(w, g, m, v)


def _adamw_small(ws, gs, ms, vs):
    n = len(ws)

    def body(*refs):
        for i in range(n):
            w_ref, g_ref, m_ref, v_ref = (refs[j * n + i] for j in range(4))
            d_ref, m2_ref, v2_ref = (refs[(4 + j) * n + i] for j in range(3))
            d_ref[...], m2_ref[...], v2_ref[...] = _adamw_math(w_ref[...], g_ref[...], m_ref[...], v_ref[...])

    vm = pl.BlockSpec(memory_space=pltpu.VMEM)
    shapes = [jax.ShapeDtypeStruct(w.shape, F32) for w in ws]
    outs = pl.pallas_call(body, name="adamw_small", in_specs=[vm] * (4 * n), out_specs=[vm] * (3 * n), out_shape=shapes * 3)(
        *ws, *gs, *ms, *vs)
    return outs[:n], outs[n:2 * n], outs[2 * n:]


_SMALL_ARG = {"conv_w": "ssd_conv_w", "conv_b": "ssd_conv_b", "dt_bias": "ssd_dt_bias", "A_log": "ssd_A_log", "D": "ssd_D",
              "ssd_norm": "ssd_norm_w", "q_norm": "mla_q_norm_w", "kv_norm": "mla_kv_norm_w", "out_norm": "mla_out_norm_w",
              "ln_mix_g": "ln_mix_g", "ln_mix_b": "ln_mix_b", "ln_ffn_g": "ln_ffn_g", "ln_ffn_b": "ln_ffn_b"}
_BIG_ARG = {"w_in": "w_in", "w_qb": "mla_w_q_b", "w_kvb": "mla_w_kv_b", "w_out": "w_out", "w_gate": "w_ffn_gate",
            "w_up": "w_ffn_up", "w_down": "w_ffn_down", "w_pg": "w_ple_gate", "w_pp": "w_ple_proj"}
_WEIGHT_ORDER = ("w_in", "ssd_conv_w", "ssd_conv_b", "ssd_dt_bias", "ssd_A_log", "ssd_D", "ssd_norm_w", "mla_q_norm_w", "mla_w_q_b",
                 "mla_kv_norm_w", "mla_w_kv_b", "mla_out_norm_w", "w_out", "ln_mix_g", "ln_mix_b", "w_ffn_gate", "w_ffn_up",
                 "w_ffn_down", "w_ple_gate", "w_ple_proj", "ln_ffn_g", "ln_ffn_b")


def _rows128(a):
    flat = a.reshape(-1)
    return jnp.pad(flat, (0, -flat.shape[0] % LANE)).reshape(-1, LANE)


def kernel(x, p, positions, w_in, ssd_conv_w, ssd_conv_b, ssd_dt_bias, ssd_A_log, ssd_D, ssd_norm_w, mla_q_norm_w, mla_w_q_b, mla_kv_norm_w, mla_w_kv_b, mla_out_norm_w, w_out, ln_mix_g, ln_mix_b, w_ffn_gate, w_ffn_up, w_ffn_down, w_ple_gate, w_ple_proj, ln_ffn_g, ln_ffn_b, loss_target, m_w_in, m_ssd_conv_w, m_ssd_conv_b, m_ssd_dt_bias, m_ssd_A_log, m_ssd_D, m_ssd_norm_w, m_mla_q_norm_w, m_mla_w_q_b, m_mla_kv_norm_w, m_mla_w_kv_b, m_mla_out_norm_w, m_w_out, m_ln_mix_g, m_ln_mix_b, m_w_ffn_gate, m_w_ffn_up, m_w_ffn_down, m_w_ple_gate, m_w_ple_proj, m_ln_ffn_g, m_ln_ffn_b, v_w_in, v_ssd_conv_w, v_ssd_conv_b, v_ssd_dt_bias, v_ssd_A_log, v_ssd_D, v_ssd_norm_w, v_mla_q_norm_w, v_mla_w_q_b, v_mla_kv_norm_w, v_mla_w_kv_b, v_mla_out_norm_w, v_w_out, v_ln_mix_g, v_ln_mix_b, v_w_ffn_gate, v_w_ffn_up, v_w_ffn_down, v_w_ple_gate, v_w_ple_proj, v_ln_ffn_g, v_ln_ffn_b):
    given = dict(locals())
    chip = 2 * lax.axis_index("x") + lax.axis_index("y")

    def local(name, prefix=""):
        a = given[prefix + _BIG_ARG[name]][0]
        return a.T if name in TRANSPOSED else a

    def updated(name, prefix=""):
        if name in ROW_MAJOR:
            _, c, r = given[prefix + _BIG_ARG[name]].shape
            return given[prefix + _BIG_ARG[name]].reshape(c // LANE, LANE, r).transpose(2, 0, 1).reshape(-1, LANE)
        return local(name, prefix)

    def global_layout(name, arr):
        if name in ROW_MAJOR:
            r, c = local(name).shape
            return arr.reshape(r, c // LANE, LANE).transpose(1, 2, 0).reshape(1, c, r)
        return (arr.T if name in TRANSPOSED else arr)[None]

    conv_bits = lax.bitcast_convert_type(ssd_conv_w[0], BF16).reshape(SSD_K, 2 * CONV_SHARD)
    w_in_all, conv_all = _gather_weights([local("w_in").astype(BF16), jnp.pad(conv_bits, ((0, BF16_ROWS - SSD_K), (0, 0)))])
    sp = {k: given[a] for k, a in _SMALL_ARG.items() if k != "conv_w"}
    sp["conv_w"] = _from_cols(lax.bitcast_convert_type(conv_all[:, :SSD_K].reshape(NCHIP, SSD_K, CONV_SHARD, 2), F32))
    gathering, tie = {}, w_in_all
    for group, names in ASYNC_GROUPS:
        gathering[group], tie = _gather_async_start(group, [local(name).astype(BF16) for name in names], tie)

    def fetch(group, after):
        return dict(zip(dict(ASYNC_GROUPS)[group], _gather_async_wait(group, *gathering[group], after)))

    reducing = {}

    def send(group, grads):
        if group == "small":
            rows = jnp.concatenate([_rows128(grads[name]) for name, _ in SMALL] + [grads["loss"]], axis=0)
            reducing[group], sent = _allreduce_small_start(jnp.pad(rows, ((0, SMALL_ROWS - rows.shape[0]), (0, 0))))
        else:
            reducing[group], sent = _reduce_async_start(group, [grads[name] for name in dict(ASYNC_GROUPS)[group]], None)
        return sent[0, 0]

    loss_row, grad_x, dw_in, g = _local_step(x[0], p[0, 0], positions[0], loss_target[0], w_in_all, fetch, send, sp, tie[0, 0])

    reducing["in"], tie = _reduce_async_start("in", [dw_in], grad_x)
    gbig = {}
    for group, names in reversed(ASYNC_GROUPS):
        arrived = _reduce_async_wait(group, *reducing[group], tie)
        gbig.update(zip(names, _reduce_finish(group, arrived, [local(name).shape for name in names])))
    small_sum = _allreduce_small_wait(*reducing.pop("small"), tie)
    gsmall, row = {}, 0
    for name, size in SMALL:
        nrow = -(-size // LANE)
        gsmall[name] = small_sum[row:row + nrow].reshape(-1)[:size]
        row += nrow
    loss = small_sum[row, 0]

    grads = {_BIG_ARG[name]: global_layout(name, arr) for name, arr in gbig.items()}
    for name, _ in SMALL:
        if name == "conv_w":
            full_g = gsmall[name].reshape(SSD_K, SSD_XBC)
            grads["ssd_conv_w"] = lax.dynamic_slice(full_g, (0, chip * CONV_SHARD), (SSD_K, CONV_SHARD))[None]
        else:
            grads[_SMALL_ARG[name]] = gsmall[name].reshape(given[_SMALL_ARG[name]].shape)

    delta, new_m, new_v = {}, {}, {}

    def update_matrix(name, grad):
        a = _BIG_ARG[name]
        d, m2, v2 = _adamw_big(updated(name), grad, updated(name, "m_"), updated(name, "v_"), name="adamw_" + a)
        delta[a], new_m[a], new_v[a] = (global_layout(name, t) for t in (d, m2, v2))
        return d

    for name, grad in gbig.items():
        last = update_matrix(name, grad)
    g_in = _reduce_finish("in", _reduce_async_wait("in", *reducing["in"], last), [updated("w_in").shape])[0]
    grads["w_in"] = global_layout("w_in", g_in)
    update_matrix("w_in", g_in)
    small_names = [_SMALL_ARG[name] for name, _ in SMALL]
    two_d = lambda t: t.reshape(t.shape[-2], t.shape[-1])
    ds, ms, vs = _adamw_small([two_d(given[a]) for a in small_names], [two_d(grads[a]) for a in small_names],
                              [two_d(given["m_" + a]) for a in small_names], [two_d(given["v_" + a]) for a in small_names])
    for a, d, m2, v2 in zip(small_names, ds, ms, vs):
        delta[a], new_m[a], new_v[a] = (t.reshape(given[a].shape) for t in (d, m2, v2))

    return (loss, grad_x[None], *[grads[n] for n in _WEIGHT_ORDER], *[delta[n] for n in _WEIGHT_ORDER],
            *[new_m[n] for n in _WEIGHT_ORDER], *[new_v[n] for n in _WEIGHT_ORDER])
```

```python
import functools
import math

import jax
import jax.numpy as jnp
from jax import lax
from jax.experimental import pallas as pl
from jax.experimental.pallas import tpu as pltpu

F32 = jnp.float32
BF16 = jnp.bfloat16

S = 2048
D = 1024
PLE = 256
H = 16
SSD_P = 64
SSD_INNER = 1024
SSD_N = 128
SSD_G = 2
SSD_L = 128
SSD_NC = S // SSD_L
SSD_XBC = 1536
SSD_K = 4
Q_RANK = 384
KV_RANK = 256
NOPE = 64
ROPE = 32
VDIM = 64
D_FF = 2816
IN_WIDTH = 3248
ALPHA = 2.0 ** 0.25
EPS_RMS = 1e-6
EPS_LN = 1e-5
ATT_SCALE = 1.0 / math.sqrt(NOPE + ROPE)
LN2 = math.log(2.0)
ATT_SCALE_LOG2 = ATT_SCALE / LN2
LANE = 128
NCHIP = 4
SMALL_W = 896
SM_Q, SM_KV, SM_KR, SM_DT = 0, 384, 640, 768
NEG = -1e30

ADAM_LR = 0.001
ADAM_B1 = 0.9
ADAM_B2 = 0.999
ADAM_EPS = 1e-08
ADAM_WD = 0.01
ADAM_STEP = 10


def _sigmoid(v):
    return 1.0 / (1.0 + jnp.exp(-v))


MM_VMEM_BUDGET = 36 * 2 ** 20
MM_MAX_ACC = 2048 * 1024


def _mm_tiles(pairs, ks, m, n, out_dtype, has_add):
    def divs(v):
        return [LANE * d for d in range(v // LANE, 0, -1) if (v // LANE) % d == 0] if v % LANE == 0 else [v]

    def cost(tm, tn):
        tot = tm * tn * (jnp.dtype(out_dtype).itemsize + (4 if has_add else 0))
        for (a, b), k in zip(pairs, ks):
            tot += k * (tm * a.dtype.itemsize + tn * b.dtype.itemsize)
        return 2 * tot

    ok = [(tm * tn, tm, tn) for tm in divs(m) for tn in divs(n) if tm * tn <= MM_MAX_ACC and cost(tm, tn) <= MM_VMEM_BUDGET]
    _, tm, tn = max(ok)
    return tm, tn


def _mm(pairs, *, ta=False, tb=False, out_dtype=F32, add=None, add_scale=1.0, chunk=None, name):
    n_pairs = len(pairs)
    windows = [pr[2] if len(pr) == 3 else None for pr in pairs]
    pairs = [pr[:2] for pr in pairs]
    assert not ((ta or tb) and any(windows))
    ks = [w[2] if w else (a.shape[-2] if ta else a.shape[-1]) for (a, _), w in zip(pairs, windows)]
    a0, b0 = pairs[0]
    m = a0.shape[-1] if ta else a0.shape[-2]
    n = b0.shape[-2] if tb else b0.shape[-1]
    tm, tn = _mm_tiles(pairs, ks, m, n, out_dtype, add is not None)
    dims = (((0 if ta else 1,), (1 if tb else 0,)), ((), ()))
    nk = NCHIP if chunk else 1
    assert chunk != "sum" or out_dtype == F32
    flat = [i for i, (a, b) in enumerate(pairs) if a.ndim == 2 and b.ndim == 2]
    stacked = [i for i in range(n_pairs) if i not in flat]

    def body(*refs):
        o_ref = refs[-1]

        def products(which):
            acc = None
            for i in which:
                a = refs[2 * i][...].astype(BF16)
                b = refs[2 * i + 1][...].astype(BF16)
                part = lax.dot_general(a, b, dims, preferred_element_type=F32)
                acc = part if acc is None else acc + part
            return acc

        if chunk == "sum":
            k = pl.program_id(2)
            acc = products(stacked)

            @pl.when(k == 0)
            def _():
                first = acc + products(flat) if flat else acc
                o_ref[...] = first + add_scale * refs[2 * n_pairs][...] if add is not None else first

            @pl.when(k > 0)
            def _():
                o_ref[...] += acc
            return
        acc = products(range(n_pairs))
        if add is not None:
            acc = acc + add_scale * refs[2 * n_pairs][...]
        o_ref[...] = acc.astype(out_dtype)

    def spec(arr, shape, idx2):
        if arr.ndim == 3:
            return pl.BlockSpec((None,) + shape, lambda i, j, k: (k,) + idx2(i, j))
        return pl.BlockSpec(shape, lambda i, j, k: idx2(i, j))

    in_specs, args = [], []
    for (a, b), kdim, window in zip(pairs, ks, windows):
        ka, kb = window[:2] if window else (0, 0)
        in_specs.append(spec(a, (kdim, tm), lambda i, j: (0, i)) if ta else spec(a, (tm, kdim), lambda i, j, ka=ka: (i, ka)))
        in_specs.append(spec(b, (tn, kdim), lambda i, j: (j, 0)) if tb else spec(b, (kdim, tn), lambda i, j, kb=kb: (kb, j)))
        args += [a, b]
    if add is not None:
        in_specs.append(pl.BlockSpec((tm, tn), lambda i, j, k: (i, j)))
        args.append(add)
    if chunk == "out":
        out_spec = pl.BlockSpec((None, tm, tn), lambda i, j, k: (k, i, j))
        out_shape = jax.ShapeDtypeStruct((nk, m, n), out_dtype)
    else:
        out_spec = pl.BlockSpec((tm, tn), lambda i, j, k: (i, j))
        out_shape = jax.ShapeDtypeStruct((m, n), out_dtype)
    return pl.pallas_call(
        body, name=name, grid=(m // tm, n // tn, nk), in_specs=in_specs, out_specs=out_spec, out_shape=out_shape,
        compiler_params=pltpu.CompilerParams(dimension_semantics=("parallel", "parallel", "arbitrary")),
    )(*args)


TR = 256


def _row_spec(c):
    return pl.BlockSpec((TR, c), lambda i: (i, 0))


def _vec_spec(c):
    return pl.BlockSpec((1, c), lambda i: (0, 0))


def _acc_rows(ref, val):
    @pl.when(pl.program_id(0) == 0)
    def _():
        ref[...] = jnp.zeros_like(ref)
    ref[...] += val


def _rms_fwd(u, w, *, name):
    c = u.shape[1]

    def body(u_ref, w_ref, o_ref):
        v = u_ref[...]
        r = lax.rsqrt(jnp.mean(v * v, axis=-1, keepdims=True) + EPS_RMS)
        o_ref[...] = (v * r * w_ref[...]).astype(BF16)

    return pl.pallas_call(body, name=name, grid=(S // TR,), in_specs=[_row_spec(c), _vec_spec(c)], out_specs=_row_spec(c),
                          out_shape=jax.ShapeDtypeStruct((S, c), BF16))(u, w)


def _rms_bwd(u, w, dy, *, name):
    c = u.shape[1]

    def body(u_ref, w_ref, dy_ref, du_ref, dw_ref):
        v = u_ref[...]
        g = dy_ref[...].astype(F32)
        r = lax.rsqrt(jnp.mean(v * v, axis=-1, keepdims=True) + EPS_RMS)
        gw = g * w_ref[...]
        du_ref[...] = r * gw - v * (r * r * r * jnp.mean(gw * v, axis=-1, keepdims=True))
        _acc_rows(dw_ref, jnp.sum(g * v * r, axis=0, keepdims=True))

    return pl.pallas_call(body, name=name, grid=(S // TR,), in_specs=[_row_spec(c), _vec_spec(c), _row_spec(c)],
                          out_specs=[_row_spec(c), _vec_spec(c)],
                          out_shape=[jax.ShapeDtypeStruct((S, c), F32), jax.ShapeDtypeStruct((1, c), F32)])(u, w, dy)


def _gate_norm_fwd(y, z, w):
    def body(y_ref, z_ref, w_ref, o_ref):
        zz = z_ref[...]
        v = y_ref[...] * (zz * _sigmoid(zz))
        r = lax.rsqrt(jnp.mean(v * v, axis=-1, keepdims=True) + EPS_RMS)
        o_ref[...] = (v * r * w_ref[...]).astype(BF16)

    c = SSD_INNER
    return pl.pallas_call(body, name="ssd_gate_norm_fwd", grid=(S // TR,), in_specs=[_row_spec(c), _row_spec(c), _vec_spec(c)],
                          out_specs=_row_spec(c), out_shape=jax.ShapeDtypeStruct((S, c), BF16))(y, z, w)


def _gate_norm_bwd(y, z, w, dout):
    def body(y_ref, z_ref, w_ref, g_ref, dy_ref, dz_ref, dw_ref):
        yy = y_ref[...]
        zz = z_ref[...]
        sg = _sigmoid(zz)
        sz = zz * sg
        v = yy * sz
        g = g_ref[...]
        r = lax.rsqrt(jnp.mean(v * v, axis=-1, keepdims=True) + EPS_RMS)
        gw = g * w_ref[...]
        dv = r * gw - v * (r * r * r * jnp.mean(gw * v, axis=-1, keepdims=True))
        dy_ref[...] = dv * sz
        dz_ref[...] = (dv * yy * (sg * (1.0 + zz * (1.0 - sg)))).astype(BF16)
        _acc_rows(dw_ref, jnp.sum(g * v * r, axis=0, keepdims=True))

    c = SSD_INNER
    return pl.pallas_call(body, name="ssd_gate_norm_bwd", grid=(S // TR,),
                          in_specs=[_row_spec(c), _row_spec(c), _vec_spec(c), _row_spec(c)],
                          out_specs=[_row_spec(c), _row_spec(c), _vec_spec(c)],
                          out_shape=[jax.ShapeDtypeStruct((S, c), F32), jax.ShapeDtypeStruct((S, c), BF16),
                                     jax.ShapeDtypeStruct((1, c), F32)])(y, z, w, dout)


MIX_ROWS = 512


def _out_proj_ln(y_ssd, y_mla, w_out, xr, g, b):
    k = y_ssd.shape[1]

    def body(ys_ref, ym_ref, w_ref, x_ref, g_ref, b_ref, m_ref, o_ref, ob_ref):
        mix = (jnp.dot(ys_ref[...], w_ref[:k], preferred_element_type=F32)
               + jnp.dot(ym_ref[...], w_ref[k:], preferred_element_type=F32))
        m_ref[...] = mix
        pre = ALPHA * x_ref[...] + mix
        mu = jnp.mean(pre, axis=-1, keepdims=True)
        d = pre - mu
        rs = lax.rsqrt(jnp.mean(d * d, axis=-1, keepdims=True) + EPS_LN)
        h = d * rs * g_ref[...] + b_ref[...]
        o_ref[...] = h
        ob_ref[...] = h.astype(BF16)

    rows = lambda c: pl.BlockSpec((MIX_ROWS, c), lambda i: (i, 0))
    return pl.pallas_call(
        body, name="out_proj_ln", grid=(S // MIX_ROWS,),
        in_specs=[rows(k), rows(k), _whole(w_out), rows(D), _vec_spec(D), _vec_spec(D)], out_specs=[rows(D)] * 3,
        out_shape=[jax.ShapeDtypeStruct((S, D), F32), jax.ShapeDtypeStruct((S, D), F32), jax.ShapeDtypeStruct((S, D), BF16)],
    )(y_ssd, y_mla, w_out, xr, g, b)


def _ln_bwd(xr, mix, g, dh):
    def body(x_ref, m_ref, g_ref, dh_ref, dpre_ref, dpreb_ref, dg_ref, db_ref):
        pre = ALPHA * x_ref[...] + m_ref[...]
        mu = jnp.mean(pre, axis=-1, keepdims=True)
        d = pre - mu
        rs = lax.rsqrt(jnp.mean(d * d, axis=-1, keepdims=True) + EPS_LN)
        xh = d * rs
        dy = dh_ref[...]
        gy = dy * g_ref[...]
        dpre = rs * (gy - jnp.mean(gy, axis=-1, keepdims=True) - xh * jnp.mean(gy * xh, axis=-1, keepdims=True))
        dpre_ref[...] = dpre
        dpreb_ref[...] = dpre.astype(BF16)
        _acc_rows(dg_ref, jnp.sum(dy * xh, axis=0, keepdims=True))
        _acc_rows(db_ref, jnp.sum(dy, axis=0, keepdims=True))

    return pl.pallas_call(body, name="ln_mix_bwd", grid=(S // TR,),
                          in_specs=[_row_spec(D), _row_spec(D), _vec_spec(D), _row_spec(D)],
                          out_specs=[_row_spec(D), _row_spec(D), _vec_spec(D), _vec_spec(D)],
                          out_shape=[jax.ShapeDtypeStruct((S, D), F32), jax.ShapeDtypeStruct((S, D), BF16),
                                     jax.ShapeDtypeStruct((1, D), F32), jax.ShapeDtypeStruct((1, D), F32)])(xr, mix, g, dh)


FF_CHUNK = D_FF // NCHIP


FF_ROWS = 1024


def _ff_act_spec():
    return pl.BlockSpec((None, FF_ROWS, FF_CHUNK), lambda i, k: (k, i, 0))


def _ff_w_spec():
    return pl.BlockSpec((None, FF_CHUNK, D), lambda i, k: (k, 0, 0))


def _ffn_hidden_fwd(h, w_gate_t, w_up_t):
    def body(h_ref, wg_ref, wu_ref, g_ref, u_ref, a_ref):
        hh = h_ref[...]
        g = _dot(hh, wg_ref[...], ((1,), (1,)))
        u = _dot(hh, wu_ref[...], ((1,), (1,)))
        g_ref[...] = g.astype(BF16)
        u_ref[...] = u.astype(BF16)
        a_ref[...] = (g * _sigmoid(g) * u).astype(BF16)

    return pl.pallas_call(
        body, name="ffn_hidden_fwd", grid=(S // FF_ROWS, NCHIP),
        in_specs=[pl.BlockSpec((FF_ROWS, D), lambda i, k: (i, 0)), _ff_w_spec(), _ff_w_spec()], out_specs=[_ff_act_spec()] * 3,
        out_shape=[jax.ShapeDtypeStruct((NCHIP, S, FF_CHUNK), BF16)] * 3,
        compiler_params=pltpu.CompilerParams(dimension_semantics=("parallel", "parallel")),
    )(h, w_gate_t, w_up_t)


def _ffn_hidden_bwd(dout, w_down, gate, up):
    def body(d_ref, wd_ref, g_ref, u_ref, dg_ref, du_ref):
        d = _dot(d_ref[...], wd_ref[...], ((1,), (1,)))
        g = g_ref[...].astype(F32)
        sg = _sigmoid(g)
        dg_ref[...] = (d * u_ref[...].astype(F32) * (sg * (1.0 + g * (1.0 - sg)))).astype(BF16)
        du_ref[...] = (d * g * sg).astype(BF16)

    return pl.pallas_call(
        body, name="ffn_hidden_bwd", grid=(S // FF_ROWS, NCHIP),
        in_specs=[pl.BlockSpec((FF_ROWS, D), lambda i, k: (i, 0)), _ff_w_spec(), _ff_act_spec(), _ff_act_spec()],
        out_specs=[_ff_act_spec()] * 2, out_shape=[jax.ShapeDtypeStruct((NCHIP, S, FF_CHUNK), BF16)] * 2,
        compiler_params=pltpu.CompilerParams(dimension_semantics=("parallel", "parallel")),
    )(dout, w_down, gate, up)


def _final_fwd_bwd(h1, ffn, h1b, pb, w_pg, w_pp, target, g2, b2):
    def body(h_ref, f_ref, hb_ref, pb_ref, wpg_ref, wpp_ref, t_ref, g_ref, b_ref,
             dpre_ref, dpreb_ref, dpg_ref, dpp_ref, dg_ref, db_ref, loss_ref):
        sg = _sigmoid(jnp.dot(hb_ref[...], wpg_ref[...], preferred_element_type=F32))
        ppv = jnp.dot(pb_ref[...], wpp_ref[...], preferred_element_type=F32)
        pre = ALPHA * h_ref[...] + f_ref[...] + sg * ppv
        mu = jnp.mean(pre, axis=-1, keepdims=True)
        d = pre - mu
        rs = lax.rsqrt(jnp.mean(d * d, axis=-1, keepdims=True) + EPS_LN)
        xh = d * rs
        err = xh * g_ref[...] + b_ref[...] - t_ref[...]
        dy = err * (1.0 / D)
        gy = dy * g_ref[...]
        dpre = rs * (gy - jnp.mean(gy, axis=-1, keepdims=True) - xh * jnp.mean(gy * xh, axis=-1, keepdims=True))
        dpre_ref[...] = dpre
        dpreb_ref[...] = dpre.astype(BF16)
        dpg_ref[...] = (dpre * ppv * sg * (1.0 - sg)).astype(BF16)
        dpp_ref[...] = (dpre * sg).astype(BF16)
        _acc_rows(dg_ref, jnp.sum(dy * xh, axis=0, keepdims=True))
        _acc_rows(db_ref, jnp.sum(dy, axis=0, keepdims=True))
        _acc_rows(loss_ref, 0.5 * jnp.sum(jnp.mean(err * err, axis=-1, keepdims=True), axis=0, keepdims=True) * jnp.ones((1, LANE), F32))

    return pl.pallas_call(
        body, name="final_ln_loss", grid=(S // TR,),
        in_specs=[_row_spec(D)] * 3 + [_row_spec(pb.shape[1]), _whole(w_pg), _whole(w_pp), _row_spec(D)] + [_vec_spec(D)] * 2,
        out_specs=[_row_spec(D)] * 4 + [_vec_spec(D), _vec_spec(D), _vec_spec(LANE)],
        out_shape=[jax.ShapeDtypeStruct((S, D), F32)] + [jax.ShapeDtypeStruct((S, D), BF16)] * 3 + [
                   jax.ShapeDtypeStruct((1, D), F32), jax.ShapeDtypeStruct((1, D), F32), jax.ShapeDtypeStruct((1, LANE), F32)],
    )(h1, ffn, h1b, pb, w_pg, w_pp, target, g2, b2)


def _rot(u, cos_t, sin_t, lane):
    partner = jnp.where(lane < NOPE + ROPE // 2, pltpu.roll(u, LANE - ROPE // 2, 1), pltpu.roll(u, ROPE // 2, 1))
    return u * cos_t + partner * sin_t


def _rms(v, w):
    r = lax.rsqrt(jnp.mean(v * v, axis=-1, keepdims=True) + EPS_RMS)
    return v * r * w, r


def _rms_grad(v, r, w, g):
    gw = g * w
    return r * gw - v * (r * r * r * jnp.mean(gw * v, axis=-1, keepdims=True)), jnp.sum(g * v * r, axis=0, keepdims=True)


def _whole(arr):
    return pl.BlockSpec(arr.shape, lambda i: (0,) * arr.ndim)


def _qkv_fwd(small, w_q, w_k, w_v, q_norm, kv_norm, cos_t, sin_t):
    def body(sm_ref, wq_ref, wk_ref, wv_ref, qw_ref, kw_ref, c_ref, s_ref, qn_ref, kvn_ref, q_ref, k_ref, kt_ref, v_ref):
        lane = lax.broadcasted_iota(jnp.int32, (TR, LANE), 1)
        c, s = c_ref[...], s_ref[...]
        qn = _rms(sm_ref[:, SM_Q:SM_Q + Q_RANK], qw_ref[...])[0].astype(BF16)
        kvn = _rms(sm_ref[:, SM_KV:SM_KV + KV_RANK], kw_ref[...])[0].astype(BF16)
        qn_ref[...] = qn
        kvn_ref[...] = kvn
        kr = _rot(pltpu.roll(sm_ref[:, SM_KR:SM_KR + LANE], NOPE, 1), c, s, lane)
        for h in range(H):
            tile = slice(h * LANE, (h + 1) * LANE)
            q_ref[:, tile] = _rot(_dot(qn, wq_ref[:, tile], ((1,), (0,))), c, s, lane).astype(BF16)
            kt = _dot(kvn, wk_ref[:, tile], ((1,), (0,))) + kr
            k_ref[:, tile] = kt.astype(BF16)
            kt_ref[tile, :] = kt.T.astype(BF16)
        v_ref[...] = _dot(kvn, wv_ref[...], ((1,), (0,))).astype(BF16)

    w = H * LANE
    return pl.pallas_call(
        body, name="qkv_fwd", grid=(S // TR,),
        in_specs=[_row_spec(SMALL_W), _whole(w_q), _whole(w_k), _whole(w_v), _vec_spec(Q_RANK), _vec_spec(KV_RANK), _row_spec(LANE), _row_spec(LANE)],
        out_specs=[_row_spec(Q_RANK), _row_spec(KV_RANK), _row_spec(w), _row_spec(w), pl.BlockSpec((w, TR), lambda i: (0, i)),
                   _row_spec(H * VDIM)],
        out_shape=[jax.ShapeDtypeStruct((S, Q_RANK), BF16), jax.ShapeDtypeStruct((S, KV_RANK), BF16), jax.ShapeDtypeStruct((S, w), BF16),
                   jax.ShapeDtypeStruct((S, w), BF16), jax.ShapeDtypeStruct((w, S), BF16), jax.ShapeDtypeStruct((S, H * VDIM), BF16)],
    )(small, w_q, w_k, w_v, q_norm, kv_norm, cos_t, sin_t)


def _qkv_bwd(dqt, dk, dv, small, w_q, w_k, w_v, q_norm, kv_norm, cos_t, sin_t):
    def body(dq_ref, dk_ref, dv_ref, sm_ref, wq_ref, wk_ref, wv_ref, qw_ref, kw_ref, c_ref, s_ref,
             ds_ref, dql_ref, dkb_ref, dqw_ref, dkw_ref):
        lane = lax.broadcasted_iota(jnp.int32, (TR, LANE), 1)
        c, s = c_ref[...], -s_ref[...]
        dqn = jnp.zeros((TR, Q_RANK), F32)
        dkvn = _dot(dv_ref[...], wv_ref[...], ((1,), (1,)))
        dkr = jnp.zeros((TR, LANE), F32)
        for h in range(H):
            tile = slice(h * LANE, (h + 1) * LANE)
            dql = _rot(dq_ref[tile, :].T, c, s, lane).astype(BF16)
            dql_ref[:, tile] = dql
            dqn = dqn + _dot(dql, wq_ref[:, tile], ((1,), (1,)))
            dkt = dk_ref[:, tile]
            dkb_ref[:, tile] = dkt.astype(BF16)
            dkvn = dkvn + _dot(dkt, wk_ref[:, tile], ((1,), (1,)))
            dkr = dkr + dkt
        dkr = jnp.where((lane >= NOPE) & (lane < NOPE + ROPE), dkr, 0.0)
        q_c, kv_c = sm_ref[:, SM_Q:SM_Q + Q_RANK], sm_ref[:, SM_KV:SM_KV + KV_RANK]
        dq_c, dqw = _rms_grad(q_c, _rms(q_c, qw_ref[...])[1], qw_ref[...], dqn)
        dkv_c, dkw = _rms_grad(kv_c, _rms(kv_c, kw_ref[...])[1], kw_ref[...], dkvn)
        ds_ref[:, SM_Q:SM_Q + Q_RANK] = dq_c.astype(BF16)
        ds_ref[:, SM_KV:SM_KV + KV_RANK] = dkv_c.astype(BF16)
        ds_ref[:, SM_KR:SM_KR + LANE] = pltpu.roll(_rot(dkr, c, s, lane), LANE - NOPE, 1).astype(BF16)
        _acc_rows(dqw_ref, dqw)
        _acc_rows(dkw_ref, dkw)

    w = H * LANE
    return pl.pallas_call(
        body, name="qkv_bwd", grid=(S // TR,),
        in_specs=[pl.BlockSpec((w, TR), lambda i: (0, i)), _row_spec(w), _row_spec(H * VDIM), _row_spec(SMALL_W), _whole(w_q), _whole(w_k),
                  _whole(w_v), _vec_spec(Q_RANK), _vec_spec(KV_RANK), _row_spec(LANE), _row_spec(LANE)],
        out_specs=[_row_spec(SM_DT), _row_spec(w), _row_spec(w), _vec_spec(Q_RANK), _vec_spec(KV_RANK)],
        out_shape=[jax.ShapeDtypeStruct((S, SM_DT), BF16), jax.ShapeDtypeStruct((S, w), BF16), jax.ShapeDtypeStruct((S, w), BF16),
                   jax.ShapeDtypeStruct((1, Q_RANK), F32), jax.ShapeDtypeStruct((1, KV_RANK), F32)],
    )(dqt, dk, dv, small, w_q, w_k, w_v, q_norm, kv_norm, cos_t, sin_t)


CB = 256


def _shift_down(u, k, row):
    if k == 0:
        return u
    return jnp.where(row >= k, pltpu.roll(u, k, 0), 0.0)


def _shift_up(u, k, row):
    if k == 0:
        return u
    return jnp.where(row < S - k, pltpu.roll(u, S - k, 0), 0.0)


def _conv_fwd(u, w, b):
    def body(u_ref, w_ref, b_ref, o_ref):
        row = lax.broadcasted_iota(jnp.int32, (S, CB), 0)
        uu = u_ref[...]
        acc = b_ref[...] + w_ref[SSD_K - 1:SSD_K, :] * uu
        for k in range(SSD_K - 1):
            acc = acc + w_ref[k:k + 1, :] * _shift_down(uu, SSD_K - 1 - k, row)
        o_ref[...] = acc * _sigmoid(acc)

    c = u.shape[1]
    return pl.pallas_call(
        body, name="conv_fwd", grid=(c // CB,),
        in_specs=[pl.BlockSpec((S, CB), lambda j: (0, j)), pl.BlockSpec((SSD_K, CB), lambda j: (0, j)), pl.BlockSpec((1, CB), lambda j: (0, j))],
        out_specs=pl.BlockSpec((S, CB), lambda j: (0, j)), out_shape=jax.ShapeDtypeStruct((S, c), F32),
    )(u, w, b)


def _conv_bwd(u, w, b, dact):
    def body(u_ref, w_ref, b_ref, d_ref, du_ref, dw_ref, db_ref):
        row = lax.broadcasted_iota(jnp.int32, (S, CB), 0)
        uu = u_ref[...]
        sh = [_shift_down(uu, SSD_K - 1 - k, row) for k in range(SSD_K)]
        acc = b_ref[...]
        for k in range(SSD_K):
            acc = acc + w_ref[k:k + 1, :] * sh[k]
        sg = _sigmoid(acc)
        dacc = d_ref[...] * (sg * (1.0 + acc * (1.0 - sg)))
        du = w_ref[SSD_K - 1:SSD_K, :] * dacc
        for k in range(SSD_K - 1):
            du = du + w_ref[k:k + 1, :] * _shift_up(dacc, SSD_K - 1 - k, row)
        du_ref[...] = du.astype(BF16)
        for k in range(SSD_K):
            dw_ref[k:k + 1, :] = jnp.sum(dacc * sh[k], axis=0, keepdims=True)
        db_ref[...] = jnp.sum(dacc, axis=0, keepdims=True)

    c = u.shape[1]
    col = lambda r: pl.BlockSpec((r, CB), lambda j: (0, j))
    return pl.pallas_call(
        body, name="conv_bwd", grid=(c // CB,), in_specs=[col(S), col(SSD_K), col(1), col(S)], out_specs=[col(S), col(SSD_K), col(1)],
        out_shape=[jax.ShapeDtypeStruct((S, c), BF16), jax.ShapeDtypeStruct((SSD_K, c), F32), jax.ShapeDtypeStruct((1, c), F32)],
    )(u, w, b, dact)


NPAIR = H // 2
PAIRS_PER_GROUP = NPAIR // SSD_G


def _softplus(v):
    return jnp.maximum(v, 0.0) + jnp.log(1.0 + jnp.exp(-jnp.abs(v)))


def _dot(a, b, dims):
    return lax.dot_general(a.astype(BF16), b.astype(BF16), (dims, ((), ())), preferred_element_type=F32)


def _dot2(a, sel):
    hi = a.astype(BF16)
    lo = (a - hi.astype(F32)).astype(BF16)
    dims = (((1,), (0,)), ((), ()))
    return lax.dot_general(hi, sel, dims, preferred_element_type=F32) + lax.dot_general(lo, sel, dims, preferred_element_type=F32)


def _dot3(a, b, dims, split_lhs):
    v = a if split_lhs else b
    v1 = v.astype(BF16)
    r1 = v - v1.astype(F32)
    v2 = r1.astype(BF16)
    v3 = (r1 - v2.astype(F32)).astype(BF16)
    acc = None
    for part in (v1, v2, v3):
        lhs, rhs = (part, b) if split_lhs else (a, part)
        t = lax.dot_general(lhs, rhs, (dims, ((), ())), preferred_element_type=F32)
        acc = t if acc is None else acc + t
    return acc


def _ssd_chunk_common(dt_ref, dtT_ref, prow_ref, pcol_ref):
    prow = prow_ref[...]
    pcol = pcol_ref[...]
    ri = lax.broadcasted_iota(jnp.int32, (SSD_L, SSD_L), 0)
    ci = lax.broadcasted_iota(jnp.int32, (SSD_L, SSD_L), 1)
    causal = ri >= ci
    pre_c = dt_ref[...] + prow[0:1, :]
    dtc = _softplus(pre_c)
    a_row = -jnp.exp(prow[1:2, :])
    cs_col = _dot3(causal.astype(BF16), dtc * a_row, ((1,), (0,)), False)
    dtr = _softplus(dtT_ref[...] + pcol[:, 0:1])
    a_col = -jnp.exp(pcol[:, 1:2])
    cs_row = _dot3(dtr * a_col, (ri <= ci).astype(BF16), ((1,), (0,)), True)
    return prow, causal, pre_c, dtc, a_row, cs_col, cs_row


def _ssd_fwd(act, small, dtT, prow, pcol):
    def body(x_ref, b_ref, c_ref, dt_ref, dtT_ref, prow_ref, pcol_ref, y_ref, st_ref, state):
        @pl.when(pl.program_id(0) == 0)
        def _():
            state[...] = jnp.zeros_like(state)

        prow, causal, _, dtc, _, cs_col, cs_row = _ssd_chunk_common(dt_ref, dtT_ref, prow_ref, pcol_ref)
        lo = lax.broadcasted_iota(jnp.int32, (SSD_L, LANE), 1) < SSD_P
        lo1 = lo[0:1, :]
        for g in range(SSD_G):
            bm = b_ref[:, g * SSD_N:(g + 1) * SSD_N]
            cm = c_ref[:, g * SSD_N:(g + 1) * SSD_N]
            cb = _dot(cm, bm, ((1,), (1,)))
            for qq in range(PAIRS_PER_GROUP):
                q = g * PAIRS_PER_GROUP + qq
                ha, hb = 2 * q, 2 * q + 1
                csa, csb = cs_col[:, ha:ha + 1], cs_col[:, hb:hb + 1]
                xp = x_ref[:, q * LANE:(q + 1) * LANE]
                xx = xp * jnp.where(lo, dtc[:, ha:ha + 1], dtc[:, hb:hb + 1])
                ga = cb * jnp.exp(jnp.where(causal, csa - cs_row[ha:ha + 1, :], NEG))
                gb = cb * jnp.exp(jnp.where(causal, csb - cs_row[hb:hb + 1, :], NEG))
                y = _dot(ga, jnp.where(lo, xx, 0.0), ((1,), (0,))) + _dot(gb, jnp.where(lo, 0.0, xx), ((1,), (0,)))
                s_in = state[q]
                y = y + _dot(cm, s_in, ((1,), (0,))) * jnp.where(lo, jnp.exp(csa), jnp.exp(csb))
                y = y + jnp.where(lo1, prow[2:3, ha:ha + 1], prow[2:3, hb:hb + 1]) * xp
                y_ref[:, q * LANE:(q + 1) * LANE] = y
                la, lb = csa[SSD_L - 1:SSD_L, :], csb[SSD_L - 1:SSD_L, :]
                decay = jnp.where(lo, jnp.exp(la - csa), jnp.exp(lb - csb))
                st_ref[q] = s_in
                state[q] = s_in * jnp.where(lo1, jnp.exp(la), jnp.exp(lb)) + _dot(bm, xx * decay, ((0,), (0,)))

    L = SSD_L
    return pl.pallas_call(
        body, name="ssd_fwd", grid=(SSD_NC,),
        in_specs=[pl.BlockSpec((L, SSD_INNER), lambda c: (c, 0)),
                  pl.BlockSpec((L, SSD_G * SSD_N), lambda c: (c, SSD_INNER // (SSD_G * SSD_N))),
                  pl.BlockSpec((L, SSD_G * SSD_N), lambda c: (c, SSD_INNER // (SSD_G * SSD_N) + 1)),
                  pl.BlockSpec((L, LANE), lambda c: (c, SM_DT // LANE)),
                  pl.BlockSpec((LANE, L), lambda c: (0, c)),
                  pl.BlockSpec((8, LANE), lambda c: (0, 0)), pl.BlockSpec((LANE, 8), lambda c: (0, 0))],
        out_specs=[pl.BlockSpec((L, SSD_INNER), lambda c: (c, 0)),
                   pl.BlockSpec((None, NPAIR, SSD_N, LANE), lambda c: (c, 0, 0, 0))],
        out_shape=[jax.ShapeDtypeStruct((S, SSD_INNER), F32), jax.ShapeDtypeStruct((SSD_NC, NPAIR, SSD_N, LANE), F32)],
        scratch_shapes=[pltpu.VMEM((NPAIR, SSD_N, LANE), F32)],
        compiler_params=pltpu.CompilerParams(dimension_semantics=("arbitrary",)),
    )(act, act, act, small, dtT, prow, pcol)


def _ssd_bwd(act, small, dtT, prow, pcol, states, dy):
    def body(x_ref, b_ref, c_ref, dt_ref, dtT_ref, prow_ref, pcol_ref, st_ref, dy_ref,
             dx_ref, ddt_ref, dp_ref, dstate):
        @pl.when(pl.program_id(0) == 0)
        def _():
            dstate[...] = jnp.zeros_like(dstate)
            dp_ref[...] = jnp.zeros_like(dp_ref)

        prow, causal, pre_c, dtc, a_row, cs_col, cs_row = _ssd_chunk_common(dt_ref, dtT_ref, prow_ref, pcol_ref)
        lane = lax.broadcasted_iota(jnp.int32, (SSD_L, LANE), 1)
        sub = lax.broadcasted_iota(jnp.int32, (LANE, SSD_L), 0)
        rowi = lax.broadcasted_iota(jnp.int32, (SSD_L, 1), 0)
        pick_p = lax.broadcasted_iota(jnp.int32, (LANE, LANE), 0)
        pick_l = lax.broadcasted_iota(jnp.int32, (LANE, LANE), 1)
        lo = lane < SSD_P
        lo1 = lo[0:1, :]
        dcs_c = jnp.zeros((SSD_L, LANE), F32)
        dcs_r = jnp.zeros((LANE, SSD_L), F32)
        ddt_x = jnp.zeros((SSD_L, LANE), F32)
        dd_row = jnp.zeros((1, LANE), F32)
        for g in range(SSD_G):
            bm = b_ref[:, g * SSD_N:(g + 1) * SSD_N]
            cm = c_ref[:, g * SSD_N:(g + 1) * SSD_N]
            cb = _dot(cm, bm, ((1,), (1,)))
            dcb = jnp.zeros((SSD_L, SSD_L), F32)
            dbm = jnp.zeros((SSD_L, SSD_N), F32)
            dcm = jnp.zeros((SSD_L, SSD_N), F32)
            for qq in range(PAIRS_PER_GROUP):
                q = g * PAIRS_PER_GROUP + qq
                ha, hb = 2 * q, 2 * q + 1
                csa, csb = cs_col[:, ha:ha + 1], cs_col[:, hb:hb + 1]
                xp = x_ref[:, q * LANE:(q + 1) * LANE]
                dtp = jnp.where(lo, dtc[:, ha:ha + 1], dtc[:, hb:hb + 1])
                xx = xp * dtp
                lma = jnp.exp(jnp.where(causal, csa - cs_row[ha:ha + 1, :], NEG))
                lmb = jnp.exp(jnp.where(causal, csb - cs_row[hb:hb + 1, :], NEG))
                ga, gb = cb * lma, cb * lmb
                dyp = dy_ref[:, q * LANE:(q + 1) * LANE]
                dya, dyb = jnp.where(lo, dyp, 0.0), jnp.where(lo, 0.0, dyp)
                s_in = st_ref[q]
                ds_out = dstate[q]
                la, lb = csa[SSD_L - 1:SSD_L, :], csb[SSD_L - 1:SSD_L, :]
                ecs = jnp.where(lo, jnp.exp(csa), jnp.exp(csb))
                decay = jnp.where(lo, jnp.exp(la - csa), jnp.exp(lb - csb))
                cd = jnp.where(lo1, jnp.exp(la), jnp.exp(lb))
                bds = _dot(bm, ds_out, ((1,), (0,)))
                dxx = _dot(ga, dya, ((0,), (0,))) + _dot(gb, dyb, ((0,), (0,))) + bds * decay
                dga = _dot(dya, xx, ((1,), (1,)))
                dgb = _dot(dyb, xx, ((1,), (1,)))
                dsega, dsegb = dga * ga, dgb * gb
                dcb = dcb + dga * lma + dgb * lmb
                yoff = _dot(cm, s_in, ((1,), (0,))) * ecs
                dye = dyp * ecs
                dcm = dcm + _dot(dye, s_in, ((1,), (1,)))
                xd = xx * decay
                dbm = dbm + _dot(xd, ds_out, ((1,), (1,)))
                wv = xd * bds
                ends = jnp.sum(wv, axis=0, keepdims=True) + cd * jnp.sum(ds_out * s_in, axis=0, keepdims=True)
                t1 = dyp * yoff - wv + jnp.where(rowi == SSD_L - 1, ends, 0.0)
                to_pair = (((pick_p < SSD_P) & (pick_l == ha)) | ((pick_p >= SSD_P) & (pick_l == hb))).astype(BF16)
                to_a_b = jnp.concatenate([(pick_l == ha).astype(BF16), (pick_l == hb).astype(BF16)], axis=0)
                dcs_c = dcs_c + _dot2(t1, to_pair) + _dot2(jnp.concatenate([dsega, dsegb], axis=1), to_a_b)
                dcs_r = (dcs_r + jnp.where(sub == ha, jnp.sum(dsega, axis=0, keepdims=True), 0.0)
                         + jnp.where(sub == hb, jnp.sum(dsegb, axis=0, keepdims=True), 0.0))
                dstate[q] = _dot(cm, dye, ((0,), (0,))) + cd * ds_out
                dpair = jnp.where(lo1, prow[2:3, ha:ha + 1], prow[2:3, hb:hb + 1])
                dx_ref[:, q * LANE:(q + 1) * LANE] = dxx * dtp + dpair * dyp
                ddt_x = ddt_x + _dot2(dxx * xp, to_pair)
                dd_row = dd_row + jnp.sum(_dot2(dyp * xp, to_pair), axis=0, keepdims=True)
            dx_ref[:, SSD_INNER + g * SSD_N:SSD_INNER + (g + 1) * SSD_N] = dbm + _dot(dcb, cm, ((0,), (0,)))
            dx_ref[:, SSD_INNER + (SSD_G + g) * SSD_N:SSD_INNER + (SSD_G + g + 1) * SSD_N] = dcm + _dot(dcb, bm, ((1,), (0,)))
        ri = lax.broadcasted_iota(jnp.int32, (SSD_L, SSD_L), 0)
        ci = lax.broadcasted_iota(jnp.int32, (SSD_L, SSD_L), 1)
        da = _dot3((ri <= ci).astype(BF16), dcs_c, ((1,), (0,)), False)
        da = da - _dot3(dcs_r, causal.astype(BF16), ((1,), (0,)), True).T
        ddt = ddt_x + da * a_row
        ddt_raw = ddt * _sigmoid(pre_c)
        ddt_ref[...] = ddt_raw
        da_head = jnp.sum(da * dtc, axis=0, keepdims=True) * a_row
        dp_ref[0:1, :] += jnp.sum(ddt_raw, axis=0, keepdims=True)
        dp_ref[1:2, :] += da_head
        dp_ref[2:3, :] += dd_row

    L = SSD_L
    rev = SSD_NC - 1
    bc_cols = SSD_INNER // (SSD_G * SSD_N)
    return pl.pallas_call(
        body, name="ssd_bwd", grid=(SSD_NC,),
        in_specs=[pl.BlockSpec((L, SSD_INNER), lambda c: (rev - c, 0)),
                  pl.BlockSpec((L, SSD_G * SSD_N), lambda c: (rev - c, bc_cols)),
                  pl.BlockSpec((L, SSD_G * SSD_N), lambda c: (rev - c, bc_cols + 1)),
                  pl.BlockSpec((L, LANE), lambda c: (rev - c, SM_DT // LANE)),
                  pl.BlockSpec((LANE, L), lambda c: (0, rev - c)),
                  pl.BlockSpec((8, LANE), lambda c: (0, 0)), pl.BlockSpec((LANE, 8), lambda c: (0, 0)),
                  pl.BlockSpec((None, NPAIR, SSD_N, LANE), lambda c: (rev - c, 0, 0, 0)),
                  pl.BlockSpec((L, SSD_INNER), lambda c: (rev - c, 0))],
        out_specs=[pl.BlockSpec((L, SSD_XBC), lambda c: (rev - c, 0)),
                   pl.BlockSpec((L, LANE), lambda c: (rev - c, 0)),
                   pl.BlockSpec((8, LANE), lambda c: (0, 0))],
        out_shape=[jax.ShapeDtypeStruct((S, SSD_XBC), F32), jax.ShapeDtypeStruct((S, LANE), F32),
                   jax.ShapeDtypeStruct((8, LANE), F32)],
        scratch_shapes=[pltpu.VMEM((NPAIR, SSD_N, LANE), F32)],
        compiler_params=pltpu.CompilerParams(dimension_semantics=("arbitrary",)),
    )(act, act, act, small, dtT, prow, pcol, states, dy)


TQ = 256
TK = 256
FWD_TQ = 256
FWD_TK = 256


def _attn_fwd(qc, kc, v):
    TQ, TK = FWD_TQ, FWD_TK

    def body(q_ref, k_ref, v_ref, o_ref, lse_ref):
        i = pl.program_id(1)
        lo = lax.broadcasted_iota(jnp.int32, (TQ, LANE), 1) < VDIM
        lo_k = lax.broadcasted_iota(jnp.int32, (TK, LANE), 1) < VDIM
        row_minus_col = lax.broadcasted_iota(jnp.int32, (TQ, TK), 0) - lax.broadcasted_iota(jnp.int32, (TQ, TK), 1)
        qa, qb = q_ref[:, 0:LANE], q_ref[:, LANE:2 * LANE]

        def scores(kb):
            kk = k_ref[pl.ds(pl.multiple_of(kb * TK, TK), TK), :]
            return (_dot(qa, kk[:, 0:LANE], ((1,), (1,))) * ATT_SCALE_LOG2, _dot(qb, kk[:, LANE:2 * LANE], ((1,), (1,))) * ATT_SCALE_LOG2)

        def update(kb, sa, sb, stats):
            ma, la, mb, lb, acc = stats
            vv = v_ref[pl.ds(pl.multiple_of(kb * TK, TK), TK), :]
            na = jnp.maximum(ma, jnp.max(sa, axis=1, keepdims=True))
            nb = jnp.maximum(mb, jnp.max(sb, axis=1, keepdims=True))
            pa, pb = jnp.exp2(sa - na), jnp.exp2(sb - nb)
            fa, fb = jnp.exp2(ma - na), jnp.exp2(mb - nb)
            la = fa * la + jnp.sum(pa, axis=1, keepdims=True)
            lb = fb * lb + jnp.sum(pb, axis=1, keepdims=True)
            acc = (acc * jnp.where(lo, fa, fb) + _dot(pa, jnp.where(lo_k, vv, 0), ((1,), (0,)))
                   + _dot(pb, jnp.where(lo_k, 0, vv), ((1,), (0,))))
            return na, la, nb, lb, acc

        def step(kb, carry):
            sa, sb = carry[:2]
            nxt = scores(kb + 1)
            return nxt + update(kb, sa, sb, carry[2:])

        neg = jnp.full((TQ, 1), NEG, F32)
        zero = jnp.zeros((TQ, 1), F32)
        n_full = i * (TQ // TK)
        carry = lax.fori_loop(0, n_full, step, scores(0) + (neg, zero, neg, zero, jnp.zeros((TQ, LANE), F32)))
        s, stats = carry[:2], carry[2:]
        for d in range(TQ // TK):
            nxt = scores(n_full + d + 1) if d + 1 < TQ // TK else None
            sa, sb = (jnp.where(row_minus_col >= d * TK, t, NEG) for t in s)
            stats = update(n_full + d, sa, sb, stats)
            s = nxt
        ma, la, mb, lb, acc = stats
        o_ref[...] = acc / jnp.where(lo, la, lb)
        lse_ref[...] = jnp.where(lo, ma + jnp.log2(la), mb + jnp.log2(lb)) * LN2

    return pl.pallas_call(
        body, name="attn_fwd", grid=(NPAIR, S // TQ),
        in_specs=[pl.BlockSpec((TQ, 2 * LANE), lambda j, i: (i, j)), pl.BlockSpec((S, 2 * LANE), lambda j, i: (0, j)),
                  pl.BlockSpec((S, LANE), lambda j, i: (0, j))],
        out_specs=[pl.BlockSpec((TQ, LANE), lambda j, i: (i, j)), pl.BlockSpec((None, TQ, LANE), lambda j, i: (j, i, 0))],
        out_shape=[jax.ShapeDtypeStruct((S, H * VDIM), F32), jax.ShapeDtypeStruct((NPAIR, S, LANE), F32)],
        compiler_params=pltpu.CompilerParams(dimension_semantics=("parallel", "parallel")),
    )(qc, kc, v)


def _attn_rows(lse, o, do):
    def body(lse_ref, o_ref, do_ref, r_ref):
        lt = lse_ref[...].T * (1.0 / LN2)
        tt = (o_ref[...] * do_ref[...]).T
        r_ref[...] = jnp.zeros_like(r_ref)
        r_ref[0:1, :] = lt[0:1, :]
        r_ref[1:2, :] = lt[VDIM:VDIM + 1, :]
        r_ref[2:3, :] = jnp.sum(tt[0:VDIM, :], axis=0, keepdims=True)
        r_ref[3:4, :] = jnp.sum(tt[VDIM:LANE, :], axis=0, keepdims=True)

    tile = pl.BlockSpec((S, LANE), lambda j: (0, j))
    return pl.pallas_call(
        body, name="attn_rows", grid=(NPAIR,), in_specs=[pl.BlockSpec((None, S, LANE), lambda j: (j, 0, 0)), tile, tile],
        out_specs=pl.BlockSpec((None, 8, S), lambda j: (j, 0, 0)), out_shape=jax.ShapeDtypeStruct((NPAIR, 8, S), F32),
    )(lse, o, do)


def _attn_bwd(qc, kc, kct, v, do, rows):
    nq = S // TQ

    def body(q_ref, k_ref, kt_ref, v_ref, do_ref, r_ref, dqt_ref, dk_ref, dv_ref):
        kb = pl.program_id(1)

        @pl.when(kb == 0)
        def _():
            dqt_ref[...] = jnp.zeros_like(dqt_ref)

        lo = lax.broadcasted_iota(jnp.int32, (TK, LANE), 1) < VDIM
        q_minus_k = lax.broadcasted_iota(jnp.int32, (TK, TQ), 1) - lax.broadcasted_iota(jnp.int32, (TK, TQ), 0)
        vv = v_ref[...]
        kk = k_ref[...]

        def step(qi, carry):
            off = pl.multiple_of(qi * TQ, TQ)
            qq = q_ref[pl.ds(off, TQ), :]
            dd = do_ref[pl.ds(off, TQ), :].astype(BF16)
            rr = r_ref[:, pl.ds(off, TQ)]
            keep = q_minus_k >= (kb - qi) * TQ
            out = []
            for x in range(2):
                sel = lo if x == 0 else jnp.logical_not(lo)
                kx, qx = kk[:, x * LANE:(x + 1) * LANE], qq[:, x * LANE:(x + 1) * LANE]
                st = jnp.where(keep, _dot(kx, qx, ((1,), (1,))) * ATT_SCALE_LOG2, NEG)
                pt = jnp.exp2(st - rr[x:x + 1, :])
                dpt = _dot(jnp.where(sel, vv, 0), dd, ((1,), (1,)))
                dst = (pt * (dpt - rr[2 + x:3 + x, :]) * ATT_SCALE).astype(BF16)
                out.append(carry[x] + _dot(dst, qx, ((1,), (0,))))
                out.append(_dot(pt, jnp.where(sel, dd, 0), ((1,), (0,))))
                dqt_ref[x * LANE:(x + 1) * LANE, pl.ds(off, TQ)] += _dot(kt_ref[x * LANE:(x + 1) * LANE, :], dst, ((1,), (0,)))
            return out[0], out[2], carry[2] + out[1] + out[3]

        z = jnp.zeros((TK, LANE), F32)
        dka, dkb, dv = lax.fori_loop(kb, nq, step, (z, z, z))
        dk_ref[:, 0:LANE] = dka
        dk_ref[:, LANE:2 * LANE] = dkb
        dv_ref[...] = dv.astype(BF16)

    return pl.pallas_call(
        body, name="attn_bwd", grid=(NPAIR, S // TK),
        in_specs=[pl.BlockSpec((S, 2 * LANE), lambda j, k: (0, j)), pl.BlockSpec((TK, 2 * LANE), lambda j, k: (k, j)),
                  pl.BlockSpec((2 * LANE, TK), lambda j, k: (j, k)), pl.BlockSpec((TK, LANE), lambda j, k: (k, j)),
                  pl.BlockSpec((S, LANE), lambda j, k: (0, j)), pl.BlockSpec((None, 8, S), lambda j, k: (j, 0, 0))],
        out_specs=[pl.BlockSpec((2 * LANE, S), lambda j, k: (j, 0)), pl.BlockSpec((TK, 2 * LANE), lambda j, k: (k, j)),
                   pl.BlockSpec((TK, LANE), lambda j, k: (k, j))],
        out_shape=[jax.ShapeDtypeStruct((H * LANE, S), F32), jax.ShapeDtypeStruct((S, H * LANE), F32),
                   jax.ShapeDtypeStruct((S, H * VDIM), BF16)],
        compiler_params=pltpu.CompilerParams(dimension_semantics=("parallel", "arbitrary")),
    )(qc, kc, kct, v, do, rows)


_IN_Z, _IN_XBC, _IN_DT, _IN_Q, _IN_KV, _IN_KR = 0, 1024, 2560, 2576, 2960, 3216


PROJ_COLS = 512
SMALL_PAD = pl.cdiv(SMALL_W, PROJ_COLS) * PROJ_COLS


def _prep_in(w_in_t):
    dt = w_in_t.dtype
    return jnp.concatenate(
        [w_in_t[_IN_Q:_IN_KV], w_in_t[_IN_KV:_IN_KR], w_in_t[_IN_KR:IN_WIDTH], jnp.zeros((LANE - ROPE, D), dt),
         w_in_t[_IN_DT:_IN_Q], jnp.zeros((SMALL_PAD - SM_DT - H, D), dt)], axis=0)


def _proj_in(xb, w_in_t, w_small):
    nz, nx, ns = (_IN_XBC - _IN_Z) // PROJ_COLS, (_IN_DT - _IN_XBC) // PROJ_COLS, SMALL_PAD // PROJ_COLS

    dt_block, dt_at = divmod(SM_DT, PROJ_COLS)

    def body(x_ref, w_ref, ws_ref, z_ref, xbc_ref, sm_ref, dtt_ref):
        i = pl.program_id(0)

        def emit(w, o_ref):
            o_ref[...] = lax.dot_general(x_ref[...], w[...], (((1,), (1,)), ((), ())), preferred_element_type=F32)

        pl.when(i < nz)(lambda: emit(w_ref, z_ref))
        pl.when((i >= nz) & (i < nz + nx))(lambda: emit(w_ref, xbc_ref))
        pl.when(i >= nz + nx)(lambda: emit(ws_ref, sm_ref))

        @pl.when(i == nz + nx + dt_block)
        def _():
            dtt_ref[...] = sm_ref[:, dt_at:dt_at + LANE].T

    def blocks(first, count, rows):
        at = lambda i: jnp.clip(i - first, 0, count - 1)
        return pl.BlockSpec((PROJ_COLS, D), lambda i: (at(i), 0)) if rows else pl.BlockSpec((S, PROJ_COLS), lambda i: (0, at(i)))

    return pl.pallas_call(
        body, name="proj_in", grid=(nz + nx + ns,),
        in_specs=[pl.BlockSpec((S, D), lambda i: (0, 0)), blocks(0, nz + nx, True), blocks(nz + nx, ns, True)],
        out_specs=[blocks(0, nz, False), blocks(nz, nx, False), blocks(nz + nx, ns, False), pl.BlockSpec((LANE, S), lambda i: (0, 0))],
        out_shape=[jax.ShapeDtypeStruct((S, _IN_XBC - _IN_Z), F32), jax.ShapeDtypeStruct((S, _IN_DT - _IN_XBC), F32),
                   jax.ShapeDtypeStruct((S, SMALL_W), F32), jax.ShapeDtypeStruct((LANE, S), F32)],
    )(xb, w_in_t, w_small)


PART_COLS = 512


def _part_blocks(widths):
    first = [0]
    for w in widths:
        first.append(first[-1] + w // PART_COLS)

    def at(part):
        return lambda i: jnp.clip(i - first[part], 0, first[part + 1] - first[part] - 1)

    return first, at


def _mm_ta_stacked(parts, b, rows, name):
    n = b.shape[1]
    first, at = _part_blocks([a.shape[1] for a in parts])
    assert first[-1] == pl.cdiv(rows, PART_COLS)

    def body(*refs):
        b_ref, o_ref = refs[-2:]
        i = pl.program_id(0)
        for part, a_ref in enumerate(refs[:-2]):
            @pl.when((i >= first[part]) & (i < first[part + 1]))
            def _(a_ref=a_ref):
                o_ref[...] = lax.dot_general(a_ref[...], b_ref[...], (((0,), (0,)), ((), ())),
                                             preferred_element_type=F32).astype(BF16)

    return pl.pallas_call(
        body, name=name, grid=(first[-1],),
        in_specs=[pl.BlockSpec((S, PART_COLS), lambda i, at=at(part): (0, at(i))) for part in range(len(parts))]
        + [pl.BlockSpec((S, n), lambda i: (0, 0))],
        out_specs=pl.BlockSpec((PART_COLS, n), lambda i: (i, 0)), out_shape=jax.ShapeDtypeStruct((rows, n), BF16),
    )(*parts, b)


def _mm_tb_split(a, b, widths, name):
    k = a.shape[1]
    first, at = _part_blocks(widths)

    def body(a_ref, b_ref, *o_refs):
        i = pl.program_id(0)
        for part, o_ref in enumerate(o_refs):
            @pl.when((i >= first[part]) & (i < first[part + 1]))
            def _(o_ref=o_ref):
                o_ref[...] = lax.dot_general(a_ref[...], b_ref[...], (((1,), (1,)), ((), ())), preferred_element_type=F32)

    return pl.pallas_call(
        body, name=name, grid=(first[-1],),
        in_specs=[pl.BlockSpec((S, k), lambda i: (0, 0)), pl.BlockSpec((PART_COLS, k), lambda i: (i, 0))],
        out_specs=[pl.BlockSpec((S, PART_COLS), lambda i, at=at(part): (0, at(i))) for part in range(len(widths))],
        out_shape=[jax.ShapeDtypeStruct((S, w), F32) for w in widths],
    )(a, b)


def _prep_attn(w_qb, w_kvb):
    w_q = jnp.pad(w_qb.reshape(Q_RANK, H, NOPE + ROPE), ((0, 0), (0, 0), (0, LANE - NOPE - ROPE))).reshape(Q_RANK, H * LANE)
    kv3 = w_kvb.reshape(KV_RANK, H, NOPE + VDIM)
    w_k = jnp.pad(kv3[:, :, :NOPE], ((0, 0), (0, 0), (0, LANE - NOPE))).reshape(KV_RANK, H * LANE)
    w_v = kv3[:, :, NOPE:].reshape(KV_RANK, H * VDIM)
    return w_q, w_k, w_v


def _rope_tables(positions):
    inv_freq = 1.0 / (10000.0 ** (jnp.arange(0, ROPE, 2, dtype=F32) / ROPE))
    ang = positions.astype(F32).reshape(S, 1) * inv_freq
    cos, sin = jnp.cos(ang), jnp.sin(ang)
    cos_t = jnp.concatenate([jnp.ones((S, NOPE), F32), cos, cos, jnp.ones((S, LANE - NOPE - ROPE), F32)], axis=1)
    sin_t = jnp.concatenate([jnp.zeros((S, NOPE), F32), -sin, sin, jnp.zeros((S, LANE - NOPE - ROPE), F32)], axis=1)
    return cos_t, sin_t


def _local_step(x, p, positions, target, w_in, fetch, send, sp, started):
    w_in_t = w_in.reshape(IN_WIDTH, D)
    w_small = _prep_in(w_in_t)
    cos_t, sin_t = _rope_tables(positions)
    prow = jnp.zeros((8, LANE), F32).at[0, :H].set(sp["dt_bias"][0]).at[1, :H].set(sp["A_log"][0]).at[2, :H].set(sp["D"][0])
    pcol = prow.T

    xb, pb = (x + started).astype(BF16), p.astype(BF16)
    z, xbc, small, dt_t = _proj_in(xb, w_in_t, w_small)
    act = _conv_fwd(xbc, sp["conv_w"], sp["conv_b"])
    y, states = _ssd_fwd(act, small, dt_t, prow, pcol)
    y_ssd = _gate_norm_fwd(y, z, sp["ssd_norm"])
    gl = fetch("attn", y_ssd)
    w_q, w_k, w_v = _prep_attn(_from_cols(gl["w_qb"]), _from_cols(gl["w_kvb"]))
    qn, kvn, qcat, kcat, kcat_t, v = _qkv_fwd(small, w_q, w_k, w_v, sp["q_norm"], sp["kv_norm"], cos_t, sin_t)
    o, lse = _attn_fwd(qcat, kcat, v)
    y_mla = _rms_fwd(o, sp["out_norm"], name="out_norm_fwd")
    w_out = fetch("out", y_mla)["w_out"]
    w_out = w_out.reshape(2 * SSD_INNER, D)
    mix, h1, h1b = _out_proj_ln(y_ssd, y_mla, w_out, x, sp["ln_mix_g"], sp["ln_mix_b"])
    gl = fetch("ffn", h1b)
    w_pg, w_pp = gl["w_pg"].reshape(D, D), _from_cols(gl["w_pp"])
    w_gate, w_up, w_down = gl["w_gate"], gl["w_up"], gl["w_down"]
    gate, up, actf = _ffn_hidden_fwd(h1b, w_gate, w_up)
    ffn = _mm([(actf, w_down)], chunk="sum", name="ffn_down")
    dpre2, dpre2b, dpg, dpp, dg2, db2, loss_row = _final_fwd_bwd(h1, ffn, h1b, pb, w_pg, w_pp, target, sp["ln_ffn_g"], sp["ln_ffn_b"])

    g = {"ln_ffn_g": dg2, "ln_ffn_b": db2}
    g["w_pp"] = _to_cols(_mm([(pb, dpp)], ta=True, out_dtype=BF16, name="d_w_ple_proj"))
    g["w_pg"] = _mm([(h1b, dpg)], ta=True, out_dtype=BF16, name="d_w_ple_gate").reshape(NCHIP, D // NCHIP, D)
    g["w_down"] = _mm([(actf, dpre2b)], ta=True, chunk="out", out_dtype=BF16, name="d_w_down")
    dgate, dup = _ffn_hidden_bwd(dpre2b, w_down, gate, up)
    g["w_gate"] = _mm([(dgate, h1b)], ta=True, chunk="out", out_dtype=BF16, name="d_w_gate")
    g["w_up"] = _mm([(dup, h1b)], ta=True, chunk="out", out_dtype=BF16, name="d_w_up")
    sent = send("ffn", {name: g.pop(name) for name in dict(ASYNC_GROUPS)["ffn"]})
    dh1 = _mm([(dgate, w_gate), (dup, w_up), (dpg, w_pg.T)], chunk="sum", add=dpre2, add_scale=ALPHA, name="d_h1")
    dpre1, dpre1b, g["ln_mix_g"], g["ln_mix_b"] = _ln_bwd(x, mix, sp["ln_mix_g"] + sent, dh1)
    dy_ssd, dy_mla = _mm_tb_split(dpre1b, w_out, (SSD_INNER, SSD_INNER), "d_y")
    dw_out = _mm_ta_stacked((y_ssd, y_mla), dpre1b, 2 * SSD_INNER, "d_w_out")
    sent = send("out", {"w_out": dw_out.reshape(NCHIP, 2 * SSD_INNER // NCHIP, D)})
    do, g["out_norm"] = _rms_bwd(o, sp["out_norm"] + sent, dy_mla, name="out_norm_bwd")
    dqt, dk, dv = _attn_bwd(qcat, kcat, kcat_t, v, do, _attn_rows(lse, o, do))
    dlatent, dqlin, dkb, g["q_norm"], g["kv_norm"] = _qkv_bwd(dqt, dk, dv, small, w_q, w_k, w_v, sp["q_norm"], sp["kv_norm"], cos_t, sin_t)
    dw_q = _mm([(qn, dqlin)], ta=True, out_dtype=BF16, name="d_w_q")
    dw_k = _mm([(kvn, dkb)], ta=True, out_dtype=BF16, name="d_w_k")
    dw_v = _mm([(kvn, dv)], ta=True, out_dtype=BF16, name="d_w_v")
    dw_qb = _to_cols(dw_q.reshape(Q_RANK, H, LANE)[:, :, :NOPE + ROPE].reshape(Q_RANK, H * (NOPE + ROPE)))
    dw_kvb = _to_cols(jnp.concatenate([dw_k.reshape(KV_RANK, H, LANE)[:, :, :NOPE], dw_v.reshape(KV_RANK, H, VDIM)],
                                       axis=2).reshape(KV_RANK, H * (NOPE + VDIM)))
    sent = send("attn", {"w_qb": dw_qb, "w_kvb": dw_kvb})
    dy, dz, g["ssd_norm"] = _gate_norm_bwd(y, z, sp["ssd_norm"] + sent, dy_ssd)
    dact, ddt, dprow = _ssd_bwd(act, small, dt_t, prow, pcol, states, dy)
    g["dt_bias"], g["A_log"], g["D"] = dprow[0:1, :H], dprow[1:2, :H], dprow[2:3, :H]
    dxbc, g["conv_w"], g["conv_b"] = _conv_bwd(xbc, sp["conv_w"], sp["conv_b"], dact)
    dsmall = jnp.concatenate([dlatent, ddt.astype(BF16)], axis=1)
    in_blocks = [(d, w_in_t, (k, first // PROJ_COLS + k, PROJ_COLS))
                 for d, first in ((dz, _IN_Z), (dxbc, _IN_XBC)) for k in range(d.shape[1] // PROJ_COLS)]
    grad_x = _mm(in_blocks + [(dsmall, w_small, (0, 0, SMALL_W))], add=dpre1, add_scale=ALPHA, name="d_x")
    sent = send("small", dict(g, loss=loss_row))
    n_small = IN_WIDTH - _IN_DT
    dsm = jnp.concatenate([(ddt[:, :H] + sent).astype(BF16), dlatent[:, :n_small - H], jnp.zeros((S, D - n_small), BF16)], axis=1)
    dw_in = _mm_ta_stacked((dz, dxbc, dsm), xb, IN_WIDTH, "d_w_in").reshape(NCHIP, IN_WIDTH // NCHIP * D // LANE, LANE)
    return loss_row, grad_x, dw_in, g


MESH = pl.DeviceIdType.MESH
BIG = (("w_in", (D, IN_WIDTH), 1), ("w_qb", (Q_RANK, H * (NOPE + ROPE)), 1), ("w_kvb", (KV_RANK, H * (NOPE + VDIM)), 1),
       ("w_out", (2 * SSD_INNER, D), 0), ("w_gate", (D, D_FF), 1), ("w_up", (D, D_FF), 1), ("w_down", (D_FF, D), 0),
       ("w_pg", (D, D), 0), ("w_pp", (PLE, D), 1))
CONV_SHARD = SSD_XBC // NCHIP
BF16_ROWS = 16


def _from_cols(stack):
    return jnp.concatenate([stack[k] for k in range(NCHIP)], axis=1)


def _to_cols(full):
    r, c4 = full.shape
    return full.reshape(r, NCHIP, c4 // NCHIP).transpose(1, 0, 2)


def _coords():
    return lax.axis_index("x"), lax.axis_index("y"), lax.axis_index("c")


def _peers():
    x, y, c = _coords()
    return 2 * x + y, c, [(1 - x, y), (x, 1 - y), (1 - x, 1 - y)], (x, y, 1 - c)


def _half_axis(shape):
    return 0 if shape[-2] % (2 * BF16_ROWS) == 0 else 1


def _half_shape(shape):
    r, c = shape[-2:]
    return (r // 2, c) if _half_axis(shape) == 0 else (r, c // 2)


def _half(core, shape):
    r, c = shape[-2:]
    if _half_axis(shape) == 0:
        return pl.ds(pl.multiple_of(core * (r // 2), BF16_ROWS), r // 2), slice(None)
    return slice(None), pl.ds(pl.multiple_of(core * (c // 2), LANE), c // 2)


def _gather_weights(shards):
    n_arr = len(shards)
    per = 2 * (NCHIP - 1)

    def body(*refs):
        ins, outs = refs[:n_arr], refs[n_arr:2 * n_arr]
        send_sems, recv_sems, local_sems = refs[2 * n_arr:]
        k, c, chips, sibling = _peers()

        def copy(idx, src, dst, to):
            return pltpu.make_async_remote_copy(src_ref=src, dst_ref=dst, send_sem=send_sems.at[idx], recv_sem=recv_sems.at[idx],
                                                device_id=to, device_id_type=MESH)

        def part(a, chip, core):
            return outs[a].at[chip, *_half(core, shards[a].shape)]

        mine = [pltpu.make_async_copy(ins[a], outs[a].at[k], local_sems.at[a]) for a in range(n_arr)]
        for cp in mine:
            cp.start()
        sends = []
        for a in range(n_arr):
            for j, (cx, cy) in enumerate(chips):
                sends.append(copy(per * a + j, ins[a].at[*_half(c, shards[a].shape)], part(a, k, c), (cx, cy, c)))
                sends[-1].start()
        for j, (cx, cy) in enumerate(chips):
            for a in range(n_arr):
                landed = part(a, 2 * cx + cy, c)
                copy(per * a + j, landed, landed, (cx, cy, c)).wait_recv()
                sends.append(copy(per * a + NCHIP - 1 + j, landed, landed, sibling))
                sends[-1].start()
        for j, (cx, cy) in enumerate(chips):
            for a in range(n_arr):
                other = part(a, 2 * cx + cy, 1 - c)
                copy(per * a + NCHIP - 1 + j, other, other, sibling).wait_recv()
        for cp in sends:
            cp.wait_send()
        for cp in mine:
            cp.wait()

    any_spec = pl.BlockSpec(memory_space=pl.ANY)
    return pl.pallas_call(
        body, name="gather_weights", in_specs=[any_spec] * n_arr, out_specs=[any_spec] * n_arr,
        out_shape=[jax.ShapeDtypeStruct((NCHIP,) + s.shape, s.dtype) for s in shards],
        scratch_shapes=[pltpu.SemaphoreType.DMA((per * n_arr,)), pltpu.SemaphoreType.DMA((per * n_arr,)),
                        pltpu.SemaphoreType.DMA((n_arr,))],
    )(*shards)


ASYNC_GROUPS = (("attn", ("w_qb", "w_kvb")), ("out", ("w_out",)), ("ffn", ("w_gate", "w_up", "w_down", "w_pg", "w_pp")))
TRANSPOSED = ("w_in", "w_gate", "w_up")
ROW_MAJOR = ("w_in",)
HBM_SPEC = pl.BlockSpec(memory_space=pltpu.HBM)
SEM_SPEC = pl.BlockSpec(memory_space=pltpu.SEMAPHORE)
IN_FLIGHT = pltpu.SideEffectType.DATAFLOW_SIDE_EFFECTING


def _in_hbm(a):
    return pltpu.with_memory_space_constraint(a, pltpu.HBM)


def _hbm_like(arrs, lead=()):
    return [pltpu.HBM(lead + a.shape, a.dtype) for a in arrs]


def _split_start(name, srcs, lands, after, n_sem, start):
    n = len(srcs)
    order = [] if after is None else [after]

    def body(*refs):
        src_refs, land_refs = refs[:n], refs[n:2 * n]
        send_sems, recv_sems = refs[2 * n + len(order)], refs[2 * n + len(order) + 1]
        token = refs[-1]

        def copy(send_idx, recv_idx, src, dst, to):
            return pltpu.make_async_remote_copy(src_ref=src, dst_ref=dst, send_sem=send_sems.at[send_idx],
                                                recv_sem=recv_sems.at[recv_idx], device_id=to, device_id_type=MESH)

        for cp in start(src_refs, land_refs, copy):
            cp.start()
        token[...] = jnp.zeros_like(token)

    sem = pltpu.SemaphoreType.DMA((n_sem,))
    outs = pl.pallas_call(
        body, name=name, in_specs=[HBM_SPEC] * (2 * n) + [pl.BlockSpec(memory_space=pl.ANY)] * len(order),
        out_specs=[SEM_SPEC, SEM_SPEC] + [HBM_SPEC] * (2 * n) + [pl.BlockSpec(memory_space=pltpu.VMEM)],
        out_shape=[sem, sem] + _hbm_like(srcs) + _hbm_like(lands) + [jax.ShapeDtypeStruct((8, LANE), F32)],
        input_output_aliases={i: 2 + i for i in range(2 * n)},
        compiler_params=pltpu.CompilerParams(has_side_effects=IN_FLIGHT),
    )(*[_in_hbm(a) for a in srcs], *[_in_hbm(a) for a in lands], *order)
    return (outs[0], outs[1], outs[2:2 + n], outs[2 + n:2 + 2 * n]), outs[-1]


def _split_wait(name, send_sems, recv_sems, srcs, lands, after, waits):
    n = len(srcs)

    def body(*refs):
        src_refs, land_refs = refs[:n], refs[n:2 * n]
        send_ref, recv_ref = refs[2 * n], refs[2 * n + 1]

        def copy(send_idx, recv_idx, src, dst, to):
            return pltpu.make_async_remote_copy(src_ref=src, dst_ref=dst, send_sem=send_ref.at[send_idx],
                                                recv_sem=recv_ref.at[recv_idx], device_id=to, device_id_type=MESH)

        for cp in waits(src_refs, land_refs, copy):
            cp.wait_send()
            cp.wait_recv()

    outs = pl.pallas_call(
        body, name=name, in_specs=[HBM_SPEC] * (2 * n) + [SEM_SPEC, SEM_SPEC, pl.BlockSpec(memory_space=pl.ANY)],
        out_specs=[HBM_SPEC] * (2 * n), out_shape=_hbm_like(srcs) + _hbm_like(lands),
        input_output_aliases={i: i for i in range(2 * n)},
        compiler_params=pltpu.CompilerParams(has_side_effects=IN_FLIGHT),
    )(*srcs, *lands, send_sems, recv_sems, after)
    return outs[:n], outs[n:]


GATHER_LATE_SEMS = 2 * (NCHIP - 1)


def _gather_async_start(tag, shards, after):
    def start(srcs, lands, copy):
        k, c, chips, _ = _peers()
        out = []
        for a, (src, dst) in enumerate(zip(srcs, lands)):
            for j, (cx, cy) in enumerate(chips):
                for core in range(2):
                    out.append(copy(GATHER_LATE_SEMS * a + 2 * j + core, GATHER_LATE_SEMS * a + 2 * j + c,
                                    src.at[*_half(c, src.shape)], dst.at[k, *_half(c, src.shape)], (cx, cy, core)))
        return out

    chip = 2 * lax.axis_index("x") + lax.axis_index("y")
    lands = [lax.dynamic_update_slice(lax.empty((NCHIP,) + s.shape, s.dtype), s[None], (chip, 0, 0)) for s in shards]
    return _split_start("gather_%s_start" % tag, shards, lands, after, GATHER_LATE_SEMS * len(shards), start)


def _gather_async_wait(tag, send_sems, recv_sems, shards, lands, after):
    def waits(srcs, lands_, copy):
        _, c, chips, _ = _peers()
        out = []
        for a, (src, dst) in enumerate(zip(srcs, lands_)):
            for j, (cx, cy) in enumerate(chips):
                for core in range(2):
                    idx = GATHER_LATE_SEMS * a + 2 * j + core
                    out.append(copy(idx, idx, src.at[*_half(c, src.shape)], dst.at[2 * cx + cy, *_half(core, src.shape)], (cx, cy, core)))
        return out

    return _split_wait("gather_%s_wait" % tag, send_sems, recv_sems, shards, lands, after, waits)[1]


def _other_devices():
    x, y, c = _coords()
    out = []
    for d in range(1, NDEV):
        tx, ty, tc = x ^ (d >> 2), y ^ ((d >> 1) & 1), c ^ (d & 1)
        out.append((d, (tx, ty, tc), 2 * tx + ty, 4 * tx + 2 * ty + tc))
    return out


def _reduce_async_start(tag, stacks, after):
    def start(srcs, lands, copy):
        x, y, c = _coords()
        me = 4 * x + 2 * y + c
        return [copy((NDEV - 1) * a + d - 1, (NDEV - 1) * a + d - 1, src.at[chip, *_half(to[2], src.shape)], dst.at[me], to)
                for a, (src, dst) in enumerate(zip(srcs, lands)) for d, to, chip, _ in _other_devices()]

    x, y, c = _coords()
    lands = []
    for s in stacks:
        hr, hc = _half_shape(s.shape)
        at = (c * hr, 0) if _half_axis(s.shape) == 0 else (0, c * hc)
        own = lax.dynamic_slice(s, (2 * x + y,) + at, (1, hr, hc))
        lands.append(lax.dynamic_update_slice(lax.empty((NDEV, hr, hc), s.dtype), own, (4 * x + 2 * y + c, 0, 0)))
    return _split_start("reduce_%s_start" % tag, stacks, lands, after, (NDEV - 1) * len(stacks), start)


def _reduce_async_wait(tag, send_sems, recv_sems, stacks, lands, after):
    def waits(srcs, lands_, copy):
        return [copy((NDEV - 1) * a + d - 1, (NDEV - 1) * a + d - 1, src.at[chip, *_half(to[2], src.shape)], dst.at[pos], to)
                for a, (src, dst) in enumerate(zip(srcs, lands_)) for d, to, chip, pos in _other_devices()]

    return _split_wait("reduce_%s_wait" % tag, send_sems, recv_sems, stacks, lands, after, waits)[1]


def _reduce_finish(tag, arrived, dims):
    n_arr = len(arrived)

    def body(*refs):
        lands, fin = refs[:n_arr], refs[n_arr:2 * n_arr]
        send_sems, recv_sems = refs[2 * n_arr:]
        _, c, _, sibling = _peers()
        sends = []
        for a in range(n_arr):
            mine = fin[a].at[*_half(c, dims[a])]

            def device_sum(vs, vf, a=a, mine=mine):
                pltpu.sync_copy(lands[a], vs)
                acc = vs[0].astype(F32)
                for i in range(1, NDEV):
                    acc = acc + vs[i].astype(F32)
                vf[...] = acc
                pltpu.sync_copy(vf, mine)

            pl.run_scoped(device_sum, pltpu.VMEM((NDEV,) + _half_shape(dims[a]), BF16), pltpu.VMEM(_half_shape(dims[a]), F32))
            sends.append(pltpu.make_async_remote_copy(src_ref=mine, dst_ref=mine, send_sem=send_sems.at[a], recv_sem=recv_sems.at[a],
                                                      device_id=sibling, device_id_type=MESH))
            sends[-1].start()
        for a in range(n_arr):
            other = fin[a].at[*_half(1 - c, dims[a])]
            pltpu.make_async_remote_copy(src_ref=other, dst_ref=other, send_sem=send_sems.at[a], recv_sem=recv_sems.at[a],
                                         device_id=sibling, device_id_type=MESH).wait_recv()
        for cp in sends:
            cp.wait_send()

    any_spec = pl.BlockSpec(memory_space=pl.ANY)
    return pl.pallas_call(
        body, name="reduce_%s_finish" % tag, in_specs=[any_spec] * n_arr, out_specs=[any_spec] * n_arr,
        out_shape=[jax.ShapeDtypeStruct(d, F32) for d in dims],
        scratch_shapes=[pltpu.SemaphoreType.DMA((n_arr,)), pltpu.SemaphoreType.DMA((n_arr,))],
    )(*arrived)


SMALL = (("conv_w", SSD_K * SSD_XBC), ("conv_b", SSD_XBC), ("dt_bias", H), ("A_log", H), ("D", H), ("ssd_norm", SSD_INNER),
         ("q_norm", Q_RANK), ("kv_norm", KV_RANK), ("out_norm", SSD_INNER), ("ln_mix_g", D), ("ln_mix_b", D),
         ("ln_ffn_g", D), ("ln_ffn_b", D))
SMALL_ROWS = 120
NDEV = 8


def _allreduce_small_start(sv):
    def start(srcs, lands, copy):
        x, y, c = _coords()
        return [copy(d - 1, d - 1, srcs[0], lands[0].at[4 * x + 2 * y + c], to) for d, to, _, _ in _other_devices()]

    x, y, c = _coords()
    slots = lax.dynamic_update_slice(lax.empty((NDEV,) + sv.shape, sv.dtype), sv[None], (4 * x + 2 * y + c, 0, 0))
    return _split_start("allreduce_small_start", [sv], [slots], None, NDEV - 1, start)


def _allreduce_small_wait(send_sems, recv_sems, srcs, lands, after):
    def waits(srcs_, lands_, copy):
        return [copy(d - 1, d - 1, srcs_[0], lands_[0].at[pos], to) for d, to, _, pos in _other_devices()]

    def device_sum(slots_ref, out_ref):
        acc = slots_ref[0]
        for i in range(1, NDEV):
            acc = acc + slots_ref[i]
        out_ref[...] = acc

    slots = _split_wait("allreduce_small_wait", send_sems, recv_sems, srcs, lands, after, waits)[1][0]
    vm = pl.BlockSpec(memory_space=pltpu.VMEM)
    return pl.pallas_call(device_sum, name="allreduce_small_sum", in_specs=[vm], out_specs=vm,
                          out_shape=jax.ShapeDtypeStruct(slots.shape[1:], slots.dtype))(slots)


def _adamw_math(w, g, m, v):
    m2 = ADAM_B1 * m + (1.0 - ADAM_B1) * g
    v2 = ADAM_B2 * v + (1.0 - ADAM_B2) * (g * g)
    m_hat = m2 / (1.0 - ADAM_B1 ** ADAM_STEP)
    v_hat = v2 / (1.0 - ADAM_B2 ** ADAM_STEP)
    return -ADAM_LR * (m_hat / (jnp.sqrt(v_hat) + ADAM_EPS) + ADAM_WD * w), m2, v2


ADAM_BLOCK_BYTES = 2 * 1024 * 1024


def _adamw_big(w, g, m, v, *, name):
    r, c = w.shape

    def body(w_ref, g_ref, m_ref, v_ref, d_ref, m2_ref, v2_ref):
        d_ref[...], m2_ref[...], v2_ref[...] = _adamw_math(w_ref[...], g_ref[...], m_ref[...], v_ref[...])

    tr = max(t for t in range(8, r + 1, 8) if r % t == 0 and t * c * 4 <= ADAM_BLOCK_BYTES)
    steps, spec = r // tr, pl.BlockSpec((tr, c), lambda i: (i, 0))
    return pl.pallas_call(body, name=name, grid=(steps,), in_specs=[spec] * 4, out_specs=[spec] * 3,
                          out_shape=[jax.ShapeDtypeStruct((r, c), F32)] * 3)(w, g, m, v)


def _adamw_small(ws, gs, ms, vs):
    n = len(ws)

    def body(*refs):
        for i in range(n):
            w_ref, g_ref, m_ref, v_ref = (refs[j * n + i] for j in range(4))
            d_ref, m2_ref, v2_ref = (refs[(4 + j) * n + i] for j in range(3))
            d_ref[...], m2_ref[...], v2_ref[...] = _adamw_math(w_ref[...], g_ref[...], m_ref[...], v_ref[...])

    vm = pl.BlockSpec(memory_space=pltpu.VMEM)
    shapes = [jax.ShapeDtypeStruct(w.shape, F32) for w in ws]
    outs = pl.pallas_call(body, name="adamw_small", in_specs=[vm] * (4 * n), out_specs=[vm] * (3 * n), out_shape=shapes * 3)(
        *ws, *gs, *ms, *vs)
    return outs[:n], outs[n:2 * n], outs[2 * n:]


_SMALL_ARG = {"conv_w": "ssd_conv_w", "conv_b": "ssd_conv_b", "dt_bias": "ssd_dt_bias", "A_log": "ssd_A_log", "D": "ssd_D",
              "ssd_norm": "ssd_norm_w", "q_norm": "mla_q_norm_w", "kv_norm": "mla_kv_norm_w", "out_norm": "mla_out_norm_w",
              "ln_mix_g": "ln_mix_g", "ln_mix_b": "ln_mix_b", "ln_ffn_g": "ln_ffn_g", "ln_ffn_b": "ln_ffn_b"}
_BIG_ARG = {"w_in": "w_in", "w_qb": "mla_w_q_b", "w_kvb": "mla_w_kv_b", "w_out": "w_out", "w_gate": "w_ffn_gate",
            "w_up": "w_ffn_up", "w_down": "w_ffn_down", "w_pg": "w_ple_gate", "w_pp": "w_ple_proj"}
_WEIGHT_ORDER = ("w_in", "ssd_conv_w", "ssd_conv_b", "ssd_dt_bias", "ssd_A_log", "ssd_D", "ssd_norm_w", "mla_q_norm_w", "mla_w_q_b",
                 "mla_kv_norm_w", "mla_w_kv_b", "mla_out_norm_w", "w_out", "ln_mix_g", "ln_mix_b", "w_ffn_gate", "w_ffn_up",
                 "w_ffn_down", "w_ple_gate", "w_ple_proj", "ln_ffn_g", "ln_ffn_b")


def _rows128(a):
    flat = a.reshape(-1)
    return jnp.pad(flat, (0, -flat.shape[0] % LANE)).reshape(-1, LANE)


def kernel(x, p, positions, w_in, ssd_conv_w, ssd_conv_b, ssd_dt_bias, ssd_A_log, ssd_D, ssd_norm_w, mla_q_norm_w, mla_w_q_b, mla_kv_norm_w, mla_w_kv_b, mla_out_norm_w, w_out, ln_mix_g, ln_mix_b, w_ffn_gate, w_ffn_up, w_ffn_down, w_ple_gate, w_ple_proj, ln_ffn_g, ln_ffn_b, loss_target, m_w_in, m_ssd_conv_w, m_ssd_conv_b, m_ssd_dt_bias, m_ssd_A_log, m_ssd_D, m_ssd_norm_w, m_mla_q_norm_w, m_mla_w_q_b, m_mla_kv_norm_w, m_mla_w_kv_b, m_mla_out_norm_w, m_w_out, m_ln_mix_g, m_ln_mix_b, m_w_ffn_gate, m_w_ffn_up, m_w_ffn_down, m_w_ple_gate, m_w_ple_proj, m_ln_ffn_g, m_ln_ffn_b, v_w_in, v_ssd_conv_w, v_ssd_conv_b, v_ssd_dt_bias, v_ssd_A_log, v_ssd_D, v_ssd_norm_w, v_mla_q_norm_w, v_mla_w_q_b, v_mla_kv_norm_w, v_mla_w_kv_b, v_mla_out_norm_w, v_w_out, v_ln_mix_g, v_ln_mix_b, v_w_ffn_gate, v_w_ffn_up, v_w_ffn_down, v_w_ple_gate, v_w_ple_proj, v_ln_ffn_g, v_ln_ffn_b):
    given = dict(locals())
    chip = 2 * lax.axis_index("x") + lax.axis_index("y")

    def local(name, prefix=""):
        a = given[prefix + _BIG_ARG[name]][0]
        return a.T if name in TRANSPOSED else a

    def updated(name, prefix=""):
        if name in ROW_MAJOR:
            _, c, r = given[prefix + _BIG_ARG[name]].shape
            return given[prefix + _BIG_ARG[name]].reshape(c // LANE, LANE, r).transpose(2, 0, 1).reshape(-1, LANE)
        return local(name, prefix)

    def global_layout(name, arr):
        if name in ROW_MAJOR:
            r, c = local(name).shape
            return arr.reshape(r, c // LANE, LANE).transpose(1, 2, 0).reshape(1, c, r)
        return (arr.T if name in TRANSPOSED else arr)[None]

    conv_bits = lax.bitcast_convert_type(ssd_conv_w[0], BF16).reshape(SSD_K, 2 * CONV_SHARD)
    w_in_all, conv_all = _gather_weights([local("w_in").astype(BF16), jnp.pad(conv_bits, ((0, BF16_ROWS - SSD_K), (0, 0)))])
    sp = {k: given[a] for k, a in _SMALL_ARG.items() if k != "conv_w"}
    sp["conv_w"] = _from_cols(lax.bitcast_convert_type(conv_all[:, :SSD_K].reshape(NCHIP, SSD_K, CONV_SHARD, 2), F32))
    gathering, tie = {}, w_in_all
    for group, names in ASYNC_GROUPS:
        gathering[group], tie = _gather_async_start(group, [local(name).astype(BF16) for name in names], tie)

    def fetch(group, after):
        return dict(zip(dict(ASYNC_GROUPS)[group], _gather_async_wait(group, *gathering[group], after)))

    reducing = {}

    def send(group, grads):
        if group == "small":
            rows = jnp.concatenate([_rows128(grads[name]) for name, _ in SMALL] + [grads["loss"]], axis=0)
            reducing[group], sent = _allreduce_small_start(jnp.pad(rows, ((0, SMALL_ROWS - rows.shape[0]), (0, 0))))
        else:
            reducing[group], sent = _reduce_async_start(group, [grads[name] for name in dict(ASYNC_GROUPS)[group]], None)
        return sent[0, 0]

    loss_row, grad_x, dw_in, g = _local_step(x[0], p[0, 0], positions[0], loss_target[0], w_in_all, fetch, send, sp, tie[0, 0])

    reducing["in"], tie = _reduce_async_start("in", [dw_in], grad_x)
    gbig = {}
    for group, names in reversed(ASYNC_GROUPS):
        arrived = _reduce_async_wait(group, *reducing[group], tie)
        gbig.update(zip(names, _reduce_finish(group, arrived, [local(name).shape for name in names])))
    small_sum = _allreduce_small_wait(*reducing.pop("small"), tie)
    gsmall, row = {}, 0
    for name, size in SMALL:
        nrow = -(-size // LANE)
        gsmall[name] = small_sum[row:row + nrow].reshape(-1)[:size]
        row += nrow
    loss = small_sum[row, 0]

    grads = {_BIG_ARG[name]: global_layout(name, arr) for name, arr in gbig.items()}
    for name, _ in SMALL:
        if name == "conv_w":
            full_g = gsmall[name].reshape(SSD_K, SSD_XBC)
            grads["ssd_conv_w"] = lax.dynamic_slice(full_g, (0, chip * CONV_SHARD), (SSD_K, CONV_SHARD))[None]
        else:
            grads[_SMALL_ARG[name]] = gsmall[name].reshape(given[_SMALL_ARG[name]].shape)

    delta, new_m, new_v = {}, {}, {}

    def update_matrix(name, grad):
        a = _BIG_ARG[name]
        d, m2, v2 = _adamw_big(updated(name), grad, updated(name, "m_"), updated(name, "v_"), name="adamw_" + a)
        delta[a], new_m[a], new_v[a] = (global_layout(name, t) for t in (d, m2, v2))
        return d

    for name, grad in gbig.items():
        last = update_matrix(name, grad)
    g_in = _reduce_finish("in", _reduce_async_wait("in", *reducing["in"], last), [updated("w_in").shape])[0]
    grads["w_in"] = global_layout("w_in", g_in)
    update_matrix("w_in", g_in)
    small_names = [_SMALL_ARG[name] for name, _ in SMALL]
    two_d = lambda t: t.reshape(t.shape[-2], t.shape[-1])
    ds, ms, vs = _adamw_small([two_d(given[a]) for a in small_names], [two_d(grads[a]) for a in small_names],
                              [two_d(given["m_" + a]) for a in small_names], [two_d(given["v_" + a]) for a in small_names])
    for a, d, m2, v2 in zip(small_names, ds, ms, vs):
        delta[a], new_m[a], new_v[a] = (t.reshape(given[a].shape) for t in (d, m2, v2))

    return (loss, grad_x[None], *[grads[n] for n in _WEIGHT_ORDER], *[delta[n] for n in _WEIGHT_ORDER],
            *[new_m[n] for n in _WEIGHT_ORDER], *[new_v[n] for n in _WEIGHT_ORDER])
```

```python
import functools
import math

import jax
import jax.numpy as jnp
from jax import lax
from jax.experimental import pallas as pl
from jax.experimental.pallas import tpu as pltpu

F32 = jnp.float32
BF16 = jnp.bfloat16

S = 2048
D = 1024
PLE = 256
H = 16
SSD_P = 64
SSD_INNER = 1024
SSD_N = 128
SSD_G = 2
SSD_L = 128
SSD_NC = S // SSD_L
SSD_XBC = 1536
SSD_K = 4
Q_RANK = 384
KV_RANK = 256
NOPE = 64
ROPE = 32
VDIM = 64
D_FF = 2816
IN_WIDTH = 3248
ALPHA = 2.0 ** 0.25
EPS_RMS = 1e-6
EPS_LN = 1e-5
ATT_SCALE = 1.0 / math.sqrt(NOPE + ROPE)
LN2 = math.log(2.0)
ATT_SCALE_LOG2 = ATT_SCALE / LN2
LANE = 128
NCHIP = 4
SMALL_W = 896
SM_Q, SM_KV, SM_KR, SM_DT = 0, 384, 640, 768
NEG = -1e30

ADAM_LR = 0.001
ADAM_B1 = 0.9
ADAM_B2 = 0.999
ADAM_EPS = 1e-08
ADAM_WD = 0.01
ADAM_STEP = 10


def _sigmoid(v):
    return 1.0 / (1.0 + jnp.exp(-v))


MM_VMEM_BUDGET = 36 * 2 ** 20
MM_MAX_ACC = 2048 * 1024


def _mm_tiles(pairs, ks, m, n, out_dtype, has_add):
    def divs(v):
        return [LANE * d for d in range(v // LANE, 0, -1) if (v // LANE) % d == 0] if v % LANE == 0 else [v]

    def cost(tm, tn):
        tot = tm * tn * (jnp.dtype(out_dtype).itemsize + (4 if has_add else 0))
        for (a, b), k in zip(pairs, ks):
            tot += k * (tm * a.dtype.itemsize + tn * b.dtype.itemsize)
        return 2 * tot

    ok = [(tm * tn, tm, tn) for tm in divs(m) for tn in divs(n) if tm * tn <= MM_MAX_ACC and cost(tm, tn) <= MM_VMEM_BUDGET]
    _, tm, tn = max(ok)
    return tm, tn


def _mm(pairs, *, ta=False, tb=False, out_dtype=F32, add=None, add_scale=1.0, chunk=None, name):
    n_pairs = len(pairs)
    windows = [pr[2] if len(pr) == 3 else None for pr in pairs]
    pairs = [pr[:2] for pr in pairs]
    assert not ((ta or tb) and any(windows))
    ks = [w[2] if w else (a.shape[-2] if ta else a.shape[-1]) for (a, _), w in zip(pairs, windows)]
    a0, b0 = pairs[0]
    m = a0.shape[-1] if ta else a0.shape[-2]
    n = b0.shape[-2] if tb else b0.shape[-1]
    tm, tn = _mm_tiles(pairs, ks, m, n, out_dtype, add is not None)
    dims = (((0 if ta else 1,), (1 if tb else 0,)), ((), ()))
    nk = NCHIP if chunk else 1
    assert chunk != "sum" or out_dtype == F32
    flat = [i for i, (a, b) in enumerate(pairs) if a.ndim == 2 and b.ndim == 2]
    stacked = [i for i in range(n_pairs) if i not in flat]

    def body(*refs):
        o_ref = refs[-1]

        def products(which):
            acc = None
            for i in which:
                a = refs[2 * i][...].astype(BF16)
                b = refs[2 * i + 1][...].astype(BF16)
                part = lax.dot_general(a, b, dims, preferred_element_type=F32)
                acc = part if acc is None else acc + part
            return acc

        if chunk == "sum":
            k = pl.program_id(2)
            acc = products(stacked)

            @pl.when(k == 0)
            def _():
                first = acc + products(flat) if flat else acc
                o_ref[...] = first + add_scale * refs[2 * n_pairs][...] if add is not None else first

            @pl.when(k > 0)
            def _():
                o_ref[...] += acc
            return
        acc = products(range(n_pairs))
        if add is not None:
            acc = acc + add_scale * refs[2 * n_pairs][...]
        o_ref[...] = acc.astype(out_dtype)

    def spec(arr, shape, idx2):
        if arr.ndim == 3:
            return pl.BlockSpec((None,) + shape, lambda i, j, k: (k,) + idx2(i, j))
        return pl.BlockSpec(shape, lambda i, j, k: idx2(i, j))

    in_specs, args = [], []
    for (a, b), kdim, window in zip(pairs, ks, windows):
        ka, kb = window[:2] if window else (0, 0)
        in_specs.append(spec(a, (kdim, tm), lambda i, j: (0, i)) if ta else spec(a, (tm, kdim), lambda i, j, ka=ka: (i, ka)))
        in_specs.append(spec(b, (tn, kdim), lambda i, j: (j, 0)) if tb else spec(b, (kdim, tn), lambda i, j, kb=kb: (kb, j)))
        args += [a, b]
    if add is not None:
        in_specs.append(pl.BlockSpec((tm, tn), lambda i, j, k: (i, j)))
        args.append(add)
    if chunk == "out":
        out_spec = pl.BlockSpec((None, tm, tn), lambda i, j, k: (k, i, j))
        out_shape = jax.ShapeDtypeStruct((nk, m, n), out_dtype)
    else:
        out_spec = pl.BlockSpec((tm, tn), lambda i, j, k: (i, j))
        out_shape = jax.ShapeDtypeStruct((m, n), out_dtype)
    return pl.pallas_call(
        body, name=name, grid=(m // tm, n // tn, nk), in_specs=in_specs, out_specs=out_spec, out_shape=out_shape,
        compiler_params=pltpu.CompilerParams(dimension_semantics=("parallel", "parallel", "arbitrary")),
    )(*args)


TR = 256


def _row_spec(c):
    return pl.BlockSpec((TR, c), lambda i: (i, 0))


def _vec_spec(c):
    return pl.BlockSpec((1, c), lambda i: (0, 0))


def _acc_rows(ref, val):
    @pl.when(pl.program_id(0) == 0)
    def _():
        ref[...] = jnp.zeros_like(ref)
    ref[...] += val


def _rms_fwd(u, w, *, name):
    c = u.shape[1]

    def body(u_ref, w_ref, o_ref):
        v = u_ref[...]
        r = lax.rsqrt(jnp.mean(v * v, axis=-1, keepdims=True) + EPS_RMS)
        o_ref[...] = (v * r * w_ref[...]).astype(BF16)

    return pl.pallas_call(body, name=name, grid=(S // TR,), in_specs=[_row_spec(c), _vec_spec(c)], out_specs=_row_spec(c),
                          out_shape=jax.ShapeDtypeStruct((S, c), BF16))(u, w)


def _rms_bwd(u, w, dy, *, name):
    c = u.shape[1]

    def body(u_ref, w_ref, dy_ref, du_ref, dw_ref):
        v = u_ref[...]
        g = dy_ref[...].astype(F32)
        r = lax.rsqrt(jnp.mean(v * v, axis=-1, keepdims=True) + EPS_RMS)
        gw = g * w_ref[...]
        du_ref[...] = r * gw - v * (r * r * r * jnp.mean(gw * v, axis=-1, keepdims=True))
        _acc_rows(dw_ref, jnp.sum(g * v * r, axis=0, keepdims=True))

    return pl.pallas_call(body, name=name, grid=(S // TR,), in_specs=[_row_spec(c), _vec_spec(c), _row_spec(c)],
                          out_specs=[_row_spec(c), _vec_spec(c)],
                          out_shape=[jax.ShapeDtypeStruct((S, c), F32), jax.ShapeDtypeStruct((1, c), F32)])(u, w, dy)


def _gate_norm_fwd(y, z, w):
    def body(y_ref, z_ref, w_ref, o_ref):
        zz = z_ref[...]
        v = y_ref[...] * (zz * _sigmoid(zz))
        r = lax.rsqrt(jnp.mean(v * v, axis=-1, keepdims=True) + EPS_RMS)
        o_ref[...] = (v * r * w_ref[...]).astype(BF16)

    c = SSD_INNER
    return pl.pallas_call(body, name="ssd_gate_norm_fwd", grid=(S // TR,), in_specs=[_row_spec(c), _row_spec(c), _vec_spec(c)],
                          out_specs=_row_spec(c), out_shape=jax.ShapeDtypeStruct((S, c), BF16))(y, z, w)


def _gate_norm_bwd(y, z, w, dout):
    def body(y_ref, z_ref, w_ref, g_ref, dy_ref, dz_ref, dw_ref):
        yy = y_ref[...]
        zz = z_ref[...]
        sg = _sigmoid(zz)
        sz = zz * sg
        v = yy * sz
        g = g_ref[...]
        r = lax.rsqrt(jnp.mean(v * v, axis=-1, keepdims=True) + EPS_RMS)
        gw = g * w_ref[...]
        dv = r * gw - v * (r * r * r * jnp.mean(gw * v, axis=-1, keepdims=True))
        dy_ref[...] = dv * sz
        dz_ref[...] = (dv * yy * (sg * (1.0 + zz * (1.0 - sg)))).astype(BF16)
        _acc_rows(dw_ref, jnp.sum(g * v * r, axis=0, keepdims=True))

    c = SSD_INNER
    return pl.pallas_call(body, name="ssd_gate_norm_bwd", grid=(S // TR,),
                          in_specs=[_row_spec(c), _row_spec(c), _vec_spec(c), _row_spec(c)],
                          out_specs=[_row_spec(c), _row_spec(c), _vec_spec(c)],
                          out_shape=[jax.ShapeDtypeStruct((S, c), F32), jax.ShapeDtypeStruct((S, c), BF16),
                                     jax.ShapeDtypeStruct((1, c), F32)])(y, z, w, dout)


MIX_ROWS = 512


def _out_proj_ln(y_ssd, y_mla, w_out, xr, g, b):
    k = y_ssd.shape[1]

    def body(ys_ref, ym_ref, w_ref, x_ref, g_ref, b_ref, m_ref, o_ref, ob_ref):
        mix = (jnp.dot(ys_ref[...], w_ref[:k], preferred_element_type=F32)
               + jnp.dot(ym_ref[...], w_ref[k:], preferred_element_type=F32))
        m_ref[...] = mix
        pre = ALPHA * x_ref[...] + mix
        mu = jnp.mean(pre, axis=-1, keepdims=True)
        d = pre - mu
        rs = lax.rsqrt(jnp.mean(d * d, axis=-1, keepdims=True) + EPS_LN)
        h = d * rs * g_ref[...] + b_ref[...]
        o_ref[...] = h
        ob_ref[...] = h.astype(BF16)

    rows = lambda c: pl.BlockSpec((MIX_ROWS, c), lambda i: (i, 0))
    return pl.pallas_call(
        body, name="out_proj_ln", grid=(S // MIX_ROWS,),
        in_specs=[rows(k), rows(k), _whole(w_out), rows(D), _vec_spec(D), _vec_spec(D)], out_specs=[rows(D)] * 3,
        out_shape=[jax.ShapeDtypeStruct((S, D), F32), jax.ShapeDtypeStruct((S, D), F32), jax.ShapeDtypeStruct((S, D), BF16)],
    )(y_ssd, y_mla, w_out, xr, g, b)


def _ln_bwd(xr, mix, g, dh, w_out):
    k = w_out.shape[0] // 2

    def body(x_ref, m_ref, g_ref, dh_ref, w_ref, dpre_ref, dpreb_ref, dg_ref, db_ref, dys_ref, dym_ref):
        pre = ALPHA * x_ref[...] + m_ref[...]
        mu = jnp.mean(pre, axis=-1, keepdims=True)
        d = pre - mu
        rs = lax.rsqrt(jnp.mean(d * d, axis=-1, keepdims=True) + EPS_LN)
        xh = d * rs
        dy = dh_ref[...]
        gy = dy * g_ref[...]
        dpre = rs * (gy - jnp.mean(gy, axis=-1, keepdims=True) - xh * jnp.mean(gy * xh, axis=-1, keepdims=True))
        dpre_ref[...] = dpre
        dpreb = dpre.astype(BF16)
        dpreb_ref[...] = dpreb
        _acc_rows(dg_ref, jnp.sum(dy * xh, axis=0, keepdims=True))
        _acc_rows(db_ref, jnp.sum(dy, axis=0, keepdims=True))
        dys_ref[...] = lax.dot_general(dpreb, w_ref[:k], (((1,), (1,)), ((), ())), preferred_element_type=F32)
        dym_ref[...] = lax.dot_general(dpreb, w_ref[k:], (((1,), (1,)), ((), ())), preferred_element_type=F32)

    rows = lambda c: pl.BlockSpec((MIX_ROWS, c), lambda i: (i, 0))
    return pl.pallas_call(
        body, name="ln_mix_bwd", grid=(S // MIX_ROWS,),
        in_specs=[rows(D), rows(D), _vec_spec(D), rows(D), _whole(w_out)],
        out_specs=[rows(D), rows(D), _vec_spec(D), _vec_spec(D), rows(k), rows(k)],
        out_shape=[jax.ShapeDtypeStruct((S, D), F32), jax.ShapeDtypeStruct((S, D), BF16), jax.ShapeDtypeStruct((1, D), F32),
                   jax.ShapeDtypeStruct((1, D), F32), jax.ShapeDtypeStruct((S, k), F32), jax.ShapeDtypeStruct((S, k), F32)],
    )(xr, mix, g, dh, w_out)


FF_CHUNK = D_FF // NCHIP


FF_ROWS = 1024


def _ff_act_spec():
    return pl.BlockSpec((None, FF_ROWS, FF_CHUNK), lambda i, k: (k, i, 0))


def _ff_w_spec():
    return pl.BlockSpec((None, FF_CHUNK, D), lambda i, k: (k, 0, 0))


def _ffn_hidden_fwd(h, w_gate_t, w_up_t):
    def body(h_ref, wg_ref, wu_ref, g_ref, u_ref, a_ref):
        hh = h_ref[...]
        g = _dot(hh, wg_ref[...], ((1,), (1,)))
        u = _dot(hh, wu_ref[...], ((1,), (1,)))
        g_ref[...] = g.astype(BF16)
        u_ref[...] = u.astype(BF16)
        a_ref[...] = (g * _sigmoid(g) * u).astype(BF16)

    return pl.pallas_call(
        body, name="ffn_hidden_fwd", grid=(S // FF_ROWS, NCHIP),
        in_specs=[pl.BlockSpec((FF_ROWS, D), lambda i, k: (i, 0)), _ff_w_spec(), _ff_w_spec()], out_specs=[_ff_act_spec()] * 3,
        out_shape=[jax.ShapeDtypeStruct((NCHIP, S, FF_CHUNK), BF16)] * 3,
        compiler_params=pltpu.CompilerParams(dimension_semantics=("parallel", "parallel")),
    )(h, w_gate_t, w_up_t)


def _ffn_hidden_bwd(dout, w_down, gate, up):
    def body(d_ref, wd_ref, g_ref, u_ref, dg_ref, du_ref):
        d = _dot(d_ref[...], wd_ref[...], ((1,), (1,)))
        g = g_ref[...].astype(F32)
        sg = _sigmoid(g)
        dg_ref[...] = (d * u_ref[...].astype(F32) * (sg * (1.0 + g * (1.0 - sg)))).astype(BF16)
        du_ref[...] = (d * g * sg).astype(BF16)

    return pl.pallas_call(
        body, name="ffn_hidden_bwd", grid=(S // FF_ROWS, NCHIP),
        in_specs=[pl.BlockSpec((FF_ROWS, D), lambda i, k: (i, 0)), _ff_w_spec(), _ff_act_spec(), _ff_act_spec()],
        out_specs=[_ff_act_spec()] * 2, out_shape=[jax.ShapeDtypeStruct((NCHIP, S, FF_CHUNK), BF16)] * 2,
        compiler_params=pltpu.CompilerParams(dimension_semantics=("parallel", "parallel")),
    )(dout, w_down, gate, up)


def _final_fwd_bwd(h1, ffn, h1b, pb, w_pg, w_pp, target, g2, b2):
    def body(h_ref, f_ref, hb_ref, pb_ref, wpg_ref, wpp_ref, t_ref, g_ref, b_ref,
             dpre_ref, dpreb_ref, dpg_ref, dpp_ref, dg_ref, db_ref, loss_ref):
        sg = _sigmoid(jnp.dot(hb_ref[...], wpg_ref[...], preferred_element_type=F32))
        ppv = jnp.dot(pb_ref[...], wpp_ref[...], preferred_element_type=F32)
        pre = ALPHA * h_ref[...] + f_ref[...] + sg * ppv
        mu = jnp.mean(pre, axis=-1, keepdims=True)
        d = pre - mu
        rs = lax.rsqrt(jnp.mean(d * d, axis=-1, keepdims=True) + EPS_LN)
        xh = d * rs
        err = xh * g_ref[...] + b_ref[...] - t_ref[...]
        dy = err * (1.0 / D)
        gy = dy * g_ref[...]
        dpre = rs * (gy - jnp.mean(gy, axis=-1, keepdims=True) - xh * jnp.mean(gy * xh, axis=-1, keepdims=True))
        dpre_ref[...] = dpre
        dpreb_ref[...] = dpre.astype(BF16)
        dpg_ref[...] = (dpre * ppv * sg * (1.0 - sg)).astype(BF16)
        dpp_ref[...] = (dpre * sg).astype(BF16)
        _acc_rows(dg_ref, jnp.sum(dy * xh, axis=0, keepdims=True))
        _acc_rows(db_ref, jnp.sum(dy, axis=0, keepdims=True))
        _acc_rows(loss_ref, 0.5 * jnp.sum(jnp.mean(err * err, axis=-1, keepdims=True), axis=0, keepdims=True) * jnp.ones((1, LANE), F32))

    return pl.pallas_call(
        body, name="final_ln_loss", grid=(S // TR,),
        in_specs=[_row_spec(D)] * 3 + [_row_spec(pb.shape[1]), _whole(w_pg), _whole(w_pp), _row_spec(D)] + [_vec_spec(D)] * 2,
        out_specs=[_row_spec(D)] * 4 + [_vec_spec(D), _vec_spec(D), _vec_spec(LANE)],
        out_shape=[jax.ShapeDtypeStruct((S, D), F32)] + [jax.ShapeDtypeStruct((S, D), BF16)] * 3 + [
                   jax.ShapeDtypeStruct((1, D), F32), jax.ShapeDtypeStruct((1, D), F32), jax.ShapeDtypeStruct((1, LANE), F32)],
    )(h1, ffn, h1b, pb, w_pg, w_pp, target, g2, b2)


def _rot(u, cos_t, sin_t, lane):
    partner = jnp.where(lane < NOPE + ROPE // 2, pltpu.roll(u, LANE - ROPE // 2, 1), pltpu.roll(u, ROPE // 2, 1))
    return u * cos_t + partner * sin_t


def _rms(v, w):
    r = lax.rsqrt(jnp.mean(v * v, axis=-1, keepdims=True) + EPS_RMS)
    return v * r * w, r


def _rms_grad(v, r, w, g):
    gw = g * w
    return r * gw - v * (r * r * r * jnp.mean(gw * v, axis=-1, keepdims=True)), jnp.sum(g * v * r, axis=0, keepdims=True)


def _whole(arr):
    return pl.BlockSpec(arr.shape, lambda i: (0,) * arr.ndim)


def _qkv_fwd(small, w_q, w_k, w_v, q_norm, kv_norm, cos_t, sin_t):
    def body(sm_ref, wq_ref, wk_ref, wv_ref, qw_ref, kw_ref, c_ref, s_ref, qn_ref, kvn_ref, q_ref, k_ref, kt_ref, v_ref):
        lane = lax.broadcasted_iota(jnp.int32, (TR, LANE), 1)
        c, s = c_ref[...], s_ref[...]
        qn = _rms(sm_ref[:, SM_Q:SM_Q + Q_RANK], qw_ref[...])[0].astype(BF16)
        kvn = _rms(sm_ref[:, SM_KV:SM_KV + KV_RANK], kw_ref[...])[0].astype(BF16)
        qn_ref[...] = qn
        kvn_ref[...] = kvn
        kr = _rot(pltpu.roll(sm_ref[:, SM_KR:SM_KR + LANE], NOPE, 1), c, s, lane)
        for h in range(H):
            tile = slice(h * LANE, (h + 1) * LANE)
            q_ref[:, tile] = _rot(_dot(qn, wq_ref[:, tile], ((1,), (0,))), c, s, lane).astype(BF16)
            kt = _dot(kvn, wk_ref[:, tile], ((1,), (0,))) + kr
            k_ref[:, tile] = kt.astype(BF16)
            kt_ref[tile, :] = kt.T.astype(BF16)
        v_ref[...] = _dot(kvn, wv_ref[...], ((1,), (0,))).astype(BF16)

    w = H * LANE
    return pl.pallas_call(
        body, name="qkv_fwd", grid=(S // TR,),
        in_specs=[_row_spec(SMALL_W), _whole(w_q), _whole(w_k), _whole(w_v), _vec_spec(Q_RANK), _vec_spec(KV_RANK), _row_spec(LANE), _row_spec(LANE)],
        out_specs=[_row_spec(Q_RANK), _row_spec(KV_RANK), _row_spec(w), _row_spec(w), pl.BlockSpec((w, TR), lambda i: (0, i)),
                   _row_spec(H * VDIM)],
        out_shape=[jax.ShapeDtypeStruct((S, Q_RANK), BF16), jax.ShapeDtypeStruct((S, KV_RANK), BF16), jax.ShapeDtypeStruct((S, w), BF16),
                   jax.ShapeDtypeStruct((S, w), BF16), jax.ShapeDtypeStruct((w, S), BF16), jax.ShapeDtypeStruct((S, H * VDIM), BF16)],
    )(small, w_q, w_k, w_v, q_norm, kv_norm, cos_t, sin_t)


def _qkv_bwd(dqt, dk, dv, small, w_q, w_k, w_v, q_norm, kv_norm, cos_t, sin_t):
    def body(dq_ref, dk_ref, dv_ref, sm_ref, wq_ref, wk_ref, wv_ref, qw_ref, kw_ref, c_ref, s_ref,
             ds_ref, dql_ref, dkb_ref, dqw_ref, dkw_ref):
        lane = lax.broadcasted_iota(jnp.int32, (TR, LANE), 1)
        c, s = c_ref[...], -s_ref[...]
        dqn = jnp.zeros((TR, Q_RANK), F32)
        dkvn = _dot(dv_ref[...], wv_ref[...], ((1,), (1,)))
        dkr = jnp.zeros((TR, LANE), F32)
        for h in range(H):
            tile = slice(h * LANE, (h + 1) * LANE)
            dql = _rot(dq_ref[tile, :].T, c, s, lane).astype(BF16)
            dql_ref[:, tile] = dql
            dqn = dqn + _dot(dql, wq_ref[:, tile], ((1,), (1,)))
            dkt = dk_ref[:, tile]
            dkb_ref[:, tile] = dkt.astype(BF16)
            dkvn = dkvn + _dot(dkt, wk_ref[:, tile], ((1,), (1,)))
            dkr = dkr + dkt
        dkr = jnp.where((lane >= NOPE) & (lane < NOPE + ROPE), dkr, 0.0)
        q_c, kv_c = sm_ref[:, SM_Q:SM_Q + Q_RANK], sm_ref[:, SM_KV:SM_KV + KV_RANK]
        dq_c, dqw = _rms_grad(q_c, _rms(q_c, qw_ref[...])[1], qw_ref[...], dqn)
        dkv_c, dkw = _rms_grad(kv_c, _rms(kv_c, kw_ref[...])[1], kw_ref[...], dkvn)
        ds_ref[:, SM_Q:SM_Q + Q_RANK] = dq_c.astype(BF16)
        ds_ref[:, SM_KV:SM_KV + KV_RANK] = dkv_c.astype(BF16)
        ds_ref[:, SM_KR:SM_KR + LANE] = pltpu.roll(_rot(dkr, c, s, lane), LANE - NOPE, 1).astype(BF16)
        _acc_rows(dqw_ref, dqw)
        _acc_rows(dkw_ref, dkw)

    w = H * LANE
    return pl.pallas_call(
        body, name="qkv_bwd", grid=(S // TR,),
        in_specs=[pl.BlockSpec((w, TR), lambda i: (0, i)), _row_spec(w), _row_spec(H * VDIM), _row_spec(SMALL_W), _whole(w_q), _whole(w_k),
                  _whole(w_v), _vec_spec(Q_RANK), _vec_spec(KV_RANK), _row_spec(LANE), _row_spec(LANE)],
        out_specs=[_row_spec(SM_DT), _row_spec(w), _row_spec(w), _vec_spec(Q_RANK), _vec_spec(KV_RANK)],
        out_shape=[jax.ShapeDtypeStruct((S, SM_DT), BF16), jax.ShapeDtypeStruct((S, w), BF16), jax.ShapeDtypeStruct((S, w), BF16),
                   jax.ShapeDtypeStruct((1, Q_RANK), F32), jax.ShapeDtypeStruct((1, KV_RANK), F32)],
    )(dqt, dk, dv, small, w_q, w_k, w_v, q_norm, kv_norm, cos_t, sin_t)


CB = 256


def _shift_down(u, k, row):
    if k == 0:
        return u
    return jnp.where(row >= k, pltpu.roll(u, k, 0), 0.0)


def _shift_up(u, k, row):
    if k == 0:
        return u
    return jnp.where(row < S - k, pltpu.roll(u, S - k, 0), 0.0)


def _conv_fwd(u, w, b):
    def body(u_ref, w_ref, b_ref, o_ref):
        row = lax.broadcasted_iota(jnp.int32, (S, CB), 0)
        uu = u_ref[...]
        acc = b_ref[...] + w_ref[SSD_K - 1:SSD_K, :] * uu
        for k in range(SSD_K - 1):
            acc = acc + w_ref[k:k + 1, :] * _shift_down(uu, SSD_K - 1 - k, row)
        o_ref[...] = acc * _sigmoid(acc)

    c = u.shape[1]
    return pl.pallas_call(
        body, name="conv_fwd", grid=(c // CB,),
        in_specs=[pl.BlockSpec((S, CB), lambda j: (0, j)), pl.BlockSpec((SSD_K, CB), lambda j: (0, j)), pl.BlockSpec((1, CB), lambda j: (0, j))],
        out_specs=pl.BlockSpec((S, CB), lambda j: (0, j)), out_shape=jax.ShapeDtypeStruct((S, c), F32),
    )(u, w, b)


def _conv_bwd(u, w, b, dact):
    def body(u_ref, w_ref, b_ref, d_ref, du_ref, dw_ref, db_ref):
        row = lax.broadcasted_iota(jnp.int32, (S, CB), 0)
        uu = u_ref[...]
        sh = [_shift_down(uu, SSD_K - 1 - k, row) for k in range(SSD_K)]
        acc = b_ref[...]
        for k in range(SSD_K):
            acc = acc + w_ref[k:k + 1, :] * sh[k]
        sg = _sigmoid(acc)
        dacc = d_ref[...] * (sg * (1.0 + acc * (1.0 - sg)))
        du = w_ref[SSD_K - 1:SSD_K, :] * dacc
        for k in range(SSD_K - 1):
            du = du + w_ref[k:k + 1, :] * _shift_up(dacc, SSD_K - 1 - k, row)
        du_ref[...] = du.astype(BF16)
        for k in range(SSD_K):
            dw_ref[k:k + 1, :] = jnp.sum(dacc * sh[k], axis=0, keepdims=True)
        db_ref[...] = jnp.sum(dacc, axis=0, keepdims=True)

    c = u.shape[1]
    col = lambda r: pl.BlockSpec((r, CB), lambda j: (0, j))
    return pl.pallas_call(
        body, name="conv_bwd", grid=(c // CB,), in_specs=[col(S), col(SSD_K), col(1), col(S)], out_specs=[col(S), col(SSD_K), col(1)],
        out_shape=[jax.ShapeDtypeStruct((S, c), BF16), jax.ShapeDtypeStruct((SSD_K, c), F32), jax.ShapeDtypeStruct((1, c), F32)],
    )(u, w, b, dact)


NPAIR = H // 2
PAIRS_PER_GROUP = NPAIR // SSD_G


def _softplus(v):
    return jnp.maximum(v, 0.0) + jnp.log(1.0 + jnp.exp(-jnp.abs(v)))


def _dot(a, b, dims):
    return lax.dot_general(a.astype(BF16), b.astype(BF16), (dims, ((), ())), preferred_element_type=F32)


def _dot2(a, sel):
    hi = a.astype(BF16)
    lo = (a - hi.astype(F32)).astype(BF16)
    dims = (((1,), (0,)), ((), ()))
    return lax.dot_general(hi, sel, dims, preferred_element_type=F32) + lax.dot_general(lo, sel, dims, preferred_element_type=F32)


def _dot3(a, b, dims, split_lhs):
    v = a if split_lhs else b
    v1 = v.astype(BF16)
    r1 = v - v1.astype(F32)
    v2 = r1.astype(BF16)
    v3 = (r1 - v2.astype(F32)).astype(BF16)
    acc = None
    for part in (v1, v2, v3):
        lhs, rhs = (part, b) if split_lhs else (a, part)
        t = lax.dot_general(lhs, rhs, (dims, ((), ())), preferred_element_type=F32)
        acc = t if acc is None else acc + t
    return acc


def _ssd_chunk_common(dt_ref, dtT_ref, prow_ref, pcol_ref):
    prow = prow_ref[...]
    pcol = pcol_ref[...]
    ri = lax.broadcasted_iota(jnp.int32, (SSD_L, SSD_L), 0)
    ci = lax.broadcasted_iota(jnp.int32, (SSD_L, SSD_L), 1)
    causal = ri >= ci
    pre_c = dt_ref[...] + prow[0:1, :]
    dtc = _softplus(pre_c)
    a_row = -jnp.exp(prow[1:2, :])
    cs_col = _dot3(causal.astype(BF16), dtc * a_row, ((1,), (0,)), False)
    dtr = _softplus(dtT_ref[...] + pcol[:, 0:1])
    a_col = -jnp.exp(pcol[:, 1:2])
    cs_row = _dot3(dtr * a_col, (ri <= ci).astype(BF16), ((1,), (0,)), True)
    return prow, causal, pre_c, dtc, a_row, cs_col, cs_row


def _ssd_fwd(act, small, dtT, prow, pcol):
    def body(x_ref, b_ref, c_ref, dt_ref, dtT_ref, prow_ref, pcol_ref, y_ref, st_ref, state):
        @pl.when(pl.program_id(0) == 0)
        def _():
            state[...] = jnp.zeros_like(state)

        prow, causal, _, dtc, _, cs_col, cs_row = _ssd_chunk_common(dt_ref, dtT_ref, prow_ref, pcol_ref)
        lo = lax.broadcasted_iota(jnp.int32, (SSD_L, LANE), 1) < SSD_P
        lo1 = lo[0:1, :]
        for g in range(SSD_G):
            bm = b_ref[:, g * SSD_N:(g + 1) * SSD_N]
            cm = c_ref[:, g * SSD_N:(g + 1) * SSD_N]
            cb = _dot(cm, bm, ((1,), (1,)))
            for qq in range(PAIRS_PER_GROUP):
                q = g * PAIRS_PER_GROUP + qq
                ha, hb = 2 * q, 2 * q + 1
                csa, csb = cs_col[:, ha:ha + 1], cs_col[:, hb:hb + 1]
                xp = x_ref[:, q * LANE:(q + 1) * LANE]
                xx = xp * jnp.where(lo, dtc[:, ha:ha + 1], dtc[:, hb:hb + 1])
                ga = cb * jnp.exp(jnp.where(causal, csa - cs_row[ha:ha + 1, :], NEG))
                gb = cb * jnp.exp(jnp.where(causal, csb - cs_row[hb:hb + 1, :], NEG))
                y = _dot(ga, jnp.where(lo, xx, 0.0), ((1,), (0,))) + _dot(gb, jnp.where(lo, 0.0, xx), ((1,), (0,)))
                s_in = state[q]
                y = y + _dot(cm, s_in, ((1,), (0,))) * jnp.where(lo, jnp.exp(csa), jnp.exp(csb))
                y = y + jnp.where(lo1, prow[2:3, ha:ha + 1], prow[2:3, hb:hb + 1]) * xp
                y_ref[:, q * LANE:(q + 1) * LANE] = y
                la, lb = csa[SSD_L - 1:SSD_L, :], csb[SSD_L - 1:SSD_L, :]
                decay = jnp.where(lo, jnp.exp(la - csa), jnp.exp(lb - csb))
                st_ref[q] = s_in
                state[q] = s_in * jnp.where(lo1, jnp.exp(la), jnp.exp(lb)) + _dot(bm, xx * decay, ((0,), (0,)))

    L = SSD_L
    return pl.pallas_call(
        body, name="ssd_fwd", grid=(SSD_NC,),
        in_specs=[pl.BlockSpec((L, SSD_INNER), lambda c: (c, 0)),
                  pl.BlockSpec((L, SSD_G * SSD_N), lambda c: (c, SSD_INNER // (SSD_G * SSD_N))),
                  pl.BlockSpec((L, SSD_G * SSD_N), lambda c: (c, SSD_INNER // (SSD_G * SSD_N) + 1)),
                  pl.BlockSpec((L, LANE), lambda c: (c, SM_DT // LANE)),
                  pl.BlockSpec((LANE, L), lambda c: (0, c)),
                  pl.BlockSpec((8, LANE), lambda c: (0, 0)), pl.BlockSpec((LANE, 8), lambda c: (0, 0))],
        out_specs=[pl.BlockSpec((L, SSD_INNER), lambda c: (c, 0)),
                   pl.BlockSpec((None, NPAIR, SSD_N, LANE), lambda c: (c, 0, 0, 0))],
        out_shape=[jax.ShapeDtypeStruct((S, SSD_INNER), F32), jax.ShapeDtypeStruct((SSD_NC, NPAIR, SSD_N, LANE), F32)],
        scratch_shapes=[pltpu.VMEM((NPAIR, SSD_N, LANE), F32)],
        compiler_params=pltpu.CompilerParams(dimension_semantics=("arbitrary",)),
    )(act, act, act, small, dtT, prow, pcol)


def _ssd_bwd(act, small, dtT, prow, pcol, states, dy):
    def body(x_ref, b_ref, c_ref, dt_ref, dtT_ref, prow_ref, pcol_ref, st_ref, dy_ref,
             dx_ref, ddt_ref, dp_ref, dstate):
        @pl.when(pl.program_id(0) == 0)
        def _():
            dstate[...] = jnp.zeros_like(dstate)
            dp_ref[...] = jnp.zeros_like(dp_ref)

        prow, causal, pre_c, dtc, a_row, cs_col, cs_row = _ssd_chunk_common(dt_ref, dtT_ref, prow_ref, pcol_ref)
        lane = lax.broadcasted_iota(jnp.int32, (SSD_L, LANE), 1)
        sub = lax.broadcasted_iota(jnp.int32, (LANE, SSD_L), 0)
        rowi = lax.broadcasted_iota(jnp.int32, (SSD_L, 1), 0)
        pick_p = lax.broadcasted_iota(jnp.int32, (LANE, LANE), 0)
        pick_l = lax.broadcasted_iota(jnp.int32, (LANE, LANE), 1)
        lo = lane < SSD_P
        lo1 = lo[0:1, :]
        dcs_c = jnp.zeros((SSD_L, LANE), F32)
        dcs_r = jnp.zeros((LANE, SSD_L), F32)
        ddt_x = jnp.zeros((SSD_L, LANE), F32)
        dd_row = jnp.zeros((1, LANE), F32)
        for g in range(SSD_G):
            bm = b_ref[:, g * SSD_N:(g + 1) * SSD_N]
            cm = c_ref[:, g * SSD_N:(g + 1) * SSD_N]
            cb = _dot(cm, bm, ((1,), (1,)))
            dcb = jnp.zeros((SSD_L, SSD_L), F32)
            dbm = jnp.zeros((SSD_L, SSD_N), F32)
            dcm = jnp.zeros((SSD_L, SSD_N), F32)
            for qq in range(PAIRS_PER_GROUP):
                q = g * PAIRS_PER_GROUP + qq
                ha, hb = 2 * q, 2 * q + 1
                csa, csb = cs_col[:, ha:ha + 1], cs_col[:, hb:hb + 1]
                xp = x_ref[:, q * LANE:(q + 1) * LANE]
                dtp = jnp.where(lo, dtc[:, ha:ha + 1], dtc[:, hb:hb + 1])
                xx = xp * dtp
                lma = jnp.exp(jnp.where(causal, csa - cs_row[ha:ha + 1, :], NEG))
                lmb = jnp.exp(jnp.where(causal, csb - cs_row[hb:hb + 1, :], NEG))
                ga, gb = cb * lma, cb * lmb
                dyp = dy_ref[:, q * LANE:(q + 1) * LANE]
                dya, dyb = jnp.where(lo, dyp, 0.0), jnp.where(lo, 0.0, dyp)
                s_in = st_ref[q]
                ds_out = dstate[q]
                la, lb = csa[SSD_L - 1:SSD_L, :], csb[SSD_L - 1:SSD_L, :]
                ecs = jnp.where(lo, jnp.exp(csa), jnp.exp(csb))
                decay = jnp.where(lo, jnp.exp(la - csa), jnp.exp(lb - csb))
                cd = jnp.where(lo1, jnp.exp(la), jnp.exp(lb))
                bds = _dot(bm, ds_out, ((1,), (0,)))
                dxx = _dot(ga, dya, ((0,), (0,))) + _dot(gb, dyb, ((0,), (0,))) + bds * decay
                dga = _dot(dya, xx, ((1,), (1,)))
                dgb = _dot(dyb, xx, ((1,), (1,)))
                dsega, dsegb = dga * ga, dgb * gb
                dcb = dcb + dga * lma + dgb * lmb
                yoff = _dot(cm, s_in, ((1,), (0,))) * ecs
                dye = dyp * ecs
                dcm = dcm + _dot(dye, s_in, ((1,), (1,)))
                xd = xx * decay
                dbm = dbm + _dot(xd, ds_out, ((1,), (1,)))
                wv = xd * bds
                ends = jnp.sum(wv, axis=0, keepdims=True) + cd * jnp.sum(ds_out * s_in, axis=0, keepdims=True)
                t1 = dyp * yoff - wv + jnp.where(rowi == SSD_L - 1, ends, 0.0)
                to_pair = (((pick_p < SSD_P) & (pick_l == ha)) | ((pick_p >= SSD_P) & (pick_l == hb))).astype(BF16)
                to_a_b = jnp.concatenate([(pick_l == ha).astype(BF16), (pick_l == hb).astype(BF16)], axis=0)
                dcs_c = dcs_c + _dot2(t1, to_pair) + _dot2(jnp.concatenate([dsega, dsegb], axis=1), to_a_b)
                dcs_r = (dcs_r + jnp.where(sub == ha, jnp.sum(dsega, axis=0, keepdims=True), 0.0)
                         + jnp.where(sub == hb, jnp.sum(dsegb, axis=0, keepdims=True), 0.0))
                dstate[q] = _dot(cm, dye, ((0,), (0,))) + cd * ds_out
                dpair = jnp.where(lo1, prow[2:3, ha:ha + 1], prow[2:3, hb:hb + 1])
                dx_ref[:, q * LANE:(q + 1) * LANE] = dxx * dtp + dpair * dyp
                ddt_x = ddt_x + _dot2(dxx * xp, to_pair)
                dd_row = dd_row + jnp.sum(_dot2(dyp * xp, to_pair), axis=0, keepdims=True)
            dx_ref[:, SSD_INNER + g * SSD_N:SSD_INNER + (g + 1) * SSD_N] = dbm + _dot(dcb, cm, ((0,), (0,)))
            dx_ref[:, SSD_INNER + (SSD_G + g) * SSD_N:SSD_INNER + (SSD_G + g + 1) * SSD_N] = dcm + _dot(dcb, bm, ((1,), (0,)))
        ri = lax.broadcasted_iota(jnp.int32, (SSD_L, SSD_L), 0)
        ci = lax.broadcasted_iota(jnp.int32, (SSD_L, SSD_L), 1)
        da = _dot3((ri <= ci).astype(BF16), dcs_c, ((1,), (0,)), False)
        da = da - _dot3(dcs_r, causal.astype(BF16), ((1,), (0,)), True).T
        ddt = ddt_x + da * a_row
        ddt_raw = ddt * _sigmoid(pre_c)
        ddt_ref[...] = ddt_raw
        da_head = jnp.sum(da * dtc, axis=0, keepdims=True) * a_row
        dp_ref[0:1, :] += jnp.sum(ddt_raw, axis=0, keepdims=True)
        dp_ref[1:2, :] += da_head
        dp_ref[2:3, :] += dd_row

    L = SSD_L
    rev = SSD_NC - 1
    bc_cols = SSD_INNER // (SSD_G * SSD_N)
    return pl.pallas_call(
        body, name="ssd_bwd", grid=(SSD_NC,),
        in_specs=[pl.BlockSpec((L, SSD_INNER), lambda c: (rev - c, 0)),
                  pl.BlockSpec((L, SSD_G * SSD_N), lambda c: (rev - c, bc_cols)),
                  pl.BlockSpec((L, SSD_G * SSD_N), lambda c: (rev - c, bc_cols + 1)),
                  pl.BlockSpec((L, LANE), lambda c: (rev - c, SM_DT // LANE)),
                  pl.BlockSpec((LANE, L), lambda c: (0, rev - c)),
                  pl.BlockSpec((8, LANE), lambda c: (0, 0)), pl.BlockSpec((LANE, 8), lambda c: (0, 0)),
                  pl.BlockSpec((None, NPAIR, SSD_N, LANE), lambda c: (rev - c, 0, 0, 0)),
                  pl.BlockSpec((L, SSD_INNER), lambda c: (rev - c, 0))],
        out_specs=[pl.BlockSpec((L, SSD_XBC), lambda c: (rev - c, 0)),
                   pl.BlockSpec((L, LANE), lambda c: (rev - c, 0)),
                   pl.BlockSpec((8, LANE), lambda c: (0, 0))],
        out_shape=[jax.ShapeDtypeStruct((S, SSD_XBC), F32), jax.ShapeDtypeStruct((S, LANE), F32),
                   jax.ShapeDtypeStruct((8, LANE), F32)],
        scratch_shapes=[pltpu.VMEM((NPAIR, SSD_N, LANE), F32)],
        compiler_params=pltpu.CompilerParams(dimension_semantics=("arbitrary",)),
    )(act, act, act, small, dtT, prow, pcol, states, dy)


TQ = 256
TK = 256
FWD_TQ = 256
FWD_TK = 256


def _attn_fwd(qc, kc, v):
    TQ, TK = FWD_TQ, FWD_TK

    def body(q_ref, k_ref, v_ref, o_ref, lse_ref):
        i = pl.program_id(1)
        lo = lax.broadcasted_iota(jnp.int32, (TQ, LANE), 1) < VDIM
        lo_k = lax.broadcasted_iota(jnp.int32, (TK, LANE), 1) < VDIM
        row_minus_col = lax.broadcasted_iota(jnp.int32, (TQ, TK), 0) - lax.broadcasted_iota(jnp.int32, (TQ, TK), 1)
        qa, qb = q_ref[:, 0:LANE], q_ref[:, LANE:2 * LANE]

        def scores(kb):
            kk = k_ref[pl.ds(pl.multiple_of(kb * TK, TK), TK), :]
            return (_dot(qa, kk[:, 0:LANE], ((1,), (1,))) * ATT_SCALE_LOG2, _dot(qb, kk[:, LANE:2 * LANE], ((1,), (1,))) * ATT_SCALE_LOG2)

        def update(kb, sa, sb, stats):
            ma, la, mb, lb, acc = stats
            vv = v_ref[pl.ds(pl.multiple_of(kb * TK, TK), TK), :]
            na = jnp.maximum(ma, jnp.max(sa, axis=1, keepdims=True))
            nb = jnp.maximum(mb, jnp.max(sb, axis=1, keepdims=True))
            pa, pb = jnp.exp2(sa - na), jnp.exp2(sb - nb)
            fa, fb = jnp.exp2(ma - na), jnp.exp2(mb - nb)
            la = fa * la + jnp.sum(pa, axis=1, keepdims=True)
            lb = fb * lb + jnp.sum(pb, axis=1, keepdims=True)
            acc = (acc * jnp.where(lo, fa, fb) + _dot(pa, jnp.where(lo_k, vv, 0), ((1,), (0,)))
                   + _dot(pb, jnp.where(lo_k, 0, vv), ((1,), (0,))))
            return na, la, nb, lb, acc

        def step(kb, carry):
            sa, sb = carry[:2]
            nxt = scores(kb + 1)
            return nxt + update(kb, sa, sb, carry[2:])

        neg = jnp.full((TQ, 1), NEG, F32)
        zero = jnp.zeros((TQ, 1), F32)
        n_full = i * (TQ // TK)
        carry = lax.fori_loop(0, n_full, step, scores(0) + (neg, zero, neg, zero, jnp.zeros((TQ, LANE), F32)))
        s, stats = carry[:2], carry[2:]
        for d in range(TQ // TK):
            nxt = scores(n_full + d + 1) if d + 1 < TQ // TK else None
            sa, sb = (jnp.where(row_minus_col >= d * TK, t, NEG) for t in s)
            stats = update(n_full + d, sa, sb, stats)
            s = nxt
        ma, la, mb, lb, acc = stats
        o_ref[...] = acc / jnp.where(lo, la, lb)
        lse_ref[...] = jnp.where(lo, ma + jnp.log2(la), mb + jnp.log2(lb)) * LN2

    return pl.pallas_call(
        body, name="attn_fwd", grid=(NPAIR, S // TQ),
        in_specs=[pl.BlockSpec((TQ, 2 * LANE), lambda j, i: (i, j)), pl.BlockSpec((S, 2 * LANE), lambda j, i: (0, j)),
                  pl.BlockSpec((S, LANE), lambda j, i: (0, j))],
        out_specs=[pl.BlockSpec((TQ, LANE), lambda j, i: (i, j)), pl.BlockSpec((None, TQ, LANE), lambda j, i: (j, i, 0))],
        out_shape=[jax.ShapeDtypeStruct((S, H * VDIM), F32), jax.ShapeDtypeStruct((NPAIR, S, LANE), F32)],
        compiler_params=pltpu.CompilerParams(dimension_semantics=("parallel", "parallel")),
    )(qc, kc, v)


def _attn_rows(lse, o, do):
    def body(lse_ref, o_ref, do_ref, r_ref):
        lt = lse_ref[...].T * (1.0 / LN2)
        tt = (o_ref[...] * do_ref[...]).T
        r_ref[...] = jnp.zeros_like(r_ref)
        r_ref[0:1, :] = lt[0:1, :]
        r_ref[1:2, :] = lt[VDIM:VDIM + 1, :]
        r_ref[2:3, :] = jnp.sum(tt[0:VDIM, :], axis=0, keepdims=True)
        r_ref[3:4, :] = jnp.sum(tt[VDIM:LANE, :], axis=0, keepdims=True)

    tile = pl.BlockSpec((S, LANE), lambda j: (0, j))
    return pl.pallas_call(
        body, name="attn_rows", grid=(NPAIR,), in_specs=[pl.BlockSpec((None, S, LANE), lambda j: (j, 0, 0)), tile, tile],
        out_specs=pl.BlockSpec((None, 8, S), lambda j: (j, 0, 0)), out_shape=jax.ShapeDtypeStruct((NPAIR, 8, S), F32),
    )(lse, o, do)


def _attn_bwd(qc, kc, kct, v, do, rows):
    nq = S // TQ

    def body(q_ref, k_ref, kt_ref, v_ref, do_ref, r_ref, dqt_ref, dk_ref, dv_ref):
        kb = pl.program_id(1)

        @pl.when(kb == 0)
        def _():
            dqt_ref[...] = jnp.zeros_like(dqt_ref)

        lo = lax.broadcasted_iota(jnp.int32, (TK, LANE), 1) < VDIM
        q_minus_k = lax.broadcasted_iota(jnp.int32, (TK, TQ), 1) - lax.broadcasted_iota(jnp.int32, (TK, TQ), 0)
        vv = v_ref[...]
        kk = k_ref[...]

        def step(qi, carry):
            off = pl.multiple_of(qi * TQ, TQ)
            qq = q_ref[pl.ds(off, TQ), :]
            dd = do_ref[pl.ds(off, TQ), :].astype(BF16)
            rr = r_ref[:, pl.ds(off, TQ)]
            keep = q_minus_k >= (kb - qi) * TQ
            out = []
            for x in range(2):
                sel = lo if x == 0 else jnp.logical_not(lo)
                kx, qx = kk[:, x * LANE:(x + 1) * LANE], qq[:, x * LANE:(x + 1) * LANE]
                st = jnp.where(keep, _dot(kx, qx, ((1,), (1,))) * ATT_SCALE_LOG2, NEG)
                pt = jnp.exp2(st - rr[x:x + 1, :])
                dpt = _dot(jnp.where(sel, vv, 0), dd, ((1,), (1,)))
                dst = (pt * (dpt - rr[2 + x:3 + x, :]) * ATT_SCALE).astype(BF16)
                out.append(carry[x] + _dot(dst, qx, ((1,), (0,))))
                out.append(_dot(pt, jnp.where(sel, dd, 0), ((1,), (0,))))
                dqt_ref[x * LANE:(x + 1) * LANE, pl.ds(off, TQ)] += _dot(kt_ref[x * LANE:(x + 1) * LANE, :], dst, ((1,), (0,)))
            return out[0], out[2], carry[2] + out[1] + out[3]

        z = jnp.zeros((TK, LANE), F32)
        dka, dkb, dv = lax.fori_loop(kb, nq, step, (z, z, z))
        dk_ref[:, 0:LANE] = dka
        dk_ref[:, LANE:2 * LANE] = dkb
        dv_ref[...] = dv.astype(BF16)

    return pl.pallas_call(
        body, name="attn_bwd", grid=(NPAIR, S // TK),
        in_specs=[pl.BlockSpec((S, 2 * LANE), lambda j, k: (0, j)), pl.BlockSpec((TK, 2 * LANE), lambda j, k: (k, j)),
                  pl.BlockSpec((2 * LANE, TK), lambda j, k: (j, k)), pl.BlockSpec((TK, LANE), lambda j, k: (k, j)),
                  pl.BlockSpec((S, LANE), lambda j, k: (0, j)), pl.BlockSpec((None, 8, S), lambda j, k: (j, 0, 0))],
        out_specs=[pl.BlockSpec((2 * LANE, S), lambda j, k: (j, 0)), pl.BlockSpec((TK, 2 * LANE), lambda j, k: (k, j)),
                   pl.BlockSpec((TK, LANE), lambda j, k: (k, j))],
        out_shape=[jax.ShapeDtypeStruct((H * LANE, S), F32), jax.ShapeDtypeStruct((S, H * LANE), F32),
                   jax.ShapeDtypeStruct((S, H * VDIM), BF16)],
        compiler_params=pltpu.CompilerParams(dimension_semantics=("parallel", "arbitrary")),
    )(qc, kc, kct, v, do, rows)


_IN_Z, _IN_XBC, _IN_DT, _IN_Q, _IN_KV, _IN_KR = 0, 1024, 2560, 2576, 2960, 3216


PROJ_COLS = 512
SMALL_PAD = pl.cdiv(SMALL_W, PROJ_COLS) * PROJ_COLS


def _prep_in(w_in_t):
    dt = w_in_t.dtype
    return jnp.concatenate(
        [w_in_t[_IN_Q:_IN_KV], w_in_t[_IN_KV:_IN_KR], w_in_t[_IN_KR:IN_WIDTH], jnp.zeros((LANE - ROPE, D), dt),
         w_in_t[_IN_DT:_IN_Q], jnp.zeros((SMALL_PAD - SM_DT - H, D), dt)], axis=0)


def _proj_in(xb, w_in_t, w_small):
    nz, nx, ns = (_IN_XBC - _IN_Z) // PROJ_COLS, (_IN_DT - _IN_XBC) // PROJ_COLS, SMALL_PAD // PROJ_COLS

    dt_block, dt_at = divmod(SM_DT, PROJ_COLS)

    def body(x_ref, w_ref, ws_ref, z_ref, xbc_ref, sm_ref, dtt_ref):
        i = pl.program_id(0)

        def emit(w, o_ref):
            o_ref[...] = lax.dot_general(x_ref[...], w[...], (((1,), (1,)), ((), ())), preferred_element_type=F32)

        pl.when(i < nz)(lambda: emit(w_ref, z_ref))
        pl.when((i >= nz) & (i < nz + nx))(lambda: emit(w_ref, xbc_ref))
        pl.when(i >= nz + nx)(lambda: emit(ws_ref, sm_ref))

        @pl.when(i == nz + nx + dt_block)
        def _():
            dtt_ref[...] = sm_ref[:, dt_at:dt_at + LANE].T

    def blocks(first, count, rows):
        at = lambda i: jnp.clip(i - first, 0, count - 1)
        return pl.BlockSpec((PROJ_COLS, D), lambda i: (at(i), 0)) if rows else pl.BlockSpec((S, PROJ_COLS), lambda i: (0, at(i)))

    return pl.pallas_call(
        body, name="proj_in", grid=(nz + nx + ns,),
        in_specs=[pl.BlockSpec((S, D), lambda i: (0, 0)), blocks(0, nz + nx, True), blocks(nz + nx, ns, True)],
        out_specs=[blocks(0, nz, False), blocks(nz, nx, False), blocks(nz + nx, ns, False), pl.BlockSpec((LANE, S), lambda i: (0, 0))],
        out_shape=[jax.ShapeDtypeStruct((S, _IN_XBC - _IN_Z), F32), jax.ShapeDtypeStruct((S, _IN_DT - _IN_XBC), F32),
                   jax.ShapeDtypeStruct((S, SMALL_W), F32), jax.ShapeDtypeStruct((LANE, S), F32)],
    )(xb, w_in_t, w_small)


PART_COLS = 512


def _part_blocks(widths):
    first = [0]
    for w in widths:
        first.append(first[-1] + w // PART_COLS)

    def at(part):
        return lambda i: jnp.clip(i - first[part], 0, first[part + 1] - first[part] - 1)

    return first, at


def _mm_ta_stacked(parts, b, rows, name):
    n = b.shape[1]
    first, at = _part_blocks([a.shape[1] for a in parts])
    assert first[-1] == pl.cdiv(rows, PART_COLS)

    def body(*refs):
        b_ref, o_ref = refs[-2:]
        i = pl.program_id(0)
        for part, a_ref in enumerate(refs[:-2]):
            @pl.when((i >= first[part]) & (i < first[part + 1]))
            def _(a_ref=a_ref):
                o_ref[...] = lax.dot_general(a_ref[...], b_ref[...], (((0,), (0,)), ((), ())),
                                             preferred_element_type=F32).astype(BF16)

    return pl.pallas_call(
        body, name=name, grid=(first[-1],),
        in_specs=[pl.BlockSpec((S, PART_COLS), lambda i, at=at(part): (0, at(i))) for part in range(len(parts))]
        + [pl.BlockSpec((S, n), lambda i: (0, 0))],
        out_specs=pl.BlockSpec((PART_COLS, n), lambda i: (i, 0)), out_shape=jax.ShapeDtypeStruct((rows, n), BF16),
    )(*parts, b)


def _prep_attn(w_qb, w_kvb):
    w_q = jnp.pad(w_qb.reshape(Q_RANK, H, NOPE + ROPE), ((0, 0), (0, 0), (0, LANE - NOPE - ROPE))).reshape(Q_RANK, H * LANE)
    kv3 = w_kvb.reshape(KV_RANK, H, NOPE + VDIM)
    w_k = jnp.pad(kv3[:, :, :NOPE], ((0, 0), (0, 0), (0, LANE - NOPE))).reshape(KV_RANK, H * LANE)
    w_v = kv3[:, :, NOPE:].reshape(KV_RANK, H * VDIM)
    return w_q, w_k, w_v


def _rope_tables(positions):
    inv_freq = 1.0 / (10000.0 ** (jnp.arange(0, ROPE, 2, dtype=F32) / ROPE))
    ang = positions.astype(F32).reshape(S, 1) * inv_freq
    cos, sin = jnp.cos(ang), jnp.sin(ang)
    cos_t = jnp.concatenate([jnp.ones((S, NOPE), F32), cos, cos, jnp.ones((S, LANE - NOPE - ROPE), F32)], axis=1)
    sin_t = jnp.concatenate([jnp.zeros((S, NOPE), F32), -sin, sin, jnp.zeros((S, LANE - NOPE - ROPE), F32)], axis=1)
    return cos_t, sin_t


def _local_step(x, p, positions, target, w_in, fetch, send, sp, started):
    w_in_t = w_in.reshape(IN_WIDTH, D)
    w_small = _prep_in(w_in_t)
    cos_t, sin_t = _rope_tables(positions)
    prow = jnp.zeros((8, LANE), F32).at[0, :H].set(sp["dt_bias"][0]).at[1, :H].set(sp["A_log"][0]).at[2, :H].set(sp["D"][0])
    pcol = prow.T

    xb, pb = (x + started).astype(BF16), p.astype(BF16)
    z, xbc, small, dt_t = _proj_in(xb, w_in_t, w_small)
    act = _conv_fwd(xbc, sp["conv_w"], sp["conv_b"])
    y, states = _ssd_fwd(act, small, dt_t, prow, pcol)
    y_ssd = _gate_norm_fwd(y, z, sp["ssd_norm"])
    gl = fetch("attn", y_ssd)
    w_q, w_k, w_v = _prep_attn(_from_cols(gl["w_qb"]), _from_cols(gl["w_kvb"]))
    qn, kvn, qcat, kcat, kcat_t, v = _qkv_fwd(small, w_q, w_k, w_v, sp["q_norm"], sp["kv_norm"], cos_t, sin_t)
    o, lse = _attn_fwd(qcat, kcat, v)
    y_mla = _rms_fwd(o, sp["out_norm"], name="out_norm_fwd")
    w_out = fetch("out", y_mla)["w_out"]
    w_out = w_out.reshape(2 * SSD_INNER, D)
    mix, h1, h1b = _out_proj_ln(y_ssd, y_mla, w_out, x, sp["ln_mix_g"], sp["ln_mix_b"])
    gl = fetch("ffn", h1b)
    w_pg, w_pp = gl["w_pg"].reshape(D, D), _from_cols(gl["w_pp"])
    w_gate, w_up, w_down = gl["w_gate"], gl["w_up"], gl["w_down"]
    gate, up, actf = _ffn_hidden_fwd(h1b, w_gate, w_up)
    ffn = _mm([(actf, w_down)], chunk="sum", name="ffn_down")
    dpre2, dpre2b, dpg, dpp, dg2, db2, loss_row = _final_fwd_bwd(h1, ffn, h1b, pb, w_pg, w_pp, target, sp["ln_ffn_g"], sp["ln_ffn_b"])

    g = {"ln_ffn_g": dg2, "ln_ffn_b": db2}
    g["w_pp"] = _to_cols(_mm([(pb, dpp)], ta=True, out_dtype=BF16, name="d_w_ple_proj"))
    g["w_pg"] = _mm([(h1b, dpg)], ta=True, out_dtype=BF16, name="d_w_ple_gate").reshape(NCHIP, D // NCHIP, D)
    g["w_down"] = _mm([(actf, dpre2b)], ta=True, chunk="out", out_dtype=BF16, name="d_w_down")
    dgate, dup = _ffn_hidden_bwd(dpre2b, w_down, gate, up)
    g["w_gate"] = _mm([(dgate, h1b)], ta=True, chunk="out", out_dtype=BF16, name="d_w_gate")
    g["w_up"] = _mm([(dup, h1b)], ta=True, chunk="out", out_dtype=BF16, name="d_w_up")
    sent = send("ffn", {name: g.pop(name) for name in dict(ASYNC_GROUPS)["ffn"]})
    dh1 = _mm([(dgate, w_gate), (dup, w_up), (dpg, w_pg.T)], chunk="sum", add=dpre2, add_scale=ALPHA, name="d_h1")
    dpre1, dpre1b, g["ln_mix_g"], g["ln_mix_b"], dy_ssd, dy_mla = _ln_bwd(x, mix, sp["ln_mix_g"] + sent, dh1, w_out)
    dw_out = _mm_ta_stacked((y_ssd, y_mla), dpre1b, 2 * SSD_INNER, "d_w_out")
    sent = send("out", {"w_out": dw_out.reshape(NCHIP, 2 * SSD_INNER // NCHIP, D)})
    do, g["out_norm"] = _rms_bwd(o, sp["out_norm"] + sent, dy_mla, name="out_norm_bwd")
    dqt, dk, dv = _attn_bwd(qcat, kcat, kcat_t, v, do, _attn_rows(lse, o, do))
    dlatent, dqlin, dkb, g["q_norm"], g["kv_norm"] = _qkv_bwd(dqt, dk, dv, small, w_q, w_k, w_v, sp["q_norm"], sp["kv_norm"], cos_t, sin_t)
    dw_q = _mm([(qn, dqlin)], ta=True, out_dtype=BF16, name="d_w_q")
    dw_k = _mm([(kvn, dkb)], ta=True, out_dtype=BF16, name="d_w_k")
    dw_v = _mm([(kvn, dv)], ta=True, out_dtype=BF16, name="d_w_v")
    dw_qb = _to_cols(dw_q.reshape(Q_RANK, H, LANE)[:, :, :NOPE + ROPE].reshape(Q_RANK, H * (NOPE + ROPE)))
    dw_kvb = _to_cols(jnp.concatenate([dw_k.reshape(KV_RANK, H, LANE)[:, :, :NOPE], dw_v.reshape(KV_RANK, H, VDIM)],
                                       axis=2).reshape(KV_RANK, H * (NOPE + VDIM)))
    sent = send("attn", {"w_qb": dw_qb, "w_kvb": dw_kvb})
    dy, dz, g["ssd_norm"] = _gate_norm_bwd(y, z, sp["ssd_norm"] + sent, dy_ssd)
    dact, ddt, dprow = _ssd_bwd(act, small, dt_t, prow, pcol, states, dy)
    g["dt_bias"], g["A_log"], g["D"] = dprow[0:1, :H], dprow[1:2, :H], dprow[2:3, :H]
    dxbc, g["conv_w"], g["conv_b"] = _conv_bwd(xbc, sp["conv_w"], sp["conv_b"], dact)
    dsmall = jnp.concatenate([dlatent, ddt.astype(BF16)], axis=1)
    in_blocks = [(d, w_in_t, (k, first // PROJ_COLS + k, PROJ_COLS))
                 for d, first in ((dz, _IN_Z), (dxbc, _IN_XBC)) for k in range(d.shape[1] // PROJ_COLS)]
    grad_x = _mm(in_blocks + [(dsmall, w_small, (0, 0, SMALL_W))], add=dpre1, add_scale=ALPHA, name="d_x")
    sent = send("small", dict(g, loss=loss_row))
    n_small = IN_WIDTH - _IN_DT
    dsm = jnp.concatenate([(ddt[:, :H] + sent).astype(BF16), dlatent[:, :n_small - H], jnp.zeros((S, D - n_small), BF16)], axis=1)
    dw_in = _mm_ta_stacked((dz, dxbc, dsm), xb, IN_WIDTH, "d_w_in").reshape(NCHIP, IN_WIDTH // NCHIP * D // LANE, LANE)
    return loss_row, grad_x, dw_in, g


MESH = pl.DeviceIdType.MESH
BIG = (("w_in", (D, IN_WIDTH), 1), ("w_qb", (Q_RANK, H * (NOPE + ROPE)), 1), ("w_kvb", (KV_RANK, H * (NOPE + VDIM)), 1),
       ("w_out", (2 * SSD_INNER, D), 0), ("w_gate", (D, D_FF), 1), ("w_up", (D, D_FF), 1), ("w_down", (D_FF, D), 0),
       ("w_pg", (D, D), 0), ("w_pp", (PLE, D), 1))
CONV_SHARD = SSD_XBC // NCHIP
BF16_ROWS = 16


def _from_cols(stack):
    return jnp.concatenate([stack[k] for k in range(NCHIP)], axis=1)


def _to_cols(full):
    r, c4 = full.shape
    return full.reshape(r, NCHIP, c4 // NCHIP).transpose(1, 0, 2)


def _coords():
    return lax.axis_index("x"), lax.axis_index("y"), lax.axis_index("c")


def _peers():
    x, y, c = _coords()
    return 2 * x + y, c, [(1 - x, y), (x, 1 - y), (1 - x, 1 - y)], (x, y, 1 - c)


def _half_axis(shape):
    return 0 if shape[-2] % (2 * BF16_ROWS) == 0 else 1


def _half_shape(shape):
    r, c = shape[-2:]
    return (r // 2, c) if _half_axis(shape) == 0 else (r, c // 2)


def _half(core, shape):
    r, c = shape[-2:]
    if _half_axis(shape) == 0:
        return pl.ds(pl.multiple_of(core * (r // 2), BF16_ROWS), r // 2), slice(None)
    return slice(None), pl.ds(pl.multiple_of(core * (c // 2), LANE), c // 2)


def _gather_weights(shards):
    n_arr = len(shards)
    per = 2 * (NCHIP - 1)

    def body(*refs):
        ins, outs = refs[:n_arr], refs[n_arr:2 * n_arr]
        send_sems, recv_sems, local_sems = refs[2 * n_arr:]
        k, c, chips, sibling = _peers()

        def copy(idx, src, dst, to):
            return pltpu.make_async_remote_copy(src_ref=src, dst_ref=dst, send_sem=send_sems.at[idx], recv_sem=recv_sems.at[idx],
                                                device_id=to, device_id_type=MESH)

        def part(a, chip, core):
            return outs[a].at[chip, *_half(core, shards[a].shape)]

        mine = [pltpu.make_async_copy(ins[a], outs[a].at[k], local_sems.at[a]) for a in range(n_arr)]
        for cp in mine:
            cp.start()
        sends = []
        for a in range(n_arr):
            for j, (cx, cy) in enumerate(chips):
                sends.append(copy(per * a + j, ins[a].at[*_half(c, shards[a].shape)], part(a, k, c), (cx, cy, c)))
                sends[-1].start()
        for j, (cx, cy) in enumerate(chips):
            for a in range(n_arr):
                landed = part(a, 2 * cx + cy, c)
                copy(per * a + j, landed, landed, (cx, cy, c)).wait_recv()
                sends.append(copy(per * a + NCHIP - 1 + j, landed, landed, sibling))
                sends[-1].start()
        for j, (cx, cy) in enumerate(chips):
            for a in range(n_arr):
                other = part(a, 2 * cx + cy, 1 - c)
                copy(per * a + NCHIP - 1 + j, other, other, sibling).wait_recv()
        for cp in sends:
            cp.wait_send()
        for cp in mine:
            cp.wait()

    any_spec = pl.BlockSpec(memory_space=pl.ANY)
    return pl.pallas_call(
        body, name="gather_weights", in_specs=[any_spec] * n_arr, out_specs=[any_spec] * n_arr,
        out_shape=[jax.ShapeDtypeStruct((NCHIP,) + s.shape, s.dtype) for s in shards],
        scratch_shapes=[pltpu.SemaphoreType.DMA((per * n_arr,)), pltpu.SemaphoreType.DMA((per * n_arr,)),
                        pltpu.SemaphoreType.DMA((n_arr,))],
    )(*shards)


ASYNC_GROUPS = (("attn", ("w_qb", "w_kvb")), ("out", ("w_out",)), ("ffn", ("w_gate", "w_up", "w_down", "w_pg", "w_pp")))
TRANSPOSED = ("w_in", "w_gate", "w_up")
ROW_MAJOR = ("w_in",)
HBM_SPEC = pl.BlockSpec(memory_space=pltpu.HBM)
SEM_SPEC = pl.BlockSpec(memory_space=pltpu.SEMAPHORE)
IN_FLIGHT = pltpu.SideEffectType.DATAFLOW_SIDE_EFFECTING


def _in_hbm(a):
    return pltpu.with_memory_space_constraint(a, pltpu.HBM)


def _hbm_like(arrs, lead=()):
    return [pltpu.HBM(lead + a.shape, a.dtype) for a in arrs]


def _split_start(name, srcs, lands, after, n_sem, start):
    n = len(srcs)
    order = [] if after is None else [after]

    def body(*refs):
        src_refs, land_refs = refs[:n], refs[n:2 * n]
        send_sems, recv_sems = refs[2 * n + len(order)], refs[2 * n + len(order) + 1]
        token = refs[-1]

        def copy(send_idx, recv_idx, src, dst, to):
            return pltpu.make_async_remote_copy(src_ref=src, dst_ref=dst, send_sem=send_sems.at[send_idx],
                                                recv_sem=recv_sems.at[recv_idx], device_id=to, device_id_type=MESH)

        for cp in start(src_refs, land_refs, copy):
            cp.start()
        token[...] = jnp.zeros_like(token)

    sem = pltpu.SemaphoreType.DMA((n_sem,))
    outs = pl.pallas_call(
        body, name=name, in_specs=[HBM_SPEC] * (2 * n) + [pl.BlockSpec(memory_space=pl.ANY)] * len(order),
        out_specs=[SEM_SPEC, SEM_SPEC] + [HBM_SPEC] * (2 * n) + [pl.BlockSpec(memory_space=pltpu.VMEM)],
        out_shape=[sem, sem] + _hbm_like(srcs) + _hbm_like(lands) + [jax.ShapeDtypeStruct((8, LANE), F32)],
        input_output_aliases={i: 2 + i for i in range(2 * n)},
        compiler_params=pltpu.CompilerParams(has_side_effects=IN_FLIGHT),
    )(*[_in_hbm(a) for a in srcs], *[_in_hbm(a) for a in lands], *order)
    return (outs[0], outs[1], outs[2:2 + n], outs[2 + n:2 + 2 * n]), outs[-1]


def _split_wait(name, send_sems, recv_sems, srcs, lands, after, waits):
    n = len(srcs)

    def body(*refs):
        src_refs, land_refs = refs[:n], refs[n:2 * n]
        send_ref, recv_ref = refs[2 * n], refs[2 * n + 1]

        def copy(send_idx, recv_idx, src, dst, to):
            return pltpu.make_async_remote_copy(src_ref=src, dst_ref=dst, send_sem=send_ref.at[send_idx],
                                                recv_sem=recv_ref.at[recv_idx], device_id=to, device_id_type=MESH)

        for cp in waits(src_refs, land_refs, copy):
            cp.wait_send()
            cp.wait_recv()

    outs = pl.pallas_call(
        body, name=name, in_specs=[HBM_SPEC] * (2 * n) + [SEM_SPEC, SEM_SPEC, pl.BlockSpec(memory_space=pl.ANY)],
        out_specs=[HBM_SPEC] * (2 * n), out_shape=_hbm_like(srcs) + _hbm_like(lands),
        input_output_aliases={i: i for i in range(2 * n)},
        compiler_params=pltpu.CompilerParams(has_side_effects=IN_FLIGHT),
    )(*srcs, *lands, send_sems, recv_sems, after)
    return outs[:n], outs[n:]


GATHER_LATE_SEMS = 2 * (NCHIP - 1)


def _gather_async_start(tag, shards, after):
    def start(srcs, lands, copy):
        k, c, chips, _ = _peers()
        out = []
        for a, (src, dst) in enumerate(zip(srcs, lands)):
            for j, (cx, cy) in enumerate(chips):
                for core in range(2):
                    out.append(copy(GATHER_LATE_SEMS * a + 2 * j + core, GATHER_LATE_SEMS * a + 2 * j + c,
                                    src.at[*_half(c, src.shape)], dst.at[k, *_half(c, src.shape)], (cx, cy, core)))
        return out

    chip = 2 * lax.axis_index("x") + lax.axis_index("y")
    lands = [lax.dynamic_update_slice(lax.empty((NCHIP,) + s.shape, s.dtype), s[None], (chip, 0, 0)) for s in shards]
    return _split_start("gather_%s_start" % tag, shards, lands, after, GATHER_LATE_SEMS * len(shards), start)


def _gather_async_wait(tag, send_sems, recv_sems, shards, lands, after):
    def waits(srcs, lands_, copy):
        _, c, chips, _ = _peers()
        out = []
        for a, (src, dst) in enumerate(zip(srcs, lands_)):
            for j, (cx, cy) in enumerate(chips):
                for core in range(2):
                    idx = GATHER_LATE_SEMS * a + 2 * j + core
                    out.append(copy(idx, idx, src.at[*_half(c, src.shape)], dst.at[2 * cx + cy, *_half(core, src.shape)], (cx, cy, core)))
        return out

    return _split_wait("gather_%s_wait" % tag, send_sems, recv_sems, shards, lands, after, waits)[1]


def _other_devices():
    x, y, c = _coords()
    out = []
    for d in range(1, NDEV):
        tx, ty, tc = x ^ (d >> 2), y ^ ((d >> 1) & 1), c ^ (d & 1)
        out.append((d, (tx, ty, tc), 2 * tx + ty, 4 * tx + 2 * ty + tc))
    return out


def _reduce_async_start(tag, stacks, after):
    def start(srcs, lands, copy):
        x, y, c = _coords()
        me = 4 * x + 2 * y + c
        return [copy((NDEV - 1) * a + d - 1, (NDEV - 1) * a + d - 1, src.at[chip, *_half(to[2], src.shape)], dst.at[me], to)
                for a, (src, dst) in enumerate(zip(srcs, lands)) for d, to, chip, _ in _other_devices()]

    x, y, c = _coords()
    lands = []
    for s in stacks:
        hr, hc = _half_shape(s.shape)
        at = (c * hr, 0) if _half_axis(s.shape) == 0 else (0, c * hc)
        own = lax.dynamic_slice(s, (2 * x + y,) + at, (1, hr, hc))
        lands.append(lax.dynamic_update_slice(lax.empty((NDEV, hr, hc), s.dtype), own, (4 * x + 2 * y + c, 0, 0)))
    return _split_start("reduce_%s_start" % tag, stacks, lands, after, (NDEV - 1) * len(stacks), start)


def _reduce_async_wait(tag, send_sems, recv_sems, stacks, lands, after):
    def waits(srcs, lands_, copy):
        return [copy((NDEV - 1) * a + d - 1, (NDEV - 1) * a + d - 1, src.at[chip, *_half(to[2], src.shape)], dst.at[pos], to)
                for a, (src, dst) in enumerate(zip(srcs, lands_)) for d, to, chip, pos in _other_devices()]

    return _split_wait("reduce_%s_wait" % tag, send_sems, recv_sems, stacks, lands, after, waits)[1]


def _reduce_finish(tag, arrived, dims):
    n_arr = len(arrived)

    def body(*refs):
        lands, fin = refs[:n_arr], refs[n_arr:2 * n_arr]
        send_sems, recv_sems = refs[2 * n_arr:]
        _, c, _, sibling = _peers()
        sends = []
        for a in range(n_arr):
            mine = fin[a].at[*_half(c, dims[a])]

            def device_sum(vs, vf, a=a, mine=mine):
                pltpu.sync_copy(lands[a], vs)
                acc = vs[0].astype(F32)
                for i in range(1, NDEV):
                    acc = acc + vs[i].astype(F32)
                vf[...] = acc
                pltpu.sync_copy(vf, mine)

            pl.run_scoped(device_sum, pltpu.VMEM((NDEV,) + _half_shape(dims[a]), BF16), pltpu.VMEM(_half_shape(dims[a]), F32))
            sends.append(pltpu.make_async_remote_copy(src_ref=mine, dst_ref=mine, send_sem=send_sems.at[a], recv_sem=recv_sems.at[a],
                                                      device_id=sibling, device_id_type=MESH))
            sends[-1].start()
        for a in range(n_arr):
            other = fin[a].at[*_half(1 - c, dims[a])]
            pltpu.make_async_remote_copy(src_ref=other, dst_ref=other, send_sem=send_sems.at[a], recv_sem=recv_sems.at[a],
                                         device_id=sibling, device_id_type=MESH).wait_recv()
        for cp in sends:
            cp.wait_send()

    any_spec = pl.BlockSpec(memory_space=pl.ANY)
    return pl.pallas_call(
        body, name="reduce_%s_finish" % tag, in_specs=[any_spec] * n_arr, out_specs=[any_spec] * n_arr,
        out_shape=[jax.ShapeDtypeStruct(d, F32) for d in dims],
        scratch_shapes=[pltpu.SemaphoreType.DMA((n_arr,)), pltpu.SemaphoreType.DMA((n_arr,))],
    )(*arrived)


SMALL = (("conv_w", SSD_K * SSD_XBC), ("conv_b", SSD_XBC), ("dt_bias", H), ("A_log", H), ("D", H), ("ssd_norm", SSD_INNER),
         ("q_norm", Q_RANK), ("kv_norm", KV_RANK), ("out_norm", SSD_INNER), ("ln_mix_g", D), ("ln_mix_b", D),
         ("ln_ffn_g", D), ("ln_ffn_b", D))
SMALL_ROWS = 120
NDEV = 8


def _allreduce_small_start(sv):
    def start(srcs, lands, copy):
        x, y, c = _coords()
        return [copy(d - 1, d - 1, srcs[0], lands[0].at[4 * x + 2 * y + c], to) for d, to, _, _ in _other_devices()]

    x, y, c = _coords()
    slots = lax.dynamic_update_slice(lax.empty((NDEV,) + sv.shape, sv.dtype), sv[None], (4 * x + 2 * y + c, 0, 0))
    return _split_start("allreduce_small_start", [sv], [slots], None, NDEV - 1, start)


def _allreduce_small_wait(send_sems, recv_sems, srcs, lands, after):
    def waits(srcs_, lands_, copy):
        return [copy(d - 1, d - 1, srcs_[0], lands_[0].at[pos], to) for d, to, _, pos in _other_devices()]

    def device_sum(slots_ref, out_ref):
        acc = slots_ref[0]
        for i in range(1, NDEV):
            acc = acc + slots_ref[i]
        out_ref[...] = acc

    slots = _split_wait("allreduce_small_wait", send_sems, recv_sems, srcs, lands, after, waits)[1][0]
    vm = pl.BlockSpec(memory_space=pltpu.VMEM)
    return pl.pallas_call(device_sum, name="allreduce_small_sum", in_specs=[vm], out_specs=vm,
                          out_shape=jax.ShapeDtypeStruct(slots.shape[1:], slots.dtype))(slots)


def _adamw_math(w, g, m, v):
    m2 = ADAM_B1 * m + (1.0 - ADAM_B1) * g
    v2 = ADAM_B2 * v + (1.0 - ADAM_B2) * (g * g)
    m_hat = m2 / (1.0 - ADAM_B1 ** ADAM_STEP)
    v_hat = v2 / (1.0 - ADAM_B2 ** ADAM_STEP)
    return -ADAM_LR * (m_hat / (jnp.sqrt(v_hat) + ADAM_EPS) + ADAM_WD * w), m2, v2


ADAM_BLOCK_BYTES = 2 * 1024 * 1024


def _adamw_big(w, g, m, v, *, name):
    r, c = w.shape

    def body(w_ref, g_ref, m_ref, v_ref, d_ref, m2_ref, v2_ref):
        d_ref[...], m2_ref[...], v2_ref[...] = _adamw_math(w_ref[...], g_ref[...], m_ref[...], v_ref[...])

    tr = max(t for t in range(8, r + 1, 8) if r % t == 0 and t * c * 4 <= ADAM_BLOCK_BYTES)
    steps, spec = r // tr, pl.BlockSpec((tr, c), lambda i: (i, 0))
    return pl.pallas_call(body, name=name, grid=(steps,), in_specs=[spec] * 4, out_specs=[spec] * 3,
                          out_shape=[jax.ShapeDtypeStruct((r, c), F32)] * 3)(w, g, m, v)


def _adamw_small(ws, gs, ms, vs):
    n = len(ws)

    def body(*refs):
        for i in range(n):
            w_ref, g_ref, m_ref, v_ref = (refs[j * n + i] for j in range(4))
            d_ref, m2_ref, v2_ref = (refs[(4 + j) * n + i] for j in range(3))
            d_ref[...], m2_ref[...], v2_ref[...] = _adamw_math(w_ref[...], g_ref[...], m_ref[...], v_ref[...])

    vm = pl.BlockSpec(memory_space=pltpu.VMEM)
    shapes = [jax.ShapeDtypeStruct(w.shape, F32) for w in ws]
    outs = pl.pallas_call(body, name="adamw_small", in_specs=[vm] * (4 * n), out_specs=[vm] * (3 * n), out_shape=shapes * 3)(
        *ws, *gs, *ms, *vs)
    return outs[:n], outs[n:2 * n], outs[2 * n:]


_SMALL_ARG = {"conv_w": "ssd_conv_w", "conv_b": "ssd_conv_b", "dt_bias": "ssd_dt_bias", "A_log": "ssd_A_log", "D": "ssd_D",
              "ssd_norm": "ssd_norm_w", "q_norm": "mla_q_norm_w", "kv_norm": "mla_kv_norm_w", "out_norm": "mla_out_norm_w",
              "ln_mix_g": "ln_mix_g", "ln_mix_b": "ln_mix_b", "ln_ffn_g": "ln_ffn_g", "ln_ffn_b": "ln_ffn_b"}
_BIG_ARG = {"w_in": "w_in", "w_qb": "mla_w_q_b", "w_kvb": "mla_w_kv_b", "w_out": "w_out", "w_gate": "w_ffn_gate",
            "w_up": "w_ffn_up", "w_down": "w_ffn_down", "w_pg": "w_ple_gate", "w_pp": "w_ple_proj"}
_WEIGHT_ORDER = ("w_in", "ssd_conv_w", "ssd_conv_b", "ssd_dt_bias", "ssd_A_log", "ssd_D", "ssd_norm_w", "mla_q_norm_w", "mla_w_q_b",
                 "mla_kv_norm_w", "mla_w_kv_b", "mla_out_norm_w", "w_out", "ln_mix_g", "ln_mix_b", "w_ffn_gate", "w_ffn_up",
                 "w_ffn_down", "w_ple_gate", "w_ple_proj", "ln_ffn_g", "ln_ffn_b")


def _rows128(a):
    flat = a.reshape(-1)
    return jnp.pad(flat, (0, -flat.shape[0] % LANE)).reshape(-1, LANE)


def kernel(x, p, positions, w_in, ssd_conv_w, ssd_conv_b, ssd_dt_bias, ssd_A_log, ssd_D, ssd_norm_w, mla_q_norm_w, mla_w_q_b, mla_kv_norm_w, mla_w_kv_b, mla_out_norm_w, w_out, ln_mix_g, ln_mix_b, w_ffn_gate, w_ffn_up, w_ffn_down, w_ple_gate, w_ple_proj, ln_ffn_g, ln_ffn_b, loss_target, m_w_in, m_ssd_conv_w, m_ssd_conv_b, m_ssd_dt_bias, m_ssd_A_log, m_ssd_D, m_ssd_norm_w, m_mla_q_norm_w, m_mla_w_q_b, m_mla_kv_norm_w, m_mla_w_kv_b, m_mla_out_norm_w, m_w_out, m_ln_mix_g, m_ln_mix_b, m_w_ffn_gate, m_w_ffn_up, m_w_ffn_down, m_w_ple_gate, m_w_ple_proj, m_ln_ffn_g, m_ln_ffn_b, v_w_in, v_ssd_conv_w, v_ssd_conv_b, v_ssd_dt_bias, v_ssd_A_log, v_ssd_D, v_ssd_norm_w, v_mla_q_norm_w, v_mla_w_q_b, v_mla_kv_norm_w, v_mla_w_kv_b, v_mla_out_norm_w, v_w_out, v_ln_mix_g, v_ln_mix_b, v_w_ffn_gate, v_w_ffn_up, v_w_ffn_down, v_w_ple_gate, v_w_ple_proj, v_ln_ffn_g, v_ln_ffn_b):
    given = dict(locals())
    chip = 2 * lax.axis_index("x") + lax.axis_index("y")

    def local(name, prefix=""):
        a = given[prefix + _BIG_ARG[name]][0]
        return a.T if name in TRANSPOSED else a

    def updated(name, prefix=""):
        if name in ROW_MAJOR:
            _, c, r = given[prefix + _BIG_ARG[name]].shape
            return given[prefix + _BIG_ARG[name]].reshape(c // LANE, LANE, r).transpose(2, 0, 1).reshape(-1, LANE)
        return local(name, prefix)

    def global_layout(name, arr):
        if name in ROW_MAJOR:
            r, c = local(name).shape
            return arr.reshape(r, c // LANE, LANE).transpose(1, 2, 0).reshape(1, c, r)
        return (arr.T if name in TRANSPOSED else arr)[None]

    conv_bits = lax.bitcast_convert_type(ssd_conv_w[0], BF16).reshape(SSD_K, 2 * CONV_SHARD)
    w_in_all, conv_all = _gather_weights([local("w_in").astype(BF16), jnp.pad(conv_bits, ((0, BF16_ROWS - SSD_K), (0, 0)))])
    sp = {k: given[a] for k, a in _SMALL_ARG.items() if k != "conv_w"}
    sp["conv_w"] = _from_cols(lax.bitcast_convert_type(conv_all[:, :SSD_K].reshape(NCHIP, SSD_K, CONV_SHARD, 2), F32))
    gathering, tie = {}, w_in_all
    for group, names in ASYNC_GROUPS:
        gathering[group], tie = _gather_async_start(group, [local(name).astype(BF16) for name in names], tie)

    def fetch(group, after):
        return dict(zip(dict(ASYNC_GROUPS)[group], _gather_async_wait(group, *gathering[group], after)))

    reducing = {}

    def send(group, grads):
        if group == "small":
            rows = jnp.concatenate([_rows128(grads[name]) for name, _ in SMALL] + [grads["loss"]], axis=0)
            reducing[group], sent = _allreduce_small_start(jnp.pad(rows, ((0, SMALL_ROWS - rows.shape[0]), (0, 0))))
        else:
            reducing[group], sent = _reduce_async_start(group, [grads[name] for name in dict(ASYNC_GROUPS)[group]], None)
        return sent[0, 0]

    loss_row, grad_x, dw_in, g = _local_step(x[0], p[0, 0], positions[0], loss_target[0], w_in_all, fetch, send, sp, tie[0, 0])

    reducing["in"], tie = _reduce_async_start("in", [dw_in], grad_x)
    gbig = {}
    for group, names in reversed(ASYNC_GROUPS):
        arrived = _reduce_async_wait(group, *reducing[group], tie)
        gbig.update(zip(names, _reduce_finish(group, arrived, [local(name).shape for name in names])))
    small_sum = _allreduce_small_wait(*reducing.pop("small"), tie)
    gsmall, row = {}, 0
    for name, size in SMALL:
        nrow = -(-size // LANE)
        gsmall[name] = small_sum[row:row + nrow].reshape(-1)[:size]
        row += nrow
    loss = small_sum[row, 0]

    grads = {_BIG_ARG[name]: global_layout(name, arr) for name, arr in gbig.items()}
    for name, _ in SMALL:
        if name == "conv_w":
            full_g = gsmall[name].reshape(SSD_K, SSD_XBC)
            grads["ssd_conv_w"] = lax.dynamic_slice(full_g, (0, chip * CONV_SHARD), (SSD_K, CONV_SHARD))[None]
        else:
            grads[_SMALL_ARG[name]] = gsmall[name].reshape(given[_SMALL_ARG[name]].shape)

    delta, new_m, new_v = {}, {}, {}

    def update_matrix(name, grad):
        a = _BIG_ARG[name]
        d, m2, v2 = _adamw_big(updated(name), grad, updated(name, "m_"), updated(name, "v_"), name="adamw_" + a)
        delta[a], new_m[a], new_v[a] = (global_layout(name, t) for t in (d, m2, v2))
        return d

    for name, grad in gbig.items():
        last = update_matrix(name, grad)
    g_in = _reduce_finish("in", _reduce_async_wait("in", *reducing["in"], last), [updated("w_in").shape])[0]
    grads["w_in"] = global_layout("w_in", g_in)
    update_matrix("w_in", g_in)
    small_names = [_SMALL_ARG[name] for name, _ in SMALL]
    two_d = lambda t: t.reshape(t.shape[-2], t.shape[-1])
    ds, ms, vs = _adamw_small([two_d(given[a]) for a in small_names], [two_d(grads[a]) for a in small_names],
                              [two_d(given["m_" + a]) for a in small_names], [two_d(given["v_" + a]) for a in small_names])
    for a, d, m2, v2 in zip(small_names, ds, ms, vs):
        delta[a], new_m[a], new_v[a] = (t.reshape(given[a].shape) for t in (d, m2, v2))

    return (loss, grad_x[None], *[grads[n] for n in _WEIGHT_ORDER], *[delta[n] for n in _WEIGHT_ORDER],
            *[new_m[n] for n in _WEIGHT_ORDER], *[new_v[n] for n in _WEIGHT_ORDER])
```

```python
import functools
import math

import jax
import jax.numpy as jnp
from jax import lax
from jax.experimental import pallas as pl
from jax.experimental.pallas import tpu as pltpu

F32 = jnp.float32
BF16 = jnp.bfloat16

S = 2048
D = 1024
PLE = 256
H = 16
SSD_P = 64
SSD_INNER = 1024
SSD_N = 128
SSD_G = 2
SSD_L = 128
SSD_NC = S // SSD_L
SSD_XBC = 1536
SSD_K = 4
Q_RANK = 384
KV_RANK = 256
NOPE = 64
ROPE = 32
VDIM = 64
D_FF = 2816
IN_WIDTH = 3248
ALPHA = 2.0 ** 0.25
EPS_RMS = 1e-6
EPS_LN = 1e-5
ATT_SCALE = 1.0 / math.sqrt(NOPE + ROPE)
LN2 = math.log(2.0)
ATT_SCALE_LOG2 = ATT_SCALE / LN2
LANE = 128
NCHIP = 4
SMALL_W = 896
SM_Q, SM_KV, SM_KR, SM_DT = 0, 384, 640, 768
NEG = -1e30

ADAM_LR = 0.001
ADAM_B1 = 0.9
ADAM_B2 = 0.999
ADAM_EPS = 1e-08
ADAM_WD = 0.01
ADAM_STEP = 10


def _sigmoid(v):
    return 1.0 / (1.0 + jnp.exp(-v))


MM_VMEM_BUDGET = 36 * 2 ** 20
MM_MAX_ACC = 2048 * 1024


def _mm_tiles(pairs, ks, m, n, out_dtype, has_add):
    def divs(v):
        return [LANE * d for d in range(v // LANE, 0, -1) if (v // LANE) % d == 0] if v % LANE == 0 else [v]

    def cost(tm, tn):
        tot = tm * tn * (jnp.dtype(out_dtype).itemsize + (4 if has_add else 0))
        for (a, b), k in zip(pairs, ks):
            tot += k * (tm * a.dtype.itemsize + tn * b.dtype.itemsize)
        return 2 * tot

    ok = [(tm * tn, tm, tn) for tm in divs(m) for tn in divs(n) if tm * tn <= MM_MAX_ACC and cost(tm, tn) <= MM_VMEM_BUDGET]
    _, tm, tn = max(ok)
    return tm, tn


def _mm(pairs, *, ta=False, tb=False, out_dtype=F32, add=None, add_scale=1.0, chunk=None, name):
    n_pairs = len(pairs)
    windows = [pr[2] if len(pr) == 3 else None for pr in pairs]
    pairs = [pr[:2] for pr in pairs]
    assert not ((ta or tb) and any(windows))
    ks = [w[2] if w else (a.shape[-2] if ta else a.shape[-1]) for (a, _), w in zip(pairs, windows)]
    a0, b0 = pairs[0]
    m = a0.shape[-1] if ta else a0.shape[-2]
    n = b0.shape[-2] if tb else b0.shape[-1]
    tm, tn = _mm_tiles(pairs, ks, m, n, out_dtype, add is not None)
    dims = (((0 if ta else 1,), (1 if tb else 0,)), ((), ()))
    nk = NCHIP if chunk else 1
    assert chunk != "sum" or out_dtype == F32
    flat = [i for i, (a, b) in enumerate(pairs) if a.ndim == 2 and b.ndim == 2]
    stacked = [i for i in range(n_pairs) if i not in flat]

    def body(*refs):
        o_ref = refs[-1]

        def products(which):
            acc = None
            for i in which:
                a = refs[2 * i][...].astype(BF16)
                b = refs[2 * i + 1][...].astype(BF16)
                part = lax.dot_general(a, b, dims, preferred_element_type=F32)
                acc = part if acc is None else acc + part
            return acc

        if chunk == "sum":
            k = pl.program_id(2)
            acc = products(stacked)

            @pl.when(k == 0)
            def _():
                first = acc + products(flat) if flat else acc
                o_ref[...] = first + add_scale * refs[2 * n_pairs][...] if add is not None else first

            @pl.when(k > 0)
            def _():
                o_ref[...] += acc
            return
        acc = products(range(n_pairs))
        if add is not None:
            acc = acc + add_scale * refs[2 * n_pairs][...]
        o_ref[...] = acc.astype(out_dtype)

    def spec(arr, shape, idx2):
        if arr.ndim == 3:
            return pl.BlockSpec((None,) + shape, lambda i, j, k: (k,) + idx2(i, j))
        return pl.BlockSpec(shape, lambda i, j, k: idx2(i, j))

    in_specs, args = [], []
    for (a, b), kdim, window in zip(pairs, ks, windows):
        ka, kb = window[:2] if window else (0, 0)
        in_specs.append(spec(a, (kdim, tm), lambda i, j: (0, i)) if ta else spec(a, (tm, kdim), lambda i, j, ka=ka: (i, ka)))
        in_specs.append(spec(b, (tn, kdim), lambda i, j: (j, 0)) if tb else spec(b, (kdim, tn), lambda i, j, kb=kb: (kb, j)))
        args += [a, b]
    if add is not None:
        in_specs.append(pl.BlockSpec((tm, tn), lambda i, j, k: (i, j)))
        args.append(add)
    if chunk == "out":
        out_spec = pl.BlockSpec((None, tm, tn), lambda i, j, k: (k, i, j))
        out_shape = jax.ShapeDtypeStruct((nk, m, n), out_dtype)
    else:
        out_spec = pl.BlockSpec((tm, tn), lambda i, j, k: (i, j))
        out_shape = jax.ShapeDtypeStruct((m, n), out_dtype)
    return pl.pallas_call(
        body, name=name, grid=(m // tm, n // tn, nk), in_specs=in_specs, out_specs=out_spec, out_shape=out_shape,
        compiler_params=pltpu.CompilerParams(dimension_semantics=("parallel", "parallel", "arbitrary")),
    )(*args)


TR = 256


def _row_spec(c):
    return pl.BlockSpec((TR, c), lambda i: (i, 0))


def _vec_spec(c):
    return pl.BlockSpec((1, c), lambda i: (0, 0))


def _acc_rows(ref, val):
    @pl.when(pl.program_id(0) == 0)
    def _():
        ref[...] = jnp.zeros_like(ref)
    ref[...] += val


def _rms_fwd(u, w, *, name):
    c = u.shape[1]

    def body(u_ref, w_ref, o_ref):
        v = u_ref[...]
        r = lax.rsqrt(jnp.mean(v * v, axis=-1, keepdims=True) + EPS_RMS)
        o_ref[...] = (v * r * w_ref[...]).astype(BF16)

    return pl.pallas_call(body, name=name, grid=(S // TR,), in_specs=[_row_spec(c), _vec_spec(c)], out_specs=_row_spec(c),
                          out_shape=jax.ShapeDtypeStruct((S, c), BF16))(u, w)


def _rms_bwd(u, w, dy, *, name):
    c = u.shape[1]

    def body(u_ref, w_ref, dy_ref, du_ref, dw_ref):
        v = u_ref[...]
        g = dy_ref[...].astype(F32)
        r = lax.rsqrt(jnp.mean(v * v, axis=-1, keepdims=True) + EPS_RMS)
        gw = g * w_ref[...]
        du_ref[...] = r * gw - v * (r * r * r * jnp.mean(gw * v, axis=-1, keepdims=True))
        _acc_rows(dw_ref, jnp.sum(g * v * r, axis=0, keepdims=True))

    return pl.pallas_call(body, name=name, grid=(S // TR,), in_specs=[_row_spec(c), _vec_spec(c), _row_spec(c)],
                          out_specs=[_row_spec(c), _vec_spec(c)],
                          out_shape=[jax.ShapeDtypeStruct((S, c), F32), jax.ShapeDtypeStruct((1, c), F32)])(u, w, dy)


def _gate_norm_fwd(y, z, w):
    def body(y_ref, z_ref, w_ref, o_ref):
        zz = z_ref[...]
        v = y_ref[...] * (zz * _sigmoid(zz))
        r = lax.rsqrt(jnp.mean(v * v, axis=-1, keepdims=True) + EPS_RMS)
        o_ref[...] = (v * r * w_ref[...]).astype(BF16)

    c = SSD_INNER
    return pl.pallas_call(body, name="ssd_gate_norm_fwd", grid=(S // TR,), in_specs=[_row_spec(c), _row_spec(c), _vec_spec(c)],
                          out_specs=_row_spec(c), out_shape=jax.ShapeDtypeStruct((S, c), BF16))(y, z, w)


def _gate_norm_bwd(y, z, w, dout):
    def body(y_ref, z_ref, w_ref, g_ref, dy_ref, dz_ref, dw_ref):
        yy = y_ref[...]
        zz = z_ref[...]
        sg = _sigmoid(zz)
        sz = zz * sg
        v = yy * sz
        g = g_ref[...]
        r = lax.rsqrt(jnp.mean(v * v, axis=-1, keepdims=True) + EPS_RMS)
        gw = g * w_ref[...]
        dv = r * gw - v * (r * r * r * jnp.mean(gw * v, axis=-1, keepdims=True))
        dy_ref[...] = dv * sz
        dz_ref[...] = (dv * yy * (sg * (1.0 + zz * (1.0 - sg)))).astype(BF16)
        _acc_rows(dw_ref, jnp.sum(g * v * r, axis=0, keepdims=True))

    c = SSD_INNER
    return pl.pallas_call(body, name="ssd_gate_norm_bwd", grid=(S // TR,),
                          in_specs=[_row_spec(c), _row_spec(c), _vec_spec(c), _row_spec(c)],
                          out_specs=[_row_spec(c), _row_spec(c), _vec_spec(c)],
                          out_shape=[jax.ShapeDtypeStruct((S, c), F32), jax.ShapeDtypeStruct((S, c), BF16),
                                     jax.ShapeDtypeStruct((1, c), F32)])(y, z, w, dout)


MIX_ROWS = 512


def _out_proj_ln(y_ssd, y_mla, w_out, xr, g, b):
    k = y_ssd.shape[1]

    def body(ys_ref, ym_ref, w_ref, x_ref, g_ref, b_ref, m_ref, o_ref, ob_ref):
        mix = (jnp.dot(ys_ref[...], w_ref[:k], preferred_element_type=F32)
               + jnp.dot(ym_ref[...], w_ref[k:], preferred_element_type=F32))
        m_ref[...] = mix
        pre = ALPHA * x_ref[...] + mix
        mu = jnp.mean(pre, axis=-1, keepdims=True)
        d = pre - mu
        rs = lax.rsqrt(jnp.mean(d * d, axis=-1, keepdims=True) + EPS_LN)
        h = d * rs * g_ref[...] + b_ref[...]
        o_ref[...] = h
        ob_ref[...] = h.astype(BF16)

    rows = lambda c: pl.BlockSpec((MIX_ROWS, c), lambda i: (i, 0))
    return pl.pallas_call(
        body, name="out_proj_ln", grid=(S // MIX_ROWS,),
        in_specs=[rows(k), rows(k), _whole(w_out), rows(D), _vec_spec(D), _vec_spec(D)], out_specs=[rows(D)] * 3,
        out_shape=[jax.ShapeDtypeStruct((S, D), F32), jax.ShapeDtypeStruct((S, D), F32), jax.ShapeDtypeStruct((S, D), BF16)],
    )(y_ssd, y_mla, w_out, xr, g, b)


def _ln_bwd(xr, mix, g, dh, w_out):
    k = w_out.shape[0] // 2

    def body(x_ref, m_ref, g_ref, dh_ref, w_ref, dpre_ref, dpreb_ref, dg_ref, db_ref, dys_ref, dym_ref):
        pre = ALPHA * x_ref[...] + m_ref[...]
        mu = jnp.mean(pre, axis=-1, keepdims=True)
        d = pre - mu
        rs = lax.rsqrt(jnp.mean(d * d, axis=-1, keepdims=True) + EPS_LN)
        xh = d * rs
        dy = dh_ref[...]
        gy = dy * g_ref[...]
        dpre = rs * (gy - jnp.mean(gy, axis=-1, keepdims=True) - xh * jnp.mean(gy * xh, axis=-1, keepdims=True))
        dpre_ref[...] = dpre
        dpreb = dpre.astype(BF16)
        dpreb_ref[...] = dpreb
        _acc_rows(dg_ref, jnp.sum(dy * xh, axis=0, keepdims=True))
        _acc_rows(db_ref, jnp.sum(dy, axis=0, keepdims=True))
        dys_ref[...] = lax.dot_general(dpreb, w_ref[:k], (((1,), (1,)), ((), ())), preferred_element_type=F32)
        dym_ref[...] = lax.dot_general(dpreb, w_ref[k:], (((1,), (1,)), ((), ())), preferred_element_type=F32)

    rows = lambda c: pl.BlockSpec((MIX_ROWS, c), lambda i: (i, 0))
    return pl.pallas_call(
        body, name="ln_mix_bwd", grid=(S // MIX_ROWS,),
        in_specs=[rows(D), rows(D), _vec_spec(D), rows(D), _whole(w_out)],
        out_specs=[rows(D), rows(D), _vec_spec(D), _vec_spec(D), rows(k), rows(k)],
        out_shape=[jax.ShapeDtypeStruct((S, D), F32), jax.ShapeDtypeStruct((S, D), BF16), jax.ShapeDtypeStruct((1, D), F32),
                   jax.ShapeDtypeStruct((1, D), F32), jax.ShapeDtypeStruct((S, k), F32), jax.ShapeDtypeStruct((S, k), F32)],
    )(xr, mix, g, dh, w_out)


FF_CHUNK = D_FF // NCHIP


FF_ROWS = 1024


def _ff_act_spec():
    return pl.BlockSpec((None, FF_ROWS, FF_CHUNK), lambda i, k: (k, i, 0))


def _ff_w_spec():
    return pl.BlockSpec((None, FF_CHUNK, D), lambda i, k: (k, 0, 0))


def _ffn_hidden_fwd(h, w_gate_t, w_up_t):
    def body(h_ref, wg_ref, wu_ref, g_ref, u_ref, a_ref):
        hh = h_ref[...]
        g = _dot(hh, wg_ref[...], ((1,), (1,)))
        u = _dot(hh, wu_ref[...], ((1,), (1,)))
        g_ref[...] = g.astype(BF16)
        u_ref[...] = u.astype(BF16)
        a_ref[...] = (g * _sigmoid(g) * u).astype(BF16)

    return pl.pallas_call(
        body, name="ffn_hidden_fwd", grid=(S // FF_ROWS, NCHIP),
        in_specs=[pl.BlockSpec((FF_ROWS, D), lambda i, k: (i, 0)), _ff_w_spec(), _ff_w_spec()], out_specs=[_ff_act_spec()] * 3,
        out_shape=[jax.ShapeDtypeStruct((NCHIP, S, FF_CHUNK), BF16)] * 3,
        compiler_params=pltpu.CompilerParams(dimension_semantics=("parallel", "parallel")),
    )(h, w_gate_t, w_up_t)


def _ffn_hidden_bwd(dout, w_down, gate, up):
    def body(d_ref, wd_ref, g_ref, u_ref, dg_ref, du_ref):
        d = _dot(d_ref[...], wd_ref[...], ((1,), (1,)))
        g = g_ref[...].astype(F32)
        sg = _sigmoid(g)
        dg_ref[...] = (d * u_ref[...].astype(F32) * (sg * (1.0 + g * (1.0 - sg)))).astype(BF16)
        du_ref[...] = (d * g * sg).astype(BF16)

    return pl.pallas_call(
        body, name="ffn_hidden_bwd", grid=(S // FF_ROWS, NCHIP),
        in_specs=[pl.BlockSpec((FF_ROWS, D), lambda i, k: (i, 0)), _ff_w_spec(), _ff_act_spec(), _ff_act_spec()],
        out_specs=[_ff_act_spec()] * 2, out_shape=[jax.ShapeDtypeStruct((NCHIP, S, FF_CHUNK), BF16)] * 2,
        compiler_params=pltpu.CompilerParams(dimension_semantics=("parallel", "parallel")),
    )(dout, w_down, gate, up)


def _final_fwd_bwd(h1, ffn, h1b, pb, w_pg, w_pp, target, g2, b2):
    def body(h_ref, f_ref, hb_ref, pb_ref, wpg_ref, wpp_ref, t_ref, g_ref, b_ref,
             dpre_ref, dpreb_ref, dpg_ref, dpp_ref, dg_ref, db_ref, loss_ref):
        sg = _sigmoid(jnp.dot(hb_ref[...], wpg_ref[...], preferred_element_type=F32))
        ppv = jnp.dot(pb_ref[...], wpp_ref[...], preferred_element_type=F32)
        pre = ALPHA * h_ref[...] + f_ref[...] + sg * ppv
        mu = jnp.mean(pre, axis=-1, keepdims=True)
        d = pre - mu
        rs = lax.rsqrt(jnp.mean(d * d, axis=-1, keepdims=True) + EPS_LN)
        xh = d * rs
        err = xh * g_ref[...] + b_ref[...] - t_ref[...]
        dy = err * (1.0 / D)
        gy = dy * g_ref[...]
        dpre = rs * (gy - jnp.mean(gy, axis=-1, keepdims=True) - xh * jnp.mean(gy * xh, axis=-1, keepdims=True))
        dpre_ref[...] = dpre
        dpreb_ref[...] = dpre.astype(BF16)
        dpg_ref[...] = (dpre * ppv * sg * (1.0 - sg)).astype(BF16)
        dpp_ref[...] = (dpre * sg).astype(BF16)
        _acc_rows(dg_ref, jnp.sum(dy * xh, axis=0, keepdims=True))
        _acc_rows(db_ref, jnp.sum(dy, axis=0, keepdims=True))
        _acc_rows(loss_ref, 0.5 * jnp.sum(jnp.mean(err * err, axis=-1, keepdims=True), axis=0, keepdims=True) * jnp.ones((1, LANE), F32))

    return pl.pallas_call(
        body, name="final_ln_loss", grid=(S // TR,),
        in_specs=[_row_spec(D)] * 3 + [_row_spec(pb.shape[1]), _whole(w_pg), _whole(w_pp), _row_spec(D)] + [_vec_spec(D)] * 2,
        out_specs=[_row_spec(D)] * 4 + [_vec_spec(D), _vec_spec(D), _vec_spec(LANE)],
        out_shape=[jax.ShapeDtypeStruct((S, D), F32)] + [jax.ShapeDtypeStruct((S, D), BF16)] * 3 + [
                   jax.ShapeDtypeStruct((1, D), F32), jax.ShapeDtypeStruct((1, D), F32), jax.ShapeDtypeStruct((1, LANE), F32)],
    )(h1, ffn, h1b, pb, w_pg, w_pp, target, g2, b2)


def _rot(u, cos_t, sin_t, lane):
    partner = jnp.where(lane < NOPE + ROPE // 2, pltpu.roll(u, LANE - ROPE // 2, 1), pltpu.roll(u, ROPE // 2, 1))
    return u * cos_t + partner * sin_t


def _rms(v, w):
    r = lax.rsqrt(jnp.mean(v * v, axis=-1, keepdims=True) + EPS_RMS)
    return v * r * w, r


def _rms_grad(v, r, w, g):
    gw = g * w
    return r * gw - v * (r * r * r * jnp.mean(gw * v, axis=-1, keepdims=True)), jnp.sum(g * v * r, axis=0, keepdims=True)


def _whole(arr):
    return pl.BlockSpec(arr.shape, lambda i: (0,) * arr.ndim)


def _qkv_fwd(small, w_q, w_k, w_v, q_norm, kv_norm, cos_t, sin_t):
    def body(sm_ref, wq_ref, wk_ref, wv_ref, qw_ref, kw_ref, c_ref, s_ref, qn_ref, kvn_ref, q_ref, k_ref, kt_ref, v_ref):
        lane = lax.broadcasted_iota(jnp.int32, (TR, LANE), 1)
        c, s = c_ref[...], s_ref[...]
        qn = _rms(sm_ref[:, SM_Q:SM_Q + Q_RANK], qw_ref[...])[0].astype(BF16)
        kvn = _rms(sm_ref[:, SM_KV:SM_KV + KV_RANK], kw_ref[...])[0].astype(BF16)
        qn_ref[...] = qn
        kvn_ref[...] = kvn
        kr = _rot(pltpu.roll(sm_ref[:, SM_KR:SM_KR + LANE], NOPE, 1), c, s, lane)
        for h in range(H):
            tile = slice(h * LANE, (h + 1) * LANE)
            q_ref[:, tile] = _rot(_dot(qn, wq_ref[:, tile], ((1,), (0,))), c, s, lane).astype(BF16)
            kt = _dot(kvn, wk_ref[:, tile], ((1,), (0,))) + kr
            k_ref[:, tile] = kt.astype(BF16)
            kt_ref[tile, :] = kt.T.astype(BF16)
        v_ref[...] = _dot(kvn, wv_ref[...], ((1,), (0,))).astype(BF16)

    w = H * LANE
    return pl.pallas_call(
        body, name="qkv_fwd", grid=(S // TR,),
        in_specs=[_row_spec(SMALL_W), _whole(w_q), _whole(w_k), _whole(w_v), _vec_spec(Q_RANK), _vec_spec(KV_RANK), _row_spec(LANE), _row_spec(LANE)],
        out_specs=[_row_spec(Q_RANK), _row_spec(KV_RANK), _row_spec(w), _row_spec(w), pl.BlockSpec((w, TR), lambda i: (0, i)),
                   _row_spec(H * VDIM)],
        out_shape=[jax.ShapeDtypeStruct((S, Q_RANK), BF16), jax.ShapeDtypeStruct((S, KV_RANK), BF16), jax.ShapeDtypeStruct((S, w), BF16),
                   jax.ShapeDtypeStruct((S, w), BF16), jax.ShapeDtypeStruct((w, S), BF16), jax.ShapeDtypeStruct((S, H * VDIM), BF16)],
    )(small, w_q, w_k, w_v, q_norm, kv_norm, cos_t, sin_t)


def _qkv_bwd(dqt, dk, dv, small, w_q, w_k, w_v, q_norm, kv_norm, cos_t, sin_t):
    def body(dq_ref, dk_ref, dv_ref, sm_ref, wq_ref, wk_ref, wv_ref, qw_ref, kw_ref, c_ref, s_ref,
             ds_ref, dql_ref, dkb_ref, dqw_ref, dkw_ref):
        lane = lax.broadcasted_iota(jnp.int32, (TR, LANE), 1)
        c, s = c_ref[...], -s_ref[...]
        dqn = jnp.zeros((TR, Q_RANK), F32)
        dkvn = _dot(dv_ref[...], wv_ref[...], ((1,), (1,)))
        dkr = jnp.zeros((TR, LANE), F32)
        for h in range(H):
            tile = slice(h * LANE, (h + 1) * LANE)
            dql = _rot(dq_ref[tile, :].T, c, s, lane).astype(BF16)
            dql_ref[:, tile] = dql
            dqn = dqn + _dot(dql, wq_ref[:, tile], ((1,), (1,)))
            dkt = dk_ref[:, tile]
            dkb_ref[:, tile] = dkt.astype(BF16)
            dkvn = dkvn + _dot(dkt, wk_ref[:, tile], ((1,), (1,)))
            dkr = dkr + dkt
        dkr = jnp.where((lane >= NOPE) & (lane < NOPE + ROPE), dkr, 0.0)
        q_c, kv_c = sm_ref[:, SM_Q:SM_Q + Q_RANK], sm_ref[:, SM_KV:SM_KV + KV_RANK]
        dq_c, dqw = _rms_grad(q_c, _rms(q_c, qw_ref[...])[1], qw_ref[...], dqn)
        dkv_c, dkw = _rms_grad(kv_c, _rms(kv_c, kw_ref[...])[1], kw_ref[...], dkvn)
        ds_ref[:, SM_Q:SM_Q + Q_RANK] = dq_c.astype(BF16)
        ds_ref[:, SM_KV:SM_KV + KV_RANK] = dkv_c.astype(BF16)
        ds_ref[:, SM_KR:SM_KR + LANE] = pltpu.roll(_rot(dkr, c, s, lane), LANE - NOPE, 1).astype(BF16)
        _acc_rows(dqw_ref, dqw)
        _acc_rows(dkw_ref, dkw)

    w = H * LANE
    return pl.pallas_call(
        body, name="qkv_bwd", grid=(S // TR,),
        in_specs=[pl.BlockSpec((w, TR), lambda i: (0, i)), _row_spec(w), _row_spec(H * VDIM), _row_spec(SMALL_W), _whole(w_q), _whole(w_k),
                  _whole(w_v), _vec_spec(Q_RANK), _vec_spec(KV_RANK), _row_spec(LANE), _row_spec(LANE)],
        out_specs=[_row_spec(SM_DT), _row_spec(w), _row_spec(w), _vec_spec(Q_RANK), _vec_spec(KV_RANK)],
        out_shape=[jax.ShapeDtypeStruct((S, SM_DT), BF16), jax.ShapeDtypeStruct((S, w), BF16), jax.ShapeDtypeStruct((S, w), BF16),
                   jax.ShapeDtypeStruct((1, Q_RANK), F32), jax.ShapeDtypeStruct((1, KV_RANK), F32)],
    )(dqt, dk, dv, small, w_q, w_k, w_v, q_norm, kv_norm, cos_t, sin_t)


CB = 256


def _shift_down(u, k, row):
    if k == 0:
        return u
    return jnp.where(row >= k, pltpu.roll(u, k, 0), 0.0)


def _shift_up(u, k, row):
    if k == 0:
        return u
    return jnp.where(row < S - k, pltpu.roll(u, S - k, 0), 0.0)


def _conv_fwd(u, w, b):
    def body(u_ref, w_ref, b_ref, o_ref):
        row = lax.broadcasted_iota(jnp.int32, (S, CB), 0)
        uu = u_ref[...]
        acc = b_ref[...] + w_ref[SSD_K - 1:SSD_K, :] * uu
        for k in range(SSD_K - 1):
            acc = acc + w_ref[k:k + 1, :] * _shift_down(uu, SSD_K - 1 - k, row)
        o_ref[...] = acc * _sigmoid(acc)

    c = u.shape[1]
    return pl.pallas_call(
        body, name="conv_fwd", grid=(c // CB,),
        in_specs=[pl.BlockSpec((S, CB), lambda j: (0, j)), pl.BlockSpec((SSD_K, CB), lambda j: (0, j)), pl.BlockSpec((1, CB), lambda j: (0, j))],
        out_specs=pl.BlockSpec((S, CB), lambda j: (0, j)), out_shape=jax.ShapeDtypeStruct((S, c), F32),
    )(u, w, b)


def _conv_bwd(u, w, b, dact):
    def body(u_ref, w_ref, b_ref, d_ref, du_ref, dw_ref, db_ref):
        row = lax.broadcasted_iota(jnp.int32, (S, CB), 0)
        uu = u_ref[...]
        sh = [_shift_down(uu, SSD_K - 1 - k, row) for k in range(SSD_K)]
        acc = b_ref[...]
        for k in range(SSD_K):
            acc = acc + w_ref[k:k + 1, :] * sh[k]
        sg = _sigmoid(acc)
        dacc = d_ref[...] * (sg * (1.0 + acc * (1.0 - sg)))
        du = w_ref[SSD_K - 1:SSD_K, :] * dacc
        for k in range(SSD_K - 1):
            du = du + w_ref[k:k + 1, :] * _shift_up(dacc, SSD_K - 1 - k, row)
        du_ref[...] = du.astype(BF16)
        for k in range(SSD_K):
            dw_ref[k:k + 1, :] = jnp.sum(dacc * sh[k], axis=0, keepdims=True)
        db_ref[...] = jnp.sum(dacc, axis=0, keepdims=True)

    c = u.shape[1]
    col = lambda r: pl.BlockSpec((r, CB), lambda j: (0, j))
    return pl.pallas_call(
        body, name="conv_bwd", grid=(c // CB,), in_specs=[col(S), col(SSD_K), col(1), col(S)], out_specs=[col(S), col(SSD_K), col(1)],
        out_shape=[jax.ShapeDtypeStruct((S, c), BF16), jax.ShapeDtypeStruct((SSD_K, c), F32), jax.ShapeDtypeStruct((1, c), F32)],
    )(u, w, b, dact)


NPAIR = H // 2
PAIRS_PER_GROUP = NPAIR // SSD_G


def _softplus(v):
    return jnp.maximum(v, 0.0) + jnp.log(1.0 + jnp.exp(-jnp.abs(v)))


def _dot(a, b, dims):
    return lax.dot_general(a.astype(BF16), b.astype(BF16), (dims, ((), ())), preferred_element_type=F32)


def _dot2(a, sel):
    hi = a.astype(BF16)
    lo = (a - hi.astype(F32)).astype(BF16)
    dims = (((1,), (0,)), ((), ()))
    return lax.dot_general(hi, sel, dims, preferred_element_type=F32) + lax.dot_general(lo, sel, dims, preferred_element_type=F32)


def _dot3(a, b, dims, split_lhs):
    v = a if split_lhs else b
    v1 = v.astype(BF16)
    r1 = v - v1.astype(F32)
    v2 = r1.astype(BF16)
    v3 = (r1 - v2.astype(F32)).astype(BF16)
    acc = None
    for part in (v1, v2, v3):
        lhs, rhs = (part, b) if split_lhs else (a, part)
        t = lax.dot_general(lhs, rhs, (dims, ((), ())), preferred_element_type=F32)
        acc = t if acc is None else acc + t
    return acc


def _ssd_chunk_common(dt_ref, dtT_ref, prow_ref, pcol_ref):
    prow = prow_ref[...]
    pcol = pcol_ref[...]
    ri = lax.broadcasted_iota(jnp.int32, (SSD_L, SSD_L), 0)
    ci = lax.broadcasted_iota(jnp.int32, (SSD_L, SSD_L), 1)
    causal = ri >= ci
    pre_c = dt_ref[...] + prow[0:1, :]
    dtc = _softplus(pre_c)
    a_row = -jnp.exp(prow[1:2, :])
    cs_col = _dot3(causal.astype(BF16), dtc * a_row, ((1,), (0,)), False)
    dtr = _softplus(dtT_ref[...] + pcol[:, 0:1])
    a_col = -jnp.exp(pcol[:, 1:2])
    cs_row = _dot3(dtr * a_col, (ri <= ci).astype(BF16), ((1,), (0,)), True)
    return prow, causal, pre_c, dtc, a_row, cs_col, cs_row


def _ssd_fwd(act, small, dtT, prow, pcol):
    def body(x_ref, b_ref, c_ref, dt_ref, dtT_ref, prow_ref, pcol_ref, y_ref, st_ref, state):
        @pl.when(pl.program_id(0) == 0)
        def _():
            state[...] = jnp.zeros_like(state)

        prow, causal, _, dtc, _, cs_col, cs_row = _ssd_chunk_common(dt_ref, dtT_ref, prow_ref, pcol_ref)
        lo = lax.broadcasted_iota(jnp.int32, (SSD_L, LANE), 1) < SSD_P
        lo1 = lo[0:1, :]
        for g in range(SSD_G):
            bm = b_ref[:, g * SSD_N:(g + 1) * SSD_N]
            cm = c_ref[:, g * SSD_N:(g + 1) * SSD_N]
            cb = _dot(cm, bm, ((1,), (1,)))
            for qq in range(PAIRS_PER_GROUP):
                q = g * PAIRS_PER_GROUP + qq
                ha, hb = 2 * q, 2 * q + 1
                csa, csb = cs_col[:, ha:ha + 1], cs_col[:, hb:hb + 1]
                xp = x_ref[:, q * LANE:(q + 1) * LANE]
                xx = xp * jnp.where(lo, dtc[:, ha:ha + 1], dtc[:, hb:hb + 1])
                ga = cb * jnp.exp(jnp.where(causal, csa - cs_row[ha:ha + 1, :], NEG))
                gb = cb * jnp.exp(jnp.where(causal, csb - cs_row[hb:hb + 1, :], NEG))
                y = _dot(ga, jnp.where(lo, xx, 0.0), ((1,), (0,))) + _dot(gb, jnp.where(lo, 0.0, xx), ((1,), (0,)))
                s_in = state[q]
                y = y + _dot(cm, s_in, ((1,), (0,))) * jnp.where(lo, jnp.exp(csa), jnp.exp(csb))
                y = y + jnp.where(lo1, prow[2:3, ha:ha + 1], prow[2:3, hb:hb + 1]) * xp
                y_ref[:, q * LANE:(q + 1) * LANE] = y
                la, lb = csa[SSD_L - 1:SSD_L, :], csb[SSD_L - 1:SSD_L, :]
                decay = jnp.where(lo, jnp.exp(la - csa), jnp.exp(lb - csb))
                st_ref[q] = s_in
                state[q] = s_in * jnp.where(lo1, jnp.exp(la), jnp.exp(lb)) + _dot(bm, xx * decay, ((0,), (0,)))

    L = SSD_L
    return pl.pallas_call(
        body, name="ssd_fwd", grid=(SSD_NC,),
        in_specs=[pl.BlockSpec((L, SSD_INNER), lambda c: (c, 0)),
                  pl.BlockSpec((L, SSD_G * SSD_N), lambda c: (c, SSD_INNER // (SSD_G * SSD_N))),
                  pl.BlockSpec((L, SSD_G * SSD_N), lambda c: (c, SSD_INNER // (SSD_G * SSD_N) + 1)),
                  pl.BlockSpec((L, LANE), lambda c: (c, SM_DT // LANE)),
                  pl.BlockSpec((LANE, L), lambda c: (0, c)),
                  pl.BlockSpec((8, LANE), lambda c: (0, 0)), pl.BlockSpec((LANE, 8), lambda c: (0, 0))],
        out_specs=[pl.BlockSpec((L, SSD_INNER), lambda c: (c, 0)),
                   pl.BlockSpec((None, NPAIR, SSD_N, LANE), lambda c: (c, 0, 0, 0))],
        out_shape=[jax.ShapeDtypeStruct((S, SSD_INNER), F32), jax.ShapeDtypeStruct((SSD_NC, NPAIR, SSD_N, LANE), F32)],
        scratch_shapes=[pltpu.VMEM((NPAIR, SSD_N, LANE), F32)],
        compiler_params=pltpu.CompilerParams(dimension_semantics=("arbitrary",)),
    )(act, act, act, small, dtT, prow, pcol)


def _ssd_bwd(act, small, dtT, prow, pcol, states, dy):
    def body(x_ref, b_ref, c_ref, dt_ref, dtT_ref, prow_ref, pcol_ref, st_ref, dy_ref,
             dx_ref, ddt_ref, dp_ref, dstate):
        @pl.when(pl.program_id(0) == 0)
        def _():
            dstate[...] = jnp.zeros_like(dstate)
            dp_ref[...] = jnp.zeros_like(dp_ref)

        prow, causal, pre_c, dtc, a_row, cs_col, cs_row = _ssd_chunk_common(dt_ref, dtT_ref, prow_ref, pcol_ref)
        lane = lax.broadcasted_iota(jnp.int32, (SSD_L, LANE), 1)
        sub = lax.broadcasted_iota(jnp.int32, (LANE, SSD_L), 0)
        rowi = lax.broadcasted_iota(jnp.int32, (SSD_L, 1), 0)
        pick_p = lax.broadcasted_iota(jnp.int32, (LANE, LANE), 0)
        pick_l = lax.broadcasted_iota(jnp.int32, (LANE, LANE), 1)
        lo = lane < SSD_P
        lo1 = lo[0:1, :]
        dcs_c = jnp.zeros((SSD_L, LANE), F32)
        dcs_r = jnp.zeros((LANE, SSD_L), F32)
        ddt_x = jnp.zeros((SSD_L, LANE), F32)
        dd_row = jnp.zeros((1, LANE), F32)
        for g in range(SSD_G):
            bm = b_ref[:, g * SSD_N:(g + 1) * SSD_N]
            cm = c_ref[:, g * SSD_N:(g + 1) * SSD_N]
            cb = _dot(cm, bm, ((1,), (1,)))
            dcb = jnp.zeros((SSD_L, SSD_L), F32)
            dbm = jnp.zeros((SSD_L, SSD_N), F32)
            dcm = jnp.zeros((SSD_L, SSD_N), F32)
            for qq in range(PAIRS_PER_GROUP):
                q = g * PAIRS_PER_GROUP + qq
                ha, hb = 2 * q, 2 * q + 1
                csa, csb = cs_col[:, ha:ha + 1], cs_col[:, hb:hb + 1]
                xp = x_ref[:, q * LANE:(q + 1) * LANE]
                dtp = jnp.where(lo, dtc[:, ha:ha + 1], dtc[:, hb:hb + 1])
                xx = xp * dtp
                lma = jnp.exp(jnp.where(causal, csa - cs_row[ha:ha + 1, :], NEG))
                lmb = jnp.exp(jnp.where(causal, csb - cs_row[hb:hb + 1, :], NEG))
                ga, gb = cb * lma, cb * lmb
                dyp = dy_ref[:, q * LANE:(q + 1) * LANE]
                dya, dyb = jnp.where(lo, dyp, 0.0), jnp.where(lo, 0.0, dyp)
                s_in = st_ref[q]
                ds_out = dstate[q]
                la, lb = csa[SSD_L - 1:SSD_L, :], csb[SSD_L - 1:SSD_L, :]
                ecs = jnp.where(lo, jnp.exp(csa), jnp.exp(csb))
                decay = jnp.where(lo, jnp.exp(la - csa), jnp.exp(lb - csb))
                cd = jnp.where(lo1, jnp.exp(la), jnp.exp(lb))
                bds = _dot(bm, ds_out, ((1,), (0,)))
                dxx = _dot(ga, dya, ((0,), (0,))) + _dot(gb, dyb, ((0,), (0,))) + bds * decay
                dga = _dot(dya, xx, ((1,), (1,)))
                dgb = _dot(dyb, xx, ((1,), (1,)))
                dsega, dsegb = dga * ga, dgb * gb
                dcb = dcb + dga * lma + dgb * lmb
                yoff = _dot(cm, s_in, ((1,), (0,))) * ecs
                dye = dyp * ecs
                dcm = dcm + _dot(dye, s_in, ((1,), (1,)))
                xd = xx * decay
                dbm = dbm + _dot(xd, ds_out, ((1,), (1,)))
                wv = xd * bds
                ends = jnp.sum(wv, axis=0, keepdims=True) + cd * jnp.sum(ds_out * s_in, axis=0, keepdims=True)
                t1 = dyp * yoff - wv + jnp.where(rowi == SSD_L - 1, ends, 0.0)
                to_pair = (((pick_p < SSD_P) & (pick_l == ha)) | ((pick_p >= SSD_P) & (pick_l == hb))).astype(BF16)
                to_a_b = jnp.concatenate([(pick_l == ha).astype(BF16), (pick_l == hb).astype(BF16)], axis=0)
                dcs_c = dcs_c + _dot2(t1, to_pair) + _dot2(jnp.concatenate([dsega, dsegb], axis=1), to_a_b)
                dcs_r = (dcs_r + jnp.where(sub == ha, jnp.sum(dsega, axis=0, keepdims=True), 0.0)
                         + jnp.where(sub == hb, jnp.sum(dsegb, axis=0, keepdims=True), 0.0))
                dstate[q] = _dot(cm, dye, ((0,), (0,))) + cd * ds_out
                dpair = jnp.where(lo1, prow[2:3, ha:ha + 1], prow[2:3, hb:hb + 1])
                dx_ref[:, q * LANE:(q + 1) * LANE] = dxx * dtp + dpair * dyp
                ddt_x = ddt_x + _dot2(dxx * xp, to_pair)
                dd_row = dd_row + jnp.sum(_dot2(dyp * xp, to_pair), axis=0, keepdims=True)
            dx_ref[:, SSD_INNER + g * SSD_N:SSD_INNER + (g + 1) * SSD_N] = dbm + _dot(dcb, cm, ((0,), (0,)))
            dx_ref[:, SSD_INNER + (SSD_G + g) * SSD_N:SSD_INNER + (SSD_G + g + 1) * SSD_N] = dcm + _dot(dcb, bm, ((1,), (0,)))
        ri = lax.broadcasted_iota(jnp.int32, (SSD_L, SSD_L), 0)
        ci = lax.broadcasted_iota(jnp.int32, (SSD_L, SSD_L), 1)
        da = _dot3((ri <= ci).astype(BF16), dcs_c, ((1,), (0,)), False)
        da = da - _dot3(dcs_r, causal.astype(BF16), ((1,), (0,)), True).T
        ddt = ddt_x + da * a_row
        ddt_raw = ddt * _sigmoid(pre_c)
        ddt_ref[...] = ddt_raw
        da_head = jnp.sum(da * dtc, axis=0, keepdims=True) * a_row
        dp_ref[0:1, :] += jnp.sum(ddt_raw, axis=0, keepdims=True)
        dp_ref[1:2, :] += da_head
        dp_ref[2:3, :] += dd_row

    L = SSD_L
    rev = SSD_NC - 1
    bc_cols = SSD_INNER // (SSD_G * SSD_N)
    return pl.pallas_call(
        body, name="ssd_bwd", grid=(SSD_NC,),
        in_specs=[pl.BlockSpec((L, SSD_INNER), lambda c: (rev - c, 0)),
                  pl.BlockSpec((L, SSD_G * SSD_N), lambda c: (rev - c, bc_cols)),
                  pl.BlockSpec((L, SSD_G * SSD_N), lambda c: (rev - c, bc_cols + 1)),
                  pl.BlockSpec((L, LANE), lambda c: (rev - c, SM_DT // LANE)),
                  pl.BlockSpec((LANE, L), lambda c: (0, rev - c)),
                  pl.BlockSpec((8, LANE), lambda c: (0, 0)), pl.BlockSpec((LANE, 8), lambda c: (0, 0)),
                  pl.BlockSpec((None, NPAIR, SSD_N, LANE), lambda c: (rev - c, 0, 0, 0)),
                  pl.BlockSpec((L, SSD_INNER), lambda c: (rev - c, 0))],
        out_specs=[pl.BlockSpec((L, SSD_XBC), lambda c: (rev - c, 0)),
                   pl.BlockSpec((L, LANE), lambda c: (rev - c, 0)),
                   pl.BlockSpec((8, LANE), lambda c: (0, 0))],
        out_shape=[jax.ShapeDtypeStruct((S, SSD_XBC), F32), jax.ShapeDtypeStruct((S, LANE), F32),
                   jax.ShapeDtypeStruct((8, LANE), F32)],
        scratch_shapes=[pltpu.VMEM((NPAIR, SSD_N, LANE), F32)],
        compiler_params=pltpu.CompilerParams(dimension_semantics=("arbitrary",)),
    )(act, act, act, small, dtT, prow, pcol, states, dy)


TQ = 256
TK = 256
FWD_TQ = 256
FWD_TK = 256


def _attn_fwd(qc, kc, v):
    TQ, TK = FWD_TQ, FWD_TK

    def body(q_ref, k_ref, v_ref, o_ref, lse_ref):
        i = pl.program_id(1)
        lo = lax.broadcasted_iota(jnp.int32, (TQ, LANE), 1) < VDIM
        lo_k = lax.broadcasted_iota(jnp.int32, (TK, LANE), 1) < VDIM
        row_minus_col = lax.broadcasted_iota(jnp.int32, (TQ, TK), 0) - lax.broadcasted_iota(jnp.int32, (TQ, TK), 1)
        qa, qb = q_ref[:, 0:LANE], q_ref[:, LANE:2 * LANE]

        def scores(kb):
            kk = k_ref[pl.ds(pl.multiple_of(kb * TK, TK), TK), :]
            return (_dot(qa, kk[:, 0:LANE], ((1,), (1,))) * ATT_SCALE_LOG2, _dot(qb, kk[:, LANE:2 * LANE], ((1,), (1,))) * ATT_SCALE_LOG2)

        def update(kb, sa, sb, stats):
            ma, la, mb, lb, acc = stats
            vv = v_ref[pl.ds(pl.multiple_of(kb * TK, TK), TK), :]
            na = jnp.maximum(ma, jnp.max(sa, axis=1, keepdims=True))
            nb = jnp.maximum(mb, jnp.max(sb, axis=1, keepdims=True))
            pa, pb = jnp.exp2(sa - na), jnp.exp2(sb - nb)
            fa, fb = jnp.exp2(ma - na), jnp.exp2(mb - nb)
            la = fa * la + jnp.sum(pa, axis=1, keepdims=True)
            lb = fb * lb + jnp.sum(pb, axis=1, keepdims=True)
            acc = (acc * jnp.where(lo, fa, fb) + _dot(pa, jnp.where(lo_k, vv, 0), ((1,), (0,)))
                   + _dot(pb, jnp.where(lo_k, 0, vv), ((1,), (0,))))
            return na, la, nb, lb, acc

        def step(kb, carry):
            sa, sb = carry[:2]
            nxt = scores(kb + 1)
            return nxt + update(kb, sa, sb, carry[2:])

        neg = jnp.full((TQ, 1), NEG, F32)
        zero = jnp.zeros((TQ, 1), F32)
        n_full = i * (TQ // TK)
        carry = lax.fori_loop(0, n_full, step, scores(0) + (neg, zero, neg, zero, jnp.zeros((TQ, LANE), F32)))
        s, stats = carry[:2], carry[2:]
        for d in range(TQ // TK):
            nxt = scores(n_full + d + 1) if d + 1 < TQ // TK else None
            sa, sb = (jnp.where(row_minus_col >= d * TK, t, NEG) for t in s)
            stats = update(n_full + d, sa, sb, stats)
            s = nxt
        ma, la, mb, lb, acc = stats
        o_ref[...] = acc / jnp.where(lo, la, lb)
        lse_ref[...] = jnp.where(lo, ma + jnp.log2(la), mb + jnp.log2(lb)) * LN2

    return pl.pallas_call(
        body, name="attn_fwd", grid=(NPAIR, S // TQ),
        in_specs=[pl.BlockSpec((TQ, 2 * LANE), lambda j, i: (i, j)), pl.BlockSpec((S, 2 * LANE), lambda j, i: (0, j)),
                  pl.BlockSpec((S, LANE), lambda j, i: (0, j))],
        out_specs=[pl.BlockSpec((TQ, LANE), lambda j, i: (i, j)), pl.BlockSpec((None, TQ, LANE), lambda j, i: (j, i, 0))],
        out_shape=[jax.ShapeDtypeStruct((S, H * VDIM), F32), jax.ShapeDtypeStruct((NPAIR, S, LANE), F32)],
        compiler_params=pltpu.CompilerParams(dimension_semantics=("parallel", "parallel")),
    )(qc, kc, v)


def _attn_rows(lse, o, do):
    def body(lse_ref, o_ref, do_ref, r_ref):
        lt = lse_ref[...].T * (1.0 / LN2)
        tt = (o_ref[...] * do_ref[...]).T
        r_ref[...] = jnp.zeros_like(r_ref)
        r_ref[0:1, :] = lt[0:1, :]
        r_ref[1:2, :] = lt[VDIM:VDIM + 1, :]
        r_ref[2:3, :] = jnp.sum(tt[0:VDIM, :], axis=0, keepdims=True)
        r_ref[3:4, :] = jnp.sum(tt[VDIM:LANE, :], axis=0, keepdims=True)

    tile = pl.BlockSpec((S, LANE), lambda j: (0, j))
    return pl.pallas_call(
        body, name="attn_rows", grid=(NPAIR,), in_specs=[pl.BlockSpec((None, S, LANE), lambda j: (j, 0, 0)), tile, tile],
        out_specs=pl.BlockSpec((None, 8, S), lambda j: (j, 0, 0)), out_shape=jax.ShapeDtypeStruct((NPAIR, 8, S), F32),
    )(lse, o, do)


def _attn_bwd(qc, kc, kct, v, do, rows):
    nq = S // TQ

    def body(q_ref, k_ref, kt_ref, v_ref, do_ref, r_ref, dqt_ref, dk_ref, dv_ref):
        kb = pl.program_id(1)

        @pl.when(kb == 0)
        def _():
            dqt_ref[...] = jnp.zeros_like(dqt_ref)

        lo = lax.broadcasted_iota(jnp.int32, (TK, LANE), 1) < VDIM
        q_minus_k = lax.broadcasted_iota(jnp.int32, (TK, TQ), 1) - lax.broadcasted_iota(jnp.int32, (TK, TQ), 0)
        vv = v_ref[...]
        kk = k_ref[...]

        def step(qi, carry):
            off = pl.multiple_of(qi * TQ, TQ)
            qq = q_ref[pl.ds(off, TQ), :]
            dd = do_ref[pl.ds(off, TQ), :].astype(BF16)
            rr = r_ref[:, pl.ds(off, TQ)]
            keep = q_minus_k >= (kb - qi) * TQ
            out = []
            for x in range(2):
                sel = lo if x == 0 else jnp.logical_not(lo)
                kx, qx = kk[:, x * LANE:(x + 1) * LANE], qq[:, x * LANE:(x + 1) * LANE]
                st = jnp.where(keep, _dot(kx, qx, ((1,), (1,))) * ATT_SCALE_LOG2, NEG)
                pt = jnp.exp2(st - rr[x:x + 1, :])
                dpt = _dot(jnp.where(sel, vv, 0), dd, ((1,), (1,)))
                dst = (pt * (dpt - rr[2 + x:3 + x, :]) * ATT_SCALE).astype(BF16)
                out.append(carry[x] + _dot(dst, qx, ((1,), (0,))))
                out.append(_dot(pt, jnp.where(sel, dd, 0), ((1,), (0,))))
                dqt_ref[x * LANE:(x + 1) * LANE, pl.ds(off, TQ)] += _dot(kt_ref[x * LANE:(x + 1) * LANE, :], dst, ((1,), (0,)))
            return out[0], out[2], carry[2] + out[1] + out[3]

        z = jnp.zeros((TK, LANE), F32)
        dka, dkb, dv = lax.fori_loop(kb, nq, step, (z, z, z))
        dk_ref[:, 0:LANE] = dka
        dk_ref[:, LANE:2 * LANE] = dkb
        dv_ref[...] = dv.astype(BF16)

    return pl.pallas_call(
        body, name="attn_bwd", grid=(NPAIR, S // TK),
        in_specs=[pl.BlockSpec((S, 2 * LANE), lambda j, k: (0, j)), pl.BlockSpec((TK, 2 * LANE), lambda j, k: (k, j)),
                  pl.BlockSpec((2 * LANE, TK), lambda j, k: (j, k)), pl.BlockSpec((TK, LANE), lambda j, k: (k, j)),
                  pl.BlockSpec((S, LANE), lambda j, k: (0, j)), pl.BlockSpec((None, 8, S), lambda j, k: (j, 0, 0))],
        out_specs=[pl.BlockSpec((2 * LANE, S), lambda j, k: (j, 0)), pl.BlockSpec((TK, 2 * LANE), lambda j, k: (k, j)),
                   pl.BlockSpec((TK, LANE), lambda j, k: (k, j))],
        out_shape=[jax.ShapeDtypeStruct((H * LANE, S), F32), jax.ShapeDtypeStruct((S, H * LANE), F32),
                   jax.ShapeDtypeStruct((S, H * VDIM), BF16)],
        compiler_params=pltpu.CompilerParams(dimension_semantics=("parallel", "arbitrary")),
    )(qc, kc, kct, v, do, rows)


_IN_Z, _IN_XBC, _IN_DT, _IN_Q, _IN_KV, _IN_KR = 0, 1024, 2560, 2576, 2960, 3216


PROJ_COLS = 512
SMALL_PAD = pl.cdiv(SMALL_W, PROJ_COLS) * PROJ_COLS


def _prep_in(w_in_t):
    dt = w_in_t.dtype
    return jnp.concatenate(
        [w_in_t[_IN_Q:_IN_KV], w_in_t[_IN_KV:_IN_KR], w_in_t[_IN_KR:IN_WIDTH], jnp.zeros((LANE - ROPE, D), dt),
         w_in_t[_IN_DT:_IN_Q], jnp.zeros((SMALL_PAD - SM_DT - H, D), dt)], axis=0)


def _proj_in(xb, w_in_t, w_small):
    nz, nx, ns = (_IN_XBC - _IN_Z) // PROJ_COLS, (_IN_DT - _IN_XBC) // PROJ_COLS, SMALL_PAD // PROJ_COLS

    dt_block, dt_at = divmod(SM_DT, PROJ_COLS)

    def body(x_ref, w_ref, ws_ref, z_ref, xbc_ref, sm_ref, dtt_ref):
        i = pl.program_id(0)

        def emit(w, o_ref):
            o_ref[...] = lax.dot_general(x_ref[...], w[...], (((1,), (1,)), ((), ())), preferred_element_type=F32)

        pl.when(i < nz)(lambda: emit(w_ref, z_ref))
        pl.when((i >= nz) & (i < nz + nx))(lambda: emit(w_ref, xbc_ref))
        pl.when(i >= nz + nx)(lambda: emit(ws_ref, sm_ref))

        @pl.when(i == nz + nx + dt_block)
        def _():
            dtt_ref[...] = sm_ref[:, dt_at:dt_at + LANE].T

    def blocks(first, count, rows):
        at = lambda i: jnp.clip(i - first, 0, count - 1)
        return pl.BlockSpec((PROJ_COLS, D), lambda i: (at(i), 0)) if rows else pl.BlockSpec((S, PROJ_COLS), lambda i: (0, at(i)))

    return pl.pallas_call(
        body, name="proj_in", grid=(nz + nx + ns,),
        in_specs=[pl.BlockSpec((S, D), lambda i: (0, 0)), blocks(0, nz + nx, True), blocks(nz + nx, ns, True)],
        out_specs=[blocks(0, nz, False), blocks(nz, nx, False), blocks(nz + nx, ns, False), pl.BlockSpec((LANE, S), lambda i: (0, 0))],
        out_shape=[jax.ShapeDtypeStruct((S, _IN_XBC - _IN_Z), F32), jax.ShapeDtypeStruct((S, _IN_DT - _IN_XBC), F32),
                   jax.ShapeDtypeStruct((S, SMALL_W), F32), jax.ShapeDtypeStruct((LANE, S), F32)],
    )(xb, w_in_t, w_small)


PART_COLS = 512


def _part_blocks(widths):
    first = [0]
    for w in widths:
        first.append(first[-1] + w // PART_COLS)

    def at(part):
        return lambda i: jnp.clip(i - first[part], 0, first[part + 1] - first[part] - 1)

    return first, at


def _mm_ta_stacked(parts, b, rows, name):
    n = b.shape[1]
    first, at = _part_blocks([a.shape[1] for a in parts])
    assert first[-1] == pl.cdiv(rows, PART_COLS)

    def body(*refs):
        b_ref, o_ref = refs[-2:]
        i = pl.program_id(0)
        for part, a_ref in enumerate(refs[:-2]):
            @pl.when((i >= first[part]) & (i < first[part + 1]))
            def _(a_ref=a_ref):
                o_ref[...] = lax.dot_general(a_ref[...], b_ref[...], (((0,), (0,)), ((), ())),
                                             preferred_element_type=F32).astype(BF16)

    return pl.pallas_call(
        body, name=name, grid=(first[-1],),
        in_specs=[pl.BlockSpec((S, PART_COLS), lambda i, at=at(part): (0, at(i))) for part in range(len(parts))]
        + [pl.BlockSpec((S, n), lambda i: (0, 0))],
        out_specs=pl.BlockSpec((PART_COLS, n), lambda i: (i, 0)), out_shape=jax.ShapeDtypeStruct((rows, n), BF16),
    )(*parts, b)


def _prep_attn(w_qb, w_kvb):
    w_q = jnp.pad(w_qb.reshape(Q_RANK, H, NOPE + ROPE), ((0, 0), (0, 0), (0, LANE - NOPE - ROPE))).reshape(Q_RANK, H * LANE)
    kv3 = w_kvb.reshape(KV_RANK, H, NOPE + VDIM)
    w_k = jnp.pad(kv3[:, :, :NOPE], ((0, 0), (0, 0), (0, LANE - NOPE))).reshape(KV_RANK, H * LANE)
    w_v = kv3[:, :, NOPE:].reshape(KV_RANK, H * VDIM)
    return w_q, w_k, w_v


def _rope_tables(positions):
    inv_freq = 1.0 / (10000.0 ** (jnp.arange(0, ROPE, 2, dtype=F32) / ROPE))
    ang = positions.astype(F32).reshape(S, 1) * inv_freq
    cos, sin = jnp.cos(ang), jnp.sin(ang)
    cos_t = jnp.concatenate([jnp.ones((S, NOPE), F32), cos, cos, jnp.ones((S, LANE - NOPE - ROPE), F32)], axis=1)
    sin_t = jnp.concatenate([jnp.zeros((S, NOPE), F32), -sin, sin, jnp.zeros((S, LANE - NOPE - ROPE), F32)], axis=1)
    return cos_t, sin_t


def _local_step(x, p, positions, target, w_in, fetch, send, sp, started):
    w_in_t = w_in.reshape(IN_WIDTH, D)
    w_small = _prep_in(w_in_t)
    cos_t, sin_t = _rope_tables(positions)
    prow = jnp.zeros((8, LANE), F32).at[0, :H].set(sp["dt_bias"][0]).at[1, :H].set(sp["A_log"][0]).at[2, :H].set(sp["D"][0])
    pcol = prow.T

    xb, pb = (x + started).astype(BF16), p.astype(BF16)
    z, xbc, small, dt_t = _proj_in(xb, w_in_t, w_small)
    act = _conv_fwd(xbc, sp["conv_w"], sp["conv_b"])
    y, states = _ssd_fwd(act, small, dt_t, prow, pcol)
    y_ssd = _gate_norm_fwd(y, z, sp["ssd_norm"])
    gl = fetch("attn", y_ssd)
    w_q, w_k, w_v = _prep_attn(_from_cols(gl["w_qb"]), _from_cols(gl["w_kvb"]))
    qn, kvn, qcat, kcat, kcat_t, v = _qkv_fwd(small, w_q, w_k, w_v, sp["q_norm"], sp["kv_norm"], cos_t, sin_t)
    o, lse = _attn_fwd(qcat, kcat, v)
    y_mla = _rms_fwd(o, sp["out_norm"], name="out_norm_fwd")
    w_out = fetch("out", y_mla)["w_out"]
    w_out = w_out.reshape(2 * SSD_INNER, D)
    mix, h1, h1b = _out_proj_ln(y_ssd, y_mla, w_out, x, sp["ln_mix_g"], sp["ln_mix_b"])
    gl = fetch("ffn", h1b)
    w_pg, w_pp = gl["w_pg"].reshape(D, D), _from_cols(gl["w_pp"])
    w_gate, w_up, w_down = gl["w_gate"], gl["w_up"], gl["w_down"]
    gate, up, actf = _ffn_hidden_fwd(h1b, w_gate, w_up)
    ffn = _mm([(actf, w_down)], chunk="sum", name="ffn_down")
    dpre2, dpre2b, dpg, dpp, dg2, db2, loss_row = _final_fwd_bwd(h1, ffn, h1b, pb, w_pg, w_pp, target, sp["ln_ffn_g"], sp["ln_ffn_b"])

    g = {"ln_ffn_g": dg2, "ln_ffn_b": db2}
    g["w_pp"] = _to_cols(_mm([(pb, dpp)], ta=True, out_dtype=BF16, name="d_w_ple_proj"))
    g["w_pg"] = _mm([(h1b, dpg)], ta=True, out_dtype=BF16, name="d_w_ple_gate").reshape(NCHIP, D // NCHIP, D)
    g["w_down"] = _mm([(actf, dpre2b)], ta=True, chunk="out", out_dtype=BF16, name="d_w_down")
    dgate, dup = _ffn_hidden_bwd(dpre2b, w_down, gate, up)
    g["w_gate"] = _mm([(dgate, h1b)], ta=True, chunk="out", out_dtype=BF16, name="d_w_gate")
    g["w_up"] = _mm([(dup, h1b)], ta=True, chunk="out", out_dtype=BF16, name="d_w_up")
    sent = send("ffn", {name: g.pop(name) for name in dict(ASYNC_GROUPS)["ffn"]})
    dh1 = _mm([(dgate, w_gate), (dup, w_up), (dpg, w_pg.T)], chunk="sum", add=dpre2, add_scale=ALPHA, name="d_h1")
    dpre1, dpre1b, g["ln_mix_g"], g["ln_mix_b"], dy_ssd, dy_mla = _ln_bwd(x, mix, sp["ln_mix_g"] + sent, dh1, w_out)
    dw_out = _mm_ta_stacked((y_ssd, y_mla), dpre1b, 2 * SSD_INNER, "d_w_out")
    sent = send("out", {"w_out": dw_out.reshape(NCHIP, 2 * SSD_INNER // NCHIP, D)})
    do, g["out_norm"] = _rms_bwd(o, sp["out_norm"] + sent, dy_mla, name="out_norm_bwd")
    dqt, dk, dv = _attn_bwd(qcat, kcat, kcat_t, v, do, _attn_rows(lse, o, do))
    dlatent, dqlin, dkb, g["q_norm"], g["kv_norm"] = _qkv_bwd(dqt, dk, dv, small, w_q, w_k, w_v, sp["q_norm"], sp["kv_norm"], cos_t, sin_t)
    dw_q = _mm([(qn, dqlin)], ta=True, out_dtype=BF16, name="d_w_q")
    dw_k = _mm([(kvn, dkb)], ta=True, out_dtype=BF16, name="d_w_k")
    dw_v = _mm([(kvn, dv)], ta=True, out_dtype=BF16, name="d_w_v")
    dw_qb = _to_cols(dw_q.reshape(Q_RANK, H, LANE)[:, :, :NOPE + ROPE].reshape(Q_RANK, H * (NOPE + ROPE)))
    dw_kvb = _to_cols(jnp.concatenate([dw_k.reshape(KV_RANK, H, LANE)[:, :, :NOPE], dw_v.reshape(KV_RANK, H, VDIM)],
                                       axis=2).reshape(KV_RANK, H * (NOPE + VDIM)))
    sent = send("attn", {"w_qb": dw_qb, "w_kvb": dw_kvb})
    dy, dz, g["ssd_norm"] = _gate_norm_bwd(y, z, sp["ssd_norm"] + sent, dy_ssd)
    dact, ddt, dprow = _ssd_bwd(act, small, dt_t, prow, pcol, states, dy)
    g["dt_bias"], g["A_log"], g["D"] = dprow[0:1, :H], dprow[1:2, :H], dprow[2:3, :H]
    dxbc, g["conv_w"], g["conv_b"] = _conv_bwd(xbc, sp["conv_w"], sp["conv_b"], dact)
    dsmall = jnp.concatenate([dlatent, ddt.astype(BF16)], axis=1)
    in_blocks = [(d, w_in_t, (k, first // PROJ_COLS + k, PROJ_COLS))
                 for d, first in ((dz, _IN_Z), (dxbc, _IN_XBC)) for k in range(d.shape[1] // PROJ_COLS)]
    grad_x = _mm(in_blocks + [(dsmall, w_small, (0, 0, SMALL_W))], add=dpre1, add_scale=ALPHA, name="d_x")
    sent = send("small", dict(g, loss=loss_row))
    n_small = IN_WIDTH - _IN_DT
    dsm = jnp.concatenate([(ddt[:, :H] + sent).astype(BF16), dlatent[:, :n_small - H], jnp.zeros((S, D - n_small), BF16)], axis=1)
    dw_in = _mm_ta_stacked((dz, dxbc, dsm), xb, IN_WIDTH, "d_w_in").reshape(NCHIP, IN_WIDTH // NCHIP * D // LANE, LANE)
    return loss_row, grad_x, dw_in, g


MESH = pl.DeviceIdType.MESH
BIG = (("w_in", (D, IN_WIDTH), 1), ("w_qb", (Q_RANK, H * (NOPE + ROPE)), 1), ("w_kvb", (KV_RANK, H * (NOPE + VDIM)), 1),
       ("w_out", (2 * SSD_INNER, D), 0), ("w_gate", (D, D_FF), 1), ("w_up", (D, D_FF), 1), ("w_down", (D_FF, D), 0),
       ("w_pg", (D, D), 0), ("w_pp", (PLE, D), 1))
CONV_SHARD = SSD_XBC // NCHIP
BF16_ROWS = 16


def _from_cols(stack):
    return jnp.concatenate([stack[k] for k in range(NCHIP)], axis=1)


def _to_cols(full):
    r, c4 = full.shape
    return full.reshape(r, NCHIP, c4 // NCHIP).transpose(1, 0, 2)


def _coords():
    return lax.axis_index("x"), lax.axis_index("y"), lax.axis_index("c")


def _peers():
    x, y, c = _coords()
    return 2 * x + y, c, [(1 - x, y), (x, 1 - y), (1 - x, 1 - y)], (x, y, 1 - c)


def _half_axis(shape):
    return 0 if shape[-2] % (2 * BF16_ROWS) == 0 else 1


def _half_shape(shape):
    r, c = shape[-2:]
    return (r // 2, c) if _half_axis(shape) == 0 else (r, c // 2)


def _half(core, shape):
    r, c = shape[-2:]
    if _half_axis(shape) == 0:
        return pl.ds(pl.multiple_of(core * (r // 2), BF16_ROWS), r // 2), slice(None)
    return slice(None), pl.ds(pl.multiple_of(core * (c // 2), LANE), c // 2)


def _gather_weights(shards):
    n_arr = len(shards)
    per = 2 * (NCHIP - 1)

    def body(*refs):
        ins, outs = refs[:n_arr], refs[n_arr:2 * n_arr]
        send_sems, recv_sems, local_sems = refs[2 * n_arr:]
        k, c, chips, sibling = _peers()

        def copy(idx, src, dst, to):
            return pltpu.make_async_remote_copy(src_ref=src, dst_ref=dst, send_sem=send_sems.at[idx], recv_sem=recv_sems.at[idx],
                                                device_id=to, device_id_type=MESH)

        def part(a, chip, core):
            return outs[a].at[chip, *_half(core, shards[a].shape)]

        mine = [pltpu.make_async_copy(ins[a], outs[a].at[k], local_sems.at[a]) for a in range(n_arr)]
        for cp in mine:
            cp.start()
        sends = []
        for a in range(n_arr):
            for j, (cx, cy) in enumerate(chips):
                sends.append(copy(per * a + j, ins[a].at[*_half(c, shards[a].shape)], part(a, k, c), (cx, cy, c)))
                sends[-1].start()
        for j, (cx, cy) in enumerate(chips):
            for a in range(n_arr):
                landed = part(a, 2 * cx + cy, c)
                copy(per * a + j, landed, landed, (cx, cy, c)).wait_recv()
                sends.append(copy(per * a + NCHIP - 1 + j, landed, landed, sibling))
                sends[-1].start()
        for j, (cx, cy) in enumerate(chips):
            for a in range(n_arr):
                other = part(a, 2 * cx + cy, 1 - c)
                copy(per * a + NCHIP - 1 + j, other, other, sibling).wait_recv()
        for cp in sends:
            cp.wait_send()
        for cp in mine:
            cp.wait()

    any_spec = pl.BlockSpec(memory_space=pl.ANY)
    return pl.pallas_call(
        body, name="gather_weights", in_specs=[any_spec] * n_arr, out_specs=[any_spec] * n_arr,
        out_shape=[jax.ShapeDtypeStruct((NCHIP,) + s.shape, s.dtype) for s in shards],
        scratch_shapes=[pltpu.SemaphoreType.DMA((per * n_arr,)), pltpu.SemaphoreType.DMA((per * n_arr,)),
                        pltpu.SemaphoreType.DMA((n_arr,))],
    )(*shards)


ASYNC_GROUPS = (("attn", ("w_qb", "w_kvb")), ("out", ("w_out",)), ("ffn", ("w_gate", "w_up", "w_down", "w_pg", "w_pp")))
TRANSPOSED = ("w_in", "w_gate", "w_up")
ROW_MAJOR = ("w_in",)
HBM_SPEC = pl.BlockSpec(memory_space=pltpu.HBM)
SEM_SPEC = pl.BlockSpec(memory_space=pltpu.SEMAPHORE)
IN_FLIGHT = pltpu.SideEffectType.DATAFLOW_SIDE_EFFECTING


def _in_hbm(a):
    return pltpu.with_memory_space_constraint(a, pltpu.HBM)


def _hbm_like(arrs, lead=()):
    return [pltpu.HBM(lead + a.shape, a.dtype) for a in arrs]


def _split_start(name, srcs, lands, after, n_sem, start):
    n = len(srcs)
    order = [] if after is None else [after]

    def body(*refs):
        src_refs, land_refs = refs[:n], refs[n:2 * n]
        send_sems, recv_sems = refs[2 * n + len(order)], refs[2 * n + len(order) + 1]
        token = refs[-1]

        def copy(send_idx, recv_idx, src, dst, to):
            return pltpu.make_async_remote_copy(src_ref=src, dst_ref=dst, send_sem=send_sems.at[send_idx],
                                                recv_sem=recv_sems.at[recv_idx], device_id=to, device_id_type=MESH)

        for cp in start(src_refs, land_refs, copy):
            cp.start()
        token[...] = jnp.zeros_like(token)

    sem = pltpu.SemaphoreType.DMA((n_sem,))
    outs = pl.pallas_call(
        body, name=name, in_specs=[HBM_SPEC] * (2 * n) + [pl.BlockSpec(memory_space=pl.ANY)] * len(order),
        out_specs=[SEM_SPEC, SEM_SPEC] + [HBM_SPEC] * (2 * n) + [pl.BlockSpec(memory_space=pltpu.VMEM)],
        out_shape=[sem, sem] + _hbm_like(srcs) + _hbm_like(lands) + [jax.ShapeDtypeStruct((8, LANE), F32)],
        input_output_aliases={i: 2 + i for i in range(2 * n)},
        compiler_params=pltpu.CompilerParams(has_side_effects=IN_FLIGHT),
    )(*[_in_hbm(a) for a in srcs], *[_in_hbm(a) for a in lands], *order)
    return (outs[0], outs[1], outs[2:2 + n], outs[2 + n:2 + 2 * n]), outs[-1]


def _split_wait(name, send_sems, recv_sems, srcs, lands, after, waits):
    n = len(srcs)

    def body(*refs):
        src_refs, land_refs = refs[:n], refs[n:2 * n]
        send_ref, recv_ref = refs[2 * n], refs[2 * n + 1]

        def copy(send_idx, recv_idx, src, dst, to):
            return pltpu.make_async_remote_copy(src_ref=src, dst_ref=dst, send_sem=send_ref.at[send_idx],
                                                recv_sem=recv_ref.at[recv_idx], device_id=to, device_id_type=MESH)

        for cp in waits(src_refs, land_refs, copy):
            cp.wait_send()
            cp.wait_recv()

    outs = pl.pallas_call(
        body, name=name, in_specs=[HBM_SPEC] * (2 * n) + [SEM_SPEC, SEM_SPEC, pl.BlockSpec(memory_space=pl.ANY)],
        out_specs=[HBM_SPEC] * (2 * n), out_shape=_hbm_like(srcs) + _hbm_like(lands),
        input_output_aliases={i: i for i in range(2 * n)},
        compiler_params=pltpu.CompilerParams(has_side_effects=IN_FLIGHT),
    )(*srcs, *lands, send_sems, recv_sems, after)
    return outs[:n], outs[n:]


GATHER_LATE_SEMS = 2 * (NCHIP - 1)


def _gather_async_start(tag, shards, after):
    def start(srcs, lands, copy):
        k, c, chips, _ = _peers()
        out = []
        for a, (src, dst) in enumerate(zip(srcs, lands)):
            for j, (cx, cy) in enumerate(chips):
                for core in range(2):
                    out.append(copy(GATHER_LATE_SEMS * a + 2 * j + core, GATHER_LATE_SEMS * a + 2 * j + c,
                                    src.at[*_half(c, src.shape)], dst.at[k, *_half(c, src.shape)], (cx, cy, core)))
        return out

    chip = 2 * lax.axis_index("x") + lax.axis_index("y")
    lands = [lax.dynamic_update_slice(lax.empty((NCHIP,) + s.shape, s.dtype), s[None], (chip, 0, 0)) for s in shards]
    return _split_start("gather_%s_start" % tag, shards, lands, after, GATHER_LATE_SEMS * len(shards), start)


def _gather_async_wait(tag, send_sems, recv_sems, shards, lands, after):
    def waits(srcs, lands_, copy):
        _, c, chips, _ = _peers()
        out = []
        for a, (src, dst) in enumerate(zip(srcs, lands_)):
            for j, (cx, cy) in enumerate(chips):
                for core in range(2):
                    idx = GATHER_LATE_SEMS * a + 2 * j + core
                    out.append(copy(idx, idx, src.at[*_half(c, src.shape)], dst.at[2 * cx + cy, *_half(core, src.shape)], (cx, cy, core)))
        return out

    return _split_wait("gather_%s_wait" % tag, send_sems, recv_sems, shards, lands, after, waits)[1]


def _other_devices():
    x, y, c = _coords()
    out = []
    for d in range(1, NDEV):
        tx, ty, tc = x ^ (d >> 2), y ^ ((d >> 1) & 1), c ^ (d & 1)
        out.append((d, (tx, ty, tc), 2 * tx + ty, 4 * tx + 2 * ty + tc))
    return out


def _reduce_async_start(tag, stacks, after):
    def start(srcs, lands, copy):
        x, y, c = _coords()
        me = 4 * x + 2 * y + c
        return [copy((NDEV - 1) * a + d - 1, (NDEV - 1) * a + d - 1, src.at[chip, *_half(to[2], src.shape)], dst.at[me], to)
                for a, (src, dst) in enumerate(zip(srcs, lands)) for d, to, chip, _ in _other_devices()]

    x, y, c = _coords()
    lands = []
    for s in stacks:
        hr, hc = _half_shape(s.shape)
        at = (c * hr, 0) if _half_axis(s.shape) == 0 else (0, c * hc)
        own = lax.dynamic_slice(s, (2 * x + y,) + at, (1, hr, hc))
        lands.append(lax.dynamic_update_slice(lax.empty((NDEV, hr, hc), s.dtype), own, (4 * x + 2 * y + c, 0, 0)))
    return _split_start("reduce_%s_start" % tag, stacks, lands, after, (NDEV - 1) * len(stacks), start)


def _reduce_async_wait(tag, send_sems, recv_sems, stacks, lands, after):
    def waits(srcs, lands_, copy):
        return [copy((NDEV - 1) * a + d - 1, (NDEV - 1) * a + d - 1, src.at[chip, *_half(to[2], src.shape)], dst.at[pos], to)
                for a, (src, dst) in enumerate(zip(srcs, lands_)) for d, to, chip, pos in _other_devices()]

    return _split_wait("reduce_%s_wait" % tag, send_sems, recv_sems, stacks, lands, after, waits)[1]


def _reduce_finish(tag, arrived, dims):
    n_arr = len(arrived)

    def body(*refs):
        lands, fin = refs[:n_arr], refs[n_arr:2 * n_arr]
        send_sems, recv_sems = refs[2 * n_arr:]
        _, c, _, sibling = _peers()
        sends = []
        for a in range(n_arr):
            mine = fin[a].at[*_half(c, dims[a])]

            def device_sum(vs, vf, a=a, mine=mine):
                pltpu.sync_copy(lands[a], vs)
                acc = vs[0].astype(F32)
                for i in range(1, NDEV):
                    acc = acc + vs[i].astype(F32)
                vf[...] = acc
                pltpu.sync_copy(vf, mine)

            pl.run_scoped(device_sum, pltpu.VMEM((NDEV,) + _half_shape(dims[a]), BF16), pltpu.VMEM(_half_shape(dims[a]), F32))
            sends.append(pltpu.make_async_remote_copy(src_ref=mine, dst_ref=mine, send_sem=send_sems.at[a], recv_sem=recv_sems.at[a],
                                                      device_id=sibling, device_id_type=MESH))
            sends[-1].start()
        for a in range(n_arr):
            other = fin[a].at[*_half(1 - c, dims[a])]
            pltpu.make_async_remote_copy(src_ref=other, dst_ref=other, send_sem=send_sems.at[a], recv_sem=recv_sems.at[a],
                                         device_id=sibling, device_id_type=MESH).wait_recv()
        for cp in sends:
            cp.wait_send()

    any_spec = pl.BlockSpec(memory_space=pl.ANY)
    return pl.pallas_call(
        body, name="reduce_%s_finish" % tag, in_specs=[any_spec] * n_arr, out_specs=[any_spec] * n_arr,
        out_shape=[jax.ShapeDtypeStruct(d, F32) for d in dims],
        scratch_shapes=[pltpu.SemaphoreType.DMA((n_arr,)), pltpu.SemaphoreType.DMA((n_arr,))],
    )(*arrived)


SMALL = (("conv_w", SSD_K * SSD_XBC), ("conv_b", SSD_XBC), ("dt_bias", H), ("A_log", H), ("D", H), ("ssd_norm", SSD_INNER),
         ("q_norm", Q_RANK), ("kv_norm", KV_RANK), ("out_norm", SSD_INNER), ("ln_mix_g", D), ("ln_mix_b", D),
         ("ln_ffn_g", D), ("ln_ffn_b", D))
SMALL_ROWS = 120
NDEV = 8


def _allreduce_small_start(sv):
    def start(srcs, lands, copy):
        x, y, c = _coords()
        return [copy(d - 1, d - 1, srcs[0], lands[0].at[4 * x + 2 * y + c], to) for d, to, _, _ in _other_devices()]

    x, y, c = _coords()
    slots = lax.dynamic_update_slice(lax.empty((NDEV,) + sv.shape, sv.dtype), sv[None], (4 * x + 2 * y + c, 0, 0))
    return _split_start("allreduce_small_start", [sv], [slots], None, NDEV - 1, start)


def _allreduce_small_wait(send_sems, recv_sems, srcs, lands, after):
    def waits(srcs_, lands_, copy):
        return [copy(d - 1, d - 1, srcs_[0], lands_[0].at[pos], to) for d, to, _, pos in _other_devices()]

    def device_sum(slots_ref, out_ref):
        acc = slots_ref[0]
        for i in range(1, NDEV):
            acc = acc + slots_ref[i]
        out_ref[...] = acc

    slots = _split_wait("allreduce_small_wait", send_sems, recv_sems, srcs, lands, after, waits)[1][0]
    vm = pl.BlockSpec(memory_space=pltpu.VMEM)
    return pl.pallas_call(device_sum, name="allreduce_small_sum", in_specs=[vm], out_specs=vm,
                          out_shape=jax.ShapeDtypeStruct(slots.shape[1:], slots.dtype))(slots)


def _adamw_math(w, g, m, v):
    m2 = ADAM_B1 * m + (1.0 - ADAM_B1) * g
    v2 = ADAM_B2 * v + (1.0 - ADAM_B2) * (g * g)
    m_hat = m2 / (1.0 - ADAM_B1 ** ADAM_STEP)
    v_hat = v2 / (1.0 - ADAM_B2 ** ADAM_STEP)
    return -ADAM_LR * (m_hat / (jnp.sqrt(v_hat) + ADAM_EPS) + ADAM_WD * w), m2, v2


ADAM_BLOCK_BYTES = 2 * 1024 * 1024


def _adamw_big(w, g, m, v, *, name):
    r, c = w.shape

    def body(w_ref, g_ref, m_ref, v_ref, d_ref, m2_ref, v2_ref):
        d_ref[...], m2_ref[...], v2_ref[...] = _adamw_math(w_ref[...], g_ref[...], m_ref[...], v_ref[...])

    tr = max(t for t in range(8, r + 1, 8) if r % t == 0 and t * c * 4 <= ADAM_BLOCK_BYTES)
    steps, spec = r // tr, pl.BlockSpec((tr, c), lambda i: (i, 0))
    return pl.pallas_call(body, name=name, grid=(steps,), in_specs=[spec] * 4, out_specs=[spec] * 3,
                          out_shape=[jax.ShapeDtypeStruct((r, c), F32)] * 3)(w, g, m, v)


def _adamw_small(ws, gs, ms, vs):
    n = len(ws)

    def body(*refs):
        for i in range(n):
            w_ref, g_ref, m_ref, v_ref = (refs[j * n + i] for j in range(4))
            d_ref, m2_ref, v2_ref = (refs[(4 + j) * n + i] for j in range(3))
            d_ref[...], m2_ref[...], v2_ref[...] = _adamw_math(w_ref[...], g_ref[...], m_ref[...], v_ref[...])

    vm = pl.BlockSpec(memory_space=pltpu.VMEM)
    shapes = [jax.ShapeDtypeStruct(w.shape, F32) for w in ws]
    outs = pl.pallas_call(body, name="adamw_small", in_specs=[vm] * (4 * n), out_specs=[vm] * (3 * n), out_shape=shapes * 3)(
        *ws, *gs, *ms, *vs)
    return outs[:n], outs[n:2 * n], outs[2 * n:]


_SMALL_ARG = {"conv_w": "ssd_conv_w", "conv_b": "ssd_conv_b", "dt_bias": "ssd_dt_bias", "A_log": "ssd_A_log", "D": "ssd_D",
              "ssd_norm": "ssd_norm_w", "q_norm": "mla_q_norm_w", "kv_norm": "mla_kv_norm_w", "out_norm": "mla_out_norm_w",
              "ln_mix_g": "ln_mix_g", "ln_mix_b": "ln_mix_b", "ln_ffn_g": "ln_ffn_g", "ln_ffn_b": "ln_ffn_b"}
_BIG_ARG = {"w_in": "w_in", "w_qb": "mla_w_q_b", "w_kvb": "mla_w_kv_b", "w_out": "w_out", "w_gate": "w_ffn_gate",
            "w_up": "w_ffn_up", "w_down": "w_ffn_down", "w_pg": "w_ple_gate", "w_pp": "w_ple_proj"}
_WEIGHT_ORDER = ("w_in", "ssd_conv_w", "ssd_conv_b", "ssd_dt_bias", "ssd_A_log", "ssd_D", "ssd_norm_w", "mla_q_norm_w", "mla_w_q_b",
                 "mla_kv_norm_w", "mla_w_kv_b", "mla_out_norm_w", "w_out", "ln_mix_g", "ln_mix_b", "w_ffn_gate", "w_ffn_up",
                 "w_ffn_down", "w_ple_gate", "w_ple_proj", "ln_ffn_g", "ln_ffn_b")


def _rows128(a):
    flat = a.reshape(-1)
    return jnp.pad(flat, (0, -flat.shape[0] % LANE)).reshape(-1, LANE)


def kernel(x, p, positions, w_in, ssd_conv_w, ssd_conv_b, ssd_dt_bias, ssd_A_log, ssd_D, ssd_norm_w, mla_q_norm_w, mla_w_q_b, mla_kv_norm_w, mla_w_kv_b, mla_out_norm_w, w_out, ln_mix_g, ln_mix_b, w_ffn_gate, w_ffn_up, w_ffn_down, w_ple_gate, w_ple_proj, ln_ffn_g, ln_ffn_b, loss_target, m_w_in, m_ssd_conv_w, m_ssd_conv_b, m_ssd_dt_bias, m_ssd_A_log, m_ssd_D, m_ssd_norm_w, m_mla_q_norm_w, m_mla_w_q_b, m_mla_kv_norm_w, m_mla_w_kv_b, m_mla_out_norm_w, m_w_out, m_ln_mix_g, m_ln_mix_b, m_w_ffn_gate, m_w_ffn_up, m_w_ffn_down, m_w_ple_gate, m_w_ple_proj, m_ln_ffn_g, m_ln_ffn_b, v_w_in, v_ssd_conv_w, v_ssd_conv_b, v_ssd_dt_bias, v_ssd_A_log, v_ssd_D, v_ssd_norm_w, v_mla_q_norm_w, v_mla_w_q_b, v_mla_kv_norm_w, v_mla_w_kv_b, v_mla_out_norm_w, v_w_out, v_ln_mix_g, v_ln_mix_b, v_w_ffn_gate, v_w_ffn_up, v_w_ffn_down, v_w_ple_gate, v_w_ple_proj, v_ln_ffn_g, v_ln_ffn_b):
    given = dict(locals())
    chip = 2 * lax.axis_index("x") + lax.axis_index("y")

    def local(name, prefix=""):
        a = given[prefix + _BIG_ARG[name]][0]
        return a.T if name in TRANSPOSED else a

    def updated(name, prefix=""):
        if name in ROW_MAJOR:
            _, c, r = given[prefix + _BIG_ARG[name]].shape
            return given[prefix + _BIG_ARG[name]].reshape(c // LANE, LANE, r).transpose(2, 0, 1).reshape(-1, LANE)
        return local(name, prefix)

    def global_layout(name, arr):
        if name in ROW_MAJOR:
            r, c = local(name).shape
            return arr.reshape(r, c // LANE, LANE).transpose(1, 2, 0).reshape(1, c, r)
        return (arr.T if name in TRANSPOSED else arr)[None]

    conv_bits = lax.bitcast_convert_type(ssd_conv_w[0], BF16).reshape(SSD_K, 2 * CONV_SHARD)
    w_in_all, conv_all = _gather_weights([local("w_in").astype(BF16), jnp.pad(conv_bits, ((0, BF16_ROWS - SSD_K), (0, 0)))])
    sp = {k: given[a] for k, a in _SMALL_ARG.items() if k != "conv_w"}
    sp["conv_w"] = _from_cols(lax.bitcast_convert_type(conv_all[:, :SSD_K].reshape(NCHIP, SSD_K, CONV_SHARD, 2), F32))
    gathering, tie = {}, w_in_all
    for group, names in ASYNC_GROUPS:
        gathering[group], tie = _gather_async_start(group, [local(name).astype(BF16) for name in names], tie)

    def fetch(group, after):
        return dict(zip(dict(ASYNC_GROUPS)[group], _gather_async_wait(group, *gathering[group], after)))

    reducing = {}

    def send(group, grads):
        if group == "small":
            rows = jnp.concatenate([_rows128(grads[name]) for name, _ in SMALL] + [grads["loss"]], axis=0)
            reducing[group], sent = _allreduce_small_start(jnp.pad(rows, ((0, SMALL_ROWS - rows.shape[0]), (0, 0))))
        else:
            reducing[group], sent = _reduce_async_start(group, [grads[name] for name in dict(ASYNC_GROUPS)[group]], None)
        return sent[0, 0]

    loss_row, grad_x, dw_in, g = _local_step(x[0], p[0, 0], positions[0], loss_target[0], w_in_all, fetch, send, sp, tie[0, 0])

    reducing["in"], tie = _reduce_async_start("in", [dw_in], grad_x)
    gbig = {}
    for group, names in reversed(ASYNC_GROUPS):
        arrived = _reduce_async_wait(group, *reducing[group], tie)
        gbig.update(zip(names, _reduce_finish(group, arrived, [local(name).shape for name in names])))
    small_sum = _allreduce_small_wait(*reducing.pop("small"), tie)
    gsmall, row = {}, 0
    for name, size in SMALL:
        nrow = -(-size // LANE)
        gsmall[name] = small_sum[row:row + nrow].reshape(-1)[:size]
        row += nrow
    loss = small_sum[row, 0]

    grads = {_BIG_ARG[name]: global_layout(name, arr) for name, arr in gbig.items()}
    for name, _ in SMALL:
        if name == "conv_w":
            full_g = gsmall[name].reshape(SSD_K, SSD_XBC)
            grads["ssd_conv_w"] = lax.dynamic_slice(full_g, (0, chip * CONV_SHARD), (SSD_K, CONV_SHARD))[None]
        else:
            grads[_SMALL_ARG[name]] = gsmall[name].reshape(given[_SMALL_ARG[name]].shape)

    delta, new_m, new_v = {}, {}, {}

    def update_matrix(name, grad):
        a = _BIG_ARG[name]
        d, m2, v2 = _adamw_big(updated(name), grad, updated(name, "m_"), updated(name, "v_"), name="adamw_" + a)
        delta[a], new_m[a], new_v[a] = (global_layout(name, t) for t in (d, m2, v2))
        return d

    all_updated = sum(update_matrix(name, grad)[:8, :LANE] for name, grad in gbig.items())
    g_in = _reduce_finish("in", _reduce_async_wait("in", *reducing["in"], all_updated), [updated("w_in").shape])[0]
    grads["w_in"] = global_layout("w_in", g_in)
    update_matrix("w_in", g_in)
    small_names = [_SMALL_ARG[name] for name, _ in SMALL]
    two_d = lambda t: t.reshape(t.shape[-2], t.shape[-1])
    ds, ms, vs = _adamw_small([two_d(given[a]) for a in small_names], [two_d(grads[a]) for a in small_names],
                              [two_d(given["m_" + a]) for a in small_names], [two_d(given["v_" + a]) for a in small_names])
    for a, d, m2, v2 in zip(small_names, ds, ms, vs):
        delta[a], new_m[a], new_v[a] = (t.reshape(given[a].shape) for t in (d, m2, v2))

    return (loss, grad_x[None], *[grads[n] for n in _WEIGHT_ORDER], *[delta[n] for n in _WEIGHT_ORDER],
            *[new_m[n] for n in _WEIGHT_ORDER], *[new_v[n] for n in _WEIGHT_ORDER])
```

```python
import functools
import math

import jax
import jax.numpy as jnp
from jax import lax
from jax.experimental import pallas as pl
from jax.experimental.pallas import tpu as pltpu

F32 = jnp.float32
BF16 = jnp.bfloat16

S = 2048
D = 1024
PLE = 256
H = 16
SSD_P = 64
SSD_INNER = 1024
SSD_N = 128
SSD_G = 2
SSD_L = 128
SSD_NC = S // SSD_L
SSD_XBC = 1536
SSD_K = 4
Q_RANK = 384
KV_RANK = 256
NOPE = 64
ROPE = 32
VDIM = 64
D_FF = 2816
IN_WIDTH = 3248
ALPHA = 2.0 ** 0.25
EPS_RMS = 1e-6
EPS_LN = 1e-5
ATT_SCALE = 1.0 / math.sqrt(NOPE + ROPE)
LN2 = math.log(2.0)
ATT_SCALE_LOG2 = ATT_SCALE / LN2
LANE = 128
NCHIP = 4
SMALL_W = 896
SM_Q, SM_KV, SM_KR, SM_DT = 0, 384, 640, 768
NEG = -1e30

ADAM_LR = 0.001
ADAM_B1 = 0.9
ADAM_B2 = 0.999
ADAM_EPS = 1e-08
ADAM_WD = 0.01
ADAM_STEP = 10


def _sigmoid(v):
    return 1.0 / (1.0 + jnp.exp(-v))


MM_VMEM_BUDGET = 36 * 2 ** 20
MM_MAX_ACC = 2048 * 1024


def _mm_tiles(pairs, ks, m, n, out_dtype, has_add):
    def divs(v):
        return [LANE * d for d in range(v // LANE, 0, -1) if (v // LANE) % d == 0] if v % LANE == 0 else [v]

    def cost(tm, tn):
        tot = tm * tn * (jnp.dtype(out_dtype).itemsize + (4 if has_add else 0))
        for (a, b), k in zip(pairs, ks):
            tot += k * (tm * a.dtype.itemsize + tn * b.dtype.itemsize)
        return 2 * tot

    ok = [(tm * tn, tm, tn) for tm in divs(m) for tn in divs(n) if tm * tn <= MM_MAX_ACC and cost(tm, tn) <= MM_VMEM_BUDGET]
    _, tm, tn = max(ok)
    return tm, tn


def _mm(pairs, *, ta=False, tb=False, out_dtype=F32, add=None, add_scale=1.0, chunk=None, name):
    n_pairs = len(pairs)
    windows = [pr[2] if len(pr) == 3 else None for pr in pairs]
    pairs = [pr[:2] for pr in pairs]
    assert not ((ta or tb) and any(windows))
    ks = [w[2] if w else (a.shape[-2] if ta else a.shape[-1]) for (a, _), w in zip(pairs, windows)]
    a0, b0 = pairs[0]
    m = a0.shape[-1] if ta else a0.shape[-2]
    n = b0.shape[-2] if tb else b0.shape[-1]
    tm, tn = _mm_tiles(pairs, ks, m, n, out_dtype, add is not None)
    dims = (((0 if ta else 1,), (1 if tb else 0,)), ((), ()))
    nk = NCHIP if chunk else 1
    assert chunk != "sum" or out_dtype == F32
    flat = [i for i, (a, b) in enumerate(pairs) if a.ndim == 2 and b.ndim == 2]
    stacked = [i for i in range(n_pairs) if i not in flat]

    def body(*refs):
        o_ref = refs[-1]

        def products(which):
            acc = None
            for i in which:
                a = refs[2 * i][...].astype(BF16)
                b = refs[2 * i + 1][...].astype(BF16)
                part = lax.dot_general(a, b, dims, preferred_element_type=F32)
                acc = part if acc is None else acc + part
            return acc

        if chunk == "sum":
            k = pl.program_id(2)
            acc = products(stacked)

            @pl.when(k == 0)
            def _():
                first = acc + products(flat) if flat else acc
                o_ref[...] = first + add_scale * refs[2 * n_pairs][...] if add is not None else first

            @pl.when(k > 0)
            def _():
                o_ref[...] += acc
            return
        acc = products(range(n_pairs))
        if add is not None:
            acc = acc + add_scale * refs[2 * n_pairs][...]
        o_ref[...] = acc.astype(out_dtype)

    def spec(arr, shape, idx2):
        if arr.ndim == 3:
            return pl.BlockSpec((None,) + shape, lambda i, j, k: (k,) + idx2(i, j))
        return pl.BlockSpec(shape, lambda i, j, k: idx2(i, j))

    in_specs, args = [], []
    for (a, b), kdim, window in zip(pairs, ks, windows):
        ka, kb = window[:2] if window else (0, 0)
        in_specs.append(spec(a, (kdim, tm), lambda i, j: (0, i)) if ta else spec(a, (tm, kdim), lambda i, j, ka=ka: (i, ka)))
        in_specs.append(spec(b, (tn, kdim), lambda i, j: (j, 0)) if tb else spec(b, (kdim, tn), lambda i, j, kb=kb: (kb, j)))
        args += [a, b]
    if add is not None:
        in_specs.append(pl.BlockSpec((tm, tn), lambda i, j, k: (i, j)))
        args.append(add)
    if chunk == "out":
        out_spec = pl.BlockSpec((None, tm, tn), lambda i, j, k: (k, i, j))
        out_shape = jax.ShapeDtypeStruct((nk, m, n), out_dtype)
    else:
        out_spec = pl.BlockSpec((tm, tn), lambda i, j, k: (i, j))
        out_shape = jax.ShapeDtypeStruct((m, n), out_dtype)
    return pl.pallas_call(
        body, name=name, grid=(m // tm, n // tn, nk), in_specs=in_specs, out_specs=out_spec, out_shape=out_shape,
        compiler_params=pltpu.CompilerParams(dimension_semantics=("parallel", "parallel", "arbitrary")),
    )(*args)


TR = 256


def _row_spec(c):
    return pl.BlockSpec((TR, c), lambda i: (i, 0))


def _vec_spec(c):
    return pl.BlockSpec((1, c), lambda i: (0, 0))


def _acc_rows(ref, val):
    @pl.when(pl.program_id(0) == 0)
    def _():
        ref[...] = jnp.zeros_like(ref)
    ref[...] += val


def _rms_fwd(u, w, *, name):
    c = u.shape[1]

    def body(u_ref, w_ref, o_ref):
        v = u_ref[...]
        r = lax.rsqrt(jnp.mean(v * v, axis=-1, keepdims=True) + EPS_RMS)
        o_ref[...] = (v * r * w_ref[...]).astype(BF16)

    return pl.pallas_call(body, name=name, grid=(S // TR,), in_specs=[_row_spec(c), _vec_spec(c)], out_specs=_row_spec(c),
                          out_shape=jax.ShapeDtypeStruct((S, c), BF16))(u, w)


def _rms_bwd(u, w, dy, *, name):
    c = u.shape[1]

    def body(u_ref, w_ref, dy_ref, du_ref, dw_ref):
        v = u_ref[...]
        g = dy_ref[...].astype(F32)
        r = lax.rsqrt(jnp.mean(v * v, axis=-1, keepdims=True) + EPS_RMS)
        gw = g * w_ref[...]
        du_ref[...] = r * gw - v * (r * r * r * jnp.mean(gw * v, axis=-1, keepdims=True))
        _acc_rows(dw_ref, jnp.sum(g * v * r, axis=0, keepdims=True))

    return pl.pallas_call(body, name=name, grid=(S // TR,), in_specs=[_row_spec(c), _vec_spec(c), _row_spec(c)],
                          out_specs=[_row_spec(c), _vec_spec(c)],
                          out_shape=[jax.ShapeDtypeStruct((S, c), F32), jax.ShapeDtypeStruct((1, c), F32)])(u, w, dy)


def _gate_norm_fwd(y, z, w):
    def body(y_ref, z_ref, w_ref, o_ref):
        zz = z_ref[...]
        v = y_ref[...] * (zz * _sigmoid(zz))
        r = lax.rsqrt(jnp.mean(v * v, axis=-1, keepdims=True) + EPS_RMS)
        o_ref[...] = (v * r * w_ref[...]).astype(BF16)

    c = SSD_INNER
    return pl.pallas_call(body, name="ssd_gate_norm_fwd", grid=(S // TR,), in_specs=[_row_spec(c), _row_spec(c), _vec_spec(c)],
                          out_specs=_row_spec(c), out_shape=jax.ShapeDtypeStruct((S, c), BF16))(y, z, w)


def _gate_norm_bwd(y, z, w, dout):
    def body(y_ref, z_ref, w_ref, g_ref, dy_ref, dz_ref, dw_ref):
        yy = y_ref[...]
        zz = z_ref[...]
        sg = _sigmoid(zz)
        sz = zz * sg
        v = yy * sz
        g = g_ref[...]
        r = lax.rsqrt(jnp.mean(v * v, axis=-1, keepdims=True) + EPS_RMS)
        gw = g * w_ref[...]
        dv = r * gw - v * (r * r * r * jnp.mean(gw * v, axis=-1, keepdims=True))
        dy_ref[...] = dv * sz
        dz_ref[...] = (dv * yy * (sg * (1.0 + zz * (1.0 - sg)))).astype(BF16)
        _acc_rows(dw_ref, jnp.sum(g * v * r, axis=0, keepdims=True))

    c = SSD_INNER
    return pl.pallas_call(body, name="ssd_gate_norm_bwd", grid=(S // TR,),
                          in_specs=[_row_spec(c), _row_spec(c), _vec_spec(c), _row_spec(c)],
                          out_specs=[_row_spec(c), _row_spec(c), _vec_spec(c)],
                          out_shape=[jax.ShapeDtypeStruct((S, c), F32), jax.ShapeDtypeStruct((S, c), BF16),
                                     jax.ShapeDtypeStruct((1, c), F32)])(y, z, w, dout)


MIX_ROWS = 512


def _out_proj_ln(y_ssd, y_mla, w_out, xr, g, b):
    k = y_ssd.shape[1]

    def body(ys_ref, ym_ref, w_ref, x_ref, g_ref, b_ref, m_ref, o_ref, ob_ref):
        mix = (jnp.dot(ys_ref[...], w_ref[:k], preferred_element_type=F32)
               + jnp.dot(ym_ref[...], w_ref[k:], preferred_element_type=F32))
        m_ref[...] = mix
        pre = ALPHA * x_ref[...] + mix
        mu = jnp.mean(pre, axis=-1, keepdims=True)
        d = pre - mu
        rs = lax.rsqrt(jnp.mean(d * d, axis=-1, keepdims=True) + EPS_LN)
        h = d * rs * g_ref[...] + b_ref[...]
        o_ref[...] = h
        ob_ref[...] = h.astype(BF16)

    rows = lambda c: pl.BlockSpec((MIX_ROWS, c), lambda i: (i, 0))
    return pl.pallas_call(
        body, name="out_proj_ln", grid=(S // MIX_ROWS,),
        in_specs=[rows(k), rows(k), _whole(w_out), rows(D), _vec_spec(D), _vec_spec(D)], out_specs=[rows(D)] * 3,
        out_shape=[jax.ShapeDtypeStruct((S, D), F32), jax.ShapeDtypeStruct((S, D), F32), jax.ShapeDtypeStruct((S, D), BF16)],
    )(y_ssd, y_mla, w_out, xr, g, b)


def _ln_bwd(xr, mix, g, dh, w_out):
    k = w_out.shape[0] // 2

    def body(x_ref, m_ref, g_ref, dh_ref, w_ref, dpre_ref, dpreb_ref, dg_ref, db_ref, dys_ref, dym_ref):
        pre = ALPHA * x_ref[...] + m_ref[...]
        mu = jnp.mean(pre, axis=-1, keepdims=True)
        d = pre - mu
        rs = lax.rsqrt(jnp.mean(d * d, axis=-1, keepdims=True) + EPS_LN)
        xh = d * rs
        dy = dh_ref[...]
        gy = dy * g_ref[...]
        dpre = rs * (gy - jnp.mean(gy, axis=-1, keepdims=True) - xh * jnp.mean(gy * xh, axis=-1, keepdims=True))
        dpre_ref[...] = dpre
        dpreb = dpre.astype(BF16)
        dpreb_ref[...] = dpreb
        _acc_rows(dg_ref, jnp.sum(dy * xh, axis=0, keepdims=True))
        _acc_rows(db_ref, jnp.sum(dy, axis=0, keepdims=True))
        dys_ref[...] = lax.dot_general(dpreb, w_ref[:k], (((1,), (1,)), ((), ())), preferred_element_type=F32)
        dym_ref[...] = lax.dot_general(dpreb, w_ref[k:], (((1,), (1,)), ((), ())), preferred_element_type=F32)

    rows = lambda c: pl.BlockSpec((MIX_ROWS, c), lambda i: (i, 0))
    return pl.pallas_call(
        body, name="ln_mix_bwd", grid=(S // MIX_ROWS,),
        in_specs=[rows(D), rows(D), _vec_spec(D), rows(D), _whole(w_out)],
        out_specs=[rows(D), rows(D), _vec_spec(D), _vec_spec(D), rows(k), rows(k)],
        out_shape=[jax.ShapeDtypeStruct((S, D), F32), jax.ShapeDtypeStruct((S, D), BF16), jax.ShapeDtypeStruct((1, D), F32),
                   jax.ShapeDtypeStruct((1, D), F32), jax.ShapeDtypeStruct((S, k), F32), jax.ShapeDtypeStruct((S, k), F32)],
    )(xr, mix, g, dh, w_out)


FF_CHUNK = D_FF // NCHIP


FF_ROWS = 1024


def _ff_act_spec():
    return pl.BlockSpec((None, FF_ROWS, FF_CHUNK), lambda i, k: (k, i, 0))


def _ff_w_spec():
    return pl.BlockSpec((None, FF_CHUNK, D), lambda i, k: (k, 0, 0))


def _ffn_hidden_fwd(h, w_gate_t, w_up_t):
    def body(h_ref, wg_ref, wu_ref, g_ref, u_ref, a_ref):
        hh = h_ref[...]
        g = _dot(hh, wg_ref[...], ((1,), (1,)))
        u = _dot(hh, wu_ref[...], ((1,), (1,)))
        g_ref[...] = g.astype(BF16)
        u_ref[...] = u.astype(BF16)
        a_ref[...] = (g * _sigmoid(g) * u).astype(BF16)

    return pl.pallas_call(
        body, name="ffn_hidden_fwd", grid=(S // FF_ROWS, NCHIP),
        in_specs=[pl.BlockSpec((FF_ROWS, D), lambda i, k: (i, 0)), _ff_w_spec(), _ff_w_spec()], out_specs=[_ff_act_spec()] * 3,
        out_shape=[jax.ShapeDtypeStruct((NCHIP, S, FF_CHUNK), BF16)] * 3,
        compiler_params=pltpu.CompilerParams(dimension_semantics=("parallel", "parallel")),
    )(h, w_gate_t, w_up_t)


def _ffn_hidden_bwd(dout, w_down, gate, up):
    def body(d_ref, wd_ref, g_ref, u_ref, dg_ref, du_ref):
        d = _dot(d_ref[...], wd_ref[...], ((1,), (1,)))
        g = g_ref[...].astype(F32)
        sg = _sigmoid(g)
        dg_ref[...] = (d * u_ref[...].astype(F32) * (sg * (1.0 + g * (1.0 - sg)))).astype(BF16)
        du_ref[...] = (d * g * sg).astype(BF16)

    return pl.pallas_call(
        body, name="ffn_hidden_bwd", grid=(S // FF_ROWS, NCHIP),
        in_specs=[pl.BlockSpec((FF_ROWS, D), lambda i, k: (i, 0)), _ff_w_spec(), _ff_act_spec(), _ff_act_spec()],
        out_specs=[_ff_act_spec()] * 2, out_shape=[jax.ShapeDtypeStruct((NCHIP, S, FF_CHUNK), BF16)] * 2,
        compiler_params=pltpu.CompilerParams(dimension_semantics=("parallel", "parallel")),
    )(dout, w_down, gate, up)


def _final_fwd_bwd(h1, ffn, h1b, pb, w_pg, w_pp, target, g2, b2):
    def body(h_ref, f_ref, hb_ref, pb_ref, wpg_ref, wpp_ref, t_ref, g_ref, b_ref,
             dpre_ref, dpreb_ref, dpg_ref, dpp_ref, dg_ref, db_ref, loss_ref):
        sg = _sigmoid(jnp.dot(hb_ref[...], wpg_ref[...], preferred_element_type=F32))
        ppv = jnp.dot(pb_ref[...], wpp_ref[...], preferred_element_type=F32)
        pre = ALPHA * h_ref[...] + f_ref[...] + sg * ppv
        mu = jnp.mean(pre, axis=-1, keepdims=True)
        d = pre - mu
        rs = lax.rsqrt(jnp.mean(d * d, axis=-1, keepdims=True) + EPS_LN)
        xh = d * rs
        err = xh * g_ref[...] + b_ref[...] - t_ref[...]
        dy = err * (1.0 / D)
        gy = dy * g_ref[...]
        dpre = rs * (gy - jnp.mean(gy, axis=-1, keepdims=True) - xh * jnp.mean(gy * xh, axis=-1, keepdims=True))
        dpre_ref[...] = dpre
        dpreb_ref[...] = dpre.astype(BF16)
        dpg_ref[...] = (dpre * ppv * sg * (1.0 - sg)).astype(BF16)
        dpp_ref[...] = (dpre * sg).astype(BF16)
        _acc_rows(dg_ref, jnp.sum(dy * xh, axis=0, keepdims=True))
        _acc_rows(db_ref, jnp.sum(dy, axis=0, keepdims=True))
        _acc_rows(loss_ref, 0.5 * jnp.sum(jnp.mean(err * err, axis=-1, keepdims=True), axis=0, keepdims=True) * jnp.ones((1, LANE), F32))

    return pl.pallas_call(
        body, name="final_ln_loss", grid=(S // TR,),
        in_specs=[_row_spec(D)] * 3 + [_row_spec(pb.shape[1]), _whole(w_pg), _whole(w_pp), _row_spec(D)] + [_vec_spec(D)] * 2,
        out_specs=[_row_spec(D)] * 4 + [_vec_spec(D), _vec_spec(D), _vec_spec(LANE)],
        out_shape=[jax.ShapeDtypeStruct((S, D), F32)] + [jax.ShapeDtypeStruct((S, D), BF16)] * 3 + [
                   jax.ShapeDtypeStruct((1, D), F32), jax.ShapeDtypeStruct((1, D), F32), jax.ShapeDtypeStruct((1, LANE), F32)],
    )(h1, ffn, h1b, pb, w_pg, w_pp, target, g2, b2)


def _rot(u, cos_t, sin_t, lane):
    partner = jnp.where(lane < NOPE + ROPE // 2, pltpu.roll(u, LANE - ROPE // 2, 1), pltpu.roll(u, ROPE // 2, 1))
    return u * cos_t + partner * sin_t


def _rms(v, w):
    r = lax.rsqrt(jnp.mean(v * v, axis=-1, keepdims=True) + EPS_RMS)
    return v * r * w, r


def _rms_grad(v, r, w, g):
    gw = g * w
    return r * gw - v * (r * r * r * jnp.mean(gw * v, axis=-1, keepdims=True)), jnp.sum(g * v * r, axis=0, keepdims=True)


def _whole(arr):
    return pl.BlockSpec(arr.shape, lambda i: (0,) * arr.ndim)


def _qkv_fwd(small, w_q, w_k, w_v, q_norm, kv_norm, cos_t, sin_t):
    def body(sm_ref, wq_ref, wk_ref, wv_ref, qw_ref, kw_ref, c_ref, s_ref, qn_ref, kvn_ref, q_ref, k_ref, kt_ref, v_ref):
        lane = lax.broadcasted_iota(jnp.int32, (TR, LANE), 1)
        c, s = c_ref[...], s_ref[...]
        qn = _rms(sm_ref[:, SM_Q:SM_Q + Q_RANK], qw_ref[...])[0].astype(BF16)
        kvn = _rms(sm_ref[:, SM_KV:SM_KV + KV_RANK], kw_ref[...])[0].astype(BF16)
        qn_ref[...] = qn
        kvn_ref[...] = kvn
        kr = _rot(pltpu.roll(sm_ref[:, SM_KR:SM_KR + LANE], NOPE, 1), c, s, lane)
        for h in range(H):
            tile = slice(h * LANE, (h + 1) * LANE)
            q_ref[:, tile] = _rot(_dot(qn, wq_ref[:, tile], ((1,), (0,))), c, s, lane).astype(BF16)
            kt = _dot(kvn, wk_ref[:, tile], ((1,), (0,))) + kr
            k_ref[:, tile] = kt.astype(BF16)
            kt_ref[tile, :] = kt.T.astype(BF16)
        v_ref[...] = _dot(kvn, wv_ref[...], ((1,), (0,))).astype(BF16)

    w = H * LANE
    return pl.pallas_call(
        body, name="qkv_fwd", grid=(S // TR,),
        in_specs=[_row_spec(SMALL_W), _whole(w_q), _whole(w_k), _whole(w_v), _vec_spec(Q_RANK), _vec_spec(KV_RANK), _row_spec(LANE), _row_spec(LANE)],
        out_specs=[_row_spec(Q_RANK), _row_spec(KV_RANK), _row_spec(w), _row_spec(w), pl.BlockSpec((w, TR), lambda i: (0, i)),
                   _row_spec(H * VDIM)],
        out_shape=[jax.ShapeDtypeStruct((S, Q_RANK), BF16), jax.ShapeDtypeStruct((S, KV_RANK), BF16), jax.ShapeDtypeStruct((S, w), BF16),
                   jax.ShapeDtypeStruct((S, w), BF16), jax.ShapeDtypeStruct((w, S), BF16), jax.ShapeDtypeStruct((S, H * VDIM), BF16)],
    )(small, w_q, w_k, w_v, q_norm, kv_norm, cos_t, sin_t)


def _qkv_bwd(dqt, dk, dv, small, w_q, w_k, w_v, q_norm, kv_norm, cos_t, sin_t):
    def body(dq_ref, dk_ref, dv_ref, sm_ref, wq_ref, wk_ref, wv_ref, qw_ref, kw_ref, c_ref, s_ref,
             ds_ref, dql_ref, dkb_ref, dqw_ref, dkw_ref):
        lane = lax.broadcasted_iota(jnp.int32, (TR, LANE), 1)
        c, s = c_ref[...], -s_ref[...]
        dqn = jnp.zeros((TR, Q_RANK), F32)
        dkvn = _dot(dv_ref[...], wv_ref[...], ((1,), (1,)))
        dkr = jnp.zeros((TR, LANE), F32)
        for h in range(H):
            tile = slice(h * LANE, (h + 1) * LANE)
            dql = _rot(dq_ref[tile, :].T, c, s, lane).astype(BF16)
            dql_ref[:, tile] = dql
            dqn = dqn + _dot(dql, wq_ref[:, tile], ((1,), (1,)))
            dkt = dk_ref[:, tile]
            dkb_ref[:, tile] = dkt.astype(BF16)
            dkvn = dkvn + _dot(dkt, wk_ref[:, tile], ((1,), (1,)))
            dkr = dkr + dkt
        dkr = jnp.where((lane >= NOPE) & (lane < NOPE + ROPE), dkr, 0.0)
        q_c, kv_c = sm_ref[:, SM_Q:SM_Q + Q_RANK], sm_ref[:, SM_KV:SM_KV + KV_RANK]
        dq_c, dqw = _rms_grad(q_c, _rms(q_c, qw_ref[...])[1], qw_ref[...], dqn)
        dkv_c, dkw = _rms_grad(kv_c, _rms(kv_c, kw_ref[...])[1], kw_ref[...], dkvn)
        ds_ref[:, SM_Q:SM_Q + Q_RANK] = dq_c.astype(BF16)
        ds_ref[:, SM_KV:SM_KV + KV_RANK] = dkv_c.astype(BF16)
        ds_ref[:, SM_KR:SM_KR + LANE] = pltpu.roll(_rot(dkr, c, s, lane), LANE - NOPE, 1).astype(BF16)
        _acc_rows(dqw_ref, dqw)
        _acc_rows(dkw_ref, dkw)

    w = H * LANE
    return pl.pallas_call(
        body, name="qkv_bwd", grid=(S // TR,),
        in_specs=[pl.BlockSpec((w, TR), lambda i: (0, i)), _row_spec(w), _row_spec(H * VDIM), _row_spec(SMALL_W), _whole(w_q), _whole(w_k),
                  _whole(w_v), _vec_spec(Q_RANK), _vec_spec(KV_RANK), _row_spec(LANE), _row_spec(LANE)],
        out_specs=[_row_spec(SM_DT), _row_spec(w), _row_spec(w), _vec_spec(Q_RANK), _vec_spec(KV_RANK)],
        out_shape=[jax.ShapeDtypeStruct((S, SM_DT), BF16), jax.ShapeDtypeStruct((S, w), BF16), jax.ShapeDtypeStruct((S, w), BF16),
                   jax.ShapeDtypeStruct((1, Q_RANK), F32), jax.ShapeDtypeStruct((1, KV_RANK), F32)],
    )(dqt, dk, dv, small, w_q, w_k, w_v, q_norm, kv_norm, cos_t, sin_t)


CB = 256


def _shift_down(u, k, row):
    if k == 0:
        return u
    return jnp.where(row >= k, pltpu.roll(u, k, 0), 0.0)


def _shift_up(u, k, row):
    if k == 0:
        return u
    return jnp.where(row < S - k, pltpu.roll(u, S - k, 0), 0.0)


def _conv_fwd(u, w, b):
    def body(u_ref, w_ref, b_ref, o_ref):
        row = lax.broadcasted_iota(jnp.int32, (S, CB), 0)
        uu = u_ref[...]
        acc = b_ref[...] + w_ref[SSD_K - 1:SSD_K, :] * uu
        for k in range(SSD_K - 1):
            acc = acc + w_ref[k:k + 1, :] * _shift_down(uu, SSD_K - 1 - k, row)
        o_ref[...] = acc * _sigmoid(acc)

    c = u.shape[1]
    return pl.pallas_call(
        body, name="conv_fwd", grid=(c // CB,),
        in_specs=[pl.BlockSpec((S, CB), lambda j: (0, j)), pl.BlockSpec((SSD_K, CB), lambda j: (0, j)), pl.BlockSpec((1, CB), lambda j: (0, j))],
        out_specs=pl.BlockSpec((S, CB), lambda j: (0, j)), out_shape=jax.ShapeDtypeStruct((S, c), F32),
    )(u, w, b)


def _conv_bwd(u, w, b, dact):
    def body(u_ref, w_ref, b_ref, d_ref, du_ref, dw_ref, db_ref):
        row = lax.broadcasted_iota(jnp.int32, (S, CB), 0)
        uu = u_ref[...]
        sh = [_shift_down(uu, SSD_K - 1 - k, row) for k in range(SSD_K)]
        acc = b_ref[...]
        for k in range(SSD_K):
            acc = acc + w_ref[k:k + 1, :] * sh[k]
        sg = _sigmoid(acc)
        dacc = d_ref[...] * (sg * (1.0 + acc * (1.0 - sg)))
        du = w_ref[SSD_K - 1:SSD_K, :] * dacc
        for k in range(SSD_K - 1):
            du = du + w_ref[k:k + 1, :] * _shift_up(dacc, SSD_K - 1 - k, row)
        du_ref[...] = du.astype(BF16)
        for k in range(SSD_K):
            dw_ref[k:k + 1, :] = jnp.sum(dacc * sh[k], axis=0, keepdims=True)
        db_ref[...] = jnp.sum(dacc, axis=0, keepdims=True)

    c = u.shape[1]
    col = lambda r: pl.BlockSpec((r, CB), lambda j: (0, j))
    return pl.pallas_call(
        body, name="conv_bwd", grid=(c // CB,), in_specs=[col(S), col(SSD_K), col(1), col(S)], out_specs=[col(S), col(SSD_K), col(1)],
        out_shape=[jax.ShapeDtypeStruct((S, c), BF16), jax.ShapeDtypeStruct((SSD_K, c), F32), jax.ShapeDtypeStruct((1, c), F32)],
    )(u, w, b, dact)


NPAIR = H // 2
PAIRS_PER_GROUP = NPAIR // SSD_G


def _softplus(v):
    return jnp.maximum(v, 0.0) + jnp.log(1.0 + jnp.exp(-jnp.abs(v)))


def _dot(a, b, dims):
    return lax.dot_general(a.astype(BF16), b.astype(BF16), (dims, ((), ())), preferred_element_type=F32)


def _dot2(a, sel):
    hi = a.astype(BF16)
    lo = (a - hi.astype(F32)).astype(BF16)
    dims = (((1,), (0,)), ((), ()))
    return lax.dot_general(hi, sel, dims, preferred_element_type=F32) + lax.dot_general(lo, sel, dims, preferred_element_type=F32)


def _dot3(a, b, dims, split_lhs):
    v = a if split_lhs else b
    v1 = v.astype(BF16)
    r1 = v - v1.astype(F32)
    v2 = r1.astype(BF16)
    v3 = (r1 - v2.astype(F32)).astype(BF16)
    acc = None
    for part in (v1, v2, v3):
        lhs, rhs = (part, b) if split_lhs else (a, part)
        t = lax.dot_general(lhs, rhs, (dims, ((), ())), preferred_element_type=F32)
        acc = t if acc is None else acc + t
    return acc


def _ssd_chunk_common(dt_ref, dtT_ref, prow_ref, pcol_ref):
    prow = prow_ref[...]
    pcol = pcol_ref[...]
    ri = lax.broadcasted_iota(jnp.int32, (SSD_L, SSD_L), 0)
    ci = lax.broadcasted_iota(jnp.int32, (SSD_L, SSD_L), 1)
    causal = ri >= ci
    pre_c = dt_ref[...] + prow[0:1, :]
    dtc = _softplus(pre_c)
    a_row = -jnp.exp(prow[1:2, :])
    cs_col = _dot3(causal.astype(BF16), dtc * a_row, ((1,), (0,)), False)
    dtr = _softplus(dtT_ref[...] + pcol[:, 0:1])
    a_col = -jnp.exp(pcol[:, 1:2])
    cs_row = _dot3(dtr * a_col, (ri <= ci).astype(BF16), ((1,), (0,)), True)
    return prow, causal, pre_c, dtc, a_row, cs_col, cs_row


def _ssd_fwd(act, small, dtT, prow, pcol):
    def body(x_ref, b_ref, c_ref, dt_ref, dtT_ref, prow_ref, pcol_ref, y_ref, st_ref, state):
        @pl.when(pl.program_id(0) == 0)
        def _():
            state[...] = jnp.zeros_like(state)

        prow, causal, _, dtc, _, cs_col, cs_row = _ssd_chunk_common(dt_ref, dtT_ref, prow_ref, pcol_ref)
        lo = lax.broadcasted_iota(jnp.int32, (SSD_L, LANE), 1) < SSD_P
        lo1 = lo[0:1, :]
        for g in range(SSD_G):
            bm = b_ref[:, g * SSD_N:(g + 1) * SSD_N]
            cm = c_ref[:, g * SSD_N:(g + 1) * SSD_N]
            cb = _dot(cm, bm, ((1,), (1,)))
            for qq in range(PAIRS_PER_GROUP):
                q = g * PAIRS_PER_GROUP + qq
                ha, hb = 2 * q, 2 * q + 1
                csa, csb = cs_col[:, ha:ha + 1], cs_col[:, hb:hb + 1]
                xp = x_ref[:, q * LANE:(q + 1) * LANE]
                xx = xp * jnp.where(lo, dtc[:, ha:ha + 1], dtc[:, hb:hb + 1])
                ga = cb * jnp.exp(jnp.where(causal, csa - cs_row[ha:ha + 1, :], NEG))
                gb = cb * jnp.exp(jnp.where(causal, csb - cs_row[hb:hb + 1, :], NEG))
                y = _dot(ga, jnp.where(lo, xx, 0.0), ((1,), (0,))) + _dot(gb, jnp.where(lo, 0.0, xx), ((1,), (0,)))
                s_in = state[q]
                y = y + _dot(cm, s_in, ((1,), (0,))) * jnp.where(lo, jnp.exp(csa), jnp.exp(csb))
                y = y + jnp.where(lo1, prow[2:3, ha:ha + 1], prow[2:3, hb:hb + 1]) * xp
                y_ref[:, q * LANE:(q + 1) * LANE] = y
                la, lb = csa[SSD_L - 1:SSD_L, :], csb[SSD_L - 1:SSD_L, :]
                decay = jnp.where(lo, jnp.exp(la - csa), jnp.exp(lb - csb))
                st_ref[q] = s_in
                state[q] = s_in * jnp.where(lo1, jnp.exp(la), jnp.exp(lb)) + _dot(bm, xx * decay, ((0,), (0,)))

    L = SSD_L
    return pl.pallas_call(
        body, name="ssd_fwd", grid=(SSD_NC,),
        in_specs=[pl.BlockSpec((L, SSD_INNER), lambda c: (c, 0)),
                  pl.BlockSpec((L, SSD_G * SSD_N), lambda c: (c, SSD_INNER // (SSD_G * SSD_N))),
                  pl.BlockSpec((L, SSD_G * SSD_N), lambda c: (c, SSD_INNER // (SSD_G * SSD_N) + 1)),
                  pl.BlockSpec((L, LANE), lambda c: (c, SM_DT // LANE)),
                  pl.BlockSpec((LANE, L), lambda c: (0, c)),
                  pl.BlockSpec((8, LANE), lambda c: (0, 0)), pl.BlockSpec((LANE, 8), lambda c: (0, 0))],
        out_specs=[pl.BlockSpec((L, SSD_INNER), lambda c: (c, 0)),
                   pl.BlockSpec((None, NPAIR, SSD_N, LANE), lambda c: (c, 0, 0, 0))],
        out_shape=[jax.ShapeDtypeStruct((S, SSD_INNER), F32), jax.ShapeDtypeStruct((SSD_NC, NPAIR, SSD_N, LANE), F32)],
        scratch_shapes=[pltpu.VMEM((NPAIR, SSD_N, LANE), F32)],
        compiler_params=pltpu.CompilerParams(dimension_semantics=("arbitrary",)),
    )(act, act, act, small, dtT, prow, pcol)


def _ssd_bwd(act, small, dtT, prow, pcol, states, dy):
    def body(x_ref, b_ref, c_ref, dt_ref, dtT_ref, prow_ref, pcol_ref, st_ref, dy_ref,
             dx_ref, ddt_ref, dp_ref, dstate):
        @pl.when(pl.program_id(0) == 0)
        def _():
            dstate[...] = jnp.zeros_like(dstate)
            dp_ref[...] = jnp.zeros_like(dp_ref)

        prow, causal, pre_c, dtc, a_row, cs_col, cs_row = _ssd_chunk_common(dt_ref, dtT_ref, prow_ref, pcol_ref)
        lane = lax.broadcasted_iota(jnp.int32, (SSD_L, LANE), 1)
        sub = lax.broadcasted_iota(jnp.int32, (LANE, SSD_L), 0)
        rowi = lax.broadcasted_iota(jnp.int32, (SSD_L, 1), 0)
        pick_p = lax.broadcasted_iota(jnp.int32, (LANE, LANE), 0)
        pick_l = lax.broadcasted_iota(jnp.int32, (LANE, LANE), 1)
        lo = lane < SSD_P
        lo1 = lo[0:1, :]
        dcs_c = jnp.zeros((SSD_L, LANE), F32)
        dcs_r = jnp.zeros((LANE, SSD_L), F32)
        ddt_x = jnp.zeros((SSD_L, LANE), F32)
        dd_row = jnp.zeros((1, LANE), F32)
        for g in range(SSD_G):
            bm = b_ref[:, g * SSD_N:(g + 1) * SSD_N]
            cm = c_ref[:, g * SSD_N:(g + 1) * SSD_N]
            cb = _dot(cm, bm, ((1,), (1,)))
            dcb = jnp.zeros((SSD_L, SSD_L), F32)
            dbm = jnp.zeros((SSD_L, SSD_N), F32)
            dcm = jnp.zeros((SSD_L, SSD_N), F32)
            for qq in range(PAIRS_PER_GROUP):
                q = g * PAIRS_PER_GROUP + qq
                ha, hb = 2 * q, 2 * q + 1
                csa, csb = cs_col[:, ha:ha + 1], cs_col[:, hb:hb + 1]
                xp = x_ref[:, q * LANE:(q + 1) * LANE]
                dtp = jnp.where(lo, dtc[:, ha:ha + 1], dtc[:, hb:hb + 1])
                xx = xp * dtp
                lma = jnp.exp(jnp.where(causal, csa - cs_row[ha:ha + 1, :], NEG))
                lmb = jnp.exp(jnp.where(causal, csb - cs_row[hb:hb + 1, :], NEG))
                ga, gb = cb * lma, cb * lmb
                dyp = dy_ref[:, q * LANE:(q + 1) * LANE]
                dya, dyb = jnp.where(lo, dyp, 0.0), jnp.where(lo, 0.0, dyp)
                s_in = st_ref[q]
                ds_out = dstate[q]
                la, lb = csa[SSD_L - 1:SSD_L, :], csb[SSD_L - 1:SSD_L, :]
                ecs = jnp.where(lo, jnp.exp(csa), jnp.exp(csb))
                decay = jnp.where(lo, jnp.exp(la - csa), jnp.exp(lb - csb))
                cd = jnp.where(lo1, jnp.exp(la), jnp.exp(lb))
                bds = _dot(bm, ds_out, ((1,), (0,)))
                dxx = _dot(ga, dya, ((0,), (0,))) + _dot(gb, dyb, ((0,), (0,))) + bds * decay
                dga = _dot(dya, xx, ((1,), (1,)))
                dgb = _dot(dyb, xx, ((1,), (1,)))
                dsega, dsegb = dga * ga, dgb * gb
                dcb = dcb + dga * lma + dgb * lmb
                yoff = _dot(cm, s_in, ((1,), (0,))) * ecs
                dye = dyp * ecs
                dcm = dcm + _dot(dye, s_in, ((1,), (1,)))
                xd = xx * decay
                dbm = dbm + _dot(xd, ds_out, ((1,), (1,)))
                wv = xd * bds
                ends = jnp.sum(wv, axis=0, keepdims=True) + cd * jnp.sum(ds_out * s_in, axis=0, keepdims=True)
                t1 = dyp * yoff - wv + jnp.where(rowi == SSD_L - 1, ends, 0.0)
                to_pair = (((pick_p < SSD_P) & (pick_l == ha)) | ((pick_p >= SSD_P) & (pick_l == hb))).astype(BF16)
                to_a_b = jnp.concatenate([(pick_l == ha).astype(BF16), (pick_l == hb).astype(BF16)], axis=0)
                dcs_c = dcs_c + _dot2(t1, to_pair) + _dot2(jnp.concatenate([dsega, dsegb], axis=1), to_a_b)
                dcs_r = (dcs_r + jnp.where(sub == ha, jnp.sum(dsega, axis=0, keepdims=True), 0.0)
                         + jnp.where(sub == hb, jnp.sum(dsegb, axis=0, keepdims=True), 0.0))
                dstate[q] = _dot(cm, dye, ((0,), (0,))) + cd * ds_out
                dpair = jnp.where(lo1, prow[2:3, ha:ha + 1], prow[2:3, hb:hb + 1])
                dx_ref[:, q * LANE:(q + 1) * LANE] = dxx * dtp + dpair * dyp
                ddt_x = ddt_x + _dot2(dxx * xp, to_pair)
                dd_row = dd_row + jnp.sum(_dot2(dyp * xp, to_pair), axis=0, keepdims=True)
            dx_ref[:, SSD_INNER + g * SSD_N:SSD_INNER + (g + 1) * SSD_N] = dbm + _dot(dcb, cm, ((0,), (0,)))
            dx_ref[:, SSD_INNER + (SSD_G + g) * SSD_N:SSD_INNER + (SSD_G + g + 1) * SSD_N] = dcm + _dot(dcb, bm, ((1,), (0,)))
        ri = lax.broadcasted_iota(jnp.int32, (SSD_L, SSD_L), 0)
        ci = lax.broadcasted_iota(jnp.int32, (SSD_L, SSD_L), 1)
        da = _dot3((ri <= ci).astype(BF16), dcs_c, ((1,), (0,)), False)
        da = da - _dot3(dcs_r, causal.astype(BF16), ((1,), (0,)), True).T
        ddt = ddt_x + da * a_row
        ddt_raw = ddt * _sigmoid(pre_c)
        ddt_ref[...] = ddt_raw
        da_head = jnp.sum(da * dtc, axis=0, keepdims=True) * a_row
        dp_ref[0:1, :] += jnp.sum(ddt_raw, axis=0, keepdims=True)
        dp_ref[1:2, :] += da_head
        dp_ref[2:3, :] += dd_row

    L = SSD_L
    rev = SSD_NC - 1
    bc_cols = SSD_INNER // (SSD_G * SSD_N)
    return pl.pallas_call(
        body, name="ssd_bwd", grid=(SSD_NC,),
        in_specs=[pl.BlockSpec((L, SSD_INNER), lambda c: (rev - c, 0)),
                  pl.BlockSpec((L, SSD_G * SSD_N), lambda c: (rev - c, bc_cols)),
                  pl.BlockSpec((L, SSD_G * SSD_N), lambda c: (rev - c, bc_cols + 1)),
                  pl.BlockSpec((L, LANE), lambda c: (rev - c, SM_DT // LANE)),
                  pl.BlockSpec((LANE, L), lambda c: (0, rev - c)),
                  pl.BlockSpec((8, LANE), lambda c: (0, 0)), pl.BlockSpec((LANE, 8), lambda c: (0, 0)),
                  pl.BlockSpec((None, NPAIR, SSD_N, LANE), lambda c: (rev - c, 0, 0, 0)),
                  pl.BlockSpec((L, SSD_INNER), lambda c: (rev - c, 0))],
        out_specs=[pl.BlockSpec((L, SSD_XBC), lambda c: (rev - c, 0)),
                   pl.BlockSpec((L, LANE), lambda c: (rev - c, 0)),
                   pl.BlockSpec((8, LANE), lambda c: (0, 0))],
        out_shape=[jax.ShapeDtypeStruct((S, SSD_XBC), F32), jax.ShapeDtypeStruct((S, LANE), F32),
                   jax.ShapeDtypeStruct((8, LANE), F32)],
        scratch_shapes=[pltpu.VMEM((NPAIR, SSD_N, LANE), F32)],
        compiler_params=pltpu.CompilerParams(dimension_semantics=("arbitrary",)),
    )(act, act, act, small, dtT, prow, pcol, states, dy)


TQ = 256
TK = 256
FWD_TQ = 256
FWD_TK = 256


def _attn_fwd(qc, kc, v):
    TQ, TK = FWD_TQ, FWD_TK

    def body(q_ref, k_ref, v_ref, o_ref, lse_ref):
        i = pl.program_id(1)
        lo = lax.broadcasted_iota(jnp.int32, (TQ, LANE), 1) < VDIM
        lo_k = lax.broadcasted_iota(jnp.int32, (TK, LANE), 1) < VDIM
        row_minus_col = lax.broadcasted_iota(jnp.int32, (TQ, TK), 0) - lax.broadcasted_iota(jnp.int32, (TQ, TK), 1)
        qa, qb = q_ref[:, 0:LANE], q_ref[:, LANE:2 * LANE]

        def scores(kb):
            kk = k_ref[pl.ds(pl.multiple_of(kb * TK, TK), TK), :]
            return (_dot(qa, kk[:, 0:LANE], ((1,), (1,))) * ATT_SCALE_LOG2, _dot(qb, kk[:, LANE:2 * LANE], ((1,), (1,))) * ATT_SCALE_LOG2)

        def update(kb, sa, sb, stats):
            ma, la, mb, lb, acc = stats
            vv = v_ref[pl.ds(pl.multiple_of(kb * TK, TK), TK), :]
            na = jnp.maximum(ma, jnp.max(sa, axis=1, keepdims=True))
            nb = jnp.maximum(mb, jnp.max(sb, axis=1, keepdims=True))
            pa, pb = jnp.exp2(sa - na), jnp.exp2(sb - nb)
            fa, fb = jnp.exp2(ma - na), jnp.exp2(mb - nb)
            la = fa * la + jnp.sum(pa, axis=1, keepdims=True)
            lb = fb * lb + jnp.sum(pb, axis=1, keepdims=True)
            acc = (acc * jnp.where(lo, fa, fb) + _dot(pa, jnp.where(lo_k, vv, 0), ((1,), (0,)))
                   + _dot(pb, jnp.where(lo_k, 0, vv), ((1,), (0,))))
            return na, la, nb, lb, acc

        def step(kb, carry):
            sa, sb = carry[:2]
            nxt = scores(kb + 1)
            return nxt + update(kb, sa, sb, carry[2:])

        neg = jnp.full((TQ, 1), NEG, F32)
        zero = jnp.zeros((TQ, 1), F32)
        n_full = i * (TQ // TK)
        carry = lax.fori_loop(0, n_full, step, scores(0) + (neg, zero, neg, zero, jnp.zeros((TQ, LANE), F32)))
        s, stats = carry[:2], carry[2:]
        for d in range(TQ // TK):
            nxt = scores(n_full + d + 1) if d + 1 < TQ // TK else None
            sa, sb = (jnp.where(row_minus_col >= d * TK, t, NEG) for t in s)
            stats = update(n_full + d, sa, sb, stats)
            s = nxt
        ma, la, mb, lb, acc = stats
        o_ref[...] = acc / jnp.where(lo, la, lb)
        lse_ref[...] = jnp.where(lo, ma + jnp.log2(la), mb + jnp.log2(lb)) * LN2

    return pl.pallas_call(
        body, name="attn_fwd", grid=(NPAIR, S // TQ),
        in_specs=[pl.BlockSpec((TQ, 2 * LANE), lambda j, i: (i, j)), pl.BlockSpec((S, 2 * LANE), lambda j, i: (0, j)),
                  pl.BlockSpec((S, LANE), lambda j, i: (0, j))],
        out_specs=[pl.BlockSpec((TQ, LANE), lambda j, i: (i, j)), pl.BlockSpec((None, TQ, LANE), lambda j, i: (j, i, 0))],
        out_shape=[jax.ShapeDtypeStruct((S, H * VDIM), F32), jax.ShapeDtypeStruct((NPAIR, S, LANE), F32)],
        compiler_params=pltpu.CompilerParams(dimension_semantics=("parallel", "parallel")),
    )(qc, kc, v)


def _attn_rows(lse, o, do):
    def body(lse_ref, o_ref, do_ref, r_ref):
        lt = lse_ref[...].T * (1.0 / LN2)
        tt = (o_ref[...] * do_ref[...]).T
        r_ref[...] = jnp.zeros_like(r_ref)
        r_ref[0:1, :] = lt[0:1, :]
        r_ref[1:2, :] = lt[VDIM:VDIM + 1, :]
        r_ref[2:3, :] = jnp.sum(tt[0:VDIM, :], axis=0, keepdims=True)
        r_ref[3:4, :] = jnp.sum(tt[VDIM:LANE, :], axis=0, keepdims=True)

    tile = pl.BlockSpec((S, LANE), lambda j: (0, j))
    return pl.pallas_call(
        body, name="attn_rows", grid=(NPAIR,), in_specs=[pl.BlockSpec((None, S, LANE), lambda j: (j, 0, 0)), tile, tile],
        out_specs=pl.BlockSpec((None, 8, S), lambda j: (j, 0, 0)), out_shape=jax.ShapeDtypeStruct((NPAIR, 8, S), F32),
    )(lse, o, do)


def _attn_bwd(qc, kc, kct, v, do, rows):
    nq = S // TQ

    def body(q_ref, k_ref, kt_ref, v_ref, do_ref, r_ref, dqt_ref, dk_ref, dv_ref):
        kb = pl.program_id(1)

        @pl.when(kb == 0)
        def _():
            dqt_ref[...] = jnp.zeros_like(dqt_ref)

        lo = lax.broadcasted_iota(jnp.int32, (TK, LANE), 1) < VDIM
        q_minus_k = lax.broadcasted_iota(jnp.int32, (TK, TQ), 1) - lax.broadcasted_iota(jnp.int32, (TK, TQ), 0)
        vv = v_ref[...]
        kk = k_ref[...]

        def step(qi, carry):
            off = pl.multiple_of(qi * TQ, TQ)
            qq = q_ref[pl.ds(off, TQ), :]
            dd = do_ref[pl.ds(off, TQ), :].astype(BF16)
            rr = r_ref[:, pl.ds(off, TQ)]
            keep = q_minus_k >= (kb - qi) * TQ
            out = []
            for x in range(2):
                sel = lo if x == 0 else jnp.logical_not(lo)
                kx, qx = kk[:, x * LANE:(x + 1) * LANE], qq[:, x * LANE:(x + 1) * LANE]
                st = jnp.where(keep, _dot(kx, qx, ((1,), (1,))) * ATT_SCALE_LOG2, NEG)
                pt = jnp.exp2(st - rr[x:x + 1, :])
                dpt = _dot(jnp.where(sel, vv, 0), dd, ((1,), (1,)))
                dst = (pt * (dpt - rr[2 + x:3 + x, :]) * ATT_SCALE).astype(BF16)
                out.append(carry[x] + _dot(dst, qx, ((1,), (0,))))
                out.append(_dot(pt, jnp.where(sel, dd, 0), ((1,), (0,))))
                dqt_ref[x * LANE:(x + 1) * LANE, pl.ds(off, TQ)] += _dot(kt_ref[x * LANE:(x + 1) * LANE, :], dst, ((1,), (0,)))
            return out[0], out[2], carry[2] + out[1] + out[3]

        z = jnp.zeros((TK, LANE), F32)
        dka, dkb, dv = lax.fori_loop(kb, nq, step, (z, z, z))
        dk_ref[:, 0:LANE] = dka
        dk_ref[:, LANE:2 * LANE] = dkb
        dv_ref[...] = dv.astype(BF16)

    return pl.pallas_call(
        body, name="attn_bwd", grid=(NPAIR, S // TK),
        in_specs=[pl.BlockSpec((S, 2 * LANE), lambda j, k: (0, j)), pl.BlockSpec((TK, 2 * LANE), lambda j, k: (k, j)),
                  pl.BlockSpec((2 * LANE, TK), lambda j, k: (j, k)), pl.BlockSpec((TK, LANE), lambda j, k: (k, j)),
                  pl.BlockSpec((S, LANE), lambda j, k: (0, j)), pl.BlockSpec((None, 8, S), lambda j, k: (j, 0, 0))],
        out_specs=[pl.BlockSpec((2 * LANE, S), lambda j, k: (j, 0)), pl.BlockSpec((TK, 2 * LANE), lambda j, k: (k, j)),
                   pl.BlockSpec((TK, LANE), lambda j, k: (k, j))],
        out_shape=[jax.ShapeDtypeStruct((H * LANE, S), F32), jax.ShapeDtypeStruct((S, H * LANE), F32),
                   jax.ShapeDtypeStruct((S, H * VDIM), BF16)],
        compiler_params=pltpu.CompilerParams(dimension_semantics=("parallel", "arbitrary")),
    )(qc, kc, kct, v, do, rows)


_IN_Z, _IN_XBC, _IN_DT, _IN_Q, _IN_KV, _IN_KR = 0, 1024, 2560, 2576, 2960, 3216


PROJ_COLS = 512
SMALL_PAD = pl.cdiv(SMALL_W, PROJ_COLS) * PROJ_COLS


def _prep_in(w_in_t):
    dt = w_in_t.dtype
    return jnp.concatenate(
        [w_in_t[_IN_Q:_IN_KV], w_in_t[_IN_KV:_IN_KR], w_in_t[_IN_KR:IN_WIDTH], jnp.zeros((LANE - ROPE, D), dt),
         w_in_t[_IN_DT:_IN_Q], jnp.zeros((SMALL_PAD - SM_DT - H, D), dt)], axis=0)


def _proj_in(xb, w_in_t, w_small):
    nz, nx, ns = (_IN_XBC - _IN_Z) // PROJ_COLS, (_IN_DT - _IN_XBC) // PROJ_COLS, SMALL_PAD // PROJ_COLS

    dt_block, dt_at = divmod(SM_DT, PROJ_COLS)

    def body(x_ref, w_ref, ws_ref, z_ref, xbc_ref, sm_ref, dtt_ref):
        i = pl.program_id(0)

        def emit(w, o_ref):
            o_ref[...] = lax.dot_general(x_ref[...], w[...], (((1,), (1,)), ((), ())), preferred_element_type=F32)

        pl.when(i < nz)(lambda: emit(w_ref, z_ref))
        pl.when((i >= nz) & (i < nz + nx))(lambda: emit(w_ref, xbc_ref))
        pl.when(i >= nz + nx)(lambda: emit(ws_ref, sm_ref))

        @pl.when(i == nz + nx + dt_block)
        def _():
            dtt_ref[...] = sm_ref[:, dt_at:dt_at + LANE].T

    def blocks(first, count, rows):
        at = lambda i: jnp.clip(i - first, 0, count - 1)
        return pl.BlockSpec((PROJ_COLS, D), lambda i: (at(i), 0)) if rows else pl.BlockSpec((S, PROJ_COLS), lambda i: (0, at(i)))

    return pl.pallas_call(
        body, name="proj_in", grid=(nz + nx + ns,),
        in_specs=[pl.BlockSpec((S, D), lambda i: (0, 0)), blocks(0, nz + nx, True), blocks(nz + nx, ns, True)],
        out_specs=[blocks(0, nz, False), blocks(nz, nx, False), blocks(nz + nx, ns, False), pl.BlockSpec((LANE, S), lambda i: (0, 0))],
        out_shape=[jax.ShapeDtypeStruct((S, _IN_XBC - _IN_Z), F32), jax.ShapeDtypeStruct((S, _IN_DT - _IN_XBC), F32),
                   jax.ShapeDtypeStruct((S, SMALL_W), F32), jax.ShapeDtypeStruct((LANE, S), F32)],
    )(xb, w_in_t, w_small)


PART_COLS = 512


def _part_blocks(widths):
    first = [0]
    for w in widths:
        first.append(first[-1] + w // PART_COLS)

    def at(part):
        return lambda i: jnp.clip(i - first[part], 0, first[part + 1] - first[part] - 1)

    return first, at


def _mm_ta_stacked(parts, b, rows, name):
    n = b.shape[1]
    first, at = _part_blocks([a.shape[1] for a in parts])
    assert first[-1] == pl.cdiv(rows, PART_COLS)

    def body(*refs):
        b_ref, o_ref = refs[-2:]
        i = pl.program_id(0)
        for part, a_ref in enumerate(refs[:-2]):
            @pl.when((i >= first[part]) & (i < first[part + 1]))
            def _(a_ref=a_ref):
                o_ref[...] = lax.dot_general(a_ref[...], b_ref[...], (((0,), (0,)), ((), ())),
                                             preferred_element_type=F32).astype(BF16)

    return pl.pallas_call(
        body, name=name, grid=(first[-1],),
        in_specs=[pl.BlockSpec((S, PART_COLS), lambda i, at=at(part): (0, at(i))) for part in range(len(parts))]
        + [pl.BlockSpec((S, n), lambda i: (0, 0))],
        out_specs=pl.BlockSpec((PART_COLS, n), lambda i: (i, 0)), out_shape=jax.ShapeDtypeStruct((rows, n), BF16),
    )(*parts, b)


def _prep_attn(w_qb, w_kvb):
    w_q = jnp.pad(w_qb.reshape(Q_RANK, H, NOPE + ROPE), ((0, 0), (0, 0), (0, LANE - NOPE - ROPE))).reshape(Q_RANK, H * LANE)
    kv3 = w_kvb.reshape(KV_RANK, H, NOPE + VDIM)
    w_k = jnp.pad(kv3[:, :, :NOPE], ((0, 0), (0, 0), (0, LANE - NOPE))).reshape(KV_RANK, H * LANE)
    w_v = kv3[:, :, NOPE:].reshape(KV_RANK, H * VDIM)
    return w_q, w_k, w_v


def _rope_tables(positions):
    inv_freq = 1.0 / (10000.0 ** (jnp.arange(0, ROPE, 2, dtype=F32) / ROPE))
    ang = positions.astype(F32).reshape(S, 1) * inv_freq
    cos, sin = jnp.cos(ang), jnp.sin(ang)
    cos_t = jnp.concatenate([jnp.ones((S, NOPE), F32), cos, cos, jnp.ones((S, LANE - NOPE - ROPE), F32)], axis=1)
    sin_t = jnp.concatenate([jnp.zeros((S, NOPE), F32), -sin, sin, jnp.zeros((S, LANE - NOPE - ROPE), F32)], axis=1)
    return cos_t, sin_t


def _local_step(x, p, positions, target, w_in, fetch, send, sp, started):
    w_in_t = w_in.reshape(IN_WIDTH, D)
    w_small = _prep_in(w_in_t)
    cos_t, sin_t = _rope_tables(positions)
    prow = jnp.zeros((8, LANE), F32).at[0, :H].set(sp["dt_bias"][0]).at[1, :H].set(sp["A_log"][0]).at[2, :H].set(sp["D"][0])
    pcol = prow.T

    xb, pb = (x + started).astype(BF16), p.astype(BF16)
    z, xbc, small, dt_t = _proj_in(xb, w_in_t, w_small)
    act = _conv_fwd(xbc, sp["conv_w"], sp["conv_b"])
    y, states = _ssd_fwd(act, small, dt_t, prow, pcol)
    y_ssd = _gate_norm_fwd(y, z, sp["ssd_norm"])
    gl = fetch("attn", y_ssd)
    w_q, w_k, w_v = _prep_attn(_from_cols(gl["w_qb"]), _from_cols(gl["w_kvb"]))
    qn, kvn, qcat, kcat, kcat_t, v = _qkv_fwd(small, w_q, w_k, w_v, sp["q_norm"], sp["kv_norm"], cos_t, sin_t)
    o, lse = _attn_fwd(qcat, kcat, v)
    y_mla = _rms_fwd(o, sp["out_norm"], name="out_norm_fwd")
    w_out = fetch("out", y_mla)["w_out"]
    w_out = w_out.reshape(2 * SSD_INNER, D)
    mix, h1, h1b = _out_proj_ln(y_ssd, y_mla, w_out, x, sp["ln_mix_g"], sp["ln_mix_b"])
    gl = fetch("ffn", h1b)
    w_pg, w_pp = gl["w_pg"].reshape(D, D), _from_cols(gl["w_pp"])
    w_gate, w_up, w_down = gl["w_gate"], gl["w_up"], gl["w_down"]
    gate, up, actf = _ffn_hidden_fwd(h1b, w_gate, w_up)
    ffn = _mm([(actf, w_down)], chunk="sum", name="ffn_down")
    dpre2, dpre2b, dpg, dpp, dg2, db2, loss_row = _final_fwd_bwd(h1, ffn, h1b, pb, w_pg, w_pp, target, sp["ln_ffn_g"], sp["ln_ffn_b"])

    g = {"ln_ffn_g": dg2, "ln_ffn_b": db2}
    g["w_pp"] = _to_cols(_mm([(pb, dpp)], ta=True, out_dtype=BF16, name="d_w_ple_proj"))
    g["w_pg"] = _mm([(h1b, dpg)], ta=True, out_dtype=BF16, name="d_w_ple_gate").reshape(NCHIP, D // NCHIP, D)
    g["w_down"] = _mm([(actf, dpre2b)], ta=True, chunk="out", out_dtype=BF16, name="d_w_down")
    dgate, dup = _ffn_hidden_bwd(dpre2b, w_down, gate, up)
    g["w_gate"] = _mm([(dgate, h1b)], ta=True, chunk="out", out_dtype=BF16, name="d_w_gate")
    g["w_up"] = _mm([(dup, h1b)], ta=True, chunk="out", out_dtype=BF16, name="d_w_up")
    sent = send("ffn", {name: g.pop(name) for name in dict(ASYNC_GROUPS)["ffn"]})
    dh1 = _mm([(dgate, w_gate), (dup, w_up), (dpg, w_pg.T)], chunk="sum", add=dpre2, add_scale=ALPHA, name="d_h1")
    dpre1, dpre1b, g["ln_mix_g"], g["ln_mix_b"], dy_ssd, dy_mla = _ln_bwd(x, mix, sp["ln_mix_g"] + sent, dh1, w_out)
    dw_out = _mm_ta_stacked((y_ssd, y_mla), dpre1b, 2 * SSD_INNER, "d_w_out")
    sent = send("out", {"w_out": dw_out.reshape(NCHIP, 2 * SSD_INNER // NCHIP, D)})
    do, g["out_norm"] = _rms_bwd(o, sp["out_norm"] + sent, dy_mla, name="out_norm_bwd")
    dqt, dk, dv = _attn_bwd(qcat, kcat, kcat_t, v, do, _attn_rows(lse, o, do))
    dlatent, dqlin, dkb, g["q_norm"], g["kv_norm"] = _qkv_bwd(dqt, dk, dv, small, w_q, w_k, w_v, sp["q_norm"], sp["kv_norm"], cos_t, sin_t)
    dw_q = _mm([(qn, dqlin)], ta=True, out_dtype=BF16, name="d_w_q")
    dw_k = _mm([(kvn, dkb)], ta=True, out_dtype=BF16, name="d_w_k")
    dw_v = _mm([(kvn, dv)], ta=True, out_dtype=BF16, name="d_w_v")
    dw_qb = _to_cols(dw_q.reshape(Q_RANK, H, LANE)[:, :, :NOPE + ROPE].reshape(Q_RANK, H * (NOPE + ROPE)))
    dw_kvb = _to_cols(jnp.concatenate([dw_k.reshape(KV_RANK, H, LANE)[:, :, :NOPE], dw_v.reshape(KV_RANK, H, VDIM)],
                                       axis=2).reshape(KV_RANK, H * (NOPE + VDIM)))
    sent = send("attn", {"w_qb": dw_qb, "w_kvb": dw_kvb})
    dy, dz, g["ssd_norm"] = _gate_norm_bwd(y, z, sp["ssd_norm"] + sent, dy_ssd)
    dact, ddt, dprow = _ssd_bwd(act, small, dt_t, prow, pcol, states, dy)
    g["dt_bias"], g["A_log"], g["D"] = dprow[0:1, :H], dprow[1:2, :H], dprow[2:3, :H]
    dxbc, g["conv_w"], g["conv_b"] = _conv_bwd(xbc, sp["conv_w"], sp["conv_b"], dact)
    dsmall = jnp.concatenate([dlatent, ddt.astype(BF16)], axis=1)
    in_blocks = [(d, w_in_t, (k, first // PROJ_COLS + k, PROJ_COLS))
                 for d, first in ((dz, _IN_Z), (dxbc, _IN_XBC)) for k in range(d.shape[1] // PROJ_COLS)]
    grad_x = _mm(in_blocks + [(dsmall, w_small, (0, 0, SMALL_W))], add=dpre1, add_scale=ALPHA, name="d_x")
    sent = send("small", dict(g, loss=loss_row))
    n_small = IN_WIDTH - _IN_DT
    dsm = jnp.concatenate([(ddt[:, :H] + sent).astype(BF16), dlatent[:, :n_small - H], jnp.zeros((S, D - n_small), BF16)], axis=1)
    dw_in = _mm_ta_stacked((dz, dxbc, dsm), xb, IN_WIDTH, "d_w_in").reshape(NCHIP, IN_WIDTH // NCHIP * D // LANE, LANE)
    return loss_row, grad_x, dw_in, g


MESH = pl.DeviceIdType.MESH
BIG = (("w_in", (D, IN_WIDTH), 1), ("w_qb", (Q_RANK, H * (NOPE + ROPE)), 1), ("w_kvb", (KV_RANK, H * (NOPE + VDIM)), 1),
       ("w_out", (2 * SSD_INNER, D), 0), ("w_gate", (D, D_FF), 1), ("w_up", (D, D_FF), 1), ("w_down", (D_FF, D), 0),
       ("w_pg", (D, D), 0), ("w_pp", (PLE, D), 1))
CONV_SHARD = SSD_XBC // NCHIP
BF16_ROWS = 16


def _from_cols(stack):
    return jnp.concatenate([stack[k] for k in range(NCHIP)], axis=1)


def _to_cols(full):
    r, c4 = full.shape
    return full.reshape(r, NCHIP, c4 // NCHIP).transpose(1, 0, 2)


def _coords():
    return lax.axis_index("x"), lax.axis_index("y"), lax.axis_index("c")


def _peers():
    x, y, c = _coords()
    return 2 * x + y, c, [(1 - x, y), (x, 1 - y), (1 - x, 1 - y)], (x, y, 1 - c)


def _half_axis(shape):
    return 0 if shape[-2] % (2 * BF16_ROWS) == 0 else 1


def _half_shape(shape):
    r, c = shape[-2:]
    return (r // 2, c) if _half_axis(shape) == 0 else (r, c // 2)


def _half(core, shape):
    r, c = shape[-2:]
    if _half_axis(shape) == 0:
        return pl.ds(pl.multiple_of(core * (r // 2), BF16_ROWS), r // 2), slice(None)
    return slice(None), pl.ds(pl.multiple_of(core * (c // 2), LANE), c // 2)


def _gather_weights(shards):
    n_arr = len(shards)
    per = 2 * (NCHIP - 1)

    def body(*refs):
        ins, outs = refs[:n_arr], refs[n_arr:2 * n_arr]
        send_sems, recv_sems, local_sems = refs[2 * n_arr:]
        k, c, chips, sibling = _peers()

        def copy(idx, src, dst, to):
            return pltpu.make_async_remote_copy(src_ref=src, dst_ref=dst, send_sem=send_sems.at[idx], recv_sem=recv_sems.at[idx],
                                                device_id=to, device_id_type=MESH)

        def part(a, chip, core):
            return outs[a].at[chip, *_half(core, shards[a].shape)]

        mine = [pltpu.make_async_copy(ins[a], outs[a].at[k], local_sems.at[a]) for a in range(n_arr)]
        for cp in mine:
            cp.start()
        sends = []
        for a in range(n_arr):
            for j, (cx, cy) in enumerate(chips):
                sends.append(copy(per * a + j, ins[a].at[*_half(c, shards[a].shape)], part(a, k, c), (cx, cy, c)))
                sends[-1].start()
        for j, (cx, cy) in enumerate(chips):
            for a in range(n_arr):
                landed = part(a, 2 * cx + cy, c)
                copy(per * a + j, landed, landed, (cx, cy, c)).wait_recv()
                sends.append(copy(per * a + NCHIP - 1 + j, landed, landed, sibling))
                sends[-1].start()
        for j, (cx, cy) in enumerate(chips):
            for a in range(n_arr):
                other = part(a, 2 * cx + cy, 1 - c)
                copy(per * a + NCHIP - 1 + j, other, other, sibling).wait_recv()
        for cp in sends:
            cp.wait_send()
        for cp in mine:
            cp.wait()

    any_spec = pl.BlockSpec(memory_space=pl.ANY)
    return pl.pallas_call(
        body, name="gather_weights", in_specs=[any_spec] * n_arr, out_specs=[any_spec] * n_arr,
        out_shape=[jax.ShapeDtypeStruct((NCHIP,) + s.shape, s.dtype) for s in shards],
        scratch_shapes=[pltpu.SemaphoreType.DMA((per * n_arr,)), pltpu.SemaphoreType.DMA((per * n_arr,)),
                        pltpu.SemaphoreType.DMA((n_arr,))],
    )(*shards)


ASYNC_GROUPS = (("attn", ("w_qb", "w_kvb")), ("out", ("w_out",)), ("ffn", ("w_gate", "w_up", "w_down", "w_pg", "w_pp")))
TRANSPOSED = ("w_in", "w_gate", "w_up")
ROW_MAJOR = ("w_in",)
HBM_SPEC = pl.BlockSpec(memory_space=pltpu.HBM)
SEM_SPEC = pl.BlockSpec(memory_space=pltpu.SEMAPHORE)
IN_FLIGHT = pltpu.SideEffectType.DATAFLOW_SIDE_EFFECTING


def _in_hbm(a):
    return pltpu.with_memory_space_constraint(a, pltpu.HBM)


def _hbm_like(arrs, lead=()):
    return [pltpu.HBM(lead + a.shape, a.dtype) for a in arrs]


def _split_start(name, srcs, lands, after, n_sem, start):
    n = len(srcs)
    order = [] if after is None else [after]

    def body(*refs):
        src_refs, land_refs = refs[:n], refs[n:2 * n]
        send_sems, recv_sems = refs[2 * n + len(order)], refs[2 * n + len(order) + 1]
        token = refs[-1]

        def copy(send_idx, recv_idx, src, dst, to):
            return pltpu.make_async_remote_copy(src_ref=src, dst_ref=dst, send_sem=send_sems.at[send_idx],
                                                recv_sem=recv_sems.at[recv_idx], device_id=to, device_id_type=MESH)

        for cp in start(src_refs, land_refs, copy):
            cp.start()
        token[...] = jnp.zeros_like(token)

    sem = pltpu.SemaphoreType.DMA((n_sem,))
    outs = pl.pallas_call(
        body, name=name, in_specs=[HBM_SPEC] * (2 * n) + [pl.BlockSpec(memory_space=pl.ANY)] * len(order),
        out_specs=[SEM_SPEC, SEM_SPEC] + [HBM_SPEC] * (2 * n) + [pl.BlockSpec(memory_space=pltpu.VMEM)],
        out_shape=[sem, sem] + _hbm_like(srcs) + _hbm_like(lands) + [jax.ShapeDtypeStruct((8, LANE), F32)],
        input_output_aliases={i: 2 + i for i in range(2 * n)},
        compiler_params=pltpu.CompilerParams(has_side_effects=IN_FLIGHT),
    )(*[_in_hbm(a) for a in srcs], *[_in_hbm(a) for a in lands], *order)
    return (outs[0], outs[1], outs[2:2 + n], outs[2 + n:2 + 2 * n]), outs[-1]


def _split_wait(name, send_sems, recv_sems, srcs, lands, after, waits):
    n = len(srcs)

    def body(*refs):
        src_refs, land_refs = refs[:n], refs[n:2 * n]
        send_ref, recv_ref = refs[2 * n], refs[2 * n + 1]

        def copy(send_idx, recv_idx, src, dst, to):
            return pltpu.make_async_remote_copy(src_ref=src, dst_ref=dst, send_sem=send_ref.at[send_idx],
                                                recv_sem=recv_ref.at[recv_idx], device_id=to, device_id_type=MESH)

        for cp in waits(src_refs, land_refs, copy):
            cp.wait_send()
            cp.wait_recv()

    outs = pl.pallas_call(
        body, name=name, in_specs=[HBM_SPEC] * (2 * n) + [SEM_SPEC, SEM_SPEC, pl.BlockSpec(memory_space=pl.ANY)],
        out_specs=[HBM_SPEC] * (2 * n), out_shape=_hbm_like(srcs) + _hbm_like(lands),
        input_output_aliases={i: i for i in range(2 * n)},
        compiler_params=pltpu.CompilerParams(has_side_effects=IN_FLIGHT),
    )(*srcs, *lands, send_sems, recv_sems, after)
    return outs[:n], outs[n:]


GATHER_LATE_SEMS = 2 * (NCHIP - 1)


def _gather_async_start(tag, shards, after):
    def start(srcs, lands, copy):
        k, c, chips, _ = _peers()
        out = []
        for a, (src, dst) in enumerate(zip(srcs, lands)):
            for j, (cx, cy) in enumerate(chips):
                for core in range(2):
                    out.append(copy(GATHER_LATE_SEMS * a + 2 * j + core, GATHER_LATE_SEMS * a + 2 * j + c,
                                    src.at[*_half(c, src.shape)], dst.at[k, *_half(c, src.shape)], (cx, cy, core)))
        return out

    chip = 2 * lax.axis_index("x") + lax.axis_index("y")
    lands = [lax.dynamic_update_slice(lax.empty((NCHIP,) + s.shape, s.dtype), s[None], (chip, 0, 0)) for s in shards]
    return _split_start("gather_%s_start" % tag, shards, lands, after, GATHER_LATE_SEMS * len(shards), start)


def _gather_async_wait(tag, send_sems, recv_sems, shards, lands, after, first=0):
    def waits(srcs, lands_, copy):
        _, c, chips, _ = _peers()
        out = []
        for a, (src, dst) in enumerate(zip(srcs, lands_)):
            for j, (cx, cy) in enumerate(chips):
                for core in range(2):
                    idx = GATHER_LATE_SEMS * (first + a) + 2 * j + core
                    out.append(copy(idx, idx, src.at[*_half(c, src.shape)], dst.at[2 * cx + cy, *_half(core, src.shape)], (cx, cy, core)))
        return out

    return _split_wait("gather_%s_wait" % tag, send_sems, recv_sems, shards, lands, after, waits)[1]


def _other_devices():
    x, y, c = _coords()
    out = []
    for d in range(1, NDEV):
        tx, ty, tc = x ^ (d >> 2), y ^ ((d >> 1) & 1), c ^ (d & 1)
        out.append((d, (tx, ty, tc), 2 * tx + ty, 4 * tx + 2 * ty + tc))
    return out


def _reduce_async_start(tag, stacks, after):
    def start(srcs, lands, copy):
        x, y, c = _coords()
        me = 4 * x + 2 * y + c
        return [copy((NDEV - 1) * a + d - 1, (NDEV - 1) * a + d - 1, src.at[chip, *_half(to[2], src.shape)], dst.at[me], to)
                for a, (src, dst) in enumerate(zip(srcs, lands)) for d, to, chip, _ in _other_devices()]

    x, y, c = _coords()
    lands = []
    for s in stacks:
        hr, hc = _half_shape(s.shape)
        at = (c * hr, 0) if _half_axis(s.shape) == 0 else (0, c * hc)
        own = lax.dynamic_slice(s, (2 * x + y,) + at, (1, hr, hc))
        lands.append(lax.dynamic_update_slice(lax.empty((NDEV, hr, hc), s.dtype), own, (4 * x + 2 * y + c, 0, 0)))
    return _split_start("reduce_%s_start" % tag, stacks, lands, after, (NDEV - 1) * len(stacks), start)


def _reduce_async_wait(tag, send_sems, recv_sems, stacks, lands, after):
    def waits(srcs, lands_, copy):
        return [copy((NDEV - 1) * a + d - 1, (NDEV - 1) * a + d - 1, src.at[chip, *_half(to[2], src.shape)], dst.at[pos], to)
                for a, (src, dst) in enumerate(zip(srcs, lands_)) for d, to, chip, pos in _other_devices()]

    return _split_wait("reduce_%s_wait" % tag, send_sems, recv_sems, stacks, lands, after, waits)[1]


def _reduce_finish(tag, arrived, dims):
    n_arr = len(arrived)

    def body(*refs):
        lands, fin = refs[:n_arr], refs[n_arr:2 * n_arr]
        send_sems, recv_sems = refs[2 * n_arr:]
        _, c, _, sibling = _peers()
        sends = []
        for a in range(n_arr):
            mine = fin[a].at[*_half(c, dims[a])]

            def device_sum(vs, vf, a=a, mine=mine):
                pltpu.sync_copy(lands[a], vs)
                acc = vs[0].astype(F32)
                for i in range(1, NDEV):
                    acc = acc + vs[i].astype(F32)
                vf[...] = acc
                pltpu.sync_copy(vf, mine)

            pl.run_scoped(device_sum, pltpu.VMEM((NDEV,) + _half_shape(dims[a]), BF16), pltpu.VMEM(_half_shape(dims[a]), F32))
            sends.append(pltpu.make_async_remote_copy(src_ref=mine, dst_ref=mine, send_sem=send_sems.at[a], recv_sem=recv_sems.at[a],
                                                      device_id=sibling, device_id_type=MESH))
            sends[-1].start()
        for a in range(n_arr):
            other = fin[a].at[*_half(1 - c, dims[a])]
            pltpu.make_async_remote_copy(src_ref=other, dst_ref=other, send_sem=send_sems.at[a], recv_sem=recv_sems.at[a],
                                         device_id=sibling, device_id_type=MESH).wait_recv()
        for cp in sends:
            cp.wait_send()

    any_spec = pl.BlockSpec(memory_space=pl.ANY)
    return pl.pallas_call(
        body, name="reduce_%s_finish" % tag, in_specs=[any_spec] * n_arr, out_specs=[any_spec] * n_arr,
        out_shape=[jax.ShapeDtypeStruct(d, F32) for d in dims],
        scratch_shapes=[pltpu.SemaphoreType.DMA((n_arr,)), pltpu.SemaphoreType.DMA((n_arr,))],
    )(*arrived)


SMALL = (("conv_w", SSD_K * SSD_XBC), ("conv_b", SSD_XBC), ("dt_bias", H), ("A_log", H), ("D", H), ("ssd_norm", SSD_INNER),
         ("q_norm", Q_RANK), ("kv_norm", KV_RANK), ("out_norm", SSD_INNER), ("ln_mix_g", D), ("ln_mix_b", D),
         ("ln_ffn_g", D), ("ln_ffn_b", D))
SMALL_ROWS = 120
NDEV = 8


def _allreduce_small_start(sv):
    def start(srcs, lands, copy):
        x, y, c = _coords()
        return [copy(d - 1, d - 1, srcs[0], lands[0].at[4 * x + 2 * y + c], to) for d, to, _, _ in _other_devices()]

    x, y, c = _coords()
    slots = lax.dynamic_update_slice(lax.empty((NDEV,) + sv.shape, sv.dtype), sv[None], (4 * x + 2 * y + c, 0, 0))
    return _split_start("allreduce_small_start", [sv], [slots], None, NDEV - 1, start)


def _allreduce_small_wait(send_sems, recv_sems, srcs, lands, after):
    def waits(srcs_, lands_, copy):
        return [copy(d - 1, d - 1, srcs_[0], lands_[0].at[pos], to) for d, to, _, pos in _other_devices()]

    def device_sum(slots_ref, out_ref):
        acc = slots_ref[0]
        for i in range(1, NDEV):
            acc = acc + slots_ref[i]
        out_ref[...] = acc

    slots = _split_wait("allreduce_small_wait", send_sems, recv_sems, srcs, lands, after, waits)[1][0]
    vm = pl.BlockSpec(memory_space=pltpu.VMEM)
    return pl.pallas_call(device_sum, name="allreduce_small_sum", in_specs=[vm], out_specs=vm,
                          out_shape=jax.ShapeDtypeStruct(slots.shape[1:], slots.dtype))(slots)


def _adamw_math(w, g, m, v):
    m2 = ADAM_B1 * m + (1.0 - ADAM_B1) * g
    v2 = ADAM_B2 * v + (1.0 - ADAM_B2) * (g * g)
    m_hat = m2 / (1.0 - ADAM_B1 ** ADAM_STEP)
    v_hat = v2 / (1.0 - ADAM_B2 ** ADAM_STEP)
    return -ADAM_LR * (m_hat / (jnp.sqrt(v_hat) + ADAM_EPS) + ADAM_WD * w), m2, v2


ADAM_BLOCK_BYTES = 2 * 1024 * 1024


def _adamw_big(w, g, m, v, *, name):
    r, c = w.shape

    def body(w_ref, g_ref, m_ref, v_ref, d_ref, m2_ref, v2_ref):
        d_ref[...], m2_ref[...], v2_ref[...] = _adamw_math(w_ref[...], g_ref[...], m_ref[...], v_ref[...])

    tr = max(t for t in range(8, r + 1, 8) if r % t == 0 and t * c * 4 <= ADAM_BLOCK_BYTES)
    steps, spec = r // tr, pl.BlockSpec((tr, c), lambda i: (i, 0))
    return pl.pallas_call(body, name=name, grid=(steps,), in_specs=[spec] * 4, out_specs=[spec] * 3,
                          out_shape=[jax.ShapeDtypeStruct((r, c), F32)] * 3)(w, g, m, v)


def _adamw_small(ws, gs, ms, vs):
    n = len(ws)

    def body(*refs):
        for i in range(n):
            w_ref, g_ref, m_ref, v_ref = (refs[j * n + i] for j in range(4))
            d_ref, m2_ref, v2_ref = (refs[(4 + j) * n + i] for j in range(3))
            d_ref[...], m2_ref[...], v2_ref[...] = _adamw_math(w_ref[...], g_ref[...], m_ref[...], v_ref[...])

    vm = pl.BlockSpec(memory_space=pltpu.VMEM)
    shapes = [jax.ShapeDtypeStruct(w.shape, F32) for w in ws]
    outs = pl.pallas_call(body, name="adamw_small", in_specs=[vm] * (4 * n), out_specs=[vm] * (3 * n), out_shape=shapes * 3)(
        *ws, *gs, *ms, *vs)
    return outs[:n], outs[n:2 * n], outs[2 * n:]


_SMALL_ARG = {"conv_w": "ssd_conv_w", "conv_b": "ssd_conv_b", "dt_bias": "ssd_dt_bias", "A_log": "ssd_A_log", "D": "ssd_D",
              "ssd_norm": "ssd_norm_w", "q_norm": "mla_q_norm_w", "kv_norm": "mla_kv_norm_w", "out_norm": "mla_out_norm_w",
              "ln_mix_g": "ln_mix_g", "ln_mix_b": "ln_mix_b", "ln_ffn_g": "ln_ffn_g", "ln_ffn_b": "ln_ffn_b"}
_BIG_ARG = {"w_in": "w_in", "w_qb": "mla_w_q_b", "w_kvb": "mla_w_kv_b", "w_out": "w_out", "w_gate": "w_ffn_gate",
            "w_up": "w_ffn_up", "w_down": "w_ffn_down", "w_pg": "w_ple_gate", "w_pp": "w_ple_proj"}
_WEIGHT_ORDER = ("w_in", "ssd_conv_w", "ssd_conv_b", "ssd_dt_bias", "ssd_A_log", "ssd_D", "ssd_norm_w", "mla_q_norm_w", "mla_w_q_b",
                 "mla_kv_norm_w", "mla_w_kv_b", "mla_out_norm_w", "w_out", "ln_mix_g", "ln_mix_b", "w_ffn_gate", "w_ffn_up",
                 "w_ffn_down", "w_ple_gate", "w_ple_proj", "ln_ffn_g", "ln_ffn_b")


def _rows128(a):
    flat = a.reshape(-1)
    return jnp.pad(flat, (0, -flat.shape[0] % LANE)).reshape(-1, LANE)


def kernel(x, p, positions, w_in, ssd_conv_w, ssd_conv_b, ssd_dt_bias, ssd_A_log, ssd_D, ssd_norm_w, mla_q_norm_w, mla_w_q_b, mla_kv_norm_w, mla_w_kv_b, mla_out_norm_w, w_out, ln_mix_g, ln_mix_b, w_ffn_gate, w_ffn_up, w_ffn_down, w_ple_gate, w_ple_proj, ln_ffn_g, ln_ffn_b, loss_target, m_w_in, m_ssd_conv_w, m_ssd_conv_b, m_ssd_dt_bias, m_ssd_A_log, m_ssd_D, m_ssd_norm_w, m_mla_q_norm_w, m_mla_w_q_b, m_mla_kv_norm_w, m_mla_w_kv_b, m_mla_out_norm_w, m_w_out, m_ln_mix_g, m_ln_mix_b, m_w_ffn_gate, m_w_ffn_up, m_w_ffn_down, m_w_ple_gate, m_w_ple_proj, m_ln_ffn_g, m_ln_ffn_b, v_w_in, v_ssd_conv_w, v_ssd_conv_b, v_ssd_dt_bias, v_ssd_A_log, v_ssd_D, v_ssd_norm_w, v_mla_q_norm_w, v_mla_w_q_b, v_mla_kv_norm_w, v_mla_w_kv_b, v_mla_out_norm_w, v_w_out, v_ln_mix_g, v_ln_mix_b, v_w_ffn_gate, v_w_ffn_up, v_w_ffn_down, v_w_ple_gate, v_w_ple_proj, v_ln_ffn_g, v_ln_ffn_b):
    given = dict(locals())
    chip = 2 * lax.axis_index("x") + lax.axis_index("y")

    def local(name, prefix=""):
        a = given[prefix + _BIG_ARG[name]][0]
        return a.T if name in TRANSPOSED else a

    def updated(name, prefix=""):
        if name in ROW_MAJOR:
            _, c, r = given[prefix + _BIG_ARG[name]].shape
            return given[prefix + _BIG_ARG[name]].reshape(c // LANE, LANE, r).transpose(2, 0, 1).reshape(-1, LANE)
        return local(name, prefix)

    def global_layout(name, arr):
        if name in ROW_MAJOR:
            r, c = local(name).shape
            return arr.reshape(r, c // LANE, LANE).transpose(1, 2, 0).reshape(1, c, r)
        return (arr.T if name in TRANSPOSED else arr)[None]

    conv_bits = lax.bitcast_convert_type(ssd_conv_w[0], BF16).reshape(SSD_K, 2 * CONV_SHARD)
    w_in_all, conv_all = _gather_weights([local("w_in").astype(BF16), jnp.pad(conv_bits, ((0, BF16_ROWS - SSD_K), (0, 0)))])
    sp = {k: given[a] for k, a in _SMALL_ARG.items() if k != "conv_w"}
    sp["conv_w"] = _from_cols(lax.bitcast_convert_type(conv_all[:, :SSD_K].reshape(NCHIP, SSD_K, CONV_SHARD, 2), F32))
    late = [name for _, names in ASYNC_GROUPS for name in names]
    (late_send, late_recv, late_shards, late_lands), tie = _gather_async_start(
        "late", [local(name).astype(BF16) for name in late], w_in_all)

    def fetch(group, after):
        names = dict(ASYNC_GROUPS)[group]
        first = late.index(names[0])
        mine = slice(first, first + len(names))
        return dict(zip(names, _gather_async_wait(group, late_send, late_recv, late_shards[mine], late_lands[mine], after, first)))

    reducing = {}

    def send(group, grads):
        if group == "small":
            rows = jnp.concatenate([_rows128(grads[name]) for name, _ in SMALL] + [grads["loss"]], axis=0)
            reducing[group], sent = _allreduce_small_start(jnp.pad(rows, ((0, SMALL_ROWS - rows.shape[0]), (0, 0))))
        else:
            reducing[group], sent = _reduce_async_start(group, [grads[name] for name in dict(ASYNC_GROUPS)[group]], None)
        return sent[0, 0]

    loss_row, grad_x, dw_in, g = _local_step(x[0], p[0, 0], positions[0], loss_target[0], w_in_all, fetch, send, sp, tie[0, 0])

    reducing["in"], tie = _reduce_async_start("in", [dw_in], grad_x)
    gbig = {}
    for group, names in reversed(ASYNC_GROUPS):
        arrived = _reduce_async_wait(group, *reducing[group], tie)
        gbig.update(zip(names, _reduce_finish(group, arrived, [local(name).shape for name in names])))
    small_sum = _allreduce_small_wait(*reducing.pop("small"), tie)
    gsmall, row = {}, 0
    for name, size in SMALL:
        nrow = -(-size // LANE)
        gsmall[name] = small_sum[row:row + nrow].reshape(-1)[:size]
        row += nrow
    loss = small_sum[row, 0]

    grads = {_BIG_ARG[name]: global_layout(name, arr) for name, arr in gbig.items()}
    for name, _ in SMALL:
        if name == "conv_w":
            full_g = gsmall[name].reshape(SSD_K, SSD_XBC)
            grads["ssd_conv_w"] = lax.dynamic_slice(full_g, (0, chip * CONV_SHARD), (SSD_K, CONV_SHARD))[None]
        else:
            grads[_SMALL_ARG[name]] = gsmall[name].reshape(given[_SMALL_ARG[name]].shape)

    delta, new_m, new_v = {}, {}, {}

    def update_matrix(name, grad):
        a = _BIG_ARG[name]
        d, m2, v2 = _adamw_big(updated(name), grad, updated(name, "m_"), updated(name, "v_"), name="adamw_" + a)
        delta[a], new_m[a], new_v[a] = (global_layout(name, t) for t in (d, m2, v2))
        return d

    all_updated = sum(update_matrix(name, grad)[:8, :LANE] for name, grad in gbig.items())
    g_in = _reduce_finish("in", _reduce_async_wait("in", *reducing["in"], all_updated), [updated("w_in").shape])[0]
    grads["w_in"] = global_layout("w_in", g_in)
    update_matrix("w_in", g_in)
    small_names = [_SMALL_ARG[name] for name, _ in SMALL]
    two_d = lambda t: t.reshape(t.shape[-2], t.shape[-1])
    ds, ms, vs = _adamw_small([two_d(given[a]) for a in small_names], [two_d(grads[a]) for a in small_names],
                              [two_d(given["m_" + a]) for a in small_names], [two_d(given["v_" + a]) for a in small_names])
    for a, d, m2, v2 in zip(small_names, ds, ms, vs):
        delta[a], new_m[a], new_v[a] = (t.reshape(given[a].shape) for t in (d, m2, v2))

    return (loss, grad_x[None], *[grads[n] for n in _WEIGHT_ORDER], *[delta[n] for n in _WEIGHT_ORDER],
            *[new_m[n] for n in _WEIGHT_ORDER], *[new_v[n] for n in _WEIGHT_ORDER])
```

```python
import functools
import math

import jax
import jax.numpy as jnp
from jax import lax
from jax.experimental import pallas as pl
from jax.experimental.pallas import tpu as pltpu

F32 = jnp.float32
BF16 = jnp.bfloat16

S = 2048
D = 1024
PLE = 256
H = 16
SSD_P = 64
SSD_INNER = 1024
SSD_N = 128
SSD_G = 2
SSD_L = 128
SSD_NC = S // SSD_L
SSD_XBC = 1536
SSD_K = 4
Q_RANK = 384
KV_RANK = 256
NOPE = 64
ROPE = 32
VDIM = 64
D_FF = 2816
IN_WIDTH = 3248
ALPHA = 2.0 ** 0.25
EPS_RMS = 1e-6
EPS_LN = 1e-5
ATT_SCALE = 1.0 / math.sqrt(NOPE + ROPE)
LN2 = math.log(2.0)
ATT_SCALE_LOG2 = ATT_SCALE / LN2
LANE = 128
NCHIP = 4
SMALL_W = 896
SM_Q, SM_KV, SM_KR, SM_DT = 0, 384, 640, 768
NEG = -1e30

ADAM_LR = 0.001
ADAM_B1 = 0.9
ADAM_B2 = 0.999
ADAM_EPS = 1e-08
ADAM_WD = 0.01
ADAM_STEP = 10


def _sigmoid(v):
    return 1.0 / (1.0 + jnp.exp(-v))


MM_VMEM_BUDGET = 36 * 2 ** 20
MM_MAX_ACC = 2048 * 1024


def _mm_tiles(pairs, ks, m, n, out_dtype, has_add):
    def divs(v):
        return [LANE * d for d in range(v // LANE, 0, -1) if (v // LANE) % d == 0] if v % LANE == 0 else [v]

    def cost(tm, tn):
        tot = tm * tn * (jnp.dtype(out_dtype).itemsize + (4 if has_add else 0))
        for (a, b), k in zip(pairs, ks):
            tot += k * (tm * a.dtype.itemsize + tn * b.dtype.itemsize)
        return 2 * tot

    ok = [(tm * tn, tm, tn) for tm in divs(m) for tn in divs(n) if tm * tn <= MM_MAX_ACC and cost(tm, tn) <= MM_VMEM_BUDGET]
    _, tm, tn = max(ok)
    return tm, tn


def _mm(pairs, *, ta=False, tb=False, out_dtype=F32, add=None, add_scale=1.0, chunk=None, name):
    n_pairs = len(pairs)
    windows = [pr[2] if len(pr) == 3 else None for pr in pairs]
    pairs = [pr[:2] for pr in pairs]
    assert not ((ta or tb) and any(windows))
    ks = [w[2] if w else (a.shape[-2] if ta else a.shape[-1]) for (a, _), w in zip(pairs, windows)]
    a0, b0 = pairs[0]
    m = a0.shape[-1] if ta else a0.shape[-2]
    n = b0.shape[-2] if tb else b0.shape[-1]
    tm, tn = _mm_tiles(pairs, ks, m, n, out_dtype, add is not None)
    dims = (((0 if ta else 1,), (1 if tb else 0,)), ((), ()))
    nk = NCHIP if chunk else 1
    assert chunk != "sum" or out_dtype == F32
    flat = [i for i, (a, b) in enumerate(pairs) if a.ndim == 2 and b.ndim == 2]
    stacked = [i for i in range(n_pairs) if i not in flat]

    def body(*refs):
        o_ref = refs[-1]

        def products(which):
            acc = None
            for i in which:
                a = refs[2 * i][...].astype(BF16)
                b = refs[2 * i + 1][...].astype(BF16)
                part = lax.dot_general(a, b, dims, preferred_element_type=F32)
                acc = part if acc is None else acc + part
            return acc

        if chunk == "sum":
            k = pl.program_id(2)
            acc = products(stacked)

            @pl.when(k == 0)
            def _():
                first = acc + products(flat) if flat else acc
                o_ref[...] = first + add_scale * refs[2 * n_pairs][...] if add is not None else first

            @pl.when(k > 0)
            def _():
                o_ref[...] += acc
            return
        acc = products(range(n_pairs))
        if add is not None:
            acc = acc + add_scale * refs[2 * n_pairs][...]
        o_ref[...] = acc.astype(out_dtype)

    def spec(arr, shape, idx2):
        if arr.ndim == 3:
            return pl.BlockSpec((None,) + shape, lambda i, j, k: (k,) + idx2(i, j))
        return pl.BlockSpec(shape, lambda i, j, k: idx2(i, j))

    in_specs, args = [], []
    for (a, b), kdim, window in zip(pairs, ks, windows):
        ka, kb = window[:2] if window else (0, 0)
        in_specs.append(spec(a, (kdim, tm), lambda i, j: (0, i)) if ta else spec(a, (tm, kdim), lambda i, j, ka=ka: (i, ka)))
        in_specs.append(spec(b, (tn, kdim), lambda i, j: (j, 0)) if tb else spec(b, (kdim, tn), lambda i, j, kb=kb: (kb, j)))
        args += [a, b]
    if add is not None:
        in_specs.append(pl.BlockSpec((tm, tn), lambda i, j, k: (i, j)))
        args.append(add)
    if chunk == "out":
        out_spec = pl.BlockSpec((None, tm, tn), lambda i, j, k: (k, i, j))
        out_shape = jax.ShapeDtypeStruct((nk, m, n), out_dtype)
    else:
        out_spec = pl.BlockSpec((tm, tn), lambda i, j, k: (i, j))
        out_shape = jax.ShapeDtypeStruct((m, n), out_dtype)
    return pl.pallas_call(
        body, name=name, grid=(m // tm, n // tn, nk), in_specs=in_specs, out_specs=out_spec, out_shape=out_shape,
        compiler_params=pltpu.CompilerParams(dimension_semantics=("parallel", "parallel", "arbitrary")),
    )(*args)


TR = 256


def _row_spec(c):
    return pl.BlockSpec((TR, c), lambda i: (i, 0))


def _vec_spec(c):
    return pl.BlockSpec((1, c), lambda i: (0, 0))


def _acc_rows(ref, val):
    @pl.when(pl.program_id(0) == 0)
    def _():
        ref[...] = jnp.zeros_like(ref)
    ref[...] += val


def _rms_fwd(u, w, *, name):
    c = u.shape[1]

    def body(u_ref, w_ref, o_ref):
        v = u_ref[...]
        r = lax.rsqrt(jnp.mean(v * v, axis=-1, keepdims=True) + EPS_RMS)
        o_ref[...] = (v * r * w_ref[...]).astype(BF16)

    return pl.pallas_call(body, name=name, grid=(S // TR,), in_specs=[_row_spec(c), _vec_spec(c)], out_specs=_row_spec(c),
                          out_shape=jax.ShapeDtypeStruct((S, c), BF16))(u, w)


def _rms_bwd(u, w, dy, *, name):
    c = u.shape[1]

    def body(u_ref, w_ref, dy_ref, du_ref, dw_ref):
        v = u_ref[...]
        g = dy_ref[...].astype(F32)
        r = lax.rsqrt(jnp.mean(v * v, axis=-1, keepdims=True) + EPS_RMS)
        gw = g * w_ref[...]
        du_ref[...] = r * gw - v * (r * r * r * jnp.mean(gw * v, axis=-1, keepdims=True))
        _acc_rows(dw_ref, jnp.sum(g * v * r, axis=0, keepdims=True))

    return pl.pallas_call(body, name=name, grid=(S // TR,), in_specs=[_row_spec(c), _vec_spec(c), _row_spec(c)],
                          out_specs=[_row_spec(c), _vec_spec(c)],
                          out_shape=[jax.ShapeDtypeStruct((S, c), F32), jax.ShapeDtypeStruct((1, c), F32)])(u, w, dy)


def _gate_norm_fwd(y, z, w):
    def body(y_ref, z_ref, w_ref, o_ref):
        zz = z_ref[...]
        v = y_ref[...] * (zz * _sigmoid(zz))
        r = lax.rsqrt(jnp.mean(v * v, axis=-1, keepdims=True) + EPS_RMS)
        o_ref[...] = (v * r * w_ref[...]).astype(BF16)

    c = SSD_INNER
    return pl.pallas_call(body, name="ssd_gate_norm_fwd", grid=(S // TR,), in_specs=[_row_spec(c), _row_spec(c), _vec_spec(c)],
                          out_specs=_row_spec(c), out_shape=jax.ShapeDtypeStruct((S, c), BF16))(y, z, w)


def _gate_norm_bwd(y, z, w, dout):
    def body(y_ref, z_ref, w_ref, g_ref, dy_ref, dz_ref, dw_ref):
        yy = y_ref[...]
        zz = z_ref[...]
        sg = _sigmoid(zz)
        sz = zz * sg
        v = yy * sz
        g = g_ref[...]
        r = lax.rsqrt(jnp.mean(v * v, axis=-1, keepdims=True) + EPS_RMS)
        gw = g * w_ref[...]
        dv = r * gw - v * (r * r * r * jnp.mean(gw * v, axis=-1, keepdims=True))
        dy_ref[...] = dv * sz
        dz_ref[...] = (dv * yy * (sg * (1.0 + zz * (1.0 - sg)))).astype(BF16)
        _acc_rows(dw_ref, jnp.sum(g * v * r, axis=0, keepdims=True))

    c = SSD_INNER
    return pl.pallas_call(body, name="ssd_gate_norm_bwd", grid=(S // TR,),
                          in_specs=[_row_spec(c), _row_spec(c), _vec_spec(c), _row_spec(c)],
                          out_specs=[_row_spec(c), _row_spec(c), _vec_spec(c)],
                          out_shape=[jax.ShapeDtypeStruct((S, c), F32), jax.ShapeDtypeStruct((S, c), BF16),
                                     jax.ShapeDtypeStruct((1, c), F32)])(y, z, w, dout)


MIX_ROWS = 512


def _out_proj_ln(y_ssd, y_mla, w_out, xr, g, b):
    k = y_ssd.shape[1]

    def body(ys_ref, ym_ref, w_ref, x_ref, g_ref, b_ref, m_ref, o_ref, ob_ref):
        mix = (jnp.dot(ys_ref[...], w_ref[:k], preferred_element_type=F32)
               + jnp.dot(ym_ref[...], w_ref[k:], preferred_element_type=F32))
        m_ref[...] = mix
        pre = ALPHA * x_ref[...] + mix
        mu = jnp.mean(pre, axis=-1, keepdims=True)
        d = pre - mu
        rs = lax.rsqrt(jnp.mean(d * d, axis=-1, keepdims=True) + EPS_LN)
        h = d * rs * g_ref[...] + b_ref[...]
        o_ref[...] = h
        ob_ref[...] = h.astype(BF16)

    rows = lambda c: pl.BlockSpec((MIX_ROWS, c), lambda i: (i, 0))
    return pl.pallas_call(
        body, name="out_proj_ln", grid=(S // MIX_ROWS,),
        in_specs=[rows(k), rows(k), _whole(w_out), rows(D), _vec_spec(D), _vec_spec(D)], out_specs=[rows(D)] * 3,
        out_shape=[jax.ShapeDtypeStruct((S, D), F32), jax.ShapeDtypeStruct((S, D), F32), jax.ShapeDtypeStruct((S, D), BF16)],
    )(y_ssd, y_mla, w_out, xr, g, b)


def _ln_bwd(xr, mix, g, dh, w_out):
    k = w_out.shape[0] // 2

    def body(x_ref, m_ref, g_ref, dh_ref, w_ref, dpre_ref, dpreb_ref, dg_ref, db_ref, dys_ref, dym_ref):
        pre = ALPHA * x_ref[...] + m_ref[...]
        mu = jnp.mean(pre, axis=-1, keepdims=True)
        d = pre - mu
        rs = lax.rsqrt(jnp.mean(d * d, axis=-1, keepdims=True) + EPS_LN)
        xh = d * rs
        dy = dh_ref[...]
        gy = dy * g_ref[...]
        dpre = rs * (gy - jnp.mean(gy, axis=-1, keepdims=True) - xh * jnp.mean(gy * xh, axis=-1, keepdims=True))
        dpre_ref[...] = dpre
        dpreb = dpre.astype(BF16)
        dpreb_ref[...] = dpreb
        _acc_rows(dg_ref, jnp.sum(dy * xh, axis=0, keepdims=True))
        _acc_rows(db_ref, jnp.sum(dy, axis=0, keepdims=True))
        dys_ref[...] = lax.dot_general(dpreb, w_ref[:k], (((1,), (1,)), ((), ())), preferred_element_type=F32)
        dym_ref[...] = lax.dot_general(dpreb, w_ref[k:], (((1,), (1,)), ((), ())), preferred_element_type=F32)

    rows = lambda c: pl.BlockSpec((MIX_ROWS, c), lambda i: (i, 0))
    return pl.pallas_call(
        body, name="ln_mix_bwd", grid=(S // MIX_ROWS,),
        in_specs=[rows(D), rows(D), _vec_spec(D), rows(D), _whole(w_out)],
        out_specs=[rows(D), rows(D), _vec_spec(D), _vec_spec(D), rows(k), rows(k)],
        out_shape=[jax.ShapeDtypeStruct((S, D), F32), jax.ShapeDtypeStruct((S, D), BF16), jax.ShapeDtypeStruct((1, D), F32),
                   jax.ShapeDtypeStruct((1, D), F32), jax.ShapeDtypeStruct((S, k), F32), jax.ShapeDtypeStruct((S, k), F32)],
    )(xr, mix, g, dh, w_out)


FF_CHUNK = D_FF // NCHIP


FF_ROWS = 1024


def _ff_act_spec():
    return pl.BlockSpec((None, FF_ROWS, FF_CHUNK), lambda i, k: (k, i, 0))


def _ff_w_spec():
    return pl.BlockSpec((None, FF_CHUNK, D), lambda i, k: (k, 0, 0))


def _ffn_hidden_fwd(h, w_gate_t, w_up_t):
    def body(h_ref, wg_ref, wu_ref, g_ref, u_ref, a_ref):
        hh = h_ref[...]
        g = _dot(hh, wg_ref[...], ((1,), (1,)))
        u = _dot(hh, wu_ref[...], ((1,), (1,)))
        g_ref[...] = g.astype(BF16)
        u_ref[...] = u.astype(BF16)
        a_ref[...] = (g * _sigmoid(g) * u).astype(BF16)

    return pl.pallas_call(
        body, name="ffn_hidden_fwd", grid=(S // FF_ROWS, NCHIP),
        in_specs=[pl.BlockSpec((FF_ROWS, D), lambda i, k: (i, 0)), _ff_w_spec(), _ff_w_spec()], out_specs=[_ff_act_spec()] * 3,
        out_shape=[jax.ShapeDtypeStruct((NCHIP, S, FF_CHUNK), BF16)] * 3,
        compiler_params=pltpu.CompilerParams(dimension_semantics=("parallel", "parallel")),
    )(h, w_gate_t, w_up_t)


def _ffn_hidden_bwd(dout, w_down, gate, up):
    def body(d_ref, wd_ref, g_ref, u_ref, dg_ref, du_ref):
        d = _dot(d_ref[...], wd_ref[...], ((1,), (1,)))
        g = g_ref[...].astype(F32)
        sg = _sigmoid(g)
        dg_ref[...] = (d * u_ref[...].astype(F32) * (sg * (1.0 + g * (1.0 - sg)))).astype(BF16)
        du_ref[...] = (d * g * sg).astype(BF16)

    return pl.pallas_call(
        body, name="ffn_hidden_bwd", grid=(S // FF_ROWS, NCHIP),
        in_specs=[pl.BlockSpec((FF_ROWS, D), lambda i, k: (i, 0)), _ff_w_spec(), _ff_act_spec(), _ff_act_spec()],
        out_specs=[_ff_act_spec()] * 2, out_shape=[jax.ShapeDtypeStruct((NCHIP, S, FF_CHUNK), BF16)] * 2,
        compiler_params=pltpu.CompilerParams(dimension_semantics=("parallel", "parallel")),
    )(dout, w_down, gate, up)


def _final_fwd_bwd(h1, ffn, h1b, pb, w_pg, w_pp, target, g2, b2):
    def body(h_ref, f_ref, hb_ref, pb_ref, wpg_ref, wpp_ref, t_ref, g_ref, b_ref,
             dpre_ref, dpreb_ref, dpg_ref, dpp_ref, dg_ref, db_ref, loss_ref):
        sg = _sigmoid(jnp.dot(hb_ref[...], wpg_ref[...], preferred_element_type=F32))
        ppv = jnp.dot(pb_ref[...], wpp_ref[...], preferred_element_type=F32)
        pre = ALPHA * h_ref[...] + f_ref[...] + sg * ppv
        mu = jnp.mean(pre, axis=-1, keepdims=True)
        d = pre - mu
        rs = lax.rsqrt(jnp.mean(d * d, axis=-1, keepdims=True) + EPS_LN)
        xh = d * rs
        err = xh * g_ref[...] + b_ref[...] - t_ref[...]
        dy = err * (1.0 / D)
        gy = dy * g_ref[...]
        dpre = rs * (gy - jnp.mean(gy, axis=-1, keepdims=True) - xh * jnp.mean(gy * xh, axis=-1, keepdims=True))
        dpre_ref[...] = dpre
        dpreb_ref[...] = dpre.astype(BF16)
        dpg_ref[...] = (dpre * ppv * sg * (1.0 - sg)).astype(BF16)
        dpp_ref[...] = (dpre * sg).astype(BF16)
        _acc_rows(dg_ref, jnp.sum(dy * xh, axis=0, keepdims=True))
        _acc_rows(db_ref, jnp.sum(dy, axis=0, keepdims=True))
        _acc_rows(loss_ref, 0.5 * jnp.sum(jnp.mean(err * err, axis=-1, keepdims=True), axis=0, keepdims=True) * jnp.ones((1, LANE), F32))

    return pl.pallas_call(
        body, name="final_ln_loss", grid=(S // TR,),
        in_specs=[_row_spec(D)] * 3 + [_row_spec(pb.shape[1]), _whole(w_pg), _whole(w_pp), _row_spec(D)] + [_vec_spec(D)] * 2,
        out_specs=[_row_spec(D)] * 4 + [_vec_spec(D), _vec_spec(D), _vec_spec(LANE)],
        out_shape=[jax.ShapeDtypeStruct((S, D), F32)] + [jax.ShapeDtypeStruct((S, D), BF16)] * 3 + [
                   jax.ShapeDtypeStruct((1, D), F32), jax.ShapeDtypeStruct((1, D), F32), jax.ShapeDtypeStruct((1, LANE), F32)],
    )(h1, ffn, h1b, pb, w_pg, w_pp, target, g2, b2)


def _rot(u, cos_t, sin_t, lane):
    partner = jnp.where(lane < NOPE + ROPE // 2, pltpu.roll(u, LANE - ROPE // 2, 1), pltpu.roll(u, ROPE // 2, 1))
    return u * cos_t + partner * sin_t


def _rms(v, w):
    r = lax.rsqrt(jnp.mean(v * v, axis=-1, keepdims=True) + EPS_RMS)
    return v * r * w, r


def _rms_grad(v, r, w, g):
    gw = g * w
    return r * gw - v * (r * r * r * jnp.mean(gw * v, axis=-1, keepdims=True)), jnp.sum(g * v * r, axis=0, keepdims=True)


def _whole(arr):
    return pl.BlockSpec(arr.shape, lambda i: (0,) * arr.ndim)


def _qkv_fwd(small, w_q, w_k, w_v, q_norm, kv_norm, cos_t, sin_t):
    def body(sm_ref, wq_ref, wk_ref, wv_ref, qw_ref, kw_ref, c_ref, s_ref, qn_ref, kvn_ref, q_ref, k_ref, kt_ref, v_ref):
        lane = lax.broadcasted_iota(jnp.int32, (TR, LANE), 1)
        c, s = c_ref[...], s_ref[...]
        qn = _rms(sm_ref[:, SM_Q:SM_Q + Q_RANK], qw_ref[...])[0].astype(BF16)
        kvn = _rms(sm_ref[:, SM_KV:SM_KV + KV_RANK], kw_ref[...])[0].astype(BF16)
        qn_ref[...] = qn
        kvn_ref[...] = kvn
        kr = _rot(pltpu.roll(sm_ref[:, SM_KR:SM_KR + LANE], NOPE, 1), c, s, lane)
        for h in range(H):
            tile = slice(h * LANE, (h + 1) * LANE)
            q_ref[:, tile] = _rot(_dot(qn, wq_ref[:, tile], ((1,), (0,))), c, s, lane).astype(BF16)
            kt = _dot(kvn, wk_ref[:, tile], ((1,), (0,))) + kr
            k_ref[:, tile] = kt.astype(BF16)
            kt_ref[tile, :] = kt.T.astype(BF16)
        v_ref[...] = _dot(kvn, wv_ref[...], ((1,), (0,))).astype(BF16)

    w = H * LANE
    return pl.pallas_call(
        body, name="qkv_fwd", grid=(S // TR,),
        in_specs=[_row_spec(SMALL_W), _whole(w_q), _whole(w_k), _whole(w_v), _vec_spec(Q_RANK), _vec_spec(KV_RANK), _row_spec(LANE), _row_spec(LANE)],
        out_specs=[_row_spec(Q_RANK), _row_spec(KV_RANK), _row_spec(w), _row_spec(w), pl.BlockSpec((w, TR), lambda i: (0, i)),
                   _row_spec(H * VDIM)],
        out_shape=[jax.ShapeDtypeStruct((S, Q_RANK), BF16), jax.ShapeDtypeStruct((S, KV_RANK), BF16), jax.ShapeDtypeStruct((S, w), BF16),
                   jax.ShapeDtypeStruct((S, w), BF16), jax.ShapeDtypeStruct((w, S), BF16), jax.ShapeDtypeStruct((S, H * VDIM), BF16)],
    )(small, w_q, w_k, w_v, q_norm, kv_norm, cos_t, sin_t)


def _qkv_bwd(dqt, dk, dv, small, w_q, w_k, w_v, q_norm, kv_norm, cos_t, sin_t):
    def body(dq_ref, dk_ref, dv_ref, sm_ref, wq_ref, wk_ref, wv_ref, qw_ref, kw_ref, c_ref, s_ref,
             ds_ref, dql_ref, dkb_ref, dqw_ref, dkw_ref):
        lane = lax.broadcasted_iota(jnp.int32, (TR, LANE), 1)
        c, s = c_ref[...], -s_ref[...]
        dqn = jnp.zeros((TR, Q_RANK), F32)
        dkvn = _dot(dv_ref[...], wv_ref[...], ((1,), (1,)))
        dkr = jnp.zeros((TR, LANE), F32)
        for h in range(H):
            tile = slice(h * LANE, (h + 1) * LANE)
            dql = _rot(dq_ref[tile, :].T, c, s, lane).astype(BF16)
            dql_ref[:, tile] = dql
            dqn = dqn + _dot(dql, wq_ref[:, tile], ((1,), (1,)))
            dkt = dk_ref[:, tile]
            dkb_ref[:, tile] = dkt.astype(BF16)
            dkvn = dkvn + _dot(dkt, wk_ref[:, tile], ((1,), (1,)))
            dkr = dkr + dkt
        dkr = jnp.where((lane >= NOPE) & (lane < NOPE + ROPE), dkr, 0.0)
        q_c, kv_c = sm_ref[:, SM_Q:SM_Q + Q_RANK], sm_ref[:, SM_KV:SM_KV + KV_RANK]
        dq_c, dqw = _rms_grad(q_c, _rms(q_c, qw_ref[...])[1], qw_ref[...], dqn)
        dkv_c, dkw = _rms_grad(kv_c, _rms(kv_c, kw_ref[...])[1], kw_ref[...], dkvn)
        ds_ref[:, SM_Q:SM_Q + Q_RANK] = dq_c.astype(BF16)
        ds_ref[:, SM_KV:SM_KV + KV_RANK] = dkv_c.astype(BF16)
        ds_ref[:, SM_KR:SM_KR + LANE] = pltpu.roll(_rot(dkr, c, s, lane), LANE - NOPE, 1).astype(BF16)
        _acc_rows(dqw_ref, dqw)
        _acc_rows(dkw_ref, dkw)

    w = H * LANE
    return pl.pallas_call(
        body, name="qkv_bwd", grid=(S // TR,),
        in_specs=[pl.BlockSpec((w, TR), lambda i: (0, i)), _row_spec(w), _row_spec(H * VDIM), _row_spec(SMALL_W), _whole(w_q), _whole(w_k),
                  _whole(w_v), _vec_spec(Q_RANK), _vec_spec(KV_RANK), _row_spec(LANE), _row_spec(LANE)],
        out_specs=[_row_spec(SM_DT), _row_spec(w), _row_spec(w), _vec_spec(Q_RANK), _vec_spec(KV_RANK)],
        out_shape=[jax.ShapeDtypeStruct((S, SM_DT), BF16), jax.ShapeDtypeStruct((S, w), BF16), jax.ShapeDtypeStruct((S, w), BF16),
                   jax.ShapeDtypeStruct((1, Q_RANK), F32), jax.ShapeDtypeStruct((1, KV_RANK), F32)],
    )(dqt, dk, dv, small, w_q, w_k, w_v, q_norm, kv_norm, cos_t, sin_t)


CB = 256


def _shift_down(u, k, row):
    if k == 0:
        return u
    return jnp.where(row >= k, pltpu.roll(u, k, 0), 0.0)


def _shift_up(u, k, row):
    if k == 0:
        return u
    return jnp.where(row < S - k, pltpu.roll(u, S - k, 0), 0.0)


def _conv_fwd(u, w, b):
    def body(u_ref, w_ref, b_ref, o_ref):
        row = lax.broadcasted_iota(jnp.int32, (S, CB), 0)
        uu = u_ref[...]
        acc = b_ref[...] + w_ref[SSD_K - 1:SSD_K, :] * uu
        for k in range(SSD_K - 1):
            acc = acc + w_ref[k:k + 1, :] * _shift_down(uu, SSD_K - 1 - k, row)
        o_ref[...] = acc * _sigmoid(acc)

    c = u.shape[1]
    return pl.pallas_call(
        body, name="conv_fwd", grid=(c // CB,),
        in_specs=[pl.BlockSpec((S, CB), lambda j: (0, j)), pl.BlockSpec((SSD_K, CB), lambda j: (0, j)), pl.BlockSpec((1, CB), lambda j: (0, j))],
        out_specs=pl.BlockSpec((S, CB), lambda j: (0, j)), out_shape=jax.ShapeDtypeStruct((S, c), F32),
    )(u, w, b)


def _conv_bwd(u, w, b, dact):
    def body(u_ref, w_ref, b_ref, d_ref, du_ref, dw_ref, db_ref):
        row = lax.broadcasted_iota(jnp.int32, (S, CB), 0)
        uu = u_ref[...]
        sh = [_shift_down(uu, SSD_K - 1 - k, row) for k in range(SSD_K)]
        acc = b_ref[...]
        for k in range(SSD_K):
            acc = acc + w_ref[k:k + 1, :] * sh[k]
        sg = _sigmoid(acc)
        dacc = d_ref[...] * (sg * (1.0 + acc * (1.0 - sg)))
        du = w_ref[SSD_K - 1:SSD_K, :] * dacc
        for k in range(SSD_K - 1):
            du = du + w_ref[k:k + 1, :] * _shift_up(dacc, SSD_K - 1 - k, row)
        du_ref[...] = du.astype(BF16)
        for k in range(SSD_K):
            dw_ref[k:k + 1, :] = jnp.sum(dacc * sh[k], axis=0, keepdims=True)
        db_ref[...] = jnp.sum(dacc, axis=0, keepdims=True)

    c = u.shape[1]
    col = lambda r: pl.BlockSpec((r, CB), lambda j: (0, j))
    return pl.pallas_call(
        body, name="conv_bwd", grid=(c // CB,), in_specs=[col(S), col(SSD_K), col(1), col(S)], out_specs=[col(S), col(SSD_K), col(1)],
        out_shape=[jax.ShapeDtypeStruct((S, c), BF16), jax.ShapeDtypeStruct((SSD_K, c), F32), jax.ShapeDtypeStruct((1, c), F32)],
    )(u, w, b, dact)


NPAIR = H // 2
PAIRS_PER_GROUP = NPAIR // SSD_G


def _softplus(v):
    return jnp.maximum(v, 0.0) + jnp.log(1.0 + jnp.exp(-jnp.abs(v)))


def _dot(a, b, dims):
    return lax.dot_general(a.astype(BF16), b.astype(BF16), (dims, ((), ())), preferred_element_type=F32)


def _dot2(a, sel):
    hi = a.astype(BF16)
    lo = (a - hi.astype(F32)).astype(BF16)
    dims = (((1,), (0,)), ((), ()))
    return lax.dot_general(hi, sel, dims, preferred_element_type=F32) + lax.dot_general(lo, sel, dims, preferred_element_type=F32)


def _dot3(a, b, dims, split_lhs):
    v = a if split_lhs else b
    v1 = v.astype(BF16)
    r1 = v - v1.astype(F32)
    v2 = r1.astype(BF16)
    v3 = (r1 - v2.astype(F32)).astype(BF16)
    acc = None
    for part in (v1, v2, v3):
        lhs, rhs = (part, b) if split_lhs else (a, part)
        t = lax.dot_general(lhs, rhs, (dims, ((), ())), preferred_element_type=F32)
        acc = t if acc is None else acc + t
    return acc


def _ssd_chunk_common(dt_ref, dtT_ref, prow_ref, pcol_ref):
    prow = prow_ref[...]
    pcol = pcol_ref[...]
    ri = lax.broadcasted_iota(jnp.int32, (SSD_L, SSD_L), 0)
    ci = lax.broadcasted_iota(jnp.int32, (SSD_L, SSD_L), 1)
    causal = ri >= ci
    pre_c = dt_ref[...] + prow[0:1, :]
    dtc = _softplus(pre_c)
    a_row = -jnp.exp(prow[1:2, :])
    cs_col = _dot3(causal.astype(BF16), dtc * a_row, ((1,), (0,)), False)
    dtr = _softplus(dtT_ref[...] + pcol[:, 0:1])
    a_col = -jnp.exp(pcol[:, 1:2])
    cs_row = _dot3(dtr * a_col, (ri <= ci).astype(BF16), ((1,), (0,)), True)
    return prow, causal, pre_c, dtc, a_row, cs_col, cs_row


def _ssd_fwd(act, small, dtT, prow, pcol):
    def body(x_ref, b_ref, c_ref, dt_ref, dtT_ref, prow_ref, pcol_ref, y_ref, st_ref, state):
        @pl.when(pl.program_id(0) == 0)
        def _():
            state[...] = jnp.zeros_like(state)

        prow, causal, _, dtc, _, cs_col, cs_row = _ssd_chunk_common(dt_ref, dtT_ref, prow_ref, pcol_ref)
        lo = lax.broadcasted_iota(jnp.int32, (SSD_L, LANE), 1) < SSD_P
        lo1 = lo[0:1, :]
        for g in range(SSD_G):
            bm = b_ref[:, g * SSD_N:(g + 1) * SSD_N]
            cm = c_ref[:, g * SSD_N:(g + 1) * SSD_N]
            cb = _dot(cm, bm, ((1,), (1,)))
            for qq in range(PAIRS_PER_GROUP):
                q = g * PAIRS_PER_GROUP + qq
                ha, hb = 2 * q, 2 * q + 1
                csa, csb = cs_col[:, ha:ha + 1], cs_col[:, hb:hb + 1]
                xp = x_ref[:, q * LANE:(q + 1) * LANE]
                xx = xp * jnp.where(lo, dtc[:, ha:ha + 1], dtc[:, hb:hb + 1])
                ga = cb * jnp.exp(jnp.where(causal, csa - cs_row[ha:ha + 1, :], NEG))
                gb = cb * jnp.exp(jnp.where(causal, csb - cs_row[hb:hb + 1, :], NEG))
                y = _dot(ga, jnp.where(lo, xx, 0.0), ((1,), (0,))) + _dot(gb, jnp.where(lo, 0.0, xx), ((1,), (0,)))
                s_in = state[q]
                y = y + _dot(cm, s_in, ((1,), (0,))) * jnp.where(lo, jnp.exp(csa), jnp.exp(csb))
                y = y + jnp.where(lo1, prow[2:3, ha:ha + 1], prow[2:3, hb:hb + 1]) * xp
                y_ref[:, q * LANE:(q + 1) * LANE] = y
                la, lb = csa[SSD_L - 1:SSD_L, :], csb[SSD_L - 1:SSD_L, :]
                decay = jnp.where(lo, jnp.exp(la - csa), jnp.exp(lb - csb))
                st_ref[q] = s_in
                state[q] = s_in * jnp.where(lo1, jnp.exp(la), jnp.exp(lb)) + _dot(bm, xx * decay, ((0,), (0,)))

    L = SSD_L
    return pl.pallas_call(
        body, name="ssd_fwd", grid=(SSD_NC,),
        in_specs=[pl.BlockSpec((L, SSD_INNER), lambda c: (c, 0)),
                  pl.BlockSpec((L, SSD_G * SSD_N), lambda c: (c, SSD_INNER // (SSD_G * SSD_N))),
                  pl.BlockSpec((L, SSD_G * SSD_N), lambda c: (c, SSD_INNER // (SSD_G * SSD_N) + 1)),
                  pl.BlockSpec((L, LANE), lambda c: (c, SM_DT // LANE)),
                  pl.BlockSpec((LANE, L), lambda c: (0, c)),
                  pl.BlockSpec((8, LANE), lambda c: (0, 0)), pl.BlockSpec((LANE, 8), lambda c: (0, 0))],
        out_specs=[pl.BlockSpec((L, SSD_INNER), lambda c: (c, 0)),
                   pl.BlockSpec((None, NPAIR, SSD_N, LANE), lambda c: (c, 0, 0, 0))],
        out_shape=[jax.ShapeDtypeStruct((S, SSD_INNER), F32), jax.ShapeDtypeStruct((SSD_NC, NPAIR, SSD_N, LANE), F32)],
        scratch_shapes=[pltpu.VMEM((NPAIR, SSD_N, LANE), F32)],
        compiler_params=pltpu.CompilerParams(dimension_semantics=("arbitrary",)),
    )(act, act, act, small, dtT, prow, pcol)


def _ssd_bwd(act, small, dtT, prow, pcol, states, dy):
    def body(x_ref, b_ref, c_ref, dt_ref, dtT_ref, prow_ref, pcol_ref, st_ref, dy_ref,
             dx_ref, ddt_ref, dp_ref, dstate):
        @pl.when(pl.program_id(0) == 0)
        def _():
            dstate[...] = jnp.zeros_like(dstate)
            dp_ref[...] = jnp.zeros_like(dp_ref)

        prow, causal, pre_c, dtc, a_row, cs_col, cs_row = _ssd_chunk_common(dt_ref, dtT_ref, prow_ref, pcol_ref)
        lane = lax.broadcasted_iota(jnp.int32, (SSD_L, LANE), 1)
        sub = lax.broadcasted_iota(jnp.int32, (LANE, SSD_L), 0)
        rowi = lax.broadcasted_iota(jnp.int32, (SSD_L, 1), 0)
        pick_p = lax.broadcasted_iota(jnp.int32, (LANE, LANE), 0)
        pick_l = lax.broadcasted_iota(jnp.int32, (LANE, LANE), 1)
        lo = lane < SSD_P
        lo1 = lo[0:1, :]
        dcs_c = jnp.zeros((SSD_L, LANE), F32)
        dcs_r = jnp.zeros((LANE, SSD_L), F32)
        ddt_x = jnp.zeros((SSD_L, LANE), F32)
        dd_row = jnp.zeros((1, LANE), F32)
        for g in range(SSD_G):
            bm = b_ref[:, g * SSD_N:(g + 1) * SSD_N]
            cm = c_ref[:, g * SSD_N:(g + 1) * SSD_N]
            cb = _dot(cm, bm, ((1,), (1,)))
            dcb = jnp.zeros((SSD_L, SSD_L), F32)
            dbm = jnp.zeros((SSD_L, SSD_N), F32)
            dcm = jnp.zeros((SSD_L, SSD_N), F32)
            for qq in range(PAIRS_PER_GROUP):
                q = g * PAIRS_PER_GROUP + qq
                ha, hb = 2 * q, 2 * q + 1
                csa, csb = cs_col[:, ha:ha + 1], cs_col[:, hb:hb + 1]
                xp = x_ref[:, q * LANE:(q + 1) * LANE]
                dtp = jnp.where(lo, dtc[:, ha:ha + 1], dtc[:, hb:hb + 1])
                xx = xp * dtp
                lma = jnp.exp(jnp.where(causal, csa - cs_row[ha:ha + 1, :], NEG))
                lmb = jnp.exp(jnp.where(causal, csb - cs_row[hb:hb + 1, :], NEG))
                ga, gb = cb * lma, cb * lmb
                dyp = dy_ref[:, q * LANE:(q + 1) * LANE]
                dya, dyb = jnp.where(lo, dyp, 0.0), jnp.where(lo, 0.0, dyp)
                s_in = st_ref[q]
                ds_out = dstate[q]
                la, lb = csa[SSD_L - 1:SSD_L, :], csb[SSD_L - 1:SSD_L, :]
                ecs = jnp.where(lo, jnp.exp(csa), jnp.exp(csb))
                decay = jnp.where(lo, jnp.exp(la - csa), jnp.exp(lb - csb))
                cd = jnp.where(lo1, jnp.exp(la), jnp.exp(lb))
                bds = _dot(bm, ds_out, ((1,), (0,)))
                dxx = _dot(ga, dya, ((0,), (0,))) + _dot(gb, dyb, ((0,), (0,))) + bds * decay
                dga = _dot(dya, xx, ((1,), (1,)))
                dgb = _dot(dyb, xx, ((1,), (1,)))
                dsega, dsegb = dga * ga, dgb * gb
                dcb = dcb + dga * lma + dgb * lmb
                yoff = _dot(cm, s_in, ((1,), (0,))) * ecs
                dye = dyp * ecs
                dcm = dcm + _dot(dye, s_in, ((1,), (1,)))
                xd = xx * decay
                dbm = dbm + _dot(xd, ds_out, ((1,), (1,)))
                wv = xd * bds
                ends = jnp.sum(wv, axis=0, keepdims=True) + cd * jnp.sum(ds_out * s_in, axis=0, keepdims=True)
                t1 = dyp * yoff - wv + jnp.where(rowi == SSD_L - 1, ends, 0.0)
                to_pair = (((pick_p < SSD_P) & (pick_l == ha)) | ((pick_p >= SSD_P) & (pick_l == hb))).astype(BF16)
                to_a_b = jnp.concatenate([(pick_l == ha).astype(BF16), (pick_l == hb).astype(BF16)], axis=0)
                dcs_c = dcs_c + _dot2(t1, to_pair) + _dot2(jnp.concatenate([dsega, dsegb], axis=1), to_a_b)
                dcs_r = (dcs_r + jnp.where(sub == ha, jnp.sum(dsega, axis=0, keepdims=True), 0.0)
                         + jnp.where(sub == hb, jnp.sum(dsegb, axis=0, keepdims=True), 0.0))
                dstate[q] = _dot(cm, dye, ((0,), (0,))) + cd * ds_out
                dpair = jnp.where(lo1, prow[2:3, ha:ha + 1], prow[2:3, hb:hb + 1])
                dx_ref[:, q * LANE:(q + 1) * LANE] = dxx * dtp + dpair * dyp
                ddt_x = ddt_x + _dot2(dxx * xp, to_pair)
                dd_row = dd_row + jnp.sum(_dot2(dyp * xp, to_pair), axis=0, keepdims=True)
            dx_ref[:, SSD_INNER + g * SSD_N:SSD_INNER + (g + 1) * SSD_N] = dbm + _dot(dcb, cm, ((0,), (0,)))
            dx_ref[:, SSD_INNER + (SSD_G + g) * SSD_N:SSD_INNER + (SSD_G + g + 1) * SSD_N] = dcm + _dot(dcb, bm, ((1,), (0,)))
        ri = lax.broadcasted_iota(jnp.int32, (SSD_L, SSD_L), 0)
        ci = lax.broadcasted_iota(jnp.int32, (SSD_L, SSD_L), 1)
        da = _dot3((ri <= ci).astype(BF16), dcs_c, ((1,), (0,)), False)
        da = da - _dot3(dcs_r, causal.astype(BF16), ((1,), (0,)), True).T
        ddt = ddt_x + da * a_row
        ddt_raw = ddt * _sigmoid(pre_c)
        ddt_ref[...] = ddt_raw
        da_head = jnp.sum(da * dtc, axis=0, keepdims=True) * a_row
        dp_ref[0:1, :] += jnp.sum(ddt_raw, axis=0, keepdims=True)
        dp_ref[1:2, :] += da_head
        dp_ref[2:3, :] += dd_row

    L = SSD_L
    rev = SSD_NC - 1
    bc_cols = SSD_INNER // (SSD_G * SSD_N)
    return pl.pallas_call(
        body, name="ssd_bwd", grid=(SSD_NC,),
        in_specs=[pl.BlockSpec((L, SSD_INNER), lambda c: (rev - c, 0)),
                  pl.BlockSpec((L, SSD_G * SSD_N), lambda c: (rev - c, bc_cols)),
                  pl.BlockSpec((L, SSD_G * SSD_N), lambda c: (rev - c, bc_cols + 1)),
                  pl.BlockSpec((L, LANE), lambda c: (rev - c, SM_DT // LANE)),
                  pl.BlockSpec((LANE, L), lambda c: (0, rev - c)),
                  pl.BlockSpec((8, LANE), lambda c: (0, 0)), pl.BlockSpec((LANE, 8), lambda c: (0, 0)),
                  pl.BlockSpec((None, NPAIR, SSD_N, LANE), lambda c: (rev - c, 0, 0, 0)),
                  pl.BlockSpec((L, SSD_INNER), lambda c: (rev - c, 0))],
        out_specs=[pl.BlockSpec((L, SSD_XBC), lambda c: (rev - c, 0)),
                   pl.BlockSpec((L, LANE), lambda c: (rev - c, 0)),
                   pl.BlockSpec((8, LANE), lambda c: (0, 0))],
        out_shape=[jax.ShapeDtypeStruct((S, SSD_XBC), F32), jax.ShapeDtypeStruct((S, LANE), F32),
                   jax.ShapeDtypeStruct((8, LANE), F32)],
        scratch_shapes=[pltpu.VMEM((NPAIR, SSD_N, LANE), F32)],
        compiler_params=pltpu.CompilerParams(dimension_semantics=("arbitrary",)),
    )(act, act, act, small, dtT, prow, pcol, states, dy)


TQ = 256
TK = 256
FWD_TQ = 256
FWD_TK = 256


def _attn_fwd(qc, kc, v):
    TQ, TK = FWD_TQ, FWD_TK

    def body(q_ref, k_ref, v_ref, o_ref, lse_ref):
        i = pl.program_id(1)
        lo = lax.broadcasted_iota(jnp.int32, (TQ, LANE), 1) < VDIM
        lo_k = lax.broadcasted_iota(jnp.int32, (TK, LANE), 1) < VDIM
        row_minus_col = lax.broadcasted_iota(jnp.int32, (TQ, TK), 0) - lax.broadcasted_iota(jnp.int32, (TQ, TK), 1)
        qa, qb = q_ref[:, 0:LANE], q_ref[:, LANE:2 * LANE]

        def scores(kb):
            kk = k_ref[pl.ds(pl.multiple_of(kb * TK, TK), TK), :]
            return (_dot(qa, kk[:, 0:LANE], ((1,), (1,))) * ATT_SCALE_LOG2, _dot(qb, kk[:, LANE:2 * LANE], ((1,), (1,))) * ATT_SCALE_LOG2)

        def update(kb, sa, sb, stats):
            ma, la, mb, lb, acc = stats
            vv = v_ref[pl.ds(pl.multiple_of(kb * TK, TK), TK), :]
            na = jnp.maximum(ma, jnp.max(sa, axis=1, keepdims=True))
            nb = jnp.maximum(mb, jnp.max(sb, axis=1, keepdims=True))
            pa, pb = jnp.exp2(sa - na), jnp.exp2(sb - nb)
            fa, fb = jnp.exp2(ma - na), jnp.exp2(mb - nb)
            la = fa * la + jnp.sum(pa, axis=1, keepdims=True)
            lb = fb * lb + jnp.sum(pb, axis=1, keepdims=True)
            acc = (acc * jnp.where(lo, fa, fb) + _dot(pa, jnp.where(lo_k, vv, 0), ((1,), (0,)))
                   + _dot(pb, jnp.where(lo_k, 0, vv), ((1,), (0,))))
            return na, la, nb, lb, acc

        def step(kb, carry):
            sa, sb = carry[:2]
            nxt = scores(kb + 1)
            return nxt + update(kb, sa, sb, carry[2:])

        neg = jnp.full((TQ, 1), NEG, F32)
        zero = jnp.zeros((TQ, 1), F32)
        n_full = i * (TQ // TK)
        carry = lax.fori_loop(0, n_full, step, scores(0) + (neg, zero, neg, zero, jnp.zeros((TQ, LANE), F32)))
        s, stats = carry[:2], carry[2:]
        for d in range(TQ // TK):
            nxt = scores(n_full + d + 1) if d + 1 < TQ // TK else None
            sa, sb = (jnp.where(row_minus_col >= d * TK, t, NEG) for t in s)
            stats = update(n_full + d, sa, sb, stats)
            s = nxt
        ma, la, mb, lb, acc = stats
        o_ref[...] = acc / jnp.where(lo, la, lb)
        lse_ref[...] = jnp.where(lo, ma + jnp.log2(la), mb + jnp.log2(lb)) * LN2

    return pl.pallas_call(
        body, name="attn_fwd", grid=(NPAIR, S // TQ),
        in_specs=[pl.BlockSpec((TQ, 2 * LANE), lambda j, i: (i, j)), pl.BlockSpec((S, 2 * LANE), lambda j, i: (0, j)),
                  pl.BlockSpec((S, LANE), lambda j, i: (0, j))],
        out_specs=[pl.BlockSpec((TQ, LANE), lambda j, i: (i, j)), pl.BlockSpec((None, TQ, LANE), lambda j, i: (j, i, 0))],
        out_shape=[jax.ShapeDtypeStruct((S, H * VDIM), F32), jax.ShapeDtypeStruct((NPAIR, S, LANE), F32)],
        compiler_params=pltpu.CompilerParams(dimension_semantics=("parallel", "parallel")),
    )(qc, kc, v)


def _attn_rows(lse, o, do):
    def body(lse_ref, o_ref, do_ref, r_ref):
        lt = lse_ref[...].T * (1.0 / LN2)
        tt = (o_ref[...] * do_ref[...]).T
        r_ref[...] = jnp.zeros_like(r_ref)
        r_ref[0:1, :] = lt[0:1, :]
        r_ref[1:2, :] = lt[VDIM:VDIM + 1, :]
        r_ref[2:3, :] = jnp.sum(tt[0:VDIM, :], axis=0, keepdims=True)
        r_ref[3:4, :] = jnp.sum(tt[VDIM:LANE, :], axis=0, keepdims=True)

    tile = pl.BlockSpec((S, LANE), lambda j: (0, j))
    return pl.pallas_call(
        body, name="attn_rows", grid=(NPAIR,), in_specs=[pl.BlockSpec((None, S, LANE), lambda j: (j, 0, 0)), tile, tile],
        out_specs=pl.BlockSpec((None, 8, S), lambda j: (j, 0, 0)), out_shape=jax.ShapeDtypeStruct((NPAIR, 8, S), F32),
    )(lse, o, do)


def _attn_bwd(qc, kc, kct, v, do, rows):
    nq = S // TQ

    def body(q_ref, k_ref, kt_ref, v_ref, do_ref, r_ref, dqt_ref, dk_ref, dv_ref):
        kb = pl.program_id(1)

        @pl.when(kb == 0)
        def _():
            dqt_ref[...] = jnp.zeros_like(dqt_ref)

        lo = lax.broadcasted_iota(jnp.int32, (TK, LANE), 1) < VDIM
        q_minus_k = lax.broadcasted_iota(jnp.int32, (TK, TQ), 1) - lax.broadcasted_iota(jnp.int32, (TK, TQ), 0)
        vv = v_ref[...]
        kk = k_ref[...]

        def step(qi, carry):
            off = pl.multiple_of(qi * TQ, TQ)
            qq = q_ref[pl.ds(off, TQ), :]
            dd = do_ref[pl.ds(off, TQ), :].astype(BF16)
            rr = r_ref[:, pl.ds(off, TQ)]
            keep = q_minus_k >= (kb - qi) * TQ
            out = []
            for x in range(2):
                sel = lo if x == 0 else jnp.logical_not(lo)
                kx, qx = kk[:, x * LANE:(x + 1) * LANE], qq[:, x * LANE:(x + 1) * LANE]
                st = jnp.where(keep, _dot(kx, qx, ((1,), (1,))) * ATT_SCALE_LOG2, NEG)
                pt = jnp.exp2(st - rr[x:x + 1, :])
                dpt = _dot(jnp.where(sel, vv, 0), dd, ((1,), (1,)))
                dst = (pt * (dpt - rr[2 + x:3 + x, :]) * ATT_SCALE).astype(BF16)
                out.append(carry[x] + _dot(dst, qx, ((1,), (0,))))
                out.append(_dot(pt, jnp.where(sel, dd, 0), ((1,), (0,))))
                dqt_ref[x * LANE:(x + 1) * LANE, pl.ds(off, TQ)] += _dot(kt_ref[x * LANE:(x + 1) * LANE, :], dst, ((1,), (0,)))
            return out[0], out[2], carry[2] + out[1] + out[3]

        z = jnp.zeros((TK, LANE), F32)
        dka, dkb, dv = lax.fori_loop(kb, nq, step, (z, z, z))
        dk_ref[:, 0:LANE] = dka
        dk_ref[:, LANE:2 * LANE] = dkb
        dv_ref[...] = dv.astype(BF16)

    return pl.pallas_call(
        body, name="attn_bwd", grid=(NPAIR, S // TK),
        in_specs=[pl.BlockSpec((S, 2 * LANE), lambda j, k: (0, j)), pl.BlockSpec((TK, 2 * LANE), lambda j, k: (k, j)),
                  pl.BlockSpec((2 * LANE, TK), lambda j, k: (j, k)), pl.BlockSpec((TK, LANE), lambda j, k: (k, j)),
                  pl.BlockSpec((S, LANE), lambda j, k: (0, j)), pl.BlockSpec((None, 8, S), lambda j, k: (j, 0, 0))],
        out_specs=[pl.BlockSpec((2 * LANE, S), lambda j, k: (j, 0)), pl.BlockSpec((TK, 2 * LANE), lambda j, k: (k, j)),
                   pl.BlockSpec((TK, LANE), lambda j, k: (k, j))],
        out_shape=[jax.ShapeDtypeStruct((H * LANE, S), F32), jax.ShapeDtypeStruct((S, H * LANE), F32),
                   jax.ShapeDtypeStruct((S, H * VDIM), BF16)],
        compiler_params=pltpu.CompilerParams(dimension_semantics=("parallel", "arbitrary")),
    )(qc, kc, kct, v, do, rows)


_IN_Z, _IN_XBC, _IN_DT, _IN_Q, _IN_KV, _IN_KR = 0, 1024, 2560, 2576, 2960, 3216


PROJ_COLS = 512
SMALL_PAD = pl.cdiv(SMALL_W, PROJ_COLS) * PROJ_COLS


def _prep_in(w_in_t):
    dt = w_in_t.dtype
    return jnp.concatenate(
        [w_in_t[_IN_Q:_IN_KV], w_in_t[_IN_KV:_IN_KR], w_in_t[_IN_KR:IN_WIDTH], jnp.zeros((LANE - ROPE, D), dt),
         w_in_t[_IN_DT:_IN_Q], jnp.zeros((SMALL_PAD - SM_DT - H, D), dt)], axis=0)


def _proj_in(xb, w_in_t, w_small):
    nz, nx, ns = (_IN_XBC - _IN_Z) // PROJ_COLS, (_IN_DT - _IN_XBC) // PROJ_COLS, SMALL_PAD // PROJ_COLS

    dt_block, dt_at = divmod(SM_DT, PROJ_COLS)

    def body(x_ref, w_ref, ws_ref, z_ref, xbc_ref, sm_ref, dtt_ref):
        i = pl.program_id(0)

        def emit(w, o_ref):
            o_ref[...] = lax.dot_general(x_ref[...], w[...], (((1,), (1,)), ((), ())), preferred_element_type=F32)

        pl.when(i < nz)(lambda: emit(w_ref, z_ref))
        pl.when((i >= nz) & (i < nz + nx))(lambda: emit(w_ref, xbc_ref))
        pl.when(i >= nz + nx)(lambda: emit(ws_ref, sm_ref))

        @pl.when(i == nz + nx + dt_block)
        def _():
            dtt_ref[...] = sm_ref[:, dt_at:dt_at + LANE].T

    def blocks(first, count, rows):
        at = lambda i: jnp.clip(i - first, 0, count - 1)
        return pl.BlockSpec((PROJ_COLS, D), lambda i: (at(i), 0)) if rows else pl.BlockSpec((S, PROJ_COLS), lambda i: (0, at(i)))

    return pl.pallas_call(
        body, name="proj_in", grid=(nz + nx + ns,),
        in_specs=[pl.BlockSpec((S, D), lambda i: (0, 0)), blocks(0, nz + nx, True), blocks(nz + nx, ns, True)],
        out_specs=[blocks(0, nz, False), blocks(nz, nx, False), blocks(nz + nx, ns, False), pl.BlockSpec((LANE, S), lambda i: (0, 0))],
        out_shape=[jax.ShapeDtypeStruct((S, _IN_XBC - _IN_Z), F32), jax.ShapeDtypeStruct((S, _IN_DT - _IN_XBC), F32),
                   jax.ShapeDtypeStruct((S, SMALL_W), F32), jax.ShapeDtypeStruct((LANE, S), F32)],
    )(xb, w_in_t, w_small)


PART_COLS = 512


def _part_blocks(widths):
    first = [0]
    for w in widths:
        first.append(first[-1] + w // PART_COLS)

    def at(part):
        return lambda i: jnp.clip(i - first[part], 0, first[part + 1] - first[part] - 1)

    return first, at


def _mm_ta_stacked(parts, b, rows, name):
    n = b.shape[1]
    first, at = _part_blocks([a.shape[1] for a in parts])
    assert first[-1] == pl.cdiv(rows, PART_COLS)

    def body(*refs):
        b_ref, o_ref = refs[-2:]
        i = pl.program_id(0)
        for part, a_ref in enumerate(refs[:-2]):
            @pl.when((i >= first[part]) & (i < first[part + 1]))
            def _(a_ref=a_ref):
                o_ref[...] = lax.dot_general(a_ref[...], b_ref[...], (((0,), (0,)), ((), ())),
                                             preferred_element_type=F32).astype(BF16)

    return pl.pallas_call(
        body, name=name, grid=(first[-1],),
        in_specs=[pl.BlockSpec((S, PART_COLS), lambda i, at=at(part): (0, at(i))) for part in range(len(parts))]
        + [pl.BlockSpec((S, n), lambda i: (0, 0))],
        out_specs=pl.BlockSpec((PART_COLS, n), lambda i: (i, 0)), out_shape=jax.ShapeDtypeStruct((rows, n), BF16),
    )(*parts, b)


def _prep_attn(w_qb, w_kvb):
    w_q = jnp.pad(w_qb.reshape(Q_RANK, H, NOPE + ROPE), ((0, 0), (0, 0), (0, LANE - NOPE - ROPE))).reshape(Q_RANK, H * LANE)
    kv3 = w_kvb.reshape(KV_RANK, H, NOPE + VDIM)
    w_k = jnp.pad(kv3[:, :, :NOPE], ((0, 0), (0, 0), (0, LANE - NOPE))).reshape(KV_RANK, H * LANE)
    w_v = kv3[:, :, NOPE:].reshape(KV_RANK, H * VDIM)
    return w_q, w_k, w_v


def _rope_tables(positions):
    inv_freq = 1.0 / (10000.0 ** (jnp.arange(0, ROPE, 2, dtype=F32) / ROPE))
    ang = positions.astype(F32).reshape(S, 1) * inv_freq
    cos, sin = jnp.cos(ang), jnp.sin(ang)
    cos_t = jnp.concatenate([jnp.ones((S, NOPE), F32), cos, cos, jnp.ones((S, LANE - NOPE - ROPE), F32)], axis=1)
    sin_t = jnp.concatenate([jnp.zeros((S, NOPE), F32), -sin, sin, jnp.zeros((S, LANE - NOPE - ROPE), F32)], axis=1)
    return cos_t, sin_t


def _local_step(x, p, positions, target, w_in, fetch, send, sp, started):
    w_in_t = w_in.reshape(IN_WIDTH, D)
    w_small = _prep_in(w_in_t)
    cos_t, sin_t = _rope_tables(positions)
    prow = jnp.zeros((8, LANE), F32).at[0, :H].set(sp["dt_bias"][0]).at[1, :H].set(sp["A_log"][0]).at[2, :H].set(sp["D"][0])
    pcol = prow.T

    xb, pb = (x + started).astype(BF16), p.astype(BF16)
    z, xbc, small, dt_t = _proj_in(xb, w_in_t, w_small)
    act = _conv_fwd(xbc, sp["conv_w"], sp["conv_b"])
    y, states = _ssd_fwd(act, small, dt_t, prow, pcol)
    y_ssd = _gate_norm_fwd(y, z, sp["ssd_norm"])
    gl = fetch("attn", y_ssd)
    w_q, w_k, w_v = _prep_attn(_from_cols(gl["w_qb"]), _from_cols(gl["w_kvb"]))
    qn, kvn, qcat, kcat, kcat_t, v = _qkv_fwd(small, w_q, w_k, w_v, sp["q_norm"], sp["kv_norm"], cos_t, sin_t)
    o, lse = _attn_fwd(qcat, kcat, v)
    y_mla = _rms_fwd(o, sp["out_norm"], name="out_norm_fwd")
    w_out = fetch("out", y_mla)["w_out"]
    w_out = w_out.reshape(2 * SSD_INNER, D)
    mix, h1, h1b = _out_proj_ln(y_ssd, y_mla, w_out, x, sp["ln_mix_g"], sp["ln_mix_b"])
    gl = fetch("ffn", h1b)
    w_pg, w_pp = gl["w_pg"].reshape(D, D), _from_cols(gl["w_pp"])
    w_gate, w_up, w_down = gl["w_gate"], gl["w_up"], gl["w_down"]
    gate, up, actf = _ffn_hidden_fwd(h1b, w_gate, w_up)
    ffn = _mm([(actf, w_down)], chunk="sum", name="ffn_down")
    dpre2, dpre2b, dpg, dpp, dg2, db2, loss_row = _final_fwd_bwd(h1, ffn, h1b, pb, w_pg, w_pp, target, sp["ln_ffn_g"], sp["ln_ffn_b"])

    g = {"ln_ffn_g": dg2, "ln_ffn_b": db2}
    g["w_pp"] = _to_cols(_mm([(pb, dpp)], ta=True, out_dtype=BF16, name="d_w_ple_proj"))
    g["w_pg"] = _mm([(h1b, dpg)], ta=True, out_dtype=BF16, name="d_w_ple_gate").reshape(NCHIP, D // NCHIP, D)
    g["w_down"] = _mm([(actf, dpre2b)], ta=True, chunk="out", out_dtype=BF16, name="d_w_down")
    dgate, dup = _ffn_hidden_bwd(dpre2b, w_down, gate, up)
    g["w_gate"] = _mm([(dgate, h1b)], ta=True, chunk="out", out_dtype=BF16, name="d_w_gate")
    g["w_up"] = _mm([(dup, h1b)], ta=True, chunk="out", out_dtype=BF16, name="d_w_up")
    sent = send("ffn", {name: g.pop(name) for name in dict(ASYNC_GROUPS)["ffn"]})
    dh1 = _mm([(dgate, w_gate), (dup, w_up), (dpg, w_pg.T)], chunk="sum", add=dpre2, add_scale=ALPHA, name="d_h1")
    dpre1, dpre1b, g["ln_mix_g"], g["ln_mix_b"], dy_ssd, dy_mla = _ln_bwd(x, mix, sp["ln_mix_g"] + sent, dh1, w_out)
    dw_out = _mm_ta_stacked((y_ssd, y_mla), dpre1b, 2 * SSD_INNER, "d_w_out")
    sent = send("out", {"w_out": dw_out.reshape(NCHIP, 2 * SSD_INNER // NCHIP, D)})
    do, g["out_norm"] = _rms_bwd(o, sp["out_norm"] + sent, dy_mla, name="out_norm_bwd")
    dqt, dk, dv = _attn_bwd(qcat, kcat, kcat_t, v, do, _attn_rows(lse, o, do))
    dlatent, dqlin, dkb, g["q_norm"], g["kv_norm"] = _qkv_bwd(dqt, dk, dv, small, w_q, w_k, w_v, sp["q_norm"], sp["kv_norm"], cos_t, sin_t)
    dw_q = _mm([(qn, dqlin)], ta=True, out_dtype=BF16, name="d_w_q")
    dw_k = _mm([(kvn, dkb)], ta=True, out_dtype=BF16, name="d_w_k")
    dw_v = _mm([(kvn, dv)], ta=True, out_dtype=BF16, name="d_w_v")
    dw_qb = _to_cols(dw_q.reshape(Q_RANK, H, LANE)[:, :, :NOPE + ROPE].reshape(Q_RANK, H * (NOPE + ROPE)))
    dw_kvb = _to_cols(jnp.concatenate([dw_k.reshape(KV_RANK, H, LANE)[:, :, :NOPE], dw_v.reshape(KV_RANK, H, VDIM)],
                                       axis=2).reshape(KV_RANK, H * (NOPE + VDIM)))
    sent = send("attn", {"w_qb": dw_qb, "w_kvb": dw_kvb})
    dy, dz, g["ssd_norm"] = _gate_norm_bwd(y, z, sp["ssd_norm"] + sent, dy_ssd)
    dact, ddt, dprow = _ssd_bwd(act, small, dt_t, prow, pcol, states, dy)
    g["dt_bias"], g["A_log"], g["D"] = dprow[0:1, :H], dprow[1:2, :H], dprow[2:3, :H]
    dxbc, g["conv_w"], g["conv_b"] = _conv_bwd(xbc, sp["conv_w"], sp["conv_b"], dact)
    dsmall = jnp.concatenate([dlatent, ddt.astype(BF16)], axis=1)
    in_blocks = [(d, w_in_t, (k, first // PROJ_COLS + k, PROJ_COLS))
                 for d, first in ((dz, _IN_Z), (dxbc, _IN_XBC)) for k in range(d.shape[1] // PROJ_COLS)]
    grad_x = _mm(in_blocks + [(dsmall, w_small, (0, 0, SMALL_W))], add=dpre1, add_scale=ALPHA, name="d_x")
    sent = send("small", dict(g, loss=loss_row))
    n_small = IN_WIDTH - _IN_DT
    dsm = jnp.concatenate([(ddt[:, :H] + sent).astype(BF16), dlatent[:, :n_small - H], jnp.zeros((S, D - n_small), BF16)], axis=1)
    dw_in = _mm_ta_stacked((dz, dxbc, dsm), xb, IN_WIDTH, "d_w_in").reshape(NCHIP, IN_WIDTH // NCHIP * D // LANE, LANE)
    return loss_row, grad_x, dw_in, g


MESH = pl.DeviceIdType.MESH
BIG = (("w_in", (D, IN_WIDTH), 1), ("w_qb", (Q_RANK, H * (NOPE + ROPE)), 1), ("w_kvb", (KV_RANK, H * (NOPE + VDIM)), 1),
       ("w_out", (2 * SSD_INNER, D), 0), ("w_gate", (D, D_FF), 1), ("w_up", (D, D_FF), 1), ("w_down", (D_FF, D), 0),
       ("w_pg", (D, D), 0), ("w_pp", (PLE, D), 1))
CONV_SHARD = SSD_XBC // NCHIP
BF16_ROWS = 16


def _from_cols(stack):
    return jnp.concatenate([stack[k] for k in range(NCHIP)], axis=1)


def _to_cols(full):
    r, c4 = full.shape
    return full.reshape(r, NCHIP, c4 // NCHIP).transpose(1, 0, 2)


def _coords():
    return lax.axis_index("x"), lax.axis_index("y"), lax.axis_index("c")


def _peers():
    x, y, c = _coords()
    return 2 * x + y, c, [(1 - x, y), (x, 1 - y), (1 - x, 1 - y)], (x, y, 1 - c)


def _half_axis(shape):
    return 0 if shape[-2] % (2 * BF16_ROWS) == 0 else 1


def _half_shape(shape):
    r, c = shape[-2:]
    return (r // 2, c) if _half_axis(shape) == 0 else (r, c // 2)


def _half(core, shape):
    r, c = shape[-2:]
    if _half_axis(shape) == 0:
        return pl.ds(pl.multiple_of(core * (r // 2), BF16_ROWS), r // 2), slice(None)
    return slice(None), pl.ds(pl.multiple_of(core * (c // 2), LANE), c // 2)


ASYNC_GROUPS = (("attn", ("w_qb", "w_kvb")), ("out", ("w_out",)), ("ffn", ("w_gate", "w_up", "w_down", "w_pg", "w_pp")))
TRANSPOSED = ("w_in", "w_gate", "w_up")
ROW_MAJOR = ("w_in",)
HBM_SPEC = pl.BlockSpec(memory_space=pltpu.HBM)
SEM_SPEC = pl.BlockSpec(memory_space=pltpu.SEMAPHORE)
IN_FLIGHT = pltpu.SideEffectType.DATAFLOW_SIDE_EFFECTING


def _in_hbm(a):
    return pltpu.with_memory_space_constraint(a, pltpu.HBM)


def _hbm_like(arrs, lead=()):
    return [pltpu.HBM(lead + a.shape, a.dtype) for a in arrs]


def _split_start(name, srcs, lands, after, n_sem, start):
    n = len(srcs)
    order = [] if after is None else [after]

    def body(*refs):
        src_refs, land_refs = refs[:n], refs[n:2 * n]
        send_sems, recv_sems = refs[2 * n + len(order)], refs[2 * n + len(order) + 1]
        token = refs[-1]

        def copy(send_idx, recv_idx, src, dst, to):
            return pltpu.make_async_remote_copy(src_ref=src, dst_ref=dst, send_sem=send_sems.at[send_idx],
                                                recv_sem=recv_sems.at[recv_idx], device_id=to, device_id_type=MESH)

        for cp in start(src_refs, land_refs, copy):
            cp.start()
        token[...] = jnp.zeros_like(token)

    sem = pltpu.SemaphoreType.DMA((n_sem,))
    outs = pl.pallas_call(
        body, name=name, in_specs=[HBM_SPEC] * (2 * n) + [pl.BlockSpec(memory_space=pl.ANY)] * len(order),
        out_specs=[SEM_SPEC, SEM_SPEC] + [HBM_SPEC] * (2 * n) + [pl.BlockSpec(memory_space=pltpu.VMEM)],
        out_shape=[sem, sem] + _hbm_like(srcs) + _hbm_like(lands) + [jax.ShapeDtypeStruct((8, LANE), F32)],
        input_output_aliases={i: 2 + i for i in range(2 * n)},
        compiler_params=pltpu.CompilerParams(has_side_effects=IN_FLIGHT),
    )(*[_in_hbm(a) for a in srcs], *[_in_hbm(a) for a in lands], *order)
    return (outs[0], outs[1], outs[2:2 + n], outs[2 + n:2 + 2 * n]), outs[-1]


def _split_wait(name, send_sems, recv_sems, srcs, lands, after, waits):
    n = len(srcs)

    def body(*refs):
        src_refs, land_refs = refs[:n], refs[n:2 * n]
        send_ref, recv_ref = refs[2 * n], refs[2 * n + 1]

        def copy(send_idx, recv_idx, src, dst, to):
            return pltpu.make_async_remote_copy(src_ref=src, dst_ref=dst, send_sem=send_ref.at[send_idx],
                                                recv_sem=recv_ref.at[recv_idx], device_id=to, device_id_type=MESH)

        for cp in waits(src_refs, land_refs, copy):
            cp.wait_send()
            cp.wait_recv()

    outs = pl.pallas_call(
        body, name=name, in_specs=[HBM_SPEC] * (2 * n) + [SEM_SPEC, SEM_SPEC, pl.BlockSpec(memory_space=pl.ANY)],
        out_specs=[HBM_SPEC] * (2 * n), out_shape=_hbm_like(srcs) + _hbm_like(lands),
        input_output_aliases={i: i for i in range(2 * n)},
        compiler_params=pltpu.CompilerParams(has_side_effects=IN_FLIGHT),
    )(*srcs, *lands, send_sems, recv_sems, after)
    return outs[:n], outs[n:]


GATHER_LATE_SEMS = 2 * (NCHIP - 1)


def _gather_async_start(tag, shards, after):
    def start(srcs, lands, copy):
        k, c, chips, _ = _peers()
        out = []
        for a, (src, dst) in enumerate(zip(srcs, lands)):
            for j, (cx, cy) in enumerate(chips):
                for core in range(2):
                    out.append(copy(GATHER_LATE_SEMS * a + 2 * j + core, GATHER_LATE_SEMS * a + 2 * j + c,
                                    src.at[*_half(c, src.shape)], dst.at[k, *_half(c, src.shape)], (cx, cy, core)))
        return out

    chip = 2 * lax.axis_index("x") + lax.axis_index("y")
    lands = [lax.dynamic_update_slice(lax.empty((NCHIP,) + s.shape, s.dtype), s[None], (chip, 0, 0)) for s in shards]
    return _split_start("gather_%s_start" % tag, shards, lands, after, GATHER_LATE_SEMS * len(shards), start)


def _gather_async_wait(tag, send_sems, recv_sems, shards, lands, after, first=0):
    def waits(srcs, lands_, copy):
        _, c, chips, _ = _peers()
        out = []
        for a, (src, dst) in enumerate(zip(srcs, lands_)):
            for j, (cx, cy) in enumerate(chips):
                for core in range(2):
                    idx = GATHER_LATE_SEMS * (first + a) + 2 * j + core
                    out.append(copy(idx, idx, src.at[*_half(c, src.shape)], dst.at[2 * cx + cy, *_half(core, src.shape)], (cx, cy, core)))
        return out

    return _split_wait("gather_%s_wait" % tag, send_sems, recv_sems, shards, lands, after, waits)[1]


def _other_devices():
    x, y, c = _coords()
    out = []
    for d in range(1, NDEV):
        tx, ty, tc = x ^ (d >> 2), y ^ ((d >> 1) & 1), c ^ (d & 1)
        out.append((d, (tx, ty, tc), 2 * tx + ty, 4 * tx + 2 * ty + tc))
    return out


def _reduce_async_start(tag, stacks, after):
    def start(srcs, lands, copy):
        x, y, c = _coords()
        me = 4 * x + 2 * y + c
        return [copy((NDEV - 1) * a + d - 1, (NDEV - 1) * a + d - 1, src.at[chip, *_half(to[2], src.shape)], dst.at[me], to)
                for a, (src, dst) in enumerate(zip(srcs, lands)) for d, to, chip, _ in _other_devices()]

    x, y, c = _coords()
    lands = []
    for s in stacks:
        hr, hc = _half_shape(s.shape)
        at = (c * hr, 0) if _half_axis(s.shape) == 0 else (0, c * hc)
        own = lax.dynamic_slice(s, (2 * x + y,) + at, (1, hr, hc))
        lands.append(lax.dynamic_update_slice(lax.empty((NDEV, hr, hc), s.dtype), own, (4 * x + 2 * y + c, 0, 0)))
    return _split_start("reduce_%s_start" % tag, stacks, lands, after, (NDEV - 1) * len(stacks), start)


def _reduce_async_wait(tag, send_sems, recv_sems, stacks, lands, after):
    def waits(srcs, lands_, copy):
        return [copy((NDEV - 1) * a + d - 1, (NDEV - 1) * a + d - 1, src.at[chip, *_half(to[2], src.shape)], dst.at[pos], to)
                for a, (src, dst) in enumerate(zip(srcs, lands_)) for d, to, chip, pos in _other_devices()]

    return _split_wait("reduce_%s_wait" % tag, send_sems, recv_sems, stacks, lands, after, waits)[1]


def _reduce_finish(tag, arrived, dims):
    n_arr = len(arrived)

    def body(*refs):
        lands, fin = refs[:n_arr], refs[n_arr:2 * n_arr]
        send_sems, recv_sems = refs[2 * n_arr:]
        _, c, _, sibling = _peers()
        sends = []
        for a in range(n_arr):
            mine = fin[a].at[*_half(c, dims[a])]

            def device_sum(vs, vf, a=a, mine=mine):
                pltpu.sync_copy(lands[a], vs)
                acc = vs[0].astype(F32)
                for i in range(1, NDEV):
                    acc = acc + vs[i].astype(F32)
                vf[...] = acc
                pltpu.sync_copy(vf, mine)

            pl.run_scoped(device_sum, pltpu.VMEM((NDEV,) + _half_shape(dims[a]), BF16), pltpu.VMEM(_half_shape(dims[a]), F32))
            sends.append(pltpu.make_async_remote_copy(src_ref=mine, dst_ref=mine, send_sem=send_sems.at[a], recv_sem=recv_sems.at[a],
                                                      device_id=sibling, device_id_type=MESH))
            sends[-1].start()
        for a in range(n_arr):
            other = fin[a].at[*_half(1 - c, dims[a])]
            pltpu.make_async_remote_copy(src_ref=other, dst_ref=other, send_sem=send_sems.at[a], recv_sem=recv_sems.at[a],
                                         device_id=sibling, device_id_type=MESH).wait_recv()
        for cp in sends:
            cp.wait_send()

    any_spec = pl.BlockSpec(memory_space=pl.ANY)
    return pl.pallas_call(
        body, name="reduce_%s_finish" % tag, in_specs=[any_spec] * n_arr, out_specs=[any_spec] * n_arr,
        out_shape=[jax.ShapeDtypeStruct(d, F32) for d in dims],
        scratch_shapes=[pltpu.SemaphoreType.DMA((n_arr,)), pltpu.SemaphoreType.DMA((n_arr,))],
    )(*arrived)


SMALL = (("conv_w", SSD_K * SSD_XBC), ("conv_b", SSD_XBC), ("dt_bias", H), ("A_log", H), ("D", H), ("ssd_norm", SSD_INNER),
         ("q_norm", Q_RANK), ("kv_norm", KV_RANK), ("out_norm", SSD_INNER), ("ln_mix_g", D), ("ln_mix_b", D),
         ("ln_ffn_g", D), ("ln_ffn_b", D))
SMALL_ROWS = 120
NDEV = 8


def _allreduce_small_start(sv):
    def start(srcs, lands, copy):
        x, y, c = _coords()
        return [copy(d - 1, d - 1, srcs[0], lands[0].at[4 * x + 2 * y + c], to) for d, to, _, _ in _other_devices()]

    x, y, c = _coords()
    slots = lax.dynamic_update_slice(lax.empty((NDEV,) + sv.shape, sv.dtype), sv[None], (4 * x + 2 * y + c, 0, 0))
    return _split_start("allreduce_small_start", [sv], [slots], None, NDEV - 1, start)


def _allreduce_small_wait(send_sems, recv_sems, srcs, lands, after):
    def waits(srcs_, lands_, copy):
        return [copy(d - 1, d - 1, srcs_[0], lands_[0].at[pos], to) for d, to, _, pos in _other_devices()]

    def device_sum(slots_ref, out_ref):
        acc = slots_ref[0]
        for i in range(1, NDEV):
            acc = acc + slots_ref[i]
        out_ref[...] = acc

    slots = _split_wait("allreduce_small_wait", send_sems, recv_sems, srcs, lands, after, waits)[1][0]
    vm = pl.BlockSpec(memory_space=pltpu.VMEM)
    return pl.pallas_call(device_sum, name="allreduce_small_sum", in_specs=[vm], out_specs=vm,
                          out_shape=jax.ShapeDtypeStruct(slots.shape[1:], slots.dtype))(slots)


def _adamw_math(w, g, m, v):
    m2 = ADAM_B1 * m + (1.0 - ADAM_B1) * g
    v2 = ADAM_B2 * v + (1.0 - ADAM_B2) * (g * g)
    m_hat = m2 / (1.0 - ADAM_B1 ** ADAM_STEP)
    v_hat = v2 / (1.0 - ADAM_B2 ** ADAM_STEP)
    return -ADAM_LR * (m_hat / (jnp.sqrt(v_hat) + ADAM_EPS) + ADAM_WD * w), m2, v2


ADAM_BLOCK_BYTES = 2 * 1024 * 1024


def _adamw_big(w, g, m, v, *, name):
    r, c = w.shape

    def body(w_ref, g_ref, m_ref, v_ref, d_ref, m2_ref, v2_ref):
        d_ref[...], m2_ref[...], v2_ref[...] = _adamw_math(w_ref[...], g_ref[...], m_ref[...], v_ref[...])

    tr = max(t for t in range(8, r + 1, 8) if r % t == 0 and t * c * 4 <= ADAM_BLOCK_BYTES)
    steps, spec = r // tr, pl.BlockSpec((tr, c), lambda i: (i, 0))
    return pl.pallas_call(body, name=name, grid=(steps,), in_specs=[spec] * 4, out_specs=[spec] * 3,
                          out_shape=[jax.ShapeDtypeStruct((r, c), F32)] * 3)(w, g, m, v)


def _adamw_small(ws, gs, ms, vs):
    n = len(ws)

    def body(*refs):
        for i in range(n):
            w_ref, g_ref, m_ref, v_ref = (refs[j * n + i] for j in range(4))
            d_ref, m2_ref, v2_ref = (refs[(4 + j) * n + i] for j in range(3))
            d_ref[...], m2_ref[...], v2_ref[...] = _adamw_math(w_ref[...], g_ref[...], m_ref[...], v_ref[...])

    vm = pl.BlockSpec(memory_space=pltpu.VMEM)
    shapes = [jax.ShapeDtypeStruct(w.shape, F32) for w in ws]
    outs = pl.pallas_call(body, name="adamw_small", in_specs=[vm] * (4 * n), out_specs=[vm] * (3 * n), out_shape=shapes * 3)(
        *ws, *gs, *ms, *vs)
    return outs[:n], outs[n:2 * n], outs[2 * n:]


_SMALL_ARG = {"conv_w": "ssd_conv_w", "conv_b": "ssd_conv_b", "dt_bias": "ssd_dt_bias", "A_log": "ssd_A_log", "D": "ssd_D",
              "ssd_norm": "ssd_norm_w", "q_norm": "mla_q_norm_w", "kv_norm": "mla_kv_norm_w", "out_norm": "mla_out_norm_w",
              "ln_mix_g": "ln_mix_g", "ln_mix_b": "ln_mix_b", "ln_ffn_g": "ln_ffn_g", "ln_ffn_b": "ln_ffn_b"}
_BIG_ARG = {"w_in": "w_in", "w_qb": "mla_w_q_b", "w_kvb": "mla_w_kv_b", "w_out": "w_out", "w_gate": "w_ffn_gate",
            "w_up": "w_ffn_up", "w_down": "w_ffn_down", "w_pg": "w_ple_gate", "w_pp": "w_ple_proj"}
_WEIGHT_ORDER = ("w_in", "ssd_conv_w", "ssd_conv_b", "ssd_dt_bias", "ssd_A_log", "ssd_D", "ssd_norm_w", "mla_q_norm_w", "mla_w_q_b",
                 "mla_kv_norm_w", "mla_w_kv_b", "mla_out_norm_w", "w_out", "ln_mix_g", "ln_mix_b", "w_ffn_gate", "w_ffn_up",
                 "w_ffn_down", "w_ple_gate", "w_ple_proj", "ln_ffn_g", "ln_ffn_b")


def _rows128(a):
    flat = a.reshape(-1)
    return jnp.pad(flat, (0, -flat.shape[0] % LANE)).reshape(-1, LANE)


def kernel(x, p, positions, w_in, ssd_conv_w, ssd_conv_b, ssd_dt_bias, ssd_A_log, ssd_D, ssd_norm_w, mla_q_norm_w, mla_w_q_b, mla_kv_norm_w, mla_w_kv_b, mla_out_norm_w, w_out, ln_mix_g, ln_mix_b, w_ffn_gate, w_ffn_up, w_ffn_down, w_ple_gate, w_ple_proj, ln_ffn_g, ln_ffn_b, loss_target, m_w_in, m_ssd_conv_w, m_ssd_conv_b, m_ssd_dt_bias, m_ssd_A_log, m_ssd_D, m_ssd_norm_w, m_mla_q_norm_w, m_mla_w_q_b, m_mla_kv_norm_w, m_mla_w_kv_b, m_mla_out_norm_w, m_w_out, m_ln_mix_g, m_ln_mix_b, m_w_ffn_gate, m_w_ffn_up, m_w_ffn_down, m_w_ple_gate, m_w_ple_proj, m_ln_ffn_g, m_ln_ffn_b, v_w_in, v_ssd_conv_w, v_ssd_conv_b, v_ssd_dt_bias, v_ssd_A_log, v_ssd_D, v_ssd_norm_w, v_mla_q_norm_w, v_mla_w_q_b, v_mla_kv_norm_w, v_mla_w_kv_b, v_mla_out_norm_w, v_w_out, v_ln_mix_g, v_ln_mix_b, v_w_ffn_gate, v_w_ffn_up, v_w_ffn_down, v_w_ple_gate, v_w_ple_proj, v_ln_ffn_g, v_ln_ffn_b):
    given = dict(locals())
    chip = 2 * lax.axis_index("x") + lax.axis_index("y")

    def local(name, prefix=""):
        a = given[prefix + _BIG_ARG[name]][0]
        return a.T if name in TRANSPOSED else a

    def updated(name, prefix=""):
        if name in ROW_MAJOR:
            _, c, r = given[prefix + _BIG_ARG[name]].shape
            return given[prefix + _BIG_ARG[name]].reshape(c // LANE, LANE, r).transpose(2, 0, 1).reshape(-1, LANE)
        return local(name, prefix)

    def global_layout(name, arr):
        if name in ROW_MAJOR:
            r, c = local(name).shape
            return arr.reshape(r, c // LANE, LANE).transpose(1, 2, 0).reshape(1, c, r)
        return (arr.T if name in TRANSPOSED else arr)[None]

    conv_bits = lax.bitcast_convert_type(ssd_conv_w[0], BF16).reshape(SSD_K, 2 * CONV_SHARD)
    early, flying = _gather_async_start("in", [local("w_in").astype(BF16), jnp.pad(conv_bits, ((0, BF16_ROWS - SSD_K), (0, 0)))], None)
    late = [name for _, names in ASYNC_GROUPS for name in names]
    late_casts = [(local(name) + flying[0, 0]).astype(BF16) for name in late]
    w_in_all, conv_all = _gather_async_wait("in", *early, sum(c[:8, :LANE].astype(F32) for c in late_casts))
    sp = {k: given[a] for k, a in _SMALL_ARG.items() if k != "conv_w"}
    sp["conv_w"] = _from_cols(lax.bitcast_convert_type(conv_all[:, :SSD_K].reshape(NCHIP, SSD_K, CONV_SHARD, 2), F32))
    (late_send, late_recv, late_shards, late_lands), tie = _gather_async_start("late", late_casts, w_in_all)

    def fetch(group, after):
        names = dict(ASYNC_GROUPS)[group]
        first = late.index(names[0])
        mine = slice(first, first + len(names))
        return dict(zip(names, _gather_async_wait(group, late_send, late_recv, late_shards[mine], late_lands[mine], after, first)))

    reducing = {}

    def send(group, grads):
        if group == "small":
            rows = jnp.concatenate([_rows128(grads[name]) for name, _ in SMALL] + [grads["loss"]], axis=0)
            reducing[group], sent = _allreduce_small_start(jnp.pad(rows, ((0, SMALL_ROWS - rows.shape[0]), (0, 0))))
        else:
            reducing[group], sent = _reduce_async_start(group, [grads[name] for name in dict(ASYNC_GROUPS)[group]], None)
        return sent[0, 0]

    loss_row, grad_x, dw_in, g = _local_step(x[0], p[0, 0], positions[0], loss_target[0], w_in_all, fetch, send, sp, tie[0, 0])

    reducing["in"], tie = _reduce_async_start("in", [dw_in], grad_x)
    gbig = {}
    for group, names in reversed(ASYNC_GROUPS):
        arrived = _reduce_async_wait(group, *reducing[group], tie)
        gbig.update(zip(names, _reduce_finish(group, arrived, [local(name).shape for name in names])))
    small_sum = _allreduce_small_wait(*reducing.pop("small"), tie)
    gsmall, row = {}, 0
    for name, size in SMALL:
        nrow = -(-size // LANE)
        gsmall[name] = small_sum[row:row + nrow].reshape(-1)[:size]
        row += nrow
    loss = small_sum[row, 0]

    grads = {_BIG_ARG[name]: global_layout(name, arr) for name, arr in gbig.items()}
    for name, _ in SMALL:
        if name == "conv_w":
            full_g = gsmall[name].reshape(SSD_K, SSD_XBC)
            grads["ssd_conv_w"] = lax.dynamic_slice(full_g, (0, chip * CONV_SHARD), (SSD_K, CONV_SHARD))[None]
        else:
            grads[_SMALL_ARG[name]] = gsmall[name].reshape(given[_SMALL_ARG[name]].shape)

    delta, new_m, new_v = {}, {}, {}

    def update_matrix(name, grad):
        a = _BIG_ARG[name]
        d, m2, v2 = _adamw_big(updated(name), grad, updated(name, "m_"), updated(name, "v_"), name="adamw_" + a)
        delta[a], new_m[a], new_v[a] = (global_layout(name, t) for t in (d, m2, v2))
        return d

    all_updated = sum(update_matrix(name, grad)[:8, :LANE] for name, grad in gbig.items())
    g_in = _reduce_finish("in", _reduce_async_wait("in", *reducing["in"], all_updated), [updated("w_in").shape])[0]
    grads["w_in"] = global_layout("w_in", g_in)
    update_matrix("w_in", g_in)
    small_names = [_SMALL_ARG[name] for name, _ in SMALL]
    two_d = lambda t: t.reshape(t.shape[-2], t.shape[-1])
    ds, ms, vs = _adamw_small([two_d(given[a]) for a in small_names], [two_d(grads[a]) for a in small_names],
                              [two_d(given["m_" + a]) for a in small_names], [two_d(given["v_" + a]) for a in small_names])
    for a, d, m2, v2 in zip(small_names, ds, ms, vs):
        delta[a], new_m[a], new_v[a] = (t.reshape(given[a].shape) for t in (d, m2, v2))

    return (loss, grad_x[None], *[grads[n] for n in _WEIGHT_ORDER], *[delta[n] for n in _WEIGHT_ORDER],
            *[new_m[n] for n in _WEIGHT_ORDER], *[new_v[n] for n in _WEIGHT_ORDER])
```

```python
import functools
import math

import jax
import jax.numpy as jnp
from jax import lax
from jax.experimental import pallas as pl
from jax.experimental.pallas import tpu as pltpu

F32 = jnp.float32
BF16 = jnp.bfloat16

S = 2048
D = 1024
PLE = 256
H = 16
SSD_P = 64
SSD_INNER = 1024
SSD_N = 128
SSD_G = 2
SSD_L = 128
SSD_NC = S // SSD_L
SSD_XBC = 1536
SSD_K = 4
Q_RANK = 384
KV_RANK = 256
NOPE = 64
ROPE = 32
VDIM = 64
D_FF = 2816
IN_WIDTH = 3248
ALPHA = 2.0 ** 0.25
EPS_RMS = 1e-6
EPS_LN = 1e-5
ATT_SCALE = 1.0 / math.sqrt(NOPE + ROPE)
LN2 = math.log(2.0)
ATT_SCALE_LOG2 = ATT_SCALE / LN2
LANE = 128
NCHIP = 4
SMALL_W = 896
SM_Q, SM_KV, SM_KR, SM_DT = 0, 384, 640, 768
NEG = -1e30

ADAM_LR = 0.001
ADAM_B1 = 0.9
ADAM_B2 = 0.999
ADAM_EPS = 1e-08
ADAM_WD = 0.01
ADAM_STEP = 10


def _sigmoid(v):
    return 1.0 / (1.0 + jnp.exp(-v))


MM_VMEM_BUDGET = 36 * 2 ** 20
MM_MAX_ACC = 2048 * 1024


def _mm_tiles(pairs, ks, m, n, out_dtype, has_add):
    def divs(v):
        return [LANE * d for d in range(v // LANE, 0, -1) if (v // LANE) % d == 0] if v % LANE == 0 else [v]

    def cost(tm, tn):
        tot = tm * tn * (jnp.dtype(out_dtype).itemsize + (4 if has_add else 0))
        for (a, b), k in zip(pairs, ks):
            tot += k * (tm * a.dtype.itemsize + tn * b.dtype.itemsize)
        return 2 * tot

    ok = [(tm * tn, tm, tn) for tm in divs(m) for tn in divs(n) if tm * tn <= MM_MAX_ACC and cost(tm, tn) <= MM_VMEM_BUDGET]
    _, tm, tn = max(ok)
    return tm, tn


def _mm(pairs, *, ta=False, tb=False, out_dtype=F32, add=None, add_scale=1.0, chunk=None, name):
    n_pairs = len(pairs)
    windows = [pr[2] if len(pr) == 3 else None for pr in pairs]
    pairs = [pr[:2] for pr in pairs]
    assert not ((ta or tb) and any(windows))
    ks = [w[2] if w else (a.shape[-2] if ta else a.shape[-1]) for (a, _), w in zip(pairs, windows)]
    a0, b0 = pairs[0]
    m = a0.shape[-1] if ta else a0.shape[-2]
    n = b0.shape[-2] if tb else b0.shape[-1]
    tm, tn = _mm_tiles(pairs, ks, m, n, out_dtype, add is not None)
    dims = (((0 if ta else 1,), (1 if tb else 0,)), ((), ()))
    nk = NCHIP if chunk else 1
    assert chunk != "sum" or out_dtype == F32
    flat = [i for i, (a, b) in enumerate(pairs) if a.ndim == 2 and b.ndim == 2]
    stacked = [i for i in range(n_pairs) if i not in flat]

    def body(*refs):
        o_ref = refs[-1]

        def products(which):
            acc = None
            for i in which:
                a = refs[2 * i][...].astype(BF16)
                b = refs[2 * i + 1][...].astype(BF16)
                part = lax.dot_general(a, b, dims, preferred_element_type=F32)
                acc = part if acc is None else acc + part
            return acc

        if chunk == "sum":
            k = pl.program_id(2)
            acc = products(stacked)

            @pl.when(k == 0)
            def _():
                first = acc + products(flat) if flat else acc
                o_ref[...] = first + add_scale * refs[2 * n_pairs][...] if add is not None else first

            @pl.when(k > 0)
            def _():
                o_ref[...] += acc
            return
        acc = products(range(n_pairs))
        if add is not None:
            acc = acc + add_scale * refs[2 * n_pairs][...]
        o_ref[...] = acc.astype(out_dtype)

    def spec(arr, shape, idx2):
        if arr.ndim == 3:
            return pl.BlockSpec((None,) + shape, lambda i, j, k: (k,) + idx2(i, j))
        return pl.BlockSpec(shape, lambda i, j, k: idx2(i, j))

    in_specs, args = [], []
    for (a, b), kdim, window in zip(pairs, ks, windows):
        ka, kb = window[:2] if window else (0, 0)
        in_specs.append(spec(a, (kdim, tm), lambda i, j: (0, i)) if ta else spec(a, (tm, kdim), lambda i, j, ka=ka: (i, ka)))
        in_specs.append(spec(b, (tn, kdim), lambda i, j: (j, 0)) if tb else spec(b, (kdim, tn), lambda i, j, kb=kb: (kb, j)))
        args += [a, b]
    if add is not None:
        in_specs.append(pl.BlockSpec((tm, tn), lambda i, j, k: (i, j)))
        args.append(add)
    if chunk == "out":
        out_spec = pl.BlockSpec((None, tm, tn), lambda i, j, k: (k, i, j))
        out_shape = jax.ShapeDtypeStruct((nk, m, n), out_dtype)
    else:
        out_spec = pl.BlockSpec((tm, tn), lambda i, j, k: (i, j))
        out_shape = jax.ShapeDtypeStruct((m, n), out_dtype)
    return pl.pallas_call(
        body, name=name, grid=(m // tm, n // tn, nk), in_specs=in_specs, out_specs=out_spec, out_shape=out_shape,
        compiler_params=pltpu.CompilerParams(dimension_semantics=("parallel", "parallel", "arbitrary")),
    )(*args)


TR = 256


def _row_spec(c):
    return pl.BlockSpec((TR, c), lambda i: (i, 0))


def _vec_spec(c):
    return pl.BlockSpec((1, c), lambda i: (0, 0))


def _acc_rows(ref, val):
    @pl.when(pl.program_id(0) == 0)
    def _():
        ref[...] = jnp.zeros_like(ref)
    ref[...] += val


def _rms_fwd(u, w, *, name):
    c = u.shape[1]

    def body(u_ref, w_ref, o_ref):
        v = u_ref[...]
        r = lax.rsqrt(jnp.mean(v * v, axis=-1, keepdims=True) + EPS_RMS)
        o_ref[...] = (v * r * w_ref[...]).astype(BF16)

    return pl.pallas_call(body, name=name, grid=(S // TR,), in_specs=[_row_spec(c), _vec_spec(c)], out_specs=_row_spec(c),
                          out_shape=jax.ShapeDtypeStruct((S, c), BF16))(u, w)


def _rms_bwd(u, w, dy, *, name):
    c = u.shape[1]

    def body(u_ref, w_ref, dy_ref, du_ref, dw_ref):
        v = u_ref[...]
        g = dy_ref[...].astype(F32)
        r = lax.rsqrt(jnp.mean(v * v, axis=-1, keepdims=True) + EPS_RMS)
        gw = g * w_ref[...]
        du_ref[...] = r * gw - v * (r * r * r * jnp.mean(gw * v, axis=-1, keepdims=True))
        _acc_rows(dw_ref, jnp.sum(g * v * r, axis=0, keepdims=True))

    return pl.pallas_call(body, name=name, grid=(S // TR,), in_specs=[_row_spec(c), _vec_spec(c), _row_spec(c)],
                          out_specs=[_row_spec(c), _vec_spec(c)],
                          out_shape=[jax.ShapeDtypeStruct((S, c), F32), jax.ShapeDtypeStruct((1, c), F32)])(u, w, dy)


def _gate_norm_fwd(y, z, w):
    def body(y_ref, z_ref, w_ref, o_ref):
        zz = z_ref[...]
        v = y_ref[...] * (zz * _sigmoid(zz))
        r = lax.rsqrt(jnp.mean(v * v, axis=-1, keepdims=True) + EPS_RMS)
        o_ref[...] = (v * r * w_ref[...]).astype(BF16)

    c = SSD_INNER
    return pl.pallas_call(body, name="ssd_gate_norm_fwd", grid=(S // TR,), in_specs=[_row_spec(c), _row_spec(c), _vec_spec(c)],
                          out_specs=_row_spec(c), out_shape=jax.ShapeDtypeStruct((S, c), BF16))(y, z, w)


def _gate_norm_bwd(y, z, w, dout):
    def body(y_ref, z_ref, w_ref, g_ref, dy_ref, dz_ref, dw_ref):
        yy = y_ref[...]
        zz = z_ref[...]
        sg = _sigmoid(zz)
        sz = zz * sg
        v = yy * sz
        g = g_ref[...]
        r = lax.rsqrt(jnp.mean(v * v, axis=-1, keepdims=True) + EPS_RMS)
        gw = g * w_ref[...]
        dv = r * gw - v * (r * r * r * jnp.mean(gw * v, axis=-1, keepdims=True))
        dy_ref[...] = dv * sz
        dz_ref[...] = (dv * yy * (sg * (1.0 + zz * (1.0 - sg)))).astype(BF16)
        _acc_rows(dw_ref, jnp.sum(g * v * r, axis=0, keepdims=True))

    c = SSD_INNER
    return pl.pallas_call(body, name="ssd_gate_norm_bwd", grid=(S // TR,),
                          in_specs=[_row_spec(c), _row_spec(c), _vec_spec(c), _row_spec(c)],
                          out_specs=[_row_spec(c), _row_spec(c), _vec_spec(c)],
                          out_shape=[jax.ShapeDtypeStruct((S, c), F32), jax.ShapeDtypeStruct((S, c), BF16),
                                     jax.ShapeDtypeStruct((1, c), F32)])(y, z, w, dout)


MIX_ROWS = 512


def _out_proj_ln(y_ssd, y_mla, w_out, xr, g, b):
    k = y_ssd.shape[1]

    def body(ys_ref, ym_ref, w_ref, x_ref, g_ref, b_ref, m_ref, o_ref, ob_ref):
        mix = (jnp.dot(ys_ref[...], w_ref[:k], preferred_element_type=F32)
               + jnp.dot(ym_ref[...], w_ref[k:], preferred_element_type=F32))
        m_ref[...] = mix
        pre = ALPHA * x_ref[...] + mix
        mu = jnp.mean(pre, axis=-1, keepdims=True)
        d = pre - mu
        rs = lax.rsqrt(jnp.mean(d * d, axis=-1, keepdims=True) + EPS_LN)
        h = d * rs * g_ref[...] + b_ref[...]
        o_ref[...] = h
        ob_ref[...] = h.astype(BF16)

    rows = lambda c: pl.BlockSpec((MIX_ROWS, c), lambda i: (i, 0))
    return pl.pallas_call(
        body, name="out_proj_ln", grid=(S // MIX_ROWS,),
        in_specs=[rows(k), rows(k), _whole(w_out), rows(D), _vec_spec(D), _vec_spec(D)], out_specs=[rows(D)] * 3,
        out_shape=[jax.ShapeDtypeStruct((S, D), F32), jax.ShapeDtypeStruct((S, D), F32), jax.ShapeDtypeStruct((S, D), BF16)],
    )(y_ssd, y_mla, w_out, xr, g, b)


def _ln_bwd(xr, mix, g, dh, w_out):
    k = w_out.shape[0] // 2

    def body(x_ref, m_ref, g_ref, dh_ref, w_ref, dpre_ref, dpreb_ref, dg_ref, db_ref, dys_ref, dym_ref):
        pre = ALPHA * x_ref[...] + m_ref[...]
        mu = jnp.mean(pre, axis=-1, keepdims=True)
        d = pre - mu
        rs = lax.rsqrt(jnp.mean(d * d, axis=-1, keepdims=True) + EPS_LN)
        xh = d * rs
        dy = dh_ref[...]
        gy = dy * g_ref[...]
        dpre = rs * (gy - jnp.mean(gy, axis=-1, keepdims=True) - xh * jnp.mean(gy * xh, axis=-1, keepdims=True))
        dpre_ref[...] = dpre
        dpreb = dpre.astype(BF16)
        dpreb_ref[...] = dpreb
        _acc_rows(dg_ref, jnp.sum(dy * xh, axis=0, keepdims=True))
        _acc_rows(db_ref, jnp.sum(dy, axis=0, keepdims=True))
        dys_ref[...] = lax.dot_general(dpreb, w_ref[:k], (((1,), (1,)), ((), ())), preferred_element_type=F32)
        dym_ref[...] = lax.dot_general(dpreb, w_ref[k:], (((1,), (1,)), ((), ())), preferred_element_type=F32)

    rows = lambda c: pl.BlockSpec((MIX_ROWS, c), lambda i: (i, 0))
    return pl.pallas_call(
        body, name="ln_mix_bwd", grid=(S // MIX_ROWS,),
        in_specs=[rows(D), rows(D), _vec_spec(D), rows(D), _whole(w_out)],
        out_specs=[rows(D), rows(D), _vec_spec(D), _vec_spec(D), rows(k), rows(k)],
        out_shape=[jax.ShapeDtypeStruct((S, D), F32), jax.ShapeDtypeStruct((S, D), BF16), jax.ShapeDtypeStruct((1, D), F32),
                   jax.ShapeDtypeStruct((1, D), F32), jax.ShapeDtypeStruct((S, k), F32), jax.ShapeDtypeStruct((S, k), F32)],
    )(xr, mix, g, dh, w_out)


FF_CHUNK = D_FF // NCHIP


FF_ROWS = 1024


def _ff_act_spec():
    return pl.BlockSpec((None, FF_ROWS, FF_CHUNK), lambda i, k: (k, i, 0))


def _ff_w_spec():
    return pl.BlockSpec((None, FF_CHUNK, D), lambda i, k: (k, 0, 0))


def _ffn_hidden_fwd(h, w_gate_t, w_up_t):
    def body(h_ref, wg_ref, wu_ref, g_ref, u_ref, a_ref):
        hh = h_ref[...]
        g = _dot(hh, wg_ref[...], ((1,), (1,)))
        u = _dot(hh, wu_ref[...], ((1,), (1,)))
        g_ref[...] = g.astype(BF16)
        u_ref[...] = u.astype(BF16)
        a_ref[...] = (g * _sigmoid(g) * u).astype(BF16)

    return pl.pallas_call(
        body, name="ffn_hidden_fwd", grid=(S // FF_ROWS, NCHIP),
        in_specs=[pl.BlockSpec((FF_ROWS, D), lambda i, k: (i, 0)), _ff_w_spec(), _ff_w_spec()], out_specs=[_ff_act_spec()] * 3,
        out_shape=[jax.ShapeDtypeStruct((NCHIP, S, FF_CHUNK), BF16)] * 3,
        compiler_params=pltpu.CompilerParams(dimension_semantics=("parallel", "parallel")),
    )(h, w_gate_t, w_up_t)


def _ffn_hidden_bwd(dout, w_down, gate, up):
    def body(d_ref, wd_ref, g_ref, u_ref, dg_ref, du_ref):
        d = _dot(d_ref[...], wd_ref[...], ((1,), (1,)))
        g = g_ref[...].astype(F32)
        sg = _sigmoid(g)
        dg_ref[...] = (d * u_ref[...].astype(F32) * (sg * (1.0 + g * (1.0 - sg)))).astype(BF16)
        du_ref[...] = (d * g * sg).astype(BF16)

    return pl.pallas_call(
        body, name="ffn_hidden_bwd", grid=(S // FF_ROWS, NCHIP),
        in_specs=[pl.BlockSpec((FF_ROWS, D), lambda i, k: (i, 0)), _ff_w_spec(), _ff_act_spec(), _ff_act_spec()],
        out_specs=[_ff_act_spec()] * 2, out_shape=[jax.ShapeDtypeStruct((NCHIP, S, FF_CHUNK), BF16)] * 2,
        compiler_params=pltpu.CompilerParams(dimension_semantics=("parallel", "parallel")),
    )(dout, w_down, gate, up)


def _final_fwd_bwd(h1, ffn, h1b, pb, w_pg, w_pp, target, g2, b2):
    def body(h_ref, f_ref, hb_ref, pb_ref, wpg_ref, wpp_ref, t_ref, g_ref, b_ref,
             dpre_ref, dpreb_ref, dpg_ref, dpp_ref, dg_ref, db_ref, loss_ref):
        sg = _sigmoid(jnp.dot(hb_ref[...], wpg_ref[...], preferred_element_type=F32))
        ppv = jnp.dot(pb_ref[...], wpp_ref[...], preferred_element_type=F32)
        pre = ALPHA * h_ref[...] + f_ref[...] + sg * ppv
        mu = jnp.mean(pre, axis=-1, keepdims=True)
        d = pre - mu
        rs = lax.rsqrt(jnp.mean(d * d, axis=-1, keepdims=True) + EPS_LN)
        xh = d * rs
        err = xh * g_ref[...] + b_ref[...] - t_ref[...]
        dy = err * (1.0 / D)
        gy = dy * g_ref[...]
        dpre = rs * (gy - jnp.mean(gy, axis=-1, keepdims=True) - xh * jnp.mean(gy * xh, axis=-1, keepdims=True))
        dpre_ref[...] = dpre
        dpreb_ref[...] = dpre.astype(BF16)
        dpg_ref[...] = (dpre * ppv * sg * (1.0 - sg)).astype(BF16)
        dpp_ref[...] = (dpre * sg).astype(BF16)
        _acc_rows(dg_ref, jnp.sum(dy * xh, axis=0, keepdims=True))
        _acc_rows(db_ref, jnp.sum(dy, axis=0, keepdims=True))
        _acc_rows(loss_ref, 0.5 * jnp.sum(jnp.mean(err * err, axis=-1, keepdims=True), axis=0, keepdims=True) * jnp.ones((1, LANE), F32))

    return pl.pallas_call(
        body, name="final_ln_loss", grid=(S // TR,),
        in_specs=[_row_spec(D)] * 3 + [_row_spec(pb.shape[1]), _whole(w_pg), _whole(w_pp), _row_spec(D)] + [_vec_spec(D)] * 2,
        out_specs=[_row_spec(D)] * 4 + [_vec_spec(D), _vec_spec(D), _vec_spec(LANE)],
        out_shape=[jax.ShapeDtypeStruct((S, D), F32)] + [jax.ShapeDtypeStruct((S, D), BF16)] * 3 + [
                   jax.ShapeDtypeStruct((1, D), F32), jax.ShapeDtypeStruct((1, D), F32), jax.ShapeDtypeStruct((1, LANE), F32)],
    )(h1, ffn, h1b, pb, w_pg, w_pp, target, g2, b2)


def _rot(u, cos_t, sin_t, lane):
    partner = jnp.where(lane < NOPE + ROPE // 2, pltpu.roll(u, LANE - ROPE // 2, 1), pltpu.roll(u, ROPE // 2, 1))
    return u * cos_t + partner * sin_t


def _rms(v, w):
    r = lax.rsqrt(jnp.mean(v * v, axis=-1, keepdims=True) + EPS_RMS)
    return v * r * w, r


def _rms_grad(v, r, w, g):
    gw = g * w
    return r * gw - v * (r * r * r * jnp.mean(gw * v, axis=-1, keepdims=True)), jnp.sum(g * v * r, axis=0, keepdims=True)


def _whole(arr):
    return pl.BlockSpec(arr.shape, lambda i: (0,) * arr.ndim)


def _qkv_fwd(small, w_q, w_k, w_v, q_norm, kv_norm, cos_t, sin_t):
    def body(sm_ref, wq_ref, wk_ref, wv_ref, qw_ref, kw_ref, c_ref, s_ref, qn_ref, kvn_ref, q_ref, k_ref, kt_ref, v_ref):
        lane = lax.broadcasted_iota(jnp.int32, (TR, LANE), 1)
        c, s = c_ref[...], s_ref[...]
        qn = _rms(sm_ref[:, SM_Q:SM_Q + Q_RANK], qw_ref[...])[0].astype(BF16)
        kvn = _rms(sm_ref[:, SM_KV:SM_KV + KV_RANK], kw_ref[...])[0].astype(BF16)
        qn_ref[...] = qn
        kvn_ref[...] = kvn
        kr = _rot(pltpu.roll(sm_ref[:, SM_KR:SM_KR + LANE], NOPE, 1), c, s, lane)
        for h in range(H):
            tile = slice(h * LANE, (h + 1) * LANE)
            q_ref[:, tile] = _rot(_dot(qn, wq_ref[:, tile], ((1,), (0,))), c, s, lane).astype(BF16)
            kt = _dot(kvn, wk_ref[:, tile], ((1,), (0,))) + kr
            k_ref[:, tile] = kt.astype(BF16)
            kt_ref[tile, :] = kt.T.astype(BF16)
        v_ref[...] = _dot(kvn, wv_ref[...], ((1,), (0,))).astype(BF16)

    w = H * LANE
    return pl.pallas_call(
        body, name="qkv_fwd", grid=(S // TR,),
        in_specs=[_row_spec(SMALL_W), _whole(w_q), _whole(w_k), _whole(w_v), _vec_spec(Q_RANK), _vec_spec(KV_RANK), _row_spec(LANE), _row_spec(LANE)],
        out_specs=[_row_spec(Q_RANK), _row_spec(KV_RANK), _row_spec(w), _row_spec(w), pl.BlockSpec((w, TR), lambda i: (0, i)),
                   _row_spec(H * VDIM)],
        out_shape=[jax.ShapeDtypeStruct((S, Q_RANK), BF16), jax.ShapeDtypeStruct((S, KV_RANK), BF16), jax.ShapeDtypeStruct((S, w), BF16),
                   jax.ShapeDtypeStruct((S, w), BF16), jax.ShapeDtypeStruct((w, S), BF16), jax.ShapeDtypeStruct((S, H * VDIM), BF16)],
    )(small, w_q, w_k, w_v, q_norm, kv_norm, cos_t, sin_t)


def _qkv_bwd(dqt, dk, dv, small, w_q, w_k, w_v, q_norm, kv_norm, cos_t, sin_t):
    def body(dq_ref, dk_ref, dv_ref, sm_ref, wq_ref, wk_ref, wv_ref, qw_ref, kw_ref, c_ref, s_ref,
             ds_ref, dql_ref, dkb_ref, dqw_ref, dkw_ref):
        lane = lax.broadcasted_iota(jnp.int32, (TR, LANE), 1)
        c, s = c_ref[...], -s_ref[...]
        dqn = jnp.zeros((TR, Q_RANK), F32)
        dkvn = _dot(dv_ref[...], wv_ref[...], ((1,), (1,)))
        dkr = jnp.zeros((TR, LANE), F32)
        for h in range(H):
            tile = slice(h * LANE, (h + 1) * LANE)
            dql = _rot(dq_ref[tile, :].T, c, s, lane).astype(BF16)
            dql_ref[:, tile] = dql
            dqn = dqn + _dot(dql, wq_ref[:, tile], ((1,), (1,)))
            dkt = dk_ref[:, tile]
            dkb_ref[:, tile] = dkt.astype(BF16)
            dkvn = dkvn + _dot(dkt, wk_ref[:, tile], ((1,), (1,)))
            dkr = dkr + dkt
        dkr = jnp.where((lane >= NOPE) & (lane < NOPE + ROPE), dkr, 0.0)
        q_c, kv_c = sm_ref[:, SM_Q:SM_Q + Q_RANK], sm_ref[:, SM_KV:SM_KV + KV_RANK]
        dq_c, dqw = _rms_grad(q_c, _rms(q_c, qw_ref[...])[1], qw_ref[...], dqn)
        dkv_c, dkw = _rms_grad(kv_c, _rms(kv_c, kw_ref[...])[1], kw_ref[...], dkvn)
        ds_ref[:, SM_Q:SM_Q + Q_RANK] = dq_c.astype(BF16)
        ds_ref[:, SM_KV:SM_KV + KV_RANK] = dkv_c.astype(BF16)
        ds_ref[:, SM_KR:SM_KR + LANE] = pltpu.roll(_rot(dkr, c, s, lane), LANE - NOPE, 1).astype(BF16)
        _acc_rows(dqw_ref, dqw)
        _acc_rows(dkw_ref, dkw)

    w = H * LANE
    return pl.pallas_call(
        body, name="qkv_bwd", grid=(S // TR,),
        in_specs=[pl.BlockSpec((w, TR), lambda i: (0, i)), _row_spec(w), _row_spec(H * VDIM), _row_spec(SMALL_W), _whole(w_q), _whole(w_k),
                  _whole(w_v), _vec_spec(Q_RANK), _vec_spec(KV_RANK), _row_spec(LANE), _row_spec(LANE)],
        out_specs=[_row_spec(SM_DT), _row_spec(w), _row_spec(w), _vec_spec(Q_RANK), _vec_spec(KV_RANK)],
        out_shape=[jax.ShapeDtypeStruct((S, SM_DT), BF16), jax.ShapeDtypeStruct((S, w), BF16), jax.ShapeDtypeStruct((S, w), BF16),
                   jax.ShapeDtypeStruct((1, Q_RANK), F32), jax.ShapeDtypeStruct((1, KV_RANK), F32)],
    )(dqt, dk, dv, small, w_q, w_k, w_v, q_norm, kv_norm, cos_t, sin_t)


CB = 256


def _shift_down(u, k, row):
    if k == 0:
        return u
    return jnp.where(row >= k, pltpu.roll(u, k, 0), 0.0)


def _shift_up(u, k, row):
    if k == 0:
        return u
    return jnp.where(row < S - k, pltpu.roll(u, S - k, 0), 0.0)


def _conv_fwd(u, w, b):
    def body(u_ref, w_ref, b_ref, o_ref):
        row = lax.broadcasted_iota(jnp.int32, (S, CB), 0)
        uu = u_ref[...]
        acc = b_ref[...] + w_ref[SSD_K - 1:SSD_K, :] * uu
        for k in range(SSD_K - 1):
            acc = acc + w_ref[k:k + 1, :] * _shift_down(uu, SSD_K - 1 - k, row)
        o_ref[...] = acc * _sigmoid(acc)

    c = u.shape[1]
    return pl.pallas_call(
        body, name="conv_fwd", grid=(c // CB,),
        in_specs=[pl.BlockSpec((S, CB), lambda j: (0, j)), pl.BlockSpec((SSD_K, CB), lambda j: (0, j)), pl.BlockSpec((1, CB), lambda j: (0, j))],
        out_specs=pl.BlockSpec((S, CB), lambda j: (0, j)), out_shape=jax.ShapeDtypeStruct((S, c), F32),
    )(u, w, b)


def _conv_bwd(u, w, b, dact):
    def body(u_ref, w_ref, b_ref, d_ref, du_ref, dw_ref, db_ref):
        row = lax.broadcasted_iota(jnp.int32, (S, CB), 0)
        uu = u_ref[...]
        sh = [_shift_down(uu, SSD_K - 1 - k, row) for k in range(SSD_K)]
        acc = b_ref[...]
        for k in range(SSD_K):
            acc = acc + w_ref[k:k + 1, :] * sh[k]
        sg = _sigmoid(acc)
        dacc = d_ref[...] * (sg * (1.0 + acc * (1.0 - sg)))
        du = w_ref[SSD_K - 1:SSD_K, :] * dacc
        for k in range(SSD_K - 1):
            du = du + w_ref[k:k + 1, :] * _shift_up(dacc, SSD_K - 1 - k, row)
        du_ref[...] = du.astype(BF16)
        for k in range(SSD_K):
            dw_ref[k:k + 1, :] = jnp.sum(dacc * sh[k], axis=0, keepdims=True)
        db_ref[...] = jnp.sum(dacc, axis=0, keepdims=True)

    c = u.shape[1]
    col = lambda r: pl.BlockSpec((r, CB), lambda j: (0, j))
    return pl.pallas_call(
        body, name="conv_bwd", grid=(c // CB,), in_specs=[col(S), col(SSD_K), col(1), col(S)], out_specs=[col(S), col(SSD_K), col(1)],
        out_shape=[jax.ShapeDtypeStruct((S, c), BF16), jax.ShapeDtypeStruct((SSD_K, c), F32), jax.ShapeDtypeStruct((1, c), F32)],
    )(u, w, b, dact)


NPAIR = H // 2
PAIRS_PER_GROUP = NPAIR // SSD_G


def _softplus(v):
    return jnp.maximum(v, 0.0) + jnp.log(1.0 + jnp.exp(-jnp.abs(v)))


def _dot(a, b, dims):
    return lax.dot_general(a.astype(BF16), b.astype(BF16), (dims, ((), ())), preferred_element_type=F32)


def _dot2(a, sel):
    hi = a.astype(BF16)
    lo = (a - hi.astype(F32)).astype(BF16)
    dims = (((1,), (0,)), ((), ()))
    return lax.dot_general(hi, sel, dims, preferred_element_type=F32) + lax.dot_general(lo, sel, dims, preferred_element_type=F32)


def _dot3(a, b, dims, split_lhs):
    v = a if split_lhs else b
    v1 = v.astype(BF16)
    r1 = v - v1.astype(F32)
    v2 = r1.astype(BF16)
    v3 = (r1 - v2.astype(F32)).astype(BF16)
    acc = None
    for part in (v1, v2, v3):
        lhs, rhs = (part, b) if split_lhs else (a, part)
        t = lax.dot_general(lhs, rhs, (dims, ((), ())), preferred_element_type=F32)
        acc = t if acc is None else acc + t
    return acc


def _ssd_chunk_common(dt_ref, dtT_ref, prow_ref, pcol_ref):
    prow = prow_ref[...]
    pcol = pcol_ref[...]
    ri = lax.broadcasted_iota(jnp.int32, (SSD_L, SSD_L), 0)
    ci = lax.broadcasted_iota(jnp.int32, (SSD_L, SSD_L), 1)
    causal = ri >= ci
    pre_c = dt_ref[...] + prow[0:1, :]
    dtc = _softplus(pre_c)
    a_row = -jnp.exp(prow[1:2, :])
    cs_col = _dot3(causal.astype(BF16), dtc * a_row, ((1,), (0,)), False)
    dtr = _softplus(dtT_ref[...] + pcol[:, 0:1])
    a_col = -jnp.exp(pcol[:, 1:2])
    cs_row = _dot3(dtr * a_col, (ri <= ci).astype(BF16), ((1,), (0,)), True)
    return prow, causal, pre_c, dtc, a_row, cs_col, cs_row


def _ssd_fwd(act, small, dtT, prow, pcol):
    def body(x_ref, b_ref, c_ref, dt_ref, dtT_ref, prow_ref, pcol_ref, y_ref, st_ref, state):
        @pl.when(pl.program_id(0) == 0)
        def _():
            state[...] = jnp.zeros_like(state)

        prow, causal, _, dtc, _, cs_col, cs_row = _ssd_chunk_common(dt_ref, dtT_ref, prow_ref, pcol_ref)
        lo = lax.broadcasted_iota(jnp.int32, (SSD_L, LANE), 1) < SSD_P
        lo1 = lo[0:1, :]
        for g in range(SSD_G):
            bm = b_ref[:, g * SSD_N:(g + 1) * SSD_N]
            cm = c_ref[:, g * SSD_N:(g + 1) * SSD_N]
            cb = _dot(cm, bm, ((1,), (1,)))
            for qq in range(PAIRS_PER_GROUP):
                q = g * PAIRS_PER_GROUP + qq
                ha, hb = 2 * q, 2 * q + 1
                csa, csb = cs_col[:, ha:ha + 1], cs_col[:, hb:hb + 1]
                xp = x_ref[:, q * LANE:(q + 1) * LANE]
                xx = xp * jnp.where(lo, dtc[:, ha:ha + 1], dtc[:, hb:hb + 1])
                ga = cb * jnp.exp(jnp.where(causal, csa - cs_row[ha:ha + 1, :], NEG))
                gb = cb * jnp.exp(jnp.where(causal, csb - cs_row[hb:hb + 1, :], NEG))
                y = _dot(ga, jnp.where(lo, xx, 0.0), ((1,), (0,))) + _dot(gb, jnp.where(lo, 0.0, xx), ((1,), (0,)))
                s_in = state[q]
                y = y + _dot(cm, s_in, ((1,), (0,))) * jnp.where(lo, jnp.exp(csa), jnp.exp(csb))
                y = y + jnp.where(lo1, prow[2:3, ha:ha + 1], prow[2:3, hb:hb + 1]) * xp
                y_ref[:, q * LANE:(q + 1) * LANE] = y
                la, lb = csa[SSD_L - 1:SSD_L, :], csb[SSD_L - 1:SSD_L, :]
                decay = jnp.where(lo, jnp.exp(la - csa), jnp.exp(lb - csb))
                st_ref[q] = s_in
                state[q] = s_in * jnp.where(lo1, jnp.exp(la), jnp.exp(lb)) + _dot(bm, xx * decay, ((0,), (0,)))

    L = SSD_L
    return pl.pallas_call(
        body, name="ssd_fwd", grid=(SSD_NC,),
        in_specs=[pl.BlockSpec((L, SSD_INNER), lambda c: (c, 0)),
                  pl.BlockSpec((L, SSD_G * SSD_N), lambda c: (c, SSD_INNER // (SSD_G * SSD_N))),
                  pl.BlockSpec((L, SSD_G * SSD_N), lambda c: (c, SSD_INNER // (SSD_G * SSD_N) + 1)),
                  pl.BlockSpec((L, LANE), lambda c: (c, SM_DT // LANE)),
                  pl.BlockSpec((LANE, L), lambda c: (0, c)),
                  pl.BlockSpec((8, LANE), lambda c: (0, 0)), pl.BlockSpec((LANE, 8), lambda c: (0, 0))],
        out_specs=[pl.BlockSpec((L, SSD_INNER), lambda c: (c, 0)),
                   pl.BlockSpec((None, NPAIR, SSD_N, LANE), lambda c: (c, 0, 0, 0))],
        out_shape=[jax.ShapeDtypeStruct((S, SSD_INNER), F32), jax.ShapeDtypeStruct((SSD_NC, NPAIR, SSD_N, LANE), F32)],
        scratch_shapes=[pltpu.VMEM((NPAIR, SSD_N, LANE), F32)],
        compiler_params=pltpu.CompilerParams(dimension_semantics=("arbitrary",)),
    )(act, act, act, small, dtT, prow, pcol)


def _ssd_bwd(act, small, dtT, prow, pcol, states, dy):
    def body(x_ref, b_ref, c_ref, dt_ref, dtT_ref, prow_ref, pcol_ref, st_ref, dy_ref,
             dx_ref, ddt_ref, dp_ref, dstate):
        @pl.when(pl.program_id(0) == 0)
        def _():
            dstate[...] = jnp.zeros_like(dstate)
            dp_ref[...] = jnp.zeros_like(dp_ref)

        prow, causal, pre_c, dtc, a_row, cs_col, cs_row = _ssd_chunk_common(dt_ref, dtT_ref, prow_ref, pcol_ref)
        lane = lax.broadcasted_iota(jnp.int32, (SSD_L, LANE), 1)
        sub = lax.broadcasted_iota(jnp.int32, (LANE, SSD_L), 0)
        rowi = lax.broadcasted_iota(jnp.int32, (SSD_L, 1), 0)
        pick_p = lax.broadcasted_iota(jnp.int32, (LANE, LANE), 0)
        pick_l = lax.broadcasted_iota(jnp.int32, (LANE, LANE), 1)
        lo = lane < SSD_P
        lo1 = lo[0:1, :]
        dcs_c = jnp.zeros((SSD_L, LANE), F32)
        dcs_r = jnp.zeros((LANE, SSD_L), F32)
        ddt_x = jnp.zeros((SSD_L, LANE), F32)
        dd_row = jnp.zeros((1, LANE), F32)
        for g in range(SSD_G):
            bm = b_ref[:, g * SSD_N:(g + 1) * SSD_N]
            cm = c_ref[:, g * SSD_N:(g + 1) * SSD_N]
            cb = _dot(cm, bm, ((1,), (1,)))
            dcb = jnp.zeros((SSD_L, SSD_L), F32)
            dbm = jnp.zeros((SSD_L, SSD_N), F32)
            dcm = jnp.zeros((SSD_L, SSD_N), F32)
            for qq in range(PAIRS_PER_GROUP):
                q = g * PAIRS_PER_GROUP + qq
                ha, hb = 2 * q, 2 * q + 1
                csa, csb = cs_col[:, ha:ha + 1], cs_col[:, hb:hb + 1]
                xp = x_ref[:, q * LANE:(q + 1) * LANE]
                dtp = jnp.where(lo, dtc[:, ha:ha + 1], dtc[:, hb:hb + 1])
                xx = xp * dtp
                lma = jnp.exp(jnp.where(causal, csa - cs_row[ha:ha + 1, :], NEG))
                lmb = jnp.exp(jnp.where(causal, csb - cs_row[hb:hb + 1, :], NEG))
                ga, gb = cb * lma, cb * lmb
                dyp = dy_ref[:, q * LANE:(q + 1) * LANE]
                dya, dyb = jnp.where(lo, dyp, 0.0), jnp.where(lo, 0.0, dyp)
                s_in = st_ref[q]
                ds_out = dstate[q]
                la, lb = csa[SSD_L - 1:SSD_L, :], csb[SSD_L - 1:SSD_L, :]
                ecs = jnp.where(lo, jnp.exp(csa), jnp.exp(csb))
                decay = jnp.where(lo, jnp.exp(la - csa), jnp.exp(lb - csb))
                cd = jnp.where(lo1, jnp.exp(la), jnp.exp(lb))
                bds = _dot(bm, ds_out, ((1,), (0,)))
                dxx = _dot(ga, dya, ((0,), (0,))) + _dot(gb, dyb, ((0,), (0,))) + bds * decay
                dga = _dot(dya, xx, ((1,), (1,)))
                dgb = _dot(dyb, xx, ((1,), (1,)))
                dsega, dsegb = dga * ga, dgb * gb
                dcb = dcb + dga * lma + dgb * lmb
                yoff = _dot(cm, s_in, ((1,), (0,))) * ecs
                dye = dyp * ecs
                dcm = dcm + _dot(dye, s_in, ((1,), (1,)))
                xd = xx * decay
                dbm = dbm + _dot(xd, ds_out, ((1,), (1,)))
                wv = xd * bds
                ends = jnp.sum(wv, axis=0, keepdims=True) + cd * jnp.sum(ds_out * s_in, axis=0, keepdims=True)
                t1 = dyp * yoff - wv + jnp.where(rowi == SSD_L - 1, ends, 0.0)
                to_pair = (((pick_p < SSD_P) & (pick_l == ha)) | ((pick_p >= SSD_P) & (pick_l == hb))).astype(BF16)
                to_a_b = jnp.concatenate([(pick_l == ha).astype(BF16), (pick_l == hb).astype(BF16)], axis=0)
                dcs_c = dcs_c + _dot2(t1, to_pair) + _dot2(jnp.concatenate([dsega, dsegb], axis=1), to_a_b)
                dcs_r = (dcs_r + jnp.where(sub == ha, jnp.sum(dsega, axis=0, keepdims=True), 0.0)
                         + jnp.where(sub == hb, jnp.sum(dsegb, axis=0, keepdims=True), 0.0))
                dstate[q] = _dot(cm, dye, ((0,), (0,))) + cd * ds_out
                dpair = jnp.where(lo1, prow[2:3, ha:ha + 1], prow[2:3, hb:hb + 1])
                dx_ref[:, q * LANE:(q + 1) * LANE] = dxx * dtp + dpair * dyp
                ddt_x = ddt_x + _dot2(dxx * xp, to_pair)
                dd_row = dd_row + jnp.sum(_dot2(dyp * xp, to_pair), axis=0, keepdims=True)
            dx_ref[:, SSD_INNER + g * SSD_N:SSD_INNER + (g + 1) * SSD_N] = dbm + _dot(dcb, cm, ((0,), (0,)))
            dx_ref[:, SSD_INNER + (SSD_G + g) * SSD_N:SSD_INNER + (SSD_G + g + 1) * SSD_N] = dcm + _dot(dcb, bm, ((1,), (0,)))
        ri = lax.broadcasted_iota(jnp.int32, (SSD_L, SSD_L), 0)
        ci = lax.broadcasted_iota(jnp.int32, (SSD_L, SSD_L), 1)
        da = _dot3((ri <= ci).astype(BF16), dcs_c, ((1,), (0,)), False)
        da = da - _dot3(dcs_r, causal.astype(BF16), ((1,), (0,)), True).T
        ddt = ddt_x + da * a_row
        ddt_raw = ddt * _sigmoid(pre_c)
        ddt_ref[...] = ddt_raw
        da_head = jnp.sum(da * dtc, axis=0, keepdims=True) * a_row
        dp_ref[0:1, :] += jnp.sum(ddt_raw, axis=0, keepdims=True)
        dp_ref[1:2, :] += da_head
        dp_ref[2:3, :] += dd_row

    L = SSD_L
    rev = SSD_NC - 1
    bc_cols = SSD_INNER // (SSD_G * SSD_N)
    return pl.pallas_call(
        body, name="ssd_bwd", grid=(SSD_NC,),
        in_specs=[pl.BlockSpec((L, SSD_INNER), lambda c: (rev - c, 0)),
                  pl.BlockSpec((L, SSD_G * SSD_N), lambda c: (rev - c, bc_cols)),
                  pl.BlockSpec((L, SSD_G * SSD_N), lambda c: (rev - c, bc_cols + 1)),
                  pl.BlockSpec((L, LANE), lambda c: (rev - c, SM_DT // LANE)),
                  pl.BlockSpec((LANE, L), lambda c: (0, rev - c)),
                  pl.BlockSpec((8, LANE), lambda c: (0, 0)), pl.BlockSpec((LANE, 8), lambda c: (0, 0)),
                  pl.BlockSpec((None, NPAIR, SSD_N, LANE), lambda c: (rev - c, 0, 0, 0)),
                  pl.BlockSpec((L, SSD_INNER), lambda c: (rev - c, 0))],
        out_specs=[pl.BlockSpec((L, SSD_XBC), lambda c: (rev - c, 0)),
                   pl.BlockSpec((L, LANE), lambda c: (rev - c, 0)),
                   pl.BlockSpec((8, LANE), lambda c: (0, 0))],
        out_shape=[jax.ShapeDtypeStruct((S, SSD_XBC), F32), jax.ShapeDtypeStruct((S, LANE), F32),
                   jax.ShapeDtypeStruct((8, LANE), F32)],
        scratch_shapes=[pltpu.VMEM((NPAIR, SSD_N, LANE), F32)],
        compiler_params=pltpu.CompilerParams(dimension_semantics=("arbitrary",)),
    )(act, act, act, small, dtT, prow, pcol, states, dy)


TQ = 256
TK = 256
FWD_TQ = 256
FWD_TK = 256


def _attn_fwd(qc, kc, v):
    TQ, TK = FWD_TQ, FWD_TK

    def body(q_ref, k_ref, v_ref, o_ref, lse_ref):
        i = pl.program_id(1)
        lo = lax.broadcasted_iota(jnp.int32, (TQ, LANE), 1) < VDIM
        lo_k = lax.broadcasted_iota(jnp.int32, (TK, LANE), 1) < VDIM
        row_minus_col = lax.broadcasted_iota(jnp.int32, (TQ, TK), 0) - lax.broadcasted_iota(jnp.int32, (TQ, TK), 1)
        qa, qb = q_ref[:, 0:LANE], q_ref[:, LANE:2 * LANE]

        def scores(kb):
            kk = k_ref[pl.ds(pl.multiple_of(kb * TK, TK), TK), :]
            return (_dot(qa, kk[:, 0:LANE], ((1,), (1,))) * ATT_SCALE_LOG2, _dot(qb, kk[:, LANE:2 * LANE], ((1,), (1,))) * ATT_SCALE_LOG2)

        def update(kb, sa, sb, stats):
            ma, la, mb, lb, acc = stats
            vv = v_ref[pl.ds(pl.multiple_of(kb * TK, TK), TK), :]
            na = jnp.maximum(ma, jnp.max(sa, axis=1, keepdims=True))
            nb = jnp.maximum(mb, jnp.max(sb, axis=1, keepdims=True))
            pa, pb = jnp.exp2(sa - na), jnp.exp2(sb - nb)
            fa, fb = jnp.exp2(ma - na), jnp.exp2(mb - nb)
            la = fa * la + jnp.sum(pa, axis=1, keepdims=True)
            lb = fb * lb + jnp.sum(pb, axis=1, keepdims=True)
            acc = (acc * jnp.where(lo, fa, fb) + _dot(pa, jnp.where(lo_k, vv, 0), ((1,), (0,)))
                   + _dot(pb, jnp.where(lo_k, 0, vv), ((1,), (0,))))
            return na, la, nb, lb, acc

        def step(kb, carry):
            sa, sb = carry[:2]
            nxt = scores(kb + 1)
            return nxt + update(kb, sa, sb, carry[2:])

        neg = jnp.full((TQ, 1), NEG, F32)
        zero = jnp.zeros((TQ, 1), F32)
        n_full = i * (TQ // TK)
        carry = lax.fori_loop(0, n_full, step, scores(0) + (neg, zero, neg, zero, jnp.zeros((TQ, LANE), F32)))
        s, stats = carry[:2], carry[2:]
        for d in range(TQ // TK):
            nxt = scores(n_full + d + 1) if d + 1 < TQ // TK else None
            sa, sb = (jnp.where(row_minus_col >= d * TK, t, NEG) for t in s)
            stats = update(n_full + d, sa, sb, stats)
            s = nxt
        ma, la, mb, lb, acc = stats
        o_ref[...] = acc / jnp.where(lo, la, lb)
        lse_ref[...] = jnp.where(lo, ma + jnp.log2(la), mb + jnp.log2(lb)) * LN2

    return pl.pallas_call(
        body, name="attn_fwd", grid=(NPAIR, S // TQ),
        in_specs=[pl.BlockSpec((TQ, 2 * LANE), lambda j, i: (i, j)), pl.BlockSpec((S, 2 * LANE), lambda j, i: (0, j)),
                  pl.BlockSpec((S, LANE), lambda j, i: (0, j))],
        out_specs=[pl.BlockSpec((TQ, LANE), lambda j, i: (i, j)), pl.BlockSpec((None, TQ, LANE), lambda j, i: (j, i, 0))],
        out_shape=[jax.ShapeDtypeStruct((S, H * VDIM), F32), jax.ShapeDtypeStruct((NPAIR, S, LANE), F32)],
        compiler_params=pltpu.CompilerParams(dimension_semantics=("parallel", "parallel")),
    )(qc, kc, v)


def _attn_rows(lse, o, do):
    def body(lse_ref, o_ref, do_ref, r_ref):
        lt = lse_ref[...].T * (1.0 / LN2)
        tt = (o_ref[...] * do_ref[...]).T
        r_ref[...] = jnp.zeros_like(r_ref)
        r_ref[0:1, :] = lt[0:1, :]
        r_ref[1:2, :] = lt[VDIM:VDIM + 1, :]
        r_ref[2:3, :] = jnp.sum(tt[0:VDIM, :], axis=0, keepdims=True)
        r_ref[3:4, :] = jnp.sum(tt[VDIM:LANE, :], axis=0, keepdims=True)

    tile = pl.BlockSpec((S, LANE), lambda j: (0, j))
    return pl.pallas_call(
        body, name="attn_rows", grid=(NPAIR,), in_specs=[pl.BlockSpec((None, S, LANE), lambda j: (j, 0, 0)), tile, tile],
        out_specs=pl.BlockSpec((None, 8, S), lambda j: (j, 0, 0)), out_shape=jax.ShapeDtypeStruct((NPAIR, 8, S), F32),
    )(lse, o, do)


def _attn_bwd(qc, kc, kct, v, do, rows):
    nq = S // TQ

    def body(q_ref, k_ref, kt_ref, v_ref, do_ref, r_ref, dqt_ref, dk_ref, dv_ref):
        kb = pl.program_id(1)

        @pl.when(kb == 0)
        def _():
            dqt_ref[...] = jnp.zeros_like(dqt_ref)

        lo = lax.broadcasted_iota(jnp.int32, (TK, LANE), 1) < VDIM
        q_minus_k = lax.broadcasted_iota(jnp.int32, (TK, TQ), 1) - lax.broadcasted_iota(jnp.int32, (TK, TQ), 0)
        vv = v_ref[...]
        kk = k_ref[...]

        def step(qi, carry):
            off = pl.multiple_of(qi * TQ, TQ)
            qq = q_ref[pl.ds(off, TQ), :]
            dd = do_ref[pl.ds(off, TQ), :].astype(BF16)
            rr = r_ref[:, pl.ds(off, TQ)]
            keep = q_minus_k >= (kb - qi) * TQ
            out = []
            for x in range(2):
                sel = lo if x == 0 else jnp.logical_not(lo)
                kx, qx = kk[:, x * LANE:(x + 1) * LANE], qq[:, x * LANE:(x + 1) * LANE]
                st = jnp.where(keep, _dot(kx, qx, ((1,), (1,))) * ATT_SCALE_LOG2, NEG)
                pt = jnp.exp2(st - rr[x:x + 1, :])
                dpt = _dot(jnp.where(sel, vv, 0), dd, ((1,), (1,)))
                dst = (pt * (dpt - rr[2 + x:3 + x, :]) * ATT_SCALE).astype(BF16)
                out.append(carry[x] + _dot(dst, qx, ((1,), (0,))))
                out.append(_dot(pt, jnp.where(sel, dd, 0), ((1,), (0,))))
                dqt_ref[x * LANE:(x + 1) * LANE, pl.ds(off, TQ)] += _dot(kt_ref[x * LANE:(x + 1) * LANE, :], dst, ((1,), (0,)))
            return out[0], out[2], carry[2] + out[1] + out[3]

        z = jnp.zeros((TK, LANE), F32)
        dka, dkb, dv = lax.fori_loop(kb, nq, step, (z, z, z))
        dk_ref[:, 0:LANE] = dka
        dk_ref[:, LANE:2 * LANE] = dkb
        dv_ref[...] = dv.astype(BF16)

    return pl.pallas_call(
        body, name="attn_bwd", grid=(NPAIR, S // TK),
        in_specs=[pl.BlockSpec((S, 2 * LANE), lambda j, k: (0, j)), pl.BlockSpec((TK, 2 * LANE), lambda j, k: (k, j)),
                  pl.BlockSpec((2 * LANE, TK), lambda j, k: (j, k)), pl.BlockSpec((TK, LANE), lambda j, k: (k, j)),
                  pl.BlockSpec((S, LANE), lambda j, k: (0, j)), pl.BlockSpec((None, 8, S), lambda j, k: (j, 0, 0))],
        out_specs=[pl.BlockSpec((2 * LANE, S), lambda j, k: (j, 0)), pl.BlockSpec((TK, 2 * LANE), lambda j, k: (k, j)),
                   pl.BlockSpec((TK, LANE), lambda j, k: (k, j))],
        out_shape=[jax.ShapeDtypeStruct((H * LANE, S), F32), jax.ShapeDtypeStruct((S, H * LANE), F32),
                   jax.ShapeDtypeStruct((S, H * VDIM), BF16)],
        compiler_params=pltpu.CompilerParams(dimension_semantics=("parallel", "arbitrary")),
    )(qc, kc, kct, v, do, rows)


_IN_Z, _IN_XBC, _IN_DT, _IN_Q, _IN_KV, _IN_KR = 0, 1024, 2560, 2576, 2960, 3216


PROJ_COLS = 512
SMALL_PAD = pl.cdiv(SMALL_W, PROJ_COLS) * PROJ_COLS


def _prep_in(w_in_t):
    dt = w_in_t.dtype
    return jnp.concatenate(
        [w_in_t[_IN_Q:_IN_KV], w_in_t[_IN_KV:_IN_KR], w_in_t[_IN_KR:IN_WIDTH], jnp.zeros((LANE - ROPE, D), dt),
         w_in_t[_IN_DT:_IN_Q], jnp.zeros((SMALL_PAD - SM_DT - H, D), dt)], axis=0)


def _proj_in(xb, w_in_t, w_small):
    nz, nx, ns = (_IN_XBC - _IN_Z) // PROJ_COLS, (_IN_DT - _IN_XBC) // PROJ_COLS, SMALL_PAD // PROJ_COLS

    dt_block, dt_at = divmod(SM_DT, PROJ_COLS)

    def body(x_ref, w_ref, ws_ref, z_ref, xbc_ref, sm_ref, dtt_ref):
        i = pl.program_id(0)

        def emit(w, o_ref):
            o_ref[...] = lax.dot_general(x_ref[...], w[...], (((1,), (1,)), ((), ())), preferred_element_type=F32)

        pl.when(i < nz)(lambda: emit(w_ref, z_ref))
        pl.when((i >= nz) & (i < nz + nx))(lambda: emit(w_ref, xbc_ref))
        pl.when(i >= nz + nx)(lambda: emit(ws_ref, sm_ref))

        @pl.when(i == nz + nx + dt_block)
        def _():
            dtt_ref[...] = sm_ref[:, dt_at:dt_at + LANE].T

    def blocks(first, count, rows):
        at = lambda i: jnp.clip(i - first, 0, count - 1)
        return pl.BlockSpec((PROJ_COLS, D), lambda i: (at(i), 0)) if rows else pl.BlockSpec((S, PROJ_COLS), lambda i: (0, at(i)))

    return pl.pallas_call(
        body, name="proj_in", grid=(nz + nx + ns,),
        in_specs=[pl.BlockSpec((S, D), lambda i: (0, 0)), blocks(0, nz + nx, True), blocks(nz + nx, ns, True)],
        out_specs=[blocks(0, nz, False), blocks(nz, nx, False), blocks(nz + nx, ns, False), pl.BlockSpec((LANE, S), lambda i: (0, 0))],
        out_shape=[jax.ShapeDtypeStruct((S, _IN_XBC - _IN_Z), F32), jax.ShapeDtypeStruct((S, _IN_DT - _IN_XBC), F32),
                   jax.ShapeDtypeStruct((S, SMALL_W), F32), jax.ShapeDtypeStruct((LANE, S), F32)],
    )(xb, w_in_t, w_small)


PART_COLS = 512


def _part_blocks(widths):
    first = [0]
    for w in widths:
        first.append(first[-1] + w // PART_COLS)

    def at(part):
        return lambda i: jnp.clip(i - first[part], 0, first[part + 1] - first[part] - 1)

    return first, at


def _mm_ta_stacked(parts, b, rows, name):
    n = b.shape[1]
    first, at = _part_blocks([a.shape[1] for a in parts])
    assert first[-1] == pl.cdiv(rows, PART_COLS)

    def body(*refs):
        b_ref, o_ref = refs[-2:]
        i = pl.program_id(0)
        for part, a_ref in enumerate(refs[:-2]):
            @pl.when((i >= first[part]) & (i < first[part + 1]))
            def _(a_ref=a_ref):
                o_ref[...] = lax.dot_general(a_ref[...], b_ref[...], (((0,), (0,)), ((), ())),
                                             preferred_element_type=F32).astype(BF16)

    return pl.pallas_call(
        body, name=name, grid=(first[-1],),
        in_specs=[pl.BlockSpec((S, PART_COLS), lambda i, at=at(part): (0, at(i))) for part in range(len(parts))]
        + [pl.BlockSpec((S, n), lambda i: (0, 0))],
        out_specs=pl.BlockSpec((PART_COLS, n), lambda i: (i, 0)), out_shape=jax.ShapeDtypeStruct((rows, n), BF16),
    )(*parts, b)


def _prep_attn(w_qb, w_kvb):
    w_q = jnp.pad(w_qb.reshape(Q_RANK, H, NOPE + ROPE), ((0, 0), (0, 0), (0, LANE - NOPE - ROPE))).reshape(Q_RANK, H * LANE)
    kv3 = w_kvb.reshape(KV_RANK, H, NOPE + VDIM)
    w_k = jnp.pad(kv3[:, :, :NOPE], ((0, 0), (0, 0), (0, LANE - NOPE))).reshape(KV_RANK, H * LANE)
    w_v = kv3[:, :, NOPE:].reshape(KV_RANK, H * VDIM)
    return w_q, w_k, w_v


def _rope_tables(positions):
    inv_freq = 1.0 / (10000.0 ** (jnp.arange(0, ROPE, 2, dtype=F32) / ROPE))
    ang = positions.astype(F32).reshape(S, 1) * inv_freq
    cos, sin = jnp.cos(ang), jnp.sin(ang)
    cos_t = jnp.concatenate([jnp.ones((S, NOPE), F32), cos, cos, jnp.ones((S, LANE - NOPE - ROPE), F32)], axis=1)
    sin_t = jnp.concatenate([jnp.zeros((S, NOPE), F32), -sin, sin, jnp.zeros((S, LANE - NOPE - ROPE), F32)], axis=1)
    return cos_t, sin_t


def _local_step(x, p, positions, target, w_in, fetch, send, sp, started):
    w_in_t = w_in.reshape(IN_WIDTH, D)
    w_small = _prep_in(w_in_t)
    cos_t, sin_t = _rope_tables(positions)
    prow = jnp.zeros((8, LANE), F32).at[0, :H].set(sp["dt_bias"][0]).at[1, :H].set(sp["A_log"][0]).at[2, :H].set(sp["D"][0])
    pcol = prow.T

    xb, pb = (x + started).astype(BF16), p.astype(BF16)
    z, xbc, small, dt_t = _proj_in(xb, w_in_t, w_small)
    act = _conv_fwd(xbc, sp["conv_w"], sp["conv_b"])
    y, states = _ssd_fwd(act, small, dt_t, prow, pcol)
    y_ssd = _gate_norm_fwd(y, z, sp["ssd_norm"])
    gl = fetch("attn", y_ssd)
    w_q, w_k, w_v = _prep_attn(_from_cols(gl["w_qb"]), _from_cols(gl["w_kvb"]))
    qn, kvn, qcat, kcat, kcat_t, v = _qkv_fwd(small, w_q, w_k, w_v, sp["q_norm"], sp["kv_norm"], cos_t, sin_t)
    o, lse = _attn_fwd(qcat, kcat, v)
    y_mla = _rms_fwd(o, sp["out_norm"], name="out_norm_fwd")
    w_out = fetch("out", y_mla)["w_out"]
    w_out = w_out.reshape(2 * SSD_INNER, D)
    mix, h1, h1b = _out_proj_ln(y_ssd, y_mla, w_out, x, sp["ln_mix_g"], sp["ln_mix_b"])
    gl = fetch("ffn", h1b)
    w_pg, w_pp = gl["w_pg"].reshape(D, D), _from_cols(gl["w_pp"])
    w_gate, w_up, w_down = gl["w_gate"], gl["w_up"], gl["w_down"]
    gate, up, actf = _ffn_hidden_fwd(h1b, w_gate, w_up)
    ffn = _mm([(actf, w_down)], chunk="sum", name="ffn_down")
    dpre2, dpre2b, dpg, dpp, dg2, db2, loss_row = _final_fwd_bwd(h1, ffn, h1b, pb, w_pg, w_pp, target, sp["ln_ffn_g"], sp["ln_ffn_b"])

    g = {"ln_ffn_g": dg2, "ln_ffn_b": db2}
    g["w_pp"] = _to_cols(_mm([(pb, dpp)], ta=True, out_dtype=BF16, name="d_w_ple_proj"))
    g["w_pg"] = _mm([(h1b, dpg)], ta=True, out_dtype=BF16, name="d_w_ple_gate").reshape(NCHIP, D // NCHIP, D)
    g["w_down"] = _mm([(actf, dpre2b)], ta=True, chunk="out", out_dtype=BF16, name="d_w_down")
    dgate, dup = _ffn_hidden_bwd(dpre2b, w_down, gate, up)
    g["w_gate"] = _mm([(dgate, h1b)], ta=True, chunk="out", out_dtype=BF16, name="d_w_gate")
    g["w_up"] = _mm([(dup, h1b)], ta=True, chunk="out", out_dtype=BF16, name="d_w_up")
    sent = send("ffn", {name: g.pop(name) for name in dict(ASYNC_GROUPS)["ffn"]})
    dh1 = _mm([(dgate, w_gate), (dup, w_up), (dpg, w_pg.T)], chunk="sum", add=dpre2, add_scale=ALPHA, name="d_h1")
    dpre1, dpre1b, g["ln_mix_g"], g["ln_mix_b"], dy_ssd, dy_mla = _ln_bwd(x, mix, sp["ln_mix_g"] + sent, dh1, w_out)
    dw_out = _mm_ta_stacked((y_ssd, y_mla), dpre1b, 2 * SSD_INNER, "d_w_out")
    sent = send("out", {"w_out": dw_out.reshape(NCHIP, 2 * SSD_INNER // NCHIP, D)})
    do, g["out_norm"] = _rms_bwd(o, sp["out_norm"] + sent, dy_mla, name="out_norm_bwd")
    dqt, dk, dv = _attn_bwd(qcat, kcat, kcat_t, v, do, _attn_rows(lse, o, do))
    dlatent, dqlin, dkb, g["q_norm"], g["kv_norm"] = _qkv_bwd(dqt, dk, dv, small, w_q, w_k, w_v, sp["q_norm"], sp["kv_norm"], cos_t, sin_t)
    dw_q = _mm([(qn, dqlin)], ta=True, out_dtype=BF16, name="d_w_q")
    dw_k = _mm([(kvn, dkb)], ta=True, out_dtype=BF16, name="d_w_k")
    dw_v = _mm([(kvn, dv)], ta=True, out_dtype=BF16, name="d_w_v")
    dw_qb = _to_cols(dw_q.reshape(Q_RANK, H, LANE)[:, :, :NOPE + ROPE].reshape(Q_RANK, H * (NOPE + ROPE)))
    dw_kvb = _to_cols(jnp.concatenate([dw_k.reshape(KV_RANK, H, LANE)[:, :, :NOPE], dw_v.reshape(KV_RANK, H, VDIM)],
                                       axis=2).reshape(KV_RANK, H * (NOPE + VDIM)))
    sent = send("attn", {"w_qb": dw_qb, "w_kvb": dw_kvb})
    dy, dz, g["ssd_norm"] = _gate_norm_bwd(y, z, sp["ssd_norm"] + sent, dy_ssd)
    dact, ddt, dprow = _ssd_bwd(act, small, dt_t, prow, pcol, states, dy)
    g["dt_bias"], g["A_log"], g["D"] = dprow[0:1, :H], dprow[1:2, :H], dprow[2:3, :H]
    dxbc, g["conv_w"], g["conv_b"] = _conv_bwd(xbc, sp["conv_w"], sp["conv_b"], dact)
    dsmall = jnp.concatenate([dlatent, ddt.astype(BF16)], axis=1)
    in_blocks = [(d, w_in_t, (k, first // PROJ_COLS + k, PROJ_COLS))
                 for d, first in ((dz, _IN_Z), (dxbc, _IN_XBC)) for k in range(d.shape[1] // PROJ_COLS)]
    grad_x = _mm(in_blocks + [(dsmall, w_small, (0, 0, SMALL_W))], add=dpre1, add_scale=ALPHA, name="d_x")
    sent = send("small", dict(g, loss=loss_row))
    n_small = IN_WIDTH - _IN_DT
    dsm = jnp.concatenate([(ddt[:, :H] + sent).astype(BF16), dlatent[:, :n_small - H], jnp.zeros((S, D - n_small), BF16)], axis=1)
    dw_in = _mm_ta_stacked((dz, dxbc, dsm), xb, IN_WIDTH, "d_w_in").reshape(NCHIP, IN_WIDTH // NCHIP * D // LANE, LANE)
    return loss_row, grad_x, dw_in, g


MESH = pl.DeviceIdType.MESH
BIG = (("w_in", (D, IN_WIDTH), 1), ("w_qb", (Q_RANK, H * (NOPE + ROPE)), 1), ("w_kvb", (KV_RANK, H * (NOPE + VDIM)), 1),
       ("w_out", (2 * SSD_INNER, D), 0), ("w_gate", (D, D_FF), 1), ("w_up", (D, D_FF), 1), ("w_down", (D_FF, D), 0),
       ("w_pg", (D, D), 0), ("w_pp", (PLE, D), 1))
CONV_SHARD = SSD_XBC // NCHIP
BF16_ROWS = 16


def _from_cols(stack):
    return jnp.concatenate([stack[k] for k in range(NCHIP)], axis=1)


def _to_cols(full):
    r, c4 = full.shape
    return full.reshape(r, NCHIP, c4 // NCHIP).transpose(1, 0, 2)


def _coords():
    return lax.axis_index("x"), lax.axis_index("y"), lax.axis_index("c")


def _peers():
    x, y, c = _coords()
    return 2 * x + y, c, [(1 - x, y), (x, 1 - y), (1 - x, 1 - y)], (x, y, 1 - c)


def _half_axis(shape):
    return 0 if shape[-2] % (2 * BF16_ROWS) == 0 else 1


def _half_shape(shape):
    r, c = shape[-2:]
    return (r // 2, c) if _half_axis(shape) == 0 else (r, c // 2)


def _half(core, shape):
    r, c = shape[-2:]
    if _half_axis(shape) == 0:
        return pl.ds(pl.multiple_of(core * (r // 2), BF16_ROWS), r // 2), slice(None)
    return slice(None), pl.ds(pl.multiple_of(core * (c // 2), LANE), c // 2)


ASYNC_GROUPS = (("attn", ("w_qb", "w_kvb")), ("out", ("w_out",)), ("ffn", ("w_gate", "w_up", "w_down", "w_pg", "w_pp")))
TRANSPOSED = ("w_in", "w_gate", "w_up")
ROW_MAJOR = ("w_in",)
HBM_SPEC = pl.BlockSpec(memory_space=pltpu.HBM)
SEM_SPEC = pl.BlockSpec(memory_space=pltpu.SEMAPHORE)
IN_FLIGHT = pltpu.SideEffectType.DATAFLOW_SIDE_EFFECTING


def _in_hbm(a):
    return pltpu.with_memory_space_constraint(a, pltpu.HBM)


def _hbm_like(arrs, lead=()):
    return [pltpu.HBM(lead + a.shape, a.dtype) for a in arrs]


def _split_start(name, srcs, lands, after, n_sem, start):
    n = len(srcs)
    order = [] if after is None else [after]

    def body(*refs):
        src_refs, land_refs = refs[:n], refs[n:2 * n]
        send_sems, recv_sems = refs[2 * n + len(order)], refs[2 * n + len(order) + 1]
        token = refs[-1]

        def copy(send_idx, recv_idx, src, dst, to):
            return pltpu.make_async_remote_copy(src_ref=src, dst_ref=dst, send_sem=send_sems.at[send_idx],
                                                recv_sem=recv_sems.at[recv_idx], device_id=to, device_id_type=MESH)

        for cp in start(src_refs, land_refs, copy):
            cp.start()
        token[...] = jnp.zeros_like(token)

    sem = pltpu.SemaphoreType.DMA((n_sem,))
    outs = pl.pallas_call(
        body, name=name, in_specs=[HBM_SPEC] * (2 * n) + [pl.BlockSpec(memory_space=pl.ANY)] * len(order),
        out_specs=[SEM_SPEC, SEM_SPEC] + [HBM_SPEC] * (2 * n) + [pl.BlockSpec(memory_space=pltpu.VMEM)],
        out_shape=[sem, sem] + _hbm_like(srcs) + _hbm_like(lands) + [jax.ShapeDtypeStruct((8, LANE), F32)],
        input_output_aliases={i: 2 + i for i in range(2 * n)},
        compiler_params=pltpu.CompilerParams(has_side_effects=IN_FLIGHT),
    )(*[_in_hbm(a) for a in srcs], *[_in_hbm(a) for a in lands], *order)
    return (outs[0], outs[1], outs[2:2 + n], outs[2 + n:2 + 2 * n]), outs[-1]


def _split_wait(name, send_sems, recv_sems, srcs, lands, after, waits):
    n = len(srcs)

    def body(*refs):
        src_refs, land_refs = refs[:n], refs[n:2 * n]
        send_ref, recv_ref = refs[2 * n], refs[2 * n + 1]

        def copy(send_idx, recv_idx, src, dst, to):
            return pltpu.make_async_remote_copy(src_ref=src, dst_ref=dst, send_sem=send_ref.at[send_idx],
                                                recv_sem=recv_ref.at[recv_idx], device_id=to, device_id_type=MESH)

        for cp in waits(src_refs, land_refs, copy):
            cp.wait_send()
            cp.wait_recv()

    outs = pl.pallas_call(
        body, name=name, in_specs=[HBM_SPEC] * (2 * n) + [SEM_SPEC, SEM_SPEC, pl.BlockSpec(memory_space=pl.ANY)],
        out_specs=[HBM_SPEC] * (2 * n), out_shape=_hbm_like(srcs) + _hbm_like(lands),
        input_output_aliases={i: i for i in range(2 * n)},
        compiler_params=pltpu.CompilerParams(has_side_effects=IN_FLIGHT),
    )(*srcs, *lands, send_sems, recv_sems, after)
    return outs[:n], outs[n:]


GATHER_LATE_SEMS = 2 * (NCHIP - 1)


def _gather_async_start(tag, shards, after):
    def start(srcs, lands, copy):
        k, c, chips, _ = _peers()
        out = []
        for a, (src, dst) in enumerate(zip(srcs, lands)):
            for j, (cx, cy) in enumerate(chips):
                for core in range(2):
                    out.append(copy(GATHER_LATE_SEMS * a + 2 * j + core, GATHER_LATE_SEMS * a + 2 * j + c,
                                    src.at[*_half(c, src.shape)], dst.at[k, *_half(c, src.shape)], (cx, cy, core)))
        return out

    chip = 2 * lax.axis_index("x") + lax.axis_index("y")
    lands = [lax.dynamic_update_slice(lax.empty((NCHIP,) + s.shape, s.dtype), s[None], (chip, 0, 0)) for s in shards]
    return _split_start("gather_%s_start" % tag, shards, lands, after, GATHER_LATE_SEMS * len(shards), start)


def _gather_async_wait(tag, send_sems, recv_sems, shards, lands, after, first=0):
    def waits(srcs, lands_, copy):
        _, c, chips, _ = _peers()
        out = []
        for a, (src, dst) in enumerate(zip(srcs, lands_)):
            for j, (cx, cy) in enumerate(chips):
                for core in range(2):
                    idx = GATHER_LATE_SEMS * (first + a) + 2 * j + core
                    out.append(copy(idx, idx, src.at[*_half(c, src.shape)], dst.at[2 * cx + cy, *_half(core, src.shape)], (cx, cy, core)))
        return out

    return _split_wait("gather_%s_wait" % tag, send_sems, recv_sems, shards, lands, after, waits)[1]


EARLY_SEMS = NCHIP - 1


def _gather_early_start(shards):
    def start(srcs, lands, copy):
        k, c, chips, _ = _peers()
        return [copy(EARLY_SEMS * a + j, EARLY_SEMS * a + j, src.at[*_half(c, src.shape)], dst.at[k, *_half(c, src.shape)], (cx, cy, c))
                for a, (src, dst) in enumerate(zip(srcs, lands)) for j, (cx, cy) in enumerate(chips)]

    chip = 2 * lax.axis_index("x") + lax.axis_index("y")
    lands = [lax.dynamic_update_slice(lax.empty((NCHIP,) + s.shape, s.dtype), s[None], (chip, 0, 0)) for s in shards]
    return _split_start("gather_in_start", shards, lands, None, EARLY_SEMS * len(shards), start)


def _gather_early_wait(send_sems, recv_sems, shards, lands, after):
    def waits(srcs, lands_, copy):
        _, c, chips, _ = _peers()
        return [copy(EARLY_SEMS * a + j, EARLY_SEMS * a + j, src.at[*_half(c, src.shape)], dst.at[2 * cx + cy, *_half(c, src.shape)], (cx, cy, c))
                for a, (src, dst) in enumerate(zip(srcs, lands_)) for j, (cx, cy) in enumerate(chips)]

    n_arr = len(shards)

    def forward(*refs):
        stacks, (fwd_send, fwd_recv) = refs[n_arr:2 * n_arr], refs[2 * n_arr:]
        _, c, chips, sibling = _peers()

        def pass_on(a, j, core):
            cx, cy = chips[j]
            part = stacks[a].at[2 * cx + cy, *_half(core, shards[a].shape)]
            return pltpu.make_async_remote_copy(src_ref=part, dst_ref=part, send_sem=fwd_send.at[EARLY_SEMS * a + j],
                                                recv_sem=fwd_recv.at[EARLY_SEMS * a + j], device_id=sibling, device_id_type=MESH)

        pairs = [(a, j) for a in range(n_arr) for j in range(len(chips))]
        sent = [pass_on(a, j, c) for a, j in pairs]
        for cp in sent:
            cp.start()
        for a, j in pairs:
            pass_on(a, j, 1 - c).wait_recv()
        for cp in sent:
            cp.wait_send()

    arrived = _split_wait("gather_in_wait", send_sems, recv_sems, shards, lands, after, waits)[1]
    any_spec = pl.BlockSpec(memory_space=pl.ANY)
    return pl.pallas_call(
        forward, name="gather_in_forward", in_specs=[any_spec] * n_arr, out_specs=[any_spec] * n_arr,
        out_shape=[jax.ShapeDtypeStruct(a.shape, a.dtype) for a in arrived], input_output_aliases={i: i for i in range(n_arr)},
        scratch_shapes=[pltpu.SemaphoreType.DMA((EARLY_SEMS * n_arr,))] * 2,
    )(*arrived)


def _other_devices():
    x, y, c = _coords()
    out = []
    for d in range(1, NDEV):
        tx, ty, tc = x ^ (d >> 2), y ^ ((d >> 1) & 1), c ^ (d & 1)
        out.append((d, (tx, ty, tc), 2 * tx + ty, 4 * tx + 2 * ty + tc))
    return out


def _reduce_async_start(tag, stacks, after):
    def start(srcs, lands, copy):
        x, y, c = _coords()
        me = 4 * x + 2 * y + c
        return [copy((NDEV - 1) * a + d - 1, (NDEV - 1) * a + d - 1, src.at[chip, *_half(to[2], src.shape)], dst.at[me], to)
                for a, (src, dst) in enumerate(zip(srcs, lands)) for d, to, chip, _ in _other_devices()]

    x, y, c = _coords()
    lands = []
    for s in stacks:
        hr, hc = _half_shape(s.shape)
        at = (c * hr, 0) if _half_axis(s.shape) == 0 else (0, c * hc)
        own = lax.dynamic_slice(s, (2 * x + y,) + at, (1, hr, hc))
        lands.append(lax.dynamic_update_slice(lax.empty((NDEV, hr, hc), s.dtype), own, (4 * x + 2 * y + c, 0, 0)))
    return _split_start("reduce_%s_start" % tag, stacks, lands, after, (NDEV - 1) * len(stacks), start)


def _reduce_async_wait(tag, send_sems, recv_sems, stacks, lands, after):
    def waits(srcs, lands_, copy):
        return [copy((NDEV - 1) * a + d - 1, (NDEV - 1) * a + d - 1, src.at[chip, *_half(to[2], src.shape)], dst.at[pos], to)
                for a, (src, dst) in enumerate(zip(srcs, lands_)) for d, to, chip, pos in _other_devices()]

    return _split_wait("reduce_%s_wait" % tag, send_sems, recv_sems, stacks, lands, after, waits)[1]


def _reduce_finish(tag, arrived, dims):
    n_arr = len(arrived)

    def body(*refs):
        lands, fin = refs[:n_arr], refs[n_arr:2 * n_arr]
        send_sems, recv_sems = refs[2 * n_arr:]
        _, c, _, sibling = _peers()
        sends = []
        for a in range(n_arr):
            mine = fin[a].at[*_half(c, dims[a])]

            def device_sum(vs, vf, a=a, mine=mine):
                pltpu.sync_copy(lands[a], vs)
                acc = vs[0].astype(F32)
                for i in range(1, NDEV):
                    acc = acc + vs[i].astype(F32)
                vf[...] = acc
                pltpu.sync_copy(vf, mine)

            pl.run_scoped(device_sum, pltpu.VMEM((NDEV,) + _half_shape(dims[a]), BF16), pltpu.VMEM(_half_shape(dims[a]), F32))
            sends.append(pltpu.make_async_remote_copy(src_ref=mine, dst_ref=mine, send_sem=send_sems.at[a], recv_sem=recv_sems.at[a],
                                                      device_id=sibling, device_id_type=MESH))
            sends[-1].start()
        for a in range(n_arr):
            other = fin[a].at[*_half(1 - c, dims[a])]
            pltpu.make_async_remote_copy(src_ref=other, dst_ref=other, send_sem=send_sems.at[a], recv_sem=recv_sems.at[a],
                                         device_id=sibling, device_id_type=MESH).wait_recv()
        for cp in sends:
            cp.wait_send()

    any_spec = pl.BlockSpec(memory_space=pl.ANY)
    return pl.pallas_call(
        body, name="reduce_%s_finish" % tag, in_specs=[any_spec] * n_arr, out_specs=[any_spec] * n_arr,
        out_shape=[jax.ShapeDtypeStruct(d, F32) for d in dims],
        scratch_shapes=[pltpu.SemaphoreType.DMA((n_arr,)), pltpu.SemaphoreType.DMA((n_arr,))],
    )(*arrived)


SMALL = (("conv_w", SSD_K * SSD_XBC), ("conv_b", SSD_XBC), ("dt_bias", H), ("A_log", H), ("D", H), ("ssd_norm", SSD_INNER),
         ("q_norm", Q_RANK), ("kv_norm", KV_RANK), ("out_norm", SSD_INNER), ("ln_mix_g", D), ("ln_mix_b", D),
         ("ln_ffn_g", D), ("ln_ffn_b", D))
SMALL_ROWS = 120
NDEV = 8


def _allreduce_small_start(sv):
    def start(srcs, lands, copy):
        x, y, c = _coords()
        return [copy(d - 1, d - 1, srcs[0], lands[0].at[4 * x + 2 * y + c], to) for d, to, _, _ in _other_devices()]

    x, y, c = _coords()
    slots = lax.dynamic_update_slice(lax.empty((NDEV,) + sv.shape, sv.dtype), sv[None], (4 * x + 2 * y + c, 0, 0))
    return _split_start("allreduce_small_start", [sv], [slots], None, NDEV - 1, start)


def _allreduce_small_wait(send_sems, recv_sems, srcs, lands, after):
    def waits(srcs_, lands_, copy):
        return [copy(d - 1, d - 1, srcs_[0], lands_[0].at[pos], to) for d, to, _, pos in _other_devices()]

    def device_sum(slots_ref, out_ref):
        acc = slots_ref[0]
        for i in range(1, NDEV):
            acc = acc + slots_ref[i]
        out_ref[...] = acc

    slots = _split_wait("allreduce_small_wait", send_sems, recv_sems, srcs, lands, after, waits)[1][0]
    vm = pl.BlockSpec(memory_space=pltpu.VMEM)
    return pl.pallas_call(device_sum, name="allreduce_small_sum", in_specs=[vm], out_specs=vm,
                          out_shape=jax.ShapeDtypeStruct(slots.shape[1:], slots.dtype))(slots)


def _adamw_math(w, g, m, v):
    m2 = ADAM_B1 * m + (1.0 - ADAM_B1) * g
    v2 = ADAM_B2 * v + (1.0 - ADAM_B2) * (g * g)
    m_hat = m2 / (1.0 - ADAM_B1 ** ADAM_STEP)
    v_hat = v2 / (1.0 - ADAM_B2 ** ADAM_STEP)
    return -ADAM_LR * (m_hat / (jnp.sqrt(v_hat) + ADAM_EPS) + ADAM_WD * w), m2, v2


ADAM_BLOCK_BYTES = 2 * 1024 * 1024


def _adamw_big(w, g, m, v, *, name):
    r, c = w.shape

    def body(w_ref, g_ref, m_ref, v_ref, d_ref, m2_ref, v2_ref):
        d_ref[...], m2_ref[...], v2_ref[...] = _adamw_math(w_ref[...], g_ref[...], m_ref[...], v_ref[...])

    tr = max(t for t in range(8, r + 1, 8) if r % t == 0 and t * c * 4 <= ADAM_BLOCK_BYTES)
    steps, spec = r // tr, pl.BlockSpec((tr, c), lambda i: (i, 0))
    return pl.pallas_call(body, name=name, grid=(steps,), in_specs=[spec] * 4, out_specs=[spec] * 3,
                          out_shape=[jax.ShapeDtypeStruct((r, c), F32)] * 3)(w, g, m, v)


def _adamw_small(ws, gs, ms, vs):
    n = len(ws)

    def body(*refs):
        for i in range(n):
            w_ref, g_ref, m_ref, v_ref = (refs[j * n + i] for j in range(4))
            d_ref, m2_ref, v2_ref = (refs[(4 + j) * n + i] for j in range(3))
            d_ref[...], m2_ref[...], v2_ref[...] = _adamw_math(w_ref[...], g_ref[...], m_ref[...], v_ref[...])

    vm = pl.BlockSpec(memory_space=pltpu.VMEM)
    shapes = [jax.ShapeDtypeStruct(w.shape, F32) for w in ws]
    outs = pl.pallas_call(body, name="adamw_small", in_specs=[vm] * (4 * n), out_specs=[vm] * (3 * n), out_shape=shapes * 3)(
        *ws, *gs, *ms, *vs)
    return outs[:n], outs[n:2 * n], outs[2 * n:]


_SMALL_ARG = {"conv_w": "ssd_conv_w", "conv_b": "ssd_conv_b", "dt_bias": "ssd_dt_bias", "A_log": "ssd_A_log", "D": "ssd_D",
              "ssd_norm": "ssd_norm_w", "q_norm": "mla_q_norm_w", "kv_norm": "mla_kv_norm_w", "out_norm": "mla_out_norm_w",
              "ln_mix_g": "ln_mix_g", "ln_mix_b": "ln_mix_b", "ln_ffn_g": "ln_ffn_g", "ln_ffn_b": "ln_ffn_b"}
_BIG_ARG = {"w_in": "w_in", "w_qb": "mla_w_q_b", "w_kvb": "mla_w_kv_b", "w_out": "w_out", "w_gate": "w_ffn_gate",
            "w_up": "w_ffn_up", "w_down": "w_ffn_down", "w_pg": "w_ple_gate", "w_pp": "w_ple_proj"}
_WEIGHT_ORDER = ("w_in", "ssd_conv_w", "ssd_conv_b", "ssd_dt_bias", "ssd_A_log", "ssd_D", "ssd_norm_w", "mla_q_norm_w", "mla_w_q_b",
                 "mla_kv_norm_w", "mla_w_kv_b", "mla_out_norm_w", "w_out", "ln_mix_g", "ln_mix_b", "w_ffn_gate", "w_ffn_up",
                 "w_ffn_down", "w_ple_gate", "w_ple_proj", "ln_ffn_g", "ln_ffn_b")


def _rows128(a):
    flat = a.reshape(-1)
    return jnp.pad(flat, (0, -flat.shape[0] % LANE)).reshape(-1, LANE)


def kernel(x, p, positions, w_in, ssd_conv_w, ssd_conv_b, ssd_dt_bias, ssd_A_log, ssd_D, ssd_norm_w, mla_q_norm_w, mla_w_q_b, mla_kv_norm_w, mla_w_kv_b, mla_out_norm_w, w_out, ln_mix_g, ln_mix_b, w_ffn_gate, w_ffn_up, w_ffn_down, w_ple_gate, w_ple_proj, ln_ffn_g, ln_ffn_b, loss_target, m_w_in, m_ssd_conv_w, m_ssd_conv_b, m_ssd_dt_bias, m_ssd_A_log, m_ssd_D, m_ssd_norm_w, m_mla_q_norm_w, m_mla_w_q_b, m_mla_kv_norm_w, m_mla_w_kv_b, m_mla_out_norm_w, m_w_out, m_ln_mix_g, m_ln_mix_b, m_w_ffn_gate, m_w_ffn_up, m_w_ffn_down, m_w_ple_gate, m_w_ple_proj, m_ln_ffn_g, m_ln_ffn_b, v_w_in, v_ssd_conv_w, v_ssd_conv_b, v_ssd_dt_bias, v_ssd_A_log, v_ssd_D, v_ssd_norm_w, v_mla_q_norm_w, v_mla_w_q_b, v_mla_kv_norm_w, v_mla_w_kv_b, v_mla_out_norm_w, v_w_out, v_ln_mix_g, v_ln_mix_b, v_w_ffn_gate, v_w_ffn_up, v_w_ffn_down, v_w_ple_gate, v_w_ple_proj, v_ln_ffn_g, v_ln_ffn_b):
    given = dict(locals())
    chip = 2 * lax.axis_index("x") + lax.axis_index("y")

    def local(name, prefix=""):
        a = given[prefix + _BIG_ARG[name]][0]
        return a.T if name in TRANSPOSED else a

    def updated(name, prefix=""):
        if name in ROW_MAJOR:
            _, c, r = given[prefix + _BIG_ARG[name]].shape
            return given[prefix + _BIG_ARG[name]].reshape(c // LANE, LANE, r).transpose(2, 0, 1).reshape(-1, LANE)
        return local(name, prefix)

    def global_layout(name, arr):
        if name in ROW_MAJOR:
            r, c = local(name).shape
            return arr.reshape(r, c // LANE, LANE).transpose(1, 2, 0).reshape(1, c, r)
        return (arr.T if name in TRANSPOSED else arr)[None]

    conv_bits = lax.bitcast_convert_type(ssd_conv_w[0], BF16).reshape(SSD_K, 2 * CONV_SHARD)
    early, flying = _gather_early_start([local("w_in").astype(BF16), jnp.pad(conv_bits, ((0, BF16_ROWS - SSD_K), (0, 0)))])
    late = [name for _, names in ASYNC_GROUPS for name in names]
    late_casts = [(local(name) + flying[0, 0]).astype(BF16) for name in late]
    w_in_all, conv_all = _gather_early_wait(*early, sum(c[:8, :LANE].astype(F32) for c in late_casts))
    sp = {k: given[a] for k, a in _SMALL_ARG.items() if k != "conv_w"}
    sp["conv_w"] = _from_cols(lax.bitcast_convert_type(conv_all[:, :SSD_K].reshape(NCHIP, SSD_K, CONV_SHARD, 2), F32))
    (late_send, late_recv, late_shards, late_lands), tie = _gather_async_start("late", late_casts, w_in_all)

    def fetch(group, after):
        names = dict(ASYNC_GROUPS)[group]
        first = late.index(names[0])
        mine = slice(first, first + len(names))
        return dict(zip(names, _gather_async_wait(group, late_send, late_recv, late_shards[mine], late_lands[mine], after, first)))

    reducing = {}

    def send(group, grads):
        if group == "small":
            rows = jnp.concatenate([_rows128(grads[name]) for name, _ in SMALL] + [grads["loss"]], axis=0)
            reducing[group], sent = _allreduce_small_start(jnp.pad(rows, ((0, SMALL_ROWS - rows.shape[0]), (0, 0))))
        else:
            reducing[group], sent = _reduce_async_start(group, [grads[name] for name in dict(ASYNC_GROUPS)[group]], None)
        return sent[0, 0]

    loss_row, grad_x, dw_in, g = _local_step(x[0], p[0, 0], positions[0], loss_target[0], w_in_all, fetch, send, sp, tie[0, 0])

    reducing["in"], tie = _reduce_async_start("in", [dw_in], grad_x)
    gbig = {}
    for group, names in reversed(ASYNC_GROUPS):
        arrived = _reduce_async_wait(group, *reducing[group], tie)
        gbig.update(zip(names, _reduce_finish(group, arrived, [local(name).shape for name in names])))
    small_sum = _allreduce_small_wait(*reducing.pop("small"), tie)
    gsmall, row = {}, 0
    for name, size in SMALL:
        nrow = -(-size // LANE)
        gsmall[name] = small_sum[row:row + nrow].reshape(-1)[:size]
        row += nrow
    loss = small_sum[row, 0]

    grads = {_BIG_ARG[name]: global_layout(name, arr) for name, arr in gbig.items()}
    for name, _ in SMALL:
        if name == "conv_w":
            full_g = gsmall[name].reshape(SSD_K, SSD_XBC)
            grads["ssd_conv_w"] = lax.dynamic_slice(full_g, (0, chip * CONV_SHARD), (SSD_K, CONV_SHARD))[None]
        else:
            grads[_SMALL_ARG[name]] = gsmall[name].reshape(given[_SMALL_ARG[name]].shape)

    delta, new_m, new_v = {}, {}, {}

    def update_matrix(name, grad):
        a = _BIG_ARG[name]
        d, m2, v2 = _adamw_big(updated(name), grad, updated(name, "m_"), updated(name, "v_"), name="adamw_" + a)
        delta[a], new_m[a], new_v[a] = (global_layout(name, t) for t in (d, m2, v2))
        return d

    all_updated = sum(update_matrix(name, grad)[:8, :LANE] for name, grad in gbig.items())
    g_in = _reduce_finish("in", _reduce_async_wait("in", *reducing["in"], all_updated), [updated("w_in").shape])[0]
    grads["w_in"] = global_layout("w_in", g_in)
    update_matrix("w_in", g_in)
    small_names = [_SMALL_ARG[name] for name, _ in SMALL]
    two_d = lambda t: t.reshape(t.shape[-2], t.shape[-1])
    ds, ms, vs = _adamw_small([two_d(given[a]) for a in small_names], [two_d(grads[a]) for a in small_names],
                              [two_d(given["m_" + a]) for a in small_names], [two_d(given["v_" + a]) for a in small_names])
    for a, d, m2, v2 in zip(small_names, ds, ms, vs):
        delta[a], new_m[a], new_v[a] = (t.reshape(given[a].shape) for t in (d, m2, v2))

    return (loss, grad_x[None], *[grads[n] for n in _WEIGHT_ORDER], *[delta[n] for n in _WEIGHT_ORDER],
            *[new_m[n] for n in _WEIGHT_ORDER], *[new_v[n] for n in _WEIGHT_ORDER])
```

```python
import functools
import math

import jax
import jax.numpy as jnp
from jax import lax
from jax.experimental import pallas as pl
from jax.experimental.pallas import tpu as pltpu

F32 = jnp.float32
BF16 = jnp.bfloat16

S = 2048
D = 1024
PLE = 256
H = 16
SSD_P = 64
SSD_INNER = 1024
SSD_N = 128
SSD_G = 2
SSD_L = 128
SSD_NC = S // SSD_L
SSD_XBC = 1536
SSD_K = 4
Q_RANK = 384
KV_RANK = 256
NOPE = 64
ROPE = 32
VDIM = 64
D_FF = 2816
IN_WIDTH = 3248
ALPHA = 2.0 ** 0.25
EPS_RMS = 1e-6
EPS_LN = 1e-5
ATT_SCALE = 1.0 / math.sqrt(NOPE + ROPE)
LN2 = math.log(2.0)
ATT_SCALE_LOG2 = ATT_SCALE / LN2
LANE = 128
NCHIP = 4
SMALL_W = 896
SM_Q, SM_KV, SM_KR, SM_DT = 0, 384, 640, 768
NEG = -1e30

ADAM_LR = 0.001
ADAM_B1 = 0.9
ADAM_B2 = 0.999
ADAM_EPS = 1e-08
ADAM_WD = 0.01
ADAM_STEP = 10


def _sigmoid(v):
    return 1.0 / (1.0 + jnp.exp(-v))


MM_VMEM_BUDGET = 36 * 2 ** 20
MM_MAX_ACC = 2048 * 1024


def _mm_tiles(pairs, ks, m, n, out_dtype, has_add):
    def divs(v):
        return [LANE * d for d in range(v // LANE, 0, -1) if (v // LANE) % d == 0] if v % LANE == 0 else [v]

    def cost(tm, tn):
        tot = tm * tn * (jnp.dtype(out_dtype).itemsize + (4 if has_add else 0))
        for (a, b), k in zip(pairs, ks):
            tot += k * (tm * a.dtype.itemsize + tn * b.dtype.itemsize)
        return 2 * tot

    ok = [(tm * tn, tm, tn) for tm in divs(m) for tn in divs(n) if tm * tn <= MM_MAX_ACC and cost(tm, tn) <= MM_VMEM_BUDGET]
    _, tm, tn = max(ok)
    return tm, tn


def _mm(pairs, *, ta=False, tb=False, out_dtype=F32, add=None, add_scale=1.0, chunk=None, name):
    n_pairs = len(pairs)
    windows = [pr[2] if len(pr) == 3 else None for pr in pairs]
    pairs = [pr[:2] for pr in pairs]
    assert not ((ta or tb) and any(windows))
    ks = [w[2] if w else (a.shape[-2] if ta else a.shape[-1]) for (a, _), w in zip(pairs, windows)]
    a0, b0 = pairs[0]
    m = a0.shape[-1] if ta else a0.shape[-2]
    n = b0.shape[-2] if tb else b0.shape[-1]
    tm, tn = _mm_tiles(pairs, ks, m, n, out_dtype, add is not None)
    dims = (((0 if ta else 1,), (1 if tb else 0,)), ((), ()))
    nk = NCHIP if chunk else 1
    assert chunk != "sum" or out_dtype == F32
    flat = [i for i, (a, b) in enumerate(pairs) if a.ndim == 2 and b.ndim == 2]
    stacked = [i for i in range(n_pairs) if i not in flat]

    def body(*refs):
        o_ref = refs[-1]

        def products(which):
            acc = None
            for i in which:
                a = refs[2 * i][...].astype(BF16)
                b = refs[2 * i + 1][...].astype(BF16)
                part = lax.dot_general(a, b, dims, preferred_element_type=F32)
                acc = part if acc is None else acc + part
            return acc

        if chunk == "sum":
            k = pl.program_id(2)
            acc = products(stacked)

            @pl.when(k == 0)
            def _():
                first = acc + products(flat) if flat else acc
                o_ref[...] = first + add_scale * refs[2 * n_pairs][...] if add is not None else first

            @pl.when(k > 0)
            def _():
                o_ref[...] += acc
            return
        acc = products(range(n_pairs))
        if add is not None:
            acc = acc + add_scale * refs[2 * n_pairs][...]
        o_ref[...] = acc.astype(out_dtype)

    def spec(arr, shape, idx2):
        if arr.ndim == 3:
            return pl.BlockSpec((None,) + shape, lambda i, j, k: (k,) + idx2(i, j))
        return pl.BlockSpec(shape, lambda i, j, k: idx2(i, j))

    in_specs, args = [], []
    for (a, b), kdim, window in zip(pairs, ks, windows):
        ka, kb = window[:2] if window else (0, 0)
        in_specs.append(spec(a, (kdim, tm), lambda i, j: (0, i)) if ta else spec(a, (tm, kdim), lambda i, j, ka=ka: (i, ka)))
        in_specs.append(spec(b, (tn, kdim), lambda i, j: (j, 0)) if tb else spec(b, (kdim, tn), lambda i, j, kb=kb: (kb, j)))
        args += [a, b]
    if add is not None:
        in_specs.append(pl.BlockSpec((tm, tn), lambda i, j, k: (i, j)))
        args.append(add)
    if chunk == "out":
        out_spec = pl.BlockSpec((None, tm, tn), lambda i, j, k: (k, i, j))
        out_shape = jax.ShapeDtypeStruct((nk, m, n), out_dtype)
    else:
        out_spec = pl.BlockSpec((tm, tn), lambda i, j, k: (i, j))
        out_shape = jax.ShapeDtypeStruct((m, n), out_dtype)
    return pl.pallas_call(
        body, name=name, grid=(m // tm, n // tn, nk), in_specs=in_specs, out_specs=out_spec, out_shape=out_shape,
        compiler_params=pltpu.CompilerParams(dimension_semantics=("parallel", "parallel", "arbitrary")),
    )(*args)


TR = 256


def _row_spec(c):
    return pl.BlockSpec((TR, c), lambda i: (i, 0))


def _vec_spec(c):
    return pl.BlockSpec((1, c), lambda i: (0, 0))


def _acc_rows(ref, val):
    @pl.when(pl.program_id(0) == 0)
    def _():
        ref[...] = jnp.zeros_like(ref)
    ref[...] += val


def _rms_fwd(u, w, *, name):
    c = u.shape[1]

    def body(u_ref, w_ref, o_ref):
        v = u_ref[...]
        r = lax.rsqrt(jnp.mean(v * v, axis=-1, keepdims=True) + EPS_RMS)
        o_ref[...] = (v * r * w_ref[...]).astype(BF16)

    return pl.pallas_call(body, name=name, grid=(S // TR,), in_specs=[_row_spec(c), _vec_spec(c)], out_specs=_row_spec(c),
                          out_shape=jax.ShapeDtypeStruct((S, c), BF16))(u, w)


def _rms_bwd(u, w, dy, *, name):
    c = u.shape[1]

    def body(u_ref, w_ref, dy_ref, du_ref, dw_ref):
        v = u_ref[...]
        g = dy_ref[...].astype(F32)
        r = lax.rsqrt(jnp.mean(v * v, axis=-1, keepdims=True) + EPS_RMS)
        gw = g * w_ref[...]
        du_ref[...] = r * gw - v * (r * r * r * jnp.mean(gw * v, axis=-1, keepdims=True))
        _acc_rows(dw_ref, jnp.sum(g * v * r, axis=0, keepdims=True))

    return pl.pallas_call(body, name=name, grid=(S // TR,), in_specs=[_row_spec(c), _vec_spec(c), _row_spec(c)],
                          out_specs=[_row_spec(c), _vec_spec(c)],
                          out_shape=[jax.ShapeDtypeStruct((S, c), F32), jax.ShapeDtypeStruct((1, c), F32)])(u, w, dy)


def _gate_norm_fwd(y, z, w):
    def body(y_ref, z_ref, w_ref, o_ref):
        zz = z_ref[...]
        v = y_ref[...] * (zz * _sigmoid(zz))
        r = lax.rsqrt(jnp.mean(v * v, axis=-1, keepdims=True) + EPS_RMS)
        o_ref[...] = (v * r * w_ref[...]).astype(BF16)

    c = SSD_INNER
    return pl.pallas_call(body, name="ssd_gate_norm_fwd", grid=(S // TR,), in_specs=[_row_spec(c), _row_spec(c), _vec_spec(c)],
                          out_specs=_row_spec(c), out_shape=jax.ShapeDtypeStruct((S, c), BF16))(y, z, w)


def _gate_norm_bwd(y, z, w, dout):
    def body(y_ref, z_ref, w_ref, g_ref, dy_ref, dz_ref, dw_ref):
        yy = y_ref[...]
        zz = z_ref[...]
        sg = _sigmoid(zz)
        sz = zz * sg
        v = yy * sz
        g = g_ref[...]
        r = lax.rsqrt(jnp.mean(v * v, axis=-1, keepdims=True) + EPS_RMS)
        gw = g * w_ref[...]
        dv = r * gw - v * (r * r * r * jnp.mean(gw * v, axis=-1, keepdims=True))
        dy_ref[...] = dv * sz
        dz_ref[...] = (dv * yy * (sg * (1.0 + zz * (1.0 - sg)))).astype(BF16)
        _acc_rows(dw_ref, jnp.sum(g * v * r, axis=0, keepdims=True))

    c = SSD_INNER
    return pl.pallas_call(body, name="ssd_gate_norm_bwd", grid=(S // TR,),
                          in_specs=[_row_spec(c), _row_spec(c), _vec_spec(c), _row_spec(c)],
                          out_specs=[_row_spec(c), _row_spec(c), _vec_spec(c)],
                          out_shape=[jax.ShapeDtypeStruct((S, c), F32), jax.ShapeDtypeStruct((S, c), BF16),
                                     jax.ShapeDtypeStruct((1, c), F32)])(y, z, w, dout)


MIX_ROWS = 512


def _out_proj_ln(y_ssd, y_mla, w_out, xr, g, b):
    k = y_ssd.shape[1]

    def body(ys_ref, ym_ref, w_ref, x_ref, g_ref, b_ref, m_ref, o_ref, ob_ref):
        mix = (jnp.dot(ys_ref[...], w_ref[:k], preferred_element_type=F32)
               + jnp.dot(ym_ref[...], w_ref[k:], preferred_element_type=F32))
        m_ref[...] = mix
        pre = ALPHA * x_ref[...] + mix
        mu = jnp.mean(pre, axis=-1, keepdims=True)
        d = pre - mu
        rs = lax.rsqrt(jnp.mean(d * d, axis=-1, keepdims=True) + EPS_LN)
        h = d * rs * g_ref[...] + b_ref[...]
        o_ref[...] = h
        ob_ref[...] = h.astype(BF16)

    rows = lambda c: pl.BlockSpec((MIX_ROWS, c), lambda i: (i, 0))
    return pl.pallas_call(
        body, name="out_proj_ln", grid=(S // MIX_ROWS,),
        in_specs=[rows(k), rows(k), _whole(w_out), rows(D), _vec_spec(D), _vec_spec(D)], out_specs=[rows(D)] * 3,
        out_shape=[jax.ShapeDtypeStruct((S, D), F32), jax.ShapeDtypeStruct((S, D), F32), jax.ShapeDtypeStruct((S, D), BF16)],
    )(y_ssd, y_mla, w_out, xr, g, b)


def _ln_bwd(xr, mix, g, dh, w_out):
    k = w_out.shape[0] // 2

    def body(x_ref, m_ref, g_ref, dh_ref, w_ref, dpre_ref, dpreb_ref, dg_ref, db_ref, dys_ref, dym_ref):
        pre = ALPHA * x_ref[...] + m_ref[...]
        mu = jnp.mean(pre, axis=-1, keepdims=True)
        d = pre - mu
        rs = lax.rsqrt(jnp.mean(d * d, axis=-1, keepdims=True) + EPS_LN)
        xh = d * rs
        dy = dh_ref[...]
        gy = dy * g_ref[...]
        dpre = rs * (gy - jnp.mean(gy, axis=-1, keepdims=True) - xh * jnp.mean(gy * xh, axis=-1, keepdims=True))
        dpre_ref[...] = dpre
        dpreb = dpre.astype(BF16)
        dpreb_ref[...] = dpreb
        _acc_rows(dg_ref, jnp.sum(dy * xh, axis=0, keepdims=True))
        _acc_rows(db_ref, jnp.sum(dy, axis=0, keepdims=True))
        dys_ref[...] = lax.dot_general(dpreb, w_ref[:k], (((1,), (1,)), ((), ())), preferred_element_type=F32)
        dym_ref[...] = lax.dot_general(dpreb, w_ref[k:], (((1,), (1,)), ((), ())), preferred_element_type=F32)

    rows = lambda c: pl.BlockSpec((MIX_ROWS, c), lambda i: (i, 0))
    return pl.pallas_call(
        body, name="ln_mix_bwd", grid=(S // MIX_ROWS,),
        in_specs=[rows(D), rows(D), _vec_spec(D), rows(D), _whole(w_out)],
        out_specs=[rows(D), rows(D), _vec_spec(D), _vec_spec(D), rows(k), rows(k)],
        out_shape=[jax.ShapeDtypeStruct((S, D), F32), jax.ShapeDtypeStruct((S, D), BF16), jax.ShapeDtypeStruct((1, D), F32),
                   jax.ShapeDtypeStruct((1, D), F32), jax.ShapeDtypeStruct((S, k), F32), jax.ShapeDtypeStruct((S, k), F32)],
    )(xr, mix, g, dh, w_out)


FF_CHUNK = D_FF // NCHIP


FF_ROWS = 1024


def _ff_act_spec():
    return pl.BlockSpec((None, FF_ROWS, FF_CHUNK), lambda i, k: (k, i, 0))


def _ff_w_spec():
    return pl.BlockSpec((None, FF_CHUNK, D), lambda i, k: (k, 0, 0))


def _ffn_hidden_fwd(h, w_gate_t, w_up_t):
    def body(h_ref, wg_ref, wu_ref, g_ref, u_ref, a_ref):
        hh = h_ref[...]
        g = _dot(hh, wg_ref[...], ((1,), (1,)))
        u = _dot(hh, wu_ref[...], ((1,), (1,)))
        g_ref[...] = g.astype(BF16)
        u_ref[...] = u.astype(BF16)
        a_ref[...] = (g * _sigmoid(g) * u).astype(BF16)

    return pl.pallas_call(
        body, name="ffn_hidden_fwd", grid=(S // FF_ROWS, NCHIP),
        in_specs=[pl.BlockSpec((FF_ROWS, D), lambda i, k: (i, 0)), _ff_w_spec(), _ff_w_spec()], out_specs=[_ff_act_spec()] * 3,
        out_shape=[jax.ShapeDtypeStruct((NCHIP, S, FF_CHUNK), BF16)] * 3,
        compiler_params=pltpu.CompilerParams(dimension_semantics=("parallel", "parallel")),
    )(h, w_gate_t, w_up_t)


def _ffn_hidden_bwd(dout, w_down, gate, up):
    def body(d_ref, wd_ref, g_ref, u_ref, dg_ref, du_ref):
        d = _dot(d_ref[...], wd_ref[...], ((1,), (1,)))
        g = g_ref[...].astype(F32)
        sg = _sigmoid(g)
        dg_ref[...] = (d * u_ref[...].astype(F32) * (sg * (1.0 + g * (1.0 - sg)))).astype(BF16)
        du_ref[...] = (d * g * sg).astype(BF16)

    return pl.pallas_call(
        body, name="ffn_hidden_bwd", grid=(S // FF_ROWS, NCHIP),
        in_specs=[pl.BlockSpec((FF_ROWS, D), lambda i, k: (i, 0)), _ff_w_spec(), _ff_act_spec(), _ff_act_spec()],
        out_specs=[_ff_act_spec()] * 2, out_shape=[jax.ShapeDtypeStruct((NCHIP, S, FF_CHUNK), BF16)] * 2,
        compiler_params=pltpu.CompilerParams(dimension_semantics=("parallel", "parallel")),
    )(dout, w_down, gate, up)


def _final_fwd_bwd(h1, ffn, h1b, pb, w_pg, w_pp, target, g2, b2):
    def body(h_ref, f_ref, hb_ref, pb_ref, wpg_ref, wpp_ref, t_ref, g_ref, b_ref,
             dpre_ref, dpreb_ref, dpg_ref, dpp_ref, dg_ref, db_ref, loss_ref):
        sg = _sigmoid(jnp.dot(hb_ref[...], wpg_ref[...], preferred_element_type=F32))
        ppv = jnp.dot(pb_ref[...], wpp_ref[...], preferred_element_type=F32)
        pre = ALPHA * h_ref[...] + f_ref[...] + sg * ppv
        mu = jnp.mean(pre, axis=-1, keepdims=True)
        d = pre - mu
        rs = lax.rsqrt(jnp.mean(d * d, axis=-1, keepdims=True) + EPS_LN)
        xh = d * rs
        err = xh * g_ref[...] + b_ref[...] - t_ref[...]
        dy = err * (1.0 / D)
        gy = dy * g_ref[...]
        dpre = rs * (gy - jnp.mean(gy, axis=-1, keepdims=True) - xh * jnp.mean(gy * xh, axis=-1, keepdims=True))
        dpre_ref[...] = dpre
        dpreb_ref[...] = dpre.astype(BF16)
        dpg_ref[...] = (dpre * ppv * sg * (1.0 - sg)).astype(BF16)
        dpp_ref[...] = (dpre * sg).astype(BF16)
        _acc_rows(dg_ref, jnp.sum(dy * xh, axis=0, keepdims=True))
        _acc_rows(db_ref, jnp.sum(dy, axis=0, keepdims=True))
        _acc_rows(loss_ref, 0.5 * jnp.sum(jnp.mean(err * err, axis=-1, keepdims=True), axis=0, keepdims=True) * jnp.ones((1, LANE), F32))

    return pl.pallas_call(
        body, name="final_ln_loss", grid=(S // TR,),
        in_specs=[_row_spec(D)] * 3 + [_row_spec(pb.shape[1]), _whole(w_pg), _whole(w_pp), _row_spec(D)] + [_vec_spec(D)] * 2,
        out_specs=[_row_spec(D)] * 4 + [_vec_spec(D), _vec_spec(D), _vec_spec(LANE)],
        out_shape=[jax.ShapeDtypeStruct((S, D), F32)] + [jax.ShapeDtypeStruct((S, D), BF16)] * 3 + [
                   jax.ShapeDtypeStruct((1, D), F32), jax.ShapeDtypeStruct((1, D), F32), jax.ShapeDtypeStruct((1, LANE), F32)],
    )(h1, ffn, h1b, pb, w_pg, w_pp, target, g2, b2)


def _rot(u, cos_t, sin_t, lane):
    partner = jnp.where(lane < NOPE + ROPE // 2, pltpu.roll(u, LANE - ROPE // 2, 1), pltpu.roll(u, ROPE // 2, 1))
    return u * cos_t + partner * sin_t


def _rms(v, w):
    r = lax.rsqrt(jnp.mean(v * v, axis=-1, keepdims=True) + EPS_RMS)
    return v * r * w, r


def _rms_grad(v, r, w, g):
    gw = g * w
    return r * gw - v * (r * r * r * jnp.mean(gw * v, axis=-1, keepdims=True)), jnp.sum(g * v * r, axis=0, keepdims=True)


def _whole(arr):
    return pl.BlockSpec(arr.shape, lambda i: (0,) * arr.ndim)


def _qkv_fwd(small, w_q, w_k, w_v, q_norm, kv_norm, cos_t, sin_t):
    def body(sm_ref, wq_ref, wk_ref, wv_ref, qw_ref, kw_ref, c_ref, s_ref, qn_ref, kvn_ref, q_ref, k_ref, kt_ref, v_ref):
        lane = lax.broadcasted_iota(jnp.int32, (TR, LANE), 1)
        c, s = c_ref[...], s_ref[...]
        qn = _rms(sm_ref[:, SM_Q:SM_Q + Q_RANK], qw_ref[...])[0].astype(BF16)
        kvn = _rms(sm_ref[:, SM_KV:SM_KV + KV_RANK], kw_ref[...])[0].astype(BF16)
        qn_ref[...] = qn
        kvn_ref[...] = kvn
        kr = _rot(pltpu.roll(sm_ref[:, SM_KR:SM_KR + LANE], NOPE, 1), c, s, lane)
        for h in range(H):
            tile = slice(h * LANE, (h + 1) * LANE)
            q_ref[:, tile] = _rot(_dot(qn, wq_ref[:, tile], ((1,), (0,))), c, s, lane).astype(BF16)
            kt = _dot(kvn, wk_ref[:, tile], ((1,), (0,))) + kr
            k_ref[:, tile] = kt.astype(BF16)
            kt_ref[tile, :] = kt.T.astype(BF16)
        v_ref[...] = _dot(kvn, wv_ref[...], ((1,), (0,))).astype(BF16)

    w = H * LANE
    return pl.pallas_call(
        body, name="qkv_fwd", grid=(S // TR,),
        in_specs=[_row_spec(SMALL_W), _whole(w_q), _whole(w_k), _whole(w_v), _vec_spec(Q_RANK), _vec_spec(KV_RANK), _row_spec(LANE), _row_spec(LANE)],
        out_specs=[_row_spec(Q_RANK), _row_spec(KV_RANK), _row_spec(w), _row_spec(w), pl.BlockSpec((w, TR), lambda i: (0, i)),
                   _row_spec(H * VDIM)],
        out_shape=[jax.ShapeDtypeStruct((S, Q_RANK), BF16), jax.ShapeDtypeStruct((S, KV_RANK), BF16), jax.ShapeDtypeStruct((S, w), BF16),
                   jax.ShapeDtypeStruct((S, w), BF16), jax.ShapeDtypeStruct((w, S), BF16), jax.ShapeDtypeStruct((S, H * VDIM), BF16)],
    )(small, w_q, w_k, w_v, q_norm, kv_norm, cos_t, sin_t)


def _qkv_bwd(dqt, dk, dv, small, w_q, w_k, w_v, q_norm, kv_norm, cos_t, sin_t):
    def body(dq_ref, dk_ref, dv_ref, sm_ref, wq_ref, wk_ref, wv_ref, qw_ref, kw_ref, c_ref, s_ref,
             ds_ref, dql_ref, dkb_ref, dqw_ref, dkw_ref):
        lane = lax.broadcasted_iota(jnp.int32, (TR, LANE), 1)
        c, s = c_ref[...], -s_ref[...]
        dqn = jnp.zeros((TR, Q_RANK), F32)
        dkvn = _dot(dv_ref[...], wv_ref[...], ((1,), (1,)))
        dkr = jnp.zeros((TR, LANE), F32)
        for h in range(H):
            tile = slice(h * LANE, (h + 1) * LANE)
            dql = _rot(dq_ref[tile, :].T, c, s, lane).astype(BF16)
            dql_ref[:, tile] = dql
            dqn = dqn + _dot(dql, wq_ref[:, tile], ((1,), (1,)))
            dkt = dk_ref[:, tile]
            dkb_ref[:, tile] = dkt.astype(BF16)
            dkvn = dkvn + _dot(dkt, wk_ref[:, tile], ((1,), (1,)))
            dkr = dkr + dkt
        dkr = jnp.where((lane >= NOPE) & (lane < NOPE + ROPE), dkr, 0.0)
        q_c, kv_c = sm_ref[:, SM_Q:SM_Q + Q_RANK], sm_ref[:, SM_KV:SM_KV + KV_RANK]
        dq_c, dqw = _rms_grad(q_c, _rms(q_c, qw_ref[...])[1], qw_ref[...], dqn)
        dkv_c, dkw = _rms_grad(kv_c, _rms(kv_c, kw_ref[...])[1], kw_ref[...], dkvn)
        ds_ref[:, SM_Q:SM_Q + Q_RANK] = dq_c.astype(BF16)
        ds_ref[:, SM_KV:SM_KV + KV_RANK] = dkv_c.astype(BF16)
        ds_ref[:, SM_KR:SM_KR + LANE] = pltpu.roll(_rot(dkr, c, s, lane), LANE - NOPE, 1).astype(BF16)
        _acc_rows(dqw_ref, dqw)
        _acc_rows(dkw_ref, dkw)

    w = H * LANE
    return pl.pallas_call(
        body, name="qkv_bwd", grid=(S // TR,),
        in_specs=[pl.BlockSpec((w, TR), lambda i: (0, i)), _row_spec(w), _row_spec(H * VDIM), _row_spec(SMALL_W), _whole(w_q), _whole(w_k),
                  _whole(w_v), _vec_spec(Q_RANK), _vec_spec(KV_RANK), _row_spec(LANE), _row_spec(LANE)],
        out_specs=[_row_spec(SM_DT), _row_spec(w), _row_spec(w), _vec_spec(Q_RANK), _vec_spec(KV_RANK)],
        out_shape=[jax.ShapeDtypeStruct((S, SM_DT), BF16), jax.ShapeDtypeStruct((S, w), BF16), jax.ShapeDtypeStruct((S, w), BF16),
                   jax.ShapeDtypeStruct((1, Q_RANK), F32), jax.ShapeDtypeStruct((1, KV_RANK), F32)],
    )(dqt, dk, dv, small, w_q, w_k, w_v, q_norm, kv_norm, cos_t, sin_t)


CB = 256


def _shift_down(u, k, row):
    if k == 0:
        return u
    return jnp.where(row >= k, pltpu.roll(u, k, 0), 0.0)


def _shift_up(u, k, row):
    if k == 0:
        return u
    return jnp.where(row < S - k, pltpu.roll(u, S - k, 0), 0.0)


def _conv_fwd(u, w, b):
    def body(u_ref, w_ref, b_ref, o_ref):
        row = lax.broadcasted_iota(jnp.int32, (S, CB), 0)
        uu = u_ref[...]
        acc = b_ref[...] + w_ref[SSD_K - 1:SSD_K, :] * uu
        for k in range(SSD_K - 1):
            acc = acc + w_ref[k:k + 1, :] * _shift_down(uu, SSD_K - 1 - k, row)
        o_ref[...] = acc * _sigmoid(acc)

    c = u.shape[1]
    return pl.pallas_call(
        body, name="conv_fwd", grid=(c // CB,),
        in_specs=[pl.BlockSpec((S, CB), lambda j: (0, j)), pl.BlockSpec((SSD_K, CB), lambda j: (0, j)), pl.BlockSpec((1, CB), lambda j: (0, j))],
        out_specs=pl.BlockSpec((S, CB), lambda j: (0, j)), out_shape=jax.ShapeDtypeStruct((S, c), F32),
    )(u, w, b)


def _conv_bwd(u, w, b, dact):
    def body(u_ref, w_ref, b_ref, d_ref, du_ref, dw_ref, db_ref):
        row = lax.broadcasted_iota(jnp.int32, (S, CB), 0)
        uu = u_ref[...]
        sh = [_shift_down(uu, SSD_K - 1 - k, row) for k in range(SSD_K)]
        acc = b_ref[...]
        for k in range(SSD_K):
            acc = acc + w_ref[k:k + 1, :] * sh[k]
        sg = _sigmoid(acc)
        dacc = d_ref[...] * (sg * (1.0 + acc * (1.0 - sg)))
        du = w_ref[SSD_K - 1:SSD_K, :] * dacc
        for k in range(SSD_K - 1):
            du = du + w_ref[k:k + 1, :] * _shift_up(dacc, SSD_K - 1 - k, row)
        du_ref[...] = du.astype(BF16)
        for k in range(SSD_K):
            dw_ref[k:k + 1, :] = jnp.sum(dacc * sh[k], axis=0, keepdims=True)
        db_ref[...] = jnp.sum(dacc, axis=0, keepdims=True)

    c = u.shape[1]
    col = lambda r: pl.BlockSpec((r, CB), lambda j: (0, j))
    return pl.pallas_call(
        body, name="conv_bwd", grid=(c // CB,), in_specs=[col(S), col(SSD_K), col(1), col(S)], out_specs=[col(S), col(SSD_K), col(1)],
        out_shape=[jax.ShapeDtypeStruct((S, c), BF16), jax.ShapeDtypeStruct((SSD_K, c), F32), jax.ShapeDtypeStruct((1, c), F32)],
    )(u, w, b, dact)


NPAIR = H // 2
PAIRS_PER_GROUP = NPAIR // SSD_G


def _softplus(v):
    return jnp.maximum(v, 0.0) + jnp.log(1.0 + jnp.exp(-jnp.abs(v)))


def _dot(a, b, dims):
    return lax.dot_general(a.astype(BF16), b.astype(BF16), (dims, ((), ())), preferred_element_type=F32)


def _dot2(a, sel):
    hi = a.astype(BF16)
    lo = (a - hi.astype(F32)).astype(BF16)
    dims = (((1,), (0,)), ((), ()))
    return lax.dot_general(hi, sel, dims, preferred_element_type=F32) + lax.dot_general(lo, sel, dims, preferred_element_type=F32)


def _dot3(a, b, dims, split_lhs):
    v = a if split_lhs else b
    v1 = v.astype(BF16)
    r1 = v - v1.astype(F32)
    v2 = r1.astype(BF16)
    v3 = (r1 - v2.astype(F32)).astype(BF16)
    acc = None
    for part in (v1, v2, v3):
        lhs, rhs = (part, b) if split_lhs else (a, part)
        t = lax.dot_general(lhs, rhs, (dims, ((), ())), preferred_element_type=F32)
        acc = t if acc is None else acc + t
    return acc


def _ssd_chunk_common(dt_ref, dtT_ref, prow_ref, pcol_ref):
    prow = prow_ref[...]
    pcol = pcol_ref[...]
    ri = lax.broadcasted_iota(jnp.int32, (SSD_L, SSD_L), 0)
    ci = lax.broadcasted_iota(jnp.int32, (SSD_L, SSD_L), 1)
    causal = ri >= ci
    pre_c = dt_ref[...] + prow[0:1, :]
    dtc = _softplus(pre_c)
    a_row = -jnp.exp(prow[1:2, :])
    cs_col = _dot3(causal.astype(BF16), dtc * a_row, ((1,), (0,)), False)
    dtr = _softplus(dtT_ref[...] + pcol[:, 0:1])
    a_col = -jnp.exp(pcol[:, 1:2])
    cs_row = _dot3(dtr * a_col, (ri <= ci).astype(BF16), ((1,), (0,)), True)
    return prow, causal, pre_c, dtc, a_row, cs_col, cs_row


def _ssd_fwd(act, small, dtT, prow, pcol):
    def body(x_ref, b_ref, c_ref, dt_ref, dtT_ref, prow_ref, pcol_ref, y_ref, st_ref, state):
        @pl.when(pl.program_id(0) == 0)
        def _():
            state[...] = jnp.zeros_like(state)

        prow, causal, _, dtc, _, cs_col, cs_row = _ssd_chunk_common(dt_ref, dtT_ref, prow_ref, pcol_ref)
        lo = lax.broadcasted_iota(jnp.int32, (SSD_L, LANE), 1) < SSD_P
        lo1 = lo[0:1, :]
        for g in range(SSD_G):
            bm = b_ref[:, g * SSD_N:(g + 1) * SSD_N]
            cm = c_ref[:, g * SSD_N:(g + 1) * SSD_N]
            cb = _dot(cm, bm, ((1,), (1,)))
            for qq in range(PAIRS_PER_GROUP):
                q = g * PAIRS_PER_GROUP + qq
                ha, hb = 2 * q, 2 * q + 1
                csa, csb = cs_col[:, ha:ha + 1], cs_col[:, hb:hb + 1]
                xp = x_ref[:, q * LANE:(q + 1) * LANE]
                xx = xp * jnp.where(lo, dtc[:, ha:ha + 1], dtc[:, hb:hb + 1])
                ga = cb * jnp.exp(jnp.where(causal, csa - cs_row[ha:ha + 1, :], NEG))
                gb = cb * jnp.exp(jnp.where(causal, csb - cs_row[hb:hb + 1, :], NEG))
                y = _dot(ga, jnp.where(lo, xx, 0.0), ((1,), (0,))) + _dot(gb, jnp.where(lo, 0.0, xx), ((1,), (0,)))
                s_in = state[q]
                y = y + _dot(cm, s_in, ((1,), (0,))) * jnp.where(lo, jnp.exp(csa), jnp.exp(csb))
                y = y + jnp.where(lo1, prow[2:3, ha:ha + 1], prow[2:3, hb:hb + 1]) * xp
                y_ref[:, q * LANE:(q + 1) * LANE] = y
                la, lb = csa[SSD_L - 1:SSD_L, :], csb[SSD_L - 1:SSD_L, :]
                decay = jnp.where(lo, jnp.exp(la - csa), jnp.exp(lb - csb))
                st_ref[q] = s_in
                state[q] = s_in * jnp.where(lo1, jnp.exp(la), jnp.exp(lb)) + _dot(bm, xx * decay, ((0,), (0,)))

    L = SSD_L
    return pl.pallas_call(
        body, name="ssd_fwd", grid=(SSD_NC,),
        in_specs=[pl.BlockSpec((L, SSD_INNER), lambda c: (c, 0)),
                  pl.BlockSpec((L, SSD_G * SSD_N), lambda c: (c, SSD_INNER // (SSD_G * SSD_N))),
                  pl.BlockSpec((L, SSD_G * SSD_N), lambda c: (c, SSD_INNER // (SSD_G * SSD_N) + 1)),
                  pl.BlockSpec((L, LANE), lambda c: (c, SM_DT // LANE)),
                  pl.BlockSpec((LANE, L), lambda c: (0, c)),
                  pl.BlockSpec((8, LANE), lambda c: (0, 0)), pl.BlockSpec((LANE, 8), lambda c: (0, 0))],
        out_specs=[pl.BlockSpec((L, SSD_INNER), lambda c: (c, 0)),
                   pl.BlockSpec((None, NPAIR, SSD_N, LANE), lambda c: (c, 0, 0, 0))],
        out_shape=[jax.ShapeDtypeStruct((S, SSD_INNER), F32), jax.ShapeDtypeStruct((SSD_NC, NPAIR, SSD_N, LANE), F32)],
        scratch_shapes=[pltpu.VMEM((NPAIR, SSD_N, LANE), F32)],
        compiler_params=pltpu.CompilerParams(dimension_semantics=("arbitrary",)),
    )(act, act, act, small, dtT, prow, pcol)


def _ssd_bwd(act, small, dtT, prow, pcol, states, dy):
    def body(x_ref, b_ref, c_ref, dt_ref, dtT_ref, prow_ref, pcol_ref, st_ref, dy_ref,
             dx_ref, ddt_ref, dp_ref, dstate):
        @pl.when(pl.program_id(0) == 0)
        def _():
            dstate[...] = jnp.zeros_like(dstate)
            dp_ref[...] = jnp.zeros_like(dp_ref)

        prow, causal, pre_c, dtc, a_row, cs_col, cs_row = _ssd_chunk_common(dt_ref, dtT_ref, prow_ref, pcol_ref)
        lane = lax.broadcasted_iota(jnp.int32, (SSD_L, LANE), 1)
        sub = lax.broadcasted_iota(jnp.int32, (LANE, SSD_L), 0)
        rowi = lax.broadcasted_iota(jnp.int32, (SSD_L, 1), 0)
        pick_p = lax.broadcasted_iota(jnp.int32, (LANE, LANE), 0)
        pick_l = lax.broadcasted_iota(jnp.int32, (LANE, LANE), 1)
        lo = lane < SSD_P
        lo1 = lo[0:1, :]
        dcs_c = jnp.zeros((SSD_L, LANE), F32)
        dcs_r = jnp.zeros((LANE, SSD_L), F32)
        ddt_x = jnp.zeros((SSD_L, LANE), F32)
        dd_row = jnp.zeros((1, LANE), F32)
        for g in range(SSD_G):
            bm = b_ref[:, g * SSD_N:(g + 1) * SSD_N]
            cm = c_ref[:, g * SSD_N:(g + 1) * SSD_N]
            cb = _dot(cm, bm, ((1,), (1,)))
            dcb = jnp.zeros((SSD_L, SSD_L), F32)
            dbm = jnp.zeros((SSD_L, SSD_N), F32)
            dcm = jnp.zeros((SSD_L, SSD_N), F32)
            for qq in range(PAIRS_PER_GROUP):
                q = g * PAIRS_PER_GROUP + qq
                ha, hb = 2 * q, 2 * q + 1
                csa, csb = cs_col[:, ha:ha + 1], cs_col[:, hb:hb + 1]
                xp = x_ref[:, q * LANE:(q + 1) * LANE]
                dtp = jnp.where(lo, dtc[:, ha:ha + 1], dtc[:, hb:hb + 1])
                xx = xp * dtp
                lma = jnp.exp(jnp.where(causal, csa - cs_row[ha:ha + 1, :], NEG))
                lmb = jnp.exp(jnp.where(causal, csb - cs_row[hb:hb + 1, :], NEG))
                ga, gb = cb * lma, cb * lmb
                dyp = dy_ref[:, q * LANE:(q + 1) * LANE]
                dya, dyb = jnp.where(lo, dyp, 0.0), jnp.where(lo, 0.0, dyp)
                s_in = st_ref[q]
                ds_out = dstate[q]
                la, lb = csa[SSD_L - 1:SSD_L, :], csb[SSD_L - 1:SSD_L, :]
                ecs = jnp.where(lo, jnp.exp(csa), jnp.exp(csb))
                decay = jnp.where(lo, jnp.exp(la - csa), jnp.exp(lb - csb))
                cd = jnp.where(lo1, jnp.exp(la), jnp.exp(lb))
                bds = _dot(bm, ds_out, ((1,), (0,)))
                dxx = _dot(ga, dya, ((0,), (0,))) + _dot(gb, dyb, ((0,), (0,))) + bds * decay
                dga = _dot(dya, xx, ((1,), (1,)))
                dgb = _dot(dyb, xx, ((1,), (1,)))
                dsega, dsegb = dga * ga, dgb * gb
                dcb = dcb + dga * lma + dgb * lmb
                yoff = _dot(cm, s_in, ((1,), (0,))) * ecs
                dye = dyp * ecs
                dcm = dcm + _dot(dye, s_in, ((1,), (1,)))
                xd = xx * decay
                dbm = dbm + _dot(xd, ds_out, ((1,), (1,)))
                wv = xd * bds
                ends = jnp.sum(wv, axis=0, keepdims=True) + cd * jnp.sum(ds_out * s_in, axis=0, keepdims=True)
                t1 = dyp * yoff - wv + jnp.where(rowi == SSD_L - 1, ends, 0.0)
                to_pair = (((pick_p < SSD_P) & (pick_l == ha)) | ((pick_p >= SSD_P) & (pick_l == hb))).astype(BF16)
                to_a_b = jnp.concatenate([(pick_l == ha).astype(BF16), (pick_l == hb).astype(BF16)], axis=0)
                dcs_c = dcs_c + _dot2(t1, to_pair) + _dot2(jnp.concatenate([dsega, dsegb], axis=1), to_a_b)
                dcs_r = (dcs_r + jnp.where(sub == ha, jnp.sum(dsega, axis=0, keepdims=True), 0.0)
                         + jnp.where(sub == hb, jnp.sum(dsegb, axis=0, keepdims=True), 0.0))
                dstate[q] = _dot(cm, dye, ((0,), (0,))) + cd * ds_out
                dpair = jnp.where(lo1, prow[2:3, ha:ha + 1], prow[2:3, hb:hb + 1])
                dx_ref[:, q * LANE:(q + 1) * LANE] = dxx * dtp + dpair * dyp
                ddt_x = ddt_x + _dot2(dxx * xp, to_pair)
                dd_row = dd_row + jnp.sum(_dot2(dyp * xp, to_pair), axis=0, keepdims=True)
            dx_ref[:, SSD_INNER + g * SSD_N:SSD_INNER + (g + 1) * SSD_N] = dbm + _dot(dcb, cm, ((0,), (0,)))
            dx_ref[:, SSD_INNER + (SSD_G + g) * SSD_N:SSD_INNER + (SSD_G + g + 1) * SSD_N] = dcm + _dot(dcb, bm, ((1,), (0,)))
        ri = lax.broadcasted_iota(jnp.int32, (SSD_L, SSD_L), 0)
        ci = lax.broadcasted_iota(jnp.int32, (SSD_L, SSD_L), 1)
        da = _dot3((ri <= ci).astype(BF16), dcs_c, ((1,), (0,)), False)
        da = da - _dot3(dcs_r, causal.astype(BF16), ((1,), (0,)), True).T
        ddt = ddt_x + da * a_row
        ddt_raw = ddt * _sigmoid(pre_c)
        ddt_ref[...] = ddt_raw
        da_head = jnp.sum(da * dtc, axis=0, keepdims=True) * a_row
        dp_ref[0:1, :] += jnp.sum(ddt_raw, axis=0, keepdims=True)
        dp_ref[1:2, :] += da_head
        dp_ref[2:3, :] += dd_row

    L = SSD_L
    rev = SSD_NC - 1
    bc_cols = SSD_INNER // (SSD_G * SSD_N)
    return pl.pallas_call(
        body, name="ssd_bwd", grid=(SSD_NC,),
        in_specs=[pl.BlockSpec((L, SSD_INNER), lambda c: (rev - c, 0)),
                  pl.BlockSpec((L, SSD_G * SSD_N), lambda c: (rev - c, bc_cols)),
                  pl.BlockSpec((L, SSD_G * SSD_N), lambda c: (rev - c, bc_cols + 1)),
                  pl.BlockSpec((L, LANE), lambda c: (rev - c, SM_DT // LANE)),
                  pl.BlockSpec((LANE, L), lambda c: (0, rev - c)),
                  pl.BlockSpec((8, LANE), lambda c: (0, 0)), pl.BlockSpec((LANE, 8), lambda c: (0, 0)),
                  pl.BlockSpec((None, NPAIR, SSD_N, LANE), lambda c: (rev - c, 0, 0, 0)),
                  pl.BlockSpec((L, SSD_INNER), lambda c: (rev - c, 0))],
        out_specs=[pl.BlockSpec((L, SSD_XBC), lambda c: (rev - c, 0)),
                   pl.BlockSpec((L, LANE), lambda c: (rev - c, 0)),
                   pl.BlockSpec((8, LANE), lambda c: (0, 0))],
        out_shape=[jax.ShapeDtypeStruct((S, SSD_XBC), F32), jax.ShapeDtypeStruct((S, LANE), F32),
                   jax.ShapeDtypeStruct((8, LANE), F32)],
        scratch_shapes=[pltpu.VMEM((NPAIR, SSD_N, LANE), F32)],
        compiler_params=pltpu.CompilerParams(dimension_semantics=("arbitrary",)),
    )(act, act, act, small, dtT, prow, pcol, states, dy)


TQ = 256
TK = 256
FWD_TQ = 256
FWD_TK = 256


def _attn_fwd(qc, kc, v):
    TQ, TK = FWD_TQ, FWD_TK

    def body(q_ref, k_ref, v_ref, o_ref, lse_ref):
        i = pl.program_id(1)
        lo = lax.broadcasted_iota(jnp.int32, (TQ, LANE), 1) < VDIM
        lo_k = lax.broadcasted_iota(jnp.int32, (TK, LANE), 1) < VDIM
        row_minus_col = lax.broadcasted_iota(jnp.int32, (TQ, TK), 0) - lax.broadcasted_iota(jnp.int32, (TQ, TK), 1)
        qa, qb = q_ref[:, 0:LANE], q_ref[:, LANE:2 * LANE]

        def scores(kb):
            kk = k_ref[pl.ds(pl.multiple_of(kb * TK, TK), TK), :]
            return (_dot(qa, kk[:, 0:LANE], ((1,), (1,))) * ATT_SCALE_LOG2, _dot(qb, kk[:, LANE:2 * LANE], ((1,), (1,))) * ATT_SCALE_LOG2)

        def update(kb, sa, sb, stats):
            ma, la, mb, lb, acc = stats
            vv = v_ref[pl.ds(pl.multiple_of(kb * TK, TK), TK), :]
            na = jnp.maximum(ma, jnp.max(sa, axis=1, keepdims=True))
            nb = jnp.maximum(mb, jnp.max(sb, axis=1, keepdims=True))
            pa, pb = jnp.exp2(sa - na), jnp.exp2(sb - nb)
            fa, fb = jnp.exp2(ma - na), jnp.exp2(mb - nb)
            la = fa * la + jnp.sum(pa, axis=1, keepdims=True)
            lb = fb * lb + jnp.sum(pb, axis=1, keepdims=True)
            acc = (acc * jnp.where(lo, fa, fb) + _dot(pa, jnp.where(lo_k, vv, 0), ((1,), (0,)))
                   + _dot(pb, jnp.where(lo_k, 0, vv), ((1,), (0,))))
            return na, la, nb, lb, acc

        def step(kb, carry):
            sa, sb = carry[:2]
            nxt = scores(kb + 1)
            return nxt + update(kb, sa, sb, carry[2:])

        neg = jnp.full((TQ, 1), NEG, F32)
        zero = jnp.zeros((TQ, 1), F32)
        n_full = i * (TQ // TK)
        carry = lax.fori_loop(0, n_full, step, scores(0) + (neg, zero, neg, zero, jnp.zeros((TQ, LANE), F32)))
        s, stats = carry[:2], carry[2:]
        for d in range(TQ // TK):
            nxt = scores(n_full + d + 1) if d + 1 < TQ // TK else None
            sa, sb = (jnp.where(row_minus_col >= d * TK, t, NEG) for t in s)
            stats = update(n_full + d, sa, sb, stats)
            s = nxt
        ma, la, mb, lb, acc = stats
        o_ref[...] = acc / jnp.where(lo, la, lb)
        lse_ref[...] = jnp.where(lo, ma + jnp.log2(la), mb + jnp.log2(lb)) * LN2

    return pl.pallas_call(
        body, name="attn_fwd", grid=(NPAIR, S // TQ),
        in_specs=[pl.BlockSpec((TQ, 2 * LANE), lambda j, i: (i, j)), pl.BlockSpec((S, 2 * LANE), lambda j, i: (0, j)),
                  pl.BlockSpec((S, LANE), lambda j, i: (0, j))],
        out_specs=[pl.BlockSpec((TQ, LANE), lambda j, i: (i, j)), pl.BlockSpec((None, TQ, LANE), lambda j, i: (j, i, 0))],
        out_shape=[jax.ShapeDtypeStruct((S, H * VDIM), F32), jax.ShapeDtypeStruct((NPAIR, S, LANE), F32)],
        compiler_params=pltpu.CompilerParams(dimension_semantics=("parallel", "parallel")),
    )(qc, kc, v)


def _attn_rows(lse, o, do):
    def body(lse_ref, o_ref, do_ref, r_ref):
        lt = lse_ref[...].T * (1.0 / LN2)
        tt = (o_ref[...] * do_ref[...]).T
        r_ref[...] = jnp.zeros_like(r_ref)
        r_ref[0:1, :] = lt[0:1, :]
        r_ref[1:2, :] = lt[VDIM:VDIM + 1, :]
        r_ref[2:3, :] = jnp.sum(tt[0:VDIM, :], axis=0, keepdims=True)
        r_ref[3:4, :] = jnp.sum(tt[VDIM:LANE, :], axis=0, keepdims=True)

    tile = pl.BlockSpec((S, LANE), lambda j: (0, j))
    return pl.pallas_call(
        body, name="attn_rows", grid=(NPAIR,), in_specs=[pl.BlockSpec((None, S, LANE), lambda j: (j, 0, 0)), tile, tile],
        out_specs=pl.BlockSpec((None, 8, S), lambda j: (j, 0, 0)), out_shape=jax.ShapeDtypeStruct((NPAIR, 8, S), F32),
    )(lse, o, do)


def _attn_bwd(qc, kc, kct, v, do, rows):
    nq = S // TQ

    def body(q_ref, k_ref, kt_ref, v_ref, do_ref, r_ref, dqt_ref, dk_ref, dv_ref):
        kb = pl.program_id(1)

        @pl.when(kb == 0)
        def _():
            dqt_ref[...] = jnp.zeros_like(dqt_ref)

        lo = lax.broadcasted_iota(jnp.int32, (TK, LANE), 1) < VDIM
        q_minus_k = lax.broadcasted_iota(jnp.int32, (TK, TQ), 1) - lax.broadcasted_iota(jnp.int32, (TK, TQ), 0)
        vv = v_ref[...]
        kk = k_ref[...]

        def step(qi, carry):
            off = pl.multiple_of(qi * TQ, TQ)
            qq = q_ref[pl.ds(off, TQ), :]
            dd = do_ref[pl.ds(off, TQ), :].astype(BF16)
            rr = r_ref[:, pl.ds(off, TQ)]
            keep = q_minus_k >= (kb - qi) * TQ
            out = []
            for x in range(2):
                sel = lo if x == 0 else jnp.logical_not(lo)
                kx, qx = kk[:, x * LANE:(x + 1) * LANE], qq[:, x * LANE:(x + 1) * LANE]
                st = jnp.where(keep, _dot(kx, qx, ((1,), (1,))) * ATT_SCALE_LOG2, NEG)
                pt = jnp.exp2(st - rr[x:x + 1, :])
                dpt = _dot(jnp.where(sel, vv, 0), dd, ((1,), (1,)))
                dst = (pt * (dpt - rr[2 + x:3 + x, :]) * ATT_SCALE).astype(BF16)
                out.append(carry[x] + _dot(dst, qx, ((1,), (0,))))
                out.append(_dot(pt, jnp.where(sel, dd, 0), ((1,), (0,))))
                dqt_ref[x * LANE:(x + 1) * LANE, pl.ds(off, TQ)] += _dot(kt_ref[x * LANE:(x + 1) * LANE, :], dst, ((1,), (0,)))
            return out[0], out[2], carry[2] + out[1] + out[3]

        z = jnp.zeros((TK, LANE), F32)
        dka, dkb, dv = lax.fori_loop(kb, nq, step, (z, z, z))
        dk_ref[:, 0:LANE] = dka
        dk_ref[:, LANE:2 * LANE] = dkb
        dv_ref[...] = dv.astype(BF16)

    return pl.pallas_call(
        body, name="attn_bwd", grid=(NPAIR, S // TK),
        in_specs=[pl.BlockSpec((S, 2 * LANE), lambda j, k: (0, j)), pl.BlockSpec((TK, 2 * LANE), lambda j, k: (k, j)),
                  pl.BlockSpec((2 * LANE, TK), lambda j, k: (j, k)), pl.BlockSpec((TK, LANE), lambda j, k: (k, j)),
                  pl.BlockSpec((S, LANE), lambda j, k: (0, j)), pl.BlockSpec((None, 8, S), lambda j, k: (j, 0, 0))],
        out_specs=[pl.BlockSpec((2 * LANE, S), lambda j, k: (j, 0)), pl.BlockSpec((TK, 2 * LANE), lambda j, k: (k, j)),
                   pl.BlockSpec((TK, LANE), lambda j, k: (k, j))],
        out_shape=[jax.ShapeDtypeStruct((H * LANE, S), F32), jax.ShapeDtypeStruct((S, H * LANE), F32),
                   jax.ShapeDtypeStruct((S, H * VDIM), BF16)],
        compiler_params=pltpu.CompilerParams(dimension_semantics=("parallel", "arbitrary")),
    )(qc, kc, kct, v, do, rows)


_IN_Z, _IN_XBC, _IN_DT, _IN_Q, _IN_KV, _IN_KR = 0, 1024, 2560, 2576, 2960, 3216


PROJ_COLS = 512
SMALL_PAD = pl.cdiv(SMALL_W, PROJ_COLS) * PROJ_COLS


def _prep_in(w_in_t):
    dt = w_in_t.dtype
    return jnp.concatenate(
        [w_in_t[_IN_Q:_IN_KV], w_in_t[_IN_KV:_IN_KR], w_in_t[_IN_KR:IN_WIDTH], jnp.zeros((LANE - ROPE, D), dt),
         w_in_t[_IN_DT:_IN_Q], jnp.zeros((SMALL_PAD - SM_DT - H, D), dt)], axis=0)


def _proj_in(xb, w_in_t, w_small):
    nz, nx, ns = (_IN_XBC - _IN_Z) // PROJ_COLS, (_IN_DT - _IN_XBC) // PROJ_COLS, SMALL_PAD // PROJ_COLS

    dt_block, dt_at = divmod(SM_DT, PROJ_COLS)

    def body(x_ref, w_ref, ws_ref, z_ref, xbc_ref, sm_ref, dtt_ref):
        i = pl.program_id(0)

        def emit(w, o_ref):
            o_ref[...] = lax.dot_general(x_ref[...], w[...], (((1,), (1,)), ((), ())), preferred_element_type=F32)

        pl.when(i < nz)(lambda: emit(w_ref, z_ref))
        pl.when((i >= nz) & (i < nz + nx))(lambda: emit(w_ref, xbc_ref))
        pl.when(i >= nz + nx)(lambda: emit(ws_ref, sm_ref))

        @pl.when(i == nz + nx + dt_block)
        def _():
            dtt_ref[...] = sm_ref[:, dt_at:dt_at + LANE].T

    def blocks(first, count, rows):
        at = lambda i: jnp.clip(i - first, 0, count - 1)
        return pl.BlockSpec((PROJ_COLS, D), lambda i: (at(i), 0)) if rows else pl.BlockSpec((S, PROJ_COLS), lambda i: (0, at(i)))

    return pl.pallas_call(
        body, name="proj_in", grid=(nz + nx + ns,),
        in_specs=[pl.BlockSpec((S, D), lambda i: (0, 0)), blocks(0, nz + nx, True), blocks(nz + nx, ns, True)],
        out_specs=[blocks(0, nz, False), blocks(nz, nx, False), blocks(nz + nx, ns, False), pl.BlockSpec((LANE, S), lambda i: (0, 0))],
        out_shape=[jax.ShapeDtypeStruct((S, _IN_XBC - _IN_Z), F32), jax.ShapeDtypeStruct((S, _IN_DT - _IN_XBC), F32),
                   jax.ShapeDtypeStruct((S, SMALL_W), F32), jax.ShapeDtypeStruct((LANE, S), F32)],
    )(xb, w_in_t, w_small)


PART_COLS = 512


def _part_blocks(widths):
    first = [0]
    for w in widths:
        first.append(first[-1] + w // PART_COLS)

    def at(part):
        return lambda i: jnp.clip(i - first[part], 0, first[part + 1] - first[part] - 1)

    return first, at


def _mm_ta_stacked(parts, b, rows, name):
    n = b.shape[1]
    first, at = _part_blocks([a.shape[1] for a in parts])
    assert first[-1] == pl.cdiv(rows, PART_COLS)

    def body(*refs):
        b_ref, o_ref = refs[-2:]
        i = pl.program_id(0)
        for part, a_ref in enumerate(refs[:-2]):
            @pl.when((i >= first[part]) & (i < first[part + 1]))
            def _(a_ref=a_ref):
                o_ref[...] = lax.dot_general(a_ref[...], b_ref[...], (((0,), (0,)), ((), ())),
                                             preferred_element_type=F32).astype(BF16)

    return pl.pallas_call(
        body, name=name, grid=(first[-1],),
        in_specs=[pl.BlockSpec((S, PART_COLS), lambda i, at=at(part): (0, at(i))) for part in range(len(parts))]
        + [pl.BlockSpec((S, n), lambda i: (0, 0))],
        out_specs=pl.BlockSpec((PART_COLS, n), lambda i: (i, 0)), out_shape=jax.ShapeDtypeStruct((rows, n), BF16),
    )(*parts, b)


def _prep_attn(w_qb, w_kvb):
    w_q = jnp.pad(w_qb.reshape(Q_RANK, H, NOPE + ROPE), ((0, 0), (0, 0), (0, LANE - NOPE - ROPE))).reshape(Q_RANK, H * LANE)
    kv3 = w_kvb.reshape(KV_RANK, H, NOPE + VDIM)
    w_k = jnp.pad(kv3[:, :, :NOPE], ((0, 0), (0, 0), (0, LANE - NOPE))).reshape(KV_RANK, H * LANE)
    w_v = kv3[:, :, NOPE:].reshape(KV_RANK, H * VDIM)
    return w_q, w_k, w_v


def _rope_tables(positions):
    inv_freq = 1.0 / (10000.0 ** (jnp.arange(0, ROPE, 2, dtype=F32) / ROPE))
    ang = positions.astype(F32).reshape(S, 1) * inv_freq
    cos, sin = jnp.cos(ang), jnp.sin(ang)
    cos_t = jnp.concatenate([jnp.ones((S, NOPE), F32), cos, cos, jnp.ones((S, LANE - NOPE - ROPE), F32)], axis=1)
    sin_t = jnp.concatenate([jnp.zeros((S, NOPE), F32), -sin, sin, jnp.zeros((S, LANE - NOPE - ROPE), F32)], axis=1)
    return cos_t, sin_t


def _local_step(x, p, positions, target, w_in, fetch, send, sp, started):
    w_in_t = w_in.reshape(IN_WIDTH, D)
    w_small = _prep_in(w_in_t)
    cos_t, sin_t = _rope_tables(positions)
    prow = jnp.zeros((8, LANE), F32).at[0, :H].set(sp["dt_bias"][0]).at[1, :H].set(sp["A_log"][0]).at[2, :H].set(sp["D"][0])
    pcol = prow.T

    xb, pb = (x + started).astype(BF16), (p + started).astype(BF16)
    z, xbc, small, dt_t = _proj_in(xb, w_in_t, w_small)
    act = _conv_fwd(xbc, sp["conv_w"], sp["conv_b"])
    y, states = _ssd_fwd(act, small, dt_t, prow, pcol)
    y_ssd = _gate_norm_fwd(y, z, sp["ssd_norm"])
    gl = fetch("attn", y_ssd)
    w_q, w_k, w_v = _prep_attn(_from_cols(gl["w_qb"]), _from_cols(gl["w_kvb"]))
    qn, kvn, qcat, kcat, kcat_t, v = _qkv_fwd(small, w_q, w_k, w_v, sp["q_norm"], sp["kv_norm"], cos_t, sin_t)
    o, lse = _attn_fwd(qcat, kcat, v)
    y_mla = _rms_fwd(o, sp["out_norm"], name="out_norm_fwd")
    w_out = fetch("out", y_mla)["w_out"]
    w_out = w_out.reshape(2 * SSD_INNER, D)
    mix, h1, h1b = _out_proj_ln(y_ssd, y_mla, w_out, x, sp["ln_mix_g"], sp["ln_mix_b"])
    gl = fetch("ffn", h1b)
    w_pg, w_pp = gl["w_pg"].reshape(D, D), _from_cols(gl["w_pp"])
    w_gate, w_up, w_down = gl["w_gate"], gl["w_up"], gl["w_down"]
    gate, up, actf = _ffn_hidden_fwd(h1b, w_gate, w_up)
    ffn = _mm([(actf, w_down)], chunk="sum", name="ffn_down")
    dpre2, dpre2b, dpg, dpp, dg2, db2, loss_row = _final_fwd_bwd(h1, ffn, h1b, pb, w_pg, w_pp, target, sp["ln_ffn_g"], sp["ln_ffn_b"])

    g = {"ln_ffn_g": dg2, "ln_ffn_b": db2}
    g["w_pp"] = _to_cols(_mm([(pb, dpp)], ta=True, out_dtype=BF16, name="d_w_ple_proj"))
    g["w_pg"] = _mm([(h1b, dpg)], ta=True, out_dtype=BF16, name="d_w_ple_gate").reshape(NCHIP, D // NCHIP, D)
    g["w_down"] = _mm([(actf, dpre2b)], ta=True, chunk="out", out_dtype=BF16, name="d_w_down")
    dgate, dup = _ffn_hidden_bwd(dpre2b, w_down, gate, up)
    g["w_gate"] = _mm([(dgate, h1b)], ta=True, chunk="out", out_dtype=BF16, name="d_w_gate")
    g["w_up"] = _mm([(dup, h1b)], ta=True, chunk="out", out_dtype=BF16, name="d_w_up")
    sent = send("ffn", {name: g.pop(name) for name in dict(ASYNC_GROUPS)["ffn"]})
    dh1 = _mm([(dgate, w_gate), (dup, w_up), (dpg, w_pg.T)], chunk="sum", add=dpre2, add_scale=ALPHA, name="d_h1")
    dpre1, dpre1b, g["ln_mix_g"], g["ln_mix_b"], dy_ssd, dy_mla = _ln_bwd(x, mix, sp["ln_mix_g"] + sent, dh1, w_out)
    dw_out = _mm_ta_stacked((y_ssd, y_mla), dpre1b, 2 * SSD_INNER, "d_w_out")
    sent = send("out", {"w_out": dw_out.reshape(NCHIP, 2 * SSD_INNER // NCHIP, D)})
    do, g["out_norm"] = _rms_bwd(o, sp["out_norm"] + sent, dy_mla, name="out_norm_bwd")
    dqt, dk, dv = _attn_bwd(qcat, kcat, kcat_t, v, do, _attn_rows(lse, o, do))
    dlatent, dqlin, dkb, g["q_norm"], g["kv_norm"] = _qkv_bwd(dqt, dk, dv, small, w_q, w_k, w_v, sp["q_norm"], sp["kv_norm"], cos_t, sin_t)
    dw_q = _mm([(qn, dqlin)], ta=True, out_dtype=BF16, name="d_w_q")
    dw_k = _mm([(kvn, dkb)], ta=True, out_dtype=BF16, name="d_w_k")
    dw_v = _mm([(kvn, dv)], ta=True, out_dtype=BF16, name="d_w_v")
    dw_qb = _to_cols(dw_q.reshape(Q_RANK, H, LANE)[:, :, :NOPE + ROPE].reshape(Q_RANK, H * (NOPE + ROPE)))
    dw_kvb = _to_cols(jnp.concatenate([dw_k.reshape(KV_RANK, H, LANE)[:, :, :NOPE], dw_v.reshape(KV_RANK, H, VDIM)],
                                       axis=2).reshape(KV_RANK, H * (NOPE + VDIM)))
    sent = send("attn", {"w_qb": dw_qb, "w_kvb": dw_kvb})
    dy, dz, g["ssd_norm"] = _gate_norm_bwd(y, z, sp["ssd_norm"] + sent, dy_ssd)
    dact, ddt, dprow = _ssd_bwd(act, small, dt_t, prow, pcol, states, dy)
    g["dt_bias"], g["A_log"], g["D"] = dprow[0:1, :H], dprow[1:2, :H], dprow[2:3, :H]
    dxbc, g["conv_w"], g["conv_b"] = _conv_bwd(xbc, sp["conv_w"], sp["conv_b"], dact)
    dsmall = jnp.concatenate([dlatent, ddt.astype(BF16)], axis=1)
    in_blocks = [(d, w_in_t, (k, first // PROJ_COLS + k, PROJ_COLS))
                 for d, first in ((dz, _IN_Z), (dxbc, _IN_XBC)) for k in range(d.shape[1] // PROJ_COLS)]
    grad_x = _mm(in_blocks + [(dsmall, w_small, (0, 0, SMALL_W))], add=dpre1, add_scale=ALPHA, name="d_x")
    sent = send("small", dict(g, loss=loss_row))
    n_small = IN_WIDTH - _IN_DT
    dsm = jnp.concatenate([(ddt[:, :H] + sent).astype(BF16), dlatent[:, :n_small - H], jnp.zeros((S, D - n_small), BF16)], axis=1)
    dw_in = _mm_ta_stacked((dz, dxbc, dsm), xb, IN_WIDTH, "d_w_in").reshape(NCHIP, IN_WIDTH // NCHIP * D // LANE, LANE)
    return loss_row, grad_x, dw_in, g


MESH = pl.DeviceIdType.MESH
BIG = (("w_in", (D, IN_WIDTH), 1), ("w_qb", (Q_RANK, H * (NOPE + ROPE)), 1), ("w_kvb", (KV_RANK, H * (NOPE + VDIM)), 1),
       ("w_out", (2 * SSD_INNER, D), 0), ("w_gate", (D, D_FF), 1), ("w_up", (D, D_FF), 1), ("w_down", (D_FF, D), 0),
       ("w_pg", (D, D), 0), ("w_pp", (PLE, D), 1))
CONV_SHARD = SSD_XBC // NCHIP
BF16_ROWS = 16


def _from_cols(stack):
    return jnp.concatenate([stack[k] for k in range(NCHIP)], axis=1)


def _to_cols(full):
    r, c4 = full.shape
    return full.reshape(r, NCHIP, c4 // NCHIP).transpose(1, 0, 2)


def _coords():
    return lax.axis_index("x"), lax.axis_index("y"), lax.axis_index("c")


def _peers():
    x, y, c = _coords()
    return 2 * x + y, c, [(1 - x, y), (x, 1 - y), (1 - x, 1 - y)], (x, y, 1 - c)


def _half_axis(shape):
    return 0 if shape[-2] % (2 * BF16_ROWS) == 0 else 1


def _half_shape(shape):
    r, c = shape[-2:]
    return (r // 2, c) if _half_axis(shape) == 0 else (r, c // 2)


def _half(core, shape):
    r, c = shape[-2:]
    if _half_axis(shape) == 0:
        return pl.ds(pl.multiple_of(core * (r // 2), BF16_ROWS), r // 2), slice(None)
    return slice(None), pl.ds(pl.multiple_of(core * (c // 2), LANE), c // 2)


ASYNC_GROUPS = (("attn", ("w_qb", "w_kvb")), ("out", ("w_out",)), ("ffn", ("w_gate", "w_up", "w_down", "w_pg", "w_pp")))
TRANSPOSED = ("w_in", "w_gate", "w_up")
ROW_MAJOR = ("w_in",)
HBM_SPEC = pl.BlockSpec(memory_space=pltpu.HBM)
SEM_SPEC = pl.BlockSpec(memory_space=pltpu.SEMAPHORE)
IN_FLIGHT = pltpu.SideEffectType.DATAFLOW_SIDE_EFFECTING


def _in_hbm(a):
    return pltpu.with_memory_space_constraint(a, pltpu.HBM)


def _hbm_like(arrs, lead=()):
    return [pltpu.HBM(lead + a.shape, a.dtype) for a in arrs]


def _split_start(name, srcs, lands, after, n_sem, start):
    n = len(srcs)
    order = [] if after is None else [after]

    def body(*refs):
        src_refs, land_refs = refs[:n], refs[n:2 * n]
        send_sems, recv_sems = refs[2 * n + len(order)], refs[2 * n + len(order) + 1]
        token = refs[-1]

        def copy(send_idx, recv_idx, src, dst, to):
            return pltpu.make_async_remote_copy(src_ref=src, dst_ref=dst, send_sem=send_sems.at[send_idx],
                                                recv_sem=recv_sems.at[recv_idx], device_id=to, device_id_type=MESH)

        for cp in start(src_refs, land_refs, copy):
            cp.start()
        token[...] = jnp.zeros_like(token)

    sem = pltpu.SemaphoreType.DMA((n_sem,))
    outs = pl.pallas_call(
        body, name=name, in_specs=[HBM_SPEC] * (2 * n) + [pl.BlockSpec(memory_space=pl.ANY)] * len(order),
        out_specs=[SEM_SPEC, SEM_SPEC] + [HBM_SPEC] * (2 * n) + [pl.BlockSpec(memory_space=pltpu.VMEM)],
        out_shape=[sem, sem] + _hbm_like(srcs) + _hbm_like(lands) + [jax.ShapeDtypeStruct((8, LANE), F32)],
        input_output_aliases={i: 2 + i for i in range(2 * n)},
        compiler_params=pltpu.CompilerParams(has_side_effects=IN_FLIGHT),
    )(*[_in_hbm(a) for a in srcs], *[_in_hbm(a) for a in lands], *order)
    return (outs[0], outs[1], outs[2:2 + n], outs[2 + n:2 + 2 * n]), outs[-1]


def _split_wait(name, send_sems, recv_sems, srcs, lands, after, waits):
    n = len(srcs)

    def body(*refs):
        src_refs, land_refs = refs[:n], refs[n:2 * n]
        send_ref, recv_ref = refs[2 * n], refs[2 * n + 1]

        def copy(send_idx, recv_idx, src, dst, to):
            return pltpu.make_async_remote_copy(src_ref=src, dst_ref=dst, send_sem=send_ref.at[send_idx],
                                                recv_sem=recv_ref.at[recv_idx], device_id=to, device_id_type=MESH)

        for cp in waits(src_refs, land_refs, copy):
            cp.wait_send()
            cp.wait_recv()

    outs = pl.pallas_call(
        body, name=name, in_specs=[HBM_SPEC] * (2 * n) + [SEM_SPEC, SEM_SPEC, pl.BlockSpec(memory_space=pl.ANY)],
        out_specs=[HBM_SPEC] * (2 * n), out_shape=_hbm_like(srcs) + _hbm_like(lands),
        input_output_aliases={i: i for i in range(2 * n)},
        compiler_params=pltpu.CompilerParams(has_side_effects=IN_FLIGHT),
    )(*srcs, *lands, send_sems, recv_sems, after)
    return outs[:n], outs[n:]


GATHER_LATE_SEMS = 2 * (NCHIP - 1)


def _gather_async_start(tag, shards, after):
    def start(srcs, lands, copy):
        k, c, chips, _ = _peers()
        out = []
        for a, (src, dst) in enumerate(zip(srcs, lands)):
            for j, (cx, cy) in enumerate(chips):
                for core in range(2):
                    out.append(copy(GATHER_LATE_SEMS * a + 2 * j + core, GATHER_LATE_SEMS * a + 2 * j + c,
                                    src.at[*_half(c, src.shape)], dst.at[k, *_half(c, src.shape)], (cx, cy, core)))
        return out

    chip = 2 * lax.axis_index("x") + lax.axis_index("y")
    lands = [lax.dynamic_update_slice(lax.empty((NCHIP,) + s.shape, s.dtype), s[None], (chip, 0, 0)) for s in shards]
    return _split_start("gather_%s_start" % tag, shards, lands, after, GATHER_LATE_SEMS * len(shards), start)


def _gather_async_wait(tag, send_sems, recv_sems, shards, lands, after, first=0):
    def waits(srcs, lands_, copy):
        _, c, chips, _ = _peers()
        out = []
        for a, (src, dst) in enumerate(zip(srcs, lands_)):
            for j, (cx, cy) in enumerate(chips):
                for core in range(2):
                    idx = GATHER_LATE_SEMS * (first + a) + 2 * j + core
                    out.append(copy(idx, idx, src.at[*_half(c, src.shape)], dst.at[2 * cx + cy, *_half(core, src.shape)], (cx, cy, core)))
        return out

    return _split_wait("gather_%s_wait" % tag, send_sems, recv_sems, shards, lands, after, waits)[1]


EARLY_SEMS = NCHIP - 1


def _gather_early_start(shards):
    def start(srcs, lands, copy):
        k, c, chips, _ = _peers()
        return [copy(EARLY_SEMS * a + j, EARLY_SEMS * a + j, src.at[*_half(c, src.shape)], dst.at[k, *_half(c, src.shape)], (cx, cy, c))
                for a, (src, dst) in enumerate(zip(srcs, lands)) for j, (cx, cy) in enumerate(chips)]

    chip = 2 * lax.axis_index("x") + lax.axis_index("y")
    lands = [lax.dynamic_update_slice(lax.empty((NCHIP,) + s.shape, s.dtype), s[None], (chip, 0, 0)) for s in shards]
    return _split_start("gather_in_start", shards, lands, None, EARLY_SEMS * len(shards), start)


def _gather_early_wait(send_sems, recv_sems, shards, lands, after):
    def waits(srcs, lands_, copy):
        _, c, chips, _ = _peers()
        return [copy(EARLY_SEMS * a + j, EARLY_SEMS * a + j, src.at[*_half(c, src.shape)], dst.at[2 * cx + cy, *_half(c, src.shape)], (cx, cy, c))
                for a, (src, dst) in enumerate(zip(srcs, lands_)) for j, (cx, cy) in enumerate(chips)]

    n_arr = len(shards)

    def forward(*refs):
        stacks, (fwd_send, fwd_recv) = refs[n_arr:2 * n_arr], refs[2 * n_arr:]
        _, c, chips, sibling = _peers()

        def pass_on(a, j, core):
            cx, cy = chips[j]
            part = stacks[a].at[2 * cx + cy, *_half(core, shards[a].shape)]
            return pltpu.make_async_remote_copy(src_ref=part, dst_ref=part, send_sem=fwd_send.at[EARLY_SEMS * a + j],
                                                recv_sem=fwd_recv.at[EARLY_SEMS * a + j], device_id=sibling, device_id_type=MESH)

        pairs = [(a, j) for a in range(n_arr) for j in range(len(chips))]
        sent = [pass_on(a, j, c) for a, j in pairs]
        for cp in sent:
            cp.start()
        for a, j in pairs:
            pass_on(a, j, 1 - c).wait_recv()
        for cp in sent:
            cp.wait_send()

    arrived = _split_wait("gather_in_wait", send_sems, recv_sems, shards, lands, after, waits)[1]
    any_spec = pl.BlockSpec(memory_space=pl.ANY)
    return pl.pallas_call(
        forward, name="gather_in_forward", in_specs=[any_spec] * n_arr, out_specs=[any_spec] * n_arr,
        out_shape=[jax.ShapeDtypeStruct(a.shape, a.dtype) for a in arrived], input_output_aliases={i: i for i in range(n_arr)},
        scratch_shapes=[pltpu.SemaphoreType.DMA((EARLY_SEMS * n_arr,))] * 2,
    )(*arrived)


def _other_devices():
    x, y, c = _coords()
    out = []
    for d in range(1, NDEV):
        tx, ty, tc = x ^ (d >> 2), y ^ ((d >> 1) & 1), c ^ (d & 1)
        out.append((d, (tx, ty, tc), 2 * tx + ty, 4 * tx + 2 * ty + tc))
    return out


def _reduce_async_start(tag, stacks, after):
    def start(srcs, lands, copy):
        x, y, c = _coords()
        me = 4 * x + 2 * y + c
        return [copy((NDEV - 1) * a + d - 1, (NDEV - 1) * a + d - 1, src.at[chip, *_half(to[2], src.shape)], dst.at[me], to)
                for a, (src, dst) in enumerate(zip(srcs, lands)) for d, to, chip, _ in _other_devices()]

    x, y, c = _coords()
    lands = []
    for s in stacks:
        hr, hc = _half_shape(s.shape)
        at = (c * hr, 0) if _half_axis(s.shape) == 0 else (0, c * hc)
        own = lax.dynamic_slice(s, (2 * x + y,) + at, (1, hr, hc))
        lands.append(lax.dynamic_update_slice(lax.empty((NDEV, hr, hc), s.dtype), own, (4 * x + 2 * y + c, 0, 0)))
    return _split_start("reduce_%s_start" % tag, stacks, lands, after, (NDEV - 1) * len(stacks), start)


def _reduce_async_wait(tag, send_sems, recv_sems, stacks, lands, after):
    def waits(srcs, lands_, copy):
        return [copy((NDEV - 1) * a + d - 1, (NDEV - 1) * a + d - 1, src.at[chip, *_half(to[2], src.shape)], dst.at[pos], to)
                for a, (src, dst) in enumerate(zip(srcs, lands_)) for d, to, chip, pos in _other_devices()]

    return _split_wait("reduce_%s_wait" % tag, send_sems, recv_sems, stacks, lands, after, waits)[1]


def _reduce_finish(tag, arrived, dims):
    n_arr = len(arrived)

    def body(*refs):
        lands, fin = refs[:n_arr], refs[n_arr:2 * n_arr]
        send_sems, recv_sems = refs[2 * n_arr:]
        _, c, _, sibling = _peers()
        sends = []
        for a in range(n_arr):
            mine = fin[a].at[*_half(c, dims[a])]

            def device_sum(vs, vf, a=a, mine=mine):
                pltpu.sync_copy(lands[a], vs)
                acc = vs[0].astype(F32)
                for i in range(1, NDEV):
                    acc = acc + vs[i].astype(F32)
                vf[...] = acc
                pltpu.sync_copy(vf, mine)

            pl.run_scoped(device_sum, pltpu.VMEM((NDEV,) + _half_shape(dims[a]), BF16), pltpu.VMEM(_half_shape(dims[a]), F32))
            sends.append(pltpu.make_async_remote_copy(src_ref=mine, dst_ref=mine, send_sem=send_sems.at[a], recv_sem=recv_sems.at[a],
                                                      device_id=sibling, device_id_type=MESH))
            sends[-1].start()
        for a in range(n_arr):
            other = fin[a].at[*_half(1 - c, dims[a])]
            pltpu.make_async_remote_copy(src_ref=other, dst_ref=other, send_sem=send_sems.at[a], recv_sem=recv_sems.at[a],
                                         device_id=sibling, device_id_type=MESH).wait_recv()
        for cp in sends:
            cp.wait_send()

    any_spec = pl.BlockSpec(memory_space=pl.ANY)
    return pl.pallas_call(
        body, name="reduce_%s_finish" % tag, in_specs=[any_spec] * n_arr, out_specs=[any_spec] * n_arr,
        out_shape=[jax.ShapeDtypeStruct(d, F32) for d in dims],
        scratch_shapes=[pltpu.SemaphoreType.DMA((n_arr,)), pltpu.SemaphoreType.DMA((n_arr,))],
    )(*arrived)


SMALL = (("conv_w", SSD_K * SSD_XBC), ("conv_b", SSD_XBC), ("dt_bias", H), ("A_log", H), ("D", H), ("ssd_norm", SSD_INNER),
         ("q_norm", Q_RANK), ("kv_norm", KV_RANK), ("out_norm", SSD_INNER), ("ln_mix_g", D), ("ln_mix_b", D),
         ("ln_ffn_g", D), ("ln_ffn_b", D))
SMALL_ROWS = 120
NDEV = 8


def _allreduce_small_start(sv):
    def start(srcs, lands, copy):
        x, y, c = _coords()
        return [copy(d - 1, d - 1, srcs[0], lands[0].at[4 * x + 2 * y + c], to) for d, to, _, _ in _other_devices()]

    x, y, c = _coords()
    slots = lax.dynamic_update_slice(lax.empty((NDEV,) + sv.shape, sv.dtype), sv[None], (4 * x + 2 * y + c, 0, 0))
    return _split_start("allreduce_small_start", [sv], [slots], None, NDEV - 1, start)


def _allreduce_small_wait(send_sems, recv_sems, srcs, lands, after):
    def waits(srcs_, lands_, copy):
        return [copy(d - 1, d - 1, srcs_[0], lands_[0].at[pos], to) for d, to, _, pos in _other_devices()]

    def device_sum(slots_ref, out_ref):
        acc = slots_ref[0]
        for i in range(1, NDEV):
            acc = acc + slots_ref[i]
        out_ref[...] = acc

    slots = _split_wait("allreduce_small_wait", send_sems, recv_sems, srcs, lands, after, waits)[1][0]
    vm = pl.BlockSpec(memory_space=pltpu.VMEM)
    return pl.pallas_call(device_sum, name="allreduce_small_sum", in_specs=[vm], out_specs=vm,
                          out_shape=jax.ShapeDtypeStruct(slots.shape[1:], slots.dtype))(slots)


def _adamw_math(w, g, m, v):
    m2 = ADAM_B1 * m + (1.0 - ADAM_B1) * g
    v2 = ADAM_B2 * v + (1.0 - ADAM_B2) * (g * g)
    m_hat = m2 / (1.0 - ADAM_B1 ** ADAM_STEP)
    v_hat = v2 / (1.0 - ADAM_B2 ** ADAM_STEP)
    return -ADAM_LR * (m_hat / (jnp.sqrt(v_hat) + ADAM_EPS) + ADAM_WD * w), m2, v2


ADAM_BLOCK_BYTES = 2 * 1024 * 1024


def _adamw_big(w, g, m, v, *, name):
    r, c = w.shape

    def body(w_ref, g_ref, m_ref, v_ref, d_ref, m2_ref, v2_ref):
        d_ref[...], m2_ref[...], v2_ref[...] = _adamw_math(w_ref[...], g_ref[...], m_ref[...], v_ref[...])

    tr = max(t for t in range(8, r + 1, 8) if r % t == 0 and t * c * 4 <= ADAM_BLOCK_BYTES)
    steps, spec = r // tr, pl.BlockSpec((tr, c), lambda i: (i, 0))
    return pl.pallas_call(body, name=name, grid=(steps,), in_specs=[spec] * 4, out_specs=[spec] * 3,
                          out_shape=[jax.ShapeDtypeStruct((r, c), F32)] * 3)(w, g, m, v)


def _adamw_small(ws, gs, ms, vs):
    n = len(ws)

    def body(*refs):
        for i in range(n):
            w_ref, g_ref, m_ref, v_ref = (refs[j * n + i] for j in range(4))
            d_ref, m2_ref, v2_ref = (refs[(4 + j) * n + i] for j in range(3))
            d_ref[...], m2_ref[...], v2_ref[...] = _adamw_math(w_ref[...], g_ref[...], m_ref[...], v_ref[...])

    vm = pl.BlockSpec(memory_space=pltpu.VMEM)
    shapes = [jax.ShapeDtypeStruct(w.shape, F32) for w in ws]
    outs = pl.pallas_call(body, name="adamw_small", in_specs=[vm] * (4 * n), out_specs=[vm] * (3 * n), out_shape=shapes * 3)(
        *ws, *gs, *ms, *vs)
    return outs[:n], outs[n:2 * n], outs[2 * n:]


_SMALL_ARG = {"conv_w": "ssd_conv_w", "conv_b": "ssd_conv_b", "dt_bias": "ssd_dt_bias", "A_log": "ssd_A_log", "D": "ssd_D",
              "ssd_norm": "ssd_norm_w", "q_norm": "mla_q_norm_w", "kv_norm": "mla_kv_norm_w", "out_norm": "mla_out_norm_w",
              "ln_mix_g": "ln_mix_g", "ln_mix_b": "ln_mix_b", "ln_ffn_g": "ln_ffn_g", "ln_ffn_b": "ln_ffn_b"}
_BIG_ARG = {"w_in": "w_in", "w_qb": "mla_w_q_b", "w_kvb": "mla_w_kv_b", "w_out": "w_out", "w_gate": "w_ffn_gate",
            "w_up": "w_ffn_up", "w_down": "w_ffn_down", "w_pg": "w_ple_gate", "w_pp": "w_ple_proj"}
_WEIGHT_ORDER = ("w_in", "ssd_conv_w", "ssd_conv_b", "ssd_dt_bias", "ssd_A_log", "ssd_D", "ssd_norm_w", "mla_q_norm_w", "mla_w_q_b",
                 "mla_kv_norm_w", "mla_w_kv_b", "mla_out_norm_w", "w_out", "ln_mix_g", "ln_mix_b", "w_ffn_gate", "w_ffn_up",
                 "w_ffn_down", "w_ple_gate", "w_ple_proj", "ln_ffn_g", "ln_ffn_b")


def _rows128(a):
    flat = a.reshape(-1)
    return jnp.pad(flat, (0, -flat.shape[0] % LANE)).reshape(-1, LANE)


def kernel(x, p, positions, w_in, ssd_conv_w, ssd_conv_b, ssd_dt_bias, ssd_A_log, ssd_D, ssd_norm_w, mla_q_norm_w, mla_w_q_b, mla_kv_norm_w, mla_w_kv_b, mla_out_norm_w, w_out, ln_mix_g, ln_mix_b, w_ffn_gate, w_ffn_up, w_ffn_down, w_ple_gate, w_ple_proj, ln_ffn_g, ln_ffn_b, loss_target, m_w_in, m_ssd_conv_w, m_ssd_conv_b, m_ssd_dt_bias, m_ssd_A_log, m_ssd_D, m_ssd_norm_w, m_mla_q_norm_w, m_mla_w_q_b, m_mla_kv_norm_w, m_mla_w_kv_b, m_mla_out_norm_w, m_w_out, m_ln_mix_g, m_ln_mix_b, m_w_ffn_gate, m_w_ffn_up, m_w_ffn_down, m_w_ple_gate, m_w_ple_proj, m_ln_ffn_g, m_ln_ffn_b, v_w_in, v_ssd_conv_w, v_ssd_conv_b, v_ssd_dt_bias, v_ssd_A_log, v_ssd_D, v_ssd_norm_w, v_mla_q_norm_w, v_mla_w_q_b, v_mla_kv_norm_w, v_mla_w_kv_b, v_mla_out_norm_w, v_w_out, v_ln_mix_g, v_ln_mix_b, v_w_ffn_gate, v_w_ffn_up, v_w_ffn_down, v_w_ple_gate, v_w_ple_proj, v_ln_ffn_g, v_ln_ffn_b):
    given = dict(locals())
    chip = 2 * lax.axis_index("x") + lax.axis_index("y")

    def local(name, prefix=""):
        a = given[prefix + _BIG_ARG[name]][0]
        return a.T if name in TRANSPOSED else a

    def updated(name, prefix=""):
        if name in ROW_MAJOR:
            _, c, r = given[prefix + _BIG_ARG[name]].shape
            return given[prefix + _BIG_ARG[name]].reshape(c // LANE, LANE, r).transpose(2, 0, 1).reshape(-1, LANE)
        return local(name, prefix)

    def global_layout(name, arr):
        if name in ROW_MAJOR:
            r, c = local(name).shape
            return arr.reshape(r, c // LANE, LANE).transpose(1, 2, 0).reshape(1, c, r)
        return (arr.T if name in TRANSPOSED else arr)[None]

    conv_bits = lax.bitcast_convert_type(ssd_conv_w[0], BF16).reshape(SSD_K, 2 * CONV_SHARD)
    early, flying = _gather_early_start([local("w_in").astype(BF16), jnp.pad(conv_bits, ((0, BF16_ROWS - SSD_K), (0, 0)))])
    late = [name for _, names in ASYNC_GROUPS for name in names]
    late_casts = [(local(name) + flying[0, 0]).astype(BF16) for name in late]
    w_in_all, conv_all = _gather_early_wait(*early, sum(c[:8, :LANE].astype(F32) for c in late_casts))
    sp = {k: given[a] for k, a in _SMALL_ARG.items() if k != "conv_w"}
    sp["conv_w"] = _from_cols(lax.bitcast_convert_type(conv_all[:, :SSD_K].reshape(NCHIP, SSD_K, CONV_SHARD, 2), F32))
    (late_send, late_recv, late_shards, late_lands), tie = _gather_async_start("late", late_casts, w_in_all)

    def fetch(group, after):
        names = dict(ASYNC_GROUPS)[group]
        first = late.index(names[0])
        mine = slice(first, first + len(names))
        return dict(zip(names, _gather_async_wait(group, late_send, late_recv, late_shards[mine], late_lands[mine], after, first)))

    reducing = {}

    def send(group, grads):
        if group == "small":
            rows = jnp.concatenate([_rows128(grads[name]) for name, _ in SMALL] + [grads["loss"]], axis=0)
            reducing[group], sent = _allreduce_small_start(jnp.pad(rows, ((0, SMALL_ROWS - rows.shape[0]), (0, 0))))
        else:
            reducing[group], sent = _reduce_async_start(group, [grads[name] for name in dict(ASYNC_GROUPS)[group]], None)
        return sent[0, 0]

    sp["conv_b"] = sp["conv_b"] + tie[0, 0]
    loss_row, grad_x, dw_in, g = _local_step(x[0], p[0, 0], positions[0], loss_target[0], w_in_all, fetch, send, sp, flying[0, 0])

    reducing["in"], tie = _reduce_async_start("in", [dw_in], grad_x)
    gbig = {}
    for group, names in reversed(ASYNC_GROUPS):
        arrived = _reduce_async_wait(group, *reducing[group], tie)
        gbig.update(zip(names, _reduce_finish(group, arrived, [local(name).shape for name in names])))
    small_sum = _allreduce_small_wait(*reducing.pop("small"), tie)
    gsmall, row = {}, 0
    for name, size in SMALL:
        nrow = -(-size // LANE)
        gsmall[name] = small_sum[row:row + nrow].reshape(-1)[:size]
        row += nrow
    loss = small_sum[row, 0]

    grads = {_BIG_ARG[name]: global_layout(name, arr) for name, arr in gbig.items()}
    for name, _ in SMALL:
        if name == "conv_w":
            full_g = gsmall[name].reshape(SSD_K, SSD_XBC)
            grads["ssd_conv_w"] = lax.dynamic_slice(full_g, (0, chip * CONV_SHARD), (SSD_K, CONV_SHARD))[None]
        else:
            grads[_SMALL_ARG[name]] = gsmall[name].reshape(given[_SMALL_ARG[name]].shape)

    delta, new_m, new_v = {}, {}, {}

    def update_matrix(name, grad):
        a = _BIG_ARG[name]
        d, m2, v2 = _adamw_big(updated(name), grad, updated(name, "m_"), updated(name, "v_"), name="adamw_" + a)
        delta[a], new_m[a], new_v[a] = (global_layout(name, t) for t in (d, m2, v2))
        return d

    all_updated = sum(update_matrix(name, grad)[:8, :LANE] for name, grad in gbig.items())
    g_in = _reduce_finish("in", _reduce_async_wait("in", *reducing["in"], all_updated), [updated("w_in").shape])[0]
    grads["w_in"] = global_layout("w_in", g_in)
    update_matrix("w_in", g_in)
    small_names = [_SMALL_ARG[name] for name, _ in SMALL]
    two_d = lambda t: t.reshape(t.shape[-2], t.shape[-1])
    ds, ms, vs = _adamw_small([two_d(given[a]) for a in small_names], [two_d(grads[a]) for a in small_names],
                              [two_d(given["m_" + a]) for a in small_names], [two_d(given["v_" + a]) for a in small_names])
    for a, d, m2, v2 in zip(small_names, ds, ms, vs):
        delta[a], new_m[a], new_v[a] = (t.reshape(given[a].shape) for t in (d, m2, v2))

    return (loss, grad_x[None], *[grads[n] for n in _WEIGHT_ORDER], *[delta[n] for n in _WEIGHT_ORDER],
            *[new_m[n] for n in _WEIGHT_ORDER], *[new_v[n] for n in _WEIGHT_ORDER])
```
